```python
import math
import jax, jax.numpy as jnp
from jax import lax
import numpy as np

D_MODEL = 2048
BATCH = 8
SEQ = 2048
DEPTH = 1

HEAD_DIM = 64
N_Q_HEADS = 16
N_KV_HEADS = 4
Q_PER_KV = N_Q_HEADS // N_KV_HEADS
ATTN_WIDTH = N_Q_HEADS * HEAD_DIM
KV_WIDTH = N_KV_HEADS * HEAD_DIM
WINDOW = 128
BLOCK = 128
NUM_BUCKETS = 32
MAX_EXACT = NUM_BUCKETS // 2
MAX_DISTANCE = 128
CONV_CHANNELS = D_MODEL // 2
CONV_WIDTH = 31
D_FF = ((8 * D_MODEL + 3 * 256 - 1) // (3 * 256)) * 256
EPS = 1e-6
IN_SIZES = [ATTN_WIDTH, KV_WIDTH, KV_WIDTH, CONV_CHANNELS, CONV_CHANNELS, D_MODEL, D_MODEL]
IN_WIDTH = sum(IN_SIZES)
IN_SPLITS = [int(s) for s in np.cumsum(IN_SIZES)[:-1]]
N_MOD = 6

kernel_name = "hybrid_swa_sink_conformer_conv_gated_block"


def rms_norm(x, g):
    xf = x.astype(jnp.float32)
    y = xf * lax.rsqrt(jnp.mean(xf * xf, axis=-1, keepdims=True) + EPS)
    return (y * g.astype(jnp.float32)).astype(x.dtype)


def layer_norm(x, g, b):
    xf = x.astype(jnp.float32)
    mu = jnp.mean(xf, axis=-1, keepdims=True)
    var = jnp.mean(jnp.square(xf - mu), axis=-1, keepdims=True)
    y = (xf - mu) * lax.rsqrt(var + EPS)
    return (y * g.astype(jnp.float32) + b.astype(jnp.float32)).astype(x.dtype)


def t5_causal_bucket(dist):
    n = jnp.maximum(dist, 0)
    nf = jnp.maximum(n, 1).astype(jnp.float32)
    large = MAX_EXACT + (jnp.log(nf / MAX_EXACT) / math.log(MAX_DISTANCE / MAX_EXACT)
                         * (NUM_BUCKETS - MAX_EXACT)).astype(jnp.int32)
    large = jnp.minimum(large, NUM_BUCKETS - 1)
    return jnp.where(n < MAX_EXACT, n, large)


def sliding_window_attention(q, k, v, rel_bias, sinks, q_g, k_g):
    b_sz, s_len = q.shape[0], q.shape[1]
    nb = s_len // BLOCK
    q = rms_norm(q, q_g)
    k = rms_norm(k, k_g)
    qb = q.reshape(b_sz, nb, BLOCK, N_KV_HEADS, Q_PER_KV, HEAD_DIM)

    def band(t):
        tb = t.reshape(b_sz, nb, BLOCK, N_KV_HEADS, HEAD_DIM)
        prev = jnp.concatenate([jnp.zeros_like(tb[:, :1]), tb[:, :-1]], axis=1)
        return jnp.concatenate([prev, tb], axis=2)

    kw, vw = band(k), band(v)
    logits = jnp.einsum('bnqhgd,bnkhd->bnhgqk', qb, kw).astype(jnp.float32) * (HEAD_DIM ** -0.5)

    q_off = jnp.arange(BLOCK)
    k_off = jnp.arange(2 * BLOCK)
    dist = q_off[:, None] + BLOCK - k_off[None, :]
    bias = rel_bias.astype(jnp.float32)[t5_causal_bucket(dist)]
    bias = jnp.transpose(bias, (2, 0, 1)).reshape(N_KV_HEADS, Q_PER_KV, BLOCK, 2 * BLOCK)
    allowed = (dist >= 0) & (dist < WINDOW)
    real_key = (jnp.arange(nb)[:, None, None] > 0) | (k_off[None, None, :] >= BLOCK)
    mask = allowed[None] & real_key

    logits = jnp.where(mask[None, :, None, None], logits + bias, -jnp.inf)
    sink = sinks.astype(jnp.float32).reshape(N_KV_HEADS, Q_PER_KV)[None, None, :, :, None, None]
    m = jnp.maximum(jnp.max(logits, axis=-1, keepdims=True), sink)
    p = jnp.exp(logits - m)
    probs = p / (jnp.sum(p, axis=-1, keepdims=True) + jnp.exp(sink - m))
    out = jnp.einsum('bnhgqk,bnkhd->bnqhgd', probs.astype(v.dtype), vw)
    return out.reshape(b_sz, s_len, ATTN_WIDTH)


def conformer_conv(a, gate, conv_w, conv_b, ln_g, ln_b, w_out):
    u = a * jax.nn.sigmoid(gate)
    u = lax.conv_general_dilated(
        u, conv_w[:, None, :].astype(u.dtype), window_strides=(1,),
        padding=[(CONV_WIDTH - 1, 0)], dimension_numbers=('NWC', 'WIO', 'NWC'),
        feature_group_count=CONV_CHANNELS) + conv_b
    u = jax.nn.silu(layer_norm(u, ln_g, ln_b))
    return u @ w_out


def _fwd_setup_inputs(seed: int = 0) -> dict:
    key = jax.random.key(seed)
    ks = jax.random.split(key, 24)
    nrm = lambda k, shape, s: jax.random.normal(k, shape, jnp.float32) * s
    d = D_MODEL
    return {
        "x": nrm(ks[0], (BATCH, SEQ, d), 1.0),
        "c": nrm(ks[1], (BATCH, d), 1.0),
        "w_ada": nrm(ks[2], (DEPTH, d, N_MOD * d), 0.5 * d ** -0.5),
        "b_ada": nrm(ks[3], (DEPTH, N_MOD * d), 0.02),
        "norm_mix_g": 1.0 + nrm(ks[4], (DEPTH, d), 0.02),
        "w_in": nrm(ks[5], (DEPTH, d, IN_WIDTH), d ** -0.5),
        "q_norm_g": 1.0 + nrm(ks[6], (DEPTH, HEAD_DIM), 0.02),
        "k_norm_g": 1.0 + nrm(ks[7], (DEPTH, HEAD_DIM), 0.02),
        "attn_sinks": nrm(ks[8], (DEPTH, N_Q_HEADS), 0.5),
        "rel_bias": nrm(ks[9], (NUM_BUCKETS, N_Q_HEADS), 0.5),
        "w_attn_out": nrm(ks[10], (DEPTH, ATTN_WIDTH, d), ATTN_WIDTH ** -0.5),
        "conv_w": nrm(ks[11], (DEPTH, CONV_WIDTH, CONV_CHANNELS), CONV_WIDTH ** -0.5),
        "conv_b": nrm(ks[12], (DEPTH, CONV_CHANNELS), 0.02),
        "conv_ln_g": 1.0 + nrm(ks[13], (DEPTH, CONV_CHANNELS), 0.02),
        "conv_ln_b": nrm(ks[14], (DEPTH, CONV_CHANNELS), 0.02),
        "w_conv_out": nrm(ks[15], (DEPTH, CONV_CHANNELS, d), CONV_CHANNELS ** -0.5),
        "w_mix_out": nrm(ks[16], (DEPTH, d, d), d ** -0.5),
        "norm_ffn_g": 1.0 + nrm(ks[17], (DEPTH, d), 0.02),
        "w_ffn_in": nrm(ks[18], (DEPTH, d, 2 * D_FF), d ** -0.5),
        "w_ffn_out": nrm(ks[19], (DEPTH, D_FF, d), D_FF ** -0.5),
    }


def _fwd_reference(x, c, w_ada, b_ada, norm_mix_g, w_in, q_norm_g, k_norm_g, attn_sinks, rel_bias,
              w_attn_out, conv_w, conv_b, conv_ln_g, conv_ln_b, w_conv_out, w_mix_out,
              norm_ffn_g, w_ffn_in, w_ffn_out):
    b_sz, s_len = x.shape[0], x.shape[1]
    c_act = jax.nn.silu(c)
    for l in range(DEPTH):
        mod = (c_act @ w_ada[l] + b_ada[l])[:, None, :]
        sh_m, sc_m, gt_m, sh_f, sc_f, gt_f = jnp.split(mod, N_MOD, axis=-1)

        h = rms_norm(x, norm_mix_g[l]) * (1.0 + sc_m) + sh_m
        q, k, v, ca, cb, g_attn, g_conv = jnp.split(h @ w_in[l], IN_SPLITS, axis=-1)
        q = q.reshape(b_sz, s_len, N_Q_HEADS, HEAD_DIM)
        k = k.reshape(b_sz, s_len, N_KV_HEADS, HEAD_DIM)
        v = v.reshape(b_sz, s_len, N_KV_HEADS, HEAD_DIM)
        y_attn = sliding_window_attention(q, k, v, rel_bias, attn_sinks[l],
                                          q_norm_g[l], k_norm_g[l]) @ w_attn_out[l]
        y_conv = conformer_conv(ca, cb, conv_w[l], conv_b[l], conv_ln_g[l], conv_ln_b[l],
                                w_conv_out[l])
        merged = jax.nn.sigmoid(g_attn) * y_attn + jax.nn.sigmoid(g_conv) * y_conv
        x = x + gt_m * (merged @ w_mix_out[l])

        h = rms_norm(x, norm_ffn_g[l]) * (1.0 + sc_f) + sh_f
        f_gate, f_up = jnp.split(h @ w_ffn_in[l], 2, axis=-1)
        x = x + gt_f * ((jax.nn.silu(f_gate) * f_up) @ w_ffn_out[l])
    return x


import jax as _jax
import jax.numpy as _jnp

TWIN_FORMAT = 'train_step'
FWD_PARAMS = ['x', 'c', 'w_ada', 'b_ada', 'norm_mix_g', 'w_in', 'q_norm_g', 'k_norm_g', 'attn_sinks', 'rel_bias', 'w_attn_out', 'conv_w', 'conv_b', 'conv_ln_g', 'conv_ln_b', 'w_conv_out', 'w_mix_out', 'norm_ffn_g', 'w_ffn_in', 'w_ffn_out']
TWIN_WEIGHTS = ['w_ada', 'b_ada', 'norm_mix_g', 'w_in', 'q_norm_g', 'k_norm_g', 'attn_sinks', 'rel_bias', 'w_attn_out', 'conv_w', 'conv_b', 'conv_ln_g', 'conv_ln_b', 'w_conv_out', 'w_mix_out', 'norm_ffn_g', 'w_ffn_in', 'w_ffn_out']
TWIN_DIFF_INPUT = 'x'
TWIN_INPUTS = ['x', 'c', 'w_ada', 'b_ada', 'norm_mix_g', 'w_in', 'q_norm_g', 'k_norm_g', 'attn_sinks', 'rel_bias', 'w_attn_out', 'conv_w', 'conv_b', 'conv_ln_g', 'conv_ln_b', 'w_conv_out', 'w_mix_out', 'norm_ffn_g', 'w_ffn_in', 'w_ffn_out', 'loss_target', 'm_w_ada', 'm_b_ada', 'm_norm_mix_g', 'm_w_in', 'm_q_norm_g', 'm_k_norm_g', 'm_attn_sinks', 'm_rel_bias', 'm_w_attn_out', 'm_conv_w', 'm_conv_b', 'm_conv_ln_g', 'm_conv_ln_b', 'm_w_conv_out', 'm_w_mix_out', 'm_norm_ffn_g', 'm_w_ffn_in', 'm_w_ffn_out', 'v_w_ada', 'v_b_ada', 'v_norm_mix_g', 'v_w_in', 'v_q_norm_g', 'v_k_norm_g', 'v_attn_sinks', 'v_rel_bias', 'v_w_attn_out', 'v_conv_w', 'v_conv_b', 'v_conv_ln_g', 'v_conv_ln_b', 'v_w_conv_out', 'v_w_mix_out', 'v_norm_ffn_g', 'v_w_ffn_in', 'v_w_ffn_out']
TWIN_OUTPUTS = ['loss', 'grad_x', 'grad_w_ada', 'grad_b_ada', 'grad_norm_mix_g', 'grad_w_in', 'grad_q_norm_g', 'grad_k_norm_g', 'grad_attn_sinks', 'grad_rel_bias', 'grad_w_attn_out', 'grad_conv_w', 'grad_conv_b', 'grad_conv_ln_g', 'grad_conv_ln_b', 'grad_w_conv_out', 'grad_w_mix_out', 'grad_norm_ffn_g', 'grad_w_ffn_in', 'grad_w_ffn_out', 'delta_w_ada', 'delta_b_ada', 'delta_norm_mix_g', 'delta_w_in', 'delta_q_norm_g', 'delta_k_norm_g', 'delta_attn_sinks', 'delta_rel_bias', 'delta_w_attn_out', 'delta_conv_w', 'delta_conv_b', 'delta_conv_ln_g', 'delta_conv_ln_b', 'delta_w_conv_out', 'delta_w_mix_out', 'delta_norm_ffn_g', 'delta_w_ffn_in', 'delta_w_ffn_out', 'new_m_w_ada', 'new_m_b_ada', 'new_m_norm_mix_g', 'new_m_w_in', 'new_m_q_norm_g', 'new_m_k_norm_g', 'new_m_attn_sinks', 'new_m_rel_bias', 'new_m_w_attn_out', 'new_m_conv_w', 'new_m_conv_b', 'new_m_conv_ln_g', 'new_m_conv_ln_b', 'new_m_w_conv_out', 'new_m_w_mix_out', 'new_m_norm_ffn_g', 'new_m_w_ffn_in', 'new_m_w_ffn_out', 'new_v_w_ada', 'new_v_b_ada', 'new_v_norm_mix_g', 'new_v_w_in', 'new_v_q_norm_g', 'new_v_k_norm_g', 'new_v_attn_sinks', 'new_v_rel_bias', 'new_v_w_attn_out', 'new_v_conv_w', 'new_v_conv_b', 'new_v_conv_ln_g', 'new_v_conv_ln_b', 'new_v_w_conv_out', 'new_v_w_mix_out', 'new_v_norm_ffn_g', 'new_v_w_ffn_in', 'new_v_w_ffn_out']
TWIN_LEAF_KINDS = {'loss': 'loss', 'grad_x': 'grad_x', 'grad_w_ada': 'grad_w', 'grad_b_ada': 'grad_w', 'grad_norm_mix_g': 'grad_w', 'grad_w_in': 'grad_w', 'grad_q_norm_g': 'grad_w', 'grad_k_norm_g': 'grad_w', 'grad_attn_sinks': 'grad_w', 'grad_rel_bias': 'grad_w', 'grad_w_attn_out': 'grad_w', 'grad_conv_w': 'grad_w', 'grad_conv_b': 'grad_w', 'grad_conv_ln_g': 'grad_w', 'grad_conv_ln_b': 'grad_w', 'grad_w_conv_out': 'grad_w', 'grad_w_mix_out': 'grad_w', 'grad_norm_ffn_g': 'grad_w', 'grad_w_ffn_in': 'grad_w', 'grad_w_ffn_out': 'grad_w', 'delta_w_ada': 'delta_w', 'delta_b_ada': 'delta_w', 'delta_norm_mix_g': 'delta_w', 'delta_w_in': 'delta_w', 'delta_q_norm_g': 'delta_w', 'delta_k_norm_g': 'delta_w', 'delta_attn_sinks': 'delta_w', 'delta_rel_bias': 'delta_w', 'delta_w_attn_out': 'delta_w', 'delta_conv_w': 'delta_w', 'delta_conv_b': 'delta_w', 'delta_conv_ln_g': 'delta_w', 'delta_conv_ln_b': 'delta_w', 'delta_w_conv_out': 'delta_w', 'delta_w_mix_out': 'delta_w', 'delta_norm_ffn_g': 'delta_w', 'delta_w_ffn_in': 'delta_w', 'delta_w_ffn_out': 'delta_w', 'new_m_w_ada': 'new_m', 'new_m_b_ada': 'new_m', 'new_m_norm_mix_g': 'new_m', 'new_m_w_in': 'new_m', 'new_m_q_norm_g': 'new_m', 'new_m_k_norm_g': 'new_m', 'new_m_attn_sinks': 'new_m', 'new_m_rel_bias': 'new_m', 'new_m_w_attn_out': 'new_m', 'new_m_conv_w': 'new_m', 'new_m_conv_b': 'new_m', 'new_m_conv_ln_g': 'new_m', 'new_m_conv_ln_b': 'new_m', 'new_m_w_conv_out': 'new_m', 'new_m_w_mix_out': 'new_m', 'new_m_norm_ffn_g': 'new_m', 'new_m_w_ffn_in': 'new_m', 'new_m_w_ffn_out': 'new_m', 'new_v_w_ada': 'new_v', 'new_v_b_ada': 'new_v', 'new_v_norm_mix_g': 'new_v', 'new_v_w_in': 'new_v', 'new_v_q_norm_g': 'new_v', 'new_v_k_norm_g': 'new_v', 'new_v_attn_sinks': 'new_v', 'new_v_rel_bias': 'new_v', 'new_v_w_attn_out': 'new_v', 'new_v_conv_w': 'new_v', 'new_v_conv_b': 'new_v', 'new_v_conv_ln_g': 'new_v', 'new_v_conv_ln_b': 'new_v', 'new_v_w_conv_out': 'new_v', 'new_v_w_mix_out': 'new_v', 'new_v_norm_ffn_g': 'new_v', 'new_v_w_ffn_in': 'new_v', 'new_v_w_ffn_out': 'new_v'}


def _forward(args):
    return _fwd_reference(*[args[k] for k in FWD_PARAMS])


def _output_shape():
    out = _jax.eval_shape(lambda: _forward(_fwd_setup_inputs(0)))
    return out.shape, out.dtype

N_MICROBATCH = 1
ADAM_LR = 0.001
ADAM_B1 = 0.9
ADAM_B2 = 0.999
ADAM_EPS = 1e-08
ADAM_WD = 0.01
ADAM_STEP = 10
PER_EXAMPLE_BATCH_AXIS = {'x': 0, 'c': 0, 'loss_target': 0}
SHARED_INPUTS = []
_WEIGHT_DTYPES = {'w_ada': _jnp.float32, 'b_ada': _jnp.float32, 'norm_mix_g': _jnp.float32, 'w_in': _jnp.float32, 'q_norm_g': _jnp.float32, 'k_norm_g': _jnp.float32, 'attn_sinks': _jnp.float32, 'rel_bias': _jnp.float32, 'w_attn_out': _jnp.float32, 'conv_w': _jnp.float32, 'conv_b': _jnp.float32, 'conv_ln_g': _jnp.float32, 'conv_ln_b': _jnp.float32, 'w_conv_out': _jnp.float32, 'w_mix_out': _jnp.float32, 'norm_ffn_g': _jnp.float32, 'w_ffn_in': _jnp.float32, 'w_ffn_out': _jnp.float32}
MOMENT_SCALE = {'w_ada': 1.944160e-01, 'b_ada': 4.286397e-01, 'norm_mix_g': 1.761390e-02, 'w_in': 1.823496e-02, 'q_norm_g': 1.110900e-01, 'k_norm_g': 1.116825e-01, 'attn_sinks': 1.977358e-02, 'rel_bias': 9.177986e-03, 'w_attn_out': 2.289729e-02, 'conv_w': 2.004836e-02, 'conv_b': 1.175564e-01, 'conv_ln_g': 2.097417e-01, 'conv_ln_b': 1.406668e-01, 'w_conv_out': 2.160552e-02, 'w_mix_out': 2.558575e-02, 'norm_ffn_g': 8.015444e-01, 'w_ffn_in': 1.730023e-02, 'w_ffn_out': 2.250628e-02}


def _to_microbatches(a, axis):
    t = _jnp.moveaxis(a, axis, 0)
    t = t.reshape((N_MICROBATCH, t.shape[0] // N_MICROBATCH) + t.shape[1:])
    return _jnp.moveaxis(t, 1, axis + 1)


def setup_inputs(seed: int = 0) -> dict:
    inp = _fwd_setup_inputs(seed)
    key = _jax.random.fold_in(_jax.random.key(seed), 7919)
    shape, _ = _output_shape()
    out = dict(inp)
    out["loss_target"] = _jax.random.normal(_jax.random.fold_in(key, 0), shape, _jnp.float32)
    for i, name in enumerate(TWIN_WEIGHTS):
        w = inp[name].astype(_jnp.float32)
        if MOMENT_SCALE is None:
            s = _jnp.sqrt(_jnp.mean(_jnp.square(w)) + 1e-30)
        else:
            s = MOMENT_SCALE[name]
        km, kv = _jax.random.split(_jax.random.fold_in(key, i + 1))
        out[name] = w
        out["m_" + name] = s * _jax.random.normal(km, w.shape, _jnp.float32)
        out["v_" + name] = (s * s) * _jax.random.uniform(kv, w.shape, _jnp.float32, 0.5, 1.5)
    if N_MICROBATCH > 1:
        for name, axis in PER_EXAMPLE_BATCH_AXIS.items():
            out[name] = _to_microbatches(out[name], axis)
    return {'x': out['x'], 'c': out['c'], 'w_ada': out['w_ada'], 'b_ada': out['b_ada'], 'norm_mix_g': out['norm_mix_g'], 'w_in': out['w_in'], 'q_norm_g': out['q_norm_g'], 'k_norm_g': out['k_norm_g'], 'attn_sinks': out['attn_sinks'], 'rel_bias': out['rel_bias'], 'w_attn_out': out['w_attn_out'], 'conv_w': out['conv_w'], 'conv_b': out['conv_b'], 'conv_ln_g': out['conv_ln_g'], 'conv_ln_b': out['conv_ln_b'], 'w_conv_out': out['w_conv_out'], 'w_mix_out': out['w_mix_out'], 'norm_ffn_g': out['norm_ffn_g'], 'w_ffn_in': out['w_ffn_in'], 'w_ffn_out': out['w_ffn_out'], 'loss_target': out['loss_target'], 'm_w_ada': out['m_w_ada'], 'm_b_ada': out['m_b_ada'], 'm_norm_mix_g': out['m_norm_mix_g'], 'm_w_in': out['m_w_in'], 'm_q_norm_g': out['m_q_norm_g'], 'm_k_norm_g': out['m_k_norm_g'], 'm_attn_sinks': out['m_attn_sinks'], 'm_rel_bias': out['m_rel_bias'], 'm_w_attn_out': out['m_w_attn_out'], 'm_conv_w': out['m_conv_w'], 'm_conv_b': out['m_conv_b'], 'm_conv_ln_g': out['m_conv_ln_g'], 'm_conv_ln_b': out['m_conv_ln_b'], 'm_w_conv_out': out['m_w_conv_out'], 'm_w_mix_out': out['m_w_mix_out'], 'm_norm_ffn_g': out['m_norm_ffn_g'], 'm_w_ffn_in': out['m_w_ffn_in'], 'm_w_ffn_out': out['m_w_ffn_out'], 'v_w_ada': out['v_w_ada'], 'v_b_ada': out['v_b_ada'], 'v_norm_mix_g': out['v_norm_mix_g'], 'v_w_in': out['v_w_in'], 'v_q_norm_g': out['v_q_norm_g'], 'v_k_norm_g': out['v_k_norm_g'], 'v_attn_sinks': out['v_attn_sinks'], 'v_rel_bias': out['v_rel_bias'], 'v_w_attn_out': out['v_w_attn_out'], 'v_conv_w': out['v_conv_w'], 'v_conv_b': out['v_conv_b'], 'v_conv_ln_g': out['v_conv_ln_g'], 'v_conv_ln_b': out['v_conv_ln_b'], 'v_w_conv_out': out['v_w_conv_out'], 'v_w_mix_out': out['v_w_mix_out'], 'v_norm_ffn_g': out['v_norm_ffn_g'], 'v_w_ffn_in': out['v_w_ffn_in'], 'v_w_ffn_out': out['v_w_ffn_out']}


def _loss(weights, diff, rest, loss_target):
    with _jax.named_scope("forward"):
        args = {**rest, TWIN_DIFF_INPUT: diff, **{k: w.astype(_WEIGHT_DTYPES[k]) for k, w in weights.items()}}
        y = _forward(args)
    with _jax.named_scope("loss_head"):
        err = _jnp.square(y.astype(_jnp.float32) - loss_target)
        return 0.5 * _jnp.sum(_jnp.mean(err, axis=-1)) if err.ndim else 0.5 * err


def _adamw(w, g, m, v):
    m = ADAM_B1 * m + (1.0 - ADAM_B1) * g
    v = ADAM_B2 * v + (1.0 - ADAM_B2) * _jnp.square(g)
    m_hat = m / (1.0 - ADAM_B1 ** ADAM_STEP)
    v_hat = v / (1.0 - ADAM_B2 ** ADAM_STEP)
    delta = -ADAM_LR * (m_hat / (_jnp.sqrt(v_hat) + ADAM_EPS) + ADAM_WD * w)
    return delta, m, v


def reference(x, c, w_ada, b_ada, norm_mix_g, w_in, q_norm_g, k_norm_g, attn_sinks, rel_bias, w_attn_out, conv_w, conv_b, conv_ln_g, conv_ln_b, w_conv_out, w_mix_out, norm_ffn_g, w_ffn_in, w_ffn_out, loss_target, m_w_ada, m_b_ada, m_norm_mix_g, m_w_in, m_q_norm_g, m_k_norm_g, m_attn_sinks, m_rel_bias, m_w_attn_out, m_conv_w, m_conv_b, m_conv_ln_g, m_conv_ln_b, m_w_conv_out, m_w_mix_out, m_norm_ffn_g, m_w_ffn_in, m_w_ffn_out, v_w_ada, v_b_ada, v_norm_mix_g, v_w_in, v_q_norm_g, v_k_norm_g, v_attn_sinks, v_rel_bias, v_w_attn_out, v_conv_w, v_conv_b, v_conv_ln_g, v_conv_ln_b, v_w_conv_out, v_w_mix_out, v_norm_ffn_g, v_w_ffn_in, v_w_ffn_out):
    given = dict(x=x, c=c, w_ada=w_ada, b_ada=b_ada, norm_mix_g=norm_mix_g, w_in=w_in, q_norm_g=q_norm_g, k_norm_g=k_norm_g, attn_sinks=attn_sinks, rel_bias=rel_bias, w_attn_out=w_attn_out, conv_w=conv_w, conv_b=conv_b, conv_ln_g=conv_ln_g, conv_ln_b=conv_ln_b, w_conv_out=w_conv_out, w_mix_out=w_mix_out, norm_ffn_g=norm_ffn_g, w_ffn_in=w_ffn_in, w_ffn_out=w_ffn_out, loss_target=loss_target, m_w_ada=m_w_ada, m_b_ada=m_b_ada, m_norm_mix_g=m_norm_mix_g, m_w_in=m_w_in, m_q_norm_g=m_q_norm_g, m_k_norm_g=m_k_norm_g, m_attn_sinks=m_attn_sinks, m_rel_bias=m_rel_bias, m_w_attn_out=m_w_attn_out, m_conv_w=m_conv_w, m_conv_b=m_conv_b, m_conv_ln_g=m_conv_ln_g, m_conv_ln_b=m_conv_ln_b, m_w_conv_out=m_w_conv_out, m_w_mix_out=m_w_mix_out, m_norm_ffn_g=m_norm_ffn_g, m_w_ffn_in=m_w_ffn_in, m_w_ffn_out=m_w_ffn_out, v_w_ada=v_w_ada, v_b_ada=v_b_ada, v_norm_mix_g=v_norm_mix_g, v_w_in=v_w_in, v_q_norm_g=v_q_norm_g, v_k_norm_g=v_k_norm_g, v_attn_sinks=v_attn_sinks, v_rel_bias=v_rel_bias, v_w_attn_out=v_w_attn_out, v_conv_w=v_conv_w, v_conv_b=v_conv_b, v_conv_ln_g=v_conv_ln_g, v_conv_ln_b=v_conv_ln_b, v_w_conv_out=v_w_conv_out, v_w_mix_out=v_w_mix_out, v_norm_ffn_g=v_norm_ffn_g, v_w_ffn_in=v_w_ffn_in, v_w_ffn_out=v_w_ffn_out)
    weights = {n: given[n] for n in TWIN_WEIGHTS}
    shared = {n: given[n] for n in SHARED_INPUTS}
    per_example = {n: given[n] for n in ['x', 'c']}
    grad_fn = _jax.value_and_grad(_loss, argnums=(0, 1))

    def one_microbatch(ex, loss_target):
        ex = dict(ex)
        diff = ex.pop(TWIN_DIFF_INPUT)
        return grad_fn(weights, diff, {**shared, **ex}, loss_target)

    if N_MICROBATCH == 1:
        loss, (grad_w, grad_x) = one_microbatch(per_example, given["loss_target"])
    else:
        def body(carry, xs):
            loss_sum, grad_sum = carry
            l_k, (gw_k, gx_k) = one_microbatch(xs[0], xs[1])
            with _jax.named_scope("update"):
                return (loss_sum + l_k, _jax.tree.map(_jnp.add, grad_sum, gw_k)), gx_k

        init = (_jnp.zeros((), _jnp.float32), _jax.tree.map(_jnp.zeros_like, weights))
        (loss, grad_w), grad_x = _jax.lax.scan(body, init, (per_example, given["loss_target"]))
    with _jax.named_scope("update"):
        delta_w, new_m, new_v = {}, {}, {}
        for n in TWIN_WEIGHTS:
            delta_w[n], new_m[n], new_v[n] = _adamw(weights[n], grad_w[n], given["m_" + n], given["v_" + n])
    return (loss, grad_x, *[grad_w[n] for n in TWIN_WEIGHTS], *[delta_w[n] for n in TWIN_WEIGHTS],
            *[new_m[n] for n in TWIN_WEIGHTS], *[new_v[n] for n in TWIN_WEIGHTS])
```

```python
import functools
import math

import jax
import jax.numpy as jnp
import numpy as np
from jax import lax
from jax.experimental import pallas as pl
from jax.experimental.pallas import tpu as pltpu

F32 = jnp.float32
BF16 = jnp.bfloat16
MESH = pl.DeviceIdType.MESH

V7X_VMEM_BYTES = 64 * 1024 * 1024
VMEM_LIMIT = V7X_VMEM_BYTES - 8 * 1024 * 1024
LANES = 128
SUBLANES = 8

EPS = 1e-6
WINDOW = 128
BLOCK = 128
NUM_BUCKETS = 32
MAX_EXACT = NUM_BUCKETS // 2
MAX_DISTANCE = 128
CONV_WIDTH = 31
CONV_HALO = 32
ADAM_LR = 0.001
ADAM_B1 = 0.9
ADAM_B2 = 0.999
ADAM_EPS = 1e-08
ADAM_WD = 0.01
ADAM_STEP = 10
N_MOD = 6
SH_M, SC_M, GT_M, SH_F, SC_F, GT_F = range(6)

N_CHIPS = 4
N_DEV = 8


def _cparams(sem=None):
    return pltpu.CompilerParams(dimension_semantics=sem, vmem_limit_bytes=VMEM_LIMIT)


def _tile(n, pref, unit=LANES):
    best = None
    for t in range(unit, min(n, pref) + 1, unit):
        if n % t == 0:
            best = t
    return best if best is not None else n


def _sigmoid(v):
    return 1.0 / (1.0 + jnp.exp(-v))


ROW_CHUNK = 512


def _row_chunks(m):
    step = ROW_CHUNK if m % ROW_CHUNK == 0 else m
    return [(s, step) for s in range(0, m, step)]


def mm_nn(a, w, *, tn, tk, out_dtype, name):
    m, k = a.shape
    j, k2, nj = w.shape
    assert k == k2 and nj % tn == 0 and k % tk == 0
    npj, nk = nj // tn, k // tk

    def body(a_ref, w_ref, o_ref, *scratch):
        kk = pl.program_id(1)
        for s, sz in _row_chunks(m):
            rows = pl.ds(s, sz)
            p = jnp.dot(a_ref[rows, :], w_ref[...], preferred_element_type=F32)
            if nk == 1:
                o_ref[rows, :] = p.astype(out_dtype)
            else:
                acc = scratch[0]

                @pl.when(kk == 0)
                def _():
                    acc[rows, :] = p

                @pl.when(kk > 0)
                def _():
                    acc[rows, :] += p

                @pl.when(kk == nk - 1)
                def _():
                    o_ref[rows, :] = acc[rows, :].astype(out_dtype)

    return pl.pallas_call(
        body,
        grid=(j * npj, nk),
        in_specs=[
            pl.BlockSpec((m, tk), lambda n, kk: (0, kk)),
            pl.BlockSpec((None, tk, tn), lambda n, kk: (n // npj, kk, n % npj)),
        ],
        out_specs=pl.BlockSpec((m, tn), lambda n, kk: (0, n)),
        out_shape=jax.ShapeDtypeStruct((m, j * nj), out_dtype),
        scratch_shapes=[pltpu.VMEM((m, tn), F32)] if nk > 1 else [],
        compiler_params=_cparams(("parallel", "arbitrary")),
        name=name,
    )(a, w)


def mm_nt(g, w, *, tko, tn, name):
    m, n = g.shape
    j, k, nj = w.shape
    assert n == j * nj and nj % tn == 0 and k % tko == 0
    npj, nr = nj // tn, n // tn

    def body(g_ref, w_ref, o_ref):
        r = pl.program_id(1)
        for s, sz in _row_chunks(m):
            rows = pl.ds(s, sz)
            p = lax.dot_general(g_ref[rows, :], w_ref[...], (((1,), (1,)), ((), ())), preferred_element_type=F32)

            @pl.when(r == 0)
            def _():
                o_ref[rows, :] = p

            @pl.when(r > 0)
            def _():
                o_ref[rows, :] += p

    return pl.pallas_call(
        body,
        grid=(k // tko, nr),
        in_specs=[
            pl.BlockSpec((m, tn), lambda ko, r: (0, r)),
            pl.BlockSpec((None, tko, tn), lambda ko, r: (r // npj, ko, r % npj)),
        ],
        out_specs=pl.BlockSpec((m, tko), lambda ko, r: (0, ko)),
        out_shape=jax.ShapeDtypeStruct((m, k), F32),
        compiler_params=_cparams(("parallel", "arbitrary")),
        name=name,
    )(g, w)


def mm_tn(a, g, n_blocks, *, tk, tn, name):
    m, k = a.shape
    m2, n = g.shape
    nj = n // n_blocks
    assert m == m2 and nj % tn == 0 and k % tk == 0
    npj = nj // tn

    def body(a_ref, g_ref, o_ref):
        p = lax.dot_general(a_ref[...], g_ref[...], (((0,), (0,)), ((), ())), preferred_element_type=F32)
        o_ref[...] = p.astype(BF16)

    return pl.pallas_call(
        body,
        grid=(k // tk, n // tn),
        in_specs=[
            pl.BlockSpec((m, tk), lambda kk, nn: (0, kk)),
            pl.BlockSpec((m, tn), lambda kk, nn: (0, nn)),
        ],
        out_specs=pl.BlockSpec((None, tk, tn), lambda kk, nn: (nn // npj, kk, nn % npj)),
        out_shape=jax.ShapeDtypeStruct((n_blocks, k, nj), BF16),
        compiler_params=_cparams(("parallel", "parallel")),
        name=name,
    )(a, g)


ROW_TILE = 256


def _row_spec(tr, width):
    return pl.BlockSpec((tr, width), lambda i: (i, 0))


def _full_spec(shape):
    return pl.BlockSpec(shape, lambda *_: (0,) * len(shape))


def _rms(xv):
    return lax.rsqrt(jnp.mean(xv * xv, axis=-1, keepdims=True) + EPS)


def _mod_row(mod_ref, row):
    return mod_ref[pl.ds(row, 1), :]


def pre_mix_fwd(x, mod, gain):
    t, d = x.shape
    tr = _tile(t, ROW_TILE, SUBLANES)

    def body(x_ref, mod_ref, g_ref, h_ref):
        xv = x_ref[...]
        y = xv * _rms(xv) * g_ref[...]
        h_ref[...] = (y * (1.0 + _mod_row(mod_ref, SC_M)) + _mod_row(mod_ref, SH_M)).astype(BF16)

    return pl.pallas_call(
        body, grid=(t // tr,),
        in_specs=[_row_spec(tr, d), _full_spec(mod.shape), _full_spec(gain.shape)],
        out_specs=_row_spec(tr, d),
        out_shape=jax.ShapeDtypeStruct((t, d), BF16),
        compiler_params=_cparams(("parallel",)), name="pre_mix_fwd",
    )(x, mod, gain)


def pre_ffn_fwd(x, o_m, mod, gain):
    t, d = x.shape
    tr = _tile(t, ROW_TILE, SUBLANES)

    def body(x_ref, om_ref, mod_ref, g_ref, x1_ref, h_ref):
        x1 = x_ref[...] + _mod_row(mod_ref, GT_M) * om_ref[...]
        x1_ref[...] = x1
        y = x1 * _rms(x1) * g_ref[...]
        h_ref[...] = (y * (1.0 + _mod_row(mod_ref, SC_F)) + _mod_row(mod_ref, SH_F)).astype(BF16)

    return pl.pallas_call(
        body, grid=(t // tr,),
        in_specs=[_row_spec(tr, d), _row_spec(tr, d), _full_spec(mod.shape), _full_spec(gain.shape)],
        out_specs=[_row_spec(tr, d), _row_spec(tr, d)],
        out_shape=[jax.ShapeDtypeStruct((t, d), F32), jax.ShapeDtypeStruct((t, d), BF16)],
        compiler_params=_cparams(("parallel",)), name="pre_ffn_fwd",
    )(x, o_m, mod, gain)


def loss_head(x1, o_f, target, mod):
    t, d = x1.shape
    tr = _tile(t, ROW_TILE, SUBLANES)

    def body(x1_ref, of_ref, tg_ref, mod_ref, loss_ref, dy_ref, dof_ref, acc_ref):
        i = pl.program_id(0)
        gt = _mod_row(mod_ref, GT_F)
        of = of_ref[...]
        err = x1_ref[...] + gt * of - tg_ref[...]
        dy = err * (1.0 / d)
        dy_ref[...] = dy
        dof_ref[...] = (dy * gt).astype(BF16)
        part = (0.5 / d) * jnp.sum(jnp.sum(err * err, axis=1, keepdims=True), axis=0, keepdims=True)
        dgt = jnp.sum(dy * of, axis=0, keepdims=True)

        @pl.when(i == 0)
        def _():
            loss_ref[...] = jnp.zeros_like(loss_ref)
            acc_ref[...] = jnp.zeros_like(acc_ref)

        loss_ref[...] += part
        acc_ref[pl.ds(0, 1), :] += dgt

    return pl.pallas_call(
        body, grid=(t // tr,),
        in_specs=[_row_spec(tr, d), _row_spec(tr, d), _row_spec(tr, d), _full_spec(mod.shape)],
        out_specs=[_full_spec((1, 1)), _row_spec(tr, d), _row_spec(tr, d), _full_spec((SUBLANES, d))],
        out_shape=[jax.ShapeDtypeStruct((1, 1), F32), jax.ShapeDtypeStruct((t, d), F32),
                   jax.ShapeDtypeStruct((t, d), BF16), jax.ShapeDtypeStruct((SUBLANES, d), F32)],
        compiler_params=_cparams(("arbitrary",)), name="loss_head",
    )(x1, o_f, target, mod)


def _norm_bwd(xv, dh, sc, gain):
    rstd = _rms(xv)
    yn = xv * rstd
    dsh = jnp.sum(dh, axis=0, keepdims=True)
    dsc = jnp.sum(dh * (yn * gain), axis=0, keepdims=True)
    dgain = jnp.sum(dh * (1.0 + sc) * yn, axis=0, keepdims=True)
    dyn = dh * ((1.0 + sc) * gain)
    dx = rstd * (dyn - yn * jnp.mean(dyn * yn, axis=-1, keepdims=True))
    return dx, dsh, dsc, dgain


def pre_ffn_bwd(x1, dh2, dy, o_m, mod, gain):
    t, d = x1.shape
    tr = _tile(t, ROW_TILE, SUBLANES)

    def body(x1_ref, dh_ref, dy_ref, om_ref, mod_ref, g_ref, dx1_ref, dom_ref, acc_ref):
        i = pl.program_id(0)
        dxn, dsh, dsc, dgain = _norm_bwd(x1_ref[...], dh_ref[...], _mod_row(mod_ref, SC_F), g_ref[...])
        dx1 = dy_ref[...] + dxn
        dx1_ref[...] = dx1
        dom_ref[...] = (dx1 * _mod_row(mod_ref, GT_M)).astype(BF16)
        dgt = jnp.sum(dx1 * om_ref[...], axis=0, keepdims=True)

        @pl.when(i == 0)
        def _():
            acc_ref[...] = jnp.zeros_like(acc_ref)

        acc_ref[pl.ds(0, 1), :] += dsh
        acc_ref[pl.ds(1, 1), :] += dsc
        acc_ref[pl.ds(2, 1), :] += dgain
        acc_ref[pl.ds(3, 1), :] += dgt

    return pl.pallas_call(
        body, grid=(t // tr,),
        in_specs=[_row_spec(tr, d)] * 4 + [_full_spec(mod.shape), _full_spec(gain.shape)],
        out_specs=[_row_spec(tr, d), _row_spec(tr, d), _full_spec((SUBLANES, d))],
        out_shape=[jax.ShapeDtypeStruct((t, d), F32), jax.ShapeDtypeStruct((t, d), BF16),
                   jax.ShapeDtypeStruct((SUBLANES, d), F32)],
        compiler_params=_cparams(("arbitrary",)), name="pre_ffn_bwd",
    )(x1, dh2, dy, o_m, mod, gain)


def pre_mix_bwd(x, dh, dx1, mod, gain):
    t, d = x.shape
    tr = _tile(t, ROW_TILE, SUBLANES)

    def body(x_ref, dh_ref, dx1_ref, mod_ref, g_ref, gx_ref, acc_ref):
        i = pl.program_id(0)
        dxn, dsh, dsc, dgain = _norm_bwd(x_ref[...], dh_ref[...], _mod_row(mod_ref, SC_M), g_ref[...])
        gx_ref[...] = dx1_ref[...] + dxn

        @pl.when(i == 0)
        def _():
            acc_ref[...] = jnp.zeros_like(acc_ref)

        acc_ref[pl.ds(0, 1), :] += dsh
        acc_ref[pl.ds(1, 1), :] += dsc
        acc_ref[pl.ds(2, 1), :] += dgain

    return pl.pallas_call(
        body, grid=(t // tr,),
        in_specs=[_row_spec(tr, d)] * 3 + [_full_spec(mod.shape), _full_spec(gain.shape)],
        out_specs=[_row_spec(tr, d), _full_spec((SUBLANES, d))],
        out_shape=[jax.ShapeDtypeStruct((t, d), F32), jax.ShapeDtypeStruct((SUBLANES, d), F32)],
        compiler_params=_cparams(("arbitrary",)), name="pre_mix_bwd",
    )(x, dh, dx1, mod, gain)


def merge_fwd(p, y_attn, y_conv, off_ga, off_gc):
    t, d = y_attn.shape
    tr = _tile(t, ROW_TILE, SUBLANES)
    cw = math.gcd(math.gcd(off_ga, off_gc), math.gcd(d, 512))
    nc = d // cw

    def body(ga_ref, gc_ref, ya_ref, yc_ref, o_ref):
        o_ref[...] = (_sigmoid(ga_ref[...]) * ya_ref[...] + _sigmoid(gc_ref[...]) * yc_ref[...]).astype(BF16)

    return pl.pallas_call(
        body, grid=(t // tr, nc),
        in_specs=[pl.BlockSpec((tr, cw), lambda i, j: (i, off_ga // cw + j)),
                  pl.BlockSpec((tr, cw), lambda i, j: (i, off_gc // cw + j)),
                  pl.BlockSpec((tr, cw), lambda i, j: (i, j)),
                  pl.BlockSpec((tr, cw), lambda i, j: (i, j))],
        out_specs=pl.BlockSpec((tr, cw), lambda i, j: (i, j)),
        out_shape=jax.ShapeDtypeStruct((t, d), BF16),
        compiler_params=_cparams(("parallel", "parallel")), name="merge_fwd",
    )(p, p, y_attn, y_conv)


def merge_bwd(p, y_attn, y_conv, dmerged, off_ga, off_gc):
    t, d = y_attn.shape
    tr = _tile(t, ROW_TILE, SUBLANES)
    cw = math.gcd(math.gcd(off_ga, off_gc), math.gcd(d, 512))
    nc = d // cw

    def body(ga_ref, gc_ref, ya_ref, yc_ref, dm_ref, dya_ref, dyc_ref, dga_ref, dgc_ref):
        dm = dm_ref[...]
        sa = _sigmoid(ga_ref[...])
        sc = _sigmoid(gc_ref[...])
        dya_ref[...] = (dm * sa).astype(BF16)
        dyc_ref[...] = (dm * sc).astype(BF16)
        dga_ref[...] = (dm * ya_ref[...] * sa * (1.0 - sa)).astype(BF16)
        dgc_ref[...] = (dm * yc_ref[...] * sc * (1.0 - sc)).astype(BF16)

    blk = pl.BlockSpec((tr, cw), lambda i, j: (i, j))
    return pl.pallas_call(
        body, grid=(t // tr, nc),
        in_specs=[pl.BlockSpec((tr, cw), lambda i, j: (i, off_ga // cw + j)),
                  pl.BlockSpec((tr, cw), lambda i, j: (i, off_gc // cw + j)), blk, blk, blk],
        out_specs=[blk] * 4,
        out_shape=[jax.ShapeDtypeStruct((t, d), BF16)] * 4,
        compiler_params=_cparams(("parallel", "parallel")), name="merge_bwd",
    )(p, p, y_attn, y_conv, dmerged)


def swiglu_fwd(f):
    t, two = f.shape
    dff = two // 2
    tr = _tile(t, ROW_TILE, SUBLANES)
    cw = math.gcd(dff, 512)
    nc = dff // cw

    def body(g_ref, u_ref, o_ref):
        g = g_ref[...].astype(F32)
        o_ref[...] = (g * _sigmoid(g) * u_ref[...].astype(F32)).astype(BF16)

    return pl.pallas_call(
        body, grid=(t // tr, nc),
        in_specs=[pl.BlockSpec((tr, cw), lambda i, j: (i, j)), pl.BlockSpec((tr, cw), lambda i, j: (i, nc + j))],
        out_specs=pl.BlockSpec((tr, cw), lambda i, j: (i, j)),
        out_shape=jax.ShapeDtypeStruct((t, dff), BF16),
        compiler_params=_cparams(("parallel", "parallel")), name="swiglu_fwd",
    )(f, f)


def swiglu_bwd(f, dact):
    t, two = f.shape
    dff = two // 2
    tr = _tile(t, ROW_TILE, SUBLANES)
    cw = math.gcd(dff, 512)
    nc = dff // cw

    def body(g_ref, u_ref, da_ref, o_ref):
        j = pl.program_id(1)
        g = g_ref[...].astype(F32)
        u = u_ref[...].astype(F32)
        da = da_ref[...]
        s = _sigmoid(g)
        dgate = da * u * (s * (1.0 + g * (1.0 - s)))
        dup = da * (g * s)
        o_ref[...] = jnp.where(j < nc, dgate, dup).astype(BF16)

    return pl.pallas_call(
        body, grid=(t // tr, 2 * nc),
        in_specs=[pl.BlockSpec((tr, cw), lambda i, j: (i, j % nc)),
                  pl.BlockSpec((tr, cw), lambda i, j: (i, nc + j % nc)),
                  pl.BlockSpec((tr, cw), lambda i, j: (i, j % nc))],
        out_specs=pl.BlockSpec((tr, cw), lambda i, j: (i, j)),
        out_shape=jax.ShapeDtypeStruct((t, two), BF16),
        compiler_params=_cparams(("parallel", "parallel")), name="swiglu_bwd",
    )(f, f, dact)


def _t5_bucket_table():
    q_off = np.arange(BLOCK)
    k_off = np.arange(2 * BLOCK)
    dist = q_off[:, None] + BLOCK - k_off[None, :]
    n = np.maximum(dist, 0)
    nf = np.maximum(n, 1).astype(np.float32)
    large = MAX_EXACT + (np.log(nf / np.float32(MAX_EXACT)) / np.float32(math.log(MAX_DISTANCE / MAX_EXACT))
                         * np.float32(NUM_BUCKETS - MAX_EXACT)).astype(np.int32)
    large = np.minimum(large, NUM_BUCKETS - 1)
    bucket = np.where(n < MAX_EXACT, n, large).astype(np.int32)
    allowed = (dist >= 0) & (dist < WINDOW)
    return np.where(allowed, bucket, -1).astype(np.int32)


def bias_table(rel_bias, bucket_p, bucket_c):
    nb, nq = rel_bias.shape

    def body(rb_ref, bkp_ref, bkc_ref, op_ref, oc_ref):
        for bk_ref, o_ref in ((bkp_ref, op_ref), (bkc_ref, oc_ref)):
            bk = bk_ref[...]
            for h in range(nq):
                acc = jnp.full(bk.shape, -jnp.inf, F32)
                for b in range(nb):
                    acc = jnp.where(bk == b, rb_ref[b, h], acc)
                o_ref[h] = acc

    vm = pl.BlockSpec(memory_space=pltpu.VMEM)
    return pl.pallas_call(
        body,
        in_specs=[pl.BlockSpec(memory_space=pltpu.SMEM), vm, vm],
        out_specs=[vm, vm],
        out_shape=[jax.ShapeDtypeStruct((nq,) + bucket_p.shape, F32)] * 2,
        compiler_params=_cparams(), name="bias_table",
    )(rel_bias, bucket_p, bucket_c)


def bias_table_bwd(dbp, dbc, bucket_p, bucket_c):
    nq = dbp.shape[0]

    def body(dbp_ref, dbc_ref, bkp_ref, bkc_ref, o_ref):
        bkp, bkc = bkp_ref[...][None], bkc_ref[...][None]
        dp, dc = dbp_ref[...], dbc_ref[...]
        for b in range(NUM_BUCKETS):
            sel = jnp.where(bkp == b, dp, 0.0) + jnp.where(bkc == b, dc, 0.0)
            o_ref[b] = jnp.sum(jnp.sum(sel, axis=2, keepdims=True), axis=1, keepdims=True)

    vm = pl.BlockSpec(memory_space=pltpu.VMEM)
    return pl.pallas_call(
        body,
        in_specs=[vm] * 4,
        out_specs=vm,
        out_shape=jax.ShapeDtypeStruct((NUM_BUCKETS, nq, 1, 1), F32),
        compiler_params=_cparams(), name="bias_table_bwd",
    )(dbp, dbc, bucket_p, bucket_c)


_NT = (((1,), (1,)), ((), ()))
_TN = (((0,), (0,)), ((), ()))


@jax.custom_vjp
def _bdot_nt(a, b):
    return lax.dot_general(a.astype(BF16), b.astype(BF16), _NT, preferred_element_type=F32)


def _bdot_nt_fwd(a, b):
    return _bdot_nt(a, b), (a, b)


def _bdot_nt_bwd(res, g):
    a, b = res
    gb = g.astype(BF16)
    da = jnp.dot(gb, b.astype(BF16), preferred_element_type=F32)
    db = lax.dot_general(gb, a.astype(BF16), _TN, preferred_element_type=F32)
    return da, db


_bdot_nt.defvjp(_bdot_nt_fwd, _bdot_nt_bwd)


@jax.custom_vjp
def _bdot_nn(a, b):
    return jnp.dot(a.astype(BF16), b.astype(BF16), preferred_element_type=F32)


def _bdot_nn_fwd(a, b):
    return _bdot_nn(a, b), (a, b)


def _bdot_nn_bwd(res, g):
    a, b = res
    gb = g.astype(BF16)
    da = lax.dot_general(gb, b.astype(BF16), _NT, preferred_element_type=F32)
    db = lax.dot_general(a.astype(BF16), gb, _TN, preferred_element_type=F32)
    return da, db


_bdot_nn.defvjp(_bdot_nn_fwd, _bdot_nn_bwd)


def _attn_math(q4, kp, kc, vp, vc, bp, bc, sink4, qg, kg, *, prev_ok, scale):
    g, b, hd = q4.shape
    q = q4.reshape(g * b, hd)
    qn = q * _rms(q) * qg
    kpn = kp * _rms(kp) * kg
    kcn = kc * _rms(kc) * kg
    lp = _bdot_nt(qn, kpn).reshape(g, b, b) * scale + bp
    lc = _bdot_nt(qn, kcn).reshape(g, b, b) * scale + bc
    lp = jnp.where(prev_ok, lp, -jnp.inf)
    m = jnp.maximum(jnp.maximum(jnp.max(lp, axis=-1, keepdims=True), jnp.max(lc, axis=-1, keepdims=True)), sink4)
    m = lax.stop_gradient(m)
    pp = jnp.exp(lp - m)
    pc = jnp.exp(lc - m)
    den = jnp.sum(pp, axis=-1, keepdims=True) + jnp.sum(pc, axis=-1, keepdims=True) + jnp.exp(sink4 - m)
    inv = 1.0 / den
    out = _bdot_nn((pp * inv).reshape(g * b, b), vp) + _bdot_nn((pc * inv).reshape(g * b, b), vc)
    return out.reshape(g, b, hd)


def _attn_specs(grp, hd, nblk, reverse):
    def blk(n):
        return nblk - 1 - n if reverse else n

    return [
        pl.BlockSpec((grp, BLOCK, hd), lambda h, n: (h, blk(n), 0)),
        pl.BlockSpec((None, BLOCK, hd), lambda h, n: (h, jnp.maximum(blk(n) - 1, 0), 0)),
        pl.BlockSpec((None, BLOCK, hd), lambda h, n: (h, blk(n), 0)),
        pl.BlockSpec((None, BLOCK, hd), lambda h, n: (h, jnp.maximum(blk(n) - 1, 0), 0)),
        pl.BlockSpec((None, BLOCK, hd), lambda h, n: (h, blk(n), 0)),
        pl.BlockSpec((grp, BLOCK, BLOCK), lambda h, n: (h, 0, 0)),
        pl.BlockSpec((grp, BLOCK, BLOCK), lambda h, n: (h, 0, 0)),
        pl.BlockSpec((grp, 1, 1), lambda h, n: (h, 0, 0)),
        pl.BlockSpec((1, hd), lambda h, n: (0, 0)),
        pl.BlockSpec((1, hd), lambda h, n: (0, 0)),
    ]


def attn_fwd(qh, kh, vh, bias_p, bias_c, sinks, qg, kg):
    nq, t, hd = qh.shape
    nkv = kh.shape[0]
    grp, nblk = nq // nkv, t // BLOCK
    scale = hd ** -0.5

    def body(q_ref, kp_ref, kc_ref, vp_ref, vc_ref, bp_ref, bc_ref, s_ref, qg_ref, kg_ref, o_ref):
        prev_ok = pl.program_id(1) > 0
        out = _attn_math(q_ref[...], kp_ref[...], kc_ref[...], vp_ref[...], vc_ref[...], bp_ref[...], bc_ref[...],
                         s_ref[...], qg_ref[...], kg_ref[...], prev_ok=prev_ok, scale=scale)
        o_ref[...] = out.astype(BF16)

    return pl.pallas_call(
        body, grid=(nkv, nblk),
        in_specs=_attn_specs(grp, hd, nblk, False),
        out_specs=pl.BlockSpec((grp, BLOCK, hd), lambda h, n: (h, n, 0)),
        out_shape=jax.ShapeDtypeStruct((nq, t, hd), BF16),
        compiler_params=_cparams(("parallel", "parallel")), name="attn_fwd",
    )(qh, kh, kh, vh, vh, bias_p, bias_c, sinks, qg, kg)


def attn_bwd(qh, kh, vh, bias_p, bias_c, sinks, qg, kg, doh):
    nq, t, hd = qh.shape
    nkv = kh.shape[0]
    grp, nblk = nq // nkv, t // BLOCK
    scale = hd ** -0.5

    def body(q_ref, kp_ref, kc_ref, vp_ref, vc_ref, bp_ref, bc_ref, s_ref, qg_ref, kg_ref, do_ref,
             dq_ref, dk_ref, dv_ref, dbp_ref, dbc_ref, ds_ref, dqg_ref, dkg_ref, ck_ref, cv_ref):
        h, i = pl.program_id(0), pl.program_id(1)
        prev_ok = (nblk - 1 - i) > 0
        fn = functools.partial(_attn_math, prev_ok=prev_ok, scale=scale)
        _, vjp = jax.vjp(fn, q_ref[...], kp_ref[...], kc_ref[...], vp_ref[...], vc_ref[...], bp_ref[...], bc_ref[...],
                         s_ref[...], qg_ref[...], kg_ref[...])
        dq, dkp, dkc, dvp, dvc, dbp, dbc, dsk, dqg, dkg = vjp(do_ref[...])
        dq_ref[...] = dq

        @pl.when(i == 0)
        def _():
            ck_ref[...] = jnp.zeros_like(ck_ref)
            cv_ref[...] = jnp.zeros_like(cv_ref)
            dbp_ref[...] = jnp.zeros_like(dbp_ref)
            dbc_ref[...] = jnp.zeros_like(dbc_ref)
            ds_ref[...] = jnp.zeros_like(ds_ref)

        @pl.when((i == 0) & (h == 0))
        def _():
            dqg_ref[...] = jnp.zeros_like(dqg_ref)
            dkg_ref[...] = jnp.zeros_like(dkg_ref)

        dk_ref[...] = dkc + ck_ref[...]
        dv_ref[...] = dvc + cv_ref[...]
        ck_ref[...] = dkp
        cv_ref[...] = dvp
        dbp_ref[...] += dbp
        dbc_ref[...] += dbc
        ds_ref[...] += dsk
        dqg_ref[...] += dqg
        dkg_ref[...] += dkg

    rev = lambda h, n: (h, nblk - 1 - n, 0)
    return pl.pallas_call(
        body, grid=(nkv, nblk),
        in_specs=_attn_specs(grp, hd, nblk, True) + [pl.BlockSpec((grp, BLOCK, hd), rev)],
        out_specs=[
            pl.BlockSpec((grp, BLOCK, hd), rev),
            pl.BlockSpec((None, BLOCK, hd), rev),
            pl.BlockSpec((None, BLOCK, hd), rev),
            pl.BlockSpec((grp, BLOCK, BLOCK), lambda h, n: (h, 0, 0)),
            pl.BlockSpec((grp, BLOCK, BLOCK), lambda h, n: (h, 0, 0)),
            pl.BlockSpec((grp, 1, 1), lambda h, n: (h, 0, 0)),
            pl.BlockSpec((1, hd), lambda h, n: (0, 0)),
            pl.BlockSpec((1, hd), lambda h, n: (0, 0)),
        ],
        out_shape=[
            jax.ShapeDtypeStruct((nq, t, hd), F32),
            jax.ShapeDtypeStruct((nkv, t, hd), F32),
            jax.ShapeDtypeStruct((nkv, t, hd), F32),
            jax.ShapeDtypeStruct((nq, BLOCK, BLOCK), F32),
            jax.ShapeDtypeStruct((nq, BLOCK, BLOCK), F32),
            jax.ShapeDtypeStruct((nq, 1, 1), F32),
            jax.ShapeDtypeStruct((1, hd), F32),
            jax.ShapeDtypeStruct((1, hd), F32),
        ],
        scratch_shapes=[pltpu.VMEM((BLOCK, hd), F32), pltpu.VMEM((BLOCK, hd), F32)],
        compiler_params=_cparams(("arbitrary", "arbitrary")), name="attn_bwd",
    )(qh, kh, kh, vh, vh, bias_p, bias_c, sinks, qg, kg, doh)


CONV_TILE = 256


def _conv_halo_specs(tb, ch, nblk):
    per = tb // CONV_HALO
    last = nblk * per - 1
    cur = pl.BlockSpec((tb, ch), lambda n: (n, 0))
    prev = pl.BlockSpec((CONV_HALO, ch), lambda n: (jnp.maximum(n * per - 1, 0), 0))
    nxt = pl.BlockSpec((CONV_HALO, ch), lambda n: (jnp.minimum((n + 1) * per, last), 0))
    return cur, prev, nxt


def _ln_silu(co, ln_g, ln_b):
    mu = jnp.mean(co, axis=-1, keepdims=True)
    cen = co - mu
    rstd = lax.rsqrt(jnp.mean(cen * cen, axis=-1, keepdims=True) + EPS)
    xhat = cen * rstd
    z = xhat * ln_g + ln_b
    return xhat, rstd, z


def conv_fwd(ca, cb, conv_w, conv_b, ln_g, ln_b):
    t, ch = ca.shape
    tb = _tile(t, CONV_TILE, CONV_HALO)
    nblk = t // tb
    cur, prev, _ = _conv_halo_specs(tb, ch, nblk)
    lead = CONV_HALO - (CONV_WIDTH - 1)

    def body(ca_ref, cb_ref, cap_ref, cbp_ref, w_ref, b_ref, g_ref, bb_ref, s_ref, ubuf):
        n = pl.program_id(0)
        halo = cap_ref[...] * _sigmoid(cbp_ref[...])
        ubuf[pl.ds(0, CONV_HALO), :] = jnp.where(n > 0, halo, 0.0)
        ubuf[pl.ds(CONV_HALO, tb), :] = ca_ref[...] * _sigmoid(cb_ref[...])
        acc = jnp.broadcast_to(b_ref[...], (tb, ch))
        for k in range(CONV_WIDTH):
            acc = acc + w_ref[pl.ds(k, 1), :] * ubuf[pl.ds(lead + k, tb), :]
        _, _, z = _ln_silu(acc, g_ref[...], bb_ref[...])
        s_ref[...] = (z * _sigmoid(z)).astype(BF16)

    vec = _full_spec((1, ch))
    return pl.pallas_call(
        body, grid=(nblk,),
        in_specs=[cur, cur, prev, prev, _full_spec(conv_w.shape), vec, vec, vec],
        out_specs=cur,
        out_shape=jax.ShapeDtypeStruct((t, ch), BF16),
        scratch_shapes=[pltpu.VMEM((CONV_HALO + tb, ch), F32)],
        compiler_params=_cparams(("parallel",)), name="conv_fwd",
    )(ca, cb, ca, cb, conv_w, conv_b, ln_g, ln_b)


def conv_bwd(ca, cb, ds, conv_w, conv_b, ln_g, ln_b):
    t, ch = ca.shape
    tb = _tile(t, CONV_TILE, CONV_HALO)
    nblk = t // tb
    cur, prev, nxt = _conv_halo_specs(tb, ch, nblk)
    lead = CONV_HALO - (CONV_WIDTH - 1)
    ext = tb + CONV_HALO

    def body(ca_ref, cb_ref, cap_ref, cbp_ref, can_ref, cbn_ref, ds_ref, dsn_ref, w_ref, b_ref, g_ref, bb_ref,
             dca_ref, dcb_ref, dw_ref, dvec_ref, ubuf, dbuf):
        n = pl.program_id(0)
        is_last = n == nblk - 1
        sig_b = _sigmoid(cb_ref[...])
        cav = ca_ref[...]
        ubuf[pl.ds(0, CONV_HALO), :] = jnp.where(n > 0, cap_ref[...] * _sigmoid(cbp_ref[...]), 0.0)
        ubuf[pl.ds(CONV_HALO, tb), :] = cav * sig_b
        ubuf[pl.ds(CONV_HALO + tb, CONV_HALO), :] = can_ref[...] * _sigmoid(cbn_ref[...])
        co = jnp.broadcast_to(b_ref[...], (ext, ch))
        for k in range(CONV_WIDTH):
            co = co + w_ref[pl.ds(k, 1), :] * ubuf[pl.ds(lead + k, ext), :]
        xhat, rstd, z = _ln_silu(co, g_ref[...], bb_ref[...])
        dsv = jnp.concatenate([ds_ref[...], jnp.where(is_last, 0.0, dsn_ref[...])], axis=0)
        sg = _sigmoid(z)
        dz = dsv * (sg * (1.0 + z * (1.0 - sg)))
        dxh = dz * g_ref[...]
        dco = rstd * (dxh - jnp.mean(dxh, axis=-1, keepdims=True)
                      - xhat * jnp.mean(dxh * xhat, axis=-1, keepdims=True))
        dbuf[...] = dco

        @pl.when(n == 0)
        def _():
            dw_ref[...] = jnp.zeros_like(dw_ref)
            dvec_ref[...] = jnp.zeros_like(dvec_ref)

        dco_cur = dco[:tb]
        dvec_ref[pl.ds(0, 1), :] += jnp.sum(dco_cur, axis=0, keepdims=True)
        dvec_ref[pl.ds(1, 1), :] += jnp.sum(dz[:tb] * xhat[:tb], axis=0, keepdims=True)
        dvec_ref[pl.ds(2, 1), :] += jnp.sum(dz[:tb], axis=0, keepdims=True)
        du = jnp.zeros((tb, ch), F32)
        for k in range(CONV_WIDTH):
            du = du + w_ref[pl.ds(k, 1), :] * dbuf[pl.ds(CONV_WIDTH - 1 - k, tb), :]
            dw_ref[pl.ds(k, 1), :] += jnp.sum(dco_cur * ubuf[pl.ds(lead + k, tb), :], axis=0, keepdims=True)
        dca_ref[...] = (du * sig_b).astype(BF16)
        dcb_ref[...] = (du * cav * sig_b * (1.0 - sig_b)).astype(BF16)

    vec = _full_spec((1, ch))
    return pl.pallas_call(
        body, grid=(nblk,),
        in_specs=[cur, cur, prev, prev, nxt, nxt, cur, nxt, _full_spec(conv_w.shape), vec, vec, vec],
        out_specs=[cur, cur, _full_spec(conv_w.shape), _full_spec((SUBLANES, ch))],
        out_shape=[jax.ShapeDtypeStruct((t, ch), BF16), jax.ShapeDtypeStruct((t, ch), BF16),
                   jax.ShapeDtypeStruct(conv_w.shape, F32), jax.ShapeDtypeStruct((SUBLANES, ch), F32)],
        scratch_shapes=[pltpu.VMEM((2 * CONV_HALO + tb, ch), F32), pltpu.VMEM((ext, ch), F32)],
        compiler_params=_cparams(("arbitrary",)), name="conv_bwd",
    )(ca, cb, ca, cb, ca, cb, ds, ds, conv_w, conv_b, ln_g, ln_b)


def ada_fwd(c_t, w_ada):
    d, nc = w_ada.shape
    nex = c_t.shape[1]
    tn = _tile(nc, 512)

    def body(ct_ref, w_ref, o_ref):
        w = w_ref[...]
        ct = ct_ref[...]
        cact = ct * _sigmoid(ct)
        rows = [jnp.sum(w * cact[:, b:b + 1], axis=0, keepdims=True) for b in range(nex)]
        o_ref[...] = jnp.concatenate(rows, axis=0)

    return pl.pallas_call(
        body, grid=(nc // tn,),
        in_specs=[_full_spec(c_t.shape), pl.BlockSpec((d, tn), lambda j: (0, j))],
        out_specs=pl.BlockSpec((nex, tn), lambda j: (0, j)),
        out_shape=jax.ShapeDtypeStruct((nex, nc), F32),
        compiler_params=_cparams(("parallel",)), name="ada_fwd",
    )(c_t, w_ada)


def _adamw_math(w, g, m, v):
    m = ADAM_B1 * m + (1.0 - ADAM_B1) * g
    v = ADAM_B2 * v + (1.0 - ADAM_B2) * (g * g)
    m_hat = m / (1.0 - ADAM_B1 ** ADAM_STEP)
    v_hat = v / (1.0 - ADAM_B2 ** ADAM_STEP)
    delta = -ADAM_LR * (m_hat / (jnp.sqrt(v_hat) + ADAM_EPS) + ADAM_WD * w)
    return delta, m, v


def adamw(w, g, m, v, name):
    r, n = w.shape
    tr = _tile(r, 512, SUBLANES)
    tn = _tile(n, 1024)

    def body(w_ref, g_ref, m_ref, v_ref, d_ref, nm_ref, nv_ref):
        d_ref[...], nm_ref[...], nv_ref[...] = _adamw_math(w_ref[...], g_ref[...], m_ref[...], v_ref[...])

    blk = pl.BlockSpec((tr, tn), lambda i, j: (i, j))
    return pl.pallas_call(
        body, grid=(r // tr, n // tn),
        in_specs=[blk] * 4, out_specs=[blk] * 3,
        out_shape=[jax.ShapeDtypeStruct((r, n), F32)] * 3,
        compiler_params=_cparams(("parallel", "parallel")), name=name,
    )(w, g, m, v)


def ada_grad_adamw(c_t, dmod_cols, w, m, v):
    d, nc = w.shape
    nex = c_t.shape[1]
    tr = _tile(d, 512, SUBLANES)
    tn = _tile(nc, 1024)

    def body(ct_ref, dm_ref, w_ref, m_ref, v_ref, g_ref, d_ref, nm_ref, nv_ref):
        ct = ct_ref[...]
        cact = ct * _sigmoid(ct)
        dm = dm_ref[...]
        g = cact[:, 0:1] * dm[0:1, :]
        for b in range(1, nex):
            g = g + cact[:, b:b + 1] * dm[b:b + 1, :]
        g_ref[...] = g
        d_ref[...], nm_ref[...], nv_ref[...] = _adamw_math(w_ref[...], g, m_ref[...], v_ref[...])

    blk = pl.BlockSpec((tr, tn), lambda i, j: (i, j))
    return pl.pallas_call(
        body, grid=(d // tr, nc // tn),
        in_specs=[pl.BlockSpec((tr, nex), lambda i, j: (i, 0)), pl.BlockSpec((nex, tn), lambda i, j: (0, j)),
                  blk, blk, blk],
        out_specs=[blk] * 4,
        out_shape=[jax.ShapeDtypeStruct((d, nc), F32)] * 4,
        compiler_params=_cparams(("parallel", "parallel")), name="ada_grad_adamw",
    )(c_t, dmod_cols, w, m, v)


def small_sum_adamw(gathered, w, m, v):
    ndev, r, n = gathered.shape

    def body(ga_ref, w_ref, m_ref, v_ref, g_ref, d_ref, nm_ref, nv_ref):
        g = ga_ref[0]
        for s in range(1, ndev):
            g = g + ga_ref[s]
        g_ref[...] = g
        d_ref[...], nm_ref[...], nv_ref[...] = _adamw_math(w_ref[...], g, m_ref[...], v_ref[...])

    vm = pl.BlockSpec(memory_space=pltpu.VMEM)
    return pl.pallas_call(
        body, in_specs=[vm] * 4, out_specs=[vm] * 4,
        out_shape=[jax.ShapeDtypeStruct((r, n), F32)] * 4,
        compiler_params=_cparams(), name="small_sum_adamw",
    )(gathered, w, m, v)


def _position():
    return lax.axis_index("x"), lax.axis_index("y"), lax.axis_index("c")


def _other_chips(x, y):
    return [(1 - x, y), (x, 1 - y), (1 - x, 1 - y)]


def allgather_small(block, name):
    rows, width = block.shape

    def body(x_ref, out_ref, send_sems, recv_sems, local_sem):
        x, y, c = _position()
        me, sibling = (x, y, c), (x, y, 1 - c)
        chips = _other_chips(x, y)

        def slot(px, py, pc):
            return out_ref.at[4 * px + 2 * py + pc]

        def copy(k, block_of, to, src=None):
            return pltpu.make_async_remote_copy(
                src_ref=slot(*block_of) if src is None else src, dst_ref=slot(*block_of),
                send_sem=send_sems.at[k], recv_sem=recv_sems.at[k], device_id=to, device_id_type=MESH)

        mine = pltpu.make_async_copy(x_ref, slot(*me), local_sem)
        mine.start()
        first = [copy(0, me, sibling, src=x_ref)]
        first += [copy(1 + j, me, (*chip, c), src=x_ref) for j, chip in enumerate(chips)]
        for cp in first:
            cp.start()
        passed = [copy(4 + j, (*chip, c), sibling) for j, chip in enumerate(chips)]
        for j, chip in enumerate(chips):
            copy(1 + j, (*chip, c), me).wait_recv()
            passed[j].start()
        copy(0, sibling, me).wait_recv()
        for j, chip in enumerate(chips):
            copy(4 + j, (*chip, 1 - c), me).wait_recv()
        for cp in first + passed:
            cp.wait_send()
        mine.wait()

    return pl.pallas_call(
        body,
        out_shape=jax.ShapeDtypeStruct((N_DEV, rows, width), block.dtype),
        in_specs=[pl.BlockSpec(memory_space=pltpu.VMEM)],
        out_specs=pl.BlockSpec(memory_space=pltpu.VMEM),
        scratch_shapes=[pltpu.SemaphoreType.DMA((7,)), pltpu.SemaphoreType.DMA((7,)), pltpu.SemaphoreType.DMA],
        compiler_params=_cparams(), name=name,
    )(block)


def allgather_weights(shards):
    nw = len(shards)
    any_spec = pl.BlockSpec(memory_space=pl.ANY)

    def body(*refs):
        x_refs, out_refs = refs[:nw], refs[nw:2 * nw]
        send_sems, recv_sems, local_sems = refs[2 * nw:]
        x, y, c = _position()
        sibling = (x, y, 1 - c)
        chips = _other_chips(x, y)
        my_chip = 2 * x + y

        def half_rows(k, chip_idx, pc):
            half = shards[k].shape[0] // 2
            return pl.ds((2 * chip_idx + pc) * half, half)

        def copy(k, s, chip_idx, pc, to, src=None):
            dst = out_refs[k].at[half_rows(k, chip_idx, pc), :]
            return pltpu.make_async_remote_copy(
                src_ref=dst if src is None else src, dst_ref=dst,
                send_sem=send_sems.at[k, s], recv_sem=recv_sems.at[k, s], device_id=to, device_id_type=MESH)

        local, first, passed = [], [], []
        for k in range(nw):
            r = shards[k].shape[0]
            lc = pltpu.make_async_copy(x_refs[k], out_refs[k].at[pl.ds(my_chip * r, r), :], local_sems.at[k])
            lc.start()
            local.append(lc)
            src = x_refs[k].at[pl.ds(c * (r // 2), r // 2), :]
            for j, chip in enumerate(chips):
                cp = copy(k, j, my_chip, c, (*chip, c), src=src)
                cp.start()
                first.append(cp)
        for k in range(nw):
            for j, (px, py) in enumerate(chips):
                copy(k, j, 2 * px + py, c, (x, y, c)).wait_recv()
                fw = copy(k, 3 + j, 2 * px + py, c, sibling)
                fw.start()
                passed.append(fw)
        for k in range(nw):
            for j, (px, py) in enumerate(chips):
                copy(k, 3 + j, 2 * px + py, 1 - c, (x, y, c)).wait_recv()
        for cp in first + passed:
            cp.wait_send()
        for lc in local:
            lc.wait()

    outs = pl.pallas_call(
        body,
        out_shape=[jax.ShapeDtypeStruct((N_CHIPS * s.shape[0], s.shape[1]), s.dtype) for s in shards],
        in_specs=[any_spec] * nw, out_specs=[any_spec] * nw,
        scratch_shapes=[pltpu.SemaphoreType.DMA((nw, 6)), pltpu.SemaphoreType.DMA((nw, 6)),
                        pltpu.SemaphoreType.DMA((nw,))],
        compiler_params=_cparams(), name="allgather_weights",
    )(*shards)
    return [o.reshape(N_CHIPS, s.shape[0], s.shape[1]) for o, s in zip(outs, shards)]


def pair_exchange(grads):
    nw = len(grads)
    any_spec = pl.BlockSpec(memory_space=pl.ANY)

    def body(*refs):
        g_refs, r_refs = refs[:nw], refs[nw:2 * nw]
        send_sems, recv_sems = refs[2 * nw:]
        x, y, c = _position()
        sibling = (x, y, 1 - c)
        copies = []
        for k in range(nw):
            for chip in range(N_CHIPS):
                cp = pltpu.make_async_remote_copy(
                    src_ref=g_refs[k].at[chip, 1 - c], dst_ref=r_refs[k].at[chip],
                    send_sem=send_sems.at[k, chip], recv_sem=recv_sems.at[k, chip],
                    device_id=sibling, device_id_type=MESH)
                cp.start()
                copies.append(cp)
        for cp in copies:
            cp.wait()

    return pl.pallas_call(
        body,
        out_shape=[jax.ShapeDtypeStruct((N_CHIPS,) + g.shape[2:], g.dtype) for g in grads],
        in_specs=[any_spec] * nw, out_specs=[any_spec] * nw,
        scratch_shapes=[pltpu.SemaphoreType.DMA((nw, N_CHIPS)), pltpu.SemaphoreType.DMA((nw, N_CHIPS))],
        compiler_params=_cparams(), name="pair_exchange",
    )(*grads)


def pair_sum(g, r, core):
    nchip, _, h, n = g.shape
    th = _tile(h, 512, 16)
    tn = _tile(n, 1024)

    def body(core_ref, g_ref, r_ref, o_ref):
        o_ref[...] = (g_ref[...].astype(F32) + r_ref[...].astype(F32)).astype(BF16)

    return pl.pallas_call(
        body,
        grid_spec=pltpu.PrefetchScalarGridSpec(
            num_scalar_prefetch=1, grid=(nchip, h // th, n // tn),
            in_specs=[pl.BlockSpec((None, None, th, tn), lambda a, i, j, cr: (a, cr[0], i, j)),
                      pl.BlockSpec((None, th, tn), lambda a, i, j, cr: (a, i, j))],
            out_specs=pl.BlockSpec((None, th, tn), lambda a, i, j, cr: (a, i, j))),
        out_shape=jax.ShapeDtypeStruct((nchip, h, n), BF16),
        compiler_params=_cparams(("parallel", "parallel", "parallel")), name="pair_sum",
    )(core, g, r)


def chip_exchange(sums):
    nw = len(sums)
    any_spec = pl.BlockSpec(memory_space=pl.ANY)

    def body(*refs):
        s_refs, r_refs = refs[:nw], refs[nw:2 * nw]
        send_sems, recv_sems, local_sems = refs[2 * nw:]
        x, y, c = _position()
        chips = _other_chips(x, y)
        my_chip = 2 * x + y
        local, sends = [], []
        for k in range(nw):
            lc = pltpu.make_async_copy(s_refs[k].at[my_chip], r_refs[k].at[my_chip], local_sems.at[k])
            lc.start()
            local.append(lc)
            for j, (px, py) in enumerate(chips):
                cp = pltpu.make_async_remote_copy(
                    src_ref=s_refs[k].at[2 * px + py], dst_ref=r_refs[k].at[my_chip],
                    send_sem=send_sems.at[k, j], recv_sem=recv_sems.at[k, j],
                    device_id=(px, py, c), device_id_type=MESH)
                cp.start()
                sends.append(cp)
        for cp in sends:
            cp.wait()
        for lc in local:
            lc.wait()

    return pl.pallas_call(
        body,
        out_shape=[jax.ShapeDtypeStruct(s.shape, s.dtype) for s in sums],
        in_specs=[any_spec] * nw, out_specs=[any_spec] * nw,
        scratch_shapes=[pltpu.SemaphoreType.DMA((nw, 3)), pltpu.SemaphoreType.DMA((nw, 3)),
                        pltpu.SemaphoreType.DMA((nw,))],
        compiler_params=_cparams(), name="chip_exchange",
    )(*sums)


def chip_sum(r):
    nchip, h, n = r.shape
    th = _tile(h, 512, 16)
    tn = _tile(n, 1024)

    def body(r_ref, o_ref):
        acc = r_ref[0].astype(F32)
        for s in range(1, nchip):
            acc = acc + r_ref[s].astype(F32)
        o_ref[...] = acc

    return pl.pallas_call(
        body, grid=(h // th, n // tn),
        in_specs=[pl.BlockSpec((nchip, th, tn), lambda i, j: (0, i, j))],
        out_specs=pl.BlockSpec((th, tn), lambda i, j: (i, j)),
        out_shape=jax.ShapeDtypeStruct((h, n), F32),
        compiler_params=_cparams(("parallel", "parallel")), name="chip_sum",
    )(r)


def pair_share(halves):
    nw = len(halves)
    any_spec = pl.BlockSpec(memory_space=pl.ANY)

    def body(*refs):
        h_refs, o_refs = refs[:nw], refs[nw:2 * nw]
        send_sems, recv_sems, local_sems = refs[2 * nw:]
        x, y, c = _position()
        sibling = (x, y, 1 - c)
        local, sends = [], []
        for k in range(nw):
            lc = pltpu.make_async_copy(h_refs[k], o_refs[k].at[c], local_sems.at[k])
            lc.start()
            local.append(lc)
            cp = pltpu.make_async_remote_copy(
                src_ref=h_refs[k], dst_ref=o_refs[k].at[c], send_sem=send_sems.at[k], recv_sem=recv_sems.at[k],
                device_id=sibling, device_id_type=MESH)
            cp.start()
            sends.append(cp)
        for cp in sends:
            cp.wait()
        for lc in local:
            lc.wait()

    return pl.pallas_call(
        body,
        out_shape=[jax.ShapeDtypeStruct((2,) + h.shape, h.dtype) for h in halves],
        in_specs=[any_spec] * nw, out_specs=[any_spec] * nw,
        scratch_shapes=[pltpu.SemaphoreType.DMA((nw,)), pltpu.SemaphoreType.DMA((nw,)),
                        pltpu.SemaphoreType.DMA((nw,))],
        compiler_params=_cparams(), name="pair_share",
    )(*halves)


def _pad_rows(a, rows):
    return jnp.concatenate([a, jnp.zeros((rows - a.shape[0],) + a.shape[1:], a.dtype)], axis=0)


def _pack_rows(parts, total_rows):
    rows = []
    for p in parts:
        flat = p.reshape(-1)
        pad = (-flat.shape[0]) % LANES
        if pad:
            flat = jnp.concatenate([flat, jnp.zeros((pad,), flat.dtype)])
        rows.append(flat.reshape(-1, LANES))
    used = sum(r.shape[0] for r in rows)
    if total_rows > used:
        rows.append(jnp.zeros((total_rows - used, LANES), rows[0].dtype))
    return jnp.concatenate(rows, axis=0)


def _packed_rows(shapes):
    used = sum(-(-int(np.prod(s)) // LANES) for s in shapes)
    return -(-used // SUBLANES) * SUBLANES


def _unpack_rows(packed, shapes):
    out, row = [], 0
    for s in shapes:
        size = int(np.prod(s))
        nrows = -(-size // LANES)
        out.append(packed[row:row + nrows].reshape(-1)[:size].reshape(s))
        row += nrows
    return out


def kernel(x, c, w_ada, b_ada, norm_mix_g, w_in, q_norm_g, k_norm_g, attn_sinks, rel_bias, w_attn_out, conv_w, conv_b, conv_ln_g, conv_ln_b, w_conv_out, w_mix_out, norm_ffn_g, w_ffn_in, w_ffn_out, loss_target, m_w_ada, m_b_ada, m_norm_mix_g, m_w_in, m_q_norm_g, m_k_norm_g, m_attn_sinks, m_rel_bias, m_w_attn_out, m_conv_w, m_conv_b, m_conv_ln_g, m_conv_ln_b, m_w_conv_out, m_w_mix_out, m_norm_ffn_g, m_w_ffn_in, m_w_ffn_out, v_w_ada, v_b_ada, v_norm_mix_g, v_w_in, v_q_norm_g, v_k_norm_g, v_attn_sinks, v_rel_bias, v_w_attn_out, v_conv_w, v_conv_b, v_conv_ln_g, v_conv_ln_b, v_w_conv_out, v_w_mix_out, v_norm_ffn_g, v_w_ffn_in, v_w_ffn_out):
    xi, yi, ci = _position()
    chip = 2 * xi + yi
    me = 2 * chip + ci

    xe, tgt = x[0], loss_target[0]
    t, d = xe.shape
    hd = q_norm_g.shape[-1]
    nq = attn_sinks.shape[-1]
    aw = nq * hd
    ch = conv_b.shape[-1]
    in_width = N_CHIPS * w_in.shape[-1]
    kvw = (in_width - aw - 2 * ch - 2 * d) // 2
    nkv = kvw // hd
    dff = N_CHIPS * w_ffn_out.shape[1]
    off_k, off_v, off_ca = aw, aw + kvw, aw + 2 * kvw
    off_cb, off_ga, off_gc = off_ca + ch, off_ca + 2 * ch, off_ca + 2 * ch + d
    nc_ada = w_ada.shape[-1]

    ch_loc = conv_w.shape[-1]
    pack1 = _pack_rows([c[0], conv_w[0]], _packed_rows([(d,), (CONV_WIDTH, ch_loc)]))
    got1 = allgather_small(pack1, "allgather_cond")
    c_rows = d // LANES
    c_all = got1[:, :c_rows].reshape(N_DEV, d)
    w_rows = -(-CONV_WIDTH * ch_loc // LANES)
    conv_w_full = got1[0::2, c_rows:c_rows + w_rows].reshape(N_CHIPS, -1)[:, :CONV_WIDTH * ch_loc]
    conv_w_full = jnp.transpose(conv_w_full.reshape(N_CHIPS, CONV_WIDTH, ch_loc), (1, 0, 2)).reshape(CONV_WIDTH, ch)
    conv_w_pad = _pad_rows(conv_w_full, CONV_WIDTH + 1)

    big = [w_in[0], w_attn_out[0], w_conv_out[0], w_mix_out[0], w_ffn_in[0], w_ffn_out[0]]
    wg_in, wg_attn_out, wg_conv_out, wg_mix_out, wg_ffn_in, wg_ffn_out = allgather_weights(
        [w.astype(BF16) for w in big])
    wg_mix_out = wg_mix_out.reshape(1, d, d)
    wg_ffn_out = wg_ffn_out.reshape(1, dff, d)

    c_t = jnp.transpose(c_all)
    mod_cols = ada_fwd(c_t, w_ada[0])
    got2 = allgather_small(mod_cols.reshape(-1, LANES), "allgather_mod")
    mod_all = got2.reshape(N_CHIPS, 2, N_DEV, nc_ada)[:, 0]
    mod = lax.dynamic_slice_in_dim(mod_all, me, 1, axis=1).reshape(1, N_CHIPS * nc_ada) + b_ada
    mod = _pad_rows(mod.reshape(N_MOD, d), SUBLANES)

    h = pre_mix_fwd(xe, mod, norm_mix_g)
    p = mm_nn(h, wg_in, tn=_tile(wg_in.shape[2], 640), tk=d, out_dtype=F32, name="mm_in")

    def heads(a, n):
        return jnp.transpose(a.reshape(t, n, hd), (1, 0, 2))

    def unheads(a):
        return jnp.transpose(a, (1, 0, 2)).reshape(t, -1)

    qh, kh, vh = heads(p[:, :aw], nq), heads(p[:, off_k:off_v], nkv), heads(p[:, off_v:off_ca], nkv)
    bucket = _t5_bucket_table()
    bucket_p, bucket_c = jnp.asarray(bucket[:, :BLOCK]), jnp.asarray(bucket[:, BLOCK:])
    bias_p, bias_c = bias_table(rel_bias, bucket_p, bucket_c)
    sinks3 = attn_sinks.reshape(nq, 1, 1)
    attn_o = unheads(attn_fwd(qh, kh, vh, bias_p, bias_c, sinks3, q_norm_g, k_norm_g))
    ca, cb = p[:, off_ca:off_cb], p[:, off_cb:off_ga]
    s_conv = conv_fwd(ca, cb, conv_w_pad, conv_b, conv_ln_g, conv_ln_b)
    y_attn = mm_nn(attn_o, wg_attn_out, tn=_tile(wg_attn_out.shape[2], 512), tk=aw, out_dtype=F32, name="mm_attn_out")
    y_conv = mm_nn(s_conv, wg_conv_out, tn=_tile(wg_conv_out.shape[2], 512), tk=ch, out_dtype=F32, name="mm_conv_out")
    merged = merge_fwd(p, y_attn, y_conv, off_ga, off_gc)
    o_m = mm_nn(merged, wg_mix_out, tn=_tile(d, 512), tk=d, out_dtype=F32, name="mm_mix_out")
    x1, h2 = pre_ffn_fwd(xe, o_m, mod, norm_ffn_g)
    f = mm_nn(h2, wg_ffn_in, tn=_tile(wg_ffn_in.shape[2], 1408), tk=d, out_dtype=BF16, name="mm_ffn_in")
    act = swiglu_fwd(f)
    o_f = mm_nn(act, wg_ffn_out, tn=_tile(d, 1024), tk=_tile(dff, 512), out_dtype=F32, name="mm_ffn_out")
    loss11, dy, dof, acc_l = loss_head(x1, o_f, tgt, mod)

    dact = mm_nt(dof, wg_ffn_out, tko=_tile(dff, 512), tn=d, name="mm_ffn_out_dx")
    gw_ffn_out = mm_tn(act, dof, 1, tk=_tile(dff, 512), tn=_tile(d, 1024), name="mm_ffn_out_dw")
    df = swiglu_bwd(f, dact)
    gw_ffn_in = mm_tn(h2, df, N_CHIPS, tk=_tile(d, 512), tn=_tile(wg_ffn_in.shape[2], 1408), name="mm_ffn_in_dw")
    dh2 = mm_nt(df, wg_ffn_in, tko=_tile(d, 1024), tn=_tile(wg_ffn_in.shape[2], 1408), name="mm_ffn_in_dx")
    dx1, dom, acc_f = pre_ffn_bwd(x1, dh2, dy, o_m, mod, norm_ffn_g)
    dmerged = mm_nt(dom, wg_mix_out, tko=_tile(d, 512), tn=d, name="mm_mix_out_dx")
    gw_mix_out = mm_tn(merged, dom, 1, tk=_tile(d, 512), tn=_tile(d, 1024), name="mm_mix_out_dw")
    dy_attn, dy_conv, dga, dgc = merge_bwd(p, y_attn, y_conv, dmerged, off_ga, off_gc)
    dattn_o = mm_nt(dy_attn, wg_attn_out, tko=_tile(aw, 1024), tn=_tile(wg_attn_out.shape[2], 512), name="mm_attn_out_dx")
    gw_attn_out = mm_tn(attn_o, dy_attn, N_CHIPS, tk=_tile(aw, 512), tn=_tile(wg_attn_out.shape[2], 512), name="mm_attn_out_dw")
    ds_conv = mm_nt(dy_conv, wg_conv_out, tko=_tile(ch, 1024), tn=_tile(wg_conv_out.shape[2], 512), name="mm_conv_out_dx")
    gw_conv_out = mm_tn(s_conv, dy_conv, N_CHIPS, tk=_tile(ch, 512), tn=_tile(wg_conv_out.shape[2], 512), name="mm_conv_out_dw")
    dca, dcb, dconv_w, dconv_vec = conv_bwd(ca, cb, ds_conv, conv_w_pad, conv_b, conv_ln_g, conv_ln_b)
    dqh, dkh, dvh, dbp, dbc, dsinks, dqg, dkg = attn_bwd(qh, kh, vh, bias_p, bias_c, sinks3, q_norm_g, k_norm_g,
                                                          heads(dattn_o, nq))
    drel = bias_table_bwd(dbp, dbc, bucket_p, bucket_c).reshape(NUM_BUCKETS, nq)
    dp = jnp.concatenate([unheads(dqh).astype(BF16), unheads(dkh).astype(BF16), unheads(dvh).astype(BF16),
                          dca, dcb, dga, dgc], axis=1)
    gw_in = mm_tn(h, dp, N_CHIPS, tk=_tile(d, 512), tn=_tile(wg_in.shape[2], 640), name="mm_in_dw")
    dh = mm_nt(dp, wg_in, tko=_tile(d, 1024), tn=_tile(wg_in.shape[2], 640), name="mm_in_dx")
    grad_x, acc_m = pre_mix_bwd(xe, dh, dx1, mod, norm_mix_g)

    dmod = jnp.concatenate([acc_m[0:1], acc_m[1:2], acc_f[3:4], acc_f[0:1], acc_f[1:2], acc_l[0:1]], axis=1)
    small_names = ["b_ada", "norm_mix_g", "q_norm_g", "k_norm_g", "attn_sinks", "rel_bias", "conv_b", "conv_ln_g",
                   "conv_ln_b", "norm_ffn_g"]
    small_w = [b_ada, norm_mix_g, q_norm_g, k_norm_g, attn_sinks, rel_bias, conv_b, conv_ln_g, conv_ln_b, norm_ffn_g]
    small_m = [m_b_ada, m_norm_mix_g, m_q_norm_g, m_k_norm_g, m_attn_sinks, m_rel_bias, m_conv_b, m_conv_ln_g,
               m_conv_ln_b, m_norm_ffn_g]
    small_v = [v_b_ada, v_norm_mix_g, v_q_norm_g, v_k_norm_g, v_attn_sinks, v_rel_bias, v_conv_b, v_conv_ln_g,
               v_conv_ln_b, v_norm_ffn_g]
    small_g = [dmod, acc_m[2:3], dqg, dkg, dsinks.reshape(1, nq), drel, dconv_vec[0:1], dconv_vec[1:2],
               dconv_vec[2:3], acc_f[2:3]]
    small_shapes = [w.shape for w in small_w]
    conv_shape = (CONV_WIDTH, ch)
    n_rows = _packed_rows(small_shapes + [conv_shape])
    packed_g = _pack_rows(small_g + [dconv_w[:CONV_WIDTH]], n_rows)
    got3 = allgather_small(packed_g, "allgather_small_grads")
    zero_conv = jnp.zeros(conv_shape, F32)
    g_small, d_small, nm_small, nv_small = small_sum_adamw(
        got3, _pack_rows(small_w + [zero_conv], n_rows), _pack_rows(small_m + [zero_conv], n_rows),
        _pack_rows(small_v + [zero_conv], n_rows))
    g_parts = _unpack_rows(g_small, small_shapes + [conv_shape])
    d_parts = _unpack_rows(d_small, small_shapes)
    nm_parts = _unpack_rows(nm_small, small_shapes)
    nv_parts = _unpack_rows(nv_small, small_shapes)
    grads = dict(zip(small_names, g_parts[:-1]))
    deltas = dict(zip(small_names, d_parts))
    new_m = dict(zip(small_names, nm_parts))
    new_v = dict(zip(small_names, nv_parts))

    g_conv_w = lax.dynamic_slice_in_dim(g_parts[-1], chip * ch_loc, ch_loc, axis=1)
    grads["conv_w"] = g_conv_w[None]
    dl, nm, nv = adamw(conv_w[0], g_conv_w, m_conv_w[0], v_conv_w[0], "adamw_conv_w")
    deltas["conv_w"], new_m["conv_w"], new_v["conv_w"] = dl[None], nm[None], nv[None]

    dmod_all = got3[:, :N_MOD * d // LANES].reshape(N_DEV, N_MOD * d)
    dmod_cols = lax.dynamic_slice_in_dim(dmod_all, chip * nc_ada, nc_ada, axis=1)
    g_ada, dl, nm, nv = ada_grad_adamw(c_t, dmod_cols, w_ada[0], m_w_ada[0], v_w_ada[0])
    grads["w_ada"], deltas["w_ada"], new_m["w_ada"], new_v["w_ada"] = g_ada[None], dl[None], nm[None], nv[None]

    big_names = ["w_in", "w_attn_out", "w_conv_out", "w_mix_out", "w_ffn_in", "w_ffn_out"]
    big_m = [m_w_in, m_w_attn_out, m_w_conv_out, m_w_mix_out, m_w_ffn_in, m_w_ffn_out]
    big_v = [v_w_in, v_w_attn_out, v_w_conv_out, v_w_mix_out, v_w_ffn_in, v_w_ffn_out]
    partial = [gw_in, gw_attn_out, gw_conv_out, gw_mix_out, gw_ffn_in, gw_ffn_out]
    blocks = []
    for w, g in zip(big, partial):
        r, n = w.shape
        blocks.append(g.reshape(N_CHIPS, 2, r // 2, n))
    from_sibling = pair_exchange(blocks)
    core = ci.astype(jnp.int32).reshape(1)
    pair_sums = [pair_sum(g, r, core) for g, r in zip(blocks, from_sibling)]
    contribs = chip_exchange(pair_sums)
    halves = [chip_sum(r) for r in contribs]
    fulls = pair_share(halves)
    for name, w, g2, m_, v_ in zip(big_names, big, fulls, big_m, big_v):
        g = g2.reshape(w.shape)
        dl, nm, nv = adamw(w, g, m_[0], v_[0], "adamw_" + name)
        grads[name], deltas[name], new_m[name], new_v[name] = g[None], dl[None], nm[None], nv[None]

    loss = lax.psum(loss11[0, 0], ("x", "y", "c"))
    order = ["w_ada", "b_ada", "norm_mix_g", "w_in", "q_norm_g", "k_norm_g", "attn_sinks", "rel_bias", "w_attn_out",
             "conv_w", "conv_b", "conv_ln_g", "conv_ln_b", "w_conv_out", "w_mix_out", "norm_ffn_g", "w_ffn_in",
             "w_ffn_out"]
    return (loss, grad_x[None], *[grads[n] for n in order], *[deltas[n] for n in order],
            *[new_m[n] for n in order], *[new_v[n] for n in order])
```

```python
import functools
import math

import jax
import jax.numpy as jnp
import numpy as np
from jax import lax
from jax.experimental import pallas as pl
from jax.experimental.pallas import tpu as pltpu

F32 = jnp.float32
BF16 = jnp.bfloat16
MESH = pl.DeviceIdType.MESH

V7X_VMEM_BYTES = 64 * 1024 * 1024
VMEM_LIMIT = V7X_VMEM_BYTES - 8 * 1024 * 1024
LANES = 128
SUBLANES = 8
BF16_SUBLANES = 16

EPS = 1e-6
WINDOW = 128
BLOCK = 128
NUM_BUCKETS = 32
MAX_EXACT = NUM_BUCKETS // 2
MAX_DISTANCE = 128
CONV_WIDTH = 31
CONV_HALO = 32
ADAM_LR = 0.001
ADAM_B1 = 0.9
ADAM_B2 = 0.999
ADAM_EPS = 1e-08
ADAM_WD = 0.01
ADAM_STEP = 10
N_MOD = 6
SH_M, SC_M, GT_M, SH_F, SC_F, GT_F = range(6)

N_CHIPS = 4
N_DEV = 8


def _cparams(sem=None):
    return pltpu.CompilerParams(dimension_semantics=sem, vmem_limit_bytes=VMEM_LIMIT)


def _tile(n, pref, unit=LANES):
    best = None
    for t in range(unit, min(n, pref) + 1, unit):
        if n % t == 0:
            best = t
    return best if best is not None else n


def _sigmoid(v):
    return 1.0 / (1.0 + jnp.exp(-v))


def _after(x, *deps):
    return lax.optimization_barrier((x, *deps))[0]


ROW_CHUNK = 512


def _row_chunks(m):
    step = ROW_CHUNK if m % ROW_CHUNK == 0 else m
    return [(s, step) for s in range(0, m, step)]


def mm_nn(a, w, *, tn, tk, out_dtype, name):
    m, k = a.shape
    j, k2, nj = w.shape
    assert k == k2 and nj % tn == 0 and k % tk == 0
    npj, nk = nj // tn, k // tk

    def body(a_ref, w_ref, o_ref, *scratch):
        kk = pl.program_id(1)
        for s, sz in _row_chunks(m):
            rows = pl.ds(s, sz)
            p = jnp.dot(a_ref[rows, :], w_ref[...], preferred_element_type=F32)
            if nk == 1:
                o_ref[rows, :] = p.astype(out_dtype)
            else:
                acc = scratch[0]

                @pl.when(kk == 0)
                def _():
                    acc[rows, :] = p

                @pl.when(kk > 0)
                def _():
                    acc[rows, :] += p

                @pl.when(kk == nk - 1)
                def _():
                    o_ref[rows, :] = acc[rows, :].astype(out_dtype)

    return pl.pallas_call(
        body,
        grid=(j * npj, nk),
        in_specs=[
            pl.BlockSpec((m, tk), lambda n, kk: (0, kk)),
            pl.BlockSpec((None, tk, tn), lambda n, kk: (n // npj, kk, n % npj)),
        ],
        out_specs=pl.BlockSpec((m, tn), lambda n, kk: (0, n)),
        out_shape=jax.ShapeDtypeStruct((m, j * nj), out_dtype),
        scratch_shapes=[pltpu.VMEM((m, tn), F32)] if nk > 1 else [],
        compiler_params=_cparams(("parallel", "arbitrary")),
        name=name,
    )(a, w)


def mm_nt(g, w, *, tko, tn, name):
    m, n = g.shape
    j, k, nj = w.shape
    assert n == j * nj and nj % tn == 0 and k % tko == 0
    npj, nr = nj // tn, n // tn

    def body(g_ref, w_ref, o_ref):
        r = pl.program_id(1)
        for s, sz in _row_chunks(m):
            rows = pl.ds(s, sz)
            p = lax.dot_general(g_ref[rows, :], w_ref[...], (((1,), (1,)), ((), ())), preferred_element_type=F32)

            @pl.when(r == 0)
            def _():
                o_ref[rows, :] = p

            @pl.when(r > 0)
            def _():
                o_ref[rows, :] += p

    return pl.pallas_call(
        body,
        grid=(k // tko, nr),
        in_specs=[
            pl.BlockSpec((m, tn), lambda ko, r: (0, r)),
            pl.BlockSpec((None, tko, tn), lambda ko, r: (r // npj, ko, r % npj)),
        ],
        out_specs=pl.BlockSpec((m, tko), lambda ko, r: (0, ko)),
        out_shape=jax.ShapeDtypeStruct((m, k), F32),
        compiler_params=_cparams(("parallel", "arbitrary")),
        name=name,
    )(g, w)


def mm_tn(a, g, n_blocks, *, tk, tn, name):
    m, k = a.shape
    m2, n = g.shape
    nj = n // n_blocks
    assert m == m2 and nj % tn == 0 and k % tk == 0
    npj = nj // tn

    def body(a_ref, g_ref, o_ref):
        p = lax.dot_general(a_ref[...], g_ref[...], (((0,), (0,)), ((), ())), preferred_element_type=F32)
        o_ref[...] = p.astype(BF16)

    return pl.pallas_call(
        body,
        grid=(k // tk, n // tn),
        in_specs=[
            pl.BlockSpec((m, tk), lambda kk, nn: (0, kk)),
            pl.BlockSpec((m, tn), lambda kk, nn: (0, nn)),
        ],
        out_specs=pl.BlockSpec((None, tk, tn), lambda kk, nn: (nn // npj, kk, nn % npj)),
        out_shape=jax.ShapeDtypeStruct((n_blocks, k, nj), BF16),
        compiler_params=_cparams(("parallel", "parallel")),
        name=name,
    )(a, g)


ROW_TILE = 256


def _row_spec(tr, width):
    return pl.BlockSpec((tr, width), lambda i: (i, 0))


def _full_spec(shape):
    return pl.BlockSpec(shape, lambda *_: (0,) * len(shape))


def _rms(xv):
    return lax.rsqrt(jnp.mean(xv * xv, axis=-1, keepdims=True) + EPS)


def _mod_row(mod_ref, row):
    return mod_ref[pl.ds(row, 1), :]


def pre_mix_fwd(x, mod, gain):
    t, d = x.shape
    tr = _tile(t, ROW_TILE, SUBLANES)

    def body(x_ref, mod_ref, g_ref, h_ref):
        xv = x_ref[...]
        y = xv * _rms(xv) * g_ref[...]
        h_ref[...] = (y * (1.0 + _mod_row(mod_ref, SC_M)) + _mod_row(mod_ref, SH_M)).astype(BF16)

    return pl.pallas_call(
        body, grid=(t // tr,),
        in_specs=[_row_spec(tr, d), _full_spec(mod.shape), _full_spec(gain.shape)],
        out_specs=_row_spec(tr, d),
        out_shape=jax.ShapeDtypeStruct((t, d), BF16),
        compiler_params=_cparams(("parallel",)), name="pre_mix_fwd",
    )(x, mod, gain)


def pre_ffn_fwd(x, o_m, mod, gain):
    t, d = x.shape
    tr = _tile(t, ROW_TILE, SUBLANES)

    def body(x_ref, om_ref, mod_ref, g_ref, x1_ref, h_ref):
        x1 = x_ref[...] + _mod_row(mod_ref, GT_M) * om_ref[...]
        x1_ref[...] = x1
        y = x1 * _rms(x1) * g_ref[...]
        h_ref[...] = (y * (1.0 + _mod_row(mod_ref, SC_F)) + _mod_row(mod_ref, SH_F)).astype(BF16)

    return pl.pallas_call(
        body, grid=(t // tr,),
        in_specs=[_row_spec(tr, d), _row_spec(tr, d), _full_spec(mod.shape), _full_spec(gain.shape)],
        out_specs=[_row_spec(tr, d), _row_spec(tr, d)],
        out_shape=[jax.ShapeDtypeStruct((t, d), F32), jax.ShapeDtypeStruct((t, d), BF16)],
        compiler_params=_cparams(("parallel",)), name="pre_ffn_fwd",
    )(x, o_m, mod, gain)


def loss_head(x1, o_f, target, mod):
    t, d = x1.shape
    tr = _tile(t, ROW_TILE, SUBLANES)

    def body(x1_ref, of_ref, tg_ref, mod_ref, loss_ref, dy_ref, dof_ref, acc_ref):
        i = pl.program_id(0)
        gt = _mod_row(mod_ref, GT_F)
        of = of_ref[...]
        err = x1_ref[...] + gt * of - tg_ref[...]
        dy = err * (1.0 / d)
        dy_ref[...] = dy
        dof_ref[...] = (dy * gt).astype(BF16)
        part = (0.5 / d) * jnp.sum(jnp.sum(err * err, axis=1, keepdims=True), axis=0, keepdims=True)
        dgt = jnp.sum(dy * of, axis=0, keepdims=True)

        @pl.when(i == 0)
        def _():
            loss_ref[...] = jnp.zeros_like(loss_ref)
            acc_ref[...] = jnp.zeros_like(acc_ref)

        loss_ref[...] += part
        acc_ref[pl.ds(0, 1), :] += dgt

    return pl.pallas_call(
        body, grid=(t // tr,),
        in_specs=[_row_spec(tr, d), _row_spec(tr, d), _row_spec(tr, d), _full_spec(mod.shape)],
        out_specs=[_full_spec((1, 1)), _row_spec(tr, d), _row_spec(tr, d), _full_spec((SUBLANES, d))],
        out_shape=[jax.ShapeDtypeStruct((1, 1), F32), jax.ShapeDtypeStruct((t, d), F32),
                   jax.ShapeDtypeStruct((t, d), BF16), jax.ShapeDtypeStruct((SUBLANES, d), F32)],
        compiler_params=_cparams(("arbitrary",)), name="loss_head",
    )(x1, o_f, target, mod)


def _norm_bwd(xv, dh, sc, gain):
    rstd = _rms(xv)
    yn = xv * rstd
    dsh = jnp.sum(dh, axis=0, keepdims=True)
    dsc = jnp.sum(dh * (yn * gain), axis=0, keepdims=True)
    dgain = jnp.sum(dh * (1.0 + sc) * yn, axis=0, keepdims=True)
    dyn = dh * ((1.0 + sc) * gain)
    dx = rstd * (dyn - yn * jnp.mean(dyn * yn, axis=-1, keepdims=True))
    return dx, dsh, dsc, dgain


def pre_ffn_bwd(x1, dh2, dy, o_m, mod, gain):
    t, d = x1.shape
    tr = _tile(t, ROW_TILE, SUBLANES)

    def body(x1_ref, dh_ref, dy_ref, om_ref, mod_ref, g_ref, dx1_ref, dom_ref, acc_ref):
        i = pl.program_id(0)
        dxn, dsh, dsc, dgain = _norm_bwd(x1_ref[...], dh_ref[...], _mod_row(mod_ref, SC_F), g_ref[...])
        dx1 = dy_ref[...] + dxn
        dx1_ref[...] = dx1
        dom_ref[...] = (dx1 * _mod_row(mod_ref, GT_M)).astype(BF16)
        dgt = jnp.sum(dx1 * om_ref[...], axis=0, keepdims=True)

        @pl.when(i == 0)
        def _():
            acc_ref[...] = jnp.zeros_like(acc_ref)

        acc_ref[pl.ds(0, 1), :] += dsh
        acc_ref[pl.ds(1, 1), :] += dsc
        acc_ref[pl.ds(2, 1), :] += dgain
        acc_ref[pl.ds(3, 1), :] += dgt

    return pl.pallas_call(
        body, grid=(t // tr,),
        in_specs=[_row_spec(tr, d)] * 4 + [_full_spec(mod.shape), _full_spec(gain.shape)],
        out_specs=[_row_spec(tr, d), _row_spec(tr, d), _full_spec((SUBLANES, d))],
        out_shape=[jax.ShapeDtypeStruct((t, d), F32), jax.ShapeDtypeStruct((t, d), BF16),
                   jax.ShapeDtypeStruct((SUBLANES, d), F32)],
        compiler_params=_cparams(("arbitrary",)), name="pre_ffn_bwd",
    )(x1, dh2, dy, o_m, mod, gain)


def pre_mix_bwd(x, dh, dx1, mod, gain):
    t, d = x.shape
    tr = _tile(t, ROW_TILE, SUBLANES)

    def body(x_ref, dh_ref, dx1_ref, mod_ref, g_ref, gx_ref, acc_ref):
        i = pl.program_id(0)
        dxn, dsh, dsc, dgain = _norm_bwd(x_ref[...], dh_ref[...], _mod_row(mod_ref, SC_M), g_ref[...])
        gx_ref[...] = dx1_ref[...] + dxn

        @pl.when(i == 0)
        def _():
            acc_ref[...] = jnp.zeros_like(acc_ref)

        acc_ref[pl.ds(0, 1), :] += dsh
        acc_ref[pl.ds(1, 1), :] += dsc
        acc_ref[pl.ds(2, 1), :] += dgain

    return pl.pallas_call(
        body, grid=(t // tr,),
        in_specs=[_row_spec(tr, d)] * 3 + [_full_spec(mod.shape), _full_spec(gain.shape)],
        out_specs=[_row_spec(tr, d), _full_spec((SUBLANES, d))],
        out_shape=[jax.ShapeDtypeStruct((t, d), F32), jax.ShapeDtypeStruct((SUBLANES, d), F32)],
        compiler_params=_cparams(("arbitrary",)), name="pre_mix_bwd",
    )(x, dh, dx1, mod, gain)


def merge_fwd(p, y_attn, y_conv, off_ga, off_gc):
    t, d = y_attn.shape
    tr = _tile(t, ROW_TILE, SUBLANES)
    cw = math.gcd(math.gcd(off_ga, off_gc), math.gcd(d, 512))
    nc = d // cw

    def body(ga_ref, gc_ref, ya_ref, yc_ref, o_ref):
        o_ref[...] = (_sigmoid(ga_ref[...]) * ya_ref[...] + _sigmoid(gc_ref[...]) * yc_ref[...]).astype(BF16)

    return pl.pallas_call(
        body, grid=(t // tr, nc),
        in_specs=[pl.BlockSpec((tr, cw), lambda i, j: (i, off_ga // cw + j)),
                  pl.BlockSpec((tr, cw), lambda i, j: (i, off_gc // cw + j)),
                  pl.BlockSpec((tr, cw), lambda i, j: (i, j)),
                  pl.BlockSpec((tr, cw), lambda i, j: (i, j))],
        out_specs=pl.BlockSpec((tr, cw), lambda i, j: (i, j)),
        out_shape=jax.ShapeDtypeStruct((t, d), BF16),
        compiler_params=_cparams(("parallel", "parallel")), name="merge_fwd",
    )(p, p, y_attn, y_conv)


def merge_bwd(p, y_attn, y_conv, dmerged, off_ga, off_gc):
    t, d = y_attn.shape
    tr = _tile(t, ROW_TILE, SUBLANES)
    cw = math.gcd(math.gcd(off_ga, off_gc), math.gcd(d, 512))
    nc = d // cw

    def body(ga_ref, gc_ref, ya_ref, yc_ref, dm_ref, dya_ref, dyc_ref, dga_ref, dgc_ref):
        dm = dm_ref[...]
        sa = _sigmoid(ga_ref[...])
        sc = _sigmoid(gc_ref[...])
        dya_ref[...] = (dm * sa).astype(BF16)
        dyc_ref[...] = (dm * sc).astype(BF16)
        dga_ref[...] = (dm * ya_ref[...] * sa * (1.0 - sa)).astype(BF16)
        dgc_ref[...] = (dm * yc_ref[...] * sc * (1.0 - sc)).astype(BF16)

    blk = pl.BlockSpec((tr, cw), lambda i, j: (i, j))
    return pl.pallas_call(
        body, grid=(t // tr, nc),
        in_specs=[pl.BlockSpec((tr, cw), lambda i, j: (i, off_ga // cw + j)),
                  pl.BlockSpec((tr, cw), lambda i, j: (i, off_gc // cw + j)), blk, blk, blk],
        out_specs=[blk] * 4,
        out_shape=[jax.ShapeDtypeStruct((t, d), BF16)] * 4,
        compiler_params=_cparams(("parallel", "parallel")), name="merge_bwd",
    )(p, p, y_attn, y_conv, dmerged)


def swiglu_fwd(f):
    t, two = f.shape
    dff = two // 2
    tr = _tile(t, ROW_TILE, SUBLANES)
    cw = math.gcd(dff, 512)
    nc = dff // cw

    def body(g_ref, u_ref, o_ref):
        g = g_ref[...].astype(F32)
        o_ref[...] = (g * _sigmoid(g) * u_ref[...].astype(F32)).astype(BF16)

    return pl.pallas_call(
        body, grid=(t // tr, nc),
        in_specs=[pl.BlockSpec((tr, cw), lambda i, j: (i, j)), pl.BlockSpec((tr, cw), lambda i, j: (i, nc + j))],
        out_specs=pl.BlockSpec((tr, cw), lambda i, j: (i, j)),
        out_shape=jax.ShapeDtypeStruct((t, dff), BF16),
        compiler_params=_cparams(("parallel", "parallel")), name="swiglu_fwd",
    )(f, f)


def swiglu_bwd(f, dact):
    t, two = f.shape
    dff = two // 2
    tr = _tile(t, ROW_TILE, SUBLANES)
    cw = math.gcd(dff, 512)
    nc = dff // cw

    def body(g_ref, u_ref, da_ref, o_ref):
        j = pl.program_id(1)
        g = g_ref[...].astype(F32)
        u = u_ref[...].astype(F32)
        da = da_ref[...]
        s = _sigmoid(g)
        dgate = da * u * (s * (1.0 + g * (1.0 - s)))
        dup = da * (g * s)
        o_ref[...] = jnp.where(j < nc, dgate, dup).astype(BF16)

    return pl.pallas_call(
        body, grid=(t // tr, 2 * nc),
        in_specs=[pl.BlockSpec((tr, cw), lambda i, j: (i, j % nc)),
                  pl.BlockSpec((tr, cw), lambda i, j: (i, nc + j % nc)),
                  pl.BlockSpec((tr, cw), lambda i, j: (i, j % nc))],
        out_specs=pl.BlockSpec((tr, cw), lambda i, j: (i, j)),
        out_shape=jax.ShapeDtypeStruct((t, two), BF16),
        compiler_params=_cparams(("parallel", "parallel")), name="swiglu_bwd",
    )(f, f, dact)


def _t5_bucket_table():
    q_off = np.arange(BLOCK)
    k_off = np.arange(2 * BLOCK)
    dist = q_off[:, None] + BLOCK - k_off[None, :]
    n = np.maximum(dist, 0)
    nf = np.maximum(n, 1).astype(np.float32)
    large = MAX_EXACT + (np.log(nf / np.float32(MAX_EXACT)) / np.float32(math.log(MAX_DISTANCE / MAX_EXACT))
                         * np.float32(NUM_BUCKETS - MAX_EXACT)).astype(np.int32)
    large = np.minimum(large, NUM_BUCKETS - 1)
    bucket = np.where(n < MAX_EXACT, n, large).astype(np.int32)
    allowed = (dist >= 0) & (dist < WINDOW)
    return np.where(allowed, bucket, -1).astype(np.int32)


def bias_table(rel_bias, bucket_p, bucket_c):
    nb, nq = rel_bias.shape

    def body(rb_ref, bkp_ref, bkc_ref, op_ref, oc_ref):
        for bk_ref, o_ref in ((bkp_ref, op_ref), (bkc_ref, oc_ref)):
            bk = bk_ref[...]
            for h in range(nq):
                acc = jnp.full(bk.shape, -jnp.inf, F32)
                for b in range(nb):
                    acc = jnp.where(bk == b, rb_ref[b, h], acc)
                o_ref[h] = acc

    vm = pl.BlockSpec(memory_space=pltpu.VMEM)
    return pl.pallas_call(
        body,
        in_specs=[pl.BlockSpec(memory_space=pltpu.SMEM), vm, vm],
        out_specs=[vm, vm],
        out_shape=[jax.ShapeDtypeStruct((nq,) + bucket_p.shape, F32)] * 2,
        compiler_params=_cparams(), name="bias_table",
    )(rel_bias, bucket_p, bucket_c)


def bias_table_bwd(dbp, dbc, bucket_p, bucket_c):
    nq = dbp.shape[0]

    def body(dbp_ref, dbc_ref, bkp_ref, bkc_ref, o_ref):
        bkp, bkc = bkp_ref[...][None], bkc_ref[...][None]
        dp, dc = dbp_ref[...], dbc_ref[...]
        for b in range(NUM_BUCKETS):
            sel = jnp.where(bkp == b, dp, 0.0) + jnp.where(bkc == b, dc, 0.0)
            o_ref[b] = jnp.sum(jnp.sum(sel, axis=2, keepdims=True), axis=1, keepdims=True)

    vm = pl.BlockSpec(memory_space=pltpu.VMEM)
    return pl.pallas_call(
        body,
        in_specs=[vm] * 4,
        out_specs=vm,
        out_shape=jax.ShapeDtypeStruct((NUM_BUCKETS, nq, 1, 1), F32),
        compiler_params=_cparams(), name="bias_table_bwd",
    )(dbp, dbc, bucket_p, bucket_c)


_NT = (((1,), (1,)), ((), ()))
_TN = (((0,), (0,)), ((), ()))


@jax.custom_vjp
def _bdot_nt(a, b):
    return lax.dot_general(a.astype(BF16), b.astype(BF16), _NT, preferred_element_type=F32)


def _bdot_nt_fwd(a, b):
    return _bdot_nt(a, b), (a, b)


def _bdot_nt_bwd(res, g):
    a, b = res
    gb = g.astype(BF16)
    da = jnp.dot(gb, b.astype(BF16), preferred_element_type=F32)
    db = lax.dot_general(gb, a.astype(BF16), _TN, preferred_element_type=F32)
    return da, db


_bdot_nt.defvjp(_bdot_nt_fwd, _bdot_nt_bwd)


@jax.custom_vjp
def _bdot_nn(a, b):
    return jnp.dot(a.astype(BF16), b.astype(BF16), preferred_element_type=F32)


def _bdot_nn_fwd(a, b):
    return _bdot_nn(a, b), (a, b)


def _bdot_nn_bwd(res, g):
    a, b = res
    gb = g.astype(BF16)
    da = lax.dot_general(gb, b.astype(BF16), _NT, preferred_element_type=F32)
    db = lax.dot_general(a.astype(BF16), gb, _TN, preferred_element_type=F32)
    return da, db


_bdot_nn.defvjp(_bdot_nn_fwd, _bdot_nn_bwd)


def _attn_math(q4, kp, kc, vp, vc, bp, bc, sink4, qg, kg, *, prev_ok, scale):
    g, b, hd = q4.shape
    q = q4.reshape(g * b, hd)
    qn = q * _rms(q) * qg
    kpn = kp * _rms(kp) * kg
    kcn = kc * _rms(kc) * kg
    lp = _bdot_nt(qn, kpn).reshape(g, b, b) * scale + bp
    lc = _bdot_nt(qn, kcn).reshape(g, b, b) * scale + bc
    lp = jnp.where(prev_ok, lp, -jnp.inf)
    m = jnp.maximum(jnp.maximum(jnp.max(lp, axis=-1, keepdims=True), jnp.max(lc, axis=-1, keepdims=True)), sink4)
    m = lax.stop_gradient(m)
    pp = jnp.exp(lp - m)
    pc = jnp.exp(lc - m)
    den = jnp.sum(pp, axis=-1, keepdims=True) + jnp.sum(pc, axis=-1, keepdims=True) + jnp.exp(sink4 - m)
    inv = 1.0 / den
    out = _bdot_nn((pp * inv).reshape(g * b, b), vp) + _bdot_nn((pc * inv).reshape(g * b, b), vc)
    return out.reshape(g, b, hd)


def _attn_specs(grp, hd, nblk, reverse):
    def blk(n):
        return nblk - 1 - n if reverse else n

    return [
        pl.BlockSpec((grp, BLOCK, hd), lambda h, n: (h, blk(n), 0)),
        pl.BlockSpec((None, BLOCK, hd), lambda h, n: (h, jnp.maximum(blk(n) - 1, 0), 0)),
        pl.BlockSpec((None, BLOCK, hd), lambda h, n: (h, blk(n), 0)),
        pl.BlockSpec((None, BLOCK, hd), lambda h, n: (h, jnp.maximum(blk(n) - 1, 0), 0)),
        pl.BlockSpec((None, BLOCK, hd), lambda h, n: (h, blk(n), 0)),
        pl.BlockSpec((grp, BLOCK, BLOCK), lambda h, n: (h, 0, 0)),
        pl.BlockSpec((grp, BLOCK, BLOCK), lambda h, n: (h, 0, 0)),
        pl.BlockSpec((grp, 1, 1), lambda h, n: (h, 0, 0)),
        pl.BlockSpec((1, hd), lambda h, n: (0, 0)),
        pl.BlockSpec((1, hd), lambda h, n: (0, 0)),
    ]


def attn_fwd(qh, kh, vh, bias_p, bias_c, sinks, qg, kg):
    nq, t, hd = qh.shape
    nkv = kh.shape[0]
    grp, nblk = nq // nkv, t // BLOCK
    scale = hd ** -0.5

    def body(q_ref, kp_ref, kc_ref, vp_ref, vc_ref, bp_ref, bc_ref, s_ref, qg_ref, kg_ref, o_ref):
        prev_ok = pl.program_id(1) > 0
        out = _attn_math(q_ref[...], kp_ref[...], kc_ref[...], vp_ref[...], vc_ref[...], bp_ref[...], bc_ref[...],
                         s_ref[...], qg_ref[...], kg_ref[...], prev_ok=prev_ok, scale=scale)
        o_ref[...] = out.astype(BF16)

    return pl.pallas_call(
        body, grid=(nkv, nblk),
        in_specs=_attn_specs(grp, hd, nblk, False),
        out_specs=pl.BlockSpec((grp, BLOCK, hd), lambda h, n: (h, n, 0)),
        out_shape=jax.ShapeDtypeStruct((nq, t, hd), BF16),
        compiler_params=_cparams(("parallel", "parallel")), name="attn_fwd",
    )(qh, kh, kh, vh, vh, bias_p, bias_c, sinks, qg, kg)


def attn_bwd(qh, kh, vh, bias_p, bias_c, sinks, qg, kg, doh):
    nq, t, hd = qh.shape
    nkv = kh.shape[0]
    grp, nblk = nq // nkv, t // BLOCK
    scale = hd ** -0.5

    def body(q_ref, kp_ref, kc_ref, vp_ref, vc_ref, bp_ref, bc_ref, s_ref, qg_ref, kg_ref, do_ref,
             dq_ref, dk_ref, dv_ref, dbp_ref, dbc_ref, ds_ref, dqg_ref, dkg_ref, ck_ref, cv_ref):
        h, i = pl.program_id(0), pl.program_id(1)
        prev_ok = (nblk - 1 - i) > 0
        fn = functools.partial(_attn_math, prev_ok=prev_ok, scale=scale)
        _, vjp = jax.vjp(fn, q_ref[...], kp_ref[...], kc_ref[...], vp_ref[...], vc_ref[...], bp_ref[...], bc_ref[...],
                         s_ref[...], qg_ref[...], kg_ref[...])
        dq, dkp, dkc, dvp, dvc, dbp, dbc, dsk, dqg, dkg = vjp(do_ref[...])
        dq_ref[...] = dq

        @pl.when(i == 0)
        def _():
            ck_ref[...] = jnp.zeros_like(ck_ref)
            cv_ref[...] = jnp.zeros_like(cv_ref)
            dbp_ref[...] = jnp.zeros_like(dbp_ref)
            dbc_ref[...] = jnp.zeros_like(dbc_ref)
            ds_ref[...] = jnp.zeros_like(ds_ref)

        @pl.when((i == 0) & (h == 0))
        def _():
            dqg_ref[...] = jnp.zeros_like(dqg_ref)
            dkg_ref[...] = jnp.zeros_like(dkg_ref)

        dk_ref[...] = dkc + ck_ref[...]
        dv_ref[...] = dvc + cv_ref[...]
        ck_ref[...] = dkp
        cv_ref[...] = dvp
        dbp_ref[...] += dbp
        dbc_ref[...] += dbc
        ds_ref[...] += dsk
        dqg_ref[...] += dqg
        dkg_ref[...] += dkg

    rev = lambda h, n: (h, nblk - 1 - n, 0)
    return pl.pallas_call(
        body, grid=(nkv, nblk),
        in_specs=_attn_specs(grp, hd, nblk, True) + [pl.BlockSpec((grp, BLOCK, hd), rev)],
        out_specs=[
            pl.BlockSpec((grp, BLOCK, hd), rev),
            pl.BlockSpec((None, BLOCK, hd), rev),
            pl.BlockSpec((None, BLOCK, hd), rev),
            pl.BlockSpec((grp, BLOCK, BLOCK), lambda h, n: (h, 0, 0)),
            pl.BlockSpec((grp, BLOCK, BLOCK), lambda h, n: (h, 0, 0)),
            pl.BlockSpec((grp, 1, 1), lambda h, n: (h, 0, 0)),
            pl.BlockSpec((1, hd), lambda h, n: (0, 0)),
            pl.BlockSpec((1, hd), lambda h, n: (0, 0)),
        ],
        out_shape=[
            jax.ShapeDtypeStruct((nq, t, hd), F32),
            jax.ShapeDtypeStruct((nkv, t, hd), F32),
            jax.ShapeDtypeStruct((nkv, t, hd), F32),
            jax.ShapeDtypeStruct((nq, BLOCK, BLOCK), F32),
            jax.ShapeDtypeStruct((nq, BLOCK, BLOCK), F32),
            jax.ShapeDtypeStruct((nq, 1, 1), F32),
            jax.ShapeDtypeStruct((1, hd), F32),
            jax.ShapeDtypeStruct((1, hd), F32),
        ],
        scratch_shapes=[pltpu.VMEM((BLOCK, hd), F32), pltpu.VMEM((BLOCK, hd), F32)],
        compiler_params=_cparams(("arbitrary", "arbitrary")), name="attn_bwd",
    )(qh, kh, kh, vh, vh, bias_p, bias_c, sinks, qg, kg, doh)


CONV_TILE = 256


def _conv_halo_specs(tb, ch, nblk):
    per = tb // CONV_HALO
    last = nblk * per - 1
    cur = pl.BlockSpec((tb, ch), lambda n: (n, 0))
    prev = pl.BlockSpec((CONV_HALO, ch), lambda n: (jnp.maximum(n * per - 1, 0), 0))
    nxt = pl.BlockSpec((CONV_HALO, ch), lambda n: (jnp.minimum((n + 1) * per, last), 0))
    return cur, prev, nxt


def _ln_silu(co, ln_g, ln_b):
    mu = jnp.mean(co, axis=-1, keepdims=True)
    cen = co - mu
    rstd = lax.rsqrt(jnp.mean(cen * cen, axis=-1, keepdims=True) + EPS)
    xhat = cen * rstd
    z = xhat * ln_g + ln_b
    return xhat, rstd, z


def conv_fwd(ca, cb, conv_w, conv_b, ln_g, ln_b):
    t, ch = ca.shape
    tb = _tile(t, CONV_TILE, CONV_HALO)
    nblk = t // tb
    cur, prev, _ = _conv_halo_specs(tb, ch, nblk)
    lead = CONV_HALO - (CONV_WIDTH - 1)

    def body(ca_ref, cb_ref, cap_ref, cbp_ref, w_ref, b_ref, g_ref, bb_ref, s_ref, ubuf):
        n = pl.program_id(0)
        halo = cap_ref[...] * _sigmoid(cbp_ref[...])
        ubuf[pl.ds(0, CONV_HALO), :] = jnp.where(n > 0, halo, 0.0)
        ubuf[pl.ds(CONV_HALO, tb), :] = ca_ref[...] * _sigmoid(cb_ref[...])
        acc = jnp.broadcast_to(b_ref[...], (tb, ch))
        for k in range(CONV_WIDTH):
            acc = acc + w_ref[pl.ds(k, 1), :] * ubuf[pl.ds(lead + k, tb), :]
        _, _, z = _ln_silu(acc, g_ref[...], bb_ref[...])
        s_ref[...] = (z * _sigmoid(z)).astype(BF16)

    vec = _full_spec((1, ch))
    return pl.pallas_call(
        body, grid=(nblk,),
        in_specs=[cur, cur, prev, prev, _full_spec(conv_w.shape), vec, vec, vec],
        out_specs=cur,
        out_shape=jax.ShapeDtypeStruct((t, ch), BF16),
        scratch_shapes=[pltpu.VMEM((CONV_HALO + tb, ch), F32)],
        compiler_params=_cparams(("parallel",)), name="conv_fwd",
    )(ca, cb, ca, cb, conv_w, conv_b, ln_g, ln_b)


def conv_bwd(ca, cb, ds, conv_w, conv_b, ln_g, ln_b):
    t, ch = ca.shape
    tb = _tile(t, CONV_TILE, CONV_HALO)
    nblk = t // tb
    cur, prev, nxt = _conv_halo_specs(tb, ch, nblk)
    lead = CONV_HALO - (CONV_WIDTH - 1)
    ext = tb + CONV_HALO

    def body(ca_ref, cb_ref, cap_ref, cbp_ref, can_ref, cbn_ref, ds_ref, dsn_ref, w_ref, b_ref, g_ref, bb_ref,
             dca_ref, dcb_ref, dw_ref, dvec_ref, ubuf, dbuf):
        n = pl.program_id(0)
        is_last = n == nblk - 1
        sig_b = _sigmoid(cb_ref[...])
        cav = ca_ref[...]
        ubuf[pl.ds(0, CONV_HALO), :] = jnp.where(n > 0, cap_ref[...] * _sigmoid(cbp_ref[...]), 0.0)
        ubuf[pl.ds(CONV_HALO, tb), :] = cav * sig_b
        ubuf[pl.ds(CONV_HALO + tb, CONV_HALO), :] = can_ref[...] * _sigmoid(cbn_ref[...])
        co = jnp.broadcast_to(b_ref[...], (ext, ch))
        for k in range(CONV_WIDTH):
            co = co + w_ref[pl.ds(k, 1), :] * ubuf[pl.ds(lead + k, ext), :]
        xhat, rstd, z = _ln_silu(co, g_ref[...], bb_ref[...])
        dsv = jnp.concatenate([ds_ref[...], jnp.where(is_last, 0.0, dsn_ref[...])], axis=0)
        sg = _sigmoid(z)
        dz = dsv * (sg * (1.0 + z * (1.0 - sg)))
        dxh = dz * g_ref[...]
        dco = rstd * (dxh - jnp.mean(dxh, axis=-1, keepdims=True)
                      - xhat * jnp.mean(dxh * xhat, axis=-1, keepdims=True))
        dbuf[...] = dco

        @pl.when(n == 0)
        def _():
            dw_ref[...] = jnp.zeros_like(dw_ref)
            dvec_ref[...] = jnp.zeros_like(dvec_ref)

        dco_cur = dco[:tb]
        dvec_ref[pl.ds(0, 1), :] += jnp.sum(dco_cur, axis=0, keepdims=True)
        dvec_ref[pl.ds(1, 1), :] += jnp.sum(dz[:tb] * xhat[:tb], axis=0, keepdims=True)
        dvec_ref[pl.ds(2, 1), :] += jnp.sum(dz[:tb], axis=0, keepdims=True)
        du = jnp.zeros((tb, ch), F32)
        for k in range(CONV_WIDTH):
            du = du + w_ref[pl.ds(k, 1), :] * dbuf[pl.ds(CONV_WIDTH - 1 - k, tb), :]
            dw_ref[pl.ds(k, 1), :] += jnp.sum(dco_cur * ubuf[pl.ds(lead + k, tb), :], axis=0, keepdims=True)
        dca_ref[...] = (du * sig_b).astype(BF16)
        dcb_ref[...] = (du * cav * sig_b * (1.0 - sig_b)).astype(BF16)

    vec = _full_spec((1, ch))
    return pl.pallas_call(
        body, grid=(nblk,),
        in_specs=[cur, cur, prev, prev, nxt, nxt, cur, nxt, _full_spec(conv_w.shape), vec, vec, vec],
        out_specs=[cur, cur, _full_spec(conv_w.shape), _full_spec((SUBLANES, ch))],
        out_shape=[jax.ShapeDtypeStruct((t, ch), BF16), jax.ShapeDtypeStruct((t, ch), BF16),
                   jax.ShapeDtypeStruct(conv_w.shape, F32), jax.ShapeDtypeStruct((SUBLANES, ch), F32)],
        scratch_shapes=[pltpu.VMEM((2 * CONV_HALO + tb, ch), F32), pltpu.VMEM((ext, ch), F32)],
        compiler_params=_cparams(("arbitrary",)), name="conv_bwd",
    )(ca, cb, ca, cb, ca, cb, ds, ds, conv_w, conv_b, ln_g, ln_b)


def ada_fwd(c_t, w_ada):
    d, nc = w_ada.shape
    nex = c_t.shape[1]
    tn = _tile(nc, 512)

    def body(ct_ref, w_ref, o_ref):
        w = w_ref[...]
        ct = ct_ref[...]
        cact = ct * _sigmoid(ct)
        rows = [jnp.sum(w * cact[:, b:b + 1], axis=0, keepdims=True) for b in range(nex)]
        o_ref[...] = jnp.concatenate(rows, axis=0)

    return pl.pallas_call(
        body, grid=(nc // tn,),
        in_specs=[_full_spec(c_t.shape), pl.BlockSpec((d, tn), lambda j: (0, j))],
        out_specs=pl.BlockSpec((nex, tn), lambda j: (0, j)),
        out_shape=jax.ShapeDtypeStruct((nex, nc), F32),
        compiler_params=_cparams(("parallel",)), name="ada_fwd",
    )(c_t, w_ada)


def _adamw_math(w, g, m, v):
    m = ADAM_B1 * m + (1.0 - ADAM_B1) * g
    v = ADAM_B2 * v + (1.0 - ADAM_B2) * (g * g)
    m_hat = m / (1.0 - ADAM_B1 ** ADAM_STEP)
    v_hat = v / (1.0 - ADAM_B2 ** ADAM_STEP)
    delta = -ADAM_LR * (m_hat / (jnp.sqrt(v_hat) + ADAM_EPS) + ADAM_WD * w)
    return delta, m, v


def adamw(w, g, m, v, name):
    r, n = w.shape
    tr = _tile(r, 512, SUBLANES)
    tn = _tile(n, 1024)

    def body(w_ref, g_ref, m_ref, v_ref, d_ref, nm_ref, nv_ref):
        d_ref[...], nm_ref[...], nv_ref[...] = _adamw_math(w_ref[...], g_ref[...], m_ref[...], v_ref[...])

    blk = pl.BlockSpec((tr, tn), lambda i, j: (i, j))
    return pl.pallas_call(
        body, grid=(r // tr, n // tn),
        in_specs=[blk] * 4, out_specs=[blk] * 3,
        out_shape=[jax.ShapeDtypeStruct((r, n), F32)] * 3,
        compiler_params=_cparams(("parallel", "parallel")), name=name,
    )(w, g, m, v)


def ada_grad_adamw(c_t, dmod_cols, w, m, v):
    d, nc = w.shape
    nex = c_t.shape[1]
    tr = _tile(d, 512, SUBLANES)
    tn = _tile(nc, 1024)

    def body(ct_ref, dm_ref, w_ref, m_ref, v_ref, g_ref, d_ref, nm_ref, nv_ref):
        ct = ct_ref[...]
        cact = ct * _sigmoid(ct)
        dm = dm_ref[...]
        g = cact[:, 0:1] * dm[0:1, :]
        for b in range(1, nex):
            g = g + cact[:, b:b + 1] * dm[b:b + 1, :]
        g_ref[...] = g
        d_ref[...], nm_ref[...], nv_ref[...] = _adamw_math(w_ref[...], g, m_ref[...], v_ref[...])

    blk = pl.BlockSpec((tr, tn), lambda i, j: (i, j))
    return pl.pallas_call(
        body, grid=(d // tr, nc // tn),
        in_specs=[pl.BlockSpec((tr, nex), lambda i, j: (i, 0)), pl.BlockSpec((nex, tn), lambda i, j: (0, j)),
                  blk, blk, blk],
        out_specs=[blk] * 4,
        out_shape=[jax.ShapeDtypeStruct((d, nc), F32)] * 4,
        compiler_params=_cparams(("parallel", "parallel")), name="ada_grad_adamw",
    )(c_t, dmod_cols, w, m, v)


def small_sum_adamw(gathered, w, m, v):
    ndev, r, n = gathered.shape

    def body(ga_ref, w_ref, m_ref, v_ref, g_ref, d_ref, nm_ref, nv_ref):
        g = ga_ref[0]
        for s in range(1, ndev):
            g = g + ga_ref[s]
        g_ref[...] = g
        d_ref[...], nm_ref[...], nv_ref[...] = _adamw_math(w_ref[...], g, m_ref[...], v_ref[...])

    vm = pl.BlockSpec(memory_space=pltpu.VMEM)
    return pl.pallas_call(
        body, in_specs=[vm] * 4, out_specs=[vm] * 4,
        out_shape=[jax.ShapeDtypeStruct((r, n), F32)] * 4,
        compiler_params=_cparams(), name="small_sum_adamw",
    )(gathered, w, m, v)


def _position():
    return lax.axis_index("x"), lax.axis_index("y"), lax.axis_index("c")


def _other_chips(x, y):
    return [(1 - x, y), (x, 1 - y), (1 - x, 1 - y)]


def allgather_small(block, name):
    rows, width = block.shape

    def body(x_ref, out_ref, send_sems, recv_sems, local_sem):
        x, y, c = _position()
        me, sibling = (x, y, c), (x, y, 1 - c)
        chips = _other_chips(x, y)

        def slot(px, py, pc):
            return out_ref.at[4 * px + 2 * py + pc]

        def copy(k, block_of, to, src=None):
            return pltpu.make_async_remote_copy(
                src_ref=slot(*block_of) if src is None else src, dst_ref=slot(*block_of),
                send_sem=send_sems.at[k], recv_sem=recv_sems.at[k], device_id=to, device_id_type=MESH)

        mine = pltpu.make_async_copy(x_ref, slot(*me), local_sem)
        mine.start()
        first = [copy(0, me, sibling, src=x_ref)]
        first += [copy(1 + j, me, (*chip, c), src=x_ref) for j, chip in enumerate(chips)]
        for cp in first:
            cp.start()
        passed = [copy(4 + j, (*chip, c), sibling) for j, chip in enumerate(chips)]
        for j, chip in enumerate(chips):
            copy(1 + j, (*chip, c), me).wait_recv()
            passed[j].start()
        copy(0, sibling, me).wait_recv()
        for j, chip in enumerate(chips):
            copy(4 + j, (*chip, 1 - c), me).wait_recv()
        for cp in first + passed:
            cp.wait_send()
        mine.wait()

    return pl.pallas_call(
        body,
        out_shape=jax.ShapeDtypeStruct((N_DEV, rows, width), block.dtype),
        in_specs=[pl.BlockSpec(memory_space=pltpu.VMEM)],
        out_specs=pl.BlockSpec(memory_space=pltpu.VMEM),
        scratch_shapes=[pltpu.SemaphoreType.DMA((7,)), pltpu.SemaphoreType.DMA((7,)), pltpu.SemaphoreType.DMA],
        compiler_params=_cparams(), name=name,
    )(block)


_HBM = pl.BlockSpec(memory_space=pltpu.HBM)
_SEM = pl.BlockSpec(memory_space=pltpu.SEMAPHORE)
_EFFECT = pltpu.SideEffectType.DATAFLOW_SIDE_EFFECTING


def exchange_start(name, bufs, n_copies, plan):
    nb = len(bufs)

    def body(*refs):
        in_refs = refs[:nb]
        send_sems, recv_sems = refs[nb], refs[nb + 1]
        token = refs[-1]
        for cp in plan(in_refs, send_sems, recv_sems):
            cp.start()
        token[...] = jnp.zeros_like(token)

    outs = pl.pallas_call(
        body, name=name,
        out_shape=(pltpu.SemaphoreType.DMA((n_copies,)), pltpu.SemaphoreType.DMA((n_copies,)),
                   *[pltpu.HBM(b.shape, b.dtype) for b in bufs],
                   jax.ShapeDtypeStruct((SUBLANES, LANES), F32)),
        in_specs=[_HBM] * nb,
        out_specs=(_SEM, _SEM, *[_HBM] * nb, pl.BlockSpec(memory_space=pltpu.VMEM)),
        input_output_aliases={i: 2 + i for i in range(nb)},
        compiler_params=pltpu.CompilerParams(has_side_effects=_EFFECT),
    )(*[pltpu.with_memory_space_constraint(b, pltpu.HBM) for b in bufs])
    return outs[0], outs[1], list(outs[2:2 + nb]), outs[-1]


def exchange_wait(name, started, plan, after=None):
    send_sems, recv_sems, bufs, _ = started
    nb = len(bufs)
    if after is not None:
        bufs = [_after(bufs[0], after)] + list(bufs[1:])

    def body(*refs):
        in_refs = refs[:nb]
        for cp in plan(in_refs, refs[nb], refs[nb + 1]):
            cp.wait_send()
            cp.wait_recv()

    outs = pl.pallas_call(
        body, name=name,
        out_shape=tuple(pltpu.HBM(b.shape, b.dtype) for b in bufs),
        in_specs=[_HBM] * nb + [_SEM, _SEM],
        out_specs=tuple([_HBM] * nb),
        input_output_aliases={i: i for i in range(nb)},
        compiler_params=pltpu.CompilerParams(has_side_effects=_EFFECT),
    )(*bufs, send_sems, recv_sems)
    return list(outs)


def _remote(src, dst, send_sems, recv_sems, i, to):
    return pltpu.make_async_remote_copy(src_ref=src, dst_ref=dst, send_sem=send_sems.at[i], recv_sem=recv_sems.at[i],
                                        device_id=to, device_id_type=MESH)


def _half_rows(buf_rows, chip_idx, pc):
    half = buf_rows // (2 * N_CHIPS)
    return pl.ds((2 * chip_idx + pc) * half, half)


def plan_gather_ici(refs, send_sems, recv_sems):
    x, y, c = _position()
    copies = []
    for k, ref in enumerate(refs):
        rows = ref.at[_half_rows(ref.shape[0], 2 * x + y, c), :]
        for j, chip in enumerate(_other_chips(x, y)):
            copies.append(_remote(rows, rows, send_sems, recv_sems, 3 * k + j, (*chip, c)))
    return copies


def plan_gather_d2d(refs, send_sems, recv_sems):
    x, y, c = _position()
    copies = []
    for k, ref in enumerate(refs):
        for j, (px, py) in enumerate(_other_chips(x, y)):
            rows = ref.at[_half_rows(ref.shape[0], 2 * px + py, c), :]
            copies.append(_remote(rows, rows, send_sems, recv_sems, 3 * k + j, (x, y, 1 - c)))
    return copies


def plan_pair_exchange(refs, send_sems, recv_sems):
    x, y, c = _position()
    nw = len(refs) // 2
    copies = []
    for k in range(nw):
        for chip in range(N_CHIPS):
            copies.append(_remote(refs[k].at[chip, 1 - c], refs[nw + k].at[chip], send_sems, recv_sems,
                                  N_CHIPS * k + chip, (x, y, 1 - c)))
    return copies


def plan_chip_exchange(refs, send_sems, recv_sems):
    x, y, c = _position()
    nw = len(refs) // 2
    copies = []
    for k in range(nw):
        for j, (px, py) in enumerate(_other_chips(x, y)):
            copies.append(_remote(refs[k].at[2 * px + py], refs[nw + k].at[2 * x + y], send_sems, recv_sems,
                                  3 * k + j, (px, py, c)))
    return copies


def plan_pair_share(refs, send_sems, recv_sems):
    x, y, c = _position()
    return [_remote(ref.at[c], ref.at[c], send_sems, recv_sems, k, (x, y, 1 - c)) for k, ref in enumerate(refs)]


def cast_into_slot(src, slot, n_slots, name):
    r, n = src.shape
    tr = _tile(r, 512, BF16_SUBLANES)
    tn = _tile(n, 1024)

    def body(slot_ref, s_ref, o_ref):
        o_ref[...] = s_ref[...].astype(BF16)

    return pl.pallas_call(
        body,
        grid_spec=pltpu.PrefetchScalarGridSpec(
            num_scalar_prefetch=1, grid=(r // tr, n // tn),
            in_specs=[pl.BlockSpec((tr, tn), lambda i, j, sl: (i, j))],
            out_specs=pl.BlockSpec((None, tr, tn), lambda i, j, sl: (sl[0], i, j))),
        out_shape=jax.ShapeDtypeStruct((n_slots, r, n), BF16),
        compiler_params=_cparams(("parallel", "parallel")), name=name,
    )(slot, src)


def pair_sum(g, r, core, name):
    nchip, _, h, n = g.shape
    th = _tile(h, 512, BF16_SUBLANES)
    tn = _tile(n, 1024)

    def body(core_ref, g_ref, r_ref, o_ref):
        o_ref[...] = (g_ref[...].astype(F32) + r_ref[...].astype(F32)).astype(BF16)

    return pl.pallas_call(
        body,
        grid_spec=pltpu.PrefetchScalarGridSpec(
            num_scalar_prefetch=1, grid=(nchip, h // th, n // tn),
            in_specs=[pl.BlockSpec((None, None, th, tn), lambda a, i, j, cr: (a, cr[0], i, j)),
                      pl.BlockSpec((None, th, tn), lambda a, i, j, cr: (a, i, j))],
            out_specs=pl.BlockSpec((None, th, tn), lambda a, i, j, cr: (a, i, j))),
        out_shape=jax.ShapeDtypeStruct((nchip, h, n), BF16),
        compiler_params=_cparams(("parallel", "parallel", "parallel")), name=name,
    )(core, g, r)


def chip_sum(own, got, where, name):
    nchip, h, n = got.shape
    th = _tile(h, 512, BF16_SUBLANES)
    tn = _tile(n, 1024)

    def body(where_ref, own_ref, *rest):
        got_refs, o_ref = rest[:nchip], rest[nchip]
        chip = where_ref[0]
        acc = None
        for s in range(nchip):
            term = jnp.where(chip == s, own_ref[...], got_refs[s][...]).astype(F32)
            acc = term if acc is None else acc + term
        o_ref[...] = acc

    def got_spec(s):
        return pl.BlockSpec((None, th, tn), lambda i, j, wr: (jnp.where(wr[0] == s, (s + 1) % nchip, s), i, j))

    return pl.pallas_call(
        body,
        grid_spec=pltpu.PrefetchScalarGridSpec(
            num_scalar_prefetch=1, grid=(h // th, n // tn),
            in_specs=[pl.BlockSpec((None, th, tn), lambda i, j, wr: (wr[0], i, j))]
            + [got_spec(s) for s in range(nchip)],
            out_specs=pl.BlockSpec((None, th, tn), lambda i, j, wr: (wr[1], i, j))),
        out_shape=jax.ShapeDtypeStruct((2, h, n), F32),
        compiler_params=_cparams(("parallel", "parallel")), name=name,
    )(where, own, *[got] * nchip)


PACK_ROWS = SUBLANES


def _part_rows(shape):
    size = int(np.prod(shape))
    return -(-size // (PACK_ROWS * LANES)) * PACK_ROWS


def _pack_rows(parts):
    rows = []
    for p in parts:
        flat = p.reshape(-1)
        flat = jnp.pad(flat, (0, _part_rows(p.shape) * LANES - flat.shape[0]))
        rows.append(flat.reshape(-1, LANES))
    return jnp.concatenate(rows, axis=0)


def _unpack_rows(packed, shapes):
    out, row = [], 0
    for s in shapes:
        size, nrows = int(np.prod(s)), _part_rows(s)
        out.append(packed[row:row + nrows].reshape(-1)[:size].reshape(s))
        row += nrows
    return out


def kernel(x, c, w_ada, b_ada, norm_mix_g, w_in, q_norm_g, k_norm_g, attn_sinks, rel_bias, w_attn_out, conv_w, conv_b, conv_ln_g, conv_ln_b, w_conv_out, w_mix_out, norm_ffn_g, w_ffn_in, w_ffn_out, loss_target, m_w_ada, m_b_ada, m_norm_mix_g, m_w_in, m_q_norm_g, m_k_norm_g, m_attn_sinks, m_rel_bias, m_w_attn_out, m_conv_w, m_conv_b, m_conv_ln_g, m_conv_ln_b, m_w_conv_out, m_w_mix_out, m_norm_ffn_g, m_w_ffn_in, m_w_ffn_out, v_w_ada, v_b_ada, v_norm_mix_g, v_w_in, v_q_norm_g, v_k_norm_g, v_attn_sinks, v_rel_bias, v_w_attn_out, v_conv_w, v_conv_b, v_conv_ln_g, v_conv_ln_b, v_w_conv_out, v_w_mix_out, v_norm_ffn_g, v_w_ffn_in, v_w_ffn_out):
    xi, yi, ci = _position()
    chip = 2 * xi + yi
    me = 2 * chip + ci
    chip_arr = chip.astype(jnp.int32).reshape(1)
    core_arr = ci.astype(jnp.int32).reshape(1)
    where_arr = jnp.stack([chip, ci]).astype(jnp.int32)

    xe, tgt = x[0], loss_target[0]
    t, d = xe.shape
    hd = q_norm_g.shape[-1]
    nq = attn_sinks.shape[-1]
    aw = nq * hd
    ch = conv_b.shape[-1]
    in_width = N_CHIPS * w_in.shape[-1]
    kvw = (in_width - aw - 2 * ch - 2 * d) // 2
    nkv = kvw // hd
    dff = N_CHIPS * w_ffn_out.shape[1]
    off_k, off_v, off_ca = aw, aw + kvw, aw + 2 * kvw
    off_cb, off_ga, off_gc = off_ca + ch, off_ca + 2 * ch, off_ca + 2 * ch + d
    nc_ada = w_ada.shape[-1]
    ch_loc = conv_w.shape[-1]

    big = {"w_in": w_in[0], "w_attn_out": w_attn_out[0], "w_conv_out": w_conv_out[0], "w_mix_out": w_mix_out[0],
           "w_ffn_in": w_ffn_in[0], "w_ffn_out": w_ffn_out[0]}
    gather_groups = [("in", ["w_in"]), ("mid", ["w_attn_out", "w_conv_out", "w_mix_out"]),
                     ("ffn_in", ["w_ffn_in"]), ("ffn_out", ["w_ffn_out"])]
    ici = {}
    token = None
    for gname, names in gather_groups:
        bufs = []
        for n in names:
            r, ncol = big[n].shape
            bufs.append(cast_into_slot(big[n], chip_arr, N_CHIPS, "cast_" + n).reshape(N_CHIPS * r, ncol))
        ici[gname] = exchange_start("gather_ici_start_" + gname, bufs, 3 * len(names), plan_gather_ici)
        token = ici[gname][3]

    def gather_pass_on(gname, after):
        landed = exchange_wait("gather_ici_wait_" + gname, ici[gname], plan_gather_ici, after=after)
        return exchange_start("gather_d2d_start_" + gname, landed, 3 * len(landed), plan_gather_d2d)

    def gathered(gname, d2d, after):
        names = dict(gather_groups)[gname]
        outs = exchange_wait("gather_d2d_wait_" + gname, d2d, plan_gather_d2d, after=after)
        return [o.reshape(N_CHIPS, big[n].shape[0], big[n].shape[1]) for o, n in zip(outs, names)]

    pack1 = _pack_rows([_after(c[0], token), conv_w[0]])
    got1 = allgather_small(pack1, "allgather_cond")
    c_rows = _part_rows((d,))
    c_all = got1[:, :c_rows].reshape(N_DEV, -1)[:, :d]
    w_rows = _part_rows((CONV_WIDTH, ch_loc))
    conv_w_full = got1[0::2, c_rows:c_rows + w_rows].reshape(N_CHIPS, -1)[:, :CONV_WIDTH * ch_loc]
    conv_w_full = jnp.transpose(conv_w_full.reshape(N_CHIPS, CONV_WIDTH, ch_loc), (1, 0, 2)).reshape(CONV_WIDTH, ch)
    conv_w_pad = jnp.pad(conv_w_full, ((0, 1), (0, 0)))
    c_t = jnp.transpose(c_all)
    mod_cols = ada_fwd(c_t, w_ada[0])
    got2 = allgather_small(mod_cols.reshape(-1, LANES), "allgather_mod")
    mod_all = got2.reshape(N_CHIPS, 2, N_DEV, nc_ada)[:, 0]
    mod = lax.dynamic_slice_in_dim(mod_all, me, 1, axis=1).reshape(1, N_CHIPS * nc_ada) + b_ada
    mod = jnp.pad(mod.reshape(N_MOD, d), ((0, SUBLANES - N_MOD), (0, 0)))

    h = pre_mix_fwd(xe, mod, norm_mix_g)
    bucket = _t5_bucket_table()
    bucket_p, bucket_c = jnp.asarray(bucket[:, :BLOCK]), jnp.asarray(bucket[:, BLOCK:])
    bias_p, bias_c = bias_table(rel_bias, bucket_p, bucket_c)
    d2d_in = gather_pass_on("in", h)
    (wg_in,) = gathered("in", d2d_in, bias_p)
    p = mm_nn(h, wg_in, tn=_tile(wg_in.shape[2], 640), tk=d, out_dtype=F32, name="mm_in")
    d2d_mid = gather_pass_on("mid", p)

    def heads(a, n):
        return jnp.transpose(a.reshape(t, n, hd), (1, 0, 2))

    def unheads(a):
        return jnp.transpose(a, (1, 0, 2)).reshape(t, -1)

    qh, kh, vh = heads(p[:, :aw], nq), heads(p[:, off_k:off_v], nkv), heads(p[:, off_v:off_ca], nkv)
    qh = _after(qh, d2d_mid[3])
    sinks3 = attn_sinks.reshape(nq, 1, 1)
    attn_o = unheads(attn_fwd(qh, kh, vh, bias_p, bias_c, sinks3, q_norm_g, k_norm_g))
    ca, cb = p[:, off_ca:off_cb], p[:, off_cb:off_ga]
    s_conv = conv_fwd(ca, cb, conv_w_pad, conv_b, conv_ln_g, conv_ln_b)
    wg_attn_out, wg_conv_out, wg_mix_out = gathered("mid", d2d_mid, s_conv)
    wg_mix_out = wg_mix_out.reshape(1, d, d)
    y_attn = mm_nn(attn_o, wg_attn_out, tn=_tile(wg_attn_out.shape[2], 512), tk=aw, out_dtype=F32, name="mm_attn_out")
    y_conv = mm_nn(s_conv, wg_conv_out, tn=_tile(wg_conv_out.shape[2], 512), tk=ch, out_dtype=F32, name="mm_conv_out")
    merged = merge_fwd(p, y_attn, y_conv, off_ga, off_gc)
    d2d_ffn_in = gather_pass_on("ffn_in", merged)
    o_m = mm_nn(_after(merged, d2d_ffn_in[3]), wg_mix_out, tn=_tile(d, 512), tk=d, out_dtype=F32, name="mm_mix_out")
    x1, h2 = pre_ffn_fwd(xe, o_m, mod, norm_ffn_g)
    (wg_ffn_in,) = gathered("ffn_in", d2d_ffn_in, h2)
    f = mm_nn(h2, wg_ffn_in, tn=_tile(wg_ffn_in.shape[2], 1408), tk=d, out_dtype=BF16, name="mm_ffn_in")
    d2d_ffn_out = gather_pass_on("ffn_out", f)
    act = swiglu_fwd(_after(f, d2d_ffn_out[3]))
    (wg_ffn_out,) = gathered("ffn_out", d2d_ffn_out, act)
    wg_ffn_out = wg_ffn_out.reshape(1, dff, d)
    o_f = mm_nn(act, wg_ffn_out, tn=_tile(d, 1024), tk=_tile(dff, 512), out_dtype=F32, name="mm_ffn_out")
    loss11, dy, dof, acc_l = loss_head(x1, o_f, tgt, mod)

    def blocks_of(names, partials):
        out = []
        for n, g in zip(names, partials):
            r, ncol = big[n].shape
            out.append(g.reshape(N_CHIPS, 2, r // 2, ncol))
        return out

    def rs_pair_start(gname, blocks):
        land = [lax.empty((N_CHIPS,) + b.shape[2:], BF16) for b in blocks]
        return exchange_start("pair_exchange_start_" + gname, blocks + land, N_CHIPS * len(blocks), plan_pair_exchange)

    def rs_chip_start(gname, names, pair, after):
        nw = len(names)
        outs = exchange_wait("pair_exchange_wait_" + gname, pair, plan_pair_exchange, after=after)
        sums = [pair_sum(g, r, core_arr, "pair_sum_" + n) for n, g, r in zip(names, outs[:nw], outs[nw:])]
        land = [lax.empty(s.shape, BF16) for s in sums]
        return exchange_start("chip_exchange_start_" + gname, sums + land, 3 * nw, plan_chip_exchange)

    def rs_share_start(gname, names, chipx, after):
        nw = len(names)
        outs = exchange_wait("chip_exchange_wait_" + gname, chipx, plan_chip_exchange, after=after)
        halves = [chip_sum(s, r, where_arr, "chip_sum_" + n) for n, s, r in zip(names, outs[:nw], outs[nw:])]
        return exchange_start("pair_share_start_" + gname, halves, nw, plan_pair_share)

    grads, deltas, new_m, new_v = {}, {}, {}, {}
    moments = {"w_in": (m_w_in, v_w_in), "w_attn_out": (m_w_attn_out, v_w_attn_out),
               "w_conv_out": (m_w_conv_out, v_w_conv_out), "w_mix_out": (m_w_mix_out, v_w_mix_out),
               "w_ffn_in": (m_w_ffn_in, v_w_ffn_in), "w_ffn_out": (m_w_ffn_out, v_w_ffn_out)}

    def rs_finish(gname, names, share, after):
        fulls = exchange_wait("pair_share_wait_" + gname, share, plan_pair_share, after=after)
        last = None
        for n, g2 in zip(names, fulls):
            g = g2.reshape(big[n].shape)
            dl, nm, nv = adamw(big[n], g, moments[n][0][0], moments[n][1][0], "adamw_" + n)
            grads[n], deltas[n], new_m[n], new_v[n] = g[None], dl[None], nm[None], nv[None]
            last = nv
        return last

    gw_ffn_out = mm_tn(act, dof, 1, tk=_tile(dff, 512), tn=_tile(d, 1024), name="mm_ffn_out_dw")
    px_ffn_out = rs_pair_start("ffn_out", blocks_of(["w_ffn_out"], [gw_ffn_out]))
    dact = mm_nt(_after(dof, px_ffn_out[3]), wg_ffn_out, tko=_tile(dff, 512), tn=d, name="mm_ffn_out_dx")
    cx_ffn_out = rs_chip_start("ffn_out", ["w_ffn_out"], px_ffn_out, dact)
    df = swiglu_bwd(f, _after(dact, cx_ffn_out[3]))
    gw_ffn_in = mm_tn(h2, df, N_CHIPS, tk=_tile(d, 512), tn=_tile(wg_ffn_in.shape[2], 1408), name="mm_ffn_in_dw")
    px_ffn_in = rs_pair_start("ffn_in", blocks_of(["w_ffn_in"], [gw_ffn_in]))
    dh2 = mm_nt(_after(df, px_ffn_in[3]), wg_ffn_in, tko=_tile(d, 1024), tn=_tile(wg_ffn_in.shape[2], 1408),
                name="mm_ffn_in_dx")
    sh_ffn_out = rs_share_start("ffn_out", ["w_ffn_out"], cx_ffn_out, dh2)
    cx_ffn_in = rs_chip_start("ffn_in", ["w_ffn_in"], px_ffn_in, sh_ffn_out[3])
    dx1, dom, acc_f = pre_ffn_bwd(x1, _after(dh2, cx_ffn_in[3]), dy, o_m, mod, norm_ffn_g)
    gw_mix_out = mm_tn(merged, dom, 1, tk=_tile(d, 512), tn=_tile(d, 1024), name="mm_mix_out_dw")
    px_mix = rs_pair_start("mix_out", blocks_of(["w_mix_out"], [gw_mix_out]))
    dmerged = mm_nt(_after(dom, px_mix[3]), wg_mix_out, tko=_tile(d, 512), tn=d, name="mm_mix_out_dx")
    dy_attn, dy_conv, dga, dgc = merge_bwd(p, y_attn, y_conv, dmerged, off_ga, off_gc)
    done_ffn_out = rs_finish("ffn_out", ["w_ffn_out"], sh_ffn_out, dy_attn)
    cx_mix = rs_chip_start("mix_out", ["w_mix_out"], px_mix, done_ffn_out)
    gw_attn_out = mm_tn(attn_o, _after(dy_attn, cx_mix[3]), N_CHIPS, tk=_tile(aw, 512),
                        tn=_tile(wg_attn_out.shape[2], 512), name="mm_attn_out_dw")
    gw_conv_out = mm_tn(s_conv, dy_conv, N_CHIPS, tk=_tile(ch, 512), tn=_tile(wg_conv_out.shape[2], 512),
                        name="mm_conv_out_dw")
    px_ac = rs_pair_start("attn_conv_out", blocks_of(["w_attn_out", "w_conv_out"], [gw_attn_out, gw_conv_out]))
    dattn_o = mm_nt(_after(dy_attn, px_ac[3]), wg_attn_out, tko=_tile(aw, 1024), tn=_tile(wg_attn_out.shape[2], 512),
                    name="mm_attn_out_dx")
    ds_conv = mm_nt(dy_conv, wg_conv_out, tko=_tile(ch, 1024), tn=_tile(wg_conv_out.shape[2], 512),
                    name="mm_conv_out_dx")
    cx_ac = rs_chip_start("attn_conv_out", ["w_attn_out", "w_conv_out"], px_ac, ds_conv)
    dca, dcb, dconv_w, dconv_vec = conv_bwd(ca, cb, _after(ds_conv, cx_ac[3]), conv_w_pad, conv_b, conv_ln_g,
                                            conv_ln_b)
    sh_ffn_in = rs_share_start("ffn_in", ["w_ffn_in"], cx_ffn_in, dca)
    sh_mix = rs_share_start("mix_out", ["w_mix_out"], cx_mix, sh_ffn_in[3])
    dqh, dkh, dvh, dbp, dbc, dsinks, dqg, dkg = attn_bwd(qh, kh, vh, bias_p, bias_c, sinks3, q_norm_g, k_norm_g,
                                                          heads(_after(dattn_o, sh_mix[3]), nq))
    drel = bias_table_bwd(dbp, dbc, bucket_p, bucket_c).reshape(NUM_BUCKETS, nq)
    dp = jnp.concatenate([unheads(dqh).astype(BF16), unheads(dkh).astype(BF16), unheads(dvh).astype(BF16),
                          dca, dcb, dga, dgc], axis=1)
    gw_in = mm_tn(h, dp, N_CHIPS, tk=_tile(d, 512), tn=_tile(wg_in.shape[2], 640), name="mm_in_dw")
    px_in = rs_pair_start("in", blocks_of(["w_in"], [gw_in]))
    sh_ac = rs_share_start("attn_conv_out", ["w_attn_out", "w_conv_out"], cx_ac, px_in[3])
    dh = mm_nt(_after(dp, sh_ac[3]), wg_in, tko=_tile(d, 1024), tn=_tile(wg_in.shape[2], 640), name="mm_in_dx")
    cx_in = rs_chip_start("in", ["w_in"], px_in, dh)
    grad_x, acc_m = pre_mix_bwd(xe, _after(dh, cx_in[3]), dx1, mod, norm_mix_g)

    dmod = jnp.concatenate([acc_m[0:1], acc_m[1:2], acc_f[3:4], acc_f[0:1], acc_f[1:2], acc_l[0:1]], axis=1)
    small_names = ["b_ada", "norm_mix_g", "q_norm_g", "k_norm_g", "attn_sinks", "rel_bias", "conv_b", "conv_ln_g",
                   "conv_ln_b", "norm_ffn_g"]
    small_w = [b_ada, norm_mix_g, q_norm_g, k_norm_g, attn_sinks, rel_bias, conv_b, conv_ln_g, conv_ln_b, norm_ffn_g]
    small_m = [m_b_ada, m_norm_mix_g, m_q_norm_g, m_k_norm_g, m_attn_sinks, m_rel_bias, m_conv_b, m_conv_ln_g,
               m_conv_ln_b, m_norm_ffn_g]
    small_v = [v_b_ada, v_norm_mix_g, v_q_norm_g, v_k_norm_g, v_attn_sinks, v_rel_bias, v_conv_b, v_conv_ln_g,
               v_conv_ln_b, v_norm_ffn_g]
    small_g = [dmod, acc_m[2:3], dqg, dkg, dsinks.reshape(1, nq), drel, dconv_vec[0:1], dconv_vec[1:2],
               dconv_vec[2:3], acc_f[2:3]]
    small_shapes = [w.shape for w in small_w]
    conv_shape = (CONV_WIDTH, ch)
    got3 = allgather_small(_pack_rows(small_g + [dconv_w[:CONV_WIDTH]]), "allgather_small_grads")
    zero_conv = jnp.zeros(conv_shape, F32)
    g_small, d_small, nm_small, nv_small = small_sum_adamw(
        got3, _pack_rows(small_w + [zero_conv]), _pack_rows(small_m + [zero_conv]), _pack_rows(small_v + [zero_conv]))
    g_parts = _unpack_rows(g_small, small_shapes + [conv_shape])
    grads.update(zip(small_names, g_parts[:-1]))
    deltas.update(zip(small_names, _unpack_rows(d_small, small_shapes)))
    new_m.update(zip(small_names, _unpack_rows(nm_small, small_shapes)))
    new_v.update(zip(small_names, _unpack_rows(nv_small, small_shapes)))

    g_conv_w = lax.dynamic_slice_in_dim(g_parts[-1], chip * ch_loc, ch_loc, axis=1)
    grads["conv_w"] = g_conv_w[None]
    dl, nm, nv = adamw(conv_w[0], g_conv_w, m_conv_w[0], v_conv_w[0], "adamw_conv_w")
    deltas["conv_w"], new_m["conv_w"], new_v["conv_w"] = dl[None], nm[None], nv[None]

    dmod_all = got3[:, :_part_rows((N_MOD * d,))].reshape(N_DEV, -1)[:, :N_MOD * d]
    dmod_cols = lax.dynamic_slice_in_dim(dmod_all, chip * nc_ada, nc_ada, axis=1)
    g_ada, dl, nm, nv = ada_grad_adamw(c_t, dmod_cols, w_ada[0], m_w_ada[0], v_w_ada[0])
    grads["w_ada"], deltas["w_ada"], new_m["w_ada"], new_v["w_ada"] = g_ada[None], dl[None], nm[None], nv[None]

    done = rs_finish("ffn_in", ["w_ffn_in"], sh_ffn_in, nv)
    done = rs_finish("mix_out", ["w_mix_out"], sh_mix, done)
    done = rs_finish("attn_conv_out", ["w_attn_out", "w_conv_out"], sh_ac, done)
    sh_in = rs_share_start("in", ["w_in"], cx_in, done)
    rs_finish("in", ["w_in"], sh_in, None)

    loss = lax.psum(loss11[0, 0], ("x", "y", "c"))
    order = ["w_ada", "b_ada", "norm_mix_g", "w_in", "q_norm_g", "k_norm_g", "attn_sinks", "rel_bias", "w_attn_out",
             "conv_w", "conv_b", "conv_ln_g", "conv_ln_b", "w_conv_out", "w_mix_out", "norm_ffn_g", "w_ffn_in",
             "w_ffn_out"]
    return (loss, grad_x[None], *[grads[n] for n in order], *[deltas[n] for n in order],
            *[new_m[n] for n in order], *[new_v[n] for n in order])
```

```python
import functools
import math
from typing import Any, NamedTuple

import jax
import jax.numpy as jnp
import numpy as np
from jax import lax
from jax.experimental import pallas as pl
from jax.experimental.pallas import tpu as pltpu

F32 = jnp.float32
BF16 = jnp.bfloat16
MESH = pl.DeviceIdType.MESH

V7X_VMEM_BYTES = 64 * 1024 * 1024
VMEM_LIMIT = V7X_VMEM_BYTES - 8 * 1024 * 1024
LANES = 128
SUBLANES = 8
BF16_SUBLANES = 16

EPS = 1e-6
WINDOW = 128
BLOCK = 128
NUM_BUCKETS = 32
MAX_EXACT = NUM_BUCKETS // 2
MAX_DISTANCE = 128
CONV_WIDTH = 31
CONV_HALO = 32
ADAM_LR = 0.001
ADAM_B1 = 0.9
ADAM_B2 = 0.999
ADAM_EPS = 1e-08
ADAM_WD = 0.01
ADAM_STEP = 10
N_MOD = 6
SH_M, SC_M, GT_M, SH_F, SC_F, GT_F = range(6)

N_CHIPS = 4
N_DEV = 8

_ANY = pl.BlockSpec(memory_space=pl.ANY)
_VMEM = pl.BlockSpec(memory_space=pltpu.VMEM)
_SMEM = pl.BlockSpec(memory_space=pltpu.SMEM)
_HBM = pl.BlockSpec(memory_space=pltpu.HBM)
_SEM = pl.BlockSpec(memory_space=pltpu.SEMAPHORE)
_EFFECT = pltpu.SideEffectType.DATAFLOW_SIDE_EFFECTING


def _pallas(body, args, *, in_specs, out_specs, out_shape, name, dep=None, grid=(), n_prefetch=0, scratch=(),
            sem=None, **kw):
    n_lead = n_prefetch + len(in_specs)
    in_specs = list(in_specs)
    args = list(args)
    if dep is not None:
        inner = body

        def body(*refs):
            return inner(*refs[:n_lead], *refs[n_lead + 1:])

        in_specs.append(_ANY)
        args.append(dep)
    params = kw.pop("compiler_params", None)
    if params is None:
        params = pltpu.CompilerParams(dimension_semantics=sem, vmem_limit_bytes=VMEM_LIMIT)
    return pl.pallas_call(
        body,
        grid_spec=pltpu.PrefetchScalarGridSpec(num_scalar_prefetch=n_prefetch, grid=grid, in_specs=in_specs,
                                               out_specs=out_specs, scratch_shapes=list(scratch)),
        out_shape=out_shape, compiler_params=params, name=name, **kw,
    )(*args)


def _tile(n, pref, unit=LANES):
    best = None
    for t in range(unit, min(n, pref) + 1, unit):
        if n % t == 0:
            best = t
    return best if best is not None else n


def _sigmoid(v):
    return 1.0 / (1.0 + jnp.exp(-v))


ROW_CHUNK = 512


def _row_chunks(m):
    step = ROW_CHUNK if m % ROW_CHUNK == 0 else m
    return [(s, step) for s in range(0, m, step)]


def _block_pos(j, perm):
    if perm is None:
        return j
    pos = 0
    for a, p in enumerate(perm):
        pos = pos + jnp.where(j == a, p, 0)
    return pos


def mm_nn(a, w, *, tn, tk, out_dtype, name, perm=None, dep=None):
    m, k = a.shape
    j, k2, nj = w.shape
    assert k == k2 and nj % tn == 0 and k % tk == 0
    npj, nk = nj // tn, k // tk

    def body(a_ref, w_ref, o_ref, *scratch):
        kk = pl.program_id(1)
        for s, sz in _row_chunks(m):
            rows = pl.ds(s, sz)
            p = jnp.dot(a_ref[rows, :], w_ref[...], preferred_element_type=F32)
            if nk == 1:
                o_ref[rows, :] = p.astype(out_dtype)
            else:
                acc = scratch[0]

                @pl.when(kk == 0)
                def _():
                    acc[rows, :] = p

                @pl.when(kk > 0)
                def _():
                    acc[rows, :] += p

                @pl.when(kk == nk - 1)
                def _():
                    o_ref[rows, :] = acc[rows, :].astype(out_dtype)

    return _pallas(
        body, [a, w], dep=dep, grid=(j * npj, nk),
        in_specs=[
            pl.BlockSpec((m, tk), lambda n, kk: (0, kk)),
            pl.BlockSpec((None, tk, tn), lambda n, kk: (n // npj, kk, n % npj)),
        ],
        out_specs=pl.BlockSpec((m, tn), lambda n, kk: (0, _block_pos(n // npj, perm) * npj + n % npj)),
        out_shape=jax.ShapeDtypeStruct((m, j * nj), out_dtype),
        scratch=[pltpu.VMEM((m, tn), F32)] if nk > 1 else [],
        sem=("parallel", "arbitrary"), name=name)


def mm_nt(g, w, *, tko, tn, name, perm=None, dep=None):
    m, n = g.shape
    j, k, nj = w.shape
    assert n == j * nj and nj % tn == 0 and k % tko == 0
    npj, nr = nj // tn, n // tn

    def body(g_ref, w_ref, o_ref):
        r = pl.program_id(1)
        for s, sz in _row_chunks(m):
            rows = pl.ds(s, sz)
            p = lax.dot_general(g_ref[rows, :], w_ref[...], (((1,), (1,)), ((), ())), preferred_element_type=F32)

            @pl.when(r == 0)
            def _():
                o_ref[rows, :] = p

            @pl.when(r > 0)
            def _():
                o_ref[rows, :] += p

    return _pallas(
        body, [g, w], dep=dep, grid=(k // tko, nr),
        in_specs=[
            pl.BlockSpec((m, tn), lambda ko, r: (0, _block_pos(r // npj, perm) * npj + r % npj)),
            pl.BlockSpec((None, tko, tn), lambda ko, r: (r // npj, ko, r % npj)),
        ],
        out_specs=pl.BlockSpec((m, tko), lambda ko, r: (0, ko)),
        out_shape=jax.ShapeDtypeStruct((m, k), F32),
        sem=("parallel", "arbitrary"), name=name)


def mm_tn(a, g, n_blocks, *, tk, tn, name, perm=None, dep=None):
    m, k = a.shape
    m2, n = g.shape
    nj = n // n_blocks
    assert m == m2 and nj % tn == 0 and k % tk == 0
    npj = nj // tn

    def body(a_ref, g_ref, o_ref):
        p = lax.dot_general(a_ref[...], g_ref[...], (((0,), (0,)), ((), ())), preferred_element_type=F32)
        o_ref[...] = p.astype(BF16)

    return _pallas(
        body, [a, g], dep=dep, grid=(k // tk, n // tn),
        in_specs=[
            pl.BlockSpec((m, tk), lambda kk, nn: (0, kk)),
            pl.BlockSpec((m, tn), lambda kk, nn: (0, _block_pos(nn // npj, perm) * npj + nn % npj)),
        ],
        out_specs=pl.BlockSpec((None, tk, tn), lambda kk, nn: (nn // npj, kk, nn % npj)),
        out_shape=jax.ShapeDtypeStruct((n_blocks, k, nj), BF16),
        sem=("parallel", "parallel"), name=name)


ROW_TILE = 256


def _row_spec(tr, width):
    return pl.BlockSpec((tr, width), lambda i: (i, 0))


def _full_spec(shape):
    return pl.BlockSpec(shape, lambda *_: (0,) * len(shape))


def _rms(xv):
    return lax.rsqrt(jnp.mean(xv * xv, axis=-1, keepdims=True) + EPS)


def _mod_row(mod_ref, row):
    return mod_ref[pl.ds(row, 1), :]


def pre_mix_fwd(x, mod, gain, dep=None):
    t, d = x.shape
    tr = _tile(t, ROW_TILE, SUBLANES)

    def body(x_ref, mod_ref, g_ref, h_ref):
        xv = x_ref[...]
        y = xv * _rms(xv) * g_ref[...]
        h_ref[...] = (y * (1.0 + _mod_row(mod_ref, SC_M)) + _mod_row(mod_ref, SH_M)).astype(BF16)

    return _pallas(
        body, [x, mod, gain], dep=dep, grid=(t // tr,),
        in_specs=[_row_spec(tr, d), _full_spec(mod.shape), _full_spec(gain.shape)],
        out_specs=_row_spec(tr, d),
        out_shape=jax.ShapeDtypeStruct((t, d), BF16),
        sem=("parallel",), name="pre_mix_fwd")


def pre_ffn_fwd(x, o_m, mod, gain, dep=None):
    t, d = x.shape
    tr = _tile(t, ROW_TILE, SUBLANES)

    def body(x_ref, om_ref, mod_ref, g_ref, x1_ref, h_ref):
        x1 = x_ref[...] + _mod_row(mod_ref, GT_M) * om_ref[...]
        x1_ref[...] = x1
        y = x1 * _rms(x1) * g_ref[...]
        h_ref[...] = (y * (1.0 + _mod_row(mod_ref, SC_F)) + _mod_row(mod_ref, SH_F)).astype(BF16)

    return _pallas(
        body, [x, o_m, mod, gain], dep=dep, grid=(t // tr,),
        in_specs=[_row_spec(tr, d), _row_spec(tr, d), _full_spec(mod.shape), _full_spec(gain.shape)],
        out_specs=[_row_spec(tr, d), _row_spec(tr, d)],
        out_shape=[jax.ShapeDtypeStruct((t, d), F32), jax.ShapeDtypeStruct((t, d), BF16)],
        sem=("parallel",), name="pre_ffn_fwd")


def loss_head(x1, o_f, target, mod, dep=None):
    t, d = x1.shape
    tr = _tile(t, ROW_TILE, SUBLANES)

    def body(x1_ref, of_ref, tg_ref, mod_ref, loss_ref, dy_ref, dof_ref, acc_ref):
        i = pl.program_id(0)
        gt = _mod_row(mod_ref, GT_F)
        of = of_ref[...]
        err = x1_ref[...] + gt * of - tg_ref[...]
        dy = err * (1.0 / d)
        dy_ref[...] = dy
        dof_ref[...] = (dy * gt).astype(BF16)
        part = (0.5 / d) * jnp.sum(jnp.sum(err * err, axis=1, keepdims=True), axis=0, keepdims=True)
        dgt = jnp.sum(dy * of, axis=0, keepdims=True)

        @pl.when(i == 0)
        def _():
            loss_ref[...] = jnp.zeros_like(loss_ref)
            acc_ref[...] = jnp.zeros_like(acc_ref)

        loss_ref[...] += part
        acc_ref[pl.ds(0, 1), :] += dgt

    return _pallas(
        body, [x1, o_f, target, mod], dep=dep, grid=(t // tr,),
        in_specs=[_row_spec(tr, d), _row_spec(tr, d), _row_spec(tr, d), _full_spec(mod.shape)],
        out_specs=[_full_spec((1, 1)), _row_spec(tr, d), _row_spec(tr, d), _full_spec((SUBLANES, d))],
        out_shape=[jax.ShapeDtypeStruct((1, 1), F32), jax.ShapeDtypeStruct((t, d), F32),
                   jax.ShapeDtypeStruct((t, d), BF16), jax.ShapeDtypeStruct((SUBLANES, d), F32)],
        sem=("arbitrary",), name="loss_head")


def _norm_bwd(xv, dh, sc, gain):
    rstd = _rms(xv)
    yn = xv * rstd
    dsh = jnp.sum(dh, axis=0, keepdims=True)
    dsc = jnp.sum(dh * (yn * gain), axis=0, keepdims=True)
    dgain = jnp.sum(dh * (1.0 + sc) * yn, axis=0, keepdims=True)
    dyn = dh * ((1.0 + sc) * gain)
    dx = rstd * (dyn - yn * jnp.mean(dyn * yn, axis=-1, keepdims=True))
    return dx, dsh, dsc, dgain


def pre_ffn_bwd(x1, dh2, dy, o_m, mod, gain, dep=None):
    t, d = x1.shape
    tr = _tile(t, ROW_TILE, SUBLANES)

    def body(x1_ref, dh_ref, dy_ref, om_ref, mod_ref, g_ref, dx1_ref, dom_ref, acc_ref):
        i = pl.program_id(0)
        dxn, dsh, dsc, dgain = _norm_bwd(x1_ref[...], dh_ref[...], _mod_row(mod_ref, SC_F), g_ref[...])
        dx1 = dy_ref[...] + dxn
        dx1_ref[...] = dx1
        dom_ref[...] = (dx1 * _mod_row(mod_ref, GT_M)).astype(BF16)
        dgt = jnp.sum(dx1 * om_ref[...], axis=0, keepdims=True)

        @pl.when(i == 0)
        def _():
            acc_ref[...] = jnp.zeros_like(acc_ref)

        acc_ref[pl.ds(0, 1), :] += dsh
        acc_ref[pl.ds(1, 1), :] += dsc
        acc_ref[pl.ds(2, 1), :] += dgain
        acc_ref[pl.ds(3, 1), :] += dgt

    return _pallas(
        body, [x1, dh2, dy, o_m, mod, gain], dep=dep, grid=(t // tr,),
        in_specs=[_row_spec(tr, d)] * 4 + [_full_spec(mod.shape), _full_spec(gain.shape)],
        out_specs=[_row_spec(tr, d), _row_spec(tr, d), _full_spec((SUBLANES, d))],
        out_shape=[jax.ShapeDtypeStruct((t, d), F32), jax.ShapeDtypeStruct((t, d), BF16),
                   jax.ShapeDtypeStruct((SUBLANES, d), F32)],
        sem=("arbitrary",), name="pre_ffn_bwd")


def pre_mix_bwd(x, dh, dx1, mod, gain, dep=None):
    t, d = x.shape
    tr = _tile(t, ROW_TILE, SUBLANES)

    def body(x_ref, dh_ref, dx1_ref, mod_ref, g_ref, gx_ref, acc_ref):
        i = pl.program_id(0)
        dxn, dsh, dsc, dgain = _norm_bwd(x_ref[...], dh_ref[...], _mod_row(mod_ref, SC_M), g_ref[...])
        gx_ref[...] = dx1_ref[...] + dxn

        @pl.when(i == 0)
        def _():
            acc_ref[...] = jnp.zeros_like(acc_ref)

        acc_ref[pl.ds(0, 1), :] += dsh
        acc_ref[pl.ds(1, 1), :] += dsc
        acc_ref[pl.ds(2, 1), :] += dgain

    return _pallas(
        body, [x, dh, dx1, mod, gain], dep=dep, grid=(t // tr,),
        in_specs=[_row_spec(tr, d)] * 3 + [_full_spec(mod.shape), _full_spec(gain.shape)],
        out_specs=[_row_spec(tr, d), _full_spec((SUBLANES, d))],
        out_shape=[jax.ShapeDtypeStruct((t, d), F32), jax.ShapeDtypeStruct((SUBLANES, d), F32)],
        sem=("arbitrary",), name="pre_mix_bwd")


def merge_fwd(p, y_attn, y_conv, off_ga, off_gc, dep=None):
    t, d = y_attn.shape
    tr = _tile(t, ROW_TILE, SUBLANES)
    cw = math.gcd(math.gcd(off_ga, off_gc), math.gcd(d, 512))
    nc = d // cw

    def body(ga_ref, gc_ref, ya_ref, yc_ref, o_ref):
        o_ref[...] = (_sigmoid(ga_ref[...]) * ya_ref[...] + _sigmoid(gc_ref[...]) * yc_ref[...]).astype(BF16)

    return _pallas(
        body, [p, p, y_attn, y_conv], dep=dep, grid=(t // tr, nc),
        in_specs=[pl.BlockSpec((tr, cw), lambda i, j: (i, off_ga // cw + j)),
                  pl.BlockSpec((tr, cw), lambda i, j: (i, off_gc // cw + j)),
                  pl.BlockSpec((tr, cw), lambda i, j: (i, j)),
                  pl.BlockSpec((tr, cw), lambda i, j: (i, j))],
        out_specs=pl.BlockSpec((tr, cw), lambda i, j: (i, j)),
        out_shape=jax.ShapeDtypeStruct((t, d), BF16),
        sem=("parallel", "parallel"), name="merge_fwd")


def merge_bwd(p, y_attn, y_conv, dmerged, off_ga, off_gc, dep=None):
    t, d = y_attn.shape
    tr = _tile(t, ROW_TILE, SUBLANES)
    cw = math.gcd(math.gcd(off_ga, off_gc), math.gcd(d, 512))
    nc = d // cw

    def body(ga_ref, gc_ref, ya_ref, yc_ref, dm_ref, dya_ref, dyc_ref, dga_ref, dgc_ref):
        dm = dm_ref[...]
        sa = _sigmoid(ga_ref[...])
        sc = _sigmoid(gc_ref[...])
        dya_ref[...] = (dm * sa).astype(BF16)
        dyc_ref[...] = (dm * sc).astype(BF16)
        dga_ref[...] = (dm * ya_ref[...] * sa * (1.0 - sa)).astype(BF16)
        dgc_ref[...] = (dm * yc_ref[...] * sc * (1.0 - sc)).astype(BF16)

    blk = pl.BlockSpec((tr, cw), lambda i, j: (i, j))
    return _pallas(
        body, [p, p, y_attn, y_conv, dmerged], dep=dep, grid=(t // tr, nc),
        in_specs=[pl.BlockSpec((tr, cw), lambda i, j: (i, off_ga // cw + j)),
                  pl.BlockSpec((tr, cw), lambda i, j: (i, off_gc // cw + j)), blk, blk, blk],
        out_specs=[blk] * 4,
        out_shape=[jax.ShapeDtypeStruct((t, d), BF16)] * 4,
        sem=("parallel", "parallel"), name="merge_bwd")


def ffn_perm(n_blocks):
    half = n_blocks // 2
    return tuple(2 * j if j < half else 2 * (j - half) + 1 for j in range(n_blocks))


def swiglu_fwd(f, nj, dep=None):
    t, two = f.shape
    tr = _tile(t, ROW_TILE, SUBLANES)
    npair = two // (2 * nj)

    def body(f_ref, o_ref):
        g = f_ref[:, :nj].astype(F32)
        u = f_ref[:, nj:].astype(F32)
        o_ref[...] = (g * _sigmoid(g) * u).astype(BF16)

    return _pallas(
        body, [f], dep=dep, grid=(t // tr, npair),
        in_specs=[pl.BlockSpec((tr, 2 * nj), lambda i, j: (i, j))],
        out_specs=pl.BlockSpec((tr, nj), lambda i, j: (i, j)),
        out_shape=jax.ShapeDtypeStruct((t, two // 2), BF16),
        sem=("parallel", "parallel"), name="swiglu_fwd")


def swiglu_bwd(f, dact, nj, dep=None):
    t, two = f.shape
    tr = _tile(t, ROW_TILE, SUBLANES)
    npair = two // (2 * nj)

    def body(f_ref, da_ref, o_ref):
        g = f_ref[:, :nj].astype(F32)
        u = f_ref[:, nj:].astype(F32)
        da = da_ref[...]
        s = _sigmoid(g)
        o_ref[:, :nj] = (da * u * (s * (1.0 + g * (1.0 - s)))).astype(BF16)
        o_ref[:, nj:] = (da * (g * s)).astype(BF16)

    return _pallas(
        body, [f, dact], dep=dep, grid=(t // tr, npair),
        in_specs=[pl.BlockSpec((tr, 2 * nj), lambda i, j: (i, j)), pl.BlockSpec((tr, nj), lambda i, j: (i, j))],
        out_specs=pl.BlockSpec((tr, 2 * nj), lambda i, j: (i, j)),
        out_shape=jax.ShapeDtypeStruct((t, two), BF16),
        sem=("parallel", "parallel"), name="swiglu_bwd")


def _t5_bucket_table():
    q_off = np.arange(BLOCK)
    k_off = np.arange(2 * BLOCK)
    dist = q_off[:, None] + BLOCK - k_off[None, :]
    n = np.maximum(dist, 0)
    nf = np.maximum(n, 1).astype(np.float32)
    large = MAX_EXACT + (np.log(nf / np.float32(MAX_EXACT)) / np.float32(math.log(MAX_DISTANCE / MAX_EXACT))
                         * np.float32(NUM_BUCKETS - MAX_EXACT)).astype(np.int32)
    large = np.minimum(large, NUM_BUCKETS - 1)
    bucket = np.where(n < MAX_EXACT, n, large).astype(np.int32)
    allowed = (dist >= 0) & (dist < WINDOW)
    return np.where(allowed, bucket, -1).astype(np.int32)


def bias_table(rel_bias, bucket_p, bucket_c, dep=None):
    nb, nq = rel_bias.shape

    def body(rb_ref, bkp_ref, bkc_ref, op_ref, oc_ref):
        for bk_ref, o_ref in ((bkp_ref, op_ref), (bkc_ref, oc_ref)):
            bk = bk_ref[...]
            for h in range(nq):
                acc = jnp.full(bk.shape, -jnp.inf, F32)
                for b in range(nb):
                    acc = jnp.where(bk == b, rb_ref[b, h], acc)
                o_ref[h] = acc

    return _pallas(
        body, [rel_bias, bucket_p, bucket_c], dep=dep,
        in_specs=[_SMEM, _VMEM, _VMEM], out_specs=[_VMEM, _VMEM],
        out_shape=[jax.ShapeDtypeStruct((nq,) + bucket_p.shape, F32)] * 2,
        name="bias_table")


def bias_table_bwd(dbp, dbc, bucket_p, bucket_c, dep=None):
    nq = dbp.shape[0]

    def body(dbp_ref, dbc_ref, bkp_ref, bkc_ref, o_ref):
        bkp, bkc = bkp_ref[...][None], bkc_ref[...][None]
        dp, dc = dbp_ref[...], dbc_ref[...]
        for b in range(NUM_BUCKETS):
            sel = jnp.where(bkp == b, dp, 0.0) + jnp.where(bkc == b, dc, 0.0)
            o_ref[b] = jnp.sum(jnp.sum(sel, axis=2, keepdims=True), axis=1, keepdims=True)

    return _pallas(
        body, [dbp, dbc, bucket_p, bucket_c], dep=dep,
        in_specs=[_VMEM] * 4, out_specs=_VMEM,
        out_shape=jax.ShapeDtypeStruct((NUM_BUCKETS, nq, 1, 1), F32),
        name="bias_table_bwd")


_NT = (((1,), (1,)), ((), ()))
_TN = (((0,), (0,)), ((), ()))


@jax.custom_vjp
def _bdot_nt(a, b):
    return lax.dot_general(a.astype(BF16), b.astype(BF16), _NT, preferred_element_type=F32)


def _bdot_nt_fwd(a, b):
    return _bdot_nt(a, b), (a, b)


def _bdot_nt_bwd(res, g):
    a, b = res
    gb = g.astype(BF16)
    da = jnp.dot(gb, b.astype(BF16), preferred_element_type=F32)
    db = lax.dot_general(gb, a.astype(BF16), _TN, preferred_element_type=F32)
    return da, db


_bdot_nt.defvjp(_bdot_nt_fwd, _bdot_nt_bwd)


@jax.custom_vjp
def _bdot_nn(a, b):
    return jnp.dot(a.astype(BF16), b.astype(BF16), preferred_element_type=F32)


def _bdot_nn_fwd(a, b):
    return _bdot_nn(a, b), (a, b)


def _bdot_nn_bwd(res, g):
    a, b = res
    gb = g.astype(BF16)
    da = lax.dot_general(gb, b.astype(BF16), _NT, preferred_element_type=F32)
    db = lax.dot_general(a.astype(BF16), gb, _TN, preferred_element_type=F32)
    return da, db


_bdot_nn.defvjp(_bdot_nn_fwd, _bdot_nn_bwd)


def _attn_math(q4, kp, kc, vp, vc, bp, bc, sink4, qg, kg, *, prev_ok, scale):
    g, b, hd = q4.shape
    q = q4.reshape(g * b, hd)
    qn = q * _rms(q) * qg
    kpn = kp * _rms(kp) * kg
    kcn = kc * _rms(kc) * kg
    lp = _bdot_nt(qn, kpn).reshape(g, b, b) * scale + bp
    lc = _bdot_nt(qn, kcn).reshape(g, b, b) * scale + bc
    lp = jnp.where(prev_ok, lp, -jnp.inf)
    m = jnp.maximum(jnp.maximum(jnp.max(lp, axis=-1, keepdims=True), jnp.max(lc, axis=-1, keepdims=True)), sink4)
    m = lax.stop_gradient(m)
    pp = jnp.exp(lp - m)
    pc = jnp.exp(lc - m)
    den = jnp.sum(pp, axis=-1, keepdims=True) + jnp.sum(pc, axis=-1, keepdims=True) + jnp.exp(sink4 - m)
    inv = 1.0 / den
    out = _bdot_nn((pp * inv).reshape(g * b, b), vp) + _bdot_nn((pc * inv).reshape(g * b, b), vc)
    return out.reshape(g, b, hd)


def _attn_specs(grp, hd, nblk, reverse):
    def blk(n):
        return nblk - 1 - n if reverse else n

    return [
        pl.BlockSpec((grp, BLOCK, hd), lambda h, n: (h, blk(n), 0)),
        pl.BlockSpec((None, BLOCK, hd), lambda h, n: (h, jnp.maximum(blk(n) - 1, 0), 0)),
        pl.BlockSpec((None, BLOCK, hd), lambda h, n: (h, blk(n), 0)),
        pl.BlockSpec((None, BLOCK, hd), lambda h, n: (h, jnp.maximum(blk(n) - 1, 0), 0)),
        pl.BlockSpec((None, BLOCK, hd), lambda h, n: (h, blk(n), 0)),
        pl.BlockSpec((grp, BLOCK, BLOCK), lambda h, n: (h, 0, 0)),
        pl.BlockSpec((grp, BLOCK, BLOCK), lambda h, n: (h, 0, 0)),
        pl.BlockSpec((grp, 1, 1), lambda h, n: (h, 0, 0)),
        pl.BlockSpec((1, hd), lambda h, n: (0, 0)),
        pl.BlockSpec((1, hd), lambda h, n: (0, 0)),
    ]


def attn_fwd(qh, kh, vh, bias_p, bias_c, sinks, qg, kg, dep=None):
    nq, t, hd = qh.shape
    nkv = kh.shape[0]
    grp, nblk = nq // nkv, t // BLOCK
    scale = hd ** -0.5

    def body(q_ref, kp_ref, kc_ref, vp_ref, vc_ref, bp_ref, bc_ref, s_ref, qg_ref, kg_ref, o_ref):
        prev_ok = pl.program_id(1) > 0
        out = _attn_math(q_ref[...], kp_ref[...], kc_ref[...], vp_ref[...], vc_ref[...], bp_ref[...], bc_ref[...],
                         s_ref[...], qg_ref[...], kg_ref[...], prev_ok=prev_ok, scale=scale)
        o_ref[...] = out.astype(BF16)

    return _pallas(
        body, [qh, kh, kh, vh, vh, bias_p, bias_c, sinks, qg, kg], dep=dep, grid=(nkv, nblk),
        in_specs=_attn_specs(grp, hd, nblk, False),
        out_specs=pl.BlockSpec((grp, BLOCK, hd), lambda h, n: (h, n, 0)),
        out_shape=jax.ShapeDtypeStruct((nq, t, hd), BF16),
        sem=("parallel", "parallel"), name="attn_fwd")


def attn_bwd(qh, kh, vh, bias_p, bias_c, sinks, qg, kg, doh, dep=None):
    nq, t, hd = qh.shape
    nkv = kh.shape[0]
    grp, nblk = nq // nkv, t // BLOCK
    scale = hd ** -0.5

    def body(q_ref, kp_ref, kc_ref, vp_ref, vc_ref, bp_ref, bc_ref, s_ref, qg_ref, kg_ref, do_ref,
             dq_ref, dk_ref, dv_ref, dbp_ref, dbc_ref, ds_ref, dqg_ref, dkg_ref, ck_ref, cv_ref):
        h, i = pl.program_id(0), pl.program_id(1)
        prev_ok = (nblk - 1 - i) > 0
        fn = functools.partial(_attn_math, prev_ok=prev_ok, scale=scale)
        _, vjp = jax.vjp(fn, q_ref[...], kp_ref[...], kc_ref[...], vp_ref[...], vc_ref[...], bp_ref[...], bc_ref[...],
                         s_ref[...], qg_ref[...], kg_ref[...])
        dq, dkp, dkc, dvp, dvc, dbp, dbc, dsk, dqg, dkg = vjp(do_ref[...])
        dq_ref[...] = dq

        @pl.when(i == 0)
        def _():
            ck_ref[...] = jnp.zeros_like(ck_ref)
            cv_ref[...] = jnp.zeros_like(cv_ref)
            dbp_ref[...] = jnp.zeros_like(dbp_ref)
            dbc_ref[...] = jnp.zeros_like(dbc_ref)
            ds_ref[...] = jnp.zeros_like(ds_ref)

        @pl.when((i == 0) & (h == 0))
        def _():
            dqg_ref[...] = jnp.zeros_like(dqg_ref)
            dkg_ref[...] = jnp.zeros_like(dkg_ref)

        dk_ref[...] = dkc + ck_ref[...]
        dv_ref[...] = dvc + cv_ref[...]
        ck_ref[...] = dkp
        cv_ref[...] = dvp
        dbp_ref[...] += dbp
        dbc_ref[...] += dbc
        ds_ref[...] += dsk
        dqg_ref[...] += dqg
        dkg_ref[...] += dkg

    rev = lambda h, n: (h, nblk - 1 - n, 0)
    return _pallas(
        body, [qh, kh, kh, vh, vh, bias_p, bias_c, sinks, qg, kg, doh], dep=dep, grid=(nkv, nblk),
        in_specs=_attn_specs(grp, hd, nblk, True) + [pl.BlockSpec((grp, BLOCK, hd), rev)],
        out_specs=[
            pl.BlockSpec((grp, BLOCK, hd), rev),
            pl.BlockSpec((None, BLOCK, hd), rev),
            pl.BlockSpec((None, BLOCK, hd), rev),
            pl.BlockSpec((grp, BLOCK, BLOCK), lambda h, n: (h, 0, 0)),
            pl.BlockSpec((grp, BLOCK, BLOCK), lambda h, n: (h, 0, 0)),
            pl.BlockSpec((grp, 1, 1), lambda h, n: (h, 0, 0)),
            pl.BlockSpec((1, hd), lambda h, n: (0, 0)),
            pl.BlockSpec((1, hd), lambda h, n: (0, 0)),
        ],
        out_shape=[
            jax.ShapeDtypeStruct((nq, t, hd), F32),
            jax.ShapeDtypeStruct((nkv, t, hd), F32),
            jax.ShapeDtypeStruct((nkv, t, hd), F32),
            jax.ShapeDtypeStruct((nq, BLOCK, BLOCK), F32),
            jax.ShapeDtypeStruct((nq, BLOCK, BLOCK), F32),
            jax.ShapeDtypeStruct((nq, 1, 1), F32),
            jax.ShapeDtypeStruct((1, hd), F32),
            jax.ShapeDtypeStruct((1, hd), F32),
        ],
        scratch=[pltpu.VMEM((BLOCK, hd), F32), pltpu.VMEM((BLOCK, hd), F32)],
        sem=("arbitrary", "arbitrary"), name="attn_bwd")


CONV_TILE = 256


def _conv_halo_specs(tb, ch, nblk):
    per = tb // CONV_HALO
    last = nblk * per - 1
    cur = pl.BlockSpec((tb, ch), lambda n: (n, 0))
    prev = pl.BlockSpec((CONV_HALO, ch), lambda n: (jnp.maximum(n * per - 1, 0), 0))
    nxt = pl.BlockSpec((CONV_HALO, ch), lambda n: (jnp.minimum((n + 1) * per, last), 0))
    return cur, prev, nxt


def _ln_silu(co, ln_g, ln_b):
    mu = jnp.mean(co, axis=-1, keepdims=True)
    cen = co - mu
    rstd = lax.rsqrt(jnp.mean(cen * cen, axis=-1, keepdims=True) + EPS)
    xhat = cen * rstd
    z = xhat * ln_g + ln_b
    return xhat, rstd, z


def conv_fwd(ca, cb, conv_w, conv_b, ln_g, ln_b, dep=None):
    t, ch = ca.shape
    tb = _tile(t, CONV_TILE, CONV_HALO)
    nblk = t // tb
    cur, prev, _ = _conv_halo_specs(tb, ch, nblk)
    lead = CONV_HALO - (CONV_WIDTH - 1)

    def body(ca_ref, cb_ref, cap_ref, cbp_ref, w_ref, b_ref, g_ref, bb_ref, s_ref, ubuf):
        n = pl.program_id(0)
        halo = cap_ref[...] * _sigmoid(cbp_ref[...])
        ubuf[pl.ds(0, CONV_HALO), :] = jnp.where(n > 0, halo, 0.0)
        ubuf[pl.ds(CONV_HALO, tb), :] = ca_ref[...] * _sigmoid(cb_ref[...])
        acc = jnp.broadcast_to(b_ref[...], (tb, ch))
        for k in range(CONV_WIDTH):
            acc = acc + w_ref[pl.ds(k, 1), :] * ubuf[pl.ds(lead + k, tb), :]
        _, _, z = _ln_silu(acc, g_ref[...], bb_ref[...])
        s_ref[...] = (z * _sigmoid(z)).astype(BF16)

    vec = _full_spec((1, ch))
    return _pallas(
        body, [ca, cb, ca, cb, conv_w, conv_b, ln_g, ln_b], dep=dep, grid=(nblk,),
        in_specs=[cur, cur, prev, prev, _full_spec(conv_w.shape), vec, vec, vec],
        out_specs=cur,
        out_shape=jax.ShapeDtypeStruct((t, ch), BF16),
        scratch=[pltpu.VMEM((CONV_HALO + tb, ch), F32)],
        sem=("parallel",), name="conv_fwd")


def conv_bwd(ca, cb, ds, conv_w, conv_b, ln_g, ln_b, dep=None):
    t, ch = ca.shape
    tb = _tile(t, CONV_TILE, CONV_HALO)
    nblk = t // tb
    cur, prev, nxt = _conv_halo_specs(tb, ch, nblk)
    lead = CONV_HALO - (CONV_WIDTH - 1)
    ext = tb + CONV_HALO

    def body(ca_ref, cb_ref, cap_ref, cbp_ref, can_ref, cbn_ref, ds_ref, dsn_ref, w_ref, b_ref, g_ref, bb_ref,
             dca_ref, dcb_ref, dw_ref, dvec_ref, ubuf, dbuf):
        n = pl.program_id(0)
        is_last = n == nblk - 1
        sig_b = _sigmoid(cb_ref[...])
        cav = ca_ref[...]
        ubuf[pl.ds(0, CONV_HALO), :] = jnp.where(n > 0, cap_ref[...] * _sigmoid(cbp_ref[...]), 0.0)
        ubuf[pl.ds(CONV_HALO, tb), :] = cav * sig_b
        ubuf[pl.ds(CONV_HALO + tb, CONV_HALO), :] = can_ref[...] * _sigmoid(cbn_ref[...])
        co = jnp.broadcast_to(b_ref[...], (ext, ch))
        for k in range(CONV_WIDTH):
            co = co + w_ref[pl.ds(k, 1), :] * ubuf[pl.ds(lead + k, ext), :]
        xhat, rstd, z = _ln_silu(co, g_ref[...], bb_ref[...])
        dsv = jnp.concatenate([ds_ref[...], jnp.where(is_last, 0.0, dsn_ref[...])], axis=0)
        sg = _sigmoid(z)
        dz = dsv * (sg * (1.0 + z * (1.0 - sg)))
        dxh = dz * g_ref[...]
        dco = rstd * (dxh - jnp.mean(dxh, axis=-1, keepdims=True)
                      - xhat * jnp.mean(dxh * xhat, axis=-1, keepdims=True))
        dbuf[...] = dco

        @pl.when(n == 0)
        def _():
            dw_ref[...] = jnp.zeros_like(dw_ref)
            dvec_ref[...] = jnp.zeros_like(dvec_ref)

        dco_cur = dco[:tb]
        dvec_ref[pl.ds(0, 1), :] += jnp.sum(dco_cur, axis=0, keepdims=True)
        dvec_ref[pl.ds(1, 1), :] += jnp.sum(dz[:tb] * xhat[:tb], axis=0, keepdims=True)
        dvec_ref[pl.ds(2, 1), :] += jnp.sum(dz[:tb], axis=0, keepdims=True)
        du = jnp.zeros((tb, ch), F32)
        for k in range(CONV_WIDTH):
            du = du + w_ref[pl.ds(k, 1), :] * dbuf[pl.ds(CONV_WIDTH - 1 - k, tb), :]
            dw_ref[pl.ds(k, 1), :] += jnp.sum(dco_cur * ubuf[pl.ds(lead + k, tb), :], axis=0, keepdims=True)
        dca_ref[...] = (du * sig_b).astype(BF16)
        dcb_ref[...] = (du * cav * sig_b * (1.0 - sig_b)).astype(BF16)

    vec = _full_spec((1, ch))
    return _pallas(
        body, [ca, cb, ca, cb, ca, cb, ds, ds, conv_w, conv_b, ln_g, ln_b], dep=dep, grid=(nblk,),
        in_specs=[cur, cur, prev, prev, nxt, nxt, cur, nxt, _full_spec(conv_w.shape), vec, vec, vec],
        out_specs=[cur, cur, _full_spec(conv_w.shape), _full_spec((SUBLANES, ch))],
        out_shape=[jax.ShapeDtypeStruct((t, ch), BF16), jax.ShapeDtypeStruct((t, ch), BF16),
                   jax.ShapeDtypeStruct(conv_w.shape, F32), jax.ShapeDtypeStruct((SUBLANES, ch), F32)],
        scratch=[pltpu.VMEM((2 * CONV_HALO + tb, ch), F32), pltpu.VMEM((ext, ch), F32)],
        sem=("arbitrary",), name="conv_bwd")


def ada_fwd(c_t, w_ada, dep=None):
    d, nc = w_ada.shape
    nex = c_t.shape[1]
    tn = _tile(nc, 512)

    def body(ct_ref, w_ref, o_ref):
        w = w_ref[...]
        ct = ct_ref[...]
        cact = ct * _sigmoid(ct)
        rows = [jnp.sum(w * cact[:, b:b + 1], axis=0, keepdims=True) for b in range(nex)]
        o_ref[...] = jnp.concatenate(rows, axis=0)

    return _pallas(
        body, [c_t, w_ada], dep=dep, grid=(nc // tn,),
        in_specs=[_full_spec(c_t.shape), pl.BlockSpec((d, tn), lambda j: (0, j))],
        out_specs=pl.BlockSpec((nex, tn), lambda j: (0, j)),
        out_shape=jax.ShapeDtypeStruct((nex, nc), F32),
        sem=("parallel",), name="ada_fwd")


def _adamw_math(w, g, m, v):
    m = ADAM_B1 * m + (1.0 - ADAM_B1) * g
    v = ADAM_B2 * v + (1.0 - ADAM_B2) * (g * g)
    m_hat = m / (1.0 - ADAM_B1 ** ADAM_STEP)
    v_hat = v / (1.0 - ADAM_B2 ** ADAM_STEP)
    delta = -ADAM_LR * (m_hat / (jnp.sqrt(v_hat) + ADAM_EPS) + ADAM_WD * w)
    return delta, m, v


def adamw(w, g, m, v, name, dep=None):
    r, n = w.shape
    tr = _tile(r, 512, SUBLANES)
    tn = _tile(n, 1024)

    def body(w_ref, g_ref, m_ref, v_ref, d_ref, nm_ref, nv_ref):
        d_ref[...], nm_ref[...], nv_ref[...] = _adamw_math(w_ref[...], g_ref[...], m_ref[...], v_ref[...])

    blk = pl.BlockSpec((tr, tn), lambda i, j: (i, j))
    return _pallas(
        body, [w, g, m, v], dep=dep, grid=(r // tr, n // tn),
        in_specs=[blk] * 4, out_specs=[blk] * 3,
        out_shape=[jax.ShapeDtypeStruct((r, n), F32)] * 3,
        sem=("parallel", "parallel"), name=name)


def ada_grad_adamw(c_t, dmod_cols, w, m, v, dep=None):
    d, nc = w.shape
    nex = c_t.shape[1]
    tr = _tile(d, 512, SUBLANES)
    tn = _tile(nc, 1024)

    def body(ct_ref, dm_ref, w_ref, m_ref, v_ref, g_ref, d_ref, nm_ref, nv_ref):
        ct = ct_ref[...]
        cact = ct * _sigmoid(ct)
        dm = dm_ref[...]
        g = cact[:, 0:1] * dm[0:1, :]
        for b in range(1, nex):
            g = g + cact[:, b:b + 1] * dm[b:b + 1, :]
        g_ref[...] = g
        d_ref[...], nm_ref[...], nv_ref[...] = _adamw_math(w_ref[...], g, m_ref[...], v_ref[...])

    blk = pl.BlockSpec((tr, tn), lambda i, j: (i, j))
    return _pallas(
        body, [c_t, dmod_cols, w, m, v], dep=dep, grid=(d // tr, nc // tn),
        in_specs=[pl.BlockSpec((tr, nex), lambda i, j: (i, 0)), pl.BlockSpec((nex, tn), lambda i, j: (0, j)),
                  blk, blk, blk],
        out_specs=[blk] * 4,
        out_shape=[jax.ShapeDtypeStruct((d, nc), F32)] * 4,
        sem=("parallel", "parallel"), name="ada_grad_adamw")


def small_sum_adamw(gathered, w, m, v, dep=None):
    ndev, r, n = gathered.shape

    def body(ga_ref, w_ref, m_ref, v_ref, g_ref, d_ref, nm_ref, nv_ref):
        g = ga_ref[0]
        for s in range(1, ndev):
            g = g + ga_ref[s]
        g_ref[...] = g
        d_ref[...], nm_ref[...], nv_ref[...] = _adamw_math(w_ref[...], g, m_ref[...], v_ref[...])

    return _pallas(
        body, [gathered, w, m, v], dep=dep, in_specs=[_VMEM] * 4, out_specs=[_VMEM] * 4,
        out_shape=[jax.ShapeDtypeStruct((r, n), F32)] * 4,
        name="small_sum_adamw")


def _position():
    return lax.axis_index("x"), lax.axis_index("y"), lax.axis_index("c")


def _other_chips(x, y):
    return [(1 - x, y), (x, 1 - y), (1 - x, 1 - y)]


def allgather_small(block, name, dep=None):
    rows, width = block.shape

    def body(x_ref, out_ref, send_sems, recv_sems, local_sem):
        x, y, c = _position()
        me, sibling = (x, y, c), (x, y, 1 - c)
        chips = _other_chips(x, y)

        def slot(px, py, pc):
            return out_ref.at[4 * px + 2 * py + pc]

        def copy(k, block_of, to, src=None):
            return pltpu.make_async_remote_copy(
                src_ref=slot(*block_of) if src is None else src, dst_ref=slot(*block_of),
                send_sem=send_sems.at[k], recv_sem=recv_sems.at[k], device_id=to, device_id_type=MESH)

        mine = pltpu.make_async_copy(x_ref, slot(*me), local_sem)
        mine.start()
        first = [copy(0, me, sibling, src=x_ref)]
        first += [copy(1 + j, me, (*chip, c), src=x_ref) for j, chip in enumerate(chips)]
        for cp in first:
            cp.start()
        passed = [copy(4 + j, (*chip, c), sibling) for j, chip in enumerate(chips)]
        for j, chip in enumerate(chips):
            copy(1 + j, (*chip, c), me).wait_recv()
            passed[j].start()
        copy(0, sibling, me).wait_recv()
        for j, chip in enumerate(chips):
            copy(4 + j, (*chip, 1 - c), me).wait_recv()
        for cp in first + passed:
            cp.wait_send()
        mine.wait()

    return _pallas(
        body, [block], dep=dep,
        out_shape=jax.ShapeDtypeStruct((N_DEV, rows, width), block.dtype),
        in_specs=[_VMEM], out_specs=_VMEM,
        scratch=[pltpu.SemaphoreType.DMA((7,)), pltpu.SemaphoreType.DMA((7,)), pltpu.SemaphoreType.DMA],
        name=name)


class Started(NamedTuple):
    send_sems: Any
    recv_sems: Any
    bufs: list
    token: Any


def exchange_start(name, bufs, n_copies, plan, dep=None):
    nb = len(bufs)

    def body(*refs):
        in_refs = refs[:nb]
        send_sems, recv_sems = refs[nb], refs[nb + 1]
        token = refs[-1]
        for cp in plan(in_refs, send_sems, recv_sems):
            cp.start()
        token[...] = jnp.zeros_like(token)

    outs = _pallas(
        body, [pltpu.with_memory_space_constraint(b, pltpu.HBM) for b in bufs], dep=dep, name=name,
        out_shape=(pltpu.SemaphoreType.DMA((n_copies,)), pltpu.SemaphoreType.DMA((n_copies,)),
                   *[pltpu.HBM(b.shape, b.dtype) for b in bufs],
                   jax.ShapeDtypeStruct((SUBLANES, LANES), F32)),
        in_specs=[_HBM] * nb,
        out_specs=(_SEM, _SEM, *[_HBM] * nb, _VMEM),
        input_output_aliases={i: 2 + i for i in range(nb)},
        compiler_params=pltpu.CompilerParams(has_side_effects=_EFFECT))
    return Started(outs[0], outs[1], list(outs[2:2 + nb]), outs[-1])


def exchange_wait(name, started, plan, dep=None):
    nb = len(started.bufs)

    def body(*refs):
        for cp in plan(refs[:nb], refs[nb], refs[nb + 1]):
            cp.wait_send()
            cp.wait_recv()

    outs = _pallas(
        body, [*started.bufs, started.send_sems, started.recv_sems], dep=dep, name=name,
        out_shape=tuple(pltpu.HBM(b.shape, b.dtype) for b in started.bufs),
        in_specs=[_HBM] * nb + [_SEM, _SEM],
        out_specs=tuple([_HBM] * nb),
        input_output_aliases={i: i for i in range(nb)},
        compiler_params=pltpu.CompilerParams(has_side_effects=_EFFECT))
    return list(outs)


def _remote(src, dst, send_sems, recv_sems, i, to):
    return pltpu.make_async_remote_copy(src_ref=src, dst_ref=dst, send_sem=send_sems.at[i], recv_sem=recv_sems.at[i],
                                        device_id=to, device_id_type=MESH)


def _half_rows(buf_rows, chip_idx, pc):
    half = buf_rows // (2 * N_CHIPS)
    return pl.ds((2 * chip_idx + pc) * half, half)


def plan_gather_ici(refs, send_sems, recv_sems):
    x, y, c = _position()
    copies = []
    for k, ref in enumerate(refs):
        rows = ref.at[_half_rows(ref.shape[0], 2 * x + y, c), :]
        for j, chip in enumerate(_other_chips(x, y)):
            copies.append(_remote(rows, rows, send_sems, recv_sems, 3 * k + j, (*chip, c)))
    return copies


def plan_gather_d2d(refs, send_sems, recv_sems):
    x, y, c = _position()
    copies = []
    for k, ref in enumerate(refs):
        for j, (px, py) in enumerate(_other_chips(x, y)):
            rows = ref.at[_half_rows(ref.shape[0], 2 * px + py, c), :]
            copies.append(_remote(rows, rows, send_sems, recv_sems, 3 * k + j, (x, y, 1 - c)))
    return copies


def plan_pair_exchange(refs, send_sems, recv_sems):
    x, y, c = _position()
    nw = len(refs) // 2
    copies = []
    for k in range(nw):
        for chip in range(N_CHIPS):
            copies.append(_remote(refs[k].at[chip, 1 - c], refs[nw + k].at[chip], send_sems, recv_sems,
                                  N_CHIPS * k + chip, (x, y, 1 - c)))
    return copies


def plan_chip_exchange(refs, send_sems, recv_sems):
    x, y, c = _position()
    nw = len(refs) // 2
    copies = []
    for k in range(nw):
        for j, (px, py) in enumerate(_other_chips(x, y)):
            copies.append(_remote(refs[k].at[2 * px + py], refs[nw + k].at[2 * x + y], send_sems, recv_sems,
                                  3 * k + j, (px, py, c)))
    return copies


def plan_pair_share(refs, send_sems, recv_sems):
    x, y, c = _position()
    return [_remote(ref.at[c], ref.at[c], send_sems, recv_sems, k, (x, y, 1 - c)) for k, ref in enumerate(refs)]


def cast_into_slot(src, slot, n_slots, name, dep=None):
    r, n = src.shape
    tr = _tile(r, 512, BF16_SUBLANES)
    tn = _tile(n, 1024)

    def body(slot_ref, s_ref, o_ref):
        o_ref[...] = s_ref[...].astype(BF16)

    return _pallas(
        body, [slot, src], dep=dep, n_prefetch=1, grid=(r // tr, n // tn),
        in_specs=[pl.BlockSpec((tr, tn), lambda i, j, sl: (i, j))],
        out_specs=pl.BlockSpec((None, tr, tn), lambda i, j, sl: (sl[0], i, j)),
        out_shape=jax.ShapeDtypeStruct((n_slots, r, n), BF16),
        sem=("parallel", "parallel"), name=name)


def pair_sum(g, r, core, name, dep=None):
    nchip, _, h, n = g.shape
    th = _tile(h, 512, BF16_SUBLANES)
    tn = _tile(n, 1024)

    def body(core_ref, g_ref, r_ref, o_ref):
        o_ref[...] = (g_ref[...].astype(F32) + r_ref[...].astype(F32)).astype(BF16)

    return _pallas(
        body, [core, g, r], dep=dep, n_prefetch=1, grid=(nchip, h // th, n // tn),
        in_specs=[pl.BlockSpec((None, None, th, tn), lambda a, i, j, cr: (a, cr[0], i, j)),
                  pl.BlockSpec((None, th, tn), lambda a, i, j, cr: (a, i, j))],
        out_specs=pl.BlockSpec((None, th, tn), lambda a, i, j, cr: (a, i, j)),
        out_shape=jax.ShapeDtypeStruct((nchip, h, n), BF16),
        sem=("parallel", "parallel", "parallel"), name=name)


def chip_sum(own, got, where, name, dep=None):
    nchip, h, n = got.shape
    th = _tile(h, 512, BF16_SUBLANES)
    tn = _tile(n, 1024)

    def body(where_ref, own_ref, *rest):
        got_refs, o_ref = rest[:nchip], rest[nchip]
        chip = where_ref[0]
        acc = None
        for s in range(nchip):
            term = jnp.where(chip == s, own_ref[...], got_refs[s][...]).astype(F32)
            acc = term if acc is None else acc + term
        o_ref[...] = acc

    def got_spec(s):
        return pl.BlockSpec((None, th, tn), lambda i, j, wr: (jnp.where(wr[0] == s, (s + 1) % nchip, s), i, j))

    return _pallas(
        body, [where, own, *[got] * nchip], dep=dep, n_prefetch=1, grid=(h // th, n // tn),
        in_specs=[pl.BlockSpec((None, th, tn), lambda i, j, wr: (wr[0], i, j))]
        + [got_spec(s) for s in range(nchip)],
        out_specs=pl.BlockSpec((None, th, tn), lambda i, j, wr: (wr[1], i, j)),
        out_shape=jax.ShapeDtypeStruct((2, h, n), F32),
        sem=("parallel", "parallel"), name=name)


PACK_ROWS = SUBLANES


def _part_rows(shape):
    size = int(np.prod(shape))
    return -(-size // (PACK_ROWS * LANES)) * PACK_ROWS


def _pack_rows(parts):
    rows = []
    for p in parts:
        flat = p.reshape(-1)
        flat = jnp.pad(flat, (0, _part_rows(p.shape) * LANES - flat.shape[0]))
        rows.append(flat.reshape(-1, LANES))
    return jnp.concatenate(rows, axis=0)


def _unpack_rows(packed, shapes):
    out, row = [], 0
    for s in shapes:
        size, nrows = int(np.prod(s)), _part_rows(s)
        out.append(packed[row:row + nrows].reshape(-1)[:size].reshape(s))
        row += nrows
    return out


class _InOrder:
    def __init__(self):
        self.last = None

    def __call__(self, fn, *args, **kw):
        out = fn(*args, dep=self.last, **kw)
        if isinstance(out, Started):
            self.last = out.token
        elif isinstance(out, (tuple, list)):
            self.last = out[0]
        else:
            self.last = out
        return out


def kernel(x, c, w_ada, b_ada, norm_mix_g, w_in, q_norm_g, k_norm_g, attn_sinks, rel_bias, w_attn_out, conv_w, conv_b, conv_ln_g, conv_ln_b, w_conv_out, w_mix_out, norm_ffn_g, w_ffn_in, w_ffn_out, loss_target, m_w_ada, m_b_ada, m_norm_mix_g, m_w_in, m_q_norm_g, m_k_norm_g, m_attn_sinks, m_rel_bias, m_w_attn_out, m_conv_w, m_conv_b, m_conv_ln_g, m_conv_ln_b, m_w_conv_out, m_w_mix_out, m_norm_ffn_g, m_w_ffn_in, m_w_ffn_out, v_w_ada, v_b_ada, v_norm_mix_g, v_w_in, v_q_norm_g, v_k_norm_g, v_attn_sinks, v_rel_bias, v_w_attn_out, v_conv_w, v_conv_b, v_conv_ln_g, v_conv_ln_b, v_w_conv_out, v_w_mix_out, v_norm_ffn_g, v_w_ffn_in, v_w_ffn_out):
    run = _InOrder()
    xi, yi, ci = _position()
    chip = 2 * xi + yi
    me = 2 * chip + ci
    chip_arr = chip.astype(jnp.int32).reshape(1)
    core_arr = ci.astype(jnp.int32).reshape(1)
    where_arr = jnp.stack([chip, ci]).astype(jnp.int32)

    xe, tgt = x[0], loss_target[0]
    t, d = xe.shape
    hd = q_norm_g.shape[-1]
    nq = attn_sinks.shape[-1]
    aw = nq * hd
    ch = conv_b.shape[-1]
    in_width = N_CHIPS * w_in.shape[-1]
    kvw = (in_width - aw - 2 * ch - 2 * d) // 2
    nkv = kvw // hd
    dff = N_CHIPS * w_ffn_out.shape[1]
    off_k, off_v, off_ca = aw, aw + kvw, aw + 2 * kvw
    off_cb, off_ga, off_gc = off_ca + ch, off_ca + 2 * ch, off_ca + 2 * ch + d
    nc_ada = w_ada.shape[-1]
    ch_loc = conv_w.shape[-1]
    nj_ffn = w_ffn_in.shape[-1]
    perm_ffn = ffn_perm(N_CHIPS)

    big = {"w_in": w_in[0], "w_attn_out": w_attn_out[0], "w_conv_out": w_conv_out[0], "w_mix_out": w_mix_out[0],
           "w_ffn_in": w_ffn_in[0], "w_ffn_out": w_ffn_out[0]}
    moments = {"w_in": (m_w_in, v_w_in), "w_attn_out": (m_w_attn_out, v_w_attn_out),
               "w_conv_out": (m_w_conv_out, v_w_conv_out), "w_mix_out": (m_w_mix_out, v_w_mix_out),
               "w_ffn_in": (m_w_ffn_in, v_w_ffn_in), "w_ffn_out": (m_w_ffn_out, v_w_ffn_out)}
    gather_groups = {"in": ["w_in"], "mid": ["w_attn_out", "w_conv_out", "w_mix_out"], "ffn_in": ["w_ffn_in"],
                     "ffn_out": ["w_ffn_out"]}
    grads, deltas, new_m, new_v = {}, {}, {}, {}

    def gather_ici_start(gname):
        bufs = []
        for n in gather_groups[gname]:
            r, ncol = big[n].shape
            bufs.append(run(cast_into_slot, big[n], chip_arr, N_CHIPS, "cast_" + n).reshape(N_CHIPS * r, ncol))
        return run(exchange_start, "gather_ici_start_" + gname, bufs, 3 * len(bufs), plan_gather_ici)

    def gather_pass_on(gname, ici):
        landed = run(exchange_wait, "gather_ici_wait_" + gname, ici, plan_gather_ici)
        return run(exchange_start, "gather_d2d_start_" + gname, landed, 3 * len(landed), plan_gather_d2d)

    def gathered(gname, d2d):
        outs = run(exchange_wait, "gather_d2d_wait_" + gname, d2d, plan_gather_d2d)
        return [o.reshape(N_CHIPS, *big[n].shape) for o, n in zip(outs, gather_groups[gname])]

    def rs_pair_start(gname, names, partials):
        blocks = [g.reshape(N_CHIPS, 2, big[n].shape[0] // 2, big[n].shape[1]) for n, g in zip(names, partials)]
        land = [lax.empty((N_CHIPS,) + b.shape[2:], BF16) for b in blocks]
        return run(exchange_start, "pair_exchange_start_" + gname, blocks + land, N_CHIPS * len(blocks),
                   plan_pair_exchange)

    def rs_chip_start(gname, names, pair):
        nw = len(names)
        outs = run(exchange_wait, "pair_exchange_wait_" + gname, pair, plan_pair_exchange)
        sums = [run(pair_sum, g, r, core_arr, "pair_sum_" + n) for n, g, r in zip(names, outs[:nw], outs[nw:])]
        land = [lax.empty(s.shape, BF16) for s in sums]
        return run(exchange_start, "chip_exchange_start_" + gname, sums + land, 3 * nw, plan_chip_exchange)

    def rs_share_start(gname, names, chipx):
        nw = len(names)
        outs = run(exchange_wait, "chip_exchange_wait_" + gname, chipx, plan_chip_exchange)
        halves = [run(chip_sum, s, r, where_arr, "chip_sum_" + n) for n, s, r in zip(names, outs[:nw], outs[nw:])]
        return run(exchange_start, "pair_share_start_" + gname, halves, nw, plan_pair_share)

    def rs_finish(gname, names, share):
        fulls = run(exchange_wait, "pair_share_wait_" + gname, share, plan_pair_share)
        for n, g2 in zip(names, fulls):
            g = g2.reshape(big[n].shape)
            dl, nm, nv = run(adamw, big[n], g, moments[n][0][0], moments[n][1][0], "adamw_" + n)
            grads[n], deltas[n], new_m[n], new_v[n] = g[None], dl[None], nm[None], nv[None]

    ici = {gname: gather_ici_start(gname) for gname in gather_groups}
    got1 = run(allgather_small, _pack_rows([c[0], conv_w[0]]), "allgather_cond")
    c_rows = _part_rows((d,))
    c_all = got1[:, :c_rows].reshape(N_DEV, -1)[:, :d]
    w_rows = _part_rows((CONV_WIDTH, ch_loc))
    conv_w_full = got1[0::2, c_rows:c_rows + w_rows].reshape(N_CHIPS, -1)[:, :CONV_WIDTH * ch_loc]
    conv_w_full = jnp.transpose(conv_w_full.reshape(N_CHIPS, CONV_WIDTH, ch_loc), (1, 0, 2)).reshape(CONV_WIDTH, ch)
    conv_w_pad = jnp.pad(conv_w_full, ((0, 1), (0, 0)))
    c_t = jnp.transpose(c_all)
    mod_cols = run(ada_fwd, c_t, w_ada[0])
    got2 = run(allgather_small, mod_cols.reshape(-1, LANES), "allgather_mod")
    mod_all = got2.reshape(N_CHIPS, 2, N_DEV, nc_ada)[:, 0]
    mod = lax.dynamic_slice_in_dim(mod_all, me, 1, axis=1).reshape(1, N_CHIPS * nc_ada) + b_ada
    mod = jnp.pad(mod.reshape(N_MOD, d), ((0, SUBLANES - N_MOD), (0, 0)))

    h = run(pre_mix_fwd, xe, mod, norm_mix_g)
    bucket = _t5_bucket_table()
    bucket_p, bucket_c = jnp.asarray(bucket[:, :BLOCK]), jnp.asarray(bucket[:, BLOCK:])
    bias_p, bias_c = run(bias_table, rel_bias, bucket_p, bucket_c)
    d2d_in = gather_pass_on("in", ici["in"])
    (wg_in,) = gathered("in", d2d_in)
    p = run(mm_nn, h, wg_in, tn=_tile(wg_in.shape[2], 640), tk=d, out_dtype=F32, name="mm_in")
    d2d_mid = gather_pass_on("mid", ici["mid"])

    def heads(a, n):
        return jnp.transpose(a.reshape(t, n, hd), (1, 0, 2))

    def unheads(a):
        return jnp.transpose(a, (1, 0, 2)).reshape(t, -1)

    qh, kh, vh = heads(p[:, :aw], nq), heads(p[:, off_k:off_v], nkv), heads(p[:, off_v:off_ca], nkv)
    sinks3 = attn_sinks.reshape(nq, 1, 1)
    attn_o = unheads(run(attn_fwd, qh, kh, vh, bias_p, bias_c, sinks3, q_norm_g, k_norm_g))
    ca, cb = p[:, off_ca:off_cb], p[:, off_cb:off_ga]
    s_conv = run(conv_fwd, ca, cb, conv_w_pad, conv_b, conv_ln_g, conv_ln_b)
    wg_attn_out, wg_conv_out, wg_mix_out = gathered("mid", d2d_mid)
    wg_mix_out = wg_mix_out.reshape(1, d, d)
    y_attn = run(mm_nn, attn_o, wg_attn_out, tn=_tile(wg_attn_out.shape[2], 512), tk=aw, out_dtype=F32,
                 name="mm_attn_out")
    y_conv = run(mm_nn, s_conv, wg_conv_out, tn=_tile(wg_conv_out.shape[2], 512), tk=ch, out_dtype=F32,
                 name="mm_conv_out")
    merged = run(merge_fwd, p, y_attn, y_conv, off_ga, off_gc)
    d2d_ffn_in = gather_pass_on("ffn_in", ici["ffn_in"])
    o_m = run(mm_nn, merged, wg_mix_out, tn=_tile(d, 512), tk=d, out_dtype=F32, name="mm_mix_out")
    x1, h2 = run(pre_ffn_fwd, xe, o_m, mod, norm_ffn_g)
    (wg_ffn_in,) = gathered("ffn_in", d2d_ffn_in)
    f = run(mm_nn, h2, wg_ffn_in, tn=_tile(nj_ffn, 1408), tk=d, out_dtype=BF16, name="mm_ffn_in", perm=perm_ffn)
    d2d_ffn_out = gather_pass_on("ffn_out", ici["ffn_out"])
    act = run(swiglu_fwd, f, nj_ffn)
    (wg_ffn_out,) = gathered("ffn_out", d2d_ffn_out)
    wg_ffn_out = wg_ffn_out.reshape(1, dff, d)
    o_f = run(mm_nn, act, wg_ffn_out, tn=_tile(d, 1024), tk=_tile(dff, 512), out_dtype=F32, name="mm_ffn_out")
    loss11, dy, dof, acc_l = run(loss_head, x1, o_f, tgt, mod)

    gw_ffn_out = run(mm_tn, act, dof, 1, tk=_tile(dff, 512), tn=_tile(d, 1024), name="mm_ffn_out_dw")
    px_ffn_out = rs_pair_start("ffn_out", ["w_ffn_out"], [gw_ffn_out])
    dact = run(mm_nt, dof, wg_ffn_out, tko=_tile(dff, 512), tn=d, name="mm_ffn_out_dx")
    cx_ffn_out = rs_chip_start("ffn_out", ["w_ffn_out"], px_ffn_out)
    df = run(swiglu_bwd, f, dact, nj_ffn)
    gw_ffn_in = run(mm_tn, h2, df, N_CHIPS, tk=_tile(d, 512), tn=_tile(nj_ffn, 1408), name="mm_ffn_in_dw",
                    perm=perm_ffn)
    px_ffn_in = rs_pair_start("ffn_in", ["w_ffn_in"], [gw_ffn_in])
    dh2 = run(mm_nt, df, wg_ffn_in, tko=_tile(d, 1024), tn=_tile(nj_ffn, 1408), name="mm_ffn_in_dx", perm=perm_ffn)
    sh_ffn_out = rs_share_start("ffn_out", ["w_ffn_out"], cx_ffn_out)
    cx_ffn_in = rs_chip_start("ffn_in", ["w_ffn_in"], px_ffn_in)
    dx1, dom, acc_f = run(pre_ffn_bwd, x1, dh2, dy, o_m, mod, norm_ffn_g)
    gw_mix_out = run(mm_tn, merged, dom, 1, tk=_tile(d, 512), tn=_tile(d, 1024), name="mm_mix_out_dw")
    px_mix = rs_pair_start("mix_out", ["w_mix_out"], [gw_mix_out])
    dmerged = run(mm_nt, dom, wg_mix_out, tko=_tile(d, 512), tn=d, name="mm_mix_out_dx")
    dy_attn, dy_conv, dga, dgc = run(merge_bwd, p, y_attn, y_conv, dmerged, off_ga, off_gc)
    rs_finish("ffn_out", ["w_ffn_out"], sh_ffn_out)
    cx_mix = rs_chip_start("mix_out", ["w_mix_out"], px_mix)
    gw_attn_out = run(mm_tn, attn_o, dy_attn, N_CHIPS, tk=_tile(aw, 512), tn=_tile(wg_attn_out.shape[2], 512),
                      name="mm_attn_out_dw")
    gw_conv_out = run(mm_tn, s_conv, dy_conv, N_CHIPS, tk=_tile(ch, 512), tn=_tile(wg_conv_out.shape[2], 512),
                      name="mm_conv_out_dw")
    ac_names = ["w_attn_out", "w_conv_out"]
    px_ac = rs_pair_start("attn_conv_out", ac_names, [gw_attn_out, gw_conv_out])
    dattn_o = run(mm_nt, dy_attn, wg_attn_out, tko=_tile(aw, 1024), tn=_tile(wg_attn_out.shape[2], 512),
                  name="mm_attn_out_dx")
    ds_conv = run(mm_nt, dy_conv, wg_conv_out, tko=_tile(ch, 1024), tn=_tile(wg_conv_out.shape[2], 512),
                  name="mm_conv_out_dx")
    cx_ac = rs_chip_start("attn_conv_out", ac_names, px_ac)
    dca, dcb, dconv_w, dconv_vec = run(conv_bwd, ca, cb, ds_conv, conv_w_pad, conv_b, conv_ln_g, conv_ln_b)
    sh_ffn_in = rs_share_start("ffn_in", ["w_ffn_in"], cx_ffn_in)
    dqh, dkh, dvh, dbp, dbc, dsinks, dqg, dkg = run(attn_bwd, qh, kh, vh, bias_p, bias_c, sinks3, q_norm_g, k_norm_g,
                                                     heads(dattn_o, nq))
    sh_mix = rs_share_start("mix_out", ["w_mix_out"], cx_mix)
    sh_ac = rs_share_start("attn_conv_out", ac_names, cx_ac)
    drel = run(bias_table_bwd, dbp, dbc, bucket_p, bucket_c).reshape(NUM_BUCKETS, nq)
    dp = jnp.concatenate([unheads(dqh).astype(BF16), unheads(dkh).astype(BF16), unheads(dvh).astype(BF16),
                          dca, dcb, dga, dgc], axis=1)
    gw_in = run(mm_tn, h, dp, N_CHIPS, tk=_tile(d, 512), tn=_tile(wg_in.shape[2], 640), name="mm_in_dw")
    px_in = rs_pair_start("in", ["w_in"], [gw_in])
    dh = run(mm_nt, dp, wg_in, tko=_tile(d, 1024), tn=_tile(wg_in.shape[2], 640), name="mm_in_dx")
    cx_in = rs_chip_start("in", ["w_in"], px_in)
    grad_x, acc_m = run(pre_mix_bwd, xe, dh, dx1, mod, norm_mix_g)

    dmod = jnp.concatenate([acc_m[0:1], acc_m[1:2], acc_f[3:4], acc_f[0:1], acc_f[1:2], acc_l[0:1]], axis=1)
    small_names = ["b_ada", "norm_mix_g", "q_norm_g", "k_norm_g", "attn_sinks", "rel_bias", "conv_b", "conv_ln_g",
                   "conv_ln_b", "norm_ffn_g"]
    small_w = [b_ada, norm_mix_g, q_norm_g, k_norm_g, attn_sinks, rel_bias, conv_b, conv_ln_g, conv_ln_b, norm_ffn_g]
    small_m = [m_b_ada, m_norm_mix_g, m_q_norm_g, m_k_norm_g, m_attn_sinks, m_rel_bias, m_conv_b, m_conv_ln_g,
               m_conv_ln_b, m_norm_ffn_g]
    small_v = [v_b_ada, v_norm_mix_g, v_q_norm_g, v_k_norm_g, v_attn_sinks, v_rel_bias, v_conv_b, v_conv_ln_g,
               v_conv_ln_b, v_norm_ffn_g]
    small_g = [dmod, acc_m[2:3], dqg, dkg, dsinks.reshape(1, nq), drel, dconv_vec[0:1], dconv_vec[1:2],
               dconv_vec[2:3], acc_f[2:3]]
    small_shapes = [w.shape for w in small_w]
    conv_shape = (CONV_WIDTH, ch)
    got3 = run(allgather_small, _pack_rows(small_g + [dconv_w[:CONV_WIDTH]]), "allgather_small_grads")
    zero_conv = jnp.zeros(conv_shape, F32)
    g_small, d_small, nm_small, nv_small = run(
        small_sum_adamw, got3, _pack_rows(small_w + [zero_conv]), _pack_rows(small_m + [zero_conv]),
        _pack_rows(small_v + [zero_conv]))
    g_parts = _unpack_rows(g_small, small_shapes + [conv_shape])
    grads.update(zip(small_names, g_parts[:-1]))
    deltas.update(zip(small_names, _unpack_rows(d_small, small_shapes)))
    new_m.update(zip(small_names, _unpack_rows(nm_small, small_shapes)))
    new_v.update(zip(small_names, _unpack_rows(nv_small, small_shapes)))

    g_conv_w = lax.dynamic_slice_in_dim(g_parts[-1], chip * ch_loc, ch_loc, axis=1)
    grads["conv_w"] = g_conv_w[None]
    dl, nm, nv = run(adamw, conv_w[0], g_conv_w, m_conv_w[0], v_conv_w[0], "adamw_conv_w")
    deltas["conv_w"], new_m["conv_w"], new_v["conv_w"] = dl[None], nm[None], nv[None]

    dmod_all = got3[:, :_part_rows((N_MOD * d,))].reshape(N_DEV, -1)[:, :N_MOD * d]
    dmod_cols = lax.dynamic_slice_in_dim(dmod_all, chip * nc_ada, nc_ada, axis=1)
    g_ada, dl, nm, nv = run(ada_grad_adamw, c_t, dmod_cols, w_ada[0], m_w_ada[0], v_w_ada[0])
    grads["w_ada"], deltas["w_ada"], new_m["w_ada"], new_v["w_ada"] = g_ada[None], dl[None], nm[None], nv[None]

    rs_finish("ffn_in", ["w_ffn_in"], sh_ffn_in)
    rs_finish("mix_out", ["w_mix_out"], sh_mix)
    rs_finish("attn_conv_out", ac_names, sh_ac)
    sh_in = rs_share_start("in", ["w_in"], cx_in)
    rs_finish("in", ["w_in"], sh_in)

    loss = lax.psum(loss11[0, 0], ("x", "y", "c"))
    order = ["w_ada", "b_ada", "norm_mix_g", "w_in", "q_norm_g", "k_norm_g", "attn_sinks", "rel_bias", "w_attn_out",
             "conv_w", "conv_b", "conv_ln_g", "conv_ln_b", "w_conv_out", "w_mix_out", "norm_ffn_g", "w_ffn_in",
             "w_ffn_out"]
    return (loss, grad_x[None], *[grads[n] for n in order], *[deltas[n] for n in order],
            *[new_m[n] for n in order], *[new_v[n] for n in order])
```

```python
import functools
import math
from typing import Any, NamedTuple

import jax
import jax.numpy as jnp
import numpy as np
from jax import lax
from jax.experimental import pallas as pl
from jax.experimental.pallas import tpu as pltpu

F32 = jnp.float32
BF16 = jnp.bfloat16
MESH = pl.DeviceIdType.MESH

V7X_VMEM_BYTES = 64 * 1024 * 1024
VMEM_LIMIT = V7X_VMEM_BYTES - 8 * 1024 * 1024
LANES = 128
SUBLANES = 8
BF16_SUBLANES = 16

EPS = 1e-6
WINDOW = 128
BLOCK = 128
NUM_BUCKETS = 32
MAX_EXACT = NUM_BUCKETS // 2
MAX_DISTANCE = 128
CONV_WIDTH = 31
CONV_HALO = 32
ADAM_LR = 0.001
ADAM_B1 = 0.9
ADAM_B2 = 0.999
ADAM_EPS = 1e-08
ADAM_WD = 0.01
ADAM_STEP = 10
N_MOD = 6
SH_M, SC_M, GT_M, SH_F, SC_F, GT_F = range(6)

N_CHIPS = 4
N_DEV = 8

_ANY = pl.BlockSpec(memory_space=pl.ANY)
_VMEM = pl.BlockSpec(memory_space=pltpu.VMEM)
_SMEM = pl.BlockSpec(memory_space=pltpu.SMEM)
_HBM = pl.BlockSpec(memory_space=pltpu.HBM)
_SEM = pl.BlockSpec(memory_space=pltpu.SEMAPHORE)
_EFFECT = pltpu.SideEffectType.DATAFLOW_SIDE_EFFECTING


class InOrder:
    def __init__(self):
        self.token = None

    def __call__(self, fn, *args, **kw):
        return fn(*args, dep=self, **kw)


def _pallas(body, args, *, in_specs, out_specs, out_shape, name, dep=None, grid=(), n_prefetch=0, scratch=(),
            sem=None, **kw):
    n_lead = n_prefetch + len(in_specs)
    in_specs, args = list(in_specs), list(args)
    single = not isinstance(out_shape, (list, tuple))
    out_shapes = [out_shape] if single else list(out_shape)
    out_specs = [out_specs] if single else list(out_specs)
    if dep is not None:
        inner, n_out, takes = body, len(out_shapes), dep.token is not None

        def body(*refs):
            rest = refs[n_lead + (1 if takes else 0):]
            rest[n_out][...] = jnp.zeros((SUBLANES, LANES), F32)
            return inner(*refs[:n_lead], *rest[:n_out], *rest[n_out + 1:])

        if takes:
            in_specs.append(_ANY)
            args.append(dep.token)
        out_shapes.append(jax.ShapeDtypeStruct((SUBLANES, LANES), F32))
        out_specs.append(pl.BlockSpec((SUBLANES, LANES), lambda *_: (0, 0)))
    params = kw.pop("compiler_params", None)
    if params is None:
        params = pltpu.CompilerParams(dimension_semantics=sem, vmem_limit_bytes=VMEM_LIMIT)
    outs = pl.pallas_call(
        body,
        grid_spec=pltpu.PrefetchScalarGridSpec(num_scalar_prefetch=n_prefetch, grid=grid, in_specs=in_specs,
                                               out_specs=out_specs, scratch_shapes=list(scratch)),
        out_shape=out_shapes, compiler_params=params, name=name, **kw,
    )(*args)
    if dep is not None:
        dep.token = outs[-1]
        outs = outs[:-1]
    return outs[0] if single else list(outs)


def _tile(n, pref, unit=LANES):
    best = None
    for t in range(unit, min(n, pref) + 1, unit):
        if n % t == 0:
            best = t
    return best if best is not None else n


def _sigmoid(v):
    return 1.0 / (1.0 + jnp.exp(-v))


ROW_CHUNK = 512


def _row_chunks(m):
    step = ROW_CHUNK if m % ROW_CHUNK == 0 else m
    return [(s, step) for s in range(0, m, step)]


def _block_pos(j, perm):
    if perm is None:
        return j
    pos = 0
    for a, p in enumerate(perm):
        pos = pos + jnp.where(j == a, p, 0)
    return pos


def mm_nn(a, w, *, tn, tk, out_dtype, name, perm=None, dep=None):
    m, k = a.shape
    j, k2, nj = w.shape
    assert k == k2 and nj % tn == 0 and k % tk == 0
    npj, nk = nj // tn, k // tk

    def body(a_ref, w_ref, o_ref, *scratch):
        kk = pl.program_id(1)
        for s, sz in _row_chunks(m):
            rows = pl.ds(s, sz)
            p = jnp.dot(a_ref[rows, :], w_ref[...], preferred_element_type=F32)
            if nk == 1:
                o_ref[rows, :] = p.astype(out_dtype)
            else:
                acc = scratch[0]

                @pl.when(kk == 0)
                def _():
                    acc[rows, :] = p

                @pl.when(kk > 0)
                def _():
                    acc[rows, :] += p

                @pl.when(kk == nk - 1)
                def _():
                    o_ref[rows, :] = acc[rows, :].astype(out_dtype)

    return _pallas(
        body, [a, w], dep=dep, grid=(j * npj, nk),
        in_specs=[
            pl.BlockSpec((m, tk), lambda n, kk: (0, kk)),
            pl.BlockSpec((None, tk, tn), lambda n, kk: (n // npj, kk, n % npj)),
        ],
        out_specs=pl.BlockSpec((m, tn), lambda n, kk: (0, _block_pos(n // npj, perm) * npj + n % npj)),
        out_shape=jax.ShapeDtypeStruct((m, j * nj), out_dtype),
        scratch=[pltpu.VMEM((m, tn), F32)] if nk > 1 else [],
        sem=("parallel", "arbitrary"), name=name)


def mm_nt(g, w, *, tko, tn, name, perm=None, dep=None):
    m, n = g.shape
    j, k, nj = w.shape
    assert n == j * nj and nj % tn == 0 and k % tko == 0
    npj, nr = nj // tn, n // tn

    def body(g_ref, w_ref, o_ref):
        r = pl.program_id(1)
        for s, sz in _row_chunks(m):
            rows = pl.ds(s, sz)
            p = lax.dot_general(g_ref[rows, :], w_ref[...], (((1,), (1,)), ((), ())), preferred_element_type=F32)

            @pl.when(r == 0)
            def _():
                o_ref[rows, :] = p

            @pl.when(r > 0)
            def _():
                o_ref[rows, :] += p

    return _pallas(
        body, [g, w], dep=dep, grid=(k // tko, nr),
        in_specs=[
            pl.BlockSpec((m, tn), lambda ko, r: (0, _block_pos(r // npj, perm) * npj + r % npj)),
            pl.BlockSpec((None, tko, tn), lambda ko, r: (r // npj, ko, r % npj)),
        ],
        out_specs=pl.BlockSpec((m, tko), lambda ko, r: (0, ko)),
        out_shape=jax.ShapeDtypeStruct((m, k), F32),
        sem=("parallel", "arbitrary"), name=name)


def mm_tn(a, g, n_blocks, *, tk, tn, name, perm=None, dep=None):
    m, k = a.shape
    m2, n = g.shape
    nj = n // n_blocks
    assert m == m2 and nj % tn == 0 and k % tk == 0
    npj = nj // tn

    def body(a_ref, g_ref, o_ref):
        p = lax.dot_general(a_ref[...], g_ref[...], (((0,), (0,)), ((), ())), preferred_element_type=F32)
        o_ref[...] = p.astype(BF16)

    return _pallas(
        body, [a, g], dep=dep, grid=(k // tk, n // tn),
        in_specs=[
            pl.BlockSpec((m, tk), lambda kk, nn: (0, kk)),
            pl.BlockSpec((m, tn), lambda kk, nn: (0, _block_pos(nn // npj, perm) * npj + nn % npj)),
        ],
        out_specs=pl.BlockSpec((None, tk, tn), lambda kk, nn: (nn // npj, kk, nn % npj)),
        out_shape=jax.ShapeDtypeStruct((n_blocks, k, nj), BF16),
        sem=("parallel", "parallel"), name=name)


ROW_TILE = 256


def _row_spec(tr, width):
    return pl.BlockSpec((tr, width), lambda i: (i, 0))


def _full_spec(shape):
    return pl.BlockSpec(shape, lambda *_: (0,) * len(shape))


def _rms(xv):
    return lax.rsqrt(jnp.mean(xv * xv, axis=-1, keepdims=True) + EPS)


def _mod_row(mod_ref, row):
    return mod_ref[pl.ds(row, 1), :]


def pre_mix_fwd(x, mod, gain, dep=None):
    t, d = x.shape
    tr = _tile(t, ROW_TILE, SUBLANES)

    def body(x_ref, mod_ref, g_ref, h_ref):
        xv = x_ref[...]
        y = xv * _rms(xv) * g_ref[...]
        h_ref[...] = (y * (1.0 + _mod_row(mod_ref, SC_M)) + _mod_row(mod_ref, SH_M)).astype(BF16)

    return _pallas(
        body, [x, mod, gain], dep=dep, grid=(t // tr,),
        in_specs=[_row_spec(tr, d), _full_spec(mod.shape), _full_spec(gain.shape)],
        out_specs=_row_spec(tr, d),
        out_shape=jax.ShapeDtypeStruct((t, d), BF16),
        sem=("parallel",), name="pre_mix_fwd")


def pre_ffn_fwd(x, o_m, mod, gain, dep=None):
    t, d = x.shape
    tr = _tile(t, ROW_TILE, SUBLANES)

    def body(x_ref, om_ref, mod_ref, g_ref, x1_ref, h_ref):
        x1 = x_ref[...] + _mod_row(mod_ref, GT_M) * om_ref[...]
        x1_ref[...] = x1
        y = x1 * _rms(x1) * g_ref[...]
        h_ref[...] = (y * (1.0 + _mod_row(mod_ref, SC_F)) + _mod_row(mod_ref, SH_F)).astype(BF16)

    return _pallas(
        body, [x, o_m, mod, gain], dep=dep, grid=(t // tr,),
        in_specs=[_row_spec(tr, d), _row_spec(tr, d), _full_spec(mod.shape), _full_spec(gain.shape)],
        out_specs=[_row_spec(tr, d), _row_spec(tr, d)],
        out_shape=[jax.ShapeDtypeStruct((t, d), F32), jax.ShapeDtypeStruct((t, d), BF16)],
        sem=("parallel",), name="pre_ffn_fwd")


def loss_head(x1, o_f, target, mod, dep=None):
    t, d = x1.shape
    tr = _tile(t, ROW_TILE, SUBLANES)

    def body(x1_ref, of_ref, tg_ref, mod_ref, loss_ref, dy_ref, dof_ref, acc_ref):
        i = pl.program_id(0)
        gt = _mod_row(mod_ref, GT_F)
        of = of_ref[...]
        err = x1_ref[...] + gt * of - tg_ref[...]
        dy = err * (1.0 / d)
        dy_ref[...] = dy
        dof_ref[...] = (dy * gt).astype(BF16)
        part = (0.5 / d) * jnp.sum(jnp.sum(err * err, axis=1, keepdims=True), axis=0, keepdims=True)
        dgt = jnp.sum(dy * of, axis=0, keepdims=True)

        @pl.when(i == 0)
        def _():
            loss_ref[...] = jnp.zeros_like(loss_ref)
            acc_ref[...] = jnp.zeros_like(acc_ref)

        loss_ref[...] += part
        acc_ref[pl.ds(0, 1), :] += dgt

    return _pallas(
        body, [x1, o_f, target, mod], dep=dep, grid=(t // tr,),
        in_specs=[_row_spec(tr, d), _row_spec(tr, d), _row_spec(tr, d), _full_spec(mod.shape)],
        out_specs=[_full_spec((1, 1)), _row_spec(tr, d), _row_spec(tr, d), _full_spec((SUBLANES, d))],
        out_shape=[jax.ShapeDtypeStruct((1, 1), F32), jax.ShapeDtypeStruct((t, d), F32),
                   jax.ShapeDtypeStruct((t, d), BF16), jax.ShapeDtypeStruct((SUBLANES, d), F32)],
        sem=("arbitrary",), name="loss_head")


def _norm_bwd(xv, dh, sc, gain):
    rstd = _rms(xv)
    yn = xv * rstd
    dsh = jnp.sum(dh, axis=0, keepdims=True)
    dsc = jnp.sum(dh * (yn * gain), axis=0, keepdims=True)
    dgain = jnp.sum(dh * (1.0 + sc) * yn, axis=0, keepdims=True)
    dyn = dh * ((1.0 + sc) * gain)
    dx = rstd * (dyn - yn * jnp.mean(dyn * yn, axis=-1, keepdims=True))
    return dx, dsh, dsc, dgain


def pre_ffn_bwd(x1, dh2, dy, o_m, mod, gain, dep=None):
    t, d = x1.shape
    tr = _tile(t, ROW_TILE, SUBLANES)

    def body(x1_ref, dh_ref, dy_ref, om_ref, mod_ref, g_ref, dx1_ref, dom_ref, acc_ref):
        i = pl.program_id(0)
        dxn, dsh, dsc, dgain = _norm_bwd(x1_ref[...], dh_ref[...], _mod_row(mod_ref, SC_F), g_ref[...])
        dx1 = dy_ref[...] + dxn
        dx1_ref[...] = dx1
        dom_ref[...] = (dx1 * _mod_row(mod_ref, GT_M)).astype(BF16)
        dgt = jnp.sum(dx1 * om_ref[...], axis=0, keepdims=True)

        @pl.when(i == 0)
        def _():
            acc_ref[...] = jnp.zeros_like(acc_ref)

        acc_ref[pl.ds(0, 1), :] += dsh
        acc_ref[pl.ds(1, 1), :] += dsc
        acc_ref[pl.ds(2, 1), :] += dgain
        acc_ref[pl.ds(3, 1), :] += dgt

    return _pallas(
        body, [x1, dh2, dy, o_m, mod, gain], dep=dep, grid=(t // tr,),
        in_specs=[_row_spec(tr, d)] * 4 + [_full_spec(mod.shape), _full_spec(gain.shape)],
        out_specs=[_row_spec(tr, d), _row_spec(tr, d), _full_spec((SUBLANES, d))],
        out_shape=[jax.ShapeDtypeStruct((t, d), F32), jax.ShapeDtypeStruct((t, d), BF16),
                   jax.ShapeDtypeStruct((SUBLANES, d), F32)],
        sem=("arbitrary",), name="pre_ffn_bwd")


def pre_mix_bwd(x, dh, dx1, mod, gain, dep=None):
    t, d = x.shape
    tr = _tile(t, ROW_TILE, SUBLANES)

    def body(x_ref, dh_ref, dx1_ref, mod_ref, g_ref, gx_ref, acc_ref):
        i = pl.program_id(0)
        dxn, dsh, dsc, dgain = _norm_bwd(x_ref[...], dh_ref[...], _mod_row(mod_ref, SC_M), g_ref[...])
        gx_ref[...] = dx1_ref[...] + dxn

        @pl.when(i == 0)
        def _():
            acc_ref[...] = jnp.zeros_like(acc_ref)

        acc_ref[pl.ds(0, 1), :] += dsh
        acc_ref[pl.ds(1, 1), :] += dsc
        acc_ref[pl.ds(2, 1), :] += dgain

    return _pallas(
        body, [x, dh, dx1, mod, gain], dep=dep, grid=(t // tr,),
        in_specs=[_row_spec(tr, d)] * 3 + [_full_spec(mod.shape), _full_spec(gain.shape)],
        out_specs=[_row_spec(tr, d), _full_spec((SUBLANES, d))],
        out_shape=[jax.ShapeDtypeStruct((t, d), F32), jax.ShapeDtypeStruct((SUBLANES, d), F32)],
        sem=("arbitrary",), name="pre_mix_bwd")


def merge_fwd(p, y_attn, y_conv, off_ga, off_gc, dep=None):
    t, d = y_attn.shape
    tr = _tile(t, ROW_TILE, SUBLANES)
    cw = math.gcd(math.gcd(off_ga, off_gc), math.gcd(d, 512))
    nc = d // cw

    def body(ga_ref, gc_ref, ya_ref, yc_ref, o_ref):
        o_ref[...] = (_sigmoid(ga_ref[...]) * ya_ref[...] + _sigmoid(gc_ref[...]) * yc_ref[...]).astype(BF16)

    return _pallas(
        body, [p, p, y_attn, y_conv], dep=dep, grid=(t // tr, nc),
        in_specs=[pl.BlockSpec((tr, cw), lambda i, j: (i, off_ga // cw + j)),
                  pl.BlockSpec((tr, cw), lambda i, j: (i, off_gc // cw + j)),
                  pl.BlockSpec((tr, cw), lambda i, j: (i, j)),
                  pl.BlockSpec((tr, cw), lambda i, j: (i, j))],
        out_specs=pl.BlockSpec((tr, cw), lambda i, j: (i, j)),
        out_shape=jax.ShapeDtypeStruct((t, d), BF16),
        sem=("parallel", "parallel"), name="merge_fwd")


def merge_bwd(p, y_attn, y_conv, dmerged, off_ga, off_gc, dep=None):
    t, d = y_attn.shape
    tr = _tile(t, ROW_TILE, SUBLANES)
    cw = math.gcd(math.gcd(off_ga, off_gc), math.gcd(d, 512))
    nc = d // cw

    def body(ga_ref, gc_ref, ya_ref, yc_ref, dm_ref, dya_ref, dyc_ref, dga_ref, dgc_ref):
        dm = dm_ref[...]
        sa = _sigmoid(ga_ref[...])
        sc = _sigmoid(gc_ref[...])
        dya_ref[...] = (dm * sa).astype(BF16)
        dyc_ref[...] = (dm * sc).astype(BF16)
        dga_ref[...] = (dm * ya_ref[...] * sa * (1.0 - sa)).astype(BF16)
        dgc_ref[...] = (dm * yc_ref[...] * sc * (1.0 - sc)).astype(BF16)

    blk = pl.BlockSpec((tr, cw), lambda i, j: (i, j))
    return _pallas(
        body, [p, p, y_attn, y_conv, dmerged], dep=dep, grid=(t // tr, nc),
        in_specs=[pl.BlockSpec((tr, cw), lambda i, j: (i, off_ga // cw + j)),
                  pl.BlockSpec((tr, cw), lambda i, j: (i, off_gc // cw + j)), blk, blk, blk],
        out_specs=[blk] * 4,
        out_shape=[jax.ShapeDtypeStruct((t, d), BF16)] * 4,
        sem=("parallel", "parallel"), name="merge_bwd")


def ffn_perm(n_blocks):
    half = n_blocks // 2
    return tuple(2 * j if j < half else 2 * (j - half) + 1 for j in range(n_blocks))


def swiglu_fwd(f, nj, dep=None):
    t, two = f.shape
    tr = _tile(t, ROW_TILE, SUBLANES)
    npair = two // (2 * nj)

    def body(f_ref, o_ref):
        g = f_ref[:, :nj].astype(F32)
        u = f_ref[:, nj:].astype(F32)
        o_ref[...] = (g * _sigmoid(g) * u).astype(BF16)

    return _pallas(
        body, [f], dep=dep, grid=(t // tr, npair),
        in_specs=[pl.BlockSpec((tr, 2 * nj), lambda i, j: (i, j))],
        out_specs=pl.BlockSpec((tr, nj), lambda i, j: (i, j)),
        out_shape=jax.ShapeDtypeStruct((t, two // 2), BF16),
        sem=("parallel", "parallel"), name="swiglu_fwd")


def swiglu_bwd(f, dact, nj, dep=None):
    t, two = f.shape
    tr = _tile(t, ROW_TILE, SUBLANES)
    npair = two // (2 * nj)

    def body(f_ref, da_ref, o_ref):
        g = f_ref[:, :nj].astype(F32)
        u = f_ref[:, nj:].astype(F32)
        da = da_ref[...]
        s = _sigmoid(g)
        o_ref[:, :nj] = (da * u * (s * (1.0 + g * (1.0 - s)))).astype(BF16)
        o_ref[:, nj:] = (da * (g * s)).astype(BF16)

    return _pallas(
        body, [f, dact], dep=dep, grid=(t // tr, npair),
        in_specs=[pl.BlockSpec((tr, 2 * nj), lambda i, j: (i, j)), pl.BlockSpec((tr, nj), lambda i, j: (i, j))],
        out_specs=pl.BlockSpec((tr, 2 * nj), lambda i, j: (i, j)),
        out_shape=jax.ShapeDtypeStruct((t, two), BF16),
        sem=("parallel", "parallel"), name="swiglu_bwd")


def _t5_bucket_table():
    q_off = np.arange(BLOCK)
    k_off = np.arange(2 * BLOCK)
    dist = q_off[:, None] + BLOCK - k_off[None, :]
    n = np.maximum(dist, 0)
    nf = np.maximum(n, 1).astype(np.float32)
    large = MAX_EXACT + (np.log(nf / np.float32(MAX_EXACT)) / np.float32(math.log(MAX_DISTANCE / MAX_EXACT))
                         * np.float32(NUM_BUCKETS - MAX_EXACT)).astype(np.int32)
    large = np.minimum(large, NUM_BUCKETS - 1)
    bucket = np.where(n < MAX_EXACT, n, large).astype(np.int32)
    allowed = (dist >= 0) & (dist < WINDOW)
    return np.where(allowed, bucket, -1).astype(np.int32)


def bias_table(rel_bias, bucket_p, bucket_c, dep=None):
    nb, nq = rel_bias.shape

    def body(rb_ref, bkp_ref, bkc_ref, op_ref, oc_ref):
        for bk_ref, o_ref in ((bkp_ref, op_ref), (bkc_ref, oc_ref)):
            bk = bk_ref[...]
            for h in range(nq):
                acc = jnp.full(bk.shape, -jnp.inf, F32)
                for b in range(nb):
                    acc = jnp.where(bk == b, rb_ref[b, h], acc)
                o_ref[h] = acc

    return _pallas(
        body, [rel_bias, bucket_p, bucket_c], dep=dep,
        in_specs=[_SMEM, _VMEM, _VMEM], out_specs=[_VMEM, _VMEM],
        out_shape=[jax.ShapeDtypeStruct((nq,) + bucket_p.shape, F32)] * 2,
        name="bias_table")


def bias_table_bwd(dbp, dbc, bucket_p, bucket_c, dep=None):
    nq = dbp.shape[0]

    def body(dbp_ref, dbc_ref, bkp_ref, bkc_ref, o_ref):
        bkp, bkc = bkp_ref[...][None], bkc_ref[...][None]
        dp, dc = dbp_ref[...], dbc_ref[...]
        for b in range(NUM_BUCKETS):
            sel = jnp.where(bkp == b, dp, 0.0) + jnp.where(bkc == b, dc, 0.0)
            o_ref[b] = jnp.sum(jnp.sum(sel, axis=2, keepdims=True), axis=1, keepdims=True)

    return _pallas(
        body, [dbp, dbc, bucket_p, bucket_c], dep=dep,
        in_specs=[_VMEM] * 4, out_specs=_VMEM,
        out_shape=jax.ShapeDtypeStruct((NUM_BUCKETS, nq, 1, 1), F32),
        name="bias_table_bwd")


_NT = (((1,), (1,)), ((), ()))
_TN = (((0,), (0,)), ((), ()))


@jax.custom_vjp
def _bdot_nt(a, b):
    return lax.dot_general(a.astype(BF16), b.astype(BF16), _NT, preferred_element_type=F32)


def _bdot_nt_fwd(a, b):
    return _bdot_nt(a, b), (a, b)


def _bdot_nt_bwd(res, g):
    a, b = res
    gb = g.astype(BF16)
    da = jnp.dot(gb, b.astype(BF16), preferred_element_type=F32)
    db = lax.dot_general(gb, a.astype(BF16), _TN, preferred_element_type=F32)
    return da, db


_bdot_nt.defvjp(_bdot_nt_fwd, _bdot_nt_bwd)


@jax.custom_vjp
def _bdot_nn(a, b):
    return jnp.dot(a.astype(BF16), b.astype(BF16), preferred_element_type=F32)


def _bdot_nn_fwd(a, b):
    return _bdot_nn(a, b), (a, b)


def _bdot_nn_bwd(res, g):
    a, b = res
    gb = g.astype(BF16)
    da = lax.dot_general(gb, b.astype(BF16), _NT, preferred_element_type=F32)
    db = lax.dot_general(a.astype(BF16), gb, _TN, preferred_element_type=F32)
    return da, db


_bdot_nn.defvjp(_bdot_nn_fwd, _bdot_nn_bwd)


def _attn_math(q4, kp, kc, vp, vc, bp, bc, sink4, qg, kg, *, prev_ok, scale):
    g, b, hd = q4.shape
    q = q4.reshape(g * b, hd)
    qn = q * _rms(q) * qg
    kpn = kp * _rms(kp) * kg
    kcn = kc * _rms(kc) * kg
    lp = _bdot_nt(qn, kpn).reshape(g, b, b) * scale + bp
    lc = _bdot_nt(qn, kcn).reshape(g, b, b) * scale + bc
    lp = jnp.where(prev_ok, lp, -jnp.inf)
    m = jnp.maximum(jnp.maximum(jnp.max(lp, axis=-1, keepdims=True), jnp.max(lc, axis=-1, keepdims=True)), sink4)
    m = lax.stop_gradient(m)
    pp = jnp.exp(lp - m)
    pc = jnp.exp(lc - m)
    den = jnp.sum(pp, axis=-1, keepdims=True) + jnp.sum(pc, axis=-1, keepdims=True) + jnp.exp(sink4 - m)
    inv = 1.0 / den
    out = _bdot_nn((pp * inv).reshape(g * b, b), vp) + _bdot_nn((pc * inv).reshape(g * b, b), vc)
    return out.reshape(g, b, hd)


def _attn_specs(grp, hd, nblk, reverse):
    def blk(n):
        return nblk - 1 - n if reverse else n

    return [
        pl.BlockSpec((grp, BLOCK, hd), lambda h, n: (h, blk(n), 0)),
        pl.BlockSpec((None, BLOCK, hd), lambda h, n: (h, jnp.maximum(blk(n) - 1, 0), 0)),
        pl.BlockSpec((None, BLOCK, hd), lambda h, n: (h, blk(n), 0)),
        pl.BlockSpec((None, BLOCK, hd), lambda h, n: (h, jnp.maximum(blk(n) - 1, 0), 0)),
        pl.BlockSpec((None, BLOCK, hd), lambda h, n: (h, blk(n), 0)),
        pl.BlockSpec((grp, BLOCK, BLOCK), lambda h, n: (h, 0, 0)),
        pl.BlockSpec((grp, BLOCK, BLOCK), lambda h, n: (h, 0, 0)),
        pl.BlockSpec((grp, 1, 1), lambda h, n: (h, 0, 0)),
        pl.BlockSpec((1, hd), lambda h, n: (0, 0)),
        pl.BlockSpec((1, hd), lambda h, n: (0, 0)),
    ]


def attn_fwd(qh, kh, vh, bias_p, bias_c, sinks, qg, kg, dep=None):
    nq, t, hd = qh.shape
    nkv = kh.shape[0]
    grp, nblk = nq // nkv, t // BLOCK
    scale = hd ** -0.5

    def body(q_ref, kp_ref, kc_ref, vp_ref, vc_ref, bp_ref, bc_ref, s_ref, qg_ref, kg_ref, o_ref):
        prev_ok = pl.program_id(1) > 0
        out = _attn_math(q_ref[...], kp_ref[...], kc_ref[...], vp_ref[...], vc_ref[...], bp_ref[...], bc_ref[...],
                         s_ref[...], qg_ref[...], kg_ref[...], prev_ok=prev_ok, scale=scale)
        o_ref[...] = out.astype(BF16)

    return _pallas(
        body, [qh, kh, kh, vh, vh, bias_p, bias_c, sinks, qg, kg], dep=dep, grid=(nkv, nblk),
        in_specs=_attn_specs(grp, hd, nblk, False),
        out_specs=pl.BlockSpec((grp, BLOCK, hd), lambda h, n: (h, n, 0)),
        out_shape=jax.ShapeDtypeStruct((nq, t, hd), BF16),
        sem=("parallel", "parallel"), name="attn_fwd")


def attn_bwd(qh, kh, vh, bias_p, bias_c, sinks, qg, kg, doh, dep=None):
    nq, t, hd = qh.shape
    nkv = kh.shape[0]
    grp, nblk = nq // nkv, t // BLOCK
    scale = hd ** -0.5

    def body(q_ref, kp_ref, kc_ref, vp_ref, vc_ref, bp_ref, bc_ref, s_ref, qg_ref, kg_ref, do_ref,
             dq_ref, dk_ref, dv_ref, dbp_ref, dbc_ref, ds_ref, dqg_ref, dkg_ref, ck_ref, cv_ref):
        h, i = pl.program_id(0), pl.program_id(1)
        prev_ok = (nblk - 1 - i) > 0
        fn = functools.partial(_attn_math, prev_ok=prev_ok, scale=scale)
        _, vjp = jax.vjp(fn, q_ref[...], kp_ref[...], kc_ref[...], vp_ref[...], vc_ref[...], bp_ref[...], bc_ref[...],
                         s_ref[...], qg_ref[...], kg_ref[...])
        dq, dkp, dkc, dvp, dvc, dbp, dbc, dsk, dqg, dkg = vjp(do_ref[...])
        dq_ref[...] = dq

        @pl.when(i == 0)
        def _():
            ck_ref[...] = jnp.zeros_like(ck_ref)
            cv_ref[...] = jnp.zeros_like(cv_ref)
            dbp_ref[...] = jnp.zeros_like(dbp_ref)
            dbc_ref[...] = jnp.zeros_like(dbc_ref)
            ds_ref[...] = jnp.zeros_like(ds_ref)

        @pl.when((i == 0) & (h == 0))
        def _():
            dqg_ref[...] = jnp.zeros_like(dqg_ref)
            dkg_ref[...] = jnp.zeros_like(dkg_ref)

        dk_ref[...] = dkc + ck_ref[...]
        dv_ref[...] = dvc + cv_ref[...]
        ck_ref[...] = dkp
        cv_ref[...] = dvp
        dbp_ref[...] += dbp
        dbc_ref[...] += dbc
        ds_ref[...] += dsk
        dqg_ref[...] += dqg
        dkg_ref[...] += dkg

    rev = lambda h, n: (h, nblk - 1 - n, 0)
    return _pallas(
        body, [qh, kh, kh, vh, vh, bias_p, bias_c, sinks, qg, kg, doh], dep=dep, grid=(nkv, nblk),
        in_specs=_attn_specs(grp, hd, nblk, True) + [pl.BlockSpec((grp, BLOCK, hd), rev)],
        out_specs=[
            pl.BlockSpec((grp, BLOCK, hd), rev),
            pl.BlockSpec((None, BLOCK, hd), rev),
            pl.BlockSpec((None, BLOCK, hd), rev),
            pl.BlockSpec((grp, BLOCK, BLOCK), lambda h, n: (h, 0, 0)),
            pl.BlockSpec((grp, BLOCK, BLOCK), lambda h, n: (h, 0, 0)),
            pl.BlockSpec((grp, 1, 1), lambda h, n: (h, 0, 0)),
            pl.BlockSpec((1, hd), lambda h, n: (0, 0)),
            pl.BlockSpec((1, hd), lambda h, n: (0, 0)),
        ],
        out_shape=[
            jax.ShapeDtypeStruct((nq, t, hd), F32),
            jax.ShapeDtypeStruct((nkv, t, hd), F32),
            jax.ShapeDtypeStruct((nkv, t, hd), F32),
            jax.ShapeDtypeStruct((nq, BLOCK, BLOCK), F32),
            jax.ShapeDtypeStruct((nq, BLOCK, BLOCK), F32),
            jax.ShapeDtypeStruct((nq, 1, 1), F32),
            jax.ShapeDtypeStruct((1, hd), F32),
            jax.ShapeDtypeStruct((1, hd), F32),
        ],
        scratch=[pltpu.VMEM((BLOCK, hd), F32), pltpu.VMEM((BLOCK, hd), F32)],
        sem=("arbitrary", "arbitrary"), name="attn_bwd")


CONV_TILE = 256


def _conv_halo_specs(tb, ch, nblk):
    per = tb // CONV_HALO
    last = nblk * per - 1
    cur = pl.BlockSpec((tb, ch), lambda n: (n, 0))
    prev = pl.BlockSpec((CONV_HALO, ch), lambda n: (jnp.maximum(n * per - 1, 0), 0))
    nxt = pl.BlockSpec((CONV_HALO, ch), lambda n: (jnp.minimum((n + 1) * per, last), 0))
    return cur, prev, nxt


def _ln_silu(co, ln_g, ln_b):
    mu = jnp.mean(co, axis=-1, keepdims=True)
    cen = co - mu
    rstd = lax.rsqrt(jnp.mean(cen * cen, axis=-1, keepdims=True) + EPS)
    xhat = cen * rstd
    z = xhat * ln_g + ln_b
    return xhat, rstd, z


def conv_fwd(ca, cb, conv_w, conv_b, ln_g, ln_b, dep=None):
    t, ch = ca.shape
    tb = _tile(t, CONV_TILE, CONV_HALO)
    nblk = t // tb
    cur, prev, _ = _conv_halo_specs(tb, ch, nblk)
    lead = CONV_HALO - (CONV_WIDTH - 1)

    def body(ca_ref, cb_ref, cap_ref, cbp_ref, w_ref, b_ref, g_ref, bb_ref, s_ref, ubuf):
        n = pl.program_id(0)
        halo = cap_ref[...] * _sigmoid(cbp_ref[...])
        ubuf[pl.ds(0, CONV_HALO), :] = jnp.where(n > 0, halo, 0.0)
        ubuf[pl.ds(CONV_HALO, tb), :] = ca_ref[...] * _sigmoid(cb_ref[...])
        acc = jnp.broadcast_to(b_ref[...], (tb, ch))
        for k in range(CONV_WIDTH):
            acc = acc + w_ref[pl.ds(k, 1), :] * ubuf[pl.ds(lead + k, tb), :]
        _, _, z = _ln_silu(acc, g_ref[...], bb_ref[...])
        s_ref[...] = (z * _sigmoid(z)).astype(BF16)

    vec = _full_spec((1, ch))
    return _pallas(
        body, [ca, cb, ca, cb, conv_w, conv_b, ln_g, ln_b], dep=dep, grid=(nblk,),
        in_specs=[cur, cur, prev, prev, _full_spec(conv_w.shape), vec, vec, vec],
        out_specs=cur,
        out_shape=jax.ShapeDtypeStruct((t, ch), BF16),
        scratch=[pltpu.VMEM((CONV_HALO + tb, ch), F32)],
        sem=("parallel",), name="conv_fwd")


def conv_bwd(ca, cb, ds, conv_w, conv_b, ln_g, ln_b, dep=None):
    t, ch = ca.shape
    tb = _tile(t, CONV_TILE, CONV_HALO)
    nblk = t // tb
    cur, prev, nxt = _conv_halo_specs(tb, ch, nblk)
    lead = CONV_HALO - (CONV_WIDTH - 1)
    ext = tb + CONV_HALO

    def body(ca_ref, cb_ref, cap_ref, cbp_ref, can_ref, cbn_ref, ds_ref, dsn_ref, w_ref, b_ref, g_ref, bb_ref,
             dca_ref, dcb_ref, dw_ref, dvec_ref, ubuf, dbuf):
        n = pl.program_id(0)
        is_last = n == nblk - 1
        sig_b = _sigmoid(cb_ref[...])
        cav = ca_ref[...]
        ubuf[pl.ds(0, CONV_HALO), :] = jnp.where(n > 0, cap_ref[...] * _sigmoid(cbp_ref[...]), 0.0)
        ubuf[pl.ds(CONV_HALO, tb), :] = cav * sig_b
        ubuf[pl.ds(CONV_HALO + tb, CONV_HALO), :] = can_ref[...] * _sigmoid(cbn_ref[...])
        co = jnp.broadcast_to(b_ref[...], (ext, ch))
        for k in range(CONV_WIDTH):
            co = co + w_ref[pl.ds(k, 1), :] * ubuf[pl.ds(lead + k, ext), :]
        xhat, rstd, z = _ln_silu(co, g_ref[...], bb_ref[...])
        dsv = jnp.concatenate([ds_ref[...], jnp.where(is_last, 0.0, dsn_ref[...])], axis=0)
        sg = _sigmoid(z)
        dz = dsv * (sg * (1.0 + z * (1.0 - sg)))
        dxh = dz * g_ref[...]
        dco = rstd * (dxh - jnp.mean(dxh, axis=-1, keepdims=True)
                      - xhat * jnp.mean(dxh * xhat, axis=-1, keepdims=True))
        dbuf[...] = dco

        @pl.when(n == 0)
        def _():
            dw_ref[...] = jnp.zeros_like(dw_ref)
            dvec_ref[...] = jnp.zeros_like(dvec_ref)

        dco_cur = dco[:tb]
        dvec_ref[pl.ds(0, 1), :] += jnp.sum(dco_cur, axis=0, keepdims=True)
        dvec_ref[pl.ds(1, 1), :] += jnp.sum(dz[:tb] * xhat[:tb], axis=0, keepdims=True)
        dvec_ref[pl.ds(2, 1), :] += jnp.sum(dz[:tb], axis=0, keepdims=True)
        du = jnp.zeros((tb, ch), F32)
        for k in range(CONV_WIDTH):
            du = du + w_ref[pl.ds(k, 1), :] * dbuf[pl.ds(CONV_WIDTH - 1 - k, tb), :]
            dw_ref[pl.ds(k, 1), :] += jnp.sum(dco_cur * ubuf[pl.ds(lead + k, tb), :], axis=0, keepdims=True)
        dca_ref[...] = (du * sig_b).astype(BF16)
        dcb_ref[...] = (du * cav * sig_b * (1.0 - sig_b)).astype(BF16)

    vec = _full_spec((1, ch))
    return _pallas(
        body, [ca, cb, ca, cb, ca, cb, ds, ds, conv_w, conv_b, ln_g, ln_b], dep=dep, grid=(nblk,),
        in_specs=[cur, cur, prev, prev, nxt, nxt, cur, nxt, _full_spec(conv_w.shape), vec, vec, vec],
        out_specs=[cur, cur, _full_spec(conv_w.shape), _full_spec((SUBLANES, ch))],
        out_shape=[jax.ShapeDtypeStruct((t, ch), BF16), jax.ShapeDtypeStruct((t, ch), BF16),
                   jax.ShapeDtypeStruct(conv_w.shape, F32), jax.ShapeDtypeStruct((SUBLANES, ch), F32)],
        scratch=[pltpu.VMEM((2 * CONV_HALO + tb, ch), F32), pltpu.VMEM((ext, ch), F32)],
        sem=("arbitrary",), name="conv_bwd")


def ada_fwd(c_t, w_ada, dep=None):
    d, nc = w_ada.shape
    nex = c_t.shape[1]
    tn = _tile(nc, 512)

    def body(ct_ref, w_ref, o_ref):
        w = w_ref[...]
        ct = ct_ref[...]
        cact = ct * _sigmoid(ct)
        rows = [jnp.sum(w * cact[:, b:b + 1], axis=0, keepdims=True) for b in range(nex)]
        o_ref[...] = jnp.concatenate(rows, axis=0)

    return _pallas(
        body, [c_t, w_ada], dep=dep, grid=(nc // tn,),
        in_specs=[_full_spec(c_t.shape), pl.BlockSpec((d, tn), lambda j: (0, j))],
        out_specs=pl.BlockSpec((nex, tn), lambda j: (0, j)),
        out_shape=jax.ShapeDtypeStruct((nex, nc), F32),
        sem=("parallel",), name="ada_fwd")


def _adamw_math(w, g, m, v):
    m = ADAM_B1 * m + (1.0 - ADAM_B1) * g
    v = ADAM_B2 * v + (1.0 - ADAM_B2) * (g * g)
    m_hat = m / (1.0 - ADAM_B1 ** ADAM_STEP)
    v_hat = v / (1.0 - ADAM_B2 ** ADAM_STEP)
    delta = -ADAM_LR * (m_hat / (jnp.sqrt(v_hat) + ADAM_EPS) + ADAM_WD * w)
    return delta, m, v


def adamw(w, g, m, v, name, dep=None):
    r, n = w.shape
    tr = _tile(r, 512, SUBLANES)
    tn = _tile(n, 1024)

    def body(w_ref, g_ref, m_ref, v_ref, d_ref, nm_ref, nv_ref):
        d_ref[...], nm_ref[...], nv_ref[...] = _adamw_math(w_ref[...], g_ref[...], m_ref[...], v_ref[...])

    blk = pl.BlockSpec((tr, tn), lambda i, j: (i, j))
    return _pallas(
        body, [w, g, m, v], dep=dep, grid=(r // tr, n // tn),
        in_specs=[blk] * 4, out_specs=[blk] * 3,
        out_shape=[jax.ShapeDtypeStruct((r, n), F32)] * 3,
        sem=("parallel", "parallel"), name=name)


def ada_grad_adamw(c_t, dmod_cols, w, m, v, dep=None):
    d, nc = w.shape
    nex = c_t.shape[1]
    tr = _tile(d, 512, SUBLANES)
    tn = _tile(nc, 1024)

    def body(ct_ref, dm_ref, w_ref, m_ref, v_ref, g_ref, d_ref, nm_ref, nv_ref):
        ct = ct_ref[...]
        cact = ct * _sigmoid(ct)
        dm = dm_ref[...]
        g = cact[:, 0:1] * dm[0:1, :]
        for b in range(1, nex):
            g = g + cact[:, b:b + 1] * dm[b:b + 1, :]
        g_ref[...] = g
        d_ref[...], nm_ref[...], nv_ref[...] = _adamw_math(w_ref[...], g, m_ref[...], v_ref[...])

    blk = pl.BlockSpec((tr, tn), lambda i, j: (i, j))
    return _pallas(
        body, [c_t, dmod_cols, w, m, v], dep=dep, grid=(d // tr, nc // tn),
        in_specs=[pl.BlockSpec((tr, nex), lambda i, j: (i, 0)), pl.BlockSpec((nex, tn), lambda i, j: (0, j)),
                  blk, blk, blk],
        out_specs=[blk] * 4,
        out_shape=[jax.ShapeDtypeStruct((d, nc), F32)] * 4,
        sem=("parallel", "parallel"), name="ada_grad_adamw")


def small_sum_adamw(gathered, w, m, v, dep=None):
    ndev, r, n = gathered.shape

    def body(ga_ref, w_ref, m_ref, v_ref, g_ref, d_ref, nm_ref, nv_ref):
        g = ga_ref[0]
        for s in range(1, ndev):
            g = g + ga_ref[s]
        g_ref[...] = g
        d_ref[...], nm_ref[...], nv_ref[...] = _adamw_math(w_ref[...], g, m_ref[...], v_ref[...])

    return _pallas(
        body, [gathered, w, m, v], dep=dep, in_specs=[_VMEM] * 4, out_specs=[_VMEM] * 4,
        out_shape=[jax.ShapeDtypeStruct((r, n), F32)] * 4,
        name="small_sum_adamw")


def _position():
    return lax.axis_index("x"), lax.axis_index("y"), lax.axis_index("c")


def _other_chips(x, y):
    return [(1 - x, y), (x, 1 - y), (1 - x, 1 - y)]


def allgather_small(block, name, dep=None):
    rows, width = block.shape

    def body(x_ref, out_ref, send_sems, recv_sems, local_sem):
        x, y, c = _position()
        me, sibling = (x, y, c), (x, y, 1 - c)
        chips = _other_chips(x, y)

        def slot(px, py, pc):
            return out_ref.at[4 * px + 2 * py + pc]

        def copy(k, block_of, to, src=None):
            return pltpu.make_async_remote_copy(
                src_ref=slot(*block_of) if src is None else src, dst_ref=slot(*block_of),
                send_sem=send_sems.at[k], recv_sem=recv_sems.at[k], device_id=to, device_id_type=MESH)

        mine = pltpu.make_async_copy(x_ref, slot(*me), local_sem)
        mine.start()
        first = [copy(0, me, sibling, src=x_ref)]
        first += [copy(1 + j, me, (*chip, c), src=x_ref) for j, chip in enumerate(chips)]
        for cp in first:
            cp.start()
        passed = [copy(4 + j, (*chip, c), sibling) for j, chip in enumerate(chips)]
        for j, chip in enumerate(chips):
            copy(1 + j, (*chip, c), me).wait_recv()
            passed[j].start()
        copy(0, sibling, me).wait_recv()
        for j, chip in enumerate(chips):
            copy(4 + j, (*chip, 1 - c), me).wait_recv()
        for cp in first + passed:
            cp.wait_send()
        mine.wait()

    return _pallas(
        body, [block], dep=dep,
        out_shape=jax.ShapeDtypeStruct((N_DEV, rows, width), block.dtype),
        in_specs=[_VMEM], out_specs=_VMEM,
        scratch=[pltpu.SemaphoreType.DMA((7,)), pltpu.SemaphoreType.DMA((7,)), pltpu.SemaphoreType.DMA],
        name=name)


class Started(NamedTuple):
    send_sems: Any
    recv_sems: Any
    bufs: list


def exchange_start(name, bufs, n_copies, plan, dep=None):
    nb = len(bufs)

    def body(*refs):
        for cp in plan(refs[:nb], refs[nb], refs[nb + 1]):
            cp.start()

    outs = _pallas(
        body, [pltpu.with_memory_space_constraint(b, pltpu.HBM) for b in bufs], dep=dep, name=name,
        out_shape=(pltpu.SemaphoreType.DMA((n_copies,)), pltpu.SemaphoreType.DMA((n_copies,)),
                   *[pltpu.HBM(b.shape, b.dtype) for b in bufs]),
        in_specs=[_HBM] * nb,
        out_specs=(_SEM, _SEM, *[_HBM] * nb),
        input_output_aliases={i: 2 + i for i in range(nb)},
        compiler_params=pltpu.CompilerParams(has_side_effects=_EFFECT))
    return Started(outs[0], outs[1], list(outs[2:2 + nb]))


def exchange_wait(name, started, plan, dep=None):
    nb = len(started.bufs)

    def body(*refs):
        for cp in plan(refs[:nb], refs[nb], refs[nb + 1]):
            cp.wait_send()
            cp.wait_recv()

    outs = _pallas(
        body, [*started.bufs, started.send_sems, started.recv_sems], dep=dep, name=name,
        out_shape=tuple(pltpu.HBM(b.shape, b.dtype) for b in started.bufs),
        in_specs=[_HBM] * nb + [_SEM, _SEM],
        out_specs=tuple([_HBM] * nb),
        input_output_aliases={i: i for i in range(nb)},
        compiler_params=pltpu.CompilerParams(has_side_effects=_EFFECT))
    return list(outs)


def _remote(src, dst, send_sems, recv_sems, i, to):
    return pltpu.make_async_remote_copy(src_ref=src, dst_ref=dst, send_sem=send_sems.at[i], recv_sem=recv_sems.at[i],
                                        device_id=to, device_id_type=MESH)


def _half_rows(buf_rows, chip_idx, pc):
    half = buf_rows // (2 * N_CHIPS)
    return pl.ds((2 * chip_idx + pc) * half, half)


def plan_gather_ici(refs, send_sems, recv_sems):
    x, y, c = _position()
    copies = []
    for k, ref in enumerate(refs):
        rows = ref.at[_half_rows(ref.shape[0], 2 * x + y, c), :]
        for j, chip in enumerate(_other_chips(x, y)):
            copies.append(_remote(rows, rows, send_sems, recv_sems, 3 * k + j, (*chip, c)))
    return copies


def plan_gather_d2d(refs, send_sems, recv_sems):
    x, y, c = _position()
    copies = []
    for k, ref in enumerate(refs):
        for j, (px, py) in enumerate(_other_chips(x, y)):
            rows = ref.at[_half_rows(ref.shape[0], 2 * px + py, c), :]
            copies.append(_remote(rows, rows, send_sems, recv_sems, 3 * k + j, (x, y, 1 - c)))
    return copies


def plan_pair_exchange(refs, send_sems, recv_sems):
    x, y, c = _position()
    nw = len(refs) // 2
    copies = []
    for k in range(nw):
        for chip in range(N_CHIPS):
            copies.append(_remote(refs[k].at[chip, 1 - c], refs[nw + k].at[chip], send_sems, recv_sems,
                                  N_CHIPS * k + chip, (x, y, 1 - c)))
    return copies


def plan_chip_exchange(refs, send_sems, recv_sems):
    x, y, c = _position()
    nw = len(refs) // 2
    copies = []
    for k in range(nw):
        for j, (px, py) in enumerate(_other_chips(x, y)):
            copies.append(_remote(refs[k].at[2 * px + py], refs[nw + k].at[2 * x + y], send_sems, recv_sems,
                                  3 * k + j, (px, py, c)))
    return copies


def plan_pair_share(refs, send_sems, recv_sems):
    x, y, c = _position()
    return [_remote(ref.at[c], ref.at[c], send_sems, recv_sems, k, (x, y, 1 - c)) for k, ref in enumerate(refs)]


def cast_into_slot(src, slot, n_slots, name, dep=None):
    r, n = src.shape
    tr = _tile(r, 512, BF16_SUBLANES)
    tn = _tile(n, 1024)

    def body(slot_ref, s_ref, o_ref):
        o_ref[...] = s_ref[...].astype(BF16)

    return _pallas(
        body, [slot, src], dep=dep, n_prefetch=1, grid=(r // tr, n // tn),
        in_specs=[pl.BlockSpec((tr, tn), lambda i, j, sl: (i, j))],
        out_specs=pl.BlockSpec((None, tr, tn), lambda i, j, sl: (sl[0], i, j)),
        out_shape=jax.ShapeDtypeStruct((n_slots, r, n), BF16),
        sem=("parallel", "parallel"), name=name)


def pair_sum(g, r, core, name, dep=None):
    nchip, _, h, n = g.shape
    th = _tile(h, 512, BF16_SUBLANES)
    tn = _tile(n, 1024)

    def body(core_ref, g_ref, r_ref, o_ref):
        o_ref[...] = (g_ref[...].astype(F32) + r_ref[...].astype(F32)).astype(BF16)

    return _pallas(
        body, [core, g, r], dep=dep, n_prefetch=1, grid=(nchip, h // th, n // tn),
        in_specs=[pl.BlockSpec((None, None, th, tn), lambda a, i, j, cr: (a, cr[0], i, j)),
                  pl.BlockSpec((None, th, tn), lambda a, i, j, cr: (a, i, j))],
        out_specs=pl.BlockSpec((None, th, tn), lambda a, i, j, cr: (a, i, j)),
        out_shape=jax.ShapeDtypeStruct((nchip, h, n), BF16),
        sem=("parallel", "parallel", "parallel"), name=name)


def chip_sum(own, got, where, name, dep=None):
    nchip, h, n = got.shape
    th = _tile(h, 512, BF16_SUBLANES)
    tn = _tile(n, 1024)

    def body(where_ref, own_ref, *rest):
        got_refs, o_ref = rest[:nchip], rest[nchip]
        chip = where_ref[0]
        acc = None
        for s in range(nchip):
            term = jnp.where(chip == s, own_ref[...], got_refs[s][...]).astype(F32)
            acc = term if acc is None else acc + term
        o_ref[...] = acc

    def got_spec(s):
        return pl.BlockSpec((None, th, tn), lambda i, j, wr: (jnp.where(wr[0] == s, (s + 1) % nchip, s), i, j))

    return _pallas(
        body, [where, own, *[got] * nchip], dep=dep, n_prefetch=1, grid=(h // th, n // tn),
        in_specs=[pl.BlockSpec((None, th, tn), lambda i, j, wr: (wr[0], i, j))]
        + [got_spec(s) for s in range(nchip)],
        out_specs=pl.BlockSpec((None, th, tn), lambda i, j, wr: (wr[1], i, j)),
        out_shape=jax.ShapeDtypeStruct((2, h, n), F32),
        sem=("parallel", "parallel"), name=name)


PACK_ROWS = SUBLANES


def _part_rows(shape):
    size = int(np.prod(shape))
    return -(-size // (PACK_ROWS * LANES)) * PACK_ROWS


def _pack_rows(parts):
    rows = []
    for p in parts:
        flat = p.reshape(-1)
        flat = jnp.pad(flat, (0, _part_rows(p.shape) * LANES - flat.shape[0]))
        rows.append(flat.reshape(-1, LANES))
    return jnp.concatenate(rows, axis=0)


def _unpack_rows(packed, shapes):
    out, row = [], 0
    for s in shapes:
        size, nrows = int(np.prod(s)), _part_rows(s)
        out.append(packed[row:row + nrows].reshape(-1)[:size].reshape(s))
        row += nrows
    return out


def kernel(x, c, w_ada, b_ada, norm_mix_g, w_in, q_norm_g, k_norm_g, attn_sinks, rel_bias, w_attn_out, conv_w, conv_b, conv_ln_g, conv_ln_b, w_conv_out, w_mix_out, norm_ffn_g, w_ffn_in, w_ffn_out, loss_target, m_w_ada, m_b_ada, m_norm_mix_g, m_w_in, m_q_norm_g, m_k_norm_g, m_attn_sinks, m_rel_bias, m_w_attn_out, m_conv_w, m_conv_b, m_conv_ln_g, m_conv_ln_b, m_w_conv_out, m_w_mix_out, m_norm_ffn_g, m_w_ffn_in, m_w_ffn_out, v_w_ada, v_b_ada, v_norm_mix_g, v_w_in, v_q_norm_g, v_k_norm_g, v_attn_sinks, v_rel_bias, v_w_attn_out, v_conv_w, v_conv_b, v_conv_ln_g, v_conv_ln_b, v_w_conv_out, v_w_mix_out, v_norm_ffn_g, v_w_ffn_in, v_w_ffn_out):
    run = InOrder()
    xi, yi, ci = _position()
    chip = 2 * xi + yi
    me = 2 * chip + ci
    chip_arr = chip.astype(jnp.int32).reshape(1)
    core_arr = ci.astype(jnp.int32).reshape(1)
    where_arr = jnp.stack([chip, ci]).astype(jnp.int32)

    xe, tgt = x[0], loss_target[0]
    t, d = xe.shape
    hd = q_norm_g.shape[-1]
    nq = attn_sinks.shape[-1]
    aw = nq * hd
    ch = conv_b.shape[-1]
    in_width = N_CHIPS * w_in.shape[-1]
    kvw = (in_width - aw - 2 * ch - 2 * d) // 2
    nkv = kvw // hd
    dff = N_CHIPS * w_ffn_out.shape[1]
    off_k, off_v, off_ca = aw, aw + kvw, aw + 2 * kvw
    off_cb, off_ga, off_gc = off_ca + ch, off_ca + 2 * ch, off_ca + 2 * ch + d
    nc_ada = w_ada.shape[-1]
    ch_loc = conv_w.shape[-1]
    nj_ffn = w_ffn_in.shape[-1]
    perm_ffn = ffn_perm(N_CHIPS)

    big = {"w_in": w_in[0], "w_attn_out": w_attn_out[0], "w_conv_out": w_conv_out[0], "w_mix_out": w_mix_out[0],
           "w_ffn_in": w_ffn_in[0], "w_ffn_out": w_ffn_out[0]}
    moments = {"w_in": (m_w_in, v_w_in), "w_attn_out": (m_w_attn_out, v_w_attn_out),
               "w_conv_out": (m_w_conv_out, v_w_conv_out), "w_mix_out": (m_w_mix_out, v_w_mix_out),
               "w_ffn_in": (m_w_ffn_in, v_w_ffn_in), "w_ffn_out": (m_w_ffn_out, v_w_ffn_out)}
    gather_groups = {"in": ["w_in"], "mid": ["w_attn_out", "w_conv_out", "w_mix_out"], "ffn_in": ["w_ffn_in"],
                     "ffn_out": ["w_ffn_out"]}
    grads, deltas, new_m, new_v = {}, {}, {}, {}

    def gather_cast(gname):
        bufs = []
        for n in gather_groups[gname]:
            r, ncol = big[n].shape
            bufs.append(run(cast_into_slot, big[n], chip_arr, N_CHIPS, "cast_" + n).reshape(N_CHIPS * r, ncol))
        return bufs

    def gather_ici_start(gname, bufs):
        return run(exchange_start, "gather_ici_start_" + gname, bufs, 3 * len(bufs), plan_gather_ici)

    def gather_pass_on(gname, ici):
        landed = run(exchange_wait, "gather_ici_wait_" + gname, ici, plan_gather_ici)
        return run(exchange_start, "gather_d2d_start_" + gname, landed, 3 * len(landed), plan_gather_d2d)

    def gathered(gname, d2d):
        outs = run(exchange_wait, "gather_d2d_wait_" + gname, d2d, plan_gather_d2d)
        return [o.reshape(N_CHIPS, *big[n].shape) for o, n in zip(outs, gather_groups[gname])]

    def rs_pair_start(gname, names, partials):
        blocks = [g.reshape(N_CHIPS, 2, big[n].shape[0] // 2, big[n].shape[1]) for n, g in zip(names, partials)]
        land = [lax.empty((N_CHIPS,) + b.shape[2:], BF16) for b in blocks]
        return run(exchange_start, "pair_exchange_start_" + gname, blocks + land, N_CHIPS * len(blocks),
                   plan_pair_exchange)

    def rs_chip_start(gname, names, pair):
        nw = len(names)
        outs = run(exchange_wait, "pair_exchange_wait_" + gname, pair, plan_pair_exchange)
        sums = [run(pair_sum, g, r, core_arr, "pair_sum_" + n) for n, g, r in zip(names, outs[:nw], outs[nw:])]
        land = [lax.empty(s.shape, BF16) for s in sums]
        return run(exchange_start, "chip_exchange_start_" + gname, sums + land, 3 * nw, plan_chip_exchange)

    def rs_share_start(gname, names, chipx):
        nw = len(names)
        outs = run(exchange_wait, "chip_exchange_wait_" + gname, chipx, plan_chip_exchange)
        halves = [run(chip_sum, s, r, where_arr, "chip_sum_" + n) for n, s, r in zip(names, outs[:nw], outs[nw:])]
        return run(exchange_start, "pair_share_start_" + gname, halves, nw, plan_pair_share)

    def rs_finish(gname, names, share):
        fulls = run(exchange_wait, "pair_share_wait_" + gname, share, plan_pair_share)
        for n, g2 in zip(names, fulls):
            g = g2.reshape(big[n].shape)
            dl, nm, nv = run(adamw, big[n], g, moments[n][0][0], moments[n][1][0], "adamw_" + n)
            grads[n], deltas[n], new_m[n], new_v[n] = g[None], dl[None], nm[None], nv[None]

    bufs_in = gather_cast("in")
    got1 = run(allgather_small, _pack_rows([c[0], conv_w[0]]), "allgather_cond")
    c_rows = _part_rows((d,))
    c_all = got1[:, :c_rows].reshape(N_DEV, -1)[:, :d]
    w_rows = _part_rows((CONV_WIDTH, ch_loc))
    conv_w_full = got1[0::2, c_rows:c_rows + w_rows].reshape(N_CHIPS, -1)[:, :CONV_WIDTH * ch_loc]
    conv_w_full = jnp.transpose(conv_w_full.reshape(N_CHIPS, CONV_WIDTH, ch_loc), (1, 0, 2)).reshape(CONV_WIDTH, ch)
    conv_w_pad = jnp.pad(conv_w_full, ((0, 1), (0, 0)))
    c_t = jnp.transpose(c_all)
    mod_cols = run(ada_fwd, c_t, w_ada[0])
    got2 = run(allgather_small, mod_cols.reshape(-1, LANES), "allgather_mod")
    mod_all = got2.reshape(N_CHIPS, 2, N_DEV, nc_ada)[:, 0]
    mod = lax.dynamic_slice_in_dim(mod_all, me, 1, axis=1).reshape(1, N_CHIPS * nc_ada) + b_ada
    mod = jnp.pad(mod.reshape(N_MOD, d), ((0, SUBLANES - N_MOD), (0, 0)))

    ici = {"in": gather_ici_start("in", bufs_in)}
    for gname in ("mid", "ffn_in", "ffn_out"):
        ici[gname] = gather_ici_start(gname, gather_cast(gname))

    h = run(pre_mix_fwd, xe, mod, norm_mix_g)
    bucket = _t5_bucket_table()
    bucket_p, bucket_c = jnp.asarray(bucket[:, :BLOCK]), jnp.asarray(bucket[:, BLOCK:])
    bias_p, bias_c = run(bias_table, rel_bias, bucket_p, bucket_c)
    d2d_in = gather_pass_on("in", ici["in"])
    (wg_in,) = gathered("in", d2d_in)
    p = run(mm_nn, h, wg_in, tn=_tile(wg_in.shape[2], 640), tk=d, out_dtype=F32, name="mm_in")
    d2d_mid = gather_pass_on("mid", ici["mid"])

    def heads(a, n):
        return jnp.transpose(a.reshape(t, n, hd), (1, 0, 2))

    def unheads(a):
        return jnp.transpose(a, (1, 0, 2)).reshape(t, -1)

    qh, kh, vh = heads(p[:, :aw], nq), heads(p[:, off_k:off_v], nkv), heads(p[:, off_v:off_ca], nkv)
    sinks3 = attn_sinks.reshape(nq, 1, 1)
    attn_o = unheads(run(attn_fwd, qh, kh, vh, bias_p, bias_c, sinks3, q_norm_g, k_norm_g))
    ca, cb = p[:, off_ca:off_cb], p[:, off_cb:off_ga]
    s_conv = run(conv_fwd, ca, cb, conv_w_pad, conv_b, conv_ln_g, conv_ln_b)
    wg_attn_out, wg_conv_out, wg_mix_out = gathered("mid", d2d_mid)
    wg_mix_out = wg_mix_out.reshape(1, d, d)
    y_attn = run(mm_nn, attn_o, wg_attn_out, tn=_tile(wg_attn_out.shape[2], 512), tk=aw, out_dtype=F32,
                 name="mm_attn_out")
    y_conv = run(mm_nn, s_conv, wg_conv_out, tn=_tile(wg_conv_out.shape[2], 512), tk=ch, out_dtype=F32,
                 name="mm_conv_out")
    merged = run(merge_fwd, p, y_attn, y_conv, off_ga, off_gc)
    d2d_ffn_in = gather_pass_on("ffn_in", ici["ffn_in"])
    o_m = run(mm_nn, merged, wg_mix_out, tn=_tile(d, 512), tk=d, out_dtype=F32, name="mm_mix_out")
    x1, h2 = run(pre_ffn_fwd, xe, o_m, mod, norm_ffn_g)
    (wg_ffn_in,) = gathered("ffn_in", d2d_ffn_in)
    f = run(mm_nn, h2, wg_ffn_in, tn=_tile(nj_ffn, 1408), tk=d, out_dtype=BF16, name="mm_ffn_in", perm=perm_ffn)
    d2d_ffn_out = gather_pass_on("ffn_out", ici["ffn_out"])
    act = run(swiglu_fwd, f, nj_ffn)
    (wg_ffn_out,) = gathered("ffn_out", d2d_ffn_out)
    wg_ffn_out = wg_ffn_out.reshape(1, dff, d)
    o_f = run(mm_nn, act, wg_ffn_out, tn=_tile(d, 1024), tk=_tile(dff, 512), out_dtype=F32, name="mm_ffn_out")
    loss11, dy, dof, acc_l = run(loss_head, x1, o_f, tgt, mod)

    gw_ffn_out = run(mm_tn, act, dof, 1, tk=_tile(dff, 512), tn=_tile(d, 1024), name="mm_ffn_out_dw")
    px_ffn_out = rs_pair_start("ffn_out", ["w_ffn_out"], [gw_ffn_out])
    dact = run(mm_nt, dof, wg_ffn_out, tko=_tile(dff, 512), tn=d, name="mm_ffn_out_dx")
    cx_ffn_out = rs_chip_start("ffn_out", ["w_ffn_out"], px_ffn_out)
    df = run(swiglu_bwd, f, dact, nj_ffn)
    gw_ffn_in = run(mm_tn, h2, df, N_CHIPS, tk=_tile(d, 512), tn=_tile(nj_ffn, 1408), name="mm_ffn_in_dw",
                    perm=perm_ffn)
    px_ffn_in = rs_pair_start("ffn_in", ["w_ffn_in"], [gw_ffn_in])
    dh2 = run(mm_nt, df, wg_ffn_in, tko=_tile(d, 1024), tn=_tile(nj_ffn, 1408), name="mm_ffn_in_dx", perm=perm_ffn)
    sh_ffn_out = rs_share_start("ffn_out", ["w_ffn_out"], cx_ffn_out)
    cx_ffn_in = rs_chip_start("ffn_in", ["w_ffn_in"], px_ffn_in)
    dx1, dom, acc_f = run(pre_ffn_bwd, x1, dh2, dy, o_m, mod, norm_ffn_g)
    gw_mix_out = run(mm_tn, merged, dom, 1, tk=_tile(d, 512), tn=_tile(d, 1024), name="mm_mix_out_dw")
    px_mix = rs_pair_start("mix_out", ["w_mix_out"], [gw_mix_out])
    dmerged = run(mm_nt, dom, wg_mix_out, tko=_tile(d, 512), tn=d, name="mm_mix_out_dx")
    dy_attn, dy_conv, dga, dgc = run(merge_bwd, p, y_attn, y_conv, dmerged, off_ga, off_gc)
    rs_finish("ffn_out", ["w_ffn_out"], sh_ffn_out)
    cx_mix = rs_chip_start("mix_out", ["w_mix_out"], px_mix)
    gw_attn_out = run(mm_tn, attn_o, dy_attn, N_CHIPS, tk=_tile(aw, 512), tn=_tile(wg_attn_out.shape[2], 512),
                      name="mm_attn_out_dw")
    gw_conv_out = run(mm_tn, s_conv, dy_conv, N_CHIPS, tk=_tile(ch, 512), tn=_tile(wg_conv_out.shape[2], 512),
                      name="mm_conv_out_dw")
    ac_names = ["w_attn_out", "w_conv_out"]
    px_ac = rs_pair_start("attn_conv_out", ac_names, [gw_attn_out, gw_conv_out])
    dattn_o = run(mm_nt, dy_attn, wg_attn_out, tko=_tile(aw, 1024), tn=_tile(wg_attn_out.shape[2], 512),
                  name="mm_attn_out_dx")
    ds_conv = run(mm_nt, dy_conv, wg_conv_out, tko=_tile(ch, 1024), tn=_tile(wg_conv_out.shape[2], 512),
                  name="mm_conv_out_dx")
    cx_ac = rs_chip_start("attn_conv_out", ac_names, px_ac)
    dca, dcb, dconv_w, dconv_vec = run(conv_bwd, ca, cb, ds_conv, conv_w_pad, conv_b, conv_ln_g, conv_ln_b)
    sh_ffn_in = rs_share_start("ffn_in", ["w_ffn_in"], cx_ffn_in)
    dqh, dkh, dvh, dbp, dbc, dsinks, dqg, dkg = run(attn_bwd, qh, kh, vh, bias_p, bias_c, sinks3, q_norm_g, k_norm_g,
                                                     heads(dattn_o, nq))
    sh_mix = rs_share_start("mix_out", ["w_mix_out"], cx_mix)
    sh_ac = rs_share_start("attn_conv_out", ac_names, cx_ac)
    drel = run(bias_table_bwd, dbp, dbc, bucket_p, bucket_c).reshape(NUM_BUCKETS, nq)
    dp = jnp.concatenate([unheads(dqh).astype(BF16), unheads(dkh).astype(BF16), unheads(dvh).astype(BF16),
                          dca, dcb, dga, dgc], axis=1)
    gw_in = run(mm_tn, h, dp, N_CHIPS, tk=_tile(d, 512), tn=_tile(wg_in.shape[2], 640), name="mm_in_dw")
    px_in = rs_pair_start("in", ["w_in"], [gw_in])
    dh = run(mm_nt, dp, wg_in, tko=_tile(d, 1024), tn=_tile(wg_in.shape[2], 640), name="mm_in_dx")
    grad_x, acc_m = run(pre_mix_bwd, xe, dh, dx1, mod, norm_mix_g)

    dmod = jnp.concatenate([acc_m[0:1], acc_m[1:2], acc_f[3:4], acc_f[0:1], acc_f[1:2], acc_l[0:1]], axis=1)
    small_names = ["b_ada", "norm_mix_g", "q_norm_g", "k_norm_g", "attn_sinks", "rel_bias", "conv_b", "conv_ln_g",
                   "conv_ln_b", "norm_ffn_g"]
    small_w = [b_ada, norm_mix_g, q_norm_g, k_norm_g, attn_sinks, rel_bias, conv_b, conv_ln_g, conv_ln_b, norm_ffn_g]
    small_m = [m_b_ada, m_norm_mix_g, m_q_norm_g, m_k_norm_g, m_attn_sinks, m_rel_bias, m_conv_b, m_conv_ln_g,
               m_conv_ln_b, m_norm_ffn_g]
    small_v = [v_b_ada, v_norm_mix_g, v_q_norm_g, v_k_norm_g, v_attn_sinks, v_rel_bias, v_conv_b, v_conv_ln_g,
               v_conv_ln_b, v_norm_ffn_g]
    small_g = [dmod, acc_m[2:3], dqg, dkg, dsinks.reshape(1, nq), drel, dconv_vec[0:1], dconv_vec[1:2],
               dconv_vec[2:3], acc_f[2:3]]
    small_shapes = [w.shape for w in small_w]
    conv_shape = (CONV_WIDTH, ch)
    got3 = run(allgather_small, _pack_rows(small_g + [dconv_w[:CONV_WIDTH]]), "allgather_small_grads")
    cx_in = rs_chip_start("in", ["w_in"], px_in)
    zero_conv = jnp.zeros(conv_shape, F32)
    g_small, d_small, nm_small, nv_small = run(
        small_sum_adamw, got3, _pack_rows(small_w + [zero_conv]), _pack_rows(small_m + [zero_conv]),
        _pack_rows(small_v + [zero_conv]))
    g_parts = _unpack_rows(g_small, small_shapes + [conv_shape])
    grads.update(zip(small_names, g_parts[:-1]))
    deltas.update(zip(small_names, _unpack_rows(d_small, small_shapes)))
    new_m.update(zip(small_names, _unpack_rows(nm_small, small_shapes)))
    new_v.update(zip(small_names, _unpack_rows(nv_small, small_shapes)))

    g_conv_w = lax.dynamic_slice_in_dim(g_parts[-1], chip * ch_loc, ch_loc, axis=1)
    grads["conv_w"] = g_conv_w[None]
    dl, nm, nv = run(adamw, conv_w[0], g_conv_w, m_conv_w[0], v_conv_w[0], "adamw_conv_w")
    deltas["conv_w"], new_m["conv_w"], new_v["conv_w"] = dl[None], nm[None], nv[None]

    dmod_all = got3[:, :_part_rows((N_MOD * d,))].reshape(N_DEV, -1)[:, :N_MOD * d]
    dmod_cols = lax.dynamic_slice_in_dim(dmod_all, chip * nc_ada, nc_ada, axis=1)
    g_ada, dl, nm, nv = run(ada_grad_adamw, c_t, dmod_cols, w_ada[0], m_w_ada[0], v_w_ada[0])
    grads["w_ada"], deltas["w_ada"], new_m["w_ada"], new_v["w_ada"] = g_ada[None], dl[None], nm[None], nv[None]

    rs_finish("ffn_in", ["w_ffn_in"], sh_ffn_in)
    rs_finish("mix_out", ["w_mix_out"], sh_mix)
    rs_finish("attn_conv_out", ac_names, sh_ac)
    sh_in = rs_share_start("in", ["w_in"], cx_in)
    rs_finish("in", ["w_in"], sh_in)

    loss = lax.psum(loss11[0, 0], ("x", "y", "c"))
    order = ["w_ada", "b_ada", "norm_mix_g", "w_in", "q_norm_g", "k_norm_g", "attn_sinks", "rel_bias", "w_attn_out",
             "conv_w", "conv_b", "conv_ln_g", "conv_ln_b", "w_conv_out", "w_mix_out", "norm_ffn_g", "w_ffn_in",
             "w_ffn_out"]
    return (loss, grad_x[None], *[grads[n] for n in order], *[deltas[n] for n in order],
            *[new_m[n] for n in order], *[new_v[n] for n in order])
```

```python
import functools
import math
from typing import Any, NamedTuple

import jax
import jax.numpy as jnp
import numpy as np
from jax import lax
from jax.experimental import pallas as pl
from jax.experimental.pallas import tpu as pltpu

F32 = jnp.float32
BF16 = jnp.bfloat16
MESH = pl.DeviceIdType.MESH

V7X_VMEM_BYTES = 64 * 1024 * 1024
VMEM_LIMIT = V7X_VMEM_BYTES - 8 * 1024 * 1024
LANES = 128
SUBLANES = 8
BF16_SUBLANES = 16

EPS = 1e-6
WINDOW = 128
BLOCK = 128
NUM_BUCKETS = 32
MAX_EXACT = NUM_BUCKETS // 2
MAX_DISTANCE = 128
CONV_WIDTH = 31
CONV_HALO = 32
ADAM_LR = 0.001
ADAM_B1 = 0.9
ADAM_B2 = 0.999
ADAM_EPS = 1e-08
ADAM_WD = 0.01
ADAM_STEP = 10
N_MOD = 6
SH_M, SC_M, GT_M, SH_F, SC_F, GT_F = range(6)

N_CHIPS = 4
N_DEV = 8

_ANY = pl.BlockSpec(memory_space=pl.ANY)
_VMEM = pl.BlockSpec(memory_space=pltpu.VMEM)
_SMEM = pl.BlockSpec(memory_space=pltpu.SMEM)
_HBM = pl.BlockSpec(memory_space=pltpu.HBM)
_SEM = pl.BlockSpec(memory_space=pltpu.SEMAPHORE)
_EFFECT = pltpu.SideEffectType.DATAFLOW_SIDE_EFFECTING


class InOrder:
    def __init__(self):
        self.token = None

    def __call__(self, fn, *args, **kw):
        return fn(*args, dep=self, **kw)


def _pallas(body, args, *, in_specs, out_specs, out_shape, name, dep=None, grid=(), n_prefetch=0, scratch=(),
            sem=None, **kw):
    n_lead = n_prefetch + len(in_specs)
    in_specs, args = list(in_specs), list(args)
    single = not isinstance(out_shape, (list, tuple))
    out_shapes = [out_shape] if single else list(out_shape)
    out_specs = [out_specs] if single else list(out_specs)
    if dep is not None:
        inner, n_out, takes = body, len(out_shapes), dep.token is not None

        def body(*refs):
            rest = refs[n_lead + (1 if takes else 0):]
            rest[n_out][...] = jnp.zeros((SUBLANES, LANES), F32)
            return inner(*refs[:n_lead], *rest[:n_out], *rest[n_out + 1:])

        if takes:
            in_specs.append(_ANY)
            args.append(dep.token)
        out_shapes.append(jax.ShapeDtypeStruct((SUBLANES, LANES), F32))
        out_specs.append(pl.BlockSpec((SUBLANES, LANES), lambda *_: (0, 0)))
    params = kw.pop("compiler_params", None)
    if params is None:
        params = pltpu.CompilerParams(dimension_semantics=sem, vmem_limit_bytes=VMEM_LIMIT)
    outs = pl.pallas_call(
        body,
        grid_spec=pltpu.PrefetchScalarGridSpec(num_scalar_prefetch=n_prefetch, grid=grid, in_specs=in_specs,
                                               out_specs=out_specs, scratch_shapes=list(scratch)),
        out_shape=out_shapes, compiler_params=params, name=name, **kw,
    )(*args)
    if dep is not None:
        dep.token = outs[-1]
        outs = outs[:-1]
    return outs[0] if single else list(outs)


def _tile(n, pref, unit=LANES):
    best = None
    for t in range(unit, min(n, pref) + 1, unit):
        if n % t == 0:
            best = t
    return best if best is not None else n


def _sigmoid(v):
    return 1.0 / (1.0 + jnp.exp(-v))


ROW_CHUNK = 512


def _row_chunks(m, unit=SUBLANES):
    step = _tile(m, ROW_CHUNK, unit)
    return [(s, step) for s in range(0, m, step)]


def _ew_tiles(r, n, unit=SUBLANES, elems=512 * 1024):
    return _tile(r, max(unit, elems // n), unit), n


def _block_pos(j, perm):
    if perm is None:
        return j
    pos = 0
    for a, p in enumerate(perm):
        pos = pos + jnp.where(j == a, p, 0)
    return pos


def mm_nn(a, w, *, tn, tk, out_dtype, name, perm=None, dep=None):
    m, k = a.shape
    j, k2, nj = w.shape
    assert k == k2 and nj % tn == 0 and k % tk == 0
    npj, nk = nj // tn, k // tk

    def body(a_ref, w_ref, o_ref, *scratch):
        kk = pl.program_id(1)
        for s, sz in _row_chunks(m):
            rows = pl.ds(s, sz)
            p = jnp.dot(a_ref[rows, :], w_ref[...], preferred_element_type=F32)
            if nk == 1:
                o_ref[rows, :] = p.astype(out_dtype)
            else:
                acc = scratch[0]

                @pl.when(kk == 0)
                def _():
                    acc[rows, :] = p

                @pl.when(kk > 0)
                def _():
                    acc[rows, :] += p

                @pl.when(kk == nk - 1)
                def _():
                    o_ref[rows, :] = acc[rows, :].astype(out_dtype)

    return _pallas(
        body, [a, w], dep=dep, grid=(j * npj, nk),
        in_specs=[
            pl.BlockSpec((m, tk), lambda n, kk: (0, kk)),
            pl.BlockSpec((None, tk, tn), lambda n, kk: (n // npj, kk, n % npj)),
        ],
        out_specs=pl.BlockSpec((m, tn), lambda n, kk: (0, _block_pos(n // npj, perm) * npj + n % npj)),
        out_shape=jax.ShapeDtypeStruct((m, j * nj), out_dtype),
        scratch=[pltpu.VMEM((m, tn), F32)] if nk > 1 else [],
        sem=("parallel", "arbitrary"), name=name)


def mm_nt(g, w, *, tko, tn, name, perm=None, dep=None):
    m, n = g.shape
    j, k, nj = w.shape
    assert n == j * nj and nj % tn == 0 and k % tko == 0
    npj, nr = nj // tn, n // tn

    def body(g_ref, w_ref, o_ref):
        r = pl.program_id(1)
        for s, sz in _row_chunks(m):
            rows = pl.ds(s, sz)
            p = lax.dot_general(g_ref[rows, :], w_ref[...], (((1,), (1,)), ((), ())), preferred_element_type=F32)

            @pl.when(r == 0)
            def _():
                o_ref[rows, :] = p

            @pl.when(r > 0)
            def _():
                o_ref[rows, :] += p

    return _pallas(
        body, [g, w], dep=dep, grid=(k // tko, nr),
        in_specs=[
            pl.BlockSpec((m, tn), lambda ko, r: (0, _block_pos(r // npj, perm) * npj + r % npj)),
            pl.BlockSpec((None, tko, tn), lambda ko, r: (r // npj, ko, r % npj)),
        ],
        out_specs=pl.BlockSpec((m, tko), lambda ko, r: (0, ko)),
        out_shape=jax.ShapeDtypeStruct((m, k), F32),
        sem=("parallel", "arbitrary"), name=name)


def mm_tn(a, g, n_blocks, *, tk, tn, name, perm=None, dep=None):
    m, k = a.shape
    m2, n = g.shape
    nj = n // n_blocks
    assert m == m2 and nj % tn == 0 and k % tk == 0
    npj = nj // tn

    def body(a_ref, g_ref, o_ref):
        for s, sz in _row_chunks(tk, LANES):
            p = lax.dot_general(a_ref[:, pl.ds(s, sz)], g_ref[...], (((0,), (0,)), ((), ())),
                                preferred_element_type=F32)
            o_ref[pl.ds(s, sz), :] = p.astype(BF16)

    return _pallas(
        body, [a, g], dep=dep, grid=(k // tk, n // tn),
        in_specs=[
            pl.BlockSpec((m, tk), lambda kk, nn: (0, kk)),
            pl.BlockSpec((m, tn), lambda kk, nn: (0, _block_pos(nn // npj, perm) * npj + nn % npj)),
        ],
        out_specs=pl.BlockSpec((None, tk, tn), lambda kk, nn: (nn // npj, kk, nn % npj)),
        out_shape=jax.ShapeDtypeStruct((n_blocks, k, nj), BF16),
        sem=("parallel", "parallel"), name=name)


ROW_TILE = 256


def _row_spec(tr, width):
    return pl.BlockSpec((tr, width), lambda i: (i, 0))


def _full_spec(shape):
    return pl.BlockSpec(shape, lambda *_: (0,) * len(shape))


def _rms(xv):
    return lax.rsqrt(jnp.mean(xv * xv, axis=-1, keepdims=True) + EPS)


def _mod_row(mod_ref, row):
    return mod_ref[pl.ds(row, 1), :]


def pre_mix_fwd(x, mod, gain, dep=None):
    t, d = x.shape
    tr = _tile(t, ROW_TILE, SUBLANES)

    def body(x_ref, mod_ref, g_ref, h_ref):
        xv = x_ref[...]
        y = xv * _rms(xv) * g_ref[...]
        h_ref[...] = (y * (1.0 + _mod_row(mod_ref, SC_M)) + _mod_row(mod_ref, SH_M)).astype(BF16)

    return _pallas(
        body, [x, mod, gain], dep=dep, grid=(t // tr,),
        in_specs=[_row_spec(tr, d), _full_spec(mod.shape), _full_spec(gain.shape)],
        out_specs=_row_spec(tr, d),
        out_shape=jax.ShapeDtypeStruct((t, d), BF16),
        sem=("parallel",), name="pre_mix_fwd")


def pre_ffn_fwd(x, o_m, mod, gain, dep=None):
    t, d = x.shape
    tr = _tile(t, ROW_TILE, SUBLANES)

    def body(x_ref, om_ref, mod_ref, g_ref, x1_ref, h_ref):
        x1 = x_ref[...] + _mod_row(mod_ref, GT_M) * om_ref[...]
        x1_ref[...] = x1
        y = x1 * _rms(x1) * g_ref[...]
        h_ref[...] = (y * (1.0 + _mod_row(mod_ref, SC_F)) + _mod_row(mod_ref, SH_F)).astype(BF16)

    return _pallas(
        body, [x, o_m, mod, gain], dep=dep, grid=(t // tr,),
        in_specs=[_row_spec(tr, d), _row_spec(tr, d), _full_spec(mod.shape), _full_spec(gain.shape)],
        out_specs=[_row_spec(tr, d), _row_spec(tr, d)],
        out_shape=[jax.ShapeDtypeStruct((t, d), F32), jax.ShapeDtypeStruct((t, d), BF16)],
        sem=("parallel",), name="pre_ffn_fwd")


def loss_head(x1, o_f, target, mod, dep=None):
    t, d = x1.shape
    tr = _tile(t, ROW_TILE, SUBLANES)

    def body(x1_ref, of_ref, tg_ref, mod_ref, loss_ref, dy_ref, dof_ref, acc_ref):
        i = pl.program_id(0)
        gt = _mod_row(mod_ref, GT_F)
        of = of_ref[...]
        err = x1_ref[...] + gt * of - tg_ref[...]
        dy = err * (1.0 / d)
        dy_ref[...] = dy
        dof_ref[...] = (dy * gt).astype(BF16)
        part = (0.5 / d) * jnp.sum(jnp.sum(err * err, axis=1, keepdims=True), axis=0, keepdims=True)
        dgt = jnp.sum(dy * of, axis=0, keepdims=True)

        @pl.when(i == 0)
        def _():
            loss_ref[...] = jnp.zeros_like(loss_ref)
            acc_ref[...] = jnp.zeros_like(acc_ref)

        loss_ref[...] += part
        acc_ref[pl.ds(0, 1), :] += dgt

    return _pallas(
        body, [x1, o_f, target, mod], dep=dep, grid=(t // tr,),
        in_specs=[_row_spec(tr, d), _row_spec(tr, d), _row_spec(tr, d), _full_spec(mod.shape)],
        out_specs=[_full_spec((1, 1)), _row_spec(tr, d), _row_spec(tr, d), _full_spec((SUBLANES, d))],
        out_shape=[jax.ShapeDtypeStruct((1, 1), F32), jax.ShapeDtypeStruct((t, d), F32),
                   jax.ShapeDtypeStruct((t, d), BF16), jax.ShapeDtypeStruct((SUBLANES, d), F32)],
        sem=("arbitrary",), name="loss_head")


def _norm_bwd(xv, dh, sc, gain):
    rstd = _rms(xv)
    yn = xv * rstd
    dsh = jnp.sum(dh, axis=0, keepdims=True)
    dsc = jnp.sum(dh * (yn * gain), axis=0, keepdims=True)
    dgain = jnp.sum(dh * (1.0 + sc) * yn, axis=0, keepdims=True)
    dyn = dh * ((1.0 + sc) * gain)
    dx = rstd * (dyn - yn * jnp.mean(dyn * yn, axis=-1, keepdims=True))
    return dx, dsh, dsc, dgain


def pre_ffn_bwd(x1, dh2, dy, o_m, mod, gain, dep=None):
    t, d = x1.shape
    tr = _tile(t, ROW_TILE, SUBLANES)

    def body(x1_ref, dh_ref, dy_ref, om_ref, mod_ref, g_ref, dx1_ref, dom_ref, acc_ref):
        i = pl.program_id(0)
        dxn, dsh, dsc, dgain = _norm_bwd(x1_ref[...], dh_ref[...], _mod_row(mod_ref, SC_F), g_ref[...])
        dx1 = dy_ref[...] + dxn
        dx1_ref[...] = dx1
        dom_ref[...] = (dx1 * _mod_row(mod_ref, GT_M)).astype(BF16)
        dgt = jnp.sum(dx1 * om_ref[...], axis=0, keepdims=True)

        @pl.when(i == 0)
        def _():
            acc_ref[...] = jnp.zeros_like(acc_ref)

        acc_ref[pl.ds(0, 1), :] += dsh
        acc_ref[pl.ds(1, 1), :] += dsc
        acc_ref[pl.ds(2, 1), :] += dgain
        acc_ref[pl.ds(3, 1), :] += dgt

    return _pallas(
        body, [x1, dh2, dy, o_m, mod, gain], dep=dep, grid=(t // tr,),
        in_specs=[_row_spec(tr, d)] * 4 + [_full_spec(mod.shape), _full_spec(gain.shape)],
        out_specs=[_row_spec(tr, d), _row_spec(tr, d), _full_spec((SUBLANES, d))],
        out_shape=[jax.ShapeDtypeStruct((t, d), F32), jax.ShapeDtypeStruct((t, d), BF16),
                   jax.ShapeDtypeStruct((SUBLANES, d), F32)],
        sem=("arbitrary",), name="pre_ffn_bwd")


def pre_mix_bwd(x, dh, dx1, mod, gain, dep=None):
    t, d = x.shape
    tr = _tile(t, ROW_TILE, SUBLANES)

    def body(x_ref, dh_ref, dx1_ref, mod_ref, g_ref, gx_ref, acc_ref):
        i = pl.program_id(0)
        dxn, dsh, dsc, dgain = _norm_bwd(x_ref[...], dh_ref[...], _mod_row(mod_ref, SC_M), g_ref[...])
        gx_ref[...] = dx1_ref[...] + dxn

        @pl.when(i == 0)
        def _():
            acc_ref[...] = jnp.zeros_like(acc_ref)

        acc_ref[pl.ds(0, 1), :] += dsh
        acc_ref[pl.ds(1, 1), :] += dsc
        acc_ref[pl.ds(2, 1), :] += dgain

    return _pallas(
        body, [x, dh, dx1, mod, gain], dep=dep, grid=(t // tr,),
        in_specs=[_row_spec(tr, d)] * 3 + [_full_spec(mod.shape), _full_spec(gain.shape)],
        out_specs=[_row_spec(tr, d), _full_spec((SUBLANES, d))],
        out_shape=[jax.ShapeDtypeStruct((t, d), F32), jax.ShapeDtypeStruct((SUBLANES, d), F32)],
        sem=("arbitrary",), name="pre_mix_bwd")


def merge_fwd(p, y_attn, y_conv, off_ga, off_gc, dep=None):
    t, d = y_attn.shape
    tr = _tile(t, ROW_TILE, SUBLANES)
    cw = math.gcd(math.gcd(off_ga, off_gc), math.gcd(d, 512))
    nc = d // cw

    def body(ga_ref, gc_ref, ya_ref, yc_ref, o_ref):
        o_ref[...] = (_sigmoid(ga_ref[...]) * ya_ref[...] + _sigmoid(gc_ref[...]) * yc_ref[...]).astype(BF16)

    return _pallas(
        body, [p, p, y_attn, y_conv], dep=dep, grid=(t // tr, nc),
        in_specs=[pl.BlockSpec((tr, cw), lambda i, j: (i, off_ga // cw + j)),
                  pl.BlockSpec((tr, cw), lambda i, j: (i, off_gc // cw + j)),
                  pl.BlockSpec((tr, cw), lambda i, j: (i, j)),
                  pl.BlockSpec((tr, cw), lambda i, j: (i, j))],
        out_specs=pl.BlockSpec((tr, cw), lambda i, j: (i, j)),
        out_shape=jax.ShapeDtypeStruct((t, d), BF16),
        sem=("parallel", "parallel"), name="merge_fwd")


def merge_bwd(p, y_attn, y_conv, dmerged, off_ga, off_gc, dep=None):
    t, d = y_attn.shape
    tr = _tile(t, ROW_TILE, SUBLANES)
    cw = math.gcd(math.gcd(off_ga, off_gc), math.gcd(d, 512))
    nc = d // cw

    def body(ga_ref, gc_ref, ya_ref, yc_ref, dm_ref, dya_ref, dyc_ref, dga_ref, dgc_ref):
        dm = dm_ref[...]
        sa = _sigmoid(ga_ref[...])
        sc = _sigmoid(gc_ref[...])
        dya_ref[...] = (dm * sa).astype(BF16)
        dyc_ref[...] = (dm * sc).astype(BF16)
        dga_ref[...] = (dm * ya_ref[...] * sa * (1.0 - sa)).astype(BF16)
        dgc_ref[...] = (dm * yc_ref[...] * sc * (1.0 - sc)).astype(BF16)

    blk = pl.BlockSpec((tr, cw), lambda i, j: (i, j))
    return _pallas(
        body, [p, p, y_attn, y_conv, dmerged], dep=dep, grid=(t // tr, nc),
        in_specs=[pl.BlockSpec((tr, cw), lambda i, j: (i, off_ga // cw + j)),
                  pl.BlockSpec((tr, cw), lambda i, j: (i, off_gc // cw + j)), blk, blk, blk],
        out_specs=[blk] * 4,
        out_shape=[jax.ShapeDtypeStruct((t, d), BF16)] * 4,
        sem=("parallel", "parallel"), name="merge_bwd")


def ffn_perm(n_blocks):
    half = n_blocks // 2
    return tuple(2 * j if j < half else 2 * (j - half) + 1 for j in range(n_blocks))


def swiglu_fwd(f, nj, dep=None):
    t, two = f.shape
    tr = _tile(t, ROW_TILE, SUBLANES)
    npair = two // (2 * nj)

    def body(f_ref, o_ref):
        g = f_ref[:, :nj].astype(F32)
        u = f_ref[:, nj:].astype(F32)
        o_ref[...] = (g * _sigmoid(g) * u).astype(BF16)

    return _pallas(
        body, [f], dep=dep, grid=(t // tr, npair),
        in_specs=[pl.BlockSpec((tr, 2 * nj), lambda i, j: (i, j))],
        out_specs=pl.BlockSpec((tr, nj), lambda i, j: (i, j)),
        out_shape=jax.ShapeDtypeStruct((t, two // 2), BF16),
        sem=("parallel", "parallel"), name="swiglu_fwd")


def swiglu_bwd(f, dact, nj, dep=None):
    t, two = f.shape
    tr = _tile(t, ROW_TILE, SUBLANES)
    npair = two // (2 * nj)

    def body(f_ref, da_ref, o_ref):
        g = f_ref[:, :nj].astype(F32)
        u = f_ref[:, nj:].astype(F32)
        da = da_ref[...]
        s = _sigmoid(g)
        o_ref[:, :nj] = (da * u * (s * (1.0 + g * (1.0 - s)))).astype(BF16)
        o_ref[:, nj:] = (da * (g * s)).astype(BF16)

    return _pallas(
        body, [f, dact], dep=dep, grid=(t // tr, npair),
        in_specs=[pl.BlockSpec((tr, 2 * nj), lambda i, j: (i, j)), pl.BlockSpec((tr, nj), lambda i, j: (i, j))],
        out_specs=pl.BlockSpec((tr, 2 * nj), lambda i, j: (i, j)),
        out_shape=jax.ShapeDtypeStruct((t, two), BF16),
        sem=("parallel", "parallel"), name="swiglu_bwd")


def _t5_bucket_table():
    q_off = np.arange(BLOCK)
    k_off = np.arange(2 * BLOCK)
    dist = q_off[:, None] + BLOCK - k_off[None, :]
    n = np.maximum(dist, 0)
    nf = np.maximum(n, 1).astype(np.float32)
    large = MAX_EXACT + (np.log(nf / np.float32(MAX_EXACT)) / np.float32(math.log(MAX_DISTANCE / MAX_EXACT))
                         * np.float32(NUM_BUCKETS - MAX_EXACT)).astype(np.int32)
    large = np.minimum(large, NUM_BUCKETS - 1)
    bucket = np.where(n < MAX_EXACT, n, large).astype(np.int32)
    allowed = (dist >= 0) & (dist < WINDOW)
    return np.where(allowed, bucket, -1).astype(np.int32)


def bias_table(rel_bias, bucket_p, bucket_c, dep=None):
    nb, nq = rel_bias.shape

    def body(rb_ref, bkp_ref, bkc_ref, op_ref, oc_ref):
        for bk_ref, o_ref in ((bkp_ref, op_ref), (bkc_ref, oc_ref)):
            bk = bk_ref[...]
            for h in range(nq):
                acc = jnp.full(bk.shape, -jnp.inf, F32)
                for b in range(nb):
                    acc = jnp.where(bk == b, rb_ref[b, h], acc)
                o_ref[h] = acc

    return _pallas(
        body, [rel_bias, bucket_p, bucket_c], dep=dep,
        in_specs=[_SMEM, _VMEM, _VMEM], out_specs=[_VMEM, _VMEM],
        out_shape=[jax.ShapeDtypeStruct((nq,) + bucket_p.shape, F32)] * 2,
        name="bias_table")


def bias_table_bwd(dbp, dbc, bucket_p, bucket_c, dep=None):
    nq = dbp.shape[0]

    def body(dbp_ref, dbc_ref, bkp_ref, bkc_ref, o_ref):
        bkp, bkc = bkp_ref[...][None], bkc_ref[...][None]
        dp, dc = dbp_ref[...], dbc_ref[...]
        for b in range(NUM_BUCKETS):
            sel = jnp.where(bkp == b, dp, 0.0) + jnp.where(bkc == b, dc, 0.0)
            o_ref[b] = jnp.sum(jnp.sum(sel, axis=2, keepdims=True), axis=1, keepdims=True)

    return _pallas(
        body, [dbp, dbc, bucket_p, bucket_c], dep=dep,
        in_specs=[_VMEM] * 4, out_specs=_VMEM,
        out_shape=jax.ShapeDtypeStruct((NUM_BUCKETS, nq, 1, 1), F32),
        name="bias_table_bwd")


_NT = (((1,), (1,)), ((), ()))
_TN = (((0,), (0,)), ((), ()))


@jax.custom_vjp
def _bdot_nt(a, b):
    return lax.dot_general(a.astype(BF16), b.astype(BF16), _NT, preferred_element_type=F32)


def _bdot_nt_fwd(a, b):
    return _bdot_nt(a, b), (a, b)


def _bdot_nt_bwd(res, g):
    a, b = res
    gb = g.astype(BF16)
    da = jnp.dot(gb, b.astype(BF16), preferred_element_type=F32)
    db = lax.dot_general(gb, a.astype(BF16), _TN, preferred_element_type=F32)
    return da, db


_bdot_nt.defvjp(_bdot_nt_fwd, _bdot_nt_bwd)


@jax.custom_vjp
def _bdot_nn(a, b):
    return jnp.dot(a.astype(BF16), b.astype(BF16), preferred_element_type=F32)


def _bdot_nn_fwd(a, b):
    return _bdot_nn(a, b), (a, b)


def _bdot_nn_bwd(res, g):
    a, b = res
    gb = g.astype(BF16)
    da = lax.dot_general(gb, b.astype(BF16), _NT, preferred_element_type=F32)
    db = lax.dot_general(a.astype(BF16), gb, _TN, preferred_element_type=F32)
    return da, db


_bdot_nn.defvjp(_bdot_nn_fwd, _bdot_nn_bwd)


def _attn_math(q4, kp, kc, vp, vc, bp, bc, sink4, qg, kg, *, prev_ok, scale):
    g, b, hd = q4.shape
    q = q4.reshape(g * b, hd)
    qn = q * _rms(q) * qg
    kpn = kp * _rms(kp) * kg
    kcn = kc * _rms(kc) * kg
    lp = _bdot_nt(qn, kpn).reshape(g, b, b) * scale + bp
    lc = _bdot_nt(qn, kcn).reshape(g, b, b) * scale + bc
    lp = jnp.where(prev_ok, lp, -jnp.inf)
    m = jnp.maximum(jnp.maximum(jnp.max(lp, axis=-1, keepdims=True), jnp.max(lc, axis=-1, keepdims=True)), sink4)
    m = lax.stop_gradient(m)
    pp = jnp.exp(lp - m)
    pc = jnp.exp(lc - m)
    den = jnp.sum(pp, axis=-1, keepdims=True) + jnp.sum(pc, axis=-1, keepdims=True) + jnp.exp(sink4 - m)
    inv = 1.0 / den
    out = _bdot_nn((pp * inv).reshape(g * b, b), vp) + _bdot_nn((pc * inv).reshape(g * b, b), vc)
    return out.reshape(g, b, hd)


def _attn_specs(grp, hd, nblk, reverse):
    def blk(n):
        return nblk - 1 - n if reverse else n

    return [
        pl.BlockSpec((grp, BLOCK, hd), lambda h, n: (h, blk(n), 0)),
        pl.BlockSpec((None, BLOCK, hd), lambda h, n: (h, jnp.maximum(blk(n) - 1, 0), 0)),
        pl.BlockSpec((None, BLOCK, hd), lambda h, n: (h, blk(n), 0)),
        pl.BlockSpec((None, BLOCK, hd), lambda h, n: (h, jnp.maximum(blk(n) - 1, 0), 0)),
        pl.BlockSpec((None, BLOCK, hd), lambda h, n: (h, blk(n), 0)),
        pl.BlockSpec((grp, BLOCK, BLOCK), lambda h, n: (h, 0, 0)),
        pl.BlockSpec((grp, BLOCK, BLOCK), lambda h, n: (h, 0, 0)),
        pl.BlockSpec((grp, 1, 1), lambda h, n: (h, 0, 0)),
        pl.BlockSpec((1, hd), lambda h, n: (0, 0)),
        pl.BlockSpec((1, hd), lambda h, n: (0, 0)),
    ]


def attn_fwd(qh, kh, vh, bias_p, bias_c, sinks, qg, kg, dep=None):
    nq, t, hd = qh.shape
    nkv = kh.shape[0]
    grp, nblk = nq // nkv, t // BLOCK
    scale = hd ** -0.5

    def body(q_ref, kp_ref, kc_ref, vp_ref, vc_ref, bp_ref, bc_ref, s_ref, qg_ref, kg_ref, o_ref):
        prev_ok = pl.program_id(1) > 0
        out = _attn_math(q_ref[...], kp_ref[...], kc_ref[...], vp_ref[...], vc_ref[...], bp_ref[...], bc_ref[...],
                         s_ref[...], qg_ref[...], kg_ref[...], prev_ok=prev_ok, scale=scale)
        o_ref[...] = out.astype(BF16)

    return _pallas(
        body, [qh, kh, kh, vh, vh, bias_p, bias_c, sinks, qg, kg], dep=dep, grid=(nkv, nblk),
        in_specs=_attn_specs(grp, hd, nblk, False),
        out_specs=pl.BlockSpec((grp, BLOCK, hd), lambda h, n: (h, n, 0)),
        out_shape=jax.ShapeDtypeStruct((nq, t, hd), BF16),
        sem=("parallel", "parallel"), name="attn_fwd")


def attn_bwd(qh, kh, vh, bias_p, bias_c, sinks, qg, kg, doh, dep=None):
    nq, t, hd = qh.shape
    nkv = kh.shape[0]
    grp, nblk = nq // nkv, t // BLOCK
    scale = hd ** -0.5

    def body(q_ref, kp_ref, kc_ref, vp_ref, vc_ref, bp_ref, bc_ref, s_ref, qg_ref, kg_ref, do_ref,
             dq_ref, dk_ref, dv_ref, dbp_ref, dbc_ref, ds_ref, dqg_ref, dkg_ref, ck_ref, cv_ref):
        h, i = pl.program_id(0), pl.program_id(1)
        prev_ok = (nblk - 1 - i) > 0
        fn = functools.partial(_attn_math, prev_ok=prev_ok, scale=scale)
        _, vjp = jax.vjp(fn, q_ref[...], kp_ref[...], kc_ref[...], vp_ref[...], vc_ref[...], bp_ref[...], bc_ref[...],
                         s_ref[...], qg_ref[...], kg_ref[...])
        dq, dkp, dkc, dvp, dvc, dbp, dbc, dsk, dqg, dkg = vjp(do_ref[...])
        dq_ref[...] = dq

        @pl.when(i == 0)
        def _():
            ck_ref[...] = jnp.zeros_like(ck_ref)
            cv_ref[...] = jnp.zeros_like(cv_ref)
            dbp_ref[...] = jnp.zeros_like(dbp_ref)
            dbc_ref[...] = jnp.zeros_like(dbc_ref)
            ds_ref[...] = jnp.zeros_like(ds_ref)

        @pl.when((i == 0) & (h == 0))
        def _():
            dqg_ref[...] = jnp.zeros_like(dqg_ref)
            dkg_ref[...] = jnp.zeros_like(dkg_ref)

        dk_ref[...] = dkc + ck_ref[...]
        dv_ref[...] = dvc + cv_ref[...]
        ck_ref[...] = dkp
        cv_ref[...] = dvp
        dbp_ref[...] += dbp
        dbc_ref[...] += dbc
        ds_ref[...] += dsk
        dqg_ref[...] += dqg
        dkg_ref[...] += dkg

    rev = lambda h, n: (h, nblk - 1 - n, 0)
    return _pallas(
        body, [qh, kh, kh, vh, vh, bias_p, bias_c, sinks, qg, kg, doh], dep=dep, grid=(nkv, nblk),
        in_specs=_attn_specs(grp, hd, nblk, True) + [pl.BlockSpec((grp, BLOCK, hd), rev)],
        out_specs=[
            pl.BlockSpec((grp, BLOCK, hd), rev),
            pl.BlockSpec((None, BLOCK, hd), rev),
            pl.BlockSpec((None, BLOCK, hd), rev),
            pl.BlockSpec((grp, BLOCK, BLOCK), lambda h, n: (h, 0, 0)),
            pl.BlockSpec((grp, BLOCK, BLOCK), lambda h, n: (h, 0, 0)),
            pl.BlockSpec((grp, 1, 1), lambda h, n: (h, 0, 0)),
            pl.BlockSpec((1, hd), lambda h, n: (0, 0)),
            pl.BlockSpec((1, hd), lambda h, n: (0, 0)),
        ],
        out_shape=[
            jax.ShapeDtypeStruct((nq, t, hd), F32),
            jax.ShapeDtypeStruct((nkv, t, hd), F32),
            jax.ShapeDtypeStruct((nkv, t, hd), F32),
            jax.ShapeDtypeStruct((nq, BLOCK, BLOCK), F32),
            jax.ShapeDtypeStruct((nq, BLOCK, BLOCK), F32),
            jax.ShapeDtypeStruct((nq, 1, 1), F32),
            jax.ShapeDtypeStruct((1, hd), F32),
            jax.ShapeDtypeStruct((1, hd), F32),
        ],
        scratch=[pltpu.VMEM((BLOCK, hd), F32), pltpu.VMEM((BLOCK, hd), F32)],
        sem=("arbitrary", "arbitrary"), name="attn_bwd")


CONV_TILE = 256


def _conv_halo_specs(tb, ch, nblk):
    per = tb // CONV_HALO
    last = nblk * per - 1
    cur = pl.BlockSpec((tb, ch), lambda n: (n, 0))
    prev = pl.BlockSpec((CONV_HALO, ch), lambda n: (jnp.maximum(n * per - 1, 0), 0))
    nxt = pl.BlockSpec((CONV_HALO, ch), lambda n: (jnp.minimum((n + 1) * per, last), 0))
    return cur, prev, nxt


def _ln_silu(co, ln_g, ln_b):
    mu = jnp.mean(co, axis=-1, keepdims=True)
    cen = co - mu
    rstd = lax.rsqrt(jnp.mean(cen * cen, axis=-1, keepdims=True) + EPS)
    xhat = cen * rstd
    z = xhat * ln_g + ln_b
    return xhat, rstd, z


def conv_fwd(ca, cb, conv_w, conv_b, ln_g, ln_b, dep=None):
    t, ch = ca.shape
    tb = _tile(t, CONV_TILE, CONV_HALO)
    nblk = t // tb
    cur, prev, _ = _conv_halo_specs(tb, ch, nblk)
    lead = CONV_HALO - (CONV_WIDTH - 1)

    def body(ca_ref, cb_ref, cap_ref, cbp_ref, w_ref, b_ref, g_ref, bb_ref, s_ref, co_ref, ubuf):
        n = pl.program_id(0)
        halo = cap_ref[...] * _sigmoid(cbp_ref[...])
        ubuf[pl.ds(0, CONV_HALO), :] = jnp.where(n > 0, halo, 0.0)
        ubuf[pl.ds(CONV_HALO, tb), :] = ca_ref[...] * _sigmoid(cb_ref[...])
        acc = jnp.broadcast_to(b_ref[...], (tb, ch))
        for k in range(CONV_WIDTH):
            acc = acc + w_ref[pl.ds(k, 1), :] * ubuf[pl.ds(lead + k, tb), :]
        co_ref[...] = acc
        _, _, z = _ln_silu(acc, g_ref[...], bb_ref[...])
        s_ref[...] = (z * _sigmoid(z)).astype(BF16)

    vec = _full_spec((1, ch))
    return _pallas(
        body, [ca, cb, ca, cb, conv_w, conv_b, ln_g, ln_b], dep=dep, grid=(nblk,),
        in_specs=[cur, cur, prev, prev, _full_spec(conv_w.shape), vec, vec, vec],
        out_specs=[cur, cur],
        out_shape=[jax.ShapeDtypeStruct((t, ch), BF16), jax.ShapeDtypeStruct((t, ch), F32)],
        scratch=[pltpu.VMEM((CONV_HALO + tb, ch), F32)],
        sem=("parallel",), name="conv_fwd")


def conv_bwd(ca, cb, co, ds, conv_w, ln_g, ln_b, dep=None):
    t, ch = ca.shape
    tb = _tile(t, CONV_TILE, CONV_HALO)
    nblk = t // tb
    cur, prev, nxt = _conv_halo_specs(tb, ch, nblk)
    lead = CONV_HALO - (CONV_WIDTH - 1)
    ext = tb + CONV_HALO

    def body(ca_ref, cb_ref, cap_ref, cbp_ref, co_ref, con_ref, ds_ref, dsn_ref, w_ref, g_ref, bb_ref,
             dca_ref, dcb_ref, dw_ref, dvec_ref, ubuf, dbuf):
        n = pl.program_id(0)
        is_last = n == nblk - 1
        sig_b = _sigmoid(cb_ref[...])
        cav = ca_ref[...]
        ubuf[pl.ds(0, CONV_HALO), :] = jnp.where(n > 0, cap_ref[...] * _sigmoid(cbp_ref[...]), 0.0)
        ubuf[pl.ds(CONV_HALO, tb), :] = cav * sig_b
        co = jnp.concatenate([co_ref[...], con_ref[...]], axis=0)
        xhat, rstd, z = _ln_silu(co, g_ref[...], bb_ref[...])
        dsv = jnp.concatenate([ds_ref[...], jnp.where(is_last, 0.0, dsn_ref[...])], axis=0)
        sg = _sigmoid(z)
        dz = dsv * (sg * (1.0 + z * (1.0 - sg)))
        dxh = dz * g_ref[...]
        dco = rstd * (dxh - jnp.mean(dxh, axis=-1, keepdims=True)
                      - xhat * jnp.mean(dxh * xhat, axis=-1, keepdims=True))
        dbuf[...] = dco

        @pl.when(n == 0)
        def _():
            dw_ref[...] = jnp.zeros_like(dw_ref)
            dvec_ref[...] = jnp.zeros_like(dvec_ref)

        dco_cur = dco[:tb]
        dvec_ref[pl.ds(0, 1), :] += jnp.sum(dco_cur, axis=0, keepdims=True)
        dvec_ref[pl.ds(1, 1), :] += jnp.sum(dz[:tb] * xhat[:tb], axis=0, keepdims=True)
        dvec_ref[pl.ds(2, 1), :] += jnp.sum(dz[:tb], axis=0, keepdims=True)
        du = jnp.zeros((tb, ch), F32)
        for k in range(CONV_WIDTH):
            du = du + w_ref[pl.ds(k, 1), :] * dbuf[pl.ds(CONV_WIDTH - 1 - k, tb), :]
            dw_ref[pl.ds(k, 1), :] += jnp.sum(dco_cur * ubuf[pl.ds(lead + k, tb), :], axis=0, keepdims=True)
        dca_ref[...] = (du * sig_b).astype(BF16)
        dcb_ref[...] = (du * cav * sig_b * (1.0 - sig_b)).astype(BF16)

    vec = _full_spec((1, ch))
    return _pallas(
        body, [ca, cb, ca, cb, co, co, ds, ds, conv_w, ln_g, ln_b], dep=dep, grid=(nblk,),
        in_specs=[cur, cur, prev, prev, cur, nxt, cur, nxt, _full_spec(conv_w.shape), vec, vec],
        out_specs=[cur, cur, _full_spec(conv_w.shape), _full_spec((SUBLANES, ch))],
        out_shape=[jax.ShapeDtypeStruct((t, ch), BF16), jax.ShapeDtypeStruct((t, ch), BF16),
                   jax.ShapeDtypeStruct(conv_w.shape, F32), jax.ShapeDtypeStruct((SUBLANES, ch), F32)],
        scratch=[pltpu.VMEM((CONV_HALO + tb, ch), F32), pltpu.VMEM((ext, ch), F32)],
        sem=("arbitrary",), name="conv_bwd")


def ada_fwd(c_t, w_ada, dep=None):
    d, nc = w_ada.shape
    nex = c_t.shape[1]
    tn = _tile(nc, 512)

    def body(ct_ref, w_ref, o_ref):
        w = w_ref[...]
        ct = ct_ref[...]
        cact = ct * _sigmoid(ct)
        rows = [jnp.sum(w * cact[:, b:b + 1], axis=0, keepdims=True) for b in range(nex)]
        o_ref[...] = jnp.concatenate(rows, axis=0)

    return _pallas(
        body, [c_t, w_ada], dep=dep, grid=(nc // tn,),
        in_specs=[_full_spec(c_t.shape), pl.BlockSpec((d, tn), lambda j: (0, j))],
        out_specs=pl.BlockSpec((nex, tn), lambda j: (0, j)),
        out_shape=jax.ShapeDtypeStruct((nex, nc), F32),
        sem=("parallel",), name="ada_fwd")


def _adamw_math(w, g, m, v):
    m = ADAM_B1 * m + (1.0 - ADAM_B1) * g
    v = ADAM_B2 * v + (1.0 - ADAM_B2) * (g * g)
    m_hat = m / (1.0 - ADAM_B1 ** ADAM_STEP)
    v_hat = v / (1.0 - ADAM_B2 ** ADAM_STEP)
    delta = -ADAM_LR * (m_hat / (jnp.sqrt(v_hat) + ADAM_EPS) + ADAM_WD * w)
    return delta, m, v


def adamw(w, g, m, v, name, dep=None):
    r, n = w.shape
    tr, tn = _ew_tiles(r, n, elems=256 * 1024)

    def body(w_ref, g_ref, m_ref, v_ref, d_ref, nm_ref, nv_ref):
        d_ref[...], nm_ref[...], nv_ref[...] = _adamw_math(w_ref[...], g_ref[...], m_ref[...], v_ref[...])

    blk = pl.BlockSpec((tr, tn), lambda i, j: (i, j))
    return _pallas(
        body, [w, g, m, v], dep=dep, grid=(r // tr, n // tn),
        in_specs=[blk] * 4, out_specs=[blk] * 3,
        out_shape=[jax.ShapeDtypeStruct((r, n), F32)] * 3,
        sem=("parallel", "parallel"), name=name)


def ada_grad_adamw(c_t, dmod_cols, w, m, v, dep=None):
    d, nc = w.shape
    nex = c_t.shape[1]
    tr, tn = _ew_tiles(d, nc, elems=256 * 1024)

    def body(ct_ref, dm_ref, w_ref, m_ref, v_ref, g_ref, d_ref, nm_ref, nv_ref):
        ct = ct_ref[...]
        cact = ct * _sigmoid(ct)
        dm = dm_ref[...]
        g = cact[:, 0:1] * dm[0:1, :]
        for b in range(1, nex):
            g = g + cact[:, b:b + 1] * dm[b:b + 1, :]
        g_ref[...] = g
        d_ref[...], nm_ref[...], nv_ref[...] = _adamw_math(w_ref[...], g, m_ref[...], v_ref[...])

    blk = pl.BlockSpec((tr, tn), lambda i, j: (i, j))
    return _pallas(
        body, [c_t, dmod_cols, w, m, v], dep=dep, grid=(d // tr, nc // tn),
        in_specs=[pl.BlockSpec((tr, nex), lambda i, j: (i, 0)), pl.BlockSpec((nex, tn), lambda i, j: (0, j)),
                  blk, blk, blk],
        out_specs=[blk] * 4,
        out_shape=[jax.ShapeDtypeStruct((d, nc), F32)] * 4,
        sem=("parallel", "parallel"), name="ada_grad_adamw")


def small_sum_adamw(gathered, w, m, v, dep=None):
    ndev, r, n = gathered.shape

    def body(ga_ref, w_ref, m_ref, v_ref, g_ref, d_ref, nm_ref, nv_ref):
        g = ga_ref[0]
        for s in range(1, ndev):
            g = g + ga_ref[s]
        g_ref[...] = g
        d_ref[...], nm_ref[...], nv_ref[...] = _adamw_math(w_ref[...], g, m_ref[...], v_ref[...])

    return _pallas(
        body, [gathered, w, m, v], dep=dep, in_specs=[_VMEM] * 4, out_specs=[_VMEM] * 4,
        out_shape=[jax.ShapeDtypeStruct((r, n), F32)] * 4,
        name="small_sum_adamw")


def _position():
    return lax.axis_index("x"), lax.axis_index("y"), lax.axis_index("c")


def _other_chips(x, y):
    return [(1 - x, y), (x, 1 - y), (1 - x, 1 - y)]


def allgather_small(block, name, dep=None):
    rows, width = block.shape

    def body(x_ref, out_ref, send_sems, recv_sems, local_sem):
        x, y, c = _position()
        me, sibling = (x, y, c), (x, y, 1 - c)
        chips = _other_chips(x, y)

        def slot(px, py, pc):
            return out_ref.at[4 * px + 2 * py + pc]

        def copy(k, block_of, to, src=None):
            return pltpu.make_async_remote_copy(
                src_ref=slot(*block_of) if src is None else src, dst_ref=slot(*block_of),
                send_sem=send_sems.at[k], recv_sem=recv_sems.at[k], device_id=to, device_id_type=MESH)

        mine = pltpu.make_async_copy(x_ref, slot(*me), local_sem)
        mine.start()
        first = [copy(0, me, sibling, src=x_ref)]
        first += [copy(1 + j, me, (*chip, c), src=x_ref) for j, chip in enumerate(chips)]
        for cp in first:
            cp.start()
        passed = [copy(4 + j, (*chip, c), sibling) for j, chip in enumerate(chips)]
        for j, chip in enumerate(chips):
            copy(1 + j, (*chip, c), me).wait_recv()
            passed[j].start()
        copy(0, sibling, me).wait_recv()
        for j, chip in enumerate(chips):
            copy(4 + j, (*chip, 1 - c), me).wait_recv()
        for cp in first + passed:
            cp.wait_send()
        mine.wait()

    return _pallas(
        body, [block], dep=dep,
        out_shape=jax.ShapeDtypeStruct((N_DEV, rows, width), block.dtype),
        in_specs=[_VMEM], out_specs=_VMEM,
        scratch=[pltpu.SemaphoreType.DMA((7,)), pltpu.SemaphoreType.DMA((7,)), pltpu.SemaphoreType.DMA],
        name=name)


class Started(NamedTuple):
    send_sems: Any
    recv_sems: Any
    bufs: list


def exchange_start(name, bufs, n_copies, plan, dep=None):
    nb = len(bufs)

    def body(*refs):
        for cp in plan(refs[:nb], refs[nb], refs[nb + 1]):
            cp.start()

    outs = _pallas(
        body, [pltpu.with_memory_space_constraint(b, pltpu.HBM) for b in bufs], dep=dep, name=name,
        out_shape=(pltpu.SemaphoreType.DMA((n_copies,)), pltpu.SemaphoreType.DMA((n_copies,)),
                   *[pltpu.HBM(b.shape, b.dtype) for b in bufs]),
        in_specs=[_HBM] * nb,
        out_specs=(_SEM, _SEM, *[_HBM] * nb),
        input_output_aliases={i: 2 + i for i in range(nb)},
        compiler_params=pltpu.CompilerParams(has_side_effects=_EFFECT))
    return Started(outs[0], outs[1], list(outs[2:2 + nb]))


def exchange_wait(name, started, plan, dep=None):
    nb = len(started.bufs)

    def body(*refs):
        for cp in plan(refs[:nb], refs[nb], refs[nb + 1]):
            cp.wait_send()
            cp.wait_recv()

    outs = _pallas(
        body, [*started.bufs, started.send_sems, started.recv_sems], dep=dep, name=name,
        out_shape=tuple(pltpu.HBM(b.shape, b.dtype) for b in started.bufs),
        in_specs=[_HBM] * nb + [_SEM, _SEM],
        out_specs=tuple([_HBM] * nb),
        input_output_aliases={i: i for i in range(nb)},
        compiler_params=pltpu.CompilerParams(has_side_effects=_EFFECT))
    return list(outs)


def _remote(src, dst, send_sems, recv_sems, i, to):
    return pltpu.make_async_remote_copy(src_ref=src, dst_ref=dst, send_sem=send_sems.at[i], recv_sem=recv_sems.at[i],
                                        device_id=to, device_id_type=MESH)


def _half_rows(buf_rows, chip_idx, pc):
    half = buf_rows // (2 * N_CHIPS)
    return pl.ds((2 * chip_idx + pc) * half, half)


def plan_gather_ici(refs, send_sems, recv_sems):
    x, y, c = _position()
    copies = []
    for k, ref in enumerate(refs):
        rows = ref.at[_half_rows(ref.shape[0], 2 * x + y, c), :]
        for j, chip in enumerate(_other_chips(x, y)):
            copies.append(_remote(rows, rows, send_sems, recv_sems, 3 * k + j, (*chip, c)))
    return copies


def plan_gather_d2d(refs, send_sems, recv_sems):
    x, y, c = _position()
    copies = []
    for k, ref in enumerate(refs):
        for j, (px, py) in enumerate(_other_chips(x, y)):
            rows = ref.at[_half_rows(ref.shape[0], 2 * px + py, c), :]
            copies.append(_remote(rows, rows, send_sems, recv_sems, 3 * k + j, (x, y, 1 - c)))
    return copies


def plan_pair_exchange(refs, send_sems, recv_sems):
    x, y, c = _position()
    nw = len(refs) // 2
    copies = []
    for k in range(nw):
        for chip in range(N_CHIPS):
            copies.append(_remote(refs[k].at[chip, 1 - c], refs[nw + k].at[chip], send_sems, recv_sems,
                                  N_CHIPS * k + chip, (x, y, 1 - c)))
    return copies


def plan_chip_exchange(refs, send_sems, recv_sems):
    x, y, c = _position()
    nw = len(refs) // 2
    copies = []
    for k in range(nw):
        for j, (px, py) in enumerate(_other_chips(x, y)):
            copies.append(_remote(refs[k].at[2 * px + py], refs[nw + k].at[2 * x + y], send_sems, recv_sems,
                                  3 * k + j, (px, py, c)))
    return copies


def plan_pair_share(refs, send_sems, recv_sems):
    x, y, c = _position()
    return [_remote(ref.at[c], ref.at[c], send_sems, recv_sems, k, (x, y, 1 - c)) for k, ref in enumerate(refs)]


def cast_into_slot(src, slot, n_slots, name, dep=None):
    r, n = src.shape
    tr, tn = _ew_tiles(r, n, BF16_SUBLANES)

    def body(slot_ref, s_ref, o_ref):
        o_ref[...] = s_ref[...].astype(BF16)

    return _pallas(
        body, [slot, src], dep=dep, n_prefetch=1, grid=(r // tr, n // tn),
        in_specs=[pl.BlockSpec((tr, tn), lambda i, j, sl: (i, j))],
        out_specs=pl.BlockSpec((None, tr, tn), lambda i, j, sl: (sl[0], i, j)),
        out_shape=jax.ShapeDtypeStruct((n_slots, r, n), BF16),
        sem=("parallel", "parallel"), name=name)


def pair_sum(g, r, core, name, dep=None):
    nchip, _, h, n = g.shape
    th, tn = _ew_tiles(h, n, BF16_SUBLANES)

    def body(core_ref, g_ref, r_ref, o_ref):
        o_ref[...] = (g_ref[...].astype(F32) + r_ref[...].astype(F32)).astype(BF16)

    return _pallas(
        body, [core, g, r], dep=dep, n_prefetch=1, grid=(nchip, h // th, n // tn),
        in_specs=[pl.BlockSpec((None, None, th, tn), lambda a, i, j, cr: (a, cr[0], i, j)),
                  pl.BlockSpec((None, th, tn), lambda a, i, j, cr: (a, i, j))],
        out_specs=pl.BlockSpec((None, th, tn), lambda a, i, j, cr: (a, i, j)),
        out_shape=jax.ShapeDtypeStruct((nchip, h, n), BF16),
        sem=("parallel", "parallel", "parallel"), name=name)


def chip_sum(own, got, where, name, dep=None):
    nchip, h, n = got.shape
    th, tn = _ew_tiles(h, n, BF16_SUBLANES, elems=256 * 1024)

    def body(where_ref, own_ref, *rest):
        got_refs, o_ref = rest[:nchip], rest[nchip]
        chip = where_ref[0]
        acc = None
        for s in range(nchip):
            term = jnp.where(chip == s, own_ref[...], got_refs[s][...]).astype(F32)
            acc = term if acc is None else acc + term
        o_ref[...] = acc

    def got_spec(s):
        return pl.BlockSpec((None, th, tn), lambda i, j, wr: (jnp.where(wr[0] == s, (s + 1) % nchip, s), i, j))

    return _pallas(
        body, [where, own, *[got] * nchip], dep=dep, n_prefetch=1, grid=(h // th, n // tn),
        in_specs=[pl.BlockSpec((None, th, tn), lambda i, j, wr: (wr[0], i, j))]
        + [got_spec(s) for s in range(nchip)],
        out_specs=pl.BlockSpec((None, th, tn), lambda i, j, wr: (wr[1], i, j)),
        out_shape=jax.ShapeDtypeStruct((2, h, n), F32),
        sem=("parallel", "parallel"), name=name)


PACK_ROWS = SUBLANES


def _part_rows(shape):
    size = int(np.prod(shape))
    return -(-size // (PACK_ROWS * LANES)) * PACK_ROWS


def _pack_rows(parts):
    rows = []
    for p in parts:
        flat = p.reshape(-1)
        flat = jnp.pad(flat, (0, _part_rows(p.shape) * LANES - flat.shape[0]))
        rows.append(flat.reshape(-1, LANES))
    return jnp.concatenate(rows, axis=0)


def _unpack_rows(packed, shapes):
    out, row = [], 0
    for s in shapes:
        size, nrows = int(np.prod(s)), _part_rows(s)
        out.append(packed[row:row + nrows].reshape(-1)[:size].reshape(s))
        row += nrows
    return out


def kernel(x, c, w_ada, b_ada, norm_mix_g, w_in, q_norm_g, k_norm_g, attn_sinks, rel_bias, w_attn_out, conv_w, conv_b, conv_ln_g, conv_ln_b, w_conv_out, w_mix_out, norm_ffn_g, w_ffn_in, w_ffn_out, loss_target, m_w_ada, m_b_ada, m_norm_mix_g, m_w_in, m_q_norm_g, m_k_norm_g, m_attn_sinks, m_rel_bias, m_w_attn_out, m_conv_w, m_conv_b, m_conv_ln_g, m_conv_ln_b, m_w_conv_out, m_w_mix_out, m_norm_ffn_g, m_w_ffn_in, m_w_ffn_out, v_w_ada, v_b_ada, v_norm_mix_g, v_w_in, v_q_norm_g, v_k_norm_g, v_attn_sinks, v_rel_bias, v_w_attn_out, v_conv_w, v_conv_b, v_conv_ln_g, v_conv_ln_b, v_w_conv_out, v_w_mix_out, v_norm_ffn_g, v_w_ffn_in, v_w_ffn_out):
    run = InOrder()
    xi, yi, ci = _position()
    chip = 2 * xi + yi
    me = 2 * chip + ci
    chip_arr = chip.astype(jnp.int32).reshape(1)
    core_arr = ci.astype(jnp.int32).reshape(1)
    where_arr = jnp.stack([chip, ci]).astype(jnp.int32)

    xe, tgt = x[0], loss_target[0]
    t, d = xe.shape
    hd = q_norm_g.shape[-1]
    nq = attn_sinks.shape[-1]
    aw = nq * hd
    ch = conv_b.shape[-1]
    in_width = N_CHIPS * w_in.shape[-1]
    kvw = (in_width - aw - 2 * ch - 2 * d) // 2
    nkv = kvw // hd
    dff = N_CHIPS * w_ffn_out.shape[1]
    off_k, off_v, off_ca = aw, aw + kvw, aw + 2 * kvw
    off_cb, off_ga, off_gc = off_ca + ch, off_ca + 2 * ch, off_ca + 2 * ch + d
    nc_ada = w_ada.shape[-1]
    ch_loc = conv_w.shape[-1]
    nj_ffn = w_ffn_in.shape[-1]
    perm_ffn = ffn_perm(N_CHIPS)

    big = {"w_in": w_in[0], "w_attn_out": w_attn_out[0], "w_conv_out": w_conv_out[0], "w_mix_out": w_mix_out[0],
           "w_ffn_in": w_ffn_in[0], "w_ffn_out": w_ffn_out[0]}
    moments = {"w_in": (m_w_in, v_w_in), "w_attn_out": (m_w_attn_out, v_w_attn_out),
               "w_conv_out": (m_w_conv_out, v_w_conv_out), "w_mix_out": (m_w_mix_out, v_w_mix_out),
               "w_ffn_in": (m_w_ffn_in, v_w_ffn_in), "w_ffn_out": (m_w_ffn_out, v_w_ffn_out)}
    gather_groups = {"in": ["w_in"], "mid": ["w_attn_out", "w_conv_out", "w_mix_out"], "ffn_in": ["w_ffn_in"],
                     "ffn_out": ["w_ffn_out"]}
    grads, deltas, new_m, new_v = {}, {}, {}, {}

    def gather_cast(gname):
        bufs = []
        for n in gather_groups[gname]:
            r, ncol = big[n].shape
            bufs.append(run(cast_into_slot, big[n], chip_arr, N_CHIPS, "cast_" + n).reshape(N_CHIPS * r, ncol))
        return bufs

    def gather_ici_start(gname, bufs):
        return run(exchange_start, "gather_ici_start_" + gname, bufs, 3 * len(bufs), plan_gather_ici)

    def gather_pass_on(gname, ici):
        landed = run(exchange_wait, "gather_ici_wait_" + gname, ici, plan_gather_ici)
        return run(exchange_start, "gather_d2d_start_" + gname, landed, 3 * len(landed), plan_gather_d2d)

    def gathered(gname, d2d):
        outs = run(exchange_wait, "gather_d2d_wait_" + gname, d2d, plan_gather_d2d)
        return [o.reshape(N_CHIPS, *big[n].shape) for o, n in zip(outs, gather_groups[gname])]

    def rs_pair_start(gname, names, partials):
        blocks = [g.reshape(N_CHIPS, 2, big[n].shape[0] // 2, big[n].shape[1]) for n, g in zip(names, partials)]
        land = [lax.empty((N_CHIPS,) + b.shape[2:], BF16) for b in blocks]
        return run(exchange_start, "pair_exchange_start_" + gname, blocks + land, N_CHIPS * len(blocks),
                   plan_pair_exchange)

    def rs_chip_start(gname, names, pair):
        nw = len(names)
        outs = run(exchange_wait, "pair_exchange_wait_" + gname, pair, plan_pair_exchange)
        sums = [run(pair_sum, g, r, core_arr, "pair_sum_" + n) for n, g, r in zip(names, outs[:nw], outs[nw:])]
        land = [lax.empty(s.shape, BF16) for s in sums]
        return run(exchange_start, "chip_exchange_start_" + gname, sums + land, 3 * nw, plan_chip_exchange)

    def rs_share_start(gname, names, chipx):
        nw = len(names)
        outs = run(exchange_wait, "chip_exchange_wait_" + gname, chipx, plan_chip_exchange)
        halves = [run(chip_sum, s, r, where_arr, "chip_sum_" + n) for n, s, r in zip(names, outs[:nw], outs[nw:])]
        return run(exchange_start, "pair_share_start_" + gname, halves, nw, plan_pair_share)

    def rs_finish(gname, names, share):
        fulls = run(exchange_wait, "pair_share_wait_" + gname, share, plan_pair_share)
        for n, g2 in zip(names, fulls):
            g = g2.reshape(big[n].shape)
            dl, nm, nv = run(adamw, big[n], g, moments[n][0][0], moments[n][1][0], "adamw_" + n)
            grads[n], deltas[n], new_m[n], new_v[n] = g[None], dl[None], nm[None], nv[None]

    bufs_in = gather_cast("in")
    got1 = run(allgather_small, _pack_rows([c[0], conv_w[0]]), "allgather_cond")
    c_rows = _part_rows((d,))
    c_all = got1[:, :c_rows].reshape(N_DEV, -1)[:, :d]
    w_rows = _part_rows((CONV_WIDTH, ch_loc))
    conv_w_full = got1[0::2, c_rows:c_rows + w_rows].reshape(N_CHIPS, -1)[:, :CONV_WIDTH * ch_loc]
    conv_w_full = jnp.transpose(conv_w_full.reshape(N_CHIPS, CONV_WIDTH, ch_loc), (1, 0, 2)).reshape(CONV_WIDTH, ch)
    conv_w_pad = jnp.pad(conv_w_full, ((0, 1), (0, 0)))
    c_t = jnp.transpose(c_all)
    mod_cols = run(ada_fwd, c_t, w_ada[0])
    got2 = run(allgather_small, mod_cols.reshape(-1, LANES), "allgather_mod")
    mod_all = got2.reshape(N_CHIPS, 2, N_DEV, nc_ada)[:, 0]
    mod = lax.dynamic_slice_in_dim(mod_all, me, 1, axis=1).reshape(1, N_CHIPS * nc_ada) + b_ada
    mod = jnp.pad(mod.reshape(N_MOD, d), ((0, SUBLANES - N_MOD), (0, 0)))

    ici = {"in": gather_ici_start("in", bufs_in)}
    for gname in ("mid", "ffn_in", "ffn_out"):
        ici[gname] = gather_ici_start(gname, gather_cast(gname))

    h = run(pre_mix_fwd, xe, mod, norm_mix_g)
    bucket = _t5_bucket_table()
    bucket_p, bucket_c = jnp.asarray(bucket[:, :BLOCK]), jnp.asarray(bucket[:, BLOCK:])
    bias_p, bias_c = run(bias_table, rel_bias, bucket_p, bucket_c)
    d2d_in = gather_pass_on("in", ici["in"])
    (wg_in,) = gathered("in", d2d_in)
    p = run(mm_nn, h, wg_in, tn=_tile(wg_in.shape[2], 640), tk=d, out_dtype=F32, name="mm_in")
    d2d_mid = gather_pass_on("mid", ici["mid"])

    def heads(a, n):
        return jnp.transpose(a.reshape(t, n, hd), (1, 0, 2))

    def unheads(a):
        return jnp.transpose(a, (1, 0, 2)).reshape(t, -1)

    qh, kh, vh = heads(p[:, :aw], nq), heads(p[:, off_k:off_v], nkv), heads(p[:, off_v:off_ca], nkv)
    sinks3 = attn_sinks.reshape(nq, 1, 1)
    attn_o = unheads(run(attn_fwd, qh, kh, vh, bias_p, bias_c, sinks3, q_norm_g, k_norm_g))
    ca, cb = p[:, off_ca:off_cb], p[:, off_cb:off_ga]
    s_conv, co_conv = run(conv_fwd, ca, cb, conv_w_pad, conv_b, conv_ln_g, conv_ln_b)
    wg_attn_out, wg_conv_out, wg_mix_out = gathered("mid", d2d_mid)
    wg_mix_out = wg_mix_out.reshape(1, d, d)
    y_attn = run(mm_nn, attn_o, wg_attn_out, tn=_tile(wg_attn_out.shape[2], 512), tk=aw, out_dtype=F32,
                 name="mm_attn_out")
    y_conv = run(mm_nn, s_conv, wg_conv_out, tn=_tile(wg_conv_out.shape[2], 512), tk=ch, out_dtype=F32,
                 name="mm_conv_out")
    merged = run(merge_fwd, p, y_attn, y_conv, off_ga, off_gc)
    d2d_ffn_in = gather_pass_on("ffn_in", ici["ffn_in"])
    o_m = run(mm_nn, merged, wg_mix_out, tn=_tile(d, 512), tk=d, out_dtype=F32, name="mm_mix_out")
    x1, h2 = run(pre_ffn_fwd, xe, o_m, mod, norm_ffn_g)
    (wg_ffn_in,) = gathered("ffn_in", d2d_ffn_in)
    f = run(mm_nn, h2, wg_ffn_in, tn=_tile(nj_ffn, 1408), tk=d, out_dtype=BF16, name="mm_ffn_in", perm=perm_ffn)
    d2d_ffn_out = gather_pass_on("ffn_out", ici["ffn_out"])
    act = run(swiglu_fwd, f, nj_ffn)
    (wg_ffn_out,) = gathered("ffn_out", d2d_ffn_out)
    wg_ffn_out = wg_ffn_out.reshape(1, dff, d)
    o_f = run(mm_nn, act, wg_ffn_out, tn=_tile(d, 1024), tk=_tile(dff, 1408), out_dtype=F32, name="mm_ffn_out")
    loss11, dy, dof, acc_l = run(loss_head, x1, o_f, tgt, mod)

    gw_ffn_out = run(mm_tn, act, dof, 1, tk=_tile(dff, 512), tn=d, name="mm_ffn_out_dw")
    px_ffn_out = rs_pair_start("ffn_out", ["w_ffn_out"], [gw_ffn_out])
    dact = run(mm_nt, dof, wg_ffn_out, tko=_tile(dff, 512), tn=d, name="mm_ffn_out_dx")
    cx_ffn_out = rs_chip_start("ffn_out", ["w_ffn_out"], px_ffn_out)
    df = run(swiglu_bwd, f, dact, nj_ffn)
    gw_ffn_in = run(mm_tn, h2, df, N_CHIPS, tk=d, tn=_tile(nj_ffn, 1408), name="mm_ffn_in_dw",
                    perm=perm_ffn)
    px_ffn_in = rs_pair_start("ffn_in", ["w_ffn_in"], [gw_ffn_in])
    dh2 = run(mm_nt, df, wg_ffn_in, tko=_tile(d, 512), tn=nj_ffn, name="mm_ffn_in_dx", perm=perm_ffn)
    sh_ffn_out = rs_share_start("ffn_out", ["w_ffn_out"], cx_ffn_out)
    cx_ffn_in = rs_chip_start("ffn_in", ["w_ffn_in"], px_ffn_in)
    dx1, dom, acc_f = run(pre_ffn_bwd, x1, dh2, dy, o_m, mod, norm_ffn_g)
    gw_mix_out = run(mm_tn, merged, dom, 1, tk=d, tn=_tile(d, 1024), name="mm_mix_out_dw")
    px_mix = rs_pair_start("mix_out", ["w_mix_out"], [gw_mix_out])
    dmerged = run(mm_nt, dom, wg_mix_out, tko=_tile(d, 512), tn=d, name="mm_mix_out_dx")
    dy_attn, dy_conv, dga, dgc = run(merge_bwd, p, y_attn, y_conv, dmerged, off_ga, off_gc)
    rs_finish("ffn_out", ["w_ffn_out"], sh_ffn_out)
    cx_mix = rs_chip_start("mix_out", ["w_mix_out"], px_mix)
    gw_attn_out = run(mm_tn, attn_o, dy_attn, N_CHIPS, tk=aw, tn=_tile(wg_attn_out.shape[2], 512),
                      name="mm_attn_out_dw")
    gw_conv_out = run(mm_tn, s_conv, dy_conv, N_CHIPS, tk=ch, tn=_tile(wg_conv_out.shape[2], 512),
                      name="mm_conv_out_dw")
    ac_names = ["w_attn_out", "w_conv_out"]
    px_ac = rs_pair_start("attn_conv_out", ac_names, [gw_attn_out, gw_conv_out])
    dattn_o = run(mm_nt, dy_attn, wg_attn_out, tko=_tile(aw, 1024), tn=_tile(wg_attn_out.shape[2], 512),
                  name="mm_attn_out_dx")
    ds_conv = run(mm_nt, dy_conv, wg_conv_out, tko=_tile(ch, 1024), tn=_tile(wg_conv_out.shape[2], 512),
                  name="mm_conv_out_dx")
    cx_ac = rs_chip_start("attn_conv_out", ac_names, px_ac)
    dca, dcb, dconv_w, dconv_vec = run(conv_bwd, ca, cb, co_conv, ds_conv, conv_w_pad, conv_ln_g, conv_ln_b)
    sh_ffn_in = rs_share_start("ffn_in", ["w_ffn_in"], cx_ffn_in)
    dqh, dkh, dvh, dbp, dbc, dsinks, dqg, dkg = run(attn_bwd, qh, kh, vh, bias_p, bias_c, sinks3, q_norm_g, k_norm_g,
                                                     heads(dattn_o, nq))
    sh_mix = rs_share_start("mix_out", ["w_mix_out"], cx_mix)
    sh_ac = rs_share_start("attn_conv_out", ac_names, cx_ac)
    drel = run(bias_table_bwd, dbp, dbc, bucket_p, bucket_c).reshape(NUM_BUCKETS, nq)
    dp = jnp.concatenate([unheads(dqh).astype(BF16), unheads(dkh).astype(BF16), unheads(dvh).astype(BF16),
                          dca, dcb, dga, dgc], axis=1)
    gw_in = run(mm_tn, h, dp, N_CHIPS, tk=d, tn=_tile(wg_in.shape[2], 640), name="mm_in_dw")
    px_in = rs_pair_start("in", ["w_in"], [gw_in])
    dh = run(mm_nt, dp, wg_in, tko=_tile(d, 1024), tn=wg_in.shape[2], name="mm_in_dx")
    grad_x, acc_m = run(pre_mix_bwd, xe, dh, dx1, mod, norm_mix_g)

    dmod = jnp.concatenate([acc_m[0:1], acc_m[1:2], acc_f[3:4], acc_f[0:1], acc_f[1:2], acc_l[0:1]], axis=1)
    small_names = ["b_ada", "norm_mix_g", "q_norm_g", "k_norm_g", "attn_sinks", "rel_bias", "conv_b", "conv_ln_g",
                   "conv_ln_b", "norm_ffn_g"]
    small_w = [b_ada, norm_mix_g, q_norm_g, k_norm_g, attn_sinks, rel_bias, conv_b, conv_ln_g, conv_ln_b, norm_ffn_g]
    small_m = [m_b_ada, m_norm_mix_g, m_q_norm_g, m_k_norm_g, m_attn_sinks, m_rel_bias, m_conv_b, m_conv_ln_g,
               m_conv_ln_b, m_norm_ffn_g]
    small_v = [v_b_ada, v_norm_mix_g, v_q_norm_g, v_k_norm_g, v_attn_sinks, v_rel_bias, v_conv_b, v_conv_ln_g,
               v_conv_ln_b, v_norm_ffn_g]
    small_g = [dmod, acc_m[2:3], dqg, dkg, dsinks.reshape(1, nq), drel, dconv_vec[0:1], dconv_vec[1:2],
               dconv_vec[2:3], acc_f[2:3]]
    small_shapes = [w.shape for w in small_w]
    conv_shape = (CONV_WIDTH, ch)
    got3 = run(allgather_small, _pack_rows(small_g + [dconv_w[:CONV_WIDTH]]), "allgather_small_grads")
    cx_in = rs_chip_start("in", ["w_in"], px_in)
    zero_conv = jnp.zeros(conv_shape, F32)
    g_small, d_small, nm_small, nv_small = run(
        small_sum_adamw, got3, _pack_rows(small_w + [zero_conv]), _pack_rows(small_m + [zero_conv]),
        _pack_rows(small_v + [zero_conv]))
    g_parts = _unpack_rows(g_small, small_shapes + [conv_shape])
    grads.update(zip(small_names, g_parts[:-1]))
    deltas.update(zip(small_names, _unpack_rows(d_small, small_shapes)))
    new_m.update(zip(small_names, _unpack_rows(nm_small, small_shapes)))
    new_v.update(zip(small_names, _unpack_rows(nv_small, small_shapes)))

    g_conv_w = lax.dynamic_slice_in_dim(g_parts[-1], chip * ch_loc, ch_loc, axis=1)
    grads["conv_w"] = g_conv_w[None]
    dl, nm, nv = run(adamw, conv_w[0], g_conv_w, m_conv_w[0], v_conv_w[0], "adamw_conv_w")
    deltas["conv_w"], new_m["conv_w"], new_v["conv_w"] = dl[None], nm[None], nv[None]

    dmod_all = got3[:, :_part_rows((N_MOD * d,))].reshape(N_DEV, -1)[:, :N_MOD * d]
    dmod_cols = lax.dynamic_slice_in_dim(dmod_all, chip * nc_ada, nc_ada, axis=1)
    g_ada, dl, nm, nv = run(ada_grad_adamw, c_t, dmod_cols, w_ada[0], m_w_ada[0], v_w_ada[0])
    grads["w_ada"], deltas["w_ada"], new_m["w_ada"], new_v["w_ada"] = g_ada[None], dl[None], nm[None], nv[None]

    rs_finish("ffn_in", ["w_ffn_in"], sh_ffn_in)
    rs_finish("mix_out", ["w_mix_out"], sh_mix)
    rs_finish("attn_conv_out", ac_names, sh_ac)
    sh_in = rs_share_start("in", ["w_in"], cx_in)
    rs_finish("in", ["w_in"], sh_in)

    loss = lax.psum(loss11[0, 0], ("x", "y", "c"))
    order = ["w_ada", "b_ada", "norm_mix_g", "w_in", "q_norm_g", "k_norm_g", "attn_sinks", "rel_bias", "w_attn_out",
             "conv_w", "conv_b", "conv_ln_g", "conv_ln_b", "w_conv_out", "w_mix_out", "norm_ffn_g", "w_ffn_in",
             "w_ffn_out"]
    return (loss, grad_x[None], *[grads[n] for n in order], *[deltas[n] for n in order],
            *[new_m[n] for n in order], *[new_v[n] for n in order])
```

```python
import functools
import math
from typing import Any, NamedTuple

import jax
import jax.numpy as jnp
import numpy as np
from jax import lax
from jax.experimental import pallas as pl
from jax.experimental.pallas import tpu as pltpu

F32 = jnp.float32
BF16 = jnp.bfloat16
MESH = pl.DeviceIdType.MESH

V7X_VMEM_BYTES = 64 * 1024 * 1024
VMEM_LIMIT = V7X_VMEM_BYTES - 8 * 1024 * 1024
LANES = 128
SUBLANES = 8
BF16_SUBLANES = 16

EPS = 1e-6
WINDOW = 128
BLOCK = 128
NUM_BUCKETS = 32
MAX_EXACT = NUM_BUCKETS // 2
MAX_DISTANCE = 128
CONV_WIDTH = 31
CONV_HALO = 32
ADAM_LR = 0.001
ADAM_B1 = 0.9
ADAM_B2 = 0.999
ADAM_EPS = 1e-08
ADAM_WD = 0.01
ADAM_STEP = 10
N_MOD = 6
SH_M, SC_M, GT_M, SH_F, SC_F, GT_F = range(6)

N_CHIPS = 4
N_DEV = 8

_ANY = pl.BlockSpec(memory_space=pl.ANY)
_VMEM = pl.BlockSpec(memory_space=pltpu.VMEM)
_SMEM = pl.BlockSpec(memory_space=pltpu.SMEM)
_HBM = pl.BlockSpec(memory_space=pltpu.HBM)
_SEM = pl.BlockSpec(memory_space=pltpu.SEMAPHORE)
_EFFECT = pltpu.SideEffectType.DATAFLOW_SIDE_EFFECTING


class InOrder:
    def __init__(self):
        self.token = None

    def __call__(self, fn, *args, **kw):
        return fn(*args, dep=self, **kw)


def _pallas(body, args, *, in_specs, out_specs, out_shape, name, dep=None, grid=(), n_prefetch=0, scratch=(),
            sem=None, **kw):
    n_lead = n_prefetch + len(in_specs)
    in_specs, args = list(in_specs), list(args)
    single = not isinstance(out_shape, (list, tuple))
    out_shapes = [out_shape] if single else list(out_shape)
    out_specs = [out_specs] if single else list(out_specs)
    if dep is not None:
        inner, n_out, takes = body, len(out_shapes), dep.token is not None

        def body(*refs):
            rest = refs[n_lead + (1 if takes else 0):]
            rest[n_out][...] = jnp.zeros((SUBLANES, LANES), F32)
            return inner(*refs[:n_lead], *rest[:n_out], *rest[n_out + 1:])

        if takes:
            in_specs.append(_ANY)
            args.append(dep.token)
        out_shapes.append(jax.ShapeDtypeStruct((SUBLANES, LANES), F32))
        out_specs.append(pl.BlockSpec((SUBLANES, LANES), lambda *_: (0, 0)))
    params = kw.pop("compiler_params", None)
    if params is None:
        params = pltpu.CompilerParams(dimension_semantics=sem, vmem_limit_bytes=VMEM_LIMIT)
    outs = pl.pallas_call(
        body,
        grid_spec=pltpu.PrefetchScalarGridSpec(num_scalar_prefetch=n_prefetch, grid=grid, in_specs=in_specs,
                                               out_specs=out_specs, scratch_shapes=list(scratch)),
        out_shape=out_shapes, compiler_params=params, name=name, **kw,
    )(*args)
    if dep is not None:
        dep.token = outs[-1]
        outs = outs[:-1]
    return outs[0] if single else list(outs)


def _tile(n, pref, unit=LANES):
    best = None
    for t in range(unit, min(n, pref) + 1, unit):
        if n % t == 0:
            best = t
    return best if best is not None else n


def _sigmoid(v):
    return 1.0 / (1.0 + jnp.exp(-v))


ROW_CHUNK = 512


def _row_chunks(m, unit=SUBLANES):
    step = _tile(m, ROW_CHUNK, unit)
    return [(s, step) for s in range(0, m, step)]


def _ew_tiles(r, n, unit=SUBLANES, elems=512 * 1024):
    return _tile(r, max(unit, elems // n), unit), n


def _block_pos(j, perm):
    if perm is None:
        return j
    pos = 0
    for a, p in enumerate(perm):
        pos = pos + jnp.where(j == a, p, 0)
    return pos


def mm_nn(a, w, *, tn, tk, out_dtype, name, perm=None, dep=None):
    m, k = a.shape
    j, k2, nj = w.shape
    assert k == k2 and nj % tn == 0 and k % tk == 0
    npj, nk = nj // tn, k // tk

    def body(a_ref, w_ref, o_ref, *scratch):
        kk = pl.program_id(1)
        for s, sz in _row_chunks(m):
            rows = pl.ds(s, sz)
            p = jnp.dot(a_ref[rows, :], w_ref[...], preferred_element_type=F32)
            if nk == 1:
                o_ref[rows, :] = p.astype(out_dtype)
            else:
                acc = scratch[0]

                @pl.when(kk == 0)
                def _():
                    acc[rows, :] = p

                @pl.when(kk > 0)
                def _():
                    acc[rows, :] += p

                @pl.when(kk == nk - 1)
                def _():
                    o_ref[rows, :] = acc[rows, :].astype(out_dtype)

    return _pallas(
        body, [a, w], dep=dep, grid=(j * npj, nk),
        in_specs=[
            pl.BlockSpec((m, tk), lambda n, kk: (0, kk)),
            pl.BlockSpec((None, tk, tn), lambda n, kk: (n // npj, kk, n % npj)),
        ],
        out_specs=pl.BlockSpec((m, tn), lambda n, kk: (0, _block_pos(n // npj, perm) * npj + n % npj)),
        out_shape=jax.ShapeDtypeStruct((m, j * nj), out_dtype),
        scratch=[pltpu.VMEM((m, tn), F32)] if nk > 1 else [],
        sem=("parallel", "arbitrary"), name=name)


def mm_nt(g, w, *, tko, tn, name, out_dtype=F32, perm=None, dep=None):
    m, n = g.shape
    j, k, nj = w.shape
    assert n == j * nj and nj % tn == 0 and k % tko == 0
    npj, nr = nj // tn, n // tn
    in_place = out_dtype == F32

    def body(g_ref, w_ref, o_ref, *scratch):
        r = pl.program_id(1)
        acc = o_ref if in_place else (scratch[0] if nr > 1 else None)
        for s, sz in _row_chunks(m):
            rows = pl.ds(s, sz)
            p = lax.dot_general(g_ref[rows, :], w_ref[...], (((1,), (1,)), ((), ())), preferred_element_type=F32)
            if acc is None:
                o_ref[rows, :] = p.astype(out_dtype)
                continue

            @pl.when(r == 0)
            def _():
                acc[rows, :] = p

            @pl.when(r > 0)
            def _():
                acc[rows, :] += p

            if not in_place:
                @pl.when(r == nr - 1)
                def _():
                    o_ref[rows, :] = acc[rows, :].astype(out_dtype)

    return _pallas(
        body, [g, w], dep=dep, grid=(k // tko, nr),
        in_specs=[
            pl.BlockSpec((m, tn), lambda ko, r: (0, _block_pos(r // npj, perm) * npj + r % npj)),
            pl.BlockSpec((None, tko, tn), lambda ko, r: (r // npj, ko, r % npj)),
        ],
        out_specs=pl.BlockSpec((m, tko), lambda ko, r: (0, ko)),
        out_shape=jax.ShapeDtypeStruct((m, k), out_dtype),
        scratch=[pltpu.VMEM((m, tko), F32)] if (nr > 1 and not in_place) else [],
        sem=("parallel", "arbitrary"), name=name)


def mm_tn(a, g, n_blocks, *, tk, tn, name, perm=None, dep=None):
    m, k = a.shape
    m2, n = g.shape
    nj = n // n_blocks
    assert m == m2 and nj % tn == 0 and k % tk == 0
    npj = nj // tn

    def body(a_ref, g_ref, o_ref):
        for s, sz in _row_chunks(tk, LANES):
            p = lax.dot_general(a_ref[:, pl.ds(s, sz)], g_ref[...], (((0,), (0,)), ((), ())),
                                preferred_element_type=F32)
            o_ref[pl.ds(s, sz), :] = p.astype(BF16)

    return _pallas(
        body, [a, g], dep=dep, grid=(k // tk, n // tn),
        in_specs=[
            pl.BlockSpec((m, tk), lambda kk, nn: (0, kk)),
            pl.BlockSpec((m, tn), lambda kk, nn: (0, _block_pos(nn // npj, perm) * npj + nn % npj)),
        ],
        out_specs=pl.BlockSpec((None, tk, tn), lambda kk, nn: (nn // npj, kk, nn % npj)),
        out_shape=jax.ShapeDtypeStruct((n_blocks, k, nj), BF16),
        sem=("parallel", "parallel"), name=name)


ROW_TILE = 256


def _row_spec(tr, width):
    return pl.BlockSpec((tr, width), lambda i: (i, 0))


def _full_spec(shape):
    return pl.BlockSpec(shape, lambda *_: (0,) * len(shape))


def _rms(xv):
    return lax.rsqrt(jnp.mean(xv * xv, axis=-1, keepdims=True) + EPS)


def _mod_row(mod_ref, row):
    return mod_ref[pl.ds(row, 1), :]


def pre_mix_fwd(x, mod, gain, dep=None):
    t, d = x.shape
    tr = _tile(t, ROW_TILE, SUBLANES)

    def body(x_ref, mod_ref, g_ref, h_ref):
        xv = x_ref[...]
        y = xv * _rms(xv) * g_ref[...]
        h_ref[...] = (y * (1.0 + _mod_row(mod_ref, SC_M)) + _mod_row(mod_ref, SH_M)).astype(BF16)

    return _pallas(
        body, [x, mod, gain], dep=dep, grid=(t // tr,),
        in_specs=[_row_spec(tr, d), _full_spec(mod.shape), _full_spec(gain.shape)],
        out_specs=_row_spec(tr, d),
        out_shape=jax.ShapeDtypeStruct((t, d), BF16),
        sem=("parallel",), name="pre_mix_fwd")


def pre_ffn_fwd(x, o_m, mod, gain, dep=None):
    t, d = x.shape
    tr = _tile(t, ROW_TILE, SUBLANES)

    def body(x_ref, om_ref, mod_ref, g_ref, x1_ref, h_ref):
        x1 = x_ref[...] + _mod_row(mod_ref, GT_M) * om_ref[...]
        x1_ref[...] = x1
        y = x1 * _rms(x1) * g_ref[...]
        h_ref[...] = (y * (1.0 + _mod_row(mod_ref, SC_F)) + _mod_row(mod_ref, SH_F)).astype(BF16)

    return _pallas(
        body, [x, o_m, mod, gain], dep=dep, grid=(t // tr,),
        in_specs=[_row_spec(tr, d), _row_spec(tr, d), _full_spec(mod.shape), _full_spec(gain.shape)],
        out_specs=[_row_spec(tr, d), _row_spec(tr, d)],
        out_shape=[jax.ShapeDtypeStruct((t, d), F32), jax.ShapeDtypeStruct((t, d), BF16)],
        sem=("parallel",), name="pre_ffn_fwd")


def loss_head(x1, o_f, target, mod, dep=None):
    t, d = x1.shape
    tr = _tile(t, ROW_TILE, SUBLANES)

    def body(x1_ref, of_ref, tg_ref, mod_ref, loss_ref, dy_ref, dof_ref, acc_ref):
        i = pl.program_id(0)
        gt = _mod_row(mod_ref, GT_F)
        of = of_ref[...]
        err = x1_ref[...] + gt * of - tg_ref[...]
        dy = err * (1.0 / d)
        dy_ref[...] = dy
        dof_ref[...] = (dy * gt).astype(BF16)
        part = (0.5 / d) * jnp.sum(jnp.sum(err * err, axis=1, keepdims=True), axis=0, keepdims=True)
        dgt = jnp.sum(dy * of, axis=0, keepdims=True)

        @pl.when(i == 0)
        def _():
            loss_ref[...] = jnp.zeros_like(loss_ref)
            acc_ref[...] = jnp.zeros_like(acc_ref)

        loss_ref[...] += part
        acc_ref[pl.ds(0, 1), :] += dgt

    return _pallas(
        body, [x1, o_f, target, mod], dep=dep, grid=(t // tr,),
        in_specs=[_row_spec(tr, d), _row_spec(tr, d), _row_spec(tr, d), _full_spec(mod.shape)],
        out_specs=[_full_spec((1, 1)), _row_spec(tr, d), _row_spec(tr, d), _full_spec((SUBLANES, d))],
        out_shape=[jax.ShapeDtypeStruct((1, 1), F32), jax.ShapeDtypeStruct((t, d), F32),
                   jax.ShapeDtypeStruct((t, d), BF16), jax.ShapeDtypeStruct((SUBLANES, d), F32)],
        sem=("arbitrary",), name="loss_head")


def _norm_bwd(xv, dh, sc, gain):
    rstd = _rms(xv)
    yn = xv * rstd
    dsh = jnp.sum(dh, axis=0, keepdims=True)
    dsc = jnp.sum(dh * (yn * gain), axis=0, keepdims=True)
    dgain = jnp.sum(dh * (1.0 + sc) * yn, axis=0, keepdims=True)
    dyn = dh * ((1.0 + sc) * gain)
    dx = rstd * (dyn - yn * jnp.mean(dyn * yn, axis=-1, keepdims=True))
    return dx, dsh, dsc, dgain


def pre_ffn_bwd(x1, dh2, dy, o_m, mod, gain, dep=None):
    t, d = x1.shape
    tr = _tile(t, ROW_TILE, SUBLANES)

    def body(x1_ref, dh_ref, dy_ref, om_ref, mod_ref, g_ref, dx1_ref, dom_ref, acc_ref):
        i = pl.program_id(0)
        dxn, dsh, dsc, dgain = _norm_bwd(x1_ref[...], dh_ref[...], _mod_row(mod_ref, SC_F), g_ref[...])
        dx1 = dy_ref[...] + dxn
        dx1_ref[...] = dx1
        dom_ref[...] = (dx1 * _mod_row(mod_ref, GT_M)).astype(BF16)
        dgt = jnp.sum(dx1 * om_ref[...], axis=0, keepdims=True)

        @pl.when(i == 0)
        def _():
            acc_ref[...] = jnp.zeros_like(acc_ref)

        acc_ref[pl.ds(0, 1), :] += dsh
        acc_ref[pl.ds(1, 1), :] += dsc
        acc_ref[pl.ds(2, 1), :] += dgain
        acc_ref[pl.ds(3, 1), :] += dgt

    return _pallas(
        body, [x1, dh2, dy, o_m, mod, gain], dep=dep, grid=(t // tr,),
        in_specs=[_row_spec(tr, d)] * 4 + [_full_spec(mod.shape), _full_spec(gain.shape)],
        out_specs=[_row_spec(tr, d), _row_spec(tr, d), _full_spec((SUBLANES, d))],
        out_shape=[jax.ShapeDtypeStruct((t, d), F32), jax.ShapeDtypeStruct((t, d), BF16),
                   jax.ShapeDtypeStruct((SUBLANES, d), F32)],
        sem=("arbitrary",), name="pre_ffn_bwd")


def pre_mix_bwd(x, dh, dx1, mod, gain, dep=None):
    t, d = x.shape
    tr = _tile(t, ROW_TILE, SUBLANES)

    def body(x_ref, dh_ref, dx1_ref, mod_ref, g_ref, gx_ref, acc_ref):
        i = pl.program_id(0)
        dxn, dsh, dsc, dgain = _norm_bwd(x_ref[...], dh_ref[...], _mod_row(mod_ref, SC_M), g_ref[...])
        gx_ref[...] = dx1_ref[...] + dxn

        @pl.when(i == 0)
        def _():
            acc_ref[...] = jnp.zeros_like(acc_ref)

        acc_ref[pl.ds(0, 1), :] += dsh
        acc_ref[pl.ds(1, 1), :] += dsc
        acc_ref[pl.ds(2, 1), :] += dgain

    return _pallas(
        body, [x, dh, dx1, mod, gain], dep=dep, grid=(t // tr,),
        in_specs=[_row_spec(tr, d)] * 3 + [_full_spec(mod.shape), _full_spec(gain.shape)],
        out_specs=[_row_spec(tr, d), _full_spec((SUBLANES, d))],
        out_shape=[jax.ShapeDtypeStruct((t, d), F32), jax.ShapeDtypeStruct((SUBLANES, d), F32)],
        sem=("arbitrary",), name="pre_mix_bwd")


def merge_fwd(p, y_attn, y_conv, off_ga, off_gc, dep=None):
    t, d = y_attn.shape
    tr = _tile(t, ROW_TILE, SUBLANES)
    cw = math.gcd(math.gcd(off_ga, off_gc), math.gcd(d, 512))
    nc = d // cw

    def body(ga_ref, gc_ref, ya_ref, yc_ref, o_ref):
        o_ref[...] = (_sigmoid(ga_ref[...]) * ya_ref[...] + _sigmoid(gc_ref[...]) * yc_ref[...]).astype(BF16)

    return _pallas(
        body, [p, p, y_attn, y_conv], dep=dep, grid=(t // tr, nc),
        in_specs=[pl.BlockSpec((tr, cw), lambda i, j: (i, off_ga // cw + j)),
                  pl.BlockSpec((tr, cw), lambda i, j: (i, off_gc // cw + j)),
                  pl.BlockSpec((tr, cw), lambda i, j: (i, j)),
                  pl.BlockSpec((tr, cw), lambda i, j: (i, j))],
        out_specs=pl.BlockSpec((tr, cw), lambda i, j: (i, j)),
        out_shape=jax.ShapeDtypeStruct((t, d), BF16),
        sem=("parallel", "parallel"), name="merge_fwd")


def merge_bwd(p, y_attn, y_conv, dmerged, off_ga, off_gc, dep=None):
    t, d = y_attn.shape
    tr = _tile(t, ROW_TILE, SUBLANES)
    cw = math.gcd(math.gcd(off_ga, off_gc), math.gcd(d, 512))
    nc = d // cw

    def body(ga_ref, gc_ref, ya_ref, yc_ref, dm_ref, dya_ref, dyc_ref, dga_ref, dgc_ref):
        dm = dm_ref[...]
        sa = _sigmoid(ga_ref[...])
        sc = _sigmoid(gc_ref[...])
        dya_ref[...] = (dm * sa).astype(BF16)
        dyc_ref[...] = (dm * sc).astype(BF16)
        dga_ref[...] = (dm * ya_ref[...] * sa * (1.0 - sa)).astype(BF16)
        dgc_ref[...] = (dm * yc_ref[...] * sc * (1.0 - sc)).astype(BF16)

    blk = pl.BlockSpec((tr, cw), lambda i, j: (i, j))
    return _pallas(
        body, [p, p, y_attn, y_conv, dmerged], dep=dep, grid=(t // tr, nc),
        in_specs=[pl.BlockSpec((tr, cw), lambda i, j: (i, off_ga // cw + j)),
                  pl.BlockSpec((tr, cw), lambda i, j: (i, off_gc // cw + j)), blk, blk, blk],
        out_specs=[blk] * 4,
        out_shape=[jax.ShapeDtypeStruct((t, d), BF16)] * 4,
        sem=("parallel", "parallel"), name="merge_bwd")


def ffn_perm(n_blocks):
    half = n_blocks // 2
    return tuple(2 * j if j < half else 2 * (j - half) + 1 for j in range(n_blocks))


def swiglu_fwd(f, nj, dep=None):
    t, two = f.shape
    tr = _tile(t, ROW_TILE, SUBLANES)
    npair = two // (2 * nj)

    def body(f_ref, o_ref):
        g = f_ref[:, :nj].astype(F32)
        u = f_ref[:, nj:].astype(F32)
        o_ref[...] = (g * _sigmoid(g) * u).astype(BF16)

    return _pallas(
        body, [f], dep=dep, grid=(t // tr, npair),
        in_specs=[pl.BlockSpec((tr, 2 * nj), lambda i, j: (i, j))],
        out_specs=pl.BlockSpec((tr, nj), lambda i, j: (i, j)),
        out_shape=jax.ShapeDtypeStruct((t, two // 2), BF16),
        sem=("parallel", "parallel"), name="swiglu_fwd")


def swiglu_bwd(f, dact, nj, dep=None):
    t, two = f.shape
    tr = _tile(t, ROW_TILE, SUBLANES)
    npair = two // (2 * nj)

    def body(f_ref, da_ref, o_ref):
        g = f_ref[:, :nj].astype(F32)
        u = f_ref[:, nj:].astype(F32)
        da = da_ref[...]
        s = _sigmoid(g)
        o_ref[:, :nj] = (da * u * (s * (1.0 + g * (1.0 - s)))).astype(BF16)
        o_ref[:, nj:] = (da * (g * s)).astype(BF16)

    return _pallas(
        body, [f, dact], dep=dep, grid=(t // tr, npair),
        in_specs=[pl.BlockSpec((tr, 2 * nj), lambda i, j: (i, j)), pl.BlockSpec((tr, nj), lambda i, j: (i, j))],
        out_specs=pl.BlockSpec((tr, 2 * nj), lambda i, j: (i, j)),
        out_shape=jax.ShapeDtypeStruct((t, two), BF16),
        sem=("parallel", "parallel"), name="swiglu_bwd")


def _t5_bucket_table():
    q_off = np.arange(BLOCK)
    k_off = np.arange(2 * BLOCK)
    dist = q_off[:, None] + BLOCK - k_off[None, :]
    n = np.maximum(dist, 0)
    nf = np.maximum(n, 1).astype(np.float32)
    large = MAX_EXACT + (np.log(nf / np.float32(MAX_EXACT)) / np.float32(math.log(MAX_DISTANCE / MAX_EXACT))
                         * np.float32(NUM_BUCKETS - MAX_EXACT)).astype(np.int32)
    large = np.minimum(large, NUM_BUCKETS - 1)
    bucket = np.where(n < MAX_EXACT, n, large).astype(np.int32)
    allowed = (dist >= 0) & (dist < WINDOW)
    return np.where(allowed, bucket, -1).astype(np.int32)


def bias_table(rel_bias, bucket_p, bucket_c, dep=None):
    nb, nq = rel_bias.shape

    def body(rb_ref, bkp_ref, bkc_ref, op_ref, oc_ref):
        for bk_ref, o_ref in ((bkp_ref, op_ref), (bkc_ref, oc_ref)):
            bk = bk_ref[...]
            for h in range(nq):
                acc = jnp.full(bk.shape, -jnp.inf, F32)
                for b in range(nb):
                    acc = jnp.where(bk == b, rb_ref[b, h], acc)
                o_ref[h] = acc

    return _pallas(
        body, [rel_bias, bucket_p, bucket_c], dep=dep,
        in_specs=[_SMEM, _VMEM, _VMEM], out_specs=[_VMEM, _VMEM],
        out_shape=[jax.ShapeDtypeStruct((nq,) + bucket_p.shape, F32)] * 2,
        name="bias_table")


def bias_table_bwd(dbp, dbc, bucket_p, bucket_c, dep=None):
    nq = dbp.shape[0]

    def body(dbp_ref, dbc_ref, bkp_ref, bkc_ref, o_ref):
        bkp, bkc = bkp_ref[...][None], bkc_ref[...][None]
        dp, dc = dbp_ref[...], dbc_ref[...]
        for b in range(NUM_BUCKETS):
            sel = jnp.where(bkp == b, dp, 0.0) + jnp.where(bkc == b, dc, 0.0)
            o_ref[b] = jnp.sum(jnp.sum(sel, axis=2, keepdims=True), axis=1, keepdims=True)

    return _pallas(
        body, [dbp, dbc, bucket_p, bucket_c], dep=dep,
        in_specs=[_VMEM] * 4, out_specs=_VMEM,
        out_shape=jax.ShapeDtypeStruct((NUM_BUCKETS, nq, 1, 1), F32),
        name="bias_table_bwd")


_NT = (((1,), (1,)), ((), ()))
_TN = (((0,), (0,)), ((), ()))


@jax.custom_vjp
def _bdot_nt(a, b):
    return lax.dot_general(a.astype(BF16), b.astype(BF16), _NT, preferred_element_type=F32)


def _bdot_nt_fwd(a, b):
    return _bdot_nt(a, b), (a, b)


def _bdot_nt_bwd(res, g):
    a, b = res
    gb = g.astype(BF16)
    da = jnp.dot(gb, b.astype(BF16), preferred_element_type=F32)
    db = lax.dot_general(gb, a.astype(BF16), _TN, preferred_element_type=F32)
    return da, db


_bdot_nt.defvjp(_bdot_nt_fwd, _bdot_nt_bwd)


@jax.custom_vjp
def _bdot_nn(a, b):
    return jnp.dot(a.astype(BF16), b.astype(BF16), preferred_element_type=F32)


def _bdot_nn_fwd(a, b):
    return _bdot_nn(a, b), (a, b)


def _bdot_nn_bwd(res, g):
    a, b = res
    gb = g.astype(BF16)
    da = lax.dot_general(gb, b.astype(BF16), _NT, preferred_element_type=F32)
    db = lax.dot_general(a.astype(BF16), gb, _TN, preferred_element_type=F32)
    return da, db


_bdot_nn.defvjp(_bdot_nn_fwd, _bdot_nn_bwd)


def _attn_math(q4, kp, kc, vp, vc, bp, bc, sink4, qg, kg, *, prev_ok, scale):
    g, b, hd = q4.shape
    q = q4.reshape(g * b, hd)
    qn = q * _rms(q) * qg
    kpn = kp * _rms(kp) * kg
    kcn = kc * _rms(kc) * kg
    lp = _bdot_nt(qn, kpn).reshape(g, b, b) * scale + bp
    lc = _bdot_nt(qn, kcn).reshape(g, b, b) * scale + bc
    lp = jnp.where(prev_ok, lp, -jnp.inf)
    m = jnp.maximum(jnp.maximum(jnp.max(lp, axis=-1, keepdims=True), jnp.max(lc, axis=-1, keepdims=True)), sink4)
    m = lax.stop_gradient(m)
    pp = jnp.exp(lp - m)
    pc = jnp.exp(lc - m)
    den = jnp.sum(pp, axis=-1, keepdims=True) + jnp.sum(pc, axis=-1, keepdims=True) + jnp.exp(sink4 - m)
    inv = 1.0 / den
    out = _bdot_nn((pp * inv).reshape(g * b, b), vp) + _bdot_nn((pc * inv).reshape(g * b, b), vc)
    return out.reshape(g, b, hd)


def _attn_specs(grp, hd, nblk, reverse):
    def blk(n):
        return nblk - 1 - n if reverse else n

    return [
        pl.BlockSpec((grp, BLOCK, hd), lambda h, n: (h, blk(n), 0)),
        pl.BlockSpec((None, BLOCK, hd), lambda h, n: (h, jnp.maximum(blk(n) - 1, 0), 0)),
        pl.BlockSpec((None, BLOCK, hd), lambda h, n: (h, blk(n), 0)),
        pl.BlockSpec((None, BLOCK, hd), lambda h, n: (h, jnp.maximum(blk(n) - 1, 0), 0)),
        pl.BlockSpec((None, BLOCK, hd), lambda h, n: (h, blk(n), 0)),
        pl.BlockSpec((grp, BLOCK, BLOCK), lambda h, n: (h, 0, 0)),
        pl.BlockSpec((grp, BLOCK, BLOCK), lambda h, n: (h, 0, 0)),
        pl.BlockSpec((grp, 1, 1), lambda h, n: (h, 0, 0)),
        pl.BlockSpec((1, hd), lambda h, n: (0, 0)),
        pl.BlockSpec((1, hd), lambda h, n: (0, 0)),
    ]


def attn_fwd(qh, kh, vh, bias_p, bias_c, sinks, qg, kg, dep=None):
    nq, t, hd = qh.shape
    nkv = kh.shape[0]
    grp, nblk = nq // nkv, t // BLOCK
    scale = hd ** -0.5

    def body(q_ref, kp_ref, kc_ref, vp_ref, vc_ref, bp_ref, bc_ref, s_ref, qg_ref, kg_ref, o_ref):
        prev_ok = pl.program_id(1) > 0
        out = _attn_math(q_ref[...], kp_ref[...], kc_ref[...], vp_ref[...], vc_ref[...], bp_ref[...], bc_ref[...],
                         s_ref[...], qg_ref[...], kg_ref[...], prev_ok=prev_ok, scale=scale)
        o_ref[...] = out.astype(BF16)

    return _pallas(
        body, [qh, kh, kh, vh, vh, bias_p, bias_c, sinks, qg, kg], dep=dep, grid=(nkv, nblk),
        in_specs=_attn_specs(grp, hd, nblk, False),
        out_specs=pl.BlockSpec((grp, BLOCK, hd), lambda h, n: (h, n, 0)),
        out_shape=jax.ShapeDtypeStruct((nq, t, hd), BF16),
        sem=("parallel", "parallel"), name="attn_fwd")


def attn_bwd(qh, kh, vh, bias_p, bias_c, sinks, qg, kg, doh, dep=None):
    nq, t, hd = qh.shape
    nkv = kh.shape[0]
    grp, nblk = nq // nkv, t // BLOCK
    scale = hd ** -0.5

    def body(q_ref, kp_ref, kc_ref, vp_ref, vc_ref, bp_ref, bc_ref, s_ref, qg_ref, kg_ref, do_ref,
             dq_ref, dk_ref, dv_ref, dbp_ref, dbc_ref, ds_ref, dqg_ref, dkg_ref, ck_ref, cv_ref):
        h, i = pl.program_id(0), pl.program_id(1)
        prev_ok = (nblk - 1 - i) > 0
        fn = functools.partial(_attn_math, prev_ok=prev_ok, scale=scale)
        _, vjp = jax.vjp(fn, q_ref[...], kp_ref[...], kc_ref[...], vp_ref[...], vc_ref[...], bp_ref[...], bc_ref[...],
                         s_ref[...], qg_ref[...], kg_ref[...])
        dq, dkp, dkc, dvp, dvc, dbp, dbc, dsk, dqg, dkg = vjp(do_ref[...].astype(F32))
        dq_ref[...] = dq

        @pl.when(i == 0)
        def _():
            ck_ref[...] = jnp.zeros_like(ck_ref)
            cv_ref[...] = jnp.zeros_like(cv_ref)
            dbp_ref[...] = jnp.zeros_like(dbp_ref)
            dbc_ref[...] = jnp.zeros_like(dbc_ref)
            ds_ref[...] = jnp.zeros_like(ds_ref)

        @pl.when((i == 0) & (h == 0))
        def _():
            dqg_ref[...] = jnp.zeros_like(dqg_ref)
            dkg_ref[...] = jnp.zeros_like(dkg_ref)

        dk_ref[...] = dkc + ck_ref[...]
        dv_ref[...] = dvc + cv_ref[...]
        ck_ref[...] = dkp
        cv_ref[...] = dvp
        dbp_ref[...] += dbp
        dbc_ref[...] += dbc
        ds_ref[...] += dsk
        dqg_ref[...] += dqg
        dkg_ref[...] += dkg

    rev = lambda h, n: (h, nblk - 1 - n, 0)
    return _pallas(
        body, [qh, kh, kh, vh, vh, bias_p, bias_c, sinks, qg, kg, doh], dep=dep, grid=(nkv, nblk),
        in_specs=_attn_specs(grp, hd, nblk, True) + [pl.BlockSpec((grp, BLOCK, hd), rev)],
        out_specs=[
            pl.BlockSpec((grp, BLOCK, hd), rev),
            pl.BlockSpec((None, BLOCK, hd), rev),
            pl.BlockSpec((None, BLOCK, hd), rev),
            pl.BlockSpec((grp, BLOCK, BLOCK), lambda h, n: (h, 0, 0)),
            pl.BlockSpec((grp, BLOCK, BLOCK), lambda h, n: (h, 0, 0)),
            pl.BlockSpec((grp, 1, 1), lambda h, n: (h, 0, 0)),
            pl.BlockSpec((1, hd), lambda h, n: (0, 0)),
            pl.BlockSpec((1, hd), lambda h, n: (0, 0)),
        ],
        out_shape=[
            jax.ShapeDtypeStruct((nq, t, hd), F32),
            jax.ShapeDtypeStruct((nkv, t, hd), F32),
            jax.ShapeDtypeStruct((nkv, t, hd), F32),
            jax.ShapeDtypeStruct((nq, BLOCK, BLOCK), F32),
            jax.ShapeDtypeStruct((nq, BLOCK, BLOCK), F32),
            jax.ShapeDtypeStruct((nq, 1, 1), F32),
            jax.ShapeDtypeStruct((1, hd), F32),
            jax.ShapeDtypeStruct((1, hd), F32),
        ],
        scratch=[pltpu.VMEM((BLOCK, hd), F32), pltpu.VMEM((BLOCK, hd), F32)],
        sem=("arbitrary", "arbitrary"), name="attn_bwd")


CONV_TILE = 256


def _conv_halo_specs(tb, ch, nblk):
    per = tb // CONV_HALO
    last = nblk * per - 1
    cur = pl.BlockSpec((tb, ch), lambda n: (n, 0))
    prev = pl.BlockSpec((CONV_HALO, ch), lambda n: (jnp.maximum(n * per - 1, 0), 0))
    nxt = pl.BlockSpec((CONV_HALO, ch), lambda n: (jnp.minimum((n + 1) * per, last), 0))
    return cur, prev, nxt


def _ln_silu(co, ln_g, ln_b):
    mu = jnp.mean(co, axis=-1, keepdims=True)
    cen = co - mu
    rstd = lax.rsqrt(jnp.mean(cen * cen, axis=-1, keepdims=True) + EPS)
    xhat = cen * rstd
    z = xhat * ln_g + ln_b
    return xhat, rstd, z


def _shifted_copies(src, shifted):
    rows = src.shape[0] - SUBLANES
    for r in range(1, SUBLANES):
        shifted[r, pl.ds(0, rows), :] = src[pl.ds(r, rows), :]


def _rows_from(src, shifted, start, n):
    r = start % SUBLANES
    if r == 0:
        return src[pl.ds(start, n), :]
    return shifted[r, pl.ds(start - r, n), :]


def conv_fwd(ca, cb, conv_w, conv_b, ln_g, ln_b, dep=None):
    t, ch = ca.shape
    tb = _tile(t, CONV_TILE, CONV_HALO)
    nblk = t // tb
    cur, prev, _ = _conv_halo_specs(tb, ch, nblk)
    lead = CONV_HALO - (CONV_WIDTH - 1)

    def body(ca_ref, cb_ref, cap_ref, cbp_ref, w_ref, b_ref, g_ref, bb_ref, s_ref, co_ref, ubuf, ushift):
        n = pl.program_id(0)
        halo = cap_ref[...] * _sigmoid(cbp_ref[...])
        ubuf[pl.ds(0, CONV_HALO), :] = jnp.where(n > 0, halo, 0.0)
        ubuf[pl.ds(CONV_HALO, tb), :] = ca_ref[...] * _sigmoid(cb_ref[...])
        _shifted_copies(ubuf, ushift)
        acc = jnp.broadcast_to(b_ref[...], (tb, ch))
        for k in range(CONV_WIDTH):
            acc = acc + w_ref[pl.ds(k, 1), :] * _rows_from(ubuf, ushift, lead + k, tb)
        co_ref[...] = acc
        _, _, z = _ln_silu(acc, g_ref[...], bb_ref[...])
        s_ref[...] = (z * _sigmoid(z)).astype(BF16)

    vec = _full_spec((1, ch))
    return _pallas(
        body, [ca, cb, ca, cb, conv_w, conv_b, ln_g, ln_b], dep=dep, grid=(nblk,),
        in_specs=[cur, cur, prev, prev, _full_spec(conv_w.shape), vec, vec, vec],
        out_specs=[cur, cur],
        out_shape=[jax.ShapeDtypeStruct((t, ch), BF16), jax.ShapeDtypeStruct((t, ch), F32)],
        scratch=[pltpu.VMEM((CONV_HALO + tb, ch), F32), pltpu.VMEM((SUBLANES, CONV_HALO + tb, ch), F32)],
        sem=("parallel",), name="conv_fwd")


def conv_bwd(ca, cb, co, ds, conv_w, ln_g, ln_b, dep=None):
    t, ch = ca.shape
    tb = _tile(t, CONV_TILE, CONV_HALO)
    nblk = t // tb
    cur, prev, nxt = _conv_halo_specs(tb, ch, nblk)
    lead = CONV_HALO - (CONV_WIDTH - 1)
    ext = tb + CONV_HALO

    def body(ca_ref, cb_ref, cap_ref, cbp_ref, co_ref, con_ref, ds_ref, dsn_ref, w_ref, g_ref, bb_ref,
             dca_ref, dcb_ref, dw_ref, dvec_ref, ubuf, dbuf, ushift, dshift):
        n = pl.program_id(0)
        is_last = n == nblk - 1
        sig_b = _sigmoid(cb_ref[...])
        cav = ca_ref[...]
        ubuf[pl.ds(0, CONV_HALO), :] = jnp.where(n > 0, cap_ref[...] * _sigmoid(cbp_ref[...]), 0.0)
        ubuf[pl.ds(CONV_HALO, tb), :] = cav * sig_b
        _shifted_copies(ubuf, ushift)
        co = jnp.concatenate([co_ref[...], con_ref[...]], axis=0)
        xhat, rstd, z = _ln_silu(co, g_ref[...], bb_ref[...])
        dsv = jnp.concatenate([ds_ref[...].astype(F32), jnp.where(is_last, 0.0, dsn_ref[...].astype(F32))], axis=0)
        sg = _sigmoid(z)
        dz = dsv * (sg * (1.0 + z * (1.0 - sg)))
        dxh = dz * g_ref[...]
        dco = rstd * (dxh - jnp.mean(dxh, axis=-1, keepdims=True)
                      - xhat * jnp.mean(dxh * xhat, axis=-1, keepdims=True))
        dbuf[...] = dco
        _shifted_copies(dbuf, dshift)

        @pl.when(n == 0)
        def _():
            dw_ref[...] = jnp.zeros_like(dw_ref)
            dvec_ref[...] = jnp.zeros_like(dvec_ref)

        dco_cur = dco[:tb]
        dvec_ref[pl.ds(0, 1), :] += jnp.sum(dco_cur, axis=0, keepdims=True)
        dvec_ref[pl.ds(1, 1), :] += jnp.sum(dz[:tb] * xhat[:tb], axis=0, keepdims=True)
        dvec_ref[pl.ds(2, 1), :] += jnp.sum(dz[:tb], axis=0, keepdims=True)
        du = jnp.zeros((tb, ch), F32)
        for k in range(CONV_WIDTH):
            du = du + w_ref[pl.ds(k, 1), :] * _rows_from(dbuf, dshift, CONV_WIDTH - 1 - k, tb)
            dw_ref[pl.ds(k, 1), :] += jnp.sum(dco_cur * _rows_from(ubuf, ushift, lead + k, tb), axis=0,
                                              keepdims=True)
        dca_ref[...] = (du * sig_b).astype(BF16)
        dcb_ref[...] = (du * cav * sig_b * (1.0 - sig_b)).astype(BF16)

    vec = _full_spec((1, ch))
    return _pallas(
        body, [ca, cb, ca, cb, co, co, ds, ds, conv_w, ln_g, ln_b], dep=dep, grid=(nblk,),
        in_specs=[cur, cur, prev, prev, cur, nxt, cur, nxt, _full_spec(conv_w.shape), vec, vec],
        out_specs=[cur, cur, _full_spec(conv_w.shape), _full_spec((SUBLANES, ch))],
        out_shape=[jax.ShapeDtypeStruct((t, ch), BF16), jax.ShapeDtypeStruct((t, ch), BF16),
                   jax.ShapeDtypeStruct(conv_w.shape, F32), jax.ShapeDtypeStruct((SUBLANES, ch), F32)],
        scratch=[pltpu.VMEM((CONV_HALO + tb, ch), F32), pltpu.VMEM((ext, ch), F32),
                 pltpu.VMEM((SUBLANES, CONV_HALO + tb, ch), F32), pltpu.VMEM((SUBLANES, ext, ch), F32)],
        sem=("arbitrary",), name="conv_bwd")


def ada_fwd(c_t, w_ada, dep=None):
    d, nc = w_ada.shape
    nex = c_t.shape[1]
    tn = _tile(nc, 512)

    def body(ct_ref, w_ref, o_ref):
        w = w_ref[...]
        ct = ct_ref[...]
        cact = ct * _sigmoid(ct)
        rows = [jnp.sum(w * cact[:, b:b + 1], axis=0, keepdims=True) for b in range(nex)]
        o_ref[...] = jnp.concatenate(rows, axis=0)

    return _pallas(
        body, [c_t, w_ada], dep=dep, grid=(nc // tn,),
        in_specs=[_full_spec(c_t.shape), pl.BlockSpec((d, tn), lambda j: (0, j))],
        out_specs=pl.BlockSpec((nex, tn), lambda j: (0, j)),
        out_shape=jax.ShapeDtypeStruct((nex, nc), F32),
        sem=("parallel",), name="ada_fwd")


def _adamw_math(w, g, m, v):
    m = ADAM_B1 * m + (1.0 - ADAM_B1) * g
    v = ADAM_B2 * v + (1.0 - ADAM_B2) * (g * g)
    m_hat = m / (1.0 - ADAM_B1 ** ADAM_STEP)
    v_hat = v / (1.0 - ADAM_B2 ** ADAM_STEP)
    delta = -ADAM_LR * (m_hat / (jnp.sqrt(v_hat) + ADAM_EPS) + ADAM_WD * w)
    return delta, m, v


def adamw(w, g, m, v, name, dep=None):
    r, n = w.shape
    tr, tn = _ew_tiles(r, n, elems=256 * 1024)

    def body(w_ref, g_ref, m_ref, v_ref, d_ref, nm_ref, nv_ref):
        d_ref[...], nm_ref[...], nv_ref[...] = _adamw_math(w_ref[...], g_ref[...], m_ref[...], v_ref[...])

    blk = pl.BlockSpec((tr, tn), lambda i, j: (i, j))
    return _pallas(
        body, [w, g, m, v], dep=dep, grid=(r // tr, n // tn),
        in_specs=[blk] * 4, out_specs=[blk] * 3,
        out_shape=[jax.ShapeDtypeStruct((r, n), F32)] * 3,
        sem=("parallel", "parallel"), name=name)


def ada_grad_adamw(c_t, dmod_cols, w, m, v, dep=None):
    d, nc = w.shape
    nex = c_t.shape[1]
    tr, tn = _ew_tiles(d, nc, elems=256 * 1024)

    def body(ct_ref, dm_ref, w_ref, m_ref, v_ref, g_ref, d_ref, nm_ref, nv_ref):
        ct = ct_ref[...]
        cact = ct * _sigmoid(ct)
        dm = dm_ref[...]
        g = cact[:, 0:1] * dm[0:1, :]
        for b in range(1, nex):
            g = g + cact[:, b:b + 1] * dm[b:b + 1, :]
        g_ref[...] = g
        d_ref[...], nm_ref[...], nv_ref[...] = _adamw_math(w_ref[...], g, m_ref[...], v_ref[...])

    blk = pl.BlockSpec((tr, tn), lambda i, j: (i, j))
    return _pallas(
        body, [c_t, dmod_cols, w, m, v], dep=dep, grid=(d // tr, nc // tn),
        in_specs=[pl.BlockSpec((tr, nex), lambda i, j: (i, 0)), pl.BlockSpec((nex, tn), lambda i, j: (0, j)),
                  blk, blk, blk],
        out_specs=[blk] * 4,
        out_shape=[jax.ShapeDtypeStruct((d, nc), F32)] * 4,
        sem=("parallel", "parallel"), name="ada_grad_adamw")


def _row_pack(parts):
    cols, offs, off = [], [], 0
    for p in parts:
        n = p.shape[1]
        width = -(-n // LANES) * LANES
        cols.append(jnp.pad(p, ((0, 0), (0, width - n))) if width != n else p)
        offs.append(off)
        off += width
    return jnp.concatenate(cols, axis=1), offs


def small_sum_adamw(gathered, offs, ws, ms, vs, extra_widths, dep=None):
    ndev = gathered.shape[0]
    npar = len(ws)

    def body(ga_ref, *refs):
        w_refs, m_refs, v_refs = refs[:npar], refs[npar:2 * npar], refs[2 * npar:3 * npar]
        outs = refs[3 * npar:]
        tot = ga_ref[0]
        for s in range(1, ndev):
            tot = tot + ga_ref[s]
        for i in range(npar):
            n = ws[i].shape[1]
            g = tot[:, offs[i]:offs[i] + n]
            outs[4 * i][...] = g
            outs[4 * i + 1][...], outs[4 * i + 2][...], outs[4 * i + 3][...] = _adamw_math(
                w_refs[i][...], g, m_refs[i][...], v_refs[i][...])
        for e, n in enumerate(extra_widths):
            off = offs[npar + e]
            outs[4 * npar + e][...] = tot[:, off:off + n]

    shapes = [jax.ShapeDtypeStruct(w.shape, F32) for w in ws for _ in range(4)]
    shapes += [jax.ShapeDtypeStruct((1, n), F32) for n in extra_widths]
    return _pallas(
        body, [gathered, *ws, *ms, *vs], dep=dep, in_specs=[_VMEM] * (1 + 3 * npar), out_specs=[_VMEM] * len(shapes),
        out_shape=shapes, name="small_sum_adamw")


def _position():
    return lax.axis_index("x"), lax.axis_index("y"), lax.axis_index("c")


def _other_chips(x, y):
    return [(1 - x, y), (x, 1 - y), (1 - x, 1 - y)]


def allgather_small(block, name, dep=None):
    def body(x_ref, out_ref, send_sems, recv_sems, local_sem):
        x, y, c = _position()
        me, sibling = (x, y, c), (x, y, 1 - c)
        chips = _other_chips(x, y)

        def slot(px, py, pc):
            return out_ref.at[4 * px + 2 * py + pc]

        def copy(k, block_of, to, src=None):
            return pltpu.make_async_remote_copy(
                src_ref=slot(*block_of) if src is None else src, dst_ref=slot(*block_of),
                send_sem=send_sems.at[k], recv_sem=recv_sems.at[k], device_id=to, device_id_type=MESH)

        mine = pltpu.make_async_copy(x_ref, slot(*me), local_sem)
        mine.start()
        first = [copy(0, me, sibling, src=x_ref)]
        first += [copy(1 + j, me, (*chip, c), src=x_ref) for j, chip in enumerate(chips)]
        for cp in first:
            cp.start()
        passed = [copy(4 + j, (*chip, c), sibling) for j, chip in enumerate(chips)]
        for j, chip in enumerate(chips):
            copy(1 + j, (*chip, c), me).wait_recv()
            passed[j].start()
        copy(0, sibling, me).wait_recv()
        for j, chip in enumerate(chips):
            copy(4 + j, (*chip, 1 - c), me).wait_recv()
        for cp in first + passed:
            cp.wait_send()
        mine.wait()

    return _pallas(
        body, [block], dep=dep,
        out_shape=jax.ShapeDtypeStruct((N_DEV, *block.shape), block.dtype),
        in_specs=[_VMEM], out_specs=_VMEM,
        scratch=[pltpu.SemaphoreType.DMA((7,)), pltpu.SemaphoreType.DMA((7,)), pltpu.SemaphoreType.DMA],
        name=name)


class Started(NamedTuple):
    send_sems: Any
    recv_sems: Any
    bufs: list


def exchange_start(name, bufs, n_copies, plan, dep=None):
    nb = len(bufs)

    def body(*refs):
        for cp in plan(refs[:nb], refs[nb], refs[nb + 1]):
            cp.start()

    outs = _pallas(
        body, [pltpu.with_memory_space_constraint(b, pltpu.HBM) for b in bufs], dep=dep, name=name,
        out_shape=(pltpu.SemaphoreType.DMA((n_copies,)), pltpu.SemaphoreType.DMA((n_copies,)),
                   *[pltpu.HBM(b.shape, b.dtype) for b in bufs]),
        in_specs=[_HBM] * nb,
        out_specs=(_SEM, _SEM, *[_HBM] * nb),
        input_output_aliases={i: 2 + i for i in range(nb)},
        compiler_params=pltpu.CompilerParams(has_side_effects=_EFFECT))
    return Started(outs[0], outs[1], list(outs[2:2 + nb]))


def exchange_wait(name, started, plan, dep=None):
    nb = len(started.bufs)

    def body(*refs):
        for cp in plan(refs[:nb], refs[nb], refs[nb + 1]):
            cp.wait_send()
            cp.wait_recv()

    outs = _pallas(
        body, [*started.bufs, started.send_sems, started.recv_sems], dep=dep, name=name,
        out_shape=tuple(pltpu.HBM(b.shape, b.dtype) for b in started.bufs),
        in_specs=[_HBM] * nb + [_SEM, _SEM],
        out_specs=tuple([_HBM] * nb),
        input_output_aliases={i: i for i in range(nb)},
        compiler_params=pltpu.CompilerParams(has_side_effects=_EFFECT))
    return list(outs)


def _remote(src, dst, send_sems, recv_sems, i, to):
    return pltpu.make_async_remote_copy(src_ref=src, dst_ref=dst, send_sem=send_sems.at[i], recv_sem=recv_sems.at[i],
                                        device_id=to, device_id_type=MESH)


def _half_rows(buf_rows, chip_idx, pc):
    half = buf_rows // (2 * N_CHIPS)
    return pl.ds((2 * chip_idx + pc) * half, half)


def plan_gather_ici(refs, send_sems, recv_sems):
    x, y, c = _position()
    copies = []
    for k, ref in enumerate(refs):
        rows = ref.at[_half_rows(ref.shape[0], 2 * x + y, c), :]
        for j, chip in enumerate(_other_chips(x, y)):
            copies.append(_remote(rows, rows, send_sems, recv_sems, 3 * k + j, (*chip, c)))
    return copies


def plan_gather_d2d(refs, send_sems, recv_sems):
    x, y, c = _position()
    copies = []
    for k, ref in enumerate(refs):
        for j, (px, py) in enumerate(_other_chips(x, y)):
            rows = ref.at[_half_rows(ref.shape[0], 2 * px + py, c), :]
            copies.append(_remote(rows, rows, send_sems, recv_sems, 3 * k + j, (x, y, 1 - c)))
    return copies


def plan_pair_exchange(refs, send_sems, recv_sems):
    x, y, c = _position()
    nw = len(refs) // 2
    copies = []
    for k in range(nw):
        for chip in range(N_CHIPS):
            copies.append(_remote(refs[k].at[chip, 1 - c], refs[nw + k].at[chip], send_sems, recv_sems,
                                  N_CHIPS * k + chip, (x, y, 1 - c)))
    return copies


def plan_chip_exchange(refs, send_sems, recv_sems):
    x, y, c = _position()
    nw = len(refs) // 2
    copies = []
    for k in range(nw):
        for j, (px, py) in enumerate(_other_chips(x, y)):
            copies.append(_remote(refs[k].at[2 * px + py], refs[nw + k].at[2 * x + y], send_sems, recv_sems,
                                  3 * k + j, (px, py, c)))
    return copies


def plan_pair_share(refs, send_sems, recv_sems):
    x, y, c = _position()
    return [_remote(ref.at[c], ref.at[c], send_sems, recv_sems, k, (x, y, 1 - c)) for k, ref in enumerate(refs)]


def cast_into_slot(src, slot, n_slots, name, dep=None):
    r, n = src.shape
    tr, tn = _ew_tiles(r, n, BF16_SUBLANES)

    def body(slot_ref, s_ref, o_ref):
        o_ref[...] = s_ref[...].astype(BF16)

    return _pallas(
        body, [slot, src], dep=dep, n_prefetch=1, grid=(r // tr, n // tn),
        in_specs=[pl.BlockSpec((tr, tn), lambda i, j, sl: (i, j))],
        out_specs=pl.BlockSpec((None, tr, tn), lambda i, j, sl: (sl[0], i, j)),
        out_shape=jax.ShapeDtypeStruct((n_slots, r, n), BF16),
        sem=("parallel", "parallel"), name=name)


def pair_sum(g, r, core, name, dep=None):
    nchip, _, h, n = g.shape
    th, tn = _ew_tiles(h, n, BF16_SUBLANES)

    def body(core_ref, g_ref, r_ref, o_ref):
        o_ref[...] = (g_ref[...].astype(F32) + r_ref[...].astype(F32)).astype(BF16)

    return _pallas(
        body, [core, g, r], dep=dep, n_prefetch=1, grid=(nchip, h // th, n // tn),
        in_specs=[pl.BlockSpec((None, None, th, tn), lambda a, i, j, cr: (a, cr[0], i, j)),
                  pl.BlockSpec((None, th, tn), lambda a, i, j, cr: (a, i, j))],
        out_specs=pl.BlockSpec((None, th, tn), lambda a, i, j, cr: (a, i, j)),
        out_shape=jax.ShapeDtypeStruct((nchip, h, n), BF16),
        sem=("parallel", "parallel", "parallel"), name=name)


def chip_sum(own, got, where, name, dep=None):
    nchip, h, n = got.shape
    th, tn = _ew_tiles(h, n, BF16_SUBLANES, elems=256 * 1024)

    def body(where_ref, own_ref, *rest):
        got_refs, o_ref = rest[:nchip], rest[nchip]
        chip = where_ref[0]
        acc = None
        for s in range(nchip):
            term = jnp.where(chip == s, own_ref[...], got_refs[s][...]).astype(F32)
            acc = term if acc is None else acc + term
        o_ref[...] = acc

    def got_spec(s):
        return pl.BlockSpec((None, th, tn), lambda i, j, wr: (jnp.where(wr[0] == s, (s + 1) % nchip, s), i, j))

    return _pallas(
        body, [where, own, *[got] * nchip], dep=dep, n_prefetch=1, grid=(h // th, n // tn),
        in_specs=[pl.BlockSpec((None, th, tn), lambda i, j, wr: (wr[0], i, j))]
        + [got_spec(s) for s in range(nchip)],
        out_specs=pl.BlockSpec((None, th, tn), lambda i, j, wr: (wr[1], i, j)),
        out_shape=jax.ShapeDtypeStruct((2, h, n), F32),
        sem=("parallel", "parallel"), name=name)


def kernel(x, c, w_ada, b_ada, norm_mix_g, w_in, q_norm_g, k_norm_g, attn_sinks, rel_bias, w_attn_out, conv_w, conv_b, conv_ln_g, conv_ln_b, w_conv_out, w_mix_out, norm_ffn_g, w_ffn_in, w_ffn_out, loss_target, m_w_ada, m_b_ada, m_norm_mix_g, m_w_in, m_q_norm_g, m_k_norm_g, m_attn_sinks, m_rel_bias, m_w_attn_out, m_conv_w, m_conv_b, m_conv_ln_g, m_conv_ln_b, m_w_conv_out, m_w_mix_out, m_norm_ffn_g, m_w_ffn_in, m_w_ffn_out, v_w_ada, v_b_ada, v_norm_mix_g, v_w_in, v_q_norm_g, v_k_norm_g, v_attn_sinks, v_rel_bias, v_w_attn_out, v_conv_w, v_conv_b, v_conv_ln_g, v_conv_ln_b, v_w_conv_out, v_w_mix_out, v_norm_ffn_g, v_w_ffn_in, v_w_ffn_out):
    run = InOrder()
    xi, yi, ci = _position()
    chip = 2 * xi + yi
    me = 2 * chip + ci
    chip_arr = chip.astype(jnp.int32).reshape(1)
    core_arr = ci.astype(jnp.int32).reshape(1)
    where_arr = jnp.stack([chip, ci]).astype(jnp.int32)

    xe, tgt = x[0], loss_target[0]
    t, d = xe.shape
    hd = q_norm_g.shape[-1]
    nq = attn_sinks.shape[-1]
    aw = nq * hd
    ch = conv_b.shape[-1]
    in_width = N_CHIPS * w_in.shape[-1]
    kvw = (in_width - aw - 2 * ch - 2 * d) // 2
    nkv = kvw // hd
    dff = N_CHIPS * w_ffn_out.shape[1]
    off_k, off_v, off_ca = aw, aw + kvw, aw + 2 * kvw
    off_cb, off_ga, off_gc = off_ca + ch, off_ca + 2 * ch, off_ca + 2 * ch + d
    nc_ada = w_ada.shape[-1]
    ch_loc = conv_w.shape[-1]
    nj_ffn = w_ffn_in.shape[-1]
    perm_ffn = ffn_perm(N_CHIPS)

    big = {"w_in": w_in[0], "w_attn_out": w_attn_out[0], "w_conv_out": w_conv_out[0], "w_mix_out": w_mix_out[0],
           "w_ffn_in": w_ffn_in[0], "w_ffn_out": w_ffn_out[0]}
    moments = {"w_in": (m_w_in, v_w_in), "w_attn_out": (m_w_attn_out, v_w_attn_out),
               "w_conv_out": (m_w_conv_out, v_w_conv_out), "w_mix_out": (m_w_mix_out, v_w_mix_out),
               "w_ffn_in": (m_w_ffn_in, v_w_ffn_in), "w_ffn_out": (m_w_ffn_out, v_w_ffn_out)}
    gather_groups = {"in": ["w_in"], "mid": ["w_attn_out", "w_conv_out", "w_mix_out"], "ffn_in": ["w_ffn_in"],
                     "ffn_out": ["w_ffn_out"]}
    grads, deltas, new_m, new_v = {}, {}, {}, {}

    def gather_cast(gname):
        bufs = []
        for n in gather_groups[gname]:
            r, ncol = big[n].shape
            bufs.append(run(cast_into_slot, big[n], chip_arr, N_CHIPS, "cast_" + n).reshape(N_CHIPS * r, ncol))
        return bufs

    def gather_ici_start(gname, bufs):
        return run(exchange_start, "gather_ici_start_" + gname, bufs, 3 * len(bufs), plan_gather_ici)

    def gather_pass_on(gname, ici):
        landed = run(exchange_wait, "gather_ici_wait_" + gname, ici, plan_gather_ici)
        return run(exchange_start, "gather_d2d_start_" + gname, landed, 3 * len(landed), plan_gather_d2d)

    def gathered(gname, d2d):
        outs = run(exchange_wait, "gather_d2d_wait_" + gname, d2d, plan_gather_d2d)
        return [o.reshape(N_CHIPS, *big[n].shape) for o, n in zip(outs, gather_groups[gname])]

    def rs_pair_start(gname, names, partials):
        blocks = [g.reshape(N_CHIPS, 2, big[n].shape[0] // 2, big[n].shape[1]) for n, g in zip(names, partials)]
        land = [lax.empty((N_CHIPS,) + b.shape[2:], BF16) for b in blocks]
        return run(exchange_start, "pair_exchange_start_" + gname, blocks + land, N_CHIPS * len(blocks),
                   plan_pair_exchange)

    def rs_chip_start(gname, names, pair):
        nw = len(names)
        outs = run(exchange_wait, "pair_exchange_wait_" + gname, pair, plan_pair_exchange)
        sums = [run(pair_sum, g, r, core_arr, "pair_sum_" + n) for n, g, r in zip(names, outs[:nw], outs[nw:])]
        land = [lax.empty(s.shape, BF16) for s in sums]
        return run(exchange_start, "chip_exchange_start_" + gname, sums + land, 3 * nw, plan_chip_exchange)

    def rs_share_start(gname, names, chipx):
        nw = len(names)
        outs = run(exchange_wait, "chip_exchange_wait_" + gname, chipx, plan_chip_exchange)
        halves = [run(chip_sum, s, r, where_arr, "chip_sum_" + n) for n, s, r in zip(names, outs[:nw], outs[nw:])]
        return run(exchange_start, "pair_share_start_" + gname, halves, nw, plan_pair_share)

    def rs_finish(gname, names, share):
        fulls = run(exchange_wait, "pair_share_wait_" + gname, share, plan_pair_share)
        for n, g2 in zip(names, fulls):
            g = g2.reshape(big[n].shape)
            dl, nm, nv = run(adamw, big[n], g, moments[n][0][0], moments[n][1][0], "adamw_" + n)
            grads[n], deltas[n], new_m[n], new_v[n] = g[None], dl[None], nm[None], nv[None]

    bufs_in = gather_cast("in")
    row1, offs1 = _row_pack([c, conv_w[0].reshape(1, CONV_WIDTH * ch_loc)])
    got1 = run(allgather_small, row1, "allgather_cond")
    c_all = got1[:, 0, :d]
    conv_w_full = got1[0::2, 0, offs1[1]:offs1[1] + CONV_WIDTH * ch_loc].reshape(N_CHIPS, CONV_WIDTH, ch_loc)
    conv_w_full = jnp.transpose(conv_w_full, (1, 0, 2)).reshape(CONV_WIDTH, ch)
    conv_w_pad = jnp.pad(conv_w_full, ((0, 1), (0, 0)))
    c_t = jnp.transpose(c_all)
    mod_cols = run(ada_fwd, c_t, w_ada[0])
    got2 = run(allgather_small, mod_cols, "allgather_mod")
    mod_all = got2.reshape(N_CHIPS, 2, N_DEV, nc_ada)[:, 0]
    mod = lax.dynamic_slice_in_dim(mod_all, me, 1, axis=1).reshape(1, N_CHIPS * nc_ada) + b_ada
    mod = jnp.pad(mod.reshape(N_MOD, d), ((0, SUBLANES - N_MOD), (0, 0)))

    ici = {"in": gather_ici_start("in", bufs_in)}
    for gname in ("mid", "ffn_in", "ffn_out"):
        ici[gname] = gather_ici_start(gname, gather_cast(gname))

    h = run(pre_mix_fwd, xe, mod, norm_mix_g)
    bucket = _t5_bucket_table()
    bucket_p, bucket_c = jnp.asarray(bucket[:, :BLOCK]), jnp.asarray(bucket[:, BLOCK:])
    bias_p, bias_c = run(bias_table, rel_bias, bucket_p, bucket_c)
    d2d_in = gather_pass_on("in", ici["in"])
    (wg_in,) = gathered("in", d2d_in)
    p = run(mm_nn, h, wg_in, tn=_tile(wg_in.shape[2], 640), tk=d, out_dtype=F32, name="mm_in")
    d2d_mid = gather_pass_on("mid", ici["mid"])

    def heads(a, n):
        return jnp.transpose(a.reshape(t, n, hd), (1, 0, 2))

    def unheads(a):
        return jnp.transpose(a, (1, 0, 2)).reshape(t, -1)

    qh, kh, vh = heads(p[:, :aw], nq), heads(p[:, off_k:off_v], nkv), heads(p[:, off_v:off_ca], nkv)
    sinks3 = attn_sinks.reshape(nq, 1, 1)
    attn_o = unheads(run(attn_fwd, qh, kh, vh, bias_p, bias_c, sinks3, q_norm_g, k_norm_g))
    ca, cb = p[:, off_ca:off_cb], p[:, off_cb:off_ga]
    s_conv, co_conv = run(conv_fwd, ca, cb, conv_w_pad, conv_b, conv_ln_g, conv_ln_b)
    wg_attn_out, wg_conv_out, wg_mix_out = gathered("mid", d2d_mid)
    wg_mix_out = wg_mix_out.reshape(1, d, d)
    y_attn = run(mm_nn, attn_o, wg_attn_out, tn=_tile(wg_attn_out.shape[2], 512), tk=aw, out_dtype=BF16,
                 name="mm_attn_out")
    y_conv = run(mm_nn, s_conv, wg_conv_out, tn=_tile(wg_conv_out.shape[2], 512), tk=ch, out_dtype=BF16,
                 name="mm_conv_out")
    merged = run(merge_fwd, p, y_attn, y_conv, off_ga, off_gc)
    d2d_ffn_in = gather_pass_on("ffn_in", ici["ffn_in"])
    o_m = run(mm_nn, merged, wg_mix_out, tn=_tile(d, 512), tk=d, out_dtype=F32, name="mm_mix_out")
    x1, h2 = run(pre_ffn_fwd, xe, o_m, mod, norm_ffn_g)
    (wg_ffn_in,) = gathered("ffn_in", d2d_ffn_in)
    f = run(mm_nn, h2, wg_ffn_in, tn=_tile(nj_ffn, 1408), tk=d, out_dtype=BF16, name="mm_ffn_in", perm=perm_ffn)
    d2d_ffn_out = gather_pass_on("ffn_out", ici["ffn_out"])
    act = run(swiglu_fwd, f, nj_ffn)
    (wg_ffn_out,) = gathered("ffn_out", d2d_ffn_out)
    wg_ffn_out = wg_ffn_out.reshape(1, dff, d)
    o_f = run(mm_nn, act, wg_ffn_out, tn=_tile(d, 1024), tk=_tile(dff, 1408), out_dtype=F32, name="mm_ffn_out")
    loss11, dy, dof, acc_l = run(loss_head, x1, o_f, tgt, mod)

    gw_ffn_out = run(mm_tn, act, dof, 1, tk=_tile(dff, 512), tn=d, name="mm_ffn_out_dw")
    px_ffn_out = rs_pair_start("ffn_out", ["w_ffn_out"], [gw_ffn_out])
    dact = run(mm_nt, dof, wg_ffn_out, tko=_tile(dff, 512), tn=d, out_dtype=BF16, name="mm_ffn_out_dx")
    cx_ffn_out = rs_chip_start("ffn_out", ["w_ffn_out"], px_ffn_out)
    df = run(swiglu_bwd, f, dact, nj_ffn)
    gw_ffn_in = run(mm_tn, h2, df, N_CHIPS, tk=d, tn=_tile(nj_ffn, 1408), name="mm_ffn_in_dw",
                    perm=perm_ffn)
    px_ffn_in = rs_pair_start("ffn_in", ["w_ffn_in"], [gw_ffn_in])
    dh2 = run(mm_nt, df, wg_ffn_in, tko=_tile(d, 512), tn=nj_ffn, name="mm_ffn_in_dx", perm=perm_ffn)
    sh_ffn_out = rs_share_start("ffn_out", ["w_ffn_out"], cx_ffn_out)
    cx_ffn_in = rs_chip_start("ffn_in", ["w_ffn_in"], px_ffn_in)
    dx1, dom, acc_f = run(pre_ffn_bwd, x1, dh2, dy, o_m, mod, norm_ffn_g)
    gw_mix_out = run(mm_tn, merged, dom, 1, tk=d, tn=_tile(d, 1024), name="mm_mix_out_dw")
    px_mix = rs_pair_start("mix_out", ["w_mix_out"], [gw_mix_out])
    dmerged = run(mm_nt, dom, wg_mix_out, tko=_tile(d, 512), tn=d, out_dtype=BF16, name="mm_mix_out_dx")
    dy_attn, dy_conv, dga, dgc = run(merge_bwd, p, y_attn, y_conv, dmerged, off_ga, off_gc)
    rs_finish("ffn_out", ["w_ffn_out"], sh_ffn_out)
    cx_mix = rs_chip_start("mix_out", ["w_mix_out"], px_mix)
    gw_attn_out = run(mm_tn, attn_o, dy_attn, N_CHIPS, tk=aw, tn=_tile(wg_attn_out.shape[2], 512),
                      name="mm_attn_out_dw")
    gw_conv_out = run(mm_tn, s_conv, dy_conv, N_CHIPS, tk=ch, tn=_tile(wg_conv_out.shape[2], 512),
                      name="mm_conv_out_dw")
    ac_names = ["w_attn_out", "w_conv_out"]
    px_ac = rs_pair_start("attn_conv_out", ac_names, [gw_attn_out, gw_conv_out])
    dattn_o = run(mm_nt, dy_attn, wg_attn_out, tko=_tile(aw, 1024), tn=_tile(wg_attn_out.shape[2], 512),
                  out_dtype=BF16, name="mm_attn_out_dx")
    ds_conv = run(mm_nt, dy_conv, wg_conv_out, tko=_tile(ch, 1024), tn=_tile(wg_conv_out.shape[2], 512),
                  out_dtype=BF16, name="mm_conv_out_dx")
    cx_ac = rs_chip_start("attn_conv_out", ac_names, px_ac)
    dca, dcb, dconv_w, dconv_vec = run(conv_bwd, ca, cb, co_conv, ds_conv, conv_w_pad, conv_ln_g, conv_ln_b)
    sh_ffn_in = rs_share_start("ffn_in", ["w_ffn_in"], cx_ffn_in)
    dqh, dkh, dvh, dbp, dbc, dsinks, dqg, dkg = run(attn_bwd, qh, kh, vh, bias_p, bias_c, sinks3, q_norm_g, k_norm_g,
                                                     heads(dattn_o, nq))
    sh_mix = rs_share_start("mix_out", ["w_mix_out"], cx_mix)
    sh_ac = rs_share_start("attn_conv_out", ac_names, cx_ac)
    drel = run(bias_table_bwd, dbp, dbc, bucket_p, bucket_c).reshape(NUM_BUCKETS, nq)
    dp = jnp.concatenate([unheads(dqh).astype(BF16), unheads(dkh).astype(BF16), unheads(dvh).astype(BF16),
                          dca, dcb, dga, dgc], axis=1)
    gw_in = run(mm_tn, h, dp, N_CHIPS, tk=d, tn=_tile(wg_in.shape[2], 640), name="mm_in_dw")
    px_in = rs_pair_start("in", ["w_in"], [gw_in])
    dh = run(mm_nt, dp, wg_in, tko=_tile(d, 1024), tn=wg_in.shape[2], name="mm_in_dx")
    grad_x, acc_m = run(pre_mix_bwd, xe, dh, dx1, mod, norm_mix_g)

    dmod = jnp.concatenate([acc_m[0:1], acc_m[1:2], acc_f[3:4], acc_f[0:1], acc_f[1:2], acc_l[0:1]], axis=1)
    small_names = ["b_ada", "norm_mix_g", "q_norm_g", "k_norm_g", "attn_sinks", "rel_bias", "conv_b", "conv_ln_g",
                   "conv_ln_b", "norm_ffn_g"]
    small_w = [b_ada, norm_mix_g, q_norm_g, k_norm_g, attn_sinks, rel_bias, conv_b, conv_ln_g, conv_ln_b, norm_ffn_g]
    small_m = [m_b_ada, m_norm_mix_g, m_q_norm_g, m_k_norm_g, m_attn_sinks, m_rel_bias, m_conv_b, m_conv_ln_g,
               m_conv_ln_b, m_norm_ffn_g]
    small_v = [v_b_ada, v_norm_mix_g, v_q_norm_g, v_k_norm_g, v_attn_sinks, v_rel_bias, v_conv_b, v_conv_ln_g,
               v_conv_ln_b, v_norm_ffn_g]
    small_g = [dmod, acc_m[2:3], dqg, dkg, dsinks.reshape(1, nq), drel.reshape(1, NUM_BUCKETS * nq),
               dconv_vec[0:1], dconv_vec[1:2], dconv_vec[2:3], acc_f[2:3]]
    row3, offs3 = _row_pack(small_g + [dconv_w[:CONV_WIDTH].reshape(1, CONV_WIDTH * ch), loss11])
    got3 = run(allgather_small, row3, "allgather_small_grads")
    cx_in = rs_chip_start("in", ["w_in"], px_in)
    as_row = lambda a: a.reshape(1, -1)
    outs3 = run(small_sum_adamw, got3, offs3, [as_row(a) for a in small_w], [as_row(a) for a in small_m],
                [as_row(a) for a in small_v], [CONV_WIDTH * ch, 1])
    for i, (n, w) in enumerate(zip(small_names, small_w)):
        grads[n], deltas[n], new_m[n], new_v[n] = (o.reshape(w.shape) for o in outs3[4 * i:4 * i + 4])
    g_conv_w_all, loss_sum = outs3[-2].reshape(CONV_WIDTH, ch), outs3[-1]

    g_conv_w = lax.dynamic_slice_in_dim(g_conv_w_all, chip * ch_loc, ch_loc, axis=1)
    grads["conv_w"] = g_conv_w[None]
    dl, nm, nv = run(adamw, conv_w[0], g_conv_w, m_conv_w[0], v_conv_w[0], "adamw_conv_w")
    deltas["conv_w"], new_m["conv_w"], new_v["conv_w"] = dl[None], nm[None], nv[None]

    dmod_all = got3[:, 0, :N_MOD * d]
    dmod_cols = lax.dynamic_slice_in_dim(dmod_all, chip * nc_ada, nc_ada, axis=1)
    g_ada, dl, nm, nv = run(ada_grad_adamw, c_t, dmod_cols, w_ada[0], m_w_ada[0], v_w_ada[0])
    grads["w_ada"], deltas["w_ada"], new_m["w_ada"], new_v["w_ada"] = g_ada[None], dl[None], nm[None], nv[None]

    rs_finish("ffn_in", ["w_ffn_in"], sh_ffn_in)
    rs_finish("mix_out", ["w_mix_out"], sh_mix)
    rs_finish("attn_conv_out", ac_names, sh_ac)
    sh_in = rs_share_start("in", ["w_in"], cx_in)
    rs_finish("in", ["w_in"], sh_in)

    loss = loss_sum[0, 0]
    order = ["w_ada", "b_ada", "norm_mix_g", "w_in", "q_norm_g", "k_norm_g", "attn_sinks", "rel_bias", "w_attn_out",
             "conv_w", "conv_b", "conv_ln_g", "conv_ln_b", "w_conv_out", "w_mix_out", "norm_ffn_g", "w_ffn_in",
             "w_ffn_out"]
    return (loss, grad_x[None], *[grads[n] for n in order], *[deltas[n] for n in order],
            *[new_m[n] for n in order], *[new_v[n] for n in order])
```

```python
import functools
import math
from typing import Any, NamedTuple

import jax
import jax.numpy as jnp
import numpy as np
from jax import lax
from jax.experimental import pallas as pl
from jax.experimental.pallas import tpu as pltpu

F32 = jnp.float32
BF16 = jnp.bfloat16
MESH = pl.DeviceIdType.MESH

V7X_VMEM_BYTES = 64 * 1024 * 1024
VMEM_LIMIT = V7X_VMEM_BYTES - 8 * 1024 * 1024
LANES = 128
SUBLANES = 8
BF16_SUBLANES = 16

EPS = 1e-6
WINDOW = 128
BLOCK = 128
NUM_BUCKETS = 32
MAX_EXACT = NUM_BUCKETS // 2
MAX_DISTANCE = 128
CONV_WIDTH = 31
CONV_HALO = 32
ADAM_LR = 0.001
ADAM_B1 = 0.9
ADAM_B2 = 0.999
ADAM_EPS = 1e-08
ADAM_WD = 0.01
ADAM_STEP = 10
N_MOD = 6
SH_M, SC_M, GT_M, SH_F, SC_F, GT_F = range(6)

N_CHIPS = 4
N_DEV = 8

_ANY = pl.BlockSpec(memory_space=pl.ANY)
_VMEM = pl.BlockSpec(memory_space=pltpu.VMEM)
_SMEM = pl.BlockSpec(memory_space=pltpu.SMEM)
_HBM = pl.BlockSpec(memory_space=pltpu.HBM)
_SEM = pl.BlockSpec(memory_space=pltpu.SEMAPHORE)
_EFFECT = pltpu.SideEffectType.DATAFLOW_SIDE_EFFECTING


class InOrder:
    def __init__(self):
        self.token = None

    def __call__(self, fn, *args, **kw):
        return fn(*args, dep=self, **kw)


def _pallas(body, args, *, in_specs, out_specs, out_shape, name, dep=None, grid=(), n_prefetch=0, scratch=(),
            sem=None, **kw):
    n_lead = n_prefetch + len(in_specs)
    in_specs, args = list(in_specs), list(args)
    single = not isinstance(out_shape, (list, tuple))
    out_shapes = [out_shape] if single else list(out_shape)
    out_specs = [out_specs] if single else list(out_specs)
    if dep is not None:
        inner, n_out, takes = body, len(out_shapes), dep.token is not None

        def body(*refs):
            rest = refs[n_lead + (1 if takes else 0):]
            rest[n_out][...] = jnp.zeros((SUBLANES, LANES), F32)
            return inner(*refs[:n_lead], *rest[:n_out], *rest[n_out + 1:])

        if takes:
            in_specs.append(_ANY)
            args.append(dep.token)
        out_shapes.append(jax.ShapeDtypeStruct((SUBLANES, LANES), F32))
        out_specs.append(pl.BlockSpec((SUBLANES, LANES), lambda *_: (0, 0)))
    params = kw.pop("compiler_params", None)
    if params is None:
        params = pltpu.CompilerParams(dimension_semantics=sem, vmem_limit_bytes=VMEM_LIMIT)
    outs = pl.pallas_call(
        body,
        grid_spec=pltpu.PrefetchScalarGridSpec(num_scalar_prefetch=n_prefetch, grid=grid, in_specs=in_specs,
                                               out_specs=out_specs, scratch_shapes=list(scratch)),
        out_shape=out_shapes, compiler_params=params, name=name, **kw,
    )(*args)
    if dep is not None:
        dep.token = outs[-1]
        outs = outs[:-1]
    return outs[0] if single else list(outs)


def _tile(n, pref, unit=LANES):
    best = None
    for t in range(unit, min(n, pref) + 1, unit):
        if n % t == 0:
            best = t
    return best if best is not None else n


def _sigmoid(v):
    return 1.0 / (1.0 + jnp.exp(-v))


ROW_CHUNK = 512


def _row_chunks(m, unit=SUBLANES):
    step = _tile(m, ROW_CHUNK, unit)
    return [(s, step) for s in range(0, m, step)]


def _ew_tiles(r, n, unit=SUBLANES, elems=512 * 1024):
    return _tile(r, max(unit, elems // n), unit), n


def _block_pos(j, perm):
    if perm is None:
        return j
    pos = 0
    for a, p in enumerate(perm):
        pos = pos + jnp.where(j == a, p, 0)
    return pos


def mm_nn(a, w, *, tn, tk, out_dtype, name, perm=None, dep=None):
    m, k = a.shape
    j, k2, nj = w.shape
    assert k == k2 and nj % tn == 0 and k % tk == 0
    npj, nk = nj // tn, k // tk

    def body(a_ref, w_ref, o_ref, *scratch):
        kk = pl.program_id(1)
        for s, sz in _row_chunks(m):
            rows = pl.ds(s, sz)
            p = jnp.dot(a_ref[rows, :], w_ref[...], preferred_element_type=F32)
            if nk == 1:
                o_ref[rows, :] = p.astype(out_dtype)
            else:
                acc = scratch[0]

                @pl.when(kk == 0)
                def _():
                    acc[rows, :] = p

                @pl.when(kk > 0)
                def _():
                    acc[rows, :] += p

                @pl.when(kk == nk - 1)
                def _():
                    o_ref[rows, :] = acc[rows, :].astype(out_dtype)

    return _pallas(
        body, [a, w], dep=dep, grid=(j * npj, nk),
        in_specs=[
            pl.BlockSpec((m, tk), lambda n, kk: (0, kk)),
            pl.BlockSpec((None, tk, tn), lambda n, kk: (n // npj, kk, n % npj)),
        ],
        out_specs=pl.BlockSpec((m, tn), lambda n, kk: (0, _block_pos(n // npj, perm) * npj + n % npj)),
        out_shape=jax.ShapeDtypeStruct((m, j * nj), out_dtype),
        scratch=[pltpu.VMEM((m, tn), F32)] if nk > 1 else [],
        sem=("parallel", "arbitrary"), name=name)


def mm_nt(g, w, *, tko, tn, name, out_dtype=F32, perm=None, dep=None):
    m, n = g.shape
    j, k, nj = w.shape
    assert n == j * nj and nj % tn == 0 and k % tko == 0
    npj, nr = nj // tn, n // tn
    in_place = out_dtype == F32

    def body(g_ref, w_ref, o_ref, *scratch):
        r = pl.program_id(1)
        acc = o_ref if in_place else (scratch[0] if nr > 1 else None)
        for s, sz in _row_chunks(m):
            rows = pl.ds(s, sz)
            p = lax.dot_general(g_ref[rows, :], w_ref[...], (((1,), (1,)), ((), ())), preferred_element_type=F32)
            if acc is None:
                o_ref[rows, :] = p.astype(out_dtype)
                continue

            @pl.when(r == 0)
            def _():
                acc[rows, :] = p

            @pl.when(r > 0)
            def _():
                acc[rows, :] += p

            if not in_place:
                @pl.when(r == nr - 1)
                def _():
                    o_ref[rows, :] = acc[rows, :].astype(out_dtype)

    return _pallas(
        body, [g, w], dep=dep, grid=(k // tko, nr),
        in_specs=[
            pl.BlockSpec((m, tn), lambda ko, r: (0, _block_pos(r // npj, perm) * npj + r % npj)),
            pl.BlockSpec((None, tko, tn), lambda ko, r: (r // npj, ko, r % npj)),
        ],
        out_specs=pl.BlockSpec((m, tko), lambda ko, r: (0, ko)),
        out_shape=jax.ShapeDtypeStruct((m, k), out_dtype),
        scratch=[pltpu.VMEM((m, tko), F32)] if (nr > 1 and not in_place) else [],
        sem=("parallel", "arbitrary"), name=name)


def mm_tn(a, g, n_blocks, *, tk, tn, name, perm=None, dep=None):
    m, k = a.shape
    m2, n = g.shape
    nj = n // n_blocks
    assert m == m2 and nj % tn == 0 and k % tk == 0
    npj = nj // tn

    def body(a_ref, g_ref, o_ref):
        for s, sz in _row_chunks(tk, LANES):
            p = lax.dot_general(a_ref[:, pl.ds(s, sz)], g_ref[...], (((0,), (0,)), ((), ())),
                                preferred_element_type=F32)
            o_ref[pl.ds(s, sz), :] = p.astype(BF16)

    return _pallas(
        body, [a, g], dep=dep, grid=(k // tk, n // tn),
        in_specs=[
            pl.BlockSpec((m, tk), lambda kk, nn: (0, kk)),
            pl.BlockSpec((m, tn), lambda kk, nn: (0, _block_pos(nn // npj, perm) * npj + nn % npj)),
        ],
        out_specs=pl.BlockSpec((None, tk, tn), lambda kk, nn: (nn // npj, kk, nn % npj)),
        out_shape=jax.ShapeDtypeStruct((n_blocks, k, nj), BF16),
        sem=("parallel", "parallel"), name=name)


ROW_TILE = 256


def _row_spec(tr, width):
    return pl.BlockSpec((tr, width), lambda i: (i, 0))


def _full_spec(shape):
    return pl.BlockSpec(shape, lambda *_: (0,) * len(shape))


def _rms(xv):
    return lax.rsqrt(jnp.mean(xv * xv, axis=-1, keepdims=True) + EPS)


def _mod_row(mod_ref, row):
    return mod_ref[pl.ds(row, 1), :]


def pre_mix_fwd(x, mod, gain, dep=None):
    t, d = x.shape
    tr = _tile(t, ROW_TILE, SUBLANES)

    def body(x_ref, mod_ref, g_ref, h_ref):
        xv = x_ref[...]
        y = xv * _rms(xv) * g_ref[...]
        h_ref[...] = (y * (1.0 + _mod_row(mod_ref, SC_M)) + _mod_row(mod_ref, SH_M)).astype(BF16)

    return _pallas(
        body, [x, mod, gain], dep=dep, grid=(t // tr,),
        in_specs=[_row_spec(tr, d), _full_spec(mod.shape), _full_spec(gain.shape)],
        out_specs=_row_spec(tr, d),
        out_shape=jax.ShapeDtypeStruct((t, d), BF16),
        sem=("parallel",), name="pre_mix_fwd")


def pre_ffn_fwd(x, o_m, mod, gain, dep=None):
    t, d = x.shape
    tr = _tile(t, ROW_TILE, SUBLANES)

    def body(x_ref, om_ref, mod_ref, g_ref, x1_ref, h_ref):
        x1 = x_ref[...] + _mod_row(mod_ref, GT_M) * om_ref[...]
        x1_ref[...] = x1
        y = x1 * _rms(x1) * g_ref[...]
        h_ref[...] = (y * (1.0 + _mod_row(mod_ref, SC_F)) + _mod_row(mod_ref, SH_F)).astype(BF16)

    return _pallas(
        body, [x, o_m, mod, gain], dep=dep, grid=(t // tr,),
        in_specs=[_row_spec(tr, d), _row_spec(tr, d), _full_spec(mod.shape), _full_spec(gain.shape)],
        out_specs=[_row_spec(tr, d), _row_spec(tr, d)],
        out_shape=[jax.ShapeDtypeStruct((t, d), F32), jax.ShapeDtypeStruct((t, d), BF16)],
        sem=("parallel",), name="pre_ffn_fwd")


def loss_head(x1, o_f, target, mod, dep=None):
    t, d = x1.shape
    tr = _tile(t, ROW_TILE, SUBLANES)

    def body(x1_ref, of_ref, tg_ref, mod_ref, loss_ref, dy_ref, dof_ref, acc_ref):
        i = pl.program_id(0)
        gt = _mod_row(mod_ref, GT_F)
        of = of_ref[...]
        err = x1_ref[...] + gt * of - tg_ref[...]
        dy = err * (1.0 / d)
        dy_ref[...] = dy
        dof_ref[...] = (dy * gt).astype(BF16)
        part = (0.5 / d) * jnp.sum(jnp.sum(err * err, axis=1, keepdims=True), axis=0, keepdims=True)
        dgt = jnp.sum(dy * of, axis=0, keepdims=True)

        @pl.when(i == 0)
        def _():
            loss_ref[...] = jnp.zeros_like(loss_ref)
            acc_ref[...] = jnp.zeros_like(acc_ref)

        loss_ref[...] += part
        acc_ref[pl.ds(0, 1), :] += dgt

    return _pallas(
        body, [x1, o_f, target, mod], dep=dep, grid=(t // tr,),
        in_specs=[_row_spec(tr, d), _row_spec(tr, d), _row_spec(tr, d), _full_spec(mod.shape)],
        out_specs=[_full_spec((1, 1)), _row_spec(tr, d), _row_spec(tr, d), _full_spec((SUBLANES, d))],
        out_shape=[jax.ShapeDtypeStruct((1, 1), F32), jax.ShapeDtypeStruct((t, d), F32),
                   jax.ShapeDtypeStruct((t, d), BF16), jax.ShapeDtypeStruct((SUBLANES, d), F32)],
        sem=("arbitrary",), name="loss_head")


def _norm_bwd(xv, dh, sc, gain):
    rstd = _rms(xv)
    yn = xv * rstd
    dsh = jnp.sum(dh, axis=0, keepdims=True)
    dsc = jnp.sum(dh * (yn * gain), axis=0, keepdims=True)
    dgain = jnp.sum(dh * (1.0 + sc) * yn, axis=0, keepdims=True)
    dyn = dh * ((1.0 + sc) * gain)
    dx = rstd * (dyn - yn * jnp.mean(dyn * yn, axis=-1, keepdims=True))
    return dx, dsh, dsc, dgain


def pre_ffn_bwd(x1, dh2, dy, o_m, mod, gain, dep=None):
    t, d = x1.shape
    tr = _tile(t, ROW_TILE, SUBLANES)

    def body(x1_ref, dh_ref, dy_ref, om_ref, mod_ref, g_ref, dx1_ref, dom_ref, acc_ref):
        i = pl.program_id(0)
        dxn, dsh, dsc, dgain = _norm_bwd(x1_ref[...], dh_ref[...], _mod_row(mod_ref, SC_F), g_ref[...])
        dx1 = dy_ref[...] + dxn
        dx1_ref[...] = dx1
        dom_ref[...] = (dx1 * _mod_row(mod_ref, GT_M)).astype(BF16)
        dgt = jnp.sum(dx1 * om_ref[...], axis=0, keepdims=True)

        @pl.when(i == 0)
        def _():
            acc_ref[...] = jnp.zeros_like(acc_ref)

        acc_ref[pl.ds(0, 1), :] += dsh
        acc_ref[pl.ds(1, 1), :] += dsc
        acc_ref[pl.ds(2, 1), :] += dgain
        acc_ref[pl.ds(3, 1), :] += dgt

    return _pallas(
        body, [x1, dh2, dy, o_m, mod, gain], dep=dep, grid=(t // tr,),
        in_specs=[_row_spec(tr, d)] * 4 + [_full_spec(mod.shape), _full_spec(gain.shape)],
        out_specs=[_row_spec(tr, d), _row_spec(tr, d), _full_spec((SUBLANES, d))],
        out_shape=[jax.ShapeDtypeStruct((t, d), F32), jax.ShapeDtypeStruct((t, d), BF16),
                   jax.ShapeDtypeStruct((SUBLANES, d), F32)],
        sem=("arbitrary",), name="pre_ffn_bwd")


def pre_mix_bwd(x, dh, dx1, mod, gain, dep=None):
    t, d = x.shape
    tr = _tile(t, ROW_TILE, SUBLANES)

    def body(x_ref, dh_ref, dx1_ref, mod_ref, g_ref, gx_ref, acc_ref):
        i = pl.program_id(0)
        dxn, dsh, dsc, dgain = _norm_bwd(x_ref[...], dh_ref[...], _mod_row(mod_ref, SC_M), g_ref[...])
        gx_ref[...] = dx1_ref[...] + dxn

        @pl.when(i == 0)
        def _():
            acc_ref[...] = jnp.zeros_like(acc_ref)

        acc_ref[pl.ds(0, 1), :] += dsh
        acc_ref[pl.ds(1, 1), :] += dsc
        acc_ref[pl.ds(2, 1), :] += dgain

    return _pallas(
        body, [x, dh, dx1, mod, gain], dep=dep, grid=(t // tr,),
        in_specs=[_row_spec(tr, d)] * 3 + [_full_spec(mod.shape), _full_spec(gain.shape)],
        out_specs=[_row_spec(tr, d), _full_spec((SUBLANES, d))],
        out_shape=[jax.ShapeDtypeStruct((t, d), F32), jax.ShapeDtypeStruct((SUBLANES, d), F32)],
        sem=("arbitrary",), name="pre_mix_bwd")


def merge_fwd(p, y_attn, y_conv, off_ga, off_gc, dep=None):
    t, d = y_attn.shape
    tr = _tile(t, ROW_TILE, SUBLANES)
    cw = math.gcd(math.gcd(off_ga, off_gc), math.gcd(d, 512))
    nc = d // cw

    def body(ga_ref, gc_ref, ya_ref, yc_ref, o_ref):
        o_ref[...] = (_sigmoid(ga_ref[...]) * ya_ref[...] + _sigmoid(gc_ref[...]) * yc_ref[...]).astype(BF16)

    return _pallas(
        body, [p, p, y_attn, y_conv], dep=dep, grid=(t // tr, nc),
        in_specs=[pl.BlockSpec((tr, cw), lambda i, j: (i, off_ga // cw + j)),
                  pl.BlockSpec((tr, cw), lambda i, j: (i, off_gc // cw + j)),
                  pl.BlockSpec((tr, cw), lambda i, j: (i, j)),
                  pl.BlockSpec((tr, cw), lambda i, j: (i, j))],
        out_specs=pl.BlockSpec((tr, cw), lambda i, j: (i, j)),
        out_shape=jax.ShapeDtypeStruct((t, d), BF16),
        sem=("parallel", "parallel"), name="merge_fwd")


def merge_bwd(p, y_attn, y_conv, dmerged, off_ga, off_gc, dep=None):
    t, d = y_attn.shape
    tr = _tile(t, ROW_TILE, SUBLANES)
    cw = math.gcd(math.gcd(off_ga, off_gc), math.gcd(d, 512))
    nc = d // cw

    def body(ga_ref, gc_ref, ya_ref, yc_ref, dm_ref, dya_ref, dyc_ref, dga_ref, dgc_ref):
        dm = dm_ref[...]
        sa = _sigmoid(ga_ref[...])
        sc = _sigmoid(gc_ref[...])
        dya_ref[...] = (dm * sa).astype(BF16)
        dyc_ref[...] = (dm * sc).astype(BF16)
        dga_ref[...] = (dm * ya_ref[...] * sa * (1.0 - sa)).astype(BF16)
        dgc_ref[...] = (dm * yc_ref[...] * sc * (1.0 - sc)).astype(BF16)

    blk = pl.BlockSpec((tr, cw), lambda i, j: (i, j))
    return _pallas(
        body, [p, p, y_attn, y_conv, dmerged], dep=dep, grid=(t // tr, nc),
        in_specs=[pl.BlockSpec((tr, cw), lambda i, j: (i, off_ga // cw + j)),
                  pl.BlockSpec((tr, cw), lambda i, j: (i, off_gc // cw + j)), blk, blk, blk],
        out_specs=[blk] * 4,
        out_shape=[jax.ShapeDtypeStruct((t, d), BF16)] * 4,
        sem=("parallel", "parallel"), name="merge_bwd")


def ffn_perm(n_blocks):
    half = n_blocks // 2
    return tuple(2 * j if j < half else 2 * (j - half) + 1 for j in range(n_blocks))


def swiglu_fwd(f, nj, dep=None):
    t, two = f.shape
    tr = _tile(t, ROW_TILE, SUBLANES)
    npair = two // (2 * nj)

    def body(f_ref, o_ref):
        g = f_ref[:, :nj].astype(F32)
        u = f_ref[:, nj:].astype(F32)
        o_ref[...] = (g * _sigmoid(g) * u).astype(BF16)

    return _pallas(
        body, [f], dep=dep, grid=(t // tr, npair),
        in_specs=[pl.BlockSpec((tr, 2 * nj), lambda i, j: (i, j))],
        out_specs=pl.BlockSpec((tr, nj), lambda i, j: (i, j)),
        out_shape=jax.ShapeDtypeStruct((t, two // 2), BF16),
        sem=("parallel", "parallel"), name="swiglu_fwd")


def swiglu_bwd(f, dact, nj, dep=None):
    t, two = f.shape
    tr = _tile(t, ROW_TILE, SUBLANES)
    npair = two // (2 * nj)

    def body(f_ref, da_ref, o_ref):
        g = f_ref[:, :nj].astype(F32)
        u = f_ref[:, nj:].astype(F32)
        da = da_ref[...]
        s = _sigmoid(g)
        o_ref[:, :nj] = (da * u * (s * (1.0 + g * (1.0 - s)))).astype(BF16)
        o_ref[:, nj:] = (da * (g * s)).astype(BF16)

    return _pallas(
        body, [f, dact], dep=dep, grid=(t // tr, npair),
        in_specs=[pl.BlockSpec((tr, 2 * nj), lambda i, j: (i, j)), pl.BlockSpec((tr, nj), lambda i, j: (i, j))],
        out_specs=pl.BlockSpec((tr, 2 * nj), lambda i, j: (i, j)),
        out_shape=jax.ShapeDtypeStruct((t, two), BF16),
        sem=("parallel", "parallel"), name="swiglu_bwd")


def _t5_bucket_table():
    q_off = np.arange(BLOCK)
    k_off = np.arange(2 * BLOCK)
    dist = q_off[:, None] + BLOCK - k_off[None, :]
    n = np.maximum(dist, 0)
    nf = np.maximum(n, 1).astype(np.float32)
    large = MAX_EXACT + (np.log(nf / np.float32(MAX_EXACT)) / np.float32(math.log(MAX_DISTANCE / MAX_EXACT))
                         * np.float32(NUM_BUCKETS - MAX_EXACT)).astype(np.int32)
    large = np.minimum(large, NUM_BUCKETS - 1)
    bucket = np.where(n < MAX_EXACT, n, large).astype(np.int32)
    allowed = (dist >= 0) & (dist < WINDOW)
    return np.where(allowed, bucket, -1).astype(np.int32)


def bias_table(rel_bias, bucket_p, bucket_c, dep=None):
    nb, nq = rel_bias.shape

    def body(rb_ref, bkp_ref, bkc_ref, op_ref, oc_ref):
        for bk_ref, o_ref in ((bkp_ref, op_ref), (bkc_ref, oc_ref)):
            bk = bk_ref[...]
            for h in range(nq):
                acc = jnp.full(bk.shape, -jnp.inf, F32)
                for b in range(nb):
                    acc = jnp.where(bk == b, rb_ref[b, h], acc)
                o_ref[h] = acc

    return _pallas(
        body, [rel_bias, bucket_p, bucket_c], dep=dep,
        in_specs=[_SMEM, _VMEM, _VMEM], out_specs=[_VMEM, _VMEM],
        out_shape=[jax.ShapeDtypeStruct((nq,) + bucket_p.shape, F32)] * 2,
        name="bias_table")


def bias_table_bwd(dbp, dbc, bucket_p, bucket_c, dep=None):
    nq = dbp.shape[0]

    def body(dbp_ref, dbc_ref, bkp_ref, bkc_ref, o_ref):
        bkp, bkc = bkp_ref[...][None], bkc_ref[...][None]
        dp, dc = dbp_ref[...], dbc_ref[...]
        for b in range(NUM_BUCKETS):
            sel = jnp.where(bkp == b, dp, 0.0) + jnp.where(bkc == b, dc, 0.0)
            o_ref[b] = jnp.sum(jnp.sum(sel, axis=2, keepdims=True), axis=1, keepdims=True)

    return _pallas(
        body, [dbp, dbc, bucket_p, bucket_c], dep=dep,
        in_specs=[_VMEM] * 4, out_specs=_VMEM,
        out_shape=jax.ShapeDtypeStruct((NUM_BUCKETS, nq, 1, 1), F32),
        name="bias_table_bwd")


_NT = (((1,), (1,)), ((), ()))
_TN = (((0,), (0,)), ((), ()))


@jax.custom_vjp
def _bdot_nt(a, b):
    return lax.dot_general(a.astype(BF16), b.astype(BF16), _NT, preferred_element_type=F32)


def _bdot_nt_fwd(a, b):
    return _bdot_nt(a, b), (a, b)


def _bdot_nt_bwd(res, g):
    a, b = res
    gb = g.astype(BF16)
    da = jnp.dot(gb, b.astype(BF16), preferred_element_type=F32)
    db = lax.dot_general(gb, a.astype(BF16), _TN, preferred_element_type=F32)
    return da, db


_bdot_nt.defvjp(_bdot_nt_fwd, _bdot_nt_bwd)


@jax.custom_vjp
def _bdot_nn(a, b):
    return jnp.dot(a.astype(BF16), b.astype(BF16), preferred_element_type=F32)


def _bdot_nn_fwd(a, b):
    return _bdot_nn(a, b), (a, b)


def _bdot_nn_bwd(res, g):
    a, b = res
    gb = g.astype(BF16)
    da = lax.dot_general(gb, b.astype(BF16), _NT, preferred_element_type=F32)
    db = lax.dot_general(a.astype(BF16), gb, _TN, preferred_element_type=F32)
    return da, db


_bdot_nn.defvjp(_bdot_nn_fwd, _bdot_nn_bwd)


def _attn_math(q4, kp, kc, vp, vc, bp, bc, sink4, qg, kg, *, prev_ok, scale):
    g, b, hd = q4.shape
    q = q4.reshape(g * b, hd)
    qn = q * _rms(q) * qg
    kpn = kp * _rms(kp) * kg
    kcn = kc * _rms(kc) * kg
    lp = _bdot_nt(qn, kpn).reshape(g, b, b) * scale + bp
    lc = _bdot_nt(qn, kcn).reshape(g, b, b) * scale + bc
    lp = jnp.where(prev_ok, lp, -jnp.inf)
    m = jnp.maximum(jnp.maximum(jnp.max(lp, axis=-1, keepdims=True), jnp.max(lc, axis=-1, keepdims=True)), sink4)
    m = lax.stop_gradient(m)
    pp = jnp.exp(lp - m)
    pc = jnp.exp(lc - m)
    den = jnp.sum(pp, axis=-1, keepdims=True) + jnp.sum(pc, axis=-1, keepdims=True) + jnp.exp(sink4 - m)
    inv = 1.0 / den
    out = _bdot_nn((pp * inv).reshape(g * b, b), vp) + _bdot_nn((pc * inv).reshape(g * b, b), vc)
    return out.reshape(g, b, hd)


def _attn_specs(p, aw, kvw, nq, hd, nblk, reverse):
    assert aw % (2 * kvw) == 0
    kv_col = aw // (2 * kvw)

    def blk(n):
        return nblk - 1 - n if reverse else n

    return [
        pl.BlockSpec((BLOCK, aw), lambda n: (blk(n), 0)),
        pl.BlockSpec((BLOCK, 2 * kvw), lambda n: (jnp.maximum(blk(n) - 1, 0), kv_col)),
        pl.BlockSpec((BLOCK, 2 * kvw), lambda n: (blk(n), kv_col)),
        _full_spec((nq, BLOCK, BLOCK)), _full_spec((nq, BLOCK, BLOCK)), _full_spec((nq, 1, 1)),
        _full_spec((1, hd)), _full_spec((1, hd)),
    ]


def _attn_head_inputs(h, grp, hd, kvw, q_ref, kvp_ref, kvc_ref, bp_ref, bc_ref, s_ref):
    heads = pl.ds(grp * h, grp)
    q4 = jnp.stack([q_ref[:, pl.ds((grp * h + g) * hd, hd)] for g in range(grp)])
    k_cols, v_cols = pl.ds(h * hd, hd), pl.ds(kvw + h * hd, hd)
    return (q4, kvp_ref[:, k_cols], kvc_ref[:, k_cols], kvp_ref[:, v_cols], kvc_ref[:, v_cols],
            bp_ref[heads], bc_ref[heads], s_ref[heads])


def attn_fwd(p, bias_p, bias_c, sinks, qg, kg, *, aw, kvw, dep=None):
    t, hd = p.shape[0], qg.shape[-1]
    nq, nkv, nblk = aw // hd, kvw // hd, t // BLOCK
    grp = nq // nkv
    scale = hd ** -0.5

    def body(q_ref, kvp_ref, kvc_ref, bp_ref, bc_ref, s_ref, qg_ref, kg_ref, o_ref):
        prev_ok = pl.program_id(0) > 0
        for h in range(nkv):
            args = _attn_head_inputs(h, grp, hd, kvw, q_ref, kvp_ref, kvc_ref, bp_ref, bc_ref, s_ref)
            out = _attn_math(*args, qg_ref[...], kg_ref[...], prev_ok=prev_ok, scale=scale)
            for g in range(grp):
                o_ref[:, pl.ds((grp * h + g) * hd, hd)] = out[g].astype(BF16)

    return _pallas(
        body, [p, p, p, bias_p, bias_c, sinks, qg, kg], dep=dep, grid=(nblk,),
        in_specs=_attn_specs(p, aw, kvw, nq, hd, nblk, False),
        out_specs=pl.BlockSpec((BLOCK, aw), lambda n: (n, 0)),
        out_shape=jax.ShapeDtypeStruct((t, aw), BF16),
        sem=("parallel",), name="attn_fwd")


def attn_bwd(p, bias_p, bias_c, sinks, qg, kg, do, *, aw, kvw, dep=None):
    t, hd = p.shape[0], qg.shape[-1]
    nq, nkv, nblk = aw // hd, kvw // hd, t // BLOCK
    grp = nq // nkv
    scale = hd ** -0.5

    def body(q_ref, kvp_ref, kvc_ref, bp_ref, bc_ref, s_ref, qg_ref, kg_ref, do_ref,
             dqkv_ref, dbp_ref, dbc_ref, ds_ref, dqg_ref, dkg_ref, carry):
        i = pl.program_id(0)
        prev_ok = (nblk - 1 - i) > 0

        @pl.when(i == 0)
        def _():
            carry[...] = jnp.zeros_like(carry)
            dbp_ref[...] = jnp.zeros_like(dbp_ref)
            dbc_ref[...] = jnp.zeros_like(dbc_ref)
            ds_ref[...] = jnp.zeros_like(ds_ref)
            dqg_ref[...] = jnp.zeros_like(dqg_ref)
            dkg_ref[...] = jnp.zeros_like(dkg_ref)

        fn = functools.partial(_attn_math, prev_ok=prev_ok, scale=scale)
        for h in range(nkv):
            args = _attn_head_inputs(h, grp, hd, kvw, q_ref, kvp_ref, kvc_ref, bp_ref, bc_ref, s_ref)
            _, vjp = jax.vjp(fn, *args, qg_ref[...], kg_ref[...])
            do4 = jnp.stack([do_ref[:, pl.ds((grp * h + g) * hd, hd)].astype(F32) for g in range(grp)])
            dq, dkp, dkc, dvp, dvc, dbp, dbc, dsk, dqg, dkg = vjp(do4)
            for g in range(grp):
                dqkv_ref[:, pl.ds((grp * h + g) * hd, hd)] = dq[g].astype(BF16)
            k_cols, v_cols = pl.ds(h * hd, hd), pl.ds(kvw + h * hd, hd)
            dqkv_ref[:, pl.ds(aw + h * hd, hd)] = (dkc + carry[:, k_cols]).astype(BF16)
            dqkv_ref[:, pl.ds(aw + kvw + h * hd, hd)] = (dvc + carry[:, v_cols]).astype(BF16)
            carry[:, k_cols] = dkp
            carry[:, v_cols] = dvp
            heads = pl.ds(grp * h, grp)
            dbp_ref[heads] += dbp
            dbc_ref[heads] += dbc
            ds_ref[heads] += dsk
            dqg_ref[...] += dqg
            dkg_ref[...] += dkg

    return _pallas(
        body, [p, p, p, bias_p, bias_c, sinks, qg, kg, do], dep=dep, grid=(nblk,),
        in_specs=_attn_specs(p, aw, kvw, nq, hd, nblk, True)
        + [pl.BlockSpec((BLOCK, aw), lambda n: (nblk - 1 - n, 0))],
        out_specs=[
            pl.BlockSpec((BLOCK, aw + 2 * kvw), lambda n: (nblk - 1 - n, 0)),
            _full_spec((nq, BLOCK, BLOCK)), _full_spec((nq, BLOCK, BLOCK)), _full_spec((nq, 1, 1)),
            _full_spec((1, hd)), _full_spec((1, hd)),
        ],
        out_shape=[
            jax.ShapeDtypeStruct((t, aw + 2 * kvw), BF16),
            jax.ShapeDtypeStruct((nq, BLOCK, BLOCK), F32),
            jax.ShapeDtypeStruct((nq, BLOCK, BLOCK), F32),
            jax.ShapeDtypeStruct((nq, 1, 1), F32),
            jax.ShapeDtypeStruct((1, hd), F32),
            jax.ShapeDtypeStruct((1, hd), F32),
        ],
        scratch=[pltpu.VMEM((BLOCK, 2 * kvw), F32)],
        sem=("arbitrary",), name="attn_bwd")


CONV_TILE = 256


def _conv_halo_specs(tb, ch, nblk):
    per = tb // CONV_HALO
    last = nblk * per - 1
    cur = pl.BlockSpec((tb, ch), lambda n: (n, 0))
    prev = pl.BlockSpec((CONV_HALO, ch), lambda n: (jnp.maximum(n * per - 1, 0), 0))
    nxt = pl.BlockSpec((CONV_HALO, ch), lambda n: (jnp.minimum((n + 1) * per, last), 0))
    return cur, prev, nxt


def _ln_silu(co, ln_g, ln_b):
    mu = jnp.mean(co, axis=-1, keepdims=True)
    cen = co - mu
    rstd = lax.rsqrt(jnp.mean(cen * cen, axis=-1, keepdims=True) + EPS)
    xhat = cen * rstd
    z = xhat * ln_g + ln_b
    return xhat, rstd, z


def _shifted_copies(src, shifted):
    rows = src.shape[0] - SUBLANES
    for r in range(1, SUBLANES):
        shifted[r, pl.ds(0, rows), :] = src[pl.ds(r, rows), :]


def _rows_from(src, shifted, start, n):
    r = start % SUBLANES
    if r == 0:
        return src[pl.ds(start, n), :]
    return shifted[r, pl.ds(start - r, n), :]


def conv_fwd(ca, cb, conv_w, conv_b, ln_g, ln_b, dep=None):
    t, ch = ca.shape
    tb = _tile(t, CONV_TILE, CONV_HALO)
    nblk = t // tb
    cur, prev, _ = _conv_halo_specs(tb, ch, nblk)
    lead = CONV_HALO - (CONV_WIDTH - 1)

    def body(ca_ref, cb_ref, cap_ref, cbp_ref, w_ref, b_ref, g_ref, bb_ref, s_ref, co_ref, ubuf, ushift):
        n = pl.program_id(0)
        halo = cap_ref[...] * _sigmoid(cbp_ref[...])
        ubuf[pl.ds(0, CONV_HALO), :] = jnp.where(n > 0, halo, 0.0)
        ubuf[pl.ds(CONV_HALO, tb), :] = ca_ref[...] * _sigmoid(cb_ref[...])
        _shifted_copies(ubuf, ushift)
        acc = jnp.broadcast_to(b_ref[...], (tb, ch))
        for k in range(CONV_WIDTH):
            acc = acc + w_ref[pl.ds(k, 1), :] * _rows_from(ubuf, ushift, lead + k, tb)
        co_ref[...] = acc
        _, _, z = _ln_silu(acc, g_ref[...], bb_ref[...])
        s_ref[...] = (z * _sigmoid(z)).astype(BF16)

    vec = _full_spec((1, ch))
    return _pallas(
        body, [ca, cb, ca, cb, conv_w, conv_b, ln_g, ln_b], dep=dep, grid=(nblk,),
        in_specs=[cur, cur, prev, prev, _full_spec(conv_w.shape), vec, vec, vec],
        out_specs=[cur, cur],
        out_shape=[jax.ShapeDtypeStruct((t, ch), BF16), jax.ShapeDtypeStruct((t, ch), F32)],
        scratch=[pltpu.VMEM((CONV_HALO + tb, ch), F32), pltpu.VMEM((SUBLANES, CONV_HALO + tb, ch), F32)],
        sem=("parallel",), name="conv_fwd")


def conv_bwd(ca, cb, co, ds, conv_w, ln_g, ln_b, dep=None):
    t, ch = ca.shape
    tb = _tile(t, CONV_TILE, CONV_HALO)
    nblk = t // tb
    cur, prev, nxt = _conv_halo_specs(tb, ch, nblk)
    lead = CONV_HALO - (CONV_WIDTH - 1)
    ext = tb + CONV_HALO

    def body(ca_ref, cb_ref, cap_ref, cbp_ref, co_ref, con_ref, ds_ref, dsn_ref, w_ref, g_ref, bb_ref,
             dca_ref, dcb_ref, dw_ref, dvec_ref, ubuf, dbuf, ushift, dshift):
        n = pl.program_id(0)
        is_last = n == nblk - 1
        sig_b = _sigmoid(cb_ref[...])
        cav = ca_ref[...]
        ubuf[pl.ds(0, CONV_HALO), :] = jnp.where(n > 0, cap_ref[...] * _sigmoid(cbp_ref[...]), 0.0)
        ubuf[pl.ds(CONV_HALO, tb), :] = cav * sig_b
        _shifted_copies(ubuf, ushift)
        co = jnp.concatenate([co_ref[...], con_ref[...]], axis=0)
        xhat, rstd, z = _ln_silu(co, g_ref[...], bb_ref[...])
        dsv = jnp.concatenate([ds_ref[...].astype(F32), jnp.where(is_last, 0.0, dsn_ref[...].astype(F32))], axis=0)
        sg = _sigmoid(z)
        dz = dsv * (sg * (1.0 + z * (1.0 - sg)))
        dxh = dz * g_ref[...]
        dco = rstd * (dxh - jnp.mean(dxh, axis=-1, keepdims=True)
                      - xhat * jnp.mean(dxh * xhat, axis=-1, keepdims=True))
        dbuf[...] = dco
        _shifted_copies(dbuf, dshift)

        @pl.when(n == 0)
        def _():
            dw_ref[...] = jnp.zeros_like(dw_ref)
            dvec_ref[...] = jnp.zeros_like(dvec_ref)

        dco_cur = dco[:tb]
        dvec_ref[pl.ds(0, 1), :] += jnp.sum(dco_cur, axis=0, keepdims=True)
        dvec_ref[pl.ds(1, 1), :] += jnp.sum(dz[:tb] * xhat[:tb], axis=0, keepdims=True)
        dvec_ref[pl.ds(2, 1), :] += jnp.sum(dz[:tb], axis=0, keepdims=True)
        du = jnp.zeros((tb, ch), F32)
        for k in range(CONV_WIDTH):
            du = du + w_ref[pl.ds(k, 1), :] * _rows_from(dbuf, dshift, CONV_WIDTH - 1 - k, tb)
            dw_ref[pl.ds(k, 1), :] += jnp.sum(dco_cur * _rows_from(ubuf, ushift, lead + k, tb), axis=0,
                                              keepdims=True)
        dca_ref[...] = (du * sig_b).astype(BF16)
        dcb_ref[...] = (du * cav * sig_b * (1.0 - sig_b)).astype(BF16)

    vec = _full_spec((1, ch))
    return _pallas(
        body, [ca, cb, ca, cb, co, co, ds, ds, conv_w, ln_g, ln_b], dep=dep, grid=(nblk,),
        in_specs=[cur, cur, prev, prev, cur, nxt, cur, nxt, _full_spec(conv_w.shape), vec, vec],
        out_specs=[cur, cur, _full_spec(conv_w.shape), _full_spec((SUBLANES, ch))],
        out_shape=[jax.ShapeDtypeStruct((t, ch), BF16), jax.ShapeDtypeStruct((t, ch), BF16),
                   jax.ShapeDtypeStruct(conv_w.shape, F32), jax.ShapeDtypeStruct((SUBLANES, ch), F32)],
        scratch=[pltpu.VMEM((CONV_HALO + tb, ch), F32), pltpu.VMEM((ext, ch), F32),
                 pltpu.VMEM((SUBLANES, CONV_HALO + tb, ch), F32), pltpu.VMEM((SUBLANES, ext, ch), F32)],
        sem=("arbitrary",), name="conv_bwd")


def ada_fwd(c_t, w_ada, dep=None):
    d, nc = w_ada.shape
    nex = c_t.shape[1]
    tn = _tile(nc, 512)

    def body(ct_ref, w_ref, o_ref):
        w = w_ref[...]
        ct = ct_ref[...]
        cact = ct * _sigmoid(ct)
        rows = [jnp.sum(w * cact[:, b:b + 1], axis=0, keepdims=True) for b in range(nex)]
        o_ref[...] = jnp.concatenate(rows, axis=0)

    return _pallas(
        body, [c_t, w_ada], dep=dep, grid=(nc // tn,),
        in_specs=[_full_spec(c_t.shape), pl.BlockSpec((d, tn), lambda j: (0, j))],
        out_specs=pl.BlockSpec((nex, tn), lambda j: (0, j)),
        out_shape=jax.ShapeDtypeStruct((nex, nc), F32),
        sem=("parallel",), name="ada_fwd")


def _adamw_math(w, g, m, v):
    m = ADAM_B1 * m + (1.0 - ADAM_B1) * g
    v = ADAM_B2 * v + (1.0 - ADAM_B2) * (g * g)
    m_hat = m / (1.0 - ADAM_B1 ** ADAM_STEP)
    v_hat = v / (1.0 - ADAM_B2 ** ADAM_STEP)
    delta = -ADAM_LR * (m_hat / (jnp.sqrt(v_hat) + ADAM_EPS) + ADAM_WD * w)
    return delta, m, v


def adamw(w, g, m, v, name, dep=None):
    r, n = w.shape
    tr, tn = _ew_tiles(r, n, elems=256 * 1024)

    def body(w_ref, g_ref, m_ref, v_ref, d_ref, nm_ref, nv_ref):
        d_ref[...], nm_ref[...], nv_ref[...] = _adamw_math(w_ref[...], g_ref[...], m_ref[...], v_ref[...])

    blk = pl.BlockSpec((tr, tn), lambda i, j: (i, j))
    return _pallas(
        body, [w, g, m, v], dep=dep, grid=(r // tr, n // tn),
        in_specs=[blk] * 4, out_specs=[blk] * 3,
        out_shape=[jax.ShapeDtypeStruct((r, n), F32)] * 3,
        sem=("parallel", "parallel"), name=name)


def ada_grad_adamw(c_t, dmod_cols, w, m, v, dep=None):
    d, nc = w.shape
    nex = c_t.shape[1]
    tr, tn = _ew_tiles(d, nc, elems=256 * 1024)

    def body(ct_ref, dm_ref, w_ref, m_ref, v_ref, g_ref, d_ref, nm_ref, nv_ref):
        ct = ct_ref[...]
        cact = ct * _sigmoid(ct)
        dm = dm_ref[...]
        g = cact[:, 0:1] * dm[0:1, :]
        for b in range(1, nex):
            g = g + cact[:, b:b + 1] * dm[b:b + 1, :]
        g_ref[...] = g
        d_ref[...], nm_ref[...], nv_ref[...] = _adamw_math(w_ref[...], g, m_ref[...], v_ref[...])

    blk = pl.BlockSpec((tr, tn), lambda i, j: (i, j))
    return _pallas(
        body, [c_t, dmod_cols, w, m, v], dep=dep, grid=(d // tr, nc // tn),
        in_specs=[pl.BlockSpec((tr, nex), lambda i, j: (i, 0)), pl.BlockSpec((nex, tn), lambda i, j: (0, j)),
                  blk, blk, blk],
        out_specs=[blk] * 4,
        out_shape=[jax.ShapeDtypeStruct((d, nc), F32)] * 4,
        sem=("parallel", "parallel"), name="ada_grad_adamw")


def _row_pack(parts):
    cols, offs, off = [], [], 0
    for p in parts:
        n = p.shape[1]
        width = -(-n // LANES) * LANES
        cols.append(jnp.pad(p, ((0, 0), (0, width - n))) if width != n else p)
        offs.append(off)
        off += width
    return jnp.concatenate(cols, axis=1), offs


def small_sum_adamw(gathered, offs, ws, ms, vs, extra_widths, dep=None):
    ndev = gathered.shape[0]
    npar = len(ws)

    def body(ga_ref, *refs):
        w_refs, m_refs, v_refs = refs[:npar], refs[npar:2 * npar], refs[2 * npar:3 * npar]
        outs = refs[3 * npar:]
        tot = ga_ref[0]
        for s in range(1, ndev):
            tot = tot + ga_ref[s]
        for i in range(npar):
            n = ws[i].shape[1]
            g = tot[:, offs[i]:offs[i] + n]
            outs[4 * i][...] = g
            outs[4 * i + 1][...], outs[4 * i + 2][...], outs[4 * i + 3][...] = _adamw_math(
                w_refs[i][...], g, m_refs[i][...], v_refs[i][...])
        for e, n in enumerate(extra_widths):
            off = offs[npar + e]
            outs[4 * npar + e][...] = tot[:, off:off + n]

    shapes = [jax.ShapeDtypeStruct(w.shape, F32) for w in ws for _ in range(4)]
    shapes += [jax.ShapeDtypeStruct((1, n), F32) for n in extra_widths]
    return _pallas(
        body, [gathered, *ws, *ms, *vs], dep=dep, in_specs=[_VMEM] * (1 + 3 * npar), out_specs=[_VMEM] * len(shapes),
        out_shape=shapes, name="small_sum_adamw")


def _position():
    return lax.axis_index("x"), lax.axis_index("y"), lax.axis_index("c")


def _other_chips(x, y):
    return [(1 - x, y), (x, 1 - y), (1 - x, 1 - y)]


def allgather_small(block, name, dep=None):
    def body(x_ref, out_ref, send_sems, recv_sems, local_sem):
        x, y, c = _position()
        me, sibling = (x, y, c), (x, y, 1 - c)
        chips = _other_chips(x, y)

        def slot(px, py, pc):
            return out_ref.at[4 * px + 2 * py + pc]

        def copy(k, block_of, to, src=None):
            return pltpu.make_async_remote_copy(
                src_ref=slot(*block_of) if src is None else src, dst_ref=slot(*block_of),
                send_sem=send_sems.at[k], recv_sem=recv_sems.at[k], device_id=to, device_id_type=MESH)

        mine = pltpu.make_async_copy(x_ref, slot(*me), local_sem)
        mine.start()
        first = [copy(0, me, sibling, src=x_ref)]
        first += [copy(1 + j, me, (*chip, c), src=x_ref) for j, chip in enumerate(chips)]
        for cp in first:
            cp.start()
        passed = [copy(4 + j, (*chip, c), sibling) for j, chip in enumerate(chips)]
        for j, chip in enumerate(chips):
            copy(1 + j, (*chip, c), me).wait_recv()
            passed[j].start()
        copy(0, sibling, me).wait_recv()
        for j, chip in enumerate(chips):
            copy(4 + j, (*chip, 1 - c), me).wait_recv()
        for cp in first + passed:
            cp.wait_send()
        mine.wait()

    return _pallas(
        body, [block], dep=dep,
        out_shape=jax.ShapeDtypeStruct((N_DEV, *block.shape), block.dtype),
        in_specs=[_VMEM], out_specs=_VMEM,
        scratch=[pltpu.SemaphoreType.DMA((7,)), pltpu.SemaphoreType.DMA((7,)), pltpu.SemaphoreType.DMA],
        name=name)


class Started(NamedTuple):
    send_sems: Any
    recv_sems: Any
    bufs: list


def exchange_start(name, bufs, n_copies, plan, dep=None):
    nb = len(bufs)

    def body(*refs):
        for cp in plan(refs[:nb], refs[nb], refs[nb + 1]):
            cp.start()

    outs = _pallas(
        body, [pltpu.with_memory_space_constraint(b, pltpu.HBM) for b in bufs], dep=dep, name=name,
        out_shape=(pltpu.SemaphoreType.DMA((n_copies,)), pltpu.SemaphoreType.DMA((n_copies,)),
                   *[pltpu.HBM(b.shape, b.dtype) for b in bufs]),
        in_specs=[_HBM] * nb,
        out_specs=(_SEM, _SEM, *[_HBM] * nb),
        input_output_aliases={i: 2 + i for i in range(nb)},
        compiler_params=pltpu.CompilerParams(has_side_effects=_EFFECT))
    return Started(outs[0], outs[1], list(outs[2:2 + nb]))


def exchange_wait(name, started, plan, dep=None):
    nb = len(started.bufs)

    def body(*refs):
        for cp in plan(refs[:nb], refs[nb], refs[nb + 1]):
            cp.wait_send()
            cp.wait_recv()

    outs = _pallas(
        body, [*started.bufs, started.send_sems, started.recv_sems], dep=dep, name=name,
        out_shape=tuple(pltpu.HBM(b.shape, b.dtype) for b in started.bufs),
        in_specs=[_HBM] * nb + [_SEM, _SEM],
        out_specs=tuple([_HBM] * nb),
        input_output_aliases={i: i for i in range(nb)},
        compiler_params=pltpu.CompilerParams(has_side_effects=_EFFECT))
    return list(outs)


def _remote(src, dst, send_sems, recv_sems, i, to):
    return pltpu.make_async_remote_copy(src_ref=src, dst_ref=dst, send_sem=send_sems.at[i], recv_sem=recv_sems.at[i],
                                        device_id=to, device_id_type=MESH)


def _half_rows(buf_rows, chip_idx, pc):
    half = buf_rows // (2 * N_CHIPS)
    return pl.ds((2 * chip_idx + pc) * half, half)


def plan_gather_ici(refs, send_sems, recv_sems):
    x, y, c = _position()
    copies = []
    for k, ref in enumerate(refs):
        rows = ref.at[_half_rows(ref.shape[0], 2 * x + y, c), :]
        for j, chip in enumerate(_other_chips(x, y)):
            copies.append(_remote(rows, rows, send_sems, recv_sems, 3 * k + j, (*chip, c)))
    return copies


def plan_gather_d2d(refs, send_sems, recv_sems):
    x, y, c = _position()
    copies = []
    for k, ref in enumerate(refs):
        for j, (px, py) in enumerate(_other_chips(x, y)):
            rows = ref.at[_half_rows(ref.shape[0], 2 * px + py, c), :]
            copies.append(_remote(rows, rows, send_sems, recv_sems, 3 * k + j, (x, y, 1 - c)))
    return copies


def plan_pair_exchange(refs, send_sems, recv_sems):
    x, y, c = _position()
    nw = len(refs) // 2
    copies = []
    for k in range(nw):
        for chip in range(N_CHIPS):
            copies.append(_remote(refs[k].at[chip, 1 - c], refs[nw + k].at[chip], send_sems, recv_sems,
                                  N_CHIPS * k + chip, (x, y, 1 - c)))
    return copies


def plan_chip_exchange(refs, send_sems, recv_sems):
    x, y, c = _position()
    nw = len(refs) // 2
    copies = []
    for k in range(nw):
        for j, (px, py) in enumerate(_other_chips(x, y)):
            copies.append(_remote(refs[k].at[2 * px + py], refs[nw + k].at[2 * x + y], send_sems, recv_sems,
                                  3 * k + j, (px, py, c)))
    return copies


def plan_pair_share(refs, send_sems, recv_sems):
    x, y, c = _position()
    return [_remote(ref.at[c], ref.at[c], send_sems, recv_sems, k, (x, y, 1 - c)) for k, ref in enumerate(refs)]


def cast_into_slot(src, slot, n_slots, name, dep=None):
    r, n = src.shape
    tr, tn = _ew_tiles(r, n, BF16_SUBLANES)

    def body(slot_ref, s_ref, o_ref):
        o_ref[...] = s_ref[...].astype(BF16)

    return _pallas(
        body, [slot, src], dep=dep, n_prefetch=1, grid=(r // tr, n // tn),
        in_specs=[pl.BlockSpec((tr, tn), lambda i, j, sl: (i, j))],
        out_specs=pl.BlockSpec((None, tr, tn), lambda i, j, sl: (sl[0], i, j)),
        out_shape=jax.ShapeDtypeStruct((n_slots, r, n), BF16),
        sem=("parallel", "parallel"), name=name)


def pair_sum(g, r, core, name, dep=None):
    nchip, _, h, n = g.shape
    th, tn = _ew_tiles(h, n, BF16_SUBLANES)

    def body(core_ref, g_ref, r_ref, o_ref):
        o_ref[...] = (g_ref[...].astype(F32) + r_ref[...].astype(F32)).astype(BF16)

    return _pallas(
        body, [core, g, r], dep=dep, n_prefetch=1, grid=(nchip, h // th, n // tn),
        in_specs=[pl.BlockSpec((None, None, th, tn), lambda a, i, j, cr: (a, cr[0], i, j)),
                  pl.BlockSpec((None, th, tn), lambda a, i, j, cr: (a, i, j))],
        out_specs=pl.BlockSpec((None, th, tn), lambda a, i, j, cr: (a, i, j)),
        out_shape=jax.ShapeDtypeStruct((nchip, h, n), BF16),
        sem=("parallel", "parallel", "parallel"), name=name)


def chip_sum(own, got, where, name, dep=None):
    nchip, h, n = got.shape
    th, tn = _ew_tiles(h, n, BF16_SUBLANES, elems=256 * 1024)

    def body(where_ref, own_ref, *rest):
        got_refs, o_ref = rest[:nchip], rest[nchip]
        chip = where_ref[0]
        acc = None
        for s in range(nchip):
            term = jnp.where(chip == s, own_ref[...], got_refs[s][...]).astype(F32)
            acc = term if acc is None else acc + term
        o_ref[...] = acc

    def got_spec(s):
        return pl.BlockSpec((None, th, tn), lambda i, j, wr: (jnp.where(wr[0] == s, (s + 1) % nchip, s), i, j))

    return _pallas(
        body, [where, own, *[got] * nchip], dep=dep, n_prefetch=1, grid=(h // th, n // tn),
        in_specs=[pl.BlockSpec((None, th, tn), lambda i, j, wr: (wr[0], i, j))]
        + [got_spec(s) for s in range(nchip)],
        out_specs=pl.BlockSpec((None, th, tn), lambda i, j, wr: (wr[1], i, j)),
        out_shape=jax.ShapeDtypeStruct((2, h, n), F32),
        sem=("parallel", "parallel"), name=name)


def kernel(x, c, w_ada, b_ada, norm_mix_g, w_in, q_norm_g, k_norm_g, attn_sinks, rel_bias, w_attn_out, conv_w, conv_b, conv_ln_g, conv_ln_b, w_conv_out, w_mix_out, norm_ffn_g, w_ffn_in, w_ffn_out, loss_target, m_w_ada, m_b_ada, m_norm_mix_g, m_w_in, m_q_norm_g, m_k_norm_g, m_attn_sinks, m_rel_bias, m_w_attn_out, m_conv_w, m_conv_b, m_conv_ln_g, m_conv_ln_b, m_w_conv_out, m_w_mix_out, m_norm_ffn_g, m_w_ffn_in, m_w_ffn_out, v_w_ada, v_b_ada, v_norm_mix_g, v_w_in, v_q_norm_g, v_k_norm_g, v_attn_sinks, v_rel_bias, v_w_attn_out, v_conv_w, v_conv_b, v_conv_ln_g, v_conv_ln_b, v_w_conv_out, v_w_mix_out, v_norm_ffn_g, v_w_ffn_in, v_w_ffn_out):
    run = InOrder()
    xi, yi, ci = _position()
    chip = 2 * xi + yi
    me = 2 * chip + ci
    chip_arr = chip.astype(jnp.int32).reshape(1)
    core_arr = ci.astype(jnp.int32).reshape(1)
    where_arr = jnp.stack([chip, ci]).astype(jnp.int32)

    xe, tgt = x[0], loss_target[0]
    t, d = xe.shape
    hd = q_norm_g.shape[-1]
    nq = attn_sinks.shape[-1]
    aw = nq * hd
    ch = conv_b.shape[-1]
    in_width = N_CHIPS * w_in.shape[-1]
    kvw = (in_width - aw - 2 * ch - 2 * d) // 2
    nkv = kvw // hd
    dff = N_CHIPS * w_ffn_out.shape[1]
    off_k, off_v, off_ca = aw, aw + kvw, aw + 2 * kvw
    off_cb, off_ga, off_gc = off_ca + ch, off_ca + 2 * ch, off_ca + 2 * ch + d
    nc_ada = w_ada.shape[-1]
    ch_loc = conv_w.shape[-1]
    nj_ffn = w_ffn_in.shape[-1]
    perm_ffn = ffn_perm(N_CHIPS)

    big = {"w_in": w_in[0], "w_attn_out": w_attn_out[0], "w_conv_out": w_conv_out[0], "w_mix_out": w_mix_out[0],
           "w_ffn_in": w_ffn_in[0], "w_ffn_out": w_ffn_out[0]}
    moments = {"w_in": (m_w_in, v_w_in), "w_attn_out": (m_w_attn_out, v_w_attn_out),
               "w_conv_out": (m_w_conv_out, v_w_conv_out), "w_mix_out": (m_w_mix_out, v_w_mix_out),
               "w_ffn_in": (m_w_ffn_in, v_w_ffn_in), "w_ffn_out": (m_w_ffn_out, v_w_ffn_out)}
    gather_groups = {"in": ["w_in"], "mid": ["w_attn_out", "w_conv_out", "w_mix_out"], "ffn_in": ["w_ffn_in"],
                     "ffn_out": ["w_ffn_out"]}
    grads, deltas, new_m, new_v = {}, {}, {}, {}

    def gather_cast(gname):
        bufs = []
        for n in gather_groups[gname]:
            r, ncol = big[n].shape
            bufs.append(run(cast_into_slot, big[n], chip_arr, N_CHIPS, "cast_" + n).reshape(N_CHIPS * r, ncol))
        return bufs

    def gather_ici_start(gname, bufs):
        return run(exchange_start, "gather_ici_start_" + gname, bufs, 3 * len(bufs), plan_gather_ici)

    def gather_pass_on(gname, ici):
        landed = run(exchange_wait, "gather_ici_wait_" + gname, ici, plan_gather_ici)
        return run(exchange_start, "gather_d2d_start_" + gname, landed, 3 * len(landed), plan_gather_d2d)

    def gathered(gname, d2d):
        outs = run(exchange_wait, "gather_d2d_wait_" + gname, d2d, plan_gather_d2d)
        return [o.reshape(N_CHIPS, *big[n].shape) for o, n in zip(outs, gather_groups[gname])]

    def rs_pair_start(gname, names, partials):
        blocks = [g.reshape(N_CHIPS, 2, big[n].shape[0] // 2, big[n].shape[1]) for n, g in zip(names, partials)]
        land = [lax.empty((N_CHIPS,) + b.shape[2:], BF16) for b in blocks]
        return run(exchange_start, "pair_exchange_start_" + gname, blocks + land, N_CHIPS * len(blocks),
                   plan_pair_exchange)

    def rs_chip_start(gname, names, pair):
        nw = len(names)
        outs = run(exchange_wait, "pair_exchange_wait_" + gname, pair, plan_pair_exchange)
        sums = [run(pair_sum, g, r, core_arr, "pair_sum_" + n) for n, g, r in zip(names, outs[:nw], outs[nw:])]
        land = [lax.empty(s.shape, BF16) for s in sums]
        return run(exchange_start, "chip_exchange_start_" + gname, sums + land, 3 * nw, plan_chip_exchange)

    def rs_share_start(gname, names, chipx):
        nw = len(names)
        outs = run(exchange_wait, "chip_exchange_wait_" + gname, chipx, plan_chip_exchange)
        halves = [run(chip_sum, s, r, where_arr, "chip_sum_" + n) for n, s, r in zip(names, outs[:nw], outs[nw:])]
        return run(exchange_start, "pair_share_start_" + gname, halves, nw, plan_pair_share)

    def rs_finish(gname, names, share):
        fulls = run(exchange_wait, "pair_share_wait_" + gname, share, plan_pair_share)
        for n, g2 in zip(names, fulls):
            g = g2.reshape(big[n].shape)
            dl, nm, nv = run(adamw, big[n], g, moments[n][0][0], moments[n][1][0], "adamw_" + n)
            grads[n], deltas[n], new_m[n], new_v[n] = g[None], dl[None], nm[None], nv[None]

    bufs_in = gather_cast("in")
    row1, offs1 = _row_pack([c, conv_w[0].reshape(1, CONV_WIDTH * ch_loc)])
    got1 = run(allgather_small, row1, "allgather_cond")
    c_all = got1[:, 0, :d]
    conv_w_full = got1[0::2, 0, offs1[1]:offs1[1] + CONV_WIDTH * ch_loc].reshape(N_CHIPS, CONV_WIDTH, ch_loc)
    conv_w_full = jnp.transpose(conv_w_full, (1, 0, 2)).reshape(CONV_WIDTH, ch)
    conv_w_pad = jnp.pad(conv_w_full, ((0, 1), (0, 0)))
    c_t = jnp.transpose(c_all)
    mod_cols = run(ada_fwd, c_t, w_ada[0])
    got2 = run(allgather_small, mod_cols, "allgather_mod")
    mod_all = got2.reshape(N_CHIPS, 2, N_DEV, nc_ada)[:, 0]
    mod = lax.dynamic_slice_in_dim(mod_all, me, 1, axis=1).reshape(1, N_CHIPS * nc_ada) + b_ada
    mod = jnp.pad(mod.reshape(N_MOD, d), ((0, SUBLANES - N_MOD), (0, 0)))

    ici = {"in": gather_ici_start("in", bufs_in)}
    for gname in ("mid", "ffn_in", "ffn_out"):
        ici[gname] = gather_ici_start(gname, gather_cast(gname))

    h = run(pre_mix_fwd, xe, mod, norm_mix_g)
    bucket = _t5_bucket_table()
    bucket_p, bucket_c = jnp.asarray(bucket[:, :BLOCK]), jnp.asarray(bucket[:, BLOCK:])
    bias_p, bias_c = run(bias_table, rel_bias, bucket_p, bucket_c)
    d2d_in = gather_pass_on("in", ici["in"])
    (wg_in,) = gathered("in", d2d_in)
    p = run(mm_nn, h, wg_in, tn=_tile(wg_in.shape[2], 640), tk=d, out_dtype=F32, name="mm_in")
    d2d_mid = gather_pass_on("mid", ici["mid"])

    sinks3 = attn_sinks.reshape(nq, 1, 1)
    attn_o = run(attn_fwd, p, bias_p, bias_c, sinks3, q_norm_g, k_norm_g, aw=aw, kvw=kvw)
    ca, cb = p[:, off_ca:off_cb], p[:, off_cb:off_ga]
    s_conv, co_conv = run(conv_fwd, ca, cb, conv_w_pad, conv_b, conv_ln_g, conv_ln_b)
    wg_attn_out, wg_conv_out, wg_mix_out = gathered("mid", d2d_mid)
    wg_mix_out = wg_mix_out.reshape(1, d, d)
    y_attn = run(mm_nn, attn_o, wg_attn_out, tn=_tile(wg_attn_out.shape[2], 512), tk=aw, out_dtype=BF16,
                 name="mm_attn_out")
    y_conv = run(mm_nn, s_conv, wg_conv_out, tn=_tile(wg_conv_out.shape[2], 512), tk=ch, out_dtype=BF16,
                 name="mm_conv_out")
    merged = run(merge_fwd, p, y_attn, y_conv, off_ga, off_gc)
    d2d_ffn_in = gather_pass_on("ffn_in", ici["ffn_in"])
    o_m = run(mm_nn, merged, wg_mix_out, tn=_tile(d, 512), tk=d, out_dtype=F32, name="mm_mix_out")
    x1, h2 = run(pre_ffn_fwd, xe, o_m, mod, norm_ffn_g)
    (wg_ffn_in,) = gathered("ffn_in", d2d_ffn_in)
    f = run(mm_nn, h2, wg_ffn_in, tn=_tile(nj_ffn, 1408), tk=d, out_dtype=BF16, name="mm_ffn_in", perm=perm_ffn)
    d2d_ffn_out = gather_pass_on("ffn_out", ici["ffn_out"])
    act = run(swiglu_fwd, f, nj_ffn)
    (wg_ffn_out,) = gathered("ffn_out", d2d_ffn_out)
    wg_ffn_out = wg_ffn_out.reshape(1, dff, d)
    o_f = run(mm_nn, act, wg_ffn_out, tn=_tile(d, 1024), tk=_tile(dff, 1408), out_dtype=F32, name="mm_ffn_out")
    loss11, dy, dof, acc_l = run(loss_head, x1, o_f, tgt, mod)

    gw_ffn_out = run(mm_tn, act, dof, 1, tk=_tile(dff, 512), tn=d, name="mm_ffn_out_dw")
    px_ffn_out = rs_pair_start("ffn_out", ["w_ffn_out"], [gw_ffn_out])
    dact = run(mm_nt, dof, wg_ffn_out, tko=_tile(dff, 512), tn=d, out_dtype=BF16, name="mm_ffn_out_dx")
    cx_ffn_out = rs_chip_start("ffn_out", ["w_ffn_out"], px_ffn_out)
    df = run(swiglu_bwd, f, dact, nj_ffn)
    gw_ffn_in = run(mm_tn, h2, df, N_CHIPS, tk=d, tn=_tile(nj_ffn, 1408), name="mm_ffn_in_dw",
                    perm=perm_ffn)
    px_ffn_in = rs_pair_start("ffn_in", ["w_ffn_in"], [gw_ffn_in])
    dh2 = run(mm_nt, df, wg_ffn_in, tko=_tile(d, 512), tn=nj_ffn, name="mm_ffn_in_dx", perm=perm_ffn)
    sh_ffn_out = rs_share_start("ffn_out", ["w_ffn_out"], cx_ffn_out)
    cx_ffn_in = rs_chip_start("ffn_in", ["w_ffn_in"], px_ffn_in)
    dx1, dom, acc_f = run(pre_ffn_bwd, x1, dh2, dy, o_m, mod, norm_ffn_g)
    gw_mix_out = run(mm_tn, merged, dom, 1, tk=d, tn=_tile(d, 1024), name="mm_mix_out_dw")
    px_mix = rs_pair_start("mix_out", ["w_mix_out"], [gw_mix_out])
    dmerged = run(mm_nt, dom, wg_mix_out, tko=_tile(d, 512), tn=d, out_dtype=BF16, name="mm_mix_out_dx")
    dy_attn, dy_conv, dga, dgc = run(merge_bwd, p, y_attn, y_conv, dmerged, off_ga, off_gc)
    rs_finish("ffn_out", ["w_ffn_out"], sh_ffn_out)
    cx_mix = rs_chip_start("mix_out", ["w_mix_out"], px_mix)
    gw_attn_out = run(mm_tn, attn_o, dy_attn, N_CHIPS, tk=aw, tn=_tile(wg_attn_out.shape[2], 512),
                      name="mm_attn_out_dw")
    gw_conv_out = run(mm_tn, s_conv, dy_conv, N_CHIPS, tk=ch, tn=_tile(wg_conv_out.shape[2], 512),
                      name="mm_conv_out_dw")
    ac_names = ["w_attn_out", "w_conv_out"]
    px_ac = rs_pair_start("attn_conv_out", ac_names, [gw_attn_out, gw_conv_out])
    dattn_o = run(mm_nt, dy_attn, wg_attn_out, tko=_tile(aw, 1024), tn=_tile(wg_attn_out.shape[2], 512),
                  out_dtype=BF16, name="mm_attn_out_dx")
    ds_conv = run(mm_nt, dy_conv, wg_conv_out, tko=_tile(ch, 1024), tn=_tile(wg_conv_out.shape[2], 512),
                  out_dtype=BF16, name="mm_conv_out_dx")
    cx_ac = rs_chip_start("attn_conv_out", ac_names, px_ac)
    dca, dcb, dconv_w, dconv_vec = run(conv_bwd, ca, cb, co_conv, ds_conv, conv_w_pad, conv_ln_g, conv_ln_b)
    sh_ffn_in = rs_share_start("ffn_in", ["w_ffn_in"], cx_ffn_in)
    dqkv, dbp, dbc, dsinks, dqg, dkg = run(attn_bwd, p, bias_p, bias_c, sinks3, q_norm_g, k_norm_g, dattn_o,
                                           aw=aw, kvw=kvw)
    sh_mix = rs_share_start("mix_out", ["w_mix_out"], cx_mix)
    sh_ac = rs_share_start("attn_conv_out", ac_names, cx_ac)
    drel = run(bias_table_bwd, dbp, dbc, bucket_p, bucket_c).reshape(NUM_BUCKETS, nq)
    dp = jnp.concatenate([dqkv, dca, dcb, dga, dgc], axis=1)
    gw_in = run(mm_tn, h, dp, N_CHIPS, tk=d, tn=_tile(wg_in.shape[2], 640), name="mm_in_dw")
    px_in = rs_pair_start("in", ["w_in"], [gw_in])
    dh = run(mm_nt, dp, wg_in, tko=_tile(d, 1024), tn=wg_in.shape[2], name="mm_in_dx")
    grad_x, acc_m = run(pre_mix_bwd, xe, dh, dx1, mod, norm_mix_g)

    dmod = jnp.concatenate([acc_m[0:1], acc_m[1:2], acc_f[3:4], acc_f[0:1], acc_f[1:2], acc_l[0:1]], axis=1)
    small_names = ["b_ada", "norm_mix_g", "q_norm_g", "k_norm_g", "attn_sinks", "rel_bias", "conv_b", "conv_ln_g",
                   "conv_ln_b", "norm_ffn_g"]
    small_w = [b_ada, norm_mix_g, q_norm_g, k_norm_g, attn_sinks, rel_bias, conv_b, conv_ln_g, conv_ln_b, norm_ffn_g]
    small_m = [m_b_ada, m_norm_mix_g, m_q_norm_g, m_k_norm_g, m_attn_sinks, m_rel_bias, m_conv_b, m_conv_ln_g,
               m_conv_ln_b, m_norm_ffn_g]
    small_v = [v_b_ada, v_norm_mix_g, v_q_norm_g, v_k_norm_g, v_attn_sinks, v_rel_bias, v_conv_b, v_conv_ln_g,
               v_conv_ln_b, v_norm_ffn_g]
    small_g = [dmod, acc_m[2:3], dqg, dkg, dsinks.reshape(1, nq), drel.reshape(1, NUM_BUCKETS * nq),
               dconv_vec[0:1], dconv_vec[1:2], dconv_vec[2:3], acc_f[2:3]]
    row3, offs3 = _row_pack(small_g + [dconv_w[:CONV_WIDTH].reshape(1, CONV_WIDTH * ch), loss11])
    got3 = run(allgather_small, row3, "allgather_small_grads")
    cx_in = rs_chip_start("in", ["w_in"], px_in)
    as_row = lambda a: a.reshape(1, -1)
    outs3 = run(small_sum_adamw, got3, offs3, [as_row(a) for a in small_w], [as_row(a) for a in small_m],
                [as_row(a) for a in small_v], [CONV_WIDTH * ch, 1])
    for i, (n, w) in enumerate(zip(small_names, small_w)):
        grads[n], deltas[n], new_m[n], new_v[n] = (o.reshape(w.shape) for o in outs3[4 * i:4 * i + 4])
    g_conv_w_all, loss_sum = outs3[-2].reshape(CONV_WIDTH, ch), outs3[-1]

    g_conv_w = lax.dynamic_slice_in_dim(g_conv_w_all, chip * ch_loc, ch_loc, axis=1)
    grads["conv_w"] = g_conv_w[None]
    dl, nm, nv = run(adamw, conv_w[0], g_conv_w, m_conv_w[0], v_conv_w[0], "adamw_conv_w")
    deltas["conv_w"], new_m["conv_w"], new_v["conv_w"] = dl[None], nm[None], nv[None]

    dmod_all = got3[:, 0, :N_MOD * d]
    dmod_cols = lax.dynamic_slice_in_dim(dmod_all, chip * nc_ada, nc_ada, axis=1)
    g_ada, dl, nm, nv = run(ada_grad_adamw, c_t, dmod_cols, w_ada[0], m_w_ada[0], v_w_ada[0])
    grads["w_ada"], deltas["w_ada"], new_m["w_ada"], new_v["w_ada"] = g_ada[None], dl[None], nm[None], nv[None]

    rs_finish("ffn_in", ["w_ffn_in"], sh_ffn_in)
    rs_finish("mix_out", ["w_mix_out"], sh_mix)
    rs_finish("attn_conv_out", ac_names, sh_ac)
    sh_in = rs_share_start("in", ["w_in"], cx_in)
    rs_finish("in", ["w_in"], sh_in)

    loss = loss_sum[0, 0]
    order = ["w_ada", "b_ada", "norm_mix_g", "w_in", "q_norm_g", "k_norm_g", "attn_sinks", "rel_bias", "w_attn_out",
             "conv_w", "conv_b", "conv_ln_g", "conv_ln_b", "w_conv_out", "w_mix_out", "norm_ffn_g", "w_ffn_in",
             "w_ffn_out"]
    return (loss, grad_x[None], *[grads[n] for n in order], *[deltas[n] for n in order],
            *[new_m[n] for n in order], *[new_v[n] for n in order])
```

```python
import functools
import math
from typing import Any, NamedTuple

import jax
import jax.numpy as jnp
import numpy as np
from jax import lax
from jax.experimental import pallas as pl
from jax.experimental.pallas import tpu as pltpu

F32 = jnp.float32
BF16 = jnp.bfloat16
MESH = pl.DeviceIdType.MESH

V7X_VMEM_BYTES = 64 * 1024 * 1024
VMEM_LIMIT = V7X_VMEM_BYTES - 8 * 1024 * 1024
LANES = 128
SUBLANES = 8
BF16_SUBLANES = 16

EPS = 1e-6
WINDOW = 128
BLOCK = 128
NUM_BUCKETS = 32
MAX_EXACT = NUM_BUCKETS // 2
MAX_DISTANCE = 128
CONV_WIDTH = 31
CONV_HALO = 32
ADAM_LR = 0.001
ADAM_B1 = 0.9
ADAM_B2 = 0.999
ADAM_EPS = 1e-08
ADAM_WD = 0.01
ADAM_STEP = 10
N_MOD = 6
SH_M, SC_M, GT_M, SH_F, SC_F, GT_F = range(6)

N_CHIPS = 4
N_DEV = 8

_ANY = pl.BlockSpec(memory_space=pl.ANY)
_VMEM = pl.BlockSpec(memory_space=pltpu.VMEM)
_SMEM = pl.BlockSpec(memory_space=pltpu.SMEM)
_HBM = pl.BlockSpec(memory_space=pltpu.HBM)
_SEM = pl.BlockSpec(memory_space=pltpu.SEMAPHORE)
_EFFECT = pltpu.SideEffectType.DATAFLOW_SIDE_EFFECTING


class InOrder:
    def __init__(self):
        self.token = None

    def __call__(self, fn, *args, **kw):
        return fn(*args, dep=self, **kw)


def _pallas(body, args, *, in_specs, out_specs, out_shape, name, dep=None, grid=(), n_prefetch=0, scratch=(),
            sem=None, **kw):
    n_lead = n_prefetch + len(in_specs)
    in_specs, args = list(in_specs), list(args)
    single = not isinstance(out_shape, (list, tuple))
    out_shapes = [out_shape] if single else list(out_shape)
    out_specs = [out_specs] if single else list(out_specs)
    if dep is not None:
        inner, n_out, takes = body, len(out_shapes), dep.token is not None

        def body(*refs):
            rest = refs[n_lead + (1 if takes else 0):]
            rest[n_out][...] = jnp.zeros((SUBLANES, LANES), F32)
            return inner(*refs[:n_lead], *rest[:n_out], *rest[n_out + 1:])

        if takes:
            in_specs.append(_ANY)
            args.append(dep.token)
        out_shapes.append(jax.ShapeDtypeStruct((SUBLANES, LANES), F32))
        out_specs.append(pl.BlockSpec((SUBLANES, LANES), lambda *_: (0, 0)))
    params = kw.pop("compiler_params", None)
    if params is None:
        params = pltpu.CompilerParams(dimension_semantics=sem, vmem_limit_bytes=VMEM_LIMIT)
    outs = pl.pallas_call(
        body,
        grid_spec=pltpu.PrefetchScalarGridSpec(num_scalar_prefetch=n_prefetch, grid=grid, in_specs=in_specs,
                                               out_specs=out_specs, scratch_shapes=list(scratch)),
        out_shape=out_shapes, compiler_params=params, name=name, **kw,
    )(*args)
    if dep is not None:
        dep.token = outs[-1]
        outs = outs[:-1]
    return outs[0] if single else list(outs)


def _tile(n, pref, unit=LANES):
    best = None
    for t in range(unit, min(n, pref) + 1, unit):
        if n % t == 0:
            best = t
    return best if best is not None else n


def _sigmoid(v):
    return 1.0 / (1.0 + jnp.exp(-v))


ROW_CHUNK = 512


def _row_chunks(m, unit=SUBLANES):
    step = _tile(m, ROW_CHUNK, unit)
    return [(s, step) for s in range(0, m, step)]


def _ew_tiles(r, n, unit=SUBLANES, elems=512 * 1024):
    return _tile(r, max(unit, elems // n), unit), n


def _block_pos(j, perm):
    if perm is None:
        return j
    pos = 0
    for a, p in enumerate(perm):
        pos = pos + jnp.where(j == a, p, 0)
    return pos


def mm_nn(a, w, *, tn, tk, out_dtype, name, perm=None, dep=None):
    m, k = a.shape
    j, k2, nj = w.shape
    assert k == k2 and nj % tn == 0 and k % tk == 0
    npj, nk = nj // tn, k // tk

    def body(a_ref, w_ref, o_ref, *scratch):
        kk = pl.program_id(1)
        for s, sz in _row_chunks(m):
            rows = pl.ds(s, sz)
            p = jnp.dot(a_ref[rows, :], w_ref[...], preferred_element_type=F32)
            if nk == 1:
                o_ref[rows, :] = p.astype(out_dtype)
            else:
                acc = scratch[0]

                @pl.when(kk == 0)
                def _():
                    acc[rows, :] = p

                @pl.when(kk > 0)
                def _():
                    acc[rows, :] += p

                @pl.when(kk == nk - 1)
                def _():
                    o_ref[rows, :] = acc[rows, :].astype(out_dtype)

    return _pallas(
        body, [a, w], dep=dep, grid=(j * npj, nk),
        in_specs=[
            pl.BlockSpec((m, tk), lambda n, kk: (0, kk)),
            pl.BlockSpec((None, tk, tn), lambda n, kk: (n // npj, kk, n % npj)),
        ],
        out_specs=pl.BlockSpec((m, tn), lambda n, kk: (0, _block_pos(n // npj, perm) * npj + n % npj)),
        out_shape=jax.ShapeDtypeStruct((m, j * nj), out_dtype),
        scratch=[pltpu.VMEM((m, tn), F32)] if nk > 1 else [],
        sem=("parallel", "arbitrary"), name=name)


def mm_nn_blocks(a, w, blocks, into, *, tn, out_dtype, name, dep=None):
    m, k = a.shape
    j, k2, nj = w.shape
    assert k == k2 and nj % tn == 0
    npj = nj // tn
    n_in = 2 if into is None else 3

    def body(blocks_ref, a_ref, w_ref, *rest):
        o_ref = rest[n_in - 2]
        for s, sz in _row_chunks(m):
            rows = pl.ds(s, sz)
            o_ref[rows, :] = jnp.dot(a_ref[rows, :], w_ref[...], preferred_element_type=F32).astype(out_dtype)

    return _pallas(
        body, [blocks, a, w] + ([] if into is None else [into]), dep=dep, n_prefetch=1,
        grid=(blocks.shape[0] * npj,),
        in_specs=[pl.BlockSpec((m, k), lambda n, bl: (0, 0)),
                  pl.BlockSpec((None, k, tn), lambda n, bl: (bl[n // npj], 0, n % npj))]
        + ([] if into is None else [_ANY]),
        out_specs=pl.BlockSpec((m, tn), lambda n, bl: (0, bl[n // npj] * npj + n % npj)),
        out_shape=jax.ShapeDtypeStruct((m, j * nj), out_dtype),
        input_output_aliases={} if into is None else {3: 0},
        sem=("arbitrary",), name=name)


def mm_nt(g, w, *, tko, tn, name, out_dtype=F32, perm=None, dep=None):
    m, n = g.shape
    j, k, nj = w.shape
    assert n == j * nj and nj % tn == 0 and k % tko == 0
    npj, nr = nj // tn, n // tn
    in_place = out_dtype == F32

    def body(g_ref, w_ref, o_ref, *scratch):
        r = pl.program_id(1)
        acc = o_ref if in_place else (scratch[0] if nr > 1 else None)
        for s, sz in _row_chunks(m):
            rows = pl.ds(s, sz)
            p = lax.dot_general(g_ref[rows, :], w_ref[...], (((1,), (1,)), ((), ())), preferred_element_type=F32)
            if acc is None:
                o_ref[rows, :] = p.astype(out_dtype)
                continue

            @pl.when(r == 0)
            def _():
                acc[rows, :] = p

            @pl.when(r > 0)
            def _():
                acc[rows, :] += p

            if not in_place:
                @pl.when(r == nr - 1)
                def _():
                    o_ref[rows, :] = acc[rows, :].astype(out_dtype)

    return _pallas(
        body, [g, w], dep=dep, grid=(k // tko, nr),
        in_specs=[
            pl.BlockSpec((m, tn), lambda ko, r: (0, _block_pos(r // npj, perm) * npj + r % npj)),
            pl.BlockSpec((None, tko, tn), lambda ko, r: (r // npj, ko, r % npj)),
        ],
        out_specs=pl.BlockSpec((m, tko), lambda ko, r: (0, ko)),
        out_shape=jax.ShapeDtypeStruct((m, k), out_dtype),
        scratch=[pltpu.VMEM((m, tko), F32)] if (nr > 1 and not in_place) else [],
        sem=("parallel", "arbitrary"), name=name)


def mm_tn(a, g, n_blocks, *, tk, tn, name, perm=None, dep=None):
    m, k = a.shape
    m2, n = g.shape
    nj = n // n_blocks
    assert m == m2 and nj % tn == 0 and k % tk == 0
    npj = nj // tn

    def body(a_ref, g_ref, o_ref):
        for s, sz in _row_chunks(tk, LANES):
            p = lax.dot_general(a_ref[:, pl.ds(s, sz)], g_ref[...], (((0,), (0,)), ((), ())),
                                preferred_element_type=F32)
            o_ref[pl.ds(s, sz), :] = p.astype(BF16)

    return _pallas(
        body, [a, g], dep=dep, grid=(k // tk, n // tn),
        in_specs=[
            pl.BlockSpec((m, tk), lambda kk, nn: (0, kk)),
            pl.BlockSpec((m, tn), lambda kk, nn: (0, _block_pos(nn // npj, perm) * npj + nn % npj)),
        ],
        out_specs=pl.BlockSpec((None, tk, tn), lambda kk, nn: (nn // npj, kk, nn % npj)),
        out_shape=jax.ShapeDtypeStruct((n_blocks, k, nj), BF16),
        sem=("parallel", "parallel"), name=name)


ROW_TILE = 256


def _row_spec(tr, width):
    return pl.BlockSpec((tr, width), lambda i: (i, 0))


def _full_spec(shape):
    return pl.BlockSpec(shape, lambda *_: (0,) * len(shape))


def _rms(xv):
    return lax.rsqrt(jnp.mean(xv * xv, axis=-1, keepdims=True) + EPS)


def _mod_row(mod_ref, row):
    return mod_ref[pl.ds(row, 1), :]


def pre_mix_fwd(x, mod, gain, dep=None):
    t, d = x.shape
    tr = _tile(t, ROW_TILE, SUBLANES)

    def body(x_ref, mod_ref, g_ref, h_ref):
        xv = x_ref[...]
        y = xv * _rms(xv) * g_ref[...]
        h_ref[...] = (y * (1.0 + _mod_row(mod_ref, SC_M)) + _mod_row(mod_ref, SH_M)).astype(BF16)

    return _pallas(
        body, [x, mod, gain], dep=dep, grid=(t // tr,),
        in_specs=[_row_spec(tr, d), _full_spec(mod.shape), _full_spec(gain.shape)],
        out_specs=_row_spec(tr, d),
        out_shape=jax.ShapeDtypeStruct((t, d), BF16),
        sem=("parallel",), name="pre_mix_fwd")


def pre_ffn_fwd(x, o_m, mod, gain, dep=None):
    t, d = x.shape
    tr = _tile(t, ROW_TILE, SUBLANES)

    def body(x_ref, om_ref, mod_ref, g_ref, x1_ref, h_ref):
        x1 = x_ref[...] + _mod_row(mod_ref, GT_M) * om_ref[...]
        x1_ref[...] = x1
        y = x1 * _rms(x1) * g_ref[...]
        h_ref[...] = (y * (1.0 + _mod_row(mod_ref, SC_F)) + _mod_row(mod_ref, SH_F)).astype(BF16)

    return _pallas(
        body, [x, o_m, mod, gain], dep=dep, grid=(t // tr,),
        in_specs=[_row_spec(tr, d), _row_spec(tr, d), _full_spec(mod.shape), _full_spec(gain.shape)],
        out_specs=[_row_spec(tr, d), _row_spec(tr, d)],
        out_shape=[jax.ShapeDtypeStruct((t, d), F32), jax.ShapeDtypeStruct((t, d), BF16)],
        sem=("parallel",), name="pre_ffn_fwd")


def loss_head(x1, o_f, target, mod, dep=None):
    t, d = x1.shape
    tr = _tile(t, ROW_TILE, SUBLANES)

    def body(x1_ref, of_ref, tg_ref, mod_ref, loss_ref, dy_ref, dof_ref, acc_ref):
        i = pl.program_id(0)
        gt = _mod_row(mod_ref, GT_F)
        of = of_ref[...]
        err = x1_ref[...] + gt * of - tg_ref[...]
        dy = err * (1.0 / d)
        dy_ref[...] = dy
        dof_ref[...] = (dy * gt).astype(BF16)
        part = (0.5 / d) * jnp.sum(jnp.sum(err * err, axis=1, keepdims=True), axis=0, keepdims=True)
        dgt = jnp.sum(dy * of, axis=0, keepdims=True)

        @pl.when(i == 0)
        def _():
            loss_ref[...] = jnp.zeros_like(loss_ref)
            acc_ref[...] = jnp.zeros_like(acc_ref)

        loss_ref[...] += part
        acc_ref[pl.ds(0, 1), :] += dgt

    return _pallas(
        body, [x1, o_f, target, mod], dep=dep, grid=(t // tr,),
        in_specs=[_row_spec(tr, d), _row_spec(tr, d), _row_spec(tr, d), _full_spec(mod.shape)],
        out_specs=[_full_spec((1, 1)), _row_spec(tr, d), _row_spec(tr, d), _full_spec((SUBLANES, d))],
        out_shape=[jax.ShapeDtypeStruct((1, 1), F32), jax.ShapeDtypeStruct((t, d), F32),
                   jax.ShapeDtypeStruct((t, d), BF16), jax.ShapeDtypeStruct((SUBLANES, d), F32)],
        sem=("arbitrary",), name="loss_head")


def _norm_bwd(xv, dh, sc, gain):
    rstd = _rms(xv)
    yn = xv * rstd
    dsh = jnp.sum(dh, axis=0, keepdims=True)
    dsc = jnp.sum(dh * (yn * gain), axis=0, keepdims=True)
    dgain = jnp.sum(dh * (1.0 + sc) * yn, axis=0, keepdims=True)
    dyn = dh * ((1.0 + sc) * gain)
    dx = rstd * (dyn - yn * jnp.mean(dyn * yn, axis=-1, keepdims=True))
    return dx, dsh, dsc, dgain


def pre_ffn_bwd(x1, dh2, dy, o_m, mod, gain, dep=None):
    t, d = x1.shape
    tr = _tile(t, ROW_TILE, SUBLANES)

    def body(x1_ref, dh_ref, dy_ref, om_ref, mod_ref, g_ref, dx1_ref, dom_ref, acc_ref):
        i = pl.program_id(0)
        dxn, dsh, dsc, dgain = _norm_bwd(x1_ref[...], dh_ref[...], _mod_row(mod_ref, SC_F), g_ref[...])
        dx1 = dy_ref[...] + dxn
        dx1_ref[...] = dx1
        dom_ref[...] = (dx1 * _mod_row(mod_ref, GT_M)).astype(BF16)
        dgt = jnp.sum(dx1 * om_ref[...], axis=0, keepdims=True)

        @pl.when(i == 0)
        def _():
            acc_ref[...] = jnp.zeros_like(acc_ref)

        acc_ref[pl.ds(0, 1), :] += dsh
        acc_ref[pl.ds(1, 1), :] += dsc
        acc_ref[pl.ds(2, 1), :] += dgain
        acc_ref[pl.ds(3, 1), :] += dgt

    return _pallas(
        body, [x1, dh2, dy, o_m, mod, gain], dep=dep, grid=(t // tr,),
        in_specs=[_row_spec(tr, d)] * 4 + [_full_spec(mod.shape), _full_spec(gain.shape)],
        out_specs=[_row_spec(tr, d), _row_spec(tr, d), _full_spec((SUBLANES, d))],
        out_shape=[jax.ShapeDtypeStruct((t, d), F32), jax.ShapeDtypeStruct((t, d), BF16),
                   jax.ShapeDtypeStruct((SUBLANES, d), F32)],
        sem=("arbitrary",), name="pre_ffn_bwd")


def pre_mix_bwd(x, dh, dx1, mod, gain, dep=None):
    t, d = x.shape
    tr = _tile(t, ROW_TILE, SUBLANES)

    def body(x_ref, dh_ref, dx1_ref, mod_ref, g_ref, gx_ref, acc_ref):
        i = pl.program_id(0)
        dxn, dsh, dsc, dgain = _norm_bwd(x_ref[...], dh_ref[...], _mod_row(mod_ref, SC_M), g_ref[...])
        gx_ref[...] = dx1_ref[...] + dxn

        @pl.when(i == 0)
        def _():
            acc_ref[...] = jnp.zeros_like(acc_ref)

        acc_ref[pl.ds(0, 1), :] += dsh
        acc_ref[pl.ds(1, 1), :] += dsc
        acc_ref[pl.ds(2, 1), :] += dgain

    return _pallas(
        body, [x, dh, dx1, mod, gain], dep=dep, grid=(t // tr,),
        in_specs=[_row_spec(tr, d)] * 3 + [_full_spec(mod.shape), _full_spec(gain.shape)],
        out_specs=[_row_spec(tr, d), _full_spec((SUBLANES, d))],
        out_shape=[jax.ShapeDtypeStruct((t, d), F32), jax.ShapeDtypeStruct((SUBLANES, d), F32)],
        sem=("arbitrary",), name="pre_mix_bwd")


def merge_fwd(p, y_attn, y_conv, off_ga, off_gc, dep=None):
    t, d = y_attn.shape
    tr = _tile(t, ROW_TILE, SUBLANES)
    cw = math.gcd(math.gcd(off_ga, off_gc), math.gcd(d, 512))
    nc = d // cw

    def body(ga_ref, gc_ref, ya_ref, yc_ref, o_ref):
        o_ref[...] = (_sigmoid(ga_ref[...]) * ya_ref[...] + _sigmoid(gc_ref[...]) * yc_ref[...]).astype(BF16)

    return _pallas(
        body, [p, p, y_attn, y_conv], dep=dep, grid=(t // tr, nc),
        in_specs=[pl.BlockSpec((tr, cw), lambda i, j: (i, off_ga // cw + j)),
                  pl.BlockSpec((tr, cw), lambda i, j: (i, off_gc // cw + j)),
                  pl.BlockSpec((tr, cw), lambda i, j: (i, j)),
                  pl.BlockSpec((tr, cw), lambda i, j: (i, j))],
        out_specs=pl.BlockSpec((tr, cw), lambda i, j: (i, j)),
        out_shape=jax.ShapeDtypeStruct((t, d), BF16),
        sem=("parallel", "parallel"), name="merge_fwd")


def merge_bwd(p, y_attn, y_conv, dmerged, off_ga, off_gc, dep=None):
    t, d = y_attn.shape
    tr = _tile(t, ROW_TILE, SUBLANES)
    cw = math.gcd(math.gcd(off_ga, off_gc), math.gcd(d, 512))
    nc = d // cw

    def body(ga_ref, gc_ref, ya_ref, yc_ref, dm_ref, dya_ref, dyc_ref, dga_ref, dgc_ref):
        dm = dm_ref[...]
        sa = _sigmoid(ga_ref[...])
        sc = _sigmoid(gc_ref[...])
        dya_ref[...] = (dm * sa).astype(BF16)
        dyc_ref[...] = (dm * sc).astype(BF16)
        dga_ref[...] = (dm * ya_ref[...] * sa * (1.0 - sa)).astype(BF16)
        dgc_ref[...] = (dm * yc_ref[...] * sc * (1.0 - sc)).astype(BF16)

    blk = pl.BlockSpec((tr, cw), lambda i, j: (i, j))
    return _pallas(
        body, [p, p, y_attn, y_conv, dmerged], dep=dep, grid=(t // tr, nc),
        in_specs=[pl.BlockSpec((tr, cw), lambda i, j: (i, off_ga // cw + j)),
                  pl.BlockSpec((tr, cw), lambda i, j: (i, off_gc // cw + j)), blk, blk, blk],
        out_specs=[blk] * 4,
        out_shape=[jax.ShapeDtypeStruct((t, d), BF16)] * 4,
        sem=("parallel", "parallel"), name="merge_bwd")


def ffn_perm(n_blocks):
    half = n_blocks // 2
    return tuple(2 * j if j < half else 2 * (j - half) + 1 for j in range(n_blocks))


def swiglu_fwd(f, nj, dep=None):
    t, two = f.shape
    tr = _tile(t, ROW_TILE, SUBLANES)
    npair = two // (2 * nj)

    def body(f_ref, o_ref):
        g = f_ref[:, :nj].astype(F32)
        u = f_ref[:, nj:].astype(F32)
        o_ref[...] = (g * _sigmoid(g) * u).astype(BF16)

    return _pallas(
        body, [f], dep=dep, grid=(t // tr, npair),
        in_specs=[pl.BlockSpec((tr, 2 * nj), lambda i, j: (i, j))],
        out_specs=pl.BlockSpec((tr, nj), lambda i, j: (i, j)),
        out_shape=jax.ShapeDtypeStruct((t, two // 2), BF16),
        sem=("parallel", "parallel"), name="swiglu_fwd")


def swiglu_bwd(f, dact, nj, dep=None):
    t, two = f.shape
    tr = _tile(t, ROW_TILE, SUBLANES)
    npair = two // (2 * nj)

    def body(f_ref, da_ref, o_ref):
        g = f_ref[:, :nj].astype(F32)
        u = f_ref[:, nj:].astype(F32)
        da = da_ref[...]
        s = _sigmoid(g)
        o_ref[:, :nj] = (da * u * (s * (1.0 + g * (1.0 - s)))).astype(BF16)
        o_ref[:, nj:] = (da * (g * s)).astype(BF16)

    return _pallas(
        body, [f, dact], dep=dep, grid=(t // tr, npair),
        in_specs=[pl.BlockSpec((tr, 2 * nj), lambda i, j: (i, j)), pl.BlockSpec((tr, nj), lambda i, j: (i, j))],
        out_specs=pl.BlockSpec((tr, 2 * nj), lambda i, j: (i, j)),
        out_shape=jax.ShapeDtypeStruct((t, two), BF16),
        sem=("parallel", "parallel"), name="swiglu_bwd")


def _t5_bucket_table():
    q_off = np.arange(BLOCK)
    k_off = np.arange(2 * BLOCK)
    dist = q_off[:, None] + BLOCK - k_off[None, :]
    n = np.maximum(dist, 0)
    nf = np.maximum(n, 1).astype(np.float32)
    large = MAX_EXACT + (np.log(nf / np.float32(MAX_EXACT)) / np.float32(math.log(MAX_DISTANCE / MAX_EXACT))
                         * np.float32(NUM_BUCKETS - MAX_EXACT)).astype(np.int32)
    large = np.minimum(large, NUM_BUCKETS - 1)
    bucket = np.where(n < MAX_EXACT, n, large).astype(np.int32)
    allowed = (dist >= 0) & (dist < WINDOW)
    return np.where(allowed, bucket, -1).astype(np.int32)


def bias_table(rel_bias, bucket_p, bucket_c, dep=None):
    nb, nq = rel_bias.shape

    def body(rb_ref, bkp_ref, bkc_ref, op_ref, oc_ref):
        for bk_ref, o_ref in ((bkp_ref, op_ref), (bkc_ref, oc_ref)):
            bk = bk_ref[...]
            for h in range(nq):
                acc = jnp.full(bk.shape, -jnp.inf, F32)
                for b in range(nb):
                    acc = jnp.where(bk == b, rb_ref[b, h], acc)
                o_ref[h] = acc

    return _pallas(
        body, [rel_bias, bucket_p, bucket_c], dep=dep,
        in_specs=[_SMEM, _VMEM, _VMEM], out_specs=[_VMEM, _VMEM],
        out_shape=[jax.ShapeDtypeStruct((nq,) + bucket_p.shape, F32)] * 2,
        name="bias_table")


def bias_table_bwd(dbp, dbc, bucket_p, bucket_c, dep=None):
    nq = dbp.shape[0]

    def body(dbp_ref, dbc_ref, bkp_ref, bkc_ref, o_ref):
        bkp, bkc = bkp_ref[...][None], bkc_ref[...][None]
        dp, dc = dbp_ref[...], dbc_ref[...]
        for b in range(NUM_BUCKETS):
            sel = jnp.where(bkp == b, dp, 0.0) + jnp.where(bkc == b, dc, 0.0)
            o_ref[b] = jnp.sum(jnp.sum(sel, axis=2, keepdims=True), axis=1, keepdims=True)

    return _pallas(
        body, [dbp, dbc, bucket_p, bucket_c], dep=dep,
        in_specs=[_VMEM] * 4, out_specs=_VMEM,
        out_shape=jax.ShapeDtypeStruct((NUM_BUCKETS, nq, 1, 1), F32),
        name="bias_table_bwd")


_NT = (((1,), (1,)), ((), ()))
_TN = (((0,), (0,)), ((), ()))


@jax.custom_vjp
def _bdot_nt(a, b):
    return lax.dot_general(a.astype(BF16), b.astype(BF16), _NT, preferred_element_type=F32)


def _bdot_nt_fwd(a, b):
    return _bdot_nt(a, b), (a, b)


def _bdot_nt_bwd(res, g):
    a, b = res
    gb = g.astype(BF16)
    da = jnp.dot(gb, b.astype(BF16), preferred_element_type=F32)
    db = lax.dot_general(gb, a.astype(BF16), _TN, preferred_element_type=F32)
    return da, db


_bdot_nt.defvjp(_bdot_nt_fwd, _bdot_nt_bwd)


@jax.custom_vjp
def _bdot_nn(a, b):
    return jnp.dot(a.astype(BF16), b.astype(BF16), preferred_element_type=F32)


def _bdot_nn_fwd(a, b):
    return _bdot_nn(a, b), (a, b)


def _bdot_nn_bwd(res, g):
    a, b = res
    gb = g.astype(BF16)
    da = lax.dot_general(gb, b.astype(BF16), _NT, preferred_element_type=F32)
    db = lax.dot_general(a.astype(BF16), gb, _TN, preferred_element_type=F32)
    return da, db


_bdot_nn.defvjp(_bdot_nn_fwd, _bdot_nn_bwd)


def _attn_math(q4, kp, kc, vp, vc, bp, bc, sink4, qg, kg, *, prev_ok, scale):
    g, b, hd = q4.shape
    q = q4.reshape(g * b, hd)
    qn = q * _rms(q) * qg
    kpn = kp * _rms(kp) * kg
    kcn = kc * _rms(kc) * kg
    lp = _bdot_nt(qn, kpn).reshape(g, b, b) * scale + bp
    lc = _bdot_nt(qn, kcn).reshape(g, b, b) * scale + bc
    lp = jnp.where(prev_ok, lp, -jnp.inf)
    m = jnp.maximum(jnp.maximum(jnp.max(lp, axis=-1, keepdims=True), jnp.max(lc, axis=-1, keepdims=True)), sink4)
    m = lax.stop_gradient(m)
    pp = jnp.exp(lp - m)
    pc = jnp.exp(lc - m)
    den = jnp.sum(pp, axis=-1, keepdims=True) + jnp.sum(pc, axis=-1, keepdims=True) + jnp.exp(sink4 - m)
    inv = 1.0 / den
    out = _bdot_nn((pp * inv).reshape(g * b, b), vp) + _bdot_nn((pc * inv).reshape(g * b, b), vc)
    return out.reshape(g, b, hd)


def _attn_specs(p, aw, kvw, nq, hd, nblk, reverse):
    assert aw % (2 * kvw) == 0
    kv_col = aw // (2 * kvw)

    def blk(n):
        return nblk - 1 - n if reverse else n

    return [
        pl.BlockSpec((BLOCK, aw), lambda n: (blk(n), 0)),
        pl.BlockSpec((BLOCK, 2 * kvw), lambda n: (jnp.maximum(blk(n) - 1, 0), kv_col)),
        pl.BlockSpec((BLOCK, 2 * kvw), lambda n: (blk(n), kv_col)),
        _full_spec((nq, BLOCK, BLOCK)), _full_spec((nq, BLOCK, BLOCK)), _full_spec((nq, 1, 1)),
        _full_spec((1, hd)), _full_spec((1, hd)),
    ]


def _attn_head_inputs(h, grp, hd, kvw, q_ref, kvp_ref, kvc_ref, bp_ref, bc_ref, s_ref):
    heads = pl.ds(grp * h, grp)
    q4 = jnp.stack([q_ref[:, pl.ds((grp * h + g) * hd, hd)] for g in range(grp)])
    k_cols, v_cols = pl.ds(h * hd, hd), pl.ds(kvw + h * hd, hd)
    return (q4, kvp_ref[:, k_cols], kvc_ref[:, k_cols], kvp_ref[:, v_cols], kvc_ref[:, v_cols],
            bp_ref[heads], bc_ref[heads], s_ref[heads])


def attn_fwd(p, bias_p, bias_c, sinks, qg, kg, *, aw, kvw, dep=None):
    t, hd = p.shape[0], qg.shape[-1]
    nq, nkv, nblk = aw // hd, kvw // hd, t // BLOCK
    grp = nq // nkv
    scale = hd ** -0.5

    def body(q_ref, kvp_ref, kvc_ref, bp_ref, bc_ref, s_ref, qg_ref, kg_ref, o_ref):
        prev_ok = pl.program_id(0) > 0
        for h in range(nkv):
            args = _attn_head_inputs(h, grp, hd, kvw, q_ref, kvp_ref, kvc_ref, bp_ref, bc_ref, s_ref)
            out = _attn_math(*args, qg_ref[...], kg_ref[...], prev_ok=prev_ok, scale=scale)
            for g in range(grp):
                o_ref[:, pl.ds((grp * h + g) * hd, hd)] = out[g].astype(BF16)

    return _pallas(
        body, [p, p, p, bias_p, bias_c, sinks, qg, kg], dep=dep, grid=(nblk,),
        in_specs=_attn_specs(p, aw, kvw, nq, hd, nblk, False),
        out_specs=pl.BlockSpec((BLOCK, aw), lambda n: (n, 0)),
        out_shape=jax.ShapeDtypeStruct((t, aw), BF16),
        sem=("parallel",), name="attn_fwd")


def attn_bwd(p, bias_p, bias_c, sinks, qg, kg, do, *, aw, kvw, dep=None):
    t, hd = p.shape[0], qg.shape[-1]
    nq, nkv, nblk = aw // hd, kvw // hd, t // BLOCK
    grp = nq // nkv
    scale = hd ** -0.5

    def body(q_ref, kvp_ref, kvc_ref, bp_ref, bc_ref, s_ref, qg_ref, kg_ref, do_ref,
             dqkv_ref, dbp_ref, dbc_ref, ds_ref, dqg_ref, dkg_ref, carry):
        i = pl.program_id(0)
        prev_ok = (nblk - 1 - i) > 0

        @pl.when(i == 0)
        def _():
            carry[...] = jnp.zeros_like(carry)
            dbp_ref[...] = jnp.zeros_like(dbp_ref)
            dbc_ref[...] = jnp.zeros_like(dbc_ref)
            ds_ref[...] = jnp.zeros_like(ds_ref)
            dqg_ref[...] = jnp.zeros_like(dqg_ref)
            dkg_ref[...] = jnp.zeros_like(dkg_ref)

        fn = functools.partial(_attn_math, prev_ok=prev_ok, scale=scale)
        for h in range(nkv):
            args = _attn_head_inputs(h, grp, hd, kvw, q_ref, kvp_ref, kvc_ref, bp_ref, bc_ref, s_ref)
            _, vjp = jax.vjp(fn, *args, qg_ref[...], kg_ref[...])
            do4 = jnp.stack([do_ref[:, pl.ds((grp * h + g) * hd, hd)].astype(F32) for g in range(grp)])
            dq, dkp, dkc, dvp, dvc, dbp, dbc, dsk, dqg, dkg = vjp(do4)
            for g in range(grp):
                dqkv_ref[:, pl.ds((grp * h + g) * hd, hd)] = dq[g].astype(BF16)
            k_cols, v_cols = pl.ds(h * hd, hd), pl.ds(kvw + h * hd, hd)
            dqkv_ref[:, pl.ds(aw + h * hd, hd)] = (dkc + carry[:, k_cols]).astype(BF16)
            dqkv_ref[:, pl.ds(aw + kvw + h * hd, hd)] = (dvc + carry[:, v_cols]).astype(BF16)
            carry[:, k_cols] = dkp
            carry[:, v_cols] = dvp
            heads = pl.ds(grp * h, grp)
            dbp_ref[heads] += dbp
            dbc_ref[heads] += dbc
            ds_ref[heads] += dsk
            dqg_ref[...] += dqg
            dkg_ref[...] += dkg

    return _pallas(
        body, [p, p, p, bias_p, bias_c, sinks, qg, kg, do], dep=dep, grid=(nblk,),
        in_specs=_attn_specs(p, aw, kvw, nq, hd, nblk, True)
        + [pl.BlockSpec((BLOCK, aw), lambda n: (nblk - 1 - n, 0))],
        out_specs=[
            pl.BlockSpec((BLOCK, aw + 2 * kvw), lambda n: (nblk - 1 - n, 0)),
            _full_spec((nq, BLOCK, BLOCK)), _full_spec((nq, BLOCK, BLOCK)), _full_spec((nq, 1, 1)),
            _full_spec((1, hd)), _full_spec((1, hd)),
        ],
        out_shape=[
            jax.ShapeDtypeStruct((t, aw + 2 * kvw), BF16),
            jax.ShapeDtypeStruct((nq, BLOCK, BLOCK), F32),
            jax.ShapeDtypeStruct((nq, BLOCK, BLOCK), F32),
            jax.ShapeDtypeStruct((nq, 1, 1), F32),
            jax.ShapeDtypeStruct((1, hd), F32),
            jax.ShapeDtypeStruct((1, hd), F32),
        ],
        scratch=[pltpu.VMEM((BLOCK, 2 * kvw), F32)],
        sem=("arbitrary",), name="attn_bwd")


CONV_TILE = 256


def _conv_halo_specs(tb, ch, nblk):
    per = tb // CONV_HALO
    last = nblk * per - 1
    cur = pl.BlockSpec((tb, ch), lambda n: (n, 0))
    prev = pl.BlockSpec((CONV_HALO, ch), lambda n: (jnp.maximum(n * per - 1, 0), 0))
    nxt = pl.BlockSpec((CONV_HALO, ch), lambda n: (jnp.minimum((n + 1) * per, last), 0))
    return cur, prev, nxt


def _ln_silu(co, ln_g, ln_b):
    mu = jnp.mean(co, axis=-1, keepdims=True)
    cen = co - mu
    rstd = lax.rsqrt(jnp.mean(cen * cen, axis=-1, keepdims=True) + EPS)
    xhat = cen * rstd
    z = xhat * ln_g + ln_b
    return xhat, rstd, z


def _shifted_copies(src, shifted):
    rows = src.shape[0] - SUBLANES
    for r in range(1, SUBLANES):
        shifted[r, pl.ds(0, rows), :] = src[pl.ds(r, rows), :]


def _rows_from(src, shifted, start, n):
    r = start % SUBLANES
    if r == 0:
        return src[pl.ds(start, n), :]
    return shifted[r, pl.ds(start - r, n), :]


def conv_fwd(ca, cb, conv_w, conv_b, ln_g, ln_b, dep=None):
    t, ch = ca.shape
    tb = _tile(t, CONV_TILE, CONV_HALO)
    nblk = t // tb
    cur, prev, _ = _conv_halo_specs(tb, ch, nblk)
    lead = CONV_HALO - (CONV_WIDTH - 1)

    def body(ca_ref, cb_ref, cap_ref, cbp_ref, w_ref, b_ref, g_ref, bb_ref, s_ref, co_ref, ubuf, ushift):
        n = pl.program_id(0)
        halo = cap_ref[...] * _sigmoid(cbp_ref[...])
        ubuf[pl.ds(0, CONV_HALO), :] = jnp.where(n > 0, halo, 0.0)
        ubuf[pl.ds(CONV_HALO, tb), :] = ca_ref[...] * _sigmoid(cb_ref[...])
        _shifted_copies(ubuf, ushift)
        acc = jnp.broadcast_to(b_ref[...], (tb, ch))
        for k in range(CONV_WIDTH):
            acc = acc + w_ref[pl.ds(k, 1), :] * _rows_from(ubuf, ushift, lead + k, tb)
        co_ref[...] = acc
        _, _, z = _ln_silu(acc, g_ref[...], bb_ref[...])
        s_ref[...] = (z * _sigmoid(z)).astype(BF16)

    vec = _full_spec((1, ch))
    return _pallas(
        body, [ca, cb, ca, cb, conv_w, conv_b, ln_g, ln_b], dep=dep, grid=(nblk,),
        in_specs=[cur, cur, prev, prev, _full_spec(conv_w.shape), vec, vec, vec],
        out_specs=[cur, cur],
        out_shape=[jax.ShapeDtypeStruct((t, ch), BF16), jax.ShapeDtypeStruct((t, ch), F32)],
        scratch=[pltpu.VMEM((CONV_HALO + tb, ch), F32), pltpu.VMEM((SUBLANES, CONV_HALO + tb, ch), F32)],
        sem=("parallel",), name="conv_fwd")


def conv_bwd(ca, cb, co, ds, conv_w, ln_g, ln_b, dep=None):
    t, ch = ca.shape
    tb = _tile(t, CONV_TILE, CONV_HALO)
    nblk = t // tb
    cur, prev, nxt = _conv_halo_specs(tb, ch, nblk)
    lead = CONV_HALO - (CONV_WIDTH - 1)
    ext = tb + CONV_HALO

    def body(ca_ref, cb_ref, cap_ref, cbp_ref, co_ref, con_ref, ds_ref, dsn_ref, w_ref, g_ref, bb_ref,
             dca_ref, dcb_ref, dw_ref, dvec_ref, ubuf, dbuf, ushift, dshift):
        n = pl.program_id(0)
        is_last = n == nblk - 1
        sig_b = _sigmoid(cb_ref[...])
        cav = ca_ref[...]
        ubuf[pl.ds(0, CONV_HALO), :] = jnp.where(n > 0, cap_ref[...] * _sigmoid(cbp_ref[...]), 0.0)
        ubuf[pl.ds(CONV_HALO, tb), :] = cav * sig_b
        _shifted_copies(ubuf, ushift)
        co = jnp.concatenate([co_ref[...], con_ref[...]], axis=0)
        xhat, rstd, z = _ln_silu(co, g_ref[...], bb_ref[...])
        dsv = jnp.concatenate([ds_ref[...].astype(F32), jnp.where(is_last, 0.0, dsn_ref[...].astype(F32))], axis=0)
        sg = _sigmoid(z)
        dz = dsv * (sg * (1.0 + z * (1.0 - sg)))
        dxh = dz * g_ref[...]
        dco = rstd * (dxh - jnp.mean(dxh, axis=-1, keepdims=True)
                      - xhat * jnp.mean(dxh * xhat, axis=-1, keepdims=True))
        dbuf[...] = dco
        _shifted_copies(dbuf, dshift)

        @pl.when(n == 0)
        def _():
            dw_ref[...] = jnp.zeros_like(dw_ref)
            dvec_ref[...] = jnp.zeros_like(dvec_ref)

        dco_cur = dco[:tb]
        dvec_ref[pl.ds(0, 1), :] += jnp.sum(dco_cur, axis=0, keepdims=True)
        dvec_ref[pl.ds(1, 1), :] += jnp.sum(dz[:tb] * xhat[:tb], axis=0, keepdims=True)
        dvec_ref[pl.ds(2, 1), :] += jnp.sum(dz[:tb], axis=0, keepdims=True)
        du = jnp.zeros((tb, ch), F32)
        for k in range(CONV_WIDTH):
            du = du + w_ref[pl.ds(k, 1), :] * _rows_from(dbuf, dshift, CONV_WIDTH - 1 - k, tb)
            dw_ref[pl.ds(k, 1), :] += jnp.sum(dco_cur * _rows_from(ubuf, ushift, lead + k, tb), axis=0,
                                              keepdims=True)
        dca_ref[...] = (du * sig_b).astype(BF16)
        dcb_ref[...] = (du * cav * sig_b * (1.0 - sig_b)).astype(BF16)

    vec = _full_spec((1, ch))
    return _pallas(
        body, [ca, cb, ca, cb, co, co, ds, ds, conv_w, ln_g, ln_b], dep=dep, grid=(nblk,),
        in_specs=[cur, cur, prev, prev, cur, nxt, cur, nxt, _full_spec(conv_w.shape), vec, vec],
        out_specs=[cur, cur, _full_spec(conv_w.shape), _full_spec((SUBLANES, ch))],
        out_shape=[jax.ShapeDtypeStruct((t, ch), BF16), jax.ShapeDtypeStruct((t, ch), BF16),
                   jax.ShapeDtypeStruct(conv_w.shape, F32), jax.ShapeDtypeStruct((SUBLANES, ch), F32)],
        scratch=[pltpu.VMEM((CONV_HALO + tb, ch), F32), pltpu.VMEM((ext, ch), F32),
                 pltpu.VMEM((SUBLANES, CONV_HALO + tb, ch), F32), pltpu.VMEM((SUBLANES, ext, ch), F32)],
        sem=("arbitrary",), name="conv_bwd")


def ada_fwd(c_t, w_ada, dep=None):
    d, nc = w_ada.shape
    nex = c_t.shape[1]
    tn = _tile(nc, 512)

    def body(ct_ref, w_ref, o_ref):
        w = w_ref[...]
        ct = ct_ref[...]
        cact = ct * _sigmoid(ct)
        rows = [jnp.sum(w * cact[:, b:b + 1], axis=0, keepdims=True) for b in range(nex)]
        o_ref[...] = jnp.concatenate(rows, axis=0)

    return _pallas(
        body, [c_t, w_ada], dep=dep, grid=(nc // tn,),
        in_specs=[_full_spec(c_t.shape), pl.BlockSpec((d, tn), lambda j: (0, j))],
        out_specs=pl.BlockSpec((nex, tn), lambda j: (0, j)),
        out_shape=jax.ShapeDtypeStruct((nex, nc), F32),
        sem=("parallel",), name="ada_fwd")


def _adamw_math(w, g, m, v):
    m = ADAM_B1 * m + (1.0 - ADAM_B1) * g
    v = ADAM_B2 * v + (1.0 - ADAM_B2) * (g * g)
    m_hat = m / (1.0 - ADAM_B1 ** ADAM_STEP)
    v_hat = v / (1.0 - ADAM_B2 ** ADAM_STEP)
    delta = -ADAM_LR * (m_hat / (jnp.sqrt(v_hat) + ADAM_EPS) + ADAM_WD * w)
    return delta, m, v


def adamw(w, g, m, v, name, copy_grad=False, dep=None):
    r, n = w.shape
    tr, tn = _ew_tiles(r, n, elems=256 * 1024)
    n_out = 4 if copy_grad else 3

    def body(w_ref, g_ref, m_ref, v_ref, *outs):
        g = g_ref[...]
        if copy_grad:
            outs[0][...] = g
        outs[-3][...], outs[-2][...], outs[-1][...] = _adamw_math(w_ref[...], g, m_ref[...], v_ref[...])

    blk = pl.BlockSpec((tr, tn), lambda i, j: (i, j))
    return _pallas(
        body, [w, g, m, v], dep=dep, grid=(r // tr, n // tn),
        in_specs=[blk] * 4, out_specs=[blk] * n_out,
        out_shape=[jax.ShapeDtypeStruct((r, n), F32)] * n_out,
        sem=("parallel", "parallel"), name=name)


def ada_grad_adamw(c_t, dmod_cols, w, m, v, dep=None):
    d, nc = w.shape
    nex = c_t.shape[1]
    tr, tn = _ew_tiles(d, nc, elems=256 * 1024)

    def body(ct_ref, dm_ref, w_ref, m_ref, v_ref, g_ref, d_ref, nm_ref, nv_ref):
        ct = ct_ref[...]
        cact = ct * _sigmoid(ct)
        dm = dm_ref[...]
        g = cact[:, 0:1] * dm[0:1, :]
        for b in range(1, nex):
            g = g + cact[:, b:b + 1] * dm[b:b + 1, :]
        g_ref[...] = g
        d_ref[...], nm_ref[...], nv_ref[...] = _adamw_math(w_ref[...], g, m_ref[...], v_ref[...])

    blk = pl.BlockSpec((tr, tn), lambda i, j: (i, j))
    return _pallas(
        body, [c_t, dmod_cols, w, m, v], dep=dep, grid=(d // tr, nc // tn),
        in_specs=[pl.BlockSpec((tr, nex), lambda i, j: (i, 0)), pl.BlockSpec((nex, tn), lambda i, j: (0, j)),
                  blk, blk, blk],
        out_specs=[blk] * 4,
        out_shape=[jax.ShapeDtypeStruct((d, nc), F32)] * 4,
        sem=("parallel", "parallel"), name="ada_grad_adamw")


def _row_pack(parts):
    cols, offs, off = [], [], 0
    for p in parts:
        n = p.shape[1]
        width = -(-n // LANES) * LANES
        cols.append(jnp.pad(p, ((0, 0), (0, width - n))) if width != n else p)
        offs.append(off)
        off += width
    return jnp.concatenate(cols, axis=1), offs


def small_sum_adamw(gathered, offs, ws, ms, vs, extra_widths, dep=None):
    ndev = gathered.shape[0]
    npar = len(ws)

    def body(ga_ref, *refs):
        w_refs, m_refs, v_refs = refs[:npar], refs[npar:2 * npar], refs[2 * npar:3 * npar]
        outs = refs[3 * npar:]
        tot = ga_ref[0]
        for s in range(1, ndev):
            tot = tot + ga_ref[s]
        for i in range(npar):
            n = ws[i].shape[1]
            g = tot[:, offs[i]:offs[i] + n]
            outs[4 * i][...] = g
            outs[4 * i + 1][...], outs[4 * i + 2][...], outs[4 * i + 3][...] = _adamw_math(
                w_refs[i][...], g, m_refs[i][...], v_refs[i][...])
        for e, n in enumerate(extra_widths):
            off = offs[npar + e]
            outs[4 * npar + e][...] = tot[:, off:off + n]

    shapes = [jax.ShapeDtypeStruct(w.shape, F32) for w in ws for _ in range(4)]
    shapes += [jax.ShapeDtypeStruct((1, n), F32) for n in extra_widths]
    return _pallas(
        body, [gathered, *ws, *ms, *vs], dep=dep, in_specs=[_VMEM] * (1 + 3 * npar), out_specs=[_VMEM] * len(shapes),
        out_shape=shapes, name="small_sum_adamw")


def _position():
    return lax.axis_index("x"), lax.axis_index("y"), lax.axis_index("c")


def _other_chips(x, y):
    return [(1 - x, y), (x, 1 - y), (1 - x, 1 - y)]


def allgather_small(block, name, dep=None):
    def body(x_ref, out_ref, send_sems, recv_sems, local_sem):
        x, y, c = _position()
        me, sibling = (x, y, c), (x, y, 1 - c)
        chips = _other_chips(x, y)

        def slot(px, py, pc):
            return out_ref.at[4 * px + 2 * py + pc]

        def copy(k, block_of, to, src=None):
            return pltpu.make_async_remote_copy(
                src_ref=slot(*block_of) if src is None else src, dst_ref=slot(*block_of),
                send_sem=send_sems.at[k], recv_sem=recv_sems.at[k], device_id=to, device_id_type=MESH)

        mine = pltpu.make_async_copy(x_ref, slot(*me), local_sem)
        mine.start()
        first = [copy(0, me, sibling, src=x_ref)]
        first += [copy(1 + j, me, (*chip, c), src=x_ref) for j, chip in enumerate(chips)]
        for cp in first:
            cp.start()
        passed = [copy(4 + j, (*chip, c), sibling) for j, chip in enumerate(chips)]
        for j, chip in enumerate(chips):
            copy(1 + j, (*chip, c), me).wait_recv()
            passed[j].start()
        copy(0, sibling, me).wait_recv()
        for j, chip in enumerate(chips):
            copy(4 + j, (*chip, 1 - c), me).wait_recv()
        for cp in first + passed:
            cp.wait_send()
        mine.wait()

    return _pallas(
        body, [block], dep=dep,
        out_shape=jax.ShapeDtypeStruct((N_DEV, *block.shape), block.dtype),
        in_specs=[_VMEM], out_specs=_VMEM,
        scratch=[pltpu.SemaphoreType.DMA((7,)), pltpu.SemaphoreType.DMA((7,)), pltpu.SemaphoreType.DMA],
        name=name)


class Started(NamedTuple):
    send_sems: Any
    recv_sems: Any
    bufs: list


def exchange_start(name, bufs, n_copies, plan, dep=None):
    nb = len(bufs)

    def body(*refs):
        for cp in plan(refs[:nb], refs[nb], refs[nb + 1]):
            cp.start()

    outs = _pallas(
        body, [pltpu.with_memory_space_constraint(b, pltpu.HBM) for b in bufs], dep=dep, name=name,
        out_shape=(pltpu.SemaphoreType.DMA((n_copies,)), pltpu.SemaphoreType.DMA((n_copies,)),
                   *[pltpu.HBM(b.shape, b.dtype) for b in bufs]),
        in_specs=[_HBM] * nb,
        out_specs=(_SEM, _SEM, *[_HBM] * nb),
        input_output_aliases={i: 2 + i for i in range(nb)},
        compiler_params=pltpu.CompilerParams(has_side_effects=_EFFECT))
    return Started(outs[0], outs[1], list(outs[2:2 + nb]))


def exchange_wait(name, started, plan, bufs=None, dep=None):
    if bufs is not None:
        started = started._replace(bufs=list(bufs))
    nb = len(started.bufs)

    def body(*refs):
        for cp in plan(refs[:nb], refs[nb], refs[nb + 1]):
            cp.wait_send()
            cp.wait_recv()

    outs = _pallas(
        body, [*started.bufs, started.send_sems, started.recv_sems], dep=dep, name=name,
        out_shape=tuple(pltpu.HBM(b.shape, b.dtype) for b in started.bufs),
        in_specs=[_HBM] * nb + [_SEM, _SEM],
        out_specs=tuple([_HBM] * nb),
        input_output_aliases={i: i for i in range(nb)},
        compiler_params=pltpu.CompilerParams(has_side_effects=_EFFECT))
    return list(outs)


def _remote(src, dst, send_sems, recv_sems, i, to):
    return pltpu.make_async_remote_copy(src_ref=src, dst_ref=dst, send_sem=send_sems.at[i], recv_sem=recv_sems.at[i],
                                        device_id=to, device_id_type=MESH)


def _half_rows(buf_rows, chip_idx, pc):
    half = buf_rows // (2 * N_CHIPS)
    return pl.ds((2 * chip_idx + pc) * half, half)


ALL_PEERS = (0, 1, 2)


def plan_gather_ici(refs, send_sems, recv_sems, peers=ALL_PEERS):
    x, y, c = _position()
    chips = _other_chips(x, y)
    copies = []
    for k, ref in enumerate(refs):
        rows = ref.at[_half_rows(ref.shape[0], 2 * x + y, c), :]
        for i, j in enumerate(peers):
            copies.append(_remote(rows, rows, send_sems, recv_sems, len(peers) * k + i, (*chips[j], c)))
    return copies


def plan_gather_d2d(refs, send_sems, recv_sems, peers=ALL_PEERS):
    x, y, c = _position()
    chips = _other_chips(x, y)
    copies = []
    for k, ref in enumerate(refs):
        for i, j in enumerate(peers):
            px, py = chips[j]
            rows = ref.at[_half_rows(ref.shape[0], 2 * px + py, c), :]
            copies.append(_remote(rows, rows, send_sems, recv_sems, len(peers) * k + i, (x, y, 1 - c)))
    return copies


def plan_pair_exchange(refs, send_sems, recv_sems):
    x, y, c = _position()
    nw = len(refs) // 2
    copies = []
    for k in range(nw):
        for chip in range(N_CHIPS):
            copies.append(_remote(refs[k].at[chip, 1 - c], refs[nw + k].at[chip], send_sems, recv_sems,
                                  N_CHIPS * k + chip, (x, y, 1 - c)))
    return copies


def plan_chip_exchange(refs, send_sems, recv_sems):
    x, y, c = _position()
    nw = len(refs) // 2
    copies = []
    for k in range(nw):
        for j, (px, py) in enumerate(_other_chips(x, y)):
            copies.append(_remote(refs[k].at[2 * px + py], refs[nw + k].at[2 * x + y], send_sems, recv_sems,
                                  3 * k + j, (px, py, c)))
    return copies


def plan_pair_share(refs, send_sems, recv_sems):
    x, y, c = _position()
    return [_remote(ref.at[c], ref.at[c], send_sems, recv_sems, k, (x, y, 1 - c)) for k, ref in enumerate(refs)]


def cast_into_slot(src, slot, n_slots, name, dep=None):
    r, n = src.shape
    tr, tn = _ew_tiles(r, n, BF16_SUBLANES)

    def body(slot_ref, s_ref, o_ref):
        o_ref[...] = s_ref[...].astype(BF16)

    return _pallas(
        body, [slot, src], dep=dep, n_prefetch=1, grid=(r // tr, n // tn),
        in_specs=[pl.BlockSpec((tr, tn), lambda i, j, sl: (i, j))],
        out_specs=pl.BlockSpec((None, tr, tn), lambda i, j, sl: (sl[0], i, j)),
        out_shape=jax.ShapeDtypeStruct((n_slots, r, n), BF16),
        sem=("parallel", "parallel"), name=name)


def pair_sum(g, r, core, name, dep=None):
    nchip, _, h, n = g.shape
    th, tn = _ew_tiles(h, n, BF16_SUBLANES)

    def body(core_ref, g_ref, r_ref, o_ref):
        o_ref[...] = (g_ref[...].astype(F32) + r_ref[...].astype(F32)).astype(BF16)

    return _pallas(
        body, [core, g, r], dep=dep, n_prefetch=1, grid=(nchip, h // th, n // tn),
        in_specs=[pl.BlockSpec((None, None, th, tn), lambda a, i, j, cr: (a, cr[0], i, j)),
                  pl.BlockSpec((None, th, tn), lambda a, i, j, cr: (a, i, j))],
        out_specs=pl.BlockSpec((None, th, tn), lambda a, i, j, cr: (a, i, j)),
        out_shape=jax.ShapeDtypeStruct((nchip, h, n), BF16),
        sem=("parallel", "parallel", "parallel"), name=name)


def chip_sum(own, got, where, name, dep=None):
    nchip, h, n = got.shape
    th, tn = _ew_tiles(h, n, BF16_SUBLANES, elems=256 * 1024)

    def body(where_ref, own_ref, *rest):
        got_refs, o_ref = rest[:nchip], rest[nchip]
        chip = where_ref[0]
        acc = None
        for s in range(nchip):
            term = jnp.where(chip == s, own_ref[...], got_refs[s][...]).astype(F32)
            acc = term if acc is None else acc + term
        o_ref[...] = acc

    def got_spec(s):
        return pl.BlockSpec((None, th, tn), lambda i, j, wr: (jnp.where(wr[0] == s, (s + 1) % nchip, s), i, j))

    return _pallas(
        body, [where, own, *[got] * nchip], dep=dep, n_prefetch=1, grid=(h // th, n // tn),
        in_specs=[pl.BlockSpec((None, th, tn), lambda i, j, wr: (wr[0], i, j))]
        + [got_spec(s) for s in range(nchip)],
        out_specs=pl.BlockSpec((None, th, tn), lambda i, j, wr: (wr[1], i, j)),
        out_shape=jax.ShapeDtypeStruct((2, h, n), F32),
        sem=("parallel", "parallel"), name=name)


def kernel(x, c, w_ada, b_ada, norm_mix_g, w_in, q_norm_g, k_norm_g, attn_sinks, rel_bias, w_attn_out, conv_w, conv_b, conv_ln_g, conv_ln_b, w_conv_out, w_mix_out, norm_ffn_g, w_ffn_in, w_ffn_out, loss_target, m_w_ada, m_b_ada, m_norm_mix_g, m_w_in, m_q_norm_g, m_k_norm_g, m_attn_sinks, m_rel_bias, m_w_attn_out, m_conv_w, m_conv_b, m_conv_ln_g, m_conv_ln_b, m_w_conv_out, m_w_mix_out, m_norm_ffn_g, m_w_ffn_in, m_w_ffn_out, v_w_ada, v_b_ada, v_norm_mix_g, v_w_in, v_q_norm_g, v_k_norm_g, v_attn_sinks, v_rel_bias, v_w_attn_out, v_conv_w, v_conv_b, v_conv_ln_g, v_conv_ln_b, v_w_conv_out, v_w_mix_out, v_norm_ffn_g, v_w_ffn_in, v_w_ffn_out):
    run = InOrder()
    xi, yi, ci = _position()
    chip = 2 * xi + yi
    me = 2 * chip + ci
    chip_arr = chip.astype(jnp.int32).reshape(1)
    core_arr = ci.astype(jnp.int32).reshape(1)
    where_arr = jnp.stack([chip, ci]).astype(jnp.int32)

    xe, tgt = x[0], loss_target[0]
    t, d = xe.shape
    hd = q_norm_g.shape[-1]
    nq = attn_sinks.shape[-1]
    aw = nq * hd
    ch = conv_b.shape[-1]
    in_width = N_CHIPS * w_in.shape[-1]
    kvw = (in_width - aw - 2 * ch - 2 * d) // 2
    nkv = kvw // hd
    dff = N_CHIPS * w_ffn_out.shape[1]
    off_k, off_v, off_ca = aw, aw + kvw, aw + 2 * kvw
    off_cb, off_ga, off_gc = off_ca + ch, off_ca + 2 * ch, off_ca + 2 * ch + d
    nc_ada = w_ada.shape[-1]
    ch_loc = conv_w.shape[-1]
    nj_ffn = w_ffn_in.shape[-1]
    perm_ffn = ffn_perm(N_CHIPS)

    big = {"w_in": w_in[0], "w_attn_out": w_attn_out[0], "w_conv_out": w_conv_out[0], "w_mix_out": w_mix_out[0],
           "w_ffn_in": w_ffn_in[0], "w_ffn_out": w_ffn_out[0]}
    moments = {"w_in": (m_w_in, v_w_in), "w_attn_out": (m_w_attn_out, v_w_attn_out),
               "w_conv_out": (m_w_conv_out, v_w_conv_out), "w_mix_out": (m_w_mix_out, v_w_mix_out),
               "w_ffn_in": (m_w_ffn_in, v_w_ffn_in), "w_ffn_out": (m_w_ffn_out, v_w_ffn_out)}
    gather_groups = {"in": ["w_in"], "mid": ["w_attn_out", "w_conv_out", "w_mix_out"], "ffn_in": ["w_ffn_in"],
                     "ffn_out": ["w_ffn_out"]}
    grads, deltas, new_m, new_v = {}, {}, {}, {}

    def gather_cast(gname):
        bufs = []
        for n in gather_groups[gname]:
            r, ncol = big[n].shape
            bufs.append(run(cast_into_slot, big[n], chip_arr, N_CHIPS, "cast_" + n).reshape(N_CHIPS * r, ncol))
        return bufs

    def gather_ici_start(gname, bufs):
        return run(exchange_start, "gather_ici_start_" + gname, bufs, 3 * len(bufs), plan_gather_ici)

    def gather_pass_on(gname, ici):
        landed = run(exchange_wait, "gather_ici_wait_" + gname, ici, plan_gather_ici)
        return run(exchange_start, "gather_d2d_start_" + gname, landed, 3 * len(landed), plan_gather_d2d)

    def gathered(gname, d2d):
        outs = run(exchange_wait, "gather_d2d_wait_" + gname, d2d, plan_gather_d2d)
        return [o.reshape(N_CHIPS, *big[n].shape) for o, n in zip(outs, gather_groups[gname])]

    def rs_pair_start(gname, names, partials):
        blocks = [g.reshape(N_CHIPS, 2, big[n].shape[0] // 2, big[n].shape[1]) for n, g in zip(names, partials)]
        land = [lax.empty((N_CHIPS,) + b.shape[2:], BF16) for b in blocks]
        return run(exchange_start, "pair_exchange_start_" + gname, blocks + land, N_CHIPS * len(blocks),
                   plan_pair_exchange)

    def rs_chip_start(gname, names, pair):
        nw = len(names)
        outs = run(exchange_wait, "pair_exchange_wait_" + gname, pair, plan_pair_exchange)
        sums = [run(pair_sum, g, r, core_arr, "pair_sum_" + n) for n, g, r in zip(names, outs[:nw], outs[nw:])]
        land = [lax.empty(s.shape, BF16) for s in sums]
        return run(exchange_start, "chip_exchange_start_" + gname, sums + land, 3 * nw, plan_chip_exchange)

    def rs_share_start(gname, names, chipx):
        nw = len(names)
        outs = run(exchange_wait, "chip_exchange_wait_" + gname, chipx, plan_chip_exchange)
        halves = [run(chip_sum, s, r, where_arr, "chip_sum_" + n) for n, s, r in zip(names, outs[:nw], outs[nw:])]
        return run(exchange_start, "pair_share_start_" + gname, halves, nw, plan_pair_share)

    def rs_finish(gname, names, share):
        fulls = run(exchange_wait, "pair_share_wait_" + gname, share, plan_pair_share)
        for n, g2 in zip(names, fulls):
            g, dl, nm, nv = run(adamw, big[n], g2.reshape(big[n].shape), moments[n][0][0], moments[n][1][0],
                                "adamw_" + n, copy_grad=True)
            grads[n], deltas[n], new_m[n], new_v[n] = g[None], dl[None], nm[None], nv[None]

    bufs_in = gather_cast("in")
    row1, offs1 = _row_pack([c, conv_w[0].reshape(1, CONV_WIDTH * ch_loc)])
    got1 = run(allgather_small, row1, "allgather_cond")
    c_all = got1[:, 0, :d]
    conv_w_full = got1[0::2, 0, offs1[1]:offs1[1] + CONV_WIDTH * ch_loc].reshape(N_CHIPS, CONV_WIDTH, ch_loc)
    conv_w_full = jnp.transpose(conv_w_full, (1, 0, 2)).reshape(CONV_WIDTH, ch)
    conv_w_pad = jnp.pad(conv_w_full, ((0, 1), (0, 0)))
    c_t = jnp.transpose(c_all)
    mod_cols = run(ada_fwd, c_t, w_ada[0])
    got2 = run(allgather_small, mod_cols, "allgather_mod")
    mod_all = got2.reshape(N_CHIPS, 2, N_DEV, nc_ada)[:, 0]
    mod = lax.dynamic_slice_in_dim(mod_all, me, 1, axis=1).reshape(1, N_CHIPS * nc_ada) + b_ada
    mod = jnp.pad(mod.reshape(N_MOD, d), ((0, SUBLANES - N_MOD), (0, 0)))

    near, far = (0, 1), (2,)
    plan_ici_near = functools.partial(plan_gather_ici, peers=near)
    plan_ici_far = functools.partial(plan_gather_ici, peers=far)
    plan_d2d_near = functools.partial(plan_gather_d2d, peers=near)
    plan_d2d_far = functools.partial(plan_gather_d2d, peers=far)
    ici_near = run(exchange_start, "gather_ici_start_in_near", bufs_in, len(near), plan_ici_near)
    ici_far = run(exchange_start, "gather_ici_start_in_far", ici_near.bufs, len(far), plan_ici_far)
    ici = {gname: gather_ici_start(gname, gather_cast(gname)) for gname in ("mid", "ffn_in", "ffn_out")}

    h = run(pre_mix_fwd, xe, mod, norm_mix_g)
    bucket = _t5_bucket_table()
    bucket_p, bucket_c = jnp.asarray(bucket[:, :BLOCK]), jnp.asarray(bucket[:, BLOCK:])
    bias_p, bias_c = run(bias_table, rel_bias, bucket_p, bucket_c)

    def in_blocks(buf):
        return buf.reshape(N_CHIPS, *big["w_in"].shape)

    def chip_ids(peers):
        others = [2 * (1 - xi) + yi, 2 * xi + (1 - yi), 2 * (1 - xi) + (1 - yi)]
        return jnp.stack([others[j] for j in peers]).astype(jnp.int32)

    tn_in = _tile(big["w_in"].shape[1], 640)
    landed = run(exchange_wait, "gather_ici_wait_in_near", ici_near, plan_ici_near, bufs=ici_far.bufs)
    d2d_near = run(exchange_start, "gather_d2d_start_in_near", landed, len(near), plan_d2d_near)
    p = run(mm_nn_blocks, h, in_blocks(d2d_near.bufs[0]), chip_arr, None, tn=tn_in, out_dtype=F32, name="mm_in_own")
    landed = run(exchange_wait, "gather_d2d_wait_in_near", d2d_near, plan_d2d_near)
    p = run(mm_nn_blocks, h, in_blocks(landed[0]), chip_ids(near), p, tn=tn_in, out_dtype=F32, name="mm_in_near")
    landed = run(exchange_wait, "gather_ici_wait_in_far", ici_far, plan_ici_far, bufs=landed)
    d2d_far = run(exchange_start, "gather_d2d_start_in_far", landed, len(far), plan_d2d_far)
    landed = run(exchange_wait, "gather_d2d_wait_in_far", d2d_far, plan_d2d_far)
    wg_in = in_blocks(landed[0])
    p = run(mm_nn_blocks, h, wg_in, chip_ids(far), p, tn=tn_in, out_dtype=F32, name="mm_in_far")
    d2d_mid = gather_pass_on("mid", ici["mid"])

    sinks3 = attn_sinks.reshape(nq, 1, 1)
    attn_o = run(attn_fwd, p, bias_p, bias_c, sinks3, q_norm_g, k_norm_g, aw=aw, kvw=kvw)
    ca, cb = p[:, off_ca:off_cb], p[:, off_cb:off_ga]
    s_conv, co_conv = run(conv_fwd, ca, cb, conv_w_pad, conv_b, conv_ln_g, conv_ln_b)
    wg_attn_out, wg_conv_out, wg_mix_out = gathered("mid", d2d_mid)
    wg_mix_out = wg_mix_out.reshape(1, d, d)
    y_attn = run(mm_nn, attn_o, wg_attn_out, tn=_tile(wg_attn_out.shape[2], 512), tk=aw, out_dtype=BF16,
                 name="mm_attn_out")
    y_conv = run(mm_nn, s_conv, wg_conv_out, tn=_tile(wg_conv_out.shape[2], 512), tk=ch, out_dtype=BF16,
                 name="mm_conv_out")
    merged = run(merge_fwd, p, y_attn, y_conv, off_ga, off_gc)
    d2d_ffn_in = gather_pass_on("ffn_in", ici["ffn_in"])
    o_m = run(mm_nn, merged, wg_mix_out, tn=_tile(d, 512), tk=d, out_dtype=F32, name="mm_mix_out")
    x1, h2 = run(pre_ffn_fwd, xe, o_m, mod, norm_ffn_g)
    (wg_ffn_in,) = gathered("ffn_in", d2d_ffn_in)
    f = run(mm_nn, h2, wg_ffn_in, tn=_tile(nj_ffn, 1408), tk=d, out_dtype=BF16, name="mm_ffn_in", perm=perm_ffn)
    d2d_ffn_out = gather_pass_on("ffn_out", ici["ffn_out"])
    act = run(swiglu_fwd, f, nj_ffn)
    (wg_ffn_out,) = gathered("ffn_out", d2d_ffn_out)
    wg_ffn_out = wg_ffn_out.reshape(1, dff, d)
    o_f = run(mm_nn, act, wg_ffn_out, tn=_tile(d, 1024), tk=_tile(dff, 1408), out_dtype=F32, name="mm_ffn_out")
    loss11, dy, dof, acc_l = run(loss_head, x1, o_f, tgt, mod)

    gw_ffn_out = run(mm_tn, act, dof, 1, tk=_tile(dff, 512), tn=d, name="mm_ffn_out_dw")
    px_ffn_out = rs_pair_start("ffn_out", ["w_ffn_out"], [gw_ffn_out])
    dact = run(mm_nt, dof, wg_ffn_out, tko=_tile(dff, 512), tn=d, out_dtype=BF16, name="mm_ffn_out_dx")
    cx_ffn_out = rs_chip_start("ffn_out", ["w_ffn_out"], px_ffn_out)
    df = run(swiglu_bwd, f, dact, nj_ffn)
    gw_ffn_in = run(mm_tn, h2, df, N_CHIPS, tk=d, tn=_tile(nj_ffn, 1408), name="mm_ffn_in_dw",
                    perm=perm_ffn)
    px_ffn_in = rs_pair_start("ffn_in", ["w_ffn_in"], [gw_ffn_in])
    dh2 = run(mm_nt, df, wg_ffn_in, tko=_tile(d, 512), tn=nj_ffn, name="mm_ffn_in_dx", perm=perm_ffn)
    sh_ffn_out = rs_share_start("ffn_out", ["w_ffn_out"], cx_ffn_out)
    cx_ffn_in = rs_chip_start("ffn_in", ["w_ffn_in"], px_ffn_in)
    dx1, dom, acc_f = run(pre_ffn_bwd, x1, dh2, dy, o_m, mod, norm_ffn_g)
    gw_mix_out = run(mm_tn, merged, dom, 1, tk=d, tn=_tile(d, 1024), name="mm_mix_out_dw")
    px_mix = rs_pair_start("mix_out", ["w_mix_out"], [gw_mix_out])
    dmerged = run(mm_nt, dom, wg_mix_out, tko=_tile(d, 512), tn=d, out_dtype=BF16, name="mm_mix_out_dx")
    dy_attn, dy_conv, dga, dgc = run(merge_bwd, p, y_attn, y_conv, dmerged, off_ga, off_gc)
    rs_finish("ffn_out", ["w_ffn_out"], sh_ffn_out)
    cx_mix = rs_chip_start("mix_out", ["w_mix_out"], px_mix)
    gw_attn_out = run(mm_tn, attn_o, dy_attn, N_CHIPS, tk=aw, tn=_tile(wg_attn_out.shape[2], 512),
                      name="mm_attn_out_dw")
    gw_conv_out = run(mm_tn, s_conv, dy_conv, N_CHIPS, tk=ch, tn=_tile(wg_conv_out.shape[2], 512),
                      name="mm_conv_out_dw")
    ac_names = ["w_attn_out", "w_conv_out"]
    px_ac = rs_pair_start("attn_conv_out", ac_names, [gw_attn_out, gw_conv_out])
    dattn_o = run(mm_nt, dy_attn, wg_attn_out, tko=_tile(aw, 1024), tn=_tile(wg_attn_out.shape[2], 512),
                  out_dtype=BF16, name="mm_attn_out_dx")
    ds_conv = run(mm_nt, dy_conv, wg_conv_out, tko=_tile(ch, 1024), tn=_tile(wg_conv_out.shape[2], 512),
                  out_dtype=BF16, name="mm_conv_out_dx")
    cx_ac = rs_chip_start("attn_conv_out", ac_names, px_ac)
    dca, dcb, dconv_w, dconv_vec = run(conv_bwd, ca, cb, co_conv, ds_conv, conv_w_pad, conv_ln_g, conv_ln_b)
    sh_ffn_in = rs_share_start("ffn_in", ["w_ffn_in"], cx_ffn_in)
    dqkv, dbp, dbc, dsinks, dqg, dkg = run(attn_bwd, p, bias_p, bias_c, sinks3, q_norm_g, k_norm_g, dattn_o,
                                           aw=aw, kvw=kvw)
    sh_mix = rs_share_start("mix_out", ["w_mix_out"], cx_mix)
    sh_ac = rs_share_start("attn_conv_out", ac_names, cx_ac)
    drel = run(bias_table_bwd, dbp, dbc, bucket_p, bucket_c).reshape(NUM_BUCKETS, nq)
    dp = jnp.concatenate([dqkv, dca, dcb, dga, dgc], axis=1)
    gw_in = run(mm_tn, h, dp, N_CHIPS, tk=d, tn=_tile(wg_in.shape[2], 640), name="mm_in_dw")
    px_in = rs_pair_start("in", ["w_in"], [gw_in])
    dh = run(mm_nt, dp, wg_in, tko=_tile(d, 1024), tn=wg_in.shape[2], name="mm_in_dx")
    grad_x, acc_m = run(pre_mix_bwd, xe, dh, dx1, mod, norm_mix_g)

    dmod = jnp.concatenate([acc_m[0:1], acc_m[1:2], acc_f[3:4], acc_f[0:1], acc_f[1:2], acc_l[0:1]], axis=1)
    small_names = ["b_ada", "norm_mix_g", "q_norm_g", "k_norm_g", "attn_sinks", "rel_bias", "conv_b", "conv_ln_g",
                   "conv_ln_b", "norm_ffn_g"]
    small_w = [b_ada, norm_mix_g, q_norm_g, k_norm_g, attn_sinks, rel_bias, conv_b, conv_ln_g, conv_ln_b, norm_ffn_g]
    small_m = [m_b_ada, m_norm_mix_g, m_q_norm_g, m_k_norm_g, m_attn_sinks, m_rel_bias, m_conv_b, m_conv_ln_g,
               m_conv_ln_b, m_norm_ffn_g]
    small_v = [v_b_ada, v_norm_mix_g, v_q_norm_g, v_k_norm_g, v_attn_sinks, v_rel_bias, v_conv_b, v_conv_ln_g,
               v_conv_ln_b, v_norm_ffn_g]
    small_g = [dmod, acc_m[2:3], dqg, dkg, dsinks.reshape(1, nq), drel.reshape(1, NUM_BUCKETS * nq),
               dconv_vec[0:1], dconv_vec[1:2], dconv_vec[2:3], acc_f[2:3]]
    row3, offs3 = _row_pack(small_g + [dconv_w[:CONV_WIDTH].reshape(1, CONV_WIDTH * ch), loss11])
    got3 = run(allgather_small, row3, "allgather_small_grads")
    cx_in = rs_chip_start("in", ["w_in"], px_in)
    as_row = lambda a: a.reshape(1, -1)
    outs3 = run(small_sum_adamw, got3, offs3, [as_row(a) for a in small_w], [as_row(a) for a in small_m],
                [as_row(a) for a in small_v], [CONV_WIDTH * ch, 1])
    for i, (n, w) in enumerate(zip(small_names, small_w)):
        grads[n], deltas[n], new_m[n], new_v[n] = (o.reshape(w.shape) for o in outs3[4 * i:4 * i + 4])
    g_conv_w_all, loss_sum = outs3[-2].reshape(CONV_WIDTH, ch), outs3[-1]

    g_conv_w = lax.dynamic_slice_in_dim(g_conv_w_all, chip * ch_loc, ch_loc, axis=1)
    grads["conv_w"] = g_conv_w[None]
    dl, nm, nv = run(adamw, conv_w[0], g_conv_w, m_conv_w[0], v_conv_w[0], "adamw_conv_w")
    deltas["conv_w"], new_m["conv_w"], new_v["conv_w"] = dl[None], nm[None], nv[None]

    dmod_all = got3[:, 0, :N_MOD * d]
    dmod_cols = lax.dynamic_slice_in_dim(dmod_all, chip * nc_ada, nc_ada, axis=1)
    g_ada, dl, nm, nv = run(ada_grad_adamw, c_t, dmod_cols, w_ada[0], m_w_ada[0], v_w_ada[0])
    grads["w_ada"], deltas["w_ada"], new_m["w_ada"], new_v["w_ada"] = g_ada[None], dl[None], nm[None], nv[None]

    rs_finish("ffn_in", ["w_ffn_in"], sh_ffn_in)
    rs_finish("mix_out", ["w_mix_out"], sh_mix)
    rs_finish("attn_conv_out", ac_names, sh_ac)
    sh_in = rs_share_start("in", ["w_in"], cx_in)
    rs_finish("in", ["w_in"], sh_in)

    loss = loss_sum[0, 0]
    order = ["w_ada", "b_ada", "norm_mix_g", "w_in", "q_norm_g", "k_norm_g", "attn_sinks", "rel_bias", "w_attn_out",
             "conv_w", "conv_b", "conv_ln_g", "conv_ln_b", "w_conv_out", "w_mix_out", "norm_ffn_g", "w_ffn_in",
             "w_ffn_out"]
    return (loss, grad_x[None], *[grads[n] for n in order], *[deltas[n] for n in order],
            *[new_m[n] for n in order], *[new_v[n] for n in order])
```

```python
import functools
import math
from typing import Any, NamedTuple

import jax
import jax.numpy as jnp
import numpy as np
from jax import lax
from jax.experimental import pallas as pl
from jax.experimental.pallas import tpu as pltpu

F32 = jnp.float32
BF16 = jnp.bfloat16
MESH = pl.DeviceIdType.MESH

V7X_VMEM_BYTES = 64 * 1024 * 1024
VMEM_LIMIT = V7X_VMEM_BYTES - 8 * 1024 * 1024
LANES = 128
SUBLANES = 8
BF16_SUBLANES = 16

EPS = 1e-6
WINDOW = 128
BLOCK = 128
NUM_BUCKETS = 32
MAX_EXACT = NUM_BUCKETS // 2
MAX_DISTANCE = 128
CONV_WIDTH = 31
CONV_HALO = 32
ADAM_LR = 0.001
ADAM_B1 = 0.9
ADAM_B2 = 0.999
ADAM_EPS = 1e-08
ADAM_WD = 0.01
ADAM_STEP = 10
N_MOD = 6
SH_M, SC_M, GT_M, SH_F, SC_F, GT_F = range(6)

N_CHIPS = 4
N_DEV = 8

_ANY = pl.BlockSpec(memory_space=pl.ANY)
_VMEM = pl.BlockSpec(memory_space=pltpu.VMEM)
_SMEM = pl.BlockSpec(memory_space=pltpu.SMEM)
_HBM = pl.BlockSpec(memory_space=pltpu.HBM)
_SEM = pl.BlockSpec(memory_space=pltpu.SEMAPHORE)
_EFFECT = pltpu.SideEffectType.DATAFLOW_SIDE_EFFECTING


class InOrder:
    def __init__(self):
        self.token = None

    def __call__(self, fn, *args, **kw):
        return fn(*args, dep=self, **kw)


def _pallas(body, args, *, in_specs, out_specs, out_shape, name, dep=None, grid=(), n_prefetch=0, scratch=(),
            sem=None, **kw):
    n_lead = n_prefetch + len(in_specs)
    in_specs, args = list(in_specs), list(args)
    single = not isinstance(out_shape, (list, tuple))
    out_shapes = [out_shape] if single else list(out_shape)
    out_specs = [out_specs] if single else list(out_specs)
    if dep is not None:
        inner, n_out, takes = body, len(out_shapes), dep.token is not None

        def body(*refs):
            rest = refs[n_lead + (1 if takes else 0):]
            rest[n_out][...] = jnp.zeros((SUBLANES, LANES), F32)
            return inner(*refs[:n_lead], *rest[:n_out], *rest[n_out + 1:])

        if takes:
            in_specs.append(_ANY)
            args.append(dep.token)
        out_shapes.append(jax.ShapeDtypeStruct((SUBLANES, LANES), F32))
        out_specs.append(pl.BlockSpec((SUBLANES, LANES), lambda *_: (0, 0)))
    params = kw.pop("compiler_params", None)
    if params is None:
        params = pltpu.CompilerParams(dimension_semantics=sem, vmem_limit_bytes=VMEM_LIMIT)
    outs = pl.pallas_call(
        body,
        grid_spec=pltpu.PrefetchScalarGridSpec(num_scalar_prefetch=n_prefetch, grid=grid, in_specs=in_specs,
                                               out_specs=out_specs, scratch_shapes=list(scratch)),
        out_shape=out_shapes, compiler_params=params, name=name, **kw,
    )(*args)
    if dep is not None:
        dep.token = outs[-1]
        outs = outs[:-1]
    return outs[0] if single else list(outs)


def _tile(n, pref, unit=LANES):
    best = None
    for t in range(unit, min(n, pref) + 1, unit):
        if n % t == 0:
            best = t
    return best if best is not None else n


def _sigmoid(v):
    return 1.0 / (1.0 + jnp.exp(-v))


ROW_CHUNK = 512


def _row_chunks(m, unit=SUBLANES):
    step = _tile(m, ROW_CHUNK, unit)
    return [(s, step) for s in range(0, m, step)]


def _ew_tiles(r, n, unit=SUBLANES, elems=512 * 1024):
    return _tile(r, max(unit, elems // n), unit), n


def _block_pos(j, perm):
    if perm is None:
        return j
    pos = 0
    for a, p in enumerate(perm):
        pos = pos + jnp.where(j == a, p, 0)
    return pos


def mm_nn(a, w, *, tn, tk, out_dtype, name, perm=None, dep=None):
    m, k = a.shape
    j, k2, nj = w.shape
    assert k == k2 and nj % tn == 0 and k % tk == 0
    npj, nk = nj // tn, k // tk

    def body(a_ref, w_ref, o_ref, *scratch):
        kk = pl.program_id(1)
        for s, sz in _row_chunks(m):
            rows = pl.ds(s, sz)
            p = jnp.dot(a_ref[rows, :], w_ref[...], preferred_element_type=F32)
            if nk == 1:
                o_ref[rows, :] = p.astype(out_dtype)
            else:
                acc = scratch[0]

                @pl.when(kk == 0)
                def _():
                    acc[rows, :] = p

                @pl.when(kk > 0)
                def _():
                    acc[rows, :] += p

                @pl.when(kk == nk - 1)
                def _():
                    o_ref[rows, :] = acc[rows, :].astype(out_dtype)

    return _pallas(
        body, [a, w], dep=dep, grid=(j * npj, nk),
        in_specs=[
            pl.BlockSpec((m, tk), lambda n, kk: (0, kk)),
            pl.BlockSpec((None, tk, tn), lambda n, kk: (n // npj, kk, n % npj)),
        ],
        out_specs=pl.BlockSpec((m, tn), lambda n, kk: (0, _block_pos(n // npj, perm) * npj + n % npj)),
        out_shape=jax.ShapeDtypeStruct((m, j * nj), out_dtype),
        scratch=[pltpu.VMEM((m, tn), F32)] if nk > 1 else [],
        sem=("parallel", "arbitrary"), name=name)


def mm_nn_blocks(a, w, blocks, into, *, tn, out_dtype, name, dep=None):
    m, k = a.shape
    j, k2, nj = w.shape
    assert k == k2 and nj % tn == 0
    npj = nj // tn
    n_in = 2 if into is None else 3

    def body(blocks_ref, a_ref, w_ref, *rest):
        o_ref = rest[n_in - 2]
        for s, sz in _row_chunks(m):
            rows = pl.ds(s, sz)
            o_ref[rows, :] = jnp.dot(a_ref[rows, :], w_ref[...], preferred_element_type=F32).astype(out_dtype)

    return _pallas(
        body, [blocks, a, w] + ([] if into is None else [into]), dep=dep, n_prefetch=1,
        grid=(blocks.shape[0] * npj,),
        in_specs=[pl.BlockSpec((m, k), lambda n, bl: (0, 0)),
                  pl.BlockSpec((None, k, tn), lambda n, bl: (bl[n // npj], 0, n % npj))]
        + ([] if into is None else [_ANY]),
        out_specs=pl.BlockSpec((m, tn), lambda n, bl: (0, bl[n // npj] * npj + n % npj)),
        out_shape=jax.ShapeDtypeStruct((m, j * nj), out_dtype),
        input_output_aliases={} if into is None else {3: 0},
        sem=("arbitrary",), name=name)


def mm_nt(g, w, *, tko, tn, name, out_dtype=F32, perm=None, dep=None):
    m, n = g.shape
    j, k, nj = w.shape
    assert n == j * nj and nj % tn == 0 and k % tko == 0
    npj, nr = nj // tn, n // tn
    in_place = out_dtype == F32

    def body(g_ref, w_ref, o_ref, *scratch):
        r = pl.program_id(1)
        acc = o_ref if in_place else (scratch[0] if nr > 1 else None)
        for s, sz in _row_chunks(m):
            rows = pl.ds(s, sz)
            p = lax.dot_general(g_ref[rows, :], w_ref[...], (((1,), (1,)), ((), ())), preferred_element_type=F32)
            if acc is None:
                o_ref[rows, :] = p.astype(out_dtype)
                continue

            @pl.when(r == 0)
            def _():
                acc[rows, :] = p

            @pl.when(r > 0)
            def _():
                acc[rows, :] += p

            if not in_place:
                @pl.when(r == nr - 1)
                def _():
                    o_ref[rows, :] = acc[rows, :].astype(out_dtype)

    return _pallas(
        body, [g, w], dep=dep, grid=(k // tko, nr),
        in_specs=[
            pl.BlockSpec((m, tn), lambda ko, r: (0, _block_pos(r // npj, perm) * npj + r % npj)),
            pl.BlockSpec((None, tko, tn), lambda ko, r: (r // npj, ko, r % npj)),
        ],
        out_specs=pl.BlockSpec((m, tko), lambda ko, r: (0, ko)),
        out_shape=jax.ShapeDtypeStruct((m, k), out_dtype),
        scratch=[pltpu.VMEM((m, tko), F32)] if (nr > 1 and not in_place) else [],
        sem=("parallel", "arbitrary"), name=name)


def mm_tn(a, g, n_blocks, *, tk, tn, name, perm=None, dep=None):
    m, k = a.shape
    m2, n = g.shape
    nj = n // n_blocks
    assert m == m2 and nj % tn == 0 and k % tk == 0
    npj = nj // tn

    def body(a_ref, g_ref, o_ref):
        for s, sz in _row_chunks(tk, LANES):
            p = lax.dot_general(a_ref[:, pl.ds(s, sz)], g_ref[...], (((0,), (0,)), ((), ())),
                                preferred_element_type=F32)
            o_ref[pl.ds(s, sz), :] = p.astype(BF16)

    return _pallas(
        body, [a, g], dep=dep, grid=(k // tk, n // tn),
        in_specs=[
            pl.BlockSpec((m, tk), lambda kk, nn: (0, kk)),
            pl.BlockSpec((m, tn), lambda kk, nn: (0, _block_pos(nn // npj, perm) * npj + nn % npj)),
        ],
        out_specs=pl.BlockSpec((None, tk, tn), lambda kk, nn: (nn // npj, kk, nn % npj)),
        out_shape=jax.ShapeDtypeStruct((n_blocks, k, nj), BF16),
        sem=("parallel", "parallel"), name=name)


ROW_TILE = 256


def _row_spec(tr, width):
    return pl.BlockSpec((tr, width), lambda i: (i, 0))


def _full_spec(shape):
    return pl.BlockSpec(shape, lambda *_: (0,) * len(shape))


def _rms(xv):
    return lax.rsqrt(jnp.mean(xv * xv, axis=-1, keepdims=True) + EPS)


def _mod_row(mod_ref, row):
    return mod_ref[pl.ds(row, 1), :]


def pre_mix_fwd(x, mod, gain, dep=None):
    t, d = x.shape
    tr = _tile(t, ROW_TILE, SUBLANES)

    def body(x_ref, mod_ref, g_ref, h_ref):
        xv = x_ref[...]
        y = xv * _rms(xv) * g_ref[...]
        h_ref[...] = (y * (1.0 + _mod_row(mod_ref, SC_M)) + _mod_row(mod_ref, SH_M)).astype(BF16)

    return _pallas(
        body, [x, mod, gain], dep=dep, grid=(t // tr,),
        in_specs=[_row_spec(tr, d), _full_spec(mod.shape), _full_spec(gain.shape)],
        out_specs=_row_spec(tr, d),
        out_shape=jax.ShapeDtypeStruct((t, d), BF16),
        sem=("parallel",), name="pre_mix_fwd")


def pre_ffn_fwd(x, o_m, mod, gain, dep=None):
    t, d = x.shape
    tr = _tile(t, ROW_TILE, SUBLANES)

    def body(x_ref, om_ref, mod_ref, g_ref, x1_ref, h_ref):
        x1 = x_ref[...] + _mod_row(mod_ref, GT_M) * om_ref[...]
        x1_ref[...] = x1
        y = x1 * _rms(x1) * g_ref[...]
        h_ref[...] = (y * (1.0 + _mod_row(mod_ref, SC_F)) + _mod_row(mod_ref, SH_F)).astype(BF16)

    return _pallas(
        body, [x, o_m, mod, gain], dep=dep, grid=(t // tr,),
        in_specs=[_row_spec(tr, d), _row_spec(tr, d), _full_spec(mod.shape), _full_spec(gain.shape)],
        out_specs=[_row_spec(tr, d), _row_spec(tr, d)],
        out_shape=[jax.ShapeDtypeStruct((t, d), F32), jax.ShapeDtypeStruct((t, d), BF16)],
        sem=("parallel",), name="pre_ffn_fwd")


def loss_head(x1, o_f, target, mod, dep=None):
    t, d = x1.shape
    tr = _tile(t, ROW_TILE, SUBLANES)

    def body(x1_ref, of_ref, tg_ref, mod_ref, loss_ref, dy_ref, dof_ref, acc_ref):
        i = pl.program_id(0)
        gt = _mod_row(mod_ref, GT_F)
        of = of_ref[...]
        err = x1_ref[...] + gt * of - tg_ref[...]
        dy = err * (1.0 / d)
        dy_ref[...] = dy
        dof_ref[...] = (dy * gt).astype(BF16)
        part = (0.5 / d) * jnp.sum(jnp.sum(err * err, axis=1, keepdims=True), axis=0, keepdims=True)
        dgt = jnp.sum(dy * of, axis=0, keepdims=True)

        @pl.when(i == 0)
        def _():
            loss_ref[...] = jnp.zeros_like(loss_ref)
            acc_ref[...] = jnp.zeros_like(acc_ref)

        loss_ref[...] += part
        acc_ref[pl.ds(0, 1), :] += dgt

    return _pallas(
        body, [x1, o_f, target, mod], dep=dep, grid=(t // tr,),
        in_specs=[_row_spec(tr, d), _row_spec(tr, d), _row_spec(tr, d), _full_spec(mod.shape)],
        out_specs=[_full_spec((1, 1)), _row_spec(tr, d), _row_spec(tr, d), _full_spec((SUBLANES, d))],
        out_shape=[jax.ShapeDtypeStruct((1, 1), F32), jax.ShapeDtypeStruct((t, d), F32),
                   jax.ShapeDtypeStruct((t, d), BF16), jax.ShapeDtypeStruct((SUBLANES, d), F32)],
        sem=("arbitrary",), name="loss_head")


def _norm_bwd(xv, dh, sc, gain):
    rstd = _rms(xv)
    yn = xv * rstd
    dsh = jnp.sum(dh, axis=0, keepdims=True)
    dsc = jnp.sum(dh * (yn * gain), axis=0, keepdims=True)
    dgain = jnp.sum(dh * (1.0 + sc) * yn, axis=0, keepdims=True)
    dyn = dh * ((1.0 + sc) * gain)
    dx = rstd * (dyn - yn * jnp.mean(dyn * yn, axis=-1, keepdims=True))
    return dx, dsh, dsc, dgain


def pre_ffn_bwd(x1, dh2, dy, o_m, mod, gain, dep=None):
    t, d = x1.shape
    tr = _tile(t, ROW_TILE, SUBLANES)

    def body(x1_ref, dh_ref, dy_ref, om_ref, mod_ref, g_ref, dx1_ref, dom_ref, acc_ref):
        i = pl.program_id(0)
        dxn, dsh, dsc, dgain = _norm_bwd(x1_ref[...], dh_ref[...], _mod_row(mod_ref, SC_F), g_ref[...])
        dx1 = dy_ref[...] + dxn
        dx1_ref[...] = dx1
        dom_ref[...] = (dx1 * _mod_row(mod_ref, GT_M)).astype(BF16)
        dgt = jnp.sum(dx1 * om_ref[...], axis=0, keepdims=True)

        @pl.when(i == 0)
        def _():
            acc_ref[...] = jnp.zeros_like(acc_ref)

        acc_ref[pl.ds(0, 1), :] += dsh
        acc_ref[pl.ds(1, 1), :] += dsc
        acc_ref[pl.ds(2, 1), :] += dgain
        acc_ref[pl.ds(3, 1), :] += dgt

    return _pallas(
        body, [x1, dh2, dy, o_m, mod, gain], dep=dep, grid=(t // tr,),
        in_specs=[_row_spec(tr, d)] * 4 + [_full_spec(mod.shape), _full_spec(gain.shape)],
        out_specs=[_row_spec(tr, d), _row_spec(tr, d), _full_spec((SUBLANES, d))],
        out_shape=[jax.ShapeDtypeStruct((t, d), F32), jax.ShapeDtypeStruct((t, d), BF16),
                   jax.ShapeDtypeStruct((SUBLANES, d), F32)],
        sem=("arbitrary",), name="pre_ffn_bwd")


def pre_mix_bwd(x, dh, dx1, mod, gain, dep=None):
    t, d = x.shape
    tr = _tile(t, ROW_TILE, SUBLANES)

    def body(x_ref, dh_ref, dx1_ref, mod_ref, g_ref, gx_ref, acc_ref):
        i = pl.program_id(0)
        dxn, dsh, dsc, dgain = _norm_bwd(x_ref[...], dh_ref[...], _mod_row(mod_ref, SC_M), g_ref[...])
        gx_ref[...] = dx1_ref[...] + dxn

        @pl.when(i == 0)
        def _():
            acc_ref[...] = jnp.zeros_like(acc_ref)

        acc_ref[pl.ds(0, 1), :] += dsh
        acc_ref[pl.ds(1, 1), :] += dsc
        acc_ref[pl.ds(2, 1), :] += dgain

    return _pallas(
        body, [x, dh, dx1, mod, gain], dep=dep, grid=(t // tr,),
        in_specs=[_row_spec(tr, d)] * 3 + [_full_spec(mod.shape), _full_spec(gain.shape)],
        out_specs=[_row_spec(tr, d), _full_spec((SUBLANES, d))],
        out_shape=[jax.ShapeDtypeStruct((t, d), F32), jax.ShapeDtypeStruct((SUBLANES, d), F32)],
        sem=("arbitrary",), name="pre_mix_bwd")


def merge_fwd(p, y_attn, y_conv, off_ga, off_gc, dep=None):
    t, d = y_attn.shape
    tr = _tile(t, ROW_TILE, SUBLANES)
    cw = math.gcd(math.gcd(off_ga, off_gc), math.gcd(d, 512))
    nc = d // cw

    def body(ga_ref, gc_ref, ya_ref, yc_ref, o_ref):
        o_ref[...] = (_sigmoid(ga_ref[...]) * ya_ref[...] + _sigmoid(gc_ref[...]) * yc_ref[...]).astype(BF16)

    return _pallas(
        body, [p, p, y_attn, y_conv], dep=dep, grid=(t // tr, nc),
        in_specs=[pl.BlockSpec((tr, cw), lambda i, j: (i, off_ga // cw + j)),
                  pl.BlockSpec((tr, cw), lambda i, j: (i, off_gc // cw + j)),
                  pl.BlockSpec((tr, cw), lambda i, j: (i, j)),
                  pl.BlockSpec((tr, cw), lambda i, j: (i, j))],
        out_specs=pl.BlockSpec((tr, cw), lambda i, j: (i, j)),
        out_shape=jax.ShapeDtypeStruct((t, d), BF16),
        sem=("parallel", "parallel"), name="merge_fwd")


def merge_bwd(p, y_attn, y_conv, dmerged, off_ga, off_gc, dep=None):
    t, d = y_attn.shape
    tr = _tile(t, ROW_TILE, SUBLANES)
    cw = math.gcd(math.gcd(off_ga, off_gc), math.gcd(d, 512))
    nc = d // cw

    def body(ga_ref, gc_ref, ya_ref, yc_ref, dm_ref, dya_ref, dyc_ref, dga_ref, dgc_ref):
        dm = dm_ref[...]
        sa = _sigmoid(ga_ref[...])
        sc = _sigmoid(gc_ref[...])
        dya_ref[...] = (dm * sa).astype(BF16)
        dyc_ref[...] = (dm * sc).astype(BF16)
        dga_ref[...] = (dm * ya_ref[...] * sa * (1.0 - sa)).astype(BF16)
        dgc_ref[...] = (dm * yc_ref[...] * sc * (1.0 - sc)).astype(BF16)

    blk = pl.BlockSpec((tr, cw), lambda i, j: (i, j))
    return _pallas(
        body, [p, p, y_attn, y_conv, dmerged], dep=dep, grid=(t // tr, nc),
        in_specs=[pl.BlockSpec((tr, cw), lambda i, j: (i, off_ga // cw + j)),
                  pl.BlockSpec((tr, cw), lambda i, j: (i, off_gc // cw + j)), blk, blk, blk],
        out_specs=[blk] * 4,
        out_shape=[jax.ShapeDtypeStruct((t, d), BF16)] * 4,
        sem=("parallel", "parallel"), name="merge_bwd")


def ffn_perm(n_blocks):
    half = n_blocks // 2
    return tuple(2 * j if j < half else 2 * (j - half) + 1 for j in range(n_blocks))


def swiglu_fwd(f, nj, dep=None):
    t, two = f.shape
    tr = _tile(t, ROW_TILE, SUBLANES)
    npair = two // (2 * nj)

    def body(f_ref, o_ref):
        g = f_ref[:, :nj].astype(F32)
        u = f_ref[:, nj:].astype(F32)
        o_ref[...] = (g * _sigmoid(g) * u).astype(BF16)

    return _pallas(
        body, [f], dep=dep, grid=(t // tr, npair),
        in_specs=[pl.BlockSpec((tr, 2 * nj), lambda i, j: (i, j))],
        out_specs=pl.BlockSpec((tr, nj), lambda i, j: (i, j)),
        out_shape=jax.ShapeDtypeStruct((t, two // 2), BF16),
        sem=("parallel", "parallel"), name="swiglu_fwd")


def swiglu_bwd(f, dact, nj, dep=None):
    t, two = f.shape
    tr = _tile(t, ROW_TILE, SUBLANES)
    npair = two // (2 * nj)

    def body(f_ref, da_ref, o_ref):
        g = f_ref[:, :nj].astype(F32)
        u = f_ref[:, nj:].astype(F32)
        da = da_ref[...]
        s = _sigmoid(g)
        o_ref[:, :nj] = (da * u * (s * (1.0 + g * (1.0 - s)))).astype(BF16)
        o_ref[:, nj:] = (da * (g * s)).astype(BF16)

    return _pallas(
        body, [f, dact], dep=dep, grid=(t // tr, npair),
        in_specs=[pl.BlockSpec((tr, 2 * nj), lambda i, j: (i, j)), pl.BlockSpec((tr, nj), lambda i, j: (i, j))],
        out_specs=pl.BlockSpec((tr, 2 * nj), lambda i, j: (i, j)),
        out_shape=jax.ShapeDtypeStruct((t, two), BF16),
        sem=("parallel", "parallel"), name="swiglu_bwd")


def _t5_bucket_table():
    q_off = np.arange(BLOCK)
    k_off = np.arange(2 * BLOCK)
    dist = q_off[:, None] + BLOCK - k_off[None, :]
    n = np.maximum(dist, 0)
    nf = np.maximum(n, 1).astype(np.float32)
    large = MAX_EXACT + (np.log(nf / np.float32(MAX_EXACT)) / np.float32(math.log(MAX_DISTANCE / MAX_EXACT))
                         * np.float32(NUM_BUCKETS - MAX_EXACT)).astype(np.int32)
    large = np.minimum(large, NUM_BUCKETS - 1)
    bucket = np.where(n < MAX_EXACT, n, large).astype(np.int32)
    allowed = (dist >= 0) & (dist < WINDOW)
    return np.where(allowed, bucket, -1).astype(np.int32)


def bias_table(rel_bias, bucket_p, bucket_c, dep=None):
    nb, nq = rel_bias.shape

    def body(rb_ref, bkp_ref, bkc_ref, op_ref, oc_ref):
        for bk_ref, o_ref in ((bkp_ref, op_ref), (bkc_ref, oc_ref)):
            bk = bk_ref[...]
            for h in range(nq):
                acc = jnp.full(bk.shape, -jnp.inf, F32)
                for b in range(nb):
                    acc = jnp.where(bk == b, rb_ref[b, h], acc)
                o_ref[h] = acc

    return _pallas(
        body, [rel_bias, bucket_p, bucket_c], dep=dep,
        in_specs=[_SMEM, _VMEM, _VMEM], out_specs=[_VMEM, _VMEM],
        out_shape=[jax.ShapeDtypeStruct((nq,) + bucket_p.shape, F32)] * 2,
        name="bias_table")


def bias_table_bwd(dbp, dbc, bucket_p, bucket_c, dep=None):
    nq = dbp.shape[0]

    def body(dbp_ref, dbc_ref, bkp_ref, bkc_ref, o_ref):
        bkp, bkc = bkp_ref[...][None], bkc_ref[...][None]
        dp, dc = dbp_ref[...], dbc_ref[...]
        for b in range(NUM_BUCKETS):
            sel = jnp.where(bkp == b, dp, 0.0) + jnp.where(bkc == b, dc, 0.0)
            o_ref[b] = jnp.sum(jnp.sum(sel, axis=2, keepdims=True), axis=1, keepdims=True)

    return _pallas(
        body, [dbp, dbc, bucket_p, bucket_c], dep=dep,
        in_specs=[_VMEM] * 4, out_specs=_VMEM,
        out_shape=jax.ShapeDtypeStruct((NUM_BUCKETS, nq, 1, 1), F32),
        name="bias_table_bwd")


_NT = (((1,), (1,)), ((), ()))
_TN = (((0,), (0,)), ((), ()))


@jax.custom_vjp
def _bdot_nt(a, b):
    return lax.dot_general(a.astype(BF16), b.astype(BF16), _NT, preferred_element_type=F32)


def _bdot_nt_fwd(a, b):
    return _bdot_nt(a, b), (a, b)


def _bdot_nt_bwd(res, g):
    a, b = res
    gb = g.astype(BF16)
    da = jnp.dot(gb, b.astype(BF16), preferred_element_type=F32)
    db = lax.dot_general(gb, a.astype(BF16), _TN, preferred_element_type=F32)
    return da, db


_bdot_nt.defvjp(_bdot_nt_fwd, _bdot_nt_bwd)


@jax.custom_vjp
def _bdot_nn(a, b):
    return jnp.dot(a.astype(BF16), b.astype(BF16), preferred_element_type=F32)


def _bdot_nn_fwd(a, b):
    return _bdot_nn(a, b), (a, b)


def _bdot_nn_bwd(res, g):
    a, b = res
    gb = g.astype(BF16)
    da = lax.dot_general(gb, b.astype(BF16), _NT, preferred_element_type=F32)
    db = lax.dot_general(a.astype(BF16), gb, _TN, preferred_element_type=F32)
    return da, db


_bdot_nn.defvjp(_bdot_nn_fwd, _bdot_nn_bwd)


def _attn_math(q4, kp, kc, vp, vc, bp, bc, sink4, qg, kg, *, prev_ok, scale):
    g, b, hd = q4.shape
    q = q4.reshape(g * b, hd)
    qn = q * _rms(q) * qg
    kpn = kp * _rms(kp) * kg
    kcn = kc * _rms(kc) * kg
    lp = _bdot_nt(qn, kpn).reshape(g, b, b) * scale + bp
    lc = _bdot_nt(qn, kcn).reshape(g, b, b) * scale + bc
    lp = jnp.where(prev_ok, lp, -jnp.inf)
    m = jnp.maximum(jnp.maximum(jnp.max(lp, axis=-1, keepdims=True), jnp.max(lc, axis=-1, keepdims=True)), sink4)
    m = lax.stop_gradient(m)
    pp = jnp.exp(lp - m)
    pc = jnp.exp(lc - m)
    den = jnp.sum(pp, axis=-1, keepdims=True) + jnp.sum(pc, axis=-1, keepdims=True) + jnp.exp(sink4 - m)
    inv = 1.0 / den
    out = _bdot_nn((pp * inv).reshape(g * b, b), vp) + _bdot_nn((pc * inv).reshape(g * b, b), vc)
    return out.reshape(g, b, hd)


def _attn_specs(p, aw, kvw, nq, hd, nblk, reverse):
    assert aw % (2 * kvw) == 0
    kv_col = aw // (2 * kvw)

    def blk(n):
        return nblk - 1 - n if reverse else n

    return [
        pl.BlockSpec((BLOCK, aw), lambda n: (blk(n), 0)),
        pl.BlockSpec((BLOCK, 2 * kvw), lambda n: (jnp.maximum(blk(n) - 1, 0), kv_col)),
        pl.BlockSpec((BLOCK, 2 * kvw), lambda n: (blk(n), kv_col)),
        _full_spec((nq, BLOCK, BLOCK)), _full_spec((nq, BLOCK, BLOCK)), _full_spec((nq, 1, 1)),
        _full_spec((1, hd)), _full_spec((1, hd)),
    ]


def _attn_head_inputs(h, grp, hd, kvw, q_ref, kvp_ref, kvc_ref, bp_ref, bc_ref, s_ref):
    heads = pl.ds(grp * h, grp)
    q4 = jnp.stack([q_ref[:, pl.ds((grp * h + g) * hd, hd)] for g in range(grp)])
    k_cols, v_cols = pl.ds(h * hd, hd), pl.ds(kvw + h * hd, hd)
    return (q4, kvp_ref[:, k_cols], kvc_ref[:, k_cols], kvp_ref[:, v_cols], kvc_ref[:, v_cols],
            bp_ref[heads], bc_ref[heads], s_ref[heads])


def attn_fwd(p, bias_p, bias_c, sinks, qg, kg, *, aw, kvw, dep=None):
    t, hd = p.shape[0], qg.shape[-1]
    nq, nkv, nblk = aw // hd, kvw // hd, t // BLOCK
    grp = nq // nkv
    scale = hd ** -0.5

    def body(q_ref, kvp_ref, kvc_ref, bp_ref, bc_ref, s_ref, qg_ref, kg_ref, o_ref):
        prev_ok = pl.program_id(0) > 0
        for h in range(nkv):
            args = _attn_head_inputs(h, grp, hd, kvw, q_ref, kvp_ref, kvc_ref, bp_ref, bc_ref, s_ref)
            out = _attn_math(*args, qg_ref[...], kg_ref[...], prev_ok=prev_ok, scale=scale)
            for g in range(grp):
                o_ref[:, pl.ds((grp * h + g) * hd, hd)] = out[g].astype(BF16)

    return _pallas(
        body, [p, p, p, bias_p, bias_c, sinks, qg, kg], dep=dep, grid=(nblk,),
        in_specs=_attn_specs(p, aw, kvw, nq, hd, nblk, False),
        out_specs=pl.BlockSpec((BLOCK, aw), lambda n: (n, 0)),
        out_shape=jax.ShapeDtypeStruct((t, aw), BF16),
        sem=("parallel",), name="attn_fwd")


def attn_bwd(p, bias_p, bias_c, sinks, qg, kg, do, *, aw, kvw, dep=None):
    t, hd = p.shape[0], qg.shape[-1]
    nq, nkv, nblk = aw // hd, kvw // hd, t // BLOCK
    grp = nq // nkv
    scale = hd ** -0.5

    def body(q_ref, kvp_ref, kvc_ref, bp_ref, bc_ref, s_ref, qg_ref, kg_ref, do_ref,
             dqkv_ref, dbp_ref, dbc_ref, ds_ref, dqg_ref, dkg_ref, carry):
        i = pl.program_id(0)
        prev_ok = (nblk - 1 - i) > 0

        @pl.when(i == 0)
        def _():
            carry[...] = jnp.zeros_like(carry)
            dbp_ref[...] = jnp.zeros_like(dbp_ref)
            dbc_ref[...] = jnp.zeros_like(dbc_ref)
            ds_ref[...] = jnp.zeros_like(ds_ref)
            dqg_ref[...] = jnp.zeros_like(dqg_ref)
            dkg_ref[...] = jnp.zeros_like(dkg_ref)

        fn = functools.partial(_attn_math, prev_ok=prev_ok, scale=scale)
        for h in range(nkv):
            args = _attn_head_inputs(h, grp, hd, kvw, q_ref, kvp_ref, kvc_ref, bp_ref, bc_ref, s_ref)
            _, vjp = jax.vjp(fn, *args, qg_ref[...], kg_ref[...])
            do4 = jnp.stack([do_ref[:, pl.ds((grp * h + g) * hd, hd)].astype(F32) for g in range(grp)])
            dq, dkp, dkc, dvp, dvc, dbp, dbc, dsk, dqg, dkg = vjp(do4)
            for g in range(grp):
                dqkv_ref[:, pl.ds((grp * h + g) * hd, hd)] = dq[g].astype(BF16)
            k_cols, v_cols = pl.ds(h * hd, hd), pl.ds(kvw + h * hd, hd)
            dqkv_ref[:, pl.ds(aw + h * hd, hd)] = (dkc + carry[:, k_cols]).astype(BF16)
            dqkv_ref[:, pl.ds(aw + kvw + h * hd, hd)] = (dvc + carry[:, v_cols]).astype(BF16)
            carry[:, k_cols] = dkp
            carry[:, v_cols] = dvp
            heads = pl.ds(grp * h, grp)
            dbp_ref[heads] += dbp
            dbc_ref[heads] += dbc
            ds_ref[heads] += dsk
            dqg_ref[...] += dqg
            dkg_ref[...] += dkg

    return _pallas(
        body, [p, p, p, bias_p, bias_c, sinks, qg, kg, do], dep=dep, grid=(nblk,),
        in_specs=_attn_specs(p, aw, kvw, nq, hd, nblk, True)
        + [pl.BlockSpec((BLOCK, aw), lambda n: (nblk - 1 - n, 0))],
        out_specs=[
            pl.BlockSpec((BLOCK, aw + 2 * kvw), lambda n: (nblk - 1 - n, 0)),
            _full_spec((nq, BLOCK, BLOCK)), _full_spec((nq, BLOCK, BLOCK)), _full_spec((nq, 1, 1)),
            _full_spec((1, hd)), _full_spec((1, hd)),
        ],
        out_shape=[
            jax.ShapeDtypeStruct((t, aw + 2 * kvw), BF16),
            jax.ShapeDtypeStruct((nq, BLOCK, BLOCK), F32),
            jax.ShapeDtypeStruct((nq, BLOCK, BLOCK), F32),
            jax.ShapeDtypeStruct((nq, 1, 1), F32),
            jax.ShapeDtypeStruct((1, hd), F32),
            jax.ShapeDtypeStruct((1, hd), F32),
        ],
        scratch=[pltpu.VMEM((BLOCK, 2 * kvw), F32)],
        sem=("arbitrary",), name="attn_bwd")


CONV_TILE = 256


def _conv_halo_specs(tb, ch, nblk):
    per = tb // CONV_HALO
    last = nblk * per - 1
    cur = pl.BlockSpec((tb, ch), lambda n: (n, 0))
    prev = pl.BlockSpec((CONV_HALO, ch), lambda n: (jnp.maximum(n * per - 1, 0), 0))
    nxt = pl.BlockSpec((CONV_HALO, ch), lambda n: (jnp.minimum((n + 1) * per, last), 0))
    return cur, prev, nxt


def _ln_silu(co, ln_g, ln_b):
    mu = jnp.mean(co, axis=-1, keepdims=True)
    cen = co - mu
    rstd = lax.rsqrt(jnp.mean(cen * cen, axis=-1, keepdims=True) + EPS)
    xhat = cen * rstd
    z = xhat * ln_g + ln_b
    return xhat, rstd, z


def _shifted_copies(src, shifted):
    rows = src.shape[0] - SUBLANES
    for r in range(1, SUBLANES):
        shifted[r, pl.ds(0, rows), :] = src[pl.ds(r, rows), :]


def _rows_from(src, shifted, start, n):
    r = start % SUBLANES
    if r == 0:
        return src[pl.ds(start, n), :]
    return shifted[r, pl.ds(start - r, n), :]


def conv_fwd(ca, cb, conv_w, conv_b, ln_g, ln_b, dep=None):
    t, ch = ca.shape
    tb = _tile(t, CONV_TILE, CONV_HALO)
    nblk = t // tb
    cur, prev, _ = _conv_halo_specs(tb, ch, nblk)
    lead = CONV_HALO - (CONV_WIDTH - 1)

    def body(ca_ref, cb_ref, cap_ref, cbp_ref, w_ref, b_ref, g_ref, bb_ref, s_ref, co_ref, ubuf, ushift):
        n = pl.program_id(0)
        halo = cap_ref[...] * _sigmoid(cbp_ref[...])
        ubuf[pl.ds(0, CONV_HALO), :] = jnp.where(n > 0, halo, 0.0)
        ubuf[pl.ds(CONV_HALO, tb), :] = ca_ref[...] * _sigmoid(cb_ref[...])
        _shifted_copies(ubuf, ushift)
        acc = jnp.broadcast_to(b_ref[...], (tb, ch))
        for k in range(CONV_WIDTH):
            acc = acc + w_ref[pl.ds(k, 1), :] * _rows_from(ubuf, ushift, lead + k, tb)
        co_ref[...] = acc
        _, _, z = _ln_silu(acc, g_ref[...], bb_ref[...])
        s_ref[...] = (z * _sigmoid(z)).astype(BF16)

    vec = _full_spec((1, ch))
    return _pallas(
        body, [ca, cb, ca, cb, conv_w, conv_b, ln_g, ln_b], dep=dep, grid=(nblk,),
        in_specs=[cur, cur, prev, prev, _full_spec(conv_w.shape), vec, vec, vec],
        out_specs=[cur, cur],
        out_shape=[jax.ShapeDtypeStruct((t, ch), BF16), jax.ShapeDtypeStruct((t, ch), F32)],
        scratch=[pltpu.VMEM((CONV_HALO + tb, ch), F32), pltpu.VMEM((SUBLANES, CONV_HALO + tb, ch), F32)],
        sem=("parallel",), name="conv_fwd")


def conv_bwd(ca, cb, co, ds, conv_w, ln_g, ln_b, dep=None):
    t, ch = ca.shape
    tb = _tile(t, CONV_TILE, CONV_HALO)
    nblk = t // tb
    cur, prev, nxt = _conv_halo_specs(tb, ch, nblk)
    lead = CONV_HALO - (CONV_WIDTH - 1)
    ext = tb + CONV_HALO

    def body(ca_ref, cb_ref, cap_ref, cbp_ref, co_ref, con_ref, ds_ref, dsn_ref, w_ref, g_ref, bb_ref,
             dca_ref, dcb_ref, dw_ref, dvec_ref, ubuf, dbuf, ushift, dshift):
        n = pl.program_id(0)
        is_last = n == nblk - 1
        sig_b = _sigmoid(cb_ref[...])
        cav = ca_ref[...]
        ubuf[pl.ds(0, CONV_HALO), :] = jnp.where(n > 0, cap_ref[...] * _sigmoid(cbp_ref[...]), 0.0)
        ubuf[pl.ds(CONV_HALO, tb), :] = cav * sig_b
        _shifted_copies(ubuf, ushift)
        co = jnp.concatenate([co_ref[...], con_ref[...]], axis=0)
        xhat, rstd, z = _ln_silu(co, g_ref[...], bb_ref[...])
        dsv = jnp.concatenate([ds_ref[...].astype(F32), jnp.where(is_last, 0.0, dsn_ref[...].astype(F32))], axis=0)
        sg = _sigmoid(z)
        dz = dsv * (sg * (1.0 + z * (1.0 - sg)))
        dxh = dz * g_ref[...]
        dco = rstd * (dxh - jnp.mean(dxh, axis=-1, keepdims=True)
                      - xhat * jnp.mean(dxh * xhat, axis=-1, keepdims=True))
        dbuf[...] = dco
        _shifted_copies(dbuf, dshift)

        @pl.when(n == 0)
        def _():
            dw_ref[...] = jnp.zeros_like(dw_ref)
            dvec_ref[...] = jnp.zeros_like(dvec_ref)

        dco_cur = dco[:tb]
        dvec_ref[pl.ds(0, 1), :] += jnp.sum(dco_cur, axis=0, keepdims=True)
        dvec_ref[pl.ds(1, 1), :] += jnp.sum(dz[:tb] * xhat[:tb], axis=0, keepdims=True)
        dvec_ref[pl.ds(2, 1), :] += jnp.sum(dz[:tb], axis=0, keepdims=True)
        du = jnp.zeros((tb, ch), F32)
        for k in range(CONV_WIDTH):
            du = du + w_ref[pl.ds(k, 1), :] * _rows_from(dbuf, dshift, CONV_WIDTH - 1 - k, tb)
            dw_ref[pl.ds(k, 1), :] += jnp.sum(dco_cur * _rows_from(ubuf, ushift, lead + k, tb), axis=0,
                                              keepdims=True)
        dca_ref[...] = (du * sig_b).astype(BF16)
        dcb_ref[...] = (du * cav * sig_b * (1.0 - sig_b)).astype(BF16)

    vec = _full_spec((1, ch))
    return _pallas(
        body, [ca, cb, ca, cb, co, co, ds, ds, conv_w, ln_g, ln_b], dep=dep, grid=(nblk,),
        in_specs=[cur, cur, prev, prev, cur, nxt, cur, nxt, _full_spec(conv_w.shape), vec, vec],
        out_specs=[cur, cur, _full_spec(conv_w.shape), _full_spec((SUBLANES, ch))],
        out_shape=[jax.ShapeDtypeStruct((t, ch), BF16), jax.ShapeDtypeStruct((t, ch), BF16),
                   jax.ShapeDtypeStruct(conv_w.shape, F32), jax.ShapeDtypeStruct((SUBLANES, ch), F32)],
        scratch=[pltpu.VMEM((CONV_HALO + tb, ch), F32), pltpu.VMEM((ext, ch), F32),
                 pltpu.VMEM((SUBLANES, CONV_HALO + tb, ch), F32), pltpu.VMEM((SUBLANES, ext, ch), F32)],
        sem=("arbitrary",), name="conv_bwd")


def ada_fwd(c_t, w_ada, dep=None):
    d, nc = w_ada.shape
    nex = c_t.shape[1]
    tn = _tile(nc, 512)

    def body(ct_ref, w_ref, o_ref):
        w = w_ref[...]
        ct = ct_ref[...]
        cact = ct * _sigmoid(ct)
        rows = [jnp.sum(w * cact[:, b:b + 1], axis=0, keepdims=True) for b in range(nex)]
        o_ref[...] = jnp.concatenate(rows, axis=0)

    return _pallas(
        body, [c_t, w_ada], dep=dep, grid=(nc // tn,),
        in_specs=[_full_spec(c_t.shape), pl.BlockSpec((d, tn), lambda j: (0, j))],
        out_specs=pl.BlockSpec((nex, tn), lambda j: (0, j)),
        out_shape=jax.ShapeDtypeStruct((nex, nc), F32),
        sem=("parallel",), name="ada_fwd")


def _adamw_math(w, g, m, v):
    m = ADAM_B1 * m + (1.0 - ADAM_B1) * g
    v = ADAM_B2 * v + (1.0 - ADAM_B2) * (g * g)
    m_hat = m / (1.0 - ADAM_B1 ** ADAM_STEP)
    v_hat = v / (1.0 - ADAM_B2 ** ADAM_STEP)
    delta = -ADAM_LR * (m_hat / (jnp.sqrt(v_hat) + ADAM_EPS) + ADAM_WD * w)
    return delta, m, v


def adamw(w, g, m, v, name, copy_grad=False, dep=None):
    r, n = w.shape
    tr, tn = _ew_tiles(r, n, elems=256 * 1024)
    n_out = 4 if copy_grad else 3

    def body(w_ref, g_ref, m_ref, v_ref, *outs):
        g = g_ref[...]
        if copy_grad:
            outs[0][...] = g
        outs[-3][...], outs[-2][...], outs[-1][...] = _adamw_math(w_ref[...], g, m_ref[...], v_ref[...])

    blk = pl.BlockSpec((tr, tn), lambda i, j: (i, j))
    return _pallas(
        body, [w, g, m, v], dep=dep, grid=(r // tr, n // tn),
        in_specs=[blk] * 4, out_specs=[blk] * n_out,
        out_shape=[jax.ShapeDtypeStruct((r, n), F32)] * n_out,
        sem=("parallel", "parallel"), name=name)


def ada_grad_adamw(c_t, dmod_cols, w, m, v, dep=None):
    d, nc = w.shape
    nex = c_t.shape[1]
    tr, tn = _ew_tiles(d, nc, elems=256 * 1024)

    def body(ct_ref, dm_ref, w_ref, m_ref, v_ref, g_ref, d_ref, nm_ref, nv_ref):
        ct = ct_ref[...]
        cact = ct * _sigmoid(ct)
        dm = dm_ref[...]
        g = cact[:, 0:1] * dm[0:1, :]
        for b in range(1, nex):
            g = g + cact[:, b:b + 1] * dm[b:b + 1, :]
        g_ref[...] = g
        d_ref[...], nm_ref[...], nv_ref[...] = _adamw_math(w_ref[...], g, m_ref[...], v_ref[...])

    blk = pl.BlockSpec((tr, tn), lambda i, j: (i, j))
    return _pallas(
        body, [c_t, dmod_cols, w, m, v], dep=dep, grid=(d // tr, nc // tn),
        in_specs=[pl.BlockSpec((tr, nex), lambda i, j: (i, 0)), pl.BlockSpec((nex, tn), lambda i, j: (0, j)),
                  blk, blk, blk],
        out_specs=[blk] * 4,
        out_shape=[jax.ShapeDtypeStruct((d, nc), F32)] * 4,
        sem=("parallel", "parallel"), name="ada_grad_adamw")


def _row_pack(parts):
    cols, offs, off = [], [], 0
    for p in parts:
        n = p.shape[1]
        width = -(-n // LANES) * LANES
        cols.append(jnp.pad(p, ((0, 0), (0, width - n))) if width != n else p)
        offs.append(off)
        off += width
    return jnp.concatenate(cols, axis=1), offs


def small_sum_adamw(gathered, offs, ws, ms, vs, extra_widths, dep=None):
    ndev = gathered.shape[0]
    npar = len(ws)

    def body(ga_ref, *refs):
        w_refs, m_refs, v_refs = refs[:npar], refs[npar:2 * npar], refs[2 * npar:3 * npar]
        outs = refs[3 * npar:]
        tot = ga_ref[0]
        for s in range(1, ndev):
            tot = tot + ga_ref[s]
        for i in range(npar):
            n = ws[i].shape[1]
            g = tot[:, offs[i]:offs[i] + n]
            outs[4 * i][...] = g
            outs[4 * i + 1][...], outs[4 * i + 2][...], outs[4 * i + 3][...] = _adamw_math(
                w_refs[i][...], g, m_refs[i][...], v_refs[i][...])
        for e, n in enumerate(extra_widths):
            off = offs[npar + e]
            outs[4 * npar + e][...] = tot[:, off:off + n]

    shapes = [jax.ShapeDtypeStruct(w.shape, F32) for w in ws for _ in range(4)]
    shapes += [jax.ShapeDtypeStruct((1, n), F32) for n in extra_widths]
    return _pallas(
        body, [gathered, *ws, *ms, *vs], dep=dep, in_specs=[_VMEM] * (1 + 3 * npar), out_specs=[_VMEM] * len(shapes),
        out_shape=shapes, name="small_sum_adamw")


def _position():
    return lax.axis_index("x"), lax.axis_index("y"), lax.axis_index("c")


def _other_chips(x, y):
    return [(1 - x, y), (x, 1 - y), (1 - x, 1 - y)]


def allgather_small(block, name, dep=None):
    def body(x_ref, out_ref, send_sems, recv_sems, local_sem):
        x, y, c = _position()
        me, sibling = (x, y, c), (x, y, 1 - c)
        chips = _other_chips(x, y)

        def slot(px, py, pc):
            return out_ref.at[4 * px + 2 * py + pc]

        def copy(k, block_of, to, src=None):
            return pltpu.make_async_remote_copy(
                src_ref=slot(*block_of) if src is None else src, dst_ref=slot(*block_of),
                send_sem=send_sems.at[k], recv_sem=recv_sems.at[k], device_id=to, device_id_type=MESH)

        mine = pltpu.make_async_copy(x_ref, slot(*me), local_sem)
        mine.start()
        first = [copy(0, me, sibling, src=x_ref)]
        first += [copy(1 + j, me, (*chip, c), src=x_ref) for j, chip in enumerate(chips)]
        for cp in first:
            cp.start()
        passed = [copy(4 + j, (*chip, c), sibling) for j, chip in enumerate(chips)]
        for j, chip in enumerate(chips):
            copy(1 + j, (*chip, c), me).wait_recv()
            passed[j].start()
        copy(0, sibling, me).wait_recv()
        for j, chip in enumerate(chips):
            copy(4 + j, (*chip, 1 - c), me).wait_recv()
        for cp in first + passed:
            cp.wait_send()
        mine.wait()

    return _pallas(
        body, [block], dep=dep,
        out_shape=jax.ShapeDtypeStruct((N_DEV, *block.shape), block.dtype),
        in_specs=[_VMEM], out_specs=_VMEM,
        scratch=[pltpu.SemaphoreType.DMA((7,)), pltpu.SemaphoreType.DMA((7,)), pltpu.SemaphoreType.DMA],
        name=name)


class Started(NamedTuple):
    send_sems: Any
    recv_sems: Any
    bufs: list


def exchange_start(name, bufs, n_copies, plan, dep=None):
    nb = len(bufs)

    def body(*refs):
        for cp in plan(refs[:nb], refs[nb], refs[nb + 1]):
            cp.start()

    outs = _pallas(
        body, [pltpu.with_memory_space_constraint(b, pltpu.HBM) for b in bufs], dep=dep, name=name,
        out_shape=(pltpu.SemaphoreType.DMA((n_copies,)), pltpu.SemaphoreType.DMA((n_copies,)),
                   *[pltpu.HBM(b.shape, b.dtype) for b in bufs]),
        in_specs=[_HBM] * nb,
        out_specs=(_SEM, _SEM, *[_HBM] * nb),
        input_output_aliases={i: 2 + i for i in range(nb)},
        compiler_params=pltpu.CompilerParams(has_side_effects=_EFFECT))
    return Started(outs[0], outs[1], list(outs[2:2 + nb]))


def exchange_wait(name, started, plan, bufs=None, dep=None):
    if bufs is not None:
        started = started._replace(bufs=list(bufs))
    nb = len(started.bufs)

    def body(*refs):
        for cp in plan(refs[:nb], refs[nb], refs[nb + 1]):
            cp.wait_send()
            cp.wait_recv()

    outs = _pallas(
        body, [*started.bufs, started.send_sems, started.recv_sems], dep=dep, name=name,
        out_shape=tuple(pltpu.HBM(b.shape, b.dtype) for b in started.bufs),
        in_specs=[_HBM] * nb + [_SEM, _SEM],
        out_specs=tuple([_HBM] * nb),
        input_output_aliases={i: i for i in range(nb)},
        compiler_params=pltpu.CompilerParams(has_side_effects=_EFFECT))
    return list(outs)


def _remote(src, dst, send_sems, recv_sems, i, to):
    return pltpu.make_async_remote_copy(src_ref=src, dst_ref=dst, send_sem=send_sems.at[i], recv_sem=recv_sems.at[i],
                                        device_id=to, device_id_type=MESH)


def _half_rows(buf_rows, chip_idx, pc):
    half = buf_rows // (2 * N_CHIPS)
    return pl.ds((2 * chip_idx + pc) * half, half)


ALL_PEERS = (0, 1, 2)


def plan_gather_ici(refs, send_sems, recv_sems, peers=ALL_PEERS):
    x, y, c = _position()
    chips = _other_chips(x, y)
    copies = []
    for k, ref in enumerate(refs):
        rows = ref.at[_half_rows(ref.shape[0], 2 * x + y, c), :]
        for i, j in enumerate(peers):
            copies.append(_remote(rows, rows, send_sems, recv_sems, len(peers) * k + i, (*chips[j], c)))
    return copies


def plan_gather_d2d(refs, send_sems, recv_sems, peers=ALL_PEERS):
    x, y, c = _position()
    chips = _other_chips(x, y)
    copies = []
    for k, ref in enumerate(refs):
        for i, j in enumerate(peers):
            px, py = chips[j]
            rows = ref.at[_half_rows(ref.shape[0], 2 * px + py, c), :]
            copies.append(_remote(rows, rows, send_sems, recv_sems, len(peers) * k + i, (x, y, 1 - c)))
    return copies


def plan_pair_exchange(refs, send_sems, recv_sems):
    x, y, c = _position()
    nw = len(refs) // 2
    copies = []
    for k in range(nw):
        for chip in range(N_CHIPS):
            copies.append(_remote(refs[k].at[chip, 1 - c], refs[nw + k].at[chip], send_sems, recv_sems,
                                  N_CHIPS * k + chip, (x, y, 1 - c)))
    return copies


def plan_chip_exchange(refs, send_sems, recv_sems):
    x, y, c = _position()
    nw = len(refs) // 2
    copies = []
    for k in range(nw):
        for j, (px, py) in enumerate(_other_chips(x, y)):
            copies.append(_remote(refs[k].at[2 * px + py], refs[nw + k].at[2 * x + y], send_sems, recv_sems,
                                  3 * k + j, (px, py, c)))
    return copies


def plan_pair_share(refs, send_sems, recv_sems):
    x, y, c = _position()
    return [_remote(ref.at[c], ref.at[c], send_sems, recv_sems, k, (x, y, 1 - c)) for k, ref in enumerate(refs)]


def cast_into_slot(src, slot, n_slots, name, dep=None):
    r, n = src.shape
    tr, tn = _ew_tiles(r, n, BF16_SUBLANES)

    def body(slot_ref, s_ref, o_ref):
        o_ref[...] = s_ref[...].astype(BF16)

    return _pallas(
        body, [slot, src], dep=dep, n_prefetch=1, grid=(r // tr, n // tn),
        in_specs=[pl.BlockSpec((tr, tn), lambda i, j, sl: (i, j))],
        out_specs=pl.BlockSpec((None, tr, tn), lambda i, j, sl: (sl[0], i, j)),
        out_shape=jax.ShapeDtypeStruct((n_slots, r, n), BF16),
        sem=("parallel", "parallel"), name=name)


def pair_sum(g, r, core, name, dep=None):
    nchip, _, h, n = g.shape
    th, tn = _ew_tiles(h, n, BF16_SUBLANES)

    def body(core_ref, g_ref, r_ref, o_ref):
        o_ref[...] = (g_ref[...].astype(F32) + r_ref[...].astype(F32)).astype(BF16)

    return _pallas(
        body, [core, g, r], dep=dep, n_prefetch=1, grid=(nchip, h // th, n // tn),
        in_specs=[pl.BlockSpec((None, None, th, tn), lambda a, i, j, cr: (a, cr[0], i, j)),
                  pl.BlockSpec((None, th, tn), lambda a, i, j, cr: (a, i, j))],
        out_specs=pl.BlockSpec((None, th, tn), lambda a, i, j, cr: (a, i, j)),
        out_shape=jax.ShapeDtypeStruct((nchip, h, n), BF16),
        sem=("parallel", "parallel", "parallel"), name=name)


def chip_sum(own, got, where, name, dep=None):
    nchip, h, n = got.shape
    th, tn = _ew_tiles(h, n, BF16_SUBLANES, elems=256 * 1024)

    def body(where_ref, own_ref, *rest):
        got_refs, o_ref = rest[:nchip], rest[nchip]
        chip = where_ref[0]
        acc = None
        for s in range(nchip):
            term = jnp.where(chip == s, own_ref[...], got_refs[s][...]).astype(F32)
            acc = term if acc is None else acc + term
        o_ref[...] = acc

    def got_spec(s):
        return pl.BlockSpec((None, th, tn), lambda i, j, wr: (jnp.where(wr[0] == s, (s + 1) % nchip, s), i, j))

    return _pallas(
        body, [where, own, *[got] * nchip], dep=dep, n_prefetch=1, grid=(h // th, n // tn),
        in_specs=[pl.BlockSpec((None, th, tn), lambda i, j, wr: (wr[0], i, j))]
        + [got_spec(s) for s in range(nchip)],
        out_specs=pl.BlockSpec((None, th, tn), lambda i, j, wr: (wr[1], i, j)),
        out_shape=jax.ShapeDtypeStruct((2, h, n), F32),
        sem=("parallel", "parallel"), name=name)


def kernel(x, c, w_ada, b_ada, norm_mix_g, w_in, q_norm_g, k_norm_g, attn_sinks, rel_bias, w_attn_out, conv_w, conv_b, conv_ln_g, conv_ln_b, w_conv_out, w_mix_out, norm_ffn_g, w_ffn_in, w_ffn_out, loss_target, m_w_ada, m_b_ada, m_norm_mix_g, m_w_in, m_q_norm_g, m_k_norm_g, m_attn_sinks, m_rel_bias, m_w_attn_out, m_conv_w, m_conv_b, m_conv_ln_g, m_conv_ln_b, m_w_conv_out, m_w_mix_out, m_norm_ffn_g, m_w_ffn_in, m_w_ffn_out, v_w_ada, v_b_ada, v_norm_mix_g, v_w_in, v_q_norm_g, v_k_norm_g, v_attn_sinks, v_rel_bias, v_w_attn_out, v_conv_w, v_conv_b, v_conv_ln_g, v_conv_ln_b, v_w_conv_out, v_w_mix_out, v_norm_ffn_g, v_w_ffn_in, v_w_ffn_out):
    run = InOrder()
    xi, yi, ci = _position()
    chip = 2 * xi + yi
    me = 2 * chip + ci
    chip_arr = chip.astype(jnp.int32).reshape(1)
    core_arr = ci.astype(jnp.int32).reshape(1)
    where_arr = jnp.stack([chip, ci]).astype(jnp.int32)

    xe, tgt = x[0], loss_target[0]
    t, d = xe.shape
    hd = q_norm_g.shape[-1]
    nq = attn_sinks.shape[-1]
    aw = nq * hd
    ch = conv_b.shape[-1]
    in_width = N_CHIPS * w_in.shape[-1]
    kvw = (in_width - aw - 2 * ch - 2 * d) // 2
    nkv = kvw // hd
    dff = N_CHIPS * w_ffn_out.shape[1]
    off_k, off_v, off_ca = aw, aw + kvw, aw + 2 * kvw
    off_cb, off_ga, off_gc = off_ca + ch, off_ca + 2 * ch, off_ca + 2 * ch + d
    nc_ada = w_ada.shape[-1]
    ch_loc = conv_w.shape[-1]
    nj_ffn = w_ffn_in.shape[-1]
    perm_ffn = ffn_perm(N_CHIPS)

    big = {"w_in": w_in[0], "w_attn_out": w_attn_out[0], "w_conv_out": w_conv_out[0], "w_mix_out": w_mix_out[0],
           "w_ffn_in": w_ffn_in[0], "w_ffn_out": w_ffn_out[0]}
    moments = {"w_in": (m_w_in, v_w_in), "w_attn_out": (m_w_attn_out, v_w_attn_out),
               "w_conv_out": (m_w_conv_out, v_w_conv_out), "w_mix_out": (m_w_mix_out, v_w_mix_out),
               "w_ffn_in": (m_w_ffn_in, v_w_ffn_in), "w_ffn_out": (m_w_ffn_out, v_w_ffn_out)}
    gather_groups = {"in": ["w_in"], "mid": ["w_attn_out", "w_conv_out", "w_mix_out"], "ffn_in": ["w_ffn_in"],
                     "ffn_out": ["w_ffn_out"]}
    grads, deltas, new_m, new_v = {}, {}, {}, {}

    def gather_cast(gname):
        bufs = []
        for n in gather_groups[gname]:
            r, ncol = big[n].shape
            bufs.append(run(cast_into_slot, big[n], chip_arr, N_CHIPS, "cast_" + n).reshape(N_CHIPS * r, ncol))
        return bufs

    def gather_ici_start(gname, bufs):
        return run(exchange_start, "gather_ici_start_" + gname, bufs, 3 * len(bufs), plan_gather_ici)

    def gather_pass_on(gname, ici):
        landed = run(exchange_wait, "gather_ici_wait_" + gname, ici, plan_gather_ici)
        return run(exchange_start, "gather_d2d_start_" + gname, landed, 3 * len(landed), plan_gather_d2d)

    def gathered(gname, d2d):
        outs = run(exchange_wait, "gather_d2d_wait_" + gname, d2d, plan_gather_d2d)
        return [o.reshape(N_CHIPS, *big[n].shape) for o, n in zip(outs, gather_groups[gname])]

    def rs_pair_start(gname, names, partials):
        blocks = [g.reshape(N_CHIPS, 2, big[n].shape[0] // 2, big[n].shape[1]) for n, g in zip(names, partials)]
        land = [lax.empty((N_CHIPS,) + b.shape[2:], BF16) for b in blocks]
        return run(exchange_start, "pair_exchange_start_" + gname, blocks + land, N_CHIPS * len(blocks),
                   plan_pair_exchange)

    def rs_chip_start(gname, names, pair):
        nw = len(names)
        outs = run(exchange_wait, "pair_exchange_wait_" + gname, pair, plan_pair_exchange)
        sums = [run(pair_sum, g, r, core_arr, "pair_sum_" + n) for n, g, r in zip(names, outs[:nw], outs[nw:])]
        land = [lax.empty(s.shape, BF16) for s in sums]
        return run(exchange_start, "chip_exchange_start_" + gname, sums + land, 3 * nw, plan_chip_exchange)

    def rs_share_start(gname, names, chipx):
        nw = len(names)
        outs = run(exchange_wait, "chip_exchange_wait_" + gname, chipx, plan_chip_exchange)
        halves = [run(chip_sum, s, r, where_arr, "chip_sum_" + n) for n, s, r in zip(names, outs[:nw], outs[nw:])]
        return run(exchange_start, "pair_share_start_" + gname, halves, nw, plan_pair_share)

    def rs_finish(gname, names, share):
        fulls = run(exchange_wait, "pair_share_wait_" + gname, share, plan_pair_share)
        for n, g2 in zip(names, fulls):
            g, dl, nm, nv = run(adamw, big[n], g2.reshape(big[n].shape), moments[n][0][0], moments[n][1][0],
                                "adamw_" + n, copy_grad=True)
            grads[n], deltas[n], new_m[n], new_v[n] = g[None], dl[None], nm[None], nv[None]

    bufs_in = gather_cast("in")
    row1, offs1 = _row_pack([c, conv_w[0].reshape(1, CONV_WIDTH * ch_loc)])
    got1 = run(allgather_small, row1, "allgather_cond")
    c_all = got1[:, 0, :d]
    conv_w_full = got1[0::2, 0, offs1[1]:offs1[1] + CONV_WIDTH * ch_loc].reshape(N_CHIPS, CONV_WIDTH, ch_loc)
    conv_w_full = jnp.transpose(conv_w_full, (1, 0, 2)).reshape(CONV_WIDTH, ch)
    conv_w_pad = jnp.pad(conv_w_full, ((0, 1), (0, 0)))
    c_t = jnp.transpose(c_all)
    mod_cols = run(ada_fwd, c_t, w_ada[0])
    got2 = run(allgather_small, mod_cols, "allgather_mod")
    mod_all = got2.reshape(N_CHIPS, 2, N_DEV, nc_ada)[:, 0]
    mod = lax.dynamic_slice_in_dim(mod_all, me, 1, axis=1).reshape(1, N_CHIPS * nc_ada) + b_ada
    mod = jnp.pad(mod.reshape(N_MOD, d), ((0, SUBLANES - N_MOD), (0, 0)))

    near, far = (0, 1), (2,)
    plan_ici_near = functools.partial(plan_gather_ici, peers=near)
    plan_ici_far = functools.partial(plan_gather_ici, peers=far)
    plan_d2d_near = functools.partial(plan_gather_d2d, peers=near)
    plan_d2d_far = functools.partial(plan_gather_d2d, peers=far)
    ici_near = run(exchange_start, "gather_ici_start_in_near", bufs_in, len(near), plan_ici_near)
    rest_bufs = {gname: gather_cast(gname) for gname in ("mid", "ffn_in", "ffn_out")}

    h = run(pre_mix_fwd, xe, mod, norm_mix_g)
    bucket = _t5_bucket_table()
    bucket_p, bucket_c = jnp.asarray(bucket[:, :BLOCK]), jnp.asarray(bucket[:, BLOCK:])
    bias_p, bias_c = run(bias_table, rel_bias, bucket_p, bucket_c)

    def in_blocks(buf):
        return buf.reshape(N_CHIPS, *big["w_in"].shape)

    def chip_ids(peers):
        others = [2 * (1 - xi) + yi, 2 * xi + (1 - yi), 2 * (1 - xi) + (1 - yi)]
        return jnp.stack([others[j] for j in peers]).astype(jnp.int32)

    tn_in = _tile(big["w_in"].shape[1], 640)
    landed = run(exchange_wait, "gather_ici_wait_in_near", ici_near, plan_ici_near)
    ici_far = run(exchange_start, "gather_ici_start_in_far", landed, len(far), plan_ici_far)
    ici = {gname: gather_ici_start(gname, bufs) for gname, bufs in rest_bufs.items()}
    d2d_near = run(exchange_start, "gather_d2d_start_in_near", ici_far.bufs, len(near), plan_d2d_near)
    p = run(mm_nn_blocks, h, in_blocks(d2d_near.bufs[0]), chip_arr, None, tn=tn_in, out_dtype=F32, name="mm_in_own")
    landed = run(exchange_wait, "gather_d2d_wait_in_near", d2d_near, plan_d2d_near)
    p = run(mm_nn_blocks, h, in_blocks(landed[0]), chip_ids(near), p, tn=tn_in, out_dtype=F32, name="mm_in_near")
    landed = run(exchange_wait, "gather_ici_wait_in_far", ici_far, plan_ici_far, bufs=landed)
    d2d_far = run(exchange_start, "gather_d2d_start_in_far", landed, len(far), plan_d2d_far)
    landed = run(exchange_wait, "gather_d2d_wait_in_far", d2d_far, plan_d2d_far)
    wg_in = in_blocks(landed[0])
    p = run(mm_nn_blocks, h, wg_in, chip_ids(far), p, tn=tn_in, out_dtype=F32, name="mm_in_far")
    d2d_mid = gather_pass_on("mid", ici["mid"])

    sinks3 = attn_sinks.reshape(nq, 1, 1)
    attn_o = run(attn_fwd, p, bias_p, bias_c, sinks3, q_norm_g, k_norm_g, aw=aw, kvw=kvw)
    ca, cb = p[:, off_ca:off_cb], p[:, off_cb:off_ga]
    s_conv, co_conv = run(conv_fwd, ca, cb, conv_w_pad, conv_b, conv_ln_g, conv_ln_b)
    wg_attn_out, wg_conv_out, wg_mix_out = gathered("mid", d2d_mid)
    wg_mix_out = wg_mix_out.reshape(1, d, d)
    y_attn = run(mm_nn, attn_o, wg_attn_out, tn=_tile(wg_attn_out.shape[2], 512), tk=aw, out_dtype=BF16,
                 name="mm_attn_out")
    y_conv = run(mm_nn, s_conv, wg_conv_out, tn=_tile(wg_conv_out.shape[2], 512), tk=ch, out_dtype=BF16,
                 name="mm_conv_out")
    merged = run(merge_fwd, p, y_attn, y_conv, off_ga, off_gc)
    d2d_ffn_in = gather_pass_on("ffn_in", ici["ffn_in"])
    o_m = run(mm_nn, merged, wg_mix_out, tn=_tile(d, 512), tk=d, out_dtype=F32, name="mm_mix_out")
    x1, h2 = run(pre_ffn_fwd, xe, o_m, mod, norm_ffn_g)
    (wg_ffn_in,) = gathered("ffn_in", d2d_ffn_in)
    f = run(mm_nn, h2, wg_ffn_in, tn=_tile(nj_ffn, 1408), tk=d, out_dtype=BF16, name="mm_ffn_in", perm=perm_ffn)
    d2d_ffn_out = gather_pass_on("ffn_out", ici["ffn_out"])
    act = run(swiglu_fwd, f, nj_ffn)
    (wg_ffn_out,) = gathered("ffn_out", d2d_ffn_out)
    wg_ffn_out = wg_ffn_out.reshape(1, dff, d)
    o_f = run(mm_nn, act, wg_ffn_out, tn=_tile(d, 1024), tk=_tile(dff, 1408), out_dtype=F32, name="mm_ffn_out")
    loss11, dy, dof, acc_l = run(loss_head, x1, o_f, tgt, mod)

    gw_ffn_out = run(mm_tn, act, dof, 1, tk=_tile(dff, 512), tn=d, name="mm_ffn_out_dw")
    px_ffn_out = rs_pair_start("ffn_out", ["w_ffn_out"], [gw_ffn_out])
    dact = run(mm_nt, dof, wg_ffn_out, tko=_tile(dff, 512), tn=d, out_dtype=BF16, name="mm_ffn_out_dx")
    cx_ffn_out = rs_chip_start("ffn_out", ["w_ffn_out"], px_ffn_out)
    df = run(swiglu_bwd, f, dact, nj_ffn)
    gw_ffn_in = run(mm_tn, h2, df, N_CHIPS, tk=d, tn=_tile(nj_ffn, 1408), name="mm_ffn_in_dw",
                    perm=perm_ffn)
    px_ffn_in = rs_pair_start("ffn_in", ["w_ffn_in"], [gw_ffn_in])
    dh2 = run(mm_nt, df, wg_ffn_in, tko=_tile(d, 512), tn=nj_ffn, name="mm_ffn_in_dx", perm=perm_ffn)
    sh_ffn_out = rs_share_start("ffn_out", ["w_ffn_out"], cx_ffn_out)
    cx_ffn_in = rs_chip_start("ffn_in", ["w_ffn_in"], px_ffn_in)
    dx1, dom, acc_f = run(pre_ffn_bwd, x1, dh2, dy, o_m, mod, norm_ffn_g)
    gw_mix_out = run(mm_tn, merged, dom, 1, tk=d, tn=_tile(d, 1024), name="mm_mix_out_dw")
    px_mix = rs_pair_start("mix_out", ["w_mix_out"], [gw_mix_out])
    dmerged = run(mm_nt, dom, wg_mix_out, tko=_tile(d, 512), tn=d, out_dtype=BF16, name="mm_mix_out_dx")
    dy_attn, dy_conv, dga, dgc = run(merge_bwd, p, y_attn, y_conv, dmerged, off_ga, off_gc)
    rs_finish("ffn_out", ["w_ffn_out"], sh_ffn_out)
    cx_mix = rs_chip_start("mix_out", ["w_mix_out"], px_mix)
    gw_attn_out = run(mm_tn, attn_o, dy_attn, N_CHIPS, tk=aw, tn=_tile(wg_attn_out.shape[2], 512),
                      name="mm_attn_out_dw")
    gw_conv_out = run(mm_tn, s_conv, dy_conv, N_CHIPS, tk=ch, tn=_tile(wg_conv_out.shape[2], 512),
                      name="mm_conv_out_dw")
    ac_names = ["w_attn_out", "w_conv_out"]
    px_ac = rs_pair_start("attn_conv_out", ac_names, [gw_attn_out, gw_conv_out])
    dattn_o = run(mm_nt, dy_attn, wg_attn_out, tko=_tile(aw, 1024), tn=_tile(wg_attn_out.shape[2], 512),
                  out_dtype=BF16, name="mm_attn_out_dx")
    ds_conv = run(mm_nt, dy_conv, wg_conv_out, tko=_tile(ch, 1024), tn=_tile(wg_conv_out.shape[2], 512),
                  out_dtype=BF16, name="mm_conv_out_dx")
    cx_ac = rs_chip_start("attn_conv_out", ac_names, px_ac)
    dca, dcb, dconv_w, dconv_vec = run(conv_bwd, ca, cb, co_conv, ds_conv, conv_w_pad, conv_ln_g, conv_ln_b)
    sh_ffn_in = rs_share_start("ffn_in", ["w_ffn_in"], cx_ffn_in)
    dqkv, dbp, dbc, dsinks, dqg, dkg = run(attn_bwd, p, bias_p, bias_c, sinks3, q_norm_g, k_norm_g, dattn_o,
                                           aw=aw, kvw=kvw)
    sh_mix = rs_share_start("mix_out", ["w_mix_out"], cx_mix)
    sh_ac = rs_share_start("attn_conv_out", ac_names, cx_ac)
    drel = run(bias_table_bwd, dbp, dbc, bucket_p, bucket_c).reshape(NUM_BUCKETS, nq)
    dp = jnp.concatenate([dqkv, dca, dcb, dga, dgc], axis=1)
    gw_in = run(mm_tn, h, dp, N_CHIPS, tk=d, tn=_tile(wg_in.shape[2], 640), name="mm_in_dw")
    px_in = rs_pair_start("in", ["w_in"], [gw_in])
    dh = run(mm_nt, dp, wg_in, tko=_tile(d, 1024), tn=wg_in.shape[2], name="mm_in_dx")
    grad_x, acc_m = run(pre_mix_bwd, xe, dh, dx1, mod, norm_mix_g)

    dmod = jnp.concatenate([acc_m[0:1], acc_m[1:2], acc_f[3:4], acc_f[0:1], acc_f[1:2], acc_l[0:1]], axis=1)
    small_names = ["b_ada", "norm_mix_g", "q_norm_g", "k_norm_g", "attn_sinks", "rel_bias", "conv_b", "conv_ln_g",
                   "conv_ln_b", "norm_ffn_g"]
    small_w = [b_ada, norm_mix_g, q_norm_g, k_norm_g, attn_sinks, rel_bias, conv_b, conv_ln_g, conv_ln_b, norm_ffn_g]
    small_m = [m_b_ada, m_norm_mix_g, m_q_norm_g, m_k_norm_g, m_attn_sinks, m_rel_bias, m_conv_b, m_conv_ln_g,
               m_conv_ln_b, m_norm_ffn_g]
    small_v = [v_b_ada, v_norm_mix_g, v_q_norm_g, v_k_norm_g, v_attn_sinks, v_rel_bias, v_conv_b, v_conv_ln_g,
               v_conv_ln_b, v_norm_ffn_g]
    small_g = [dmod, acc_m[2:3], dqg, dkg, dsinks.reshape(1, nq), drel.reshape(1, NUM_BUCKETS * nq),
               dconv_vec[0:1], dconv_vec[1:2], dconv_vec[2:3], acc_f[2:3]]
    row3, offs3 = _row_pack(small_g + [dconv_w[:CONV_WIDTH].reshape(1, CONV_WIDTH * ch), loss11])
    got3 = run(allgather_small, row3, "allgather_small_grads")
    cx_in = rs_chip_start("in", ["w_in"], px_in)
    as_row = lambda a: a.reshape(1, -1)
    outs3 = run(small_sum_adamw, got3, offs3, [as_row(a) for a in small_w], [as_row(a) for a in small_m],
                [as_row(a) for a in small_v], [CONV_WIDTH * ch, 1])
    for i, (n, w) in enumerate(zip(small_names, small_w)):
        grads[n], deltas[n], new_m[n], new_v[n] = (o.reshape(w.shape) for o in outs3[4 * i:4 * i + 4])
    g_conv_w_all, loss_sum = outs3[-2].reshape(CONV_WIDTH, ch), outs3[-1]

    g_conv_w = lax.dynamic_slice_in_dim(g_conv_w_all, chip * ch_loc, ch_loc, axis=1)
    grads["conv_w"] = g_conv_w[None]
    dl, nm, nv = run(adamw, conv_w[0], g_conv_w, m_conv_w[0], v_conv_w[0], "adamw_conv_w")
    deltas["conv_w"], new_m["conv_w"], new_v["conv_w"] = dl[None], nm[None], nv[None]

    dmod_all = got3[:, 0, :N_MOD * d]
    dmod_cols = lax.dynamic_slice_in_dim(dmod_all, chip * nc_ada, nc_ada, axis=1)
    g_ada, dl, nm, nv = run(ada_grad_adamw, c_t, dmod_cols, w_ada[0], m_w_ada[0], v_w_ada[0])
    grads["w_ada"], deltas["w_ada"], new_m["w_ada"], new_v["w_ada"] = g_ada[None], dl[None], nm[None], nv[None]

    rs_finish("ffn_in", ["w_ffn_in"], sh_ffn_in)
    rs_finish("mix_out", ["w_mix_out"], sh_mix)
    rs_finish("attn_conv_out", ac_names, sh_ac)
    sh_in = rs_share_start("in", ["w_in"], cx_in)
    rs_finish("in", ["w_in"], sh_in)

    loss = loss_sum[0, 0]
    order = ["w_ada", "b_ada", "norm_mix_g", "w_in", "q_norm_g", "k_norm_g", "attn_sinks", "rel_bias", "w_attn_out",
             "conv_w", "conv_b", "conv_ln_g", "conv_ln_b", "w_conv_out", "w_mix_out", "norm_ffn_g", "w_ffn_in",
             "w_ffn_out"]
    return (loss, grad_x[None], *[grads[n] for n in order], *[deltas[n] for n in order],
            *[new_m[n] for n in order], *[new_v[n] for n in order])
```

```python
import functools
import math
from typing import Any, NamedTuple

import jax
import jax.numpy as jnp
import numpy as np
from jax import lax
from jax.experimental import pallas as pl
from jax.experimental.pallas import tpu as pltpu

F32 = jnp.float32
BF16 = jnp.bfloat16
MESH = pl.DeviceIdType.MESH

V7X_VMEM_BYTES = 64 * 1024 * 1024
VMEM_LIMIT = V7X_VMEM_BYTES - 8 * 1024 * 1024
LANES = 128
SUBLANES = 8
BF16_SUBLANES = 16

EPS = 1e-6
WINDOW = 128
BLOCK = 128
NUM_BUCKETS = 32
MAX_EXACT = NUM_BUCKETS // 2
MAX_DISTANCE = 128
CONV_WIDTH = 31
CONV_HALO = 32
ADAM_LR = 0.001
ADAM_B1 = 0.9
ADAM_B2 = 0.999
ADAM_EPS = 1e-08
ADAM_WD = 0.01
ADAM_STEP = 10
N_MOD = 6
SH_M, SC_M, GT_M, SH_F, SC_F, GT_F = range(6)

N_CHIPS = 4
N_DEV = 8

_ANY = pl.BlockSpec(memory_space=pl.ANY)
_VMEM = pl.BlockSpec(memory_space=pltpu.VMEM)
_SMEM = pl.BlockSpec(memory_space=pltpu.SMEM)
_HBM = pl.BlockSpec(memory_space=pltpu.HBM)
_SEM = pl.BlockSpec(memory_space=pltpu.SEMAPHORE)
_EFFECT = pltpu.SideEffectType.DATAFLOW_SIDE_EFFECTING


class InOrder:
    def __init__(self):
        self.token = None

    def __call__(self, fn, *args, **kw):
        return fn(*args, dep=self, **kw)


def _pallas(body, args, *, in_specs, out_specs, out_shape, name, dep=None, grid=(), n_prefetch=0, scratch=(),
            sem=None, **kw):
    n_lead = n_prefetch + len(in_specs)
    in_specs, args = list(in_specs), list(args)
    single = not isinstance(out_shape, (list, tuple))
    out_shapes = [out_shape] if single else list(out_shape)
    out_specs = [out_specs] if single else list(out_specs)
    if dep is not None:
        inner, n_out, takes = body, len(out_shapes), dep.token is not None

        def body(*refs):
            rest = refs[n_lead + (1 if takes else 0):]
            rest[n_out][...] = jnp.zeros((SUBLANES, LANES), F32)
            return inner(*refs[:n_lead], *rest[:n_out], *rest[n_out + 1:])

        if takes:
            in_specs.append(_ANY)
            args.append(dep.token)
        out_shapes.append(jax.ShapeDtypeStruct((SUBLANES, LANES), F32))
        out_specs.append(pl.BlockSpec((SUBLANES, LANES), lambda *_: (0, 0)))
    params = kw.pop("compiler_params", None)
    if params is None:
        params = pltpu.CompilerParams(dimension_semantics=sem, vmem_limit_bytes=VMEM_LIMIT)
    outs = pl.pallas_call(
        body,
        grid_spec=pltpu.PrefetchScalarGridSpec(num_scalar_prefetch=n_prefetch, grid=grid, in_specs=in_specs,
                                               out_specs=out_specs, scratch_shapes=list(scratch)),
        out_shape=out_shapes, compiler_params=params, name=name, **kw,
    )(*args)
    if dep is not None:
        dep.token = outs[-1]
        outs = outs[:-1]
    return outs[0] if single else list(outs)


def _tile(n, pref, unit=LANES):
    best = None
    for t in range(unit, min(n, pref) + 1, unit):
        if n % t == 0:
            best = t
    return best if best is not None else n


def _sigmoid(v):
    return 1.0 / (1.0 + jnp.exp(-v.astype(F32)))


ROW_CHUNK = 512


def _row_chunks(m, unit=SUBLANES):
    step = _tile(m, ROW_CHUNK, unit)
    return [(s, step) for s in range(0, m, step)]


def _ew_tiles(r, n, unit=SUBLANES, elems=512 * 1024):
    return _tile(r, max(unit, elems // n), unit), n


def _block_pos(j, perm):
    if perm is None:
        return j
    pos = 0
    for a, p in enumerate(perm):
        pos = pos + jnp.where(j == a, p, 0)
    return pos


def mm_nn(a, w, *, tn, tk, out_dtype, name, perm=None, dep=None):
    m, k = a.shape
    j, k2, nj = w.shape
    assert k == k2 and nj % tn == 0 and k % tk == 0
    npj, nk = nj // tn, k // tk

    def body(a_ref, w_ref, o_ref, *scratch):
        kk = pl.program_id(1)
        for s, sz in _row_chunks(m):
            rows = pl.ds(s, sz)
            p = jnp.dot(a_ref[rows, :], w_ref[...], preferred_element_type=F32)
            if nk == 1:
                o_ref[rows, :] = p.astype(out_dtype)
            else:
                acc = scratch[0]

                @pl.when(kk == 0)
                def _():
                    acc[rows, :] = p

                @pl.when(kk > 0)
                def _():
                    acc[rows, :] += p

                @pl.when(kk == nk - 1)
                def _():
                    o_ref[rows, :] = acc[rows, :].astype(out_dtype)

    return _pallas(
        body, [a, w], dep=dep, grid=(j * npj, nk),
        in_specs=[
            pl.BlockSpec((m, tk), lambda n, kk: (0, kk)),
            pl.BlockSpec((None, tk, tn), lambda n, kk: (n // npj, kk, n % npj)),
        ],
        out_specs=pl.BlockSpec((m, tn), lambda n, kk: (0, _block_pos(n // npj, perm) * npj + n % npj)),
        out_shape=jax.ShapeDtypeStruct((m, j * nj), out_dtype),
        scratch=[pltpu.VMEM((m, tn), F32)] if nk > 1 else [],
        sem=("parallel", "arbitrary"), name=name)


def mm_nn_blocks(a, w, blocks, into, *, tn, out_dtype, name, dep=None):
    m, k = a.shape
    j, k2, nj = w.shape
    assert k == k2 and nj % tn == 0
    npj = nj // tn
    n_in = 2 if into is None else 3

    def body(blocks_ref, a_ref, w_ref, *rest):
        o_ref = rest[n_in - 2]
        for s, sz in _row_chunks(m):
            rows = pl.ds(s, sz)
            o_ref[rows, :] = jnp.dot(a_ref[rows, :], w_ref[...], preferred_element_type=F32).astype(out_dtype)

    return _pallas(
        body, [blocks, a, w] + ([] if into is None else [into]), dep=dep, n_prefetch=1,
        grid=(blocks.shape[0] * npj,),
        in_specs=[pl.BlockSpec((m, k), lambda n, bl: (0, 0)),
                  pl.BlockSpec((None, k, tn), lambda n, bl: (bl[n // npj], 0, n % npj))]
        + ([] if into is None else [_ANY]),
        out_specs=pl.BlockSpec((m, tn), lambda n, bl: (0, bl[n // npj] * npj + n % npj)),
        out_shape=jax.ShapeDtypeStruct((m, j * nj), out_dtype),
        input_output_aliases={} if into is None else {3: 0},
        sem=("arbitrary",), name=name)


def mm_nt(g, w, *, tko, tn, name, out_dtype=F32, perm=None, dep=None):
    m, n = g.shape
    j, k, nj = w.shape
    assert n == j * nj and nj % tn == 0 and k % tko == 0
    npj, nr = nj // tn, n // tn
    in_place = out_dtype == F32

    def body(g_ref, w_ref, o_ref, *scratch):
        r = pl.program_id(1)
        acc = o_ref if in_place else (scratch[0] if nr > 1 else None)
        for s, sz in _row_chunks(m):
            rows = pl.ds(s, sz)
            p = lax.dot_general(g_ref[rows, :], w_ref[...], (((1,), (1,)), ((), ())), preferred_element_type=F32)
            if acc is None:
                o_ref[rows, :] = p.astype(out_dtype)
                continue

            @pl.when(r == 0)
            def _():
                acc[rows, :] = p

            @pl.when(r > 0)
            def _():
                acc[rows, :] += p

            if not in_place:
                @pl.when(r == nr - 1)
                def _():
                    o_ref[rows, :] = acc[rows, :].astype(out_dtype)

    return _pallas(
        body, [g, w], dep=dep, grid=(k // tko, nr),
        in_specs=[
            pl.BlockSpec((m, tn), lambda ko, r: (0, _block_pos(r // npj, perm) * npj + r % npj)),
            pl.BlockSpec((None, tko, tn), lambda ko, r: (r // npj, ko, r % npj)),
        ],
        out_specs=pl.BlockSpec((m, tko), lambda ko, r: (0, ko)),
        out_shape=jax.ShapeDtypeStruct((m, k), out_dtype),
        scratch=[pltpu.VMEM((m, tko), F32)] if (nr > 1 and not in_place) else [],
        sem=("parallel", "arbitrary"), name=name)


def mm_tn(a, g, n_blocks, *, tk, tn, name, perm=None, dep=None):
    m, k = a.shape
    m2, n = g.shape
    nj = n // n_blocks
    assert m == m2 and nj % tn == 0 and k % tk == 0
    npj = nj // tn

    def body(a_ref, g_ref, o_ref):
        for s, sz in _row_chunks(tk, LANES):
            p = lax.dot_general(a_ref[:, pl.ds(s, sz)], g_ref[...], (((0,), (0,)), ((), ())),
                                preferred_element_type=F32)
            o_ref[pl.ds(s, sz), :] = p.astype(BF16)

    return _pallas(
        body, [a, g], dep=dep, grid=(k // tk, n // tn),
        in_specs=[
            pl.BlockSpec((m, tk), lambda kk, nn: (0, kk)),
            pl.BlockSpec((m, tn), lambda kk, nn: (0, _block_pos(nn // npj, perm) * npj + nn % npj)),
        ],
        out_specs=pl.BlockSpec((None, tk, tn), lambda kk, nn: (nn // npj, kk, nn % npj)),
        out_shape=jax.ShapeDtypeStruct((n_blocks, k, nj), BF16),
        sem=("parallel", "parallel"), name=name)


ROW_TILE = 256


def _row_spec(tr, width):
    return pl.BlockSpec((tr, width), lambda i: (i, 0))


def _full_spec(shape):
    return pl.BlockSpec(shape, lambda *_: (0,) * len(shape))


def _rms(xv):
    return lax.rsqrt(jnp.mean(xv * xv, axis=-1, keepdims=True) + EPS)


def _mod_row(mod_ref, row):
    return mod_ref[pl.ds(row, 1), :]


def pre_mix_fwd(x, mod, gain, dep=None):
    t, d = x.shape
    tr = _tile(t, ROW_TILE, SUBLANES)

    def body(x_ref, mod_ref, g_ref, h_ref):
        xv = x_ref[...]
        y = xv * _rms(xv) * g_ref[...]
        h_ref[...] = (y * (1.0 + _mod_row(mod_ref, SC_M)) + _mod_row(mod_ref, SH_M)).astype(BF16)

    return _pallas(
        body, [x, mod, gain], dep=dep, grid=(t // tr,),
        in_specs=[_row_spec(tr, d), _full_spec(mod.shape), _full_spec(gain.shape)],
        out_specs=_row_spec(tr, d),
        out_shape=jax.ShapeDtypeStruct((t, d), BF16),
        sem=("parallel",), name="pre_mix_fwd")


def pre_ffn_fwd(x, o_m, mod, gain, dep=None):
    t, d = x.shape
    tr = _tile(t, ROW_TILE, SUBLANES)

    def body(x_ref, om_ref, mod_ref, g_ref, x1_ref, h_ref):
        x1 = x_ref[...] + _mod_row(mod_ref, GT_M) * om_ref[...]
        x1_ref[...] = x1
        y = x1 * _rms(x1) * g_ref[...]
        h_ref[...] = (y * (1.0 + _mod_row(mod_ref, SC_F)) + _mod_row(mod_ref, SH_F)).astype(BF16)

    return _pallas(
        body, [x, o_m, mod, gain], dep=dep, grid=(t // tr,),
        in_specs=[_row_spec(tr, d), _row_spec(tr, d), _full_spec(mod.shape), _full_spec(gain.shape)],
        out_specs=[_row_spec(tr, d), _row_spec(tr, d)],
        out_shape=[jax.ShapeDtypeStruct((t, d), F32), jax.ShapeDtypeStruct((t, d), BF16)],
        sem=("parallel",), name="pre_ffn_fwd")


def loss_head(x1, o_f, target, mod, dep=None):
    t, d = x1.shape
    tr = _tile(t, ROW_TILE, SUBLANES)

    def body(x1_ref, of_ref, tg_ref, mod_ref, loss_ref, dy_ref, dof_ref, acc_ref):
        i = pl.program_id(0)
        gt = _mod_row(mod_ref, GT_F)
        of = of_ref[...]
        err = x1_ref[...] + gt * of - tg_ref[...]
        dy = err * (1.0 / d)
        dy_ref[...] = dy
        dof_ref[...] = (dy * gt).astype(BF16)
        part = (0.5 / d) * jnp.sum(jnp.sum(err * err, axis=1, keepdims=True), axis=0, keepdims=True)
        dgt = jnp.sum(dy * of, axis=0, keepdims=True)

        @pl.when(i == 0)
        def _():
            loss_ref[...] = jnp.zeros_like(loss_ref)
            acc_ref[...] = jnp.zeros_like(acc_ref)

        loss_ref[...] += part
        acc_ref[pl.ds(0, 1), :] += dgt

    return _pallas(
        body, [x1, o_f, target, mod], dep=dep, grid=(t // tr,),
        in_specs=[_row_spec(tr, d), _row_spec(tr, d), _row_spec(tr, d), _full_spec(mod.shape)],
        out_specs=[_full_spec((1, 1)), _row_spec(tr, d), _row_spec(tr, d), _full_spec((SUBLANES, d))],
        out_shape=[jax.ShapeDtypeStruct((1, 1), F32), jax.ShapeDtypeStruct((t, d), F32),
                   jax.ShapeDtypeStruct((t, d), BF16), jax.ShapeDtypeStruct((SUBLANES, d), F32)],
        sem=("arbitrary",), name="loss_head")


def _norm_bwd(xv, dh, sc, gain):
    rstd = _rms(xv)
    yn = xv * rstd
    dsh = jnp.sum(dh, axis=0, keepdims=True)
    dsc = jnp.sum(dh * (yn * gain), axis=0, keepdims=True)
    dgain = jnp.sum(dh * (1.0 + sc) * yn, axis=0, keepdims=True)
    dyn = dh * ((1.0 + sc) * gain)
    dx = rstd * (dyn - yn * jnp.mean(dyn * yn, axis=-1, keepdims=True))
    return dx, dsh, dsc, dgain


def pre_ffn_bwd(x1, dh2, dy, o_m, mod, gain, dep=None):
    t, d = x1.shape
    tr = _tile(t, ROW_TILE, SUBLANES)

    def body(x1_ref, dh_ref, dy_ref, om_ref, mod_ref, g_ref, dx1_ref, dom_ref, acc_ref):
        i = pl.program_id(0)
        dxn, dsh, dsc, dgain = _norm_bwd(x1_ref[...], dh_ref[...], _mod_row(mod_ref, SC_F), g_ref[...])
        dx1 = dy_ref[...] + dxn
        dx1_ref[...] = dx1
        dom_ref[...] = (dx1 * _mod_row(mod_ref, GT_M)).astype(BF16)
        dgt = jnp.sum(dx1 * om_ref[...], axis=0, keepdims=True)

        @pl.when(i == 0)
        def _():
            acc_ref[...] = jnp.zeros_like(acc_ref)

        acc_ref[pl.ds(0, 1), :] += dsh
        acc_ref[pl.ds(1, 1), :] += dsc
        acc_ref[pl.ds(2, 1), :] += dgain
        acc_ref[pl.ds(3, 1), :] += dgt

    return _pallas(
        body, [x1, dh2, dy, o_m, mod, gain], dep=dep, grid=(t // tr,),
        in_specs=[_row_spec(tr, d)] * 4 + [_full_spec(mod.shape), _full_spec(gain.shape)],
        out_specs=[_row_spec(tr, d), _row_spec(tr, d), _full_spec((SUBLANES, d))],
        out_shape=[jax.ShapeDtypeStruct((t, d), F32), jax.ShapeDtypeStruct((t, d), BF16),
                   jax.ShapeDtypeStruct((SUBLANES, d), F32)],
        sem=("arbitrary",), name="pre_ffn_bwd")


def pre_mix_bwd(x, dh, dx1, mod, gain, dep=None):
    t, d = x.shape
    tr = _tile(t, ROW_TILE, SUBLANES)

    def body(x_ref, dh_ref, dx1_ref, mod_ref, g_ref, gx_ref, acc_ref):
        i = pl.program_id(0)
        dxn, dsh, dsc, dgain = _norm_bwd(x_ref[...], dh_ref[...], _mod_row(mod_ref, SC_M), g_ref[...])
        gx_ref[...] = dx1_ref[...] + dxn

        @pl.when(i == 0)
        def _():
            acc_ref[...] = jnp.zeros_like(acc_ref)

        acc_ref[pl.ds(0, 1), :] += dsh
        acc_ref[pl.ds(1, 1), :] += dsc
        acc_ref[pl.ds(2, 1), :] += dgain

    return _pallas(
        body, [x, dh, dx1, mod, gain], dep=dep, grid=(t // tr,),
        in_specs=[_row_spec(tr, d)] * 3 + [_full_spec(mod.shape), _full_spec(gain.shape)],
        out_specs=[_row_spec(tr, d), _full_spec((SUBLANES, d))],
        out_shape=[jax.ShapeDtypeStruct((t, d), F32), jax.ShapeDtypeStruct((SUBLANES, d), F32)],
        sem=("arbitrary",), name="pre_mix_bwd")


def merge_fwd(p, y_attn, y_conv, off_ga, off_gc, dep=None):
    t, d = y_attn.shape
    tr = _tile(t, ROW_TILE, SUBLANES)
    cw = math.gcd(math.gcd(off_ga, off_gc), math.gcd(d, 512))
    nc = d // cw

    def body(ga_ref, gc_ref, ya_ref, yc_ref, o_ref):
        o_ref[...] = (_sigmoid(ga_ref[...]) * ya_ref[...] + _sigmoid(gc_ref[...]) * yc_ref[...]).astype(BF16)

    return _pallas(
        body, [p, p, y_attn, y_conv], dep=dep, grid=(t // tr, nc),
        in_specs=[pl.BlockSpec((tr, cw), lambda i, j: (i, off_ga // cw + j)),
                  pl.BlockSpec((tr, cw), lambda i, j: (i, off_gc // cw + j)),
                  pl.BlockSpec((tr, cw), lambda i, j: (i, j)),
                  pl.BlockSpec((tr, cw), lambda i, j: (i, j))],
        out_specs=pl.BlockSpec((tr, cw), lambda i, j: (i, j)),
        out_shape=jax.ShapeDtypeStruct((t, d), BF16),
        sem=("parallel", "parallel"), name="merge_fwd")


def merge_bwd(p, y_attn, y_conv, dmerged, off_ga, off_gc, dep=None):
    t, d = y_attn.shape
    tr = _tile(t, ROW_TILE, SUBLANES)
    cw = math.gcd(math.gcd(off_ga, off_gc), math.gcd(d, 512))
    nc = d // cw

    def body(ga_ref, gc_ref, ya_ref, yc_ref, dm_ref, dya_ref, dyc_ref, dga_ref, dgc_ref):
        dm = dm_ref[...].astype(F32)
        sa = _sigmoid(ga_ref[...])
        sc = _sigmoid(gc_ref[...])
        dya_ref[...] = (dm * sa).astype(BF16)
        dyc_ref[...] = (dm * sc).astype(BF16)
        dga_ref[...] = (dm * ya_ref[...] * sa * (1.0 - sa)).astype(BF16)
        dgc_ref[...] = (dm * yc_ref[...] * sc * (1.0 - sc)).astype(BF16)

    blk = pl.BlockSpec((tr, cw), lambda i, j: (i, j))
    return _pallas(
        body, [p, p, y_attn, y_conv, dmerged], dep=dep, grid=(t // tr, nc),
        in_specs=[pl.BlockSpec((tr, cw), lambda i, j: (i, off_ga // cw + j)),
                  pl.BlockSpec((tr, cw), lambda i, j: (i, off_gc // cw + j)), blk, blk, blk],
        out_specs=[blk] * 4,
        out_shape=[jax.ShapeDtypeStruct((t, d), BF16)] * 4,
        sem=("parallel", "parallel"), name="merge_bwd")


def ffn_perm(n_blocks):
    half = n_blocks // 2
    return tuple(2 * j if j < half else 2 * (j - half) + 1 for j in range(n_blocks))


def swiglu_fwd(f, nj, dep=None):
    t, two = f.shape
    tr = _tile(t, ROW_TILE, SUBLANES)
    npair = two // (2 * nj)

    def body(f_ref, o_ref):
        g = f_ref[:, :nj].astype(F32)
        u = f_ref[:, nj:].astype(F32)
        o_ref[...] = (g * _sigmoid(g) * u).astype(BF16)

    return _pallas(
        body, [f], dep=dep, grid=(t // tr, npair),
        in_specs=[pl.BlockSpec((tr, 2 * nj), lambda i, j: (i, j))],
        out_specs=pl.BlockSpec((tr, nj), lambda i, j: (i, j)),
        out_shape=jax.ShapeDtypeStruct((t, two // 2), BF16),
        sem=("parallel", "parallel"), name="swiglu_fwd")


def swiglu_bwd(f, dact, nj, dep=None):
    t, two = f.shape
    tr = _tile(t, ROW_TILE, SUBLANES)
    npair = two // (2 * nj)

    def body(f_ref, da_ref, o_ref):
        g = f_ref[:, :nj].astype(F32)
        u = f_ref[:, nj:].astype(F32)
        da = da_ref[...]
        s = _sigmoid(g)
        o_ref[:, :nj] = (da * u * (s * (1.0 + g * (1.0 - s)))).astype(BF16)
        o_ref[:, nj:] = (da * (g * s)).astype(BF16)

    return _pallas(
        body, [f, dact], dep=dep, grid=(t // tr, npair),
        in_specs=[pl.BlockSpec((tr, 2 * nj), lambda i, j: (i, j)), pl.BlockSpec((tr, nj), lambda i, j: (i, j))],
        out_specs=pl.BlockSpec((tr, 2 * nj), lambda i, j: (i, j)),
        out_shape=jax.ShapeDtypeStruct((t, two), BF16),
        sem=("parallel", "parallel"), name="swiglu_bwd")


def _t5_bucket_table():
    q_off = np.arange(BLOCK)
    k_off = np.arange(2 * BLOCK)
    dist = q_off[:, None] + BLOCK - k_off[None, :]
    n = np.maximum(dist, 0)
    nf = np.maximum(n, 1).astype(np.float32)
    large = MAX_EXACT + (np.log(nf / np.float32(MAX_EXACT)) / np.float32(math.log(MAX_DISTANCE / MAX_EXACT))
                         * np.float32(NUM_BUCKETS - MAX_EXACT)).astype(np.int32)
    large = np.minimum(large, NUM_BUCKETS - 1)
    bucket = np.where(n < MAX_EXACT, n, large).astype(np.int32)
    allowed = (dist >= 0) & (dist < WINDOW)
    return np.where(allowed, bucket, -1).astype(np.int32)


def bias_table(rel_bias, bucket_p, bucket_c, dep=None):
    nb, nq = rel_bias.shape

    def body(rb_ref, bkp_ref, bkc_ref, op_ref, oc_ref):
        for bk_ref, o_ref in ((bkp_ref, op_ref), (bkc_ref, oc_ref)):
            bk = bk_ref[...]
            for h in range(nq):
                acc = jnp.full(bk.shape, -jnp.inf, F32)
                for b in range(nb):
                    acc = jnp.where(bk == b, rb_ref[b, h], acc)
                o_ref[h] = acc

    return _pallas(
        body, [rel_bias, bucket_p, bucket_c], dep=dep,
        in_specs=[_SMEM, _VMEM, _VMEM], out_specs=[_VMEM, _VMEM],
        out_shape=[jax.ShapeDtypeStruct((nq,) + bucket_p.shape, F32)] * 2,
        name="bias_table")


def bias_table_bwd(dbp, dbc, bucket_p, bucket_c, dep=None):
    nq = dbp.shape[0]

    def body(dbp_ref, dbc_ref, bkp_ref, bkc_ref, o_ref):
        bkp, bkc = bkp_ref[...][None], bkc_ref[...][None]
        dp, dc = dbp_ref[...], dbc_ref[...]
        for b in range(NUM_BUCKETS):
            sel = jnp.where(bkp == b, dp, 0.0) + jnp.where(bkc == b, dc, 0.0)
            o_ref[b] = jnp.sum(jnp.sum(sel, axis=2, keepdims=True), axis=1, keepdims=True)

    return _pallas(
        body, [dbp, dbc, bucket_p, bucket_c], dep=dep,
        in_specs=[_VMEM] * 4, out_specs=_VMEM,
        out_shape=jax.ShapeDtypeStruct((NUM_BUCKETS, nq, 1, 1), F32),
        name="bias_table_bwd")


_NT = (((1,), (1,)), ((), ()))
_TN = (((0,), (0,)), ((), ()))


@jax.custom_vjp
def _bdot_nt(a, b):
    return lax.dot_general(a.astype(BF16), b.astype(BF16), _NT, preferred_element_type=F32)


def _bdot_nt_fwd(a, b):
    return _bdot_nt(a, b), (a, b)


def _bdot_nt_bwd(res, g):
    a, b = res
    gb = g.astype(BF16)
    da = jnp.dot(gb, b.astype(BF16), preferred_element_type=F32)
    db = lax.dot_general(gb, a.astype(BF16), _TN, preferred_element_type=F32)
    return da, db


_bdot_nt.defvjp(_bdot_nt_fwd, _bdot_nt_bwd)


@jax.custom_vjp
def _bdot_nn(a, b):
    return jnp.dot(a.astype(BF16), b.astype(BF16), preferred_element_type=F32)


def _bdot_nn_fwd(a, b):
    return _bdot_nn(a, b), (a, b)


def _bdot_nn_bwd(res, g):
    a, b = res
    gb = g.astype(BF16)
    da = lax.dot_general(gb, b.astype(BF16), _NT, preferred_element_type=F32)
    db = lax.dot_general(a.astype(BF16), gb, _TN, preferred_element_type=F32)
    return da, db


_bdot_nn.defvjp(_bdot_nn_fwd, _bdot_nn_bwd)


def _attn_math(q4, kp, kc, vp, vc, bp, bc, sink4, qg, kg, *, prev_ok, scale):
    g, b, hd = q4.shape
    q = q4.reshape(g * b, hd)
    qn = q * _rms(q) * qg
    kpn = kp * _rms(kp) * kg
    kcn = kc * _rms(kc) * kg
    lp = _bdot_nt(qn, kpn).reshape(g, b, b) * scale + bp
    lc = _bdot_nt(qn, kcn).reshape(g, b, b) * scale + bc
    lp = jnp.where(prev_ok, lp, -jnp.inf)
    m = jnp.maximum(jnp.maximum(jnp.max(lp, axis=-1, keepdims=True), jnp.max(lc, axis=-1, keepdims=True)), sink4)
    m = lax.stop_gradient(m)
    pp = jnp.exp(lp - m)
    pc = jnp.exp(lc - m)
    den = jnp.sum(pp, axis=-1, keepdims=True) + jnp.sum(pc, axis=-1, keepdims=True) + jnp.exp(sink4 - m)
    inv = 1.0 / den
    out = _bdot_nn((pp * inv).reshape(g * b, b), vp) + _bdot_nn((pc * inv).reshape(g * b, b), vc)
    return out.reshape(g, b, hd)


def _attn_specs(p, aw, kvw, nq, hd, nblk, reverse):
    assert aw % (2 * kvw) == 0
    kv_col = aw // (2 * kvw)

    def blk(n):
        return nblk - 1 - n if reverse else n

    return [
        pl.BlockSpec((BLOCK, aw), lambda n: (blk(n), 0)),
        pl.BlockSpec((BLOCK, 2 * kvw), lambda n: (jnp.maximum(blk(n) - 1, 0), kv_col)),
        pl.BlockSpec((BLOCK, 2 * kvw), lambda n: (blk(n), kv_col)),
        _full_spec((nq, BLOCK, BLOCK)), _full_spec((nq, BLOCK, BLOCK)), _full_spec((nq, 1, 1)),
        _full_spec((1, hd)), _full_spec((1, hd)),
    ]


def _attn_head_inputs(h, grp, hd, kvw, q_ref, kvp_ref, kvc_ref, bp_ref, bc_ref, s_ref):
    heads = pl.ds(grp * h, grp)
    q4 = jnp.stack([q_ref[:, pl.ds((grp * h + g) * hd, hd)].astype(F32) for g in range(grp)])
    k_cols, v_cols = pl.ds(h * hd, hd), pl.ds(kvw + h * hd, hd)
    kv = [r[:, cols].astype(F32) for cols in (k_cols, v_cols) for r in (kvp_ref, kvc_ref)]
    return (q4, *kv, bp_ref[heads], bc_ref[heads], s_ref[heads])


def attn_fwd(p, bias_p, bias_c, sinks, qg, kg, *, aw, kvw, dep=None):
    t, hd = p.shape[0], qg.shape[-1]
    nq, nkv, nblk = aw // hd, kvw // hd, t // BLOCK
    grp = nq // nkv
    scale = hd ** -0.5

    def body(q_ref, kvp_ref, kvc_ref, bp_ref, bc_ref, s_ref, qg_ref, kg_ref, o_ref):
        prev_ok = pl.program_id(0) > 0
        for h in range(nkv):
            args = _attn_head_inputs(h, grp, hd, kvw, q_ref, kvp_ref, kvc_ref, bp_ref, bc_ref, s_ref)
            out = _attn_math(*args, qg_ref[...], kg_ref[...], prev_ok=prev_ok, scale=scale)
            for g in range(grp):
                o_ref[:, pl.ds((grp * h + g) * hd, hd)] = out[g].astype(BF16)

    return _pallas(
        body, [p, p, p, bias_p, bias_c, sinks, qg, kg], dep=dep, grid=(nblk,),
        in_specs=_attn_specs(p, aw, kvw, nq, hd, nblk, False),
        out_specs=pl.BlockSpec((BLOCK, aw), lambda n: (n, 0)),
        out_shape=jax.ShapeDtypeStruct((t, aw), BF16),
        sem=("parallel",), name="attn_fwd")


def attn_bwd(p, bias_p, bias_c, sinks, qg, kg, do, *, aw, kvw, dep=None):
    t, hd = p.shape[0], qg.shape[-1]
    nq, nkv, nblk = aw // hd, kvw // hd, t // BLOCK
    grp = nq // nkv
    scale = hd ** -0.5

    def body(q_ref, kvp_ref, kvc_ref, bp_ref, bc_ref, s_ref, qg_ref, kg_ref, do_ref,
             dqkv_ref, dbp_ref, dbc_ref, ds_ref, dqg_ref, dkg_ref, carry):
        i = pl.program_id(0)
        prev_ok = (nblk - 1 - i) > 0

        @pl.when(i == 0)
        def _():
            carry[...] = jnp.zeros_like(carry)
            dbp_ref[...] = jnp.zeros_like(dbp_ref)
            dbc_ref[...] = jnp.zeros_like(dbc_ref)
            ds_ref[...] = jnp.zeros_like(ds_ref)
            dqg_ref[...] = jnp.zeros_like(dqg_ref)
            dkg_ref[...] = jnp.zeros_like(dkg_ref)

        fn = functools.partial(_attn_math, prev_ok=prev_ok, scale=scale)
        for h in range(nkv):
            args = _attn_head_inputs(h, grp, hd, kvw, q_ref, kvp_ref, kvc_ref, bp_ref, bc_ref, s_ref)
            _, vjp = jax.vjp(fn, *args, qg_ref[...], kg_ref[...])
            do4 = jnp.stack([do_ref[:, pl.ds((grp * h + g) * hd, hd)].astype(F32) for g in range(grp)])
            dq, dkp, dkc, dvp, dvc, dbp, dbc, dsk, dqg, dkg = vjp(do4)
            for g in range(grp):
                dqkv_ref[:, pl.ds((grp * h + g) * hd, hd)] = dq[g].astype(BF16)
            k_cols, v_cols = pl.ds(h * hd, hd), pl.ds(kvw + h * hd, hd)
            dqkv_ref[:, pl.ds(aw + h * hd, hd)] = (dkc + carry[:, k_cols]).astype(BF16)
            dqkv_ref[:, pl.ds(aw + kvw + h * hd, hd)] = (dvc + carry[:, v_cols]).astype(BF16)
            carry[:, k_cols] = dkp
            carry[:, v_cols] = dvp
            heads = pl.ds(grp * h, grp)
            dbp_ref[heads] += dbp
            dbc_ref[heads] += dbc
            ds_ref[heads] += dsk
            dqg_ref[...] += dqg
            dkg_ref[...] += dkg

    return _pallas(
        body, [p, p, p, bias_p, bias_c, sinks, qg, kg, do], dep=dep, grid=(nblk,),
        in_specs=_attn_specs(p, aw, kvw, nq, hd, nblk, True)
        + [pl.BlockSpec((BLOCK, aw), lambda n: (nblk - 1 - n, 0))],
        out_specs=[
            pl.BlockSpec((BLOCK, aw + 2 * kvw), lambda n: (nblk - 1 - n, 0)),
            _full_spec((nq, BLOCK, BLOCK)), _full_spec((nq, BLOCK, BLOCK)), _full_spec((nq, 1, 1)),
            _full_spec((1, hd)), _full_spec((1, hd)),
        ],
        out_shape=[
            jax.ShapeDtypeStruct((t, aw + 2 * kvw), BF16),
            jax.ShapeDtypeStruct((nq, BLOCK, BLOCK), F32),
            jax.ShapeDtypeStruct((nq, BLOCK, BLOCK), F32),
            jax.ShapeDtypeStruct((nq, 1, 1), F32),
            jax.ShapeDtypeStruct((1, hd), F32),
            jax.ShapeDtypeStruct((1, hd), F32),
        ],
        scratch=[pltpu.VMEM((BLOCK, 2 * kvw), F32)],
        sem=("arbitrary",), name="attn_bwd")


CONV_TILE = 256


def _conv_halo_specs(tb, ch, nblk):
    per = tb // CONV_HALO
    last = nblk * per - 1
    cur = pl.BlockSpec((tb, ch), lambda n: (n, 0))
    prev = pl.BlockSpec((CONV_HALO, ch), lambda n: (jnp.maximum(n * per - 1, 0), 0))
    nxt = pl.BlockSpec((CONV_HALO, ch), lambda n: (jnp.minimum((n + 1) * per, last), 0))
    return cur, prev, nxt


def _ln_silu(co, ln_g, ln_b):
    mu = jnp.mean(co, axis=-1, keepdims=True)
    cen = co - mu
    rstd = lax.rsqrt(jnp.mean(cen * cen, axis=-1, keepdims=True) + EPS)
    xhat = cen * rstd
    z = xhat * ln_g + ln_b
    return xhat, rstd, z


def _shifted_copies(src, shifted):
    rows = src.shape[0] - SUBLANES
    for r in range(1, SUBLANES):
        shifted[r, pl.ds(0, rows), :] = src[pl.ds(r, rows), :]


def _rows_from(src, shifted, start, n):
    r = start % SUBLANES
    if r == 0:
        return src[pl.ds(start, n), :]
    return shifted[r, pl.ds(start - r, n), :]


def conv_fwd(ca, cb, conv_w, conv_b, ln_g, ln_b, dep=None):
    t, ch = ca.shape
    tb = _tile(t, CONV_TILE, CONV_HALO)
    nblk = t // tb
    cur, prev, _ = _conv_halo_specs(tb, ch, nblk)
    lead = CONV_HALO - (CONV_WIDTH - 1)

    def body(ca_ref, cb_ref, cap_ref, cbp_ref, w_ref, b_ref, g_ref, bb_ref, s_ref, co_ref, ubuf, ushift):
        n = pl.program_id(0)
        halo = cap_ref[...] * _sigmoid(cbp_ref[...])
        ubuf[pl.ds(0, CONV_HALO), :] = jnp.where(n > 0, halo, 0.0)
        ubuf[pl.ds(CONV_HALO, tb), :] = ca_ref[...] * _sigmoid(cb_ref[...])
        _shifted_copies(ubuf, ushift)
        acc = jnp.broadcast_to(b_ref[...], (tb, ch))
        for k in range(CONV_WIDTH):
            acc = acc + w_ref[pl.ds(k, 1), :] * _rows_from(ubuf, ushift, lead + k, tb)
        co_ref[...] = acc
        _, _, z = _ln_silu(acc, g_ref[...], bb_ref[...])
        s_ref[...] = (z * _sigmoid(z)).astype(BF16)

    vec = _full_spec((1, ch))
    return _pallas(
        body, [ca, cb, ca, cb, conv_w, conv_b, ln_g, ln_b], dep=dep, grid=(nblk,),
        in_specs=[cur, cur, prev, prev, _full_spec(conv_w.shape), vec, vec, vec],
        out_specs=[cur, cur],
        out_shape=[jax.ShapeDtypeStruct((t, ch), BF16), jax.ShapeDtypeStruct((t, ch), F32)],
        scratch=[pltpu.VMEM((CONV_HALO + tb, ch), F32), pltpu.VMEM((SUBLANES, CONV_HALO + tb, ch), F32)],
        sem=("parallel",), name="conv_fwd")


def conv_bwd(ca, cb, co, ds, conv_w, ln_g, ln_b, dep=None):
    t, ch = ca.shape
    tb = _tile(t, CONV_TILE, CONV_HALO)
    nblk = t // tb
    cur, prev, nxt = _conv_halo_specs(tb, ch, nblk)
    lead = CONV_HALO - (CONV_WIDTH - 1)
    ext = tb + CONV_HALO

    def body(ca_ref, cb_ref, cap_ref, cbp_ref, co_ref, con_ref, ds_ref, dsn_ref, w_ref, g_ref, bb_ref,
             dca_ref, dcb_ref, dw_ref, dvec_ref, ubuf, dbuf, ushift, dshift):
        n = pl.program_id(0)
        is_last = n == nblk - 1
        sig_b = _sigmoid(cb_ref[...])
        cav = ca_ref[...].astype(F32)
        ubuf[pl.ds(0, CONV_HALO), :] = jnp.where(n > 0, cap_ref[...] * _sigmoid(cbp_ref[...]), 0.0)
        ubuf[pl.ds(CONV_HALO, tb), :] = cav * sig_b
        _shifted_copies(ubuf, ushift)
        co = jnp.concatenate([co_ref[...], con_ref[...]], axis=0)
        xhat, rstd, z = _ln_silu(co, g_ref[...], bb_ref[...])
        dsv = jnp.concatenate([ds_ref[...].astype(F32), jnp.where(is_last, 0.0, dsn_ref[...].astype(F32))], axis=0)
        sg = _sigmoid(z)
        dz = dsv * (sg * (1.0 + z * (1.0 - sg)))
        dxh = dz * g_ref[...]
        dco = rstd * (dxh - jnp.mean(dxh, axis=-1, keepdims=True)
                      - xhat * jnp.mean(dxh * xhat, axis=-1, keepdims=True))
        dbuf[...] = dco
        _shifted_copies(dbuf, dshift)

        @pl.when(n == 0)
        def _():
            dw_ref[...] = jnp.zeros_like(dw_ref)
            dvec_ref[...] = jnp.zeros_like(dvec_ref)

        dco_cur = dco[:tb]
        dvec_ref[pl.ds(0, 1), :] += jnp.sum(dco_cur, axis=0, keepdims=True)
        dvec_ref[pl.ds(1, 1), :] += jnp.sum(dz[:tb] * xhat[:tb], axis=0, keepdims=True)
        dvec_ref[pl.ds(2, 1), :] += jnp.sum(dz[:tb], axis=0, keepdims=True)
        du = jnp.zeros((tb, ch), F32)
        for k in range(CONV_WIDTH):
            du = du + w_ref[pl.ds(k, 1), :] * _rows_from(dbuf, dshift, CONV_WIDTH - 1 - k, tb)
            dw_ref[pl.ds(k, 1), :] += jnp.sum(dco_cur * _rows_from(ubuf, ushift, lead + k, tb), axis=0,
                                              keepdims=True)
        dca_ref[...] = (du * sig_b).astype(BF16)
        dcb_ref[...] = (du * cav * sig_b * (1.0 - sig_b)).astype(BF16)

    vec = _full_spec((1, ch))
    return _pallas(
        body, [ca, cb, ca, cb, co, co, ds, ds, conv_w, ln_g, ln_b], dep=dep, grid=(nblk,),
        in_specs=[cur, cur, prev, prev, cur, nxt, cur, nxt, _full_spec(conv_w.shape), vec, vec],
        out_specs=[cur, cur, _full_spec(conv_w.shape), _full_spec((SUBLANES, ch))],
        out_shape=[jax.ShapeDtypeStruct((t, ch), BF16), jax.ShapeDtypeStruct((t, ch), BF16),
                   jax.ShapeDtypeStruct(conv_w.shape, F32), jax.ShapeDtypeStruct((SUBLANES, ch), F32)],
        scratch=[pltpu.VMEM((CONV_HALO + tb, ch), F32), pltpu.VMEM((ext, ch), F32),
                 pltpu.VMEM((SUBLANES, CONV_HALO + tb, ch), F32), pltpu.VMEM((SUBLANES, ext, ch), F32)],
        sem=("arbitrary",), name="conv_bwd")


def ada_fwd(c_t, w_ada, dep=None):
    d, nc = w_ada.shape
    nex = c_t.shape[1]
    tn = _tile(nc, 512)

    def body(ct_ref, w_ref, o_ref):
        w = w_ref[...]
        ct = ct_ref[...]
        cact = ct * _sigmoid(ct)
        rows = [jnp.sum(w * cact[:, b:b + 1], axis=0, keepdims=True) for b in range(nex)]
        o_ref[...] = jnp.concatenate(rows, axis=0)

    return _pallas(
        body, [c_t, w_ada], dep=dep, grid=(nc // tn,),
        in_specs=[_full_spec(c_t.shape), pl.BlockSpec((d, tn), lambda j: (0, j))],
        out_specs=pl.BlockSpec((nex, tn), lambda j: (0, j)),
        out_shape=jax.ShapeDtypeStruct((nex, nc), F32),
        sem=("parallel",), name="ada_fwd")


def _adamw_math(w, g, m, v):
    m = ADAM_B1 * m + (1.0 - ADAM_B1) * g
    v = ADAM_B2 * v + (1.0 - ADAM_B2) * (g * g)
    m_hat = m / (1.0 - ADAM_B1 ** ADAM_STEP)
    v_hat = v / (1.0 - ADAM_B2 ** ADAM_STEP)
    delta = -ADAM_LR * (m_hat / (jnp.sqrt(v_hat) + ADAM_EPS) + ADAM_WD * w)
    return delta, m, v


def adamw(w, g, m, v, name, copy_grad=False, dep=None):
    r, n = w.shape
    tr, tn = _ew_tiles(r, n, elems=256 * 1024)
    n_out = 4 if copy_grad else 3

    def body(w_ref, g_ref, m_ref, v_ref, *outs):
        g = g_ref[...]
        if copy_grad:
            outs[0][...] = g
        outs[-3][...], outs[-2][...], outs[-1][...] = _adamw_math(w_ref[...], g, m_ref[...], v_ref[...])

    blk = pl.BlockSpec((tr, tn), lambda i, j: (i, j))
    return _pallas(
        body, [w, g, m, v], dep=dep, grid=(r // tr, n // tn),
        in_specs=[blk] * 4, out_specs=[blk] * n_out,
        out_shape=[jax.ShapeDtypeStruct((r, n), F32)] * n_out,
        sem=("parallel", "parallel"), name=name)


def ada_grad_adamw(c_t, dmod_cols, w, m, v, dep=None):
    d, nc = w.shape
    nex = c_t.shape[1]
    tr, tn = _ew_tiles(d, nc, elems=256 * 1024)

    def body(ct_ref, dm_ref, w_ref, m_ref, v_ref, g_ref, d_ref, nm_ref, nv_ref):
        ct = ct_ref[...]
        cact = ct * _sigmoid(ct)
        dm = dm_ref[...]
        g = cact[:, 0:1] * dm[0:1, :]
        for b in range(1, nex):
            g = g + cact[:, b:b + 1] * dm[b:b + 1, :]
        g_ref[...] = g
        d_ref[...], nm_ref[...], nv_ref[...] = _adamw_math(w_ref[...], g, m_ref[...], v_ref[...])

    blk = pl.BlockSpec((tr, tn), lambda i, j: (i, j))
    return _pallas(
        body, [c_t, dmod_cols, w, m, v], dep=dep, grid=(d // tr, nc // tn),
        in_specs=[pl.BlockSpec((tr, nex), lambda i, j: (i, 0)), pl.BlockSpec((nex, tn), lambda i, j: (0, j)),
                  blk, blk, blk],
        out_specs=[blk] * 4,
        out_shape=[jax.ShapeDtypeStruct((d, nc), F32)] * 4,
        sem=("parallel", "parallel"), name="ada_grad_adamw")


def _row_pack(parts):
    cols, offs, off = [], [], 0
    for p in parts:
        n = p.shape[1]
        width = -(-n // LANES) * LANES
        cols.append(jnp.pad(p, ((0, 0), (0, width - n))) if width != n else p)
        offs.append(off)
        off += width
    return jnp.concatenate(cols, axis=1), offs


def small_sum_adamw(gathered, offs, ws, ms, vs, extra_widths, dep=None):
    ndev = gathered.shape[0]
    npar = len(ws)

    def body(ga_ref, *refs):
        w_refs, m_refs, v_refs = refs[:npar], refs[npar:2 * npar], refs[2 * npar:3 * npar]
        outs = refs[3 * npar:]
        tot = ga_ref[0]
        for s in range(1, ndev):
            tot = tot + ga_ref[s]
        for i in range(npar):
            n = ws[i].shape[1]
            g = tot[:, offs[i]:offs[i] + n]
            outs[4 * i][...] = g
            outs[4 * i + 1][...], outs[4 * i + 2][...], outs[4 * i + 3][...] = _adamw_math(
                w_refs[i][...], g, m_refs[i][...], v_refs[i][...])
        for e, n in enumerate(extra_widths):
            off = offs[npar + e]
            outs[4 * npar + e][...] = tot[:, off:off + n]

    shapes = [jax.ShapeDtypeStruct(w.shape, F32) for w in ws for _ in range(4)]
    shapes += [jax.ShapeDtypeStruct((1, n), F32) for n in extra_widths]
    return _pallas(
        body, [gathered, *ws, *ms, *vs], dep=dep, in_specs=[_VMEM] * (1 + 3 * npar), out_specs=[_VMEM] * len(shapes),
        out_shape=shapes, name="small_sum_adamw")


def _position():
    return lax.axis_index("x"), lax.axis_index("y"), lax.axis_index("c")


def _other_chips(x, y):
    return [(1 - x, y), (x, 1 - y), (1 - x, 1 - y)]


def allgather_small(block, name, dep=None):
    def body(x_ref, out_ref, send_sems, recv_sems, local_sem):
        x, y, c = _position()
        me, sibling = (x, y, c), (x, y, 1 - c)
        chips = _other_chips(x, y)

        def slot(px, py, pc):
            return out_ref.at[4 * px + 2 * py + pc]

        def copy(k, block_of, to, src=None):
            return pltpu.make_async_remote_copy(
                src_ref=slot(*block_of) if src is None else src, dst_ref=slot(*block_of),
                send_sem=send_sems.at[k], recv_sem=recv_sems.at[k], device_id=to, device_id_type=MESH)

        mine = pltpu.make_async_copy(x_ref, slot(*me), local_sem)
        mine.start()
        first = [copy(0, me, sibling, src=x_ref)]
        first += [copy(1 + j, me, (*chip, c), src=x_ref) for j, chip in enumerate(chips)]
        for cp in first:
            cp.start()
        passed = [copy(4 + j, (*chip, c), sibling) for j, chip in enumerate(chips)]
        for j, chip in enumerate(chips):
            copy(1 + j, (*chip, c), me).wait_recv()
            passed[j].start()
        copy(0, sibling, me).wait_recv()
        for j, chip in enumerate(chips):
            copy(4 + j, (*chip, 1 - c), me).wait_recv()
        for cp in first + passed:
            cp.wait_send()
        mine.wait()

    return _pallas(
        body, [block], dep=dep,
        out_shape=jax.ShapeDtypeStruct((N_DEV, *block.shape), block.dtype),
        in_specs=[_VMEM], out_specs=_VMEM,
        scratch=[pltpu.SemaphoreType.DMA((7,)), pltpu.SemaphoreType.DMA((7,)), pltpu.SemaphoreType.DMA],
        name=name)


class Started(NamedTuple):
    send_sems: Any
    recv_sems: Any
    bufs: list


def exchange_start(name, bufs, n_copies, plan, dep=None):
    nb = len(bufs)

    def body(*refs):
        for cp in plan(refs[:nb], refs[nb], refs[nb + 1]):
            cp.start()

    outs = _pallas(
        body, [pltpu.with_memory_space_constraint(b, pltpu.HBM) for b in bufs], dep=dep, name=name,
        out_shape=(pltpu.SemaphoreType.DMA((n_copies,)), pltpu.SemaphoreType.DMA((n_copies,)),
                   *[pltpu.HBM(b.shape, b.dtype) for b in bufs]),
        in_specs=[_HBM] * nb,
        out_specs=(_SEM, _SEM, *[_HBM] * nb),
        input_output_aliases={i: 2 + i for i in range(nb)},
        compiler_params=pltpu.CompilerParams(has_side_effects=_EFFECT))
    return Started(outs[0], outs[1], list(outs[2:2 + nb]))


def exchange_wait(name, started, plan, bufs=None, dep=None):
    if bufs is not None:
        started = started._replace(bufs=list(bufs))
    nb = len(started.bufs)

    def body(*refs):
        for cp in plan(refs[:nb], refs[nb], refs[nb + 1]):
            cp.wait_send()
            cp.wait_recv()

    outs = _pallas(
        body, [*started.bufs, started.send_sems, started.recv_sems], dep=dep, name=name,
        out_shape=tuple(pltpu.HBM(b.shape, b.dtype) for b in started.bufs),
        in_specs=[_HBM] * nb + [_SEM, _SEM],
        out_specs=tuple([_HBM] * nb),
        input_output_aliases={i: i for i in range(nb)},
        compiler_params=pltpu.CompilerParams(has_side_effects=_EFFECT))
    return list(outs)


def _remote(src, dst, send_sems, recv_sems, i, to):
    return pltpu.make_async_remote_copy(src_ref=src, dst_ref=dst, send_sem=send_sems.at[i], recv_sem=recv_sems.at[i],
                                        device_id=to, device_id_type=MESH)


def _half_rows(buf_rows, chip_idx, pc):
    half = buf_rows // (2 * N_CHIPS)
    return pl.ds((2 * chip_idx + pc) * half, half)


ALL_PEERS = (0, 1, 2)


def plan_gather_ici(refs, send_sems, recv_sems, peers=ALL_PEERS):
    x, y, c = _position()
    chips = _other_chips(x, y)
    copies = []
    for k, ref in enumerate(refs):
        rows = ref.at[_half_rows(ref.shape[0], 2 * x + y, c), :]
        for i, j in enumerate(peers):
            copies.append(_remote(rows, rows, send_sems, recv_sems, len(peers) * k + i, (*chips[j], c)))
    return copies


def plan_gather_d2d(refs, send_sems, recv_sems, peers=ALL_PEERS):
    x, y, c = _position()
    chips = _other_chips(x, y)
    copies = []
    for k, ref in enumerate(refs):
        for i, j in enumerate(peers):
            px, py = chips[j]
            rows = ref.at[_half_rows(ref.shape[0], 2 * px + py, c), :]
            copies.append(_remote(rows, rows, send_sems, recv_sems, len(peers) * k + i, (x, y, 1 - c)))
    return copies


def plan_pair_exchange(refs, send_sems, recv_sems):
    x, y, c = _position()
    nw = len(refs) // 2
    copies = []
    for k in range(nw):
        for chip in range(N_CHIPS):
            copies.append(_remote(refs[k].at[chip, 1 - c], refs[nw + k].at[chip], send_sems, recv_sems,
                                  N_CHIPS * k + chip, (x, y, 1 - c)))
    return copies


def plan_chip_exchange(refs, send_sems, recv_sems):
    x, y, c = _position()
    nw = len(refs) // 2
    copies = []
    for k in range(nw):
        for j, (px, py) in enumerate(_other_chips(x, y)):
            copies.append(_remote(refs[k].at[2 * px + py], refs[nw + k].at[2 * x + y], send_sems, recv_sems,
                                  3 * k + j, (px, py, c)))
    return copies


def plan_pair_share(refs, send_sems, recv_sems):
    x, y, c = _position()
    return [_remote(ref.at[c], ref.at[c], send_sems, recv_sems, k, (x, y, 1 - c)) for k, ref in enumerate(refs)]


def cast_into_slot(src, slot, n_slots, name, dep=None):
    r, n = src.shape
    tr, tn = _ew_tiles(r, n, BF16_SUBLANES)

    def body(slot_ref, s_ref, o_ref):
        o_ref[...] = s_ref[...].astype(BF16)

    return _pallas(
        body, [slot, src], dep=dep, n_prefetch=1, grid=(r // tr, n // tn),
        in_specs=[pl.BlockSpec((tr, tn), lambda i, j, sl: (i, j))],
        out_specs=pl.BlockSpec((None, tr, tn), lambda i, j, sl: (sl[0], i, j)),
        out_shape=jax.ShapeDtypeStruct((n_slots, r, n), BF16),
        sem=("parallel", "parallel"), name=name)


def pair_sum(g, r, core, name, dep=None):
    nchip, _, h, n = g.shape
    th, tn = _ew_tiles(h, n, BF16_SUBLANES)

    def body(core_ref, g_ref, r_ref, o_ref):
        o_ref[...] = (g_ref[...].astype(F32) + r_ref[...].astype(F32)).astype(BF16)

    return _pallas(
        body, [core, g, r], dep=dep, n_prefetch=1, grid=(nchip, h // th, n // tn),
        in_specs=[pl.BlockSpec((None, None, th, tn), lambda a, i, j, cr: (a, cr[0], i, j)),
                  pl.BlockSpec((None, th, tn), lambda a, i, j, cr: (a, i, j))],
        out_specs=pl.BlockSpec((None, th, tn), lambda a, i, j, cr: (a, i, j)),
        out_shape=jax.ShapeDtypeStruct((nchip, h, n), BF16),
        sem=("parallel", "parallel", "parallel"), name=name)


def chip_sum(own, got, where, name, dep=None):
    nchip, h, n = got.shape
    th, tn = _ew_tiles(h, n, BF16_SUBLANES, elems=256 * 1024)

    def body(where_ref, own_ref, *rest):
        got_refs, o_ref = rest[:nchip], rest[nchip]
        chip = where_ref[0]
        acc = None
        for s in range(nchip):
            term = jnp.where(chip == s, own_ref[...], got_refs[s][...]).astype(F32)
            acc = term if acc is None else acc + term
        o_ref[...] = acc

    def got_spec(s):
        return pl.BlockSpec((None, th, tn), lambda i, j, wr: (jnp.where(wr[0] == s, (s + 1) % nchip, s), i, j))

    return _pallas(
        body, [where, own, *[got] * nchip], dep=dep, n_prefetch=1, grid=(h // th, n // tn),
        in_specs=[pl.BlockSpec((None, th, tn), lambda i, j, wr: (wr[0], i, j))]
        + [got_spec(s) for s in range(nchip)],
        out_specs=pl.BlockSpec((None, th, tn), lambda i, j, wr: (wr[1], i, j)),
        out_shape=jax.ShapeDtypeStruct((2, h, n), F32),
        sem=("parallel", "parallel"), name=name)


def kernel(x, c, w_ada, b_ada, norm_mix_g, w_in, q_norm_g, k_norm_g, attn_sinks, rel_bias, w_attn_out, conv_w, conv_b, conv_ln_g, conv_ln_b, w_conv_out, w_mix_out, norm_ffn_g, w_ffn_in, w_ffn_out, loss_target, m_w_ada, m_b_ada, m_norm_mix_g, m_w_in, m_q_norm_g, m_k_norm_g, m_attn_sinks, m_rel_bias, m_w_attn_out, m_conv_w, m_conv_b, m_conv_ln_g, m_conv_ln_b, m_w_conv_out, m_w_mix_out, m_norm_ffn_g, m_w_ffn_in, m_w_ffn_out, v_w_ada, v_b_ada, v_norm_mix_g, v_w_in, v_q_norm_g, v_k_norm_g, v_attn_sinks, v_rel_bias, v_w_attn_out, v_conv_w, v_conv_b, v_conv_ln_g, v_conv_ln_b, v_w_conv_out, v_w_mix_out, v_norm_ffn_g, v_w_ffn_in, v_w_ffn_out):
    run = InOrder()
    xi, yi, ci = _position()
    chip = 2 * xi + yi
    me = 2 * chip + ci
    chip_arr = chip.astype(jnp.int32).reshape(1)
    core_arr = ci.astype(jnp.int32).reshape(1)
    where_arr = jnp.stack([chip, ci]).astype(jnp.int32)

    xe, tgt = x[0], loss_target[0]
    t, d = xe.shape
    hd = q_norm_g.shape[-1]
    nq = attn_sinks.shape[-1]
    aw = nq * hd
    ch = conv_b.shape[-1]
    in_width = N_CHIPS * w_in.shape[-1]
    kvw = (in_width - aw - 2 * ch - 2 * d) // 2
    nkv = kvw // hd
    dff = N_CHIPS * w_ffn_out.shape[1]
    off_k, off_v, off_ca = aw, aw + kvw, aw + 2 * kvw
    off_cb, off_ga, off_gc = off_ca + ch, off_ca + 2 * ch, off_ca + 2 * ch + d
    nc_ada = w_ada.shape[-1]
    ch_loc = conv_w.shape[-1]
    nj_ffn = w_ffn_in.shape[-1]
    perm_ffn = ffn_perm(N_CHIPS)

    big = {"w_in": w_in[0], "w_attn_out": w_attn_out[0], "w_conv_out": w_conv_out[0], "w_mix_out": w_mix_out[0],
           "w_ffn_in": w_ffn_in[0], "w_ffn_out": w_ffn_out[0]}
    moments = {"w_in": (m_w_in, v_w_in), "w_attn_out": (m_w_attn_out, v_w_attn_out),
               "w_conv_out": (m_w_conv_out, v_w_conv_out), "w_mix_out": (m_w_mix_out, v_w_mix_out),
               "w_ffn_in": (m_w_ffn_in, v_w_ffn_in), "w_ffn_out": (m_w_ffn_out, v_w_ffn_out)}
    gather_groups = {"in": ["w_in"], "mid": ["w_attn_out", "w_conv_out", "w_mix_out"], "ffn_in": ["w_ffn_in"],
                     "ffn_out": ["w_ffn_out"]}
    grads, deltas, new_m, new_v = {}, {}, {}, {}

    def gather_cast(gname):
        bufs = []
        for n in gather_groups[gname]:
            r, ncol = big[n].shape
            bufs.append(run(cast_into_slot, big[n], chip_arr, N_CHIPS, "cast_" + n).reshape(N_CHIPS * r, ncol))
        return bufs

    def gather_ici_start(gname, bufs):
        return run(exchange_start, "gather_ici_start_" + gname, bufs, 3 * len(bufs), plan_gather_ici)

    def gather_pass_on(gname, ici):
        landed = run(exchange_wait, "gather_ici_wait_" + gname, ici, plan_gather_ici)
        return run(exchange_start, "gather_d2d_start_" + gname, landed, 3 * len(landed), plan_gather_d2d)

    def gathered(gname, d2d):
        outs = run(exchange_wait, "gather_d2d_wait_" + gname, d2d, plan_gather_d2d)
        return [o.reshape(N_CHIPS, *big[n].shape) for o, n in zip(outs, gather_groups[gname])]

    def rs_pair_start(gname, names, partials):
        blocks = [g.reshape(N_CHIPS, 2, big[n].shape[0] // 2, big[n].shape[1]) for n, g in zip(names, partials)]
        land = [lax.empty((N_CHIPS,) + b.shape[2:], BF16) for b in blocks]
        return run(exchange_start, "pair_exchange_start_" + gname, blocks + land, N_CHIPS * len(blocks),
                   plan_pair_exchange)

    def rs_chip_start(gname, names, pair):
        nw = len(names)
        outs = run(exchange_wait, "pair_exchange_wait_" + gname, pair, plan_pair_exchange)
        sums = [run(pair_sum, g, r, core_arr, "pair_sum_" + n) for n, g, r in zip(names, outs[:nw], outs[nw:])]
        land = [lax.empty(s.shape, BF16) for s in sums]
        return run(exchange_start, "chip_exchange_start_" + gname, sums + land, 3 * nw, plan_chip_exchange)

    def rs_share_start(gname, names, chipx):
        nw = len(names)
        outs = run(exchange_wait, "chip_exchange_wait_" + gname, chipx, plan_chip_exchange)
        halves = [run(chip_sum, s, r, where_arr, "chip_sum_" + n) for n, s, r in zip(names, outs[:nw], outs[nw:])]
        return run(exchange_start, "pair_share_start_" + gname, halves, nw, plan_pair_share)

    def rs_finish(gname, names, share):
        fulls = run(exchange_wait, "pair_share_wait_" + gname, share, plan_pair_share)
        for n, g2 in zip(names, fulls):
            g, dl, nm, nv = run(adamw, big[n], g2.reshape(big[n].shape), moments[n][0][0], moments[n][1][0],
                                "adamw_" + n, copy_grad=True)
            grads[n], deltas[n], new_m[n], new_v[n] = g[None], dl[None], nm[None], nv[None]

    near, far = (0, 1), (2,)
    plan_ici_near = functools.partial(plan_gather_ici, peers=near)
    plan_ici_far = functools.partial(plan_gather_ici, peers=far)
    plan_d2d_near = functools.partial(plan_gather_d2d, peers=near)
    plan_d2d_far = functools.partial(plan_gather_d2d, peers=far)
    bufs_in = gather_cast("in")
    row1, offs1 = _row_pack([c, conv_w[0].reshape(1, CONV_WIDTH * ch_loc)])
    got1 = run(allgather_small, row1, "allgather_cond")
    ici_near = run(exchange_start, "gather_ici_start_in_near", bufs_in, len(near), plan_ici_near)
    c_all = got1[:, 0, :d]
    conv_w_full = got1[0::2, 0, offs1[1]:offs1[1] + CONV_WIDTH * ch_loc].reshape(N_CHIPS, CONV_WIDTH, ch_loc)
    conv_w_full = jnp.transpose(conv_w_full, (1, 0, 2)).reshape(CONV_WIDTH, ch)
    conv_w_pad = jnp.pad(conv_w_full, ((0, 1), (0, 0)))
    c_t = jnp.transpose(c_all)
    mod_cols = run(ada_fwd, c_t, w_ada[0])
    rest_bufs = {gname: gather_cast(gname) for gname in ("mid", "ffn_in", "ffn_out")}
    got2 = run(allgather_small, mod_cols, "allgather_mod")
    mod_all = got2.reshape(N_CHIPS, 2, N_DEV, nc_ada)[:, 0]
    mod = lax.dynamic_slice_in_dim(mod_all, me, 1, axis=1).reshape(1, N_CHIPS * nc_ada) + b_ada
    mod = jnp.pad(mod.reshape(N_MOD, d), ((0, SUBLANES - N_MOD), (0, 0)))

    h = run(pre_mix_fwd, xe, mod, norm_mix_g)
    bucket = _t5_bucket_table()
    bucket_p, bucket_c = jnp.asarray(bucket[:, :BLOCK]), jnp.asarray(bucket[:, BLOCK:])
    bias_p, bias_c = run(bias_table, rel_bias, bucket_p, bucket_c)

    def in_blocks(buf):
        return buf.reshape(N_CHIPS, *big["w_in"].shape)

    def chip_ids(peers):
        others = [2 * (1 - xi) + yi, 2 * xi + (1 - yi), 2 * (1 - xi) + (1 - yi)]
        return jnp.stack([others[j] for j in peers]).astype(jnp.int32)

    tn_in = big["w_in"].shape[1]
    landed = run(exchange_wait, "gather_ici_wait_in_near", ici_near, plan_ici_near)
    ici_far = run(exchange_start, "gather_ici_start_in_far", landed, len(far), plan_ici_far)
    ici = {gname: gather_ici_start(gname, bufs) for gname, bufs in rest_bufs.items()}
    d2d_near = run(exchange_start, "gather_d2d_start_in_near", ici_far.bufs, len(near), plan_d2d_near)
    p = run(mm_nn_blocks, h, in_blocks(d2d_near.bufs[0]), chip_arr, None, tn=tn_in, out_dtype=BF16, name="mm_in_own")
    landed = run(exchange_wait, "gather_d2d_wait_in_near", d2d_near, plan_d2d_near)
    p = run(mm_nn_blocks, h, in_blocks(landed[0]), chip_ids(near), p, tn=tn_in, out_dtype=BF16, name="mm_in_near")
    landed = run(exchange_wait, "gather_ici_wait_in_far", ici_far, plan_ici_far, bufs=landed)
    d2d_far = run(exchange_start, "gather_d2d_start_in_far", landed, len(far), plan_d2d_far)
    landed = run(exchange_wait, "gather_d2d_wait_in_far", d2d_far, plan_d2d_far)
    wg_in = in_blocks(landed[0])
    p = run(mm_nn_blocks, h, wg_in, chip_ids(far), p, tn=tn_in, out_dtype=BF16, name="mm_in_far")
    d2d_mid = gather_pass_on("mid", ici["mid"])

    sinks3 = attn_sinks.reshape(nq, 1, 1)
    attn_o = run(attn_fwd, p, bias_p, bias_c, sinks3, q_norm_g, k_norm_g, aw=aw, kvw=kvw)
    ca, cb = p[:, off_ca:off_cb], p[:, off_cb:off_ga]
    s_conv, co_conv = run(conv_fwd, ca, cb, conv_w_pad, conv_b, conv_ln_g, conv_ln_b)
    wg_attn_out, wg_conv_out, wg_mix_out = gathered("mid", d2d_mid)
    wg_mix_out = wg_mix_out.reshape(1, d, d)
    y_attn = run(mm_nn, attn_o, wg_attn_out, tn=_tile(wg_attn_out.shape[2], 512), tk=aw, out_dtype=BF16,
                 name="mm_attn_out")
    y_conv = run(mm_nn, s_conv, wg_conv_out, tn=_tile(wg_conv_out.shape[2], 512), tk=ch, out_dtype=BF16,
                 name="mm_conv_out")
    merged = run(merge_fwd, p, y_attn, y_conv, off_ga, off_gc)
    d2d_ffn_in = gather_pass_on("ffn_in", ici["ffn_in"])
    o_m = run(mm_nn, merged, wg_mix_out, tn=_tile(d, 512), tk=d, out_dtype=F32, name="mm_mix_out")
    x1, h2 = run(pre_ffn_fwd, xe, o_m, mod, norm_ffn_g)
    (wg_ffn_in,) = gathered("ffn_in", d2d_ffn_in)
    f = run(mm_nn, h2, wg_ffn_in, tn=_tile(nj_ffn, 1408), tk=d, out_dtype=BF16, name="mm_ffn_in", perm=perm_ffn)
    d2d_ffn_out = gather_pass_on("ffn_out", ici["ffn_out"])
    act = run(swiglu_fwd, f, nj_ffn)
    (wg_ffn_out,) = gathered("ffn_out", d2d_ffn_out)
    wg_ffn_out = wg_ffn_out.reshape(1, dff, d)
    o_f = run(mm_nn, act, wg_ffn_out, tn=_tile(d, 512), tk=_tile(dff, 2816), out_dtype=F32, name="mm_ffn_out")
    loss11, dy, dof, acc_l = run(loss_head, x1, o_f, tgt, mod)

    gw_ffn_out = run(mm_tn, act, dof, 1, tk=_tile(dff, 512), tn=d, name="mm_ffn_out_dw")
    px_ffn_out = rs_pair_start("ffn_out", ["w_ffn_out"], [gw_ffn_out])
    dact = run(mm_nt, dof, wg_ffn_out, tko=_tile(dff, 512), tn=d, out_dtype=BF16, name="mm_ffn_out_dx")
    cx_ffn_out = rs_chip_start("ffn_out", ["w_ffn_out"], px_ffn_out)
    df = run(swiglu_bwd, f, dact, nj_ffn)
    gw_ffn_in = run(mm_tn, h2, df, N_CHIPS, tk=d, tn=_tile(nj_ffn, 1408), name="mm_ffn_in_dw",
                    perm=perm_ffn)
    px_ffn_in = rs_pair_start("ffn_in", ["w_ffn_in"], [gw_ffn_in])
    dh2 = run(mm_nt, df, wg_ffn_in, tko=_tile(d, 512), tn=nj_ffn, name="mm_ffn_in_dx", perm=perm_ffn)
    sh_ffn_out = rs_share_start("ffn_out", ["w_ffn_out"], cx_ffn_out)
    cx_ffn_in = rs_chip_start("ffn_in", ["w_ffn_in"], px_ffn_in)
    dx1, dom, acc_f = run(pre_ffn_bwd, x1, dh2, dy, o_m, mod, norm_ffn_g)
    gw_mix_out = run(mm_tn, merged, dom, 1, tk=d, tn=_tile(d, 1024), name="mm_mix_out_dw")
    px_mix = rs_pair_start("mix_out", ["w_mix_out"], [gw_mix_out])
    dmerged = run(mm_nt, dom, wg_mix_out, tko=_tile(d, 512), tn=d, out_dtype=BF16, name="mm_mix_out_dx")
    dy_attn, dy_conv, dga, dgc = run(merge_bwd, p, y_attn, y_conv, dmerged, off_ga, off_gc)
    rs_finish("ffn_out", ["w_ffn_out"], sh_ffn_out)
    cx_mix = rs_chip_start("mix_out", ["w_mix_out"], px_mix)
    gw_attn_out = run(mm_tn, attn_o, dy_attn, N_CHIPS, tk=aw, tn=_tile(wg_attn_out.shape[2], 512),
                      name="mm_attn_out_dw")
    gw_conv_out = run(mm_tn, s_conv, dy_conv, N_CHIPS, tk=ch, tn=_tile(wg_conv_out.shape[2], 512),
                      name="mm_conv_out_dw")
    ac_names = ["w_attn_out", "w_conv_out"]
    px_ac = rs_pair_start("attn_conv_out", ac_names, [gw_attn_out, gw_conv_out])
    dattn_o = run(mm_nt, dy_attn, wg_attn_out, tko=_tile(aw, 1024), tn=_tile(wg_attn_out.shape[2], 512),
                  out_dtype=BF16, name="mm_attn_out_dx")
    ds_conv = run(mm_nt, dy_conv, wg_conv_out, tko=_tile(ch, 1024), tn=_tile(wg_conv_out.shape[2], 512),
                  out_dtype=BF16, name="mm_conv_out_dx")
    cx_ac = rs_chip_start("attn_conv_out", ac_names, px_ac)
    dca, dcb, dconv_w, dconv_vec = run(conv_bwd, ca, cb, co_conv, ds_conv, conv_w_pad, conv_ln_g, conv_ln_b)
    sh_ffn_in = rs_share_start("ffn_in", ["w_ffn_in"], cx_ffn_in)
    dqkv, dbp, dbc, dsinks, dqg, dkg = run(attn_bwd, p, bias_p, bias_c, sinks3, q_norm_g, k_norm_g, dattn_o,
                                           aw=aw, kvw=kvw)
    sh_mix = rs_share_start("mix_out", ["w_mix_out"], cx_mix)
    sh_ac = rs_share_start("attn_conv_out", ac_names, cx_ac)
    drel = run(bias_table_bwd, dbp, dbc, bucket_p, bucket_c).reshape(NUM_BUCKETS, nq)
    dp = jnp.concatenate([dqkv, dca, dcb, dga, dgc], axis=1)
    gw_in = run(mm_tn, h, dp, N_CHIPS, tk=d, tn=wg_in.shape[2], name="mm_in_dw")
    px_in = rs_pair_start("in", ["w_in"], [gw_in])
    dh = run(mm_nt, dp, wg_in, tko=_tile(d, 1024), tn=wg_in.shape[2], name="mm_in_dx")
    grad_x, acc_m = run(pre_mix_bwd, xe, dh, dx1, mod, norm_mix_g)

    dmod = jnp.concatenate([acc_m[0:1], acc_m[1:2], acc_f[3:4], acc_f[0:1], acc_f[1:2], acc_l[0:1]], axis=1)
    small_names = ["b_ada", "norm_mix_g", "q_norm_g", "k_norm_g", "attn_sinks", "rel_bias", "conv_b", "conv_ln_g",
                   "conv_ln_b", "norm_ffn_g"]
    small_w = [b_ada, norm_mix_g, q_norm_g, k_norm_g, attn_sinks, rel_bias, conv_b, conv_ln_g, conv_ln_b, norm_ffn_g]
    small_m = [m_b_ada, m_norm_mix_g, m_q_norm_g, m_k_norm_g, m_attn_sinks, m_rel_bias, m_conv_b, m_conv_ln_g,
               m_conv_ln_b, m_norm_ffn_g]
    small_v = [v_b_ada, v_norm_mix_g, v_q_norm_g, v_k_norm_g, v_attn_sinks, v_rel_bias, v_conv_b, v_conv_ln_g,
               v_conv_ln_b, v_norm_ffn_g]
    small_g = [dmod, acc_m[2:3], dqg, dkg, dsinks.reshape(1, nq), drel.reshape(1, NUM_BUCKETS * nq),
               dconv_vec[0:1], dconv_vec[1:2], dconv_vec[2:3], acc_f[2:3]]
    row3, offs3 = _row_pack(small_g + [dconv_w[:CONV_WIDTH].reshape(1, CONV_WIDTH * ch), loss11])
    got3 = run(allgather_small, row3, "allgather_small_grads")
    cx_in = rs_chip_start("in", ["w_in"], px_in)
    as_row = lambda a: a.reshape(1, -1)
    outs3 = run(small_sum_adamw, got3, offs3, [as_row(a) for a in small_w], [as_row(a) for a in small_m],
                [as_row(a) for a in small_v], [CONV_WIDTH * ch, 1])
    for i, (n, w) in enumerate(zip(small_names, small_w)):
        grads[n], deltas[n], new_m[n], new_v[n] = (o.reshape(w.shape) for o in outs3[4 * i:4 * i + 4])
    g_conv_w_all, loss_sum = outs3[-2].reshape(CONV_WIDTH, ch), outs3[-1]

    g_conv_w = lax.dynamic_slice_in_dim(g_conv_w_all, chip * ch_loc, ch_loc, axis=1)
    grads["conv_w"] = g_conv_w[None]
    dl, nm, nv = run(adamw, conv_w[0], g_conv_w, m_conv_w[0], v_conv_w[0], "adamw_conv_w")
    deltas["conv_w"], new_m["conv_w"], new_v["conv_w"] = dl[None], nm[None], nv[None]

    dmod_all = got3[:, 0, :N_MOD * d]
    dmod_cols = lax.dynamic_slice_in_dim(dmod_all, chip * nc_ada, nc_ada, axis=1)
    g_ada, dl, nm, nv = run(ada_grad_adamw, c_t, dmod_cols, w_ada[0], m_w_ada[0], v_w_ada[0])
    grads["w_ada"], deltas["w_ada"], new_m["w_ada"], new_v["w_ada"] = g_ada[None], dl[None], nm[None], nv[None]

    rs_finish("ffn_in", ["w_ffn_in"], sh_ffn_in)
    rs_finish("mix_out", ["w_mix_out"], sh_mix)
    rs_finish("attn_conv_out", ac_names, sh_ac)
    sh_in = rs_share_start("in", ["w_in"], cx_in)
    rs_finish("in", ["w_in"], sh_in)

    loss = loss_sum[0, 0]
    order = ["w_ada", "b_ada", "norm_mix_g", "w_in", "q_norm_g", "k_norm_g", "attn_sinks", "rel_bias", "w_attn_out",
             "conv_w", "conv_b", "conv_ln_g", "conv_ln_b", "w_conv_out", "w_mix_out", "norm_ffn_g", "w_ffn_in",
             "w_ffn_out"]
    return (loss, grad_x[None], *[grads[n] for n in order], *[deltas[n] for n in order],
            *[new_m[n] for n in order], *[new_v[n] for n in order])
```

```python
import functools
import math
from typing import Any, NamedTuple

import jax
import jax.numpy as jnp
import numpy as np
from jax import lax
from jax.experimental import pallas as pl
from jax.experimental.pallas import tpu as pltpu

F32 = jnp.float32
BF16 = jnp.bfloat16
MESH = pl.DeviceIdType.MESH

V7X_VMEM_BYTES = 64 * 1024 * 1024
VMEM_LIMIT = V7X_VMEM_BYTES - 8 * 1024 * 1024
LANES = 128
SUBLANES = 8
BF16_SUBLANES = 16

EPS = 1e-6
WINDOW = 128
BLOCK = 128
NUM_BUCKETS = 32
MAX_EXACT = NUM_BUCKETS // 2
MAX_DISTANCE = 128
CONV_WIDTH = 31
CONV_HALO = 32
ADAM_LR = 0.001
ADAM_B1 = 0.9
ADAM_B2 = 0.999
ADAM_EPS = 1e-08
ADAM_WD = 0.01
ADAM_STEP = 10
N_MOD = 6
SH_M, SC_M, GT_M, SH_F, SC_F, GT_F = range(6)

N_CHIPS = 4
N_DEV = 8

_ANY = pl.BlockSpec(memory_space=pl.ANY)
_VMEM = pl.BlockSpec(memory_space=pltpu.VMEM)
_SMEM = pl.BlockSpec(memory_space=pltpu.SMEM)
_HBM = pl.BlockSpec(memory_space=pltpu.HBM)
_SEM = pl.BlockSpec(memory_space=pltpu.SEMAPHORE)
_EFFECT = pltpu.SideEffectType.DATAFLOW_SIDE_EFFECTING


class InOrder:
    def __init__(self):
        self.token = None

    def __call__(self, fn, *args, **kw):
        return fn(*args, dep=self, **kw)


def _pallas(body, args, *, in_specs, out_specs, out_shape, name, dep=None, grid=(), n_prefetch=0, scratch=(),
            sem=None, **kw):
    n_lead = n_prefetch + len(in_specs)
    in_specs, args = list(in_specs), list(args)
    single = not isinstance(out_shape, (list, tuple))
    out_shapes = [out_shape] if single else list(out_shape)
    out_specs = [out_specs] if single else list(out_specs)
    if dep is not None:
        inner, n_out, takes = body, len(out_shapes), dep.token is not None

        def body(*refs):
            rest = refs[n_lead + (1 if takes else 0):]
            rest[n_out][...] = jnp.zeros((SUBLANES, LANES), F32)
            return inner(*refs[:n_lead], *rest[:n_out], *rest[n_out + 1:])

        if takes:
            in_specs.append(_ANY)
            args.append(dep.token)
        out_shapes.append(jax.ShapeDtypeStruct((SUBLANES, LANES), F32))
        out_specs.append(pl.BlockSpec((SUBLANES, LANES), lambda *_: (0, 0)))
    params = kw.pop("compiler_params", None)
    if params is None:
        params = pltpu.CompilerParams(dimension_semantics=sem, vmem_limit_bytes=VMEM_LIMIT)
    outs = pl.pallas_call(
        body,
        grid_spec=pltpu.PrefetchScalarGridSpec(num_scalar_prefetch=n_prefetch, grid=grid, in_specs=in_specs,
                                               out_specs=out_specs, scratch_shapes=list(scratch)),
        out_shape=out_shapes, compiler_params=params, name=name, **kw,
    )(*args)
    if dep is not None:
        dep.token = outs[-1]
        outs = outs[:-1]
    return outs[0] if single else list(outs)


def _tile(n, pref, unit=LANES):
    best = None
    for t in range(unit, min(n, pref) + 1, unit):
        if n % t == 0:
            best = t
    return best if best is not None else n


def _sigmoid(v):
    return 1.0 / (1.0 + jnp.exp(-v.astype(F32)))


ROW_CHUNK = 512


def _row_chunks(m, unit=SUBLANES):
    step = _tile(m, ROW_CHUNK, unit)
    return [(s, step) for s in range(0, m, step)]


def _ew_tiles(r, n, unit=SUBLANES, elems=512 * 1024):
    return _tile(r, max(unit, elems // n), unit), n


def _block_pos(j, perm):
    if perm is None:
        return j
    pos = 0
    for a, p in enumerate(perm):
        pos = pos + jnp.where(j == a, p, 0)
    return pos


def mm_nn(a, w, *, tn, tk, out_dtype, name, perm=None, dep=None):
    m, k = a.shape
    j, k2, nj = w.shape
    assert k == k2 and nj % tn == 0 and k % tk == 0
    npj, nk = nj // tn, k // tk

    def body(a_ref, w_ref, o_ref, *scratch):
        kk = pl.program_id(1)
        for s, sz in _row_chunks(m):
            rows = pl.ds(s, sz)
            p = jnp.dot(a_ref[rows, :], w_ref[...], preferred_element_type=F32)
            if nk == 1:
                o_ref[rows, :] = p.astype(out_dtype)
            else:
                acc = scratch[0]

                @pl.when(kk == 0)
                def _():
                    acc[rows, :] = p

                @pl.when(kk > 0)
                def _():
                    acc[rows, :] += p

                @pl.when(kk == nk - 1)
                def _():
                    o_ref[rows, :] = acc[rows, :].astype(out_dtype)

    return _pallas(
        body, [a, w], dep=dep, grid=(j * npj, nk),
        in_specs=[
            pl.BlockSpec((m, tk), lambda n, kk: (0, kk)),
            pl.BlockSpec((None, tk, tn), lambda n, kk: (n // npj, kk, n % npj)),
        ],
        out_specs=pl.BlockSpec((m, tn), lambda n, kk: (0, _block_pos(n // npj, perm) * npj + n % npj)),
        out_shape=jax.ShapeDtypeStruct((m, j * nj), out_dtype),
        scratch=[pltpu.VMEM((m, tn), F32)] if nk > 1 else [],
        sem=("parallel", "arbitrary"), name=name)


def mm_nn_blocks(a, w, blocks, into, *, tn, out_dtype, name, dep=None):
    m, k = a.shape
    j, k2, nj = w.shape
    assert k == k2 and nj % tn == 0
    npj = nj // tn
    n_in = 2 if into is None else 3

    def body(blocks_ref, a_ref, w_ref, *rest):
        o_ref = rest[n_in - 2]
        for s, sz in _row_chunks(m):
            rows = pl.ds(s, sz)
            o_ref[rows, :] = jnp.dot(a_ref[rows, :], w_ref[...], preferred_element_type=F32).astype(out_dtype)

    return _pallas(
        body, [blocks, a, w] + ([] if into is None else [into]), dep=dep, n_prefetch=1,
        grid=(blocks.shape[0] * npj,),
        in_specs=[pl.BlockSpec((m, k), lambda n, bl: (0, 0)),
                  pl.BlockSpec((None, k, tn), lambda n, bl: (bl[n // npj], 0, n % npj))]
        + ([] if into is None else [_ANY]),
        out_specs=pl.BlockSpec((m, tn), lambda n, bl: (0, bl[n // npj] * npj + n % npj)),
        out_shape=jax.ShapeDtypeStruct((m, j * nj), out_dtype),
        input_output_aliases={} if into is None else {3: 0},
        sem=("arbitrary",), name=name)


def mm_nt(g, w, *, tko, tn, name, out_dtype=F32, perm=None, dep=None):
    m, n = g.shape
    j, k, nj = w.shape
    assert n == j * nj and nj % tn == 0 and k % tko == 0
    npj, nr = nj // tn, n // tn
    in_place = out_dtype == F32

    def body(g_ref, w_ref, o_ref, *scratch):
        r = pl.program_id(1)
        acc = o_ref if in_place else (scratch[0] if nr > 1 else None)
        for s, sz in _row_chunks(m):
            rows = pl.ds(s, sz)
            p = lax.dot_general(g_ref[rows, :], w_ref[...], (((1,), (1,)), ((), ())), preferred_element_type=F32)
            if acc is None:
                o_ref[rows, :] = p.astype(out_dtype)
                continue

            @pl.when(r == 0)
            def _():
                acc[rows, :] = p

            @pl.when(r > 0)
            def _():
                acc[rows, :] += p

            if not in_place:
                @pl.when(r == nr - 1)
                def _():
                    o_ref[rows, :] = acc[rows, :].astype(out_dtype)

    return _pallas(
        body, [g, w], dep=dep, grid=(k // tko, nr),
        in_specs=[
            pl.BlockSpec((m, tn), lambda ko, r: (0, _block_pos(r // npj, perm) * npj + r % npj)),
            pl.BlockSpec((None, tko, tn), lambda ko, r: (r // npj, ko, r % npj)),
        ],
        out_specs=pl.BlockSpec((m, tko), lambda ko, r: (0, ko)),
        out_shape=jax.ShapeDtypeStruct((m, k), out_dtype),
        scratch=[pltpu.VMEM((m, tko), F32)] if (nr > 1 and not in_place) else [],
        sem=("parallel", "arbitrary"), name=name)


def mm_tn(a, g, n_blocks, *, tk, tn, name, perm=None, dep=None):
    m, k = a.shape
    m2, n = g.shape
    nj = n // n_blocks
    assert m == m2 and nj % tn == 0 and k % tk == 0
    npj = nj // tn

    def body(a_ref, g_ref, o_ref):
        for s, sz in _row_chunks(tk, LANES):
            p = lax.dot_general(a_ref[:, pl.ds(s, sz)], g_ref[...], (((0,), (0,)), ((), ())),
                                preferred_element_type=F32)
            o_ref[pl.ds(s, sz), :] = p.astype(BF16)

    return _pallas(
        body, [a, g], dep=dep, grid=(k // tk, n // tn),
        in_specs=[
            pl.BlockSpec((m, tk), lambda kk, nn: (0, kk)),
            pl.BlockSpec((m, tn), lambda kk, nn: (0, _block_pos(nn // npj, perm) * npj + nn % npj)),
        ],
        out_specs=pl.BlockSpec((None, tk, tn), lambda kk, nn: (nn // npj, kk, nn % npj)),
        out_shape=jax.ShapeDtypeStruct((n_blocks, k, nj), BF16),
        sem=("parallel", "parallel"), name=name)


ROW_TILE = 256


def _row_spec(tr, width):
    return pl.BlockSpec((tr, width), lambda i: (i, 0))


def _full_spec(shape):
    return pl.BlockSpec(shape, lambda *_: (0,) * len(shape))


def _rms(xv):
    return lax.rsqrt(jnp.mean(xv * xv, axis=-1, keepdims=True) + EPS)


def _mod_row(mod_ref, row):
    return mod_ref[pl.ds(row, 1), :]


def pre_mix_fwd(x, mod, gain, dep=None):
    t, d = x.shape
    tr = _tile(t, ROW_TILE, SUBLANES)

    def body(x_ref, mod_ref, g_ref, h_ref):
        xv = x_ref[...]
        y = xv * _rms(xv) * g_ref[...]
        h_ref[...] = (y * (1.0 + _mod_row(mod_ref, SC_M)) + _mod_row(mod_ref, SH_M)).astype(BF16)

    return _pallas(
        body, [x, mod, gain], dep=dep, grid=(t // tr,),
        in_specs=[_row_spec(tr, d), _full_spec(mod.shape), _full_spec(gain.shape)],
        out_specs=_row_spec(tr, d),
        out_shape=jax.ShapeDtypeStruct((t, d), BF16),
        sem=("parallel",), name="pre_mix_fwd")


def pre_ffn_fwd(x, o_m, mod, gain, dep=None):
    t, d = x.shape
    tr = _tile(t, ROW_TILE, SUBLANES)

    def body(x_ref, om_ref, mod_ref, g_ref, x1_ref, h_ref):
        x1 = x_ref[...] + _mod_row(mod_ref, GT_M) * om_ref[...]
        x1_ref[...] = x1
        y = x1 * _rms(x1) * g_ref[...]
        h_ref[...] = (y * (1.0 + _mod_row(mod_ref, SC_F)) + _mod_row(mod_ref, SH_F)).astype(BF16)

    return _pallas(
        body, [x, o_m, mod, gain], dep=dep, grid=(t // tr,),
        in_specs=[_row_spec(tr, d), _row_spec(tr, d), _full_spec(mod.shape), _full_spec(gain.shape)],
        out_specs=[_row_spec(tr, d), _row_spec(tr, d)],
        out_shape=[jax.ShapeDtypeStruct((t, d), F32), jax.ShapeDtypeStruct((t, d), BF16)],
        sem=("parallel",), name="pre_ffn_fwd")


def loss_head(x1, o_f, target, mod, dep=None):
    t, d = x1.shape
    tr = _tile(t, ROW_TILE, SUBLANES)

    def body(x1_ref, of_ref, tg_ref, mod_ref, loss_ref, dy_ref, dof_ref, acc_ref):
        i = pl.program_id(0)
        gt = _mod_row(mod_ref, GT_F)
        of = of_ref[...]
        err = x1_ref[...] + gt * of - tg_ref[...]
        dy = err * (1.0 / d)
        dy_ref[...] = dy
        dof_ref[...] = (dy * gt).astype(BF16)
        part = (0.5 / d) * jnp.sum(jnp.sum(err * err, axis=1, keepdims=True), axis=0, keepdims=True)
        dgt = jnp.sum(dy * of, axis=0, keepdims=True)

        @pl.when(i == 0)
        def _():
            loss_ref[...] = jnp.zeros_like(loss_ref)
            acc_ref[...] = jnp.zeros_like(acc_ref)

        loss_ref[...] += part
        acc_ref[pl.ds(0, 1), :] += dgt

    return _pallas(
        body, [x1, o_f, target, mod], dep=dep, grid=(t // tr,),
        in_specs=[_row_spec(tr, d), _row_spec(tr, d), _row_spec(tr, d), _full_spec(mod.shape)],
        out_specs=[_full_spec((1, 1)), _row_spec(tr, d), _row_spec(tr, d), _full_spec((SUBLANES, d))],
        out_shape=[jax.ShapeDtypeStruct((1, 1), F32), jax.ShapeDtypeStruct((t, d), F32),
                   jax.ShapeDtypeStruct((t, d), BF16), jax.ShapeDtypeStruct((SUBLANES, d), F32)],
        sem=("arbitrary",), name="loss_head")


def _norm_bwd(xv, dh, sc, gain):
    rstd = _rms(xv)
    yn = xv * rstd
    dsh = jnp.sum(dh, axis=0, keepdims=True)
    dsc = jnp.sum(dh * (yn * gain), axis=0, keepdims=True)
    dgain = jnp.sum(dh * (1.0 + sc) * yn, axis=0, keepdims=True)
    dyn = dh * ((1.0 + sc) * gain)
    dx = rstd * (dyn - yn * jnp.mean(dyn * yn, axis=-1, keepdims=True))
    return dx, dsh, dsc, dgain


def pre_ffn_bwd(x1, dh2, dy, o_m, mod, gain, dep=None):
    t, d = x1.shape
    tr = _tile(t, ROW_TILE, SUBLANES)

    def body(x1_ref, dh_ref, dy_ref, om_ref, mod_ref, g_ref, dx1_ref, dom_ref, acc_ref):
        i = pl.program_id(0)
        dxn, dsh, dsc, dgain = _norm_bwd(x1_ref[...], dh_ref[...], _mod_row(mod_ref, SC_F), g_ref[...])
        dx1 = dy_ref[...] + dxn
        dx1_ref[...] = dx1
        dom_ref[...] = (dx1 * _mod_row(mod_ref, GT_M)).astype(BF16)
        dgt = jnp.sum(dx1 * om_ref[...], axis=0, keepdims=True)

        @pl.when(i == 0)
        def _():
            acc_ref[...] = jnp.zeros_like(acc_ref)

        acc_ref[pl.ds(0, 1), :] += dsh
        acc_ref[pl.ds(1, 1), :] += dsc
        acc_ref[pl.ds(2, 1), :] += dgain
        acc_ref[pl.ds(3, 1), :] += dgt

    return _pallas(
        body, [x1, dh2, dy, o_m, mod, gain], dep=dep, grid=(t // tr,),
        in_specs=[_row_spec(tr, d)] * 4 + [_full_spec(mod.shape), _full_spec(gain.shape)],
        out_specs=[_row_spec(tr, d), _row_spec(tr, d), _full_spec((SUBLANES, d))],
        out_shape=[jax.ShapeDtypeStruct((t, d), F32), jax.ShapeDtypeStruct((t, d), BF16),
                   jax.ShapeDtypeStruct((SUBLANES, d), F32)],
        sem=("arbitrary",), name="pre_ffn_bwd")


def pre_mix_bwd(x, dh, dx1, mod, gain, dep=None):
    t, d = x.shape
    tr = _tile(t, ROW_TILE, SUBLANES)

    def body(x_ref, dh_ref, dx1_ref, mod_ref, g_ref, gx_ref, acc_ref):
        i = pl.program_id(0)
        dxn, dsh, dsc, dgain = _norm_bwd(x_ref[...], dh_ref[...], _mod_row(mod_ref, SC_M), g_ref[...])
        gx_ref[...] = dx1_ref[...] + dxn

        @pl.when(i == 0)
        def _():
            acc_ref[...] = jnp.zeros_like(acc_ref)

        acc_ref[pl.ds(0, 1), :] += dsh
        acc_ref[pl.ds(1, 1), :] += dsc
        acc_ref[pl.ds(2, 1), :] += dgain

    return _pallas(
        body, [x, dh, dx1, mod, gain], dep=dep, grid=(t // tr,),
        in_specs=[_row_spec(tr, d)] * 3 + [_full_spec(mod.shape), _full_spec(gain.shape)],
        out_specs=[_row_spec(tr, d), _full_spec((SUBLANES, d))],
        out_shape=[jax.ShapeDtypeStruct((t, d), F32), jax.ShapeDtypeStruct((SUBLANES, d), F32)],
        sem=("arbitrary",), name="pre_mix_bwd")


def merge_fwd(p, y_attn, y_conv, off_ga, off_gc, dep=None):
    t, d = y_attn.shape
    tr = _tile(t, ROW_TILE, SUBLANES)
    cw = math.gcd(math.gcd(off_ga, off_gc), math.gcd(d, 512))
    nc = d // cw

    def body(ga_ref, gc_ref, ya_ref, yc_ref, o_ref):
        o_ref[...] = (_sigmoid(ga_ref[...]) * ya_ref[...] + _sigmoid(gc_ref[...]) * yc_ref[...]).astype(BF16)

    return _pallas(
        body, [p, p, y_attn, y_conv], dep=dep, grid=(t // tr, nc),
        in_specs=[pl.BlockSpec((tr, cw), lambda i, j: (i, off_ga // cw + j)),
                  pl.BlockSpec((tr, cw), lambda i, j: (i, off_gc // cw + j)),
                  pl.BlockSpec((tr, cw), lambda i, j: (i, j)),
                  pl.BlockSpec((tr, cw), lambda i, j: (i, j))],
        out_specs=pl.BlockSpec((tr, cw), lambda i, j: (i, j)),
        out_shape=jax.ShapeDtypeStruct((t, d), BF16),
        sem=("parallel", "parallel"), name="merge_fwd")


def merge_bwd(p, y_attn, y_conv, dmerged, off_ga, off_gc, dep=None):
    t, d = y_attn.shape
    tr = _tile(t, ROW_TILE, SUBLANES)
    cw = math.gcd(math.gcd(off_ga, off_gc), math.gcd(d, 512))
    nc = d // cw

    def body(ga_ref, gc_ref, ya_ref, yc_ref, dm_ref, dya_ref, dyc_ref, dga_ref, dgc_ref):
        dm = dm_ref[...].astype(F32)
        sa = _sigmoid(ga_ref[...])
        sc = _sigmoid(gc_ref[...])
        dya_ref[...] = (dm * sa).astype(BF16)
        dyc_ref[...] = (dm * sc).astype(BF16)
        dga_ref[...] = (dm * ya_ref[...] * sa * (1.0 - sa)).astype(BF16)
        dgc_ref[...] = (dm * yc_ref[...] * sc * (1.0 - sc)).astype(BF16)

    blk = pl.BlockSpec((tr, cw), lambda i, j: (i, j))
    return _pallas(
        body, [p, p, y_attn, y_conv, dmerged], dep=dep, grid=(t // tr, nc),
        in_specs=[pl.BlockSpec((tr, cw), lambda i, j: (i, off_ga // cw + j)),
                  pl.BlockSpec((tr, cw), lambda i, j: (i, off_gc // cw + j)), blk, blk, blk],
        out_specs=[blk] * 4,
        out_shape=[jax.ShapeDtypeStruct((t, d), BF16)] * 4,
        sem=("parallel", "parallel"), name="merge_bwd")


def ffn_perm(n_blocks):
    half = n_blocks // 2
    return tuple(2 * j if j < half else 2 * (j - half) + 1 for j in range(n_blocks))


def swiglu_fwd(f, nj, dep=None):
    t, two = f.shape
    tr = _tile(t, ROW_TILE, SUBLANES)
    npair = two // (2 * nj)

    def body(f_ref, o_ref):
        g = f_ref[:, :nj].astype(F32)
        u = f_ref[:, nj:].astype(F32)
        o_ref[...] = (g * _sigmoid(g) * u).astype(BF16)

    return _pallas(
        body, [f], dep=dep, grid=(t // tr, npair),
        in_specs=[pl.BlockSpec((tr, 2 * nj), lambda i, j: (i, j))],
        out_specs=pl.BlockSpec((tr, nj), lambda i, j: (i, j)),
        out_shape=jax.ShapeDtypeStruct((t, two // 2), BF16),
        sem=("parallel", "parallel"), name="swiglu_fwd")


def swiglu_bwd(f, dact, nj, dep=None):
    t, two = f.shape
    tr = _tile(t, ROW_TILE, SUBLANES)
    npair = two // (2 * nj)

    def body(f_ref, da_ref, o_ref):
        g = f_ref[:, :nj].astype(F32)
        u = f_ref[:, nj:].astype(F32)
        da = da_ref[...]
        s = _sigmoid(g)
        o_ref[:, :nj] = (da * u * (s * (1.0 + g * (1.0 - s)))).astype(BF16)
        o_ref[:, nj:] = (da * (g * s)).astype(BF16)

    return _pallas(
        body, [f, dact], dep=dep, grid=(t // tr, npair),
        in_specs=[pl.BlockSpec((tr, 2 * nj), lambda i, j: (i, j)), pl.BlockSpec((tr, nj), lambda i, j: (i, j))],
        out_specs=pl.BlockSpec((tr, 2 * nj), lambda i, j: (i, j)),
        out_shape=jax.ShapeDtypeStruct((t, two), BF16),
        sem=("parallel", "parallel"), name="swiglu_bwd")


def _t5_bucket_table():
    q_off = np.arange(BLOCK)
    k_off = np.arange(2 * BLOCK)
    dist = q_off[:, None] + BLOCK - k_off[None, :]
    n = np.maximum(dist, 0)
    nf = np.maximum(n, 1).astype(np.float32)
    large = MAX_EXACT + (np.log(nf / np.float32(MAX_EXACT)) / np.float32(math.log(MAX_DISTANCE / MAX_EXACT))
                         * np.float32(NUM_BUCKETS - MAX_EXACT)).astype(np.int32)
    large = np.minimum(large, NUM_BUCKETS - 1)
    bucket = np.where(n < MAX_EXACT, n, large).astype(np.int32)
    allowed = (dist >= 0) & (dist < WINDOW)
    return np.where(allowed, bucket, -1).astype(np.int32)


def bias_table(rel_bias, bucket_p, bucket_c, dep=None):
    nb, nq = rel_bias.shape

    def body(rb_ref, bkp_ref, bkc_ref, op_ref, oc_ref):
        for bk_ref, o_ref in ((bkp_ref, op_ref), (bkc_ref, oc_ref)):
            bk = bk_ref[...]
            for h in range(nq):
                acc = jnp.full(bk.shape, -jnp.inf, F32)
                for b in range(nb):
                    acc = jnp.where(bk == b, rb_ref[b, h], acc)
                o_ref[h] = acc

    return _pallas(
        body, [rel_bias, bucket_p, bucket_c], dep=dep,
        in_specs=[_SMEM, _VMEM, _VMEM], out_specs=[_VMEM, _VMEM],
        out_shape=[jax.ShapeDtypeStruct((nq,) + bucket_p.shape, F32)] * 2,
        name="bias_table")


def bias_table_bwd(dbp, dbc, bucket_p, bucket_c, dep=None):
    nq = dbp.shape[0]

    def body(dbp_ref, dbc_ref, bkp_ref, bkc_ref, o_ref):
        bkp, bkc = bkp_ref[...][None], bkc_ref[...][None]
        dp, dc = dbp_ref[...], dbc_ref[...]
        for b in range(NUM_BUCKETS):
            sel = jnp.where(bkp == b, dp, 0.0) + jnp.where(bkc == b, dc, 0.0)
            o_ref[b] = jnp.sum(jnp.sum(sel, axis=2, keepdims=True), axis=1, keepdims=True)

    return _pallas(
        body, [dbp, dbc, bucket_p, bucket_c], dep=dep,
        in_specs=[_VMEM] * 4, out_specs=_VMEM,
        out_shape=jax.ShapeDtypeStruct((NUM_BUCKETS, nq, 1, 1), F32),
        name="bias_table_bwd")


_NT = (((1,), (1,)), ((), ()))
_TN = (((0,), (0,)), ((), ()))


@jax.custom_vjp
def _bdot_nt(a, b):
    return lax.dot_general(a.astype(BF16), b.astype(BF16), _NT, preferred_element_type=F32)


def _bdot_nt_fwd(a, b):
    return _bdot_nt(a, b), (a, b)


def _bdot_nt_bwd(res, g):
    a, b = res
    gb = g.astype(BF16)
    da = jnp.dot(gb, b.astype(BF16), preferred_element_type=F32)
    db = lax.dot_general(gb, a.astype(BF16), _TN, preferred_element_type=F32)
    return da, db


_bdot_nt.defvjp(_bdot_nt_fwd, _bdot_nt_bwd)


@jax.custom_vjp
def _bdot_nn(a, b):
    return jnp.dot(a.astype(BF16), b.astype(BF16), preferred_element_type=F32)


def _bdot_nn_fwd(a, b):
    return _bdot_nn(a, b), (a, b)


def _bdot_nn_bwd(res, g):
    a, b = res
    gb = g.astype(BF16)
    da = lax.dot_general(gb, b.astype(BF16), _NT, preferred_element_type=F32)
    db = lax.dot_general(a.astype(BF16), gb, _TN, preferred_element_type=F32)
    return da, db


_bdot_nn.defvjp(_bdot_nn_fwd, _bdot_nn_bwd)


def _attn_math(q4, kp, kc, vp, vc, bp, bc, sink4, qg, kg, *, prev_ok, scale):
    g, b, hd = q4.shape
    q = q4.reshape(g * b, hd)
    qn = q * _rms(q) * qg
    kpn = kp * _rms(kp) * kg
    kcn = kc * _rms(kc) * kg
    lp = _bdot_nt(qn, kpn).reshape(g, b, b) * scale + bp
    lc = _bdot_nt(qn, kcn).reshape(g, b, b) * scale + bc
    lp = jnp.where(prev_ok, lp, -jnp.inf)
    m = jnp.maximum(jnp.maximum(jnp.max(lp, axis=-1, keepdims=True), jnp.max(lc, axis=-1, keepdims=True)), sink4)
    m = lax.stop_gradient(m)
    pp = jnp.exp(lp - m)
    pc = jnp.exp(lc - m)
    den = jnp.sum(pp, axis=-1, keepdims=True) + jnp.sum(pc, axis=-1, keepdims=True) + jnp.exp(sink4 - m)
    inv = 1.0 / den
    out = _bdot_nn((pp * inv).reshape(g * b, b), vp) + _bdot_nn((pc * inv).reshape(g * b, b), vc)
    return out.reshape(g, b, hd)


def _attn_specs(p, aw, kvw, nq, hd, nblk, reverse):
    assert aw % (2 * kvw) == 0
    kv_col = aw // (2 * kvw)

    def blk(n):
        return nblk - 1 - n if reverse else n

    return [
        pl.BlockSpec((BLOCK, aw), lambda n: (blk(n), 0)),
        pl.BlockSpec((BLOCK, 2 * kvw), lambda n: (jnp.maximum(blk(n) - 1, 0), kv_col)),
        pl.BlockSpec((BLOCK, 2 * kvw), lambda n: (blk(n), kv_col)),
        _full_spec((nq, BLOCK, BLOCK)), _full_spec((nq, BLOCK, BLOCK)), _full_spec((nq, 1, 1)),
        _full_spec((1, hd)), _full_spec((1, hd)),
    ]


def _attn_head_inputs(h, grp, hd, kvw, q_ref, kvp_ref, kvc_ref, bp_ref, bc_ref, s_ref):
    heads = pl.ds(grp * h, grp)
    q4 = jnp.stack([q_ref[:, pl.ds((grp * h + g) * hd, hd)].astype(F32) for g in range(grp)])
    k_cols, v_cols = pl.ds(h * hd, hd), pl.ds(kvw + h * hd, hd)
    kv = [r[:, cols].astype(F32) for cols in (k_cols, v_cols) for r in (kvp_ref, kvc_ref)]
    return (q4, *kv, bp_ref[heads], bc_ref[heads], s_ref[heads])


def attn_fwd(p, bias_p, bias_c, sinks, qg, kg, *, aw, kvw, dep=None):
    t, hd = p.shape[0], qg.shape[-1]
    nq, nkv, nblk = aw // hd, kvw // hd, t // BLOCK
    grp = nq // nkv
    scale = hd ** -0.5

    def body(q_ref, kvp_ref, kvc_ref, bp_ref, bc_ref, s_ref, qg_ref, kg_ref, o_ref):
        prev_ok = pl.program_id(0) > 0
        for h in range(nkv):
            args = _attn_head_inputs(h, grp, hd, kvw, q_ref, kvp_ref, kvc_ref, bp_ref, bc_ref, s_ref)
            out = _attn_math(*args, qg_ref[...], kg_ref[...], prev_ok=prev_ok, scale=scale)
            for g in range(grp):
                o_ref[:, pl.ds((grp * h + g) * hd, hd)] = out[g].astype(BF16)

    return _pallas(
        body, [p, p, p, bias_p, bias_c, sinks, qg, kg], dep=dep, grid=(nblk,),
        in_specs=_attn_specs(p, aw, kvw, nq, hd, nblk, False),
        out_specs=pl.BlockSpec((BLOCK, aw), lambda n: (n, 0)),
        out_shape=jax.ShapeDtypeStruct((t, aw), BF16),
        sem=("parallel",), name="attn_fwd")


def attn_bwd(p, bias_p, bias_c, sinks, qg, kg, do, *, aw, kvw, dep=None):
    t, hd = p.shape[0], qg.shape[-1]
    nq, nkv, nblk = aw // hd, kvw // hd, t // BLOCK
    grp = nq // nkv
    scale = hd ** -0.5

    def body(q_ref, kvp_ref, kvc_ref, bp_ref, bc_ref, s_ref, qg_ref, kg_ref, do_ref,
             dqkv_ref, dbp_ref, dbc_ref, ds_ref, dqg_ref, dkg_ref, carry):
        i = pl.program_id(0)
        prev_ok = (nblk - 1 - i) > 0

        @pl.when(i == 0)
        def _():
            carry[...] = jnp.zeros_like(carry)
            dbp_ref[...] = jnp.zeros_like(dbp_ref)
            dbc_ref[...] = jnp.zeros_like(dbc_ref)
            ds_ref[...] = jnp.zeros_like(ds_ref)
            dqg_ref[...] = jnp.zeros_like(dqg_ref)
            dkg_ref[...] = jnp.zeros_like(dkg_ref)

        fn = functools.partial(_attn_math, prev_ok=prev_ok, scale=scale)
        for h in range(nkv):
            args = _attn_head_inputs(h, grp, hd, kvw, q_ref, kvp_ref, kvc_ref, bp_ref, bc_ref, s_ref)
            _, vjp = jax.vjp(fn, *args, qg_ref[...], kg_ref[...])
            do4 = jnp.stack([do_ref[:, pl.ds((grp * h + g) * hd, hd)].astype(F32) for g in range(grp)])
            dq, dkp, dkc, dvp, dvc, dbp, dbc, dsk, dqg, dkg = vjp(do4)
            for g in range(grp):
                dqkv_ref[:, pl.ds((grp * h + g) * hd, hd)] = dq[g].astype(BF16)
            k_cols, v_cols = pl.ds(h * hd, hd), pl.ds(kvw + h * hd, hd)
            dqkv_ref[:, pl.ds(aw + h * hd, hd)] = (dkc + carry[:, k_cols]).astype(BF16)
            dqkv_ref[:, pl.ds(aw + kvw + h * hd, hd)] = (dvc + carry[:, v_cols]).astype(BF16)
            carry[:, k_cols] = dkp
            carry[:, v_cols] = dvp
            heads = pl.ds(grp * h, grp)
            dbp_ref[heads] += dbp
            dbc_ref[heads] += dbc
            ds_ref[heads] += dsk
            dqg_ref[...] += dqg
            dkg_ref[...] += dkg

    return _pallas(
        body, [p, p, p, bias_p, bias_c, sinks, qg, kg, do], dep=dep, grid=(nblk,),
        in_specs=_attn_specs(p, aw, kvw, nq, hd, nblk, True)
        + [pl.BlockSpec((BLOCK, aw), lambda n: (nblk - 1 - n, 0))],
        out_specs=[
            pl.BlockSpec((BLOCK, aw + 2 * kvw), lambda n: (nblk - 1 - n, 0)),
            _full_spec((nq, BLOCK, BLOCK)), _full_spec((nq, BLOCK, BLOCK)), _full_spec((nq, 1, 1)),
            _full_spec((1, hd)), _full_spec((1, hd)),
        ],
        out_shape=[
            jax.ShapeDtypeStruct((t, aw + 2 * kvw), BF16),
            jax.ShapeDtypeStruct((nq, BLOCK, BLOCK), F32),
            jax.ShapeDtypeStruct((nq, BLOCK, BLOCK), F32),
            jax.ShapeDtypeStruct((nq, 1, 1), F32),
            jax.ShapeDtypeStruct((1, hd), F32),
            jax.ShapeDtypeStruct((1, hd), F32),
        ],
        scratch=[pltpu.VMEM((BLOCK, 2 * kvw), F32)],
        sem=("arbitrary",), name="attn_bwd")


CONV_TILE = 256


def _conv_halo_specs(tb, ch, nblk):
    per = tb // CONV_HALO
    last = nblk * per - 1
    cur = pl.BlockSpec((tb, ch), lambda n: (n, 0))
    prev = pl.BlockSpec((CONV_HALO, ch), lambda n: (jnp.maximum(n * per - 1, 0), 0))
    nxt = pl.BlockSpec((CONV_HALO, ch), lambda n: (jnp.minimum((n + 1) * per, last), 0))
    return cur, prev, nxt


def _ln_silu(co, ln_g, ln_b):
    mu = jnp.mean(co, axis=-1, keepdims=True)
    cen = co - mu
    rstd = lax.rsqrt(jnp.mean(cen * cen, axis=-1, keepdims=True) + EPS)
    xhat = cen * rstd
    z = xhat * ln_g + ln_b
    return xhat, rstd, z


def _shifted_copies(src, shifted):
    rows = src.shape[0] - SUBLANES
    for r in range(1, SUBLANES):
        shifted[r, pl.ds(0, rows), :] = src[pl.ds(r, rows), :]


def _rows_from(src, shifted, start, n):
    r = start % SUBLANES
    if r == 0:
        return src[pl.ds(start, n), :]
    return shifted[r, pl.ds(start - r, n), :]


def conv_fwd(ca, cb, conv_w, conv_b, ln_g, ln_b, dep=None):
    t, ch = ca.shape
    tb = _tile(t, CONV_TILE, CONV_HALO)
    nblk = t // tb
    cur, prev, _ = _conv_halo_specs(tb, ch, nblk)
    lead = CONV_HALO - (CONV_WIDTH - 1)

    def body(ca_ref, cb_ref, cap_ref, cbp_ref, w_ref, b_ref, g_ref, bb_ref, s_ref, co_ref, ubuf, ushift):
        n = pl.program_id(0)
        halo = cap_ref[...] * _sigmoid(cbp_ref[...])
        ubuf[pl.ds(0, CONV_HALO), :] = jnp.where(n > 0, halo, 0.0)
        ubuf[pl.ds(CONV_HALO, tb), :] = ca_ref[...] * _sigmoid(cb_ref[...])
        _shifted_copies(ubuf, ushift)
        acc = jnp.broadcast_to(b_ref[...], (tb, ch))
        for k in range(CONV_WIDTH):
            acc = acc + w_ref[pl.ds(k, 1), :] * _rows_from(ubuf, ushift, lead + k, tb)
        co_ref[...] = acc
        _, _, z = _ln_silu(acc, g_ref[...], bb_ref[...])
        s_ref[...] = (z * _sigmoid(z)).astype(BF16)

    vec = _full_spec((1, ch))
    return _pallas(
        body, [ca, cb, ca, cb, conv_w, conv_b, ln_g, ln_b], dep=dep, grid=(nblk,),
        in_specs=[cur, cur, prev, prev, _full_spec(conv_w.shape), vec, vec, vec],
        out_specs=[cur, cur],
        out_shape=[jax.ShapeDtypeStruct((t, ch), BF16), jax.ShapeDtypeStruct((t, ch), F32)],
        scratch=[pltpu.VMEM((CONV_HALO + tb, ch), F32), pltpu.VMEM((SUBLANES, CONV_HALO + tb, ch), F32)],
        sem=("parallel",), name="conv_fwd")


def conv_bwd(ca, cb, co, ds, conv_w, ln_g, ln_b, dep=None):
    t, ch = ca.shape
    tb = _tile(t, CONV_TILE, CONV_HALO)
    nblk = t // tb
    cur, prev, nxt = _conv_halo_specs(tb, ch, nblk)
    lead = CONV_HALO - (CONV_WIDTH - 1)
    ext = tb + CONV_HALO

    def body(ca_ref, cb_ref, cap_ref, cbp_ref, co_ref, con_ref, ds_ref, dsn_ref, w_ref, g_ref, bb_ref,
             dca_ref, dcb_ref, dw_ref, dvec_ref, ubuf, dbuf, ushift, dshift):
        n = pl.program_id(0)
        is_last = n == nblk - 1
        sig_b = _sigmoid(cb_ref[...])
        cav = ca_ref[...].astype(F32)
        ubuf[pl.ds(0, CONV_HALO), :] = jnp.where(n > 0, cap_ref[...] * _sigmoid(cbp_ref[...]), 0.0)
        ubuf[pl.ds(CONV_HALO, tb), :] = cav * sig_b
        _shifted_copies(ubuf, ushift)
        co = jnp.concatenate([co_ref[...], con_ref[...]], axis=0)
        xhat, rstd, z = _ln_silu(co, g_ref[...], bb_ref[...])
        dsv = jnp.concatenate([ds_ref[...].astype(F32), jnp.where(is_last, 0.0, dsn_ref[...].astype(F32))], axis=0)
        sg = _sigmoid(z)
        dz = dsv * (sg * (1.0 + z * (1.0 - sg)))
        dxh = dz * g_ref[...]
        dco = rstd * (dxh - jnp.mean(dxh, axis=-1, keepdims=True)
                      - xhat * jnp.mean(dxh * xhat, axis=-1, keepdims=True))
        dbuf[...] = dco
        _shifted_copies(dbuf, dshift)

        @pl.when(n == 0)
        def _():
            dw_ref[...] = jnp.zeros_like(dw_ref)
            dvec_ref[...] = jnp.zeros_like(dvec_ref)

        dco_cur = dco[:tb]
        dvec_ref[pl.ds(0, 1), :] += jnp.sum(dco_cur, axis=0, keepdims=True)
        dvec_ref[pl.ds(1, 1), :] += jnp.sum(dz[:tb] * xhat[:tb], axis=0, keepdims=True)
        dvec_ref[pl.ds(2, 1), :] += jnp.sum(dz[:tb], axis=0, keepdims=True)
        du = jnp.zeros((tb, ch), F32)
        for k in range(CONV_WIDTH):
            du = du + w_ref[pl.ds(k, 1), :] * _rows_from(dbuf, dshift, CONV_WIDTH - 1 - k, tb)
            dw_ref[pl.ds(k, 1), :] += jnp.sum(dco_cur * _rows_from(ubuf, ushift, lead + k, tb), axis=0,
                                              keepdims=True)
        dca_ref[...] = (du * sig_b).astype(BF16)
        dcb_ref[...] = (du * cav * sig_b * (1.0 - sig_b)).astype(BF16)

    vec = _full_spec((1, ch))
    return _pallas(
        body, [ca, cb, ca, cb, co, co, ds, ds, conv_w, ln_g, ln_b], dep=dep, grid=(nblk,),
        in_specs=[cur, cur, prev, prev, cur, nxt, cur, nxt, _full_spec(conv_w.shape), vec, vec],
        out_specs=[cur, cur, _full_spec(conv_w.shape), _full_spec((SUBLANES, ch))],
        out_shape=[jax.ShapeDtypeStruct((t, ch), BF16), jax.ShapeDtypeStruct((t, ch), BF16),
                   jax.ShapeDtypeStruct(conv_w.shape, F32), jax.ShapeDtypeStruct((SUBLANES, ch), F32)],
        scratch=[pltpu.VMEM((CONV_HALO + tb, ch), F32), pltpu.VMEM((ext, ch), F32),
                 pltpu.VMEM((SUBLANES, CONV_HALO + tb, ch), F32), pltpu.VMEM((SUBLANES, ext, ch), F32)],
        sem=("arbitrary",), name="conv_bwd")


def ada_fwd(c_t, w_ada, dep=None):
    d, nc = w_ada.shape
    nex = c_t.shape[1]
    tn = _tile(nc, 512)

    def body(ct_ref, w_ref, o_ref):
        w = w_ref[...]
        ct = ct_ref[...]
        cact = ct * _sigmoid(ct)
        rows = [jnp.sum(w * cact[:, b:b + 1], axis=0, keepdims=True) for b in range(nex)]
        o_ref[...] = jnp.concatenate(rows, axis=0)

    return _pallas(
        body, [c_t, w_ada], dep=dep, grid=(nc // tn,),
        in_specs=[_full_spec(c_t.shape), pl.BlockSpec((d, tn), lambda j: (0, j))],
        out_specs=pl.BlockSpec((nex, tn), lambda j: (0, j)),
        out_shape=jax.ShapeDtypeStruct((nex, nc), F32),
        sem=("parallel",), name="ada_fwd")


def _adamw_math(w, g, m, v):
    m = ADAM_B1 * m + (1.0 - ADAM_B1) * g
    v = ADAM_B2 * v + (1.0 - ADAM_B2) * (g * g)
    m_hat = m / (1.0 - ADAM_B1 ** ADAM_STEP)
    v_hat = v / (1.0 - ADAM_B2 ** ADAM_STEP)
    delta = -ADAM_LR * (m_hat / (jnp.sqrt(v_hat) + ADAM_EPS) + ADAM_WD * w)
    return delta, m, v


def adamw(w, g, m, v, name, copy_grad=False, dep=None):
    r, n = w.shape
    tr, tn = _ew_tiles(r, n, elems=256 * 1024)
    n_out = 4 if copy_grad else 3

    def body(w_ref, g_ref, m_ref, v_ref, *outs):
        g = g_ref[...]
        if copy_grad:
            outs[0][...] = g
        outs[-3][...], outs[-2][...], outs[-1][...] = _adamw_math(w_ref[...], g, m_ref[...], v_ref[...])

    blk = pl.BlockSpec((tr, tn), lambda i, j: (i, j))
    return _pallas(
        body, [w, g, m, v], dep=dep, grid=(r // tr, n // tn),
        in_specs=[blk] * 4, out_specs=[blk] * n_out,
        out_shape=[jax.ShapeDtypeStruct((r, n), F32)] * n_out,
        sem=("parallel", "parallel"), name=name)


def ada_grad_adamw(c_t, dmod_cols, w, m, v, dep=None):
    d, nc = w.shape
    nex = c_t.shape[1]
    tr, tn = _ew_tiles(d, nc, elems=256 * 1024)

    def body(ct_ref, dm_ref, w_ref, m_ref, v_ref, g_ref, d_ref, nm_ref, nv_ref):
        ct = ct_ref[...]
        cact = ct * _sigmoid(ct)
        dm = dm_ref[...]
        g = cact[:, 0:1] * dm[0:1, :]
        for b in range(1, nex):
            g = g + cact[:, b:b + 1] * dm[b:b + 1, :]
        g_ref[...] = g
        d_ref[...], nm_ref[...], nv_ref[...] = _adamw_math(w_ref[...], g, m_ref[...], v_ref[...])

    blk = pl.BlockSpec((tr, tn), lambda i, j: (i, j))
    return _pallas(
        body, [c_t, dmod_cols, w, m, v], dep=dep, grid=(d // tr, nc // tn),
        in_specs=[pl.BlockSpec((tr, nex), lambda i, j: (i, 0)), pl.BlockSpec((nex, tn), lambda i, j: (0, j)),
                  blk, blk, blk],
        out_specs=[blk] * 4,
        out_shape=[jax.ShapeDtypeStruct((d, nc), F32)] * 4,
        sem=("parallel", "parallel"), name="ada_grad_adamw")


def _row_pack(parts):
    cols, offs, off = [], [], 0
    for p in parts:
        n = p.shape[1]
        width = -(-n // LANES) * LANES
        cols.append(jnp.pad(p, ((0, 0), (0, width - n))) if width != n else p)
        offs.append(off)
        off += width
    return jnp.concatenate(cols, axis=1), offs


def small_sum_adamw(gathered, offs, ws, ms, vs, extra_widths, dep=None):
    ndev = gathered.shape[0]
    npar = len(ws)

    def body(ga_ref, *refs):
        w_refs, m_refs, v_refs = refs[:npar], refs[npar:2 * npar], refs[2 * npar:3 * npar]
        outs = refs[3 * npar:]
        tot = ga_ref[0]
        for s in range(1, ndev):
            tot = tot + ga_ref[s]
        for i in range(npar):
            n = ws[i].shape[1]
            g = tot[:, offs[i]:offs[i] + n]
            outs[4 * i][...] = g
            outs[4 * i + 1][...], outs[4 * i + 2][...], outs[4 * i + 3][...] = _adamw_math(
                w_refs[i][...], g, m_refs[i][...], v_refs[i][...])
        for e, n in enumerate(extra_widths):
            off = offs[npar + e]
            outs[4 * npar + e][...] = tot[:, off:off + n]

    shapes = [jax.ShapeDtypeStruct(w.shape, F32) for w in ws for _ in range(4)]
    shapes += [jax.ShapeDtypeStruct((1, n), F32) for n in extra_widths]
    return _pallas(
        body, [gathered, *ws, *ms, *vs], dep=dep, in_specs=[_VMEM] * (1 + 3 * npar), out_specs=[_VMEM] * len(shapes),
        out_shape=shapes, name="small_sum_adamw")


def _position():
    return lax.axis_index("x"), lax.axis_index("y"), lax.axis_index("c")


def _other_chips(x, y):
    return [(1 - x, y), (x, 1 - y), (1 - x, 1 - y)]


def allgather_small(block, name, dep=None):
    def body(x_ref, out_ref, send_sems, recv_sems, local_sem):
        x, y, c = _position()
        me, sibling = (x, y, c), (x, y, 1 - c)
        chips = _other_chips(x, y)

        def slot(px, py, pc):
            return out_ref.at[4 * px + 2 * py + pc]

        def copy(k, block_of, to, src=None):
            return pltpu.make_async_remote_copy(
                src_ref=slot(*block_of) if src is None else src, dst_ref=slot(*block_of),
                send_sem=send_sems.at[k], recv_sem=recv_sems.at[k], device_id=to, device_id_type=MESH)

        mine = pltpu.make_async_copy(x_ref, slot(*me), local_sem)
        mine.start()
        first = [copy(0, me, sibling, src=x_ref)]
        first += [copy(1 + j, me, (*chip, c), src=x_ref) for j, chip in enumerate(chips)]
        for cp in first:
            cp.start()
        passed = [copy(4 + j, (*chip, c), sibling) for j, chip in enumerate(chips)]
        for j, chip in enumerate(chips):
            copy(1 + j, (*chip, c), me).wait_recv()
            passed[j].start()
        copy(0, sibling, me).wait_recv()
        for j, chip in enumerate(chips):
            copy(4 + j, (*chip, 1 - c), me).wait_recv()
        for cp in first + passed:
            cp.wait_send()
        mine.wait()

    return _pallas(
        body, [block], dep=dep,
        out_shape=jax.ShapeDtypeStruct((N_DEV, *block.shape), block.dtype),
        in_specs=[_VMEM], out_specs=_VMEM,
        scratch=[pltpu.SemaphoreType.DMA((7,)), pltpu.SemaphoreType.DMA((7,)), pltpu.SemaphoreType.DMA],
        name=name)


class Started(NamedTuple):
    send_sems: Any
    recv_sems: Any
    bufs: list


def exchange_start(name, bufs, n_copies, plan, dep=None):
    nb = len(bufs)

    def body(*refs):
        for cp in plan(refs[:nb], refs[nb], refs[nb + 1]):
            cp.start()

    outs = _pallas(
        body, [pltpu.with_memory_space_constraint(b, pltpu.HBM) for b in bufs], dep=dep, name=name,
        out_shape=(pltpu.SemaphoreType.DMA((n_copies,)), pltpu.SemaphoreType.DMA((n_copies,)),
                   *[pltpu.HBM(b.shape, b.dtype) for b in bufs]),
        in_specs=[_HBM] * nb,
        out_specs=(_SEM, _SEM, *[_HBM] * nb),
        input_output_aliases={i: 2 + i for i in range(nb)},
        compiler_params=pltpu.CompilerParams(has_side_effects=_EFFECT))
    return Started(outs[0], outs[1], list(outs[2:2 + nb]))


def exchange_wait(name, started, plan, bufs=None, dep=None):
    if bufs is not None:
        started = started._replace(bufs=list(bufs))
    nb = len(started.bufs)

    def body(*refs):
        for cp in plan(refs[:nb], refs[nb], refs[nb + 1]):
            cp.wait_send()
            cp.wait_recv()

    outs = _pallas(
        body, [*started.bufs, started.send_sems, started.recv_sems], dep=dep, name=name,
        out_shape=tuple(pltpu.HBM(b.shape, b.dtype) for b in started.bufs),
        in_specs=[_HBM] * nb + [_SEM, _SEM],
        out_specs=tuple([_HBM] * nb),
        input_output_aliases={i: i for i in range(nb)},
        compiler_params=pltpu.CompilerParams(has_side_effects=_EFFECT))
    return list(outs)


def _remote(src, dst, send_sems, recv_sems, i, to):
    return pltpu.make_async_remote_copy(src_ref=src, dst_ref=dst, send_sem=send_sems.at[i], recv_sem=recv_sems.at[i],
                                        device_id=to, device_id_type=MESH)


def _half_rows(buf_rows, chip_idx, pc):
    half = buf_rows // (2 * N_CHIPS)
    return pl.ds((2 * chip_idx + pc) * half, half)


ALL_PEERS = (0, 1, 2)


def plan_gather_ici(refs, send_sems, recv_sems, peers=ALL_PEERS):
    x, y, c = _position()
    chips = _other_chips(x, y)
    copies = []
    for k, ref in enumerate(refs):
        rows = ref.at[_half_rows(ref.shape[0], 2 * x + y, c), :]
        for i, j in enumerate(peers):
            copies.append(_remote(rows, rows, send_sems, recv_sems, len(peers) * k + i, (*chips[j], c)))
    return copies


def plan_gather_d2d(refs, send_sems, recv_sems, peers=ALL_PEERS):
    x, y, c = _position()
    chips = _other_chips(x, y)
    copies = []
    for k, ref in enumerate(refs):
        for i, j in enumerate(peers):
            px, py = chips[j]
            rows = ref.at[_half_rows(ref.shape[0], 2 * px + py, c), :]
            copies.append(_remote(rows, rows, send_sems, recv_sems, len(peers) * k + i, (x, y, 1 - c)))
    return copies


def plan_pair_exchange(refs, send_sems, recv_sems):
    x, y, c = _position()
    nw = len(refs) // 2
    copies = []
    for k in range(nw):
        for chip in range(N_CHIPS):
            copies.append(_remote(refs[k].at[chip, 1 - c], refs[nw + k].at[chip], send_sems, recv_sems,
                                  N_CHIPS * k + chip, (x, y, 1 - c)))
    return copies


def plan_chip_exchange(refs, send_sems, recv_sems):
    x, y, c = _position()
    nw = len(refs) // 2
    copies = []
    for k in range(nw):
        for j, (px, py) in enumerate(_other_chips(x, y)):
            copies.append(_remote(refs[k].at[2 * px + py], refs[nw + k].at[2 * x + y], send_sems, recv_sems,
                                  3 * k + j, (px, py, c)))
    return copies


def plan_pair_share(refs, send_sems, recv_sems):
    x, y, c = _position()
    return [_remote(ref.at[c], ref.at[c], send_sems, recv_sems, k, (x, y, 1 - c)) for k, ref in enumerate(refs)]


def cast_into_slot(src, slot, n_slots, name, dep=None):
    r, n = src.shape
    tr, tn = _ew_tiles(r, n, BF16_SUBLANES)

    def body(slot_ref, s_ref, o_ref):
        o_ref[...] = s_ref[...].astype(BF16)

    return _pallas(
        body, [slot, src], dep=dep, n_prefetch=1, grid=(r // tr, n // tn),
        in_specs=[pl.BlockSpec((tr, tn), lambda i, j, sl: (i, j))],
        out_specs=pl.BlockSpec((None, tr, tn), lambda i, j, sl: (sl[0], i, j)),
        out_shape=jax.ShapeDtypeStruct((n_slots, r, n), BF16),
        sem=("parallel", "parallel"), name=name)


def pair_sum(g, r, core, name, dep=None):
    nchip, _, h, n = g.shape
    th, tn = _ew_tiles(h, n, BF16_SUBLANES)

    def body(core_ref, g_ref, r_ref, o_ref):
        o_ref[...] = (g_ref[...].astype(F32) + r_ref[...].astype(F32)).astype(BF16)

    return _pallas(
        body, [core, g, r], dep=dep, n_prefetch=1, grid=(nchip, h // th, n // tn),
        in_specs=[pl.BlockSpec((None, None, th, tn), lambda a, i, j, cr: (a, cr[0], i, j)),
                  pl.BlockSpec((None, th, tn), lambda a, i, j, cr: (a, i, j))],
        out_specs=pl.BlockSpec((None, th, tn), lambda a, i, j, cr: (a, i, j)),
        out_shape=jax.ShapeDtypeStruct((nchip, h, n), BF16),
        sem=("parallel", "parallel", "parallel"), name=name)


def chip_sum(own, got, where, name, dep=None):
    nchip, h, n = got.shape
    th, tn = _ew_tiles(h, n, BF16_SUBLANES, elems=256 * 1024)

    def body(where_ref, own_ref, *rest):
        got_refs, o_ref = rest[:nchip], rest[nchip]
        chip = where_ref[0]
        acc = None
        for s in range(nchip):
            term = jnp.where(chip == s, own_ref[...], got_refs[s][...]).astype(F32)
            acc = term if acc is None else acc + term
        o_ref[...] = acc

    def got_spec(s):
        return pl.BlockSpec((None, th, tn), lambda i, j, wr: (jnp.where(wr[0] == s, (s + 1) % nchip, s), i, j))

    return _pallas(
        body, [where, own, *[got] * nchip], dep=dep, n_prefetch=1, grid=(h // th, n // tn),
        in_specs=[pl.BlockSpec((None, th, tn), lambda i, j, wr: (wr[0], i, j))]
        + [got_spec(s) for s in range(nchip)],
        out_specs=pl.BlockSpec((None, th, tn), lambda i, j, wr: (wr[1], i, j)),
        out_shape=jax.ShapeDtypeStruct((2, h, n), F32),
        sem=("parallel", "parallel"), name=name)


def kernel(x, c, w_ada, b_ada, norm_mix_g, w_in, q_norm_g, k_norm_g, attn_sinks, rel_bias, w_attn_out, conv_w, conv_b, conv_ln_g, conv_ln_b, w_conv_out, w_mix_out, norm_ffn_g, w_ffn_in, w_ffn_out, loss_target, m_w_ada, m_b_ada, m_norm_mix_g, m_w_in, m_q_norm_g, m_k_norm_g, m_attn_sinks, m_rel_bias, m_w_attn_out, m_conv_w, m_conv_b, m_conv_ln_g, m_conv_ln_b, m_w_conv_out, m_w_mix_out, m_norm_ffn_g, m_w_ffn_in, m_w_ffn_out, v_w_ada, v_b_ada, v_norm_mix_g, v_w_in, v_q_norm_g, v_k_norm_g, v_attn_sinks, v_rel_bias, v_w_attn_out, v_conv_w, v_conv_b, v_conv_ln_g, v_conv_ln_b, v_w_conv_out, v_w_mix_out, v_norm_ffn_g, v_w_ffn_in, v_w_ffn_out):
    run = InOrder()
    xi, yi, ci = _position()
    chip = 2 * xi + yi
    me = 2 * chip + ci
    chip_arr = chip.astype(jnp.int32).reshape(1)
    core_arr = ci.astype(jnp.int32).reshape(1)
    where_arr = jnp.stack([chip, ci]).astype(jnp.int32)

    xe, tgt = x[0], loss_target[0]
    t, d = xe.shape
    hd = q_norm_g.shape[-1]
    nq = attn_sinks.shape[-1]
    aw = nq * hd
    ch = conv_b.shape[-1]
    in_width = N_CHIPS * w_in.shape[-1]
    kvw = (in_width - aw - 2 * ch - 2 * d) // 2
    nkv = kvw // hd
    dff = N_CHIPS * w_ffn_out.shape[1]
    off_k, off_v, off_ca = aw, aw + kvw, aw + 2 * kvw
    off_cb, off_ga, off_gc = off_ca + ch, off_ca + 2 * ch, off_ca + 2 * ch + d
    nc_ada = w_ada.shape[-1]
    ch_loc = conv_w.shape[-1]
    nj_ffn = w_ffn_in.shape[-1]
    perm_ffn = ffn_perm(N_CHIPS)

    big = {"w_in": w_in[0], "w_attn_out": w_attn_out[0], "w_conv_out": w_conv_out[0], "w_mix_out": w_mix_out[0],
           "w_ffn_in": w_ffn_in[0], "w_ffn_out": w_ffn_out[0]}
    moments = {"w_in": (m_w_in, v_w_in), "w_attn_out": (m_w_attn_out, v_w_attn_out),
               "w_conv_out": (m_w_conv_out, v_w_conv_out), "w_mix_out": (m_w_mix_out, v_w_mix_out),
               "w_ffn_in": (m_w_ffn_in, v_w_ffn_in), "w_ffn_out": (m_w_ffn_out, v_w_ffn_out)}
    gather_groups = {"in": ["w_in"], "branch_out": ["w_attn_out", "w_conv_out"], "ffn_in": ["w_ffn_in"],
                     "mix_out": ["w_mix_out"], "ffn_out": ["w_ffn_out"]}
    grads, deltas, new_m, new_v = {}, {}, {}, {}

    def gather_cast(gname):
        bufs = []
        for n in gather_groups[gname]:
            r, ncol = big[n].shape
            bufs.append(run(cast_into_slot, big[n], chip_arr, N_CHIPS, "cast_" + n).reshape(N_CHIPS * r, ncol))
        return bufs

    def gather_ici_start(gname, bufs):
        return run(exchange_start, "gather_ici_start_" + gname, bufs, 3 * len(bufs), plan_gather_ici)

    def gather_pass_on(gname, ici):
        landed = run(exchange_wait, "gather_ici_wait_" + gname, ici, plan_gather_ici)
        return run(exchange_start, "gather_d2d_start_" + gname, landed, 3 * len(landed), plan_gather_d2d)

    def gathered(gname, d2d):
        outs = run(exchange_wait, "gather_d2d_wait_" + gname, d2d, plan_gather_d2d)
        return [o.reshape(N_CHIPS, *big[n].shape) for o, n in zip(outs, gather_groups[gname])]

    def rs_pair_start(gname, names, partials):
        blocks = [g.reshape(N_CHIPS, 2, big[n].shape[0] // 2, big[n].shape[1]) for n, g in zip(names, partials)]
        land = [lax.empty((N_CHIPS,) + b.shape[2:], BF16) for b in blocks]
        return run(exchange_start, "pair_exchange_start_" + gname, blocks + land, N_CHIPS * len(blocks),
                   plan_pair_exchange)

    def rs_chip_start(gname, names, pair):
        nw = len(names)
        outs = run(exchange_wait, "pair_exchange_wait_" + gname, pair, plan_pair_exchange)
        sums = [run(pair_sum, g, r, core_arr, "pair_sum_" + n) for n, g, r in zip(names, outs[:nw], outs[nw:])]
        land = [lax.empty(s.shape, BF16) for s in sums]
        return run(exchange_start, "chip_exchange_start_" + gname, sums + land, 3 * nw, plan_chip_exchange)

    def rs_share_start(gname, names, chipx):
        nw = len(names)
        outs = run(exchange_wait, "chip_exchange_wait_" + gname, chipx, plan_chip_exchange)
        halves = [run(chip_sum, s, r, where_arr, "chip_sum_" + n) for n, s, r in zip(names, outs[:nw], outs[nw:])]
        return run(exchange_start, "pair_share_start_" + gname, halves, nw, plan_pair_share)

    def rs_finish(gname, names, share):
        fulls = run(exchange_wait, "pair_share_wait_" + gname, share, plan_pair_share)
        for n, g2 in zip(names, fulls):
            g, dl, nm, nv = run(adamw, big[n], g2.reshape(big[n].shape), moments[n][0][0], moments[n][1][0],
                                "adamw_" + n, copy_grad=True)
            grads[n], deltas[n], new_m[n], new_v[n] = g[None], dl[None], nm[None], nv[None]

    near, far = (0, 1), (2,)
    plan_ici_near = functools.partial(plan_gather_ici, peers=near)
    plan_ici_far = functools.partial(plan_gather_ici, peers=far)
    plan_d2d_near = functools.partial(plan_gather_d2d, peers=near)
    plan_d2d_far = functools.partial(plan_gather_d2d, peers=far)
    bufs_in = gather_cast("in")
    row1, offs1 = _row_pack([c, conv_w[0].reshape(1, CONV_WIDTH * ch_loc)])
    got1 = run(allgather_small, row1, "allgather_cond")
    ici_near = run(exchange_start, "gather_ici_start_in_near", bufs_in, len(near), plan_ici_near)
    c_all = got1[:, 0, :d]
    conv_w_full = got1[0::2, 0, offs1[1]:offs1[1] + CONV_WIDTH * ch_loc].reshape(N_CHIPS, CONV_WIDTH, ch_loc)
    conv_w_full = jnp.transpose(conv_w_full, (1, 0, 2)).reshape(CONV_WIDTH, ch)
    conv_w_pad = jnp.pad(conv_w_full, ((0, 1), (0, 0)))
    c_t = jnp.transpose(c_all)
    mod_cols = run(ada_fwd, c_t, w_ada[0])
    rest_bufs = {gname: gather_cast(gname) for gname in gather_groups if gname != "in"}
    got2 = run(allgather_small, mod_cols, "allgather_mod")
    mod_all = got2.reshape(N_CHIPS, 2, N_DEV, nc_ada)[:, 0]
    mod = lax.dynamic_slice_in_dim(mod_all, me, 1, axis=1).reshape(1, N_CHIPS * nc_ada) + b_ada
    mod = jnp.pad(mod.reshape(N_MOD, d), ((0, SUBLANES - N_MOD), (0, 0)))

    h = run(pre_mix_fwd, xe, mod, norm_mix_g)
    bucket = _t5_bucket_table()
    bucket_p, bucket_c = jnp.asarray(bucket[:, :BLOCK]), jnp.asarray(bucket[:, BLOCK:])
    bias_p, bias_c = run(bias_table, rel_bias, bucket_p, bucket_c)

    def in_blocks(buf):
        return buf.reshape(N_CHIPS, *big["w_in"].shape)

    def chip_ids(peers):
        others = [2 * (1 - xi) + yi, 2 * xi + (1 - yi), 2 * (1 - xi) + (1 - yi)]
        return jnp.stack([others[j] for j in peers]).astype(jnp.int32)

    tn_in = big["w_in"].shape[1]
    landed = run(exchange_wait, "gather_ici_wait_in_near", ici_near, plan_ici_near)
    ici_far = run(exchange_start, "gather_ici_start_in_far", landed, len(far), plan_ici_far)
    ici = {gname: gather_ici_start(gname, bufs) for gname, bufs in rest_bufs.items()}
    d2d_near = run(exchange_start, "gather_d2d_start_in_near", ici_far.bufs, len(near), plan_d2d_near)
    p = run(mm_nn_blocks, h, in_blocks(d2d_near.bufs[0]), chip_arr, None, tn=tn_in, out_dtype=BF16, name="mm_in_own")
    landed = run(exchange_wait, "gather_d2d_wait_in_near", d2d_near, plan_d2d_near)
    p = run(mm_nn_blocks, h, in_blocks(landed[0]), chip_ids(near), p, tn=tn_in, out_dtype=BF16, name="mm_in_near")
    landed = run(exchange_wait, "gather_ici_wait_in_far", ici_far, plan_ici_far, bufs=landed)
    d2d_far = run(exchange_start, "gather_d2d_start_in_far", landed, len(far), plan_d2d_far)
    landed = run(exchange_wait, "gather_d2d_wait_in_far", d2d_far, plan_d2d_far)
    wg_in = in_blocks(landed[0])
    p = run(mm_nn_blocks, h, wg_in, chip_ids(far), p, tn=tn_in, out_dtype=BF16, name="mm_in_far")
    d2d_branch = gather_pass_on("branch_out", ici["branch_out"])

    sinks3 = attn_sinks.reshape(nq, 1, 1)
    attn_o = run(attn_fwd, p, bias_p, bias_c, sinks3, q_norm_g, k_norm_g, aw=aw, kvw=kvw)
    ca, cb = p[:, off_ca:off_cb], p[:, off_cb:off_ga]
    s_conv, co_conv = run(conv_fwd, ca, cb, conv_w_pad, conv_b, conv_ln_g, conv_ln_b)
    wg_attn_out, wg_conv_out = gathered("branch_out", d2d_branch)
    y_attn = run(mm_nn, attn_o, wg_attn_out, tn=_tile(wg_attn_out.shape[2], 512), tk=aw, out_dtype=BF16,
                 name="mm_attn_out")
    y_conv = run(mm_nn, s_conv, wg_conv_out, tn=_tile(wg_conv_out.shape[2], 512), tk=ch, out_dtype=BF16,
                 name="mm_conv_out")
    d2d_ffn_in = gather_pass_on("ffn_in", ici["ffn_in"])
    merged = run(merge_fwd, p, y_attn, y_conv, off_ga, off_gc)
    d2d_mix = gather_pass_on("mix_out", ici["mix_out"])
    (wg_mix_out,) = gathered("mix_out", d2d_mix)
    wg_mix_out = wg_mix_out.reshape(1, d, d)
    o_m = run(mm_nn, merged, wg_mix_out, tn=_tile(d, 512), tk=d, out_dtype=F32, name="mm_mix_out")
    x1, h2 = run(pre_ffn_fwd, xe, o_m, mod, norm_ffn_g)
    (wg_ffn_in,) = gathered("ffn_in", d2d_ffn_in)
    f = run(mm_nn, h2, wg_ffn_in, tn=_tile(nj_ffn, 1408), tk=d, out_dtype=BF16, name="mm_ffn_in", perm=perm_ffn)
    d2d_ffn_out = gather_pass_on("ffn_out", ici["ffn_out"])
    act = run(swiglu_fwd, f, nj_ffn)
    (wg_ffn_out,) = gathered("ffn_out", d2d_ffn_out)
    wg_ffn_out = wg_ffn_out.reshape(1, dff, d)
    o_f = run(mm_nn, act, wg_ffn_out, tn=_tile(d, 512), tk=_tile(dff, 2816), out_dtype=F32, name="mm_ffn_out")
    loss11, dy, dof, acc_l = run(loss_head, x1, o_f, tgt, mod)

    gw_ffn_out = run(mm_tn, act, dof, 1, tk=_tile(dff, 512), tn=d, name="mm_ffn_out_dw")
    px_ffn_out = rs_pair_start("ffn_out", ["w_ffn_out"], [gw_ffn_out])
    dact = run(mm_nt, dof, wg_ffn_out, tko=_tile(dff, 512), tn=d, out_dtype=BF16, name="mm_ffn_out_dx")
    cx_ffn_out = rs_chip_start("ffn_out", ["w_ffn_out"], px_ffn_out)
    df = run(swiglu_bwd, f, dact, nj_ffn)
    gw_ffn_in = run(mm_tn, h2, df, N_CHIPS, tk=d, tn=_tile(nj_ffn, 1408), name="mm_ffn_in_dw",
                    perm=perm_ffn)
    px_ffn_in = rs_pair_start("ffn_in", ["w_ffn_in"], [gw_ffn_in])
    dh2 = run(mm_nt, df, wg_ffn_in, tko=_tile(d, 512), tn=nj_ffn, name="mm_ffn_in_dx", perm=perm_ffn)
    sh_ffn_out = rs_share_start("ffn_out", ["w_ffn_out"], cx_ffn_out)
    cx_ffn_in = rs_chip_start("ffn_in", ["w_ffn_in"], px_ffn_in)
    dx1, dom, acc_f = run(pre_ffn_bwd, x1, dh2, dy, o_m, mod, norm_ffn_g)
    gw_mix_out = run(mm_tn, merged, dom, 1, tk=d, tn=_tile(d, 1024), name="mm_mix_out_dw")
    px_mix = rs_pair_start("mix_out", ["w_mix_out"], [gw_mix_out])
    dmerged = run(mm_nt, dom, wg_mix_out, tko=_tile(d, 512), tn=d, out_dtype=BF16, name="mm_mix_out_dx")
    dy_attn, dy_conv, dga, dgc = run(merge_bwd, p, y_attn, y_conv, dmerged, off_ga, off_gc)
    rs_finish("ffn_out", ["w_ffn_out"], sh_ffn_out)
    cx_mix = rs_chip_start("mix_out", ["w_mix_out"], px_mix)
    gw_attn_out = run(mm_tn, attn_o, dy_attn, N_CHIPS, tk=aw, tn=_tile(wg_attn_out.shape[2], 512),
                      name="mm_attn_out_dw")
    gw_conv_out = run(mm_tn, s_conv, dy_conv, N_CHIPS, tk=ch, tn=_tile(wg_conv_out.shape[2], 512),
                      name="mm_conv_out_dw")
    ac_names = ["w_attn_out", "w_conv_out"]
    px_ac = rs_pair_start("attn_conv_out", ac_names, [gw_attn_out, gw_conv_out])
    dattn_o = run(mm_nt, dy_attn, wg_attn_out, tko=_tile(aw, 1024), tn=_tile(wg_attn_out.shape[2], 512),
                  out_dtype=BF16, name="mm_attn_out_dx")
    ds_conv = run(mm_nt, dy_conv, wg_conv_out, tko=_tile(ch, 1024), tn=_tile(wg_conv_out.shape[2], 512),
                  out_dtype=BF16, name="mm_conv_out_dx")
    cx_ac = rs_chip_start("attn_conv_out", ac_names, px_ac)
    dca, dcb, dconv_w, dconv_vec = run(conv_bwd, ca, cb, co_conv, ds_conv, conv_w_pad, conv_ln_g, conv_ln_b)
    sh_ffn_in = rs_share_start("ffn_in", ["w_ffn_in"], cx_ffn_in)
    dqkv, dbp, dbc, dsinks, dqg, dkg = run(attn_bwd, p, bias_p, bias_c, sinks3, q_norm_g, k_norm_g, dattn_o,
                                           aw=aw, kvw=kvw)
    sh_mix = rs_share_start("mix_out", ["w_mix_out"], cx_mix)
    sh_ac = rs_share_start("attn_conv_out", ac_names, cx_ac)
    drel = run(bias_table_bwd, dbp, dbc, bucket_p, bucket_c).reshape(NUM_BUCKETS, nq)
    dp = jnp.concatenate([dqkv, dca, dcb, dga, dgc], axis=1)
    gw_in = run(mm_tn, h, dp, N_CHIPS, tk=d, tn=wg_in.shape[2], name="mm_in_dw")
    px_in = rs_pair_start("in", ["w_in"], [gw_in])
    dh = run(mm_nt, dp, wg_in, tko=_tile(d, 1024), tn=wg_in.shape[2], name="mm_in_dx")
    grad_x, acc_m = run(pre_mix_bwd, xe, dh, dx1, mod, norm_mix_g)

    dmod = jnp.concatenate([acc_m[0:1], acc_m[1:2], acc_f[3:4], acc_f[0:1], acc_f[1:2], acc_l[0:1]], axis=1)
    small_names = ["b_ada", "norm_mix_g", "q_norm_g", "k_norm_g", "attn_sinks", "rel_bias", "conv_b", "conv_ln_g",
                   "conv_ln_b", "norm_ffn_g"]
    small_w = [b_ada, norm_mix_g, q_norm_g, k_norm_g, attn_sinks, rel_bias, conv_b, conv_ln_g, conv_ln_b, norm_ffn_g]
    small_m = [m_b_ada, m_norm_mix_g, m_q_norm_g, m_k_norm_g, m_attn_sinks, m_rel_bias, m_conv_b, m_conv_ln_g,
               m_conv_ln_b, m_norm_ffn_g]
    small_v = [v_b_ada, v_norm_mix_g, v_q_norm_g, v_k_norm_g, v_attn_sinks, v_rel_bias, v_conv_b, v_conv_ln_g,
               v_conv_ln_b, v_norm_ffn_g]
    small_g = [dmod, acc_m[2:3], dqg, dkg, dsinks.reshape(1, nq), drel.reshape(1, NUM_BUCKETS * nq),
               dconv_vec[0:1], dconv_vec[1:2], dconv_vec[2:3], acc_f[2:3]]
    row3, offs3 = _row_pack(small_g + [dconv_w[:CONV_WIDTH].reshape(1, CONV_WIDTH * ch), loss11])
    got3 = run(allgather_small, row3, "allgather_small_grads")
    cx_in = rs_chip_start("in", ["w_in"], px_in)
    as_row = lambda a: a.reshape(1, -1)
    outs3 = run(small_sum_adamw, got3, offs3, [as_row(a) for a in small_w], [as_row(a) for a in small_m],
                [as_row(a) for a in small_v], [CONV_WIDTH * ch, 1])
    for i, (n, w) in enumerate(zip(small_names, small_w)):
        grads[n], deltas[n], new_m[n], new_v[n] = (o.reshape(w.shape) for o in outs3[4 * i:4 * i + 4])
    g_conv_w_all, loss_sum = outs3[-2].reshape(CONV_WIDTH, ch), outs3[-1]

    g_conv_w = lax.dynamic_slice_in_dim(g_conv_w_all, chip * ch_loc, ch_loc, axis=1)
    grads["conv_w"] = g_conv_w[None]
    dl, nm, nv = run(adamw, conv_w[0], g_conv_w, m_conv_w[0], v_conv_w[0], "adamw_conv_w")
    deltas["conv_w"], new_m["conv_w"], new_v["conv_w"] = dl[None], nm[None], nv[None]

    dmod_all = got3[:, 0, :N_MOD * d]
    dmod_cols = lax.dynamic_slice_in_dim(dmod_all, chip * nc_ada, nc_ada, axis=1)
    g_ada, dl, nm, nv = run(ada_grad_adamw, c_t, dmod_cols, w_ada[0], m_w_ada[0], v_w_ada[0])
    grads["w_ada"], deltas["w_ada"], new_m["w_ada"], new_v["w_ada"] = g_ada[None], dl[None], nm[None], nv[None]

    rs_finish("ffn_in", ["w_ffn_in"], sh_ffn_in)
    rs_finish("mix_out", ["w_mix_out"], sh_mix)
    rs_finish("attn_conv_out", ac_names, sh_ac)
    sh_in = rs_share_start("in", ["w_in"], cx_in)
    rs_finish("in", ["w_in"], sh_in)

    loss = loss_sum[0, 0]
    order = ["w_ada", "b_ada", "norm_mix_g", "w_in", "q_norm_g", "k_norm_g", "attn_sinks", "rel_bias", "w_attn_out",
             "conv_w", "conv_b", "conv_ln_g", "conv_ln_b", "w_conv_out", "w_mix_out", "norm_ffn_g", "w_ffn_in",
             "w_ffn_out"]
    return (loss, grad_x[None], *[grads[n] for n in order], *[deltas[n] for n in order],
            *[new_m[n] for n in order], *[new_v[n] for n in order])
```

```python
import functools
import math
from typing import Any, NamedTuple

import jax
import jax.numpy as jnp
import numpy as np
from jax import lax
from jax.experimental import pallas as pl
from jax.experimental.pallas import tpu as pltpu

F32 = jnp.float32
BF16 = jnp.bfloat16
MESH = pl.DeviceIdType.MESH

V7X_VMEM_BYTES = 64 * 1024 * 1024
VMEM_LIMIT = V7X_VMEM_BYTES - 8 * 1024 * 1024
LANES = 128
SUBLANES = 8
BF16_SUBLANES = 16

EPS = 1e-6
WINDOW = 128
BLOCK = 128
NUM_BUCKETS = 32
MAX_EXACT = NUM_BUCKETS // 2
MAX_DISTANCE = 128
CONV_WIDTH = 31
CONV_HALO = 32
ADAM_LR = 0.001
ADAM_B1 = 0.9
ADAM_B2 = 0.999
ADAM_EPS = 1e-08
ADAM_WD = 0.01
ADAM_STEP = 10
N_MOD = 6
SH_M, SC_M, GT_M, SH_F, SC_F, GT_F = range(6)

N_CHIPS = 4
N_DEV = 8

_ANY = pl.BlockSpec(memory_space=pl.ANY)
_VMEM = pl.BlockSpec(memory_space=pltpu.VMEM)
_SMEM = pl.BlockSpec(memory_space=pltpu.SMEM)
_HBM = pl.BlockSpec(memory_space=pltpu.HBM)
_SEM = pl.BlockSpec(memory_space=pltpu.SEMAPHORE)
_EFFECT = pltpu.SideEffectType.DATAFLOW_SIDE_EFFECTING


class InOrder:
    def __init__(self):
        self.token = None

    def __call__(self, fn, *args, **kw):
        return fn(*args, dep=self, **kw)


def _pallas(body, args, *, in_specs, out_specs, out_shape, name, dep=None, grid=(), n_prefetch=0, scratch=(),
            sem=None, **kw):
    n_lead = n_prefetch + len(in_specs)
    in_specs, args = list(in_specs), list(args)
    single = not isinstance(out_shape, (list, tuple))
    out_shapes = [out_shape] if single else list(out_shape)
    out_specs = [out_specs] if single else list(out_specs)
    if dep is not None:
        inner, n_out, takes = body, len(out_shapes), dep.token is not None

        def body(*refs):
            rest = refs[n_lead + (1 if takes else 0):]
            rest[n_out][...] = jnp.zeros((SUBLANES, LANES), F32)
            return inner(*refs[:n_lead], *rest[:n_out], *rest[n_out + 1:])

        if takes:
            in_specs.append(_ANY)
            args.append(dep.token)
        out_shapes.append(jax.ShapeDtypeStruct((SUBLANES, LANES), F32))
        out_specs.append(pl.BlockSpec((SUBLANES, LANES), lambda *_: (0, 0)))
    params = kw.pop("compiler_params", None)
    if params is None:
        params = pltpu.CompilerParams(dimension_semantics=sem, vmem_limit_bytes=VMEM_LIMIT)
    outs = pl.pallas_call(
        body,
        grid_spec=pltpu.PrefetchScalarGridSpec(num_scalar_prefetch=n_prefetch, grid=grid, in_specs=in_specs,
                                               out_specs=out_specs, scratch_shapes=list(scratch)),
        out_shape=out_shapes, compiler_params=params, name=name, **kw,
    )(*args)
    if dep is not None:
        dep.token = outs[-1]
        outs = outs[:-1]
    return outs[0] if single else list(outs)


def _tile(n, pref, unit=LANES):
    best = None
    for t in range(unit, min(n, pref) + 1, unit):
        if n % t == 0:
            best = t
    return best if best is not None else n


def _sigmoid(v):
    return 1.0 / (1.0 + jnp.exp(-v.astype(F32)))


ROW_CHUNK = 512


def _row_chunks(m, unit=SUBLANES):
    step = _tile(m, ROW_CHUNK, unit)
    return [(s, step) for s in range(0, m, step)]


def _ew_tiles(r, n, unit=SUBLANES, elems=512 * 1024):
    return _tile(r, max(unit, elems // n), unit), n


def _block_pos(j, perm):
    if perm is None:
        return j
    pos = 0
    for a, p in enumerate(perm):
        pos = pos + jnp.where(j == a, p, 0)
    return pos


def mm_nn(a, w, *, tn, tk, out_dtype, name, perm=None, dep=None):
    m, k = a.shape
    j, k2, nj = w.shape
    assert k == k2 and nj % tn == 0 and k % tk == 0
    npj, nk = nj // tn, k // tk

    def body(a_ref, w_ref, o_ref, *scratch):
        kk = pl.program_id(1)
        for s, sz in _row_chunks(m):
            rows = pl.ds(s, sz)
            p = jnp.dot(a_ref[rows, :], w_ref[...], preferred_element_type=F32)
            if nk == 1:
                o_ref[rows, :] = p.astype(out_dtype)
            else:
                acc = scratch[0]

                @pl.when(kk == 0)
                def _():
                    acc[rows, :] = p

                @pl.when(kk > 0)
                def _():
                    acc[rows, :] += p

                @pl.when(kk == nk - 1)
                def _():
                    o_ref[rows, :] = acc[rows, :].astype(out_dtype)

    return _pallas(
        body, [a, w], dep=dep, grid=(j * npj, nk),
        in_specs=[
            pl.BlockSpec((m, tk), lambda n, kk: (0, kk)),
            pl.BlockSpec((None, tk, tn), lambda n, kk: (n // npj, kk, n % npj)),
        ],
        out_specs=pl.BlockSpec((m, tn), lambda n, kk: (0, _block_pos(n // npj, perm) * npj + n % npj)),
        out_shape=jax.ShapeDtypeStruct((m, j * nj), out_dtype),
        scratch=[pltpu.VMEM((m, tn), F32)] if nk > 1 else [],
        sem=("parallel", "arbitrary"), name=name)


def mm_nn_blocks(a, w, blocks, into, *, tn, out_dtype, name, perm=None, dep=None):
    m, k = a.shape
    j, k2, nj = w.shape
    assert k == k2 and nj % tn == 0
    npj = nj // tn
    n_in = 2 if into is None else 3

    def body(blocks_ref, a_ref, w_ref, *rest):
        o_ref = rest[n_in - 2]
        for s, sz in _row_chunks(m):
            rows = pl.ds(s, sz)
            o_ref[rows, :] = jnp.dot(a_ref[rows, :], w_ref[...], preferred_element_type=F32).astype(out_dtype)

    return _pallas(
        body, [blocks, a, w] + ([] if into is None else [into]), dep=dep, n_prefetch=1,
        grid=(blocks.shape[0] * npj,),
        in_specs=[pl.BlockSpec((m, k), lambda n, bl: (0, 0)),
                  pl.BlockSpec((None, k, tn), lambda n, bl: (bl[n // npj], 0, n % npj))]
        + ([] if into is None else [_ANY]),
        out_specs=pl.BlockSpec((m, tn), lambda n, bl: (0, _block_pos(bl[n // npj], perm) * npj + n % npj)),
        out_shape=jax.ShapeDtypeStruct((m, j * nj), out_dtype),
        input_output_aliases={} if into is None else {3: 0},
        sem=("arbitrary",), name=name)


def mm_nt(g, w, *, tko, tn, name, out_dtype=F32, perm=None, dep=None):
    m, n = g.shape
    j, k, nj = w.shape
    assert n == j * nj and nj % tn == 0 and k % tko == 0
    npj, nr = nj // tn, n // tn
    in_place = out_dtype == F32

    def body(g_ref, w_ref, o_ref, *scratch):
        r = pl.program_id(1)
        acc = o_ref if in_place else (scratch[0] if nr > 1 else None)
        for s, sz in _row_chunks(m):
            rows = pl.ds(s, sz)
            p = lax.dot_general(g_ref[rows, :], w_ref[...], (((1,), (1,)), ((), ())), preferred_element_type=F32)
            if acc is None:
                o_ref[rows, :] = p.astype(out_dtype)
                continue

            @pl.when(r == 0)
            def _():
                acc[rows, :] = p

            @pl.when(r > 0)
            def _():
                acc[rows, :] += p

            if not in_place:
                @pl.when(r == nr - 1)
                def _():
                    o_ref[rows, :] = acc[rows, :].astype(out_dtype)

    return _pallas(
        body, [g, w], dep=dep, grid=(k // tko, nr),
        in_specs=[
            pl.BlockSpec((m, tn), lambda ko, r: (0, _block_pos(r // npj, perm) * npj + r % npj)),
            pl.BlockSpec((None, tko, tn), lambda ko, r: (r // npj, ko, r % npj)),
        ],
        out_specs=pl.BlockSpec((m, tko), lambda ko, r: (0, ko)),
        out_shape=jax.ShapeDtypeStruct((m, k), out_dtype),
        scratch=[pltpu.VMEM((m, tko), F32)] if (nr > 1 and not in_place) else [],
        sem=("parallel", "arbitrary"), name=name)


def mm_tn(a, g, n_blocks, *, tk, tn, name, perm=None, dep=None):
    m, k = a.shape
    m2, n = g.shape
    nj = n // n_blocks
    assert m == m2 and nj % tn == 0 and k % tk == 0
    npj = nj // tn

    def body(a_ref, g_ref, o_ref):
        for s, sz in _row_chunks(tk, LANES):
            p = lax.dot_general(a_ref[:, pl.ds(s, sz)], g_ref[...], (((0,), (0,)), ((), ())),
                                preferred_element_type=F32)
            o_ref[pl.ds(s, sz), :] = p.astype(BF16)

    return _pallas(
        body, [a, g], dep=dep, grid=(k // tk, n // tn),
        in_specs=[
            pl.BlockSpec((m, tk), lambda kk, nn: (0, kk)),
            pl.BlockSpec((m, tn), lambda kk, nn: (0, _block_pos(nn // npj, perm) * npj + nn % npj)),
        ],
        out_specs=pl.BlockSpec((None, tk, tn), lambda kk, nn: (nn // npj, kk, nn % npj)),
        out_shape=jax.ShapeDtypeStruct((n_blocks, k, nj), BF16),
        sem=("parallel", "parallel"), name=name)


ROW_TILE = 256


def _row_spec(tr, width):
    return pl.BlockSpec((tr, width), lambda i: (i, 0))


def _full_spec(shape):
    return pl.BlockSpec(shape, lambda *_: (0,) * len(shape))


def _rms(xv):
    return lax.rsqrt(jnp.mean(xv * xv, axis=-1, keepdims=True) + EPS)


def _mod_row(mod_ref, row):
    return mod_ref[pl.ds(row, 1), :]


def pre_mix_fwd(x, mod, gain, dep=None):
    t, d = x.shape
    tr = _tile(t, ROW_TILE, SUBLANES)

    def body(x_ref, mod_ref, g_ref, h_ref):
        xv = x_ref[...]
        y = xv * _rms(xv) * g_ref[...]
        h_ref[...] = (y * (1.0 + _mod_row(mod_ref, SC_M)) + _mod_row(mod_ref, SH_M)).astype(BF16)

    return _pallas(
        body, [x, mod, gain], dep=dep, grid=(t // tr,),
        in_specs=[_row_spec(tr, d), _full_spec(mod.shape), _full_spec(gain.shape)],
        out_specs=_row_spec(tr, d),
        out_shape=jax.ShapeDtypeStruct((t, d), BF16),
        sem=("parallel",), name="pre_mix_fwd")


def pre_ffn_fwd(x, o_m, mod, gain, dep=None):
    t, d = x.shape
    tr = _tile(t, ROW_TILE, SUBLANES)

    def body(x_ref, om_ref, mod_ref, g_ref, x1_ref, h_ref):
        x1 = x_ref[...] + _mod_row(mod_ref, GT_M) * om_ref[...]
        x1_ref[...] = x1
        y = x1 * _rms(x1) * g_ref[...]
        h_ref[...] = (y * (1.0 + _mod_row(mod_ref, SC_F)) + _mod_row(mod_ref, SH_F)).astype(BF16)

    return _pallas(
        body, [x, o_m, mod, gain], dep=dep, grid=(t // tr,),
        in_specs=[_row_spec(tr, d), _row_spec(tr, d), _full_spec(mod.shape), _full_spec(gain.shape)],
        out_specs=[_row_spec(tr, d), _row_spec(tr, d)],
        out_shape=[jax.ShapeDtypeStruct((t, d), F32), jax.ShapeDtypeStruct((t, d), BF16)],
        sem=("parallel",), name="pre_ffn_fwd")


def loss_head(x1, o_f, target, mod, dep=None):
    t, d = x1.shape
    tr = _tile(t, ROW_TILE, SUBLANES)

    def body(x1_ref, of_ref, tg_ref, mod_ref, loss_ref, dy_ref, dof_ref, acc_ref):
        i = pl.program_id(0)
        gt = _mod_row(mod_ref, GT_F)
        of = of_ref[...]
        err = x1_ref[...] + gt * of - tg_ref[...]
        dy = err * (1.0 / d)
        dy_ref[...] = dy
        dof_ref[...] = (dy * gt).astype(BF16)
        part = (0.5 / d) * jnp.sum(jnp.sum(err * err, axis=1, keepdims=True), axis=0, keepdims=True)
        dgt = jnp.sum(dy * of, axis=0, keepdims=True)

        @pl.when(i == 0)
        def _():
            loss_ref[...] = jnp.zeros_like(loss_ref)
            acc_ref[...] = jnp.zeros_like(acc_ref)

        loss_ref[...] += part
        acc_ref[pl.ds(0, 1), :] += dgt

    return _pallas(
        body, [x1, o_f, target, mod], dep=dep, grid=(t // tr,),
        in_specs=[_row_spec(tr, d), _row_spec(tr, d), _row_spec(tr, d), _full_spec(mod.shape)],
        out_specs=[_full_spec((1, 1)), _row_spec(tr, d), _row_spec(tr, d), _full_spec((SUBLANES, d))],
        out_shape=[jax.ShapeDtypeStruct((1, 1), F32), jax.ShapeDtypeStruct((t, d), F32),
                   jax.ShapeDtypeStruct((t, d), BF16), jax.ShapeDtypeStruct((SUBLANES, d), F32)],
        sem=("arbitrary",), name="loss_head")


def _norm_bwd(xv, dh, sc, gain):
    rstd = _rms(xv)
    yn = xv * rstd
    dsh = jnp.sum(dh, axis=0, keepdims=True)
    dsc = jnp.sum(dh * (yn * gain), axis=0, keepdims=True)
    dgain = jnp.sum(dh * (1.0 + sc) * yn, axis=0, keepdims=True)
    dyn = dh * ((1.0 + sc) * gain)
    dx = rstd * (dyn - yn * jnp.mean(dyn * yn, axis=-1, keepdims=True))
    return dx, dsh, dsc, dgain


def pre_ffn_bwd(x1, dh2, dy, o_m, mod, gain, dep=None):
    t, d = x1.shape
    tr = _tile(t, ROW_TILE, SUBLANES)

    def body(x1_ref, dh_ref, dy_ref, om_ref, mod_ref, g_ref, dx1_ref, dom_ref, acc_ref):
        i = pl.program_id(0)
        dxn, dsh, dsc, dgain = _norm_bwd(x1_ref[...], dh_ref[...], _mod_row(mod_ref, SC_F), g_ref[...])
        dx1 = dy_ref[...] + dxn
        dx1_ref[...] = dx1
        dom_ref[...] = (dx1 * _mod_row(mod_ref, GT_M)).astype(BF16)
        dgt = jnp.sum(dx1 * om_ref[...], axis=0, keepdims=True)

        @pl.when(i == 0)
        def _():
            acc_ref[...] = jnp.zeros_like(acc_ref)

        acc_ref[pl.ds(0, 1), :] += dsh
        acc_ref[pl.ds(1, 1), :] += dsc
        acc_ref[pl.ds(2, 1), :] += dgain
        acc_ref[pl.ds(3, 1), :] += dgt

    return _pallas(
        body, [x1, dh2, dy, o_m, mod, gain], dep=dep, grid=(t // tr,),
        in_specs=[_row_spec(tr, d)] * 4 + [_full_spec(mod.shape), _full_spec(gain.shape)],
        out_specs=[_row_spec(tr, d), _row_spec(tr, d), _full_spec((SUBLANES, d))],
        out_shape=[jax.ShapeDtypeStruct((t, d), F32), jax.ShapeDtypeStruct((t, d), BF16),
                   jax.ShapeDtypeStruct((SUBLANES, d), F32)],
        sem=("arbitrary",), name="pre_ffn_bwd")


def pre_mix_bwd(x, dh, dx1, mod, gain, dep=None):
    t, d = x.shape
    tr = _tile(t, ROW_TILE, SUBLANES)

    def body(x_ref, dh_ref, dx1_ref, mod_ref, g_ref, gx_ref, acc_ref):
        i = pl.program_id(0)
        dxn, dsh, dsc, dgain = _norm_bwd(x_ref[...], dh_ref[...], _mod_row(mod_ref, SC_M), g_ref[...])
        gx_ref[...] = dx1_ref[...] + dxn

        @pl.when(i == 0)
        def _():
            acc_ref[...] = jnp.zeros_like(acc_ref)

        acc_ref[pl.ds(0, 1), :] += dsh
        acc_ref[pl.ds(1, 1), :] += dsc
        acc_ref[pl.ds(2, 1), :] += dgain

    return _pallas(
        body, [x, dh, dx1, mod, gain], dep=dep, grid=(t // tr,),
        in_specs=[_row_spec(tr, d)] * 3 + [_full_spec(mod.shape), _full_spec(gain.shape)],
        out_specs=[_row_spec(tr, d), _full_spec((SUBLANES, d))],
        out_shape=[jax.ShapeDtypeStruct((t, d), F32), jax.ShapeDtypeStruct((SUBLANES, d), F32)],
        sem=("arbitrary",), name="pre_mix_bwd")


def merge_fwd(p, y_attn, y_conv, off_ga, off_gc, dep=None):
    t, d = y_attn.shape
    tr = _tile(t, ROW_TILE, SUBLANES)
    cw = math.gcd(math.gcd(off_ga, off_gc), math.gcd(d, 512))
    nc = d // cw

    def body(ga_ref, gc_ref, ya_ref, yc_ref, o_ref):
        o_ref[...] = (_sigmoid(ga_ref[...]) * ya_ref[...] + _sigmoid(gc_ref[...]) * yc_ref[...]).astype(BF16)

    return _pallas(
        body, [p, p, y_attn, y_conv], dep=dep, grid=(t // tr, nc),
        in_specs=[pl.BlockSpec((tr, cw), lambda i, j: (i, off_ga // cw + j)),
                  pl.BlockSpec((tr, cw), lambda i, j: (i, off_gc // cw + j)),
                  pl.BlockSpec((tr, cw), lambda i, j: (i, j)),
                  pl.BlockSpec((tr, cw), lambda i, j: (i, j))],
        out_specs=pl.BlockSpec((tr, cw), lambda i, j: (i, j)),
        out_shape=jax.ShapeDtypeStruct((t, d), BF16),
        sem=("parallel", "parallel"), name="merge_fwd")


def merge_bwd(p, y_attn, y_conv, dmerged, off_ga, off_gc, dep=None):
    t, d = y_attn.shape
    tr = _tile(t, ROW_TILE, SUBLANES)
    cw = math.gcd(math.gcd(off_ga, off_gc), math.gcd(d, 512))
    nc = d // cw

    def body(ga_ref, gc_ref, ya_ref, yc_ref, dm_ref, dya_ref, dyc_ref, dga_ref, dgc_ref):
        dm = dm_ref[...].astype(F32)
        sa = _sigmoid(ga_ref[...])
        sc = _sigmoid(gc_ref[...])
        dya_ref[...] = (dm * sa).astype(BF16)
        dyc_ref[...] = (dm * sc).astype(BF16)
        dga_ref[...] = (dm * ya_ref[...] * sa * (1.0 - sa)).astype(BF16)
        dgc_ref[...] = (dm * yc_ref[...] * sc * (1.0 - sc)).astype(BF16)

    blk = pl.BlockSpec((tr, cw), lambda i, j: (i, j))
    return _pallas(
        body, [p, p, y_attn, y_conv, dmerged], dep=dep, grid=(t // tr, nc),
        in_specs=[pl.BlockSpec((tr, cw), lambda i, j: (i, off_ga // cw + j)),
                  pl.BlockSpec((tr, cw), lambda i, j: (i, off_gc // cw + j)), blk, blk, blk],
        out_specs=[blk] * 4,
        out_shape=[jax.ShapeDtypeStruct((t, d), BF16)] * 4,
        sem=("parallel", "parallel"), name="merge_bwd")


def ffn_perm(n_blocks):
    half = n_blocks // 2
    return tuple(2 * j if j < half else 2 * (j - half) + 1 for j in range(n_blocks))


def swiglu_fwd(f, nj, dep=None):
    t, two = f.shape
    tr = _tile(t, ROW_TILE, SUBLANES)
    npair = two // (2 * nj)

    def body(f_ref, o_ref):
        g = f_ref[:, :nj].astype(F32)
        u = f_ref[:, nj:].astype(F32)
        o_ref[...] = (g * _sigmoid(g) * u).astype(BF16)

    return _pallas(
        body, [f], dep=dep, grid=(t // tr, npair),
        in_specs=[pl.BlockSpec((tr, 2 * nj), lambda i, j: (i, j))],
        out_specs=pl.BlockSpec((tr, nj), lambda i, j: (i, j)),
        out_shape=jax.ShapeDtypeStruct((t, two // 2), BF16),
        sem=("parallel", "parallel"), name="swiglu_fwd")


def swiglu_bwd(f, dact, nj, dep=None):
    t, two = f.shape
    tr = _tile(t, ROW_TILE, SUBLANES)
    npair = two // (2 * nj)

    def body(f_ref, da_ref, o_ref):
        g = f_ref[:, :nj].astype(F32)
        u = f_ref[:, nj:].astype(F32)
        da = da_ref[...]
        s = _sigmoid(g)
        o_ref[:, :nj] = (da * u * (s * (1.0 + g * (1.0 - s)))).astype(BF16)
        o_ref[:, nj:] = (da * (g * s)).astype(BF16)

    return _pallas(
        body, [f, dact], dep=dep, grid=(t // tr, npair),
        in_specs=[pl.BlockSpec((tr, 2 * nj), lambda i, j: (i, j)), pl.BlockSpec((tr, nj), lambda i, j: (i, j))],
        out_specs=pl.BlockSpec((tr, 2 * nj), lambda i, j: (i, j)),
        out_shape=jax.ShapeDtypeStruct((t, two), BF16),
        sem=("parallel", "parallel"), name="swiglu_bwd")


def _t5_bucket_table():
    q_off = np.arange(BLOCK)
    k_off = np.arange(2 * BLOCK)
    dist = q_off[:, None] + BLOCK - k_off[None, :]
    n = np.maximum(dist, 0)
    nf = np.maximum(n, 1).astype(np.float32)
    large = MAX_EXACT + (np.log(nf / np.float32(MAX_EXACT)) / np.float32(math.log(MAX_DISTANCE / MAX_EXACT))
                         * np.float32(NUM_BUCKETS - MAX_EXACT)).astype(np.int32)
    large = np.minimum(large, NUM_BUCKETS - 1)
    bucket = np.where(n < MAX_EXACT, n, large).astype(np.int32)
    allowed = (dist >= 0) & (dist < WINDOW)
    return np.where(allowed, bucket, -1).astype(np.int32)


def bias_table(rel_bias, bucket_p, bucket_c, dep=None):
    nb, nq = rel_bias.shape

    def body(rb_ref, bkp_ref, bkc_ref, op_ref, oc_ref):
        for bk_ref, o_ref in ((bkp_ref, op_ref), (bkc_ref, oc_ref)):
            bk = bk_ref[...]
            for h in range(nq):
                acc = jnp.full(bk.shape, -jnp.inf, F32)
                for b in range(nb):
                    acc = jnp.where(bk == b, rb_ref[b, h], acc)
                o_ref[h] = acc

    return _pallas(
        body, [rel_bias, bucket_p, bucket_c], dep=dep,
        in_specs=[_SMEM, _VMEM, _VMEM], out_specs=[_VMEM, _VMEM],
        out_shape=[jax.ShapeDtypeStruct((nq,) + bucket_p.shape, F32)] * 2,
        name="bias_table")


def bias_table_bwd(dbp, dbc, bucket_p, bucket_c, dep=None):
    nq = dbp.shape[0]

    def body(dbp_ref, dbc_ref, bkp_ref, bkc_ref, o_ref):
        bkp, bkc = bkp_ref[...][None], bkc_ref[...][None]
        dp, dc = dbp_ref[...], dbc_ref[...]
        for b in range(NUM_BUCKETS):
            sel = jnp.where(bkp == b, dp, 0.0) + jnp.where(bkc == b, dc, 0.0)
            o_ref[b] = jnp.sum(jnp.sum(sel, axis=2, keepdims=True), axis=1, keepdims=True)

    return _pallas(
        body, [dbp, dbc, bucket_p, bucket_c], dep=dep,
        in_specs=[_VMEM] * 4, out_specs=_VMEM,
        out_shape=jax.ShapeDtypeStruct((NUM_BUCKETS, nq, 1, 1), F32),
        name="bias_table_bwd")


_NT = (((1,), (1,)), ((), ()))
_TN = (((0,), (0,)), ((), ()))


@jax.custom_vjp
def _bdot_nt(a, b):
    return lax.dot_general(a.astype(BF16), b.astype(BF16), _NT, preferred_element_type=F32)


def _bdot_nt_fwd(a, b):
    return _bdot_nt(a, b), (a, b)


def _bdot_nt_bwd(res, g):
    a, b = res
    gb = g.astype(BF16)
    da = jnp.dot(gb, b.astype(BF16), preferred_element_type=F32)
    db = lax.dot_general(gb, a.astype(BF16), _TN, preferred_element_type=F32)
    return da, db


_bdot_nt.defvjp(_bdot_nt_fwd, _bdot_nt_bwd)


@jax.custom_vjp
def _bdot_nn(a, b):
    return jnp.dot(a.astype(BF16), b.astype(BF16), preferred_element_type=F32)


def _bdot_nn_fwd(a, b):
    return _bdot_nn(a, b), (a, b)


def _bdot_nn_bwd(res, g):
    a, b = res
    gb = g.astype(BF16)
    da = lax.dot_general(gb, b.astype(BF16), _NT, preferred_element_type=F32)
    db = lax.dot_general(a.astype(BF16), gb, _TN, preferred_element_type=F32)
    return da, db


_bdot_nn.defvjp(_bdot_nn_fwd, _bdot_nn_bwd)


def _attn_math(q4, kp, kc, vp, vc, bp, bc, sink4, qg, kg, *, prev_ok, scale):
    g, b, hd = q4.shape
    q = q4.reshape(g * b, hd)
    qn = q * _rms(q) * qg
    kpn = kp * _rms(kp) * kg
    kcn = kc * _rms(kc) * kg
    lp = _bdot_nt(qn, kpn).reshape(g, b, b) * scale + bp
    lc = _bdot_nt(qn, kcn).reshape(g, b, b) * scale + bc
    lp = jnp.where(prev_ok, lp, -jnp.inf)
    m = jnp.maximum(jnp.maximum(jnp.max(lp, axis=-1, keepdims=True), jnp.max(lc, axis=-1, keepdims=True)), sink4)
    m = lax.stop_gradient(m)
    pp = jnp.exp(lp - m)
    pc = jnp.exp(lc - m)
    den = jnp.sum(pp, axis=-1, keepdims=True) + jnp.sum(pc, axis=-1, keepdims=True) + jnp.exp(sink4 - m)
    inv = 1.0 / den
    out = _bdot_nn((pp * inv).reshape(g * b, b), vp) + _bdot_nn((pc * inv).reshape(g * b, b), vc)
    return out.reshape(g, b, hd)


def _attn_specs(p, aw, kvw, nq, hd, nblk, reverse):
    assert aw % (2 * kvw) == 0
    kv_col = aw // (2 * kvw)

    def blk(n):
        return nblk - 1 - n if reverse else n

    return [
        pl.BlockSpec((BLOCK, aw), lambda n: (blk(n), 0)),
        pl.BlockSpec((BLOCK, 2 * kvw), lambda n: (jnp.maximum(blk(n) - 1, 0), kv_col)),
        pl.BlockSpec((BLOCK, 2 * kvw), lambda n: (blk(n), kv_col)),
        _full_spec((nq, BLOCK, BLOCK)), _full_spec((nq, BLOCK, BLOCK)), _full_spec((nq, 1, 1)),
        _full_spec((1, hd)), _full_spec((1, hd)),
    ]


def _attn_head_inputs(h, grp, hd, kvw, q_ref, kvp_ref, kvc_ref, bp_ref, bc_ref, s_ref):
    heads = pl.ds(grp * h, grp)
    q4 = jnp.stack([q_ref[:, pl.ds((grp * h + g) * hd, hd)].astype(F32) for g in range(grp)])
    k_cols, v_cols = pl.ds(h * hd, hd), pl.ds(kvw + h * hd, hd)
    kv = [r[:, cols].astype(F32) for cols in (k_cols, v_cols) for r in (kvp_ref, kvc_ref)]
    return (q4, *kv, bp_ref[heads], bc_ref[heads], s_ref[heads])


def attn_fwd(p, bias_p, bias_c, sinks, qg, kg, *, aw, kvw, dep=None):
    t, hd = p.shape[0], qg.shape[-1]
    nq, nkv, nblk = aw // hd, kvw // hd, t // BLOCK
    grp = nq // nkv
    scale = hd ** -0.5

    def body(q_ref, kvp_ref, kvc_ref, bp_ref, bc_ref, s_ref, qg_ref, kg_ref, o_ref):
        prev_ok = pl.program_id(0) > 0
        for h in range(nkv):
            args = _attn_head_inputs(h, grp, hd, kvw, q_ref, kvp_ref, kvc_ref, bp_ref, bc_ref, s_ref)
            out = _attn_math(*args, qg_ref[...], kg_ref[...], prev_ok=prev_ok, scale=scale)
            for g in range(grp):
                o_ref[:, pl.ds((grp * h + g) * hd, hd)] = out[g].astype(BF16)

    return _pallas(
        body, [p, p, p, bias_p, bias_c, sinks, qg, kg], dep=dep, grid=(nblk,),
        in_specs=_attn_specs(p, aw, kvw, nq, hd, nblk, False),
        out_specs=pl.BlockSpec((BLOCK, aw), lambda n: (n, 0)),
        out_shape=jax.ShapeDtypeStruct((t, aw), BF16),
        sem=("parallel",), name="attn_fwd")


def attn_bwd(p, bias_p, bias_c, sinks, qg, kg, do, *, aw, kvw, dep=None):
    t, hd = p.shape[0], qg.shape[-1]
    nq, nkv, nblk = aw // hd, kvw // hd, t // BLOCK
    grp = nq // nkv
    scale = hd ** -0.5

    def body(q_ref, kvp_ref, kvc_ref, bp_ref, bc_ref, s_ref, qg_ref, kg_ref, do_ref,
             dqkv_ref, dbp_ref, dbc_ref, ds_ref, dqg_ref, dkg_ref, carry):
        i = pl.program_id(0)
        prev_ok = (nblk - 1 - i) > 0

        @pl.when(i == 0)
        def _():
            carry[...] = jnp.zeros_like(carry)
            dbp_ref[...] = jnp.zeros_like(dbp_ref)
            dbc_ref[...] = jnp.zeros_like(dbc_ref)
            ds_ref[...] = jnp.zeros_like(ds_ref)
            dqg_ref[...] = jnp.zeros_like(dqg_ref)
            dkg_ref[...] = jnp.zeros_like(dkg_ref)

        fn = functools.partial(_attn_math, prev_ok=prev_ok, scale=scale)
        for h in range(nkv):
            args = _attn_head_inputs(h, grp, hd, kvw, q_ref, kvp_ref, kvc_ref, bp_ref, bc_ref, s_ref)
            _, vjp = jax.vjp(fn, *args, qg_ref[...], kg_ref[...])
            do4 = jnp.stack([do_ref[:, pl.ds((grp * h + g) * hd, hd)].astype(F32) for g in range(grp)])
            dq, dkp, dkc, dvp, dvc, dbp, dbc, dsk, dqg, dkg = vjp(do4)
            for g in range(grp):
                dqkv_ref[:, pl.ds((grp * h + g) * hd, hd)] = dq[g].astype(BF16)
            k_cols, v_cols = pl.ds(h * hd, hd), pl.ds(kvw + h * hd, hd)
            dqkv_ref[:, pl.ds(aw + h * hd, hd)] = (dkc + carry[:, k_cols]).astype(BF16)
            dqkv_ref[:, pl.ds(aw + kvw + h * hd, hd)] = (dvc + carry[:, v_cols]).astype(BF16)
            carry[:, k_cols] = dkp
            carry[:, v_cols] = dvp
            heads = pl.ds(grp * h, grp)
            dbp_ref[heads] += dbp
            dbc_ref[heads] += dbc
            ds_ref[heads] += dsk
            dqg_ref[...] += dqg
            dkg_ref[...] += dkg

    return _pallas(
        body, [p, p, p, bias_p, bias_c, sinks, qg, kg, do], dep=dep, grid=(nblk,),
        in_specs=_attn_specs(p, aw, kvw, nq, hd, nblk, True)
        + [pl.BlockSpec((BLOCK, aw), lambda n: (nblk - 1 - n, 0))],
        out_specs=[
            pl.BlockSpec((BLOCK, aw + 2 * kvw), lambda n: (nblk - 1 - n, 0)),
            _full_spec((nq, BLOCK, BLOCK)), _full_spec((nq, BLOCK, BLOCK)), _full_spec((nq, 1, 1)),
            _full_spec((1, hd)), _full_spec((1, hd)),
        ],
        out_shape=[
            jax.ShapeDtypeStruct((t, aw + 2 * kvw), BF16),
            jax.ShapeDtypeStruct((nq, BLOCK, BLOCK), F32),
            jax.ShapeDtypeStruct((nq, BLOCK, BLOCK), F32),
            jax.ShapeDtypeStruct((nq, 1, 1), F32),
            jax.ShapeDtypeStruct((1, hd), F32),
            jax.ShapeDtypeStruct((1, hd), F32),
        ],
        scratch=[pltpu.VMEM((BLOCK, 2 * kvw), F32)],
        sem=("arbitrary",), name="attn_bwd")


CONV_TILE = 256


def _conv_halo_specs(tb, ch, nblk):
    per = tb // CONV_HALO
    last = nblk * per - 1
    cur = pl.BlockSpec((tb, ch), lambda n: (n, 0))
    prev = pl.BlockSpec((CONV_HALO, ch), lambda n: (jnp.maximum(n * per - 1, 0), 0))
    nxt = pl.BlockSpec((CONV_HALO, ch), lambda n: (jnp.minimum((n + 1) * per, last), 0))
    return cur, prev, nxt


def _ln_silu(co, ln_g, ln_b):
    mu = jnp.mean(co, axis=-1, keepdims=True)
    cen = co - mu
    rstd = lax.rsqrt(jnp.mean(cen * cen, axis=-1, keepdims=True) + EPS)
    xhat = cen * rstd
    z = xhat * ln_g + ln_b
    return xhat, rstd, z


def _shifted_copies(src, shifted):
    rows = src.shape[0] - SUBLANES
    for r in range(1, SUBLANES):
        shifted[r, pl.ds(0, rows), :] = src[pl.ds(r, rows), :]


def _rows_from(src, shifted, start, n):
    r = start % SUBLANES
    if r == 0:
        return src[pl.ds(start, n), :]
    return shifted[r, pl.ds(start - r, n), :]


def conv_fwd(ca, cb, conv_w, conv_b, ln_g, ln_b, dep=None):
    t, ch = ca.shape
    tb = _tile(t, CONV_TILE, CONV_HALO)
    nblk = t // tb
    cur, prev, _ = _conv_halo_specs(tb, ch, nblk)
    lead = CONV_HALO - (CONV_WIDTH - 1)

    def body(ca_ref, cb_ref, cap_ref, cbp_ref, w_ref, b_ref, g_ref, bb_ref, s_ref, co_ref, ubuf, ushift):
        n = pl.program_id(0)
        halo = cap_ref[...] * _sigmoid(cbp_ref[...])
        ubuf[pl.ds(0, CONV_HALO), :] = jnp.where(n > 0, halo, 0.0)
        ubuf[pl.ds(CONV_HALO, tb), :] = ca_ref[...] * _sigmoid(cb_ref[...])
        _shifted_copies(ubuf, ushift)
        acc = jnp.broadcast_to(b_ref[...], (tb, ch))
        for k in range(CONV_WIDTH):
            acc = acc + w_ref[pl.ds(k, 1), :] * _rows_from(ubuf, ushift, lead + k, tb)
        co_ref[...] = acc
        _, _, z = _ln_silu(acc, g_ref[...], bb_ref[...])
        s_ref[...] = (z * _sigmoid(z)).astype(BF16)

    vec = _full_spec((1, ch))
    return _pallas(
        body, [ca, cb, ca, cb, conv_w, conv_b, ln_g, ln_b], dep=dep, grid=(nblk,),
        in_specs=[cur, cur, prev, prev, _full_spec(conv_w.shape), vec, vec, vec],
        out_specs=[cur, cur],
        out_shape=[jax.ShapeDtypeStruct((t, ch), BF16), jax.ShapeDtypeStruct((t, ch), F32)],
        scratch=[pltpu.VMEM((CONV_HALO + tb, ch), F32), pltpu.VMEM((SUBLANES, CONV_HALO + tb, ch), F32)],
        sem=("parallel",), name="conv_fwd")


def conv_bwd(ca, cb, co, ds, conv_w, ln_g, ln_b, dep=None):
    t, ch = ca.shape
    tb = _tile(t, CONV_TILE, CONV_HALO)
    nblk = t // tb
    cur, prev, nxt = _conv_halo_specs(tb, ch, nblk)
    lead = CONV_HALO - (CONV_WIDTH - 1)
    ext = tb + CONV_HALO

    def body(ca_ref, cb_ref, cap_ref, cbp_ref, co_ref, con_ref, ds_ref, dsn_ref, w_ref, g_ref, bb_ref,
             dca_ref, dcb_ref, dw_ref, dvec_ref, ubuf, dbuf, ushift, dshift):
        n = pl.program_id(0)
        is_last = n == nblk - 1
        sig_b = _sigmoid(cb_ref[...])
        cav = ca_ref[...].astype(F32)
        ubuf[pl.ds(0, CONV_HALO), :] = jnp.where(n > 0, cap_ref[...] * _sigmoid(cbp_ref[...]), 0.0)
        ubuf[pl.ds(CONV_HALO, tb), :] = cav * sig_b
        _shifted_copies(ubuf, ushift)
        co = jnp.concatenate([co_ref[...], con_ref[...]], axis=0)
        xhat, rstd, z = _ln_silu(co, g_ref[...], bb_ref[...])
        dsv = jnp.concatenate([ds_ref[...].astype(F32), jnp.where(is_last, 0.0, dsn_ref[...].astype(F32))], axis=0)
        sg = _sigmoid(z)
        dz = dsv * (sg * (1.0 + z * (1.0 - sg)))
        dxh = dz * g_ref[...]
        dco = rstd * (dxh - jnp.mean(dxh, axis=-1, keepdims=True)
                      - xhat * jnp.mean(dxh * xhat, axis=-1, keepdims=True))
        dbuf[...] = dco
        _shifted_copies(dbuf, dshift)

        @pl.when(n == 0)
        def _():
            dw_ref[...] = jnp.zeros_like(dw_ref)
            dvec_ref[...] = jnp.zeros_like(dvec_ref)

        dco_cur = dco[:tb]
        dvec_ref[pl.ds(0, 1), :] += jnp.sum(dco_cur, axis=0, keepdims=True)
        dvec_ref[pl.ds(1, 1), :] += jnp.sum(dz[:tb] * xhat[:tb], axis=0, keepdims=True)
        dvec_ref[pl.ds(2, 1), :] += jnp.sum(dz[:tb], axis=0, keepdims=True)
        du = jnp.zeros((tb, ch), F32)
        for k in range(CONV_WIDTH):
            du = du + w_ref[pl.ds(k, 1), :] * _rows_from(dbuf, dshift, CONV_WIDTH - 1 - k, tb)
            dw_ref[pl.ds(k, 1), :] += jnp.sum(dco_cur * _rows_from(ubuf, ushift, lead + k, tb), axis=0,
                                              keepdims=True)
        dca_ref[...] = (du * sig_b).astype(BF16)
        dcb_ref[...] = (du * cav * sig_b * (1.0 - sig_b)).astype(BF16)

    vec = _full_spec((1, ch))
    return _pallas(
        body, [ca, cb, ca, cb, co, co, ds, ds, conv_w, ln_g, ln_b], dep=dep, grid=(nblk,),
        in_specs=[cur, cur, prev, prev, cur, nxt, cur, nxt, _full_spec(conv_w.shape), vec, vec],
        out_specs=[cur, cur, _full_spec(conv_w.shape), _full_spec((SUBLANES, ch))],
        out_shape=[jax.ShapeDtypeStruct((t, ch), BF16), jax.ShapeDtypeStruct((t, ch), BF16),
                   jax.ShapeDtypeStruct(conv_w.shape, F32), jax.ShapeDtypeStruct((SUBLANES, ch), F32)],
        scratch=[pltpu.VMEM((CONV_HALO + tb, ch), F32), pltpu.VMEM((ext, ch), F32),
                 pltpu.VMEM((SUBLANES, CONV_HALO + tb, ch), F32), pltpu.VMEM((SUBLANES, ext, ch), F32)],
        sem=("arbitrary",), name="conv_bwd")


def ada_fwd(c_t, w_ada, dep=None):
    d, nc = w_ada.shape
    nex = c_t.shape[1]
    tn = _tile(nc, 512)

    def body(ct_ref, w_ref, o_ref):
        w = w_ref[...]
        ct = ct_ref[...]
        cact = ct * _sigmoid(ct)
        rows = [jnp.sum(w * cact[:, b:b + 1], axis=0, keepdims=True) for b in range(nex)]
        o_ref[...] = jnp.concatenate(rows, axis=0)

    return _pallas(
        body, [c_t, w_ada], dep=dep, grid=(nc // tn,),
        in_specs=[_full_spec(c_t.shape), pl.BlockSpec((d, tn), lambda j: (0, j))],
        out_specs=pl.BlockSpec((nex, tn), lambda j: (0, j)),
        out_shape=jax.ShapeDtypeStruct((nex, nc), F32),
        sem=("parallel",), name="ada_fwd")


def _adamw_math(w, g, m, v):
    m = ADAM_B1 * m + (1.0 - ADAM_B1) * g
    v = ADAM_B2 * v + (1.0 - ADAM_B2) * (g * g)
    m_hat = m / (1.0 - ADAM_B1 ** ADAM_STEP)
    v_hat = v / (1.0 - ADAM_B2 ** ADAM_STEP)
    delta = -ADAM_LR * (m_hat / (jnp.sqrt(v_hat) + ADAM_EPS) + ADAM_WD * w)
    return delta, m, v


def adamw(w, g, m, v, name, copy_grad=False, dep=None):
    r, n = w.shape
    tr, tn = _ew_tiles(r, n, elems=256 * 1024)
    n_out = 4 if copy_grad else 3

    def body(w_ref, g_ref, m_ref, v_ref, *outs):
        g = g_ref[...]
        if copy_grad:
            outs[0][...] = g
        outs[-3][...], outs[-2][...], outs[-1][...] = _adamw_math(w_ref[...], g, m_ref[...], v_ref[...])

    blk = pl.BlockSpec((tr, tn), lambda i, j: (i, j))
    return _pallas(
        body, [w, g, m, v], dep=dep, grid=(r // tr, n // tn),
        in_specs=[blk] * 4, out_specs=[blk] * n_out,
        out_shape=[jax.ShapeDtypeStruct((r, n), F32)] * n_out,
        sem=("parallel", "parallel"), name=name)


def ada_grad_adamw(c_t, dmod_cols, w, m, v, dep=None):
    d, nc = w.shape
    nex = c_t.shape[1]
    tr, tn = _ew_tiles(d, nc, elems=256 * 1024)

    def body(ct_ref, dm_ref, w_ref, m_ref, v_ref, g_ref, d_ref, nm_ref, nv_ref):
        ct = ct_ref[...]
        cact = ct * _sigmoid(ct)
        dm = dm_ref[...]
        g = cact[:, 0:1] * dm[0:1, :]
        for b in range(1, nex):
            g = g + cact[:, b:b + 1] * dm[b:b + 1, :]
        g_ref[...] = g
        d_ref[...], nm_ref[...], nv_ref[...] = _adamw_math(w_ref[...], g, m_ref[...], v_ref[...])

    blk = pl.BlockSpec((tr, tn), lambda i, j: (i, j))
    return _pallas(
        body, [c_t, dmod_cols, w, m, v], dep=dep, grid=(d // tr, nc // tn),
        in_specs=[pl.BlockSpec((tr, nex), lambda i, j: (i, 0)), pl.BlockSpec((nex, tn), lambda i, j: (0, j)),
                  blk, blk, blk],
        out_specs=[blk] * 4,
        out_shape=[jax.ShapeDtypeStruct((d, nc), F32)] * 4,
        sem=("parallel", "parallel"), name="ada_grad_adamw")


def _row_pack(parts):
    cols, offs, off = [], [], 0
    for p in parts:
        n = p.shape[1]
        width = -(-n // LANES) * LANES
        cols.append(jnp.pad(p, ((0, 0), (0, width - n))) if width != n else p)
        offs.append(off)
        off += width
    return jnp.concatenate(cols, axis=1), offs


def small_sum_adamw(gathered, offs, ws, ms, vs, extra_widths, dep=None):
    ndev = gathered.shape[0]
    npar = len(ws)

    def body(ga_ref, *refs):
        w_refs, m_refs, v_refs = refs[:npar], refs[npar:2 * npar], refs[2 * npar:3 * npar]
        outs = refs[3 * npar:]
        tot = ga_ref[0]
        for s in range(1, ndev):
            tot = tot + ga_ref[s]
        for i in range(npar):
            n = ws[i].shape[1]
            g = tot[:, offs[i]:offs[i] + n]
            outs[4 * i][...] = g
            outs[4 * i + 1][...], outs[4 * i + 2][...], outs[4 * i + 3][...] = _adamw_math(
                w_refs[i][...], g, m_refs[i][...], v_refs[i][...])
        for e, n in enumerate(extra_widths):
            off = offs[npar + e]
            outs[4 * npar + e][...] = tot[:, off:off + n]

    shapes = [jax.ShapeDtypeStruct(w.shape, F32) for w in ws for _ in range(4)]
    shapes += [jax.ShapeDtypeStruct((1, n), F32) for n in extra_widths]
    return _pallas(
        body, [gathered, *ws, *ms, *vs], dep=dep, in_specs=[_VMEM] * (1 + 3 * npar), out_specs=[_VMEM] * len(shapes),
        out_shape=shapes, name="small_sum_adamw")


def _position():
    return lax.axis_index("x"), lax.axis_index("y"), lax.axis_index("c")


def _other_chips(x, y):
    return [(1 - x, y), (x, 1 - y), (1 - x, 1 - y)]


def allgather_small(block, name, dep=None):
    def body(x_ref, out_ref, send_sems, recv_sems, local_sem):
        x, y, c = _position()
        me, sibling = (x, y, c), (x, y, 1 - c)
        chips = _other_chips(x, y)

        def slot(px, py, pc):
            return out_ref.at[4 * px + 2 * py + pc]

        def copy(k, block_of, to, src=None):
            return pltpu.make_async_remote_copy(
                src_ref=slot(*block_of) if src is None else src, dst_ref=slot(*block_of),
                send_sem=send_sems.at[k], recv_sem=recv_sems.at[k], device_id=to, device_id_type=MESH)

        mine = pltpu.make_async_copy(x_ref, slot(*me), local_sem)
        mine.start()
        first = [copy(0, me, sibling, src=x_ref)]
        first += [copy(1 + j, me, (*chip, c), src=x_ref) for j, chip in enumerate(chips)]
        for cp in first:
            cp.start()
        passed = [copy(4 + j, (*chip, c), sibling) for j, chip in enumerate(chips)]
        for j, chip in enumerate(chips):
            copy(1 + j, (*chip, c), me).wait_recv()
            passed[j].start()
        copy(0, sibling, me).wait_recv()
        for j, chip in enumerate(chips):
            copy(4 + j, (*chip, 1 - c), me).wait_recv()
        for cp in first + passed:
            cp.wait_send()
        mine.wait()

    return _pallas(
        body, [block], dep=dep,
        out_shape=jax.ShapeDtypeStruct((N_DEV, *block.shape), block.dtype),
        in_specs=[_VMEM], out_specs=_VMEM,
        scratch=[pltpu.SemaphoreType.DMA((7,)), pltpu.SemaphoreType.DMA((7,)), pltpu.SemaphoreType.DMA],
        name=name)


class Started(NamedTuple):
    send_sems: Any
    recv_sems: Any
    bufs: list


def exchange_start(name, bufs, n_copies, plan, dep=None):
    nb = len(bufs)

    def body(*refs):
        for cp in plan(refs[:nb], refs[nb], refs[nb + 1]):
            cp.start()

    outs = _pallas(
        body, [pltpu.with_memory_space_constraint(b, pltpu.HBM) for b in bufs], dep=dep, name=name,
        out_shape=(pltpu.SemaphoreType.DMA((n_copies,)), pltpu.SemaphoreType.DMA((n_copies,)),
                   *[pltpu.HBM(b.shape, b.dtype) for b in bufs]),
        in_specs=[_HBM] * nb,
        out_specs=(_SEM, _SEM, *[_HBM] * nb),
        input_output_aliases={i: 2 + i for i in range(nb)},
        compiler_params=pltpu.CompilerParams(has_side_effects=_EFFECT))
    return Started(outs[0], outs[1], list(outs[2:2 + nb]))


def exchange_wait(name, started, plan, bufs=None, dep=None):
    if bufs is not None:
        started = started._replace(bufs=list(bufs))
    nb = len(started.bufs)

    def body(*refs):
        for cp in plan(refs[:nb], refs[nb], refs[nb + 1]):
            cp.wait_send()
            cp.wait_recv()

    outs = _pallas(
        body, [*started.bufs, started.send_sems, started.recv_sems], dep=dep, name=name,
        out_shape=tuple(pltpu.HBM(b.shape, b.dtype) for b in started.bufs),
        in_specs=[_HBM] * nb + [_SEM, _SEM],
        out_specs=tuple([_HBM] * nb),
        input_output_aliases={i: i for i in range(nb)},
        compiler_params=pltpu.CompilerParams(has_side_effects=_EFFECT))
    return list(outs)


def _remote(src, dst, send_sems, recv_sems, i, to):
    return pltpu.make_async_remote_copy(src_ref=src, dst_ref=dst, send_sem=send_sems.at[i], recv_sem=recv_sems.at[i],
                                        device_id=to, device_id_type=MESH)


def _half_rows(buf_rows, chip_idx, pc):
    half = buf_rows // (2 * N_CHIPS)
    return pl.ds((2 * chip_idx + pc) * half, half)


ALL_PEERS = (0, 1, 2)


def plan_gather_ici(refs, send_sems, recv_sems, peers=ALL_PEERS):
    x, y, c = _position()
    chips = _other_chips(x, y)
    copies = []
    for k, ref in enumerate(refs):
        rows = ref.at[_half_rows(ref.shape[0], 2 * x + y, c), :]
        for i, j in enumerate(peers):
            copies.append(_remote(rows, rows, send_sems, recv_sems, len(peers) * k + i, (*chips[j], c)))
    return copies


def plan_gather_d2d(refs, send_sems, recv_sems, peers=ALL_PEERS):
    x, y, c = _position()
    chips = _other_chips(x, y)
    copies = []
    for k, ref in enumerate(refs):
        for i, j in enumerate(peers):
            px, py = chips[j]
            rows = ref.at[_half_rows(ref.shape[0], 2 * px + py, c), :]
            copies.append(_remote(rows, rows, send_sems, recv_sems, len(peers) * k + i, (x, y, 1 - c)))
    return copies


def plan_pair_exchange(refs, send_sems, recv_sems):
    x, y, c = _position()
    nw = len(refs) // 2
    copies = []
    for k in range(nw):
        for chip in range(N_CHIPS):
            copies.append(_remote(refs[k].at[chip, 1 - c], refs[nw + k].at[chip], send_sems, recv_sems,
                                  N_CHIPS * k + chip, (x, y, 1 - c)))
    return copies


def plan_chip_exchange(refs, send_sems, recv_sems):
    x, y, c = _position()
    nw = len(refs) // 2
    copies = []
    for k in range(nw):
        for j, (px, py) in enumerate(_other_chips(x, y)):
            copies.append(_remote(refs[k].at[2 * px + py], refs[nw + k].at[2 * x + y], send_sems, recv_sems,
                                  3 * k + j, (px, py, c)))
    return copies


def plan_pair_share(refs, send_sems, recv_sems):
    x, y, c = _position()
    return [_remote(ref.at[c], ref.at[c], send_sems, recv_sems, k, (x, y, 1 - c)) for k, ref in enumerate(refs)]


def cast_into_slot(src, slot, n_slots, name, dep=None):
    r, n = src.shape
    tr, tn = _ew_tiles(r, n, BF16_SUBLANES)

    def body(slot_ref, s_ref, o_ref):
        o_ref[...] = s_ref[...].astype(BF16)

    return _pallas(
        body, [slot, src], dep=dep, n_prefetch=1, grid=(r // tr, n // tn),
        in_specs=[pl.BlockSpec((tr, tn), lambda i, j, sl: (i, j))],
        out_specs=pl.BlockSpec((None, tr, tn), lambda i, j, sl: (sl[0], i, j)),
        out_shape=jax.ShapeDtypeStruct((n_slots, r, n), BF16),
        sem=("parallel", "parallel"), name=name)


def pair_sum(g, r, core, name, dep=None):
    nchip, _, h, n = g.shape
    th, tn = _ew_tiles(h, n, BF16_SUBLANES)

    def body(core_ref, g_ref, r_ref, o_ref):
        o_ref[...] = (g_ref[...].astype(F32) + r_ref[...].astype(F32)).astype(BF16)

    return _pallas(
        body, [core, g, r], dep=dep, n_prefetch=1, grid=(nchip, h // th, n // tn),
        in_specs=[pl.BlockSpec((None, None, th, tn), lambda a, i, j, cr: (a, cr[0], i, j)),
                  pl.BlockSpec((None, th, tn), lambda a, i, j, cr: (a, i, j))],
        out_specs=pl.BlockSpec((None, th, tn), lambda a, i, j, cr: (a, i, j)),
        out_shape=jax.ShapeDtypeStruct((nchip, h, n), BF16),
        sem=("parallel", "parallel", "parallel"), name=name)


def chip_sum(own, got, where, name, dep=None):
    nchip, h, n = got.shape
    th, tn = _ew_tiles(h, n, BF16_SUBLANES, elems=256 * 1024)

    def body(where_ref, own_ref, *rest):
        got_refs, o_ref = rest[:nchip], rest[nchip]
        chip = where_ref[0]
        acc = None
        for s in range(nchip):
            term = jnp.where(chip == s, own_ref[...], got_refs[s][...]).astype(F32)
            acc = term if acc is None else acc + term
        o_ref[...] = acc

    def got_spec(s):
        return pl.BlockSpec((None, th, tn), lambda i, j, wr: (jnp.where(wr[0] == s, (s + 1) % nchip, s), i, j))

    return _pallas(
        body, [where, own, *[got] * nchip], dep=dep, n_prefetch=1, grid=(h // th, n // tn),
        in_specs=[pl.BlockSpec((None, th, tn), lambda i, j, wr: (wr[0], i, j))]
        + [got_spec(s) for s in range(nchip)],
        out_specs=pl.BlockSpec((None, th, tn), lambda i, j, wr: (wr[1], i, j)),
        out_shape=jax.ShapeDtypeStruct((2, h, n), F32),
        sem=("parallel", "parallel"), name=name)


def kernel(x, c, w_ada, b_ada, norm_mix_g, w_in, q_norm_g, k_norm_g, attn_sinks, rel_bias, w_attn_out, conv_w, conv_b, conv_ln_g, conv_ln_b, w_conv_out, w_mix_out, norm_ffn_g, w_ffn_in, w_ffn_out, loss_target, m_w_ada, m_b_ada, m_norm_mix_g, m_w_in, m_q_norm_g, m_k_norm_g, m_attn_sinks, m_rel_bias, m_w_attn_out, m_conv_w, m_conv_b, m_conv_ln_g, m_conv_ln_b, m_w_conv_out, m_w_mix_out, m_norm_ffn_g, m_w_ffn_in, m_w_ffn_out, v_w_ada, v_b_ada, v_norm_mix_g, v_w_in, v_q_norm_g, v_k_norm_g, v_attn_sinks, v_rel_bias, v_w_attn_out, v_conv_w, v_conv_b, v_conv_ln_g, v_conv_ln_b, v_w_conv_out, v_w_mix_out, v_norm_ffn_g, v_w_ffn_in, v_w_ffn_out):
    run = InOrder()
    xi, yi, ci = _position()
    chip = 2 * xi + yi
    me = 2 * chip + ci
    chip_arr = chip.astype(jnp.int32).reshape(1)
    core_arr = ci.astype(jnp.int32).reshape(1)
    where_arr = jnp.stack([chip, ci]).astype(jnp.int32)

    xe, tgt = x[0], loss_target[0]
    t, d = xe.shape
    hd = q_norm_g.shape[-1]
    nq = attn_sinks.shape[-1]
    aw = nq * hd
    ch = conv_b.shape[-1]
    in_width = N_CHIPS * w_in.shape[-1]
    kvw = (in_width - aw - 2 * ch - 2 * d) // 2
    nkv = kvw // hd
    dff = N_CHIPS * w_ffn_out.shape[1]
    off_k, off_v, off_ca = aw, aw + kvw, aw + 2 * kvw
    off_cb, off_ga, off_gc = off_ca + ch, off_ca + 2 * ch, off_ca + 2 * ch + d
    nc_ada = w_ada.shape[-1]
    ch_loc = conv_w.shape[-1]
    nj_ffn = w_ffn_in.shape[-1]
    perm_ffn = ffn_perm(N_CHIPS)

    big = {"w_in": w_in[0], "w_attn_out": w_attn_out[0], "w_conv_out": w_conv_out[0], "w_mix_out": w_mix_out[0],
           "w_ffn_in": w_ffn_in[0], "w_ffn_out": w_ffn_out[0]}
    moments = {"w_in": (m_w_in, v_w_in), "w_attn_out": (m_w_attn_out, v_w_attn_out),
               "w_conv_out": (m_w_conv_out, v_w_conv_out), "w_mix_out": (m_w_mix_out, v_w_mix_out),
               "w_ffn_in": (m_w_ffn_in, v_w_ffn_in), "w_ffn_out": (m_w_ffn_out, v_w_ffn_out)}
    gather_groups = {"in": ["w_in"], "branch_out": ["w_attn_out", "w_conv_out"], "mix_out": ["w_mix_out"],
                     "ffn_in": ["w_ffn_in"], "ffn_out": ["w_ffn_out"]}
    grads, deltas, new_m, new_v = {}, {}, {}, {}

    def gather_cast(gname):
        bufs = []
        for n in gather_groups[gname]:
            r, ncol = big[n].shape
            bufs.append(run(cast_into_slot, big[n], chip_arr, N_CHIPS, "cast_" + n).reshape(N_CHIPS * r, ncol))
        return bufs

    def gather_ici_start(gname, bufs):
        return run(exchange_start, "gather_ici_start_" + gname, bufs, 3 * len(bufs), plan_gather_ici)

    def gather_pass_on(gname, ici):
        landed = run(exchange_wait, "gather_ici_wait_" + gname, ici, plan_gather_ici)
        return run(exchange_start, "gather_d2d_start_" + gname, landed, 3 * len(landed), plan_gather_d2d)

    def gathered(gname, d2d):
        outs = run(exchange_wait, "gather_d2d_wait_" + gname, d2d, plan_gather_d2d)
        return [o.reshape(N_CHIPS, *big[n].shape) for o, n in zip(outs, gather_groups[gname])]

    def rs_pair_start(gname, names, partials):
        blocks = [g.reshape(N_CHIPS, 2, big[n].shape[0] // 2, big[n].shape[1]) for n, g in zip(names, partials)]
        land = [lax.empty((N_CHIPS,) + b.shape[2:], BF16) for b in blocks]
        return run(exchange_start, "pair_exchange_start_" + gname, blocks + land, N_CHIPS * len(blocks),
                   plan_pair_exchange)

    def rs_chip_start(gname, names, pair):
        nw = len(names)
        outs = run(exchange_wait, "pair_exchange_wait_" + gname, pair, plan_pair_exchange)
        sums = [run(pair_sum, g, r, core_arr, "pair_sum_" + n) for n, g, r in zip(names, outs[:nw], outs[nw:])]
        land = [lax.empty(s.shape, BF16) for s in sums]
        return run(exchange_start, "chip_exchange_start_" + gname, sums + land, 3 * nw, plan_chip_exchange)

    def rs_share_start(gname, names, chipx):
        nw = len(names)
        outs = run(exchange_wait, "chip_exchange_wait_" + gname, chipx, plan_chip_exchange)
        halves = [run(chip_sum, s, r, where_arr, "chip_sum_" + n) for n, s, r in zip(names, outs[:nw], outs[nw:])]
        return run(exchange_start, "pair_share_start_" + gname, halves, nw, plan_pair_share)

    def rs_finish(gname, names, share):
        fulls = run(exchange_wait, "pair_share_wait_" + gname, share, plan_pair_share)
        for n, g2 in zip(names, fulls):
            g, dl, nm, nv = run(adamw, big[n], g2.reshape(big[n].shape), moments[n][0][0], moments[n][1][0],
                                "adamw_" + n, copy_grad=True)
            grads[n], deltas[n], new_m[n], new_v[n] = g[None], dl[None], nm[None], nv[None]

    near, far = (0, 1), (2,)
    plan_ici_near = functools.partial(plan_gather_ici, peers=near)
    plan_ici_far = functools.partial(plan_gather_ici, peers=far)
    plan_d2d_near = functools.partial(plan_gather_d2d, peers=near)
    plan_d2d_far = functools.partial(plan_gather_d2d, peers=far)
    bufs_in = gather_cast("in")
    row1, offs1 = _row_pack([c, conv_w[0].reshape(1, CONV_WIDTH * ch_loc)])
    got1 = run(allgather_small, row1, "allgather_cond")
    ici_near = run(exchange_start, "gather_ici_start_in_near", bufs_in, len(near), plan_ici_near)
    c_all = got1[:, 0, :d]
    conv_w_full = got1[0::2, 0, offs1[1]:offs1[1] + CONV_WIDTH * ch_loc].reshape(N_CHIPS, CONV_WIDTH, ch_loc)
    conv_w_full = jnp.transpose(conv_w_full, (1, 0, 2)).reshape(CONV_WIDTH, ch)
    conv_w_pad = jnp.pad(conv_w_full, ((0, 1), (0, 0)))
    c_t = jnp.transpose(c_all)
    mod_cols = run(ada_fwd, c_t, w_ada[0])
    rest_bufs = {gname: gather_cast(gname) for gname in gather_groups if gname != "in"}
    got2 = run(allgather_small, mod_cols, "allgather_mod")
    mod_all = got2.reshape(N_CHIPS, 2, N_DEV, nc_ada)[:, 0]
    mod = lax.dynamic_slice_in_dim(mod_all, me, 1, axis=1).reshape(1, N_CHIPS * nc_ada) + b_ada
    mod = jnp.pad(mod.reshape(N_MOD, d), ((0, SUBLANES - N_MOD), (0, 0)))

    h = run(pre_mix_fwd, xe, mod, norm_mix_g)
    bucket = _t5_bucket_table()
    bucket_p, bucket_c = jnp.asarray(bucket[:, :BLOCK]), jnp.asarray(bucket[:, BLOCK:])
    bias_p, bias_c = run(bias_table, rel_bias, bucket_p, bucket_c)

    def in_blocks(buf):
        return buf.reshape(N_CHIPS, *big["w_in"].shape)

    def chip_ids(peers):
        others = [2 * (1 - xi) + yi, 2 * xi + (1 - yi), 2 * (1 - xi) + (1 - yi)]
        return jnp.stack([others[j] for j in peers]).astype(jnp.int32)

    tn_in = big["w_in"].shape[1]
    landed = run(exchange_wait, "gather_ici_wait_in_near", ici_near, plan_ici_near)
    ici_far = run(exchange_start, "gather_ici_start_in_far", landed, len(far), plan_ici_far)
    ici = {gname: gather_ici_start(gname, rest_bufs[gname]) for gname in ("branch_out", "mix_out")}
    ici_near_ffn = run(exchange_start, "gather_ici_start_ffn_in_near", rest_bufs["ffn_in"], len(near), plan_ici_near)
    d2d_near = run(exchange_start, "gather_d2d_start_in_near", ici_far.bufs, len(near), plan_d2d_near)
    p = run(mm_nn_blocks, h, in_blocks(d2d_near.bufs[0]), chip_arr, None, tn=tn_in, out_dtype=BF16, name="mm_in_own")
    landed = run(exchange_wait, "gather_d2d_wait_in_near", d2d_near, plan_d2d_near)
    p = run(mm_nn_blocks, h, in_blocks(landed[0]), chip_ids(near), p, tn=tn_in, out_dtype=BF16, name="mm_in_near")
    landed = run(exchange_wait, "gather_ici_wait_in_far", ici_far, plan_ici_far, bufs=landed)
    d2d_far = run(exchange_start, "gather_d2d_start_in_far", landed, len(far), plan_d2d_far)
    landed = run(exchange_wait, "gather_d2d_wait_in_far", d2d_far, plan_d2d_far)
    wg_in = in_blocks(landed[0])
    p = run(mm_nn_blocks, h, wg_in, chip_ids(far), p, tn=tn_in, out_dtype=BF16, name="mm_in_far")
    d2d_branch = gather_pass_on("branch_out", ici["branch_out"])

    sinks3 = attn_sinks.reshape(nq, 1, 1)
    attn_o = run(attn_fwd, p, bias_p, bias_c, sinks3, q_norm_g, k_norm_g, aw=aw, kvw=kvw)
    ca, cb = p[:, off_ca:off_cb], p[:, off_cb:off_ga]
    s_conv, co_conv = run(conv_fwd, ca, cb, conv_w_pad, conv_b, conv_ln_g, conv_ln_b)
    wg_attn_out, wg_conv_out = gathered("branch_out", d2d_branch)
    y_attn = run(mm_nn, attn_o, wg_attn_out, tn=_tile(wg_attn_out.shape[2], 512), tk=aw, out_dtype=BF16,
                 name="mm_attn_out")
    y_conv = run(mm_nn, s_conv, wg_conv_out, tn=_tile(wg_conv_out.shape[2], 512), tk=ch, out_dtype=BF16,
                 name="mm_conv_out")
    landed = run(exchange_wait, "gather_ici_wait_ffn_in_near", ici_near_ffn, plan_ici_near)
    ici_far_ffn = run(exchange_start, "gather_ici_start_ffn_in_far", landed, len(far), plan_ici_far)
    ici["ffn_out"] = gather_ici_start("ffn_out", rest_bufs["ffn_out"])
    d2d_near_ffn = run(exchange_start, "gather_d2d_start_ffn_in_near", ici_far_ffn.bufs, len(near), plan_d2d_near)
    merged = run(merge_fwd, p, y_attn, y_conv, off_ga, off_gc)
    d2d_mix = gather_pass_on("mix_out", ici["mix_out"])
    (wg_mix_out,) = gathered("mix_out", d2d_mix)
    wg_mix_out = wg_mix_out.reshape(1, d, d)
    o_m = run(mm_nn, merged, wg_mix_out, tn=_tile(d, 512), tk=d, out_dtype=F32, name="mm_mix_out")
    x1, h2 = run(pre_ffn_fwd, xe, o_m, mod, norm_ffn_g)

    def ffn_blocks(buf):
        return buf.reshape(N_CHIPS, *big["w_ffn_in"].shape)

    tn_ffn = _tile(nj_ffn, 1408)
    landed = run(exchange_wait, "gather_d2d_wait_ffn_in_near", d2d_near_ffn, plan_d2d_near)
    f = run(mm_nn_blocks, h2, ffn_blocks(landed[0]), chip_arr, None, tn=tn_ffn, out_dtype=BF16, perm=perm_ffn,
            name="mm_ffn_in_own")
    landed = run(exchange_wait, "gather_ici_wait_ffn_in_far", ici_far_ffn, plan_ici_far, bufs=landed)
    d2d_far_ffn = run(exchange_start, "gather_d2d_start_ffn_in_far", landed, len(far), plan_d2d_far)
    f = run(mm_nn_blocks, h2, ffn_blocks(d2d_far_ffn.bufs[0]), chip_ids(near), f, tn=tn_ffn, out_dtype=BF16,
            perm=perm_ffn, name="mm_ffn_in_near")
    landed = run(exchange_wait, "gather_d2d_wait_ffn_in_far", d2d_far_ffn, plan_d2d_far)
    wg_ffn_in = ffn_blocks(landed[0])
    f = run(mm_nn_blocks, h2, wg_ffn_in, chip_ids(far), f, tn=tn_ffn, out_dtype=BF16, perm=perm_ffn,
            name="mm_ffn_in_far")
    d2d_ffn_out = gather_pass_on("ffn_out", ici["ffn_out"])
    act = run(swiglu_fwd, f, nj_ffn)
    (wg_ffn_out,) = gathered("ffn_out", d2d_ffn_out)
    wg_ffn_out = wg_ffn_out.reshape(1, dff, d)
    o_f = run(mm_nn, act, wg_ffn_out, tn=_tile(d, 512), tk=_tile(dff, 2816), out_dtype=F32, name="mm_ffn_out")
    loss11, dy, dof, acc_l = run(loss_head, x1, o_f, tgt, mod)

    gw_ffn_out = run(mm_tn, act, dof, 1, tk=_tile(dff, 512), tn=d, name="mm_ffn_out_dw")
    px_ffn_out = rs_pair_start("ffn_out", ["w_ffn_out"], [gw_ffn_out])
    dact = run(mm_nt, dof, wg_ffn_out, tko=_tile(dff, 512), tn=d, out_dtype=BF16, name="mm_ffn_out_dx")
    cx_ffn_out = rs_chip_start("ffn_out", ["w_ffn_out"], px_ffn_out)
    df = run(swiglu_bwd, f, dact, nj_ffn)
    gw_ffn_in = run(mm_tn, h2, df, N_CHIPS, tk=d, tn=_tile(nj_ffn, 1408), name="mm_ffn_in_dw",
                    perm=perm_ffn)
    px_ffn_in = rs_pair_start("ffn_in", ["w_ffn_in"], [gw_ffn_in])
    dh2 = run(mm_nt, df, wg_ffn_in, tko=_tile(d, 512), tn=nj_ffn, name="mm_ffn_in_dx", perm=perm_ffn)
    sh_ffn_out = rs_share_start("ffn_out", ["w_ffn_out"], cx_ffn_out)
    cx_ffn_in = rs_chip_start("ffn_in", ["w_ffn_in"], px_ffn_in)
    dx1, dom, acc_f = run(pre_ffn_bwd, x1, dh2, dy, o_m, mod, norm_ffn_g)
    gw_mix_out = run(mm_tn, merged, dom, 1, tk=d, tn=_tile(d, 1024), name="mm_mix_out_dw")
    px_mix = rs_pair_start("mix_out", ["w_mix_out"], [gw_mix_out])
    dmerged = run(mm_nt, dom, wg_mix_out, tko=_tile(d, 512), tn=d, out_dtype=BF16, name="mm_mix_out_dx")
    dy_attn, dy_conv, dga, dgc = run(merge_bwd, p, y_attn, y_conv, dmerged, off_ga, off_gc)
    rs_finish("ffn_out", ["w_ffn_out"], sh_ffn_out)
    cx_mix = rs_chip_start("mix_out", ["w_mix_out"], px_mix)
    gw_attn_out = run(mm_tn, attn_o, dy_attn, N_CHIPS, tk=aw, tn=_tile(wg_attn_out.shape[2], 512),
                      name="mm_attn_out_dw")
    gw_conv_out = run(mm_tn, s_conv, dy_conv, N_CHIPS, tk=ch, tn=_tile(wg_conv_out.shape[2], 512),
                      name="mm_conv_out_dw")
    ac_names = ["w_attn_out", "w_conv_out"]
    px_ac = rs_pair_start("attn_conv_out", ac_names, [gw_attn_out, gw_conv_out])
    dattn_o = run(mm_nt, dy_attn, wg_attn_out, tko=_tile(aw, 1024), tn=_tile(wg_attn_out.shape[2], 512),
                  out_dtype=BF16, name="mm_attn_out_dx")
    ds_conv = run(mm_nt, dy_conv, wg_conv_out, tko=_tile(ch, 1024), tn=_tile(wg_conv_out.shape[2], 512),
                  out_dtype=BF16, name="mm_conv_out_dx")
    cx_ac = rs_chip_start("attn_conv_out", ac_names, px_ac)
    dca, dcb, dconv_w, dconv_vec = run(conv_bwd, ca, cb, co_conv, ds_conv, conv_w_pad, conv_ln_g, conv_ln_b)
    sh_ffn_in = rs_share_start("ffn_in", ["w_ffn_in"], cx_ffn_in)
    dqkv, dbp, dbc, dsinks, dqg, dkg = run(attn_bwd, p, bias_p, bias_c, sinks3, q_norm_g, k_norm_g, dattn_o,
                                           aw=aw, kvw=kvw)
    sh_mix = rs_share_start("mix_out", ["w_mix_out"], cx_mix)
    sh_ac = rs_share_start("attn_conv_out", ac_names, cx_ac)
    drel = run(bias_table_bwd, dbp, dbc, bucket_p, bucket_c).reshape(NUM_BUCKETS, nq)
    dp = jnp.concatenate([dqkv, dca, dcb, dga, dgc], axis=1)
    gw_in = run(mm_tn, h, dp, N_CHIPS, tk=d, tn=wg_in.shape[2], name="mm_in_dw")
    px_in = rs_pair_start("in", ["w_in"], [gw_in])
    dh = run(mm_nt, dp, wg_in, tko=_tile(d, 1024), tn=wg_in.shape[2], name="mm_in_dx")
    grad_x, acc_m = run(pre_mix_bwd, xe, dh, dx1, mod, norm_mix_g)

    dmod = jnp.concatenate([acc_m[0:1], acc_m[1:2], acc_f[3:4], acc_f[0:1], acc_f[1:2], acc_l[0:1]], axis=1)
    small_names = ["b_ada", "norm_mix_g", "q_norm_g", "k_norm_g", "attn_sinks", "rel_bias", "conv_b", "conv_ln_g",
                   "conv_ln_b", "norm_ffn_g"]
    small_w = [b_ada, norm_mix_g, q_norm_g, k_norm_g, attn_sinks, rel_bias, conv_b, conv_ln_g, conv_ln_b, norm_ffn_g]
    small_m = [m_b_ada, m_norm_mix_g, m_q_norm_g, m_k_norm_g, m_attn_sinks, m_rel_bias, m_conv_b, m_conv_ln_g,
               m_conv_ln_b, m_norm_ffn_g]
    small_v = [v_b_ada, v_norm_mix_g, v_q_norm_g, v_k_norm_g, v_attn_sinks, v_rel_bias, v_conv_b, v_conv_ln_g,
               v_conv_ln_b, v_norm_ffn_g]
    small_g = [dmod, acc_m[2:3], dqg, dkg, dsinks.reshape(1, nq), drel.reshape(1, NUM_BUCKETS * nq),
               dconv_vec[0:1], dconv_vec[1:2], dconv_vec[2:3], acc_f[2:3]]
    row3, offs3 = _row_pack(small_g + [dconv_w[:CONV_WIDTH].reshape(1, CONV_WIDTH * ch), loss11])
    got3 = run(allgather_small, row3, "allgather_small_grads")
    cx_in = rs_chip_start("in", ["w_in"], px_in)
    as_row = lambda a: a.reshape(1, -1)
    outs3 = run(small_sum_adamw, got3, offs3, [as_row(a) for a in small_w], [as_row(a) for a in small_m],
                [as_row(a) for a in small_v], [CONV_WIDTH * ch, 1])
    for i, (n, w) in enumerate(zip(small_names, small_w)):
        grads[n], deltas[n], new_m[n], new_v[n] = (o.reshape(w.shape) for o in outs3[4 * i:4 * i + 4])
    g_conv_w_all, loss_sum = outs3[-2].reshape(CONV_WIDTH, ch), outs3[-1]

    g_conv_w = lax.dynamic_slice_in_dim(g_conv_w_all, chip * ch_loc, ch_loc, axis=1)
    grads["conv_w"] = g_conv_w[None]
    dl, nm, nv = run(adamw, conv_w[0], g_conv_w, m_conv_w[0], v_conv_w[0], "adamw_conv_w")
    deltas["conv_w"], new_m["conv_w"], new_v["conv_w"] = dl[None], nm[None], nv[None]

    dmod_all = got3[:, 0, :N_MOD * d]
    dmod_cols = lax.dynamic_slice_in_dim(dmod_all, chip * nc_ada, nc_ada, axis=1)
    g_ada, dl, nm, nv = run(ada_grad_adamw, c_t, dmod_cols, w_ada[0], m_w_ada[0], v_w_ada[0])
    grads["w_ada"], deltas["w_ada"], new_m["w_ada"], new_v["w_ada"] = g_ada[None], dl[None], nm[None], nv[None]

    rs_finish("ffn_in", ["w_ffn_in"], sh_ffn_in)
    rs_finish("mix_out", ["w_mix_out"], sh_mix)
    rs_finish("attn_conv_out", ac_names, sh_ac)
    sh_in = rs_share_start("in", ["w_in"], cx_in)
    rs_finish("in", ["w_in"], sh_in)

    loss = loss_sum[0, 0]
    order = ["w_ada", "b_ada", "norm_mix_g", "w_in", "q_norm_g", "k_norm_g", "attn_sinks", "rel_bias", "w_attn_out",
             "conv_w", "conv_b", "conv_ln_g", "conv_ln_b", "w_conv_out", "w_mix_out", "norm_ffn_g", "w_ffn_in",
             "w_ffn_out"]
    return (loss, grad_x[None], *[grads[n] for n in order], *[deltas[n] for n in order],
            *[new_m[n] for n in order], *[new_v[n] for n in order])
```

```python
import functools
import math
from typing import Any, NamedTuple

import jax
import jax.numpy as jnp
import numpy as np
from jax import lax
from jax.experimental import pallas as pl
from jax.experimental.pallas import tpu as pltpu

F32 = jnp.float32
BF16 = jnp.bfloat16
MESH = pl.DeviceIdType.MESH

V7X_VMEM_BYTES = 64 * 1024 * 1024
VMEM_LIMIT = V7X_VMEM_BYTES - 8 * 1024 * 1024
LANES = 128
SUBLANES = 8
BF16_SUBLANES = 16

EPS = 1e-6
WINDOW = 128
BLOCK = 128
NUM_BUCKETS = 32
MAX_EXACT = NUM_BUCKETS // 2
MAX_DISTANCE = 128
CONV_WIDTH = 31
CONV_HALO = 32
ADAM_LR = 0.001
ADAM_B1 = 0.9
ADAM_B2 = 0.999
ADAM_EPS = 1e-08
ADAM_WD = 0.01
ADAM_STEP = 10
N_MOD = 6
SH_M, SC_M, GT_M, SH_F, SC_F, GT_F = range(6)

N_CHIPS = 4
N_DEV = 8

_ANY = pl.BlockSpec(memory_space=pl.ANY)
_VMEM = pl.BlockSpec(memory_space=pltpu.VMEM)
_SMEM = pl.BlockSpec(memory_space=pltpu.SMEM)
_HBM = pl.BlockSpec(memory_space=pltpu.HBM)
_SEM = pl.BlockSpec(memory_space=pltpu.SEMAPHORE)
_EFFECT = pltpu.SideEffectType.DATAFLOW_SIDE_EFFECTING


class InOrder:
    def __init__(self):
        self.token = None

    def __call__(self, fn, *args, **kw):
        return fn(*args, dep=self, **kw)


def _pallas(body, args, *, in_specs, out_specs, out_shape, name, dep=None, grid=(), n_prefetch=0, scratch=(),
            sem=None, **kw):
    n_lead = n_prefetch + len(in_specs)
    in_specs, args = list(in_specs), list(args)
    single = not isinstance(out_shape, (list, tuple))
    out_shapes = [out_shape] if single else list(out_shape)
    out_specs = [out_specs] if single else list(out_specs)
    if dep is not None:
        inner, n_out, takes = body, len(out_shapes), dep.token is not None

        def body(*refs):
            rest = refs[n_lead + (1 if takes else 0):]
            rest[n_out][...] = jnp.zeros((SUBLANES, LANES), F32)
            return inner(*refs[:n_lead], *rest[:n_out], *rest[n_out + 1:])

        if takes:
            in_specs.append(_ANY)
            args.append(dep.token)
        out_shapes.append(jax.ShapeDtypeStruct((SUBLANES, LANES), F32))
        out_specs.append(pl.BlockSpec((SUBLANES, LANES), lambda *_: (0, 0)))
    params = kw.pop("compiler_params", None)
    if params is None:
        params = pltpu.CompilerParams(dimension_semantics=sem, vmem_limit_bytes=VMEM_LIMIT)
    outs = pl.pallas_call(
        body,
        grid_spec=pltpu.PrefetchScalarGridSpec(num_scalar_prefetch=n_prefetch, grid=grid, in_specs=in_specs,
                                               out_specs=out_specs, scratch_shapes=list(scratch)),
        out_shape=out_shapes, compiler_params=params, name=name, **kw,
    )(*args)
    if dep is not None:
        dep.token = outs[-1]
        outs = outs[:-1]
    return outs[0] if single else list(outs)


def _tile(n, pref, unit=LANES):
    best = None
    for t in range(unit, min(n, pref) + 1, unit):
        if n % t == 0:
            best = t
    return best if best is not None else n


def _sigmoid(v):
    return 1.0 / (1.0 + jnp.exp(-v.astype(F32)))


ROW_CHUNK = 512


def _row_chunks(m, unit=SUBLANES):
    step = _tile(m, ROW_CHUNK, unit)
    return [(s, step) for s in range(0, m, step)]


def _ew_tiles(r, n, unit=SUBLANES, elems=512 * 1024):
    return _tile(r, max(unit, elems // n), unit), n


def _block_pos(j, perm):
    if perm is None:
        return j
    pos = 0
    for a, p in enumerate(perm):
        pos = pos + jnp.where(j == a, p, 0)
    return pos


def mm_nn(a, w, *, tn, tk, out_dtype, name, perm=None, dep=None):
    m, k = a.shape
    j, k2, nj = w.shape
    assert k == k2 and nj % tn == 0 and k % tk == 0
    npj, nk = nj // tn, k // tk

    def body(a_ref, w_ref, o_ref, *scratch):
        kk = pl.program_id(1)
        for s, sz in _row_chunks(m):
            rows = pl.ds(s, sz)
            p = jnp.dot(a_ref[rows, :], w_ref[...], preferred_element_type=F32)
            if nk == 1:
                o_ref[rows, :] = p.astype(out_dtype)
            else:
                acc = scratch[0]

                @pl.when(kk == 0)
                def _():
                    acc[rows, :] = p

                @pl.when(kk > 0)
                def _():
                    acc[rows, :] += p

                @pl.when(kk == nk - 1)
                def _():
                    o_ref[rows, :] = acc[rows, :].astype(out_dtype)

    return _pallas(
        body, [a, w], dep=dep, grid=(j * npj, nk),
        in_specs=[
            pl.BlockSpec((m, tk), lambda n, kk: (0, kk)),
            pl.BlockSpec((None, tk, tn), lambda n, kk: (n // npj, kk, n % npj)),
        ],
        out_specs=pl.BlockSpec((m, tn), lambda n, kk: (0, _block_pos(n // npj, perm) * npj + n % npj)),
        out_shape=jax.ShapeDtypeStruct((m, j * nj), out_dtype),
        scratch=[pltpu.VMEM((m, tn), F32)] if nk > 1 else [],
        sem=("parallel", "arbitrary"), name=name)


def mm_nn_blocks(a, w, blocks, into, *, tn, out_dtype, name, perm=None, dep=None):
    m, k = a.shape
    j, k2, nj = w.shape
    assert k == k2 and nj % tn == 0
    npj = nj // tn
    n_in = 2 if into is None else 3

    def body(blocks_ref, a_ref, w_ref, *rest):
        o_ref = rest[n_in - 2]
        for s, sz in _row_chunks(m):
            rows = pl.ds(s, sz)
            o_ref[rows, :] = jnp.dot(a_ref[rows, :], w_ref[...], preferred_element_type=F32).astype(out_dtype)

    return _pallas(
        body, [blocks, a, w] + ([] if into is None else [into]), dep=dep, n_prefetch=1,
        grid=(blocks.shape[0] * npj,),
        in_specs=[pl.BlockSpec((m, k), lambda n, bl: (0, 0)),
                  pl.BlockSpec((None, k, tn), lambda n, bl: (bl[n // npj], 0, n % npj))]
        + ([] if into is None else [_ANY]),
        out_specs=pl.BlockSpec((m, tn), lambda n, bl: (0, _block_pos(bl[n // npj], perm) * npj + n % npj)),
        out_shape=jax.ShapeDtypeStruct((m, j * nj), out_dtype),
        input_output_aliases={} if into is None else {3: 0},
        sem=("arbitrary",), name=name)


def mm_nt(g, w, *, tko, tn, name, out_dtype=F32, perm=None, dep=None):
    m, n = g.shape
    j, k, nj = w.shape
    assert n == j * nj and nj % tn == 0 and k % tko == 0
    npj, nr = nj // tn, n // tn
    in_place = out_dtype == F32

    def body(g_ref, w_ref, o_ref, *scratch):
        r = pl.program_id(1)
        acc = o_ref if in_place else (scratch[0] if nr > 1 else None)
        for s, sz in _row_chunks(m):
            rows = pl.ds(s, sz)
            p = lax.dot_general(g_ref[rows, :], w_ref[...], (((1,), (1,)), ((), ())), preferred_element_type=F32)
            if acc is None:
                o_ref[rows, :] = p.astype(out_dtype)
                continue

            @pl.when(r == 0)
            def _():
                acc[rows, :] = p

            @pl.when(r > 0)
            def _():
                acc[rows, :] += p

            if not in_place:
                @pl.when(r == nr - 1)
                def _():
                    o_ref[rows, :] = acc[rows, :].astype(out_dtype)

    return _pallas(
        body, [g, w], dep=dep, grid=(k // tko, nr),
        in_specs=[
            pl.BlockSpec((m, tn), lambda ko, r: (0, _block_pos(r // npj, perm) * npj + r % npj)),
            pl.BlockSpec((None, tko, tn), lambda ko, r: (r // npj, ko, r % npj)),
        ],
        out_specs=pl.BlockSpec((m, tko), lambda ko, r: (0, ko)),
        out_shape=jax.ShapeDtypeStruct((m, k), out_dtype),
        scratch=[pltpu.VMEM((m, tko), F32)] if (nr > 1 and not in_place) else [],
        sem=("parallel", "arbitrary"), name=name)


def mm_tn(a, g, n_blocks, *, tk, tn, name, perm=None, dep=None):
    m, k = a.shape
    m2, n = g.shape
    nj = n // n_blocks
    assert m == m2 and nj % tn == 0 and k % tk == 0
    npj = nj // tn

    def body(a_ref, g_ref, o_ref):
        for s, sz in _row_chunks(tk, LANES):
            p = lax.dot_general(a_ref[:, pl.ds(s, sz)], g_ref[...], (((0,), (0,)), ((), ())),
                                preferred_element_type=F32)
            o_ref[pl.ds(s, sz), :] = p.astype(BF16)

    return _pallas(
        body, [a, g], dep=dep, grid=(k // tk, n // tn),
        in_specs=[
            pl.BlockSpec((m, tk), lambda kk, nn: (0, kk)),
            pl.BlockSpec((m, tn), lambda kk, nn: (0, _block_pos(nn // npj, perm) * npj + nn % npj)),
        ],
        out_specs=pl.BlockSpec((None, tk, tn), lambda kk, nn: (nn // npj, kk, nn % npj)),
        out_shape=jax.ShapeDtypeStruct((n_blocks, k, nj), BF16),
        sem=("parallel", "parallel"), name=name)


ROW_TILE = 256


def _row_spec(tr, width):
    return pl.BlockSpec((tr, width), lambda i: (i, 0))


def _full_spec(shape):
    return pl.BlockSpec(shape, lambda *_: (0,) * len(shape))


def _rms(xv):
    return lax.rsqrt(jnp.mean(xv * xv, axis=-1, keepdims=True) + EPS)


def _mod_row(mod_ref, row):
    return mod_ref[pl.ds(row, 1), :]


def pre_mix_fwd(x, mod, gain, dep=None):
    t, d = x.shape
    tr = _tile(t, ROW_TILE, SUBLANES)

    def body(x_ref, mod_ref, g_ref, h_ref):
        xv = x_ref[...]
        y = xv * _rms(xv) * g_ref[...]
        h_ref[...] = (y * (1.0 + _mod_row(mod_ref, SC_M)) + _mod_row(mod_ref, SH_M)).astype(BF16)

    return _pallas(
        body, [x, mod, gain], dep=dep, grid=(t // tr,),
        in_specs=[_row_spec(tr, d), _full_spec(mod.shape), _full_spec(gain.shape)],
        out_specs=_row_spec(tr, d),
        out_shape=jax.ShapeDtypeStruct((t, d), BF16),
        sem=("parallel",), name="pre_mix_fwd")


def pre_ffn_fwd(x, o_m, mod, gain, dep=None):
    t, d = x.shape
    tr = _tile(t, ROW_TILE, SUBLANES)

    def body(x_ref, om_ref, mod_ref, g_ref, x1_ref, h_ref):
        x1 = x_ref[...] + _mod_row(mod_ref, GT_M) * om_ref[...]
        x1_ref[...] = x1
        y = x1 * _rms(x1) * g_ref[...]
        h_ref[...] = (y * (1.0 + _mod_row(mod_ref, SC_F)) + _mod_row(mod_ref, SH_F)).astype(BF16)

    return _pallas(
        body, [x, o_m, mod, gain], dep=dep, grid=(t // tr,),
        in_specs=[_row_spec(tr, d), _row_spec(tr, d), _full_spec(mod.shape), _full_spec(gain.shape)],
        out_specs=[_row_spec(tr, d), _row_spec(tr, d)],
        out_shape=[jax.ShapeDtypeStruct((t, d), F32), jax.ShapeDtypeStruct((t, d), BF16)],
        sem=("parallel",), name="pre_ffn_fwd")


def loss_head(x1, o_f, target, mod, dep=None):
    t, d = x1.shape
    tr = _tile(t, ROW_TILE, SUBLANES)

    def body(x1_ref, of_ref, tg_ref, mod_ref, loss_ref, dy_ref, dof_ref, acc_ref):
        i = pl.program_id(0)
        gt = _mod_row(mod_ref, GT_F)
        of = of_ref[...]
        err = x1_ref[...] + gt * of - tg_ref[...]
        dy = err * (1.0 / d)
        dy_ref[...] = dy
        dof_ref[...] = (dy * gt).astype(BF16)
        part = (0.5 / d) * jnp.sum(jnp.sum(err * err, axis=1, keepdims=True), axis=0, keepdims=True)
        dgt = jnp.sum(dy * of, axis=0, keepdims=True)

        @pl.when(i == 0)
        def _():
            loss_ref[...] = jnp.zeros_like(loss_ref)
            acc_ref[...] = jnp.zeros_like(acc_ref)

        loss_ref[...] += part
        acc_ref[pl.ds(0, 1), :] += dgt

    return _pallas(
        body, [x1, o_f, target, mod], dep=dep, grid=(t // tr,),
        in_specs=[_row_spec(tr, d), _row_spec(tr, d), _row_spec(tr, d), _full_spec(mod.shape)],
        out_specs=[_full_spec((1, 1)), _row_spec(tr, d), _row_spec(tr, d), _full_spec((SUBLANES, d))],
        out_shape=[jax.ShapeDtypeStruct((1, 1), F32), jax.ShapeDtypeStruct((t, d), F32),
                   jax.ShapeDtypeStruct((t, d), BF16), jax.ShapeDtypeStruct((SUBLANES, d), F32)],
        sem=("arbitrary",), name="loss_head")


def _norm_bwd(xv, dh, sc, gain):
    rstd = _rms(xv)
    yn = xv * rstd
    dsh = jnp.sum(dh, axis=0, keepdims=True)
    dsc = jnp.sum(dh * (yn * gain), axis=0, keepdims=True)
    dgain = jnp.sum(dh * (1.0 + sc) * yn, axis=0, keepdims=True)
    dyn = dh * ((1.0 + sc) * gain)
    dx = rstd * (dyn - yn * jnp.mean(dyn * yn, axis=-1, keepdims=True))
    return dx, dsh, dsc, dgain


def pre_ffn_bwd(x1, dh2, dy, o_m, mod, gain, dep=None):
    t, d = x1.shape
    tr = _tile(t, ROW_TILE, SUBLANES)

    def body(x1_ref, dh_ref, dy_ref, om_ref, mod_ref, g_ref, dx1_ref, dom_ref, acc_ref):
        i = pl.program_id(0)
        dxn, dsh, dsc, dgain = _norm_bwd(x1_ref[...], dh_ref[...], _mod_row(mod_ref, SC_F), g_ref[...])
        dx1 = dy_ref[...] + dxn
        dx1_ref[...] = dx1
        dom_ref[...] = (dx1 * _mod_row(mod_ref, GT_M)).astype(BF16)
        dgt = jnp.sum(dx1 * om_ref[...], axis=0, keepdims=True)

        @pl.when(i == 0)
        def _():
            acc_ref[...] = jnp.zeros_like(acc_ref)

        acc_ref[pl.ds(0, 1), :] += dsh
        acc_ref[pl.ds(1, 1), :] += dsc
        acc_ref[pl.ds(2, 1), :] += dgain
        acc_ref[pl.ds(3, 1), :] += dgt

    return _pallas(
        body, [x1, dh2, dy, o_m, mod, gain], dep=dep, grid=(t // tr,),
        in_specs=[_row_spec(tr, d)] * 4 + [_full_spec(mod.shape), _full_spec(gain.shape)],
        out_specs=[_row_spec(tr, d), _row_spec(tr, d), _full_spec((SUBLANES, d))],
        out_shape=[jax.ShapeDtypeStruct((t, d), F32), jax.ShapeDtypeStruct((t, d), BF16),
                   jax.ShapeDtypeStruct((SUBLANES, d), F32)],
        sem=("arbitrary",), name="pre_ffn_bwd")


def pre_mix_bwd(x, dh, dx1, mod, gain, dep=None):
    t, d = x.shape
    tr = _tile(t, ROW_TILE, SUBLANES)

    def body(x_ref, dh_ref, dx1_ref, mod_ref, g_ref, gx_ref, acc_ref):
        i = pl.program_id(0)
        dxn, dsh, dsc, dgain = _norm_bwd(x_ref[...], dh_ref[...], _mod_row(mod_ref, SC_M), g_ref[...])
        gx_ref[...] = dx1_ref[...] + dxn

        @pl.when(i == 0)
        def _():
            acc_ref[...] = jnp.zeros_like(acc_ref)

        acc_ref[pl.ds(0, 1), :] += dsh
        acc_ref[pl.ds(1, 1), :] += dsc
        acc_ref[pl.ds(2, 1), :] += dgain

    return _pallas(
        body, [x, dh, dx1, mod, gain], dep=dep, grid=(t // tr,),
        in_specs=[_row_spec(tr, d)] * 3 + [_full_spec(mod.shape), _full_spec(gain.shape)],
        out_specs=[_row_spec(tr, d), _full_spec((SUBLANES, d))],
        out_shape=[jax.ShapeDtypeStruct((t, d), F32), jax.ShapeDtypeStruct((SUBLANES, d), F32)],
        sem=("arbitrary",), name="pre_mix_bwd")


def merge_fwd(p, y_attn, y_conv, off_ga, off_gc, dep=None):
    t, d = y_attn.shape
    tr = _tile(t, ROW_TILE, SUBLANES)
    cw = math.gcd(math.gcd(off_ga, off_gc), math.gcd(d, 512))
    nc = d // cw

    def body(ga_ref, gc_ref, ya_ref, yc_ref, o_ref):
        o_ref[...] = (_sigmoid(ga_ref[...]) * ya_ref[...] + _sigmoid(gc_ref[...]) * yc_ref[...]).astype(BF16)

    return _pallas(
        body, [p, p, y_attn, y_conv], dep=dep, grid=(t // tr, nc),
        in_specs=[pl.BlockSpec((tr, cw), lambda i, j: (i, off_ga // cw + j)),
                  pl.BlockSpec((tr, cw), lambda i, j: (i, off_gc // cw + j)),
                  pl.BlockSpec((tr, cw), lambda i, j: (i, j)),
                  pl.BlockSpec((tr, cw), lambda i, j: (i, j))],
        out_specs=pl.BlockSpec((tr, cw), lambda i, j: (i, j)),
        out_shape=jax.ShapeDtypeStruct((t, d), BF16),
        sem=("parallel", "parallel"), name="merge_fwd")


def merge_bwd(p, y_attn, y_conv, dmerged, off_ga, off_gc, dep=None):
    t, d = y_attn.shape
    tr = _tile(t, ROW_TILE, SUBLANES)
    cw = math.gcd(math.gcd(off_ga, off_gc), math.gcd(d, 512))
    nc = d // cw

    def body(ga_ref, gc_ref, ya_ref, yc_ref, dm_ref, dya_ref, dyc_ref, dga_ref, dgc_ref):
        dm = dm_ref[...].astype(F32)
        sa = _sigmoid(ga_ref[...])
        sc = _sigmoid(gc_ref[...])
        dya_ref[...] = (dm * sa).astype(BF16)
        dyc_ref[...] = (dm * sc).astype(BF16)
        dga_ref[...] = (dm * ya_ref[...] * sa * (1.0 - sa)).astype(BF16)
        dgc_ref[...] = (dm * yc_ref[...] * sc * (1.0 - sc)).astype(BF16)

    blk = pl.BlockSpec((tr, cw), lambda i, j: (i, j))
    return _pallas(
        body, [p, p, y_attn, y_conv, dmerged], dep=dep, grid=(t // tr, nc),
        in_specs=[pl.BlockSpec((tr, cw), lambda i, j: (i, off_ga // cw + j)),
                  pl.BlockSpec((tr, cw), lambda i, j: (i, off_gc // cw + j)), blk, blk, blk],
        out_specs=[blk] * 4,
        out_shape=[jax.ShapeDtypeStruct((t, d), BF16)] * 4,
        sem=("parallel", "parallel"), name="merge_bwd")


def ffn_perm(n_blocks):
    half = n_blocks // 2
    return tuple(2 * j if j < half else 2 * (j - half) + 1 for j in range(n_blocks))


def swiglu_fwd(f, nj, dep=None):
    t, two = f.shape
    tr = _tile(t, ROW_TILE, SUBLANES)
    npair = two // (2 * nj)

    def body(f_ref, o_ref):
        g = f_ref[:, :nj].astype(F32)
        u = f_ref[:, nj:].astype(F32)
        o_ref[...] = (g * _sigmoid(g) * u).astype(BF16)

    return _pallas(
        body, [f], dep=dep, grid=(t // tr, npair),
        in_specs=[pl.BlockSpec((tr, 2 * nj), lambda i, j: (i, j))],
        out_specs=pl.BlockSpec((tr, nj), lambda i, j: (i, j)),
        out_shape=jax.ShapeDtypeStruct((t, two // 2), BF16),
        sem=("parallel", "parallel"), name="swiglu_fwd")


def swiglu_bwd(f, dact, nj, dep=None):
    t, two = f.shape
    tr = _tile(t, ROW_TILE, SUBLANES)
    npair = two // (2 * nj)

    def body(f_ref, da_ref, o_ref):
        g = f_ref[:, :nj].astype(F32)
        u = f_ref[:, nj:].astype(F32)
        da = da_ref[...]
        s = _sigmoid(g)
        o_ref[:, :nj] = (da * u * (s * (1.0 + g * (1.0 - s)))).astype(BF16)
        o_ref[:, nj:] = (da * (g * s)).astype(BF16)

    return _pallas(
        body, [f, dact], dep=dep, grid=(t // tr, npair),
        in_specs=[pl.BlockSpec((tr, 2 * nj), lambda i, j: (i, j)), pl.BlockSpec((tr, nj), lambda i, j: (i, j))],
        out_specs=pl.BlockSpec((tr, 2 * nj), lambda i, j: (i, j)),
        out_shape=jax.ShapeDtypeStruct((t, two), BF16),
        sem=("parallel", "parallel"), name="swiglu_bwd")


def _t5_bucket_table():
    q_off = np.arange(BLOCK)
    k_off = np.arange(2 * BLOCK)
    dist = q_off[:, None] + BLOCK - k_off[None, :]
    n = np.maximum(dist, 0)
    nf = np.maximum(n, 1).astype(np.float32)
    large = MAX_EXACT + (np.log(nf / np.float32(MAX_EXACT)) / np.float32(math.log(MAX_DISTANCE / MAX_EXACT))
                         * np.float32(NUM_BUCKETS - MAX_EXACT)).astype(np.int32)
    large = np.minimum(large, NUM_BUCKETS - 1)
    bucket = np.where(n < MAX_EXACT, n, large).astype(np.int32)
    allowed = (dist >= 0) & (dist < WINDOW)
    return np.where(allowed, bucket, -1).astype(np.int32)


def bias_table(rel_bias, bucket_p, bucket_c, dep=None):
    nb, nq = rel_bias.shape

    def body(rb_ref, bkp_ref, bkc_ref, op_ref, oc_ref):
        for bk_ref, o_ref in ((bkp_ref, op_ref), (bkc_ref, oc_ref)):
            bk = bk_ref[...]
            for h in range(nq):
                acc = jnp.full(bk.shape, -jnp.inf, F32)
                for b in range(nb):
                    acc = jnp.where(bk == b, rb_ref[b, h], acc)
                o_ref[h] = acc

    return _pallas(
        body, [rel_bias, bucket_p, bucket_c], dep=dep,
        in_specs=[_SMEM, _VMEM, _VMEM], out_specs=[_VMEM, _VMEM],
        out_shape=[jax.ShapeDtypeStruct((nq,) + bucket_p.shape, F32)] * 2,
        name="bias_table")


def bias_table_bwd(dbp, dbc, bucket_p, bucket_c, dep=None):
    nq = dbp.shape[0]

    def body(dbp_ref, dbc_ref, bkp_ref, bkc_ref, o_ref):
        bkp, bkc = bkp_ref[...][None], bkc_ref[...][None]
        dp, dc = dbp_ref[...], dbc_ref[...]
        for b in range(NUM_BUCKETS):
            sel = jnp.where(bkp == b, dp, 0.0) + jnp.where(bkc == b, dc, 0.0)
            o_ref[b] = jnp.sum(jnp.sum(sel, axis=2, keepdims=True), axis=1, keepdims=True)

    return _pallas(
        body, [dbp, dbc, bucket_p, bucket_c], dep=dep,
        in_specs=[_VMEM] * 4, out_specs=_VMEM,
        out_shape=jax.ShapeDtypeStruct((NUM_BUCKETS, nq, 1, 1), F32),
        name="bias_table_bwd")


_BNT = (((2,), (2,)), ((0,), (0,)))
_BNN = (((2,), (1,)), ((0,), (0,)))
_BTN = (((1,), (1,)), ((0,), (0,)))


@jax.custom_vjp
def _bdot_nt(a, b):
    return lax.dot_general(a.astype(BF16), b.astype(BF16), _BNT, preferred_element_type=F32)


def _bdot_nt_fwd(a, b):
    return _bdot_nt(a, b), (a, b)


def _bdot_nt_bwd(res, g):
    a, b = res
    gb = g.astype(BF16)
    da = lax.dot_general(gb, b.astype(BF16), _BNN, preferred_element_type=F32)
    db = lax.dot_general(gb, a.astype(BF16), _BTN, preferred_element_type=F32)
    return da, db


_bdot_nt.defvjp(_bdot_nt_fwd, _bdot_nt_bwd)


@jax.custom_vjp
def _bdot_nn(a, b):
    return lax.dot_general(a.astype(BF16), b.astype(BF16), _BNN, preferred_element_type=F32)


def _bdot_nn_fwd(a, b):
    return _bdot_nn(a, b), (a, b)


def _bdot_nn_bwd(res, g):
    a, b = res
    gb = g.astype(BF16)
    da = lax.dot_general(gb, b.astype(BF16), _BNT, preferred_element_type=F32)
    db = lax.dot_general(a.astype(BF16), gb, _BTN, preferred_element_type=F32)
    return da, db


_bdot_nn.defvjp(_bdot_nn_fwd, _bdot_nn_bwd)


def _attn_math(q, kp, kc, vp, vc, bp, bc, sinks, qg, kg, *, prev_ok, scale):
    h, rows, _ = q.shape
    b = kp.shape[1]
    qn = q * _rms(q) * qg
    kpn = kp * _rms(kp) * kg
    kcn = kc * _rms(kc) * kg
    lp = _bdot_nt(qn, kpn) * scale + bp.reshape(h, rows, b)
    lc = _bdot_nt(qn, kcn) * scale + bc.reshape(h, rows, b)
    lp = jnp.where(prev_ok, lp, -jnp.inf)
    sink = jnp.broadcast_to(sinks, (sinks.shape[0], b, 1)).reshape(h, rows, 1)
    m = jnp.maximum(jnp.maximum(jnp.max(lp, axis=-1, keepdims=True), jnp.max(lc, axis=-1, keepdims=True)), sink)
    m = lax.stop_gradient(m)
    pp = jnp.exp(lp - m)
    pc = jnp.exp(lc - m)
    den = jnp.sum(pp, axis=-1, keepdims=True) + jnp.sum(pc, axis=-1, keepdims=True) + jnp.exp(sink - m)
    inv = 1.0 / den
    return _bdot_nn(pp * inv, vp) + _bdot_nn(pc * inv, vc)


def _attn_specs(p, aw, kvw, nq, hd, nblk, reverse):
    assert aw % (2 * kvw) == 0
    kv_col = aw // (2 * kvw)

    def blk(n):
        return nblk - 1 - n if reverse else n

    return [
        pl.BlockSpec((BLOCK, aw), lambda n: (blk(n), 0)),
        pl.BlockSpec((BLOCK, 2 * kvw), lambda n: (jnp.maximum(blk(n) - 1, 0), kv_col)),
        pl.BlockSpec((BLOCK, 2 * kvw), lambda n: (blk(n), kv_col)),
        _full_spec((nq, BLOCK, BLOCK)), _full_spec((nq, BLOCK, BLOCK)), _full_spec((nq, 1, 1)),
        _full_spec((1, hd)), _full_spec((1, hd)),
    ]


def _head_major(ref, n_heads, grp, hd, offset=0):
    return jnp.stack([
        jnp.concatenate([ref[:, pl.ds(offset + (grp * h + g) * hd, hd)].astype(F32) for g in range(grp)], axis=0)
        for h in range(n_heads)])


def _attn_inputs(nkv, grp, hd, kvw, q_ref, kvp_ref, kvc_ref):
    return (_head_major(q_ref, nkv, grp, hd), _head_major(kvp_ref, nkv, 1, hd), _head_major(kvc_ref, nkv, 1, hd),
            _head_major(kvp_ref, nkv, 1, hd, kvw), _head_major(kvc_ref, nkv, 1, hd, kvw))


def attn_fwd(p, bias_p, bias_c, sinks, qg, kg, *, aw, kvw, dep=None):
    t, hd = p.shape[0], qg.shape[-1]
    nq, nkv, nblk = aw // hd, kvw // hd, t // BLOCK
    grp = nq // nkv
    scale = hd ** -0.5

    def body(q_ref, kvp_ref, kvc_ref, bp_ref, bc_ref, s_ref, qg_ref, kg_ref, o_ref):
        prev_ok = pl.program_id(0) > 0
        out = _attn_math(*_attn_inputs(nkv, grp, hd, kvw, q_ref, kvp_ref, kvc_ref), bp_ref[...], bc_ref[...],
                         s_ref[...], qg_ref[...], kg_ref[...], prev_ok=prev_ok, scale=scale)
        for h in range(nkv):
            for g in range(grp):
                o_ref[:, pl.ds((grp * h + g) * hd, hd)] = out[h, g * BLOCK:(g + 1) * BLOCK].astype(BF16)

    return _pallas(
        body, [p, p, p, bias_p, bias_c, sinks, qg, kg], dep=dep, grid=(nblk,),
        in_specs=_attn_specs(p, aw, kvw, nq, hd, nblk, False),
        out_specs=pl.BlockSpec((BLOCK, aw), lambda n: (n, 0)),
        out_shape=jax.ShapeDtypeStruct((t, aw), BF16),
        sem=("parallel",), name="attn_fwd")


def attn_bwd(p, bias_p, bias_c, sinks, qg, kg, do, *, aw, kvw, dep=None):
    t, hd = p.shape[0], qg.shape[-1]
    nq, nkv, nblk = aw // hd, kvw // hd, t // BLOCK
    grp = nq // nkv
    scale = hd ** -0.5

    def body(q_ref, kvp_ref, kvc_ref, bp_ref, bc_ref, s_ref, qg_ref, kg_ref, do_ref,
             dqkv_ref, dbp_ref, dbc_ref, ds_ref, dqg_ref, dkg_ref, carry):
        i = pl.program_id(0)
        prev_ok = (nblk - 1 - i) > 0

        @pl.when(i == 0)
        def _():
            carry[...] = jnp.zeros_like(carry)
            dbp_ref[...] = jnp.zeros_like(dbp_ref)
            dbc_ref[...] = jnp.zeros_like(dbc_ref)
            ds_ref[...] = jnp.zeros_like(ds_ref)
            dqg_ref[...] = jnp.zeros_like(dqg_ref)
            dkg_ref[...] = jnp.zeros_like(dkg_ref)

        fn = functools.partial(_attn_math, prev_ok=prev_ok, scale=scale)
        _, vjp = jax.vjp(fn, *_attn_inputs(nkv, grp, hd, kvw, q_ref, kvp_ref, kvc_ref), bp_ref[...], bc_ref[...],
                         s_ref[...], qg_ref[...], kg_ref[...])
        dq, dkp, dkc, dvp, dvc, dbp, dbc, dsk, dqg, dkg = vjp(_head_major(do_ref, nkv, grp, hd))
        for h in range(nkv):
            for g in range(grp):
                dqkv_ref[:, pl.ds((grp * h + g) * hd, hd)] = dq[h, g * BLOCK:(g + 1) * BLOCK].astype(BF16)
            k_cols, v_cols = pl.ds(h * hd, hd), pl.ds(kvw + h * hd, hd)
            dqkv_ref[:, pl.ds(aw + h * hd, hd)] = (dkc[h] + carry[:, k_cols]).astype(BF16)
            dqkv_ref[:, pl.ds(aw + kvw + h * hd, hd)] = (dvc[h] + carry[:, v_cols]).astype(BF16)
            carry[:, k_cols] = dkp[h]
            carry[:, v_cols] = dvp[h]
        dbp_ref[...] += dbp
        dbc_ref[...] += dbc
        ds_ref[...] += dsk
        dqg_ref[...] += dqg
        dkg_ref[...] += dkg

    return _pallas(
        body, [p, p, p, bias_p, bias_c, sinks, qg, kg, do], dep=dep, grid=(nblk,),
        in_specs=_attn_specs(p, aw, kvw, nq, hd, nblk, True)
        + [pl.BlockSpec((BLOCK, aw), lambda n: (nblk - 1 - n, 0))],
        out_specs=[
            pl.BlockSpec((BLOCK, aw + 2 * kvw), lambda n: (nblk - 1 - n, 0)),
            _full_spec((nq, BLOCK, BLOCK)), _full_spec((nq, BLOCK, BLOCK)), _full_spec((nq, 1, 1)),
            _full_spec((1, hd)), _full_spec((1, hd)),
        ],
        out_shape=[
            jax.ShapeDtypeStruct((t, aw + 2 * kvw), BF16),
            jax.ShapeDtypeStruct((nq, BLOCK, BLOCK), F32),
            jax.ShapeDtypeStruct((nq, BLOCK, BLOCK), F32),
            jax.ShapeDtypeStruct((nq, 1, 1), F32),
            jax.ShapeDtypeStruct((1, hd), F32),
            jax.ShapeDtypeStruct((1, hd), F32),
        ],
        scratch=[pltpu.VMEM((BLOCK, 2 * kvw), F32)],
        sem=("arbitrary",), name="attn_bwd")


CONV_TILE = 256


def _conv_halo_specs(tb, ch, nblk):
    per = tb // CONV_HALO
    last = nblk * per - 1
    cur = pl.BlockSpec((tb, ch), lambda n: (n, 0))
    prev = pl.BlockSpec((CONV_HALO, ch), lambda n: (jnp.maximum(n * per - 1, 0), 0))
    nxt = pl.BlockSpec((CONV_HALO, ch), lambda n: (jnp.minimum((n + 1) * per, last), 0))
    return cur, prev, nxt


def _ln_silu(co, ln_g, ln_b):
    mu = jnp.mean(co, axis=-1, keepdims=True)
    cen = co - mu
    rstd = lax.rsqrt(jnp.mean(cen * cen, axis=-1, keepdims=True) + EPS)
    xhat = cen * rstd
    z = xhat * ln_g + ln_b
    return xhat, rstd, z


def _shifted_copies(src, shifted):
    rows = src.shape[0] - SUBLANES
    for r in range(1, SUBLANES):
        shifted[r, pl.ds(0, rows), :] = src[pl.ds(r, rows), :]


def _rows_from(src, shifted, start, n):
    r = start % SUBLANES
    if r == 0:
        return src[pl.ds(start, n), :]
    return shifted[r, pl.ds(start - r, n), :]


def conv_fwd(ca, cb, conv_w, conv_b, ln_g, ln_b, dep=None):
    t, ch = ca.shape
    tb = _tile(t, CONV_TILE, CONV_HALO)
    nblk = t // tb
    cur, prev, _ = _conv_halo_specs(tb, ch, nblk)
    lead = CONV_HALO - (CONV_WIDTH - 1)

    def body(ca_ref, cb_ref, cap_ref, cbp_ref, w_ref, b_ref, g_ref, bb_ref, s_ref, co_ref, ubuf, ushift):
        n = pl.program_id(0)
        halo = cap_ref[...] * _sigmoid(cbp_ref[...])
        ubuf[pl.ds(0, CONV_HALO), :] = jnp.where(n > 0, halo, 0.0)
        ubuf[pl.ds(CONV_HALO, tb), :] = ca_ref[...] * _sigmoid(cb_ref[...])
        _shifted_copies(ubuf, ushift)
        acc = jnp.broadcast_to(b_ref[...], (tb, ch))
        for k in range(CONV_WIDTH):
            acc = acc + w_ref[pl.ds(k, 1), :] * _rows_from(ubuf, ushift, lead + k, tb)
        co_ref[...] = acc
        _, _, z = _ln_silu(acc, g_ref[...], bb_ref[...])
        s_ref[...] = (z * _sigmoid(z)).astype(BF16)

    vec = _full_spec((1, ch))
    return _pallas(
        body, [ca, cb, ca, cb, conv_w, conv_b, ln_g, ln_b], dep=dep, grid=(nblk,),
        in_specs=[cur, cur, prev, prev, _full_spec(conv_w.shape), vec, vec, vec],
        out_specs=[cur, cur],
        out_shape=[jax.ShapeDtypeStruct((t, ch), BF16), jax.ShapeDtypeStruct((t, ch), F32)],
        scratch=[pltpu.VMEM((CONV_HALO + tb, ch), F32), pltpu.VMEM((SUBLANES, CONV_HALO + tb, ch), F32)],
        sem=("parallel",), name="conv_fwd")


def conv_bwd(ca, cb, co, ds, conv_w, ln_g, ln_b, dep=None):
    t, ch = ca.shape
    tb = _tile(t, CONV_TILE, CONV_HALO)
    nblk = t // tb
    cur, prev, nxt = _conv_halo_specs(tb, ch, nblk)
    lead = CONV_HALO - (CONV_WIDTH - 1)
    ext = tb + CONV_HALO

    def body(ca_ref, cb_ref, cap_ref, cbp_ref, co_ref, con_ref, ds_ref, dsn_ref, w_ref, g_ref, bb_ref,
             dca_ref, dcb_ref, dw_ref, dvec_ref, ubuf, dbuf, ushift, dshift):
        n = pl.program_id(0)
        is_last = n == nblk - 1
        sig_b = _sigmoid(cb_ref[...])
        cav = ca_ref[...].astype(F32)
        ubuf[pl.ds(0, CONV_HALO), :] = jnp.where(n > 0, cap_ref[...] * _sigmoid(cbp_ref[...]), 0.0)
        ubuf[pl.ds(CONV_HALO, tb), :] = cav * sig_b
        _shifted_copies(ubuf, ushift)
        co = jnp.concatenate([co_ref[...], con_ref[...]], axis=0)
        xhat, rstd, z = _ln_silu(co, g_ref[...], bb_ref[...])
        dsv = jnp.concatenate([ds_ref[...].astype(F32), jnp.where(is_last, 0.0, dsn_ref[...].astype(F32))], axis=0)
        sg = _sigmoid(z)
        dz = dsv * (sg * (1.0 + z * (1.0 - sg)))
        dxh = dz * g_ref[...]
        dco = rstd * (dxh - jnp.mean(dxh, axis=-1, keepdims=True)
                      - xhat * jnp.mean(dxh * xhat, axis=-1, keepdims=True))
        dbuf[...] = dco
        _shifted_copies(dbuf, dshift)

        @pl.when(n == 0)
        def _():
            dw_ref[...] = jnp.zeros_like(dw_ref)
            dvec_ref[...] = jnp.zeros_like(dvec_ref)

        dco_cur = dco[:tb]
        dvec_ref[pl.ds(0, 1), :] += jnp.sum(dco_cur, axis=0, keepdims=True)
        dvec_ref[pl.ds(1, 1), :] += jnp.sum(dz[:tb] * xhat[:tb], axis=0, keepdims=True)
        dvec_ref[pl.ds(2, 1), :] += jnp.sum(dz[:tb], axis=0, keepdims=True)
        du = jnp.zeros((tb, ch), F32)
        for k in range(CONV_WIDTH):
            du = du + w_ref[pl.ds(k, 1), :] * _rows_from(dbuf, dshift, CONV_WIDTH - 1 - k, tb)
            dw_ref[pl.ds(k, 1), :] += jnp.sum(dco_cur * _rows_from(ubuf, ushift, lead + k, tb), axis=0,
                                              keepdims=True)
        dca_ref[...] = (du * sig_b).astype(BF16)
        dcb_ref[...] = (du * cav * sig_b * (1.0 - sig_b)).astype(BF16)

    vec = _full_spec((1, ch))
    return _pallas(
        body, [ca, cb, ca, cb, co, co, ds, ds, conv_w, ln_g, ln_b], dep=dep, grid=(nblk,),
        in_specs=[cur, cur, prev, prev, cur, nxt, cur, nxt, _full_spec(conv_w.shape), vec, vec],
        out_specs=[cur, cur, _full_spec(conv_w.shape), _full_spec((SUBLANES, ch))],
        out_shape=[jax.ShapeDtypeStruct((t, ch), BF16), jax.ShapeDtypeStruct((t, ch), BF16),
                   jax.ShapeDtypeStruct(conv_w.shape, F32), jax.ShapeDtypeStruct((SUBLANES, ch), F32)],
        scratch=[pltpu.VMEM((CONV_HALO + tb, ch), F32), pltpu.VMEM((ext, ch), F32),
                 pltpu.VMEM((SUBLANES, CONV_HALO + tb, ch), F32), pltpu.VMEM((SUBLANES, ext, ch), F32)],
        sem=("arbitrary",), name="conv_bwd")


def ada_fwd(c_t, w_ada, dep=None):
    d, nc = w_ada.shape
    nex = c_t.shape[1]
    tn = _tile(nc, 512)

    def body(ct_ref, w_ref, o_ref):
        w = w_ref[...]
        ct = ct_ref[...]
        cact = ct * _sigmoid(ct)
        rows = [jnp.sum(w * cact[:, b:b + 1], axis=0, keepdims=True) for b in range(nex)]
        o_ref[...] = jnp.concatenate(rows, axis=0)

    return _pallas(
        body, [c_t, w_ada], dep=dep, grid=(nc // tn,),
        in_specs=[_full_spec(c_t.shape), pl.BlockSpec((d, tn), lambda j: (0, j))],
        out_specs=pl.BlockSpec((nex, tn), lambda j: (0, j)),
        out_shape=jax.ShapeDtypeStruct((nex, nc), F32),
        sem=("parallel",), name="ada_fwd")


def _adamw_math(w, g, m, v):
    m = ADAM_B1 * m + (1.0 - ADAM_B1) * g
    v = ADAM_B2 * v + (1.0 - ADAM_B2) * (g * g)
    m_hat = m / (1.0 - ADAM_B1 ** ADAM_STEP)
    v_hat = v / (1.0 - ADAM_B2 ** ADAM_STEP)
    delta = -ADAM_LR * (m_hat / (jnp.sqrt(v_hat) + ADAM_EPS) + ADAM_WD * w)
    return delta, m, v


def adamw(w, g, m, v, name, copy_grad=False, dep=None):
    r, n = w.shape
    tr, tn = _ew_tiles(r, n, elems=256 * 1024)
    n_out = 4 if copy_grad else 3

    def body(w_ref, g_ref, m_ref, v_ref, *outs):
        g = g_ref[...]
        if copy_grad:
            outs[0][...] = g
        outs[-3][...], outs[-2][...], outs[-1][...] = _adamw_math(w_ref[...], g, m_ref[...], v_ref[...])

    blk = pl.BlockSpec((tr, tn), lambda i, j: (i, j))
    return _pallas(
        body, [w, g, m, v], dep=dep, grid=(r // tr, n // tn),
        in_specs=[blk] * 4, out_specs=[blk] * n_out,
        out_shape=[jax.ShapeDtypeStruct((r, n), F32)] * n_out,
        sem=("parallel", "parallel"), name=name)


def ada_grad_adamw(c_t, dmod_cols, w, m, v, dep=None):
    d, nc = w.shape
    nex = c_t.shape[1]
    tr, tn = _ew_tiles(d, nc, elems=256 * 1024)

    def body(ct_ref, dm_ref, w_ref, m_ref, v_ref, g_ref, d_ref, nm_ref, nv_ref):
        ct = ct_ref[...]
        cact = ct * _sigmoid(ct)
        dm = dm_ref[...]
        g = cact[:, 0:1] * dm[0:1, :]
        for b in range(1, nex):
            g = g + cact[:, b:b + 1] * dm[b:b + 1, :]
        g_ref[...] = g
        d_ref[...], nm_ref[...], nv_ref[...] = _adamw_math(w_ref[...], g, m_ref[...], v_ref[...])

    blk = pl.BlockSpec((tr, tn), lambda i, j: (i, j))
    return _pallas(
        body, [c_t, dmod_cols, w, m, v], dep=dep, grid=(d // tr, nc // tn),
        in_specs=[pl.BlockSpec((tr, nex), lambda i, j: (i, 0)), pl.BlockSpec((nex, tn), lambda i, j: (0, j)),
                  blk, blk, blk],
        out_specs=[blk] * 4,
        out_shape=[jax.ShapeDtypeStruct((d, nc), F32)] * 4,
        sem=("parallel", "parallel"), name="ada_grad_adamw")


def _row_pack(parts):
    cols, offs, off = [], [], 0
    for p in parts:
        n = p.shape[1]
        width = -(-n // LANES) * LANES
        cols.append(jnp.pad(p, ((0, 0), (0, width - n))) if width != n else p)
        offs.append(off)
        off += width
    return jnp.concatenate(cols, axis=1), offs


def small_sum_adamw(gathered, offs, ws, ms, vs, extra_widths, dep=None):
    ndev = gathered.shape[0]
    npar = len(ws)

    def body(ga_ref, *refs):
        w_refs, m_refs, v_refs = refs[:npar], refs[npar:2 * npar], refs[2 * npar:3 * npar]
        outs = refs[3 * npar:]
        tot = ga_ref[0]
        for s in range(1, ndev):
            tot = tot + ga_ref[s]
        for i in range(npar):
            n = ws[i].shape[1]
            g = tot[:, offs[i]:offs[i] + n]
            outs[4 * i][...] = g
            outs[4 * i + 1][...], outs[4 * i + 2][...], outs[4 * i + 3][...] = _adamw_math(
                w_refs[i][...], g, m_refs[i][...], v_refs[i][...])
        for e, n in enumerate(extra_widths):
            off = offs[npar + e]
            outs[4 * npar + e][...] = tot[:, off:off + n]

    shapes = [jax.ShapeDtypeStruct(w.shape, F32) for w in ws for _ in range(4)]
    shapes += [jax.ShapeDtypeStruct((1, n), F32) for n in extra_widths]
    return _pallas(
        body, [gathered, *ws, *ms, *vs], dep=dep, in_specs=[_VMEM] * (1 + 3 * npar), out_specs=[_VMEM] * len(shapes),
        out_shape=shapes, name="small_sum_adamw")


def _position():
    return lax.axis_index("x"), lax.axis_index("y"), lax.axis_index("c")


def _other_chips(x, y):
    return [(1 - x, y), (x, 1 - y), (1 - x, 1 - y)]


def allgather_small(block, name, dep=None):
    def body(x_ref, out_ref, send_sems, recv_sems, local_sem):
        x, y, c = _position()
        me, sibling = (x, y, c), (x, y, 1 - c)
        chips = _other_chips(x, y)

        def slot(px, py, pc):
            return out_ref.at[4 * px + 2 * py + pc]

        def copy(k, block_of, to, src=None):
            return pltpu.make_async_remote_copy(
                src_ref=slot(*block_of) if src is None else src, dst_ref=slot(*block_of),
                send_sem=send_sems.at[k], recv_sem=recv_sems.at[k], device_id=to, device_id_type=MESH)

        mine = pltpu.make_async_copy(x_ref, slot(*me), local_sem)
        mine.start()
        first = [copy(0, me, sibling, src=x_ref)]
        first += [copy(1 + j, me, (*chip, c), src=x_ref) for j, chip in enumerate(chips)]
        for cp in first:
            cp.start()
        passed = [copy(4 + j, (*chip, c), sibling) for j, chip in enumerate(chips)]
        for j, chip in enumerate(chips):
            copy(1 + j, (*chip, c), me).wait_recv()
            passed[j].start()
        copy(0, sibling, me).wait_recv()
        for j, chip in enumerate(chips):
            copy(4 + j, (*chip, 1 - c), me).wait_recv()
        for cp in first + passed:
            cp.wait_send()
        mine.wait()

    return _pallas(
        body, [block], dep=dep,
        out_shape=jax.ShapeDtypeStruct((N_DEV, *block.shape), block.dtype),
        in_specs=[_VMEM], out_specs=_VMEM,
        scratch=[pltpu.SemaphoreType.DMA((7,)), pltpu.SemaphoreType.DMA((7,)), pltpu.SemaphoreType.DMA],
        name=name)


class Started(NamedTuple):
    send_sems: Any
    recv_sems: Any
    bufs: list


def exchange_start(name, bufs, n_copies, plan, dep=None):
    nb = len(bufs)

    def body(*refs):
        for cp in plan(refs[:nb], refs[nb], refs[nb + 1]):
            cp.start()

    outs = _pallas(
        body, [pltpu.with_memory_space_constraint(b, pltpu.HBM) for b in bufs], dep=dep, name=name,
        out_shape=(pltpu.SemaphoreType.DMA((n_copies,)), pltpu.SemaphoreType.DMA((n_copies,)),
                   *[pltpu.HBM(b.shape, b.dtype) for b in bufs]),
        in_specs=[_HBM] * nb,
        out_specs=(_SEM, _SEM, *[_HBM] * nb),
        input_output_aliases={i: 2 + i for i in range(nb)},
        compiler_params=pltpu.CompilerParams(has_side_effects=_EFFECT))
    return Started(outs[0], outs[1], list(outs[2:2 + nb]))


def exchange_wait(name, started, plan, bufs=None, dep=None):
    if bufs is not None:
        started = started._replace(bufs=list(bufs))
    nb = len(started.bufs)

    def body(*refs):
        for cp in plan(refs[:nb], refs[nb], refs[nb + 1]):
            cp.wait_send()
            cp.wait_recv()

    outs = _pallas(
        body, [*started.bufs, started.send_sems, started.recv_sems], dep=dep, name=name,
        out_shape=tuple(pltpu.HBM(b.shape, b.dtype) for b in started.bufs),
        in_specs=[_HBM] * nb + [_SEM, _SEM],
        out_specs=tuple([_HBM] * nb),
        input_output_aliases={i: i for i in range(nb)},
        compiler_params=pltpu.CompilerParams(has_side_effects=_EFFECT))
    return list(outs)


def _remote(src, dst, send_sems, recv_sems, i, to):
    return pltpu.make_async_remote_copy(src_ref=src, dst_ref=dst, send_sem=send_sems.at[i], recv_sem=recv_sems.at[i],
                                        device_id=to, device_id_type=MESH)


def _half_rows(buf_rows, chip_idx, pc):
    half = buf_rows // (2 * N_CHIPS)
    return pl.ds((2 * chip_idx + pc) * half, half)


ALL_PEERS = (0, 1, 2)


def plan_gather_ici(refs, send_sems, recv_sems, peers=ALL_PEERS):
    x, y, c = _position()
    chips = _other_chips(x, y)
    copies = []
    for k, ref in enumerate(refs):
        rows = ref.at[_half_rows(ref.shape[0], 2 * x + y, c), :]
        for i, j in enumerate(peers):
            copies.append(_remote(rows, rows, send_sems, recv_sems, len(peers) * k + i, (*chips[j], c)))
    return copies


def plan_gather_d2d(refs, send_sems, recv_sems, peers=ALL_PEERS):
    x, y, c = _position()
    chips = _other_chips(x, y)
    copies = []
    for k, ref in enumerate(refs):
        for i, j in enumerate(peers):
            px, py = chips[j]
            rows = ref.at[_half_rows(ref.shape[0], 2 * px + py, c), :]
            copies.append(_remote(rows, rows, send_sems, recv_sems, len(peers) * k + i, (x, y, 1 - c)))
    return copies


def plan_pair_exchange(refs, send_sems, recv_sems):
    x, y, c = _position()
    nw = len(refs) // 2
    copies = []
    for k in range(nw):
        for chip in range(N_CHIPS):
            copies.append(_remote(refs[k].at[chip, 1 - c], refs[nw + k].at[chip], send_sems, recv_sems,
                                  N_CHIPS * k + chip, (x, y, 1 - c)))
    return copies


def plan_chip_exchange(refs, send_sems, recv_sems):
    x, y, c = _position()
    nw = len(refs) // 2
    copies = []
    for k in range(nw):
        for j, (px, py) in enumerate(_other_chips(x, y)):
            copies.append(_remote(refs[k].at[2 * px + py], refs[nw + k].at[2 * x + y], send_sems, recv_sems,
                                  3 * k + j, (px, py, c)))
    return copies


def plan_pair_share(refs, send_sems, recv_sems):
    x, y, c = _position()
    return [_remote(ref.at[c], ref.at[c], send_sems, recv_sems, k, (x, y, 1 - c)) for k, ref in enumerate(refs)]


def cast_into_slot(src, slot, n_slots, name, dep=None):
    r, n = src.shape
    tr, tn = _ew_tiles(r, n, BF16_SUBLANES)

    def body(slot_ref, s_ref, o_ref):
        o_ref[...] = s_ref[...].astype(BF16)

    return _pallas(
        body, [slot, src], dep=dep, n_prefetch=1, grid=(r // tr, n // tn),
        in_specs=[pl.BlockSpec((tr, tn), lambda i, j, sl: (i, j))],
        out_specs=pl.BlockSpec((None, tr, tn), lambda i, j, sl: (sl[0], i, j)),
        out_shape=jax.ShapeDtypeStruct((n_slots, r, n), BF16),
        sem=("parallel", "parallel"), name=name)


def pair_sum(g, r, core, name, dep=None):
    nchip, _, h, n = g.shape
    th, tn = _ew_tiles(h, n, BF16_SUBLANES)

    def body(core_ref, g_ref, r_ref, o_ref):
        o_ref[...] = (g_ref[...].astype(F32) + r_ref[...].astype(F32)).astype(BF16)

    return _pallas(
        body, [core, g, r], dep=dep, n_prefetch=1, grid=(nchip, h // th, n // tn),
        in_specs=[pl.BlockSpec((None, None, th, tn), lambda a, i, j, cr: (a, cr[0], i, j)),
                  pl.BlockSpec((None, th, tn), lambda a, i, j, cr: (a, i, j))],
        out_specs=pl.BlockSpec((None, th, tn), lambda a, i, j, cr: (a, i, j)),
        out_shape=jax.ShapeDtypeStruct((nchip, h, n), BF16),
        sem=("parallel", "parallel", "parallel"), name=name)


def chip_sum(own, got, where, name, dep=None):
    nchip, h, n = got.shape
    th, tn = _ew_tiles(h, n, BF16_SUBLANES, elems=256 * 1024)

    def body(where_ref, own_ref, *rest):
        got_refs, o_ref = rest[:nchip], rest[nchip]
        chip = where_ref[0]
        acc = None
        for s in range(nchip):
            term = jnp.where(chip == s, own_ref[...], got_refs[s][...]).astype(F32)
            acc = term if acc is None else acc + term
        o_ref[...] = acc

    def got_spec(s):
        return pl.BlockSpec((None, th, tn), lambda i, j, wr: (jnp.where(wr[0] == s, (s + 1) % nchip, s), i, j))

    return _pallas(
        body, [where, own, *[got] * nchip], dep=dep, n_prefetch=1, grid=(h // th, n // tn),
        in_specs=[pl.BlockSpec((None, th, tn), lambda i, j, wr: (wr[0], i, j))]
        + [got_spec(s) for s in range(nchip)],
        out_specs=pl.BlockSpec((None, th, tn), lambda i, j, wr: (wr[1], i, j)),
        out_shape=jax.ShapeDtypeStruct((2, h, n), F32),
        sem=("parallel", "parallel"), name=name)


def kernel(x, c, w_ada, b_ada, norm_mix_g, w_in, q_norm_g, k_norm_g, attn_sinks, rel_bias, w_attn_out, conv_w, conv_b, conv_ln_g, conv_ln_b, w_conv_out, w_mix_out, norm_ffn_g, w_ffn_in, w_ffn_out, loss_target, m_w_ada, m_b_ada, m_norm_mix_g, m_w_in, m_q_norm_g, m_k_norm_g, m_attn_sinks, m_rel_bias, m_w_attn_out, m_conv_w, m_conv_b, m_conv_ln_g, m_conv_ln_b, m_w_conv_out, m_w_mix_out, m_norm_ffn_g, m_w_ffn_in, m_w_ffn_out, v_w_ada, v_b_ada, v_norm_mix_g, v_w_in, v_q_norm_g, v_k_norm_g, v_attn_sinks, v_rel_bias, v_w_attn_out, v_conv_w, v_conv_b, v_conv_ln_g, v_conv_ln_b, v_w_conv_out, v_w_mix_out, v_norm_ffn_g, v_w_ffn_in, v_w_ffn_out):
    run = InOrder()
    xi, yi, ci = _position()
    chip = 2 * xi + yi
    me = 2 * chip + ci
    chip_arr = chip.astype(jnp.int32).reshape(1)
    core_arr = ci.astype(jnp.int32).reshape(1)
    where_arr = jnp.stack([chip, ci]).astype(jnp.int32)

    xe, tgt = x[0], loss_target[0]
    t, d = xe.shape
    hd = q_norm_g.shape[-1]
    nq = attn_sinks.shape[-1]
    aw = nq * hd
    ch = conv_b.shape[-1]
    in_width = N_CHIPS * w_in.shape[-1]
    kvw = (in_width - aw - 2 * ch - 2 * d) // 2
    nkv = kvw // hd
    dff = N_CHIPS * w_ffn_out.shape[1]
    off_k, off_v, off_ca = aw, aw + kvw, aw + 2 * kvw
    off_cb, off_ga, off_gc = off_ca + ch, off_ca + 2 * ch, off_ca + 2 * ch + d
    nc_ada = w_ada.shape[-1]
    ch_loc = conv_w.shape[-1]
    nj_ffn = w_ffn_in.shape[-1]
    perm_ffn = ffn_perm(N_CHIPS)

    big = {"w_in": w_in[0], "w_attn_out": w_attn_out[0], "w_conv_out": w_conv_out[0], "w_mix_out": w_mix_out[0],
           "w_ffn_in": w_ffn_in[0], "w_ffn_out": w_ffn_out[0]}
    moments = {"w_in": (m_w_in, v_w_in), "w_attn_out": (m_w_attn_out, v_w_attn_out),
               "w_conv_out": (m_w_conv_out, v_w_conv_out), "w_mix_out": (m_w_mix_out, v_w_mix_out),
               "w_ffn_in": (m_w_ffn_in, v_w_ffn_in), "w_ffn_out": (m_w_ffn_out, v_w_ffn_out)}
    gather_groups = {"in": ["w_in"], "branch_out": ["w_attn_out", "w_conv_out"], "mix_out": ["w_mix_out"],
                     "ffn_in": ["w_ffn_in"], "ffn_out": ["w_ffn_out"]}
    grads, deltas, new_m, new_v = {}, {}, {}, {}

    def gather_cast(gname):
        bufs = []
        for n in gather_groups[gname]:
            r, ncol = big[n].shape
            bufs.append(run(cast_into_slot, big[n], chip_arr, N_CHIPS, "cast_" + n).reshape(N_CHIPS * r, ncol))
        return bufs

    def gather_ici_start(gname, bufs):
        return run(exchange_start, "gather_ici_start_" + gname, bufs, 3 * len(bufs), plan_gather_ici)

    def gather_pass_on(gname, ici):
        landed = run(exchange_wait, "gather_ici_wait_" + gname, ici, plan_gather_ici)
        return run(exchange_start, "gather_d2d_start_" + gname, landed, 3 * len(landed), plan_gather_d2d)

    def gathered(gname, d2d):
        outs = run(exchange_wait, "gather_d2d_wait_" + gname, d2d, plan_gather_d2d)
        return [o.reshape(N_CHIPS, *big[n].shape) for o, n in zip(outs, gather_groups[gname])]

    def rs_pair_start(gname, names, partials):
        blocks = [g.reshape(N_CHIPS, 2, big[n].shape[0] // 2, big[n].shape[1]) for n, g in zip(names, partials)]
        land = [lax.empty((N_CHIPS,) + b.shape[2:], BF16) for b in blocks]
        return run(exchange_start, "pair_exchange_start_" + gname, blocks + land, N_CHIPS * len(blocks),
                   plan_pair_exchange)

    def rs_chip_start(gname, names, pair):
        nw = len(names)
        outs = run(exchange_wait, "pair_exchange_wait_" + gname, pair, plan_pair_exchange)
        sums = [run(pair_sum, g, r, core_arr, "pair_sum_" + n) for n, g, r in zip(names, outs[:nw], outs[nw:])]
        land = [lax.empty(s.shape, BF16) for s in sums]
        return run(exchange_start, "chip_exchange_start_" + gname, sums + land, 3 * nw, plan_chip_exchange)

    def rs_share_start(gname, names, chipx):
        nw = len(names)
        outs = run(exchange_wait, "chip_exchange_wait_" + gname, chipx, plan_chip_exchange)
        halves = [run(chip_sum, s, r, where_arr, "chip_sum_" + n) for n, s, r in zip(names, outs[:nw], outs[nw:])]
        return run(exchange_start, "pair_share_start_" + gname, halves, nw, plan_pair_share)

    def rs_finish(gname, names, share):
        fulls = run(exchange_wait, "pair_share_wait_" + gname, share, plan_pair_share)
        for n, g2 in zip(names, fulls):
            g, dl, nm, nv = run(adamw, big[n], g2.reshape(big[n].shape), moments[n][0][0], moments[n][1][0],
                                "adamw_" + n, copy_grad=True)
            grads[n], deltas[n], new_m[n], new_v[n] = g[None], dl[None], nm[None], nv[None]

    near, far = (0, 1), (2,)
    plan_ici_near = functools.partial(plan_gather_ici, peers=near)
    plan_ici_far = functools.partial(plan_gather_ici, peers=far)
    plan_d2d_near = functools.partial(plan_gather_d2d, peers=near)
    plan_d2d_far = functools.partial(plan_gather_d2d, peers=far)
    bufs_in = gather_cast("in")
    row1, offs1 = _row_pack([c, conv_w[0].reshape(1, CONV_WIDTH * ch_loc)])
    got1 = run(allgather_small, row1, "allgather_cond")
    ici_near = run(exchange_start, "gather_ici_start_in_near", bufs_in, len(near), plan_ici_near)
    c_all = got1[:, 0, :d]
    conv_w_full = got1[0::2, 0, offs1[1]:offs1[1] + CONV_WIDTH * ch_loc].reshape(N_CHIPS, CONV_WIDTH, ch_loc)
    conv_w_full = jnp.transpose(conv_w_full, (1, 0, 2)).reshape(CONV_WIDTH, ch)
    conv_w_pad = jnp.pad(conv_w_full, ((0, 1), (0, 0)))
    c_t = jnp.transpose(c_all)
    mod_cols = run(ada_fwd, c_t, w_ada[0])
    rest_bufs = {gname: gather_cast(gname) for gname in gather_groups if gname != "in"}
    got2 = run(allgather_small, mod_cols, "allgather_mod")
    mod_all = got2.reshape(N_CHIPS, 2, N_DEV, nc_ada)[:, 0]
    mod = lax.dynamic_slice_in_dim(mod_all, me, 1, axis=1).reshape(1, N_CHIPS * nc_ada) + b_ada
    mod = jnp.pad(mod.reshape(N_MOD, d), ((0, SUBLANES - N_MOD), (0, 0)))

    h = run(pre_mix_fwd, xe, mod, norm_mix_g)
    bucket = _t5_bucket_table()
    bucket_p, bucket_c = jnp.asarray(bucket[:, :BLOCK]), jnp.asarray(bucket[:, BLOCK:])
    bias_p, bias_c = run(bias_table, rel_bias, bucket_p, bucket_c)

    def in_blocks(buf):
        return buf.reshape(N_CHIPS, *big["w_in"].shape)

    def chip_ids(peers):
        others = [2 * (1 - xi) + yi, 2 * xi + (1 - yi), 2 * (1 - xi) + (1 - yi)]
        return jnp.stack([others[j] for j in peers]).astype(jnp.int32)

    tn_in = big["w_in"].shape[1]
    landed = run(exchange_wait, "gather_ici_wait_in_near", ici_near, plan_ici_near)
    ici_far = run(exchange_start, "gather_ici_start_in_far", landed, len(far), plan_ici_far)
    ici = {gname: gather_ici_start(gname, rest_bufs[gname]) for gname in ("branch_out", "mix_out")}
    ici_near_ffn = run(exchange_start, "gather_ici_start_ffn_in_near", rest_bufs["ffn_in"], len(near), plan_ici_near)
    d2d_near = run(exchange_start, "gather_d2d_start_in_near", ici_far.bufs, len(near), plan_d2d_near)
    p = run(mm_nn_blocks, h, in_blocks(d2d_near.bufs[0]), chip_arr, None, tn=tn_in, out_dtype=BF16, name="mm_in_own")
    landed = run(exchange_wait, "gather_d2d_wait_in_near", d2d_near, plan_d2d_near)
    p = run(mm_nn_blocks, h, in_blocks(landed[0]), chip_ids(near), p, tn=tn_in, out_dtype=BF16, name="mm_in_near")
    landed = run(exchange_wait, "gather_ici_wait_in_far", ici_far, plan_ici_far, bufs=landed)
    d2d_far = run(exchange_start, "gather_d2d_start_in_far", landed, len(far), plan_d2d_far)
    landed = run(exchange_wait, "gather_d2d_wait_in_far", d2d_far, plan_d2d_far)
    wg_in = in_blocks(landed[0])
    p = run(mm_nn_blocks, h, wg_in, chip_ids(far), p, tn=tn_in, out_dtype=BF16, name="mm_in_far")
    d2d_branch = gather_pass_on("branch_out", ici["branch_out"])

    sinks3 = attn_sinks.reshape(nq, 1, 1)
    attn_o = run(attn_fwd, p, bias_p, bias_c, sinks3, q_norm_g, k_norm_g, aw=aw, kvw=kvw)
    ca, cb = p[:, off_ca:off_cb], p[:, off_cb:off_ga]
    s_conv, co_conv = run(conv_fwd, ca, cb, conv_w_pad, conv_b, conv_ln_g, conv_ln_b)
    wg_attn_out, wg_conv_out = gathered("branch_out", d2d_branch)
    y_attn = run(mm_nn, attn_o, wg_attn_out, tn=_tile(wg_attn_out.shape[2], 512), tk=aw, out_dtype=BF16,
                 name="mm_attn_out")
    y_conv = run(mm_nn, s_conv, wg_conv_out, tn=_tile(wg_conv_out.shape[2], 512), tk=ch, out_dtype=BF16,
                 name="mm_conv_out")
    landed = run(exchange_wait, "gather_ici_wait_ffn_in_near", ici_near_ffn, plan_ici_near)
    ici_far_ffn = run(exchange_start, "gather_ici_start_ffn_in_far", landed, len(far), plan_ici_far)
    ici["ffn_out"] = gather_ici_start("ffn_out", rest_bufs["ffn_out"])
    d2d_near_ffn = run(exchange_start, "gather_d2d_start_ffn_in_near", ici_far_ffn.bufs, len(near), plan_d2d_near)
    merged = run(merge_fwd, p, y_attn, y_conv, off_ga, off_gc)
    d2d_mix = gather_pass_on("mix_out", ici["mix_out"])
    (wg_mix_out,) = gathered("mix_out", d2d_mix)
    wg_mix_out = wg_mix_out.reshape(1, d, d)
    o_m = run(mm_nn, merged, wg_mix_out, tn=_tile(d, 512), tk=d, out_dtype=F32, name="mm_mix_out")
    x1, h2 = run(pre_ffn_fwd, xe, o_m, mod, norm_ffn_g)

    def ffn_blocks(buf):
        return buf.reshape(N_CHIPS, *big["w_ffn_in"].shape)

    tn_ffn = _tile(nj_ffn, 1408)
    landed = run(exchange_wait, "gather_d2d_wait_ffn_in_near", d2d_near_ffn, plan_d2d_near)
    f = run(mm_nn_blocks, h2, ffn_blocks(landed[0]), chip_arr, None, tn=tn_ffn, out_dtype=BF16, perm=perm_ffn,
            name="mm_ffn_in_own")
    landed = run(exchange_wait, "gather_ici_wait_ffn_in_far", ici_far_ffn, plan_ici_far, bufs=landed)
    d2d_far_ffn = run(exchange_start, "gather_d2d_start_ffn_in_far", landed, len(far), plan_d2d_far)
    f = run(mm_nn_blocks, h2, ffn_blocks(d2d_far_ffn.bufs[0]), chip_ids(near), f, tn=tn_ffn, out_dtype=BF16,
            perm=perm_ffn, name="mm_ffn_in_near")
    landed = run(exchange_wait, "gather_d2d_wait_ffn_in_far", d2d_far_ffn, plan_d2d_far)
    wg_ffn_in = ffn_blocks(landed[0])
    f = run(mm_nn_blocks, h2, wg_ffn_in, chip_ids(far), f, tn=tn_ffn, out_dtype=BF16, perm=perm_ffn,
            name="mm_ffn_in_far")
    d2d_ffn_out = gather_pass_on("ffn_out", ici["ffn_out"])
    act = run(swiglu_fwd, f, nj_ffn)
    (wg_ffn_out,) = gathered("ffn_out", d2d_ffn_out)
    wg_ffn_out = wg_ffn_out.reshape(1, dff, d)
    o_f = run(mm_nn, act, wg_ffn_out, tn=_tile(d, 512), tk=_tile(dff, 2816), out_dtype=F32, name="mm_ffn_out")
    loss11, dy, dof, acc_l = run(loss_head, x1, o_f, tgt, mod)

    gw_ffn_out = run(mm_tn, act, dof, 1, tk=_tile(dff, 512), tn=d, name="mm_ffn_out_dw")
    px_ffn_out = rs_pair_start("ffn_out", ["w_ffn_out"], [gw_ffn_out])
    dact = run(mm_nt, dof, wg_ffn_out, tko=_tile(dff, 512), tn=d, out_dtype=BF16, name="mm_ffn_out_dx")
    cx_ffn_out = rs_chip_start("ffn_out", ["w_ffn_out"], px_ffn_out)
    df = run(swiglu_bwd, f, dact, nj_ffn)
    gw_ffn_in = run(mm_tn, h2, df, N_CHIPS, tk=d, tn=_tile(nj_ffn, 1408), name="mm_ffn_in_dw",
                    perm=perm_ffn)
    px_ffn_in = rs_pair_start("ffn_in", ["w_ffn_in"], [gw_ffn_in])
    dh2 = run(mm_nt, df, wg_ffn_in, tko=_tile(d, 512), tn=nj_ffn, name="mm_ffn_in_dx", perm=perm_ffn)
    sh_ffn_out = rs_share_start("ffn_out", ["w_ffn_out"], cx_ffn_out)
    cx_ffn_in = rs_chip_start("ffn_in", ["w_ffn_in"], px_ffn_in)
    dx1, dom, acc_f = run(pre_ffn_bwd, x1, dh2, dy, o_m, mod, norm_ffn_g)
    gw_mix_out = run(mm_tn, merged, dom, 1, tk=d, tn=_tile(d, 1024), name="mm_mix_out_dw")
    px_mix = rs_pair_start("mix_out", ["w_mix_out"], [gw_mix_out])
    dmerged = run(mm_nt, dom, wg_mix_out, tko=_tile(d, 512), tn=d, out_dtype=BF16, name="mm_mix_out_dx")
    dy_attn, dy_conv, dga, dgc = run(merge_bwd, p, y_attn, y_conv, dmerged, off_ga, off_gc)
    rs_finish("ffn_out", ["w_ffn_out"], sh_ffn_out)
    cx_mix = rs_chip_start("mix_out", ["w_mix_out"], px_mix)
    gw_attn_out = run(mm_tn, attn_o, dy_attn, N_CHIPS, tk=aw, tn=_tile(wg_attn_out.shape[2], 512),
                      name="mm_attn_out_dw")
    gw_conv_out = run(mm_tn, s_conv, dy_conv, N_CHIPS, tk=ch, tn=_tile(wg_conv_out.shape[2], 512),
                      name="mm_conv_out_dw")
    ac_names = ["w_attn_out", "w_conv_out"]
    px_ac = rs_pair_start("attn_conv_out", ac_names, [gw_attn_out, gw_conv_out])
    dattn_o = run(mm_nt, dy_attn, wg_attn_out, tko=_tile(aw, 1024), tn=_tile(wg_attn_out.shape[2], 512),
                  out_dtype=BF16, name="mm_attn_out_dx")
    ds_conv = run(mm_nt, dy_conv, wg_conv_out, tko=_tile(ch, 1024), tn=_tile(wg_conv_out.shape[2], 512),
                  out_dtype=BF16, name="mm_conv_out_dx")
    cx_ac = rs_chip_start("attn_conv_out", ac_names, px_ac)
    dca, dcb, dconv_w, dconv_vec = run(conv_bwd, ca, cb, co_conv, ds_conv, conv_w_pad, conv_ln_g, conv_ln_b)
    sh_ffn_in = rs_share_start("ffn_in", ["w_ffn_in"], cx_ffn_in)
    dqkv, dbp, dbc, dsinks, dqg, dkg = run(attn_bwd, p, bias_p, bias_c, sinks3, q_norm_g, k_norm_g, dattn_o,
                                           aw=aw, kvw=kvw)
    sh_mix = rs_share_start("mix_out", ["w_mix_out"], cx_mix)
    sh_ac = rs_share_start("attn_conv_out", ac_names, cx_ac)
    drel = run(bias_table_bwd, dbp, dbc, bucket_p, bucket_c).reshape(NUM_BUCKETS, nq)
    dp = jnp.concatenate([dqkv, dca, dcb, dga, dgc], axis=1)
    gw_in = run(mm_tn, h, dp, N_CHIPS, tk=d, tn=wg_in.shape[2], name="mm_in_dw")
    px_in = rs_pair_start("in", ["w_in"], [gw_in])
    dh = run(mm_nt, dp, wg_in, tko=_tile(d, 1024), tn=wg_in.shape[2], name="mm_in_dx")
    grad_x, acc_m = run(pre_mix_bwd, xe, dh, dx1, mod, norm_mix_g)

    dmod = jnp.concatenate([acc_m[0:1], acc_m[1:2], acc_f[3:4], acc_f[0:1], acc_f[1:2], acc_l[0:1]], axis=1)
    small_names = ["b_ada", "norm_mix_g", "q_norm_g", "k_norm_g", "attn_sinks", "rel_bias", "conv_b", "conv_ln_g",
                   "conv_ln_b", "norm_ffn_g"]
    small_w = [b_ada, norm_mix_g, q_norm_g, k_norm_g, attn_sinks, rel_bias, conv_b, conv_ln_g, conv_ln_b, norm_ffn_g]
    small_m = [m_b_ada, m_norm_mix_g, m_q_norm_g, m_k_norm_g, m_attn_sinks, m_rel_bias, m_conv_b, m_conv_ln_g,
               m_conv_ln_b, m_norm_ffn_g]
    small_v = [v_b_ada, v_norm_mix_g, v_q_norm_g, v_k_norm_g, v_attn_sinks, v_rel_bias, v_conv_b, v_conv_ln_g,
               v_conv_ln_b, v_norm_ffn_g]
    small_g = [dmod, acc_m[2:3], dqg, dkg, dsinks.reshape(1, nq), drel.reshape(1, NUM_BUCKETS * nq),
               dconv_vec[0:1], dconv_vec[1:2], dconv_vec[2:3], acc_f[2:3]]
    row3, offs3 = _row_pack(small_g + [dconv_w[:CONV_WIDTH].reshape(1, CONV_WIDTH * ch), loss11])
    got3 = run(allgather_small, row3, "allgather_small_grads")
    cx_in = rs_chip_start("in", ["w_in"], px_in)
    as_row = lambda a: a.reshape(1, -1)
    outs3 = run(small_sum_adamw, got3, offs3, [as_row(a) for a in small_w], [as_row(a) for a in small_m],
                [as_row(a) for a in small_v], [CONV_WIDTH * ch, 1])
    for i, (n, w) in enumerate(zip(small_names, small_w)):
        grads[n], deltas[n], new_m[n], new_v[n] = (o.reshape(w.shape) for o in outs3[4 * i:4 * i + 4])
    g_conv_w_all, loss_sum = outs3[-2].reshape(CONV_WIDTH, ch), outs3[-1]

    g_conv_w = lax.dynamic_slice_in_dim(g_conv_w_all, chip * ch_loc, ch_loc, axis=1)
    grads["conv_w"] = g_conv_w[None]
    dl, nm, nv = run(adamw, conv_w[0], g_conv_w, m_conv_w[0], v_conv_w[0], "adamw_conv_w")
    deltas["conv_w"], new_m["conv_w"], new_v["conv_w"] = dl[None], nm[None], nv[None]

    dmod_all = got3[:, 0, :N_MOD * d]
    dmod_cols = lax.dynamic_slice_in_dim(dmod_all, chip * nc_ada, nc_ada, axis=1)
    g_ada, dl, nm, nv = run(ada_grad_adamw, c_t, dmod_cols, w_ada[0], m_w_ada[0], v_w_ada[0])
    grads["w_ada"], deltas["w_ada"], new_m["w_ada"], new_v["w_ada"] = g_ada[None], dl[None], nm[None], nv[None]

    rs_finish("ffn_in", ["w_ffn_in"], sh_ffn_in)
    rs_finish("mix_out", ["w_mix_out"], sh_mix)
    rs_finish("attn_conv_out", ac_names, sh_ac)
    sh_in = rs_share_start("in", ["w_in"], cx_in)
    rs_finish("in", ["w_in"], sh_in)

    loss = loss_sum[0, 0]
    order = ["w_ada", "b_ada", "norm_mix_g", "w_in", "q_norm_g", "k_norm_g", "attn_sinks", "rel_bias", "w_attn_out",
             "conv_w", "conv_b", "conv_ln_g", "conv_ln_b", "w_conv_out", "w_mix_out", "norm_ffn_g", "w_ffn_in",
             "w_ffn_out"]
    return (loss, grad_x[None], *[grads[n] for n in order], *[deltas[n] for n in order],
            *[new_m[n] for n in order], *[new_v[n] for n in order])
```

```python
import functools
import math
from typing import Any, NamedTuple

import jax
import jax.numpy as jnp
import numpy as np
from jax import lax
from jax.experimental import pallas as pl
from jax.experimental.pallas import tpu as pltpu

F32 = jnp.float32
BF16 = jnp.bfloat16
MESH = pl.DeviceIdType.MESH

V7X_VMEM_BYTES = 64 * 1024 * 1024
VMEM_LIMIT = V7X_VMEM_BYTES - 8 * 1024 * 1024
LANES = 128
SUBLANES = 8
BF16_SUBLANES = 16

EPS = 1e-6
WINDOW = 128
BLOCK = 128
NUM_BUCKETS = 32
MAX_EXACT = NUM_BUCKETS // 2
MAX_DISTANCE = 128
CONV_WIDTH = 31
CONV_HALO = 32
ADAM_LR = 0.001
ADAM_B1 = 0.9
ADAM_B2 = 0.999
ADAM_EPS = 1e-08
ADAM_WD = 0.01
ADAM_STEP = 10
N_MOD = 6
SH_M, SC_M, GT_M, SH_F, SC_F, GT_F = range(6)

N_CHIPS = 4
N_DEV = 8

_ANY = pl.BlockSpec(memory_space=pl.ANY)
_VMEM = pl.BlockSpec(memory_space=pltpu.VMEM)
_SMEM = pl.BlockSpec(memory_space=pltpu.SMEM)
_HBM = pl.BlockSpec(memory_space=pltpu.HBM)
_SEM = pl.BlockSpec(memory_space=pltpu.SEMAPHORE)
_EFFECT = pltpu.SideEffectType.DATAFLOW_SIDE_EFFECTING


class InOrder:
    def __init__(self):
        self.token = None

    def __call__(self, fn, *args, **kw):
        return fn(*args, dep=self, **kw)


def _pallas(body, args, *, in_specs, out_specs, out_shape, name, dep=None, grid=(), n_prefetch=0, scratch=(),
            sem=None, **kw):
    n_lead = n_prefetch + len(in_specs)
    in_specs, args = list(in_specs), list(args)
    single = not isinstance(out_shape, (list, tuple))
    out_shapes = [out_shape] if single else list(out_shape)
    out_specs = [out_specs] if single else list(out_specs)
    if dep is not None:
        inner, n_out, takes = body, len(out_shapes), dep.token is not None

        def body(*refs):
            rest = refs[n_lead + (1 if takes else 0):]
            rest[n_out][...] = jnp.zeros((SUBLANES, LANES), F32)
            return inner(*refs[:n_lead], *rest[:n_out], *rest[n_out + 1:])

        if takes:
            in_specs.append(_ANY)
            args.append(dep.token)
        out_shapes.append(jax.ShapeDtypeStruct((SUBLANES, LANES), F32))
        out_specs.append(pl.BlockSpec((SUBLANES, LANES), lambda *_: (0, 0)))
    params = kw.pop("compiler_params", None)
    if params is None:
        params = pltpu.CompilerParams(dimension_semantics=sem, vmem_limit_bytes=VMEM_LIMIT)
    outs = pl.pallas_call(
        body,
        grid_spec=pltpu.PrefetchScalarGridSpec(num_scalar_prefetch=n_prefetch, grid=grid, in_specs=in_specs,
                                               out_specs=out_specs, scratch_shapes=list(scratch)),
        out_shape=out_shapes, compiler_params=params, name=name, **kw,
    )(*args)
    if dep is not None:
        dep.token = outs[-1]
        outs = outs[:-1]
    return outs[0] if single else list(outs)


def _tile(n, pref, unit=LANES):
    best = None
    for t in range(unit, min(n, pref) + 1, unit):
        if n % t == 0:
            best = t
    return best if best is not None else n


def _sigmoid(v):
    return 1.0 / (1.0 + jnp.exp(-v.astype(F32)))


ROW_CHUNK = 512


def _row_chunks(m, unit=SUBLANES):
    step = _tile(m, ROW_CHUNK, unit)
    return [(s, step) for s in range(0, m, step)]


def _ew_tiles(r, n, unit=SUBLANES, elems=512 * 1024):
    return _tile(r, max(unit, elems // n), unit), n


def _block_pos(j, perm):
    if perm is None:
        return j
    pos = 0
    for a, p in enumerate(perm):
        pos = pos + jnp.where(j == a, p, 0)
    return pos


def mm_nn(a, w, *, tn, tk, out_dtype, name, perm=None, dep=None):
    m, k = a.shape
    j, k2, nj = w.shape
    assert k == k2 and nj % tn == 0 and k % tk == 0
    npj, nk = nj // tn, k // tk

    def body(a_ref, w_ref, o_ref, *scratch):
        kk = pl.program_id(1)
        for s, sz in _row_chunks(m):
            rows = pl.ds(s, sz)
            p = jnp.dot(a_ref[rows, :], w_ref[...], preferred_element_type=F32)
            if nk == 1:
                o_ref[rows, :] = p.astype(out_dtype)
            else:
                acc = scratch[0]

                @pl.when(kk == 0)
                def _():
                    acc[rows, :] = p

                @pl.when(kk > 0)
                def _():
                    acc[rows, :] += p

                @pl.when(kk == nk - 1)
                def _():
                    o_ref[rows, :] = acc[rows, :].astype(out_dtype)

    return _pallas(
        body, [a, w], dep=dep, grid=(j * npj, nk),
        in_specs=[
            pl.BlockSpec((m, tk), lambda n, kk: (0, kk)),
            pl.BlockSpec((None, tk, tn), lambda n, kk: (n // npj, kk, n % npj)),
        ],
        out_specs=pl.BlockSpec((m, tn), lambda n, kk: (0, _block_pos(n // npj, perm) * npj + n % npj)),
        out_shape=jax.ShapeDtypeStruct((m, j * nj), out_dtype),
        scratch=[pltpu.VMEM((m, tn), F32)] if nk > 1 else [],
        sem=("parallel", "arbitrary"), name=name)


def mm_nn_blocks(a, w, blocks, into, *, tn, out_dtype, name, perm=None, dep=None):
    m, k = a.shape
    j, k2, nj = w.shape
    assert k == k2 and nj % tn == 0
    npj = nj // tn
    n_in = 2 if into is None else 3

    def body(blocks_ref, a_ref, w_ref, *rest):
        o_ref = rest[n_in - 2]
        for s, sz in _row_chunks(m):
            rows = pl.ds(s, sz)
            o_ref[rows, :] = jnp.dot(a_ref[rows, :], w_ref[...], preferred_element_type=F32).astype(out_dtype)

    return _pallas(
        body, [blocks, a, w] + ([] if into is None else [into]), dep=dep, n_prefetch=1,
        grid=(blocks.shape[0] * npj,),
        in_specs=[pl.BlockSpec((m, k), lambda n, bl: (0, 0)),
                  pl.BlockSpec((None, k, tn), lambda n, bl: (bl[n // npj], 0, n % npj))]
        + ([] if into is None else [_ANY]),
        out_specs=pl.BlockSpec((m, tn), lambda n, bl: (0, _block_pos(bl[n // npj], perm) * npj + n % npj)),
        out_shape=jax.ShapeDtypeStruct((m, j * nj), out_dtype),
        input_output_aliases={} if into is None else {3: 0},
        sem=("arbitrary",), name=name)


def mm_nn_rowblocks(a, w, blocks, into, *, tn, name, dep=None):
    m, k = a.shape
    j, kj, n = w.shape
    assert k == j * kj and n % tn == 0
    nb = blocks.shape[0]
    n_in = 2 if into is None else 3

    def body(blocks_ref, a_ref, w_ref, *rest):
        o_ref = rest[n_in - 2]
        b = pl.program_id(1)
        for s, sz in _row_chunks(m):
            rows = pl.ds(s, sz)
            part = jnp.dot(a_ref[rows, :], w_ref[...], preferred_element_type=F32)

            @pl.when(b == 0)
            def _():
                o_ref[rows, :] = part if into is None else rest[0][rows, :] + part

            @pl.when(b > 0)
            def _():
                o_ref[rows, :] += part

    acc = pl.BlockSpec((m, tn), lambda i, b, bl: (0, i))
    return _pallas(
        body, [blocks, a, w] + ([] if into is None else [into]), dep=dep, n_prefetch=1, grid=(n // tn, nb),
        in_specs=[pl.BlockSpec((m, kj), lambda i, b, bl: (0, bl[b])),
                  pl.BlockSpec((None, kj, tn), lambda i, b, bl: (bl[b], 0, i))]
        + ([] if into is None else [acc]),
        out_specs=acc,
        out_shape=jax.ShapeDtypeStruct((m, n), F32),
        input_output_aliases={} if into is None else {3: 0},
        sem=("parallel", "arbitrary"), name=name)


def mm_nt(g, w, *, tko, tn, name, out_dtype=F32, perm=None, dep=None):
    m, n = g.shape
    j, k, nj = w.shape
    assert n == j * nj and nj % tn == 0 and k % tko == 0
    npj, nr = nj // tn, n // tn
    in_place = out_dtype == F32

    def body(g_ref, w_ref, o_ref, *scratch):
        r = pl.program_id(1)
        acc = o_ref if in_place else (scratch[0] if nr > 1 else None)
        for s, sz in _row_chunks(m):
            rows = pl.ds(s, sz)
            p = lax.dot_general(g_ref[rows, :], w_ref[...], (((1,), (1,)), ((), ())), preferred_element_type=F32)
            if acc is None:
                o_ref[rows, :] = p.astype(out_dtype)
                continue

            @pl.when(r == 0)
            def _():
                acc[rows, :] = p

            @pl.when(r > 0)
            def _():
                acc[rows, :] += p

            if not in_place:
                @pl.when(r == nr - 1)
                def _():
                    o_ref[rows, :] = acc[rows, :].astype(out_dtype)

    return _pallas(
        body, [g, w], dep=dep, grid=(k // tko, nr),
        in_specs=[
            pl.BlockSpec((m, tn), lambda ko, r: (0, _block_pos(r // npj, perm) * npj + r % npj)),
            pl.BlockSpec((None, tko, tn), lambda ko, r: (r // npj, ko, r % npj)),
        ],
        out_specs=pl.BlockSpec((m, tko), lambda ko, r: (0, ko)),
        out_shape=jax.ShapeDtypeStruct((m, k), out_dtype),
        scratch=[pltpu.VMEM((m, tko), F32)] if (nr > 1 and not in_place) else [],
        sem=("parallel", "arbitrary"), name=name)


def mm_tn(a, g, n_blocks, *, tk, tn, name, perm=None, dep=None):
    m, k = a.shape
    m2, n = g.shape
    nj = n // n_blocks
    assert m == m2 and nj % tn == 0 and k % tk == 0
    npj = nj // tn

    def body(a_ref, g_ref, o_ref):
        for s, sz in _row_chunks(tk, LANES):
            p = lax.dot_general(a_ref[:, pl.ds(s, sz)], g_ref[...], (((0,), (0,)), ((), ())),
                                preferred_element_type=F32)
            o_ref[pl.ds(s, sz), :] = p.astype(BF16)

    return _pallas(
        body, [a, g], dep=dep, grid=(k // tk, n // tn),
        in_specs=[
            pl.BlockSpec((m, tk), lambda kk, nn: (0, kk)),
            pl.BlockSpec((m, tn), lambda kk, nn: (0, _block_pos(nn // npj, perm) * npj + nn % npj)),
        ],
        out_specs=pl.BlockSpec((None, tk, tn), lambda kk, nn: (nn // npj, kk, nn % npj)),
        out_shape=jax.ShapeDtypeStruct((n_blocks, k, nj), BF16),
        sem=("parallel", "parallel"), name=name)


ROW_TILE = 256


def _row_spec(tr, width):
    return pl.BlockSpec((tr, width), lambda i: (i, 0))


def _full_spec(shape):
    return pl.BlockSpec(shape, lambda *_: (0,) * len(shape))


def _rms(xv):
    return lax.rsqrt(jnp.mean(xv * xv, axis=-1, keepdims=True) + EPS)


def _mod_row(mod_ref, row):
    return mod_ref[pl.ds(row, 1), :]


def pre_mix_fwd(x, mod, gain, dep=None):
    t, d = x.shape
    tr = _tile(t, ROW_TILE, SUBLANES)

    def body(x_ref, mod_ref, g_ref, h_ref):
        xv = x_ref[...]
        y = xv * _rms(xv) * g_ref[...]
        h_ref[...] = (y * (1.0 + _mod_row(mod_ref, SC_M)) + _mod_row(mod_ref, SH_M)).astype(BF16)

    return _pallas(
        body, [x, mod, gain], dep=dep, grid=(t // tr,),
        in_specs=[_row_spec(tr, d), _full_spec(mod.shape), _full_spec(gain.shape)],
        out_specs=_row_spec(tr, d),
        out_shape=jax.ShapeDtypeStruct((t, d), BF16),
        sem=("parallel",), name="pre_mix_fwd")


def pre_ffn_fwd(x, o_m, mod, gain, dep=None):
    t, d = x.shape
    tr = _tile(t, ROW_TILE, SUBLANES)

    def body(x_ref, om_ref, mod_ref, g_ref, x1_ref, h_ref):
        x1 = x_ref[...] + _mod_row(mod_ref, GT_M) * om_ref[...]
        x1_ref[...] = x1
        y = x1 * _rms(x1) * g_ref[...]
        h_ref[...] = (y * (1.0 + _mod_row(mod_ref, SC_F)) + _mod_row(mod_ref, SH_F)).astype(BF16)

    return _pallas(
        body, [x, o_m, mod, gain], dep=dep, grid=(t // tr,),
        in_specs=[_row_spec(tr, d), _row_spec(tr, d), _full_spec(mod.shape), _full_spec(gain.shape)],
        out_specs=[_row_spec(tr, d), _row_spec(tr, d)],
        out_shape=[jax.ShapeDtypeStruct((t, d), F32), jax.ShapeDtypeStruct((t, d), BF16)],
        sem=("parallel",), name="pre_ffn_fwd")


def loss_head(x1, o_f, target, mod, dep=None):
    t, d = x1.shape
    tr = _tile(t, ROW_TILE, SUBLANES)

    def body(x1_ref, of_ref, tg_ref, mod_ref, loss_ref, dy_ref, dof_ref, acc_ref):
        i = pl.program_id(0)
        gt = _mod_row(mod_ref, GT_F)
        of = of_ref[...]
        err = x1_ref[...] + gt * of - tg_ref[...]
        dy = err * (1.0 / d)
        dy_ref[...] = dy
        dof_ref[...] = (dy * gt).astype(BF16)
        part = (0.5 / d) * jnp.sum(jnp.sum(err * err, axis=1, keepdims=True), axis=0, keepdims=True)
        dgt = jnp.sum(dy * of, axis=0, keepdims=True)

        @pl.when(i == 0)
        def _():
            loss_ref[...] = jnp.zeros_like(loss_ref)
            acc_ref[...] = jnp.zeros_like(acc_ref)

        loss_ref[...] += part
        acc_ref[pl.ds(0, 1), :] += dgt

    return _pallas(
        body, [x1, o_f, target, mod], dep=dep, grid=(t // tr,),
        in_specs=[_row_spec(tr, d), _row_spec(tr, d), _row_spec(tr, d), _full_spec(mod.shape)],
        out_specs=[_full_spec((1, 1)), _row_spec(tr, d), _row_spec(tr, d), _full_spec((SUBLANES, d))],
        out_shape=[jax.ShapeDtypeStruct((1, 1), F32), jax.ShapeDtypeStruct((t, d), F32),
                   jax.ShapeDtypeStruct((t, d), BF16), jax.ShapeDtypeStruct((SUBLANES, d), F32)],
        sem=("arbitrary",), name="loss_head")


def _norm_bwd(xv, dh, sc, gain):
    rstd = _rms(xv)
    yn = xv * rstd
    dsh = jnp.sum(dh, axis=0, keepdims=True)
    dsc = jnp.sum(dh * (yn * gain), axis=0, keepdims=True)
    dgain = jnp.sum(dh * (1.0 + sc) * yn, axis=0, keepdims=True)
    dyn = dh * ((1.0 + sc) * gain)
    dx = rstd * (dyn - yn * jnp.mean(dyn * yn, axis=-1, keepdims=True))
    return dx, dsh, dsc, dgain


def pre_ffn_bwd(x1, dh2, dy, o_m, mod, gain, dep=None):
    t, d = x1.shape
    tr = _tile(t, ROW_TILE, SUBLANES)

    def body(x1_ref, dh_ref, dy_ref, om_ref, mod_ref, g_ref, dx1_ref, dom_ref, acc_ref):
        i = pl.program_id(0)
        dxn, dsh, dsc, dgain = _norm_bwd(x1_ref[...], dh_ref[...], _mod_row(mod_ref, SC_F), g_ref[...])
        dx1 = dy_ref[...] + dxn
        dx1_ref[...] = dx1
        dom_ref[...] = (dx1 * _mod_row(mod_ref, GT_M)).astype(BF16)
        dgt = jnp.sum(dx1 * om_ref[...], axis=0, keepdims=True)

        @pl.when(i == 0)
        def _():
            acc_ref[...] = jnp.zeros_like(acc_ref)

        acc_ref[pl.ds(0, 1), :] += dsh
        acc_ref[pl.ds(1, 1), :] += dsc
        acc_ref[pl.ds(2, 1), :] += dgain
        acc_ref[pl.ds(3, 1), :] += dgt

    return _pallas(
        body, [x1, dh2, dy, o_m, mod, gain], dep=dep, grid=(t // tr,),
        in_specs=[_row_spec(tr, d)] * 4 + [_full_spec(mod.shape), _full_spec(gain.shape)],
        out_specs=[_row_spec(tr, d), _row_spec(tr, d), _full_spec((SUBLANES, d))],
        out_shape=[jax.ShapeDtypeStruct((t, d), F32), jax.ShapeDtypeStruct((t, d), BF16),
                   jax.ShapeDtypeStruct((SUBLANES, d), F32)],
        sem=("arbitrary",), name="pre_ffn_bwd")


def pre_mix_bwd(x, dh, dx1, mod, gain, dep=None):
    t, d = x.shape
    tr = _tile(t, ROW_TILE, SUBLANES)

    def body(x_ref, dh_ref, dx1_ref, mod_ref, g_ref, gx_ref, acc_ref):
        i = pl.program_id(0)
        dxn, dsh, dsc, dgain = _norm_bwd(x_ref[...], dh_ref[...], _mod_row(mod_ref, SC_M), g_ref[...])
        gx_ref[...] = dx1_ref[...] + dxn

        @pl.when(i == 0)
        def _():
            acc_ref[...] = jnp.zeros_like(acc_ref)

        acc_ref[pl.ds(0, 1), :] += dsh
        acc_ref[pl.ds(1, 1), :] += dsc
        acc_ref[pl.ds(2, 1), :] += dgain

    return _pallas(
        body, [x, dh, dx1, mod, gain], dep=dep, grid=(t // tr,),
        in_specs=[_row_spec(tr, d)] * 3 + [_full_spec(mod.shape), _full_spec(gain.shape)],
        out_specs=[_row_spec(tr, d), _full_spec((SUBLANES, d))],
        out_shape=[jax.ShapeDtypeStruct((t, d), F32), jax.ShapeDtypeStruct((SUBLANES, d), F32)],
        sem=("arbitrary",), name="pre_mix_bwd")


def merge_fwd(p, y_attn, y_conv, off_ga, off_gc, dep=None):
    t, d = y_attn.shape
    tr = _tile(t, ROW_TILE, SUBLANES)
    cw = math.gcd(math.gcd(off_ga, off_gc), math.gcd(d, 512))
    nc = d // cw

    def body(ga_ref, gc_ref, ya_ref, yc_ref, o_ref):
        o_ref[...] = (_sigmoid(ga_ref[...]) * ya_ref[...] + _sigmoid(gc_ref[...]) * yc_ref[...]).astype(BF16)

    return _pallas(
        body, [p, p, y_attn, y_conv], dep=dep, grid=(t // tr, nc),
        in_specs=[pl.BlockSpec((tr, cw), lambda i, j: (i, off_ga // cw + j)),
                  pl.BlockSpec((tr, cw), lambda i, j: (i, off_gc // cw + j)),
                  pl.BlockSpec((tr, cw), lambda i, j: (i, j)),
                  pl.BlockSpec((tr, cw), lambda i, j: (i, j))],
        out_specs=pl.BlockSpec((tr, cw), lambda i, j: (i, j)),
        out_shape=jax.ShapeDtypeStruct((t, d), BF16),
        sem=("parallel", "parallel"), name="merge_fwd")


def merge_bwd(p, y_attn, y_conv, dmerged, off_ga, off_gc, dep=None):
    t, d = y_attn.shape
    tr = _tile(t, ROW_TILE, SUBLANES)
    cw = math.gcd(math.gcd(off_ga, off_gc), math.gcd(d, 512))
    nc = d // cw

    def body(ga_ref, gc_ref, ya_ref, yc_ref, dm_ref, dya_ref, dyc_ref, dga_ref, dgc_ref):
        dm = dm_ref[...].astype(F32)
        sa = _sigmoid(ga_ref[...])
        sc = _sigmoid(gc_ref[...])
        dya_ref[...] = (dm * sa).astype(BF16)
        dyc_ref[...] = (dm * sc).astype(BF16)
        dga_ref[...] = (dm * ya_ref[...] * sa * (1.0 - sa)).astype(BF16)
        dgc_ref[...] = (dm * yc_ref[...] * sc * (1.0 - sc)).astype(BF16)

    blk = pl.BlockSpec((tr, cw), lambda i, j: (i, j))
    return _pallas(
        body, [p, p, y_attn, y_conv, dmerged], dep=dep, grid=(t // tr, nc),
        in_specs=[pl.BlockSpec((tr, cw), lambda i, j: (i, off_ga // cw + j)),
                  pl.BlockSpec((tr, cw), lambda i, j: (i, off_gc // cw + j)), blk, blk, blk],
        out_specs=[blk] * 4,
        out_shape=[jax.ShapeDtypeStruct((t, d), BF16)] * 4,
        sem=("parallel", "parallel"), name="merge_bwd")


def ffn_perm(n_blocks):
    half = n_blocks // 2
    return tuple(2 * j if j < half else 2 * (j - half) + 1 for j in range(n_blocks))


def swiglu_fwd(f, nj, dep=None):
    t, two = f.shape
    tr = _tile(t, ROW_TILE, SUBLANES)
    npair = two // (2 * nj)

    def body(f_ref, o_ref):
        g = f_ref[:, :nj].astype(F32)
        u = f_ref[:, nj:].astype(F32)
        o_ref[...] = (g * _sigmoid(g) * u).astype(BF16)

    return _pallas(
        body, [f], dep=dep, grid=(t // tr, npair),
        in_specs=[pl.BlockSpec((tr, 2 * nj), lambda i, j: (i, j))],
        out_specs=pl.BlockSpec((tr, nj), lambda i, j: (i, j)),
        out_shape=jax.ShapeDtypeStruct((t, two // 2), BF16),
        sem=("parallel", "parallel"), name="swiglu_fwd")


def swiglu_bwd(f, dact, nj, dep=None):
    t, two = f.shape
    tr = _tile(t, ROW_TILE, SUBLANES)
    npair = two // (2 * nj)

    def body(f_ref, da_ref, o_ref):
        g = f_ref[:, :nj].astype(F32)
        u = f_ref[:, nj:].astype(F32)
        da = da_ref[...]
        s = _sigmoid(g)
        o_ref[:, :nj] = (da * u * (s * (1.0 + g * (1.0 - s)))).astype(BF16)
        o_ref[:, nj:] = (da * (g * s)).astype(BF16)

    return _pallas(
        body, [f, dact], dep=dep, grid=(t // tr, npair),
        in_specs=[pl.BlockSpec((tr, 2 * nj), lambda i, j: (i, j)), pl.BlockSpec((tr, nj), lambda i, j: (i, j))],
        out_specs=pl.BlockSpec((tr, 2 * nj), lambda i, j: (i, j)),
        out_shape=jax.ShapeDtypeStruct((t, two), BF16),
        sem=("parallel", "parallel"), name="swiglu_bwd")


def _t5_bucket_table():
    q_off = np.arange(BLOCK)
    k_off = np.arange(2 * BLOCK)
    dist = q_off[:, None] + BLOCK - k_off[None, :]
    n = np.maximum(dist, 0)
    nf = np.maximum(n, 1).astype(np.float32)
    large = MAX_EXACT + (np.log(nf / np.float32(MAX_EXACT)) / np.float32(math.log(MAX_DISTANCE / MAX_EXACT))
                         * np.float32(NUM_BUCKETS - MAX_EXACT)).astype(np.int32)
    large = np.minimum(large, NUM_BUCKETS - 1)
    bucket = np.where(n < MAX_EXACT, n, large).astype(np.int32)
    allowed = (dist >= 0) & (dist < WINDOW)
    return np.where(allowed, bucket, -1).astype(np.int32)


def bias_table(rel_bias, bucket_p, bucket_c, dep=None):
    nb, nq = rel_bias.shape

    def body(rb_ref, bkp_ref, bkc_ref, op_ref, oc_ref):
        for bk_ref, o_ref in ((bkp_ref, op_ref), (bkc_ref, oc_ref)):
            bk = bk_ref[...]
            for h in range(nq):
                acc = jnp.full(bk.shape, -jnp.inf, F32)
                for b in range(nb):
                    acc = jnp.where(bk == b, rb_ref[b, h], acc)
                o_ref[h] = acc

    return _pallas(
        body, [rel_bias, bucket_p, bucket_c], dep=dep,
        in_specs=[_SMEM, _VMEM, _VMEM], out_specs=[_VMEM, _VMEM],
        out_shape=[jax.ShapeDtypeStruct((nq,) + bucket_p.shape, F32)] * 2,
        name="bias_table")


def bias_table_bwd(dbp, dbc, bucket_p, bucket_c, dep=None):
    nq = dbp.shape[0]

    def body(dbp_ref, dbc_ref, bkp_ref, bkc_ref, o_ref):
        bkp, bkc = bkp_ref[...][None], bkc_ref[...][None]
        dp, dc = dbp_ref[...], dbc_ref[...]
        for b in range(NUM_BUCKETS):
            sel = jnp.where(bkp == b, dp, 0.0) + jnp.where(bkc == b, dc, 0.0)
            o_ref[b] = jnp.sum(jnp.sum(sel, axis=2, keepdims=True), axis=1, keepdims=True)

    return _pallas(
        body, [dbp, dbc, bucket_p, bucket_c], dep=dep,
        in_specs=[_VMEM] * 4, out_specs=_VMEM,
        out_shape=jax.ShapeDtypeStruct((NUM_BUCKETS, nq, 1, 1), F32),
        name="bias_table_bwd")


_BNT = (((2,), (2,)), ((0,), (0,)))
_BNN = (((2,), (1,)), ((0,), (0,)))
_BTN = (((1,), (1,)), ((0,), (0,)))


@jax.custom_vjp
def _bdot_nt(a, b):
    return lax.dot_general(a.astype(BF16), b.astype(BF16), _BNT, preferred_element_type=F32)


def _bdot_nt_fwd(a, b):
    return _bdot_nt(a, b), (a, b)


def _bdot_nt_bwd(res, g):
    a, b = res
    gb = g.astype(BF16)
    da = lax.dot_general(gb, b.astype(BF16), _BNN, preferred_element_type=F32)
    db = lax.dot_general(gb, a.astype(BF16), _BTN, preferred_element_type=F32)
    return da, db


_bdot_nt.defvjp(_bdot_nt_fwd, _bdot_nt_bwd)


@jax.custom_vjp
def _bdot_nn(a, b):
    return lax.dot_general(a.astype(BF16), b.astype(BF16), _BNN, preferred_element_type=F32)


def _bdot_nn_fwd(a, b):
    return _bdot_nn(a, b), (a, b)


def _bdot_nn_bwd(res, g):
    a, b = res
    gb = g.astype(BF16)
    da = lax.dot_general(gb, b.astype(BF16), _BNT, preferred_element_type=F32)
    db = lax.dot_general(a.astype(BF16), gb, _BTN, preferred_element_type=F32)
    return da, db


_bdot_nn.defvjp(_bdot_nn_fwd, _bdot_nn_bwd)


def _attn_math(q, kp, kc, vp, vc, bp, bc, sinks, qg, kg, *, prev_ok, scale):
    h, rows, _ = q.shape
    b = kp.shape[1]
    qn = q * _rms(q) * qg
    kpn = kp * _rms(kp) * kg
    kcn = kc * _rms(kc) * kg
    lp = _bdot_nt(qn, kpn) * scale + bp.reshape(h, rows, b)
    lc = _bdot_nt(qn, kcn) * scale + bc.reshape(h, rows, b)
    lp = jnp.where(prev_ok, lp, -jnp.inf)
    sink = jnp.broadcast_to(sinks, (sinks.shape[0], b, 1)).reshape(h, rows, 1)
    m = jnp.maximum(jnp.maximum(jnp.max(lp, axis=-1, keepdims=True), jnp.max(lc, axis=-1, keepdims=True)), sink)
    m = lax.stop_gradient(m)
    pp = jnp.exp(lp - m)
    pc = jnp.exp(lc - m)
    den = jnp.sum(pp, axis=-1, keepdims=True) + jnp.sum(pc, axis=-1, keepdims=True) + jnp.exp(sink - m)
    inv = 1.0 / den
    return _bdot_nn(pp * inv, vp) + _bdot_nn(pc * inv, vc)


def _attn_specs(p, aw, kvw, nq, hd, nblk, reverse):
    assert aw % (2 * kvw) == 0
    kv_col = aw // (2 * kvw)

    def blk(n):
        return nblk - 1 - n if reverse else n

    return [
        pl.BlockSpec((BLOCK, aw), lambda n: (blk(n), 0)),
        pl.BlockSpec((BLOCK, 2 * kvw), lambda n: (jnp.maximum(blk(n) - 1, 0), kv_col)),
        pl.BlockSpec((BLOCK, 2 * kvw), lambda n: (blk(n), kv_col)),
        _full_spec((nq, BLOCK, BLOCK)), _full_spec((nq, BLOCK, BLOCK)), _full_spec((nq, 1, 1)),
        _full_spec((1, hd)), _full_spec((1, hd)),
    ]


def _head_major(ref, n_heads, grp, hd, offset=0):
    return jnp.stack([
        jnp.concatenate([ref[:, pl.ds(offset + (grp * h + g) * hd, hd)].astype(F32) for g in range(grp)], axis=0)
        for h in range(n_heads)])


def _attn_inputs(nkv, grp, hd, kvw, q_ref, kvp_ref, kvc_ref):
    return (_head_major(q_ref, nkv, grp, hd), _head_major(kvp_ref, nkv, 1, hd), _head_major(kvc_ref, nkv, 1, hd),
            _head_major(kvp_ref, nkv, 1, hd, kvw), _head_major(kvc_ref, nkv, 1, hd, kvw))


def attn_fwd(p, bias_p, bias_c, sinks, qg, kg, *, aw, kvw, dep=None):
    t, hd = p.shape[0], qg.shape[-1]
    nq, nkv, nblk = aw // hd, kvw // hd, t // BLOCK
    grp = nq // nkv
    scale = hd ** -0.5

    def body(q_ref, kvp_ref, kvc_ref, bp_ref, bc_ref, s_ref, qg_ref, kg_ref, o_ref):
        prev_ok = pl.program_id(0) > 0
        out = _attn_math(*_attn_inputs(nkv, grp, hd, kvw, q_ref, kvp_ref, kvc_ref), bp_ref[...], bc_ref[...],
                         s_ref[...], qg_ref[...], kg_ref[...], prev_ok=prev_ok, scale=scale)
        for h in range(nkv):
            for g in range(grp):
                o_ref[:, pl.ds((grp * h + g) * hd, hd)] = out[h, g * BLOCK:(g + 1) * BLOCK].astype(BF16)

    return _pallas(
        body, [p, p, p, bias_p, bias_c, sinks, qg, kg], dep=dep, grid=(nblk,),
        in_specs=_attn_specs(p, aw, kvw, nq, hd, nblk, False),
        out_specs=pl.BlockSpec((BLOCK, aw), lambda n: (n, 0)),
        out_shape=jax.ShapeDtypeStruct((t, aw), BF16),
        sem=("parallel",), name="attn_fwd")


def attn_bwd(p, bias_p, bias_c, sinks, qg, kg, do, *, aw, kvw, dep=None):
    t, hd = p.shape[0], qg.shape[-1]
    nq, nkv, nblk = aw // hd, kvw // hd, t // BLOCK
    grp = nq // nkv
    scale = hd ** -0.5

    def body(q_ref, kvp_ref, kvc_ref, bp_ref, bc_ref, s_ref, qg_ref, kg_ref, do_ref,
             dqkv_ref, dbp_ref, dbc_ref, ds_ref, dqg_ref, dkg_ref, carry):
        i = pl.program_id(0)
        prev_ok = (nblk - 1 - i) > 0

        @pl.when(i == 0)
        def _():
            carry[...] = jnp.zeros_like(carry)
            dbp_ref[...] = jnp.zeros_like(dbp_ref)
            dbc_ref[...] = jnp.zeros_like(dbc_ref)
            ds_ref[...] = jnp.zeros_like(ds_ref)
            dqg_ref[...] = jnp.zeros_like(dqg_ref)
            dkg_ref[...] = jnp.zeros_like(dkg_ref)

        fn = functools.partial(_attn_math, prev_ok=prev_ok, scale=scale)
        _, vjp = jax.vjp(fn, *_attn_inputs(nkv, grp, hd, kvw, q_ref, kvp_ref, kvc_ref), bp_ref[...], bc_ref[...],
                         s_ref[...], qg_ref[...], kg_ref[...])
        dq, dkp, dkc, dvp, dvc, dbp, dbc, dsk, dqg, dkg = vjp(_head_major(do_ref, nkv, grp, hd))
        for h in range(nkv):
            for g in range(grp):
                dqkv_ref[:, pl.ds((grp * h + g) * hd, hd)] = dq[h, g * BLOCK:(g + 1) * BLOCK].astype(BF16)
            k_cols, v_cols = pl.ds(h * hd, hd), pl.ds(kvw + h * hd, hd)
            dqkv_ref[:, pl.ds(aw + h * hd, hd)] = (dkc[h] + carry[:, k_cols]).astype(BF16)
            dqkv_ref[:, pl.ds(aw + kvw + h * hd, hd)] = (dvc[h] + carry[:, v_cols]).astype(BF16)
            carry[:, k_cols] = dkp[h]
            carry[:, v_cols] = dvp[h]
        dbp_ref[...] += dbp
        dbc_ref[...] += dbc
        ds_ref[...] += dsk
        dqg_ref[...] += dqg
        dkg_ref[...] += dkg

    return _pallas(
        body, [p, p, p, bias_p, bias_c, sinks, qg, kg, do], dep=dep, grid=(nblk,),
        in_specs=_attn_specs(p, aw, kvw, nq, hd, nblk, True)
        + [pl.BlockSpec((BLOCK, aw), lambda n: (nblk - 1 - n, 0))],
        out_specs=[
            pl.BlockSpec((BLOCK, aw + 2 * kvw), lambda n: (nblk - 1 - n, 0)),
            _full_spec((nq, BLOCK, BLOCK)), _full_spec((nq, BLOCK, BLOCK)), _full_spec((nq, 1, 1)),
            _full_spec((1, hd)), _full_spec((1, hd)),
        ],
        out_shape=[
            jax.ShapeDtypeStruct((t, aw + 2 * kvw), BF16),
            jax.ShapeDtypeStruct((nq, BLOCK, BLOCK), F32),
            jax.ShapeDtypeStruct((nq, BLOCK, BLOCK), F32),
            jax.ShapeDtypeStruct((nq, 1, 1), F32),
            jax.ShapeDtypeStruct((1, hd), F32),
            jax.ShapeDtypeStruct((1, hd), F32),
        ],
        scratch=[pltpu.VMEM((BLOCK, 2 * kvw), F32)],
        sem=("arbitrary",), name="attn_bwd")


CONV_TILE = 256


def _conv_halo_specs(tb, ch, nblk):
    per = tb // CONV_HALO
    last = nblk * per - 1
    cur = pl.BlockSpec((tb, ch), lambda n: (n, 0))
    prev = pl.BlockSpec((CONV_HALO, ch), lambda n: (jnp.maximum(n * per - 1, 0), 0))
    nxt = pl.BlockSpec((CONV_HALO, ch), lambda n: (jnp.minimum((n + 1) * per, last), 0))
    return cur, prev, nxt


def _ln_silu(co, ln_g, ln_b):
    mu = jnp.mean(co, axis=-1, keepdims=True)
    cen = co - mu
    rstd = lax.rsqrt(jnp.mean(cen * cen, axis=-1, keepdims=True) + EPS)
    xhat = cen * rstd
    z = xhat * ln_g + ln_b
    return xhat, rstd, z


def _shifted_copies(src, shifted):
    rows = src.shape[0] - SUBLANES
    for r in range(1, SUBLANES):
        shifted[r, pl.ds(0, rows), :] = src[pl.ds(r, rows), :]


def _rows_from(src, shifted, start, n):
    r = start % SUBLANES
    if r == 0:
        return src[pl.ds(start, n), :]
    return shifted[r, pl.ds(start - r, n), :]


def conv_fwd(ca, cb, conv_w, conv_b, ln_g, ln_b, dep=None):
    t, ch = ca.shape
    tb = _tile(t, CONV_TILE, CONV_HALO)
    nblk = t // tb
    cur, prev, _ = _conv_halo_specs(tb, ch, nblk)
    lead = CONV_HALO - (CONV_WIDTH - 1)

    def body(ca_ref, cb_ref, cap_ref, cbp_ref, w_ref, b_ref, g_ref, bb_ref, s_ref, co_ref, ubuf, ushift):
        n = pl.program_id(0)
        halo = cap_ref[...] * _sigmoid(cbp_ref[...])
        ubuf[pl.ds(0, CONV_HALO), :] = jnp.where(n > 0, halo, 0.0)
        ubuf[pl.ds(CONV_HALO, tb), :] = ca_ref[...] * _sigmoid(cb_ref[...])
        _shifted_copies(ubuf, ushift)
        acc = jnp.broadcast_to(b_ref[...], (tb, ch))
        for k in range(CONV_WIDTH):
            acc = acc + w_ref[pl.ds(k, 1), :] * _rows_from(ubuf, ushift, lead + k, tb)
        co_ref[...] = acc
        _, _, z = _ln_silu(acc, g_ref[...], bb_ref[...])
        s_ref[...] = (z * _sigmoid(z)).astype(BF16)

    vec = _full_spec((1, ch))
    return _pallas(
        body, [ca, cb, ca, cb, conv_w, conv_b, ln_g, ln_b], dep=dep, grid=(nblk,),
        in_specs=[cur, cur, prev, prev, _full_spec(conv_w.shape), vec, vec, vec],
        out_specs=[cur, cur],
        out_shape=[jax.ShapeDtypeStruct((t, ch), BF16), jax.ShapeDtypeStruct((t, ch), F32)],
        scratch=[pltpu.VMEM((CONV_HALO + tb, ch), F32), pltpu.VMEM((SUBLANES, CONV_HALO + tb, ch), F32)],
        sem=("parallel",), name="conv_fwd")


def conv_bwd(ca, cb, co, ds, conv_w, ln_g, ln_b, dep=None):
    t, ch = ca.shape
    tb = _tile(t, CONV_TILE, CONV_HALO)
    nblk = t // tb
    cur, prev, nxt = _conv_halo_specs(tb, ch, nblk)
    lead = CONV_HALO - (CONV_WIDTH - 1)
    ext = tb + CONV_HALO

    def body(ca_ref, cb_ref, cap_ref, cbp_ref, co_ref, con_ref, ds_ref, dsn_ref, w_ref, g_ref, bb_ref,
             dca_ref, dcb_ref, dw_ref, dvec_ref, ubuf, dbuf, ushift, dshift):
        n = pl.program_id(0)
        is_last = n == nblk - 1
        sig_b = _sigmoid(cb_ref[...])
        cav = ca_ref[...].astype(F32)
        ubuf[pl.ds(0, CONV_HALO), :] = jnp.where(n > 0, cap_ref[...] * _sigmoid(cbp_ref[...]), 0.0)
        ubuf[pl.ds(CONV_HALO, tb), :] = cav * sig_b
        _shifted_copies(ubuf, ushift)
        co = jnp.concatenate([co_ref[...], con_ref[...]], axis=0)
        xhat, rstd, z = _ln_silu(co, g_ref[...], bb_ref[...])
        dsv = jnp.concatenate([ds_ref[...].astype(F32), jnp.where(is_last, 0.0, dsn_ref[...].astype(F32))], axis=0)
        sg = _sigmoid(z)
        dz = dsv * (sg * (1.0 + z * (1.0 - sg)))
        dxh = dz * g_ref[...]
        dco = rstd * (dxh - jnp.mean(dxh, axis=-1, keepdims=True)
                      - xhat * jnp.mean(dxh * xhat, axis=-1, keepdims=True))
        dbuf[...] = dco
        _shifted_copies(dbuf, dshift)

        @pl.when(n == 0)
        def _():
            dw_ref[...] = jnp.zeros_like(dw_ref)
            dvec_ref[...] = jnp.zeros_like(dvec_ref)

        dco_cur = dco[:tb]
        dvec_ref[pl.ds(0, 1), :] += jnp.sum(dco_cur, axis=0, keepdims=True)
        dvec_ref[pl.ds(1, 1), :] += jnp.sum(dz[:tb] * xhat[:tb], axis=0, keepdims=True)
        dvec_ref[pl.ds(2, 1), :] += jnp.sum(dz[:tb], axis=0, keepdims=True)
        du = jnp.zeros((tb, ch), F32)
        for k in range(CONV_WIDTH):
            du = du + w_ref[pl.ds(k, 1), :] * _rows_from(dbuf, dshift, CONV_WIDTH - 1 - k, tb)
            dw_ref[pl.ds(k, 1), :] += jnp.sum(dco_cur * _rows_from(ubuf, ushift, lead + k, tb), axis=0,
                                              keepdims=True)
        dca_ref[...] = (du * sig_b).astype(BF16)
        dcb_ref[...] = (du * cav * sig_b * (1.0 - sig_b)).astype(BF16)

    vec = _full_spec((1, ch))
    return _pallas(
        body, [ca, cb, ca, cb, co, co, ds, ds, conv_w, ln_g, ln_b], dep=dep, grid=(nblk,),
        in_specs=[cur, cur, prev, prev, cur, nxt, cur, nxt, _full_spec(conv_w.shape), vec, vec],
        out_specs=[cur, cur, _full_spec(conv_w.shape), _full_spec((SUBLANES, ch))],
        out_shape=[jax.ShapeDtypeStruct((t, ch), BF16), jax.ShapeDtypeStruct((t, ch), BF16),
                   jax.ShapeDtypeStruct(conv_w.shape, F32), jax.ShapeDtypeStruct((SUBLANES, ch), F32)],
        scratch=[pltpu.VMEM((CONV_HALO + tb, ch), F32), pltpu.VMEM((ext, ch), F32),
                 pltpu.VMEM((SUBLANES, CONV_HALO + tb, ch), F32), pltpu.VMEM((SUBLANES, ext, ch), F32)],
        sem=("arbitrary",), name="conv_bwd")


def ada_fwd(c_t, w_ada, dep=None):
    d, nc = w_ada.shape
    nex = c_t.shape[1]
    tn = _tile(nc, 512)

    def body(ct_ref, w_ref, o_ref):
        w = w_ref[...]
        ct = ct_ref[...]
        cact = ct * _sigmoid(ct)
        rows = [jnp.sum(w * cact[:, b:b + 1], axis=0, keepdims=True) for b in range(nex)]
        o_ref[...] = jnp.concatenate(rows, axis=0)

    return _pallas(
        body, [c_t, w_ada], dep=dep, grid=(nc // tn,),
        in_specs=[_full_spec(c_t.shape), pl.BlockSpec((d, tn), lambda j: (0, j))],
        out_specs=pl.BlockSpec((nex, tn), lambda j: (0, j)),
        out_shape=jax.ShapeDtypeStruct((nex, nc), F32),
        sem=("parallel",), name="ada_fwd")


def _adamw_math(w, g, m, v):
    m = ADAM_B1 * m + (1.0 - ADAM_B1) * g
    v = ADAM_B2 * v + (1.0 - ADAM_B2) * (g * g)
    m_hat = m / (1.0 - ADAM_B1 ** ADAM_STEP)
    v_hat = v / (1.0 - ADAM_B2 ** ADAM_STEP)
    delta = -ADAM_LR * (m_hat / (jnp.sqrt(v_hat) + ADAM_EPS) + ADAM_WD * w)
    return delta, m, v


def adamw(w, g, m, v, name, copy_grad=False, dep=None):
    r, n = w.shape
    tr, tn = _ew_tiles(r, n, elems=256 * 1024)
    n_out = 4 if copy_grad else 3

    def body(w_ref, g_ref, m_ref, v_ref, *outs):
        g = g_ref[...]
        if copy_grad:
            outs[0][...] = g
        outs[-3][...], outs[-2][...], outs[-1][...] = _adamw_math(w_ref[...], g, m_ref[...], v_ref[...])

    blk = pl.BlockSpec((tr, tn), lambda i, j: (i, j))
    return _pallas(
        body, [w, g, m, v], dep=dep, grid=(r // tr, n // tn),
        in_specs=[blk] * 4, out_specs=[blk] * n_out,
        out_shape=[jax.ShapeDtypeStruct((r, n), F32)] * n_out,
        sem=("parallel", "parallel"), name=name)


def ada_grad_adamw(c_t, dmod_cols, w, m, v, dep=None):
    d, nc = w.shape
    nex = c_t.shape[1]
    tr, tn = _ew_tiles(d, nc, elems=256 * 1024)

    def body(ct_ref, dm_ref, w_ref, m_ref, v_ref, g_ref, d_ref, nm_ref, nv_ref):
        ct = ct_ref[...]
        cact = ct * _sigmoid(ct)
        dm = dm_ref[...]
        g = cact[:, 0:1] * dm[0:1, :]
        for b in range(1, nex):
            g = g + cact[:, b:b + 1] * dm[b:b + 1, :]
        g_ref[...] = g
        d_ref[...], nm_ref[...], nv_ref[...] = _adamw_math(w_ref[...], g, m_ref[...], v_ref[...])

    blk = pl.BlockSpec((tr, tn), lambda i, j: (i, j))
    return _pallas(
        body, [c_t, dmod_cols, w, m, v], dep=dep, grid=(d // tr, nc // tn),
        in_specs=[pl.BlockSpec((tr, nex), lambda i, j: (i, 0)), pl.BlockSpec((nex, tn), lambda i, j: (0, j)),
                  blk, blk, blk],
        out_specs=[blk] * 4,
        out_shape=[jax.ShapeDtypeStruct((d, nc), F32)] * 4,
        sem=("parallel", "parallel"), name="ada_grad_adamw")


def _row_pack(parts):
    cols, offs, off = [], [], 0
    for p in parts:
        n = p.shape[1]
        width = -(-n // LANES) * LANES
        cols.append(jnp.pad(p, ((0, 0), (0, width - n))) if width != n else p)
        offs.append(off)
        off += width
    return jnp.concatenate(cols, axis=1), offs


def small_sum_adamw(gathered, offs, ws, ms, vs, extra_widths, dep=None):
    ndev = gathered.shape[0]
    npar = len(ws)

    def body(ga_ref, *refs):
        w_refs, m_refs, v_refs = refs[:npar], refs[npar:2 * npar], refs[2 * npar:3 * npar]
        outs = refs[3 * npar:]
        tot = ga_ref[0]
        for s in range(1, ndev):
            tot = tot + ga_ref[s]
        for i in range(npar):
            n = ws[i].shape[1]
            g = tot[:, offs[i]:offs[i] + n]
            outs[4 * i][...] = g
            outs[4 * i + 1][...], outs[4 * i + 2][...], outs[4 * i + 3][...] = _adamw_math(
                w_refs[i][...], g, m_refs[i][...], v_refs[i][...])
        for e, n in enumerate(extra_widths):
            off = offs[npar + e]
            outs[4 * npar + e][...] = tot[:, off:off + n]

    shapes = [jax.ShapeDtypeStruct(w.shape, F32) for w in ws for _ in range(4)]
    shapes += [jax.ShapeDtypeStruct((1, n), F32) for n in extra_widths]
    return _pallas(
        body, [gathered, *ws, *ms, *vs], dep=dep, in_specs=[_VMEM] * (1 + 3 * npar), out_specs=[_VMEM] * len(shapes),
        out_shape=shapes, name="small_sum_adamw")


def _position():
    return lax.axis_index("x"), lax.axis_index("y"), lax.axis_index("c")


def _other_chips(x, y):
    return [(1 - x, y), (x, 1 - y), (1 - x, 1 - y)]


def allgather_small(block, name, dep=None):
    def body(x_ref, out_ref, send_sems, recv_sems, local_sem):
        x, y, c = _position()
        me, sibling = (x, y, c), (x, y, 1 - c)
        chips = _other_chips(x, y)

        def slot(px, py, pc):
            return out_ref.at[4 * px + 2 * py + pc]

        def copy(k, block_of, to, src=None):
            return pltpu.make_async_remote_copy(
                src_ref=slot(*block_of) if src is None else src, dst_ref=slot(*block_of),
                send_sem=send_sems.at[k], recv_sem=recv_sems.at[k], device_id=to, device_id_type=MESH)

        mine = pltpu.make_async_copy(x_ref, slot(*me), local_sem)
        mine.start()
        first = [copy(0, me, sibling, src=x_ref)]
        first += [copy(1 + j, me, (*chip, c), src=x_ref) for j, chip in enumerate(chips)]
        for cp in first:
            cp.start()
        passed = [copy(4 + j, (*chip, c), sibling) for j, chip in enumerate(chips)]
        for j, chip in enumerate(chips):
            copy(1 + j, (*chip, c), me).wait_recv()
            passed[j].start()
        copy(0, sibling, me).wait_recv()
        for j, chip in enumerate(chips):
            copy(4 + j, (*chip, 1 - c), me).wait_recv()
        for cp in first + passed:
            cp.wait_send()
        mine.wait()

    return _pallas(
        body, [block], dep=dep,
        out_shape=jax.ShapeDtypeStruct((N_DEV, *block.shape), block.dtype),
        in_specs=[_VMEM], out_specs=_VMEM,
        scratch=[pltpu.SemaphoreType.DMA((7,)), pltpu.SemaphoreType.DMA((7,)), pltpu.SemaphoreType.DMA],
        name=name)


class Started(NamedTuple):
    send_sems: Any
    recv_sems: Any
    bufs: list


def exchange_start(name, bufs, n_copies, plan, dep=None):
    nb = len(bufs)

    def body(*refs):
        for cp in plan(refs[:nb], refs[nb], refs[nb + 1]):
            cp.start()

    outs = _pallas(
        body, [pltpu.with_memory_space_constraint(b, pltpu.HBM) for b in bufs], dep=dep, name=name,
        out_shape=(pltpu.SemaphoreType.DMA((n_copies,)), pltpu.SemaphoreType.DMA((n_copies,)),
                   *[pltpu.HBM(b.shape, b.dtype) for b in bufs]),
        in_specs=[_HBM] * nb,
        out_specs=(_SEM, _SEM, *[_HBM] * nb),
        input_output_aliases={i: 2 + i for i in range(nb)},
        compiler_params=pltpu.CompilerParams(has_side_effects=_EFFECT))
    return Started(outs[0], outs[1], list(outs[2:2 + nb]))


def exchange_wait(name, started, plan, bufs=None, dep=None):
    if bufs is not None:
        started = started._replace(bufs=list(bufs))
    nb = len(started.bufs)

    def body(*refs):
        for cp in plan(refs[:nb], refs[nb], refs[nb + 1]):
            cp.wait_send()
            cp.wait_recv()

    outs = _pallas(
        body, [*started.bufs, started.send_sems, started.recv_sems], dep=dep, name=name,
        out_shape=tuple(pltpu.HBM(b.shape, b.dtype) for b in started.bufs),
        in_specs=[_HBM] * nb + [_SEM, _SEM],
        out_specs=tuple([_HBM] * nb),
        input_output_aliases={i: i for i in range(nb)},
        compiler_params=pltpu.CompilerParams(has_side_effects=_EFFECT))
    return list(outs)


def _remote(src, dst, send_sems, recv_sems, i, to):
    return pltpu.make_async_remote_copy(src_ref=src, dst_ref=dst, send_sem=send_sems.at[i], recv_sem=recv_sems.at[i],
                                        device_id=to, device_id_type=MESH)


def _half_rows(buf_rows, chip_idx, pc):
    half = buf_rows // (2 * N_CHIPS)
    return pl.ds((2 * chip_idx + pc) * half, half)


ALL_PEERS = (0, 1, 2)


def plan_gather_ici(refs, send_sems, recv_sems, peers=ALL_PEERS):
    x, y, c = _position()
    chips = _other_chips(x, y)
    copies = []
    for k, ref in enumerate(refs):
        rows = ref.at[_half_rows(ref.shape[0], 2 * x + y, c), :]
        for i, j in enumerate(peers):
            copies.append(_remote(rows, rows, send_sems, recv_sems, len(peers) * k + i, (*chips[j], c)))
    return copies


def plan_gather_d2d(refs, send_sems, recv_sems, peers=ALL_PEERS):
    x, y, c = _position()
    chips = _other_chips(x, y)
    copies = []
    for k, ref in enumerate(refs):
        for i, j in enumerate(peers):
            px, py = chips[j]
            rows = ref.at[_half_rows(ref.shape[0], 2 * px + py, c), :]
            copies.append(_remote(rows, rows, send_sems, recv_sems, len(peers) * k + i, (x, y, 1 - c)))
    return copies


def plan_pair_exchange(refs, send_sems, recv_sems):
    x, y, c = _position()
    nw = len(refs) // 2
    copies = []
    for k in range(nw):
        for chip in range(N_CHIPS):
            copies.append(_remote(refs[k].at[chip, 1 - c], refs[nw + k].at[chip], send_sems, recv_sems,
                                  N_CHIPS * k + chip, (x, y, 1 - c)))
    return copies


def plan_chip_exchange(refs, send_sems, recv_sems):
    x, y, c = _position()
    nw = len(refs) // 2
    copies = []
    for k in range(nw):
        for j, (px, py) in enumerate(_other_chips(x, y)):
            copies.append(_remote(refs[k].at[2 * px + py], refs[nw + k].at[2 * x + y], send_sems, recv_sems,
                                  3 * k + j, (px, py, c)))
    return copies


def plan_pair_share(refs, send_sems, recv_sems):
    x, y, c = _position()
    return [_remote(ref.at[c], ref.at[c], send_sems, recv_sems, k, (x, y, 1 - c)) for k, ref in enumerate(refs)]


def cast_into_slot(src, slot, n_slots, name, dep=None):
    r, n = src.shape
    tr, tn = _ew_tiles(r, n, BF16_SUBLANES)

    def body(slot_ref, s_ref, o_ref):
        o_ref[...] = s_ref[...].astype(BF16)

    return _pallas(
        body, [slot, src], dep=dep, n_prefetch=1, grid=(r // tr, n // tn),
        in_specs=[pl.BlockSpec((tr, tn), lambda i, j, sl: (i, j))],
        out_specs=pl.BlockSpec((None, tr, tn), lambda i, j, sl: (sl[0], i, j)),
        out_shape=jax.ShapeDtypeStruct((n_slots, r, n), BF16),
        sem=("parallel", "parallel"), name=name)


def pair_sum(g, r, core, name, dep=None):
    nchip, _, h, n = g.shape
    th, tn = _ew_tiles(h, n, BF16_SUBLANES)

    def body(core_ref, g_ref, r_ref, o_ref):
        o_ref[...] = (g_ref[...].astype(F32) + r_ref[...].astype(F32)).astype(BF16)

    return _pallas(
        body, [core, g, r], dep=dep, n_prefetch=1, grid=(nchip, h // th, n // tn),
        in_specs=[pl.BlockSpec((None, None, th, tn), lambda a, i, j, cr: (a, cr[0], i, j)),
                  pl.BlockSpec((None, th, tn), lambda a, i, j, cr: (a, i, j))],
        out_specs=pl.BlockSpec((None, th, tn), lambda a, i, j, cr: (a, i, j)),
        out_shape=jax.ShapeDtypeStruct((nchip, h, n), BF16),
        sem=("parallel", "parallel", "parallel"), name=name)


def chip_sum(own, got, where, name, dep=None):
    nchip, h, n = got.shape
    th, tn = _ew_tiles(h, n, BF16_SUBLANES, elems=256 * 1024)

    def body(where_ref, own_ref, *rest):
        got_refs, o_ref = rest[:nchip], rest[nchip]
        chip = where_ref[0]
        acc = None
        for s in range(nchip):
            term = jnp.where(chip == s, own_ref[...], got_refs[s][...]).astype(F32)
            acc = term if acc is None else acc + term
        o_ref[...] = acc

    def got_spec(s):
        return pl.BlockSpec((None, th, tn), lambda i, j, wr: (jnp.where(wr[0] == s, (s + 1) % nchip, s), i, j))

    return _pallas(
        body, [where, own, *[got] * nchip], dep=dep, n_prefetch=1, grid=(h // th, n // tn),
        in_specs=[pl.BlockSpec((None, th, tn), lambda i, j, wr: (wr[0], i, j))]
        + [got_spec(s) for s in range(nchip)],
        out_specs=pl.BlockSpec((None, th, tn), lambda i, j, wr: (wr[1], i, j)),
        out_shape=jax.ShapeDtypeStruct((2, h, n), F32),
        sem=("parallel", "parallel"), name=name)


def kernel(x, c, w_ada, b_ada, norm_mix_g, w_in, q_norm_g, k_norm_g, attn_sinks, rel_bias, w_attn_out, conv_w, conv_b, conv_ln_g, conv_ln_b, w_conv_out, w_mix_out, norm_ffn_g, w_ffn_in, w_ffn_out, loss_target, m_w_ada, m_b_ada, m_norm_mix_g, m_w_in, m_q_norm_g, m_k_norm_g, m_attn_sinks, m_rel_bias, m_w_attn_out, m_conv_w, m_conv_b, m_conv_ln_g, m_conv_ln_b, m_w_conv_out, m_w_mix_out, m_norm_ffn_g, m_w_ffn_in, m_w_ffn_out, v_w_ada, v_b_ada, v_norm_mix_g, v_w_in, v_q_norm_g, v_k_norm_g, v_attn_sinks, v_rel_bias, v_w_attn_out, v_conv_w, v_conv_b, v_conv_ln_g, v_conv_ln_b, v_w_conv_out, v_w_mix_out, v_norm_ffn_g, v_w_ffn_in, v_w_ffn_out):
    run = InOrder()
    xi, yi, ci = _position()
    chip = 2 * xi + yi
    me = 2 * chip + ci
    chip_arr = chip.astype(jnp.int32).reshape(1)
    core_arr = ci.astype(jnp.int32).reshape(1)
    where_arr = jnp.stack([chip, ci]).astype(jnp.int32)

    xe, tgt = x[0], loss_target[0]
    t, d = xe.shape
    hd = q_norm_g.shape[-1]
    nq = attn_sinks.shape[-1]
    aw = nq * hd
    ch = conv_b.shape[-1]
    in_width = N_CHIPS * w_in.shape[-1]
    kvw = (in_width - aw - 2 * ch - 2 * d) // 2
    nkv = kvw // hd
    dff = N_CHIPS * w_ffn_out.shape[1]
    off_k, off_v, off_ca = aw, aw + kvw, aw + 2 * kvw
    off_cb, off_ga, off_gc = off_ca + ch, off_ca + 2 * ch, off_ca + 2 * ch + d
    nc_ada = w_ada.shape[-1]
    ch_loc = conv_w.shape[-1]
    nj_ffn = w_ffn_in.shape[-1]
    perm_ffn = ffn_perm(N_CHIPS)

    big = {"w_in": w_in[0], "w_attn_out": w_attn_out[0], "w_conv_out": w_conv_out[0], "w_mix_out": w_mix_out[0],
           "w_ffn_in": w_ffn_in[0], "w_ffn_out": w_ffn_out[0]}
    moments = {"w_in": (m_w_in, v_w_in), "w_attn_out": (m_w_attn_out, v_w_attn_out),
               "w_conv_out": (m_w_conv_out, v_w_conv_out), "w_mix_out": (m_w_mix_out, v_w_mix_out),
               "w_ffn_in": (m_w_ffn_in, v_w_ffn_in), "w_ffn_out": (m_w_ffn_out, v_w_ffn_out)}
    gather_groups = {"in": ["w_in"], "branch_out": ["w_attn_out", "w_conv_out"], "mix_out": ["w_mix_out"],
                     "ffn_in": ["w_ffn_in"], "ffn_out": ["w_ffn_out"]}
    grads, deltas, new_m, new_v = {}, {}, {}, {}

    def gather_cast(gname):
        bufs = []
        for n in gather_groups[gname]:
            r, ncol = big[n].shape
            bufs.append(run(cast_into_slot, big[n], chip_arr, N_CHIPS, "cast_" + n).reshape(N_CHIPS * r, ncol))
        return bufs

    def gather_ici_start(gname, bufs):
        return run(exchange_start, "gather_ici_start_" + gname, bufs, 3 * len(bufs), plan_gather_ici)

    def gather_pass_on(gname, ici):
        landed = run(exchange_wait, "gather_ici_wait_" + gname, ici, plan_gather_ici)
        return run(exchange_start, "gather_d2d_start_" + gname, landed, 3 * len(landed), plan_gather_d2d)

    def gathered(gname, d2d):
        outs = run(exchange_wait, "gather_d2d_wait_" + gname, d2d, plan_gather_d2d)
        return [o.reshape(N_CHIPS, *big[n].shape) for o, n in zip(outs, gather_groups[gname])]

    def rs_pair_start(gname, names, partials):
        blocks = [g.reshape(N_CHIPS, 2, big[n].shape[0] // 2, big[n].shape[1]) for n, g in zip(names, partials)]
        land = [lax.empty((N_CHIPS,) + b.shape[2:], BF16) for b in blocks]
        return run(exchange_start, "pair_exchange_start_" + gname, blocks + land, N_CHIPS * len(blocks),
                   plan_pair_exchange)

    def rs_chip_start(gname, names, pair):
        nw = len(names)
        outs = run(exchange_wait, "pair_exchange_wait_" + gname, pair, plan_pair_exchange)
        sums = [run(pair_sum, g, r, core_arr, "pair_sum_" + n) for n, g, r in zip(names, outs[:nw], outs[nw:])]
        land = [lax.empty(s.shape, BF16) for s in sums]
        return run(exchange_start, "chip_exchange_start_" + gname, sums + land, 3 * nw, plan_chip_exchange)

    def rs_share_start(gname, names, chipx):
        nw = len(names)
        outs = run(exchange_wait, "chip_exchange_wait_" + gname, chipx, plan_chip_exchange)
        halves = [run(chip_sum, s, r, where_arr, "chip_sum_" + n) for n, s, r in zip(names, outs[:nw], outs[nw:])]
        return run(exchange_start, "pair_share_start_" + gname, halves, nw, plan_pair_share)

    def rs_finish(gname, names, share):
        fulls = run(exchange_wait, "pair_share_wait_" + gname, share, plan_pair_share)
        for n, g2 in zip(names, fulls):
            g, dl, nm, nv = run(adamw, big[n], g2.reshape(big[n].shape), moments[n][0][0], moments[n][1][0],
                                "adamw_" + n, copy_grad=True)
            grads[n], deltas[n], new_m[n], new_v[n] = g[None], dl[None], nm[None], nv[None]

    near, far = (0, 1), (2,)
    plan_ici_near = functools.partial(plan_gather_ici, peers=near)
    plan_ici_far = functools.partial(plan_gather_ici, peers=far)
    plan_d2d_near = functools.partial(plan_gather_d2d, peers=near)
    plan_d2d_far = functools.partial(plan_gather_d2d, peers=far)
    bufs_in = gather_cast("in")
    row1, offs1 = _row_pack([c, conv_w[0].reshape(1, CONV_WIDTH * ch_loc)])
    got1 = run(allgather_small, row1, "allgather_cond")
    ici_near = run(exchange_start, "gather_ici_start_in_near", bufs_in, len(near), plan_ici_near)
    c_all = got1[:, 0, :d]
    conv_w_full = got1[0::2, 0, offs1[1]:offs1[1] + CONV_WIDTH * ch_loc].reshape(N_CHIPS, CONV_WIDTH, ch_loc)
    conv_w_full = jnp.transpose(conv_w_full, (1, 0, 2)).reshape(CONV_WIDTH, ch)
    conv_w_pad = jnp.pad(conv_w_full, ((0, 1), (0, 0)))
    c_t = jnp.transpose(c_all)
    mod_cols = run(ada_fwd, c_t, w_ada[0])
    rest_bufs = {gname: gather_cast(gname) for gname in gather_groups if gname != "in"}
    got2 = run(allgather_small, mod_cols, "allgather_mod")
    mod_all = got2.reshape(N_CHIPS, 2, N_DEV, nc_ada)[:, 0]
    mod = lax.dynamic_slice_in_dim(mod_all, me, 1, axis=1).reshape(1, N_CHIPS * nc_ada) + b_ada
    mod = jnp.pad(mod.reshape(N_MOD, d), ((0, SUBLANES - N_MOD), (0, 0)))

    h = run(pre_mix_fwd, xe, mod, norm_mix_g)
    bucket = _t5_bucket_table()
    bucket_p, bucket_c = jnp.asarray(bucket[:, :BLOCK]), jnp.asarray(bucket[:, BLOCK:])
    bias_p, bias_c = run(bias_table, rel_bias, bucket_p, bucket_c)

    def in_blocks(buf):
        return buf.reshape(N_CHIPS, *big["w_in"].shape)

    def chip_ids(peers):
        others = [2 * (1 - xi) + yi, 2 * xi + (1 - yi), 2 * (1 - xi) + (1 - yi)]
        return jnp.stack([others[j] for j in peers]).astype(jnp.int32)

    tn_in = big["w_in"].shape[1]
    landed = run(exchange_wait, "gather_ici_wait_in_near", ici_near, plan_ici_near)
    ici_far = run(exchange_start, "gather_ici_start_in_far", landed, len(far), plan_ici_far)
    ici_near_ffn = run(exchange_start, "gather_ici_start_ffn_in_near", rest_bufs["ffn_in"], len(near), plan_ici_near)
    ici = {gname: gather_ici_start(gname, rest_bufs[gname]) for gname in ("branch_out", "mix_out")}
    d2d_near = run(exchange_start, "gather_d2d_start_in_near", ici_far.bufs, len(near), plan_d2d_near)
    p = run(mm_nn_blocks, h, in_blocks(d2d_near.bufs[0]), chip_arr, None, tn=tn_in, out_dtype=BF16, name="mm_in_own")
    landed = run(exchange_wait, "gather_d2d_wait_in_near", d2d_near, plan_d2d_near)
    p = run(mm_nn_blocks, h, in_blocks(landed[0]), chip_ids(near), p, tn=tn_in, out_dtype=BF16, name="mm_in_near")
    landed = run(exchange_wait, "gather_ici_wait_in_far", ici_far, plan_ici_far, bufs=landed)
    d2d_far = run(exchange_start, "gather_d2d_start_in_far", landed, len(far), plan_d2d_far)
    landed = run(exchange_wait, "gather_d2d_wait_in_far", d2d_far, plan_d2d_far)
    wg_in = in_blocks(landed[0])
    p = run(mm_nn_blocks, h, wg_in, chip_ids(far), p, tn=tn_in, out_dtype=BF16, name="mm_in_far")
    d2d_branch = gather_pass_on("branch_out", ici["branch_out"])

    sinks3 = attn_sinks.reshape(nq, 1, 1)
    attn_o = run(attn_fwd, p, bias_p, bias_c, sinks3, q_norm_g, k_norm_g, aw=aw, kvw=kvw)
    ca, cb = p[:, off_ca:off_cb], p[:, off_cb:off_ga]
    s_conv, co_conv = run(conv_fwd, ca, cb, conv_w_pad, conv_b, conv_ln_g, conv_ln_b)
    wg_attn_out, wg_conv_out = gathered("branch_out", d2d_branch)
    y_attn = run(mm_nn, attn_o, wg_attn_out, tn=_tile(wg_attn_out.shape[2], 512), tk=aw, out_dtype=BF16,
                 name="mm_attn_out")
    y_conv = run(mm_nn, s_conv, wg_conv_out, tn=_tile(wg_conv_out.shape[2], 512), tk=ch, out_dtype=BF16,
                 name="mm_conv_out")
    landed = run(exchange_wait, "gather_ici_wait_ffn_in_near", ici_near_ffn, plan_ici_near)
    ici_far_ffn = run(exchange_start, "gather_ici_start_ffn_in_far", landed, len(far), plan_ici_far)
    ici_near_out = run(exchange_start, "gather_ici_start_ffn_out_near", rest_bufs["ffn_out"], len(near),
                       plan_ici_near)
    d2d_near_ffn = run(exchange_start, "gather_d2d_start_ffn_in_near", ici_far_ffn.bufs, len(near), plan_d2d_near)
    merged = run(merge_fwd, p, y_attn, y_conv, off_ga, off_gc)
    d2d_mix = gather_pass_on("mix_out", ici["mix_out"])
    (wg_mix_out,) = gathered("mix_out", d2d_mix)
    wg_mix_out = wg_mix_out.reshape(1, d, d)
    o_m = run(mm_nn, merged, wg_mix_out, tn=_tile(d, 512), tk=d, out_dtype=F32, name="mm_mix_out")
    x1, h2 = run(pre_ffn_fwd, xe, o_m, mod, norm_ffn_g)

    def ffn_blocks(buf):
        return buf.reshape(N_CHIPS, *big["w_ffn_in"].shape)

    tn_ffn = _tile(nj_ffn, 1408)
    landed = run(exchange_wait, "gather_d2d_wait_ffn_in_near", d2d_near_ffn, plan_d2d_near)
    f = run(mm_nn_blocks, h2, ffn_blocks(landed[0]), chip_arr, None, tn=tn_ffn, out_dtype=BF16, perm=perm_ffn,
            name="mm_ffn_in_own")
    landed = run(exchange_wait, "gather_ici_wait_ffn_in_far", ici_far_ffn, plan_ici_far, bufs=landed)
    d2d_far_ffn = run(exchange_start, "gather_d2d_start_ffn_in_far", landed, len(far), plan_d2d_far)
    f = run(mm_nn_blocks, h2, ffn_blocks(d2d_far_ffn.bufs[0]), chip_ids(near), f, tn=tn_ffn, out_dtype=BF16,
            perm=perm_ffn, name="mm_ffn_in_near")
    landed_o = run(exchange_wait, "gather_ici_wait_ffn_out_near", ici_near_out, plan_ici_near)
    ici_far_out = run(exchange_start, "gather_ici_start_ffn_out_far", landed_o, len(far), plan_ici_far)
    d2d_near_out = run(exchange_start, "gather_d2d_start_ffn_out_near", ici_far_out.bufs, len(near), plan_d2d_near)
    landed = run(exchange_wait, "gather_d2d_wait_ffn_in_far", d2d_far_ffn, plan_d2d_far)
    wg_ffn_in = ffn_blocks(landed[0])
    f = run(mm_nn_blocks, h2, wg_ffn_in, chip_ids(far), f, tn=tn_ffn, out_dtype=BF16, perm=perm_ffn,
            name="mm_ffn_in_far")
    act = run(swiglu_fwd, f, nj_ffn)

    def out_blocks(buf):
        return buf.reshape(N_CHIPS, *big["w_ffn_out"].shape)

    tn_out = _tile(d, 512)
    landed_o = run(exchange_wait, "gather_d2d_wait_ffn_out_near", d2d_near_out, plan_d2d_near)
    o_f = run(mm_nn_rowblocks, act, out_blocks(landed_o[0]), chip_arr, None, tn=tn_out, name="mm_ffn_out_own")
    landed_o = run(exchange_wait, "gather_ici_wait_ffn_out_far", ici_far_out, plan_ici_far, bufs=landed_o)
    d2d_far_out = run(exchange_start, "gather_d2d_start_ffn_out_far", landed_o, len(far), plan_d2d_far)
    o_f = run(mm_nn_rowblocks, act, out_blocks(d2d_far_out.bufs[0]), chip_ids(near), o_f, tn=tn_out,
              name="mm_ffn_out_near")
    landed_o = run(exchange_wait, "gather_d2d_wait_ffn_out_far", d2d_far_out, plan_d2d_far)
    wg_ffn_out = landed_o[0].reshape(1, dff, d)
    o_f = run(mm_nn_rowblocks, act, out_blocks(landed_o[0]), chip_ids(far), o_f, tn=tn_out, name="mm_ffn_out_far")
    loss11, dy, dof, acc_l = run(loss_head, x1, o_f, tgt, mod)

    gw_ffn_out = run(mm_tn, act, dof, 1, tk=_tile(dff, 512), tn=d, name="mm_ffn_out_dw")
    px_ffn_out = rs_pair_start("ffn_out", ["w_ffn_out"], [gw_ffn_out])
    dact = run(mm_nt, dof, wg_ffn_out, tko=_tile(dff, 512), tn=d, out_dtype=BF16, name="mm_ffn_out_dx")
    cx_ffn_out = rs_chip_start("ffn_out", ["w_ffn_out"], px_ffn_out)
    df = run(swiglu_bwd, f, dact, nj_ffn)
    gw_ffn_in = run(mm_tn, h2, df, N_CHIPS, tk=d, tn=_tile(nj_ffn, 1408), name="mm_ffn_in_dw",
                    perm=perm_ffn)
    px_ffn_in = rs_pair_start("ffn_in", ["w_ffn_in"], [gw_ffn_in])
    dh2 = run(mm_nt, df, wg_ffn_in, tko=_tile(d, 512), tn=nj_ffn, name="mm_ffn_in_dx", perm=perm_ffn)
    sh_ffn_out = rs_share_start("ffn_out", ["w_ffn_out"], cx_ffn_out)
    cx_ffn_in = rs_chip_start("ffn_in", ["w_ffn_in"], px_ffn_in)
    dx1, dom, acc_f = run(pre_ffn_bwd, x1, dh2, dy, o_m, mod, norm_ffn_g)
    gw_mix_out = run(mm_tn, merged, dom, 1, tk=d, tn=_tile(d, 1024), name="mm_mix_out_dw")
    px_mix = rs_pair_start("mix_out", ["w_mix_out"], [gw_mix_out])
    dmerged = run(mm_nt, dom, wg_mix_out, tko=_tile(d, 512), tn=d, out_dtype=BF16, name="mm_mix_out_dx")
    dy_attn, dy_conv, dga, dgc = run(merge_bwd, p, y_attn, y_conv, dmerged, off_ga, off_gc)
    rs_finish("ffn_out", ["w_ffn_out"], sh_ffn_out)
    cx_mix = rs_chip_start("mix_out", ["w_mix_out"], px_mix)
    gw_attn_out = run(mm_tn, attn_o, dy_attn, N_CHIPS, tk=aw, tn=_tile(wg_attn_out.shape[2], 512),
                      name="mm_attn_out_dw")
    gw_conv_out = run(mm_tn, s_conv, dy_conv, N_CHIPS, tk=ch, tn=_tile(wg_conv_out.shape[2], 512),
                      name="mm_conv_out_dw")
    ac_names = ["w_attn_out", "w_conv_out"]
    px_ac = rs_pair_start("attn_conv_out", ac_names, [gw_attn_out, gw_conv_out])
    dattn_o = run(mm_nt, dy_attn, wg_attn_out, tko=_tile(aw, 1024), tn=_tile(wg_attn_out.shape[2], 512),
                  out_dtype=BF16, name="mm_attn_out_dx")
    ds_conv = run(mm_nt, dy_conv, wg_conv_out, tko=_tile(ch, 1024), tn=_tile(wg_conv_out.shape[2], 512),
                  out_dtype=BF16, name="mm_conv_out_dx")
    cx_ac = rs_chip_start("attn_conv_out", ac_names, px_ac)
    dca, dcb, dconv_w, dconv_vec = run(conv_bwd, ca, cb, co_conv, ds_conv, conv_w_pad, conv_ln_g, conv_ln_b)
    sh_ffn_in = rs_share_start("ffn_in", ["w_ffn_in"], cx_ffn_in)
    dqkv, dbp, dbc, dsinks, dqg, dkg = run(attn_bwd, p, bias_p, bias_c, sinks3, q_norm_g, k_norm_g, dattn_o,
                                           aw=aw, kvw=kvw)
    sh_mix = rs_share_start("mix_out", ["w_mix_out"], cx_mix)
    sh_ac = rs_share_start("attn_conv_out", ac_names, cx_ac)
    drel = run(bias_table_bwd, dbp, dbc, bucket_p, bucket_c).reshape(NUM_BUCKETS, nq)
    dp = jnp.concatenate([dqkv, dca, dcb, dga, dgc], axis=1)
    gw_in = run(mm_tn, h, dp, N_CHIPS, tk=d, tn=wg_in.shape[2], name="mm_in_dw")
    px_in = rs_pair_start("in", ["w_in"], [gw_in])
    dh = run(mm_nt, dp, wg_in, tko=_tile(d, 1024), tn=wg_in.shape[2], name="mm_in_dx")
    grad_x, acc_m = run(pre_mix_bwd, xe, dh, dx1, mod, norm_mix_g)

    dmod = jnp.concatenate([acc_m[0:1], acc_m[1:2], acc_f[3:4], acc_f[0:1], acc_f[1:2], acc_l[0:1]], axis=1)
    small_names = ["b_ada", "norm_mix_g", "q_norm_g", "k_norm_g", "attn_sinks", "rel_bias", "conv_b", "conv_ln_g",
                   "conv_ln_b", "norm_ffn_g"]
    small_w = [b_ada, norm_mix_g, q_norm_g, k_norm_g, attn_sinks, rel_bias, conv_b, conv_ln_g, conv_ln_b, norm_ffn_g]
    small_m = [m_b_ada, m_norm_mix_g, m_q_norm_g, m_k_norm_g, m_attn_sinks, m_rel_bias, m_conv_b, m_conv_ln_g,
               m_conv_ln_b, m_norm_ffn_g]
    small_v = [v_b_ada, v_norm_mix_g, v_q_norm_g, v_k_norm_g, v_attn_sinks, v_rel_bias, v_conv_b, v_conv_ln_g,
               v_conv_ln_b, v_norm_ffn_g]
    small_g = [dmod, acc_m[2:3], dqg, dkg, dsinks.reshape(1, nq), drel.reshape(1, NUM_BUCKETS * nq),
               dconv_vec[0:1], dconv_vec[1:2], dconv_vec[2:3], acc_f[2:3]]
    row3, offs3 = _row_pack(small_g + [dconv_w[:CONV_WIDTH].reshape(1, CONV_WIDTH * ch), loss11])
    got3 = run(allgather_small, row3, "allgather_small_grads")
    cx_in = rs_chip_start("in", ["w_in"], px_in)
    as_row = lambda a: a.reshape(1, -1)
    outs3 = run(small_sum_adamw, got3, offs3, [as_row(a) for a in small_w], [as_row(a) for a in small_m],
                [as_row(a) for a in small_v], [CONV_WIDTH * ch, 1])
    for i, (n, w) in enumerate(zip(small_names, small_w)):
        grads[n], deltas[n], new_m[n], new_v[n] = (o.reshape(w.shape) for o in outs3[4 * i:4 * i + 4])
    g_conv_w_all, loss_sum = outs3[-2].reshape(CONV_WIDTH, ch), outs3[-1]

    g_conv_w = lax.dynamic_slice_in_dim(g_conv_w_all, chip * ch_loc, ch_loc, axis=1)
    grads["conv_w"] = g_conv_w[None]
    dl, nm, nv = run(adamw, conv_w[0], g_conv_w, m_conv_w[0], v_conv_w[0], "adamw_conv_w")
    deltas["conv_w"], new_m["conv_w"], new_v["conv_w"] = dl[None], nm[None], nv[None]

    dmod_all = got3[:, 0, :N_MOD * d]
    dmod_cols = lax.dynamic_slice_in_dim(dmod_all, chip * nc_ada, nc_ada, axis=1)
    g_ada, dl, nm, nv = run(ada_grad_adamw, c_t, dmod_cols, w_ada[0], m_w_ada[0], v_w_ada[0])
    grads["w_ada"], deltas["w_ada"], new_m["w_ada"], new_v["w_ada"] = g_ada[None], dl[None], nm[None], nv[None]

    rs_finish("ffn_in", ["w_ffn_in"], sh_ffn_in)
    rs_finish("mix_out", ["w_mix_out"], sh_mix)
    rs_finish("attn_conv_out", ac_names, sh_ac)
    sh_in = rs_share_start("in", ["w_in"], cx_in)
    rs_finish("in", ["w_in"], sh_in)

    loss = loss_sum[0, 0]
    order = ["w_ada", "b_ada", "norm_mix_g", "w_in", "q_norm_g", "k_norm_g", "attn_sinks", "rel_bias", "w_attn_out",
             "conv_w", "conv_b", "conv_ln_g", "conv_ln_b", "w_conv_out", "w_mix_out", "norm_ffn_g", "w_ffn_in",
             "w_ffn_out"]
    return (loss, grad_x[None], *[grads[n] for n in order], *[deltas[n] for n in order],
            *[new_m[n] for n in order], *[new_v[n] for n in order])
```

```python
import functools
import math
from typing import Any, NamedTuple

import jax
import jax.numpy as jnp
import numpy as np
from jax import lax
from jax.experimental import pallas as pl
from jax.experimental.pallas import tpu as pltpu

F32 = jnp.float32
BF16 = jnp.bfloat16
MESH = pl.DeviceIdType.MESH

V7X_VMEM_BYTES = 64 * 1024 * 1024
VMEM_LIMIT = V7X_VMEM_BYTES - 8 * 1024 * 1024
LANES = 128
SUBLANES = 8
BF16_SUBLANES = 16

EPS = 1e-6
WINDOW = 128
BLOCK = 128
NUM_BUCKETS = 32
MAX_EXACT = NUM_BUCKETS // 2
MAX_DISTANCE = 128
CONV_WIDTH = 31
CONV_HALO = 32
ADAM_LR = 0.001
ADAM_B1 = 0.9
ADAM_B2 = 0.999
ADAM_EPS = 1e-08
ADAM_WD = 0.01
ADAM_STEP = 10
N_MOD = 6
SH_M, SC_M, GT_M, SH_F, SC_F, GT_F = range(6)

N_CHIPS = 4
N_DEV = 8

_ANY = pl.BlockSpec(memory_space=pl.ANY)
_VMEM = pl.BlockSpec(memory_space=pltpu.VMEM)
_SMEM = pl.BlockSpec(memory_space=pltpu.SMEM)
_HBM = pl.BlockSpec(memory_space=pltpu.HBM)
_SEM = pl.BlockSpec(memory_space=pltpu.SEMAPHORE)
_EFFECT = pltpu.SideEffectType.DATAFLOW_SIDE_EFFECTING


class InOrder:
    def __init__(self):
        self.token = None

    def __call__(self, fn, *args, **kw):
        return fn(*args, dep=self, **kw)


def _pallas(body, args, *, in_specs, out_specs, out_shape, name, dep=None, grid=(), n_prefetch=0, scratch=(),
            sem=None, **kw):
    n_lead = n_prefetch + len(in_specs)
    in_specs, args = list(in_specs), list(args)
    single = not isinstance(out_shape, (list, tuple))
    out_shapes = [out_shape] if single else list(out_shape)
    out_specs = [out_specs] if single else list(out_specs)
    if dep is not None:
        inner, n_out, takes = body, len(out_shapes), dep.token is not None

        def body(*refs):
            rest = refs[n_lead + (1 if takes else 0):]
            rest[n_out][...] = jnp.zeros((SUBLANES, LANES), F32)
            return inner(*refs[:n_lead], *rest[:n_out], *rest[n_out + 1:])

        if takes:
            in_specs.append(_ANY)
            args.append(dep.token)
        out_shapes.append(jax.ShapeDtypeStruct((SUBLANES, LANES), F32))
        out_specs.append(pl.BlockSpec((SUBLANES, LANES), lambda *_: (0, 0)))
    params = kw.pop("compiler_params", None)
    if params is None:
        params = pltpu.CompilerParams(dimension_semantics=sem, vmem_limit_bytes=VMEM_LIMIT)
    outs = pl.pallas_call(
        body,
        grid_spec=pltpu.PrefetchScalarGridSpec(num_scalar_prefetch=n_prefetch, grid=grid, in_specs=in_specs,
                                               out_specs=out_specs, scratch_shapes=list(scratch)),
        out_shape=out_shapes, compiler_params=params, name=name, **kw,
    )(*args)
    if dep is not None:
        dep.token = outs[-1]
        outs = outs[:-1]
    return outs[0] if single else list(outs)


def _tile(n, pref, unit=LANES):
    best = None
    for t in range(unit, min(n, pref) + 1, unit):
        if n % t == 0:
            best = t
    return best if best is not None else n


def _sigmoid(v):
    return 1.0 / (1.0 + jnp.exp(-v.astype(F32)))


ROW_CHUNK = 512


def _row_chunks(m, unit=SUBLANES):
    step = _tile(m, ROW_CHUNK, unit)
    return [(s, step) for s in range(0, m, step)]


def _ew_tiles(r, n, unit=SUBLANES, elems=512 * 1024):
    return _tile(r, max(unit, elems // n), unit), n


def _block_pos(j, perm):
    if perm is None:
        return j
    pos = 0
    for a, p in enumerate(perm):
        pos = pos + jnp.where(j == a, p, 0)
    return pos


def mm_nn(a, w, *, tn, tk, out_dtype, name, perm=None, dep=None):
    m, k = a.shape
    j, k2, nj = w.shape
    assert k == k2 and nj % tn == 0 and k % tk == 0
    npj, nk = nj // tn, k // tk

    def body(a_ref, w_ref, o_ref, *scratch):
        kk = pl.program_id(1)
        for s, sz in _row_chunks(m):
            rows = pl.ds(s, sz)
            p = jnp.dot(a_ref[rows, :], w_ref[...], preferred_element_type=F32)
            if nk == 1:
                o_ref[rows, :] = p.astype(out_dtype)
            else:
                acc = scratch[0]

                @pl.when(kk == 0)
                def _():
                    acc[rows, :] = p

                @pl.when(kk > 0)
                def _():
                    acc[rows, :] += p

                @pl.when(kk == nk - 1)
                def _():
                    o_ref[rows, :] = acc[rows, :].astype(out_dtype)

    return _pallas(
        body, [a, w], dep=dep, grid=(j * npj, nk),
        in_specs=[
            pl.BlockSpec((m, tk), lambda n, kk: (0, kk)),
            pl.BlockSpec((None, tk, tn), lambda n, kk: (n // npj, kk, n % npj)),
        ],
        out_specs=pl.BlockSpec((m, tn), lambda n, kk: (0, _block_pos(n // npj, perm) * npj + n % npj)),
        out_shape=jax.ShapeDtypeStruct((m, j * nj), out_dtype),
        scratch=[pltpu.VMEM((m, tn), F32)] if nk > 1 else [],
        sem=("parallel", "arbitrary"), name=name)


def mm_nn_blocks(a, w, blocks, into, *, tn, out_dtype, name, perm=None, dep=None):
    m, k = a.shape
    j, k2, nj = w.shape
    assert k == k2 and nj % tn == 0
    npj = nj // tn
    n_in = 2 if into is None else 3

    def body(blocks_ref, a_ref, w_ref, *rest):
        o_ref = rest[n_in - 2]
        for s, sz in _row_chunks(m):
            rows = pl.ds(s, sz)
            o_ref[rows, :] = jnp.dot(a_ref[rows, :], w_ref[...], preferred_element_type=F32).astype(out_dtype)

    return _pallas(
        body, [blocks, a, w] + ([] if into is None else [into]), dep=dep, n_prefetch=1,
        grid=(blocks.shape[0] * npj,),
        in_specs=[pl.BlockSpec((m, k), lambda n, bl: (0, 0)),
                  pl.BlockSpec((None, k, tn), lambda n, bl: (bl[n // npj], 0, n % npj))]
        + ([] if into is None else [_ANY]),
        out_specs=pl.BlockSpec((m, tn), lambda n, bl: (0, _block_pos(bl[n // npj], perm) * npj + n % npj)),
        out_shape=jax.ShapeDtypeStruct((m, j * nj), out_dtype),
        input_output_aliases={} if into is None else {3: 0},
        sem=("arbitrary",), name=name)


def mm_nn_rowblocks(a, w, blocks, into, *, tn, name, dep=None):
    m, k = a.shape
    j, kj, n = w.shape
    assert k == j * kj and n % tn == 0
    nb = blocks.shape[0]
    n_in = 2 if into is None else 3

    def body(blocks_ref, a_ref, w_ref, *rest):
        o_ref = rest[n_in - 2]
        b = pl.program_id(1)
        for s, sz in _row_chunks(m):
            rows = pl.ds(s, sz)
            part = jnp.dot(a_ref[rows, :], w_ref[...], preferred_element_type=F32)

            @pl.when(b == 0)
            def _():
                o_ref[rows, :] = part if into is None else rest[0][rows, :] + part

            @pl.when(b > 0)
            def _():
                o_ref[rows, :] += part

    acc = pl.BlockSpec((m, tn), lambda i, b, bl: (0, i))
    return _pallas(
        body, [blocks, a, w] + ([] if into is None else [into]), dep=dep, n_prefetch=1, grid=(n // tn, nb),
        in_specs=[pl.BlockSpec((m, kj), lambda i, b, bl: (0, bl[b])),
                  pl.BlockSpec((None, kj, tn), lambda i, b, bl: (bl[b], 0, i))]
        + ([] if into is None else [acc]),
        out_specs=acc,
        out_shape=jax.ShapeDtypeStruct((m, n), F32),
        input_output_aliases={} if into is None else {3: 0},
        sem=("parallel", "arbitrary"), name=name)


def mm_nt(g, w, *, tko, tn, name, out_dtype=F32, perm=None, dep=None):
    m, n = g.shape
    j, k, nj = w.shape
    assert n == j * nj and nj % tn == 0 and k % tko == 0
    npj, nr = nj // tn, n // tn
    in_place = out_dtype == F32

    def body(g_ref, w_ref, o_ref, *scratch):
        r = pl.program_id(1)
        acc = o_ref if in_place else (scratch[0] if nr > 1 else None)
        for s, sz in _row_chunks(m):
            rows = pl.ds(s, sz)
            p = lax.dot_general(g_ref[rows, :], w_ref[...], (((1,), (1,)), ((), ())), preferred_element_type=F32)
            if acc is None:
                o_ref[rows, :] = p.astype(out_dtype)
                continue

            @pl.when(r == 0)
            def _():
                acc[rows, :] = p

            @pl.when(r > 0)
            def _():
                acc[rows, :] += p

            if not in_place:
                @pl.when(r == nr - 1)
                def _():
                    o_ref[rows, :] = acc[rows, :].astype(out_dtype)

    return _pallas(
        body, [g, w], dep=dep, grid=(k // tko, nr),
        in_specs=[
            pl.BlockSpec((m, tn), lambda ko, r: (0, _block_pos(r // npj, perm) * npj + r % npj)),
            pl.BlockSpec((None, tko, tn), lambda ko, r: (r // npj, ko, r % npj)),
        ],
        out_specs=pl.BlockSpec((m, tko), lambda ko, r: (0, ko)),
        out_shape=jax.ShapeDtypeStruct((m, k), out_dtype),
        scratch=[pltpu.VMEM((m, tko), F32)] if (nr > 1 and not in_place) else [],
        sem=("parallel", "arbitrary"), name=name)


def mm_tn(a, g, n_blocks, *, tk, tn, name, perm=None, dep=None):
    m, k = a.shape
    m2, n = g.shape
    nj = n // n_blocks
    assert m == m2 and nj % tn == 0 and k % tk == 0
    npj = nj // tn

    def body(a_ref, g_ref, o_ref):
        for s, sz in _row_chunks(tk, LANES):
            p = lax.dot_general(a_ref[:, pl.ds(s, sz)], g_ref[...], (((0,), (0,)), ((), ())),
                                preferred_element_type=F32)
            o_ref[pl.ds(s, sz), :] = p.astype(BF16)

    return _pallas(
        body, [a, g], dep=dep, grid=(k // tk, n // tn),
        in_specs=[
            pl.BlockSpec((m, tk), lambda kk, nn: (0, kk)),
            pl.BlockSpec((m, tn), lambda kk, nn: (0, _block_pos(nn // npj, perm) * npj + nn % npj)),
        ],
        out_specs=pl.BlockSpec((None, tk, tn), lambda kk, nn: (nn // npj, kk, nn % npj)),
        out_shape=jax.ShapeDtypeStruct((n_blocks, k, nj), BF16),
        sem=("parallel", "parallel"), name=name)


ROW_TILE = 256


def _row_spec(tr, width):
    return pl.BlockSpec((tr, width), lambda i: (i, 0))


def _full_spec(shape):
    return pl.BlockSpec(shape, lambda *_: (0,) * len(shape))


def _rms(xv):
    return lax.rsqrt(jnp.mean(xv * xv, axis=-1, keepdims=True) + EPS)


def _mod_row(mod_ref, row):
    return mod_ref[pl.ds(row, 1), :]


def pre_mix_fwd(x, mod, gain, dep=None):
    t, d = x.shape
    tr = _tile(t, ROW_TILE, SUBLANES)

    def body(x_ref, mod_ref, g_ref, h_ref):
        xv = x_ref[...]
        y = xv * _rms(xv) * g_ref[...]
        h_ref[...] = (y * (1.0 + _mod_row(mod_ref, SC_M)) + _mod_row(mod_ref, SH_M)).astype(BF16)

    return _pallas(
        body, [x, mod, gain], dep=dep, grid=(t // tr,),
        in_specs=[_row_spec(tr, d), _full_spec(mod.shape), _full_spec(gain.shape)],
        out_specs=_row_spec(tr, d),
        out_shape=jax.ShapeDtypeStruct((t, d), BF16),
        sem=("parallel",), name="pre_mix_fwd")


def pre_ffn_fwd(x, o_m, mod, gain, dep=None):
    t, d = x.shape
    tr = _tile(t, ROW_TILE, SUBLANES)

    def body(x_ref, om_ref, mod_ref, g_ref, x1_ref, h_ref):
        x1 = x_ref[...] + _mod_row(mod_ref, GT_M) * om_ref[...]
        x1_ref[...] = x1
        y = x1 * _rms(x1) * g_ref[...]
        h_ref[...] = (y * (1.0 + _mod_row(mod_ref, SC_F)) + _mod_row(mod_ref, SH_F)).astype(BF16)

    return _pallas(
        body, [x, o_m, mod, gain], dep=dep, grid=(t // tr,),
        in_specs=[_row_spec(tr, d), _row_spec(tr, d), _full_spec(mod.shape), _full_spec(gain.shape)],
        out_specs=[_row_spec(tr, d), _row_spec(tr, d)],
        out_shape=[jax.ShapeDtypeStruct((t, d), F32), jax.ShapeDtypeStruct((t, d), BF16)],
        sem=("parallel",), name="pre_ffn_fwd")


def loss_head(x1, o_f, target, mod, dep=None):
    t, d = x1.shape
    tr = _tile(t, ROW_TILE, SUBLANES)

    def body(x1_ref, of_ref, tg_ref, mod_ref, loss_ref, dy_ref, dof_ref, acc_ref):
        i = pl.program_id(0)
        gt = _mod_row(mod_ref, GT_F)
        of = of_ref[...]
        err = x1_ref[...] + gt * of - tg_ref[...]
        dy = err * (1.0 / d)
        dy_ref[...] = dy
        dof_ref[...] = (dy * gt).astype(BF16)
        part = (0.5 / d) * jnp.sum(jnp.sum(err * err, axis=1, keepdims=True), axis=0, keepdims=True)
        dgt = jnp.sum(dy * of, axis=0, keepdims=True)

        @pl.when(i == 0)
        def _():
            loss_ref[...] = jnp.zeros_like(loss_ref)
            acc_ref[...] = jnp.zeros_like(acc_ref)

        loss_ref[...] += part
        acc_ref[pl.ds(0, 1), :] += dgt

    return _pallas(
        body, [x1, o_f, target, mod], dep=dep, grid=(t // tr,),
        in_specs=[_row_spec(tr, d), _row_spec(tr, d), _row_spec(tr, d), _full_spec(mod.shape)],
        out_specs=[_full_spec((1, 1)), _row_spec(tr, d), _row_spec(tr, d), _full_spec((SUBLANES, d))],
        out_shape=[jax.ShapeDtypeStruct((1, 1), F32), jax.ShapeDtypeStruct((t, d), F32),
                   jax.ShapeDtypeStruct((t, d), BF16), jax.ShapeDtypeStruct((SUBLANES, d), F32)],
        sem=("arbitrary",), name="loss_head")


def _norm_bwd(xv, dh, sc, gain):
    rstd = _rms(xv)
    yn = xv * rstd
    dsh = jnp.sum(dh, axis=0, keepdims=True)
    dsc = jnp.sum(dh * (yn * gain), axis=0, keepdims=True)
    dgain = jnp.sum(dh * (1.0 + sc) * yn, axis=0, keepdims=True)
    dyn = dh * ((1.0 + sc) * gain)
    dx = rstd * (dyn - yn * jnp.mean(dyn * yn, axis=-1, keepdims=True))
    return dx, dsh, dsc, dgain


def pre_ffn_bwd(x1, dh2, dy, o_m, mod, gain, dep=None):
    t, d = x1.shape
    tr = _tile(t, ROW_TILE, SUBLANES)

    def body(x1_ref, dh_ref, dy_ref, om_ref, mod_ref, g_ref, dx1_ref, dom_ref, acc_ref):
        i = pl.program_id(0)
        dxn, dsh, dsc, dgain = _norm_bwd(x1_ref[...], dh_ref[...], _mod_row(mod_ref, SC_F), g_ref[...])
        dx1 = dy_ref[...] + dxn
        dx1_ref[...] = dx1
        dom_ref[...] = (dx1 * _mod_row(mod_ref, GT_M)).astype(BF16)
        dgt = jnp.sum(dx1 * om_ref[...], axis=0, keepdims=True)

        @pl.when(i == 0)
        def _():
            acc_ref[...] = jnp.zeros_like(acc_ref)

        acc_ref[pl.ds(0, 1), :] += dsh
        acc_ref[pl.ds(1, 1), :] += dsc
        acc_ref[pl.ds(2, 1), :] += dgain
        acc_ref[pl.ds(3, 1), :] += dgt

    return _pallas(
        body, [x1, dh2, dy, o_m, mod, gain], dep=dep, grid=(t // tr,),
        in_specs=[_row_spec(tr, d)] * 4 + [_full_spec(mod.shape), _full_spec(gain.shape)],
        out_specs=[_row_spec(tr, d), _row_spec(tr, d), _full_spec((SUBLANES, d))],
        out_shape=[jax.ShapeDtypeStruct((t, d), F32), jax.ShapeDtypeStruct((t, d), BF16),
                   jax.ShapeDtypeStruct((SUBLANES, d), F32)],
        sem=("arbitrary",), name="pre_ffn_bwd")


def pre_mix_bwd(x, dh, dx1, mod, gain, dep=None):
    t, d = x.shape
    tr = _tile(t, ROW_TILE, SUBLANES)

    def body(x_ref, dh_ref, dx1_ref, mod_ref, g_ref, gx_ref, acc_ref):
        i = pl.program_id(0)
        dxn, dsh, dsc, dgain = _norm_bwd(x_ref[...], dh_ref[...], _mod_row(mod_ref, SC_M), g_ref[...])
        gx_ref[...] = dx1_ref[...] + dxn

        @pl.when(i == 0)
        def _():
            acc_ref[...] = jnp.zeros_like(acc_ref)

        acc_ref[pl.ds(0, 1), :] += dsh
        acc_ref[pl.ds(1, 1), :] += dsc
        acc_ref[pl.ds(2, 1), :] += dgain

    return _pallas(
        body, [x, dh, dx1, mod, gain], dep=dep, grid=(t // tr,),
        in_specs=[_row_spec(tr, d)] * 3 + [_full_spec(mod.shape), _full_spec(gain.shape)],
        out_specs=[_row_spec(tr, d), _full_spec((SUBLANES, d))],
        out_shape=[jax.ShapeDtypeStruct((t, d), F32), jax.ShapeDtypeStruct((SUBLANES, d), F32)],
        sem=("arbitrary",), name="pre_mix_bwd")


def merge_fwd(p, y_attn, y_conv, off_ga, off_gc, dep=None):
    t, d = y_attn.shape
    tr = _tile(t, ROW_TILE, SUBLANES)
    cw = math.gcd(math.gcd(off_ga, off_gc), math.gcd(d, 512))
    nc = d // cw

    def body(ga_ref, gc_ref, ya_ref, yc_ref, o_ref):
        o_ref[...] = (_sigmoid(ga_ref[...]) * ya_ref[...] + _sigmoid(gc_ref[...]) * yc_ref[...]).astype(BF16)

    return _pallas(
        body, [p, p, y_attn, y_conv], dep=dep, grid=(t // tr, nc),
        in_specs=[pl.BlockSpec((tr, cw), lambda i, j: (i, off_ga // cw + j)),
                  pl.BlockSpec((tr, cw), lambda i, j: (i, off_gc // cw + j)),
                  pl.BlockSpec((tr, cw), lambda i, j: (i, j)),
                  pl.BlockSpec((tr, cw), lambda i, j: (i, j))],
        out_specs=pl.BlockSpec((tr, cw), lambda i, j: (i, j)),
        out_shape=jax.ShapeDtypeStruct((t, d), BF16),
        sem=("parallel", "parallel"), name="merge_fwd")


def merge_bwd(p, y_attn, y_conv, dmerged, off_ga, off_gc, dep=None):
    t, d = y_attn.shape
    tr = _tile(t, ROW_TILE, SUBLANES)
    cw = math.gcd(math.gcd(off_ga, off_gc), math.gcd(d, 512))
    nc = d // cw

    def body(ga_ref, gc_ref, ya_ref, yc_ref, dm_ref, dya_ref, dyc_ref, dga_ref, dgc_ref):
        dm = dm_ref[...].astype(F32)
        sa = _sigmoid(ga_ref[...])
        sc = _sigmoid(gc_ref[...])
        dya_ref[...] = (dm * sa).astype(BF16)
        dyc_ref[...] = (dm * sc).astype(BF16)
        dga_ref[...] = (dm * ya_ref[...] * sa * (1.0 - sa)).astype(BF16)
        dgc_ref[...] = (dm * yc_ref[...] * sc * (1.0 - sc)).astype(BF16)

    blk = pl.BlockSpec((tr, cw), lambda i, j: (i, j))
    return _pallas(
        body, [p, p, y_attn, y_conv, dmerged], dep=dep, grid=(t // tr, nc),
        in_specs=[pl.BlockSpec((tr, cw), lambda i, j: (i, off_ga // cw + j)),
                  pl.BlockSpec((tr, cw), lambda i, j: (i, off_gc // cw + j)), blk, blk, blk],
        out_specs=[blk] * 4,
        out_shape=[jax.ShapeDtypeStruct((t, d), BF16)] * 4,
        sem=("parallel", "parallel"), name="merge_bwd")


def ffn_perm(n_blocks):
    half = n_blocks // 2
    return tuple(2 * j if j < half else 2 * (j - half) + 1 for j in range(n_blocks))


def swiglu_fwd(f, nj, dep=None):
    t, two = f.shape
    tr = _tile(t, ROW_TILE, SUBLANES)
    npair = two // (2 * nj)

    def body(f_ref, o_ref):
        g = f_ref[:, :nj].astype(F32)
        u = f_ref[:, nj:].astype(F32)
        o_ref[...] = (g * _sigmoid(g) * u).astype(BF16)

    return _pallas(
        body, [f], dep=dep, grid=(t // tr, npair),
        in_specs=[pl.BlockSpec((tr, 2 * nj), lambda i, j: (i, j))],
        out_specs=pl.BlockSpec((tr, nj), lambda i, j: (i, j)),
        out_shape=jax.ShapeDtypeStruct((t, two // 2), BF16),
        sem=("parallel", "parallel"), name="swiglu_fwd")


def swiglu_bwd(f, dact, nj, dep=None):
    t, two = f.shape
    tr = _tile(t, ROW_TILE, SUBLANES)
    npair = two // (2 * nj)

    def body(f_ref, da_ref, o_ref):
        g = f_ref[:, :nj].astype(F32)
        u = f_ref[:, nj:].astype(F32)
        da = da_ref[...]
        s = _sigmoid(g)
        o_ref[:, :nj] = (da * u * (s * (1.0 + g * (1.0 - s)))).astype(BF16)
        o_ref[:, nj:] = (da * (g * s)).astype(BF16)

    return _pallas(
        body, [f, dact], dep=dep, grid=(t // tr, npair),
        in_specs=[pl.BlockSpec((tr, 2 * nj), lambda i, j: (i, j)), pl.BlockSpec((tr, nj), lambda i, j: (i, j))],
        out_specs=pl.BlockSpec((tr, 2 * nj), lambda i, j: (i, j)),
        out_shape=jax.ShapeDtypeStruct((t, two), BF16),
        sem=("parallel", "parallel"), name="swiglu_bwd")


def _t5_bucket_table():
    q_off = np.arange(BLOCK)
    k_off = np.arange(2 * BLOCK)
    dist = q_off[:, None] + BLOCK - k_off[None, :]
    n = np.maximum(dist, 0)
    nf = np.maximum(n, 1).astype(np.float32)
    large = MAX_EXACT + (np.log(nf / np.float32(MAX_EXACT)) / np.float32(math.log(MAX_DISTANCE / MAX_EXACT))
                         * np.float32(NUM_BUCKETS - MAX_EXACT)).astype(np.int32)
    large = np.minimum(large, NUM_BUCKETS - 1)
    bucket = np.where(n < MAX_EXACT, n, large).astype(np.int32)
    allowed = (dist >= 0) & (dist < WINDOW)
    return np.where(allowed, bucket, -1).astype(np.int32)


def bias_table(rel_bias, bucket_p, bucket_c, dep=None):
    nb, nq = rel_bias.shape

    def body(rb_ref, bkp_ref, bkc_ref, op_ref, oc_ref):
        for bk_ref, o_ref in ((bkp_ref, op_ref), (bkc_ref, oc_ref)):
            bk = bk_ref[...]
            for h in range(nq):
                acc = jnp.full(bk.shape, -jnp.inf, F32)
                for b in range(nb):
                    acc = jnp.where(bk == b, rb_ref[b, h], acc)
                o_ref[h] = acc

    return _pallas(
        body, [rel_bias, bucket_p, bucket_c], dep=dep,
        in_specs=[_SMEM, _VMEM, _VMEM], out_specs=[_VMEM, _VMEM],
        out_shape=[jax.ShapeDtypeStruct((nq,) + bucket_p.shape, F32)] * 2,
        name="bias_table")


def bias_table_bwd(dbp, dbc, bucket_p, bucket_c, dep=None):
    nq = dbp.shape[0]

    def body(dbp_ref, dbc_ref, bkp_ref, bkc_ref, o_ref):
        bkp, bkc = bkp_ref[...][None], bkc_ref[...][None]
        dp, dc = dbp_ref[...], dbc_ref[...]
        for b in range(NUM_BUCKETS):
            sel = jnp.where(bkp == b, dp, 0.0) + jnp.where(bkc == b, dc, 0.0)
            o_ref[b] = jnp.sum(jnp.sum(sel, axis=2, keepdims=True), axis=1, keepdims=True)

    return _pallas(
        body, [dbp, dbc, bucket_p, bucket_c], dep=dep,
        in_specs=[_VMEM] * 4, out_specs=_VMEM,
        out_shape=jax.ShapeDtypeStruct((NUM_BUCKETS, nq, 1, 1), F32),
        name="bias_table_bwd")


_BNT = (((2,), (2,)), ((0,), (0,)))
_BNN = (((2,), (1,)), ((0,), (0,)))
_BTN = (((1,), (1,)), ((0,), (0,)))


@jax.custom_vjp
def _bdot_nt(a, b):
    return lax.dot_general(a.astype(BF16), b.astype(BF16), _BNT, preferred_element_type=F32)


def _bdot_nt_fwd(a, b):
    return _bdot_nt(a, b), (a, b)


def _bdot_nt_bwd(res, g):
    a, b = res
    gb = g.astype(BF16)
    da = lax.dot_general(gb, b.astype(BF16), _BNN, preferred_element_type=F32)
    db = lax.dot_general(gb, a.astype(BF16), _BTN, preferred_element_type=F32)
    return da, db


_bdot_nt.defvjp(_bdot_nt_fwd, _bdot_nt_bwd)


@jax.custom_vjp
def _bdot_nn(a, b):
    return lax.dot_general(a.astype(BF16), b.astype(BF16), _BNN, preferred_element_type=F32)


def _bdot_nn_fwd(a, b):
    return _bdot_nn(a, b), (a, b)


def _bdot_nn_bwd(res, g):
    a, b = res
    gb = g.astype(BF16)
    da = lax.dot_general(gb, b.astype(BF16), _BNT, preferred_element_type=F32)
    db = lax.dot_general(a.astype(BF16), gb, _BTN, preferred_element_type=F32)
    return da, db


_bdot_nn.defvjp(_bdot_nn_fwd, _bdot_nn_bwd)


def _attn_math(q, kp, kc, vp, vc, bp, bc, sinks, qg, kg, *, prev_ok, scale):
    h, rows, _ = q.shape
    b = kp.shape[1]
    qn = q * _rms(q) * qg
    kpn = kp * _rms(kp) * kg
    kcn = kc * _rms(kc) * kg
    lp = _bdot_nt(qn, kpn) * scale + bp.reshape(h, rows, b)
    lc = _bdot_nt(qn, kcn) * scale + bc.reshape(h, rows, b)
    lp = jnp.where(prev_ok, lp, -jnp.inf)
    sink = jnp.broadcast_to(sinks, (sinks.shape[0], b, 1)).reshape(h, rows, 1)
    m = jnp.maximum(jnp.maximum(jnp.max(lp, axis=-1, keepdims=True), jnp.max(lc, axis=-1, keepdims=True)), sink)
    m = lax.stop_gradient(m)
    pp = jnp.exp(lp - m)
    pc = jnp.exp(lc - m)
    den = jnp.sum(pp, axis=-1, keepdims=True) + jnp.sum(pc, axis=-1, keepdims=True) + jnp.exp(sink - m)
    inv = 1.0 / den
    return _bdot_nn(pp * inv, vp) + _bdot_nn(pc * inv, vc)


def _attn_specs(p, aw, kvw, nq, hd, nblk, reverse):
    assert aw % (2 * kvw) == 0
    kv_col = aw // (2 * kvw)

    def blk(n):
        return nblk - 1 - n if reverse else n

    return [
        pl.BlockSpec((BLOCK, aw), lambda n: (blk(n), 0)),
        pl.BlockSpec((BLOCK, 2 * kvw), lambda n: (jnp.maximum(blk(n) - 1, 0), kv_col)),
        pl.BlockSpec((BLOCK, 2 * kvw), lambda n: (blk(n), kv_col)),
        _full_spec((nq, BLOCK, BLOCK)), _full_spec((nq, BLOCK, BLOCK)), _full_spec((nq, 1, 1)),
        _full_spec((1, hd)), _full_spec((1, hd)),
    ]


def _head_major(ref, n_heads, grp, hd, offset=0):
    return jnp.stack([
        jnp.concatenate([ref[:, pl.ds(offset + (grp * h + g) * hd, hd)].astype(F32) for g in range(grp)], axis=0)
        for h in range(n_heads)])


def _attn_inputs(nkv, grp, hd, kvw, q_ref, kvp_ref, kvc_ref):
    return (_head_major(q_ref, nkv, grp, hd), _head_major(kvp_ref, nkv, 1, hd), _head_major(kvc_ref, nkv, 1, hd),
            _head_major(kvp_ref, nkv, 1, hd, kvw), _head_major(kvc_ref, nkv, 1, hd, kvw))


def attn_fwd(p, bias_p, bias_c, sinks, qg, kg, *, aw, kvw, dep=None):
    t, hd = p.shape[0], qg.shape[-1]
    nq, nkv, nblk = aw // hd, kvw // hd, t // BLOCK
    grp = nq // nkv
    scale = hd ** -0.5

    def body(q_ref, kvp_ref, kvc_ref, bp_ref, bc_ref, s_ref, qg_ref, kg_ref, o_ref):
        prev_ok = pl.program_id(0) > 0
        out = _attn_math(*_attn_inputs(nkv, grp, hd, kvw, q_ref, kvp_ref, kvc_ref), bp_ref[...], bc_ref[...],
                         s_ref[...], qg_ref[...], kg_ref[...], prev_ok=prev_ok, scale=scale)
        for h in range(nkv):
            for g in range(grp):
                o_ref[:, pl.ds((grp * h + g) * hd, hd)] = out[h, g * BLOCK:(g + 1) * BLOCK].astype(BF16)

    return _pallas(
        body, [p, p, p, bias_p, bias_c, sinks, qg, kg], dep=dep, grid=(nblk,),
        in_specs=_attn_specs(p, aw, kvw, nq, hd, nblk, False),
        out_specs=pl.BlockSpec((BLOCK, aw), lambda n: (n, 0)),
        out_shape=jax.ShapeDtypeStruct((t, aw), BF16),
        sem=("parallel",), name="attn_fwd")


def attn_bwd(p, bias_p, bias_c, sinks, qg, kg, do, *, aw, kvw, dep=None):
    t, hd = p.shape[0], qg.shape[-1]
    nq, nkv, nblk = aw // hd, kvw // hd, t // BLOCK
    grp = nq // nkv
    scale = hd ** -0.5

    def body(q_ref, kvp_ref, kvc_ref, bp_ref, bc_ref, s_ref, qg_ref, kg_ref, do_ref,
             dqkv_ref, dbp_ref, dbc_ref, ds_ref, dqg_ref, dkg_ref, carry):
        i = pl.program_id(0)
        prev_ok = (nblk - 1 - i) > 0

        @pl.when(i == 0)
        def _():
            carry[...] = jnp.zeros_like(carry)
            dbp_ref[...] = jnp.zeros_like(dbp_ref)
            dbc_ref[...] = jnp.zeros_like(dbc_ref)
            ds_ref[...] = jnp.zeros_like(ds_ref)
            dqg_ref[...] = jnp.zeros_like(dqg_ref)
            dkg_ref[...] = jnp.zeros_like(dkg_ref)

        fn = functools.partial(_attn_math, prev_ok=prev_ok, scale=scale)
        _, vjp = jax.vjp(fn, *_attn_inputs(nkv, grp, hd, kvw, q_ref, kvp_ref, kvc_ref), bp_ref[...], bc_ref[...],
                         s_ref[...], qg_ref[...], kg_ref[...])
        dq, dkp, dkc, dvp, dvc, dbp, dbc, dsk, dqg, dkg = vjp(_head_major(do_ref, nkv, grp, hd))
        for h in range(nkv):
            for g in range(grp):
                dqkv_ref[:, pl.ds((grp * h + g) * hd, hd)] = dq[h, g * BLOCK:(g + 1) * BLOCK].astype(BF16)
            k_cols, v_cols = pl.ds(h * hd, hd), pl.ds(kvw + h * hd, hd)
            dqkv_ref[:, pl.ds(aw + h * hd, hd)] = (dkc[h] + carry[:, k_cols]).astype(BF16)
            dqkv_ref[:, pl.ds(aw + kvw + h * hd, hd)] = (dvc[h] + carry[:, v_cols]).astype(BF16)
            carry[:, k_cols] = dkp[h]
            carry[:, v_cols] = dvp[h]
        dbp_ref[...] += dbp
        dbc_ref[...] += dbc
        ds_ref[...] += dsk
        dqg_ref[...] += dqg
        dkg_ref[...] += dkg

    return _pallas(
        body, [p, p, p, bias_p, bias_c, sinks, qg, kg, do], dep=dep, grid=(nblk,),
        in_specs=_attn_specs(p, aw, kvw, nq, hd, nblk, True)
        + [pl.BlockSpec((BLOCK, aw), lambda n: (nblk - 1 - n, 0))],
        out_specs=[
            pl.BlockSpec((BLOCK, aw + 2 * kvw), lambda n: (nblk - 1 - n, 0)),
            _full_spec((nq, BLOCK, BLOCK)), _full_spec((nq, BLOCK, BLOCK)), _full_spec((nq, 1, 1)),
            _full_spec((1, hd)), _full_spec((1, hd)),
        ],
        out_shape=[
            jax.ShapeDtypeStruct((t, aw + 2 * kvw), BF16),
            jax.ShapeDtypeStruct((nq, BLOCK, BLOCK), F32),
            jax.ShapeDtypeStruct((nq, BLOCK, BLOCK), F32),
            jax.ShapeDtypeStruct((nq, 1, 1), F32),
            jax.ShapeDtypeStruct((1, hd), F32),
            jax.ShapeDtypeStruct((1, hd), F32),
        ],
        scratch=[pltpu.VMEM((BLOCK, 2 * kvw), F32)],
        sem=("arbitrary",), name="attn_bwd")


CONV_TILE = 256


def _conv_halo_specs(tb, ch, nblk):
    per = tb // CONV_HALO
    last = nblk * per - 1
    cur = pl.BlockSpec((tb, ch), lambda n: (n, 0))
    prev = pl.BlockSpec((CONV_HALO, ch), lambda n: (jnp.maximum(n * per - 1, 0), 0))
    nxt = pl.BlockSpec((CONV_HALO, ch), lambda n: (jnp.minimum((n + 1) * per, last), 0))
    return cur, prev, nxt


def _ln_silu(co, ln_g, ln_b):
    mu = jnp.mean(co, axis=-1, keepdims=True)
    cen = co - mu
    rstd = lax.rsqrt(jnp.mean(cen * cen, axis=-1, keepdims=True) + EPS)
    xhat = cen * rstd
    z = xhat * ln_g + ln_b
    return xhat, rstd, z


def _shifted_copies(src, shifted):
    rows = src.shape[0] - SUBLANES
    for r in range(1, SUBLANES):
        shifted[r, pl.ds(0, rows), :] = src[pl.ds(r, rows), :]


def _rows_from(src, shifted, start, n):
    r = start % SUBLANES
    if r == 0:
        return src[pl.ds(start, n), :]
    return shifted[r, pl.ds(start - r, n), :]


def conv_fwd(ca, cb, conv_w, conv_b, ln_g, ln_b, dep=None):
    t, ch = ca.shape
    tb = _tile(t, CONV_TILE, CONV_HALO)
    nblk = t // tb
    cur, prev, _ = _conv_halo_specs(tb, ch, nblk)
    lead = CONV_HALO - (CONV_WIDTH - 1)

    def body(ca_ref, cb_ref, cap_ref, cbp_ref, w_ref, b_ref, g_ref, bb_ref, s_ref, co_ref, ubuf, ushift):
        n = pl.program_id(0)
        halo = cap_ref[...] * _sigmoid(cbp_ref[...])
        ubuf[pl.ds(0, CONV_HALO), :] = jnp.where(n > 0, halo, 0.0)
        ubuf[pl.ds(CONV_HALO, tb), :] = ca_ref[...] * _sigmoid(cb_ref[...])
        _shifted_copies(ubuf, ushift)
        acc = jnp.broadcast_to(b_ref[...], (tb, ch))
        for k in range(CONV_WIDTH):
            acc = acc + w_ref[pl.ds(k, 1), :] * _rows_from(ubuf, ushift, lead + k, tb)
        co_ref[...] = acc
        _, _, z = _ln_silu(acc, g_ref[...], bb_ref[...])
        s_ref[...] = (z * _sigmoid(z)).astype(BF16)

    vec = _full_spec((1, ch))
    return _pallas(
        body, [ca, cb, ca, cb, conv_w, conv_b, ln_g, ln_b], dep=dep, grid=(nblk,),
        in_specs=[cur, cur, prev, prev, _full_spec(conv_w.shape), vec, vec, vec],
        out_specs=[cur, cur],
        out_shape=[jax.ShapeDtypeStruct((t, ch), BF16), jax.ShapeDtypeStruct((t, ch), F32)],
        scratch=[pltpu.VMEM((CONV_HALO + tb, ch), F32), pltpu.VMEM((SUBLANES, CONV_HALO + tb, ch), F32)],
        sem=("parallel",), name="conv_fwd")


def conv_bwd(ca, cb, co, ds, conv_w, ln_g, ln_b, dep=None):
    t, ch = ca.shape
    tb = _tile(t, CONV_TILE, CONV_HALO)
    nblk = t // tb
    cur, prev, nxt = _conv_halo_specs(tb, ch, nblk)
    lead = CONV_HALO - (CONV_WIDTH - 1)
    ext = tb + CONV_HALO

    def body(ca_ref, cb_ref, cap_ref, cbp_ref, co_ref, con_ref, ds_ref, dsn_ref, w_ref, g_ref, bb_ref,
             dca_ref, dcb_ref, dw_ref, dvec_ref, ubuf, dbuf, ushift, dshift):
        n = pl.program_id(0)
        is_last = n == nblk - 1
        sig_b = _sigmoid(cb_ref[...])
        cav = ca_ref[...].astype(F32)
        ubuf[pl.ds(0, CONV_HALO), :] = jnp.where(n > 0, cap_ref[...] * _sigmoid(cbp_ref[...]), 0.0)
        ubuf[pl.ds(CONV_HALO, tb), :] = cav * sig_b
        _shifted_copies(ubuf, ushift)
        co = jnp.concatenate([co_ref[...], con_ref[...]], axis=0)
        xhat, rstd, z = _ln_silu(co, g_ref[...], bb_ref[...])
        dsv = jnp.concatenate([ds_ref[...].astype(F32), jnp.where(is_last, 0.0, dsn_ref[...].astype(F32))], axis=0)
        sg = _sigmoid(z)
        dz = dsv * (sg * (1.0 + z * (1.0 - sg)))
        dxh = dz * g_ref[...]
        dco = rstd * (dxh - jnp.mean(dxh, axis=-1, keepdims=True)
                      - xhat * jnp.mean(dxh * xhat, axis=-1, keepdims=True))
        dbuf[...] = dco
        _shifted_copies(dbuf, dshift)

        @pl.when(n == 0)
        def _():
            dw_ref[...] = jnp.zeros_like(dw_ref)
            dvec_ref[...] = jnp.zeros_like(dvec_ref)

        dco_cur = dco[:tb]
        dvec_ref[pl.ds(0, 1), :] += jnp.sum(dco_cur, axis=0, keepdims=True)
        dvec_ref[pl.ds(1, 1), :] += jnp.sum(dz[:tb] * xhat[:tb], axis=0, keepdims=True)
        dvec_ref[pl.ds(2, 1), :] += jnp.sum(dz[:tb], axis=0, keepdims=True)
        du = jnp.zeros((tb, ch), F32)
        for k in range(CONV_WIDTH):
            du = du + w_ref[pl.ds(k, 1), :] * _rows_from(dbuf, dshift, CONV_WIDTH - 1 - k, tb)
            dw_ref[pl.ds(k, 1), :] += jnp.sum(dco_cur * _rows_from(ubuf, ushift, lead + k, tb), axis=0,
                                              keepdims=True)
        dca_ref[...] = (du * sig_b).astype(BF16)
        dcb_ref[...] = (du * cav * sig_b * (1.0 - sig_b)).astype(BF16)

    vec = _full_spec((1, ch))
    return _pallas(
        body, [ca, cb, ca, cb, co, co, ds, ds, conv_w, ln_g, ln_b], dep=dep, grid=(nblk,),
        in_specs=[cur, cur, prev, prev, cur, nxt, cur, nxt, _full_spec(conv_w.shape), vec, vec],
        out_specs=[cur, cur, _full_spec(conv_w.shape), _full_spec((SUBLANES, ch))],
        out_shape=[jax.ShapeDtypeStruct((t, ch), BF16), jax.ShapeDtypeStruct((t, ch), BF16),
                   jax.ShapeDtypeStruct(conv_w.shape, F32), jax.ShapeDtypeStruct((SUBLANES, ch), F32)],
        scratch=[pltpu.VMEM((CONV_HALO + tb, ch), F32), pltpu.VMEM((ext, ch), F32),
                 pltpu.VMEM((SUBLANES, CONV_HALO + tb, ch), F32), pltpu.VMEM((SUBLANES, ext, ch), F32)],
        sem=("arbitrary",), name="conv_bwd")


def ada_fwd(c_t, w_ada, dep=None):
    d, nc = w_ada.shape
    nex = c_t.shape[1]
    tn = _tile(nc, 512)

    def body(ct_ref, w_ref, o_ref):
        w = w_ref[...]
        ct = ct_ref[...]
        cact = ct * _sigmoid(ct)
        rows = [jnp.sum(w * cact[:, b:b + 1], axis=0, keepdims=True) for b in range(nex)]
        o_ref[...] = jnp.concatenate(rows, axis=0)

    return _pallas(
        body, [c_t, w_ada], dep=dep, grid=(nc // tn,),
        in_specs=[_full_spec(c_t.shape), pl.BlockSpec((d, tn), lambda j: (0, j))],
        out_specs=pl.BlockSpec((nex, tn), lambda j: (0, j)),
        out_shape=jax.ShapeDtypeStruct((nex, nc), F32),
        sem=("parallel",), name="ada_fwd")


def _adamw_math(w, g, m, v):
    m = ADAM_B1 * m + (1.0 - ADAM_B1) * g
    v = ADAM_B2 * v + (1.0 - ADAM_B2) * (g * g)
    m_hat = m / (1.0 - ADAM_B1 ** ADAM_STEP)
    v_hat = v / (1.0 - ADAM_B2 ** ADAM_STEP)
    delta = -ADAM_LR * (m_hat / (jnp.sqrt(v_hat) + ADAM_EPS) + ADAM_WD * w)
    return delta, m, v


def adamw(w, g, m, v, name, copy_grad=False, dep=None):
    r, n = w.shape
    tr, tn = _ew_tiles(r, n, elems=256 * 1024)
    n_out = 4 if copy_grad else 3

    def body(w_ref, g_ref, m_ref, v_ref, *outs):
        g = g_ref[...]
        if copy_grad:
            outs[0][...] = g
        outs[-3][...], outs[-2][...], outs[-1][...] = _adamw_math(w_ref[...], g, m_ref[...], v_ref[...])

    blk = pl.BlockSpec((tr, tn), lambda i, j: (i, j))
    return _pallas(
        body, [w, g, m, v], dep=dep, grid=(r // tr, n // tn),
        in_specs=[blk] * 4, out_specs=[blk] * n_out,
        out_shape=[jax.ShapeDtypeStruct((r, n), F32)] * n_out,
        sem=("parallel", "parallel"), name=name)


def ada_grad_adamw(c_t, dmod_cols, w, m, v, dep=None):
    d, nc = w.shape
    nex = c_t.shape[1]
    tr, tn = _ew_tiles(d, nc, elems=256 * 1024)

    def body(ct_ref, dm_ref, w_ref, m_ref, v_ref, g_ref, d_ref, nm_ref, nv_ref):
        ct = ct_ref[...]
        cact = ct * _sigmoid(ct)
        dm = dm_ref[...]
        g = cact[:, 0:1] * dm[0:1, :]
        for b in range(1, nex):
            g = g + cact[:, b:b + 1] * dm[b:b + 1, :]
        g_ref[...] = g
        d_ref[...], nm_ref[...], nv_ref[...] = _adamw_math(w_ref[...], g, m_ref[...], v_ref[...])

    blk = pl.BlockSpec((tr, tn), lambda i, j: (i, j))
    return _pallas(
        body, [c_t, dmod_cols, w, m, v], dep=dep, grid=(d // tr, nc // tn),
        in_specs=[pl.BlockSpec((tr, nex), lambda i, j: (i, 0)), pl.BlockSpec((nex, tn), lambda i, j: (0, j)),
                  blk, blk, blk],
        out_specs=[blk] * 4,
        out_shape=[jax.ShapeDtypeStruct((d, nc), F32)] * 4,
        sem=("parallel", "parallel"), name="ada_grad_adamw")


def _row_pack(parts):
    cols, offs, off = [], [], 0
    for p in parts:
        n = p.shape[1]
        width = -(-n // LANES) * LANES
        cols.append(jnp.pad(p, ((0, 0), (0, width - n))) if width != n else p)
        offs.append(off)
        off += width
    return jnp.concatenate(cols, axis=1), offs


def small_sum_adamw(gathered, offs, ws, ms, vs, extra_widths, dep=None):
    ndev = gathered.shape[0]
    npar = len(ws)

    def body(ga_ref, *refs):
        w_refs, m_refs, v_refs = refs[:npar], refs[npar:2 * npar], refs[2 * npar:3 * npar]
        outs = refs[3 * npar:]
        tot = ga_ref[0]
        for s in range(1, ndev):
            tot = tot + ga_ref[s]
        for i in range(npar):
            n = ws[i].shape[1]
            g = tot[:, offs[i]:offs[i] + n]
            outs[4 * i][...] = g
            outs[4 * i + 1][...], outs[4 * i + 2][...], outs[4 * i + 3][...] = _adamw_math(
                w_refs[i][...], g, m_refs[i][...], v_refs[i][...])
        for e, n in enumerate(extra_widths):
            off = offs[npar + e]
            outs[4 * npar + e][...] = tot[:, off:off + n]

    shapes = [jax.ShapeDtypeStruct(w.shape, F32) for w in ws for _ in range(4)]
    shapes += [jax.ShapeDtypeStruct((1, n), F32) for n in extra_widths]
    return _pallas(
        body, [gathered, *ws, *ms, *vs], dep=dep, in_specs=[_VMEM] * (1 + 3 * npar), out_specs=[_VMEM] * len(shapes),
        out_shape=shapes, name="small_sum_adamw")


def _position():
    return lax.axis_index("x"), lax.axis_index("y"), lax.axis_index("c")


def _other_chips(x, y):
    return [(1 - x, y), (x, 1 - y), (1 - x, 1 - y)]


def allgather_small(block, name, dep=None):
    def body(x_ref, out_ref, send_sems, recv_sems, local_sem):
        x, y, c = _position()
        me, sibling = (x, y, c), (x, y, 1 - c)
        chips = _other_chips(x, y)

        def slot(px, py, pc):
            return out_ref.at[4 * px + 2 * py + pc]

        def copy(k, block_of, to, src=None):
            return pltpu.make_async_remote_copy(
                src_ref=slot(*block_of) if src is None else src, dst_ref=slot(*block_of),
                send_sem=send_sems.at[k], recv_sem=recv_sems.at[k], device_id=to, device_id_type=MESH)

        mine = pltpu.make_async_copy(x_ref, slot(*me), local_sem)
        mine.start()
        first = [copy(0, me, sibling, src=x_ref)]
        first += [copy(1 + j, me, (*chip, c), src=x_ref) for j, chip in enumerate(chips)]
        for cp in first:
            cp.start()
        passed = [copy(4 + j, (*chip, c), sibling) for j, chip in enumerate(chips)]
        for j, chip in enumerate(chips):
            copy(1 + j, (*chip, c), me).wait_recv()
            passed[j].start()
        copy(0, sibling, me).wait_recv()
        for j, chip in enumerate(chips):
            copy(4 + j, (*chip, 1 - c), me).wait_recv()
        for cp in first + passed:
            cp.wait_send()
        mine.wait()

    return _pallas(
        body, [block], dep=dep,
        out_shape=jax.ShapeDtypeStruct((N_DEV, *block.shape), block.dtype),
        in_specs=[_VMEM], out_specs=_VMEM,
        scratch=[pltpu.SemaphoreType.DMA((7,)), pltpu.SemaphoreType.DMA((7,)), pltpu.SemaphoreType.DMA],
        name=name)


class Started(NamedTuple):
    send_sems: Any
    recv_sems: Any
    bufs: list


def exchange_start(name, bufs, n_copies, plan, dep=None):
    nb = len(bufs)

    def body(*refs):
        for cp in plan(refs[:nb], refs[nb], refs[nb + 1]):
            cp.start()

    outs = _pallas(
        body, [pltpu.with_memory_space_constraint(b, pltpu.HBM) for b in bufs], dep=dep, name=name,
        out_shape=(pltpu.SemaphoreType.DMA((n_copies,)), pltpu.SemaphoreType.DMA((n_copies,)),
                   *[pltpu.HBM(b.shape, b.dtype) for b in bufs]),
        in_specs=[_HBM] * nb,
        out_specs=(_SEM, _SEM, *[_HBM] * nb),
        input_output_aliases={i: 2 + i for i in range(nb)},
        compiler_params=pltpu.CompilerParams(has_side_effects=_EFFECT))
    return Started(outs[0], outs[1], list(outs[2:2 + nb]))


def exchange_wait(name, started, plan, bufs=None, dep=None):
    if bufs is not None:
        started = started._replace(bufs=list(bufs))
    nb = len(started.bufs)

    def body(*refs):
        for cp in plan(refs[:nb], refs[nb], refs[nb + 1]):
            cp.wait_send()
            cp.wait_recv()

    outs = _pallas(
        body, [*started.bufs, started.send_sems, started.recv_sems], dep=dep, name=name,
        out_shape=tuple(pltpu.HBM(b.shape, b.dtype) for b in started.bufs),
        in_specs=[_HBM] * nb + [_SEM, _SEM],
        out_specs=tuple([_HBM] * nb),
        input_output_aliases={i: i for i in range(nb)},
        compiler_params=pltpu.CompilerParams(has_side_effects=_EFFECT))
    return list(outs)


def _remote(src, dst, send_sems, recv_sems, i, to):
    return pltpu.make_async_remote_copy(src_ref=src, dst_ref=dst, send_sem=send_sems.at[i], recv_sem=recv_sems.at[i],
                                        device_id=to, device_id_type=MESH)


def _half_rows(buf_rows, chip_idx, pc):
    half = buf_rows // (2 * N_CHIPS)
    return pl.ds((2 * chip_idx + pc) * half, half)


ALL_PEERS = (0, 1, 2)


def plan_gather_ici(refs, send_sems, recv_sems, peers=ALL_PEERS):
    x, y, c = _position()
    chips = _other_chips(x, y)
    copies = []
    for k, ref in enumerate(refs):
        rows = ref.at[_half_rows(ref.shape[0], 2 * x + y, c), :]
        for i, j in enumerate(peers):
            copies.append(_remote(rows, rows, send_sems, recv_sems, len(peers) * k + i, (*chips[j], c)))
    return copies


def plan_gather_d2d(refs, send_sems, recv_sems, peers=ALL_PEERS):
    x, y, c = _position()
    chips = _other_chips(x, y)
    copies = []
    for k, ref in enumerate(refs):
        for i, j in enumerate(peers):
            px, py = chips[j]
            rows = ref.at[_half_rows(ref.shape[0], 2 * px + py, c), :]
            copies.append(_remote(rows, rows, send_sems, recv_sems, len(peers) * k + i, (x, y, 1 - c)))
    return copies


def plan_pair_exchange(refs, send_sems, recv_sems):
    x, y, c = _position()
    nw = len(refs) // 2
    copies = []
    for k in range(nw):
        for chip in range(N_CHIPS):
            copies.append(_remote(refs[k].at[chip, 1 - c], refs[nw + k].at[chip], send_sems, recv_sems,
                                  N_CHIPS * k + chip, (x, y, 1 - c)))
    return copies


def plan_chip_exchange(refs, send_sems, recv_sems):
    x, y, c = _position()
    nw = len(refs) // 2
    copies = []
    for k in range(nw):
        for j, (px, py) in enumerate(_other_chips(x, y)):
            copies.append(_remote(refs[k].at[2 * px + py], refs[nw + k].at[2 * x + y], send_sems, recv_sems,
                                  3 * k + j, (px, py, c)))
    return copies


def plan_pair_share(refs, send_sems, recv_sems):
    x, y, c = _position()
    return [_remote(ref.at[c], ref.at[c], send_sems, recv_sems, k, (x, y, 1 - c)) for k, ref in enumerate(refs)]


def cast_into_slot(src, slot, n_slots, name, dep=None):
    r, n = src.shape
    tr, tn = _ew_tiles(r, n, BF16_SUBLANES)

    def body(slot_ref, s_ref, o_ref):
        o_ref[...] = s_ref[...].astype(BF16)

    return _pallas(
        body, [slot, src], dep=dep, n_prefetch=1, grid=(r // tr, n // tn),
        in_specs=[pl.BlockSpec((tr, tn), lambda i, j, sl: (i, j))],
        out_specs=pl.BlockSpec((None, tr, tn), lambda i, j, sl: (sl[0], i, j)),
        out_shape=jax.ShapeDtypeStruct((n_slots, r, n), BF16),
        sem=("parallel", "parallel"), name=name)


def pair_sum(g, r, core, name, dep=None):
    nchip, _, h, n = g.shape
    th, tn = _ew_tiles(h, n, BF16_SUBLANES)

    def body(core_ref, g_ref, r_ref, o_ref):
        o_ref[...] = (g_ref[...].astype(F32) + r_ref[...].astype(F32)).astype(BF16)

    return _pallas(
        body, [core, g, r], dep=dep, n_prefetch=1, grid=(nchip, h // th, n // tn),
        in_specs=[pl.BlockSpec((None, None, th, tn), lambda a, i, j, cr: (a, cr[0], i, j)),
                  pl.BlockSpec((None, th, tn), lambda a, i, j, cr: (a, i, j))],
        out_specs=pl.BlockSpec((None, th, tn), lambda a, i, j, cr: (a, i, j)),
        out_shape=jax.ShapeDtypeStruct((nchip, h, n), BF16),
        sem=("parallel", "parallel", "parallel"), name=name)


def chip_sum(own, got, where, name, dep=None):
    nchip, h, n = got.shape
    th, tn = _ew_tiles(h, n, BF16_SUBLANES, elems=256 * 1024)

    def body(where_ref, own_ref, *rest):
        got_refs, o_ref = rest[:nchip], rest[nchip]
        chip = where_ref[0]
        acc = None
        for s in range(nchip):
            term = jnp.where(chip == s, own_ref[...], got_refs[s][...]).astype(F32)
            acc = term if acc is None else acc + term
        o_ref[...] = acc

    def got_spec(s):
        return pl.BlockSpec((None, th, tn), lambda i, j, wr: (jnp.where(wr[0] == s, (s + 1) % nchip, s), i, j))

    return _pallas(
        body, [where, own, *[got] * nchip], dep=dep, n_prefetch=1, grid=(h // th, n // tn),
        in_specs=[pl.BlockSpec((None, th, tn), lambda i, j, wr: (wr[0], i, j))]
        + [got_spec(s) for s in range(nchip)],
        out_specs=pl.BlockSpec((None, th, tn), lambda i, j, wr: (wr[1], i, j)),
        out_shape=jax.ShapeDtypeStruct((2, h, n), F32),
        sem=("parallel", "parallel"), name=name)


def kernel(x, c, w_ada, b_ada, norm_mix_g, w_in, q_norm_g, k_norm_g, attn_sinks, rel_bias, w_attn_out, conv_w, conv_b, conv_ln_g, conv_ln_b, w_conv_out, w_mix_out, norm_ffn_g, w_ffn_in, w_ffn_out, loss_target, m_w_ada, m_b_ada, m_norm_mix_g, m_w_in, m_q_norm_g, m_k_norm_g, m_attn_sinks, m_rel_bias, m_w_attn_out, m_conv_w, m_conv_b, m_conv_ln_g, m_conv_ln_b, m_w_conv_out, m_w_mix_out, m_norm_ffn_g, m_w_ffn_in, m_w_ffn_out, v_w_ada, v_b_ada, v_norm_mix_g, v_w_in, v_q_norm_g, v_k_norm_g, v_attn_sinks, v_rel_bias, v_w_attn_out, v_conv_w, v_conv_b, v_conv_ln_g, v_conv_ln_b, v_w_conv_out, v_w_mix_out, v_norm_ffn_g, v_w_ffn_in, v_w_ffn_out):
    run = InOrder()
    xi, yi, ci = _position()
    chip = 2 * xi + yi
    me = 2 * chip + ci
    chip_arr = chip.astype(jnp.int32).reshape(1)
    core_arr = ci.astype(jnp.int32).reshape(1)
    where_arr = jnp.stack([chip, ci]).astype(jnp.int32)

    xe, tgt = x[0], loss_target[0]
    t, d = xe.shape
    hd = q_norm_g.shape[-1]
    nq = attn_sinks.shape[-1]
    aw = nq * hd
    ch = conv_b.shape[-1]
    in_width = N_CHIPS * w_in.shape[-1]
    kvw = (in_width - aw - 2 * ch - 2 * d) // 2
    nkv = kvw // hd
    dff = N_CHIPS * w_ffn_out.shape[1]
    off_k, off_v, off_ca = aw, aw + kvw, aw + 2 * kvw
    off_cb, off_ga, off_gc = off_ca + ch, off_ca + 2 * ch, off_ca + 2 * ch + d
    nc_ada = w_ada.shape[-1]
    ch_loc = conv_w.shape[-1]
    nj_ffn = w_ffn_in.shape[-1]
    perm_ffn = ffn_perm(N_CHIPS)

    big = {"w_in": w_in[0], "w_attn_out": w_attn_out[0], "w_conv_out": w_conv_out[0], "w_mix_out": w_mix_out[0],
           "w_ffn_in": w_ffn_in[0], "w_ffn_out": w_ffn_out[0]}
    moments = {"w_in": (m_w_in, v_w_in), "w_attn_out": (m_w_attn_out, v_w_attn_out),
               "w_conv_out": (m_w_conv_out, v_w_conv_out), "w_mix_out": (m_w_mix_out, v_w_mix_out),
               "w_ffn_in": (m_w_ffn_in, v_w_ffn_in), "w_ffn_out": (m_w_ffn_out, v_w_ffn_out)}
    gather_groups = {"in": ["w_in"], "branch_out": ["w_attn_out", "w_conv_out"], "mix_out": ["w_mix_out"],
                     "ffn_in": ["w_ffn_in"], "ffn_out": ["w_ffn_out"]}
    grads, deltas, new_m, new_v = {}, {}, {}, {}

    def gather_cast(gname):
        bufs = []
        for n in gather_groups[gname]:
            r, ncol = big[n].shape
            bufs.append(run(cast_into_slot, big[n], chip_arr, N_CHIPS, "cast_" + n).reshape(N_CHIPS * r, ncol))
        return bufs

    def gather_ici_start(gname, bufs):
        return run(exchange_start, "gather_ici_start_" + gname, bufs, 3 * len(bufs), plan_gather_ici)

    def gather_pass_on(gname, ici):
        landed = run(exchange_wait, "gather_ici_wait_" + gname, ici, plan_gather_ici)
        return run(exchange_start, "gather_d2d_start_" + gname, landed, 3 * len(landed), plan_gather_d2d)

    def gathered(gname, d2d):
        outs = run(exchange_wait, "gather_d2d_wait_" + gname, d2d, plan_gather_d2d)
        return [o.reshape(N_CHIPS, *big[n].shape) for o, n in zip(outs, gather_groups[gname])]

    def rs_pair_start(gname, names, partials):
        blocks = [g.reshape(N_CHIPS, 2, big[n].shape[0] // 2, big[n].shape[1]) for n, g in zip(names, partials)]
        land = [lax.empty((N_CHIPS,) + b.shape[2:], BF16) for b in blocks]
        return run(exchange_start, "pair_exchange_start_" + gname, blocks + land, N_CHIPS * len(blocks),
                   plan_pair_exchange)

    def rs_chip_start(gname, names, pair):
        nw = len(names)
        outs = run(exchange_wait, "pair_exchange_wait_" + gname, pair, plan_pair_exchange)
        sums = [run(pair_sum, g, r, core_arr, "pair_sum_" + n) for n, g, r in zip(names, outs[:nw], outs[nw:])]
        land = [lax.empty(s.shape, BF16) for s in sums]
        return run(exchange_start, "chip_exchange_start_" + gname, sums + land, 3 * nw, plan_chip_exchange)

    def rs_share_start(gname, names, chipx):
        nw = len(names)
        outs = run(exchange_wait, "chip_exchange_wait_" + gname, chipx, plan_chip_exchange)
        halves = [run(chip_sum, s, r, where_arr, "chip_sum_" + n) for n, s, r in zip(names, outs[:nw], outs[nw:])]
        return run(exchange_start, "pair_share_start_" + gname, halves, nw, plan_pair_share)

    def rs_finish(gname, names, share):
        fulls = run(exchange_wait, "pair_share_wait_" + gname, share, plan_pair_share)
        for n, g2 in zip(names, fulls):
            g, dl, nm, nv = run(adamw, big[n], g2.reshape(big[n].shape), moments[n][0][0], moments[n][1][0],
                                "adamw_" + n, copy_grad=True)
            grads[n], deltas[n], new_m[n], new_v[n] = g[None], dl[None], nm[None], nv[None]

    near, far = (0, 1), (2,)
    plan_ici_near = functools.partial(plan_gather_ici, peers=near)
    plan_ici_far = functools.partial(plan_gather_ici, peers=far)
    plan_d2d_near = functools.partial(plan_gather_d2d, peers=near)
    plan_d2d_far = functools.partial(plan_gather_d2d, peers=far)
    bufs_in = gather_cast("in")
    row1, offs1 = _row_pack([c, conv_w[0].reshape(1, CONV_WIDTH * ch_loc)])
    got1 = run(allgather_small, row1, "allgather_cond")
    ici_near = run(exchange_start, "gather_ici_start_in_near", bufs_in, len(near), plan_ici_near)
    c_all = got1[:, 0, :d]
    conv_w_full = got1[0::2, 0, offs1[1]:offs1[1] + CONV_WIDTH * ch_loc].reshape(N_CHIPS, CONV_WIDTH, ch_loc)
    conv_w_full = jnp.transpose(conv_w_full, (1, 0, 2)).reshape(CONV_WIDTH, ch)
    conv_w_pad = jnp.pad(conv_w_full, ((0, 1), (0, 0)))
    c_t = jnp.transpose(c_all)
    mod_cols = run(ada_fwd, c_t, w_ada[0])
    rest_bufs = {gname: gather_cast(gname) for gname in gather_groups if gname != "in"}
    got2 = run(allgather_small, mod_cols, "allgather_mod")
    mod_all = got2.reshape(N_CHIPS, 2, N_DEV, nc_ada)[:, 0]
    mod = lax.dynamic_slice_in_dim(mod_all, me, 1, axis=1).reshape(1, N_CHIPS * nc_ada) + b_ada
    mod = jnp.pad(mod.reshape(N_MOD, d), ((0, SUBLANES - N_MOD), (0, 0)))

    h = run(pre_mix_fwd, xe, mod, norm_mix_g)
    bucket = _t5_bucket_table()
    bucket_p, bucket_c = jnp.asarray(bucket[:, :BLOCK]), jnp.asarray(bucket[:, BLOCK:])
    bias_p, bias_c = run(bias_table, rel_bias, bucket_p, bucket_c)

    def in_blocks(buf):
        return buf.reshape(N_CHIPS, *big["w_in"].shape)

    def chip_ids(peers):
        others = [2 * (1 - xi) + yi, 2 * xi + (1 - yi), 2 * (1 - xi) + (1 - yi)]
        return jnp.stack([others[j] for j in peers]).astype(jnp.int32)

    tn_in = big["w_in"].shape[1]
    landed = run(exchange_wait, "gather_ici_wait_in_near", ici_near, plan_ici_near)
    ici_far = run(exchange_start, "gather_ici_start_in_far", landed, len(far), plan_ici_far)
    ici_near_ffn = run(exchange_start, "gather_ici_start_ffn_in_near", rest_bufs["ffn_in"], len(near), plan_ici_near)
    ici = {gname: gather_ici_start(gname, rest_bufs[gname]) for gname in ("branch_out", "mix_out")}
    d2d_near = run(exchange_start, "gather_d2d_start_in_near", ici_far.bufs, len(near), plan_d2d_near)
    p = run(mm_nn_blocks, h, in_blocks(d2d_near.bufs[0]), chip_arr, None, tn=tn_in, out_dtype=BF16, name="mm_in_own")
    landed = run(exchange_wait, "gather_d2d_wait_in_near", d2d_near, plan_d2d_near)
    p = run(mm_nn_blocks, h, in_blocks(landed[0]), chip_ids(near), p, tn=tn_in, out_dtype=BF16, name="mm_in_near")
    landed = run(exchange_wait, "gather_ici_wait_in_far", ici_far, plan_ici_far, bufs=landed)
    d2d_far = run(exchange_start, "gather_d2d_start_in_far", landed, len(far), plan_d2d_far)
    landed = run(exchange_wait, "gather_d2d_wait_in_far", d2d_far, plan_d2d_far)
    wg_in = in_blocks(landed[0])
    p = run(mm_nn_blocks, h, wg_in, chip_ids(far), p, tn=tn_in, out_dtype=BF16, name="mm_in_far")

    sinks3 = attn_sinks.reshape(nq, 1, 1)
    attn_o = run(attn_fwd, p, bias_p, bias_c, sinks3, q_norm_g, k_norm_g, aw=aw, kvw=kvw)
    ca, cb = p[:, off_ca:off_cb], p[:, off_cb:off_ga]
    s_conv, co_conv = run(conv_fwd, ca, cb, conv_w_pad, conv_b, conv_ln_g, conv_ln_b)
    d2d_branch = gather_pass_on("branch_out", ici["branch_out"])
    wg_attn_out, wg_conv_out = gathered("branch_out", d2d_branch)
    y_attn = run(mm_nn, attn_o, wg_attn_out, tn=_tile(wg_attn_out.shape[2], 512), tk=aw, out_dtype=BF16,
                 name="mm_attn_out")
    y_conv = run(mm_nn, s_conv, wg_conv_out, tn=_tile(wg_conv_out.shape[2], 512), tk=ch, out_dtype=BF16,
                 name="mm_conv_out")
    landed = run(exchange_wait, "gather_ici_wait_ffn_in_near", ici_near_ffn, plan_ici_near)
    ici_far_ffn = run(exchange_start, "gather_ici_start_ffn_in_far", landed, len(far), plan_ici_far)
    ici_near_out = run(exchange_start, "gather_ici_start_ffn_out_near", rest_bufs["ffn_out"], len(near),
                       plan_ici_near)
    d2d_near_ffn = run(exchange_start, "gather_d2d_start_ffn_in_near", ici_far_ffn.bufs, len(near), plan_d2d_near)
    merged = run(merge_fwd, p, y_attn, y_conv, off_ga, off_gc)
    d2d_mix = gather_pass_on("mix_out", ici["mix_out"])
    (wg_mix_out,) = gathered("mix_out", d2d_mix)
    wg_mix_out = wg_mix_out.reshape(1, d, d)
    o_m = run(mm_nn, merged, wg_mix_out, tn=_tile(d, 512), tk=d, out_dtype=F32, name="mm_mix_out")
    x1, h2 = run(pre_ffn_fwd, xe, o_m, mod, norm_ffn_g)

    def ffn_blocks(buf):
        return buf.reshape(N_CHIPS, *big["w_ffn_in"].shape)

    tn_ffn = _tile(nj_ffn, 1408)
    landed = run(exchange_wait, "gather_d2d_wait_ffn_in_near", d2d_near_ffn, plan_d2d_near)
    f = run(mm_nn_blocks, h2, ffn_blocks(landed[0]), chip_arr, None, tn=tn_ffn, out_dtype=BF16, perm=perm_ffn,
            name="mm_ffn_in_own")
    landed = run(exchange_wait, "gather_ici_wait_ffn_in_far", ici_far_ffn, plan_ici_far, bufs=landed)
    d2d_far_ffn = run(exchange_start, "gather_d2d_start_ffn_in_far", landed, len(far), plan_d2d_far)
    f = run(mm_nn_blocks, h2, ffn_blocks(d2d_far_ffn.bufs[0]), chip_ids(near), f, tn=tn_ffn, out_dtype=BF16,
            perm=perm_ffn, name="mm_ffn_in_near")
    landed_o = run(exchange_wait, "gather_ici_wait_ffn_out_near", ici_near_out, plan_ici_near)
    ici_far_out = run(exchange_start, "gather_ici_start_ffn_out_far", landed_o, len(far), plan_ici_far)
    d2d_near_out = run(exchange_start, "gather_d2d_start_ffn_out_near", ici_far_out.bufs, len(near), plan_d2d_near)
    landed = run(exchange_wait, "gather_d2d_wait_ffn_in_far", d2d_far_ffn, plan_d2d_far)
    wg_ffn_in = ffn_blocks(landed[0])
    f = run(mm_nn_blocks, h2, wg_ffn_in, chip_ids(far), f, tn=tn_ffn, out_dtype=BF16, perm=perm_ffn,
            name="mm_ffn_in_far")
    act = run(swiglu_fwd, f, nj_ffn)

    def out_blocks(buf):
        return buf.reshape(N_CHIPS, *big["w_ffn_out"].shape)

    tn_out = _tile(d, 512)
    landed_o = run(exchange_wait, "gather_d2d_wait_ffn_out_near", d2d_near_out, plan_d2d_near)
    o_f = run(mm_nn_rowblocks, act, out_blocks(landed_o[0]), chip_arr, None, tn=tn_out, name="mm_ffn_out_own")
    landed_o = run(exchange_wait, "gather_ici_wait_ffn_out_far", ici_far_out, plan_ici_far, bufs=landed_o)
    d2d_far_out = run(exchange_start, "gather_d2d_start_ffn_out_far", landed_o, len(far), plan_d2d_far)
    o_f = run(mm_nn_rowblocks, act, out_blocks(d2d_far_out.bufs[0]), chip_ids(near), o_f, tn=tn_out,
              name="mm_ffn_out_near")
    landed_o = run(exchange_wait, "gather_d2d_wait_ffn_out_far", d2d_far_out, plan_d2d_far)
    wg_ffn_out = landed_o[0].reshape(1, dff, d)
    o_f = run(mm_nn_rowblocks, act, out_blocks(landed_o[0]), chip_ids(far), o_f, tn=tn_out, name="mm_ffn_out_far")
    loss11, dy, dof, acc_l = run(loss_head, x1, o_f, tgt, mod)

    gw_ffn_out = run(mm_tn, act, dof, 1, tk=_tile(dff, 512), tn=d, name="mm_ffn_out_dw")
    px_ffn_out = rs_pair_start("ffn_out", ["w_ffn_out"], [gw_ffn_out])
    dact = run(mm_nt, dof, wg_ffn_out, tko=_tile(dff, 512), tn=d, out_dtype=BF16, name="mm_ffn_out_dx")
    cx_ffn_out = rs_chip_start("ffn_out", ["w_ffn_out"], px_ffn_out)
    df = run(swiglu_bwd, f, dact, nj_ffn)
    gw_ffn_in = run(mm_tn, h2, df, N_CHIPS, tk=d, tn=_tile(nj_ffn, 1408), name="mm_ffn_in_dw",
                    perm=perm_ffn)
    px_ffn_in = rs_pair_start("ffn_in", ["w_ffn_in"], [gw_ffn_in])
    dh2 = run(mm_nt, df, wg_ffn_in, tko=_tile(d, 512), tn=nj_ffn, name="mm_ffn_in_dx", perm=perm_ffn)
    sh_ffn_out = rs_share_start("ffn_out", ["w_ffn_out"], cx_ffn_out)
    cx_ffn_in = rs_chip_start("ffn_in", ["w_ffn_in"], px_ffn_in)
    dx1, dom, acc_f = run(pre_ffn_bwd, x1, dh2, dy, o_m, mod, norm_ffn_g)
    gw_mix_out = run(mm_tn, merged, dom, 1, tk=d, tn=_tile(d, 1024), name="mm_mix_out_dw")
    px_mix = rs_pair_start("mix_out", ["w_mix_out"], [gw_mix_out])
    dmerged = run(mm_nt, dom, wg_mix_out, tko=_tile(d, 512), tn=d, out_dtype=BF16, name="mm_mix_out_dx")
    dy_attn, dy_conv, dga, dgc = run(merge_bwd, p, y_attn, y_conv, dmerged, off_ga, off_gc)
    rs_finish("ffn_out", ["w_ffn_out"], sh_ffn_out)
    cx_mix = rs_chip_start("mix_out", ["w_mix_out"], px_mix)
    gw_attn_out = run(mm_tn, attn_o, dy_attn, N_CHIPS, tk=aw, tn=_tile(wg_attn_out.shape[2], 512),
                      name="mm_attn_out_dw")
    gw_conv_out = run(mm_tn, s_conv, dy_conv, N_CHIPS, tk=ch, tn=_tile(wg_conv_out.shape[2], 512),
                      name="mm_conv_out_dw")
    ac_names = ["w_attn_out", "w_conv_out"]
    px_ac = rs_pair_start("attn_conv_out", ac_names, [gw_attn_out, gw_conv_out])
    dattn_o = run(mm_nt, dy_attn, wg_attn_out, tko=_tile(aw, 1024), tn=_tile(wg_attn_out.shape[2], 512),
                  out_dtype=BF16, name="mm_attn_out_dx")
    ds_conv = run(mm_nt, dy_conv, wg_conv_out, tko=_tile(ch, 1024), tn=_tile(wg_conv_out.shape[2], 512),
                  out_dtype=BF16, name="mm_conv_out_dx")
    cx_ac = rs_chip_start("attn_conv_out", ac_names, px_ac)
    dca, dcb, dconv_w, dconv_vec = run(conv_bwd, ca, cb, co_conv, ds_conv, conv_w_pad, conv_ln_g, conv_ln_b)
    sh_ffn_in = rs_share_start("ffn_in", ["w_ffn_in"], cx_ffn_in)
    dqkv, dbp, dbc, dsinks, dqg, dkg = run(attn_bwd, p, bias_p, bias_c, sinks3, q_norm_g, k_norm_g, dattn_o,
                                           aw=aw, kvw=kvw)
    sh_mix = rs_share_start("mix_out", ["w_mix_out"], cx_mix)
    sh_ac = rs_share_start("attn_conv_out", ac_names, cx_ac)
    drel = run(bias_table_bwd, dbp, dbc, bucket_p, bucket_c).reshape(NUM_BUCKETS, nq)
    dp = jnp.concatenate([dqkv, dca, dcb, dga, dgc], axis=1)
    gw_in = run(mm_tn, h, dp, N_CHIPS, tk=d, tn=wg_in.shape[2], name="mm_in_dw")
    px_in = rs_pair_start("in", ["w_in"], [gw_in])
    dh = run(mm_nt, dp, wg_in, tko=_tile(d, 1024), tn=wg_in.shape[2], name="mm_in_dx")
    grad_x, acc_m = run(pre_mix_bwd, xe, dh, dx1, mod, norm_mix_g)

    dmod = jnp.concatenate([acc_m[0:1], acc_m[1:2], acc_f[3:4], acc_f[0:1], acc_f[1:2], acc_l[0:1]], axis=1)
    small_names = ["b_ada", "norm_mix_g", "q_norm_g", "k_norm_g", "attn_sinks", "rel_bias", "conv_b", "conv_ln_g",
                   "conv_ln_b", "norm_ffn_g"]
    small_w = [b_ada, norm_mix_g, q_norm_g, k_norm_g, attn_sinks, rel_bias, conv_b, conv_ln_g, conv_ln_b, norm_ffn_g]
    small_m = [m_b_ada, m_norm_mix_g, m_q_norm_g, m_k_norm_g, m_attn_sinks, m_rel_bias, m_conv_b, m_conv_ln_g,
               m_conv_ln_b, m_norm_ffn_g]
    small_v = [v_b_ada, v_norm_mix_g, v_q_norm_g, v_k_norm_g, v_attn_sinks, v_rel_bias, v_conv_b, v_conv_ln_g,
               v_conv_ln_b, v_norm_ffn_g]
    small_g = [dmod, acc_m[2:3], dqg, dkg, dsinks.reshape(1, nq), drel.reshape(1, NUM_BUCKETS * nq),
               dconv_vec[0:1], dconv_vec[1:2], dconv_vec[2:3], acc_f[2:3]]
    row3, offs3 = _row_pack(small_g + [dconv_w[:CONV_WIDTH].reshape(1, CONV_WIDTH * ch), loss11])
    got3 = run(allgather_small, row3, "allgather_small_grads")
    cx_in = rs_chip_start("in", ["w_in"], px_in)
    as_row = lambda a: a.reshape(1, -1)
    outs3 = run(small_sum_adamw, got3, offs3, [as_row(a) for a in small_w], [as_row(a) for a in small_m],
                [as_row(a) for a in small_v], [CONV_WIDTH * ch, 1])
    for i, (n, w) in enumerate(zip(small_names, small_w)):
        grads[n], deltas[n], new_m[n], new_v[n] = (o.reshape(w.shape) for o in outs3[4 * i:4 * i + 4])
    g_conv_w_all, loss_sum = outs3[-2].reshape(CONV_WIDTH, ch), outs3[-1]

    g_conv_w = lax.dynamic_slice_in_dim(g_conv_w_all, chip * ch_loc, ch_loc, axis=1)
    grads["conv_w"] = g_conv_w[None]
    dl, nm, nv = run(adamw, conv_w[0], g_conv_w, m_conv_w[0], v_conv_w[0], "adamw_conv_w")
    deltas["conv_w"], new_m["conv_w"], new_v["conv_w"] = dl[None], nm[None], nv[None]

    dmod_all = got3[:, 0, :N_MOD * d]
    dmod_cols = lax.dynamic_slice_in_dim(dmod_all, chip * nc_ada, nc_ada, axis=1)
    g_ada, dl, nm, nv = run(ada_grad_adamw, c_t, dmod_cols, w_ada[0], m_w_ada[0], v_w_ada[0])
    grads["w_ada"], deltas["w_ada"], new_m["w_ada"], new_v["w_ada"] = g_ada[None], dl[None], nm[None], nv[None]

    rs_finish("ffn_in", ["w_ffn_in"], sh_ffn_in)
    rs_finish("mix_out", ["w_mix_out"], sh_mix)
    rs_finish("attn_conv_out", ac_names, sh_ac)
    sh_in = rs_share_start("in", ["w_in"], cx_in)
    rs_finish("in", ["w_in"], sh_in)

    loss = loss_sum[0, 0]
    order = ["w_ada", "b_ada", "norm_mix_g", "w_in", "q_norm_g", "k_norm_g", "attn_sinks", "rel_bias", "w_attn_out",
             "conv_w", "conv_b", "conv_ln_g", "conv_ln_b", "w_conv_out", "w_mix_out", "norm_ffn_g", "w_ffn_in",
             "w_ffn_out"]
    return (loss, grad_x[None], *[grads[n] for n in order], *[deltas[n] for n in order],
            *[new_m[n] for n in order], *[new_v[n] for n in order])
```

```python
import functools
import math
from typing import Any, NamedTuple

import jax
import jax.numpy as jnp
import numpy as np
from jax import lax
from jax.experimental import pallas as pl
from jax.experimental.pallas import tpu as pltpu

F32 = jnp.float32
BF16 = jnp.bfloat16
MESH = pl.DeviceIdType.MESH

V7X_VMEM_BYTES = 64 * 1024 * 1024
VMEM_LIMIT = V7X_VMEM_BYTES - 8 * 1024 * 1024
LANES = 128
SUBLANES = 8
BF16_SUBLANES = 16

EPS = 1e-6
WINDOW = 128
BLOCK = 128
NUM_BUCKETS = 32
MAX_EXACT = NUM_BUCKETS // 2
MAX_DISTANCE = 128
CONV_WIDTH = 31
CONV_HALO = 32
ADAM_LR = 0.001
ADAM_B1 = 0.9
ADAM_B2 = 0.999
ADAM_EPS = 1e-08
ADAM_WD = 0.01
ADAM_STEP = 10
N_MOD = 6
SH_M, SC_M, GT_M, SH_F, SC_F, GT_F = range(6)

N_CHIPS = 4
N_DEV = 8

_ANY = pl.BlockSpec(memory_space=pl.ANY)
_VMEM = pl.BlockSpec(memory_space=pltpu.VMEM)
_SMEM = pl.BlockSpec(memory_space=pltpu.SMEM)
_HBM = pl.BlockSpec(memory_space=pltpu.HBM)
_SEM = pl.BlockSpec(memory_space=pltpu.SEMAPHORE)
_EFFECT = pltpu.SideEffectType.DATAFLOW_SIDE_EFFECTING


class InOrder:
    def __init__(self):
        self.token = None

    def __call__(self, fn, *args, **kw):
        return fn(*args, dep=self, **kw)


def _pallas(body, args, *, in_specs, out_specs, out_shape, name, dep=None, grid=(), n_prefetch=0, scratch=(),
            sem=None, **kw):
    n_lead = n_prefetch + len(in_specs)
    in_specs, args = list(in_specs), list(args)
    single = not isinstance(out_shape, (list, tuple))
    out_shapes = [out_shape] if single else list(out_shape)
    out_specs = [out_specs] if single else list(out_specs)
    if dep is not None:
        inner, n_out, takes = body, len(out_shapes), dep.token is not None

        def body(*refs):
            rest = refs[n_lead + (1 if takes else 0):]
            rest[n_out][...] = jnp.zeros((SUBLANES, LANES), F32)
            return inner(*refs[:n_lead], *rest[:n_out], *rest[n_out + 1:])

        if takes:
            in_specs.append(_ANY)
            args.append(dep.token)
        out_shapes.append(jax.ShapeDtypeStruct((SUBLANES, LANES), F32))
        out_specs.append(pl.BlockSpec((SUBLANES, LANES), lambda *_: (0, 0)))
    params = kw.pop("compiler_params", None)
    if params is None:
        params = pltpu.CompilerParams(dimension_semantics=sem, vmem_limit_bytes=VMEM_LIMIT)
    outs = pl.pallas_call(
        body,
        grid_spec=pltpu.PrefetchScalarGridSpec(num_scalar_prefetch=n_prefetch, grid=grid, in_specs=in_specs,
                                               out_specs=out_specs, scratch_shapes=list(scratch)),
        out_shape=out_shapes, compiler_params=params, name=name, **kw,
    )(*args)
    if dep is not None:
        dep.token = outs[-1]
        outs = outs[:-1]
    return outs[0] if single else list(outs)


def _tile(n, pref, unit=LANES):
    best = None
    for t in range(unit, min(n, pref) + 1, unit):
        if n % t == 0:
            best = t
    return best if best is not None else n


def _sigmoid(v):
    return 1.0 / (1.0 + jnp.exp(-v.astype(F32)))


ROW_CHUNK = 512


def _row_chunks(m, unit=SUBLANES):
    step = _tile(m, ROW_CHUNK, unit)
    return [(s, step) for s in range(0, m, step)]


def _ew_tiles(r, n, unit=SUBLANES, elems=512 * 1024):
    return _tile(r, max(unit, elems // n), unit), n


def _block_pos(j, perm):
    if perm is None:
        return j
    pos = 0
    for a, p in enumerate(perm):
        pos = pos + jnp.where(j == a, p, 0)
    return pos


def mm_nn(a, w, *, tn, tk, out_dtype, name, perm=None, dep=None):
    m, k = a.shape
    j, k2, nj = w.shape
    assert k == k2 and nj % tn == 0 and k % tk == 0
    npj, nk = nj // tn, k // tk

    def body(a_ref, w_ref, o_ref, *scratch):
        kk = pl.program_id(1)
        for s, sz in _row_chunks(m):
            rows = pl.ds(s, sz)
            p = jnp.dot(a_ref[rows, :], w_ref[...], preferred_element_type=F32)
            if nk == 1:
                o_ref[rows, :] = p.astype(out_dtype)
            else:
                acc = scratch[0]

                @pl.when(kk == 0)
                def _():
                    acc[rows, :] = p

                @pl.when(kk > 0)
                def _():
                    acc[rows, :] += p

                @pl.when(kk == nk - 1)
                def _():
                    o_ref[rows, :] = acc[rows, :].astype(out_dtype)

    return _pallas(
        body, [a, w], dep=dep, grid=(j * npj, nk),
        in_specs=[
            pl.BlockSpec((m, tk), lambda n, kk: (0, kk)),
            pl.BlockSpec((None, tk, tn), lambda n, kk: (n // npj, kk, n % npj)),
        ],
        out_specs=pl.BlockSpec((m, tn), lambda n, kk: (0, _block_pos(n // npj, perm) * npj + n % npj)),
        out_shape=jax.ShapeDtypeStruct((m, j * nj), out_dtype),
        scratch=[pltpu.VMEM((m, tn), F32)] if nk > 1 else [],
        sem=("parallel", "arbitrary"), name=name)


def mm_nn_blocks(a, w, blocks, into, *, tn, out_dtype, name, perm=None, dep=None):
    m, k = a.shape
    j, k2, nj = w.shape
    assert k == k2 and nj % tn == 0
    npj = nj // tn
    n_in = 2 if into is None else 3

    def body(blocks_ref, a_ref, w_ref, *rest):
        o_ref = rest[n_in - 2]
        for s, sz in _row_chunks(m):
            rows = pl.ds(s, sz)
            o_ref[rows, :] = jnp.dot(a_ref[rows, :], w_ref[...], preferred_element_type=F32).astype(out_dtype)

    return _pallas(
        body, [blocks, a, w] + ([] if into is None else [into]), dep=dep, n_prefetch=1,
        grid=(blocks.shape[0] * npj,),
        in_specs=[pl.BlockSpec((m, k), lambda n, bl: (0, 0)),
                  pl.BlockSpec((None, k, tn), lambda n, bl: (bl[n // npj], 0, n % npj))]
        + ([] if into is None else [_ANY]),
        out_specs=pl.BlockSpec((m, tn), lambda n, bl: (0, _block_pos(bl[n // npj], perm) * npj + n % npj)),
        out_shape=jax.ShapeDtypeStruct((m, j * nj), out_dtype),
        input_output_aliases={} if into is None else {3: 0},
        sem=("arbitrary",), name=name)


def mm_nt(g, w, *, tko, tn, name, out_dtype=F32, perm=None, dep=None):
    m, n = g.shape
    j, k, nj = w.shape
    assert n == j * nj and nj % tn == 0 and k % tko == 0
    npj, nr = nj // tn, n // tn
    in_place = out_dtype == F32

    def body(g_ref, w_ref, o_ref, *scratch):
        r = pl.program_id(1)
        acc = o_ref if in_place else (scratch[0] if nr > 1 else None)
        for s, sz in _row_chunks(m):
            rows = pl.ds(s, sz)
            p = lax.dot_general(g_ref[rows, :], w_ref[...], (((1,), (1,)), ((), ())), preferred_element_type=F32)
            if acc is None:
                o_ref[rows, :] = p.astype(out_dtype)
                continue

            @pl.when(r == 0)
            def _():
                acc[rows, :] = p

            @pl.when(r > 0)
            def _():
                acc[rows, :] += p

            if not in_place:
                @pl.when(r == nr - 1)
                def _():
                    o_ref[rows, :] = acc[rows, :].astype(out_dtype)

    return _pallas(
        body, [g, w], dep=dep, grid=(k // tko, nr),
        in_specs=[
            pl.BlockSpec((m, tn), lambda ko, r: (0, _block_pos(r // npj, perm) * npj + r % npj)),
            pl.BlockSpec((None, tko, tn), lambda ko, r: (r // npj, ko, r % npj)),
        ],
        out_specs=pl.BlockSpec((m, tko), lambda ko, r: (0, ko)),
        out_shape=jax.ShapeDtypeStruct((m, k), out_dtype),
        scratch=[pltpu.VMEM((m, tko), F32)] if (nr > 1 and not in_place) else [],
        sem=("parallel", "arbitrary"), name=name)


def mm_tn(a, g, n_blocks, *, tk, tn, name, perm=None, dep=None):
    m, k = a.shape
    m2, n = g.shape
    nj = n // n_blocks
    assert m == m2 and nj % tn == 0 and k % tk == 0
    npj = nj // tn

    def body(a_ref, g_ref, o_ref):
        for s, sz in _row_chunks(tk, LANES):
            p = lax.dot_general(a_ref[:, pl.ds(s, sz)], g_ref[...], (((0,), (0,)), ((), ())),
                                preferred_element_type=F32)
            o_ref[pl.ds(s, sz), :] = p.astype(BF16)

    return _pallas(
        body, [a, g], dep=dep, grid=(k // tk, n // tn),
        in_specs=[
            pl.BlockSpec((m, tk), lambda kk, nn: (0, kk)),
            pl.BlockSpec((m, tn), lambda kk, nn: (0, _block_pos(nn // npj, perm) * npj + nn % npj)),
        ],
        out_specs=pl.BlockSpec((None, tk, tn), lambda kk, nn: (nn // npj, kk, nn % npj)),
        out_shape=jax.ShapeDtypeStruct((n_blocks, k, nj), BF16),
        sem=("parallel", "parallel"), name=name)


ROW_TILE = 256


def _row_spec(tr, width):
    return pl.BlockSpec((tr, width), lambda i: (i, 0))


def _full_spec(shape):
    return pl.BlockSpec(shape, lambda *_: (0,) * len(shape))


def _rms(xv):
    return lax.rsqrt(jnp.mean(xv * xv, axis=-1, keepdims=True) + EPS)


def _mod_row(mod_ref, row):
    return mod_ref[pl.ds(row, 1), :]


def pre_mix_fwd(x, mod, gain, dep=None):
    t, d = x.shape
    tr = _tile(t, ROW_TILE, SUBLANES)

    def body(x_ref, mod_ref, g_ref, h_ref):
        xv = x_ref[...]
        y = xv * _rms(xv) * g_ref[...]
        h_ref[...] = (y * (1.0 + _mod_row(mod_ref, SC_M)) + _mod_row(mod_ref, SH_M)).astype(BF16)

    return _pallas(
        body, [x, mod, gain], dep=dep, grid=(t // tr,),
        in_specs=[_row_spec(tr, d), _full_spec(mod.shape), _full_spec(gain.shape)],
        out_specs=_row_spec(tr, d),
        out_shape=jax.ShapeDtypeStruct((t, d), BF16),
        sem=("parallel",), name="pre_mix_fwd")


def pre_ffn_fwd(x, o_m, mod, gain, dep=None):
    t, d = x.shape
    tr = _tile(t, ROW_TILE, SUBLANES)

    def body(x_ref, om_ref, mod_ref, g_ref, x1_ref, h_ref):
        x1 = x_ref[...] + _mod_row(mod_ref, GT_M) * om_ref[...]
        x1_ref[...] = x1
        y = x1 * _rms(x1) * g_ref[...]
        h_ref[...] = (y * (1.0 + _mod_row(mod_ref, SC_F)) + _mod_row(mod_ref, SH_F)).astype(BF16)

    return _pallas(
        body, [x, o_m, mod, gain], dep=dep, grid=(t // tr,),
        in_specs=[_row_spec(tr, d), _row_spec(tr, d), _full_spec(mod.shape), _full_spec(gain.shape)],
        out_specs=[_row_spec(tr, d), _row_spec(tr, d)],
        out_shape=[jax.ShapeDtypeStruct((t, d), F32), jax.ShapeDtypeStruct((t, d), BF16)],
        sem=("parallel",), name="pre_ffn_fwd")


def loss_head(x1, o_f, target, mod, dep=None):
    t, d = x1.shape
    tr = _tile(t, ROW_TILE, SUBLANES)

    def body(x1_ref, of_ref, tg_ref, mod_ref, loss_ref, dy_ref, dof_ref, acc_ref):
        i = pl.program_id(0)
        gt = _mod_row(mod_ref, GT_F)
        of = of_ref[...]
        err = x1_ref[...] + gt * of - tg_ref[...]
        dy = err * (1.0 / d)
        dy_ref[...] = dy
        dof_ref[...] = (dy * gt).astype(BF16)
        part = (0.5 / d) * jnp.sum(jnp.sum(err * err, axis=1, keepdims=True), axis=0, keepdims=True)
        dgt = jnp.sum(dy * of, axis=0, keepdims=True)

        @pl.when(i == 0)
        def _():
            loss_ref[...] = jnp.zeros_like(loss_ref)
            acc_ref[...] = jnp.zeros_like(acc_ref)

        loss_ref[...] += part
        acc_ref[pl.ds(0, 1), :] += dgt

    return _pallas(
        body, [x1, o_f, target, mod], dep=dep, grid=(t // tr,),
        in_specs=[_row_spec(tr, d), _row_spec(tr, d), _row_spec(tr, d), _full_spec(mod.shape)],
        out_specs=[_full_spec((1, 1)), _row_spec(tr, d), _row_spec(tr, d), _full_spec((SUBLANES, d))],
        out_shape=[jax.ShapeDtypeStruct((1, 1), F32), jax.ShapeDtypeStruct((t, d), F32),
                   jax.ShapeDtypeStruct((t, d), BF16), jax.ShapeDtypeStruct((SUBLANES, d), F32)],
        sem=("arbitrary",), name="loss_head")


def _norm_bwd(xv, dh, sc, gain):
    rstd = _rms(xv)
    yn = xv * rstd
    dsh = jnp.sum(dh, axis=0, keepdims=True)
    dsc = jnp.sum(dh * (yn * gain), axis=0, keepdims=True)
    dgain = jnp.sum(dh * (1.0 + sc) * yn, axis=0, keepdims=True)
    dyn = dh * ((1.0 + sc) * gain)
    dx = rstd * (dyn - yn * jnp.mean(dyn * yn, axis=-1, keepdims=True))
    return dx, dsh, dsc, dgain


def pre_ffn_bwd(x1, dh2, dy, o_m, mod, gain, dep=None):
    t, d = x1.shape
    tr = _tile(t, ROW_TILE, SUBLANES)

    def body(x1_ref, dh_ref, dy_ref, om_ref, mod_ref, g_ref, dx1_ref, dom_ref, acc_ref):
        i = pl.program_id(0)
        dxn, dsh, dsc, dgain = _norm_bwd(x1_ref[...], dh_ref[...], _mod_row(mod_ref, SC_F), g_ref[...])
        dx1 = dy_ref[...] + dxn
        dx1_ref[...] = dx1
        dom_ref[...] = (dx1 * _mod_row(mod_ref, GT_M)).astype(BF16)
        dgt = jnp.sum(dx1 * om_ref[...], axis=0, keepdims=True)

        @pl.when(i == 0)
        def _():
            acc_ref[...] = jnp.zeros_like(acc_ref)

        acc_ref[pl.ds(0, 1), :] += dsh
        acc_ref[pl.ds(1, 1), :] += dsc
        acc_ref[pl.ds(2, 1), :] += dgain
        acc_ref[pl.ds(3, 1), :] += dgt

    return _pallas(
        body, [x1, dh2, dy, o_m, mod, gain], dep=dep, grid=(t // tr,),
        in_specs=[_row_spec(tr, d)] * 4 + [_full_spec(mod.shape), _full_spec(gain.shape)],
        out_specs=[_row_spec(tr, d), _row_spec(tr, d), _full_spec((SUBLANES, d))],
        out_shape=[jax.ShapeDtypeStruct((t, d), F32), jax.ShapeDtypeStruct((t, d), BF16),
                   jax.ShapeDtypeStruct((SUBLANES, d), F32)],
        sem=("arbitrary",), name="pre_ffn_bwd")


def pre_mix_bwd(x, dh, dx1, mod, gain, dep=None):
    t, d = x.shape
    tr = _tile(t, ROW_TILE, SUBLANES)

    def body(x_ref, dh_ref, dx1_ref, mod_ref, g_ref, gx_ref, acc_ref):
        i = pl.program_id(0)
        dxn, dsh, dsc, dgain = _norm_bwd(x_ref[...], dh_ref[...], _mod_row(mod_ref, SC_M), g_ref[...])
        gx_ref[...] = dx1_ref[...] + dxn

        @pl.when(i == 0)
        def _():
            acc_ref[...] = jnp.zeros_like(acc_ref)

        acc_ref[pl.ds(0, 1), :] += dsh
        acc_ref[pl.ds(1, 1), :] += dsc
        acc_ref[pl.ds(2, 1), :] += dgain

    return _pallas(
        body, [x, dh, dx1, mod, gain], dep=dep, grid=(t // tr,),
        in_specs=[_row_spec(tr, d)] * 3 + [_full_spec(mod.shape), _full_spec(gain.shape)],
        out_specs=[_row_spec(tr, d), _full_spec((SUBLANES, d))],
        out_shape=[jax.ShapeDtypeStruct((t, d), F32), jax.ShapeDtypeStruct((SUBLANES, d), F32)],
        sem=("arbitrary",), name="pre_mix_bwd")


def merge_fwd(p, y_attn, y_conv, off_ga, off_gc, dep=None):
    t, d = y_attn.shape
    tr = _tile(t, ROW_TILE, SUBLANES)
    cw = math.gcd(math.gcd(off_ga, off_gc), math.gcd(d, 512))
    nc = d // cw

    def body(ga_ref, gc_ref, ya_ref, yc_ref, o_ref):
        o_ref[...] = (_sigmoid(ga_ref[...]) * ya_ref[...] + _sigmoid(gc_ref[...]) * yc_ref[...]).astype(BF16)

    return _pallas(
        body, [p, p, y_attn, y_conv], dep=dep, grid=(t // tr, nc),
        in_specs=[pl.BlockSpec((tr, cw), lambda i, j: (i, off_ga // cw + j)),
                  pl.BlockSpec((tr, cw), lambda i, j: (i, off_gc // cw + j)),
                  pl.BlockSpec((tr, cw), lambda i, j: (i, j)),
                  pl.BlockSpec((tr, cw), lambda i, j: (i, j))],
        out_specs=pl.BlockSpec((tr, cw), lambda i, j: (i, j)),
        out_shape=jax.ShapeDtypeStruct((t, d), BF16),
        sem=("parallel", "parallel"), name="merge_fwd")


def merge_bwd(p, y_attn, y_conv, dmerged, off_ga, off_gc, dep=None):
    t, d = y_attn.shape
    tr = _tile(t, ROW_TILE, SUBLANES)
    cw = math.gcd(math.gcd(off_ga, off_gc), math.gcd(d, 512))
    nc = d // cw

    def body(ga_ref, gc_ref, ya_ref, yc_ref, dm_ref, dya_ref, dyc_ref, dga_ref, dgc_ref):
        dm = dm_ref[...].astype(F32)
        sa = _sigmoid(ga_ref[...])
        sc = _sigmoid(gc_ref[...])
        dya_ref[...] = (dm * sa).astype(BF16)
        dyc_ref[...] = (dm * sc).astype(BF16)
        dga_ref[...] = (dm * ya_ref[...] * sa * (1.0 - sa)).astype(BF16)
        dgc_ref[...] = (dm * yc_ref[...] * sc * (1.0 - sc)).astype(BF16)

    blk = pl.BlockSpec((tr, cw), lambda i, j: (i, j))
    return _pallas(
        body, [p, p, y_attn, y_conv, dmerged], dep=dep, grid=(t // tr, nc),
        in_specs=[pl.BlockSpec((tr, cw), lambda i, j: (i, off_ga // cw + j)),
                  pl.BlockSpec((tr, cw), lambda i, j: (i, off_gc // cw + j)), blk, blk, blk],
        out_specs=[blk] * 4,
        out_shape=[jax.ShapeDtypeStruct((t, d), BF16)] * 4,
        sem=("parallel", "parallel"), name="merge_bwd")


def ffn_perm(n_blocks):
    half = n_blocks // 2
    return tuple(2 * j if j < half else 2 * (j - half) + 1 for j in range(n_blocks))


def swiglu_fwd(f, nj, dep=None):
    t, two = f.shape
    tr = _tile(t, ROW_TILE, SUBLANES)
    npair = two // (2 * nj)

    def body(f_ref, o_ref):
        g = f_ref[:, :nj].astype(F32)
        u = f_ref[:, nj:].astype(F32)
        o_ref[...] = (g * _sigmoid(g) * u).astype(BF16)

    return _pallas(
        body, [f], dep=dep, grid=(t // tr, npair),
        in_specs=[pl.BlockSpec((tr, 2 * nj), lambda i, j: (i, j))],
        out_specs=pl.BlockSpec((tr, nj), lambda i, j: (i, j)),
        out_shape=jax.ShapeDtypeStruct((t, two // 2), BF16),
        sem=("parallel", "parallel"), name="swiglu_fwd")


def swiglu_bwd(f, dact, nj, dep=None):
    t, two = f.shape
    tr = _tile(t, ROW_TILE, SUBLANES)
    npair = two // (2 * nj)

    def body(f_ref, da_ref, o_ref):
        g = f_ref[:, :nj].astype(F32)
        u = f_ref[:, nj:].astype(F32)
        da = da_ref[...]
        s = _sigmoid(g)
        o_ref[:, :nj] = (da * u * (s * (1.0 + g * (1.0 - s)))).astype(BF16)
        o_ref[:, nj:] = (da * (g * s)).astype(BF16)

    return _pallas(
        body, [f, dact], dep=dep, grid=(t // tr, npair),
        in_specs=[pl.BlockSpec((tr, 2 * nj), lambda i, j: (i, j)), pl.BlockSpec((tr, nj), lambda i, j: (i, j))],
        out_specs=pl.BlockSpec((tr, 2 * nj), lambda i, j: (i, j)),
        out_shape=jax.ShapeDtypeStruct((t, two), BF16),
        sem=("parallel", "parallel"), name="swiglu_bwd")


def _t5_bucket_table():
    q_off = np.arange(BLOCK)
    k_off = np.arange(2 * BLOCK)
    dist = q_off[:, None] + BLOCK - k_off[None, :]
    n = np.maximum(dist, 0)
    nf = np.maximum(n, 1).astype(np.float32)
    large = MAX_EXACT + (np.log(nf / np.float32(MAX_EXACT)) / np.float32(math.log(MAX_DISTANCE / MAX_EXACT))
                         * np.float32(NUM_BUCKETS - MAX_EXACT)).astype(np.int32)
    large = np.minimum(large, NUM_BUCKETS - 1)
    bucket = np.where(n < MAX_EXACT, n, large).astype(np.int32)
    allowed = (dist >= 0) & (dist < WINDOW)
    return np.where(allowed, bucket, -1).astype(np.int32)


def bias_table(rel_bias, bucket_p, bucket_c, dep=None):
    nb, nq = rel_bias.shape

    def body(rb_ref, bkp_ref, bkc_ref, op_ref, oc_ref):
        for bk_ref, o_ref in ((bkp_ref, op_ref), (bkc_ref, oc_ref)):
            bk = bk_ref[...]
            for h in range(nq):
                acc = jnp.full(bk.shape, -jnp.inf, F32)
                for b in range(nb):
                    acc = jnp.where(bk == b, rb_ref[b, h], acc)
                o_ref[h] = acc

    return _pallas(
        body, [rel_bias, bucket_p, bucket_c], dep=dep,
        in_specs=[_SMEM, _VMEM, _VMEM], out_specs=[_VMEM, _VMEM],
        out_shape=[jax.ShapeDtypeStruct((nq,) + bucket_p.shape, F32)] * 2,
        name="bias_table")


def bias_table_bwd(dbp, dbc, bucket_p, bucket_c, dep=None):
    nq = dbp.shape[0]

    def body(dbp_ref, dbc_ref, bkp_ref, bkc_ref, o_ref):
        bkp, bkc = bkp_ref[...][None], bkc_ref[...][None]
        dp, dc = dbp_ref[...], dbc_ref[...]
        for b in range(NUM_BUCKETS):
            sel = jnp.where(bkp == b, dp, 0.0) + jnp.where(bkc == b, dc, 0.0)
            o_ref[b] = jnp.sum(jnp.sum(sel, axis=2, keepdims=True), axis=1, keepdims=True)

    return _pallas(
        body, [dbp, dbc, bucket_p, bucket_c], dep=dep,
        in_specs=[_VMEM] * 4, out_specs=_VMEM,
        out_shape=jax.ShapeDtypeStruct((NUM_BUCKETS, nq, 1, 1), F32),
        name="bias_table_bwd")


_BNT = (((2,), (2,)), ((0,), (0,)))
_BNN = (((2,), (1,)), ((0,), (0,)))
_BTN = (((1,), (1,)), ((0,), (0,)))


@jax.custom_vjp
def _bdot_nt(a, b):
    return lax.dot_general(a.astype(BF16), b.astype(BF16), _BNT, preferred_element_type=F32)


def _bdot_nt_fwd(a, b):
    return _bdot_nt(a, b), (a, b)


def _bdot_nt_bwd(res, g):
    a, b = res
    gb = g.astype(BF16)
    da = lax.dot_general(gb, b.astype(BF16), _BNN, preferred_element_type=F32)
    db = lax.dot_general(gb, a.astype(BF16), _BTN, preferred_element_type=F32)
    return da, db


_bdot_nt.defvjp(_bdot_nt_fwd, _bdot_nt_bwd)


@jax.custom_vjp
def _bdot_nn(a, b):
    return lax.dot_general(a.astype(BF16), b.astype(BF16), _BNN, preferred_element_type=F32)


def _bdot_nn_fwd(a, b):
    return _bdot_nn(a, b), (a, b)


def _bdot_nn_bwd(res, g):
    a, b = res
    gb = g.astype(BF16)
    da = lax.dot_general(gb, b.astype(BF16), _BNT, preferred_element_type=F32)
    db = lax.dot_general(a.astype(BF16), gb, _BTN, preferred_element_type=F32)
    return da, db


_bdot_nn.defvjp(_bdot_nn_fwd, _bdot_nn_bwd)


def _attn_math(q, kp, kc, vp, vc, bp, bc, sinks, qg, kg, *, prev_ok, scale):
    h, rows, _ = q.shape
    b = kp.shape[1]
    qn = q * _rms(q) * qg
    kpn = kp * _rms(kp) * kg
    kcn = kc * _rms(kc) * kg
    lp = _bdot_nt(qn, kpn) * scale + bp.reshape(h, rows, b)
    lc = _bdot_nt(qn, kcn) * scale + bc.reshape(h, rows, b)
    lp = jnp.where(prev_ok, lp, -jnp.inf)
    sink = jnp.broadcast_to(sinks, (sinks.shape[0], b, 1)).reshape(h, rows, 1)
    m = jnp.maximum(jnp.maximum(jnp.max(lp, axis=-1, keepdims=True), jnp.max(lc, axis=-1, keepdims=True)), sink)
    m = lax.stop_gradient(m)
    pp = jnp.exp(lp - m)
    pc = jnp.exp(lc - m)
    den = jnp.sum(pp, axis=-1, keepdims=True) + jnp.sum(pc, axis=-1, keepdims=True) + jnp.exp(sink - m)
    inv = 1.0 / den
    return _bdot_nn(pp * inv, vp) + _bdot_nn(pc * inv, vc)


def _attn_specs(p, aw, kvw, nq, hd, nblk, reverse):
    assert aw % (2 * kvw) == 0
    kv_col = aw // (2 * kvw)

    def blk(n):
        return nblk - 1 - n if reverse else n

    return [
        pl.BlockSpec((BLOCK, aw), lambda n: (blk(n), 0)),
        pl.BlockSpec((BLOCK, 2 * kvw), lambda n: (jnp.maximum(blk(n) - 1, 0), kv_col)),
        pl.BlockSpec((BLOCK, 2 * kvw), lambda n: (blk(n), kv_col)),
        _full_spec((nq, BLOCK, BLOCK)), _full_spec((nq, BLOCK, BLOCK)), _full_spec((nq, 1, 1)),
        _full_spec((1, hd)), _full_spec((1, hd)),
    ]


def _head_major(ref, n_heads, grp, hd, offset=0):
    return jnp.stack([
        jnp.concatenate([ref[:, pl.ds(offset + (grp * h + g) * hd, hd)].astype(F32) for g in range(grp)], axis=0)
        for h in range(n_heads)])


def _attn_inputs(nkv, grp, hd, kvw, q_ref, kvp_ref, kvc_ref):
    return (_head_major(q_ref, nkv, grp, hd), _head_major(kvp_ref, nkv, 1, hd), _head_major(kvc_ref, nkv, 1, hd),
            _head_major(kvp_ref, nkv, 1, hd, kvw), _head_major(kvc_ref, nkv, 1, hd, kvw))


def attn_fwd(p, bias_p, bias_c, sinks, qg, kg, *, aw, kvw, dep=None):
    t, hd = p.shape[0], qg.shape[-1]
    nq, nkv, nblk = aw // hd, kvw // hd, t // BLOCK
    grp = nq // nkv
    scale = hd ** -0.5

    def body(q_ref, kvp_ref, kvc_ref, bp_ref, bc_ref, s_ref, qg_ref, kg_ref, o_ref):
        prev_ok = pl.program_id(0) > 0
        out = _attn_math(*_attn_inputs(nkv, grp, hd, kvw, q_ref, kvp_ref, kvc_ref), bp_ref[...], bc_ref[...],
                         s_ref[...], qg_ref[...], kg_ref[...], prev_ok=prev_ok, scale=scale)
        for h in range(nkv):
            for g in range(grp):
                o_ref[:, pl.ds((grp * h + g) * hd, hd)] = out[h, g * BLOCK:(g + 1) * BLOCK].astype(BF16)

    return _pallas(
        body, [p, p, p, bias_p, bias_c, sinks, qg, kg], dep=dep, grid=(nblk,),
        in_specs=_attn_specs(p, aw, kvw, nq, hd, nblk, False),
        out_specs=pl.BlockSpec((BLOCK, aw), lambda n: (n, 0)),
        out_shape=jax.ShapeDtypeStruct((t, aw), BF16),
        sem=("parallel",), name="attn_fwd")


def attn_bwd(p, bias_p, bias_c, sinks, qg, kg, do, *, aw, kvw, dep=None):
    t, hd = p.shape[0], qg.shape[-1]
    nq, nkv, nblk = aw // hd, kvw // hd, t // BLOCK
    grp = nq // nkv
    scale = hd ** -0.5

    def body(q_ref, kvp_ref, kvc_ref, bp_ref, bc_ref, s_ref, qg_ref, kg_ref, do_ref,
             dqkv_ref, dbp_ref, dbc_ref, ds_ref, dqg_ref, dkg_ref, carry):
        i = pl.program_id(0)
        prev_ok = (nblk - 1 - i) > 0

        @pl.when(i == 0)
        def _():
            carry[...] = jnp.zeros_like(carry)
            dbp_ref[...] = jnp.zeros_like(dbp_ref)
            dbc_ref[...] = jnp.zeros_like(dbc_ref)
            ds_ref[...] = jnp.zeros_like(ds_ref)
            dqg_ref[...] = jnp.zeros_like(dqg_ref)
            dkg_ref[...] = jnp.zeros_like(dkg_ref)

        fn = functools.partial(_attn_math, prev_ok=prev_ok, scale=scale)
        _, vjp = jax.vjp(fn, *_attn_inputs(nkv, grp, hd, kvw, q_ref, kvp_ref, kvc_ref), bp_ref[...], bc_ref[...],
                         s_ref[...], qg_ref[...], kg_ref[...])
        dq, dkp, dkc, dvp, dvc, dbp, dbc, dsk, dqg, dkg = vjp(_head_major(do_ref, nkv, grp, hd))
        for h in range(nkv):
            for g in range(grp):
                dqkv_ref[:, pl.ds((grp * h + g) * hd, hd)] = dq[h, g * BLOCK:(g + 1) * BLOCK].astype(BF16)
            k_cols, v_cols = pl.ds(h * hd, hd), pl.ds(kvw + h * hd, hd)
            dqkv_ref[:, pl.ds(aw + h * hd, hd)] = (dkc[h] + carry[:, k_cols]).astype(BF16)
            dqkv_ref[:, pl.ds(aw + kvw + h * hd, hd)] = (dvc[h] + carry[:, v_cols]).astype(BF16)
            carry[:, k_cols] = dkp[h]
            carry[:, v_cols] = dvp[h]
        dbp_ref[...] += dbp
        dbc_ref[...] += dbc
        ds_ref[...] += dsk
        dqg_ref[...] += dqg
        dkg_ref[...] += dkg

    return _pallas(
        body, [p, p, p, bias_p, bias_c, sinks, qg, kg, do], dep=dep, grid=(nblk,),
        in_specs=_attn_specs(p, aw, kvw, nq, hd, nblk, True)
        + [pl.BlockSpec((BLOCK, aw), lambda n: (nblk - 1 - n, 0))],
        out_specs=[
            pl.BlockSpec((BLOCK, aw + 2 * kvw), lambda n: (nblk - 1 - n, 0)),
            _full_spec((nq, BLOCK, BLOCK)), _full_spec((nq, BLOCK, BLOCK)), _full_spec((nq, 1, 1)),
            _full_spec((1, hd)), _full_spec((1, hd)),
        ],
        out_shape=[
            jax.ShapeDtypeStruct((t, aw + 2 * kvw), BF16),
            jax.ShapeDtypeStruct((nq, BLOCK, BLOCK), F32),
            jax.ShapeDtypeStruct((nq, BLOCK, BLOCK), F32),
            jax.ShapeDtypeStruct((nq, 1, 1), F32),
            jax.ShapeDtypeStruct((1, hd), F32),
            jax.ShapeDtypeStruct((1, hd), F32),
        ],
        scratch=[pltpu.VMEM((BLOCK, 2 * kvw), F32)],
        sem=("arbitrary",), name="attn_bwd")


CONV_TILE = 256


def _conv_halo_specs(tb, ch, nblk):
    per = tb // CONV_HALO
    last = nblk * per - 1
    cur = pl.BlockSpec((tb, ch), lambda n: (n, 0))
    prev = pl.BlockSpec((CONV_HALO, ch), lambda n: (jnp.maximum(n * per - 1, 0), 0))
    nxt = pl.BlockSpec((CONV_HALO, ch), lambda n: (jnp.minimum((n + 1) * per, last), 0))
    return cur, prev, nxt


def _ln_silu(co, ln_g, ln_b):
    mu = jnp.mean(co, axis=-1, keepdims=True)
    cen = co - mu
    rstd = lax.rsqrt(jnp.mean(cen * cen, axis=-1, keepdims=True) + EPS)
    xhat = cen * rstd
    z = xhat * ln_g + ln_b
    return xhat, rstd, z


def _shifted_copies(src, shifted):
    rows = src.shape[0] - SUBLANES
    for r in range(1, SUBLANES):
        shifted[r, pl.ds(0, rows), :] = src[pl.ds(r, rows), :]


def _rows_from(src, shifted, start, n):
    r = start % SUBLANES
    if r == 0:
        return src[pl.ds(start, n), :]
    return shifted[r, pl.ds(start - r, n), :]


def conv_fwd(ca, cb, conv_w, conv_b, ln_g, ln_b, dep=None):
    t, ch = ca.shape
    tb = _tile(t, CONV_TILE, CONV_HALO)
    nblk = t // tb
    cur, prev, _ = _conv_halo_specs(tb, ch, nblk)
    lead = CONV_HALO - (CONV_WIDTH - 1)

    def body(ca_ref, cb_ref, cap_ref, cbp_ref, w_ref, b_ref, g_ref, bb_ref, s_ref, co_ref, ubuf, ushift):
        n = pl.program_id(0)
        halo = cap_ref[...] * _sigmoid(cbp_ref[...])
        ubuf[pl.ds(0, CONV_HALO), :] = jnp.where(n > 0, halo, 0.0)
        ubuf[pl.ds(CONV_HALO, tb), :] = ca_ref[...] * _sigmoid(cb_ref[...])
        _shifted_copies(ubuf, ushift)
        acc = jnp.broadcast_to(b_ref[...], (tb, ch))
        for k in range(CONV_WIDTH):
            acc = acc + w_ref[pl.ds(k, 1), :] * _rows_from(ubuf, ushift, lead + k, tb)
        co_ref[...] = acc
        _, _, z = _ln_silu(acc, g_ref[...], bb_ref[...])
        s_ref[...] = (z * _sigmoid(z)).astype(BF16)

    vec = _full_spec((1, ch))
    return _pallas(
        body, [ca, cb, ca, cb, conv_w, conv_b, ln_g, ln_b], dep=dep, grid=(nblk,),
        in_specs=[cur, cur, prev, prev, _full_spec(conv_w.shape), vec, vec, vec],
        out_specs=[cur, cur],
        out_shape=[jax.ShapeDtypeStruct((t, ch), BF16), jax.ShapeDtypeStruct((t, ch), F32)],
        scratch=[pltpu.VMEM((CONV_HALO + tb, ch), F32), pltpu.VMEM((SUBLANES, CONV_HALO + tb, ch), F32)],
        sem=("parallel",), name="conv_fwd")


def conv_bwd(ca, cb, co, ds, conv_w, ln_g, ln_b, dep=None):
    t, ch = ca.shape
    tb = _tile(t, CONV_TILE, CONV_HALO)
    nblk = t // tb
    cur, prev, nxt = _conv_halo_specs(tb, ch, nblk)
    lead = CONV_HALO - (CONV_WIDTH - 1)
    ext = tb + CONV_HALO

    def body(ca_ref, cb_ref, cap_ref, cbp_ref, co_ref, con_ref, ds_ref, dsn_ref, w_ref, g_ref, bb_ref,
             dca_ref, dcb_ref, dw_ref, dvec_ref, ubuf, dbuf, ushift, dshift):
        n = pl.program_id(0)
        is_last = n == nblk - 1
        sig_b = _sigmoid(cb_ref[...])
        cav = ca_ref[...].astype(F32)
        ubuf[pl.ds(0, CONV_HALO), :] = jnp.where(n > 0, cap_ref[...] * _sigmoid(cbp_ref[...]), 0.0)
        ubuf[pl.ds(CONV_HALO, tb), :] = cav * sig_b
        _shifted_copies(ubuf, ushift)
        co = jnp.concatenate([co_ref[...], con_ref[...]], axis=0)
        xhat, rstd, z = _ln_silu(co, g_ref[...], bb_ref[...])
        dsv = jnp.concatenate([ds_ref[...].astype(F32), jnp.where(is_last, 0.0, dsn_ref[...].astype(F32))], axis=0)
        sg = _sigmoid(z)
        dz = dsv * (sg * (1.0 + z * (1.0 - sg)))
        dxh = dz * g_ref[...]
        dco = rstd * (dxh - jnp.mean(dxh, axis=-1, keepdims=True)
                      - xhat * jnp.mean(dxh * xhat, axis=-1, keepdims=True))
        dbuf[...] = dco
        _shifted_copies(dbuf, dshift)

        @pl.when(n == 0)
        def _():
            dw_ref[...] = jnp.zeros_like(dw_ref)
            dvec_ref[...] = jnp.zeros_like(dvec_ref)

        dco_cur = dco[:tb]
        dvec_ref[pl.ds(0, 1), :] += jnp.sum(dco_cur, axis=0, keepdims=True)
        dvec_ref[pl.ds(1, 1), :] += jnp.sum(dz[:tb] * xhat[:tb], axis=0, keepdims=True)
        dvec_ref[pl.ds(2, 1), :] += jnp.sum(dz[:tb], axis=0, keepdims=True)
        du = jnp.zeros((tb, ch), F32)
        for k in range(CONV_WIDTH):
            du = du + w_ref[pl.ds(k, 1), :] * _rows_from(dbuf, dshift, CONV_WIDTH - 1 - k, tb)
            dw_ref[pl.ds(k, 1), :] += jnp.sum(dco_cur * _rows_from(ubuf, ushift, lead + k, tb), axis=0,
                                              keepdims=True)
        dca_ref[...] = (du * sig_b).astype(BF16)
        dcb_ref[...] = (du * cav * sig_b * (1.0 - sig_b)).astype(BF16)

    vec = _full_spec((1, ch))
    return _pallas(
        body, [ca, cb, ca, cb, co, co, ds, ds, conv_w, ln_g, ln_b], dep=dep, grid=(nblk,),
        in_specs=[cur, cur, prev, prev, cur, nxt, cur, nxt, _full_spec(conv_w.shape), vec, vec],
        out_specs=[cur, cur, _full_spec(conv_w.shape), _full_spec((SUBLANES, ch))],
        out_shape=[jax.ShapeDtypeStruct((t, ch), BF16), jax.ShapeDtypeStruct((t, ch), BF16),
                   jax.ShapeDtypeStruct(conv_w.shape, F32), jax.ShapeDtypeStruct((SUBLANES, ch), F32)],
        scratch=[pltpu.VMEM((CONV_HALO + tb, ch), F32), pltpu.VMEM((ext, ch), F32),
                 pltpu.VMEM((SUBLANES, CONV_HALO + tb, ch), F32), pltpu.VMEM((SUBLANES, ext, ch), F32)],
        sem=("arbitrary",), name="conv_bwd")


def ada_fwd(c_t, w_ada, dep=None):
    d, nc = w_ada.shape
    nex = c_t.shape[1]
    tn = _tile(nc, 512)

    def body(ct_ref, w_ref, o_ref):
        w = w_ref[...]
        ct = ct_ref[...]
        cact = ct * _sigmoid(ct)
        rows = [jnp.sum(w * cact[:, b:b + 1], axis=0, keepdims=True) for b in range(nex)]
        o_ref[...] = jnp.concatenate(rows, axis=0)

    return _pallas(
        body, [c_t, w_ada], dep=dep, grid=(nc // tn,),
        in_specs=[_full_spec(c_t.shape), pl.BlockSpec((d, tn), lambda j: (0, j))],
        out_specs=pl.BlockSpec((nex, tn), lambda j: (0, j)),
        out_shape=jax.ShapeDtypeStruct((nex, nc), F32),
        sem=("parallel",), name="ada_fwd")


def _adamw_math(w, g, m, v):
    m = ADAM_B1 * m + (1.0 - ADAM_B1) * g
    v = ADAM_B2 * v + (1.0 - ADAM_B2) * (g * g)
    m_hat = m / (1.0 - ADAM_B1 ** ADAM_STEP)
    v_hat = v / (1.0 - ADAM_B2 ** ADAM_STEP)
    delta = -ADAM_LR * (m_hat / (jnp.sqrt(v_hat) + ADAM_EPS) + ADAM_WD * w)
    return delta, m, v


def adamw(w, g, m, v, name, copy_grad=False, dep=None):
    r, n = w.shape
    tr, tn = _ew_tiles(r, n, elems=256 * 1024)
    n_out = 4 if copy_grad else 3

    def body(w_ref, g_ref, m_ref, v_ref, *outs):
        g = g_ref[...]
        if copy_grad:
            outs[0][...] = g
        outs[-3][...], outs[-2][...], outs[-1][...] = _adamw_math(w_ref[...], g, m_ref[...], v_ref[...])

    blk = pl.BlockSpec((tr, tn), lambda i, j: (i, j))
    return _pallas(
        body, [w, g, m, v], dep=dep, grid=(r // tr, n // tn),
        in_specs=[blk] * 4, out_specs=[blk] * n_out,
        out_shape=[jax.ShapeDtypeStruct((r, n), F32)] * n_out,
        sem=("parallel", "parallel"), name=name)


def ada_grad_adamw(c_t, dmod_cols, w, m, v, dep=None):
    d, nc = w.shape
    nex = c_t.shape[1]
    tr, tn = _ew_tiles(d, nc, elems=256 * 1024)

    def body(ct_ref, dm_ref, w_ref, m_ref, v_ref, g_ref, d_ref, nm_ref, nv_ref):
        ct = ct_ref[...]
        cact = ct * _sigmoid(ct)
        dm = dm_ref[...]
        g = cact[:, 0:1] * dm[0:1, :]
        for b in range(1, nex):
            g = g + cact[:, b:b + 1] * dm[b:b + 1, :]
        g_ref[...] = g
        d_ref[...], nm_ref[...], nv_ref[...] = _adamw_math(w_ref[...], g, m_ref[...], v_ref[...])

    blk = pl.BlockSpec((tr, tn), lambda i, j: (i, j))
    return _pallas(
        body, [c_t, dmod_cols, w, m, v], dep=dep, grid=(d // tr, nc // tn),
        in_specs=[pl.BlockSpec((tr, nex), lambda i, j: (i, 0)), pl.BlockSpec((nex, tn), lambda i, j: (0, j)),
                  blk, blk, blk],
        out_specs=[blk] * 4,
        out_shape=[jax.ShapeDtypeStruct((d, nc), F32)] * 4,
        sem=("parallel", "parallel"), name="ada_grad_adamw")


def _row_pack(parts):
    cols, offs, off = [], [], 0
    for p in parts:
        n = p.shape[1]
        width = -(-n // LANES) * LANES
        cols.append(jnp.pad(p, ((0, 0), (0, width - n))) if width != n else p)
        offs.append(off)
        off += width
    return jnp.concatenate(cols, axis=1), offs


def small_sum_adamw(gathered, offs, ws, ms, vs, extra_widths, dep=None):
    ndev = gathered.shape[0]
    npar = len(ws)

    def body(ga_ref, *refs):
        w_refs, m_refs, v_refs = refs[:npar], refs[npar:2 * npar], refs[2 * npar:3 * npar]
        outs = refs[3 * npar:]
        tot = ga_ref[0]
        for s in range(1, ndev):
            tot = tot + ga_ref[s]
        for i in range(npar):
            n = ws[i].shape[1]
            g = tot[:, offs[i]:offs[i] + n]
            outs[4 * i][...] = g
            outs[4 * i + 1][...], outs[4 * i + 2][...], outs[4 * i + 3][...] = _adamw_math(
                w_refs[i][...], g, m_refs[i][...], v_refs[i][...])
        for e, n in enumerate(extra_widths):
            off = offs[npar + e]
            outs[4 * npar + e][...] = tot[:, off:off + n]

    shapes = [jax.ShapeDtypeStruct(w.shape, F32) for w in ws for _ in range(4)]
    shapes += [jax.ShapeDtypeStruct((1, n), F32) for n in extra_widths]
    return _pallas(
        body, [gathered, *ws, *ms, *vs], dep=dep, in_specs=[_VMEM] * (1 + 3 * npar), out_specs=[_VMEM] * len(shapes),
        out_shape=shapes, name="small_sum_adamw")


def _position():
    return lax.axis_index("x"), lax.axis_index("y"), lax.axis_index("c")


def _other_chips(x, y):
    return [(1 - x, y), (x, 1 - y), (1 - x, 1 - y)]


def allgather_small(block, name, dep=None):
    def body(x_ref, out_ref, send_sems, recv_sems, local_sem):
        x, y, c = _position()
        me, sibling = (x, y, c), (x, y, 1 - c)
        chips = _other_chips(x, y)

        def slot(px, py, pc):
            return out_ref.at[4 * px + 2 * py + pc]

        def copy(k, block_of, to, src=None):
            return pltpu.make_async_remote_copy(
                src_ref=slot(*block_of) if src is None else src, dst_ref=slot(*block_of),
                send_sem=send_sems.at[k], recv_sem=recv_sems.at[k], device_id=to, device_id_type=MESH)

        mine = pltpu.make_async_copy(x_ref, slot(*me), local_sem)
        mine.start()
        first = [copy(0, me, sibling, src=x_ref)]
        first += [copy(1 + j, me, (*chip, c), src=x_ref) for j, chip in enumerate(chips)]
        for cp in first:
            cp.start()
        passed = [copy(4 + j, (*chip, c), sibling) for j, chip in enumerate(chips)]
        for j, chip in enumerate(chips):
            copy(1 + j, (*chip, c), me).wait_recv()
            passed[j].start()
        copy(0, sibling, me).wait_recv()
        for j, chip in enumerate(chips):
            copy(4 + j, (*chip, 1 - c), me).wait_recv()
        for cp in first + passed:
            cp.wait_send()
        mine.wait()

    return _pallas(
        body, [block], dep=dep,
        out_shape=jax.ShapeDtypeStruct((N_DEV, *block.shape), block.dtype),
        in_specs=[_VMEM], out_specs=_VMEM,
        scratch=[pltpu.SemaphoreType.DMA((7,)), pltpu.SemaphoreType.DMA((7,)), pltpu.SemaphoreType.DMA],
        name=name)


class Started(NamedTuple):
    send_sems: Any
    recv_sems: Any
    bufs: list


def exchange_start(name, bufs, n_copies, plan, dep=None):
    nb = len(bufs)

    def body(*refs):
        for cp in plan(refs[:nb], refs[nb], refs[nb + 1]):
            cp.start()

    outs = _pallas(
        body, [pltpu.with_memory_space_constraint(b, pltpu.HBM) for b in bufs], dep=dep, name=name,
        out_shape=(pltpu.SemaphoreType.DMA((n_copies,)), pltpu.SemaphoreType.DMA((n_copies,)),
                   *[pltpu.HBM(b.shape, b.dtype) for b in bufs]),
        in_specs=[_HBM] * nb,
        out_specs=(_SEM, _SEM, *[_HBM] * nb),
        input_output_aliases={i: 2 + i for i in range(nb)},
        compiler_params=pltpu.CompilerParams(has_side_effects=_EFFECT))
    return Started(outs[0], outs[1], list(outs[2:2 + nb]))


def exchange_wait(name, started, plan, bufs=None, dep=None):
    if bufs is not None:
        started = started._replace(bufs=list(bufs))
    nb = len(started.bufs)

    def body(*refs):
        for cp in plan(refs[:nb], refs[nb], refs[nb + 1]):
            cp.wait_send()
            cp.wait_recv()

    outs = _pallas(
        body, [*started.bufs, started.send_sems, started.recv_sems], dep=dep, name=name,
        out_shape=tuple(pltpu.HBM(b.shape, b.dtype) for b in started.bufs),
        in_specs=[_HBM] * nb + [_SEM, _SEM],
        out_specs=tuple([_HBM] * nb),
        input_output_aliases={i: i for i in range(nb)},
        compiler_params=pltpu.CompilerParams(has_side_effects=_EFFECT))
    return list(outs)


def _remote(src, dst, send_sems, recv_sems, i, to):
    return pltpu.make_async_remote_copy(src_ref=src, dst_ref=dst, send_sem=send_sems.at[i], recv_sem=recv_sems.at[i],
                                        device_id=to, device_id_type=MESH)


def _half_rows(buf_rows, chip_idx, pc):
    half = buf_rows // (2 * N_CHIPS)
    return pl.ds((2 * chip_idx + pc) * half, half)


ALL_PEERS = (0, 1, 2)


def plan_gather_ici(refs, send_sems, recv_sems, peers=ALL_PEERS):
    x, y, c = _position()
    chips = _other_chips(x, y)
    copies = []
    for k, ref in enumerate(refs):
        rows = ref.at[_half_rows(ref.shape[0], 2 * x + y, c), :]
        for i, j in enumerate(peers):
            copies.append(_remote(rows, rows, send_sems, recv_sems, len(peers) * k + i, (*chips[j], c)))
    return copies


def plan_gather_relay(refs, send_sems, recv_sems):
    x, y, c = _position()
    copies = []
    for k, ref in enumerate(refs):
        quarter = ref.shape[0] // (4 * N_CHIPS)
        for i, (src_chip, to) in enumerate((((1 - x, y), (x, 1 - y, c)), ((x, 1 - y), (1 - x, y, c)))):
            start = (2 * (2 * src_chip[0] + src_chip[1]) + c) * 2 * quarter + i * quarter
            rows = ref.at[pl.ds(start, quarter), :]
            copies.append(_remote(rows, rows, send_sems, recv_sems, 2 * k + i, to))
    return copies


def plan_gather_d2d(refs, send_sems, recv_sems, peers=ALL_PEERS):
    x, y, c = _position()
    chips = _other_chips(x, y)
    copies = []
    for k, ref in enumerate(refs):
        for i, j in enumerate(peers):
            px, py = chips[j]
            rows = ref.at[_half_rows(ref.shape[0], 2 * px + py, c), :]
            copies.append(_remote(rows, rows, send_sems, recv_sems, len(peers) * k + i, (x, y, 1 - c)))
    return copies


def plan_pair_exchange(refs, send_sems, recv_sems):
    x, y, c = _position()
    nw = len(refs) // 2
    copies = []
    for k in range(nw):
        for chip in range(N_CHIPS):
            copies.append(_remote(refs[k].at[chip, 1 - c], refs[nw + k].at[chip], send_sems, recv_sems,
                                  N_CHIPS * k + chip, (x, y, 1 - c)))
    return copies


def plan_chip_exchange(refs, send_sems, recv_sems):
    x, y, c = _position()
    nw = len(refs) // 2
    copies = []
    for k in range(nw):
        for j, (px, py) in enumerate(_other_chips(x, y)):
            copies.append(_remote(refs[k].at[2 * px + py], refs[nw + k].at[2 * x + y], send_sems, recv_sems,
                                  3 * k + j, (px, py, c)))
    return copies


def plan_pair_share(refs, send_sems, recv_sems):
    x, y, c = _position()
    return [_remote(ref.at[c], ref.at[c], send_sems, recv_sems, k, (x, y, 1 - c)) for k, ref in enumerate(refs)]


def cast_into_slot(src, slot, n_slots, name, dep=None):
    r, n = src.shape
    tr, tn = _ew_tiles(r, n, BF16_SUBLANES)

    def body(slot_ref, s_ref, o_ref):
        o_ref[...] = s_ref[...].astype(BF16)

    return _pallas(
        body, [slot, src], dep=dep, n_prefetch=1, grid=(r // tr, n // tn),
        in_specs=[pl.BlockSpec((tr, tn), lambda i, j, sl: (i, j))],
        out_specs=pl.BlockSpec((None, tr, tn), lambda i, j, sl: (sl[0], i, j)),
        out_shape=jax.ShapeDtypeStruct((n_slots, r, n), BF16),
        sem=("parallel", "parallel"), name=name)


def pair_sum(g, r, core, name, dep=None):
    nchip, _, h, n = g.shape
    th, tn = _ew_tiles(h, n, BF16_SUBLANES)

    def body(core_ref, g_ref, r_ref, o_ref):
        o_ref[...] = (g_ref[...].astype(F32) + r_ref[...].astype(F32)).astype(BF16)

    return _pallas(
        body, [core, g, r], dep=dep, n_prefetch=1, grid=(nchip, h // th, n // tn),
        in_specs=[pl.BlockSpec((None, None, th, tn), lambda a, i, j, cr: (a, cr[0], i, j)),
                  pl.BlockSpec((None, th, tn), lambda a, i, j, cr: (a, i, j))],
        out_specs=pl.BlockSpec((None, th, tn), lambda a, i, j, cr: (a, i, j)),
        out_shape=jax.ShapeDtypeStruct((nchip, h, n), BF16),
        sem=("parallel", "parallel", "parallel"), name=name)


def chip_sum(own, got, where, name, dep=None):
    nchip, h, n = got.shape
    th, tn = _ew_tiles(h, n, BF16_SUBLANES, elems=256 * 1024)

    def body(where_ref, own_ref, *rest):
        got_refs, o_ref = rest[:nchip], rest[nchip]
        chip = where_ref[0]
        acc = None
        for s in range(nchip):
            term = jnp.where(chip == s, own_ref[...], got_refs[s][...]).astype(F32)
            acc = term if acc is None else acc + term
        o_ref[...] = acc

    def got_spec(s):
        return pl.BlockSpec((None, th, tn), lambda i, j, wr: (jnp.where(wr[0] == s, (s + 1) % nchip, s), i, j))

    return _pallas(
        body, [where, own, *[got] * nchip], dep=dep, n_prefetch=1, grid=(h // th, n // tn),
        in_specs=[pl.BlockSpec((None, th, tn), lambda i, j, wr: (wr[0], i, j))]
        + [got_spec(s) for s in range(nchip)],
        out_specs=pl.BlockSpec((None, th, tn), lambda i, j, wr: (wr[1], i, j)),
        out_shape=jax.ShapeDtypeStruct((2, h, n), F32),
        sem=("parallel", "parallel"), name=name)


def kernel(x, c, w_ada, b_ada, norm_mix_g, w_in, q_norm_g, k_norm_g, attn_sinks, rel_bias, w_attn_out, conv_w, conv_b, conv_ln_g, conv_ln_b, w_conv_out, w_mix_out, norm_ffn_g, w_ffn_in, w_ffn_out, loss_target, m_w_ada, m_b_ada, m_norm_mix_g, m_w_in, m_q_norm_g, m_k_norm_g, m_attn_sinks, m_rel_bias, m_w_attn_out, m_conv_w, m_conv_b, m_conv_ln_g, m_conv_ln_b, m_w_conv_out, m_w_mix_out, m_norm_ffn_g, m_w_ffn_in, m_w_ffn_out, v_w_ada, v_b_ada, v_norm_mix_g, v_w_in, v_q_norm_g, v_k_norm_g, v_attn_sinks, v_rel_bias, v_w_attn_out, v_conv_w, v_conv_b, v_conv_ln_g, v_conv_ln_b, v_w_conv_out, v_w_mix_out, v_norm_ffn_g, v_w_ffn_in, v_w_ffn_out):
    run = InOrder()
    xi, yi, ci = _position()
    chip = 2 * xi + yi
    me = 2 * chip + ci
    chip_arr = chip.astype(jnp.int32).reshape(1)
    core_arr = ci.astype(jnp.int32).reshape(1)
    where_arr = jnp.stack([chip, ci]).astype(jnp.int32)

    xe, tgt = x[0], loss_target[0]
    t, d = xe.shape
    hd = q_norm_g.shape[-1]
    nq = attn_sinks.shape[-1]
    aw = nq * hd
    ch = conv_b.shape[-1]
    in_width = N_CHIPS * w_in.shape[-1]
    kvw = (in_width - aw - 2 * ch - 2 * d) // 2
    nkv = kvw // hd
    dff = N_CHIPS * w_ffn_out.shape[1]
    off_k, off_v, off_ca = aw, aw + kvw, aw + 2 * kvw
    off_cb, off_ga, off_gc = off_ca + ch, off_ca + 2 * ch, off_ca + 2 * ch + d
    nc_ada = w_ada.shape[-1]
    ch_loc = conv_w.shape[-1]
    nj_ffn = w_ffn_in.shape[-1]
    perm_ffn = ffn_perm(N_CHIPS)

    big = {"w_in": w_in[0], "w_attn_out": w_attn_out[0], "w_conv_out": w_conv_out[0], "w_mix_out": w_mix_out[0],
           "w_ffn_in": w_ffn_in[0], "w_ffn_out": w_ffn_out[0]}
    moments = {"w_in": (m_w_in, v_w_in), "w_attn_out": (m_w_attn_out, v_w_attn_out),
               "w_conv_out": (m_w_conv_out, v_w_conv_out), "w_mix_out": (m_w_mix_out, v_w_mix_out),
               "w_ffn_in": (m_w_ffn_in, v_w_ffn_in), "w_ffn_out": (m_w_ffn_out, v_w_ffn_out)}
    gather_groups = {"in": ["w_in"], "branch_out": ["w_attn_out", "w_conv_out"], "mix_out": ["w_mix_out"],
                     "ffn_in": ["w_ffn_in"], "ffn_out": ["w_ffn_out"]}
    grads, deltas, new_m, new_v = {}, {}, {}, {}

    def gather_cast(gname):
        bufs = []
        for n in gather_groups[gname]:
            r, ncol = big[n].shape
            bufs.append(run(cast_into_slot, big[n], chip_arr, N_CHIPS, "cast_" + n).reshape(N_CHIPS * r, ncol))
        return bufs

    def gather_ici_start(gname, bufs):
        return run(exchange_start, "gather_ici_start_" + gname, bufs, 3 * len(bufs), plan_gather_ici)

    def gather_pass_on(gname, ici):
        landed = run(exchange_wait, "gather_ici_wait_" + gname, ici, plan_gather_ici)
        return run(exchange_start, "gather_d2d_start_" + gname, landed, 3 * len(landed), plan_gather_d2d)

    def gathered(gname, d2d):
        outs = run(exchange_wait, "gather_d2d_wait_" + gname, d2d, plan_gather_d2d)
        return [o.reshape(N_CHIPS, *big[n].shape) for o, n in zip(outs, gather_groups[gname])]

    def rs_pair_start(gname, names, partials):
        blocks = [g.reshape(N_CHIPS, 2, big[n].shape[0] // 2, big[n].shape[1]) for n, g in zip(names, partials)]
        land = [lax.empty((N_CHIPS,) + b.shape[2:], BF16) for b in blocks]
        return run(exchange_start, "pair_exchange_start_" + gname, blocks + land, N_CHIPS * len(blocks),
                   plan_pair_exchange)

    def rs_chip_start(gname, names, pair):
        nw = len(names)
        outs = run(exchange_wait, "pair_exchange_wait_" + gname, pair, plan_pair_exchange)
        sums = [run(pair_sum, g, r, core_arr, "pair_sum_" + n) for n, g, r in zip(names, outs[:nw], outs[nw:])]
        land = [lax.empty(s.shape, BF16) for s in sums]
        return run(exchange_start, "chip_exchange_start_" + gname, sums + land, 3 * nw, plan_chip_exchange)

    def rs_share_start(gname, names, chipx):
        nw = len(names)
        outs = run(exchange_wait, "chip_exchange_wait_" + gname, chipx, plan_chip_exchange)
        halves = [run(chip_sum, s, r, where_arr, "chip_sum_" + n) for n, s, r in zip(names, outs[:nw], outs[nw:])]
        return run(exchange_start, "pair_share_start_" + gname, halves, nw, plan_pair_share)

    def rs_finish(gname, names, share):
        fulls = run(exchange_wait, "pair_share_wait_" + gname, share, plan_pair_share)
        for n, g2 in zip(names, fulls):
            g, dl, nm, nv = run(adamw, big[n], g2.reshape(big[n].shape), moments[n][0][0], moments[n][1][0],
                                "adamw_" + n, copy_grad=True)
            grads[n], deltas[n], new_m[n], new_v[n] = g[None], dl[None], nm[None], nv[None]

    near, far = (0, 1), (2,)
    plan_ici_near = functools.partial(plan_gather_ici, peers=near)
    plan_ici_far, n_far = plan_gather_relay, 2
    plan_d2d_near = functools.partial(plan_gather_d2d, peers=near)
    plan_d2d_far = functools.partial(plan_gather_d2d, peers=far)
    bufs_in = gather_cast("in")
    row1, offs1 = _row_pack([c, conv_w[0].reshape(1, CONV_WIDTH * ch_loc)])
    got1 = run(allgather_small, row1, "allgather_cond")
    ici_near = run(exchange_start, "gather_ici_start_in_near", bufs_in, len(near), plan_ici_near)
    c_all = got1[:, 0, :d]
    conv_w_full = got1[0::2, 0, offs1[1]:offs1[1] + CONV_WIDTH * ch_loc].reshape(N_CHIPS, CONV_WIDTH, ch_loc)
    conv_w_full = jnp.transpose(conv_w_full, (1, 0, 2)).reshape(CONV_WIDTH, ch)
    conv_w_pad = jnp.pad(conv_w_full, ((0, 1), (0, 0)))
    c_t = jnp.transpose(c_all)
    mod_cols = run(ada_fwd, c_t, w_ada[0])
    rest_bufs = {gname: gather_cast(gname) for gname in gather_groups if gname != "in"}
    got2 = run(allgather_small, mod_cols, "allgather_mod")
    mod_all = got2.reshape(N_CHIPS, 2, N_DEV, nc_ada)[:, 0]
    mod = lax.dynamic_slice_in_dim(mod_all, me, 1, axis=1).reshape(1, N_CHIPS * nc_ada) + b_ada
    mod = jnp.pad(mod.reshape(N_MOD, d), ((0, SUBLANES - N_MOD), (0, 0)))

    h = run(pre_mix_fwd, xe, mod, norm_mix_g)
    bucket = _t5_bucket_table()
    bucket_p, bucket_c = jnp.asarray(bucket[:, :BLOCK]), jnp.asarray(bucket[:, BLOCK:])
    bias_p, bias_c = run(bias_table, rel_bias, bucket_p, bucket_c)

    def in_blocks(buf):
        return buf.reshape(N_CHIPS, *big["w_in"].shape)

    def chip_ids(peers):
        others = [2 * (1 - xi) + yi, 2 * xi + (1 - yi), 2 * (1 - xi) + (1 - yi)]
        return jnp.stack([others[j] for j in peers]).astype(jnp.int32)

    tn_in = big["w_in"].shape[1]
    landed = run(exchange_wait, "gather_ici_wait_in_near", ici_near, plan_ici_near)
    ici_far = run(exchange_start, "gather_ici_start_in_far", landed, n_far, plan_ici_far)
    ici = {gname: gather_ici_start(gname, rest_bufs[gname]) for gname in ("branch_out", "mix_out")}
    ici_near_ffn = run(exchange_start, "gather_ici_start_ffn_in_near", rest_bufs["ffn_in"], len(near), plan_ici_near)
    d2d_near = run(exchange_start, "gather_d2d_start_in_near", ici_far.bufs, len(near), plan_d2d_near)
    p = run(mm_nn_blocks, h, in_blocks(d2d_near.bufs[0]), chip_arr, None, tn=tn_in, out_dtype=BF16, name="mm_in_own")
    landed = run(exchange_wait, "gather_d2d_wait_in_near", d2d_near, plan_d2d_near)
    p = run(mm_nn_blocks, h, in_blocks(landed[0]), chip_ids(near), p, tn=tn_in, out_dtype=BF16, name="mm_in_near")
    landed = run(exchange_wait, "gather_ici_wait_in_far", ici_far, plan_ici_far, bufs=landed)
    d2d_far = run(exchange_start, "gather_d2d_start_in_far", landed, len(far), plan_d2d_far)
    landed = run(exchange_wait, "gather_d2d_wait_in_far", d2d_far, plan_d2d_far)
    wg_in = in_blocks(landed[0])
    p = run(mm_nn_blocks, h, wg_in, chip_ids(far), p, tn=tn_in, out_dtype=BF16, name="mm_in_far")
    d2d_branch = gather_pass_on("branch_out", ici["branch_out"])

    sinks3 = attn_sinks.reshape(nq, 1, 1)
    attn_o = run(attn_fwd, p, bias_p, bias_c, sinks3, q_norm_g, k_norm_g, aw=aw, kvw=kvw)
    ca, cb = p[:, off_ca:off_cb], p[:, off_cb:off_ga]
    s_conv, co_conv = run(conv_fwd, ca, cb, conv_w_pad, conv_b, conv_ln_g, conv_ln_b)
    wg_attn_out, wg_conv_out = gathered("branch_out", d2d_branch)
    y_attn = run(mm_nn, attn_o, wg_attn_out, tn=_tile(wg_attn_out.shape[2], 512), tk=aw, out_dtype=BF16,
                 name="mm_attn_out")
    y_conv = run(mm_nn, s_conv, wg_conv_out, tn=_tile(wg_conv_out.shape[2], 512), tk=ch, out_dtype=BF16,
                 name="mm_conv_out")
    landed = run(exchange_wait, "gather_ici_wait_ffn_in_near", ici_near_ffn, plan_ici_near)
    ici_far_ffn = run(exchange_start, "gather_ici_start_ffn_in_far", landed, n_far, plan_ici_far)
    ici["ffn_out"] = gather_ici_start("ffn_out", rest_bufs["ffn_out"])
    d2d_near_ffn = run(exchange_start, "gather_d2d_start_ffn_in_near", ici_far_ffn.bufs, len(near), plan_d2d_near)
    merged = run(merge_fwd, p, y_attn, y_conv, off_ga, off_gc)
    d2d_mix = gather_pass_on("mix_out", ici["mix_out"])
    (wg_mix_out,) = gathered("mix_out", d2d_mix)
    wg_mix_out = wg_mix_out.reshape(1, d, d)
    o_m = run(mm_nn, merged, wg_mix_out, tn=_tile(d, 512), tk=d, out_dtype=F32, name="mm_mix_out")
    x1, h2 = run(pre_ffn_fwd, xe, o_m, mod, norm_ffn_g)

    def ffn_blocks(buf):
        return buf.reshape(N_CHIPS, *big["w_ffn_in"].shape)

    tn_ffn = _tile(nj_ffn, 1408)
    landed = run(exchange_wait, "gather_d2d_wait_ffn_in_near", d2d_near_ffn, plan_d2d_near)
    f = run(mm_nn_blocks, h2, ffn_blocks(landed[0]), chip_arr, None, tn=tn_ffn, out_dtype=BF16, perm=perm_ffn,
            name="mm_ffn_in_own")
    landed = run(exchange_wait, "gather_ici_wait_ffn_in_far", ici_far_ffn, plan_ici_far, bufs=landed)
    d2d_far_ffn = run(exchange_start, "gather_d2d_start_ffn_in_far", landed, len(far), plan_d2d_far)
    f = run(mm_nn_blocks, h2, ffn_blocks(d2d_far_ffn.bufs[0]), chip_ids(near), f, tn=tn_ffn, out_dtype=BF16,
            perm=perm_ffn, name="mm_ffn_in_near")
    landed = run(exchange_wait, "gather_d2d_wait_ffn_in_far", d2d_far_ffn, plan_d2d_far)
    wg_ffn_in = ffn_blocks(landed[0])
    f = run(mm_nn_blocks, h2, wg_ffn_in, chip_ids(far), f, tn=tn_ffn, out_dtype=BF16, perm=perm_ffn,
            name="mm_ffn_in_far")
    d2d_ffn_out = gather_pass_on("ffn_out", ici["ffn_out"])
    act = run(swiglu_fwd, f, nj_ffn)
    (wg_ffn_out,) = gathered("ffn_out", d2d_ffn_out)
    wg_ffn_out = wg_ffn_out.reshape(1, dff, d)
    o_f = run(mm_nn, act, wg_ffn_out, tn=_tile(d, 512), tk=_tile(dff, 2816), out_dtype=F32, name="mm_ffn_out")
    loss11, dy, dof, acc_l = run(loss_head, x1, o_f, tgt, mod)

    gw_ffn_out = run(mm_tn, act, dof, 1, tk=_tile(dff, 512), tn=d, name="mm_ffn_out_dw")
    px_ffn_out = rs_pair_start("ffn_out", ["w_ffn_out"], [gw_ffn_out])
    dact = run(mm_nt, dof, wg_ffn_out, tko=_tile(dff, 512), tn=d, out_dtype=BF16, name="mm_ffn_out_dx")
    cx_ffn_out = rs_chip_start("ffn_out", ["w_ffn_out"], px_ffn_out)
    df = run(swiglu_bwd, f, dact, nj_ffn)
    gw_ffn_in = run(mm_tn, h2, df, N_CHIPS, tk=d, tn=_tile(nj_ffn, 1408), name="mm_ffn_in_dw",
                    perm=perm_ffn)
    px_ffn_in = rs_pair_start("ffn_in", ["w_ffn_in"], [gw_ffn_in])
    dh2 = run(mm_nt, df, wg_ffn_in, tko=_tile(d, 512), tn=nj_ffn, name="mm_ffn_in_dx", perm=perm_ffn)
    sh_ffn_out = rs_share_start("ffn_out", ["w_ffn_out"], cx_ffn_out)
    cx_ffn_in = rs_chip_start("ffn_in", ["w_ffn_in"], px_ffn_in)
    dx1, dom, acc_f = run(pre_ffn_bwd, x1, dh2, dy, o_m, mod, norm_ffn_g)
    gw_mix_out = run(mm_tn, merged, dom, 1, tk=d, tn=_tile(d, 1024), name="mm_mix_out_dw")
    px_mix = rs_pair_start("mix_out", ["w_mix_out"], [gw_mix_out])
    dmerged = run(mm_nt, dom, wg_mix_out, tko=_tile(d, 512), tn=d, out_dtype=BF16, name="mm_mix_out_dx")
    dy_attn, dy_conv, dga, dgc = run(merge_bwd, p, y_attn, y_conv, dmerged, off_ga, off_gc)
    rs_finish("ffn_out", ["w_ffn_out"], sh_ffn_out)
    cx_mix = rs_chip_start("mix_out", ["w_mix_out"], px_mix)
    gw_attn_out = run(mm_tn, attn_o, dy_attn, N_CHIPS, tk=aw, tn=_tile(wg_attn_out.shape[2], 512),
                      name="mm_attn_out_dw")
    gw_conv_out = run(mm_tn, s_conv, dy_conv, N_CHIPS, tk=ch, tn=_tile(wg_conv_out.shape[2], 512),
                      name="mm_conv_out_dw")
    ac_names = ["w_attn_out", "w_conv_out"]
    px_ac = rs_pair_start("attn_conv_out", ac_names, [gw_attn_out, gw_conv_out])
    dattn_o = run(mm_nt, dy_attn, wg_attn_out, tko=_tile(aw, 1024), tn=_tile(wg_attn_out.shape[2], 512),
                  out_dtype=BF16, name="mm_attn_out_dx")
    ds_conv = run(mm_nt, dy_conv, wg_conv_out, tko=_tile(ch, 1024), tn=_tile(wg_conv_out.shape[2], 512),
                  out_dtype=BF16, name="mm_conv_out_dx")
    cx_ac = rs_chip_start("attn_conv_out", ac_names, px_ac)
    dca, dcb, dconv_w, dconv_vec = run(conv_bwd, ca, cb, co_conv, ds_conv, conv_w_pad, conv_ln_g, conv_ln_b)
    sh_ffn_in = rs_share_start("ffn_in", ["w_ffn_in"], cx_ffn_in)
    dqkv, dbp, dbc, dsinks, dqg, dkg = run(attn_bwd, p, bias_p, bias_c, sinks3, q_norm_g, k_norm_g, dattn_o,
                                           aw=aw, kvw=kvw)
    sh_mix = rs_share_start("mix_out", ["w_mix_out"], cx_mix)
    sh_ac = rs_share_start("attn_conv_out", ac_names, cx_ac)
    drel = run(bias_table_bwd, dbp, dbc, bucket_p, bucket_c).reshape(NUM_BUCKETS, nq)
    dp = jnp.concatenate([dqkv, dca, dcb, dga, dgc], axis=1)
    gw_in = run(mm_tn, h, dp, N_CHIPS, tk=d, tn=wg_in.shape[2], name="mm_in_dw")
    px_in = rs_pair_start("in", ["w_in"], [gw_in])
    dh = run(mm_nt, dp, wg_in, tko=_tile(d, 1024), tn=wg_in.shape[2], name="mm_in_dx")
    grad_x, acc_m = run(pre_mix_bwd, xe, dh, dx1, mod, norm_mix_g)

    dmod = jnp.concatenate([acc_m[0:1], acc_m[1:2], acc_f[3:4], acc_f[0:1], acc_f[1:2], acc_l[0:1]], axis=1)
    small_names = ["b_ada", "norm_mix_g", "q_norm_g", "k_norm_g", "attn_sinks", "rel_bias", "conv_b", "conv_ln_g",
                   "conv_ln_b", "norm_ffn_g"]
    small_w = [b_ada, norm_mix_g, q_norm_g, k_norm_g, attn_sinks, rel_bias, conv_b, conv_ln_g, conv_ln_b, norm_ffn_g]
    small_m = [m_b_ada, m_norm_mix_g, m_q_norm_g, m_k_norm_g, m_attn_sinks, m_rel_bias, m_conv_b, m_conv_ln_g,
               m_conv_ln_b, m_norm_ffn_g]
    small_v = [v_b_ada, v_norm_mix_g, v_q_norm_g, v_k_norm_g, v_attn_sinks, v_rel_bias, v_conv_b, v_conv_ln_g,
               v_conv_ln_b, v_norm_ffn_g]
    small_g = [dmod, acc_m[2:3], dqg, dkg, dsinks.reshape(1, nq), drel.reshape(1, NUM_BUCKETS * nq),
               dconv_vec[0:1], dconv_vec[1:2], dconv_vec[2:3], acc_f[2:3]]
    row3, offs3 = _row_pack(small_g + [dconv_w[:CONV_WIDTH].reshape(1, CONV_WIDTH * ch), loss11])
    got3 = run(allgather_small, row3, "allgather_small_grads")
    cx_in = rs_chip_start("in", ["w_in"], px_in)
    as_row = lambda a: a.reshape(1, -1)
    outs3 = run(small_sum_adamw, got3, offs3, [as_row(a) for a in small_w], [as_row(a) for a in small_m],
                [as_row(a) for a in small_v], [CONV_WIDTH * ch, 1])
    for i, (n, w) in enumerate(zip(small_names, small_w)):
        grads[n], deltas[n], new_m[n], new_v[n] = (o.reshape(w.shape) for o in outs3[4 * i:4 * i + 4])
    g_conv_w_all, loss_sum = outs3[-2].reshape(CONV_WIDTH, ch), outs3[-1]

    g_conv_w = lax.dynamic_slice_in_dim(g_conv_w_all, chip * ch_loc, ch_loc, axis=1)
    grads["conv_w"] = g_conv_w[None]
    dl, nm, nv = run(adamw, conv_w[0], g_conv_w, m_conv_w[0], v_conv_w[0], "adamw_conv_w")
    deltas["conv_w"], new_m["conv_w"], new_v["conv_w"] = dl[None], nm[None], nv[None]

    dmod_all = got3[:, 0, :N_MOD * d]
    dmod_cols = lax.dynamic_slice_in_dim(dmod_all, chip * nc_ada, nc_ada, axis=1)
    g_ada, dl, nm, nv = run(ada_grad_adamw, c_t, dmod_cols, w_ada[0], m_w_ada[0], v_w_ada[0])
    grads["w_ada"], deltas["w_ada"], new_m["w_ada"], new_v["w_ada"] = g_ada[None], dl[None], nm[None], nv[None]

    rs_finish("ffn_in", ["w_ffn_in"], sh_ffn_in)
    rs_finish("mix_out", ["w_mix_out"], sh_mix)
    rs_finish("attn_conv_out", ac_names, sh_ac)
    sh_in = rs_share_start("in", ["w_in"], cx_in)
    rs_finish("in", ["w_in"], sh_in)

    loss = loss_sum[0, 0]
    order = ["w_ada", "b_ada", "norm_mix_g", "w_in", "q_norm_g", "k_norm_g", "attn_sinks", "rel_bias", "w_attn_out",
             "conv_w", "conv_b", "conv_ln_g", "conv_ln_b", "w_conv_out", "w_mix_out", "norm_ffn_g", "w_ffn_in",
             "w_ffn_out"]
    return (loss, grad_x[None], *[grads[n] for n in order], *[deltas[n] for n in order],
            *[new_m[n] for n in order], *[new_v[n] for n in order])
```

```python
import functools
import math
from typing import Any, NamedTuple

import jax
import jax.numpy as jnp
import numpy as np
from jax import lax
from jax.experimental import pallas as pl
from jax.experimental.pallas import tpu as pltpu

F32 = jnp.float32
BF16 = jnp.bfloat16
MESH = pl.DeviceIdType.MESH

V7X_VMEM_BYTES = 64 * 1024 * 1024
VMEM_LIMIT = V7X_VMEM_BYTES - 8 * 1024 * 1024
LANES = 128
SUBLANES = 8
BF16_SUBLANES = 16

EPS = 1e-6
WINDOW = 128
BLOCK = 128
NUM_BUCKETS = 32
MAX_EXACT = NUM_BUCKETS // 2
MAX_DISTANCE = 128
CONV_WIDTH = 31
CONV_HALO = 32
ADAM_LR = 0.001
ADAM_B1 = 0.9
ADAM_B2 = 0.999
ADAM_EPS = 1e-08
ADAM_WD = 0.01
ADAM_STEP = 10
N_MOD = 6
SH_M, SC_M, GT_M, SH_F, SC_F, GT_F = range(6)

N_CHIPS = 4
N_DEV = 8

_ANY = pl.BlockSpec(memory_space=pl.ANY)
_VMEM = pl.BlockSpec(memory_space=pltpu.VMEM)
_SMEM = pl.BlockSpec(memory_space=pltpu.SMEM)
_HBM = pl.BlockSpec(memory_space=pltpu.HBM)
_SEM = pl.BlockSpec(memory_space=pltpu.SEMAPHORE)
_EFFECT = pltpu.SideEffectType.DATAFLOW_SIDE_EFFECTING


class InOrder:
    def __init__(self):
        self.token = None

    def __call__(self, fn, *args, **kw):
        return fn(*args, dep=self, **kw)


def _pallas(body, args, *, in_specs, out_specs, out_shape, name, dep=None, grid=(), n_prefetch=0, scratch=(),
            sem=None, **kw):
    n_lead = n_prefetch + len(in_specs)
    in_specs, args = list(in_specs), list(args)
    single = not isinstance(out_shape, (list, tuple))
    out_shapes = [out_shape] if single else list(out_shape)
    out_specs = [out_specs] if single else list(out_specs)
    if dep is not None:
        inner, n_out, takes = body, len(out_shapes), dep.token is not None

        def body(*refs):
            rest = refs[n_lead + (1 if takes else 0):]
            rest[n_out][...] = jnp.zeros((SUBLANES, LANES), F32)
            return inner(*refs[:n_lead], *rest[:n_out], *rest[n_out + 1:])

        if takes:
            in_specs.append(_ANY)
            args.append(dep.token)
        out_shapes.append(jax.ShapeDtypeStruct((SUBLANES, LANES), F32))
        out_specs.append(pl.BlockSpec((SUBLANES, LANES), lambda *_: (0, 0)))
    params = kw.pop("compiler_params", None)
    if params is None:
        params = pltpu.CompilerParams(dimension_semantics=sem, vmem_limit_bytes=VMEM_LIMIT)
    outs = pl.pallas_call(
        body,
        grid_spec=pltpu.PrefetchScalarGridSpec(num_scalar_prefetch=n_prefetch, grid=grid, in_specs=in_specs,
                                               out_specs=out_specs, scratch_shapes=list(scratch)),
        out_shape=out_shapes, compiler_params=params, name=name, **kw,
    )(*args)
    if dep is not None:
        dep.token = outs[-1]
        outs = outs[:-1]
    return outs[0] if single else list(outs)


def _tile(n, pref, unit=LANES):
    best = None
    for t in range(unit, min(n, pref) + 1, unit):
        if n % t == 0:
            best = t
    return best if best is not None else n


def _sigmoid(v):
    return 1.0 / (1.0 + jnp.exp(-v.astype(F32)))


ROW_CHUNK = 512


def _row_chunks(m, unit=SUBLANES):
    step = _tile(m, ROW_CHUNK, unit)
    return [(s, step) for s in range(0, m, step)]


def _ew_tiles(r, n, unit=SUBLANES, elems=512 * 1024):
    return _tile(r, max(unit, elems // n), unit), n


def _block_pos(j, perm):
    if perm is None:
        return j
    pos = 0
    for a, p in enumerate(perm):
        pos = pos + jnp.where(j == a, p, 0)
    return pos


def mm_nn(a, w, *, tn, tk, out_dtype, name, perm=None, dep=None):
    m, k = a.shape
    j, k2, nj = w.shape
    assert k == k2 and nj % tn == 0 and k % tk == 0
    npj, nk = nj // tn, k // tk

    def body(a_ref, w_ref, o_ref, *scratch):
        kk = pl.program_id(1)
        for s, sz in _row_chunks(m):
            rows = pl.ds(s, sz)
            p = jnp.dot(a_ref[rows, :], w_ref[...], preferred_element_type=F32)
            if nk == 1:
                o_ref[rows, :] = p.astype(out_dtype)
            else:
                acc = scratch[0]

                @pl.when(kk == 0)
                def _():
                    acc[rows, :] = p

                @pl.when(kk > 0)
                def _():
                    acc[rows, :] += p

                @pl.when(kk == nk - 1)
                def _():
                    o_ref[rows, :] = acc[rows, :].astype(out_dtype)

    return _pallas(
        body, [a, w], dep=dep, grid=(j * npj, nk),
        in_specs=[
            pl.BlockSpec((m, tk), lambda n, kk: (0, kk)),
            pl.BlockSpec((None, tk, tn), lambda n, kk: (n // npj, kk, n % npj)),
        ],
        out_specs=pl.BlockSpec((m, tn), lambda n, kk: (0, _block_pos(n // npj, perm) * npj + n % npj)),
        out_shape=jax.ShapeDtypeStruct((m, j * nj), out_dtype),
        scratch=[pltpu.VMEM((m, tn), F32)] if nk > 1 else [],
        sem=("parallel", "arbitrary"), name=name)


def mm_nt(g, w, *, tko, tn, name, out_dtype=F32, perm=None, dep=None):
    m, n = g.shape
    j, k, nj = w.shape
    assert n == j * nj and nj % tn == 0 and k % tko == 0
    npj, nr = nj // tn, n // tn
    in_place = out_dtype == F32

    def body(g_ref, w_ref, o_ref, *scratch):
        r = pl.program_id(1)
        acc = o_ref if in_place else (scratch[0] if nr > 1 else None)
        for s, sz in _row_chunks(m):
            rows = pl.ds(s, sz)
            p = lax.dot_general(g_ref[rows, :], w_ref[...], (((1,), (1,)), ((), ())), preferred_element_type=F32)
            if acc is None:
                o_ref[rows, :] = p.astype(out_dtype)
                continue

            @pl.when(r == 0)
            def _():
                acc[rows, :] = p

            @pl.when(r > 0)
            def _():
                acc[rows, :] += p

            if not in_place:
                @pl.when(r == nr - 1)
                def _():
                    o_ref[rows, :] = acc[rows, :].astype(out_dtype)

    return _pallas(
        body, [g, w], dep=dep, grid=(k // tko, nr),
        in_specs=[
            pl.BlockSpec((m, tn), lambda ko, r: (0, _block_pos(r // npj, perm) * npj + r % npj)),
            pl.BlockSpec((None, tko, tn), lambda ko, r: (r // npj, ko, r % npj)),
        ],
        out_specs=pl.BlockSpec((m, tko), lambda ko, r: (0, ko)),
        out_shape=jax.ShapeDtypeStruct((m, k), out_dtype),
        scratch=[pltpu.VMEM((m, tko), F32)] if (nr > 1 and not in_place) else [],
        sem=("parallel", "arbitrary"), name=name)


def mm_tn(a, g, n_blocks, *, tk, tn, name, perm=None, dep=None):
    m, k = a.shape
    m2, n = g.shape
    nj = n // n_blocks
    assert m == m2 and nj % tn == 0 and k % tk == 0
    npj = nj // tn

    def body(a_ref, g_ref, o_ref):
        for s, sz in _row_chunks(tk, LANES):
            p = lax.dot_general(a_ref[:, pl.ds(s, sz)], g_ref[...], (((0,), (0,)), ((), ())),
                                preferred_element_type=F32)
            o_ref[pl.ds(s, sz), :] = p.astype(BF16)

    return _pallas(
        body, [a, g], dep=dep, grid=(k // tk, n // tn),
        in_specs=[
            pl.BlockSpec((m, tk), lambda kk, nn: (0, kk)),
            pl.BlockSpec((m, tn), lambda kk, nn: (0, _block_pos(nn // npj, perm) * npj + nn % npj)),
        ],
        out_specs=pl.BlockSpec((None, tk, tn), lambda kk, nn: (nn // npj, kk, nn % npj)),
        out_shape=jax.ShapeDtypeStruct((n_blocks, k, nj), BF16),
        sem=("parallel", "parallel"), name=name)


ROW_TILE = 256


def _row_spec(tr, width):
    return pl.BlockSpec((tr, width), lambda i: (i, 0))


def _full_spec(shape):
    return pl.BlockSpec(shape, lambda *_: (0,) * len(shape))


def _rms(xv):
    return lax.rsqrt(jnp.mean(xv * xv, axis=-1, keepdims=True) + EPS)


def _mod_row(mod_ref, row):
    return mod_ref[pl.ds(row, 1), :]


def pre_mix_fwd(x, mod, gain, dep=None):
    t, d = x.shape
    tr = _tile(t, ROW_TILE, SUBLANES)

    def body(x_ref, mod_ref, g_ref, h_ref):
        xv = x_ref[...]
        y = xv * _rms(xv) * g_ref[...]
        h_ref[...] = (y * (1.0 + _mod_row(mod_ref, SC_M)) + _mod_row(mod_ref, SH_M)).astype(BF16)

    return _pallas(
        body, [x, mod, gain], dep=dep, grid=(t // tr,),
        in_specs=[_row_spec(tr, d), _full_spec(mod.shape), _full_spec(gain.shape)],
        out_specs=_row_spec(tr, d),
        out_shape=jax.ShapeDtypeStruct((t, d), BF16),
        sem=("parallel",), name="pre_mix_fwd")


def pre_ffn_fwd(x, o_m, mod, gain, dep=None):
    t, d = x.shape
    tr = _tile(t, ROW_TILE, SUBLANES)

    def body(x_ref, om_ref, mod_ref, g_ref, x1_ref, h_ref):
        x1 = x_ref[...] + _mod_row(mod_ref, GT_M) * om_ref[...]
        x1_ref[...] = x1
        y = x1 * _rms(x1) * g_ref[...]
        h_ref[...] = (y * (1.0 + _mod_row(mod_ref, SC_F)) + _mod_row(mod_ref, SH_F)).astype(BF16)

    return _pallas(
        body, [x, o_m, mod, gain], dep=dep, grid=(t // tr,),
        in_specs=[_row_spec(tr, d), _row_spec(tr, d), _full_spec(mod.shape), _full_spec(gain.shape)],
        out_specs=[_row_spec(tr, d), _row_spec(tr, d)],
        out_shape=[jax.ShapeDtypeStruct((t, d), F32), jax.ShapeDtypeStruct((t, d), BF16)],
        sem=("parallel",), name="pre_ffn_fwd")


def loss_head(x1, o_f, target, mod, dep=None):
    t, d = x1.shape
    tr = _tile(t, ROW_TILE, SUBLANES)

    def body(x1_ref, of_ref, tg_ref, mod_ref, loss_ref, dy_ref, dof_ref, acc_ref):
        i = pl.program_id(0)
        gt = _mod_row(mod_ref, GT_F)
        of = of_ref[...]
        err = x1_ref[...] + gt * of - tg_ref[...]
        dy = err * (1.0 / d)
        dy_ref[...] = dy
        dof_ref[...] = (dy * gt).astype(BF16)
        part = (0.5 / d) * jnp.sum(jnp.sum(err * err, axis=1, keepdims=True), axis=0, keepdims=True)
        dgt = jnp.sum(dy * of, axis=0, keepdims=True)

        @pl.when(i == 0)
        def _():
            loss_ref[...] = jnp.zeros_like(loss_ref)
            acc_ref[...] = jnp.zeros_like(acc_ref)

        loss_ref[...] += part
        acc_ref[pl.ds(0, 1), :] += dgt

    return _pallas(
        body, [x1, o_f, target, mod], dep=dep, grid=(t // tr,),
        in_specs=[_row_spec(tr, d), _row_spec(tr, d), _row_spec(tr, d), _full_spec(mod.shape)],
        out_specs=[_full_spec((1, 1)), _row_spec(tr, d), _row_spec(tr, d), _full_spec((SUBLANES, d))],
        out_shape=[jax.ShapeDtypeStruct((1, 1), F32), jax.ShapeDtypeStruct((t, d), F32),
                   jax.ShapeDtypeStruct((t, d), BF16), jax.ShapeDtypeStruct((SUBLANES, d), F32)],
        sem=("arbitrary",), name="loss_head")


def _norm_bwd(xv, dh, sc, gain):
    rstd = _rms(xv)
    yn = xv * rstd
    dsh = jnp.sum(dh, axis=0, keepdims=True)
    dsc = jnp.sum(dh * (yn * gain), axis=0, keepdims=True)
    dgain = jnp.sum(dh * (1.0 + sc) * yn, axis=0, keepdims=True)
    dyn = dh * ((1.0 + sc) * gain)
    dx = rstd * (dyn - yn * jnp.mean(dyn * yn, axis=-1, keepdims=True))
    return dx, dsh, dsc, dgain


def pre_ffn_bwd(x1, dh2, dy, o_m, mod, gain, dep=None):
    t, d = x1.shape
    tr = _tile(t, ROW_TILE, SUBLANES)

    def body(x1_ref, dh_ref, dy_ref, om_ref, mod_ref, g_ref, dx1_ref, dom_ref, acc_ref):
        i = pl.program_id(0)
        dxn, dsh, dsc, dgain = _norm_bwd(x1_ref[...], dh_ref[...], _mod_row(mod_ref, SC_F), g_ref[...])
        dx1 = dy_ref[...] + dxn
        dx1_ref[...] = dx1
        dom_ref[...] = (dx1 * _mod_row(mod_ref, GT_M)).astype(BF16)
        dgt = jnp.sum(dx1 * om_ref[...], axis=0, keepdims=True)

        @pl.when(i == 0)
        def _():
            acc_ref[...] = jnp.zeros_like(acc_ref)

        acc_ref[pl.ds(0, 1), :] += dsh
        acc_ref[pl.ds(1, 1), :] += dsc
        acc_ref[pl.ds(2, 1), :] += dgain
        acc_ref[pl.ds(3, 1), :] += dgt

    return _pallas(
        body, [x1, dh2, dy, o_m, mod, gain], dep=dep, grid=(t // tr,),
        in_specs=[_row_spec(tr, d)] * 4 + [_full_spec(mod.shape), _full_spec(gain.shape)],
        out_specs=[_row_spec(tr, d), _row_spec(tr, d), _full_spec((SUBLANES, d))],
        out_shape=[jax.ShapeDtypeStruct((t, d), F32), jax.ShapeDtypeStruct((t, d), BF16),
                   jax.ShapeDtypeStruct((SUBLANES, d), F32)],
        sem=("arbitrary",), name="pre_ffn_bwd")


def pre_mix_bwd(x, dh, dx1, mod, gain, dep=None):
    t, d = x.shape
    tr = _tile(t, ROW_TILE, SUBLANES)

    def body(x_ref, dh_ref, dx1_ref, mod_ref, g_ref, gx_ref, acc_ref):
        i = pl.program_id(0)
        dxn, dsh, dsc, dgain = _norm_bwd(x_ref[...], dh_ref[...], _mod_row(mod_ref, SC_M), g_ref[...])
        gx_ref[...] = dx1_ref[...] + dxn

        @pl.when(i == 0)
        def _():
            acc_ref[...] = jnp.zeros_like(acc_ref)

        acc_ref[pl.ds(0, 1), :] += dsh
        acc_ref[pl.ds(1, 1), :] += dsc
        acc_ref[pl.ds(2, 1), :] += dgain

    return _pallas(
        body, [x, dh, dx1, mod, gain], dep=dep, grid=(t // tr,),
        in_specs=[_row_spec(tr, d)] * 3 + [_full_spec(mod.shape), _full_spec(gain.shape)],
        out_specs=[_row_spec(tr, d), _full_spec((SUBLANES, d))],
        out_shape=[jax.ShapeDtypeStruct((t, d), F32), jax.ShapeDtypeStruct((SUBLANES, d), F32)],
        sem=("arbitrary",), name="pre_mix_bwd")


def merge_fwd(p, y_attn, y_conv, off_ga, off_gc, dep=None):
    t, d = y_attn.shape
    tr = _tile(t, ROW_TILE, SUBLANES)
    cw = math.gcd(math.gcd(off_ga, off_gc), math.gcd(d, 512))
    nc = d // cw

    def body(ga_ref, gc_ref, ya_ref, yc_ref, o_ref):
        o_ref[...] = (_sigmoid(ga_ref[...]) * ya_ref[...] + _sigmoid(gc_ref[...]) * yc_ref[...]).astype(BF16)

    return _pallas(
        body, [p, p, y_attn, y_conv], dep=dep, grid=(t // tr, nc),
        in_specs=[pl.BlockSpec((tr, cw), lambda i, j: (i, off_ga // cw + j)),
                  pl.BlockSpec((tr, cw), lambda i, j: (i, off_gc // cw + j)),
                  pl.BlockSpec((tr, cw), lambda i, j: (i, j)),
                  pl.BlockSpec((tr, cw), lambda i, j: (i, j))],
        out_specs=pl.BlockSpec((tr, cw), lambda i, j: (i, j)),
        out_shape=jax.ShapeDtypeStruct((t, d), BF16),
        sem=("parallel", "parallel"), name="merge_fwd")


def merge_bwd(p, y_attn, y_conv, dmerged, off_ga, off_gc, dep=None):
    t, d = y_attn.shape
    tr = _tile(t, ROW_TILE, SUBLANES)
    cw = math.gcd(math.gcd(off_ga, off_gc), math.gcd(d, 512))
    nc = d // cw

    def body(ga_ref, gc_ref, ya_ref, yc_ref, dm_ref, dya_ref, dyc_ref, dga_ref, dgc_ref):
        dm = dm_ref[...].astype(F32)
        sa = _sigmoid(ga_ref[...])
        sc = _sigmoid(gc_ref[...])
        dya_ref[...] = (dm * sa).astype(BF16)
        dyc_ref[...] = (dm * sc).astype(BF16)
        dga_ref[...] = (dm * ya_ref[...] * sa * (1.0 - sa)).astype(BF16)
        dgc_ref[...] = (dm * yc_ref[...] * sc * (1.0 - sc)).astype(BF16)

    blk = pl.BlockSpec((tr, cw), lambda i, j: (i, j))
    return _pallas(
        body, [p, p, y_attn, y_conv, dmerged], dep=dep, grid=(t // tr, nc),
        in_specs=[pl.BlockSpec((tr, cw), lambda i, j: (i, off_ga // cw + j)),
                  pl.BlockSpec((tr, cw), lambda i, j: (i, off_gc // cw + j)), blk, blk, blk],
        out_specs=[blk] * 4,
        out_shape=[jax.ShapeDtypeStruct((t, d), BF16)] * 4,
        sem=("parallel", "parallel"), name="merge_bwd")


def ffn_perm(n_blocks):
    half = n_blocks // 2
    return tuple(2 * j if j < half else 2 * (j - half) + 1 for j in range(n_blocks))


def swiglu_fwd(f, nj, dep=None):
    t, two = f.shape
    tr = _tile(t, ROW_TILE, SUBLANES)
    npair = two // (2 * nj)

    def body(f_ref, o_ref):
        g = f_ref[:, :nj].astype(F32)
        u = f_ref[:, nj:].astype(F32)
        o_ref[...] = (g * _sigmoid(g) * u).astype(BF16)

    return _pallas(
        body, [f], dep=dep, grid=(t // tr, npair),
        in_specs=[pl.BlockSpec((tr, 2 * nj), lambda i, j: (i, j))],
        out_specs=pl.BlockSpec((tr, nj), lambda i, j: (i, j)),
        out_shape=jax.ShapeDtypeStruct((t, two // 2), BF16),
        sem=("parallel", "parallel"), name="swiglu_fwd")


def swiglu_bwd(f, dact, nj, dep=None):
    t, two = f.shape
    tr = _tile(t, ROW_TILE, SUBLANES)
    npair = two // (2 * nj)

    def body(f_ref, da_ref, o_ref):
        g = f_ref[:, :nj].astype(F32)
        u = f_ref[:, nj:].astype(F32)
        da = da_ref[...]
        s = _sigmoid(g)
        o_ref[:, :nj] = (da * u * (s * (1.0 + g * (1.0 - s)))).astype(BF16)
        o_ref[:, nj:] = (da * (g * s)).astype(BF16)

    return _pallas(
        body, [f, dact], dep=dep, grid=(t // tr, npair),
        in_specs=[pl.BlockSpec((tr, 2 * nj), lambda i, j: (i, j)), pl.BlockSpec((tr, nj), lambda i, j: (i, j))],
        out_specs=pl.BlockSpec((tr, 2 * nj), lambda i, j: (i, j)),
        out_shape=jax.ShapeDtypeStruct((t, two), BF16),
        sem=("parallel", "parallel"), name="swiglu_bwd")


def _t5_bucket_table():
    q_off = np.arange(BLOCK)
    k_off = np.arange(2 * BLOCK)
    dist = q_off[:, None] + BLOCK - k_off[None, :]
    n = np.maximum(dist, 0)
    nf = np.maximum(n, 1).astype(np.float32)
    large = MAX_EXACT + (np.log(nf / np.float32(MAX_EXACT)) / np.float32(math.log(MAX_DISTANCE / MAX_EXACT))
                         * np.float32(NUM_BUCKETS - MAX_EXACT)).astype(np.int32)
    large = np.minimum(large, NUM_BUCKETS - 1)
    bucket = np.where(n < MAX_EXACT, n, large).astype(np.int32)
    allowed = (dist >= 0) & (dist < WINDOW)
    return np.where(allowed, bucket, -1).astype(np.int32)


def bias_table(rel_bias, bucket_p, bucket_c, dep=None):
    nb, nq = rel_bias.shape

    def body(rb_ref, bkp_ref, bkc_ref, op_ref, oc_ref):
        for bk_ref, o_ref in ((bkp_ref, op_ref), (bkc_ref, oc_ref)):
            bk = bk_ref[...]
            for h in range(nq):
                acc = jnp.full(bk.shape, -jnp.inf, F32)
                for b in range(nb):
                    acc = jnp.where(bk == b, rb_ref[b, h], acc)
                o_ref[h] = acc

    return _pallas(
        body, [rel_bias, bucket_p, bucket_c], dep=dep,
        in_specs=[_SMEM, _VMEM, _VMEM], out_specs=[_VMEM, _VMEM],
        out_shape=[jax.ShapeDtypeStruct((nq,) + bucket_p.shape, F32)] * 2,
        name="bias_table")


def bias_table_bwd(dbp, dbc, bucket_p, bucket_c, dep=None):
    nq = dbp.shape[0]

    def body(dbp_ref, dbc_ref, bkp_ref, bkc_ref, o_ref):
        bkp, bkc = bkp_ref[...][None], bkc_ref[...][None]
        dp, dc = dbp_ref[...], dbc_ref[...]
        for b in range(NUM_BUCKETS):
            sel = jnp.where(bkp == b, dp, 0.0) + jnp.where(bkc == b, dc, 0.0)
            o_ref[b] = jnp.sum(jnp.sum(sel, axis=2, keepdims=True), axis=1, keepdims=True)

    return _pallas(
        body, [dbp, dbc, bucket_p, bucket_c], dep=dep,
        in_specs=[_VMEM] * 4, out_specs=_VMEM,
        out_shape=jax.ShapeDtypeStruct((NUM_BUCKETS, nq, 1, 1), F32),
        name="bias_table_bwd")


_BNT = (((2,), (2,)), ((0,), (0,)))
_BNN = (((2,), (1,)), ((0,), (0,)))
_BTN = (((1,), (1,)), ((0,), (0,)))


@jax.custom_vjp
def _bdot_nt(a, b):
    return lax.dot_general(a.astype(BF16), b.astype(BF16), _BNT, preferred_element_type=F32)


def _bdot_nt_fwd(a, b):
    return _bdot_nt(a, b), (a, b)


def _bdot_nt_bwd(res, g):
    a, b = res
    gb = g.astype(BF16)
    da = lax.dot_general(gb, b.astype(BF16), _BNN, preferred_element_type=F32)
    db = lax.dot_general(gb, a.astype(BF16), _BTN, preferred_element_type=F32)
    return da, db


_bdot_nt.defvjp(_bdot_nt_fwd, _bdot_nt_bwd)


@jax.custom_vjp
def _bdot_nn(a, b):
    return lax.dot_general(a.astype(BF16), b.astype(BF16), _BNN, preferred_element_type=F32)


def _bdot_nn_fwd(a, b):
    return _bdot_nn(a, b), (a, b)


def _bdot_nn_bwd(res, g):
    a, b = res
    gb = g.astype(BF16)
    da = lax.dot_general(gb, b.astype(BF16), _BNT, preferred_element_type=F32)
    db = lax.dot_general(a.astype(BF16), gb, _BTN, preferred_element_type=F32)
    return da, db


_bdot_nn.defvjp(_bdot_nn_fwd, _bdot_nn_bwd)


def _attn_math(q, kp, kc, vp, vc, bp, bc, sinks, qg, kg, *, prev_ok, scale):
    h, rows, _ = q.shape
    b = kp.shape[1]
    qn = q * _rms(q) * qg
    kpn = kp * _rms(kp) * kg
    kcn = kc * _rms(kc) * kg
    lp = _bdot_nt(qn, kpn) * scale + bp.reshape(h, rows, b)
    lc = _bdot_nt(qn, kcn) * scale + bc.reshape(h, rows, b)
    lp = jnp.where(prev_ok, lp, -jnp.inf)
    sink = jnp.broadcast_to(sinks, (sinks.shape[0], b, 1)).reshape(h, rows, 1)
    m = jnp.maximum(jnp.maximum(jnp.max(lp, axis=-1, keepdims=True), jnp.max(lc, axis=-1, keepdims=True)), sink)
    m = lax.stop_gradient(m)
    pp = jnp.exp(lp - m)
    pc = jnp.exp(lc - m)
    den = jnp.sum(pp, axis=-1, keepdims=True) + jnp.sum(pc, axis=-1, keepdims=True) + jnp.exp(sink - m)
    inv = 1.0 / den
    return _bdot_nn(pp * inv, vp) + _bdot_nn(pc * inv, vc)


def _attn_specs(p, aw, kvw, nq, hd, nblk, reverse):
    assert aw % (2 * kvw) == 0
    kv_col = aw // (2 * kvw)

    def blk(n):
        return nblk - 1 - n if reverse else n

    return [
        pl.BlockSpec((BLOCK, aw), lambda n: (blk(n), 0)),
        pl.BlockSpec((BLOCK, 2 * kvw), lambda n: (jnp.maximum(blk(n) - 1, 0), kv_col)),
        pl.BlockSpec((BLOCK, 2 * kvw), lambda n: (blk(n), kv_col)),
        _full_spec((nq, BLOCK, BLOCK)), _full_spec((nq, BLOCK, BLOCK)), _full_spec((nq, 1, 1)),
        _full_spec((1, hd)), _full_spec((1, hd)),
    ]


def _head_major(ref, n_heads, grp, hd, offset=0):
    return jnp.stack([
        jnp.concatenate([ref[:, pl.ds(offset + (grp * h + g) * hd, hd)].astype(F32) for g in range(grp)], axis=0)
        for h in range(n_heads)])


def _attn_inputs(nkv, grp, hd, kvw, q_ref, kvp_ref, kvc_ref):
    return (_head_major(q_ref, nkv, grp, hd), _head_major(kvp_ref, nkv, 1, hd), _head_major(kvc_ref, nkv, 1, hd),
            _head_major(kvp_ref, nkv, 1, hd, kvw), _head_major(kvc_ref, nkv, 1, hd, kvw))


def attn_fwd(p, bias_p, bias_c, sinks, qg, kg, *, aw, kvw, dep=None):
    t, hd = p.shape[0], qg.shape[-1]
    nq, nkv, nblk = aw // hd, kvw // hd, t // BLOCK
    grp = nq // nkv
    scale = hd ** -0.5

    def body(q_ref, kvp_ref, kvc_ref, bp_ref, bc_ref, s_ref, qg_ref, kg_ref, o_ref):
        prev_ok = pl.program_id(0) > 0
        out = _attn_math(*_attn_inputs(nkv, grp, hd, kvw, q_ref, kvp_ref, kvc_ref), bp_ref[...], bc_ref[...],
                         s_ref[...], qg_ref[...], kg_ref[...], prev_ok=prev_ok, scale=scale)
        for h in range(nkv):
            for g in range(grp):
                o_ref[:, pl.ds((grp * h + g) * hd, hd)] = out[h, g * BLOCK:(g + 1) * BLOCK].astype(BF16)

    return _pallas(
        body, [p, p, p, bias_p, bias_c, sinks, qg, kg], dep=dep, grid=(nblk,),
        in_specs=_attn_specs(p, aw, kvw, nq, hd, nblk, False),
        out_specs=pl.BlockSpec((BLOCK, aw), lambda n: (n, 0)),
        out_shape=jax.ShapeDtypeStruct((t, aw), BF16),
        sem=("parallel",), name="attn_fwd")


def attn_bwd(p, bias_p, bias_c, sinks, qg, kg, do, *, aw, kvw, dep=None):
    t, hd = p.shape[0], qg.shape[-1]
    nq, nkv, nblk = aw // hd, kvw // hd, t // BLOCK
    grp = nq // nkv
    scale = hd ** -0.5

    def body(q_ref, kvp_ref, kvc_ref, bp_ref, bc_ref, s_ref, qg_ref, kg_ref, do_ref,
             dqkv_ref, dbp_ref, dbc_ref, ds_ref, dqg_ref, dkg_ref, carry):
        i = pl.program_id(0)
        prev_ok = (nblk - 1 - i) > 0

        @pl.when(i == 0)
        def _():
            carry[...] = jnp.zeros_like(carry)
            dbp_ref[...] = jnp.zeros_like(dbp_ref)
            dbc_ref[...] = jnp.zeros_like(dbc_ref)
            ds_ref[...] = jnp.zeros_like(ds_ref)
            dqg_ref[...] = jnp.zeros_like(dqg_ref)
            dkg_ref[...] = jnp.zeros_like(dkg_ref)

        fn = functools.partial(_attn_math, prev_ok=prev_ok, scale=scale)
        _, vjp = jax.vjp(fn, *_attn_inputs(nkv, grp, hd, kvw, q_ref, kvp_ref, kvc_ref), bp_ref[...], bc_ref[...],
                         s_ref[...], qg_ref[...], kg_ref[...])
        dq, dkp, dkc, dvp, dvc, dbp, dbc, dsk, dqg, dkg = vjp(_head_major(do_ref, nkv, grp, hd))
        for h in range(nkv):
            for g in range(grp):
                dqkv_ref[:, pl.ds((grp * h + g) * hd, hd)] = dq[h, g * BLOCK:(g + 1) * BLOCK].astype(BF16)
            k_cols, v_cols = pl.ds(h * hd, hd), pl.ds(kvw + h * hd, hd)
            dqkv_ref[:, pl.ds(aw + h * hd, hd)] = (dkc[h] + carry[:, k_cols]).astype(BF16)
            dqkv_ref[:, pl.ds(aw + kvw + h * hd, hd)] = (dvc[h] + carry[:, v_cols]).astype(BF16)
            carry[:, k_cols] = dkp[h]
            carry[:, v_cols] = dvp[h]
        dbp_ref[...] += dbp
        dbc_ref[...] += dbc
        ds_ref[...] += dsk
        dqg_ref[...] += dqg
        dkg_ref[...] += dkg

    return _pallas(
        body, [p, p, p, bias_p, bias_c, sinks, qg, kg, do], dep=dep, grid=(nblk,),
        in_specs=_attn_specs(p, aw, kvw, nq, hd, nblk, True)
        + [pl.BlockSpec((BLOCK, aw), lambda n: (nblk - 1 - n, 0))],
        out_specs=[
            pl.BlockSpec((BLOCK, aw + 2 * kvw), lambda n: (nblk - 1 - n, 0)),
            _full_spec((nq, BLOCK, BLOCK)), _full_spec((nq, BLOCK, BLOCK)), _full_spec((nq, 1, 1)),
            _full_spec((1, hd)), _full_spec((1, hd)),
        ],
        out_shape=[
            jax.ShapeDtypeStruct((t, aw + 2 * kvw), BF16),
            jax.ShapeDtypeStruct((nq, BLOCK, BLOCK), F32),
            jax.ShapeDtypeStruct((nq, BLOCK, BLOCK), F32),
            jax.ShapeDtypeStruct((nq, 1, 1), F32),
            jax.ShapeDtypeStruct((1, hd), F32),
            jax.ShapeDtypeStruct((1, hd), F32),
        ],
        scratch=[pltpu.VMEM((BLOCK, 2 * kvw), F32)],
        sem=("arbitrary",), name="attn_bwd")


CONV_TILE = 256


def _conv_halo_specs(tb, ch, nblk):
    per = tb // CONV_HALO
    last = nblk * per - 1
    cur = pl.BlockSpec((tb, ch), lambda n: (n, 0))
    prev = pl.BlockSpec((CONV_HALO, ch), lambda n: (jnp.maximum(n * per - 1, 0), 0))
    nxt = pl.BlockSpec((CONV_HALO, ch), lambda n: (jnp.minimum((n + 1) * per, last), 0))
    return cur, prev, nxt


def _ln_silu(co, ln_g, ln_b):
    mu = jnp.mean(co, axis=-1, keepdims=True)
    cen = co - mu
    rstd = lax.rsqrt(jnp.mean(cen * cen, axis=-1, keepdims=True) + EPS)
    xhat = cen * rstd
    z = xhat * ln_g + ln_b
    return xhat, rstd, z


def _shifted_copies(src, shifted):
    rows = src.shape[0] - SUBLANES
    for r in range(1, SUBLANES):
        shifted[r, pl.ds(0, rows), :] = src[pl.ds(r, rows), :]


def _rows_from(src, shifted, start, n):
    r = start % SUBLANES
    if r == 0:
        return src[pl.ds(start, n), :]
    return shifted[r, pl.ds(start - r, n), :]


def conv_fwd(ca, cb, conv_w, conv_b, ln_g, ln_b, dep=None):
    t, ch = ca.shape
    tb = _tile(t, CONV_TILE, CONV_HALO)
    nblk = t // tb
    cur, prev, _ = _conv_halo_specs(tb, ch, nblk)
    lead = CONV_HALO - (CONV_WIDTH - 1)

    def body(ca_ref, cb_ref, cap_ref, cbp_ref, w_ref, b_ref, g_ref, bb_ref, s_ref, co_ref, ubuf, ushift):
        n = pl.program_id(0)
        halo = cap_ref[...] * _sigmoid(cbp_ref[...])
        ubuf[pl.ds(0, CONV_HALO), :] = jnp.where(n > 0, halo, 0.0)
        ubuf[pl.ds(CONV_HALO, tb), :] = ca_ref[...] * _sigmoid(cb_ref[...])
        _shifted_copies(ubuf, ushift)
        acc = jnp.broadcast_to(b_ref[...], (tb, ch))
        for k in range(CONV_WIDTH):
            acc = acc + w_ref[pl.ds(k, 1), :] * _rows_from(ubuf, ushift, lead + k, tb)
        co_ref[...] = acc
        _, _, z = _ln_silu(acc, g_ref[...], bb_ref[...])
        s_ref[...] = (z * _sigmoid(z)).astype(BF16)

    vec = _full_spec((1, ch))
    return _pallas(
        body, [ca, cb, ca, cb, conv_w, conv_b, ln_g, ln_b], dep=dep, grid=(nblk,),
        in_specs=[cur, cur, prev, prev, _full_spec(conv_w.shape), vec, vec, vec],
        out_specs=[cur, cur],
        out_shape=[jax.ShapeDtypeStruct((t, ch), BF16), jax.ShapeDtypeStruct((t, ch), F32)],
        scratch=[pltpu.VMEM((CONV_HALO + tb, ch), F32), pltpu.VMEM((SUBLANES, CONV_HALO + tb, ch), F32)],
        sem=("parallel",), name="conv_fwd")


def conv_bwd(ca, cb, co, ds, conv_w, ln_g, ln_b, dep=None):
    t, ch = ca.shape
    tb = _tile(t, CONV_TILE, CONV_HALO)
    nblk = t // tb
    cur, prev, nxt = _conv_halo_specs(tb, ch, nblk)
    lead = CONV_HALO - (CONV_WIDTH - 1)
    ext = tb + CONV_HALO

    def body(ca_ref, cb_ref, cap_ref, cbp_ref, co_ref, con_ref, ds_ref, dsn_ref, w_ref, g_ref, bb_ref,
             dca_ref, dcb_ref, dw_ref, dvec_ref, ubuf, dbuf, ushift, dshift):
        n = pl.program_id(0)
        is_last = n == nblk - 1
        sig_b = _sigmoid(cb_ref[...])
        cav = ca_ref[...].astype(F32)
        ubuf[pl.ds(0, CONV_HALO), :] = jnp.where(n > 0, cap_ref[...] * _sigmoid(cbp_ref[...]), 0.0)
        ubuf[pl.ds(CONV_HALO, tb), :] = cav * sig_b
        _shifted_copies(ubuf, ushift)
        co = jnp.concatenate([co_ref[...], con_ref[...]], axis=0)
        xhat, rstd, z = _ln_silu(co, g_ref[...], bb_ref[...])
        dsv = jnp.concatenate([ds_ref[...].astype(F32), jnp.where(is_last, 0.0, dsn_ref[...].astype(F32))], axis=0)
        sg = _sigmoid(z)
        dz = dsv * (sg * (1.0 + z * (1.0 - sg)))
        dxh = dz * g_ref[...]
        dco = rstd * (dxh - jnp.mean(dxh, axis=-1, keepdims=True)
                      - xhat * jnp.mean(dxh * xhat, axis=-1, keepdims=True))
        dbuf[...] = dco
        _shifted_copies(dbuf, dshift)

        @pl.when(n == 0)
        def _():
            dw_ref[...] = jnp.zeros_like(dw_ref)
            dvec_ref[...] = jnp.zeros_like(dvec_ref)

        dco_cur = dco[:tb]
        dvec_ref[pl.ds(0, 1), :] += jnp.sum(dco_cur, axis=0, keepdims=True)
        dvec_ref[pl.ds(1, 1), :] += jnp.sum(dz[:tb] * xhat[:tb], axis=0, keepdims=True)
        dvec_ref[pl.ds(2, 1), :] += jnp.sum(dz[:tb], axis=0, keepdims=True)
        du = jnp.zeros((tb, ch), F32)
        for k in range(CONV_WIDTH):
            du = du + w_ref[pl.ds(k, 1), :] * _rows_from(dbuf, dshift, CONV_WIDTH - 1 - k, tb)
            dw_ref[pl.ds(k, 1), :] += jnp.sum(dco_cur * _rows_from(ubuf, ushift, lead + k, tb), axis=0,
                                              keepdims=True)
        dca_ref[...] = (du * sig_b).astype(BF16)
        dcb_ref[...] = (du * cav * sig_b * (1.0 - sig_b)).astype(BF16)

    vec = _full_spec((1, ch))
    return _pallas(
        body, [ca, cb, ca, cb, co, co, ds, ds, conv_w, ln_g, ln_b], dep=dep, grid=(nblk,),
        in_specs=[cur, cur, prev, prev, cur, nxt, cur, nxt, _full_spec(conv_w.shape), vec, vec],
        out_specs=[cur, cur, _full_spec(conv_w.shape), _full_spec((SUBLANES, ch))],
        out_shape=[jax.ShapeDtypeStruct((t, ch), BF16), jax.ShapeDtypeStruct((t, ch), BF16),
                   jax.ShapeDtypeStruct(conv_w.shape, F32), jax.ShapeDtypeStruct((SUBLANES, ch), F32)],
        scratch=[pltpu.VMEM((CONV_HALO + tb, ch), F32), pltpu.VMEM((ext, ch), F32),
                 pltpu.VMEM((SUBLANES, CONV_HALO + tb, ch), F32), pltpu.VMEM((SUBLANES, ext, ch), F32)],
        sem=("arbitrary",), name="conv_bwd")


def ada_fwd(c_t, w_ada, dep=None):
    d, nc = w_ada.shape
    nex = c_t.shape[1]
    tn = _tile(nc, 512)

    def body(ct_ref, w_ref, o_ref):
        w = w_ref[...]
        ct = ct_ref[...]
        cact = ct * _sigmoid(ct)
        rows = [jnp.sum(w * cact[:, b:b + 1], axis=0, keepdims=True) for b in range(nex)]
        o_ref[...] = jnp.concatenate(rows, axis=0)

    return _pallas(
        body, [c_t, w_ada], dep=dep, grid=(nc // tn,),
        in_specs=[_full_spec(c_t.shape), pl.BlockSpec((d, tn), lambda j: (0, j))],
        out_specs=pl.BlockSpec((nex, tn), lambda j: (0, j)),
        out_shape=jax.ShapeDtypeStruct((nex, nc), F32),
        sem=("parallel",), name="ada_fwd")


def _adamw_math(w, g, m, v):
    m = ADAM_B1 * m + (1.0 - ADAM_B1) * g
    v = ADAM_B2 * v + (1.0 - ADAM_B2) * (g * g)
    m_hat = m / (1.0 - ADAM_B1 ** ADAM_STEP)
    v_hat = v / (1.0 - ADAM_B2 ** ADAM_STEP)
    delta = -ADAM_LR * (m_hat / (jnp.sqrt(v_hat) + ADAM_EPS) + ADAM_WD * w)
    return delta, m, v


def adamw(w, g, m, v, name, copy_grad=False, dep=None):
    r, n = w.shape
    tr, tn = _ew_tiles(r, n, elems=256 * 1024)
    n_out = 4 if copy_grad else 3

    def body(w_ref, g_ref, m_ref, v_ref, *outs):
        g = g_ref[...]
        if copy_grad:
            outs[0][...] = g
        outs[-3][...], outs[-2][...], outs[-1][...] = _adamw_math(w_ref[...], g, m_ref[...], v_ref[...])

    blk = pl.BlockSpec((tr, tn), lambda i, j: (i, j))
    return _pallas(
        body, [w, g, m, v], dep=dep, grid=(r // tr, n // tn),
        in_specs=[blk] * 4, out_specs=[blk] * n_out,
        out_shape=[jax.ShapeDtypeStruct((r, n), F32)] * n_out,
        sem=("parallel", "parallel"), name=name)


def ada_grad_adamw(c_t, dmod_cols, w, m, v, dep=None):
    d, nc = w.shape
    nex = c_t.shape[1]
    tr, tn = _ew_tiles(d, nc, elems=256 * 1024)

    def body(ct_ref, dm_ref, w_ref, m_ref, v_ref, g_ref, d_ref, nm_ref, nv_ref):
        ct = ct_ref[...]
        cact = ct * _sigmoid(ct)
        dm = dm_ref[...]
        g = cact[:, 0:1] * dm[0:1, :]
        for b in range(1, nex):
            g = g + cact[:, b:b + 1] * dm[b:b + 1, :]
        g_ref[...] = g
        d_ref[...], nm_ref[...], nv_ref[...] = _adamw_math(w_ref[...], g, m_ref[...], v_ref[...])

    blk = pl.BlockSpec((tr, tn), lambda i, j: (i, j))
    return _pallas(
        body, [c_t, dmod_cols, w, m, v], dep=dep, grid=(d // tr, nc // tn),
        in_specs=[pl.BlockSpec((tr, nex), lambda i, j: (i, 0)), pl.BlockSpec((nex, tn), lambda i, j: (0, j)),
                  blk, blk, blk],
        out_specs=[blk] * 4,
        out_shape=[jax.ShapeDtypeStruct((d, nc), F32)] * 4,
        sem=("parallel", "parallel"), name="ada_grad_adamw")


def _row_pack(parts):
    cols, offs, off = [], [], 0
    for p in parts:
        n = p.shape[1]
        width = -(-n // LANES) * LANES
        cols.append(jnp.pad(p, ((0, 0), (0, width - n))) if width != n else p)
        offs.append(off)
        off += width
    return jnp.concatenate(cols, axis=1), offs


def small_sum_adamw(gathered, offs, ws, ms, vs, extra_widths, dep=None):
    ndev = gathered.shape[0]
    npar = len(ws)

    def body(ga_ref, *refs):
        w_refs, m_refs, v_refs = refs[:npar], refs[npar:2 * npar], refs[2 * npar:3 * npar]
        outs = refs[3 * npar:]
        tot = ga_ref[0]
        for s in range(1, ndev):
            tot = tot + ga_ref[s]
        for i in range(npar):
            n = ws[i].shape[1]
            g = tot[:, offs[i]:offs[i] + n]
            outs[4 * i][...] = g
            outs[4 * i + 1][...], outs[4 * i + 2][...], outs[4 * i + 3][...] = _adamw_math(
                w_refs[i][...], g, m_refs[i][...], v_refs[i][...])
        for e, n in enumerate(extra_widths):
            off = offs[npar + e]
            outs[4 * npar + e][...] = tot[:, off:off + n]

    shapes = [jax.ShapeDtypeStruct(w.shape, F32) for w in ws for _ in range(4)]
    shapes += [jax.ShapeDtypeStruct((1, n), F32) for n in extra_widths]
    return _pallas(
        body, [gathered, *ws, *ms, *vs], dep=dep, in_specs=[_VMEM] * (1 + 3 * npar), out_specs=[_VMEM] * len(shapes),
        out_shape=shapes, name="small_sum_adamw")


def _position():
    return lax.axis_index("x"), lax.axis_index("y"), lax.axis_index("c")


def _other_chips(x, y):
    return [(1 - x, y), (x, 1 - y), (1 - x, 1 - y)]


def allgather_small(block, name, dep=None):
    def body(x_ref, out_ref, send_sems, recv_sems, local_sem):
        x, y, c = _position()
        me, sibling = (x, y, c), (x, y, 1 - c)
        chips = _other_chips(x, y)

        def slot(px, py, pc):
            return out_ref.at[4 * px + 2 * py + pc]

        def copy(k, block_of, to, src=None):
            return pltpu.make_async_remote_copy(
                src_ref=slot(*block_of) if src is None else src, dst_ref=slot(*block_of),
                send_sem=send_sems.at[k], recv_sem=recv_sems.at[k], device_id=to, device_id_type=MESH)

        mine = pltpu.make_async_copy(x_ref, slot(*me), local_sem)
        mine.start()
        first = [copy(0, me, sibling, src=x_ref)]
        first += [copy(1 + j, me, (*chip, c), src=x_ref) for j, chip in enumerate(chips)]
        for cp in first:
            cp.start()
        passed = [copy(4 + j, (*chip, c), sibling) for j, chip in enumerate(chips)]
        for j, chip in enumerate(chips):
            copy(1 + j, (*chip, c), me).wait_recv()
            passed[j].start()
        copy(0, sibling, me).wait_recv()
        for j, chip in enumerate(chips):
            copy(4 + j, (*chip, 1 - c), me).wait_recv()
        for cp in first + passed:
            cp.wait_send()
        mine.wait()

    return _pallas(
        body, [block], dep=dep,
        out_shape=jax.ShapeDtypeStruct((N_DEV, *block.shape), block.dtype),
        in_specs=[_VMEM], out_specs=_VMEM,
        scratch=[pltpu.SemaphoreType.DMA((7,)), pltpu.SemaphoreType.DMA((7,)), pltpu.SemaphoreType.DMA],
        name=name)


class Started(NamedTuple):
    send_sems: Any
    recv_sems: Any
    bufs: list


def exchange_start(name, bufs, n_copies, plan, dep=None):
    nb = len(bufs)

    def body(*refs):
        for cp in plan(refs[:nb], refs[nb], refs[nb + 1]):
            cp.start()

    outs = _pallas(
        body, [pltpu.with_memory_space_constraint(b, pltpu.HBM) for b in bufs], dep=dep, name=name,
        out_shape=(pltpu.SemaphoreType.DMA((n_copies,)), pltpu.SemaphoreType.DMA((n_copies,)),
                   *[pltpu.HBM(b.shape, b.dtype) for b in bufs]),
        in_specs=[_HBM] * nb,
        out_specs=(_SEM, _SEM, *[_HBM] * nb),
        input_output_aliases={i: 2 + i for i in range(nb)},
        compiler_params=pltpu.CompilerParams(has_side_effects=_EFFECT))
    return Started(outs[0], outs[1], list(outs[2:2 + nb]))


def exchange_wait(name, started, plan, bufs=None, dep=None):
    if bufs is not None:
        started = started._replace(bufs=list(bufs))
    nb = len(started.bufs)

    def body(*refs):
        for cp in plan(refs[:nb], refs[nb], refs[nb + 1]):
            cp.wait_send()
            cp.wait_recv()

    outs = _pallas(
        body, [*started.bufs, started.send_sems, started.recv_sems], dep=dep, name=name,
        out_shape=tuple(pltpu.HBM(b.shape, b.dtype) for b in started.bufs),
        in_specs=[_HBM] * nb + [_SEM, _SEM],
        out_specs=tuple([_HBM] * nb),
        input_output_aliases={i: i for i in range(nb)},
        compiler_params=pltpu.CompilerParams(has_side_effects=_EFFECT))
    return list(outs)


def _remote(src, dst, send_sems, recv_sems, i, to):
    return pltpu.make_async_remote_copy(src_ref=src, dst_ref=dst, send_sem=send_sems.at[i], recv_sem=recv_sems.at[i],
                                        device_id=to, device_id_type=MESH)


def _half_rows(buf_rows, chip_idx, pc):
    half = buf_rows // (2 * N_CHIPS)
    return pl.ds((2 * chip_idx + pc) * half, half)


ALL_PEERS = (0, 1, 2)


def plan_gather_ici(refs, send_sems, recv_sems, peers=ALL_PEERS):
    x, y, c = _position()
    chips = _other_chips(x, y)
    copies = []
    for k, ref in enumerate(refs):
        rows = ref.at[_half_rows(ref.shape[0], 2 * x + y, c), :]
        for i, j in enumerate(peers):
            copies.append(_remote(rows, rows, send_sems, recv_sems, len(peers) * k + i, (*chips[j], c)))
    return copies


def plan_gather_relay(refs, send_sems, recv_sems):
    x, y, c = _position()
    copies = []
    for k, ref in enumerate(refs):
        quarter = ref.shape[0] // (4 * N_CHIPS)
        for i, (src_chip, to) in enumerate((((1 - x, y), (x, 1 - y, c)), ((x, 1 - y), (1 - x, y, c)))):
            start = (2 * (2 * src_chip[0] + src_chip[1]) + c) * 2 * quarter + i * quarter
            rows = ref.at[pl.ds(start, quarter), :]
            copies.append(_remote(rows, rows, send_sems, recv_sems, 2 * k + i, to))
    return copies


def plan_gather_d2d(refs, send_sems, recv_sems, peers=ALL_PEERS):
    x, y, c = _position()
    chips = _other_chips(x, y)
    copies = []
    for k, ref in enumerate(refs):
        for i, j in enumerate(peers):
            px, py = chips[j]
            rows = ref.at[_half_rows(ref.shape[0], 2 * px + py, c), :]
            copies.append(_remote(rows, rows, send_sems, recv_sems, len(peers) * k + i, (x, y, 1 - c)))
    return copies


def plan_pair_exchange(refs, send_sems, recv_sems):
    x, y, c = _position()
    nw = len(refs) // 2
    copies = []
    for k in range(nw):
        for chip in range(N_CHIPS):
            copies.append(_remote(refs[k].at[chip, 1 - c], refs[nw + k].at[chip], send_sems, recv_sems,
                                  N_CHIPS * k + chip, (x, y, 1 - c)))
    return copies


def plan_chip_exchange(refs, send_sems, recv_sems):
    x, y, c = _position()
    nw = len(refs) // 2
    copies = []
    for k in range(nw):
        for j, (px, py) in enumerate(_other_chips(x, y)):
            copies.append(_remote(refs[k].at[2 * px + py], refs[nw + k].at[2 * x + y], send_sems, recv_sems,
                                  3 * k + j, (px, py, c)))
    return copies


def plan_pair_share(refs, send_sems, recv_sems):
    x, y, c = _position()
    return [_remote(ref.at[c], ref.at[c], send_sems, recv_sems, k, (x, y, 1 - c)) for k, ref in enumerate(refs)]


def cast_into_slot(src, slot, n_slots, name, dep=None):
    r, n = src.shape
    tr, tn = _ew_tiles(r, n, BF16_SUBLANES)

    def body(slot_ref, s_ref, o_ref):
        o_ref[...] = s_ref[...].astype(BF16)

    return _pallas(
        body, [slot, src], dep=dep, n_prefetch=1, grid=(r // tr, n // tn),
        in_specs=[pl.BlockSpec((tr, tn), lambda i, j, sl: (i, j))],
        out_specs=pl.BlockSpec((None, tr, tn), lambda i, j, sl: (sl[0], i, j)),
        out_shape=jax.ShapeDtypeStruct((n_slots, r, n), BF16),
        sem=("parallel", "parallel"), name=name)


def pair_sum(g, r, core, name, dep=None):
    nchip, _, h, n = g.shape
    th, tn = _ew_tiles(h, n, BF16_SUBLANES)

    def body(core_ref, g_ref, r_ref, o_ref):
        o_ref[...] = (g_ref[...].astype(F32) + r_ref[...].astype(F32)).astype(BF16)

    return _pallas(
        body, [core, g, r], dep=dep, n_prefetch=1, grid=(nchip, h // th, n // tn),
        in_specs=[pl.BlockSpec((None, None, th, tn), lambda a, i, j, cr: (a, cr[0], i, j)),
                  pl.BlockSpec((None, th, tn), lambda a, i, j, cr: (a, i, j))],
        out_specs=pl.BlockSpec((None, th, tn), lambda a, i, j, cr: (a, i, j)),
        out_shape=jax.ShapeDtypeStruct((nchip, h, n), BF16),
        sem=("parallel", "parallel", "parallel"), name=name)


def chip_sum(own, got, where, name, dep=None):
    nchip, h, n = got.shape
    th, tn = _ew_tiles(h, n, BF16_SUBLANES, elems=256 * 1024)

    def body(where_ref, own_ref, *rest):
        got_refs, o_ref = rest[:nchip], rest[nchip]
        chip = where_ref[0]
        acc = None
        for s in range(nchip):
            term = jnp.where(chip == s, own_ref[...], got_refs[s][...]).astype(F32)
            acc = term if acc is None else acc + term
        o_ref[...] = acc

    def got_spec(s):
        return pl.BlockSpec((None, th, tn), lambda i, j, wr: (jnp.where(wr[0] == s, (s + 1) % nchip, s), i, j))

    return _pallas(
        body, [where, own, *[got] * nchip], dep=dep, n_prefetch=1, grid=(h // th, n // tn),
        in_specs=[pl.BlockSpec((None, th, tn), lambda i, j, wr: (wr[0], i, j))]
        + [got_spec(s) for s in range(nchip)],
        out_specs=pl.BlockSpec((None, th, tn), lambda i, j, wr: (wr[1], i, j)),
        out_shape=jax.ShapeDtypeStruct((2, h, n), F32),
        sem=("parallel", "parallel"), name=name)


def kernel(x, c, w_ada, b_ada, norm_mix_g, w_in, q_norm_g, k_norm_g, attn_sinks, rel_bias, w_attn_out, conv_w, conv_b, conv_ln_g, conv_ln_b, w_conv_out, w_mix_out, norm_ffn_g, w_ffn_in, w_ffn_out, loss_target, m_w_ada, m_b_ada, m_norm_mix_g, m_w_in, m_q_norm_g, m_k_norm_g, m_attn_sinks, m_rel_bias, m_w_attn_out, m_conv_w, m_conv_b, m_conv_ln_g, m_conv_ln_b, m_w_conv_out, m_w_mix_out, m_norm_ffn_g, m_w_ffn_in, m_w_ffn_out, v_w_ada, v_b_ada, v_norm_mix_g, v_w_in, v_q_norm_g, v_k_norm_g, v_attn_sinks, v_rel_bias, v_w_attn_out, v_conv_w, v_conv_b, v_conv_ln_g, v_conv_ln_b, v_w_conv_out, v_w_mix_out, v_norm_ffn_g, v_w_ffn_in, v_w_ffn_out):
    run = InOrder()
    xi, yi, ci = _position()
    chip = 2 * xi + yi
    me = 2 * chip + ci
    chip_arr = chip.astype(jnp.int32).reshape(1)
    core_arr = ci.astype(jnp.int32).reshape(1)
    where_arr = jnp.stack([chip, ci]).astype(jnp.int32)

    xe, tgt = x[0], loss_target[0]
    t, d = xe.shape
    hd = q_norm_g.shape[-1]
    nq = attn_sinks.shape[-1]
    aw = nq * hd
    ch = conv_b.shape[-1]
    in_width = N_CHIPS * w_in.shape[-1]
    kvw = (in_width - aw - 2 * ch - 2 * d) // 2
    nkv = kvw // hd
    dff = N_CHIPS * w_ffn_out.shape[1]
    off_k, off_v, off_ca = aw, aw + kvw, aw + 2 * kvw
    off_cb, off_ga, off_gc = off_ca + ch, off_ca + 2 * ch, off_ca + 2 * ch + d
    nc_ada = w_ada.shape[-1]
    ch_loc = conv_w.shape[-1]
    nj_ffn = w_ffn_in.shape[-1]
    perm_ffn = ffn_perm(N_CHIPS)

    big = {"w_in": w_in[0], "w_attn_out": w_attn_out[0], "w_conv_out": w_conv_out[0], "w_mix_out": w_mix_out[0],
           "w_ffn_in": w_ffn_in[0], "w_ffn_out": w_ffn_out[0]}
    moments = {"w_in": (m_w_in, v_w_in), "w_attn_out": (m_w_attn_out, v_w_attn_out),
               "w_conv_out": (m_w_conv_out, v_w_conv_out), "w_mix_out": (m_w_mix_out, v_w_mix_out),
               "w_ffn_in": (m_w_ffn_in, v_w_ffn_in), "w_ffn_out": (m_w_ffn_out, v_w_ffn_out)}
    gather_groups = {"in": ["w_in"], "branch_out": ["w_attn_out", "w_conv_out"], "mix_out": ["w_mix_out"],
                     "ffn_in": ["w_ffn_in"], "ffn_out": ["w_ffn_out"]}
    grads, deltas, new_m, new_v = {}, {}, {}, {}

    def gather_cast(gname):
        bufs = []
        for n in gather_groups[gname]:
            r, ncol = big[n].shape
            bufs.append(run(cast_into_slot, big[n], chip_arr, N_CHIPS, "cast_" + n).reshape(N_CHIPS * r, ncol))
        return bufs

    def gather_ici_start(gname, bufs):
        return run(exchange_start, "gather_ici_start_" + gname, bufs, 3 * len(bufs), plan_gather_ici)

    def gather_pass_on(gname, ici):
        landed = run(exchange_wait, "gather_ici_wait_" + gname, ici, plan_gather_ici)
        return run(exchange_start, "gather_d2d_start_" + gname, landed, 3 * len(landed), plan_gather_d2d)

    def gathered(gname, d2d):
        outs = run(exchange_wait, "gather_d2d_wait_" + gname, d2d, plan_gather_d2d)
        return [o.reshape(N_CHIPS, *big[n].shape) for o, n in zip(outs, gather_groups[gname])]

    def rs_pair_start(gname, names, partials):
        blocks = [g.reshape(N_CHIPS, 2, big[n].shape[0] // 2, big[n].shape[1]) for n, g in zip(names, partials)]
        land = [lax.empty((N_CHIPS,) + b.shape[2:], BF16) for b in blocks]
        return run(exchange_start, "pair_exchange_start_" + gname, blocks + land, N_CHIPS * len(blocks),
                   plan_pair_exchange)

    def rs_chip_start(gname, names, pair):
        nw = len(names)
        outs = run(exchange_wait, "pair_exchange_wait_" + gname, pair, plan_pair_exchange)
        sums = [run(pair_sum, g, r, core_arr, "pair_sum_" + n) for n, g, r in zip(names, outs[:nw], outs[nw:])]
        land = [lax.empty(s.shape, BF16) for s in sums]
        return run(exchange_start, "chip_exchange_start_" + gname, sums + land, 3 * nw, plan_chip_exchange)

    def rs_share_start(gname, names, chipx):
        nw = len(names)
        outs = run(exchange_wait, "chip_exchange_wait_" + gname, chipx, plan_chip_exchange)
        halves = [run(chip_sum, s, r, where_arr, "chip_sum_" + n) for n, s, r in zip(names, outs[:nw], outs[nw:])]
        return run(exchange_start, "pair_share_start_" + gname, halves, nw, plan_pair_share)

    def rs_finish(gname, names, share):
        fulls = run(exchange_wait, "pair_share_wait_" + gname, share, plan_pair_share)
        for n, g2 in zip(names, fulls):
            g, dl, nm, nv = run(adamw, big[n], g2.reshape(big[n].shape), moments[n][0][0], moments[n][1][0],
                                "adamw_" + n, copy_grad=True)
            grads[n], deltas[n], new_m[n], new_v[n] = g[None], dl[None], nm[None], nv[None]

    near, far = (0, 1), (2,)
    plan_ici_near = functools.partial(plan_gather_ici, peers=near)
    plan_ici_far, n_far = plan_gather_relay, 2
    plan_d2d_near = functools.partial(plan_gather_d2d, peers=near)
    plan_d2d_far = functools.partial(plan_gather_d2d, peers=far)
    bufs_in = gather_cast("in")
    row1, offs1 = _row_pack([c, conv_w[0].reshape(1, CONV_WIDTH * ch_loc)])
    got1 = run(allgather_small, row1, "allgather_cond")
    ici_near = run(exchange_start, "gather_ici_start_in_near", bufs_in, len(near), plan_ici_near)
    c_all = got1[:, 0, :d]
    conv_w_full = got1[0::2, 0, offs1[1]:offs1[1] + CONV_WIDTH * ch_loc].reshape(N_CHIPS, CONV_WIDTH, ch_loc)
    conv_w_full = jnp.transpose(conv_w_full, (1, 0, 2)).reshape(CONV_WIDTH, ch)
    conv_w_pad = jnp.pad(conv_w_full, ((0, 1), (0, 0)))
    c_t = jnp.transpose(c_all)
    mod_cols = run(ada_fwd, c_t, w_ada[0])
    rest_bufs = {gname: gather_cast(gname) for gname in gather_groups if gname != "in"}
    bucket = _t5_bucket_table()
    bucket_p, bucket_c = jnp.asarray(bucket[:, :BLOCK]), jnp.asarray(bucket[:, BLOCK:])
    bias_p, bias_c = run(bias_table, rel_bias, bucket_p, bucket_c)
    got2 = run(allgather_small, mod_cols, "allgather_mod")
    mod_all = got2.reshape(N_CHIPS, 2, N_DEV, nc_ada)[:, 0]
    mod = lax.dynamic_slice_in_dim(mod_all, me, 1, axis=1).reshape(1, N_CHIPS * nc_ada) + b_ada
    mod = jnp.pad(mod.reshape(N_MOD, d), ((0, SUBLANES - N_MOD), (0, 0)))

    landed = run(exchange_wait, "gather_ici_wait_in_near", ici_near, plan_ici_near)
    ici_far = run(exchange_start, "gather_ici_start_in_far", landed, n_far, plan_ici_far)
    d2d_near = run(exchange_start, "gather_d2d_start_in_near", ici_far.bufs, len(near), plan_d2d_near)
    h = run(pre_mix_fwd, xe, mod, norm_mix_g)
    ici = {gname: gather_ici_start(gname, rest_bufs[gname]) for gname in ("branch_out", "mix_out")}
    ici_near_ffn = run(exchange_start, "gather_ici_start_ffn_in_near", rest_bufs["ffn_in"], len(near), plan_ici_near)
    landed = run(exchange_wait, "gather_d2d_wait_in_near", d2d_near, plan_d2d_near)
    landed = run(exchange_wait, "gather_ici_wait_in_far", ici_far, plan_ici_far, bufs=landed)
    d2d_far = run(exchange_start, "gather_d2d_start_in_far", landed, len(far), plan_d2d_far)
    landed = run(exchange_wait, "gather_d2d_wait_in_far", d2d_far, plan_d2d_far)
    wg_in = landed[0].reshape(N_CHIPS, *big["w_in"].shape)
    p = run(mm_nn, h, wg_in, tn=wg_in.shape[2], tk=d, out_dtype=BF16, name="mm_in")
    d2d_branch = gather_pass_on("branch_out", ici["branch_out"])

    sinks3 = attn_sinks.reshape(nq, 1, 1)
    attn_o = run(attn_fwd, p, bias_p, bias_c, sinks3, q_norm_g, k_norm_g, aw=aw, kvw=kvw)
    ca, cb = p[:, off_ca:off_cb], p[:, off_cb:off_ga]
    s_conv, co_conv = run(conv_fwd, ca, cb, conv_w_pad, conv_b, conv_ln_g, conv_ln_b)
    wg_attn_out, wg_conv_out = gathered("branch_out", d2d_branch)
    y_attn = run(mm_nn, attn_o, wg_attn_out, tn=_tile(wg_attn_out.shape[2], 512), tk=aw, out_dtype=BF16,
                 name="mm_attn_out")
    y_conv = run(mm_nn, s_conv, wg_conv_out, tn=_tile(wg_conv_out.shape[2], 512), tk=ch, out_dtype=BF16,
                 name="mm_conv_out")
    landed = run(exchange_wait, "gather_ici_wait_ffn_in_near", ici_near_ffn, plan_ici_near)
    ici_far_ffn = run(exchange_start, "gather_ici_start_ffn_in_far", landed, n_far, plan_ici_far)
    ici["ffn_out"] = gather_ici_start("ffn_out", rest_bufs["ffn_out"])
    d2d_near_ffn = run(exchange_start, "gather_d2d_start_ffn_in_near", ici_far_ffn.bufs, len(near), plan_d2d_near)
    merged = run(merge_fwd, p, y_attn, y_conv, off_ga, off_gc)
    d2d_mix = gather_pass_on("mix_out", ici["mix_out"])
    (wg_mix_out,) = gathered("mix_out", d2d_mix)
    wg_mix_out = wg_mix_out.reshape(1, d, d)
    o_m = run(mm_nn, merged, wg_mix_out, tn=_tile(d, 512), tk=d, out_dtype=F32, name="mm_mix_out")
    landed = run(exchange_wait, "gather_d2d_wait_ffn_in_near", d2d_near_ffn, plan_d2d_near)
    landed = run(exchange_wait, "gather_ici_wait_ffn_in_far", ici_far_ffn, plan_ici_far, bufs=landed)
    d2d_far_ffn = run(exchange_start, "gather_d2d_start_ffn_in_far", landed, len(far), plan_d2d_far)
    x1, h2 = run(pre_ffn_fwd, xe, o_m, mod, norm_ffn_g)
    landed = run(exchange_wait, "gather_d2d_wait_ffn_in_far", d2d_far_ffn, plan_d2d_far)
    wg_ffn_in = landed[0].reshape(N_CHIPS, *big["w_ffn_in"].shape)
    f = run(mm_nn, h2, wg_ffn_in, tn=_tile(nj_ffn, 1408), tk=d, out_dtype=BF16, name="mm_ffn_in", perm=perm_ffn)
    d2d_ffn_out = gather_pass_on("ffn_out", ici["ffn_out"])
    act = run(swiglu_fwd, f, nj_ffn)
    (wg_ffn_out,) = gathered("ffn_out", d2d_ffn_out)
    wg_ffn_out = wg_ffn_out.reshape(1, dff, d)
    o_f = run(mm_nn, act, wg_ffn_out, tn=_tile(d, 512), tk=_tile(dff, 2816), out_dtype=F32, name="mm_ffn_out")
    loss11, dy, dof, acc_l = run(loss_head, x1, o_f, tgt, mod)

    gw_ffn_out = run(mm_tn, act, dof, 1, tk=_tile(dff, 512), tn=d, name="mm_ffn_out_dw")
    px_ffn_out = rs_pair_start("ffn_out", ["w_ffn_out"], [gw_ffn_out])
    dact = run(mm_nt, dof, wg_ffn_out, tko=_tile(dff, 512), tn=d, out_dtype=BF16, name="mm_ffn_out_dx")
    cx_ffn_out = rs_chip_start("ffn_out", ["w_ffn_out"], px_ffn_out)
    df = run(swiglu_bwd, f, dact, nj_ffn)
    gw_ffn_in = run(mm_tn, h2, df, N_CHIPS, tk=d, tn=_tile(nj_ffn, 1408), name="mm_ffn_in_dw",
                    perm=perm_ffn)
    px_ffn_in = rs_pair_start("ffn_in", ["w_ffn_in"], [gw_ffn_in])
    dh2 = run(mm_nt, df, wg_ffn_in, tko=_tile(d, 512), tn=nj_ffn, name="mm_ffn_in_dx", perm=perm_ffn)
    sh_ffn_out = rs_share_start("ffn_out", ["w_ffn_out"], cx_ffn_out)
    cx_ffn_in = rs_chip_start("ffn_in", ["w_ffn_in"], px_ffn_in)
    dx1, dom, acc_f = run(pre_ffn_bwd, x1, dh2, dy, o_m, mod, norm_ffn_g)
    gw_mix_out = run(mm_tn, merged, dom, 1, tk=d, tn=_tile(d, 1024), name="mm_mix_out_dw")
    px_mix = rs_pair_start("mix_out", ["w_mix_out"], [gw_mix_out])
    dmerged = run(mm_nt, dom, wg_mix_out, tko=_tile(d, 512), tn=d, out_dtype=BF16, name="mm_mix_out_dx")
    dy_attn, dy_conv, dga, dgc = run(merge_bwd, p, y_attn, y_conv, dmerged, off_ga, off_gc)
    rs_finish("ffn_out", ["w_ffn_out"], sh_ffn_out)
    cx_mix = rs_chip_start("mix_out", ["w_mix_out"], px_mix)
    gw_attn_out = run(mm_tn, attn_o, dy_attn, N_CHIPS, tk=aw, tn=_tile(wg_attn_out.shape[2], 512),
                      name="mm_attn_out_dw")
    gw_conv_out = run(mm_tn, s_conv, dy_conv, N_CHIPS, tk=ch, tn=_tile(wg_conv_out.shape[2], 512),
                      name="mm_conv_out_dw")
    ac_names = ["w_attn_out", "w_conv_out"]
    px_ac = rs_pair_start("attn_conv_out", ac_names, [gw_attn_out, gw_conv_out])
    dattn_o = run(mm_nt, dy_attn, wg_attn_out, tko=_tile(aw, 1024), tn=_tile(wg_attn_out.shape[2], 512),
                  out_dtype=BF16, name="mm_attn_out_dx")
    ds_conv = run(mm_nt, dy_conv, wg_conv_out, tko=_tile(ch, 1024), tn=_tile(wg_conv_out.shape[2], 512),
                  out_dtype=BF16, name="mm_conv_out_dx")
    cx_ac = rs_chip_start("attn_conv_out", ac_names, px_ac)
    dca, dcb, dconv_w, dconv_vec = run(conv_bwd, ca, cb, co_conv, ds_conv, conv_w_pad, conv_ln_g, conv_ln_b)
    sh_ffn_in = rs_share_start("ffn_in", ["w_ffn_in"], cx_ffn_in)
    dqkv, dbp, dbc, dsinks, dqg, dkg = run(attn_bwd, p, bias_p, bias_c, sinks3, q_norm_g, k_norm_g, dattn_o,
                                           aw=aw, kvw=kvw)
    sh_mix = rs_share_start("mix_out", ["w_mix_out"], cx_mix)
    sh_ac = rs_share_start("attn_conv_out", ac_names, cx_ac)
    drel = run(bias_table_bwd, dbp, dbc, bucket_p, bucket_c).reshape(NUM_BUCKETS, nq)
    dp = jnp.concatenate([dqkv, dca, dcb, dga, dgc], axis=1)
    gw_in = run(mm_tn, h, dp, N_CHIPS, tk=d, tn=wg_in.shape[2], name="mm_in_dw")
    px_in = rs_pair_start("in", ["w_in"], [gw_in])
    dh = run(mm_nt, dp, wg_in, tko=_tile(d, 1024), tn=wg_in.shape[2], name="mm_in_dx")
    grad_x, acc_m = run(pre_mix_bwd, xe, dh, dx1, mod, norm_mix_g)

    dmod = jnp.concatenate([acc_m[0:1], acc_m[1:2], acc_f[3:4], acc_f[0:1], acc_f[1:2], acc_l[0:1]], axis=1)
    small_names = ["b_ada", "norm_mix_g", "q_norm_g", "k_norm_g", "attn_sinks", "rel_bias", "conv_b", "conv_ln_g",
                   "conv_ln_b", "norm_ffn_g"]
    small_w = [b_ada, norm_mix_g, q_norm_g, k_norm_g, attn_sinks, rel_bias, conv_b, conv_ln_g, conv_ln_b, norm_ffn_g]
    small_m = [m_b_ada, m_norm_mix_g, m_q_norm_g, m_k_norm_g, m_attn_sinks, m_rel_bias, m_conv_b, m_conv_ln_g,
               m_conv_ln_b, m_norm_ffn_g]
    small_v = [v_b_ada, v_norm_mix_g, v_q_norm_g, v_k_norm_g, v_attn_sinks, v_rel_bias, v_conv_b, v_conv_ln_g,
               v_conv_ln_b, v_norm_ffn_g]
    small_g = [dmod, acc_m[2:3], dqg, dkg, dsinks.reshape(1, nq), drel.reshape(1, NUM_BUCKETS * nq),
               dconv_vec[0:1], dconv_vec[1:2], dconv_vec[2:3], acc_f[2:3]]
    row3, offs3 = _row_pack(small_g + [dconv_w[:CONV_WIDTH].reshape(1, CONV_WIDTH * ch), loss11])
    got3 = run(allgather_small, row3, "allgather_small_grads")
    cx_in = rs_chip_start("in", ["w_in"], px_in)
    as_row = lambda a: a.reshape(1, -1)
    outs3 = run(small_sum_adamw, got3, offs3, [as_row(a) for a in small_w], [as_row(a) for a in small_m],
                [as_row(a) for a in small_v], [CONV_WIDTH * ch, 1])
    for i, (n, w) in enumerate(zip(small_names, small_w)):
        grads[n], deltas[n], new_m[n], new_v[n] = (o.reshape(w.shape) for o in outs3[4 * i:4 * i + 4])
    g_conv_w_all, loss_sum = outs3[-2].reshape(CONV_WIDTH, ch), outs3[-1]

    g_conv_w = lax.dynamic_slice_in_dim(g_conv_w_all, chip * ch_loc, ch_loc, axis=1)
    grads["conv_w"] = g_conv_w[None]
    dl, nm, nv = run(adamw, conv_w[0], g_conv_w, m_conv_w[0], v_conv_w[0], "adamw_conv_w")
    deltas["conv_w"], new_m["conv_w"], new_v["conv_w"] = dl[None], nm[None], nv[None]

    dmod_all = got3[:, 0, :N_MOD * d]
    dmod_cols = lax.dynamic_slice_in_dim(dmod_all, chip * nc_ada, nc_ada, axis=1)
    g_ada, dl, nm, nv = run(ada_grad_adamw, c_t, dmod_cols, w_ada[0], m_w_ada[0], v_w_ada[0])
    grads["w_ada"], deltas["w_ada"], new_m["w_ada"], new_v["w_ada"] = g_ada[None], dl[None], nm[None], nv[None]

    rs_finish("ffn_in", ["w_ffn_in"], sh_ffn_in)
    rs_finish("mix_out", ["w_mix_out"], sh_mix)
    rs_finish("attn_conv_out", ac_names, sh_ac)
    sh_in = rs_share_start("in", ["w_in"], cx_in)
    rs_finish("in", ["w_in"], sh_in)

    loss = loss_sum[0, 0]
    order = ["w_ada", "b_ada", "norm_mix_g", "w_in", "q_norm_g", "k_norm_g", "attn_sinks", "rel_bias", "w_attn_out",
             "conv_w", "conv_b", "conv_ln_g", "conv_ln_b", "w_conv_out", "w_mix_out", "norm_ffn_g", "w_ffn_in",
             "w_ffn_out"]
    return (loss, grad_x[None], *[grads[n] for n in order], *[deltas[n] for n in order],
            *[new_m[n] for n in order], *[new_v[n] for n in order])
```

```python
import functools
import math
from typing import Any, NamedTuple

import jax
import jax.numpy as jnp
import numpy as np
from jax import lax
from jax.experimental import pallas as pl
from jax.experimental.pallas import tpu as pltpu

F32 = jnp.float32
BF16 = jnp.bfloat16
MESH = pl.DeviceIdType.MESH

V7X_VMEM_BYTES = 64 * 1024 * 1024
VMEM_LIMIT = V7X_VMEM_BYTES - 8 * 1024 * 1024
LANES = 128
SUBLANES = 8
BF16_SUBLANES = 16

EPS = 1e-6
WINDOW = 128
BLOCK = 128
NUM_BUCKETS = 32
MAX_EXACT = NUM_BUCKETS // 2
MAX_DISTANCE = 128
CONV_WIDTH = 31
CONV_HALO = 32
ADAM_LR = 0.001
ADAM_B1 = 0.9
ADAM_B2 = 0.999
ADAM_EPS = 1e-08
ADAM_WD = 0.01
ADAM_STEP = 10
N_MOD = 6
SH_M, SC_M, GT_M, SH_F, SC_F, GT_F = range(6)

N_CHIPS = 4
N_DEV = 8

_ANY = pl.BlockSpec(memory_space=pl.ANY)
_VMEM = pl.BlockSpec(memory_space=pltpu.VMEM)
_SMEM = pl.BlockSpec(memory_space=pltpu.SMEM)
_HBM = pl.BlockSpec(memory_space=pltpu.HBM)
_SEM = pl.BlockSpec(memory_space=pltpu.SEMAPHORE)
_EFFECT = pltpu.SideEffectType.DATAFLOW_SIDE_EFFECTING


class InOrder:
    def __init__(self):
        self.token = None

    def __call__(self, fn, *args, **kw):
        return fn(*args, dep=self, **kw)


def _pallas(body, args, *, in_specs, out_specs, out_shape, name, dep=None, grid=(), n_prefetch=0, scratch=(),
            sem=None, **kw):
    n_lead = n_prefetch + len(in_specs)
    in_specs, args = list(in_specs), list(args)
    single = not isinstance(out_shape, (list, tuple))
    out_shapes = [out_shape] if single else list(out_shape)
    out_specs = [out_specs] if single else list(out_specs)
    if dep is not None:
        inner, n_out, takes = body, len(out_shapes), dep.token is not None

        def body(*refs):
            rest = refs[n_lead + (1 if takes else 0):]
            rest[n_out][...] = jnp.zeros((SUBLANES, LANES), F32)
            return inner(*refs[:n_lead], *rest[:n_out], *rest[n_out + 1:])

        if takes:
            in_specs.append(_ANY)
            args.append(dep.token)
        out_shapes.append(jax.ShapeDtypeStruct((SUBLANES, LANES), F32))
        out_specs.append(pl.BlockSpec((SUBLANES, LANES), lambda *_: (0, 0)))
    params = kw.pop("compiler_params", None)
    if params is None:
        params = pltpu.CompilerParams(dimension_semantics=sem, vmem_limit_bytes=VMEM_LIMIT)
    outs = pl.pallas_call(
        body,
        grid_spec=pltpu.PrefetchScalarGridSpec(num_scalar_prefetch=n_prefetch, grid=grid, in_specs=in_specs,
                                               out_specs=out_specs, scratch_shapes=list(scratch)),
        out_shape=out_shapes, compiler_params=params, name=name, **kw,
    )(*args)
    if dep is not None:
        dep.token = outs[-1]
        outs = outs[:-1]
    return outs[0] if single else list(outs)


def _tile(n, pref, unit=LANES):
    best = None
    for t in range(unit, min(n, pref) + 1, unit):
        if n % t == 0:
            best = t
    return best if best is not None else n


def _sigmoid(v):
    return 1.0 / (1.0 + jnp.exp(-v.astype(F32)))


ROW_CHUNK = 512


def _row_chunks(m, unit=SUBLANES):
    step = _tile(m, ROW_CHUNK, unit)
    return [(s, step) for s in range(0, m, step)]


def _ew_tiles(r, n, unit=SUBLANES, elems=512 * 1024):
    return _tile(r, max(unit, elems // n), unit), n


def _block_pos(j, perm):
    if perm is None:
        return j
    pos = 0
    for a, p in enumerate(perm):
        pos = pos + jnp.where(j == a, p, 0)
    return pos


def mm_nn(a, w, *, tn, tk, out_dtype, name, perm=None, dep=None):
    m, k = a.shape
    j, k2, nj = w.shape
    assert k == k2 and nj % tn == 0 and k % tk == 0
    npj, nk = nj // tn, k // tk

    def body(a_ref, w_ref, o_ref, *scratch):
        kk = pl.program_id(1)
        for s, sz in _row_chunks(m):
            rows = pl.ds(s, sz)
            p = jnp.dot(a_ref[rows, :], w_ref[...], preferred_element_type=F32)
            if nk == 1:
                o_ref[rows, :] = p.astype(out_dtype)
            else:
                acc = scratch[0]

                @pl.when(kk == 0)
                def _():
                    acc[rows, :] = p

                @pl.when(kk > 0)
                def _():
                    acc[rows, :] += p

                @pl.when(kk == nk - 1)
                def _():
                    o_ref[rows, :] = acc[rows, :].astype(out_dtype)

    return _pallas(
        body, [a, w], dep=dep, grid=(j * npj, nk),
        in_specs=[
            pl.BlockSpec((m, tk), lambda n, kk: (0, kk)),
            pl.BlockSpec((None, tk, tn), lambda n, kk: (n // npj, kk, n % npj)),
        ],
        out_specs=pl.BlockSpec((m, tn), lambda n, kk: (0, _block_pos(n // npj, perm) * npj + n % npj)),
        out_shape=jax.ShapeDtypeStruct((m, j * nj), out_dtype),
        scratch=[pltpu.VMEM((m, tn), F32)] if nk > 1 else [],
        sem=("parallel", "arbitrary"), name=name)


def mm_nt(g, w, *, tko, tn, name, out_dtype=F32, perm=None, dep=None):
    m, n = g.shape
    j, k, nj = w.shape
    assert n == j * nj and nj % tn == 0 and k % tko == 0
    npj, nr = nj // tn, n // tn
    in_place = out_dtype == F32

    def body(g_ref, w_ref, o_ref, *scratch):
        r = pl.program_id(1)
        acc = o_ref if in_place else (scratch[0] if nr > 1 else None)
        for s, sz in _row_chunks(m):
            rows = pl.ds(s, sz)
            p = lax.dot_general(g_ref[rows, :], w_ref[...], (((1,), (1,)), ((), ())), preferred_element_type=F32)
            if acc is None:
                o_ref[rows, :] = p.astype(out_dtype)
                continue

            @pl.when(r == 0)
            def _():
                acc[rows, :] = p

            @pl.when(r > 0)
            def _():
                acc[rows, :] += p

            if not in_place:
                @pl.when(r == nr - 1)
                def _():
                    o_ref[rows, :] = acc[rows, :].astype(out_dtype)

    return _pallas(
        body, [g, w], dep=dep, grid=(k // tko, nr),
        in_specs=[
            pl.BlockSpec((m, tn), lambda ko, r: (0, _block_pos(r // npj, perm) * npj + r % npj)),
            pl.BlockSpec((None, tko, tn), lambda ko, r: (r // npj, ko, r % npj)),
        ],
        out_specs=pl.BlockSpec((m, tko), lambda ko, r: (0, ko)),
        out_shape=jax.ShapeDtypeStruct((m, k), out_dtype),
        scratch=[pltpu.VMEM((m, tko), F32)] if (nr > 1 and not in_place) else [],
        sem=("parallel", "arbitrary"), name=name)


def mm_tn(a, g, n_blocks, *, tk, tn, name, perm=None, dep=None):
    m, k = a.shape
    m2, n = g.shape
    nj = n // n_blocks
    assert m == m2 and nj % tn == 0 and k % tk == 0
    npj = nj // tn

    def body(a_ref, g_ref, o_ref):
        for s, sz in _row_chunks(tk, LANES):
            p = lax.dot_general(a_ref[:, pl.ds(s, sz)], g_ref[...], (((0,), (0,)), ((), ())),
                                preferred_element_type=F32)
            o_ref[pl.ds(s, sz), :] = p.astype(BF16)

    return _pallas(
        body, [a, g], dep=dep, grid=(k // tk, n // tn),
        in_specs=[
            pl.BlockSpec((m, tk), lambda kk, nn: (0, kk)),
            pl.BlockSpec((m, tn), lambda kk, nn: (0, _block_pos(nn // npj, perm) * npj + nn % npj)),
        ],
        out_specs=pl.BlockSpec((None, tk, tn), lambda kk, nn: (nn // npj, kk, nn % npj)),
        out_shape=jax.ShapeDtypeStruct((n_blocks, k, nj), BF16),
        sem=("parallel", "parallel"), name=name)


ROW_TILE = 256


def _row_spec(tr, width):
    return pl.BlockSpec((tr, width), lambda i: (i, 0))


def _full_spec(shape):
    return pl.BlockSpec(shape, lambda *_: (0,) * len(shape))


def _rms(xv):
    return lax.rsqrt(jnp.mean(xv * xv, axis=-1, keepdims=True) + EPS)


def _mod_row(mod_ref, row):
    return mod_ref[pl.ds(row, 1), :]


def pre_mix_fwd(x, mod, gain, dep=None):
    t, d = x.shape
    tr = _tile(t, ROW_TILE, SUBLANES)

    def body(x_ref, mod_ref, g_ref, h_ref):
        xv = x_ref[...]
        y = xv * _rms(xv) * g_ref[...]
        h_ref[...] = (y * (1.0 + _mod_row(mod_ref, SC_M)) + _mod_row(mod_ref, SH_M)).astype(BF16)

    return _pallas(
        body, [x, mod, gain], dep=dep, grid=(t // tr,),
        in_specs=[_row_spec(tr, d), _full_spec(mod.shape), _full_spec(gain.shape)],
        out_specs=_row_spec(tr, d),
        out_shape=jax.ShapeDtypeStruct((t, d), BF16),
        sem=("parallel",), name="pre_mix_fwd")


def pre_ffn_fwd(x, o_m, mod, gain, dep=None):
    t, d = x.shape
    tr = _tile(t, ROW_TILE, SUBLANES)

    def body(x_ref, om_ref, mod_ref, g_ref, x1_ref, h_ref):
        x1 = x_ref[...] + _mod_row(mod_ref, GT_M) * om_ref[...]
        x1_ref[...] = x1
        y = x1 * _rms(x1) * g_ref[...]
        h_ref[...] = (y * (1.0 + _mod_row(mod_ref, SC_F)) + _mod_row(mod_ref, SH_F)).astype(BF16)

    return _pallas(
        body, [x, o_m, mod, gain], dep=dep, grid=(t // tr,),
        in_specs=[_row_spec(tr, d), _row_spec(tr, d), _full_spec(mod.shape), _full_spec(gain.shape)],
        out_specs=[_row_spec(tr, d), _row_spec(tr, d)],
        out_shape=[jax.ShapeDtypeStruct((t, d), F32), jax.ShapeDtypeStruct((t, d), BF16)],
        sem=("parallel",), name="pre_ffn_fwd")


def loss_head(x1, o_f, target, mod, dep=None):
    t, d = x1.shape
    tr = _tile(t, ROW_TILE, SUBLANES)

    def body(x1_ref, of_ref, tg_ref, mod_ref, loss_ref, dy_ref, dof_ref, acc_ref):
        i = pl.program_id(0)
        gt = _mod_row(mod_ref, GT_F)
        of = of_ref[...].astype(F32)
        err = x1_ref[...] + gt * of - tg_ref[...]
        dy = err * (1.0 / d)
        dy_ref[...] = dy.astype(BF16)
        dof_ref[...] = (dy * gt).astype(BF16)
        part = (0.5 / d) * jnp.sum(jnp.sum(err * err, axis=1, keepdims=True), axis=0, keepdims=True)
        dgt = jnp.sum(dy * of, axis=0, keepdims=True)

        @pl.when(i == 0)
        def _():
            loss_ref[...] = jnp.zeros_like(loss_ref)
            acc_ref[...] = jnp.zeros_like(acc_ref)

        loss_ref[...] += part
        acc_ref[pl.ds(0, 1), :] += dgt

    return _pallas(
        body, [x1, o_f, target, mod], dep=dep, grid=(t // tr,),
        in_specs=[_row_spec(tr, d), _row_spec(tr, d), _row_spec(tr, d), _full_spec(mod.shape)],
        out_specs=[_full_spec((1, 1)), _row_spec(tr, d), _row_spec(tr, d), _full_spec((SUBLANES, d))],
        out_shape=[jax.ShapeDtypeStruct((1, 1), F32), jax.ShapeDtypeStruct((t, d), BF16),
                   jax.ShapeDtypeStruct((t, d), BF16), jax.ShapeDtypeStruct((SUBLANES, d), F32)],
        sem=("arbitrary",), name="loss_head")


def _norm_bwd(xv, dh, sc, gain):
    rstd = _rms(xv)
    yn = xv * rstd
    dsh = jnp.sum(dh, axis=0, keepdims=True)
    dsc = jnp.sum(dh * (yn * gain), axis=0, keepdims=True)
    dgain = jnp.sum(dh * (1.0 + sc) * yn, axis=0, keepdims=True)
    dyn = dh * ((1.0 + sc) * gain)
    dx = rstd * (dyn - yn * jnp.mean(dyn * yn, axis=-1, keepdims=True))
    return dx, dsh, dsc, dgain


def pre_ffn_bwd(x1, dh2, dy, o_m, mod, gain, dep=None):
    t, d = x1.shape
    tr = _tile(t, ROW_TILE, SUBLANES)

    def body(x1_ref, dh_ref, dy_ref, om_ref, mod_ref, g_ref, dx1_ref, dom_ref, acc_ref):
        i = pl.program_id(0)
        dxn, dsh, dsc, dgain = _norm_bwd(x1_ref[...], dh_ref[...].astype(F32), _mod_row(mod_ref, SC_F), g_ref[...])
        dx1 = dy_ref[...] + dxn
        dx1_ref[...] = dx1
        dom_ref[...] = (dx1 * _mod_row(mod_ref, GT_M)).astype(BF16)
        dgt = jnp.sum(dx1 * om_ref[...], axis=0, keepdims=True)

        @pl.when(i == 0)
        def _():
            acc_ref[...] = jnp.zeros_like(acc_ref)

        acc_ref[pl.ds(0, 1), :] += dsh
        acc_ref[pl.ds(1, 1), :] += dsc
        acc_ref[pl.ds(2, 1), :] += dgain
        acc_ref[pl.ds(3, 1), :] += dgt

    return _pallas(
        body, [x1, dh2, dy, o_m, mod, gain], dep=dep, grid=(t // tr,),
        in_specs=[_row_spec(tr, d)] * 4 + [_full_spec(mod.shape), _full_spec(gain.shape)],
        out_specs=[_row_spec(tr, d), _row_spec(tr, d), _full_spec((SUBLANES, d))],
        out_shape=[jax.ShapeDtypeStruct((t, d), F32), jax.ShapeDtypeStruct((t, d), BF16),
                   jax.ShapeDtypeStruct((SUBLANES, d), F32)],
        sem=("arbitrary",), name="pre_ffn_bwd")


def pre_mix_bwd(x, dh, dx1, mod, gain, dep=None):
    t, d = x.shape
    tr = _tile(t, ROW_TILE, SUBLANES)

    def body(x_ref, dh_ref, dx1_ref, mod_ref, g_ref, gx_ref, acc_ref):
        i = pl.program_id(0)
        dxn, dsh, dsc, dgain = _norm_bwd(x_ref[...], dh_ref[...].astype(F32), _mod_row(mod_ref, SC_M), g_ref[...])
        gx_ref[...] = dx1_ref[...] + dxn

        @pl.when(i == 0)
        def _():
            acc_ref[...] = jnp.zeros_like(acc_ref)

        acc_ref[pl.ds(0, 1), :] += dsh
        acc_ref[pl.ds(1, 1), :] += dsc
        acc_ref[pl.ds(2, 1), :] += dgain

    return _pallas(
        body, [x, dh, dx1, mod, gain], dep=dep, grid=(t // tr,),
        in_specs=[_row_spec(tr, d)] * 3 + [_full_spec(mod.shape), _full_spec(gain.shape)],
        out_specs=[_row_spec(tr, d), _full_spec((SUBLANES, d))],
        out_shape=[jax.ShapeDtypeStruct((t, d), F32), jax.ShapeDtypeStruct((SUBLANES, d), F32)],
        sem=("arbitrary",), name="pre_mix_bwd")


def merge_fwd(p, y_attn, y_conv, off_ga, off_gc, dep=None):
    t, d = y_attn.shape
    tr = _tile(t, ROW_TILE, SUBLANES)
    cw = math.gcd(math.gcd(off_ga, off_gc), math.gcd(d, 512))
    nc = d // cw

    def body(ga_ref, gc_ref, ya_ref, yc_ref, o_ref):
        o_ref[...] = (_sigmoid(ga_ref[...]) * ya_ref[...] + _sigmoid(gc_ref[...]) * yc_ref[...]).astype(BF16)

    return _pallas(
        body, [p, p, y_attn, y_conv], dep=dep, grid=(t // tr, nc),
        in_specs=[pl.BlockSpec((tr, cw), lambda i, j: (i, off_ga // cw + j)),
                  pl.BlockSpec((tr, cw), lambda i, j: (i, off_gc // cw + j)),
                  pl.BlockSpec((tr, cw), lambda i, j: (i, j)),
                  pl.BlockSpec((tr, cw), lambda i, j: (i, j))],
        out_specs=pl.BlockSpec((tr, cw), lambda i, j: (i, j)),
        out_shape=jax.ShapeDtypeStruct((t, d), BF16),
        sem=("parallel", "parallel"), name="merge_fwd")


def merge_bwd(p, y_attn, y_conv, dmerged, off_ga, off_gc, dep=None):
    t, d = y_attn.shape
    tr = _tile(t, ROW_TILE, SUBLANES)
    cw = math.gcd(math.gcd(off_ga, off_gc), math.gcd(d, 512))
    nc = d // cw

    def body(ga_ref, gc_ref, ya_ref, yc_ref, dm_ref, dya_ref, dyc_ref, dga_ref, dgc_ref):
        dm = dm_ref[...].astype(F32)
        sa = _sigmoid(ga_ref[...])
        sc = _sigmoid(gc_ref[...])
        dya_ref[...] = (dm * sa).astype(BF16)
        dyc_ref[...] = (dm * sc).astype(BF16)
        dga_ref[...] = (dm * ya_ref[...] * sa * (1.0 - sa)).astype(BF16)
        dgc_ref[...] = (dm * yc_ref[...] * sc * (1.0 - sc)).astype(BF16)

    blk = pl.BlockSpec((tr, cw), lambda i, j: (i, j))
    return _pallas(
        body, [p, p, y_attn, y_conv, dmerged], dep=dep, grid=(t // tr, nc),
        in_specs=[pl.BlockSpec((tr, cw), lambda i, j: (i, off_ga // cw + j)),
                  pl.BlockSpec((tr, cw), lambda i, j: (i, off_gc // cw + j)), blk, blk, blk],
        out_specs=[blk] * 4,
        out_shape=[jax.ShapeDtypeStruct((t, d), BF16)] * 4,
        sem=("parallel", "parallel"), name="merge_bwd")


def ffn_perm(n_blocks):
    half = n_blocks // 2
    return tuple(2 * j if j < half else 2 * (j - half) + 1 for j in range(n_blocks))


def swiglu_fwd(f, nj, dep=None):
    t, two = f.shape
    tr = _tile(t, ROW_TILE, SUBLANES)
    npair = two // (2 * nj)

    def body(f_ref, o_ref):
        g = f_ref[:, :nj].astype(F32)
        u = f_ref[:, nj:].astype(F32)
        o_ref[...] = (g * _sigmoid(g) * u).astype(BF16)

    return _pallas(
        body, [f], dep=dep, grid=(t // tr, npair),
        in_specs=[pl.BlockSpec((tr, 2 * nj), lambda i, j: (i, j))],
        out_specs=pl.BlockSpec((tr, nj), lambda i, j: (i, j)),
        out_shape=jax.ShapeDtypeStruct((t, two // 2), BF16),
        sem=("parallel", "parallel"), name="swiglu_fwd")


def swiglu_bwd(f, dact, nj, dep=None):
    t, two = f.shape
    tr = _tile(t, ROW_TILE, SUBLANES)
    npair = two // (2 * nj)

    def body(f_ref, da_ref, o_ref):
        g = f_ref[:, :nj].astype(F32)
        u = f_ref[:, nj:].astype(F32)
        da = da_ref[...]
        s = _sigmoid(g)
        o_ref[:, :nj] = (da * u * (s * (1.0 + g * (1.0 - s)))).astype(BF16)
        o_ref[:, nj:] = (da * (g * s)).astype(BF16)

    return _pallas(
        body, [f, dact], dep=dep, grid=(t // tr, npair),
        in_specs=[pl.BlockSpec((tr, 2 * nj), lambda i, j: (i, j)), pl.BlockSpec((tr, nj), lambda i, j: (i, j))],
        out_specs=pl.BlockSpec((tr, 2 * nj), lambda i, j: (i, j)),
        out_shape=jax.ShapeDtypeStruct((t, two), BF16),
        sem=("parallel", "parallel"), name="swiglu_bwd")


def _t5_bucket_table():
    q_off = np.arange(BLOCK)
    k_off = np.arange(2 * BLOCK)
    dist = q_off[:, None] + BLOCK - k_off[None, :]
    n = np.maximum(dist, 0)
    nf = np.maximum(n, 1).astype(np.float32)
    large = MAX_EXACT + (np.log(nf / np.float32(MAX_EXACT)) / np.float32(math.log(MAX_DISTANCE / MAX_EXACT))
                         * np.float32(NUM_BUCKETS - MAX_EXACT)).astype(np.int32)
    large = np.minimum(large, NUM_BUCKETS - 1)
    bucket = np.where(n < MAX_EXACT, n, large).astype(np.int32)
    allowed = (dist >= 0) & (dist < WINDOW)
    return np.where(allowed, bucket, -1).astype(np.int32)


def bias_table(rel_bias, bucket_p, bucket_c, dep=None):
    nb, nq = rel_bias.shape

    def body(rb_ref, bkp_ref, bkc_ref, op_ref, oc_ref):
        for bk_ref, o_ref in ((bkp_ref, op_ref), (bkc_ref, oc_ref)):
            bk = bk_ref[...]
            for h in range(nq):
                acc = jnp.full(bk.shape, -jnp.inf, F32)
                for b in range(nb):
                    acc = jnp.where(bk == b, rb_ref[b, h], acc)
                o_ref[h] = acc

    return _pallas(
        body, [rel_bias, bucket_p, bucket_c], dep=dep,
        in_specs=[_SMEM, _VMEM, _VMEM], out_specs=[_VMEM, _VMEM],
        out_shape=[jax.ShapeDtypeStruct((nq,) + bucket_p.shape, F32)] * 2,
        name="bias_table")


def bias_table_bwd(dbp, dbc, bucket_p, bucket_c, dep=None):
    nq = dbp.shape[0]

    def body(dbp_ref, dbc_ref, bkp_ref, bkc_ref, o_ref):
        bkp, bkc = bkp_ref[...][None], bkc_ref[...][None]
        dp, dc = dbp_ref[...], dbc_ref[...]
        for b in range(NUM_BUCKETS):
            sel = jnp.where(bkp == b, dp, 0.0) + jnp.where(bkc == b, dc, 0.0)
            o_ref[b] = jnp.sum(jnp.sum(sel, axis=2, keepdims=True), axis=1, keepdims=True)

    return _pallas(
        body, [dbp, dbc, bucket_p, bucket_c], dep=dep,
        in_specs=[_VMEM] * 4, out_specs=_VMEM,
        out_shape=jax.ShapeDtypeStruct((NUM_BUCKETS, nq, 1, 1), F32),
        name="bias_table_bwd")


_BNT = (((2,), (2,)), ((0,), (0,)))
_BNN = (((2,), (1,)), ((0,), (0,)))
_BTN = (((1,), (1,)), ((0,), (0,)))


@jax.custom_vjp
def _bdot_nt(a, b):
    return lax.dot_general(a.astype(BF16), b.astype(BF16), _BNT, preferred_element_type=F32)


def _bdot_nt_fwd(a, b):
    return _bdot_nt(a, b), (a, b)


def _bdot_nt_bwd(res, g):
    a, b = res
    gb = g.astype(BF16)
    da = lax.dot_general(gb, b.astype(BF16), _BNN, preferred_element_type=F32)
    db = lax.dot_general(gb, a.astype(BF16), _BTN, preferred_element_type=F32)
    return da, db


_bdot_nt.defvjp(_bdot_nt_fwd, _bdot_nt_bwd)


@jax.custom_vjp
def _bdot_nn(a, b):
    return lax.dot_general(a.astype(BF16), b.astype(BF16), _BNN, preferred_element_type=F32)


def _bdot_nn_fwd(a, b):
    return _bdot_nn(a, b), (a, b)


def _bdot_nn_bwd(res, g):
    a, b = res
    gb = g.astype(BF16)
    da = lax.dot_general(gb, b.astype(BF16), _BNT, preferred_element_type=F32)
    db = lax.dot_general(a.astype(BF16), gb, _BTN, preferred_element_type=F32)
    return da, db


_bdot_nn.defvjp(_bdot_nn_fwd, _bdot_nn_bwd)


def _attn_math(q, kp, kc, vp, vc, bp, bc, sinks, qg, kg, *, prev_ok, scale):
    h, rows, _ = q.shape
    b = kp.shape[1]
    qn = q * _rms(q) * qg
    kpn = kp * _rms(kp) * kg
    kcn = kc * _rms(kc) * kg
    lp = _bdot_nt(qn, kpn) * scale + bp.reshape(h, rows, b)
    lc = _bdot_nt(qn, kcn) * scale + bc.reshape(h, rows, b)
    lp = jnp.where(prev_ok, lp, -jnp.inf)
    sink = jnp.broadcast_to(sinks, (sinks.shape[0], b, 1)).reshape(h, rows, 1)
    m = jnp.maximum(jnp.maximum(jnp.max(lp, axis=-1, keepdims=True), jnp.max(lc, axis=-1, keepdims=True)), sink)
    m = lax.stop_gradient(m)
    pp = jnp.exp(lp - m)
    pc = jnp.exp(lc - m)
    den = jnp.sum(pp, axis=-1, keepdims=True) + jnp.sum(pc, axis=-1, keepdims=True) + jnp.exp(sink - m)
    inv = 1.0 / den
    return _bdot_nn(pp * inv, vp) + _bdot_nn(pc * inv, vc)


def _attn_specs(p, aw, kvw, nq, hd, nblk, reverse):
    assert aw % (2 * kvw) == 0
    kv_col = aw // (2 * kvw)

    def blk(n):
        return nblk - 1 - n if reverse else n

    return [
        pl.BlockSpec((BLOCK, aw), lambda n: (blk(n), 0)),
        pl.BlockSpec((BLOCK, 2 * kvw), lambda n: (jnp.maximum(blk(n) - 1, 0), kv_col)),
        pl.BlockSpec((BLOCK, 2 * kvw), lambda n: (blk(n), kv_col)),
        _full_spec((nq, BLOCK, BLOCK)), _full_spec((nq, BLOCK, BLOCK)), _full_spec((nq, 1, 1)),
        _full_spec((1, hd)), _full_spec((1, hd)),
    ]


def _head_major(ref, n_heads, grp, hd, offset=0):
    return jnp.stack([
        jnp.concatenate([ref[:, pl.ds(offset + (grp * h + g) * hd, hd)].astype(F32) for g in range(grp)], axis=0)
        for h in range(n_heads)])


def _attn_inputs(nkv, grp, hd, kvw, q_ref, kvp_ref, kvc_ref):
    return (_head_major(q_ref, nkv, grp, hd), _head_major(kvp_ref, nkv, 1, hd), _head_major(kvc_ref, nkv, 1, hd),
            _head_major(kvp_ref, nkv, 1, hd, kvw), _head_major(kvc_ref, nkv, 1, hd, kvw))


def attn_fwd(p, bias_p, bias_c, sinks, qg, kg, *, aw, kvw, dep=None):
    t, hd = p.shape[0], qg.shape[-1]
    nq, nkv, nblk = aw // hd, kvw // hd, t // BLOCK
    grp = nq // nkv
    scale = hd ** -0.5

    def body(q_ref, kvp_ref, kvc_ref, bp_ref, bc_ref, s_ref, qg_ref, kg_ref, o_ref):
        prev_ok = pl.program_id(0) > 0
        out = _attn_math(*_attn_inputs(nkv, grp, hd, kvw, q_ref, kvp_ref, kvc_ref), bp_ref[...], bc_ref[...],
                         s_ref[...], qg_ref[...], kg_ref[...], prev_ok=prev_ok, scale=scale)
        for h in range(nkv):
            for g in range(grp):
                o_ref[:, pl.ds((grp * h + g) * hd, hd)] = out[h, g * BLOCK:(g + 1) * BLOCK].astype(BF16)

    return _pallas(
        body, [p, p, p, bias_p, bias_c, sinks, qg, kg], dep=dep, grid=(nblk,),
        in_specs=_attn_specs(p, aw, kvw, nq, hd, nblk, False),
        out_specs=pl.BlockSpec((BLOCK, aw), lambda n: (n, 0)),
        out_shape=jax.ShapeDtypeStruct((t, aw), BF16),
        sem=("parallel",), name="attn_fwd")


def attn_bwd(p, bias_p, bias_c, sinks, qg, kg, do, *, aw, kvw, dep=None):
    t, hd = p.shape[0], qg.shape[-1]
    nq, nkv, nblk = aw // hd, kvw // hd, t // BLOCK
    grp = nq // nkv
    scale = hd ** -0.5

    def body(q_ref, kvp_ref, kvc_ref, bp_ref, bc_ref, s_ref, qg_ref, kg_ref, do_ref,
             dqkv_ref, dbp_ref, dbc_ref, ds_ref, dqg_ref, dkg_ref, carry):
        i = pl.program_id(0)
        prev_ok = (nblk - 1 - i) > 0

        @pl.when(i == 0)
        def _():
            carry[...] = jnp.zeros_like(carry)
            dbp_ref[...] = jnp.zeros_like(dbp_ref)
            dbc_ref[...] = jnp.zeros_like(dbc_ref)
            ds_ref[...] = jnp.zeros_like(ds_ref)
            dqg_ref[...] = jnp.zeros_like(dqg_ref)
            dkg_ref[...] = jnp.zeros_like(dkg_ref)

        fn = functools.partial(_attn_math, prev_ok=prev_ok, scale=scale)
        _, vjp = jax.vjp(fn, *_attn_inputs(nkv, grp, hd, kvw, q_ref, kvp_ref, kvc_ref), bp_ref[...], bc_ref[...],
                         s_ref[...], qg_ref[...], kg_ref[...])
        dq, dkp, dkc, dvp, dvc, dbp, dbc, dsk, dqg, dkg = vjp(_head_major(do_ref, nkv, grp, hd))
        for h in range(nkv):
            for g in range(grp):
                dqkv_ref[:, pl.ds((grp * h + g) * hd, hd)] = dq[h, g * BLOCK:(g + 1) * BLOCK].astype(BF16)
            k_cols, v_cols = pl.ds(h * hd, hd), pl.ds(kvw + h * hd, hd)
            dqkv_ref[:, pl.ds(aw + h * hd, hd)] = (dkc[h] + carry[:, k_cols]).astype(BF16)
            dqkv_ref[:, pl.ds(aw + kvw + h * hd, hd)] = (dvc[h] + carry[:, v_cols]).astype(BF16)
            carry[:, k_cols] = dkp[h]
            carry[:, v_cols] = dvp[h]
        dbp_ref[...] += dbp
        dbc_ref[...] += dbc
        ds_ref[...] += dsk
        dqg_ref[...] += dqg
        dkg_ref[...] += dkg

    return _pallas(
        body, [p, p, p, bias_p, bias_c, sinks, qg, kg, do], dep=dep, grid=(nblk,),
        in_specs=_attn_specs(p, aw, kvw, nq, hd, nblk, True)
        + [pl.BlockSpec((BLOCK, aw), lambda n: (nblk - 1 - n, 0))],
        out_specs=[
            pl.BlockSpec((BLOCK, aw + 2 * kvw), lambda n: (nblk - 1 - n, 0)),
            _full_spec((nq, BLOCK, BLOCK)), _full_spec((nq, BLOCK, BLOCK)), _full_spec((nq, 1, 1)),
            _full_spec((1, hd)), _full_spec((1, hd)),
        ],
        out_shape=[
            jax.ShapeDtypeStruct((t, aw + 2 * kvw), BF16),
            jax.ShapeDtypeStruct((nq, BLOCK, BLOCK), F32),
            jax.ShapeDtypeStruct((nq, BLOCK, BLOCK), F32),
            jax.ShapeDtypeStruct((nq, 1, 1), F32),
            jax.ShapeDtypeStruct((1, hd), F32),
            jax.ShapeDtypeStruct((1, hd), F32),
        ],
        scratch=[pltpu.VMEM((BLOCK, 2 * kvw), F32)],
        sem=("arbitrary",), name="attn_bwd")


CONV_TILE = 256


def _conv_halo_specs(tb, ch, nblk):
    per = tb // CONV_HALO
    last = nblk * per - 1
    cur = pl.BlockSpec((tb, ch), lambda n: (n, 0))
    prev = pl.BlockSpec((CONV_HALO, ch), lambda n: (jnp.maximum(n * per - 1, 0), 0))
    nxt = pl.BlockSpec((CONV_HALO, ch), lambda n: (jnp.minimum((n + 1) * per, last), 0))
    return cur, prev, nxt


def _ln_silu(co, ln_g, ln_b):
    mu = jnp.mean(co, axis=-1, keepdims=True)
    cen = co - mu
    rstd = lax.rsqrt(jnp.mean(cen * cen, axis=-1, keepdims=True) + EPS)
    xhat = cen * rstd
    z = xhat * ln_g + ln_b
    return xhat, rstd, z


def _shifted_copies(src, shifted):
    rows = src.shape[0] - SUBLANES
    for r in range(1, SUBLANES):
        shifted[r, pl.ds(0, rows), :] = src[pl.ds(r, rows), :]


def _rows_from(src, shifted, start, n):
    r = start % SUBLANES
    if r == 0:
        return src[pl.ds(start, n), :]
    return shifted[r, pl.ds(start - r, n), :]


def conv_fwd(ca, cb, conv_w, conv_b, ln_g, ln_b, dep=None):
    t, ch = ca.shape
    tb = _tile(t, CONV_TILE, CONV_HALO)
    nblk = t // tb
    cur, prev, _ = _conv_halo_specs(tb, ch, nblk)
    lead = CONV_HALO - (CONV_WIDTH - 1)

    def body(ca_ref, cb_ref, cap_ref, cbp_ref, w_ref, b_ref, g_ref, bb_ref, s_ref, co_ref, ubuf, ushift):
        n = pl.program_id(0)
        halo = cap_ref[...] * _sigmoid(cbp_ref[...])
        ubuf[pl.ds(0, CONV_HALO), :] = jnp.where(n > 0, halo, 0.0)
        ubuf[pl.ds(CONV_HALO, tb), :] = ca_ref[...] * _sigmoid(cb_ref[...])
        _shifted_copies(ubuf, ushift)
        acc = jnp.broadcast_to(b_ref[...], (tb, ch))
        for k in range(CONV_WIDTH):
            acc = acc + w_ref[pl.ds(k, 1), :] * _rows_from(ubuf, ushift, lead + k, tb)
        co_ref[...] = acc
        _, _, z = _ln_silu(acc, g_ref[...], bb_ref[...])
        s_ref[...] = (z * _sigmoid(z)).astype(BF16)

    vec = _full_spec((1, ch))
    return _pallas(
        body, [ca, cb, ca, cb, conv_w, conv_b, ln_g, ln_b], dep=dep, grid=(nblk,),
        in_specs=[cur, cur, prev, prev, _full_spec(conv_w.shape), vec, vec, vec],
        out_specs=[cur, cur],
        out_shape=[jax.ShapeDtypeStruct((t, ch), BF16), jax.ShapeDtypeStruct((t, ch), F32)],
        scratch=[pltpu.VMEM((CONV_HALO + tb, ch), F32), pltpu.VMEM((SUBLANES, CONV_HALO + tb, ch), F32)],
        sem=("parallel",), name="conv_fwd")


def conv_bwd(ca, cb, co, ds, conv_w, ln_g, ln_b, dep=None):
    t, ch = ca.shape
    tb = _tile(t, CONV_TILE, CONV_HALO)
    nblk = t // tb
    cur, prev, nxt = _conv_halo_specs(tb, ch, nblk)
    lead = CONV_HALO - (CONV_WIDTH - 1)
    ext = tb + CONV_HALO

    def body(ca_ref, cb_ref, cap_ref, cbp_ref, co_ref, con_ref, ds_ref, dsn_ref, w_ref, g_ref, bb_ref,
             dca_ref, dcb_ref, dw_ref, dvec_ref, ubuf, dbuf, ushift, dshift):
        n = pl.program_id(0)
        is_last = n == nblk - 1
        sig_b = _sigmoid(cb_ref[...])
        cav = ca_ref[...].astype(F32)
        ubuf[pl.ds(0, CONV_HALO), :] = jnp.where(n > 0, cap_ref[...] * _sigmoid(cbp_ref[...]), 0.0)
        ubuf[pl.ds(CONV_HALO, tb), :] = cav * sig_b
        _shifted_copies(ubuf, ushift)
        co = jnp.concatenate([co_ref[...], con_ref[...]], axis=0)
        xhat, rstd, z = _ln_silu(co, g_ref[...], bb_ref[...])
        dsv = jnp.concatenate([ds_ref[...].astype(F32), jnp.where(is_last, 0.0, dsn_ref[...].astype(F32))], axis=0)
        sg = _sigmoid(z)
        dz = dsv * (sg * (1.0 + z * (1.0 - sg)))
        dxh = dz * g_ref[...]
        dco = rstd * (dxh - jnp.mean(dxh, axis=-1, keepdims=True)
                      - xhat * jnp.mean(dxh * xhat, axis=-1, keepdims=True))
        dbuf[...] = dco
        _shifted_copies(dbuf, dshift)

        @pl.when(n == 0)
        def _():
            dw_ref[...] = jnp.zeros_like(dw_ref)
            dvec_ref[...] = jnp.zeros_like(dvec_ref)

        dco_cur = dco[:tb]
        dvec_ref[pl.ds(0, 1), :] += jnp.sum(dco_cur, axis=0, keepdims=True)
        dvec_ref[pl.ds(1, 1), :] += jnp.sum(dz[:tb] * xhat[:tb], axis=0, keepdims=True)
        dvec_ref[pl.ds(2, 1), :] += jnp.sum(dz[:tb], axis=0, keepdims=True)
        du = jnp.zeros((tb, ch), F32)
        for k in range(CONV_WIDTH):
            du = du + w_ref[pl.ds(k, 1), :] * _rows_from(dbuf, dshift, CONV_WIDTH - 1 - k, tb)
            dw_ref[pl.ds(k, 1), :] += jnp.sum(dco_cur * _rows_from(ubuf, ushift, lead + k, tb), axis=0,
                                              keepdims=True)
        dca_ref[...] = (du * sig_b).astype(BF16)
        dcb_ref[...] = (du * cav * sig_b * (1.0 - sig_b)).astype(BF16)

    vec = _full_spec((1, ch))
    return _pallas(
        body, [ca, cb, ca, cb, co, co, ds, ds, conv_w, ln_g, ln_b], dep=dep, grid=(nblk,),
        in_specs=[cur, cur, prev, prev, cur, nxt, cur, nxt, _full_spec(conv_w.shape), vec, vec],
        out_specs=[cur, cur, _full_spec(conv_w.shape), _full_spec((SUBLANES, ch))],
        out_shape=[jax.ShapeDtypeStruct((t, ch), BF16), jax.ShapeDtypeStruct((t, ch), BF16),
                   jax.ShapeDtypeStruct(conv_w.shape, F32), jax.ShapeDtypeStruct((SUBLANES, ch), F32)],
        scratch=[pltpu.VMEM((CONV_HALO + tb, ch), F32), pltpu.VMEM((ext, ch), F32),
                 pltpu.VMEM((SUBLANES, CONV_HALO + tb, ch), F32), pltpu.VMEM((SUBLANES, ext, ch), F32)],
        sem=("arbitrary",), name="conv_bwd")


def ada_fwd(c_t, w_ada, dep=None):
    d, nc = w_ada.shape
    nex = c_t.shape[1]
    tn = _tile(nc, 512)

    def body(ct_ref, w_ref, o_ref):
        w = w_ref[...]
        ct = ct_ref[...]
        cact = ct * _sigmoid(ct)
        rows = [jnp.sum(w * cact[:, b:b + 1], axis=0, keepdims=True) for b in range(nex)]
        o_ref[...] = jnp.concatenate(rows, axis=0)

    return _pallas(
        body, [c_t, w_ada], dep=dep, grid=(nc // tn,),
        in_specs=[_full_spec(c_t.shape), pl.BlockSpec((d, tn), lambda j: (0, j))],
        out_specs=pl.BlockSpec((nex, tn), lambda j: (0, j)),
        out_shape=jax.ShapeDtypeStruct((nex, nc), F32),
        sem=("parallel",), name="ada_fwd")


def _adamw_math(w, g, m, v):
    m = ADAM_B1 * m + (1.0 - ADAM_B1) * g
    v = ADAM_B2 * v + (1.0 - ADAM_B2) * (g * g)
    m_hat = m / (1.0 - ADAM_B1 ** ADAM_STEP)
    v_hat = v / (1.0 - ADAM_B2 ** ADAM_STEP)
    delta = -ADAM_LR * (m_hat / (jnp.sqrt(v_hat) + ADAM_EPS) + ADAM_WD * w)
    return delta, m, v


def adamw(w, g, m, v, name, copy_grad=False, dep=None):
    r, n = w.shape
    tr, tn = _ew_tiles(r, n, elems=256 * 1024)
    n_out = 4 if copy_grad else 3

    def body(w_ref, g_ref, m_ref, v_ref, *outs):
        g = g_ref[...]
        if copy_grad:
            outs[0][...] = g
        outs[-3][...], outs[-2][...], outs[-1][...] = _adamw_math(w_ref[...], g, m_ref[...], v_ref[...])

    blk = pl.BlockSpec((tr, tn), lambda i, j: (i, j))
    return _pallas(
        body, [w, g, m, v], dep=dep, grid=(r // tr, n // tn),
        in_specs=[blk] * 4, out_specs=[blk] * n_out,
        out_shape=[jax.ShapeDtypeStruct((r, n), F32)] * n_out,
        sem=("parallel", "parallel"), name=name)


def ada_grad_adamw(c_t, dmod_cols, w, m, v, dep=None):
    d, nc = w.shape
    nex = c_t.shape[1]
    tr, tn = _ew_tiles(d, nc, elems=256 * 1024)

    def body(ct_ref, dm_ref, w_ref, m_ref, v_ref, g_ref, d_ref, nm_ref, nv_ref):
        ct = ct_ref[...]
        cact = ct * _sigmoid(ct)
        dm = dm_ref[...]
        g = cact[:, 0:1] * dm[0:1, :]
        for b in range(1, nex):
            g = g + cact[:, b:b + 1] * dm[b:b + 1, :]
        g_ref[...] = g
        d_ref[...], nm_ref[...], nv_ref[...] = _adamw_math(w_ref[...], g, m_ref[...], v_ref[...])

    blk = pl.BlockSpec((tr, tn), lambda i, j: (i, j))
    return _pallas(
        body, [c_t, dmod_cols, w, m, v], dep=dep, grid=(d // tr, nc // tn),
        in_specs=[pl.BlockSpec((tr, nex), lambda i, j: (i, 0)), pl.BlockSpec((nex, tn), lambda i, j: (0, j)),
                  blk, blk, blk],
        out_specs=[blk] * 4,
        out_shape=[jax.ShapeDtypeStruct((d, nc), F32)] * 4,
        sem=("parallel", "parallel"), name="ada_grad_adamw")


def _row_pack(parts):
    cols, offs, off = [], [], 0
    for p in parts:
        n = p.shape[1]
        width = -(-n // LANES) * LANES
        cols.append(jnp.pad(p, ((0, 0), (0, width - n))) if width != n else p)
        offs.append(off)
        off += width
    return jnp.concatenate(cols, axis=1), offs


def small_sum_adamw(gathered, offs, ws, ms, vs, extra_widths, dep=None):
    ndev = gathered.shape[0]
    npar = len(ws)

    def body(ga_ref, *refs):
        w_refs, m_refs, v_refs = refs[:npar], refs[npar:2 * npar], refs[2 * npar:3 * npar]
        outs = refs[3 * npar:]
        tot = ga_ref[0]
        for s in range(1, ndev):
            tot = tot + ga_ref[s]
        for i in range(npar):
            n = ws[i].shape[1]
            g = tot[:, offs[i]:offs[i] + n]
            outs[4 * i][...] = g
            outs[4 * i + 1][...], outs[4 * i + 2][...], outs[4 * i + 3][...] = _adamw_math(
                w_refs[i][...], g, m_refs[i][...], v_refs[i][...])
        for e, n in enumerate(extra_widths):
            off = offs[npar + e]
            outs[4 * npar + e][...] = tot[:, off:off + n]

    shapes = [jax.ShapeDtypeStruct(w.shape, F32) for w in ws for _ in range(4)]
    shapes += [jax.ShapeDtypeStruct((1, n), F32) for n in extra_widths]
    return _pallas(
        body, [gathered, *ws, *ms, *vs], dep=dep, in_specs=[_VMEM] * (1 + 3 * npar), out_specs=[_VMEM] * len(shapes),
        out_shape=shapes, name="small_sum_adamw")


def _position():
    return lax.axis_index("x"), lax.axis_index("y"), lax.axis_index("c")


def _other_chips(x, y):
    return [(1 - x, y), (x, 1 - y), (1 - x, 1 - y)]


def allgather_small(block, name, dep=None):
    def body(x_ref, out_ref, send_sems, recv_sems, local_sem):
        x, y, c = _position()
        me, sibling = (x, y, c), (x, y, 1 - c)
        chips = _other_chips(x, y)

        def slot(px, py, pc):
            return out_ref.at[4 * px + 2 * py + pc]

        def copy(k, block_of, to, src=None):
            return pltpu.make_async_remote_copy(
                src_ref=slot(*block_of) if src is None else src, dst_ref=slot(*block_of),
                send_sem=send_sems.at[k], recv_sem=recv_sems.at[k], device_id=to, device_id_type=MESH)

        mine = pltpu.make_async_copy(x_ref, slot(*me), local_sem)
        mine.start()
        first = [copy(0, me, sibling, src=x_ref)]
        first += [copy(1 + j, me, (*chip, c), src=x_ref) for j, chip in enumerate(chips)]
        for cp in first:
            cp.start()
        passed = [copy(4 + j, (*chip, c), sibling) for j, chip in enumerate(chips)]
        for j, chip in enumerate(chips):
            copy(1 + j, (*chip, c), me).wait_recv()
            passed[j].start()
        copy(0, sibling, me).wait_recv()
        for j, chip in enumerate(chips):
            copy(4 + j, (*chip, 1 - c), me).wait_recv()
        for cp in first + passed:
            cp.wait_send()
        mine.wait()

    return _pallas(
        body, [block], dep=dep,
        out_shape=jax.ShapeDtypeStruct((N_DEV, *block.shape), block.dtype),
        in_specs=[_VMEM], out_specs=_VMEM,
        scratch=[pltpu.SemaphoreType.DMA((7,)), pltpu.SemaphoreType.DMA((7,)), pltpu.SemaphoreType.DMA],
        name=name)


class Started(NamedTuple):
    send_sems: Any
    recv_sems: Any
    bufs: list


def exchange_start(name, bufs, n_copies, plan, dep=None):
    nb = len(bufs)

    def body(*refs):
        for cp in plan(refs[:nb], refs[nb], refs[nb + 1]):
            cp.start()

    outs = _pallas(
        body, [pltpu.with_memory_space_constraint(b, pltpu.HBM) for b in bufs], dep=dep, name=name,
        out_shape=(pltpu.SemaphoreType.DMA((n_copies,)), pltpu.SemaphoreType.DMA((n_copies,)),
                   *[pltpu.HBM(b.shape, b.dtype) for b in bufs]),
        in_specs=[_HBM] * nb,
        out_specs=(_SEM, _SEM, *[_HBM] * nb),
        input_output_aliases={i: 2 + i for i in range(nb)},
        compiler_params=pltpu.CompilerParams(has_side_effects=_EFFECT))
    return Started(outs[0], outs[1], list(outs[2:2 + nb]))


def exchange_wait(name, started, plan, bufs=None, dep=None):
    if bufs is not None:
        started = started._replace(bufs=list(bufs))
    nb = len(started.bufs)

    def body(*refs):
        for cp in plan(refs[:nb], refs[nb], refs[nb + 1]):
            cp.wait_send()
            cp.wait_recv()

    outs = _pallas(
        body, [*started.bufs, started.send_sems, started.recv_sems], dep=dep, name=name,
        out_shape=tuple(pltpu.HBM(b.shape, b.dtype) for b in started.bufs),
        in_specs=[_HBM] * nb + [_SEM, _SEM],
        out_specs=tuple([_HBM] * nb),
        input_output_aliases={i: i for i in range(nb)},
        compiler_params=pltpu.CompilerParams(has_side_effects=_EFFECT))
    return list(outs)


def _remote(src, dst, send_sems, recv_sems, i, to):
    return pltpu.make_async_remote_copy(src_ref=src, dst_ref=dst, send_sem=send_sems.at[i], recv_sem=recv_sems.at[i],
                                        device_id=to, device_id_type=MESH)


def _half_rows(buf_rows, chip_idx, pc):
    half = buf_rows // (2 * N_CHIPS)
    return pl.ds((2 * chip_idx + pc) * half, half)


ALL_PEERS = (0, 1, 2)


def plan_gather_ici(refs, send_sems, recv_sems, peers=ALL_PEERS):
    x, y, c = _position()
    chips = _other_chips(x, y)
    copies = []
    for k, ref in enumerate(refs):
        rows = ref.at[_half_rows(ref.shape[0], 2 * x + y, c), :]
        for i, j in enumerate(peers):
            copies.append(_remote(rows, rows, send_sems, recv_sems, len(peers) * k + i, (*chips[j], c)))
    return copies


def plan_gather_relay(refs, send_sems, recv_sems):
    x, y, c = _position()
    copies = []
    for k, ref in enumerate(refs):
        quarter = ref.shape[0] // (4 * N_CHIPS)
        for i, (src_chip, to) in enumerate((((1 - x, y), (x, 1 - y, c)), ((x, 1 - y), (1 - x, y, c)))):
            start = (2 * (2 * src_chip[0] + src_chip[1]) + c) * 2 * quarter + i * quarter
            rows = ref.at[pl.ds(start, quarter), :]
            copies.append(_remote(rows, rows, send_sems, recv_sems, 2 * k + i, to))
    return copies


def plan_gather_d2d(refs, send_sems, recv_sems, peers=ALL_PEERS):
    x, y, c = _position()
    chips = _other_chips(x, y)
    copies = []
    for k, ref in enumerate(refs):
        for i, j in enumerate(peers):
            px, py = chips[j]
            rows = ref.at[_half_rows(ref.shape[0], 2 * px + py, c), :]
            copies.append(_remote(rows, rows, send_sems, recv_sems, len(peers) * k + i, (x, y, 1 - c)))
    return copies


def plan_pair_exchange(refs, send_sems, recv_sems):
    x, y, c = _position()
    nw = len(refs) // 2
    copies = []
    for k in range(nw):
        for chip in range(N_CHIPS):
            copies.append(_remote(refs[k].at[chip, 1 - c], refs[nw + k].at[chip], send_sems, recv_sems,
                                  N_CHIPS * k + chip, (x, y, 1 - c)))
    return copies


def plan_chip_exchange(refs, send_sems, recv_sems):
    x, y, c = _position()
    nw = len(refs) // 2
    copies = []
    for k in range(nw):
        for j, (px, py) in enumerate(_other_chips(x, y)):
            copies.append(_remote(refs[k].at[2 * px + py], refs[nw + k].at[2 * x + y], send_sems, recv_sems,
                                  3 * k + j, (px, py, c)))
    return copies


def plan_pair_share(refs, send_sems, recv_sems):
    x, y, c = _position()
    return [_remote(ref.at[c], ref.at[c], send_sems, recv_sems, k, (x, y, 1 - c)) for k, ref in enumerate(refs)]


def cast_into_slot(src, slot, n_slots, name, dep=None):
    r, n = src.shape
    tr, tn = _ew_tiles(r, n, BF16_SUBLANES)

    def body(slot_ref, s_ref, o_ref):
        o_ref[...] = s_ref[...].astype(BF16)

    return _pallas(
        body, [slot, src], dep=dep, n_prefetch=1, grid=(r // tr, n // tn),
        in_specs=[pl.BlockSpec((tr, tn), lambda i, j, sl: (i, j))],
        out_specs=pl.BlockSpec((None, tr, tn), lambda i, j, sl: (sl[0], i, j)),
        out_shape=jax.ShapeDtypeStruct((n_slots, r, n), BF16),
        sem=("parallel", "parallel"), name=name)


def pair_sum(g, r, core, name, dep=None):
    nchip, _, h, n = g.shape
    th, tn = _ew_tiles(h, n, BF16_SUBLANES)

    def body(core_ref, g_ref, r_ref, o_ref):
        o_ref[...] = (g_ref[...].astype(F32) + r_ref[...].astype(F32)).astype(BF16)

    return _pallas(
        body, [core, g, r], dep=dep, n_prefetch=1, grid=(nchip, h // th, n // tn),
        in_specs=[pl.BlockSpec((None, None, th, tn), lambda a, i, j, cr: (a, cr[0], i, j)),
                  pl.BlockSpec((None, th, tn), lambda a, i, j, cr: (a, i, j))],
        out_specs=pl.BlockSpec((None, th, tn), lambda a, i, j, cr: (a, i, j)),
        out_shape=jax.ShapeDtypeStruct((nchip, h, n), BF16),
        sem=("parallel", "parallel", "parallel"), name=name)


def chip_sum(own, got, where, name, dep=None):
    nchip, h, n = got.shape
    th, tn = _ew_tiles(h, n, BF16_SUBLANES, elems=256 * 1024)

    def body(where_ref, own_ref, *rest):
        got_refs, o_ref = rest[:nchip], rest[nchip]
        chip = where_ref[0]
        acc = None
        for s in range(nchip):
            term = jnp.where(chip == s, own_ref[...], got_refs[s][...]).astype(F32)
            acc = term if acc is None else acc + term
        o_ref[...] = acc

    def got_spec(s):
        return pl.BlockSpec((None, th, tn), lambda i, j, wr: (jnp.where(wr[0] == s, (s + 1) % nchip, s), i, j))

    return _pallas(
        body, [where, own, *[got] * nchip], dep=dep, n_prefetch=1, grid=(h // th, n // tn),
        in_specs=[pl.BlockSpec((None, th, tn), lambda i, j, wr: (wr[0], i, j))]
        + [got_spec(s) for s in range(nchip)],
        out_specs=pl.BlockSpec((None, th, tn), lambda i, j, wr: (wr[1], i, j)),
        out_shape=jax.ShapeDtypeStruct((2, h, n), F32),
        sem=("parallel", "parallel"), name=name)


def kernel(x, c, w_ada, b_ada, norm_mix_g, w_in, q_norm_g, k_norm_g, attn_sinks, rel_bias, w_attn_out, conv_w, conv_b, conv_ln_g, conv_ln_b, w_conv_out, w_mix_out, norm_ffn_g, w_ffn_in, w_ffn_out, loss_target, m_w_ada, m_b_ada, m_norm_mix_g, m_w_in, m_q_norm_g, m_k_norm_g, m_attn_sinks, m_rel_bias, m_w_attn_out, m_conv_w, m_conv_b, m_conv_ln_g, m_conv_ln_b, m_w_conv_out, m_w_mix_out, m_norm_ffn_g, m_w_ffn_in, m_w_ffn_out, v_w_ada, v_b_ada, v_norm_mix_g, v_w_in, v_q_norm_g, v_k_norm_g, v_attn_sinks, v_rel_bias, v_w_attn_out, v_conv_w, v_conv_b, v_conv_ln_g, v_conv_ln_b, v_w_conv_out, v_w_mix_out, v_norm_ffn_g, v_w_ffn_in, v_w_ffn_out):
    run = InOrder()
    xi, yi, ci = _position()
    chip = 2 * xi + yi
    me = 2 * chip + ci
    chip_arr = chip.astype(jnp.int32).reshape(1)
    core_arr = ci.astype(jnp.int32).reshape(1)
    where_arr = jnp.stack([chip, ci]).astype(jnp.int32)

    xe, tgt = x[0], loss_target[0]
    t, d = xe.shape
    hd = q_norm_g.shape[-1]
    nq = attn_sinks.shape[-1]
    aw = nq * hd
    ch = conv_b.shape[-1]
    in_width = N_CHIPS * w_in.shape[-1]
    kvw = (in_width - aw - 2 * ch - 2 * d) // 2
    nkv = kvw // hd
    dff = N_CHIPS * w_ffn_out.shape[1]
    off_k, off_v, off_ca = aw, aw + kvw, aw + 2 * kvw
    off_cb, off_ga, off_gc = off_ca + ch, off_ca + 2 * ch, off_ca + 2 * ch + d
    nc_ada = w_ada.shape[-1]
    ch_loc = conv_w.shape[-1]
    nj_ffn = w_ffn_in.shape[-1]
    perm_ffn = ffn_perm(N_CHIPS)

    big = {"w_in": w_in[0], "w_attn_out": w_attn_out[0], "w_conv_out": w_conv_out[0], "w_mix_out": w_mix_out[0],
           "w_ffn_in": w_ffn_in[0], "w_ffn_out": w_ffn_out[0]}
    moments = {"w_in": (m_w_in, v_w_in), "w_attn_out": (m_w_attn_out, v_w_attn_out),
               "w_conv_out": (m_w_conv_out, v_w_conv_out), "w_mix_out": (m_w_mix_out, v_w_mix_out),
               "w_ffn_in": (m_w_ffn_in, v_w_ffn_in), "w_ffn_out": (m_w_ffn_out, v_w_ffn_out)}
    gather_groups = {"in": ["w_in"], "branch_out": ["w_attn_out", "w_conv_out"], "mix_out": ["w_mix_out"],
                     "ffn_in": ["w_ffn_in"], "ffn_out": ["w_ffn_out"]}
    grads, deltas, new_m, new_v = {}, {}, {}, {}

    def gather_cast(gname):
        bufs = []
        for n in gather_groups[gname]:
            r, ncol = big[n].shape
            bufs.append(run(cast_into_slot, big[n], chip_arr, N_CHIPS, "cast_" + n).reshape(N_CHIPS * r, ncol))
        return bufs

    def gather_ici_start(gname, bufs):
        return run(exchange_start, "gather_ici_start_" + gname, bufs, 3 * len(bufs), plan_gather_ici)

    def gather_pass_on(gname, ici):
        landed = run(exchange_wait, "gather_ici_wait_" + gname, ici, plan_gather_ici)
        return run(exchange_start, "gather_d2d_start_" + gname, landed, 3 * len(landed), plan_gather_d2d)

    def gathered(gname, d2d):
        outs = run(exchange_wait, "gather_d2d_wait_" + gname, d2d, plan_gather_d2d)
        return [o.reshape(N_CHIPS, *big[n].shape) for o, n in zip(outs, gather_groups[gname])]

    def rs_pair_start(gname, names, partials):
        blocks = [g.reshape(N_CHIPS, 2, big[n].shape[0] // 2, big[n].shape[1]) for n, g in zip(names, partials)]
        land = [lax.empty((N_CHIPS,) + b.shape[2:], BF16) for b in blocks]
        return run(exchange_start, "pair_exchange_start_" + gname, blocks + land, N_CHIPS * len(blocks),
                   plan_pair_exchange)

    def rs_chip_start(gname, names, pair):
        nw = len(names)
        outs = run(exchange_wait, "pair_exchange_wait_" + gname, pair, plan_pair_exchange)
        sums = [run(pair_sum, g, r, core_arr, "pair_sum_" + n) for n, g, r in zip(names, outs[:nw], outs[nw:])]
        land = [lax.empty(s.shape, BF16) for s in sums]
        return run(exchange_start, "chip_exchange_start_" + gname, sums + land, 3 * nw, plan_chip_exchange)

    def rs_share_start(gname, names, chipx):
        nw = len(names)
        outs = run(exchange_wait, "chip_exchange_wait_" + gname, chipx, plan_chip_exchange)
        halves = [run(chip_sum, s, r, where_arr, "chip_sum_" + n) for n, s, r in zip(names, outs[:nw], outs[nw:])]
        return run(exchange_start, "pair_share_start_" + gname, halves, nw, plan_pair_share)

    def rs_finish(gname, names, share):
        fulls = run(exchange_wait, "pair_share_wait_" + gname, share, plan_pair_share)
        for n, g2 in zip(names, fulls):
            g, dl, nm, nv = run(adamw, big[n], g2.reshape(big[n].shape), moments[n][0][0], moments[n][1][0],
                                "adamw_" + n, copy_grad=True)
            grads[n], deltas[n], new_m[n], new_v[n] = g[None], dl[None], nm[None], nv[None]

    near, far = (0, 1), (2,)
    plan_ici_near = functools.partial(plan_gather_ici, peers=near)
    plan_ici_far, n_far = plan_gather_relay, 2
    plan_d2d_near = functools.partial(plan_gather_d2d, peers=near)
    plan_d2d_far = functools.partial(plan_gather_d2d, peers=far)
    bufs_in = gather_cast("in")
    row1, offs1 = _row_pack([c, conv_w[0].reshape(1, CONV_WIDTH * ch_loc)])
    got1 = run(allgather_small, row1, "allgather_cond")
    ici_near = run(exchange_start, "gather_ici_start_in_near", bufs_in, len(near), plan_ici_near)
    c_all = got1[:, 0, :d]
    conv_w_full = got1[0::2, 0, offs1[1]:offs1[1] + CONV_WIDTH * ch_loc].reshape(N_CHIPS, CONV_WIDTH, ch_loc)
    conv_w_full = jnp.transpose(conv_w_full, (1, 0, 2)).reshape(CONV_WIDTH, ch)
    conv_w_pad = jnp.pad(conv_w_full, ((0, 1), (0, 0)))
    c_t = jnp.transpose(c_all)
    mod_cols = run(ada_fwd, c_t, w_ada[0])
    rest_bufs = {gname: gather_cast(gname) for gname in gather_groups if gname != "in"}
    bucket = _t5_bucket_table()
    bucket_p, bucket_c = jnp.asarray(bucket[:, :BLOCK]), jnp.asarray(bucket[:, BLOCK:])
    bias_p, bias_c = run(bias_table, rel_bias, bucket_p, bucket_c)
    got2 = run(allgather_small, mod_cols, "allgather_mod")
    mod_all = got2.reshape(N_CHIPS, 2, N_DEV, nc_ada)[:, 0]
    mod = lax.dynamic_slice_in_dim(mod_all, me, 1, axis=1).reshape(1, N_CHIPS * nc_ada) + b_ada
    mod = jnp.pad(mod.reshape(N_MOD, d), ((0, SUBLANES - N_MOD), (0, 0)))

    landed = run(exchange_wait, "gather_ici_wait_in_near", ici_near, plan_ici_near)
    ici_far = run(exchange_start, "gather_ici_start_in_far", landed, n_far, plan_ici_far)
    d2d_near = run(exchange_start, "gather_d2d_start_in_near", ici_far.bufs, len(near), plan_d2d_near)
    h = run(pre_mix_fwd, xe, mod, norm_mix_g)
    ici = {gname: gather_ici_start(gname, rest_bufs[gname]) for gname in ("branch_out", "mix_out")}
    ici_near_ffn = run(exchange_start, "gather_ici_start_ffn_in_near", rest_bufs["ffn_in"], len(near), plan_ici_near)
    landed = run(exchange_wait, "gather_d2d_wait_in_near", d2d_near, plan_d2d_near)
    landed = run(exchange_wait, "gather_ici_wait_in_far", ici_far, plan_ici_far, bufs=landed)
    d2d_far = run(exchange_start, "gather_d2d_start_in_far", landed, len(far), plan_d2d_far)
    landed = run(exchange_wait, "gather_d2d_wait_in_far", d2d_far, plan_d2d_far)
    wg_in = landed[0].reshape(N_CHIPS, *big["w_in"].shape)
    p = run(mm_nn, h, wg_in, tn=wg_in.shape[2], tk=d, out_dtype=BF16, name="mm_in")
    d2d_branch = gather_pass_on("branch_out", ici["branch_out"])

    sinks3 = attn_sinks.reshape(nq, 1, 1)
    attn_o = run(attn_fwd, p, bias_p, bias_c, sinks3, q_norm_g, k_norm_g, aw=aw, kvw=kvw)
    ca, cb = p[:, off_ca:off_cb], p[:, off_cb:off_ga]
    s_conv, co_conv = run(conv_fwd, ca, cb, conv_w_pad, conv_b, conv_ln_g, conv_ln_b)
    wg_attn_out, wg_conv_out = gathered("branch_out", d2d_branch)
    y_attn = run(mm_nn, attn_o, wg_attn_out, tn=_tile(wg_attn_out.shape[2], 512), tk=aw, out_dtype=BF16,
                 name="mm_attn_out")
    y_conv = run(mm_nn, s_conv, wg_conv_out, tn=_tile(wg_conv_out.shape[2], 512), tk=ch, out_dtype=BF16,
                 name="mm_conv_out")
    landed = run(exchange_wait, "gather_ici_wait_ffn_in_near", ici_near_ffn, plan_ici_near)
    ici_far_ffn = run(exchange_start, "gather_ici_start_ffn_in_far", landed, n_far, plan_ici_far)
    ici["ffn_out"] = gather_ici_start("ffn_out", rest_bufs["ffn_out"])
    d2d_near_ffn = run(exchange_start, "gather_d2d_start_ffn_in_near", ici_far_ffn.bufs, len(near), plan_d2d_near)
    merged = run(merge_fwd, p, y_attn, y_conv, off_ga, off_gc)
    d2d_mix = gather_pass_on("mix_out", ici["mix_out"])
    (wg_mix_out,) = gathered("mix_out", d2d_mix)
    wg_mix_out = wg_mix_out.reshape(1, d, d)
    o_m = run(mm_nn, merged, wg_mix_out, tn=_tile(d, 512), tk=d, out_dtype=BF16, name="mm_mix_out")
    landed = run(exchange_wait, "gather_d2d_wait_ffn_in_near", d2d_near_ffn, plan_d2d_near)
    landed = run(exchange_wait, "gather_ici_wait_ffn_in_far", ici_far_ffn, plan_ici_far, bufs=landed)
    d2d_far_ffn = run(exchange_start, "gather_d2d_start_ffn_in_far", landed, len(far), plan_d2d_far)
    x1, h2 = run(pre_ffn_fwd, xe, o_m, mod, norm_ffn_g)
    landed = run(exchange_wait, "gather_d2d_wait_ffn_in_far", d2d_far_ffn, plan_d2d_far)
    wg_ffn_in = landed[0].reshape(N_CHIPS, *big["w_ffn_in"].shape)
    f = run(mm_nn, h2, wg_ffn_in, tn=_tile(nj_ffn, 1408), tk=d, out_dtype=BF16, name="mm_ffn_in", perm=perm_ffn)
    d2d_ffn_out = gather_pass_on("ffn_out", ici["ffn_out"])
    act = run(swiglu_fwd, f, nj_ffn)
    (wg_ffn_out,) = gathered("ffn_out", d2d_ffn_out)
    wg_ffn_out = wg_ffn_out.reshape(1, dff, d)
    o_f = run(mm_nn, act, wg_ffn_out, tn=_tile(d, 512), tk=_tile(dff, 2816), out_dtype=BF16, name="mm_ffn_out")
    loss11, dy, dof, acc_l = run(loss_head, x1, o_f, tgt, mod)

    gw_ffn_out = run(mm_tn, act, dof, 1, tk=_tile(dff, 512), tn=d, name="mm_ffn_out_dw")
    px_ffn_out = rs_pair_start("ffn_out", ["w_ffn_out"], [gw_ffn_out])
    dact = run(mm_nt, dof, wg_ffn_out, tko=_tile(dff, 512), tn=d, out_dtype=BF16, name="mm_ffn_out_dx")
    cx_ffn_out = rs_chip_start("ffn_out", ["w_ffn_out"], px_ffn_out)
    df = run(swiglu_bwd, f, dact, nj_ffn)
    gw_ffn_in = run(mm_tn, h2, df, N_CHIPS, tk=d, tn=_tile(nj_ffn, 1408), name="mm_ffn_in_dw",
                    perm=perm_ffn)
    px_ffn_in = rs_pair_start("ffn_in", ["w_ffn_in"], [gw_ffn_in])
    dh2 = run(mm_nt, df, wg_ffn_in, tko=_tile(d, 512), tn=nj_ffn, out_dtype=BF16, name="mm_ffn_in_dx", perm=perm_ffn)
    sh_ffn_out = rs_share_start("ffn_out", ["w_ffn_out"], cx_ffn_out)
    cx_ffn_in = rs_chip_start("ffn_in", ["w_ffn_in"], px_ffn_in)
    dx1, dom, acc_f = run(pre_ffn_bwd, x1, dh2, dy, o_m, mod, norm_ffn_g)
    gw_mix_out = run(mm_tn, merged, dom, 1, tk=d, tn=_tile(d, 1024), name="mm_mix_out_dw")
    px_mix = rs_pair_start("mix_out", ["w_mix_out"], [gw_mix_out])
    dmerged = run(mm_nt, dom, wg_mix_out, tko=_tile(d, 512), tn=d, out_dtype=BF16, name="mm_mix_out_dx")
    dy_attn, dy_conv, dga, dgc = run(merge_bwd, p, y_attn, y_conv, dmerged, off_ga, off_gc)
    rs_finish("ffn_out", ["w_ffn_out"], sh_ffn_out)
    cx_mix = rs_chip_start("mix_out", ["w_mix_out"], px_mix)
    gw_attn_out = run(mm_tn, attn_o, dy_attn, N_CHIPS, tk=aw, tn=_tile(wg_attn_out.shape[2], 512),
                      name="mm_attn_out_dw")
    gw_conv_out = run(mm_tn, s_conv, dy_conv, N_CHIPS, tk=ch, tn=_tile(wg_conv_out.shape[2], 512),
                      name="mm_conv_out_dw")
    ac_names = ["w_attn_out", "w_conv_out"]
    px_ac = rs_pair_start("attn_conv_out", ac_names, [gw_attn_out, gw_conv_out])
    dattn_o = run(mm_nt, dy_attn, wg_attn_out, tko=_tile(aw, 1024), tn=_tile(wg_attn_out.shape[2], 512),
                  out_dtype=BF16, name="mm_attn_out_dx")
    ds_conv = run(mm_nt, dy_conv, wg_conv_out, tko=_tile(ch, 1024), tn=_tile(wg_conv_out.shape[2], 512),
                  out_dtype=BF16, name="mm_conv_out_dx")
    cx_ac = rs_chip_start("attn_conv_out", ac_names, px_ac)
    dca, dcb, dconv_w, dconv_vec = run(conv_bwd, ca, cb, co_conv, ds_conv, conv_w_pad, conv_ln_g, conv_ln_b)
    sh_ffn_in = rs_share_start("ffn_in", ["w_ffn_in"], cx_ffn_in)
    dqkv, dbp, dbc, dsinks, dqg, dkg = run(attn_bwd, p, bias_p, bias_c, sinks3, q_norm_g, k_norm_g, dattn_o,
                                           aw=aw, kvw=kvw)
    sh_mix = rs_share_start("mix_out", ["w_mix_out"], cx_mix)
    sh_ac = rs_share_start("attn_conv_out", ac_names, cx_ac)
    drel = run(bias_table_bwd, dbp, dbc, bucket_p, bucket_c).reshape(NUM_BUCKETS, nq)
    dp = jnp.concatenate([dqkv, dca, dcb, dga, dgc], axis=1)
    gw_in = run(mm_tn, h, dp, N_CHIPS, tk=d, tn=wg_in.shape[2], name="mm_in_dw")
    px_in = rs_pair_start("in", ["w_in"], [gw_in])
    dh = run(mm_nt, dp, wg_in, tko=_tile(d, 1024), tn=wg_in.shape[2], out_dtype=BF16, name="mm_in_dx")
    grad_x, acc_m = run(pre_mix_bwd, xe, dh, dx1, mod, norm_mix_g)

    dmod = jnp.concatenate([acc_m[0:1], acc_m[1:2], acc_f[3:4], acc_f[0:1], acc_f[1:2], acc_l[0:1]], axis=1)
    small_names = ["b_ada", "norm_mix_g", "q_norm_g", "k_norm_g", "attn_sinks", "rel_bias", "conv_b", "conv_ln_g",
                   "conv_ln_b", "norm_ffn_g"]
    small_w = [b_ada, norm_mix_g, q_norm_g, k_norm_g, attn_sinks, rel_bias, conv_b, conv_ln_g, conv_ln_b, norm_ffn_g]
    small_m = [m_b_ada, m_norm_mix_g, m_q_norm_g, m_k_norm_g, m_attn_sinks, m_rel_bias, m_conv_b, m_conv_ln_g,
               m_conv_ln_b, m_norm_ffn_g]
    small_v = [v_b_ada, v_norm_mix_g, v_q_norm_g, v_k_norm_g, v_attn_sinks, v_rel_bias, v_conv_b, v_conv_ln_g,
               v_conv_ln_b, v_norm_ffn_g]
    small_g = [dmod, acc_m[2:3], dqg, dkg, dsinks.reshape(1, nq), drel.reshape(1, NUM_BUCKETS * nq),
               dconv_vec[0:1], dconv_vec[1:2], dconv_vec[2:3], acc_f[2:3]]
    row3, offs3 = _row_pack(small_g + [dconv_w[:CONV_WIDTH].reshape(1, CONV_WIDTH * ch), loss11])
    got3 = run(allgather_small, row3, "allgather_small_grads")
    cx_in = rs_chip_start("in", ["w_in"], px_in)
    as_row = lambda a: a.reshape(1, -1)
    outs3 = run(small_sum_adamw, got3, offs3, [as_row(a) for a in small_w], [as_row(a) for a in small_m],
                [as_row(a) for a in small_v], [CONV_WIDTH * ch, 1])
    for i, (n, w) in enumerate(zip(small_names, small_w)):
        grads[n], deltas[n], new_m[n], new_v[n] = (o.reshape(w.shape) for o in outs3[4 * i:4 * i + 4])
    g_conv_w_all, loss_sum = outs3[-2].reshape(CONV_WIDTH, ch), outs3[-1]

    g_conv_w = lax.dynamic_slice_in_dim(g_conv_w_all, chip * ch_loc, ch_loc, axis=1)
    grads["conv_w"] = g_conv_w[None]
    dl, nm, nv = run(adamw, conv_w[0], g_conv_w, m_conv_w[0], v_conv_w[0], "adamw_conv_w")
    deltas["conv_w"], new_m["conv_w"], new_v["conv_w"] = dl[None], nm[None], nv[None]

    dmod_all = got3[:, 0, :N_MOD * d]
    dmod_cols = lax.dynamic_slice_in_dim(dmod_all, chip * nc_ada, nc_ada, axis=1)
    g_ada, dl, nm, nv = run(ada_grad_adamw, c_t, dmod_cols, w_ada[0], m_w_ada[0], v_w_ada[0])
    grads["w_ada"], deltas["w_ada"], new_m["w_ada"], new_v["w_ada"] = g_ada[None], dl[None], nm[None], nv[None]

    rs_finish("ffn_in", ["w_ffn_in"], sh_ffn_in)
    rs_finish("mix_out", ["w_mix_out"], sh_mix)
    rs_finish("attn_conv_out", ac_names, sh_ac)
    sh_in = rs_share_start("in", ["w_in"], cx_in)
    rs_finish("in", ["w_in"], sh_in)

    loss = loss_sum[0, 0]
    order = ["w_ada", "b_ada", "norm_mix_g", "w_in", "q_norm_g", "k_norm_g", "attn_sinks", "rel_bias", "w_attn_out",
             "conv_w", "conv_b", "conv_ln_g", "conv_ln_b", "w_conv_out", "w_mix_out", "norm_ffn_g", "w_ffn_in",
             "w_ffn_out"]
    return (loss, grad_x[None], *[grads[n] for n in order], *[deltas[n] for n in order],
            *[new_m[n] for n in order], *[new_v[n] for n in order])
```

```python
import functools
import math
from typing import Any, NamedTuple

import jax
import jax.numpy as jnp
import numpy as np
from jax import lax
from jax.experimental import pallas as pl
from jax.experimental.pallas import tpu as pltpu

F32 = jnp.float32
BF16 = jnp.bfloat16
MESH = pl.DeviceIdType.MESH

V7X_VMEM_BYTES = 64 * 1024 * 1024
VMEM_LIMIT = V7X_VMEM_BYTES - 8 * 1024 * 1024
LANES = 128
SUBLANES = 8
BF16_SUBLANES = 16

EPS = 1e-6
WINDOW = 128
BLOCK = 128
NUM_BUCKETS = 32
MAX_EXACT = NUM_BUCKETS // 2
MAX_DISTANCE = 128
CONV_WIDTH = 31
CONV_HALO = 32
ADAM_LR = 0.001
ADAM_B1 = 0.9
ADAM_B2 = 0.999
ADAM_EPS = 1e-08
ADAM_WD = 0.01
ADAM_STEP = 10
N_MOD = 6
SH_M, SC_M, GT_M, SH_F, SC_F, GT_F = range(6)

N_CHIPS = 4
N_DEV = 8

_ANY = pl.BlockSpec(memory_space=pl.ANY)
_VMEM = pl.BlockSpec(memory_space=pltpu.VMEM)
_SMEM = pl.BlockSpec(memory_space=pltpu.SMEM)
_HBM = pl.BlockSpec(memory_space=pltpu.HBM)
_SEM = pl.BlockSpec(memory_space=pltpu.SEMAPHORE)
_EFFECT = pltpu.SideEffectType.DATAFLOW_SIDE_EFFECTING


class InOrder:
    def __init__(self):
        self.token = None

    def __call__(self, fn, *args, **kw):
        return fn(*args, dep=self, **kw)


def _pallas(body, args, *, in_specs, out_specs, out_shape, name, dep=None, grid=(), n_prefetch=0, scratch=(),
            sem=None, **kw):
    n_lead = n_prefetch + len(in_specs)
    in_specs, args = list(in_specs), list(args)
    single = not isinstance(out_shape, (list, tuple))
    out_shapes = [out_shape] if single else list(out_shape)
    out_specs = [out_specs] if single else list(out_specs)
    if dep is not None:
        inner, n_out, takes = body, len(out_shapes), dep.token is not None

        def body(*refs):
            rest = refs[n_lead + (1 if takes else 0):]
            rest[n_out][...] = jnp.zeros((SUBLANES, LANES), F32)
            return inner(*refs[:n_lead], *rest[:n_out], *rest[n_out + 1:])

        if takes:
            in_specs.append(_ANY)
            args.append(dep.token)
        out_shapes.append(jax.ShapeDtypeStruct((SUBLANES, LANES), F32))
        out_specs.append(pl.BlockSpec((SUBLANES, LANES), lambda *_: (0, 0)))
    params = kw.pop("compiler_params", None)
    if params is None:
        params = pltpu.CompilerParams(dimension_semantics=sem, vmem_limit_bytes=VMEM_LIMIT)
    outs = pl.pallas_call(
        body,
        grid_spec=pltpu.PrefetchScalarGridSpec(num_scalar_prefetch=n_prefetch, grid=grid, in_specs=in_specs,
                                               out_specs=out_specs, scratch_shapes=list(scratch)),
        out_shape=out_shapes, compiler_params=params, name=name, **kw,
    )(*args)
    if dep is not None:
        dep.token = outs[-1]
        outs = outs[:-1]
    return outs[0] if single else list(outs)


def _tile(n, pref, unit=LANES):
    best = None
    for t in range(unit, min(n, pref) + 1, unit):
        if n % t == 0:
            best = t
    return best if best is not None else n


def _sigmoid(v):
    return 1.0 / (1.0 + jnp.exp(-v.astype(F32)))


ROW_CHUNK = 512


def _row_chunks(m, unit=SUBLANES):
    step = _tile(m, ROW_CHUNK, unit)
    return [(s, step) for s in range(0, m, step)]


def _ew_tiles(r, n, unit=SUBLANES, elems=512 * 1024):
    return _tile(r, max(unit, elems // n), unit), n


def _block_pos(j, perm):
    if perm is None:
        return j
    pos = 0
    for a, p in enumerate(perm):
        pos = pos + jnp.where(j == a, p, 0)
    return pos


def mm_nn(a, w, *, tn, tk, out_dtype, name, perm=None, dep=None):
    m, k = a.shape
    j, k2, nj = w.shape
    assert k == k2 and nj % tn == 0 and k % tk == 0
    npj, nk = nj // tn, k // tk

    def body(a_ref, w_ref, o_ref, *scratch):
        kk = pl.program_id(1)
        for s, sz in _row_chunks(m):
            rows = pl.ds(s, sz)
            p = jnp.dot(a_ref[rows, :], w_ref[...], preferred_element_type=F32)
            if nk == 1:
                o_ref[rows, :] = p.astype(out_dtype)
            else:
                acc = scratch[0]

                @pl.when(kk == 0)
                def _():
                    acc[rows, :] = p

                @pl.when(kk > 0)
                def _():
                    acc[rows, :] += p

                @pl.when(kk == nk - 1)
                def _():
                    o_ref[rows, :] = acc[rows, :].astype(out_dtype)

    return _pallas(
        body, [a, w], dep=dep, grid=(j * npj, nk),
        in_specs=[
            pl.BlockSpec((m, tk), lambda n, kk: (0, kk)),
            pl.BlockSpec((None, tk, tn), lambda n, kk: (n // npj, kk, n % npj)),
        ],
        out_specs=pl.BlockSpec((m, tn), lambda n, kk: (0, _block_pos(n // npj, perm) * npj + n % npj)),
        out_shape=jax.ShapeDtypeStruct((m, j * nj), out_dtype),
        scratch=[pltpu.VMEM((m, tn), F32)] if nk > 1 else [],
        sem=("parallel", "arbitrary"), name=name)


def mm_nt(g, w, *, tko, tn, name, out_dtype=F32, perm=None, dep=None):
    m, n = g.shape
    j, k, nj = w.shape
    assert n == j * nj and nj % tn == 0 and k % tko == 0
    npj, nr = nj // tn, n // tn
    in_place = out_dtype == F32

    def body(g_ref, w_ref, o_ref, *scratch):
        r = pl.program_id(1)
        acc = o_ref if in_place else (scratch[0] if nr > 1 else None)
        for s, sz in _row_chunks(m):
            rows = pl.ds(s, sz)
            p = lax.dot_general(g_ref[rows, :], w_ref[...], (((1,), (1,)), ((), ())), preferred_element_type=F32)
            if acc is None:
                o_ref[rows, :] = p.astype(out_dtype)
                continue

            @pl.when(r == 0)
            def _():
                acc[rows, :] = p

            @pl.when(r > 0)
            def _():
                acc[rows, :] += p

            if not in_place:
                @pl.when(r == nr - 1)
                def _():
                    o_ref[rows, :] = acc[rows, :].astype(out_dtype)

    return _pallas(
        body, [g, w], dep=dep, grid=(k // tko, nr),
        in_specs=[
            pl.BlockSpec((m, tn), lambda ko, r: (0, _block_pos(r // npj, perm) * npj + r % npj)),
            pl.BlockSpec((None, tko, tn), lambda ko, r: (r // npj, ko, r % npj)),
        ],
        out_specs=pl.BlockSpec((m, tko), lambda ko, r: (0, ko)),
        out_shape=jax.ShapeDtypeStruct((m, k), out_dtype),
        scratch=[pltpu.VMEM((m, tko), F32)] if (nr > 1 and not in_place) else [],
        sem=("parallel", "arbitrary"), name=name)


def mm_tn(a, g, n_blocks, *, tk, tn, name, perm=None, dep=None):
    m, k = a.shape
    m2, n = g.shape
    nj = n // n_blocks
    assert m == m2 and nj % tn == 0 and k % tk == 0
    npj = nj // tn

    def body(a_ref, g_ref, o_ref):
        for s, sz in _row_chunks(tk, LANES):
            p = lax.dot_general(a_ref[:, pl.ds(s, sz)], g_ref[...], (((0,), (0,)), ((), ())),
                                preferred_element_type=F32)
            o_ref[pl.ds(s, sz), :] = p.astype(BF16)

    return _pallas(
        body, [a, g], dep=dep, grid=(k // tk, n // tn),
        in_specs=[
            pl.BlockSpec((m, tk), lambda kk, nn: (0, kk)),
            pl.BlockSpec((m, tn), lambda kk, nn: (0, _block_pos(nn // npj, perm) * npj + nn % npj)),
        ],
        out_specs=pl.BlockSpec((None, tk, tn), lambda kk, nn: (nn // npj, kk, nn % npj)),
        out_shape=jax.ShapeDtypeStruct((n_blocks, k, nj), BF16),
        sem=("parallel", "parallel"), name=name)


ROW_TILE = 256


def _row_spec(tr, width):
    return pl.BlockSpec((tr, width), lambda i: (i, 0))


def _full_spec(shape):
    return pl.BlockSpec(shape, lambda *_: (0,) * len(shape))


def _rms(xv):
    return lax.rsqrt(jnp.mean(xv * xv, axis=-1, keepdims=True) + EPS)


def _mod_row(mod_ref, row):
    return mod_ref[pl.ds(row, 1), :]


def pre_mix_fwd(x, mod, gain, dep=None):
    t, d = x.shape
    tr = _tile(t, ROW_TILE, SUBLANES)

    def body(x_ref, mod_ref, g_ref, h_ref):
        xv = x_ref[...]
        y = xv * _rms(xv) * g_ref[...]
        h_ref[...] = (y * (1.0 + _mod_row(mod_ref, SC_M)) + _mod_row(mod_ref, SH_M)).astype(BF16)

    return _pallas(
        body, [x, mod, gain], dep=dep, grid=(t // tr,),
        in_specs=[_row_spec(tr, d), _full_spec(mod.shape), _full_spec(gain.shape)],
        out_specs=_row_spec(tr, d),
        out_shape=jax.ShapeDtypeStruct((t, d), BF16),
        sem=("parallel",), name="pre_mix_fwd")


def pre_ffn_fwd(x, o_m, mod, gain, dep=None):
    t, d = x.shape
    tr = _tile(t, ROW_TILE, SUBLANES)

    def body(x_ref, om_ref, mod_ref, g_ref, x1_ref, h_ref):
        x1 = x_ref[...] + _mod_row(mod_ref, GT_M) * om_ref[...]
        x1_ref[...] = x1
        y = x1 * _rms(x1) * g_ref[...]
        h_ref[...] = (y * (1.0 + _mod_row(mod_ref, SC_F)) + _mod_row(mod_ref, SH_F)).astype(BF16)

    return _pallas(
        body, [x, o_m, mod, gain], dep=dep, grid=(t // tr,),
        in_specs=[_row_spec(tr, d), _row_spec(tr, d), _full_spec(mod.shape), _full_spec(gain.shape)],
        out_specs=[_row_spec(tr, d), _row_spec(tr, d)],
        out_shape=[jax.ShapeDtypeStruct((t, d), F32), jax.ShapeDtypeStruct((t, d), BF16)],
        sem=("parallel",), name="pre_ffn_fwd")


def loss_head(x1, o_f, target, mod, dep=None):
    t, d = x1.shape
    tr = _tile(t, ROW_TILE, SUBLANES)

    def body(x1_ref, of_ref, tg_ref, mod_ref, loss_ref, dy_ref, dof_ref, acc_ref):
        i = pl.program_id(0)
        gt = _mod_row(mod_ref, GT_F)
        of = of_ref[...].astype(F32)
        err = x1_ref[...] + gt * of - tg_ref[...]
        dy = err * (1.0 / d)
        dy_ref[...] = dy.astype(BF16)
        dof_ref[...] = (dy * gt).astype(BF16)
        part = (0.5 / d) * jnp.sum(jnp.sum(err * err, axis=1, keepdims=True), axis=0, keepdims=True)
        dgt = jnp.sum(dy * of, axis=0, keepdims=True)

        @pl.when(i == 0)
        def _():
            loss_ref[...] = jnp.zeros_like(loss_ref)
            acc_ref[...] = jnp.zeros_like(acc_ref)

        loss_ref[...] += part
        acc_ref[pl.ds(0, 1), :] += dgt

    return _pallas(
        body, [x1, o_f, target, mod], dep=dep, grid=(t // tr,),
        in_specs=[_row_spec(tr, d), _row_spec(tr, d), _row_spec(tr, d), _full_spec(mod.shape)],
        out_specs=[_full_spec((1, 1)), _row_spec(tr, d), _row_spec(tr, d), _full_spec((SUBLANES, d))],
        out_shape=[jax.ShapeDtypeStruct((1, 1), F32), jax.ShapeDtypeStruct((t, d), BF16),
                   jax.ShapeDtypeStruct((t, d), BF16), jax.ShapeDtypeStruct((SUBLANES, d), F32)],
        sem=("arbitrary",), name="loss_head")


def _norm_bwd(xv, dh, sc, gain):
    rstd = _rms(xv)
    yn = xv * rstd
    dsh = jnp.sum(dh, axis=0, keepdims=True)
    dsc = jnp.sum(dh * (yn * gain), axis=0, keepdims=True)
    dgain = jnp.sum(dh * (1.0 + sc) * yn, axis=0, keepdims=True)
    dyn = dh * ((1.0 + sc) * gain)
    dx = rstd * (dyn - yn * jnp.mean(dyn * yn, axis=-1, keepdims=True))
    return dx, dsh, dsc, dgain


def pre_ffn_bwd(x1, dh2, dy, o_m, mod, gain, dep=None):
    t, d = x1.shape
    tr = _tile(t, ROW_TILE, SUBLANES)

    def body(x1_ref, dh_ref, dy_ref, om_ref, mod_ref, g_ref, dx1_ref, dom_ref, acc_ref):
        i = pl.program_id(0)
        dxn, dsh, dsc, dgain = _norm_bwd(x1_ref[...], dh_ref[...].astype(F32), _mod_row(mod_ref, SC_F), g_ref[...])
        dx1 = dy_ref[...] + dxn
        dx1_ref[...] = dx1
        dom_ref[...] = (dx1 * _mod_row(mod_ref, GT_M)).astype(BF16)
        dgt = jnp.sum(dx1 * om_ref[...], axis=0, keepdims=True)

        @pl.when(i == 0)
        def _():
            acc_ref[...] = jnp.zeros_like(acc_ref)

        acc_ref[pl.ds(0, 1), :] += dsh
        acc_ref[pl.ds(1, 1), :] += dsc
        acc_ref[pl.ds(2, 1), :] += dgain
        acc_ref[pl.ds(3, 1), :] += dgt

    return _pallas(
        body, [x1, dh2, dy, o_m, mod, gain], dep=dep, grid=(t // tr,),
        in_specs=[_row_spec(tr, d)] * 4 + [_full_spec(mod.shape), _full_spec(gain.shape)],
        out_specs=[_row_spec(tr, d), _row_spec(tr, d), _full_spec((SUBLANES, d))],
        out_shape=[jax.ShapeDtypeStruct((t, d), F32), jax.ShapeDtypeStruct((t, d), BF16),
                   jax.ShapeDtypeStruct((SUBLANES, d), F32)],
        sem=("arbitrary",), name="pre_ffn_bwd")


def pre_mix_bwd(x, dh, dx1, mod, gain, dep=None):
    t, d = x.shape
    tr = _tile(t, ROW_TILE, SUBLANES)

    def body(x_ref, dh_ref, dx1_ref, mod_ref, g_ref, gx_ref, acc_ref):
        i = pl.program_id(0)
        dxn, dsh, dsc, dgain = _norm_bwd(x_ref[...], dh_ref[...].astype(F32), _mod_row(mod_ref, SC_M), g_ref[...])
        gx_ref[...] = dx1_ref[...] + dxn

        @pl.when(i == 0)
        def _():
            acc_ref[...] = jnp.zeros_like(acc_ref)

        acc_ref[pl.ds(0, 1), :] += dsh
        acc_ref[pl.ds(1, 1), :] += dsc
        acc_ref[pl.ds(2, 1), :] += dgain

    return _pallas(
        body, [x, dh, dx1, mod, gain], dep=dep, grid=(t // tr,),
        in_specs=[_row_spec(tr, d)] * 3 + [_full_spec(mod.shape), _full_spec(gain.shape)],
        out_specs=[_row_spec(tr, d), _full_spec((SUBLANES, d))],
        out_shape=[jax.ShapeDtypeStruct((t, d), F32), jax.ShapeDtypeStruct((SUBLANES, d), F32)],
        sem=("arbitrary",), name="pre_mix_bwd")


def merge_fwd(p, y_attn, y_conv, off_ga, off_gc, dep=None):
    t, d = y_attn.shape
    tr = _tile(t, ROW_TILE, SUBLANES)
    cw = math.gcd(math.gcd(off_ga, off_gc), math.gcd(d, 512))
    nc = d // cw

    def body(ga_ref, gc_ref, ya_ref, yc_ref, o_ref):
        o_ref[...] = (_sigmoid(ga_ref[...]) * ya_ref[...] + _sigmoid(gc_ref[...]) * yc_ref[...]).astype(BF16)

    return _pallas(
        body, [p, p, y_attn, y_conv], dep=dep, grid=(t // tr, nc),
        in_specs=[pl.BlockSpec((tr, cw), lambda i, j: (i, off_ga // cw + j)),
                  pl.BlockSpec((tr, cw), lambda i, j: (i, off_gc // cw + j)),
                  pl.BlockSpec((tr, cw), lambda i, j: (i, j)),
                  pl.BlockSpec((tr, cw), lambda i, j: (i, j))],
        out_specs=pl.BlockSpec((tr, cw), lambda i, j: (i, j)),
        out_shape=jax.ShapeDtypeStruct((t, d), BF16),
        sem=("parallel", "parallel"), name="merge_fwd")


def mix_out_dx_merge_bwd(dom, w_mix, p, y_attn, y_conv, off_ga, off_gc, dep=None):
    t, d = y_attn.shape
    cw = math.gcd(math.gcd(off_ga, off_gc), math.gcd(d, 256))

    def body(dom_ref, w_ref, ga_ref, gc_ref, ya_ref, yc_ref, dya_ref, dyc_ref, dga_ref, dgc_ref):
        for s, sz in _row_chunks(t):
            rows = pl.ds(s, sz)
            dm = lax.dot_general(dom_ref[rows, :], w_ref[...], (((1,), (1,)), ((), ())), preferred_element_type=F32)
            sa = _sigmoid(ga_ref[rows, :])
            sc = _sigmoid(gc_ref[rows, :])
            dya_ref[rows, :] = (dm * sa).astype(BF16)
            dyc_ref[rows, :] = (dm * sc).astype(BF16)
            dga_ref[rows, :] = (dm * ya_ref[rows, :] * sa * (1.0 - sa)).astype(BF16)
            dgc_ref[rows, :] = (dm * yc_ref[rows, :] * sc * (1.0 - sc)).astype(BF16)

    col = pl.BlockSpec((t, cw), lambda j: (0, j))
    return _pallas(
        body, [dom, w_mix, p, p, y_attn, y_conv], dep=dep, grid=(d // cw,),
        in_specs=[pl.BlockSpec((t, d), lambda j: (0, 0)), pl.BlockSpec((cw, d), lambda j: (j, 0)),
                  pl.BlockSpec((t, cw), lambda j: (0, off_ga // cw + j)),
                  pl.BlockSpec((t, cw), lambda j: (0, off_gc // cw + j)), col, col],
        out_specs=[col] * 4,
        out_shape=[jax.ShapeDtypeStruct((t, d), BF16)] * 4,
        sem=("parallel",), name="mm_mix_out_dx_merge_bwd")


def ffn_perm(n_blocks):
    half = n_blocks // 2
    return tuple(2 * j if j < half else 2 * (j - half) + 1 for j in range(n_blocks))


def swiglu_fwd(f, nj, dep=None):
    t, two = f.shape
    tr = _tile(t, ROW_TILE, SUBLANES)
    npair = two // (2 * nj)

    def body(f_ref, o_ref):
        g = f_ref[:, :nj].astype(F32)
        u = f_ref[:, nj:].astype(F32)
        o_ref[...] = (g * _sigmoid(g) * u).astype(BF16)

    return _pallas(
        body, [f], dep=dep, grid=(t // tr, npair),
        in_specs=[pl.BlockSpec((tr, 2 * nj), lambda i, j: (i, j))],
        out_specs=pl.BlockSpec((tr, nj), lambda i, j: (i, j)),
        out_shape=jax.ShapeDtypeStruct((t, two // 2), BF16),
        sem=("parallel", "parallel"), name="swiglu_fwd")


def swiglu_bwd(f, dact, nj, dep=None):
    t, two = f.shape
    tr = _tile(t, ROW_TILE, SUBLANES)
    npair = two // (2 * nj)

    def body(f_ref, da_ref, o_ref):
        g = f_ref[:, :nj].astype(F32)
        u = f_ref[:, nj:].astype(F32)
        da = da_ref[...]
        s = _sigmoid(g)
        o_ref[:, :nj] = (da * u * (s * (1.0 + g * (1.0 - s)))).astype(BF16)
        o_ref[:, nj:] = (da * (g * s)).astype(BF16)

    return _pallas(
        body, [f, dact], dep=dep, grid=(t // tr, npair),
        in_specs=[pl.BlockSpec((tr, 2 * nj), lambda i, j: (i, j)), pl.BlockSpec((tr, nj), lambda i, j: (i, j))],
        out_specs=pl.BlockSpec((tr, 2 * nj), lambda i, j: (i, j)),
        out_shape=jax.ShapeDtypeStruct((t, two), BF16),
        sem=("parallel", "parallel"), name="swiglu_bwd")


def _t5_bucket_table():
    q_off = np.arange(BLOCK)
    k_off = np.arange(2 * BLOCK)
    dist = q_off[:, None] + BLOCK - k_off[None, :]
    n = np.maximum(dist, 0)
    nf = np.maximum(n, 1).astype(np.float32)
    large = MAX_EXACT + (np.log(nf / np.float32(MAX_EXACT)) / np.float32(math.log(MAX_DISTANCE / MAX_EXACT))
                         * np.float32(NUM_BUCKETS - MAX_EXACT)).astype(np.int32)
    large = np.minimum(large, NUM_BUCKETS - 1)
    bucket = np.where(n < MAX_EXACT, n, large).astype(np.int32)
    allowed = (dist >= 0) & (dist < WINDOW)
    return np.where(allowed, bucket, -1).astype(np.int32)


def bias_table(rel_bias, bucket_p, bucket_c, dep=None):
    nb, nq = rel_bias.shape

    def body(rb_ref, bkp_ref, bkc_ref, op_ref, oc_ref):
        for bk_ref, o_ref in ((bkp_ref, op_ref), (bkc_ref, oc_ref)):
            bk = bk_ref[...]
            for h in range(nq):
                acc = jnp.full(bk.shape, -jnp.inf, F32)
                for b in range(nb):
                    acc = jnp.where(bk == b, rb_ref[b, h], acc)
                o_ref[h] = acc

    return _pallas(
        body, [rel_bias, bucket_p, bucket_c], dep=dep,
        in_specs=[_SMEM, _VMEM, _VMEM], out_specs=[_VMEM, _VMEM],
        out_shape=[jax.ShapeDtypeStruct((nq,) + bucket_p.shape, F32)] * 2,
        name="bias_table")


def bias_table_bwd(dbp, dbc, bucket_p, bucket_c, dep=None):
    nq = dbp.shape[0]

    def body(dbp_ref, dbc_ref, bkp_ref, bkc_ref, o_ref):
        bkp, bkc = bkp_ref[...][None], bkc_ref[...][None]
        dp, dc = dbp_ref[...], dbc_ref[...]
        for b in range(NUM_BUCKETS):
            sel = jnp.where(bkp == b, dp, 0.0) + jnp.where(bkc == b, dc, 0.0)
            o_ref[b] = jnp.sum(jnp.sum(sel, axis=2, keepdims=True), axis=1, keepdims=True)

    return _pallas(
        body, [dbp, dbc, bucket_p, bucket_c], dep=dep,
        in_specs=[_VMEM] * 4, out_specs=_VMEM,
        out_shape=jax.ShapeDtypeStruct((NUM_BUCKETS, nq, 1, 1), F32),
        name="bias_table_bwd")


_BNT = (((2,), (2,)), ((0,), (0,)))
_BNN = (((2,), (1,)), ((0,), (0,)))
_BTN = (((1,), (1,)), ((0,), (0,)))


@jax.custom_vjp
def _bdot_nt(a, b):
    return lax.dot_general(a.astype(BF16), b.astype(BF16), _BNT, preferred_element_type=F32)


def _bdot_nt_fwd(a, b):
    return _bdot_nt(a, b), (a, b)


def _bdot_nt_bwd(res, g):
    a, b = res
    gb = g.astype(BF16)
    da = lax.dot_general(gb, b.astype(BF16), _BNN, preferred_element_type=F32)
    db = lax.dot_general(gb, a.astype(BF16), _BTN, preferred_element_type=F32)
    return da, db


_bdot_nt.defvjp(_bdot_nt_fwd, _bdot_nt_bwd)


@jax.custom_vjp
def _bdot_nn(a, b):
    return lax.dot_general(a.astype(BF16), b.astype(BF16), _BNN, preferred_element_type=F32)


def _bdot_nn_fwd(a, b):
    return _bdot_nn(a, b), (a, b)


def _bdot_nn_bwd(res, g):
    a, b = res
    gb = g.astype(BF16)
    da = lax.dot_general(gb, b.astype(BF16), _BNT, preferred_element_type=F32)
    db = lax.dot_general(a.astype(BF16), gb, _BTN, preferred_element_type=F32)
    return da, db


_bdot_nn.defvjp(_bdot_nn_fwd, _bdot_nn_bwd)


def _attn_math(q, kp, kc, vp, vc, bp, bc, sinks, qg, kg, *, prev_ok, scale):
    h, rows, _ = q.shape
    b = kp.shape[1]
    qn = q * _rms(q) * qg
    kpn = kp * _rms(kp) * kg
    kcn = kc * _rms(kc) * kg
    lp = _bdot_nt(qn, kpn) * scale + bp.reshape(h, rows, b)
    lc = _bdot_nt(qn, kcn) * scale + bc.reshape(h, rows, b)
    lp = jnp.where(prev_ok, lp, -jnp.inf)
    sink = jnp.broadcast_to(sinks, (sinks.shape[0], b, 1)).reshape(h, rows, 1)
    m = jnp.maximum(jnp.maximum(jnp.max(lp, axis=-1, keepdims=True), jnp.max(lc, axis=-1, keepdims=True)), sink)
    m = lax.stop_gradient(m)
    pp = jnp.exp(lp - m)
    pc = jnp.exp(lc - m)
    den = jnp.sum(pp, axis=-1, keepdims=True) + jnp.sum(pc, axis=-1, keepdims=True) + jnp.exp(sink - m)
    inv = 1.0 / den
    return _bdot_nn(pp * inv, vp) + _bdot_nn(pc * inv, vc)


def _attn_specs(p, aw, kvw, nq, hd, nblk, reverse):
    assert aw % (2 * kvw) == 0
    kv_col = aw // (2 * kvw)

    def blk(n):
        return nblk - 1 - n if reverse else n

    return [
        pl.BlockSpec((BLOCK, aw), lambda n: (blk(n), 0)),
        pl.BlockSpec((BLOCK, 2 * kvw), lambda n: (jnp.maximum(blk(n) - 1, 0), kv_col)),
        pl.BlockSpec((BLOCK, 2 * kvw), lambda n: (blk(n), kv_col)),
        _full_spec((nq, BLOCK, BLOCK)), _full_spec((nq, BLOCK, BLOCK)), _full_spec((nq, 1, 1)),
        _full_spec((1, hd)), _full_spec((1, hd)),
    ]


def _head_major(ref, n_heads, grp, hd, offset=0):
    return jnp.stack([
        jnp.concatenate([ref[:, pl.ds(offset + (grp * h + g) * hd, hd)].astype(F32) for g in range(grp)], axis=0)
        for h in range(n_heads)])


def _attn_inputs(nkv, grp, hd, kvw, q_ref, kvp_ref, kvc_ref):
    return (_head_major(q_ref, nkv, grp, hd), _head_major(kvp_ref, nkv, 1, hd), _head_major(kvc_ref, nkv, 1, hd),
            _head_major(kvp_ref, nkv, 1, hd, kvw), _head_major(kvc_ref, nkv, 1, hd, kvw))


def attn_fwd(p, bias_p, bias_c, sinks, qg, kg, *, aw, kvw, dep=None):
    t, hd = p.shape[0], qg.shape[-1]
    nq, nkv, nblk = aw // hd, kvw // hd, t // BLOCK
    grp = nq // nkv
    scale = hd ** -0.5

    def body(q_ref, kvp_ref, kvc_ref, bp_ref, bc_ref, s_ref, qg_ref, kg_ref, o_ref):
        prev_ok = pl.program_id(0) > 0
        out = _attn_math(*_attn_inputs(nkv, grp, hd, kvw, q_ref, kvp_ref, kvc_ref), bp_ref[...], bc_ref[...],
                         s_ref[...], qg_ref[...], kg_ref[...], prev_ok=prev_ok, scale=scale)
        for h in range(nkv):
            for g in range(grp):
                o_ref[:, pl.ds((grp * h + g) * hd, hd)] = out[h, g * BLOCK:(g + 1) * BLOCK].astype(BF16)

    return _pallas(
        body, [p, p, p, bias_p, bias_c, sinks, qg, kg], dep=dep, grid=(nblk,),
        in_specs=_attn_specs(p, aw, kvw, nq, hd, nblk, False),
        out_specs=pl.BlockSpec((BLOCK, aw), lambda n: (n, 0)),
        out_shape=jax.ShapeDtypeStruct((t, aw), BF16),
        sem=("parallel",), name="attn_fwd")


def attn_bwd(p, bias_p, bias_c, sinks, qg, kg, do, *, aw, kvw, dep=None):
    t, hd = p.shape[0], qg.shape[-1]
    nq, nkv, nblk = aw // hd, kvw // hd, t // BLOCK
    grp = nq // nkv
    scale = hd ** -0.5

    def body(q_ref, kvp_ref, kvc_ref, bp_ref, bc_ref, s_ref, qg_ref, kg_ref, do_ref,
             dqkv_ref, dbp_ref, dbc_ref, ds_ref, dqg_ref, dkg_ref, carry):
        i = pl.program_id(0)
        prev_ok = (nblk - 1 - i) > 0

        @pl.when(i == 0)
        def _():
            carry[...] = jnp.zeros_like(carry)
            dbp_ref[...] = jnp.zeros_like(dbp_ref)
            dbc_ref[...] = jnp.zeros_like(dbc_ref)
            ds_ref[...] = jnp.zeros_like(ds_ref)
            dqg_ref[...] = jnp.zeros_like(dqg_ref)
            dkg_ref[...] = jnp.zeros_like(dkg_ref)

        fn = functools.partial(_attn_math, prev_ok=prev_ok, scale=scale)
        _, vjp = jax.vjp(fn, *_attn_inputs(nkv, grp, hd, kvw, q_ref, kvp_ref, kvc_ref), bp_ref[...], bc_ref[...],
                         s_ref[...], qg_ref[...], kg_ref[...])
        dq, dkp, dkc, dvp, dvc, dbp, dbc, dsk, dqg, dkg = vjp(_head_major(do_ref, nkv, grp, hd))
        for h in range(nkv):
            for g in range(grp):
                dqkv_ref[:, pl.ds((grp * h + g) * hd, hd)] = dq[h, g * BLOCK:(g + 1) * BLOCK].astype(BF16)
            k_cols, v_cols = pl.ds(h * hd, hd), pl.ds(kvw + h * hd, hd)
            dqkv_ref[:, pl.ds(aw + h * hd, hd)] = (dkc[h] + carry[:, k_cols]).astype(BF16)
            dqkv_ref[:, pl.ds(aw + kvw + h * hd, hd)] = (dvc[h] + carry[:, v_cols]).astype(BF16)
            carry[:, k_cols] = dkp[h]
            carry[:, v_cols] = dvp[h]
        dbp_ref[...] += dbp
        dbc_ref[...] += dbc
        ds_ref[...] += dsk
        dqg_ref[...] += dqg
        dkg_ref[...] += dkg

    return _pallas(
        body, [p, p, p, bias_p, bias_c, sinks, qg, kg, do], dep=dep, grid=(nblk,),
        in_specs=_attn_specs(p, aw, kvw, nq, hd, nblk, True)
        + [pl.BlockSpec((BLOCK, aw), lambda n: (nblk - 1 - n, 0))],
        out_specs=[
            pl.BlockSpec((BLOCK, aw + 2 * kvw), lambda n: (nblk - 1 - n, 0)),
            _full_spec((nq, BLOCK, BLOCK)), _full_spec((nq, BLOCK, BLOCK)), _full_spec((nq, 1, 1)),
            _full_spec((1, hd)), _full_spec((1, hd)),
        ],
        out_shape=[
            jax.ShapeDtypeStruct((t, aw + 2 * kvw), BF16),
            jax.ShapeDtypeStruct((nq, BLOCK, BLOCK), F32),
            jax.ShapeDtypeStruct((nq, BLOCK, BLOCK), F32),
            jax.ShapeDtypeStruct((nq, 1, 1), F32),
            jax.ShapeDtypeStruct((1, hd), F32),
            jax.ShapeDtypeStruct((1, hd), F32),
        ],
        scratch=[pltpu.VMEM((BLOCK, 2 * kvw), F32)],
        sem=("arbitrary",), name="attn_bwd")


CONV_TILE = 256


def _conv_halo_specs(tb, ch, nblk):
    per = tb // CONV_HALO
    last = nblk * per - 1
    cur = pl.BlockSpec((tb, ch), lambda n: (n, 0))
    prev = pl.BlockSpec((CONV_HALO, ch), lambda n: (jnp.maximum(n * per - 1, 0), 0))
    nxt = pl.BlockSpec((CONV_HALO, ch), lambda n: (jnp.minimum((n + 1) * per, last), 0))
    return cur, prev, nxt


def _ln_silu(co, ln_g, ln_b):
    mu = jnp.mean(co, axis=-1, keepdims=True)
    cen = co - mu
    rstd = lax.rsqrt(jnp.mean(cen * cen, axis=-1, keepdims=True) + EPS)
    xhat = cen * rstd
    z = xhat * ln_g + ln_b
    return xhat, rstd, z


def _shifted_copies(src, shifted):
    rows = src.shape[0] - SUBLANES
    for r in range(1, SUBLANES):
        shifted[r, pl.ds(0, rows), :] = src[pl.ds(r, rows), :]


def _rows_from(src, shifted, start, n):
    r = start % SUBLANES
    if r == 0:
        return src[pl.ds(start, n), :]
    return shifted[r, pl.ds(start - r, n), :]


def conv_fwd(ca, cb, conv_w, conv_b, ln_g, ln_b, dep=None):
    t, ch = ca.shape
    tb = _tile(t, CONV_TILE, CONV_HALO)
    nblk = t // tb
    cur, prev, _ = _conv_halo_specs(tb, ch, nblk)
    lead = CONV_HALO - (CONV_WIDTH - 1)

    def body(ca_ref, cb_ref, cap_ref, cbp_ref, w_ref, b_ref, g_ref, bb_ref, s_ref, co_ref, ubuf, ushift):
        n = pl.program_id(0)
        halo = cap_ref[...] * _sigmoid(cbp_ref[...])
        ubuf[pl.ds(0, CONV_HALO), :] = jnp.where(n > 0, halo, 0.0)
        ubuf[pl.ds(CONV_HALO, tb), :] = ca_ref[...] * _sigmoid(cb_ref[...])
        _shifted_copies(ubuf, ushift)
        acc = jnp.broadcast_to(b_ref[...], (tb, ch))
        for k in range(CONV_WIDTH):
            acc = acc + w_ref[pl.ds(k, 1), :] * _rows_from(ubuf, ushift, lead + k, tb)
        co_ref[...] = acc
        _, _, z = _ln_silu(acc, g_ref[...], bb_ref[...])
        s_ref[...] = (z * _sigmoid(z)).astype(BF16)

    vec = _full_spec((1, ch))
    return _pallas(
        body, [ca, cb, ca, cb, conv_w, conv_b, ln_g, ln_b], dep=dep, grid=(nblk,),
        in_specs=[cur, cur, prev, prev, _full_spec(conv_w.shape), vec, vec, vec],
        out_specs=[cur, cur],
        out_shape=[jax.ShapeDtypeStruct((t, ch), BF16), jax.ShapeDtypeStruct((t, ch), F32)],
        scratch=[pltpu.VMEM((CONV_HALO + tb, ch), F32), pltpu.VMEM((SUBLANES, CONV_HALO + tb, ch), F32)],
        sem=("parallel",), name="conv_fwd")


def conv_bwd(ca, cb, co, ds, conv_w, ln_g, ln_b, dep=None):
    t, ch = ca.shape
    tb = _tile(t, CONV_TILE, CONV_HALO)
    nblk = t // tb
    cur, prev, nxt = _conv_halo_specs(tb, ch, nblk)
    lead = CONV_HALO - (CONV_WIDTH - 1)
    ext = tb + CONV_HALO

    def body(ca_ref, cb_ref, cap_ref, cbp_ref, co_ref, con_ref, ds_ref, dsn_ref, w_ref, g_ref, bb_ref,
             dca_ref, dcb_ref, dw_ref, dvec_ref, ubuf, dbuf, ushift, dshift):
        n = pl.program_id(0)
        is_last = n == nblk - 1
        sig_b = _sigmoid(cb_ref[...])
        cav = ca_ref[...].astype(F32)
        ubuf[pl.ds(0, CONV_HALO), :] = jnp.where(n > 0, cap_ref[...] * _sigmoid(cbp_ref[...]), 0.0)
        ubuf[pl.ds(CONV_HALO, tb), :] = cav * sig_b
        _shifted_copies(ubuf, ushift)
        co = jnp.concatenate([co_ref[...], con_ref[...]], axis=0)
        xhat, rstd, z = _ln_silu(co, g_ref[...], bb_ref[...])
        dsv = jnp.concatenate([ds_ref[...].astype(F32), jnp.where(is_last, 0.0, dsn_ref[...].astype(F32))], axis=0)
        sg = _sigmoid(z)
        dz = dsv * (sg * (1.0 + z * (1.0 - sg)))
        dxh = dz * g_ref[...]
        dco = rstd * (dxh - jnp.mean(dxh, axis=-1, keepdims=True)
                      - xhat * jnp.mean(dxh * xhat, axis=-1, keepdims=True))
        dbuf[...] = dco
        _shifted_copies(dbuf, dshift)

        @pl.when(n == 0)
        def _():
            dw_ref[...] = jnp.zeros_like(dw_ref)
            dvec_ref[...] = jnp.zeros_like(dvec_ref)

        dco_cur = dco[:tb]
        dvec_ref[pl.ds(0, 1), :] += jnp.sum(dco_cur, axis=0, keepdims=True)
        dvec_ref[pl.ds(1, 1), :] += jnp.sum(dz[:tb] * xhat[:tb], axis=0, keepdims=True)
        dvec_ref[pl.ds(2, 1), :] += jnp.sum(dz[:tb], axis=0, keepdims=True)
        du = jnp.zeros((tb, ch), F32)
        for k in range(CONV_WIDTH):
            du = du + w_ref[pl.ds(k, 1), :] * _rows_from(dbuf, dshift, CONV_WIDTH - 1 - k, tb)
            dw_ref[pl.ds(k, 1), :] += jnp.sum(dco_cur * _rows_from(ubuf, ushift, lead + k, tb), axis=0,
                                              keepdims=True)
        dca_ref[...] = (du * sig_b).astype(BF16)
        dcb_ref[...] = (du * cav * sig_b * (1.0 - sig_b)).astype(BF16)

    vec = _full_spec((1, ch))
    return _pallas(
        body, [ca, cb, ca, cb, co, co, ds, ds, conv_w, ln_g, ln_b], dep=dep, grid=(nblk,),
        in_specs=[cur, cur, prev, prev, cur, nxt, cur, nxt, _full_spec(conv_w.shape), vec, vec],
        out_specs=[cur, cur, _full_spec(conv_w.shape), _full_spec((SUBLANES, ch))],
        out_shape=[jax.ShapeDtypeStruct((t, ch), BF16), jax.ShapeDtypeStruct((t, ch), BF16),
                   jax.ShapeDtypeStruct(conv_w.shape, F32), jax.ShapeDtypeStruct((SUBLANES, ch), F32)],
        scratch=[pltpu.VMEM((CONV_HALO + tb, ch), F32), pltpu.VMEM((ext, ch), F32),
                 pltpu.VMEM((SUBLANES, CONV_HALO + tb, ch), F32), pltpu.VMEM((SUBLANES, ext, ch), F32)],
        sem=("arbitrary",), name="conv_bwd")


def ada_fwd(c_t, w_ada, dep=None):
    d, nc = w_ada.shape
    nex = c_t.shape[1]
    tn = _tile(nc, 512)

    def body(ct_ref, w_ref, o_ref):
        w = w_ref[...]
        ct = ct_ref[...]
        cact = ct * _sigmoid(ct)
        rows = [jnp.sum(w * cact[:, b:b + 1], axis=0, keepdims=True) for b in range(nex)]
        o_ref[...] = jnp.concatenate(rows, axis=0)

    return _pallas(
        body, [c_t, w_ada], dep=dep, grid=(nc // tn,),
        in_specs=[_full_spec(c_t.shape), pl.BlockSpec((d, tn), lambda j: (0, j))],
        out_specs=pl.BlockSpec((nex, tn), lambda j: (0, j)),
        out_shape=jax.ShapeDtypeStruct((nex, nc), F32),
        sem=("parallel",), name="ada_fwd")


def _adamw_math(w, g, m, v):
    m = ADAM_B1 * m + (1.0 - ADAM_B1) * g
    v = ADAM_B2 * v + (1.0 - ADAM_B2) * (g * g)
    m_hat = m / (1.0 - ADAM_B1 ** ADAM_STEP)
    v_hat = v / (1.0 - ADAM_B2 ** ADAM_STEP)
    delta = -ADAM_LR * (m_hat / (jnp.sqrt(v_hat) + ADAM_EPS) + ADAM_WD * w)
    return delta, m, v


def adamw(w, g, m, v, name, copy_grad=False, dep=None):
    r, n = w.shape
    tr, tn = _ew_tiles(r, n, elems=256 * 1024)
    n_out = 4 if copy_grad else 3

    def body(w_ref, g_ref, m_ref, v_ref, *outs):
        g = g_ref[...]
        if copy_grad:
            outs[0][...] = g
        outs[-3][...], outs[-2][...], outs[-1][...] = _adamw_math(w_ref[...], g, m_ref[...], v_ref[...])

    blk = pl.BlockSpec((tr, tn), lambda i, j: (i, j))
    return _pallas(
        body, [w, g, m, v], dep=dep, grid=(r // tr, n // tn),
        in_specs=[blk] * 4, out_specs=[blk] * n_out,
        out_shape=[jax.ShapeDtypeStruct((r, n), F32)] * n_out,
        sem=("parallel", "parallel"), name=name)


def ada_grad_adamw(c_t, dmod_cols, w, m, v, dep=None):
    d, nc = w.shape
    nex = c_t.shape[1]
    tr, tn = _ew_tiles(d, nc, elems=256 * 1024)

    def body(ct_ref, dm_ref, w_ref, m_ref, v_ref, g_ref, d_ref, nm_ref, nv_ref):
        ct = ct_ref[...]
        cact = ct * _sigmoid(ct)
        dm = dm_ref[...]
        g = cact[:, 0:1] * dm[0:1, :]
        for b in range(1, nex):
            g = g + cact[:, b:b + 1] * dm[b:b + 1, :]
        g_ref[...] = g
        d_ref[...], nm_ref[...], nv_ref[...] = _adamw_math(w_ref[...], g, m_ref[...], v_ref[...])

    blk = pl.BlockSpec((tr, tn), lambda i, j: (i, j))
    return _pallas(
        body, [c_t, dmod_cols, w, m, v], dep=dep, grid=(d // tr, nc // tn),
        in_specs=[pl.BlockSpec((tr, nex), lambda i, j: (i, 0)), pl.BlockSpec((nex, tn), lambda i, j: (0, j)),
                  blk, blk, blk],
        out_specs=[blk] * 4,
        out_shape=[jax.ShapeDtypeStruct((d, nc), F32)] * 4,
        sem=("parallel", "parallel"), name="ada_grad_adamw")


def _row_pack(parts):
    cols, offs, off = [], [], 0
    for p in parts:
        n = p.shape[1]
        width = -(-n // LANES) * LANES
        cols.append(jnp.pad(p, ((0, 0), (0, width - n))) if width != n else p)
        offs.append(off)
        off += width
    return jnp.concatenate(cols, axis=1), offs


def small_sum_adamw(gathered, offs, ws, ms, vs, extra_widths, dep=None):
    ndev = gathered.shape[0]
    npar = len(ws)

    def body(ga_ref, *refs):
        w_refs, m_refs, v_refs = refs[:npar], refs[npar:2 * npar], refs[2 * npar:3 * npar]
        outs = refs[3 * npar:]
        tot = ga_ref[0]
        for s in range(1, ndev):
            tot = tot + ga_ref[s]
        for i in range(npar):
            n = ws[i].shape[1]
            g = tot[:, offs[i]:offs[i] + n]
            outs[4 * i][...] = g
            outs[4 * i + 1][...], outs[4 * i + 2][...], outs[4 * i + 3][...] = _adamw_math(
                w_refs[i][...], g, m_refs[i][...], v_refs[i][...])
        for e, n in enumerate(extra_widths):
            off = offs[npar + e]
            outs[4 * npar + e][...] = tot[:, off:off + n]

    shapes = [jax.ShapeDtypeStruct(w.shape, F32) for w in ws for _ in range(4)]
    shapes += [jax.ShapeDtypeStruct((1, n), F32) for n in extra_widths]
    return _pallas(
        body, [gathered, *ws, *ms, *vs], dep=dep, in_specs=[_VMEM] * (1 + 3 * npar), out_specs=[_VMEM] * len(shapes),
        out_shape=shapes, name="small_sum_adamw")


def _position():
    return lax.axis_index("x"), lax.axis_index("y"), lax.axis_index("c")


def _other_chips(x, y):
    return [(1 - x, y), (x, 1 - y), (1 - x, 1 - y)]


def allgather_small(block, name, dep=None):
    def body(x_ref, out_ref, send_sems, recv_sems, local_sem):
        x, y, c = _position()
        me, sibling = (x, y, c), (x, y, 1 - c)
        chips = _other_chips(x, y)

        def slot(px, py, pc):
            return out_ref.at[4 * px + 2 * py + pc]

        def copy(k, block_of, to, src=None):
            return pltpu.make_async_remote_copy(
                src_ref=slot(*block_of) if src is None else src, dst_ref=slot(*block_of),
                send_sem=send_sems.at[k], recv_sem=recv_sems.at[k], device_id=to, device_id_type=MESH)

        mine = pltpu.make_async_copy(x_ref, slot(*me), local_sem)
        mine.start()
        first = [copy(0, me, sibling, src=x_ref)]
        first += [copy(1 + j, me, (*chip, c), src=x_ref) for j, chip in enumerate(chips)]
        for cp in first:
            cp.start()
        passed = [copy(4 + j, (*chip, c), sibling) for j, chip in enumerate(chips)]
        for j, chip in enumerate(chips):
            copy(1 + j, (*chip, c), me).wait_recv()
            passed[j].start()
        copy(0, sibling, me).wait_recv()
        for j, chip in enumerate(chips):
            copy(4 + j, (*chip, 1 - c), me).wait_recv()
        for cp in first + passed:
            cp.wait_send()
        mine.wait()

    return _pallas(
        body, [block], dep=dep,
        out_shape=jax.ShapeDtypeStruct((N_DEV, *block.shape), block.dtype),
        in_specs=[_VMEM], out_specs=_VMEM,
        scratch=[pltpu.SemaphoreType.DMA((7,)), pltpu.SemaphoreType.DMA((7,)), pltpu.SemaphoreType.DMA],
        name=name)


class Started(NamedTuple):
    send_sems: Any
    recv_sems: Any
    bufs: list


def exchange_start(name, bufs, n_copies, plan, dep=None):
    nb = len(bufs)

    def body(*refs):
        for cp in plan(refs[:nb], refs[nb], refs[nb + 1]):
            cp.start()

    outs = _pallas(
        body, [pltpu.with_memory_space_constraint(b, pltpu.HBM) for b in bufs], dep=dep, name=name,
        out_shape=(pltpu.SemaphoreType.DMA((n_copies,)), pltpu.SemaphoreType.DMA((n_copies,)),
                   *[pltpu.HBM(b.shape, b.dtype) for b in bufs]),
        in_specs=[_HBM] * nb,
        out_specs=(_SEM, _SEM, *[_HBM] * nb),
        input_output_aliases={i: 2 + i for i in range(nb)},
        compiler_params=pltpu.CompilerParams(has_side_effects=_EFFECT))
    return Started(outs[0], outs[1], list(outs[2:2 + nb]))


def exchange_wait(name, started, plan, bufs=None, dep=None):
    if bufs is not None:
        started = started._replace(bufs=list(bufs))
    nb = len(started.bufs)

    def body(*refs):
        for cp in plan(refs[:nb], refs[nb], refs[nb + 1]):
            cp.wait_send()
            cp.wait_recv()

    outs = _pallas(
        body, [*started.bufs, started.send_sems, started.recv_sems], dep=dep, name=name,
        out_shape=tuple(pltpu.HBM(b.shape, b.dtype) for b in started.bufs),
        in_specs=[_HBM] * nb + [_SEM, _SEM],
        out_specs=tuple([_HBM] * nb),
        input_output_aliases={i: i for i in range(nb)},
        compiler_params=pltpu.CompilerParams(has_side_effects=_EFFECT))
    return list(outs)


def _remote(src, dst, send_sems, recv_sems, i, to):
    return pltpu.make_async_remote_copy(src_ref=src, dst_ref=dst, send_sem=send_sems.at[i], recv_sem=recv_sems.at[i],
                                        device_id=to, device_id_type=MESH)


def _half_rows(buf_rows, chip_idx, pc):
    half = buf_rows // (2 * N_CHIPS)
    return pl.ds((2 * chip_idx + pc) * half, half)


ALL_PEERS = (0, 1, 2)


def plan_gather_ici(refs, send_sems, recv_sems, peers=ALL_PEERS):
    x, y, c = _position()
    chips = _other_chips(x, y)
    copies = []
    for k, ref in enumerate(refs):
        rows = ref.at[_half_rows(ref.shape[0], 2 * x + y, c), :]
        for i, j in enumerate(peers):
            copies.append(_remote(rows, rows, send_sems, recv_sems, len(peers) * k + i, (*chips[j], c)))
    return copies


def plan_gather_relay(refs, send_sems, recv_sems):
    x, y, c = _position()
    copies = []
    for k, ref in enumerate(refs):
        quarter = ref.shape[0] // (4 * N_CHIPS)
        for i, (src_chip, to) in enumerate((((1 - x, y), (x, 1 - y, c)), ((x, 1 - y), (1 - x, y, c)))):
            start = (2 * (2 * src_chip[0] + src_chip[1]) + c) * 2 * quarter + i * quarter
            rows = ref.at[pl.ds(start, quarter), :]
            copies.append(_remote(rows, rows, send_sems, recv_sems, 2 * k + i, to))
    return copies


def plan_gather_d2d(refs, send_sems, recv_sems, peers=ALL_PEERS):
    x, y, c = _position()
    chips = _other_chips(x, y)
    copies = []
    for k, ref in enumerate(refs):
        for i, j in enumerate(peers):
            px, py = chips[j]
            rows = ref.at[_half_rows(ref.shape[0], 2 * px + py, c), :]
            copies.append(_remote(rows, rows, send_sems, recv_sems, len(peers) * k + i, (x, y, 1 - c)))
    return copies


def plan_pair_exchange(refs, send_sems, recv_sems):
    x, y, c = _position()
    nw = len(refs) // 2
    copies = []
    for k in range(nw):
        for chip in range(N_CHIPS):
            copies.append(_remote(refs[k].at[chip, 1 - c], refs[nw + k].at[chip], send_sems, recv_sems,
                                  N_CHIPS * k + chip, (x, y, 1 - c)))
    return copies


def plan_chip_exchange(refs, send_sems, recv_sems):
    x, y, c = _position()
    nw = len(refs) // 2
    copies = []
    for k in range(nw):
        for j, (px, py) in enumerate(_other_chips(x, y)):
            copies.append(_remote(refs[k].at[2 * px + py], refs[nw + k].at[2 * x + y], send_sems, recv_sems,
                                  3 * k + j, (px, py, c)))
    return copies


def plan_pair_share(refs, send_sems, recv_sems):
    x, y, c = _position()
    return [_remote(ref.at[c], ref.at[c], send_sems, recv_sems, k, (x, y, 1 - c)) for k, ref in enumerate(refs)]


def cast_into_slot(src, slot, n_slots, name, dep=None):
    r, n = src.shape
    tr, tn = _ew_tiles(r, n, BF16_SUBLANES)

    def body(slot_ref, s_ref, o_ref):
        o_ref[...] = s_ref[...].astype(BF16)

    return _pallas(
        body, [slot, src], dep=dep, n_prefetch=1, grid=(r // tr, n // tn),
        in_specs=[pl.BlockSpec((tr, tn), lambda i, j, sl: (i, j))],
        out_specs=pl.BlockSpec((None, tr, tn), lambda i, j, sl: (sl[0], i, j)),
        out_shape=jax.ShapeDtypeStruct((n_slots, r, n), BF16),
        sem=("parallel", "parallel"), name=name)


def pair_sum(g, r, core, name, dep=None):
    nchip, _, h, n = g.shape
    th, tn = _ew_tiles(h, n, BF16_SUBLANES)

    def body(core_ref, g_ref, r_ref, o_ref):
        o_ref[...] = (g_ref[...].astype(F32) + r_ref[...].astype(F32)).astype(BF16)

    return _pallas(
        body, [core, g, r], dep=dep, n_prefetch=1, grid=(nchip, h // th, n // tn),
        in_specs=[pl.BlockSpec((None, None, th, tn), lambda a, i, j, cr: (a, cr[0], i, j)),
                  pl.BlockSpec((None, th, tn), lambda a, i, j, cr: (a, i, j))],
        out_specs=pl.BlockSpec((None, th, tn), lambda a, i, j, cr: (a, i, j)),
        out_shape=jax.ShapeDtypeStruct((nchip, h, n), BF16),
        sem=("parallel", "parallel", "parallel"), name=name)


def chip_sum(own, got, where, name, dep=None):
    nchip, h, n = got.shape
    th, tn = _ew_tiles(h, n, BF16_SUBLANES, elems=256 * 1024)

    def body(where_ref, own_ref, *rest):
        got_refs, o_ref = rest[:nchip], rest[nchip]
        chip = where_ref[0]
        acc = None
        for s in range(nchip):
            term = jnp.where(chip == s, own_ref[...], got_refs[s][...]).astype(F32)
            acc = term if acc is None else acc + term
        o_ref[...] = acc

    def got_spec(s):
        return pl.BlockSpec((None, th, tn), lambda i, j, wr: (jnp.where(wr[0] == s, (s + 1) % nchip, s), i, j))

    return _pallas(
        body, [where, own, *[got] * nchip], dep=dep, n_prefetch=1, grid=(h // th, n // tn),
        in_specs=[pl.BlockSpec((None, th, tn), lambda i, j, wr: (wr[0], i, j))]
        + [got_spec(s) for s in range(nchip)],
        out_specs=pl.BlockSpec((None, th, tn), lambda i, j, wr: (wr[1], i, j)),
        out_shape=jax.ShapeDtypeStruct((2, h, n), F32),
        sem=("parallel", "parallel"), name=name)


def kernel(x, c, w_ada, b_ada, norm_mix_g, w_in, q_norm_g, k_norm_g, attn_sinks, rel_bias, w_attn_out, conv_w, conv_b, conv_ln_g, conv_ln_b, w_conv_out, w_mix_out, norm_ffn_g, w_ffn_in, w_ffn_out, loss_target, m_w_ada, m_b_ada, m_norm_mix_g, m_w_in, m_q_norm_g, m_k_norm_g, m_attn_sinks, m_rel_bias, m_w_attn_out, m_conv_w, m_conv_b, m_conv_ln_g, m_conv_ln_b, m_w_conv_out, m_w_mix_out, m_norm_ffn_g, m_w_ffn_in, m_w_ffn_out, v_w_ada, v_b_ada, v_norm_mix_g, v_w_in, v_q_norm_g, v_k_norm_g, v_attn_sinks, v_rel_bias, v_w_attn_out, v_conv_w, v_conv_b, v_conv_ln_g, v_conv_ln_b, v_w_conv_out, v_w_mix_out, v_norm_ffn_g, v_w_ffn_in, v_w_ffn_out):
    run = InOrder()
    xi, yi, ci = _position()
    chip = 2 * xi + yi
    me = 2 * chip + ci
    chip_arr = chip.astype(jnp.int32).reshape(1)
    core_arr = ci.astype(jnp.int32).reshape(1)
    where_arr = jnp.stack([chip, ci]).astype(jnp.int32)

    xe, tgt = x[0], loss_target[0]
    t, d = xe.shape
    hd = q_norm_g.shape[-1]
    nq = attn_sinks.shape[-1]
    aw = nq * hd
    ch = conv_b.shape[-1]
    in_width = N_CHIPS * w_in.shape[-1]
    kvw = (in_width - aw - 2 * ch - 2 * d) // 2
    nkv = kvw // hd
    dff = N_CHIPS * w_ffn_out.shape[1]
    off_k, off_v, off_ca = aw, aw + kvw, aw + 2 * kvw
    off_cb, off_ga, off_gc = off_ca + ch, off_ca + 2 * ch, off_ca + 2 * ch + d
    nc_ada = w_ada.shape[-1]
    ch_loc = conv_w.shape[-1]
    nj_ffn = w_ffn_in.shape[-1]
    perm_ffn = ffn_perm(N_CHIPS)

    big = {"w_in": w_in[0], "w_attn_out": w_attn_out[0], "w_conv_out": w_conv_out[0], "w_mix_out": w_mix_out[0],
           "w_ffn_in": w_ffn_in[0], "w_ffn_out": w_ffn_out[0]}
    moments = {"w_in": (m_w_in, v_w_in), "w_attn_out": (m_w_attn_out, v_w_attn_out),
               "w_conv_out": (m_w_conv_out, v_w_conv_out), "w_mix_out": (m_w_mix_out, v_w_mix_out),
               "w_ffn_in": (m_w_ffn_in, v_w_ffn_in), "w_ffn_out": (m_w_ffn_out, v_w_ffn_out)}
    gather_groups = {"in": ["w_in"], "branch_out": ["w_attn_out", "w_conv_out"], "mix_out": ["w_mix_out"],
                     "ffn_in": ["w_ffn_in"], "ffn_out": ["w_ffn_out"]}
    grads, deltas, new_m, new_v = {}, {}, {}, {}

    def gather_cast(gname):
        bufs = []
        for n in gather_groups[gname]:
            r, ncol = big[n].shape
            bufs.append(run(cast_into_slot, big[n], chip_arr, N_CHIPS, "cast_" + n).reshape(N_CHIPS * r, ncol))
        return bufs

    def gather_ici_start(gname, bufs):
        return run(exchange_start, "gather_ici_start_" + gname, bufs, 3 * len(bufs), plan_gather_ici)

    def gather_pass_on(gname, ici):
        landed = run(exchange_wait, "gather_ici_wait_" + gname, ici, plan_gather_ici)
        return run(exchange_start, "gather_d2d_start_" + gname, landed, 3 * len(landed), plan_gather_d2d)

    def gathered(gname, d2d):
        outs = run(exchange_wait, "gather_d2d_wait_" + gname, d2d, plan_gather_d2d)
        return [o.reshape(N_CHIPS, *big[n].shape) for o, n in zip(outs, gather_groups[gname])]

    def rs_pair_start(gname, names, partials):
        blocks = [g.reshape(N_CHIPS, 2, big[n].shape[0] // 2, big[n].shape[1]) for n, g in zip(names, partials)]
        land = [lax.empty((N_CHIPS,) + b.shape[2:], BF16) for b in blocks]
        return run(exchange_start, "pair_exchange_start_" + gname, blocks + land, N_CHIPS * len(blocks),
                   plan_pair_exchange)

    def rs_chip_start(gname, names, pair):
        nw = len(names)
        outs = run(exchange_wait, "pair_exchange_wait_" + gname, pair, plan_pair_exchange)
        sums = [run(pair_sum, g, r, core_arr, "pair_sum_" + n) for n, g, r in zip(names, outs[:nw], outs[nw:])]
        land = [lax.empty(s.shape, BF16) for s in sums]
        return run(exchange_start, "chip_exchange_start_" + gname, sums + land, 3 * nw, plan_chip_exchange)

    def rs_share_start(gname, names, chipx):
        nw = len(names)
        outs = run(exchange_wait, "chip_exchange_wait_" + gname, chipx, plan_chip_exchange)
        halves = [run(chip_sum, s, r, where_arr, "chip_sum_" + n) for n, s, r in zip(names, outs[:nw], outs[nw:])]
        return run(exchange_start, "pair_share_start_" + gname, halves, nw, plan_pair_share)

    def rs_finish(gname, names, share):
        fulls = run(exchange_wait, "pair_share_wait_" + gname, share, plan_pair_share)
        for n, g2 in zip(names, fulls):
            g, dl, nm, nv = run(adamw, big[n], g2.reshape(big[n].shape), moments[n][0][0], moments[n][1][0],
                                "adamw_" + n, copy_grad=True)
            grads[n], deltas[n], new_m[n], new_v[n] = g[None], dl[None], nm[None], nv[None]

    near, far = (0, 1), (2,)
    plan_ici_near = functools.partial(plan_gather_ici, peers=near)
    plan_ici_far, n_far = plan_gather_relay, 2
    plan_d2d_near = functools.partial(plan_gather_d2d, peers=near)
    plan_d2d_far = functools.partial(plan_gather_d2d, peers=far)
    bufs_in = gather_cast("in")
    row1, offs1 = _row_pack([c, conv_w[0].reshape(1, CONV_WIDTH * ch_loc)])
    got1 = run(allgather_small, row1, "allgather_cond")
    ici_near = run(exchange_start, "gather_ici_start_in_near", bufs_in, len(near), plan_ici_near)
    c_all = got1[:, 0, :d]
    conv_w_full = got1[0::2, 0, offs1[1]:offs1[1] + CONV_WIDTH * ch_loc].reshape(N_CHIPS, CONV_WIDTH, ch_loc)
    conv_w_full = jnp.transpose(conv_w_full, (1, 0, 2)).reshape(CONV_WIDTH, ch)
    conv_w_pad = jnp.pad(conv_w_full, ((0, 1), (0, 0)))
    c_t = jnp.transpose(c_all)
    mod_cols = run(ada_fwd, c_t, w_ada[0])
    rest_bufs = {gname: gather_cast(gname) for gname in gather_groups if gname != "in"}
    bucket = _t5_bucket_table()
    bucket_p, bucket_c = jnp.asarray(bucket[:, :BLOCK]), jnp.asarray(bucket[:, BLOCK:])
    bias_p, bias_c = run(bias_table, rel_bias, bucket_p, bucket_c)
    got2 = run(allgather_small, mod_cols, "allgather_mod")
    mod_all = got2.reshape(N_CHIPS, 2, N_DEV, nc_ada)[:, 0]
    mod = lax.dynamic_slice_in_dim(mod_all, me, 1, axis=1).reshape(1, N_CHIPS * nc_ada) + b_ada
    mod = jnp.pad(mod.reshape(N_MOD, d), ((0, SUBLANES - N_MOD), (0, 0)))

    landed = run(exchange_wait, "gather_ici_wait_in_near", ici_near, plan_ici_near)
    ici_far = run(exchange_start, "gather_ici_start_in_far", landed, n_far, plan_ici_far)
    d2d_near = run(exchange_start, "gather_d2d_start_in_near", ici_far.bufs, len(near), plan_d2d_near)
    h = run(pre_mix_fwd, xe, mod, norm_mix_g)
    ici = {gname: gather_ici_start(gname, rest_bufs[gname]) for gname in ("branch_out", "mix_out")}
    ici_near_ffn = run(exchange_start, "gather_ici_start_ffn_in_near", rest_bufs["ffn_in"], len(near), plan_ici_near)
    landed = run(exchange_wait, "gather_d2d_wait_in_near", d2d_near, plan_d2d_near)
    landed = run(exchange_wait, "gather_ici_wait_in_far", ici_far, plan_ici_far, bufs=landed)
    d2d_far = run(exchange_start, "gather_d2d_start_in_far", landed, len(far), plan_d2d_far)
    landed = run(exchange_wait, "gather_d2d_wait_in_far", d2d_far, plan_d2d_far)
    wg_in = landed[0].reshape(N_CHIPS, *big["w_in"].shape)
    p = run(mm_nn, h, wg_in, tn=wg_in.shape[2], tk=d, out_dtype=BF16, name="mm_in")
    d2d_branch = gather_pass_on("branch_out", ici["branch_out"])

    sinks3 = attn_sinks.reshape(nq, 1, 1)
    attn_o = run(attn_fwd, p, bias_p, bias_c, sinks3, q_norm_g, k_norm_g, aw=aw, kvw=kvw)
    ca, cb = p[:, off_ca:off_cb], p[:, off_cb:off_ga]
    s_conv, co_conv = run(conv_fwd, ca, cb, conv_w_pad, conv_b, conv_ln_g, conv_ln_b)
    wg_attn_out, wg_conv_out = gathered("branch_out", d2d_branch)
    y_attn = run(mm_nn, attn_o, wg_attn_out, tn=_tile(wg_attn_out.shape[2], 512), tk=aw, out_dtype=BF16,
                 name="mm_attn_out")
    y_conv = run(mm_nn, s_conv, wg_conv_out, tn=_tile(wg_conv_out.shape[2], 512), tk=ch, out_dtype=BF16,
                 name="mm_conv_out")
    landed = run(exchange_wait, "gather_ici_wait_ffn_in_near", ici_near_ffn, plan_ici_near)
    ici_far_ffn = run(exchange_start, "gather_ici_start_ffn_in_far", landed, n_far, plan_ici_far)
    ici["ffn_out"] = gather_ici_start("ffn_out", rest_bufs["ffn_out"])
    d2d_near_ffn = run(exchange_start, "gather_d2d_start_ffn_in_near", ici_far_ffn.bufs, len(near), plan_d2d_near)
    merged = run(merge_fwd, p, y_attn, y_conv, off_ga, off_gc)
    d2d_mix = gather_pass_on("mix_out", ici["mix_out"])
    (wg_mix_out,) = gathered("mix_out", d2d_mix)
    wg_mix_out = wg_mix_out.reshape(1, d, d)
    o_m = run(mm_nn, merged, wg_mix_out, tn=_tile(d, 512), tk=d, out_dtype=BF16, name="mm_mix_out")
    landed = run(exchange_wait, "gather_d2d_wait_ffn_in_near", d2d_near_ffn, plan_d2d_near)
    landed = run(exchange_wait, "gather_ici_wait_ffn_in_far", ici_far_ffn, plan_ici_far, bufs=landed)
    d2d_far_ffn = run(exchange_start, "gather_d2d_start_ffn_in_far", landed, len(far), plan_d2d_far)
    x1, h2 = run(pre_ffn_fwd, xe, o_m, mod, norm_ffn_g)
    landed = run(exchange_wait, "gather_d2d_wait_ffn_in_far", d2d_far_ffn, plan_d2d_far)
    wg_ffn_in = landed[0].reshape(N_CHIPS, *big["w_ffn_in"].shape)
    f = run(mm_nn, h2, wg_ffn_in, tn=_tile(nj_ffn, 1408), tk=d, out_dtype=BF16, name="mm_ffn_in", perm=perm_ffn)
    d2d_ffn_out = gather_pass_on("ffn_out", ici["ffn_out"])
    act = run(swiglu_fwd, f, nj_ffn)
    (wg_ffn_out,) = gathered("ffn_out", d2d_ffn_out)
    wg_ffn_out = wg_ffn_out.reshape(1, dff, d)
    o_f = run(mm_nn, act, wg_ffn_out, tn=_tile(d, 512), tk=_tile(dff, 2816), out_dtype=BF16, name="mm_ffn_out")
    loss11, dy, dof, acc_l = run(loss_head, x1, o_f, tgt, mod)

    gw_ffn_out = run(mm_tn, act, dof, 1, tk=_tile(dff, 512), tn=d, name="mm_ffn_out_dw")
    px_ffn_out = rs_pair_start("ffn_out", ["w_ffn_out"], [gw_ffn_out])
    dact = run(mm_nt, dof, wg_ffn_out, tko=_tile(dff, 512), tn=d, out_dtype=BF16, name="mm_ffn_out_dx")
    cx_ffn_out = rs_chip_start("ffn_out", ["w_ffn_out"], px_ffn_out)
    df = run(swiglu_bwd, f, dact, nj_ffn)
    gw_ffn_in = run(mm_tn, h2, df, N_CHIPS, tk=d, tn=_tile(nj_ffn, 1408), name="mm_ffn_in_dw",
                    perm=perm_ffn)
    px_ffn_in = rs_pair_start("ffn_in", ["w_ffn_in"], [gw_ffn_in])
    dh2 = run(mm_nt, df, wg_ffn_in, tko=_tile(d, 512), tn=nj_ffn, out_dtype=BF16, name="mm_ffn_in_dx", perm=perm_ffn)
    sh_ffn_out = rs_share_start("ffn_out", ["w_ffn_out"], cx_ffn_out)
    cx_ffn_in = rs_chip_start("ffn_in", ["w_ffn_in"], px_ffn_in)
    dx1, dom, acc_f = run(pre_ffn_bwd, x1, dh2, dy, o_m, mod, norm_ffn_g)
    gw_mix_out = run(mm_tn, merged, dom, 1, tk=d, tn=_tile(d, 1024), name="mm_mix_out_dw")
    px_mix = rs_pair_start("mix_out", ["w_mix_out"], [gw_mix_out])
    dy_attn, dy_conv, dga, dgc = run(mix_out_dx_merge_bwd, dom, wg_mix_out.reshape(d, d), p, y_attn, y_conv,
                                     off_ga, off_gc)
    rs_finish("ffn_out", ["w_ffn_out"], sh_ffn_out)
    cx_mix = rs_chip_start("mix_out", ["w_mix_out"], px_mix)
    gw_attn_out = run(mm_tn, attn_o, dy_attn, N_CHIPS, tk=aw, tn=_tile(wg_attn_out.shape[2], 512),
                      name="mm_attn_out_dw")
    gw_conv_out = run(mm_tn, s_conv, dy_conv, N_CHIPS, tk=ch, tn=_tile(wg_conv_out.shape[2], 512),
                      name="mm_conv_out_dw")
    ac_names = ["w_attn_out", "w_conv_out"]
    px_ac = rs_pair_start("attn_conv_out", ac_names, [gw_attn_out, gw_conv_out])
    dattn_o = run(mm_nt, dy_attn, wg_attn_out, tko=_tile(aw, 1024), tn=_tile(wg_attn_out.shape[2], 512),
                  out_dtype=BF16, name="mm_attn_out_dx")
    ds_conv = run(mm_nt, dy_conv, wg_conv_out, tko=_tile(ch, 1024), tn=_tile(wg_conv_out.shape[2], 512),
                  out_dtype=BF16, name="mm_conv_out_dx")
    cx_ac = rs_chip_start("attn_conv_out", ac_names, px_ac)
    dca, dcb, dconv_w, dconv_vec = run(conv_bwd, ca, cb, co_conv, ds_conv, conv_w_pad, conv_ln_g, conv_ln_b)
    sh_ffn_in = rs_share_start("ffn_in", ["w_ffn_in"], cx_ffn_in)
    dqkv, dbp, dbc, dsinks, dqg, dkg = run(attn_bwd, p, bias_p, bias_c, sinks3, q_norm_g, k_norm_g, dattn_o,
                                           aw=aw, kvw=kvw)
    sh_mix = rs_share_start("mix_out", ["w_mix_out"], cx_mix)
    sh_ac = rs_share_start("attn_conv_out", ac_names, cx_ac)
    drel = run(bias_table_bwd, dbp, dbc, bucket_p, bucket_c).reshape(NUM_BUCKETS, nq)
    dp = jnp.concatenate([dqkv, dca, dcb, dga, dgc], axis=1)
    gw_in = run(mm_tn, h, dp, N_CHIPS, tk=d, tn=wg_in.shape[2], name="mm_in_dw")
    px_in = rs_pair_start("in", ["w_in"], [gw_in])
    dh = run(mm_nt, dp, wg_in, tko=_tile(d, 1024), tn=wg_in.shape[2], out_dtype=BF16, name="mm_in_dx")
    grad_x, acc_m = run(pre_mix_bwd, xe, dh, dx1, mod, norm_mix_g)

    dmod = jnp.concatenate([acc_m[0:1], acc_m[1:2], acc_f[3:4], acc_f[0:1], acc_f[1:2], acc_l[0:1]], axis=1)
    small_names = ["b_ada", "norm_mix_g", "q_norm_g", "k_norm_g", "attn_sinks", "rel_bias", "conv_b", "conv_ln_g",
                   "conv_ln_b", "norm_ffn_g"]
    small_w = [b_ada, norm_mix_g, q_norm_g, k_norm_g, attn_sinks, rel_bias, conv_b, conv_ln_g, conv_ln_b, norm_ffn_g]
    small_m = [m_b_ada, m_norm_mix_g, m_q_norm_g, m_k_norm_g, m_attn_sinks, m_rel_bias, m_conv_b, m_conv_ln_g,
               m_conv_ln_b, m_norm_ffn_g]
    small_v = [v_b_ada, v_norm_mix_g, v_q_norm_g, v_k_norm_g, v_attn_sinks, v_rel_bias, v_conv_b, v_conv_ln_g,
               v_conv_ln_b, v_norm_ffn_g]
    small_g = [dmod, acc_m[2:3], dqg, dkg, dsinks.reshape(1, nq), drel.reshape(1, NUM_BUCKETS * nq),
               dconv_vec[0:1], dconv_vec[1:2], dconv_vec[2:3], acc_f[2:3]]
    row3, offs3 = _row_pack(small_g + [dconv_w[:CONV_WIDTH].reshape(1, CONV_WIDTH * ch), loss11])
    got3 = run(allgather_small, row3, "allgather_small_grads")
    cx_in = rs_chip_start("in", ["w_in"], px_in)
    as_row = lambda a: a.reshape(1, -1)
    outs3 = run(small_sum_adamw, got3, offs3, [as_row(a) for a in small_w], [as_row(a) for a in small_m],
                [as_row(a) for a in small_v], [CONV_WIDTH * ch, 1])
    for i, (n, w) in enumerate(zip(small_names, small_w)):
        grads[n], deltas[n], new_m[n], new_v[n] = (o.reshape(w.shape) for o in outs3[4 * i:4 * i + 4])
    g_conv_w_all, loss_sum = outs3[-2].reshape(CONV_WIDTH, ch), outs3[-1]

    g_conv_w = lax.dynamic_slice_in_dim(g_conv_w_all, chip * ch_loc, ch_loc, axis=1)
    grads["conv_w"] = g_conv_w[None]
    dl, nm, nv = run(adamw, conv_w[0], g_conv_w, m_conv_w[0], v_conv_w[0], "adamw_conv_w")
    deltas["conv_w"], new_m["conv_w"], new_v["conv_w"] = dl[None], nm[None], nv[None]

    dmod_all = got3[:, 0, :N_MOD * d]
    dmod_cols = lax.dynamic_slice_in_dim(dmod_all, chip * nc_ada, nc_ada, axis=1)
    g_ada, dl, nm, nv = run(ada_grad_adamw, c_t, dmod_cols, w_ada[0], m_w_ada[0], v_w_ada[0])
    grads["w_ada"], deltas["w_ada"], new_m["w_ada"], new_v["w_ada"] = g_ada[None], dl[None], nm[None], nv[None]

    rs_finish("ffn_in", ["w_ffn_in"], sh_ffn_in)
    rs_finish("mix_out", ["w_mix_out"], sh_mix)
    rs_finish("attn_conv_out", ac_names, sh_ac)
    sh_in = rs_share_start("in", ["w_in"], cx_in)
    rs_finish("in", ["w_in"], sh_in)

    loss = loss_sum[0, 0]
    order = ["w_ada", "b_ada", "norm_mix_g", "w_in", "q_norm_g", "k_norm_g", "attn_sinks", "rel_bias", "w_attn_out",
             "conv_w", "conv_b", "conv_ln_g", "conv_ln_b", "w_conv_out", "w_mix_out", "norm_ffn_g", "w_ffn_in",
             "w_ffn_out"]
    return (loss, grad_x[None], *[grads[n] for n in order], *[deltas[n] for n in order],
            *[new_m[n] for n in order], *[new_v[n] for n in order])
```

```python
import functools
import math
from typing import Any, NamedTuple

import jax
import jax.numpy as jnp
import numpy as np
from jax import lax
from jax.experimental import pallas as pl
from jax.experimental.pallas import tpu as pltpu

F32 = jnp.float32
BF16 = jnp.bfloat16
MESH = pl.DeviceIdType.MESH

V7X_VMEM_BYTES = 64 * 1024 * 1024
VMEM_LIMIT = V7X_VMEM_BYTES - 8 * 1024 * 1024
LANES = 128
SUBLANES = 8
BF16_SUBLANES = 16

EPS = 1e-6
WINDOW = 128
BLOCK = 128
NUM_BUCKETS = 32
MAX_EXACT = NUM_BUCKETS // 2
MAX_DISTANCE = 128
CONV_WIDTH = 31
CONV_HALO = 32
ADAM_LR = 0.001
ADAM_B1 = 0.9
ADAM_B2 = 0.999
ADAM_EPS = 1e-08
ADAM_WD = 0.01
ADAM_STEP = 10
N_MOD = 6
SH_M, SC_M, GT_M, SH_F, SC_F, GT_F = range(6)

N_CHIPS = 4
N_DEV = 8

_ANY = pl.BlockSpec(memory_space=pl.ANY)
_VMEM = pl.BlockSpec(memory_space=pltpu.VMEM)
_SMEM = pl.BlockSpec(memory_space=pltpu.SMEM)
_HBM = pl.BlockSpec(memory_space=pltpu.HBM)
_SEM = pl.BlockSpec(memory_space=pltpu.SEMAPHORE)
_EFFECT = pltpu.SideEffectType.DATAFLOW_SIDE_EFFECTING


class InOrder:
    def __init__(self):
        self.token = None

    def __call__(self, fn, *args, **kw):
        return fn(*args, dep=self, **kw)


def _pallas(body, args, *, in_specs, out_specs, out_shape, name, dep=None, grid=(), n_prefetch=0, scratch=(),
            sem=None, **kw):
    n_lead = n_prefetch + len(in_specs)
    in_specs, args = list(in_specs), list(args)
    single = not isinstance(out_shape, (list, tuple))
    out_shapes = [out_shape] if single else list(out_shape)
    out_specs = [out_specs] if single else list(out_specs)
    if dep is not None:
        inner, n_out, takes = body, len(out_shapes), dep.token is not None

        def body(*refs):
            rest = refs[n_lead + (1 if takes else 0):]
            rest[n_out][...] = jnp.zeros((SUBLANES, LANES), F32)
            return inner(*refs[:n_lead], *rest[:n_out], *rest[n_out + 1:])

        if takes:
            in_specs.append(_ANY)
            args.append(dep.token)
        out_shapes.append(jax.ShapeDtypeStruct((SUBLANES, LANES), F32))
        out_specs.append(pl.BlockSpec((SUBLANES, LANES), lambda *_: (0, 0)))
    params = kw.pop("compiler_params", None)
    if params is None:
        params = pltpu.CompilerParams(dimension_semantics=sem, vmem_limit_bytes=VMEM_LIMIT)
    outs = pl.pallas_call(
        body,
        grid_spec=pltpu.PrefetchScalarGridSpec(num_scalar_prefetch=n_prefetch, grid=grid, in_specs=in_specs,
                                               out_specs=out_specs, scratch_shapes=list(scratch)),
        out_shape=out_shapes, compiler_params=params, name=name, **kw,
    )(*args)
    if dep is not None:
        dep.token = outs[-1]
        outs = outs[:-1]
    return outs[0] if single else list(outs)


def _tile(n, pref, unit=LANES):
    best = None
    for t in range(unit, min(n, pref) + 1, unit):
        if n % t == 0:
            best = t
    return best if best is not None else n


def _sigmoid(v):
    return 1.0 / (1.0 + jnp.exp(-v.astype(F32)))


ROW_CHUNK = 512


def _row_chunks(m, unit=SUBLANES):
    step = _tile(m, ROW_CHUNK, unit)
    return [(s, step) for s in range(0, m, step)]


def _ew_tiles(r, n, unit=SUBLANES, elems=512 * 1024):
    return _tile(r, max(unit, elems // n), unit), n


def _block_pos(j, perm):
    if perm is None:
        return j
    pos = 0
    for a, p in enumerate(perm):
        pos = pos + jnp.where(j == a, p, 0)
    return pos


def mm_nn(a, w, *, tn, tk, out_dtype, name, perm=None, dep=None):
    m, k = a.shape
    j, k2, nj = w.shape
    assert k == k2 and nj % tn == 0 and k % tk == 0
    npj, nk = nj // tn, k // tk

    def body(a_ref, w_ref, o_ref, *scratch):
        kk = pl.program_id(1)
        for s, sz in _row_chunks(m):
            rows = pl.ds(s, sz)
            p = jnp.dot(a_ref[rows, :], w_ref[...], preferred_element_type=F32)
            if nk == 1:
                o_ref[rows, :] = p.astype(out_dtype)
            else:
                acc = scratch[0]

                @pl.when(kk == 0)
                def _():
                    acc[rows, :] = p

                @pl.when(kk > 0)
                def _():
                    acc[rows, :] += p

                @pl.when(kk == nk - 1)
                def _():
                    o_ref[rows, :] = acc[rows, :].astype(out_dtype)

    return _pallas(
        body, [a, w], dep=dep, grid=(j * npj, nk),
        in_specs=[
            pl.BlockSpec((m, tk), lambda n, kk: (0, kk)),
            pl.BlockSpec((None, tk, tn), lambda n, kk: (n // npj, kk, n % npj)),
        ],
        out_specs=pl.BlockSpec((m, tn), lambda n, kk: (0, _block_pos(n // npj, perm) * npj + n % npj)),
        out_shape=jax.ShapeDtypeStruct((m, j * nj), out_dtype),
        scratch=[pltpu.VMEM((m, tn), F32)] if nk > 1 else [],
        sem=("parallel", "arbitrary"), name=name)


def mm_nt(g, w, *, tko, tn, name, out_dtype=F32, perm=None, dep=None):
    m, n = g.shape
    j, k, nj = w.shape
    assert n == j * nj and nj % tn == 0 and k % tko == 0
    npj, nr = nj // tn, n // tn
    in_place = out_dtype == F32

    def body(g_ref, w_ref, o_ref, *scratch):
        r = pl.program_id(1)
        acc = o_ref if in_place else (scratch[0] if nr > 1 else None)
        for s, sz in _row_chunks(m):
            rows = pl.ds(s, sz)
            p = lax.dot_general(g_ref[rows, :], w_ref[...], (((1,), (1,)), ((), ())), preferred_element_type=F32)
            if acc is None:
                o_ref[rows, :] = p.astype(out_dtype)
                continue

            @pl.when(r == 0)
            def _():
                acc[rows, :] = p

            @pl.when(r > 0)
            def _():
                acc[rows, :] += p

            if not in_place:
                @pl.when(r == nr - 1)
                def _():
                    o_ref[rows, :] = acc[rows, :].astype(out_dtype)

    return _pallas(
        body, [g, w], dep=dep, grid=(k // tko, nr),
        in_specs=[
            pl.BlockSpec((m, tn), lambda ko, r: (0, _block_pos(r // npj, perm) * npj + r % npj)),
            pl.BlockSpec((None, tko, tn), lambda ko, r: (r // npj, ko, r % npj)),
        ],
        out_specs=pl.BlockSpec((m, tko), lambda ko, r: (0, ko)),
        out_shape=jax.ShapeDtypeStruct((m, k), out_dtype),
        scratch=[pltpu.VMEM((m, tko), F32)] if (nr > 1 and not in_place) else [],
        sem=("parallel", "arbitrary"), name=name)


def mm_tn(a, g, n_blocks, *, tk, tn, name, perm=None, dep=None):
    m, k = a.shape
    m2, n = g.shape
    nj = n // n_blocks
    assert m == m2 and nj % tn == 0 and k % tk == 0
    npj = nj // tn

    def body(a_ref, g_ref, o_ref):
        for s, sz in _row_chunks(tk, LANES):
            p = lax.dot_general(a_ref[:, pl.ds(s, sz)], g_ref[...], (((0,), (0,)), ((), ())),
                                preferred_element_type=F32)
            o_ref[pl.ds(s, sz), :] = p.astype(BF16)

    return _pallas(
        body, [a, g], dep=dep, grid=(k // tk, n // tn),
        in_specs=[
            pl.BlockSpec((m, tk), lambda kk, nn: (0, kk)),
            pl.BlockSpec((m, tn), lambda kk, nn: (0, _block_pos(nn // npj, perm) * npj + nn % npj)),
        ],
        out_specs=pl.BlockSpec((None, tk, tn), lambda kk, nn: (nn // npj, kk, nn % npj)),
        out_shape=jax.ShapeDtypeStruct((n_blocks, k, nj), BF16),
        sem=("parallel", "parallel"), name=name)


ROW_TILE = 256


def _row_spec(tr, width):
    return pl.BlockSpec((tr, width), lambda i: (i, 0))


def _full_spec(shape):
    return pl.BlockSpec(shape, lambda *_: (0,) * len(shape))


def _rms(xv):
    return lax.rsqrt(jnp.mean(xv * xv, axis=-1, keepdims=True) + EPS)


def _mod_row(mod_ref, row):
    return mod_ref[pl.ds(row, 1), :]


def pre_mix_fwd(x, mod, gain, dep=None):
    t, d = x.shape
    tr = _tile(t, ROW_TILE, SUBLANES)

    def body(x_ref, mod_ref, g_ref, h_ref):
        xv = x_ref[...]
        y = xv * _rms(xv) * g_ref[...]
        h_ref[...] = (y * (1.0 + _mod_row(mod_ref, SC_M)) + _mod_row(mod_ref, SH_M)).astype(BF16)

    return _pallas(
        body, [x, mod, gain], dep=dep, grid=(t // tr,),
        in_specs=[_row_spec(tr, d), _full_spec(mod.shape), _full_spec(gain.shape)],
        out_specs=_row_spec(tr, d),
        out_shape=jax.ShapeDtypeStruct((t, d), BF16),
        sem=("parallel",), name="pre_mix_fwd")


def pre_ffn_fwd(x, o_m, mod, gain, dep=None):
    t, d = x.shape
    tr = _tile(t, ROW_TILE, SUBLANES)

    def body(x_ref, om_ref, mod_ref, g_ref, x1_ref, h_ref):
        x1 = x_ref[...] + _mod_row(mod_ref, GT_M) * om_ref[...]
        x1_ref[...] = x1
        y = x1 * _rms(x1) * g_ref[...]
        h_ref[...] = (y * (1.0 + _mod_row(mod_ref, SC_F)) + _mod_row(mod_ref, SH_F)).astype(BF16)

    return _pallas(
        body, [x, o_m, mod, gain], dep=dep, grid=(t // tr,),
        in_specs=[_row_spec(tr, d), _row_spec(tr, d), _full_spec(mod.shape), _full_spec(gain.shape)],
        out_specs=[_row_spec(tr, d), _row_spec(tr, d)],
        out_shape=[jax.ShapeDtypeStruct((t, d), F32), jax.ShapeDtypeStruct((t, d), BF16)],
        sem=("parallel",), name="pre_ffn_fwd")


def loss_head(x1, o_f, target, mod, dep=None):
    t, d = x1.shape
    tr = _tile(t, ROW_TILE, SUBLANES)

    def body(x1_ref, of_ref, tg_ref, mod_ref, loss_ref, dy_ref, dof_ref, acc_ref):
        i = pl.program_id(0)
        gt = _mod_row(mod_ref, GT_F)
        of = of_ref[...].astype(F32)
        err = x1_ref[...] + gt * of - tg_ref[...]
        dy = err * (1.0 / d)
        dy_ref[...] = dy.astype(BF16)
        dof_ref[...] = (dy * gt).astype(BF16)
        part = (0.5 / d) * jnp.sum(jnp.sum(err * err, axis=1, keepdims=True), axis=0, keepdims=True)
        dgt = jnp.sum(dy * of, axis=0, keepdims=True)

        @pl.when(i == 0)
        def _():
            loss_ref[...] = jnp.zeros_like(loss_ref)
            acc_ref[...] = jnp.zeros_like(acc_ref)

        loss_ref[...] += part
        acc_ref[pl.ds(0, 1), :] += dgt

    return _pallas(
        body, [x1, o_f, target, mod], dep=dep, grid=(t // tr,),
        in_specs=[_row_spec(tr, d), _row_spec(tr, d), _row_spec(tr, d), _full_spec(mod.shape)],
        out_specs=[_full_spec((1, 1)), _row_spec(tr, d), _row_spec(tr, d), _full_spec((SUBLANES, d))],
        out_shape=[jax.ShapeDtypeStruct((1, 1), F32), jax.ShapeDtypeStruct((t, d), BF16),
                   jax.ShapeDtypeStruct((t, d), BF16), jax.ShapeDtypeStruct((SUBLANES, d), F32)],
        sem=("arbitrary",), name="loss_head")


def _norm_bwd(xv, dh, sc, gain):
    rstd = _rms(xv)
    yn = xv * rstd
    dsh = jnp.sum(dh, axis=0, keepdims=True)
    dsc = jnp.sum(dh * (yn * gain), axis=0, keepdims=True)
    dgain = jnp.sum(dh * (1.0 + sc) * yn, axis=0, keepdims=True)
    dyn = dh * ((1.0 + sc) * gain)
    dx = rstd * (dyn - yn * jnp.mean(dyn * yn, axis=-1, keepdims=True))
    return dx, dsh, dsc, dgain


def pre_ffn_bwd(x1, dh2, dy, o_m, mod, gain, dep=None):
    t, d = x1.shape
    tr = _tile(t, ROW_TILE, SUBLANES)

    def body(x1_ref, dh_ref, dy_ref, om_ref, mod_ref, g_ref, dx1_ref, dom_ref, acc_ref):
        i = pl.program_id(0)
        dxn, dsh, dsc, dgain = _norm_bwd(x1_ref[...], dh_ref[...].astype(F32), _mod_row(mod_ref, SC_F), g_ref[...])
        dx1 = dy_ref[...] + dxn
        dx1_ref[...] = dx1
        dom_ref[...] = (dx1 * _mod_row(mod_ref, GT_M)).astype(BF16)
        dgt = jnp.sum(dx1 * om_ref[...], axis=0, keepdims=True)

        @pl.when(i == 0)
        def _():
            acc_ref[...] = jnp.zeros_like(acc_ref)

        acc_ref[pl.ds(0, 1), :] += dsh
        acc_ref[pl.ds(1, 1), :] += dsc
        acc_ref[pl.ds(2, 1), :] += dgain
        acc_ref[pl.ds(3, 1), :] += dgt

    return _pallas(
        body, [x1, dh2, dy, o_m, mod, gain], dep=dep, grid=(t // tr,),
        in_specs=[_row_spec(tr, d)] * 4 + [_full_spec(mod.shape), _full_spec(gain.shape)],
        out_specs=[_row_spec(tr, d), _row_spec(tr, d), _full_spec((SUBLANES, d))],
        out_shape=[jax.ShapeDtypeStruct((t, d), F32), jax.ShapeDtypeStruct((t, d), BF16),
                   jax.ShapeDtypeStruct((SUBLANES, d), F32)],
        sem=("arbitrary",), name="pre_ffn_bwd")


def pre_mix_bwd(x, dh, dx1, mod, gain, dep=None):
    t, d = x.shape
    tr = _tile(t, ROW_TILE, SUBLANES)

    def body(x_ref, dh_ref, dx1_ref, mod_ref, g_ref, gx_ref, acc_ref):
        i = pl.program_id(0)
        dxn, dsh, dsc, dgain = _norm_bwd(x_ref[...], dh_ref[...].astype(F32), _mod_row(mod_ref, SC_M), g_ref[...])
        gx_ref[...] = dx1_ref[...] + dxn

        @pl.when(i == 0)
        def _():
            acc_ref[...] = jnp.zeros_like(acc_ref)

        acc_ref[pl.ds(0, 1), :] += dsh
        acc_ref[pl.ds(1, 1), :] += dsc
        acc_ref[pl.ds(2, 1), :] += dgain

    return _pallas(
        body, [x, dh, dx1, mod, gain], dep=dep, grid=(t // tr,),
        in_specs=[_row_spec(tr, d)] * 3 + [_full_spec(mod.shape), _full_spec(gain.shape)],
        out_specs=[_row_spec(tr, d), _full_spec((SUBLANES, d))],
        out_shape=[jax.ShapeDtypeStruct((t, d), F32), jax.ShapeDtypeStruct((SUBLANES, d), F32)],
        sem=("arbitrary",), name="pre_mix_bwd")


def branch_out_merge(attn_o, s_conv, w_attn, w_conv, p, off_ga, off_gc, dep=None):
    t = attn_o.shape[0]
    j, ka, nj = w_attn.shape
    kc = w_conv.shape[1]
    assert off_ga % nj == 0 and off_gc % nj == 0

    def body(a_ref, s_ref, wa_ref, wc_ref, ga_ref, gc_ref, ya_ref, yc_ref, m_ref):
        for s, sz in _row_chunks(t):
            rows = pl.ds(s, sz)
            ya = jnp.dot(a_ref[rows, :], wa_ref[...], preferred_element_type=F32)
            yc = jnp.dot(s_ref[rows, :], wc_ref[...], preferred_element_type=F32)
            ya_ref[rows, :] = ya.astype(BF16)
            yc_ref[rows, :] = yc.astype(BF16)
            m_ref[rows, :] = (_sigmoid(ga_ref[rows, :]) * ya + _sigmoid(gc_ref[rows, :]) * yc).astype(BF16)

    col = pl.BlockSpec((t, nj), lambda b: (0, b))
    return _pallas(
        body, [attn_o, s_conv, w_attn, w_conv, p, p], dep=dep, grid=(j,),
        in_specs=[pl.BlockSpec((t, ka), lambda b: (0, 0)), pl.BlockSpec((t, kc), lambda b: (0, 0)),
                  pl.BlockSpec((None, ka, nj), lambda b: (b, 0, 0)), pl.BlockSpec((None, kc, nj), lambda b: (b, 0, 0)),
                  pl.BlockSpec((t, nj), lambda b: (0, off_ga // nj + b)),
                  pl.BlockSpec((t, nj), lambda b: (0, off_gc // nj + b))],
        out_specs=[col] * 3,
        out_shape=[jax.ShapeDtypeStruct((t, j * nj), BF16)] * 3,
        sem=("parallel",), name="branch_out_merge")


def mix_out_dx_merge_bwd(dom, w_mix, p, y_attn, y_conv, off_ga, off_gc, dep=None):
    t, d = y_attn.shape
    cw = math.gcd(math.gcd(off_ga, off_gc), math.gcd(d, 256))

    def body(dom_ref, w_ref, ga_ref, gc_ref, ya_ref, yc_ref, dya_ref, dyc_ref, dga_ref, dgc_ref):
        for s, sz in _row_chunks(t):
            rows = pl.ds(s, sz)
            dm = lax.dot_general(dom_ref[rows, :], w_ref[...], (((1,), (1,)), ((), ())), preferred_element_type=F32)
            sa = _sigmoid(ga_ref[rows, :])
            sc = _sigmoid(gc_ref[rows, :])
            dya_ref[rows, :] = (dm * sa).astype(BF16)
            dyc_ref[rows, :] = (dm * sc).astype(BF16)
            dga_ref[rows, :] = (dm * ya_ref[rows, :] * sa * (1.0 - sa)).astype(BF16)
            dgc_ref[rows, :] = (dm * yc_ref[rows, :] * sc * (1.0 - sc)).astype(BF16)

    col = pl.BlockSpec((t, cw), lambda j: (0, j))
    return _pallas(
        body, [dom, w_mix, p, p, y_attn, y_conv], dep=dep, grid=(d // cw,),
        in_specs=[pl.BlockSpec((t, d), lambda j: (0, 0)), pl.BlockSpec((cw, d), lambda j: (j, 0)),
                  pl.BlockSpec((t, cw), lambda j: (0, off_ga // cw + j)),
                  pl.BlockSpec((t, cw), lambda j: (0, off_gc // cw + j)), col, col],
        out_specs=[col] * 4,
        out_shape=[jax.ShapeDtypeStruct((t, d), BF16)] * 4,
        sem=("parallel",), name="mm_mix_out_dx_merge_bwd")


def ffn_perm(n_blocks):
    half = n_blocks // 2
    return tuple(2 * j if j < half else 2 * (j - half) + 1 for j in range(n_blocks))


def swiglu_fwd(f, nj, dep=None):
    t, two = f.shape
    tr = _tile(t, ROW_TILE, SUBLANES)
    npair = two // (2 * nj)

    def body(f_ref, o_ref):
        g = f_ref[:, :nj].astype(F32)
        u = f_ref[:, nj:].astype(F32)
        o_ref[...] = (g * _sigmoid(g) * u).astype(BF16)

    return _pallas(
        body, [f], dep=dep, grid=(t // tr, npair),
        in_specs=[pl.BlockSpec((tr, 2 * nj), lambda i, j: (i, j))],
        out_specs=pl.BlockSpec((tr, nj), lambda i, j: (i, j)),
        out_shape=jax.ShapeDtypeStruct((t, two // 2), BF16),
        sem=("parallel", "parallel"), name="swiglu_fwd")


def swiglu_bwd(f, dact, nj, dep=None):
    t, two = f.shape
    tr = _tile(t, ROW_TILE, SUBLANES)
    npair = two // (2 * nj)

    def body(f_ref, da_ref, o_ref):
        g = f_ref[:, :nj].astype(F32)
        u = f_ref[:, nj:].astype(F32)
        da = da_ref[...]
        s = _sigmoid(g)
        o_ref[:, :nj] = (da * u * (s * (1.0 + g * (1.0 - s)))).astype(BF16)
        o_ref[:, nj:] = (da * (g * s)).astype(BF16)

    return _pallas(
        body, [f, dact], dep=dep, grid=(t // tr, npair),
        in_specs=[pl.BlockSpec((tr, 2 * nj), lambda i, j: (i, j)), pl.BlockSpec((tr, nj), lambda i, j: (i, j))],
        out_specs=pl.BlockSpec((tr, 2 * nj), lambda i, j: (i, j)),
        out_shape=jax.ShapeDtypeStruct((t, two), BF16),
        sem=("parallel", "parallel"), name="swiglu_bwd")


def _t5_bucket_table():
    q_off = np.arange(BLOCK)
    k_off = np.arange(2 * BLOCK)
    dist = q_off[:, None] + BLOCK - k_off[None, :]
    n = np.maximum(dist, 0)
    nf = np.maximum(n, 1).astype(np.float32)
    large = MAX_EXACT + (np.log(nf / np.float32(MAX_EXACT)) / np.float32(math.log(MAX_DISTANCE / MAX_EXACT))
                         * np.float32(NUM_BUCKETS - MAX_EXACT)).astype(np.int32)
    large = np.minimum(large, NUM_BUCKETS - 1)
    bucket = np.where(n < MAX_EXACT, n, large).astype(np.int32)
    allowed = (dist >= 0) & (dist < WINDOW)
    return np.where(allowed, bucket, -1).astype(np.int32)


def bias_table(rel_bias, bucket_p, bucket_c, dep=None):
    nb, nq = rel_bias.shape

    def body(rb_ref, bkp_ref, bkc_ref, op_ref, oc_ref):
        for bk_ref, o_ref in ((bkp_ref, op_ref), (bkc_ref, oc_ref)):
            bk = bk_ref[...]
            for h in range(nq):
                acc = jnp.full(bk.shape, -jnp.inf, F32)
                for b in range(nb):
                    acc = jnp.where(bk == b, rb_ref[b, h], acc)
                o_ref[h] = acc

    return _pallas(
        body, [rel_bias, bucket_p, bucket_c], dep=dep,
        in_specs=[_SMEM, _VMEM, _VMEM], out_specs=[_VMEM, _VMEM],
        out_shape=[jax.ShapeDtypeStruct((nq,) + bucket_p.shape, F32)] * 2,
        name="bias_table")


def bias_table_bwd(dbp, dbc, bucket_p, bucket_c, dep=None):
    nq = dbp.shape[0]

    def body(dbp_ref, dbc_ref, bkp_ref, bkc_ref, o_ref):
        bkp, bkc = bkp_ref[...][None], bkc_ref[...][None]
        dp, dc = dbp_ref[...], dbc_ref[...]
        for b in range(NUM_BUCKETS):
            sel = jnp.where(bkp == b, dp, 0.0) + jnp.where(bkc == b, dc, 0.0)
            o_ref[b] = jnp.sum(jnp.sum(sel, axis=2, keepdims=True), axis=1, keepdims=True)

    return _pallas(
        body, [dbp, dbc, bucket_p, bucket_c], dep=dep,
        in_specs=[_VMEM] * 4, out_specs=_VMEM,
        out_shape=jax.ShapeDtypeStruct((NUM_BUCKETS, nq, 1, 1), F32),
        name="bias_table_bwd")


_BNT = (((2,), (2,)), ((0,), (0,)))
_BNN = (((2,), (1,)), ((0,), (0,)))
_BTN = (((1,), (1,)), ((0,), (0,)))


@jax.custom_vjp
def _bdot_nt(a, b):
    return lax.dot_general(a.astype(BF16), b.astype(BF16), _BNT, preferred_element_type=F32)


def _bdot_nt_fwd(a, b):
    return _bdot_nt(a, b), (a, b)


def _bdot_nt_bwd(res, g):
    a, b = res
    gb = g.astype(BF16)
    da = lax.dot_general(gb, b.astype(BF16), _BNN, preferred_element_type=F32)
    db = lax.dot_general(gb, a.astype(BF16), _BTN, preferred_element_type=F32)
    return da, db


_bdot_nt.defvjp(_bdot_nt_fwd, _bdot_nt_bwd)


@jax.custom_vjp
def _bdot_nn(a, b):
    return lax.dot_general(a.astype(BF16), b.astype(BF16), _BNN, preferred_element_type=F32)


def _bdot_nn_fwd(a, b):
    return _bdot_nn(a, b), (a, b)


def _bdot_nn_bwd(res, g):
    a, b = res
    gb = g.astype(BF16)
    da = lax.dot_general(gb, b.astype(BF16), _BNT, preferred_element_type=F32)
    db = lax.dot_general(a.astype(BF16), gb, _BTN, preferred_element_type=F32)
    return da, db


_bdot_nn.defvjp(_bdot_nn_fwd, _bdot_nn_bwd)


def _attn_math(q, kp, kc, vp, vc, bp, bc, sinks, qg, kg, *, prev_ok, scale):
    h, rows, _ = q.shape
    b = kp.shape[1]
    qn = q * _rms(q) * qg
    kpn = kp * _rms(kp) * kg
    kcn = kc * _rms(kc) * kg
    lp = _bdot_nt(qn, kpn) * scale + bp.reshape(h, rows, b)
    lc = _bdot_nt(qn, kcn) * scale + bc.reshape(h, rows, b)
    lp = jnp.where(prev_ok, lp, -jnp.inf)
    sink = jnp.broadcast_to(sinks, (sinks.shape[0], b, 1)).reshape(h, rows, 1)
    m = jnp.maximum(jnp.maximum(jnp.max(lp, axis=-1, keepdims=True), jnp.max(lc, axis=-1, keepdims=True)), sink)
    m = lax.stop_gradient(m)
    pp = jnp.exp(lp - m)
    pc = jnp.exp(lc - m)
    den = jnp.sum(pp, axis=-1, keepdims=True) + jnp.sum(pc, axis=-1, keepdims=True) + jnp.exp(sink - m)
    inv = 1.0 / den
    return _bdot_nn(pp * inv, vp) + _bdot_nn(pc * inv, vc)


def _attn_specs(p, aw, kvw, nq, hd, nblk, reverse):
    assert aw % (2 * kvw) == 0
    kv_col = aw // (2 * kvw)

    def blk(n):
        return nblk - 1 - n if reverse else n

    return [
        pl.BlockSpec((BLOCK, aw), lambda n: (blk(n), 0)),
        pl.BlockSpec((BLOCK, 2 * kvw), lambda n: (jnp.maximum(blk(n) - 1, 0), kv_col)),
        pl.BlockSpec((BLOCK, 2 * kvw), lambda n: (blk(n), kv_col)),
        _full_spec((nq, BLOCK, BLOCK)), _full_spec((nq, BLOCK, BLOCK)), _full_spec((nq, 1, 1)),
        _full_spec((1, hd)), _full_spec((1, hd)),
    ]


def _head_major(ref, n_heads, grp, hd, offset=0):
    return jnp.stack([
        jnp.concatenate([ref[:, pl.ds(offset + (grp * h + g) * hd, hd)].astype(F32) for g in range(grp)], axis=0)
        for h in range(n_heads)])


def _attn_inputs(nkv, grp, hd, kvw, q_ref, kvp_ref, kvc_ref):
    return (_head_major(q_ref, nkv, grp, hd), _head_major(kvp_ref, nkv, 1, hd), _head_major(kvc_ref, nkv, 1, hd),
            _head_major(kvp_ref, nkv, 1, hd, kvw), _head_major(kvc_ref, nkv, 1, hd, kvw))


def attn_fwd(p, bias_p, bias_c, sinks, qg, kg, *, aw, kvw, dep=None):
    t, hd = p.shape[0], qg.shape[-1]
    nq, nkv, nblk = aw // hd, kvw // hd, t // BLOCK
    grp = nq // nkv
    scale = hd ** -0.5

    def body(q_ref, kvp_ref, kvc_ref, bp_ref, bc_ref, s_ref, qg_ref, kg_ref, o_ref):
        prev_ok = pl.program_id(0) > 0
        out = _attn_math(*_attn_inputs(nkv, grp, hd, kvw, q_ref, kvp_ref, kvc_ref), bp_ref[...], bc_ref[...],
                         s_ref[...], qg_ref[...], kg_ref[...], prev_ok=prev_ok, scale=scale)
        for h in range(nkv):
            for g in range(grp):
                o_ref[:, pl.ds((grp * h + g) * hd, hd)] = out[h, g * BLOCK:(g + 1) * BLOCK].astype(BF16)

    return _pallas(
        body, [p, p, p, bias_p, bias_c, sinks, qg, kg], dep=dep, grid=(nblk,),
        in_specs=_attn_specs(p, aw, kvw, nq, hd, nblk, False),
        out_specs=pl.BlockSpec((BLOCK, aw), lambda n: (n, 0)),
        out_shape=jax.ShapeDtypeStruct((t, aw), BF16),
        sem=("parallel",), name="attn_fwd")


def attn_bwd(p, bias_p, bias_c, sinks, qg, kg, do, *, aw, kvw, dep=None):
    t, hd = p.shape[0], qg.shape[-1]
    nq, nkv, nblk = aw // hd, kvw // hd, t // BLOCK
    grp = nq // nkv
    scale = hd ** -0.5

    def body(q_ref, kvp_ref, kvc_ref, bp_ref, bc_ref, s_ref, qg_ref, kg_ref, do_ref,
             dqkv_ref, dbp_ref, dbc_ref, ds_ref, dqg_ref, dkg_ref, carry):
        i = pl.program_id(0)
        prev_ok = (nblk - 1 - i) > 0

        @pl.when(i == 0)
        def _():
            carry[...] = jnp.zeros_like(carry)
            dbp_ref[...] = jnp.zeros_like(dbp_ref)
            dbc_ref[...] = jnp.zeros_like(dbc_ref)
            ds_ref[...] = jnp.zeros_like(ds_ref)
            dqg_ref[...] = jnp.zeros_like(dqg_ref)
            dkg_ref[...] = jnp.zeros_like(dkg_ref)

        fn = functools.partial(_attn_math, prev_ok=prev_ok, scale=scale)
        _, vjp = jax.vjp(fn, *_attn_inputs(nkv, grp, hd, kvw, q_ref, kvp_ref, kvc_ref), bp_ref[...], bc_ref[...],
                         s_ref[...], qg_ref[...], kg_ref[...])
        dq, dkp, dkc, dvp, dvc, dbp, dbc, dsk, dqg, dkg = vjp(_head_major(do_ref, nkv, grp, hd))
        for h in range(nkv):
            for g in range(grp):
                dqkv_ref[:, pl.ds((grp * h + g) * hd, hd)] = dq[h, g * BLOCK:(g + 1) * BLOCK].astype(BF16)
            k_cols, v_cols = pl.ds(h * hd, hd), pl.ds(kvw + h * hd, hd)
            dqkv_ref[:, pl.ds(aw + h * hd, hd)] = (dkc[h] + carry[:, k_cols]).astype(BF16)
            dqkv_ref[:, pl.ds(aw + kvw + h * hd, hd)] = (dvc[h] + carry[:, v_cols]).astype(BF16)
            carry[:, k_cols] = dkp[h]
            carry[:, v_cols] = dvp[h]
        dbp_ref[...] += dbp
        dbc_ref[...] += dbc
        ds_ref[...] += dsk
        dqg_ref[...] += dqg
        dkg_ref[...] += dkg

    return _pallas(
        body, [p, p, p, bias_p, bias_c, sinks, qg, kg, do], dep=dep, grid=(nblk,),
        in_specs=_attn_specs(p, aw, kvw, nq, hd, nblk, True)
        + [pl.BlockSpec((BLOCK, aw), lambda n: (nblk - 1 - n, 0))],
        out_specs=[
            pl.BlockSpec((BLOCK, aw + 2 * kvw), lambda n: (nblk - 1 - n, 0)),
            _full_spec((nq, BLOCK, BLOCK)), _full_spec((nq, BLOCK, BLOCK)), _full_spec((nq, 1, 1)),
            _full_spec((1, hd)), _full_spec((1, hd)),
        ],
        out_shape=[
            jax.ShapeDtypeStruct((t, aw + 2 * kvw), BF16),
            jax.ShapeDtypeStruct((nq, BLOCK, BLOCK), F32),
            jax.ShapeDtypeStruct((nq, BLOCK, BLOCK), F32),
            jax.ShapeDtypeStruct((nq, 1, 1), F32),
            jax.ShapeDtypeStruct((1, hd), F32),
            jax.ShapeDtypeStruct((1, hd), F32),
        ],
        scratch=[pltpu.VMEM((BLOCK, 2 * kvw), F32)],
        sem=("arbitrary",), name="attn_bwd")


CONV_TILE = 256


def _conv_halo_specs(tb, ch, nblk):
    per = tb // CONV_HALO
    last = nblk * per - 1
    cur = pl.BlockSpec((tb, ch), lambda n: (n, 0))
    prev = pl.BlockSpec((CONV_HALO, ch), lambda n: (jnp.maximum(n * per - 1, 0), 0))
    nxt = pl.BlockSpec((CONV_HALO, ch), lambda n: (jnp.minimum((n + 1) * per, last), 0))
    return cur, prev, nxt


def _ln_silu(co, ln_g, ln_b):
    mu = jnp.mean(co, axis=-1, keepdims=True)
    cen = co - mu
    rstd = lax.rsqrt(jnp.mean(cen * cen, axis=-1, keepdims=True) + EPS)
    xhat = cen * rstd
    z = xhat * ln_g + ln_b
    return xhat, rstd, z


def _shifted_copies(src, shifted):
    rows = src.shape[0] - SUBLANES
    for r in range(1, SUBLANES):
        shifted[r, pl.ds(0, rows), :] = src[pl.ds(r, rows), :]


def _rows_from(src, shifted, start, n):
    r = start % SUBLANES
    if r == 0:
        return src[pl.ds(start, n), :]
    return shifted[r, pl.ds(start - r, n), :]


def conv_fwd(ca, cb, conv_w, conv_b, ln_g, ln_b, dep=None):
    t, ch = ca.shape
    tb = _tile(t, CONV_TILE, CONV_HALO)
    nblk = t // tb
    cur, prev, _ = _conv_halo_specs(tb, ch, nblk)
    lead = CONV_HALO - (CONV_WIDTH - 1)

    def body(ca_ref, cb_ref, cap_ref, cbp_ref, w_ref, b_ref, g_ref, bb_ref, s_ref, co_ref, ubuf, ushift):
        n = pl.program_id(0)
        halo = cap_ref[...] * _sigmoid(cbp_ref[...])
        ubuf[pl.ds(0, CONV_HALO), :] = jnp.where(n > 0, halo, 0.0)
        ubuf[pl.ds(CONV_HALO, tb), :] = ca_ref[...] * _sigmoid(cb_ref[...])
        _shifted_copies(ubuf, ushift)
        acc = jnp.broadcast_to(b_ref[...], (tb, ch))
        for k in range(CONV_WIDTH):
            acc = acc + w_ref[pl.ds(k, 1), :] * _rows_from(ubuf, ushift, lead + k, tb)
        co_ref[...] = acc
        _, _, z = _ln_silu(acc, g_ref[...], bb_ref[...])
        s_ref[...] = (z * _sigmoid(z)).astype(BF16)

    vec = _full_spec((1, ch))
    return _pallas(
        body, [ca, cb, ca, cb, conv_w, conv_b, ln_g, ln_b], dep=dep, grid=(nblk,),
        in_specs=[cur, cur, prev, prev, _full_spec(conv_w.shape), vec, vec, vec],
        out_specs=[cur, cur],
        out_shape=[jax.ShapeDtypeStruct((t, ch), BF16), jax.ShapeDtypeStruct((t, ch), F32)],
        scratch=[pltpu.VMEM((CONV_HALO + tb, ch), F32), pltpu.VMEM((SUBLANES, CONV_HALO + tb, ch), F32)],
        sem=("parallel",), name="conv_fwd")


def conv_bwd(ca, cb, co, ds, conv_w, ln_g, ln_b, dep=None):
    t, ch = ca.shape
    tb = _tile(t, CONV_TILE, CONV_HALO)
    nblk = t // tb
    cur, prev, nxt = _conv_halo_specs(tb, ch, nblk)
    lead = CONV_HALO - (CONV_WIDTH - 1)
    ext = tb + CONV_HALO

    def body(ca_ref, cb_ref, cap_ref, cbp_ref, co_ref, con_ref, ds_ref, dsn_ref, w_ref, g_ref, bb_ref,
             dca_ref, dcb_ref, dw_ref, dvec_ref, ubuf, dbuf, ushift, dshift):
        n = pl.program_id(0)
        is_last = n == nblk - 1
        sig_b = _sigmoid(cb_ref[...])
        cav = ca_ref[...].astype(F32)
        ubuf[pl.ds(0, CONV_HALO), :] = jnp.where(n > 0, cap_ref[...] * _sigmoid(cbp_ref[...]), 0.0)
        ubuf[pl.ds(CONV_HALO, tb), :] = cav * sig_b
        _shifted_copies(ubuf, ushift)
        co = jnp.concatenate([co_ref[...], con_ref[...]], axis=0)
        xhat, rstd, z = _ln_silu(co, g_ref[...], bb_ref[...])
        dsv = jnp.concatenate([ds_ref[...].astype(F32), jnp.where(is_last, 0.0, dsn_ref[...].astype(F32))], axis=0)
        sg = _sigmoid(z)
        dz = dsv * (sg * (1.0 + z * (1.0 - sg)))
        dxh = dz * g_ref[...]
        dco = rstd * (dxh - jnp.mean(dxh, axis=-1, keepdims=True)
                      - xhat * jnp.mean(dxh * xhat, axis=-1, keepdims=True))
        dbuf[...] = dco
        _shifted_copies(dbuf, dshift)

        @pl.when(n == 0)
        def _():
            dw_ref[...] = jnp.zeros_like(dw_ref)
            dvec_ref[...] = jnp.zeros_like(dvec_ref)

        dco_cur = dco[:tb]
        dvec_ref[pl.ds(0, 1), :] += jnp.sum(dco_cur, axis=0, keepdims=True)
        dvec_ref[pl.ds(1, 1), :] += jnp.sum(dz[:tb] * xhat[:tb], axis=0, keepdims=True)
        dvec_ref[pl.ds(2, 1), :] += jnp.sum(dz[:tb], axis=0, keepdims=True)
        du = jnp.zeros((tb, ch), F32)
        for k in range(CONV_WIDTH):
            du = du + w_ref[pl.ds(k, 1), :] * _rows_from(dbuf, dshift, CONV_WIDTH - 1 - k, tb)
            dw_ref[pl.ds(k, 1), :] += jnp.sum(dco_cur * _rows_from(ubuf, ushift, lead + k, tb), axis=0,
                                              keepdims=True)
        dca_ref[...] = (du * sig_b).astype(BF16)
        dcb_ref[...] = (du * cav * sig_b * (1.0 - sig_b)).astype(BF16)

    vec = _full_spec((1, ch))
    return _pallas(
        body, [ca, cb, ca, cb, co, co, ds, ds, conv_w, ln_g, ln_b], dep=dep, grid=(nblk,),
        in_specs=[cur, cur, prev, prev, cur, nxt, cur, nxt, _full_spec(conv_w.shape), vec, vec],
        out_specs=[cur, cur, _full_spec(conv_w.shape), _full_spec((SUBLANES, ch))],
        out_shape=[jax.ShapeDtypeStruct((t, ch), BF16), jax.ShapeDtypeStruct((t, ch), BF16),
                   jax.ShapeDtypeStruct(conv_w.shape, F32), jax.ShapeDtypeStruct((SUBLANES, ch), F32)],
        scratch=[pltpu.VMEM((CONV_HALO + tb, ch), F32), pltpu.VMEM((ext, ch), F32),
                 pltpu.VMEM((SUBLANES, CONV_HALO + tb, ch), F32), pltpu.VMEM((SUBLANES, ext, ch), F32)],
        sem=("arbitrary",), name="conv_bwd")


def ada_fwd(c_t, w_ada, dep=None):
    d, nc = w_ada.shape
    nex = c_t.shape[1]
    tn = _tile(nc, 512)

    def body(ct_ref, w_ref, o_ref):
        w = w_ref[...]
        ct = ct_ref[...]
        cact = ct * _sigmoid(ct)
        rows = [jnp.sum(w * cact[:, b:b + 1], axis=0, keepdims=True) for b in range(nex)]
        o_ref[...] = jnp.concatenate(rows, axis=0)

    return _pallas(
        body, [c_t, w_ada], dep=dep, grid=(nc // tn,),
        in_specs=[_full_spec(c_t.shape), pl.BlockSpec((d, tn), lambda j: (0, j))],
        out_specs=pl.BlockSpec((nex, tn), lambda j: (0, j)),
        out_shape=jax.ShapeDtypeStruct((nex, nc), F32),
        sem=("parallel",), name="ada_fwd")


def _adamw_math(w, g, m, v):
    m = ADAM_B1 * m + (1.0 - ADAM_B1) * g
    v = ADAM_B2 * v + (1.0 - ADAM_B2) * (g * g)
    m_hat = m / (1.0 - ADAM_B1 ** ADAM_STEP)
    v_hat = v / (1.0 - ADAM_B2 ** ADAM_STEP)
    delta = -ADAM_LR * (m_hat / (jnp.sqrt(v_hat) + ADAM_EPS) + ADAM_WD * w)
    return delta, m, v


def adamw(w, g, m, v, name, copy_grad=False, dep=None):
    r, n = w.shape
    tr, tn = _ew_tiles(r, n, elems=256 * 1024)
    n_out = 4 if copy_grad else 3

    def body(w_ref, g_ref, m_ref, v_ref, *outs):
        g = g_ref[...]
        if copy_grad:
            outs[0][...] = g
        outs[-3][...], outs[-2][...], outs[-1][...] = _adamw_math(w_ref[...], g, m_ref[...], v_ref[...])

    blk = pl.BlockSpec((tr, tn), lambda i, j: (i, j))
    return _pallas(
        body, [w, g, m, v], dep=dep, grid=(r // tr, n // tn),
        in_specs=[blk] * 4, out_specs=[blk] * n_out,
        out_shape=[jax.ShapeDtypeStruct((r, n), F32)] * n_out,
        sem=("parallel", "parallel"), name=name)


def ada_grad_adamw(c_t, dmod_cols, w, m, v, dep=None):
    d, nc = w.shape
    nex = c_t.shape[1]
    tr, tn = _ew_tiles(d, nc, elems=256 * 1024)

    def body(ct_ref, dm_ref, w_ref, m_ref, v_ref, g_ref, d_ref, nm_ref, nv_ref):
        ct = ct_ref[...]
        cact = ct * _sigmoid(ct)
        dm = dm_ref[...]
        g = cact[:, 0:1] * dm[0:1, :]
        for b in range(1, nex):
            g = g + cact[:, b:b + 1] * dm[b:b + 1, :]
        g_ref[...] = g
        d_ref[...], nm_ref[...], nv_ref[...] = _adamw_math(w_ref[...], g, m_ref[...], v_ref[...])

    blk = pl.BlockSpec((tr, tn), lambda i, j: (i, j))
    return _pallas(
        body, [c_t, dmod_cols, w, m, v], dep=dep, grid=(d // tr, nc // tn),
        in_specs=[pl.BlockSpec((tr, nex), lambda i, j: (i, 0)), pl.BlockSpec((nex, tn), lambda i, j: (0, j)),
                  blk, blk, blk],
        out_specs=[blk] * 4,
        out_shape=[jax.ShapeDtypeStruct((d, nc), F32)] * 4,
        sem=("parallel", "parallel"), name="ada_grad_adamw")


def _row_pack(parts):
    cols, offs, off = [], [], 0
    for p in parts:
        n = p.shape[1]
        width = -(-n // LANES) * LANES
        cols.append(jnp.pad(p, ((0, 0), (0, width - n))) if width != n else p)
        offs.append(off)
        off += width
    return jnp.concatenate(cols, axis=1), offs


def small_sum_adamw(gathered, offs, ws, ms, vs, extra_widths, dep=None):
    ndev = gathered.shape[0]
    npar = len(ws)

    def body(ga_ref, *refs):
        w_refs, m_refs, v_refs = refs[:npar], refs[npar:2 * npar], refs[2 * npar:3 * npar]
        outs = refs[3 * npar:]
        tot = ga_ref[0]
        for s in range(1, ndev):
            tot = tot + ga_ref[s]
        for i in range(npar):
            n = ws[i].shape[1]
            g = tot[:, offs[i]:offs[i] + n]
            outs[4 * i][...] = g
            outs[4 * i + 1][...], outs[4 * i + 2][...], outs[4 * i + 3][...] = _adamw_math(
                w_refs[i][...], g, m_refs[i][...], v_refs[i][...])
        for e, n in enumerate(extra_widths):
            off = offs[npar + e]
            outs[4 * npar + e][...] = tot[:, off:off + n]

    shapes = [jax.ShapeDtypeStruct(w.shape, F32) for w in ws for _ in range(4)]
    shapes += [jax.ShapeDtypeStruct((1, n), F32) for n in extra_widths]
    return _pallas(
        body, [gathered, *ws, *ms, *vs], dep=dep, in_specs=[_VMEM] * (1 + 3 * npar), out_specs=[_VMEM] * len(shapes),
        out_shape=shapes, name="small_sum_adamw")


def _position():
    return lax.axis_index("x"), lax.axis_index("y"), lax.axis_index("c")


def _other_chips(x, y):
    return [(1 - x, y), (x, 1 - y), (1 - x, 1 - y)]


def allgather_small(block, name, dep=None):
    def body(x_ref, out_ref, send_sems, recv_sems, local_sem):
        x, y, c = _position()
        me, sibling = (x, y, c), (x, y, 1 - c)
        chips = _other_chips(x, y)

        def slot(px, py, pc):
            return out_ref.at[4 * px + 2 * py + pc]

        def copy(k, block_of, to, src=None):
            return pltpu.make_async_remote_copy(
                src_ref=slot(*block_of) if src is None else src, dst_ref=slot(*block_of),
                send_sem=send_sems.at[k], recv_sem=recv_sems.at[k], device_id=to, device_id_type=MESH)

        mine = pltpu.make_async_copy(x_ref, slot(*me), local_sem)
        mine.start()
        first = [copy(0, me, sibling, src=x_ref)]
        first += [copy(1 + j, me, (*chip, c), src=x_ref) for j, chip in enumerate(chips)]
        for cp in first:
            cp.start()
        passed = [copy(4 + j, (*chip, c), sibling) for j, chip in enumerate(chips)]
        for j, chip in enumerate(chips):
            copy(1 + j, (*chip, c), me).wait_recv()
            passed[j].start()
        copy(0, sibling, me).wait_recv()
        for j, chip in enumerate(chips):
            copy(4 + j, (*chip, 1 - c), me).wait_recv()
        for cp in first + passed:
            cp.wait_send()
        mine.wait()

    return _pallas(
        body, [block], dep=dep,
        out_shape=jax.ShapeDtypeStruct((N_DEV, *block.shape), block.dtype),
        in_specs=[_VMEM], out_specs=_VMEM,
        scratch=[pltpu.SemaphoreType.DMA((7,)), pltpu.SemaphoreType.DMA((7,)), pltpu.SemaphoreType.DMA],
        name=name)


class Started(NamedTuple):
    send_sems: Any
    recv_sems: Any
    bufs: list


def exchange_start(name, bufs, n_copies, plan, dep=None):
    nb = len(bufs)

    def body(*refs):
        for cp in plan(refs[:nb], refs[nb], refs[nb + 1]):
            cp.start()

    outs = _pallas(
        body, [pltpu.with_memory_space_constraint(b, pltpu.HBM) for b in bufs], dep=dep, name=name,
        out_shape=(pltpu.SemaphoreType.DMA((n_copies,)), pltpu.SemaphoreType.DMA((n_copies,)),
                   *[pltpu.HBM(b.shape, b.dtype) for b in bufs]),
        in_specs=[_HBM] * nb,
        out_specs=(_SEM, _SEM, *[_HBM] * nb),
        input_output_aliases={i: 2 + i for i in range(nb)},
        compiler_params=pltpu.CompilerParams(has_side_effects=_EFFECT))
    return Started(outs[0], outs[1], list(outs[2:2 + nb]))


def exchange_wait(name, started, plan, bufs=None, dep=None):
    if bufs is not None:
        started = started._replace(bufs=list(bufs))
    nb = len(started.bufs)

    def body(*refs):
        for cp in plan(refs[:nb], refs[nb], refs[nb + 1]):
            cp.wait_send()
            cp.wait_recv()

    outs = _pallas(
        body, [*started.bufs, started.send_sems, started.recv_sems], dep=dep, name=name,
        out_shape=tuple(pltpu.HBM(b.shape, b.dtype) for b in started.bufs),
        in_specs=[_HBM] * nb + [_SEM, _SEM],
        out_specs=tuple([_HBM] * nb),
        input_output_aliases={i: i for i in range(nb)},
        compiler_params=pltpu.CompilerParams(has_side_effects=_EFFECT))
    return list(outs)


def _remote(src, dst, send_sems, recv_sems, i, to):
    return pltpu.make_async_remote_copy(src_ref=src, dst_ref=dst, send_sem=send_sems.at[i], recv_sem=recv_sems.at[i],
                                        device_id=to, device_id_type=MESH)


def _half_rows(buf_rows, chip_idx, pc):
    half = buf_rows // (2 * N_CHIPS)
    return pl.ds((2 * chip_idx + pc) * half, half)


ALL_PEERS = (0, 1, 2)


def plan_gather_ici(refs, send_sems, recv_sems, peers=ALL_PEERS):
    x, y, c = _position()
    chips = _other_chips(x, y)
    copies = []
    for k, ref in enumerate(refs):
        rows = ref.at[_half_rows(ref.shape[0], 2 * x + y, c), :]
        for i, j in enumerate(peers):
            copies.append(_remote(rows, rows, send_sems, recv_sems, len(peers) * k + i, (*chips[j], c)))
    return copies


def plan_gather_relay(refs, send_sems, recv_sems):
    x, y, c = _position()
    copies = []
    for k, ref in enumerate(refs):
        quarter = ref.shape[0] // (4 * N_CHIPS)
        for i, (src_chip, to) in enumerate((((1 - x, y), (x, 1 - y, c)), ((x, 1 - y), (1 - x, y, c)))):
            start = (2 * (2 * src_chip[0] + src_chip[1]) + c) * 2 * quarter + i * quarter
            rows = ref.at[pl.ds(start, quarter), :]
            copies.append(_remote(rows, rows, send_sems, recv_sems, 2 * k + i, to))
    return copies


def plan_gather_d2d(refs, send_sems, recv_sems, peers=ALL_PEERS):
    x, y, c = _position()
    chips = _other_chips(x, y)
    copies = []
    for k, ref in enumerate(refs):
        for i, j in enumerate(peers):
            px, py = chips[j]
            rows = ref.at[_half_rows(ref.shape[0], 2 * px + py, c), :]
            copies.append(_remote(rows, rows, send_sems, recv_sems, len(peers) * k + i, (x, y, 1 - c)))
    return copies


def plan_pair_exchange(refs, send_sems, recv_sems):
    x, y, c = _position()
    nw = len(refs) // 2
    copies = []
    for k in range(nw):
        for chip in range(N_CHIPS):
            copies.append(_remote(refs[k].at[chip, 1 - c], refs[nw + k].at[chip], send_sems, recv_sems,
                                  N_CHIPS * k + chip, (x, y, 1 - c)))
    return copies


def plan_chip_exchange(refs, send_sems, recv_sems):
    x, y, c = _position()
    nw = len(refs) // 2
    copies = []
    for k in range(nw):
        for j, (px, py) in enumerate(_other_chips(x, y)):
            copies.append(_remote(refs[k].at[2 * px + py], refs[nw + k].at[2 * x + y], send_sems, recv_sems,
                                  3 * k + j, (px, py, c)))
    return copies


def plan_pair_share(refs, send_sems, recv_sems):
    x, y, c = _position()
    return [_remote(ref.at[c], ref.at[c], send_sems, recv_sems, k, (x, y, 1 - c)) for k, ref in enumerate(refs)]


def cast_into_slot(src, slot, n_slots, name, dep=None):
    r, n = src.shape
    tr, tn = _ew_tiles(r, n, BF16_SUBLANES)

    def body(slot_ref, s_ref, o_ref):
        o_ref[...] = s_ref[...].astype(BF16)

    return _pallas(
        body, [slot, src], dep=dep, n_prefetch=1, grid=(r // tr, n // tn),
        in_specs=[pl.BlockSpec((tr, tn), lambda i, j, sl: (i, j))],
        out_specs=pl.BlockSpec((None, tr, tn), lambda i, j, sl: (sl[0], i, j)),
        out_shape=jax.ShapeDtypeStruct((n_slots, r, n), BF16),
        sem=("parallel", "parallel"), name=name)


def pair_sum(g, r, core, name, dep=None):
    nchip, _, h, n = g.shape
    th, tn = _ew_tiles(h, n, BF16_SUBLANES)

    def body(core_ref, g_ref, r_ref, o_ref):
        o_ref[...] = (g_ref[...].astype(F32) + r_ref[...].astype(F32)).astype(BF16)

    return _pallas(
        body, [core, g, r], dep=dep, n_prefetch=1, grid=(nchip, h // th, n // tn),
        in_specs=[pl.BlockSpec((None, None, th, tn), lambda a, i, j, cr: (a, cr[0], i, j)),
                  pl.BlockSpec((None, th, tn), lambda a, i, j, cr: (a, i, j))],
        out_specs=pl.BlockSpec((None, th, tn), lambda a, i, j, cr: (a, i, j)),
        out_shape=jax.ShapeDtypeStruct((nchip, h, n), BF16),
        sem=("parallel", "parallel", "parallel"), name=name)


def chip_sum(own, got, where, name, dep=None):
    nchip, h, n = got.shape
    th, tn = _ew_tiles(h, n, BF16_SUBLANES, elems=256 * 1024)

    def body(where_ref, own_ref, *rest):
        got_refs, o_ref = rest[:nchip], rest[nchip]
        chip = where_ref[0]
        acc = None
        for s in range(nchip):
            term = jnp.where(chip == s, own_ref[...], got_refs[s][...]).astype(F32)
            acc = term if acc is None else acc + term
        o_ref[...] = acc

    def got_spec(s):
        return pl.BlockSpec((None, th, tn), lambda i, j, wr: (jnp.where(wr[0] == s, (s + 1) % nchip, s), i, j))

    return _pallas(
        body, [where, own, *[got] * nchip], dep=dep, n_prefetch=1, grid=(h // th, n // tn),
        in_specs=[pl.BlockSpec((None, th, tn), lambda i, j, wr: (wr[0], i, j))]
        + [got_spec(s) for s in range(nchip)],
        out_specs=pl.BlockSpec((None, th, tn), lambda i, j, wr: (wr[1], i, j)),
        out_shape=jax.ShapeDtypeStruct((2, h, n), F32),
        sem=("parallel", "parallel"), name=name)


def kernel(x, c, w_ada, b_ada, norm_mix_g, w_in, q_norm_g, k_norm_g, attn_sinks, rel_bias, w_attn_out, conv_w, conv_b, conv_ln_g, conv_ln_b, w_conv_out, w_mix_out, norm_ffn_g, w_ffn_in, w_ffn_out, loss_target, m_w_ada, m_b_ada, m_norm_mix_g, m_w_in, m_q_norm_g, m_k_norm_g, m_attn_sinks, m_rel_bias, m_w_attn_out, m_conv_w, m_conv_b, m_conv_ln_g, m_conv_ln_b, m_w_conv_out, m_w_mix_out, m_norm_ffn_g, m_w_ffn_in, m_w_ffn_out, v_w_ada, v_b_ada, v_norm_mix_g, v_w_in, v_q_norm_g, v_k_norm_g, v_attn_sinks, v_rel_bias, v_w_attn_out, v_conv_w, v_conv_b, v_conv_ln_g, v_conv_ln_b, v_w_conv_out, v_w_mix_out, v_norm_ffn_g, v_w_ffn_in, v_w_ffn_out):
    run = InOrder()
    xi, yi, ci = _position()
    chip = 2 * xi + yi
    me = 2 * chip + ci
    chip_arr = chip.astype(jnp.int32).reshape(1)
    core_arr = ci.astype(jnp.int32).reshape(1)
    where_arr = jnp.stack([chip, ci]).astype(jnp.int32)

    xe, tgt = x[0], loss_target[0]
    t, d = xe.shape
    hd = q_norm_g.shape[-1]
    nq = attn_sinks.shape[-1]
    aw = nq * hd
    ch = conv_b.shape[-1]
    in_width = N_CHIPS * w_in.shape[-1]
    kvw = (in_width - aw - 2 * ch - 2 * d) // 2
    nkv = kvw // hd
    dff = N_CHIPS * w_ffn_out.shape[1]
    off_k, off_v, off_ca = aw, aw + kvw, aw + 2 * kvw
    off_cb, off_ga, off_gc = off_ca + ch, off_ca + 2 * ch, off_ca + 2 * ch + d
    nc_ada = w_ada.shape[-1]
    ch_loc = conv_w.shape[-1]
    nj_ffn = w_ffn_in.shape[-1]
    perm_ffn = ffn_perm(N_CHIPS)

    big = {"w_in": w_in[0], "w_attn_out": w_attn_out[0], "w_conv_out": w_conv_out[0], "w_mix_out": w_mix_out[0],
           "w_ffn_in": w_ffn_in[0], "w_ffn_out": w_ffn_out[0]}
    moments = {"w_in": (m_w_in, v_w_in), "w_attn_out": (m_w_attn_out, v_w_attn_out),
               "w_conv_out": (m_w_conv_out, v_w_conv_out), "w_mix_out": (m_w_mix_out, v_w_mix_out),
               "w_ffn_in": (m_w_ffn_in, v_w_ffn_in), "w_ffn_out": (m_w_ffn_out, v_w_ffn_out)}
    gather_groups = {"in": ["w_in"], "branch_out": ["w_attn_out", "w_conv_out"], "mix_out": ["w_mix_out"],
                     "ffn_in": ["w_ffn_in"], "ffn_out": ["w_ffn_out"]}
    grads, deltas, new_m, new_v = {}, {}, {}, {}

    def gather_cast(gname):
        bufs = []
        for n in gather_groups[gname]:
            r, ncol = big[n].shape
            bufs.append(run(cast_into_slot, big[n], chip_arr, N_CHIPS, "cast_" + n).reshape(N_CHIPS * r, ncol))
        return bufs

    def gather_ici_start(gname, bufs):
        return run(exchange_start, "gather_ici_start_" + gname, bufs, 3 * len(bufs), plan_gather_ici)

    def gather_pass_on(gname, ici):
        landed = run(exchange_wait, "gather_ici_wait_" + gname, ici, plan_gather_ici)
        return run(exchange_start, "gather_d2d_start_" + gname, landed, 3 * len(landed), plan_gather_d2d)

    def gathered(gname, d2d):
        outs = run(exchange_wait, "gather_d2d_wait_" + gname, d2d, plan_gather_d2d)
        return [o.reshape(N_CHIPS, *big[n].shape) for o, n in zip(outs, gather_groups[gname])]

    def rs_pair_start(gname, names, partials):
        blocks = [g.reshape(N_CHIPS, 2, big[n].shape[0] // 2, big[n].shape[1]) for n, g in zip(names, partials)]
        land = [lax.empty((N_CHIPS,) + b.shape[2:], BF16) for b in blocks]
        return run(exchange_start, "pair_exchange_start_" + gname, blocks + land, N_CHIPS * len(blocks),
                   plan_pair_exchange)

    def rs_chip_start(gname, names, pair):
        nw = len(names)
        outs = run(exchange_wait, "pair_exchange_wait_" + gname, pair, plan_pair_exchange)
        sums = [run(pair_sum, g, r, core_arr, "pair_sum_" + n) for n, g, r in zip(names, outs[:nw], outs[nw:])]
        land = [lax.empty(s.shape, BF16) for s in sums]
        return run(exchange_start, "chip_exchange_start_" + gname, sums + land, 3 * nw, plan_chip_exchange)

    def rs_share_start(gname, names, chipx):
        nw = len(names)
        outs = run(exchange_wait, "chip_exchange_wait_" + gname, chipx, plan_chip_exchange)
        halves = [run(chip_sum, s, r, where_arr, "chip_sum_" + n) for n, s, r in zip(names, outs[:nw], outs[nw:])]
        return run(exchange_start, "pair_share_start_" + gname, halves, nw, plan_pair_share)

    def rs_finish(gname, names, share):
        fulls = run(exchange_wait, "pair_share_wait_" + gname, share, plan_pair_share)
        for n, g2 in zip(names, fulls):
            g, dl, nm, nv = run(adamw, big[n], g2.reshape(big[n].shape), moments[n][0][0], moments[n][1][0],
                                "adamw_" + n, copy_grad=True)
            grads[n], deltas[n], new_m[n], new_v[n] = g[None], dl[None], nm[None], nv[None]

    near, far = (0, 1), (2,)
    plan_ici_near = functools.partial(plan_gather_ici, peers=near)
    plan_ici_far, n_far = plan_gather_relay, 2
    plan_d2d_near = functools.partial(plan_gather_d2d, peers=near)
    plan_d2d_far = functools.partial(plan_gather_d2d, peers=far)
    bufs_in = gather_cast("in")
    row1, offs1 = _row_pack([c, conv_w[0].reshape(1, CONV_WIDTH * ch_loc)])
    got1 = run(allgather_small, row1, "allgather_cond")
    ici_near = run(exchange_start, "gather_ici_start_in_near", bufs_in, len(near), plan_ici_near)
    c_all = got1[:, 0, :d]
    conv_w_full = got1[0::2, 0, offs1[1]:offs1[1] + CONV_WIDTH * ch_loc].reshape(N_CHIPS, CONV_WIDTH, ch_loc)
    conv_w_full = jnp.transpose(conv_w_full, (1, 0, 2)).reshape(CONV_WIDTH, ch)
    conv_w_pad = jnp.pad(conv_w_full, ((0, 1), (0, 0)))
    c_t = jnp.transpose(c_all)
    mod_cols = run(ada_fwd, c_t, w_ada[0])
    rest_bufs = {gname: gather_cast(gname) for gname in gather_groups if gname != "in"}
    bucket = _t5_bucket_table()
    bucket_p, bucket_c = jnp.asarray(bucket[:, :BLOCK]), jnp.asarray(bucket[:, BLOCK:])
    bias_p, bias_c = run(bias_table, rel_bias, bucket_p, bucket_c)
    got2 = run(allgather_small, mod_cols, "allgather_mod")
    mod_all = got2.reshape(N_CHIPS, 2, N_DEV, nc_ada)[:, 0]
    mod = lax.dynamic_slice_in_dim(mod_all, me, 1, axis=1).reshape(1, N_CHIPS * nc_ada) + b_ada
    mod = jnp.pad(mod.reshape(N_MOD, d), ((0, SUBLANES - N_MOD), (0, 0)))

    landed = run(exchange_wait, "gather_ici_wait_in_near", ici_near, plan_ici_near)
    ici_far = run(exchange_start, "gather_ici_start_in_far", landed, n_far, plan_ici_far)
    d2d_near = run(exchange_start, "gather_d2d_start_in_near", ici_far.bufs, len(near), plan_d2d_near)
    h = run(pre_mix_fwd, xe, mod, norm_mix_g)
    ici = {gname: gather_ici_start(gname, rest_bufs[gname]) for gname in ("branch_out", "mix_out")}
    ici_near_ffn = run(exchange_start, "gather_ici_start_ffn_in_near", rest_bufs["ffn_in"], len(near), plan_ici_near)
    landed = run(exchange_wait, "gather_d2d_wait_in_near", d2d_near, plan_d2d_near)
    landed = run(exchange_wait, "gather_ici_wait_in_far", ici_far, plan_ici_far, bufs=landed)
    d2d_far = run(exchange_start, "gather_d2d_start_in_far", landed, len(far), plan_d2d_far)
    landed = run(exchange_wait, "gather_d2d_wait_in_far", d2d_far, plan_d2d_far)
    wg_in = landed[0].reshape(N_CHIPS, *big["w_in"].shape)
    p = run(mm_nn, h, wg_in, tn=wg_in.shape[2], tk=d, out_dtype=BF16, name="mm_in")
    d2d_branch = gather_pass_on("branch_out", ici["branch_out"])

    sinks3 = attn_sinks.reshape(nq, 1, 1)
    attn_o = run(attn_fwd, p, bias_p, bias_c, sinks3, q_norm_g, k_norm_g, aw=aw, kvw=kvw)
    ca, cb = p[:, off_ca:off_cb], p[:, off_cb:off_ga]
    s_conv, co_conv = run(conv_fwd, ca, cb, conv_w_pad, conv_b, conv_ln_g, conv_ln_b)
    wg_attn_out, wg_conv_out = gathered("branch_out", d2d_branch)
    y_attn, y_conv, merged = run(branch_out_merge, attn_o, s_conv, wg_attn_out, wg_conv_out, p, off_ga, off_gc)
    landed = run(exchange_wait, "gather_ici_wait_ffn_in_near", ici_near_ffn, plan_ici_near)
    ici_far_ffn = run(exchange_start, "gather_ici_start_ffn_in_far", landed, n_far, plan_ici_far)
    ici["ffn_out"] = gather_ici_start("ffn_out", rest_bufs["ffn_out"])
    d2d_near_ffn = run(exchange_start, "gather_d2d_start_ffn_in_near", ici_far_ffn.bufs, len(near), plan_d2d_near)
    d2d_mix = gather_pass_on("mix_out", ici["mix_out"])
    (wg_mix_out,) = gathered("mix_out", d2d_mix)
    wg_mix_out = wg_mix_out.reshape(1, d, d)
    o_m = run(mm_nn, merged, wg_mix_out, tn=_tile(d, 512), tk=d, out_dtype=BF16, name="mm_mix_out")
    landed = run(exchange_wait, "gather_d2d_wait_ffn_in_near", d2d_near_ffn, plan_d2d_near)
    landed = run(exchange_wait, "gather_ici_wait_ffn_in_far", ici_far_ffn, plan_ici_far, bufs=landed)
    d2d_far_ffn = run(exchange_start, "gather_d2d_start_ffn_in_far", landed, len(far), plan_d2d_far)
    x1, h2 = run(pre_ffn_fwd, xe, o_m, mod, norm_ffn_g)
    landed = run(exchange_wait, "gather_d2d_wait_ffn_in_far", d2d_far_ffn, plan_d2d_far)
    wg_ffn_in = landed[0].reshape(N_CHIPS, *big["w_ffn_in"].shape)
    f = run(mm_nn, h2, wg_ffn_in, tn=_tile(nj_ffn, 1408), tk=d, out_dtype=BF16, name="mm_ffn_in", perm=perm_ffn)
    d2d_ffn_out = gather_pass_on("ffn_out", ici["ffn_out"])
    act = run(swiglu_fwd, f, nj_ffn)
    (wg_ffn_out,) = gathered("ffn_out", d2d_ffn_out)
    wg_ffn_out = wg_ffn_out.reshape(1, dff, d)
    o_f = run(mm_nn, act, wg_ffn_out, tn=_tile(d, 512), tk=_tile(dff, 2816), out_dtype=BF16, name="mm_ffn_out")
    loss11, dy, dof, acc_l = run(loss_head, x1, o_f, tgt, mod)

    gw_ffn_out = run(mm_tn, act, dof, 1, tk=_tile(dff, 512), tn=d, name="mm_ffn_out_dw")
    px_ffn_out = rs_pair_start("ffn_out", ["w_ffn_out"], [gw_ffn_out])
    dact = run(mm_nt, dof, wg_ffn_out, tko=_tile(dff, 512), tn=d, out_dtype=BF16, name="mm_ffn_out_dx")
    cx_ffn_out = rs_chip_start("ffn_out", ["w_ffn_out"], px_ffn_out)
    df = run(swiglu_bwd, f, dact, nj_ffn)
    gw_ffn_in = run(mm_tn, h2, df, N_CHIPS, tk=d, tn=_tile(nj_ffn, 1408), name="mm_ffn_in_dw",
                    perm=perm_ffn)
    px_ffn_in = rs_pair_start("ffn_in", ["w_ffn_in"], [gw_ffn_in])
    dh2 = run(mm_nt, df, wg_ffn_in, tko=_tile(d, 512), tn=nj_ffn, out_dtype=BF16, name="mm_ffn_in_dx", perm=perm_ffn)
    sh_ffn_out = rs_share_start("ffn_out", ["w_ffn_out"], cx_ffn_out)
    cx_ffn_in = rs_chip_start("ffn_in", ["w_ffn_in"], px_ffn_in)
    dx1, dom, acc_f = run(pre_ffn_bwd, x1, dh2, dy, o_m, mod, norm_ffn_g)
    gw_mix_out = run(mm_tn, merged, dom, 1, tk=d, tn=_tile(d, 1024), name="mm_mix_out_dw")
    px_mix = rs_pair_start("mix_out", ["w_mix_out"], [gw_mix_out])
    dy_attn, dy_conv, dga, dgc = run(mix_out_dx_merge_bwd, dom, wg_mix_out.reshape(d, d), p, y_attn, y_conv,
                                     off_ga, off_gc)
    rs_finish("ffn_out", ["w_ffn_out"], sh_ffn_out)
    cx_mix = rs_chip_start("mix_out", ["w_mix_out"], px_mix)
    gw_attn_out = run(mm_tn, attn_o, dy_attn, N_CHIPS, tk=aw, tn=_tile(wg_attn_out.shape[2], 512),
                      name="mm_attn_out_dw")
    gw_conv_out = run(mm_tn, s_conv, dy_conv, N_CHIPS, tk=ch, tn=_tile(wg_conv_out.shape[2], 512),
                      name="mm_conv_out_dw")
    ac_names = ["w_attn_out", "w_conv_out"]
    px_ac = rs_pair_start("attn_conv_out", ac_names, [gw_attn_out, gw_conv_out])
    dattn_o = run(mm_nt, dy_attn, wg_attn_out, tko=_tile(aw, 1024), tn=_tile(wg_attn_out.shape[2], 512),
                  out_dtype=BF16, name="mm_attn_out_dx")
    ds_conv = run(mm_nt, dy_conv, wg_conv_out, tko=_tile(ch, 1024), tn=_tile(wg_conv_out.shape[2], 512),
                  out_dtype=BF16, name="mm_conv_out_dx")
    cx_ac = rs_chip_start("attn_conv_out", ac_names, px_ac)
    dca, dcb, dconv_w, dconv_vec = run(conv_bwd, ca, cb, co_conv, ds_conv, conv_w_pad, conv_ln_g, conv_ln_b)
    sh_ffn_in = rs_share_start("ffn_in", ["w_ffn_in"], cx_ffn_in)
    dqkv, dbp, dbc, dsinks, dqg, dkg = run(attn_bwd, p, bias_p, bias_c, sinks3, q_norm_g, k_norm_g, dattn_o,
                                           aw=aw, kvw=kvw)
    sh_mix = rs_share_start("mix_out", ["w_mix_out"], cx_mix)
    sh_ac = rs_share_start("attn_conv_out", ac_names, cx_ac)
    drel = run(bias_table_bwd, dbp, dbc, bucket_p, bucket_c).reshape(NUM_BUCKETS, nq)
    dp = jnp.concatenate([dqkv, dca, dcb, dga, dgc], axis=1)
    gw_in = run(mm_tn, h, dp, N_CHIPS, tk=d, tn=wg_in.shape[2], name="mm_in_dw")
    px_in = rs_pair_start("in", ["w_in"], [gw_in])
    dh = run(mm_nt, dp, wg_in, tko=_tile(d, 1024), tn=wg_in.shape[2], out_dtype=BF16, name="mm_in_dx")
    grad_x, acc_m = run(pre_mix_bwd, xe, dh, dx1, mod, norm_mix_g)

    dmod = jnp.concatenate([acc_m[0:1], acc_m[1:2], acc_f[3:4], acc_f[0:1], acc_f[1:2], acc_l[0:1]], axis=1)
    small_names = ["b_ada", "norm_mix_g", "q_norm_g", "k_norm_g", "attn_sinks", "rel_bias", "conv_b", "conv_ln_g",
                   "conv_ln_b", "norm_ffn_g"]
    small_w = [b_ada, norm_mix_g, q_norm_g, k_norm_g, attn_sinks, rel_bias, conv_b, conv_ln_g, conv_ln_b, norm_ffn_g]
    small_m = [m_b_ada, m_norm_mix_g, m_q_norm_g, m_k_norm_g, m_attn_sinks, m_rel_bias, m_conv_b, m_conv_ln_g,
               m_conv_ln_b, m_norm_ffn_g]
    small_v = [v_b_ada, v_norm_mix_g, v_q_norm_g, v_k_norm_g, v_attn_sinks, v_rel_bias, v_conv_b, v_conv_ln_g,
               v_conv_ln_b, v_norm_ffn_g]
    small_g = [dmod, acc_m[2:3], dqg, dkg, dsinks.reshape(1, nq), drel.reshape(1, NUM_BUCKETS * nq),
               dconv_vec[0:1], dconv_vec[1:2], dconv_vec[2:3], acc_f[2:3]]
    row3, offs3 = _row_pack(small_g + [dconv_w[:CONV_WIDTH].reshape(1, CONV_WIDTH * ch), loss11])
    got3 = run(allgather_small, row3, "allgather_small_grads")
    cx_in = rs_chip_start("in", ["w_in"], px_in)
    as_row = lambda a: a.reshape(1, -1)
    outs3 = run(small_sum_adamw, got3, offs3, [as_row(a) for a in small_w], [as_row(a) for a in small_m],
                [as_row(a) for a in small_v], [CONV_WIDTH * ch, 1])
    for i, (n, w) in enumerate(zip(small_names, small_w)):
        grads[n], deltas[n], new_m[n], new_v[n] = (o.reshape(w.shape) for o in outs3[4 * i:4 * i + 4])
    g_conv_w_all, loss_sum = outs3[-2].reshape(CONV_WIDTH, ch), outs3[-1]

    g_conv_w = lax.dynamic_slice_in_dim(g_conv_w_all, chip * ch_loc, ch_loc, axis=1)
    grads["conv_w"] = g_conv_w[None]
    dl, nm, nv = run(adamw, conv_w[0], g_conv_w, m_conv_w[0], v_conv_w[0], "adamw_conv_w")
    deltas["conv_w"], new_m["conv_w"], new_v["conv_w"] = dl[None], nm[None], nv[None]

    dmod_all = got3[:, 0, :N_MOD * d]
    dmod_cols = lax.dynamic_slice_in_dim(dmod_all, chip * nc_ada, nc_ada, axis=1)
    g_ada, dl, nm, nv = run(ada_grad_adamw, c_t, dmod_cols, w_ada[0], m_w_ada[0], v_w_ada[0])
    grads["w_ada"], deltas["w_ada"], new_m["w_ada"], new_v["w_ada"] = g_ada[None], dl[None], nm[None], nv[None]

    rs_finish("ffn_in", ["w_ffn_in"], sh_ffn_in)
    rs_finish("mix_out", ["w_mix_out"], sh_mix)
    rs_finish("attn_conv_out", ac_names, sh_ac)
    sh_in = rs_share_start("in", ["w_in"], cx_in)
    rs_finish("in", ["w_in"], sh_in)

    loss = loss_sum[0, 0]
    order = ["w_ada", "b_ada", "norm_mix_g", "w_in", "q_norm_g", "k_norm_g", "attn_sinks", "rel_bias", "w_attn_out",
             "conv_w", "conv_b", "conv_ln_g", "conv_ln_b", "w_conv_out", "w_mix_out", "norm_ffn_g", "w_ffn_in",
             "w_ffn_out"]
    return (loss, grad_x[None], *[grads[n] for n in order], *[deltas[n] for n in order],
            *[new_m[n] for n in order], *[new_v[n] for n in order])
```

```python
import functools
import math
from typing import Any, NamedTuple

import jax
import jax.numpy as jnp
import numpy as np
from jax import lax
from jax.experimental import pallas as pl
from jax.experimental.pallas import tpu as pltpu

F32 = jnp.float32
BF16 = jnp.bfloat16
MESH = pl.DeviceIdType.MESH

V7X_VMEM_BYTES = 64 * 1024 * 1024
VMEM_LIMIT = V7X_VMEM_BYTES - 8 * 1024 * 1024
LANES = 128
SUBLANES = 8
BF16_SUBLANES = 16

EPS = 1e-6
WINDOW = 128
BLOCK = 128
NUM_BUCKETS = 32
MAX_EXACT = NUM_BUCKETS // 2
MAX_DISTANCE = 128
CONV_WIDTH = 31
CONV_HALO = 32
ADAM_LR = 0.001
ADAM_B1 = 0.9
ADAM_B2 = 0.999
ADAM_EPS = 1e-08
ADAM_WD = 0.01
ADAM_STEP = 10
N_MOD = 6
SH_M, SC_M, GT_M, SH_F, SC_F, GT_F = range(6)

N_CHIPS = 4
N_DEV = 8

_ANY = pl.BlockSpec(memory_space=pl.ANY)
_VMEM = pl.BlockSpec(memory_space=pltpu.VMEM)
_SMEM = pl.BlockSpec(memory_space=pltpu.SMEM)
_HBM = pl.BlockSpec(memory_space=pltpu.HBM)
_SEM = pl.BlockSpec(memory_space=pltpu.SEMAPHORE)
_EFFECT = pltpu.SideEffectType.DATAFLOW_SIDE_EFFECTING


class InOrder:
    def __init__(self):
        self.token = None

    def __call__(self, fn, *args, **kw):
        return fn(*args, dep=self, **kw)


def _pallas(body, args, *, in_specs, out_specs, out_shape, name, dep=None, grid=(), n_prefetch=0, scratch=(),
            sem=None, **kw):
    n_lead = n_prefetch + len(in_specs)
    in_specs, args = list(in_specs), list(args)
    single = not isinstance(out_shape, (list, tuple))
    out_shapes = [out_shape] if single else list(out_shape)
    out_specs = [out_specs] if single else list(out_specs)
    if dep is not None:
        inner, n_out, takes = body, len(out_shapes), dep.token is not None

        def body(*refs):
            rest = refs[n_lead + (1 if takes else 0):]
            rest[n_out][...] = jnp.zeros((SUBLANES, LANES), F32)
            return inner(*refs[:n_lead], *rest[:n_out], *rest[n_out + 1:])

        if takes:
            in_specs.append(_ANY)
            args.append(dep.token)
        out_shapes.append(jax.ShapeDtypeStruct((SUBLANES, LANES), F32))
        out_specs.append(pl.BlockSpec((SUBLANES, LANES), lambda *_: (0, 0)))
    params = kw.pop("compiler_params", None)
    if params is None:
        params = pltpu.CompilerParams(dimension_semantics=sem, vmem_limit_bytes=VMEM_LIMIT)
    outs = pl.pallas_call(
        body,
        grid_spec=pltpu.PrefetchScalarGridSpec(num_scalar_prefetch=n_prefetch, grid=grid, in_specs=in_specs,
                                               out_specs=out_specs, scratch_shapes=list(scratch)),
        out_shape=out_shapes, compiler_params=params, name=name, **kw,
    )(*args)
    if dep is not None:
        dep.token = outs[-1]
        outs = outs[:-1]
    return outs[0] if single else list(outs)


def _tile(n, pref, unit=LANES):
    best = None
    for t in range(unit, min(n, pref) + 1, unit):
        if n % t == 0:
            best = t
    return best if best is not None else n


def _sigmoid(v):
    return 1.0 / (1.0 + jnp.exp(-v.astype(F32)))


ROW_CHUNK = 512


def _row_chunks(m, unit=SUBLANES):
    step = _tile(m, ROW_CHUNK, unit)
    return [(s, step) for s in range(0, m, step)]


def _ew_tiles(r, n, unit=SUBLANES, elems=512 * 1024):
    return _tile(r, max(unit, elems // n), unit), n


def _block_pos(j, perm):
    if perm is None:
        return j
    pos = 0
    for a, p in enumerate(perm):
        pos = pos + jnp.where(j == a, p, 0)
    return pos


def mm_nn(a, w, *, tn, tk, out_dtype, name, perm=None, dep=None):
    m, k = a.shape
    j, k2, nj = w.shape
    assert k == k2 and nj % tn == 0 and k % tk == 0
    npj, nk = nj // tn, k // tk

    def body(a_ref, w_ref, o_ref, *scratch):
        kk = pl.program_id(1)
        for s, sz in _row_chunks(m):
            rows = pl.ds(s, sz)
            p = jnp.dot(a_ref[rows, :], w_ref[...], preferred_element_type=F32)
            if nk == 1:
                o_ref[rows, :] = p.astype(out_dtype)
            else:
                acc = scratch[0]

                @pl.when(kk == 0)
                def _():
                    acc[rows, :] = p

                @pl.when(kk > 0)
                def _():
                    acc[rows, :] += p

                @pl.when(kk == nk - 1)
                def _():
                    o_ref[rows, :] = acc[rows, :].astype(out_dtype)

    return _pallas(
        body, [a, w], dep=dep, grid=(j * npj, nk),
        in_specs=[
            pl.BlockSpec((m, tk), lambda n, kk: (0, kk)),
            pl.BlockSpec((None, tk, tn), lambda n, kk: (n // npj, kk, n % npj)),
        ],
        out_specs=pl.BlockSpec((m, tn), lambda n, kk: (0, _block_pos(n // npj, perm) * npj + n % npj)),
        out_shape=jax.ShapeDtypeStruct((m, j * nj), out_dtype),
        scratch=[pltpu.VMEM((m, tn), F32)] if nk > 1 else [],
        sem=("parallel", "arbitrary"), name=name)


def mm_nt(g, w, *, tko, tn, name, out_dtype=F32, perm=None, dep=None):
    m, n = g.shape
    j, k, nj = w.shape
    assert n == j * nj and nj % tn == 0 and k % tko == 0
    npj, nr = nj // tn, n // tn
    in_place = out_dtype == F32

    def body(g_ref, w_ref, o_ref, *scratch):
        r = pl.program_id(1)
        acc = o_ref if in_place else (scratch[0] if nr > 1 else None)
        for s, sz in _row_chunks(m):
            rows = pl.ds(s, sz)
            p = lax.dot_general(g_ref[rows, :], w_ref[...], (((1,), (1,)), ((), ())), preferred_element_type=F32)
            if acc is None:
                o_ref[rows, :] = p.astype(out_dtype)
                continue

            @pl.when(r == 0)
            def _():
                acc[rows, :] = p

            @pl.when(r > 0)
            def _():
                acc[rows, :] += p

            if not in_place:
                @pl.when(r == nr - 1)
                def _():
                    o_ref[rows, :] = acc[rows, :].astype(out_dtype)

    return _pallas(
        body, [g, w], dep=dep, grid=(k // tko, nr),
        in_specs=[
            pl.BlockSpec((m, tn), lambda ko, r: (0, _block_pos(r // npj, perm) * npj + r % npj)),
            pl.BlockSpec((None, tko, tn), lambda ko, r: (r // npj, ko, r % npj)),
        ],
        out_specs=pl.BlockSpec((m, tko), lambda ko, r: (0, ko)),
        out_shape=jax.ShapeDtypeStruct((m, k), out_dtype),
        scratch=[pltpu.VMEM((m, tko), F32)] if (nr > 1 and not in_place) else [],
        sem=("parallel", "arbitrary"), name=name)


def mm_tn(a, g, n_blocks, *, tk, tn, name, perm=None, dep=None):
    m, k = a.shape
    m2, n = g.shape
    nj = n // n_blocks
    assert m == m2 and nj % tn == 0 and k % tk == 0
    npj = nj // tn

    def body(a_ref, g_ref, o_ref):
        for s, sz in _row_chunks(tk, LANES):
            p = lax.dot_general(a_ref[:, pl.ds(s, sz)], g_ref[...], (((0,), (0,)), ((), ())),
                                preferred_element_type=F32)
            o_ref[pl.ds(s, sz), :] = p.astype(BF16)

    return _pallas(
        body, [a, g], dep=dep, grid=(k // tk, n // tn),
        in_specs=[
            pl.BlockSpec((m, tk), lambda kk, nn: (0, kk)),
            pl.BlockSpec((m, tn), lambda kk, nn: (0, _block_pos(nn // npj, perm) * npj + nn % npj)),
        ],
        out_specs=pl.BlockSpec((None, tk, tn), lambda kk, nn: (nn // npj, kk, nn % npj)),
        out_shape=jax.ShapeDtypeStruct((n_blocks, k, nj), BF16),
        sem=("parallel", "parallel"), name=name)


ROW_TILE = 256


def _row_spec(tr, width):
    return pl.BlockSpec((tr, width), lambda i: (i, 0))


def _full_spec(shape):
    return pl.BlockSpec(shape, lambda *_: (0,) * len(shape))


def _rms(xv):
    return lax.rsqrt(jnp.mean(xv * xv, axis=-1, keepdims=True) + EPS)


def _mod_row(mod_ref, row):
    return mod_ref[pl.ds(row, 1), :]


def pre_mix_fwd(x, mod, gain, dep=None):
    t, d = x.shape
    tr = _tile(t, ROW_TILE, SUBLANES)

    def body(x_ref, mod_ref, g_ref, h_ref):
        xv = x_ref[...]
        y = xv * _rms(xv) * g_ref[...]
        h_ref[...] = (y * (1.0 + _mod_row(mod_ref, SC_M)) + _mod_row(mod_ref, SH_M)).astype(BF16)

    return _pallas(
        body, [x, mod, gain], dep=dep, grid=(t // tr,),
        in_specs=[_row_spec(tr, d), _full_spec(mod.shape), _full_spec(gain.shape)],
        out_specs=_row_spec(tr, d),
        out_shape=jax.ShapeDtypeStruct((t, d), BF16),
        sem=("parallel",), name="pre_mix_fwd")


def pre_ffn_fwd(x, o_m, mod, gain, dep=None):
    t, d = x.shape
    tr = _tile(t, ROW_TILE, SUBLANES)

    def body(x_ref, om_ref, mod_ref, g_ref, x1_ref, h_ref):
        x1 = x_ref[...] + _mod_row(mod_ref, GT_M) * om_ref[...]
        x1_ref[...] = x1
        y = x1 * _rms(x1) * g_ref[...]
        h_ref[...] = (y * (1.0 + _mod_row(mod_ref, SC_F)) + _mod_row(mod_ref, SH_F)).astype(BF16)

    return _pallas(
        body, [x, o_m, mod, gain], dep=dep, grid=(t // tr,),
        in_specs=[_row_spec(tr, d), _row_spec(tr, d), _full_spec(mod.shape), _full_spec(gain.shape)],
        out_specs=[_row_spec(tr, d), _row_spec(tr, d)],
        out_shape=[jax.ShapeDtypeStruct((t, d), F32), jax.ShapeDtypeStruct((t, d), BF16)],
        sem=("parallel",), name="pre_ffn_fwd")


def loss_head(x1, o_f, target, mod, dep=None):
    t, d = x1.shape
    tr = _tile(t, ROW_TILE, SUBLANES)

    def body(x1_ref, of_ref, tg_ref, mod_ref, loss_ref, dy_ref, dof_ref, acc_ref):
        i = pl.program_id(0)
        gt = _mod_row(mod_ref, GT_F)
        of = of_ref[...].astype(F32)
        err = x1_ref[...] + gt * of - tg_ref[...]
        dy = err * (1.0 / d)
        dy_ref[...] = dy.astype(BF16)
        dof_ref[...] = (dy * gt).astype(BF16)
        part = (0.5 / d) * jnp.sum(jnp.sum(err * err, axis=1, keepdims=True), axis=0, keepdims=True)
        dgt = jnp.sum(dy * of, axis=0, keepdims=True)

        @pl.when(i == 0)
        def _():
            loss_ref[...] = jnp.zeros_like(loss_ref)
            acc_ref[...] = jnp.zeros_like(acc_ref)

        loss_ref[...] += part
        acc_ref[pl.ds(0, 1), :] += dgt

    return _pallas(
        body, [x1, o_f, target, mod], dep=dep, grid=(t // tr,),
        in_specs=[_row_spec(tr, d), _row_spec(tr, d), _row_spec(tr, d), _full_spec(mod.shape)],
        out_specs=[_full_spec((1, 1)), _row_spec(tr, d), _row_spec(tr, d), _full_spec((SUBLANES, d))],
        out_shape=[jax.ShapeDtypeStruct((1, 1), F32), jax.ShapeDtypeStruct((t, d), BF16),
                   jax.ShapeDtypeStruct((t, d), BF16), jax.ShapeDtypeStruct((SUBLANES, d), F32)],
        sem=("arbitrary",), name="loss_head")


def _norm_bwd(xv, dh, sc, gain):
    rstd = _rms(xv)
    yn = xv * rstd
    dsh = jnp.sum(dh, axis=0, keepdims=True)
    dsc = jnp.sum(dh * (yn * gain), axis=0, keepdims=True)
    dgain = jnp.sum(dh * (1.0 + sc) * yn, axis=0, keepdims=True)
    dyn = dh * ((1.0 + sc) * gain)
    dx = rstd * (dyn - yn * jnp.mean(dyn * yn, axis=-1, keepdims=True))
    return dx, dsh, dsc, dgain


def pre_ffn_bwd(x1, dh2, dy, o_m, mod, gain, dep=None):
    t, d = x1.shape
    tr = _tile(t, ROW_TILE, SUBLANES)

    def body(x1_ref, dh_ref, dy_ref, om_ref, mod_ref, g_ref, dx1_ref, dom_ref, acc_ref):
        i = pl.program_id(0)
        dxn, dsh, dsc, dgain = _norm_bwd(x1_ref[...], dh_ref[...].astype(F32), _mod_row(mod_ref, SC_F), g_ref[...])
        dx1 = dy_ref[...] + dxn
        dx1_ref[...] = dx1
        dom_ref[...] = (dx1 * _mod_row(mod_ref, GT_M)).astype(BF16)
        dgt = jnp.sum(dx1 * om_ref[...], axis=0, keepdims=True)

        @pl.when(i == 0)
        def _():
            acc_ref[...] = jnp.zeros_like(acc_ref)

        acc_ref[pl.ds(0, 1), :] += dsh
        acc_ref[pl.ds(1, 1), :] += dsc
        acc_ref[pl.ds(2, 1), :] += dgain
        acc_ref[pl.ds(3, 1), :] += dgt

    return _pallas(
        body, [x1, dh2, dy, o_m, mod, gain], dep=dep, grid=(t // tr,),
        in_specs=[_row_spec(tr, d)] * 4 + [_full_spec(mod.shape), _full_spec(gain.shape)],
        out_specs=[_row_spec(tr, d), _row_spec(tr, d), _full_spec((SUBLANES, d))],
        out_shape=[jax.ShapeDtypeStruct((t, d), F32), jax.ShapeDtypeStruct((t, d), BF16),
                   jax.ShapeDtypeStruct((SUBLANES, d), F32)],
        sem=("arbitrary",), name="pre_ffn_bwd")


def pre_mix_bwd(x, dh, dx1, mod, gain, dep=None):
    t, d = x.shape
    tr = _tile(t, ROW_TILE, SUBLANES)

    def body(x_ref, dh_ref, dx1_ref, mod_ref, g_ref, gx_ref, acc_ref):
        i = pl.program_id(0)
        dxn, dsh, dsc, dgain = _norm_bwd(x_ref[...], dh_ref[...].astype(F32), _mod_row(mod_ref, SC_M), g_ref[...])
        gx_ref[...] = dx1_ref[...] + dxn

        @pl.when(i == 0)
        def _():
            acc_ref[...] = jnp.zeros_like(acc_ref)

        acc_ref[pl.ds(0, 1), :] += dsh
        acc_ref[pl.ds(1, 1), :] += dsc
        acc_ref[pl.ds(2, 1), :] += dgain

    return _pallas(
        body, [x, dh, dx1, mod, gain], dep=dep, grid=(t // tr,),
        in_specs=[_row_spec(tr, d)] * 3 + [_full_spec(mod.shape), _full_spec(gain.shape)],
        out_specs=[_row_spec(tr, d), _full_spec((SUBLANES, d))],
        out_shape=[jax.ShapeDtypeStruct((t, d), F32), jax.ShapeDtypeStruct((SUBLANES, d), F32)],
        sem=("arbitrary",), name="pre_mix_bwd")


def branch_out_merge(attn_o, s_conv, w_attn, w_conv, p, off_ga, off_gc, dep=None):
    t = attn_o.shape[0]
    j, ka, nj = w_attn.shape
    kc = w_conv.shape[1]
    assert off_ga % nj == 0 and off_gc % nj == 0

    def body(a_ref, s_ref, wa_ref, wc_ref, ga_ref, gc_ref, ya_ref, yc_ref, m_ref):
        for s, sz in _row_chunks(t):
            rows = pl.ds(s, sz)
            ya = jnp.dot(a_ref[rows, :], wa_ref[...], preferred_element_type=F32)
            yc = jnp.dot(s_ref[rows, :], wc_ref[...], preferred_element_type=F32)
            ya_ref[rows, :] = ya.astype(BF16)
            yc_ref[rows, :] = yc.astype(BF16)
            m_ref[rows, :] = (_sigmoid(ga_ref[rows, :]) * ya + _sigmoid(gc_ref[rows, :]) * yc).astype(BF16)

    col = pl.BlockSpec((t, nj), lambda b: (0, b))
    return _pallas(
        body, [attn_o, s_conv, w_attn, w_conv, p, p], dep=dep, grid=(j,),
        in_specs=[pl.BlockSpec((t, ka), lambda b: (0, 0)), pl.BlockSpec((t, kc), lambda b: (0, 0)),
                  pl.BlockSpec((None, ka, nj), lambda b: (b, 0, 0)), pl.BlockSpec((None, kc, nj), lambda b: (b, 0, 0)),
                  pl.BlockSpec((t, nj), lambda b: (0, off_ga // nj + b)),
                  pl.BlockSpec((t, nj), lambda b: (0, off_gc // nj + b))],
        out_specs=[col] * 3,
        out_shape=[jax.ShapeDtypeStruct((t, j * nj), BF16)] * 3,
        sem=("parallel",), name="branch_out_merge")


def mix_out_dx_merge_bwd(dom, w_mix, p, y_attn, y_conv, off_ga, off_gc, dep=None):
    t, d = y_attn.shape
    cw = math.gcd(math.gcd(off_ga, off_gc), math.gcd(d, 256))

    def body(dom_ref, w_ref, ga_ref, gc_ref, ya_ref, yc_ref, dya_ref, dyc_ref, dga_ref, dgc_ref):
        for s, sz in _row_chunks(t):
            rows = pl.ds(s, sz)
            dm = lax.dot_general(dom_ref[rows, :], w_ref[...], (((1,), (1,)), ((), ())), preferred_element_type=F32)
            sa = _sigmoid(ga_ref[rows, :])
            sc = _sigmoid(gc_ref[rows, :])
            dya_ref[rows, :] = (dm * sa).astype(BF16)
            dyc_ref[rows, :] = (dm * sc).astype(BF16)
            dga_ref[rows, :] = (dm * ya_ref[rows, :] * sa * (1.0 - sa)).astype(BF16)
            dgc_ref[rows, :] = (dm * yc_ref[rows, :] * sc * (1.0 - sc)).astype(BF16)

    col = pl.BlockSpec((t, cw), lambda j: (0, j))
    return _pallas(
        body, [dom, w_mix, p, p, y_attn, y_conv], dep=dep, grid=(d // cw,),
        in_specs=[pl.BlockSpec((t, d), lambda j: (0, 0)), pl.BlockSpec((cw, d), lambda j: (j, 0)),
                  pl.BlockSpec((t, cw), lambda j: (0, off_ga // cw + j)),
                  pl.BlockSpec((t, cw), lambda j: (0, off_gc // cw + j)), col, col],
        out_specs=[col] * 4,
        out_shape=[jax.ShapeDtypeStruct((t, d), BF16)] * 4,
        sem=("parallel",), name="mm_mix_out_dx_merge_bwd")


def ffn_perm(n_blocks):
    half = n_blocks // 2
    return tuple(2 * j if j < half else 2 * (j - half) + 1 for j in range(n_blocks))


def swiglu_fwd(f, nj, dep=None):
    t, two = f.shape
    tr = _tile(t, ROW_TILE, SUBLANES)
    npair = two // (2 * nj)

    def body(f_ref, o_ref):
        g = f_ref[:, :nj].astype(F32)
        u = f_ref[:, nj:].astype(F32)
        o_ref[...] = (g * _sigmoid(g) * u).astype(BF16)

    return _pallas(
        body, [f], dep=dep, grid=(t // tr, npair),
        in_specs=[pl.BlockSpec((tr, 2 * nj), lambda i, j: (i, j))],
        out_specs=pl.BlockSpec((tr, nj), lambda i, j: (i, j)),
        out_shape=jax.ShapeDtypeStruct((t, two // 2), BF16),
        sem=("parallel", "parallel"), name="swiglu_fwd")


def swiglu_bwd(f, dact, nj, dep=None):
    t, two = f.shape
    tr = _tile(t, ROW_TILE, SUBLANES)
    npair = two // (2 * nj)

    def body(f_ref, da_ref, o_ref):
        g = f_ref[:, :nj].astype(F32)
        u = f_ref[:, nj:].astype(F32)
        da = da_ref[...]
        s = _sigmoid(g)
        o_ref[:, :nj] = (da * u * (s * (1.0 + g * (1.0 - s)))).astype(BF16)
        o_ref[:, nj:] = (da * (g * s)).astype(BF16)

    return _pallas(
        body, [f, dact], dep=dep, grid=(t // tr, npair),
        in_specs=[pl.BlockSpec((tr, 2 * nj), lambda i, j: (i, j)), pl.BlockSpec((tr, nj), lambda i, j: (i, j))],
        out_specs=pl.BlockSpec((tr, 2 * nj), lambda i, j: (i, j)),
        out_shape=jax.ShapeDtypeStruct((t, two), BF16),
        sem=("parallel", "parallel"), name="swiglu_bwd")


def _t5_bucket_table():
    q_off = np.arange(BLOCK)
    k_off = np.arange(2 * BLOCK)
    dist = q_off[:, None] + BLOCK - k_off[None, :]
    n = np.maximum(dist, 0)
    nf = np.maximum(n, 1).astype(np.float32)
    large = MAX_EXACT + (np.log(nf / np.float32(MAX_EXACT)) / np.float32(math.log(MAX_DISTANCE / MAX_EXACT))
                         * np.float32(NUM_BUCKETS - MAX_EXACT)).astype(np.int32)
    large = np.minimum(large, NUM_BUCKETS - 1)
    bucket = np.where(n < MAX_EXACT, n, large).astype(np.int32)
    allowed = (dist >= 0) & (dist < WINDOW)
    return np.where(allowed, bucket, -1).astype(np.int32)


def bias_table(rel_bias, bucket_p, bucket_c, dep=None):
    nb, nq = rel_bias.shape

    def body(rb_ref, bkp_ref, bkc_ref, op_ref, oc_ref):
        for bk_ref, o_ref in ((bkp_ref, op_ref), (bkc_ref, oc_ref)):
            bk = bk_ref[...]
            for h in range(nq):
                acc = jnp.full(bk.shape, -jnp.inf, F32)
                for b in range(nb):
                    acc = jnp.where(bk == b, rb_ref[b, h], acc)
                o_ref[h] = acc

    return _pallas(
        body, [rel_bias, bucket_p, bucket_c], dep=dep,
        in_specs=[_SMEM, _VMEM, _VMEM], out_specs=[_VMEM, _VMEM],
        out_shape=[jax.ShapeDtypeStruct((nq,) + bucket_p.shape, F32)] * 2,
        name="bias_table")


def bias_table_bwd(dbp, dbc, bucket_p, bucket_c, dep=None):
    nq = dbp.shape[0]

    def body(dbp_ref, dbc_ref, bkp_ref, bkc_ref, o_ref):
        bkp, bkc = bkp_ref[...][None], bkc_ref[...][None]
        dp, dc = dbp_ref[...], dbc_ref[...]
        for b in range(NUM_BUCKETS):
            sel = jnp.where(bkp == b, dp, 0.0) + jnp.where(bkc == b, dc, 0.0)
            o_ref[b] = jnp.sum(jnp.sum(sel, axis=2, keepdims=True), axis=1, keepdims=True)

    return _pallas(
        body, [dbp, dbc, bucket_p, bucket_c], dep=dep,
        in_specs=[_VMEM] * 4, out_specs=_VMEM,
        out_shape=jax.ShapeDtypeStruct((NUM_BUCKETS, nq, 1, 1), F32),
        name="bias_table_bwd")


_BNT = (((2,), (2,)), ((0,), (0,)))
_BNN = (((2,), (1,)), ((0,), (0,)))
_BTN = (((1,), (1,)), ((0,), (0,)))


@jax.custom_vjp
def _bdot_nt(a, b):
    return lax.dot_general(a.astype(BF16), b.astype(BF16), _BNT, preferred_element_type=F32)


def _bdot_nt_fwd(a, b):
    return _bdot_nt(a, b), (a, b)


def _bdot_nt_bwd(res, g):
    a, b = res
    gb = g.astype(BF16)
    da = lax.dot_general(gb, b.astype(BF16), _BNN, preferred_element_type=F32)
    db = lax.dot_general(gb, a.astype(BF16), _BTN, preferred_element_type=F32)
    return da, db


_bdot_nt.defvjp(_bdot_nt_fwd, _bdot_nt_bwd)


@jax.custom_vjp
def _bdot_nn(a, b):
    return lax.dot_general(a.astype(BF16), b.astype(BF16), _BNN, preferred_element_type=F32)


def _bdot_nn_fwd(a, b):
    return _bdot_nn(a, b), (a, b)


def _bdot_nn_bwd(res, g):
    a, b = res
    gb = g.astype(BF16)
    da = lax.dot_general(gb, b.astype(BF16), _BNT, preferred_element_type=F32)
    db = lax.dot_general(a.astype(BF16), gb, _BTN, preferred_element_type=F32)
    return da, db


_bdot_nn.defvjp(_bdot_nn_fwd, _bdot_nn_bwd)


def _attn_math(q, kp, kc, vp, vc, bp, bc, sinks, qg, kg, *, prev_ok, scale):
    h, rows, _ = q.shape
    b = kp.shape[1]
    qn = q * _rms(q) * qg
    kpn = kp * _rms(kp) * kg
    kcn = kc * _rms(kc) * kg
    lp = _bdot_nt(qn, kpn) * scale + bp.reshape(h, rows, b)
    lc = _bdot_nt(qn, kcn) * scale + bc.reshape(h, rows, b)
    lp = jnp.where(prev_ok, lp, -jnp.inf)
    sink = jnp.broadcast_to(sinks, (sinks.shape[0], b, 1)).reshape(h, rows, 1)
    m = jnp.maximum(jnp.maximum(jnp.max(lp, axis=-1, keepdims=True), jnp.max(lc, axis=-1, keepdims=True)), sink)
    m = lax.stop_gradient(m)
    pp = jnp.exp(lp - m)
    pc = jnp.exp(lc - m)
    den = jnp.sum(pp, axis=-1, keepdims=True) + jnp.sum(pc, axis=-1, keepdims=True) + jnp.exp(sink - m)
    inv = 1.0 / den
    return _bdot_nn(pp * inv, vp) + _bdot_nn(pc * inv, vc)


def _attn_specs(p, aw, kvw, nq, hd, nblk, reverse):
    assert aw % (2 * kvw) == 0
    kv_col = aw // (2 * kvw)

    def blk(n):
        return nblk - 1 - n if reverse else n

    return [
        pl.BlockSpec((BLOCK, aw), lambda n: (blk(n), 0)),
        pl.BlockSpec((BLOCK, 2 * kvw), lambda n: (jnp.maximum(blk(n) - 1, 0), kv_col)),
        pl.BlockSpec((BLOCK, 2 * kvw), lambda n: (blk(n), kv_col)),
        _full_spec((nq, BLOCK, BLOCK)), _full_spec((nq, BLOCK, BLOCK)), _full_spec((nq, 1, 1)),
        _full_spec((1, hd)), _full_spec((1, hd)),
    ]


def _head_major(ref, n_heads, grp, hd, offset=0):
    return jnp.stack([
        jnp.concatenate([ref[:, pl.ds(offset + (grp * h + g) * hd, hd)].astype(F32) for g in range(grp)], axis=0)
        for h in range(n_heads)])


def _attn_inputs(nkv, grp, hd, kvw, q_ref, kvp_ref, kvc_ref):
    return (_head_major(q_ref, nkv, grp, hd), _head_major(kvp_ref, nkv, 1, hd), _head_major(kvc_ref, nkv, 1, hd),
            _head_major(kvp_ref, nkv, 1, hd, kvw), _head_major(kvc_ref, nkv, 1, hd, kvw))


def attn_fwd(p, bias_p, bias_c, sinks, qg, kg, *, aw, kvw, dep=None):
    t, hd = p.shape[0], qg.shape[-1]
    nq, nkv, nblk = aw // hd, kvw // hd, t // BLOCK
    grp = nq // nkv
    scale = hd ** -0.5

    def body(q_ref, kvp_ref, kvc_ref, bp_ref, bc_ref, s_ref, qg_ref, kg_ref, o_ref):
        prev_ok = pl.program_id(0) > 0
        out = _attn_math(*_attn_inputs(nkv, grp, hd, kvw, q_ref, kvp_ref, kvc_ref), bp_ref[...], bc_ref[...],
                         s_ref[...], qg_ref[...], kg_ref[...], prev_ok=prev_ok, scale=scale)
        for h in range(nkv):
            for g in range(grp):
                o_ref[:, pl.ds((grp * h + g) * hd, hd)] = out[h, g * BLOCK:(g + 1) * BLOCK].astype(BF16)

    return _pallas(
        body, [p, p, p, bias_p, bias_c, sinks, qg, kg], dep=dep, grid=(nblk,),
        in_specs=_attn_specs(p, aw, kvw, nq, hd, nblk, False),
        out_specs=pl.BlockSpec((BLOCK, aw), lambda n: (n, 0)),
        out_shape=jax.ShapeDtypeStruct((t, aw), BF16),
        sem=("parallel",), name="attn_fwd")


def attn_bwd(p, bias_p, bias_c, sinks, qg, kg, do, *, aw, kvw, dep=None):
    t, hd = p.shape[0], qg.shape[-1]
    nq, nkv, nblk = aw // hd, kvw // hd, t // BLOCK
    grp = nq // nkv
    scale = hd ** -0.5

    def body(q_ref, kvp_ref, kvc_ref, bp_ref, bc_ref, s_ref, qg_ref, kg_ref, do_ref,
             dqkv_ref, dbp_ref, dbc_ref, ds_ref, dqg_ref, dkg_ref, carry):
        i = pl.program_id(0)
        prev_ok = (nblk - 1 - i) > 0

        @pl.when(i == 0)
        def _():
            carry[...] = jnp.zeros_like(carry)
            dbp_ref[...] = jnp.zeros_like(dbp_ref)
            dbc_ref[...] = jnp.zeros_like(dbc_ref)
            ds_ref[...] = jnp.zeros_like(ds_ref)
            dqg_ref[...] = jnp.zeros_like(dqg_ref)
            dkg_ref[...] = jnp.zeros_like(dkg_ref)

        fn = functools.partial(_attn_math, prev_ok=prev_ok, scale=scale)
        _, vjp = jax.vjp(fn, *_attn_inputs(nkv, grp, hd, kvw, q_ref, kvp_ref, kvc_ref), bp_ref[...], bc_ref[...],
                         s_ref[...], qg_ref[...], kg_ref[...])
        dq, dkp, dkc, dvp, dvc, dbp, dbc, dsk, dqg, dkg = vjp(_head_major(do_ref, nkv, grp, hd))
        for h in range(nkv):
            for g in range(grp):
                dqkv_ref[:, pl.ds((grp * h + g) * hd, hd)] = dq[h, g * BLOCK:(g + 1) * BLOCK].astype(BF16)
            k_cols, v_cols = pl.ds(h * hd, hd), pl.ds(kvw + h * hd, hd)
            dqkv_ref[:, pl.ds(aw + h * hd, hd)] = (dkc[h] + carry[:, k_cols]).astype(BF16)
            dqkv_ref[:, pl.ds(aw + kvw + h * hd, hd)] = (dvc[h] + carry[:, v_cols]).astype(BF16)
            carry[:, k_cols] = dkp[h]
            carry[:, v_cols] = dvp[h]
        dbp_ref[...] += dbp
        dbc_ref[...] += dbc
        ds_ref[...] += dsk
        dqg_ref[...] += dqg
        dkg_ref[...] += dkg

    return _pallas(
        body, [p, p, p, bias_p, bias_c, sinks, qg, kg, do], dep=dep, grid=(nblk,),
        in_specs=_attn_specs(p, aw, kvw, nq, hd, nblk, True)
        + [pl.BlockSpec((BLOCK, aw), lambda n: (nblk - 1 - n, 0))],
        out_specs=[
            pl.BlockSpec((BLOCK, aw + 2 * kvw), lambda n: (nblk - 1 - n, 0)),
            _full_spec((nq, BLOCK, BLOCK)), _full_spec((nq, BLOCK, BLOCK)), _full_spec((nq, 1, 1)),
            _full_spec((1, hd)), _full_spec((1, hd)),
        ],
        out_shape=[
            jax.ShapeDtypeStruct((t, aw + 2 * kvw), BF16),
            jax.ShapeDtypeStruct((nq, BLOCK, BLOCK), F32),
            jax.ShapeDtypeStruct((nq, BLOCK, BLOCK), F32),
            jax.ShapeDtypeStruct((nq, 1, 1), F32),
            jax.ShapeDtypeStruct((1, hd), F32),
            jax.ShapeDtypeStruct((1, hd), F32),
        ],
        scratch=[pltpu.VMEM((BLOCK, 2 * kvw), F32)],
        sem=("arbitrary",), name="attn_bwd")


CONV_TILE = 256


def _conv_halo_specs(tb, ch, nblk):
    per = tb // CONV_HALO
    last = nblk * per - 1
    cur = pl.BlockSpec((tb, ch), lambda n: (n, 0))
    prev = pl.BlockSpec((CONV_HALO, ch), lambda n: (jnp.maximum(n * per - 1, 0), 0))
    nxt = pl.BlockSpec((CONV_HALO, ch), lambda n: (jnp.minimum((n + 1) * per, last), 0))
    return cur, prev, nxt


def _ln_silu(co, ln_g, ln_b):
    mu = jnp.mean(co, axis=-1, keepdims=True)
    cen = co - mu
    rstd = lax.rsqrt(jnp.mean(cen * cen, axis=-1, keepdims=True) + EPS)
    xhat = cen * rstd
    z = xhat * ln_g + ln_b
    return xhat, rstd, z


def _shifted_copies(src, shifted):
    rows = src.shape[0] - SUBLANES
    for r in range(1, SUBLANES):
        shifted[r, pl.ds(0, rows), :] = src[pl.ds(r, rows), :]


def _rows_from(src, shifted, start, n):
    r = start % SUBLANES
    if r == 0:
        return src[pl.ds(start, n), :]
    return shifted[r, pl.ds(start - r, n), :]


def conv_fwd(ca, cb, conv_w, conv_b, ln_g, ln_b, dep=None):
    t, ch = ca.shape
    tb = _tile(t, CONV_TILE, CONV_HALO)
    nblk = t // tb
    cur, prev, _ = _conv_halo_specs(tb, ch, nblk)
    lead = CONV_HALO - (CONV_WIDTH - 1)

    def body(ca_ref, cb_ref, cap_ref, cbp_ref, w_ref, b_ref, g_ref, bb_ref, s_ref, co_ref, ubuf, ushift):
        n = pl.program_id(0)
        halo = cap_ref[...] * _sigmoid(cbp_ref[...])
        ubuf[pl.ds(0, CONV_HALO), :] = jnp.where(n > 0, halo, 0.0)
        ubuf[pl.ds(CONV_HALO, tb), :] = ca_ref[...] * _sigmoid(cb_ref[...])
        _shifted_copies(ubuf, ushift)
        acc = jnp.broadcast_to(b_ref[...], (tb, ch))
        for k in range(CONV_WIDTH):
            acc = acc + w_ref[pl.ds(k, 1), :] * _rows_from(ubuf, ushift, lead + k, tb)
        co_ref[...] = acc
        _, _, z = _ln_silu(acc, g_ref[...], bb_ref[...])
        s_ref[...] = (z * _sigmoid(z)).astype(BF16)

    vec = _full_spec((1, ch))
    return _pallas(
        body, [ca, cb, ca, cb, conv_w, conv_b, ln_g, ln_b], dep=dep, grid=(nblk,),
        in_specs=[cur, cur, prev, prev, _full_spec(conv_w.shape), vec, vec, vec],
        out_specs=[cur, cur],
        out_shape=[jax.ShapeDtypeStruct((t, ch), BF16), jax.ShapeDtypeStruct((t, ch), F32)],
        scratch=[pltpu.VMEM((CONV_HALO + tb, ch), F32), pltpu.VMEM((SUBLANES, CONV_HALO + tb, ch), F32)],
        sem=("parallel",), name="conv_fwd")


def conv_bwd(ca, cb, co, ds, conv_w, ln_g, ln_b, dep=None):
    t, ch = ca.shape
    tb = _tile(t, CONV_TILE, CONV_HALO)
    nblk = t // tb
    cur, prev, nxt = _conv_halo_specs(tb, ch, nblk)
    lead = CONV_HALO - (CONV_WIDTH - 1)
    ext = tb + CONV_HALO

    def body(ca_ref, cb_ref, cap_ref, cbp_ref, co_ref, con_ref, ds_ref, dsn_ref, w_ref, g_ref, bb_ref,
             dca_ref, dcb_ref, dw_ref, dvec_ref, ubuf, dbuf, ushift, dshift):
        n = pl.program_id(0)
        is_last = n == nblk - 1
        sig_b = _sigmoid(cb_ref[...])
        cav = ca_ref[...].astype(F32)
        ubuf[pl.ds(0, CONV_HALO), :] = jnp.where(n > 0, cap_ref[...] * _sigmoid(cbp_ref[...]), 0.0)
        ubuf[pl.ds(CONV_HALO, tb), :] = cav * sig_b
        _shifted_copies(ubuf, ushift)
        co = jnp.concatenate([co_ref[...], con_ref[...]], axis=0)
        xhat, rstd, z = _ln_silu(co, g_ref[...], bb_ref[...])
        dsv = jnp.concatenate([ds_ref[...].astype(F32), jnp.where(is_last, 0.0, dsn_ref[...].astype(F32))], axis=0)
        sg = _sigmoid(z)
        dz = dsv * (sg * (1.0 + z * (1.0 - sg)))
        dxh = dz * g_ref[...]
        dco = rstd * (dxh - jnp.mean(dxh, axis=-1, keepdims=True)
                      - xhat * jnp.mean(dxh * xhat, axis=-1, keepdims=True))
        dbuf[...] = dco
        _shifted_copies(dbuf, dshift)

        @pl.when(n == 0)
        def _():
            dw_ref[...] = jnp.zeros_like(dw_ref)
            dvec_ref[...] = jnp.zeros_like(dvec_ref)

        dco_cur = dco[:tb]
        dvec_ref[pl.ds(0, 1), :] += jnp.sum(dco_cur, axis=0, keepdims=True)
        dvec_ref[pl.ds(1, 1), :] += jnp.sum(dz[:tb] * xhat[:tb], axis=0, keepdims=True)
        dvec_ref[pl.ds(2, 1), :] += jnp.sum(dz[:tb], axis=0, keepdims=True)
        du = jnp.zeros((tb, ch), F32)
        for k in range(CONV_WIDTH):
            du = du + w_ref[pl.ds(k, 1), :] * _rows_from(dbuf, dshift, CONV_WIDTH - 1 - k, tb)
            dw_ref[pl.ds(k, 1), :] += jnp.sum(dco_cur * _rows_from(ubuf, ushift, lead + k, tb), axis=0,
                                              keepdims=True)
        dca_ref[...] = (du * sig_b).astype(BF16)
        dcb_ref[...] = (du * cav * sig_b * (1.0 - sig_b)).astype(BF16)

    vec = _full_spec((1, ch))
    return _pallas(
        body, [ca, cb, ca, cb, co, co, ds, ds, conv_w, ln_g, ln_b], dep=dep, grid=(nblk,),
        in_specs=[cur, cur, prev, prev, cur, nxt, cur, nxt, _full_spec(conv_w.shape), vec, vec],
        out_specs=[cur, cur, _full_spec(conv_w.shape), _full_spec((SUBLANES, ch))],
        out_shape=[jax.ShapeDtypeStruct((t, ch), BF16), jax.ShapeDtypeStruct((t, ch), BF16),
                   jax.ShapeDtypeStruct(conv_w.shape, F32), jax.ShapeDtypeStruct((SUBLANES, ch), F32)],
        scratch=[pltpu.VMEM((CONV_HALO + tb, ch), F32), pltpu.VMEM((ext, ch), F32),
                 pltpu.VMEM((SUBLANES, CONV_HALO + tb, ch), F32), pltpu.VMEM((SUBLANES, ext, ch), F32)],
        sem=("arbitrary",), name="conv_bwd")


def ada_fwd(c_t, w_ada, dep=None):
    d, nc = w_ada.shape
    nex = c_t.shape[1]
    tn = _tile(nc, 512)

    def body(ct_ref, w_ref, o_ref):
        w = w_ref[...]
        ct = ct_ref[...]
        cact = ct * _sigmoid(ct)
        rows = [jnp.sum(w * cact[:, b:b + 1], axis=0, keepdims=True) for b in range(nex)]
        o_ref[...] = jnp.concatenate(rows, axis=0)

    return _pallas(
        body, [c_t, w_ada], dep=dep, grid=(nc // tn,),
        in_specs=[_full_spec(c_t.shape), pl.BlockSpec((d, tn), lambda j: (0, j))],
        out_specs=pl.BlockSpec((nex, tn), lambda j: (0, j)),
        out_shape=jax.ShapeDtypeStruct((nex, nc), F32),
        sem=("parallel",), name="ada_fwd")


def _adamw_math(w, g, m, v):
    m = ADAM_B1 * m + (1.0 - ADAM_B1) * g
    v = ADAM_B2 * v + (1.0 - ADAM_B2) * (g * g)
    m_hat = m / (1.0 - ADAM_B1 ** ADAM_STEP)
    v_hat = v / (1.0 - ADAM_B2 ** ADAM_STEP)
    delta = -ADAM_LR * (m_hat / (jnp.sqrt(v_hat) + ADAM_EPS) + ADAM_WD * w)
    return delta, m, v


def adamw(w, g, m, v, name, copy_grad=False, dep=None):
    r, n = w.shape
    tr, tn = _ew_tiles(r, n, elems=256 * 1024)
    n_out = 4 if copy_grad else 3

    def body(w_ref, g_ref, m_ref, v_ref, *outs):
        g = g_ref[...]
        if copy_grad:
            outs[0][...] = g
        outs[-3][...], outs[-2][...], outs[-1][...] = _adamw_math(w_ref[...], g, m_ref[...], v_ref[...])

    blk = pl.BlockSpec((tr, tn), lambda i, j: (i, j))
    return _pallas(
        body, [w, g, m, v], dep=dep, grid=(r // tr, n // tn),
        in_specs=[blk] * 4, out_specs=[blk] * n_out,
        out_shape=[jax.ShapeDtypeStruct((r, n), F32)] * n_out,
        sem=("parallel", "parallel"), name=name)


def ada_grad_adamw(c_t, dmod_cols, w, m, v, dep=None):
    d, nc = w.shape
    nex = c_t.shape[1]
    tr, tn = _ew_tiles(d, nc, elems=256 * 1024)

    def body(ct_ref, dm_ref, w_ref, m_ref, v_ref, g_ref, d_ref, nm_ref, nv_ref):
        ct = ct_ref[...]
        cact = ct * _sigmoid(ct)
        dm = dm_ref[...]
        g = cact[:, 0:1] * dm[0:1, :]
        for b in range(1, nex):
            g = g + cact[:, b:b + 1] * dm[b:b + 1, :]
        g_ref[...] = g
        d_ref[...], nm_ref[...], nv_ref[...] = _adamw_math(w_ref[...], g, m_ref[...], v_ref[...])

    blk = pl.BlockSpec((tr, tn), lambda i, j: (i, j))
    return _pallas(
        body, [c_t, dmod_cols, w, m, v], dep=dep, grid=(d // tr, nc // tn),
        in_specs=[pl.BlockSpec((tr, nex), lambda i, j: (i, 0)), pl.BlockSpec((nex, tn), lambda i, j: (0, j)),
                  blk, blk, blk],
        out_specs=[blk] * 4,
        out_shape=[jax.ShapeDtypeStruct((d, nc), F32)] * 4,
        sem=("parallel", "parallel"), name="ada_grad_adamw")


def _row_pack(parts):
    cols, offs, off = [], [], 0
    for p in parts:
        n = p.shape[1]
        width = -(-n // LANES) * LANES
        cols.append(jnp.pad(p, ((0, 0), (0, width - n))) if width != n else p)
        offs.append(off)
        off += width
    return jnp.concatenate(cols, axis=1), offs


def small_sum_adamw(gathered, offs, ws, ms, vs, extra_widths, dep=None):
    ndev = gathered.shape[0]
    npar = len(ws)

    def body(ga_ref, *refs):
        w_refs, m_refs, v_refs = refs[:npar], refs[npar:2 * npar], refs[2 * npar:3 * npar]
        outs = refs[3 * npar:]
        tot = ga_ref[0]
        for s in range(1, ndev):
            tot = tot + ga_ref[s]
        for i in range(npar):
            n = ws[i].shape[1]
            g = tot[:, offs[i]:offs[i] + n]
            outs[4 * i][...] = g
            outs[4 * i + 1][...], outs[4 * i + 2][...], outs[4 * i + 3][...] = _adamw_math(
                w_refs[i][...], g, m_refs[i][...], v_refs[i][...])
        for e, n in enumerate(extra_widths):
            off = offs[npar + e]
            outs[4 * npar + e][...] = tot[:, off:off + n]

    shapes = [jax.ShapeDtypeStruct(w.shape, F32) for w in ws for _ in range(4)]
    shapes += [jax.ShapeDtypeStruct((1, n), F32) for n in extra_widths]
    return _pallas(
        body, [gathered, *ws, *ms, *vs], dep=dep, in_specs=[_VMEM] * (1 + 3 * npar), out_specs=[_VMEM] * len(shapes),
        out_shape=shapes, name="small_sum_adamw")


def _position():
    return lax.axis_index("x"), lax.axis_index("y"), lax.axis_index("c")


def _other_chips(x, y):
    return [(1 - x, y), (x, 1 - y), (1 - x, 1 - y)]


def allgather_small(block, name, dep=None):
    def body(x_ref, out_ref, send_sems, recv_sems, local_sem):
        x, y, c = _position()
        me, sibling = (x, y, c), (x, y, 1 - c)
        chips = _other_chips(x, y)

        def slot(px, py, pc):
            return out_ref.at[4 * px + 2 * py + pc]

        def copy(k, block_of, to, src=None):
            return pltpu.make_async_remote_copy(
                src_ref=slot(*block_of) if src is None else src, dst_ref=slot(*block_of),
                send_sem=send_sems.at[k], recv_sem=recv_sems.at[k], device_id=to, device_id_type=MESH)

        mine = pltpu.make_async_copy(x_ref, slot(*me), local_sem)
        mine.start()
        first = [copy(0, me, sibling, src=x_ref)]
        first += [copy(1 + j, me, (*chip, c), src=x_ref) for j, chip in enumerate(chips)]
        for cp in first:
            cp.start()
        passed = [copy(4 + j, (*chip, c), sibling) for j, chip in enumerate(chips)]
        for j, chip in enumerate(chips):
            copy(1 + j, (*chip, c), me).wait_recv()
            passed[j].start()
        copy(0, sibling, me).wait_recv()
        for j, chip in enumerate(chips):
            copy(4 + j, (*chip, 1 - c), me).wait_recv()
        for cp in first + passed:
            cp.wait_send()
        mine.wait()

    return _pallas(
        body, [block], dep=dep,
        out_shape=jax.ShapeDtypeStruct((N_DEV, *block.shape), block.dtype),
        in_specs=[_VMEM], out_specs=_VMEM,
        scratch=[pltpu.SemaphoreType.DMA((7,)), pltpu.SemaphoreType.DMA((7,)), pltpu.SemaphoreType.DMA],
        name=name)


class Started(NamedTuple):
    send_sems: Any
    recv_sems: Any
    bufs: list


def exchange_start_many(name, buf_sets, plans, dep=None):
    sizes = [len(bufs) for bufs in buf_sets]
    first = [sum(sizes[:i]) for i in range(len(sizes))]
    flat = [b for bufs in buf_sets for b in bufs]
    nb, npl = len(flat), len(plans)

    def body(*refs):
        for i, (s, _, plan) in enumerate(plans):
            for cp in plan(refs[first[s]:first[s] + sizes[s]], refs[nb + 2 * i], refs[nb + 2 * i + 1]):
                cp.start()

    sems = [pltpu.SemaphoreType.DMA((n,)) for _, n, _ in plans for _ in range(2)]
    outs = _pallas(
        body, [pltpu.with_memory_space_constraint(b, pltpu.HBM) for b in flat], dep=dep, name=name,
        out_shape=(*sems, *[pltpu.HBM(b.shape, b.dtype) for b in flat]),
        in_specs=[_HBM] * nb,
        out_specs=(*[_SEM] * (2 * npl), *[_HBM] * nb),
        input_output_aliases={i: 2 * npl + i for i in range(nb)},
        compiler_params=pltpu.CompilerParams(has_side_effects=_EFFECT))
    new_bufs = outs[2 * npl:]
    return [Started(outs[2 * i], outs[2 * i + 1], list(new_bufs[first[s]:first[s] + sizes[s]]))
            for i, (s, _, _) in enumerate(plans)]


def exchange_start(name, bufs, n_copies, plan, dep=None):
    return exchange_start_many(name, [bufs], [(0, n_copies, plan)], dep=dep)[0]


def exchange_wait(name, started, plan, bufs=None, dep=None):
    if bufs is not None:
        started = started._replace(bufs=list(bufs))
    nb = len(started.bufs)

    def body(*refs):
        for cp in plan(refs[:nb], refs[nb], refs[nb + 1]):
            cp.wait_send()
            cp.wait_recv()

    outs = _pallas(
        body, [*started.bufs, started.send_sems, started.recv_sems], dep=dep, name=name,
        out_shape=tuple(pltpu.HBM(b.shape, b.dtype) for b in started.bufs),
        in_specs=[_HBM] * nb + [_SEM, _SEM],
        out_specs=tuple([_HBM] * nb),
        input_output_aliases={i: i for i in range(nb)},
        compiler_params=pltpu.CompilerParams(has_side_effects=_EFFECT))
    return list(outs)


def _remote(src, dst, send_sems, recv_sems, i, to):
    return pltpu.make_async_remote_copy(src_ref=src, dst_ref=dst, send_sem=send_sems.at[i], recv_sem=recv_sems.at[i],
                                        device_id=to, device_id_type=MESH)


def _half_rows(buf_rows, chip_idx, pc):
    half = buf_rows // (2 * N_CHIPS)
    return pl.ds((2 * chip_idx + pc) * half, half)


ALL_PEERS = (0, 1, 2)


def plan_gather_ici(refs, send_sems, recv_sems, peers=ALL_PEERS):
    x, y, c = _position()
    chips = _other_chips(x, y)
    copies = []
    for k, ref in enumerate(refs):
        rows = ref.at[_half_rows(ref.shape[0], 2 * x + y, c), :]
        for i, j in enumerate(peers):
            copies.append(_remote(rows, rows, send_sems, recv_sems, len(peers) * k + i, (*chips[j], c)))
    return copies


def plan_gather_relay(refs, send_sems, recv_sems):
    x, y, c = _position()
    copies = []
    for k, ref in enumerate(refs):
        quarter = ref.shape[0] // (4 * N_CHIPS)
        for i, (src_chip, to) in enumerate((((1 - x, y), (x, 1 - y, c)), ((x, 1 - y), (1 - x, y, c)))):
            start = (2 * (2 * src_chip[0] + src_chip[1]) + c) * 2 * quarter + i * quarter
            rows = ref.at[pl.ds(start, quarter), :]
            copies.append(_remote(rows, rows, send_sems, recv_sems, 2 * k + i, to))
    return copies


def plan_gather_d2d(refs, send_sems, recv_sems, peers=ALL_PEERS):
    x, y, c = _position()
    chips = _other_chips(x, y)
    copies = []
    for k, ref in enumerate(refs):
        for i, j in enumerate(peers):
            px, py = chips[j]
            rows = ref.at[_half_rows(ref.shape[0], 2 * px + py, c), :]
            copies.append(_remote(rows, rows, send_sems, recv_sems, len(peers) * k + i, (x, y, 1 - c)))
    return copies


def plan_pair_exchange(refs, send_sems, recv_sems):
    x, y, c = _position()
    nw = len(refs) // 2
    copies = []
    for k in range(nw):
        for chip in range(N_CHIPS):
            copies.append(_remote(refs[k].at[chip, 1 - c], refs[nw + k].at[chip], send_sems, recv_sems,
                                  N_CHIPS * k + chip, (x, y, 1 - c)))
    return copies


def plan_chip_exchange(refs, send_sems, recv_sems):
    x, y, c = _position()
    nw = len(refs) // 2
    copies = []
    for k in range(nw):
        for j, (px, py) in enumerate(_other_chips(x, y)):
            copies.append(_remote(refs[k].at[2 * px + py], refs[nw + k].at[2 * x + y], send_sems, recv_sems,
                                  3 * k + j, (px, py, c)))
    return copies


def plan_pair_share(refs, send_sems, recv_sems):
    x, y, c = _position()
    return [_remote(ref.at[c], ref.at[c], send_sems, recv_sems, k, (x, y, 1 - c)) for k, ref in enumerate(refs)]


def cast_into_slot(src, slot, n_slots, name, dep=None):
    r, n = src.shape
    tr, tn = _ew_tiles(r, n, BF16_SUBLANES)

    def body(slot_ref, s_ref, o_ref):
        o_ref[...] = s_ref[...].astype(BF16)

    return _pallas(
        body, [slot, src], dep=dep, n_prefetch=1, grid=(r // tr, n // tn),
        in_specs=[pl.BlockSpec((tr, tn), lambda i, j, sl: (i, j))],
        out_specs=pl.BlockSpec((None, tr, tn), lambda i, j, sl: (sl[0], i, j)),
        out_shape=jax.ShapeDtypeStruct((n_slots, r, n), BF16),
        sem=("parallel", "parallel"), name=name)


def pair_sum(g, r, core, name, dep=None):
    nchip, _, h, n = g.shape
    th, tn = _ew_tiles(h, n, BF16_SUBLANES)

    def body(core_ref, g_ref, r_ref, o_ref):
        o_ref[...] = (g_ref[...].astype(F32) + r_ref[...].astype(F32)).astype(BF16)

    return _pallas(
        body, [core, g, r], dep=dep, n_prefetch=1, grid=(nchip, h // th, n // tn),
        in_specs=[pl.BlockSpec((None, None, th, tn), lambda a, i, j, cr: (a, cr[0], i, j)),
                  pl.BlockSpec((None, th, tn), lambda a, i, j, cr: (a, i, j))],
        out_specs=pl.BlockSpec((None, th, tn), lambda a, i, j, cr: (a, i, j)),
        out_shape=jax.ShapeDtypeStruct((nchip, h, n), BF16),
        sem=("parallel", "parallel", "parallel"), name=name)


def chip_sum(own, got, where, name, dep=None):
    nchip, h, n = got.shape
    th, tn = _ew_tiles(h, n, BF16_SUBLANES, elems=256 * 1024)

    def body(where_ref, own_ref, *rest):
        got_refs, o_ref = rest[:nchip], rest[nchip]
        chip = where_ref[0]
        acc = None
        for s in range(nchip):
            term = jnp.where(chip == s, own_ref[...], got_refs[s][...]).astype(F32)
            acc = term if acc is None else acc + term
        o_ref[...] = acc

    def got_spec(s):
        return pl.BlockSpec((None, th, tn), lambda i, j, wr: (jnp.where(wr[0] == s, (s + 1) % nchip, s), i, j))

    return _pallas(
        body, [where, own, *[got] * nchip], dep=dep, n_prefetch=1, grid=(h // th, n // tn),
        in_specs=[pl.BlockSpec((None, th, tn), lambda i, j, wr: (wr[0], i, j))]
        + [got_spec(s) for s in range(nchip)],
        out_specs=pl.BlockSpec((None, th, tn), lambda i, j, wr: (wr[1], i, j)),
        out_shape=jax.ShapeDtypeStruct((2, h, n), F32),
        sem=("parallel", "parallel"), name=name)


def kernel(x, c, w_ada, b_ada, norm_mix_g, w_in, q_norm_g, k_norm_g, attn_sinks, rel_bias, w_attn_out, conv_w, conv_b, conv_ln_g, conv_ln_b, w_conv_out, w_mix_out, norm_ffn_g, w_ffn_in, w_ffn_out, loss_target, m_w_ada, m_b_ada, m_norm_mix_g, m_w_in, m_q_norm_g, m_k_norm_g, m_attn_sinks, m_rel_bias, m_w_attn_out, m_conv_w, m_conv_b, m_conv_ln_g, m_conv_ln_b, m_w_conv_out, m_w_mix_out, m_norm_ffn_g, m_w_ffn_in, m_w_ffn_out, v_w_ada, v_b_ada, v_norm_mix_g, v_w_in, v_q_norm_g, v_k_norm_g, v_attn_sinks, v_rel_bias, v_w_attn_out, v_conv_w, v_conv_b, v_conv_ln_g, v_conv_ln_b, v_w_conv_out, v_w_mix_out, v_norm_ffn_g, v_w_ffn_in, v_w_ffn_out):
    run = InOrder()
    xi, yi, ci = _position()
    chip = 2 * xi + yi
    me = 2 * chip + ci
    chip_arr = chip.astype(jnp.int32).reshape(1)
    core_arr = ci.astype(jnp.int32).reshape(1)
    where_arr = jnp.stack([chip, ci]).astype(jnp.int32)

    xe, tgt = x[0], loss_target[0]
    t, d = xe.shape
    hd = q_norm_g.shape[-1]
    nq = attn_sinks.shape[-1]
    aw = nq * hd
    ch = conv_b.shape[-1]
    in_width = N_CHIPS * w_in.shape[-1]
    kvw = (in_width - aw - 2 * ch - 2 * d) // 2
    nkv = kvw // hd
    dff = N_CHIPS * w_ffn_out.shape[1]
    off_k, off_v, off_ca = aw, aw + kvw, aw + 2 * kvw
    off_cb, off_ga, off_gc = off_ca + ch, off_ca + 2 * ch, off_ca + 2 * ch + d
    nc_ada = w_ada.shape[-1]
    ch_loc = conv_w.shape[-1]
    nj_ffn = w_ffn_in.shape[-1]
    perm_ffn = ffn_perm(N_CHIPS)

    big = {"w_in": w_in[0], "w_attn_out": w_attn_out[0], "w_conv_out": w_conv_out[0], "w_mix_out": w_mix_out[0],
           "w_ffn_in": w_ffn_in[0], "w_ffn_out": w_ffn_out[0]}
    moments = {"w_in": (m_w_in, v_w_in), "w_attn_out": (m_w_attn_out, v_w_attn_out),
               "w_conv_out": (m_w_conv_out, v_w_conv_out), "w_mix_out": (m_w_mix_out, v_w_mix_out),
               "w_ffn_in": (m_w_ffn_in, v_w_ffn_in), "w_ffn_out": (m_w_ffn_out, v_w_ffn_out)}
    gather_groups = {"in": ["w_in"], "branch_out": ["w_attn_out", "w_conv_out"], "mix_out": ["w_mix_out"],
                     "ffn_in": ["w_ffn_in"], "ffn_out": ["w_ffn_out"]}
    grads, deltas, new_m, new_v = {}, {}, {}, {}

    def gather_cast(gname):
        bufs = []
        for n in gather_groups[gname]:
            r, ncol = big[n].shape
            bufs.append(run(cast_into_slot, big[n], chip_arr, N_CHIPS, "cast_" + n).reshape(N_CHIPS * r, ncol))
        return bufs

    def gather_ici_start(gname, bufs):
        return run(exchange_start, "gather_ici_start_" + gname, bufs, 3 * len(bufs), plan_gather_ici)

    def gather_pass_on(gname, ici):
        landed = run(exchange_wait, "gather_ici_wait_" + gname, ici, plan_gather_ici)
        return run(exchange_start, "gather_d2d_start_" + gname, landed, 3 * len(landed), plan_gather_d2d)

    def gathered(gname, d2d):
        outs = run(exchange_wait, "gather_d2d_wait_" + gname, d2d, plan_gather_d2d)
        return [o.reshape(N_CHIPS, *big[n].shape) for o, n in zip(outs, gather_groups[gname])]

    def rs_pair_start(gname, names, partials):
        blocks = [g.reshape(N_CHIPS, 2, big[n].shape[0] // 2, big[n].shape[1]) for n, g in zip(names, partials)]
        land = [lax.empty((N_CHIPS,) + b.shape[2:], BF16) for b in blocks]
        return run(exchange_start, "pair_exchange_start_" + gname, blocks + land, N_CHIPS * len(blocks),
                   plan_pair_exchange)

    def rs_chip_start(gname, names, pair):
        nw = len(names)
        outs = run(exchange_wait, "pair_exchange_wait_" + gname, pair, plan_pair_exchange)
        sums = [run(pair_sum, g, r, core_arr, "pair_sum_" + n) for n, g, r in zip(names, outs[:nw], outs[nw:])]
        land = [lax.empty(s.shape, BF16) for s in sums]
        return run(exchange_start, "chip_exchange_start_" + gname, sums + land, 3 * nw, plan_chip_exchange)

    def rs_share_start(gname, names, chipx):
        nw = len(names)
        outs = run(exchange_wait, "chip_exchange_wait_" + gname, chipx, plan_chip_exchange)
        halves = [run(chip_sum, s, r, where_arr, "chip_sum_" + n) for n, s, r in zip(names, outs[:nw], outs[nw:])]
        return run(exchange_start, "pair_share_start_" + gname, halves, nw, plan_pair_share)

    def rs_finish(gname, names, share):
        fulls = run(exchange_wait, "pair_share_wait_" + gname, share, plan_pair_share)
        for n, g2 in zip(names, fulls):
            g, dl, nm, nv = run(adamw, big[n], g2.reshape(big[n].shape), moments[n][0][0], moments[n][1][0],
                                "adamw_" + n, copy_grad=True)
            grads[n], deltas[n], new_m[n], new_v[n] = g[None], dl[None], nm[None], nv[None]

    near, far = (0, 1), (2,)
    plan_ici_near = functools.partial(plan_gather_ici, peers=near)
    plan_ici_far, n_far = plan_gather_relay, 2
    plan_d2d_near = functools.partial(plan_gather_d2d, peers=near)
    plan_d2d_far = functools.partial(plan_gather_d2d, peers=far)
    bufs_in = gather_cast("in")
    row1, offs1 = _row_pack([c, conv_w[0].reshape(1, CONV_WIDTH * ch_loc)])
    got1 = run(allgather_small, row1, "allgather_cond")
    ici_near = run(exchange_start, "gather_ici_start_in_near", bufs_in, len(near), plan_ici_near)
    c_all = got1[:, 0, :d]
    conv_w_full = got1[0::2, 0, offs1[1]:offs1[1] + CONV_WIDTH * ch_loc].reshape(N_CHIPS, CONV_WIDTH, ch_loc)
    conv_w_full = jnp.transpose(conv_w_full, (1, 0, 2)).reshape(CONV_WIDTH, ch)
    conv_w_pad = jnp.pad(conv_w_full, ((0, 1), (0, 0)))
    c_t = jnp.transpose(c_all)
    mod_cols = run(ada_fwd, c_t, w_ada[0])
    rest_bufs = {gname: gather_cast(gname) for gname in gather_groups if gname != "in"}
    bucket = _t5_bucket_table()
    bucket_p, bucket_c = jnp.asarray(bucket[:, :BLOCK]), jnp.asarray(bucket[:, BLOCK:])
    bias_p, bias_c = run(bias_table, rel_bias, bucket_p, bucket_c)
    got2 = run(allgather_small, mod_cols, "allgather_mod")
    mod_all = got2.reshape(N_CHIPS, 2, N_DEV, nc_ada)[:, 0]
    mod = lax.dynamic_slice_in_dim(mod_all, me, 1, axis=1).reshape(1, N_CHIPS * nc_ada) + b_ada
    mod = jnp.pad(mod.reshape(N_MOD, d), ((0, SUBLANES - N_MOD), (0, 0)))

    landed = run(exchange_wait, "gather_ici_wait_in_near", ici_near, plan_ici_near)
    ici_far, d2d_near = run(exchange_start_many, "gather_start_in_far", [landed],
                            [(0, n_far, plan_ici_far), (0, len(near), plan_d2d_near)])
    h = run(pre_mix_fwd, xe, mod, norm_mix_g)
    ici = {}
    ici["branch_out"], ici["mix_out"], ici_near_ffn = run(
        exchange_start_many, "gather_ici_start_mid", [rest_bufs["branch_out"], rest_bufs["mix_out"], rest_bufs["ffn_in"]],
        [(0, 3 * len(rest_bufs["branch_out"]), plan_gather_ici), (1, 3, plan_gather_ici), (2, len(near), plan_ici_near)])
    landed = run(exchange_wait, "gather_d2d_wait_in_near", d2d_near, plan_d2d_near)
    landed = run(exchange_wait, "gather_ici_wait_in_far", ici_far, plan_ici_far, bufs=landed)
    d2d_far = run(exchange_start, "gather_d2d_start_in_far", landed, len(far), plan_d2d_far)
    landed = run(exchange_wait, "gather_d2d_wait_in_far", d2d_far, plan_d2d_far)
    wg_in = landed[0].reshape(N_CHIPS, *big["w_in"].shape)
    p = run(mm_nn, h, wg_in, tn=wg_in.shape[2], tk=d, out_dtype=BF16, name="mm_in")
    d2d_branch = gather_pass_on("branch_out", ici["branch_out"])

    sinks3 = attn_sinks.reshape(nq, 1, 1)
    attn_o = run(attn_fwd, p, bias_p, bias_c, sinks3, q_norm_g, k_norm_g, aw=aw, kvw=kvw)
    ca, cb = p[:, off_ca:off_cb], p[:, off_cb:off_ga]
    s_conv, co_conv = run(conv_fwd, ca, cb, conv_w_pad, conv_b, conv_ln_g, conv_ln_b)
    wg_attn_out, wg_conv_out = gathered("branch_out", d2d_branch)
    d2d_mix = gather_pass_on("mix_out", ici["mix_out"])
    y_attn, y_conv, merged = run(branch_out_merge, attn_o, s_conv, wg_attn_out, wg_conv_out, p, off_ga, off_gc)
    landed = run(exchange_wait, "gather_ici_wait_ffn_in_near", ici_near_ffn, plan_ici_near)
    ici_far_ffn, d2d_near_ffn, ici["ffn_out"] = run(
        exchange_start_many, "gather_start_ffn_in_far", [landed, rest_bufs["ffn_out"]],
        [(0, n_far, plan_ici_far), (0, len(near), plan_d2d_near), (1, 3, plan_gather_ici)])
    (wg_mix_out,) = gathered("mix_out", d2d_mix)
    wg_mix_out = wg_mix_out.reshape(1, d, d)
    o_m = run(mm_nn, merged, wg_mix_out, tn=_tile(d, 512), tk=d, out_dtype=BF16, name="mm_mix_out")
    landed = run(exchange_wait, "gather_d2d_wait_ffn_in_near", d2d_near_ffn, plan_d2d_near)
    landed = run(exchange_wait, "gather_ici_wait_ffn_in_far", ici_far_ffn, plan_ici_far, bufs=landed)
    d2d_far_ffn = run(exchange_start, "gather_d2d_start_ffn_in_far", landed, len(far), plan_d2d_far)
    x1, h2 = run(pre_ffn_fwd, xe, o_m, mod, norm_ffn_g)
    landed = run(exchange_wait, "gather_d2d_wait_ffn_in_far", d2d_far_ffn, plan_d2d_far)
    wg_ffn_in = landed[0].reshape(N_CHIPS, *big["w_ffn_in"].shape)
    f = run(mm_nn, h2, wg_ffn_in, tn=_tile(nj_ffn, 1408), tk=d, out_dtype=BF16, name="mm_ffn_in", perm=perm_ffn)
    d2d_ffn_out = gather_pass_on("ffn_out", ici["ffn_out"])
    act = run(swiglu_fwd, f, nj_ffn)
    (wg_ffn_out,) = gathered("ffn_out", d2d_ffn_out)
    wg_ffn_out = wg_ffn_out.reshape(1, dff, d)
    o_f = run(mm_nn, act, wg_ffn_out, tn=_tile(d, 512), tk=_tile(dff, 2816), out_dtype=BF16, name="mm_ffn_out")
    loss11, dy, dof, acc_l = run(loss_head, x1, o_f, tgt, mod)

    gw_ffn_out = run(mm_tn, act, dof, 1, tk=_tile(dff, 512), tn=d, name="mm_ffn_out_dw")
    px_ffn_out = rs_pair_start("ffn_out", ["w_ffn_out"], [gw_ffn_out])
    dact = run(mm_nt, dof, wg_ffn_out, tko=_tile(dff, 512), tn=d, out_dtype=BF16, name="mm_ffn_out_dx")
    cx_ffn_out = rs_chip_start("ffn_out", ["w_ffn_out"], px_ffn_out)
    df = run(swiglu_bwd, f, dact, nj_ffn)
    gw_ffn_in = run(mm_tn, h2, df, N_CHIPS, tk=d, tn=_tile(nj_ffn, 1408), name="mm_ffn_in_dw",
                    perm=perm_ffn)
    px_ffn_in = rs_pair_start("ffn_in", ["w_ffn_in"], [gw_ffn_in])
    dh2 = run(mm_nt, df, wg_ffn_in, tko=_tile(d, 512), tn=nj_ffn, out_dtype=BF16, name="mm_ffn_in_dx", perm=perm_ffn)
    sh_ffn_out = rs_share_start("ffn_out", ["w_ffn_out"], cx_ffn_out)
    cx_ffn_in = rs_chip_start("ffn_in", ["w_ffn_in"], px_ffn_in)
    dx1, dom, acc_f = run(pre_ffn_bwd, x1, dh2, dy, o_m, mod, norm_ffn_g)
    gw_mix_out = run(mm_tn, merged, dom, 1, tk=d, tn=_tile(d, 1024), name="mm_mix_out_dw")
    px_mix = rs_pair_start("mix_out", ["w_mix_out"], [gw_mix_out])
    dy_attn, dy_conv, dga, dgc = run(mix_out_dx_merge_bwd, dom, wg_mix_out.reshape(d, d), p, y_attn, y_conv,
                                     off_ga, off_gc)
    rs_finish("ffn_out", ["w_ffn_out"], sh_ffn_out)
    cx_mix = rs_chip_start("mix_out", ["w_mix_out"], px_mix)
    gw_attn_out = run(mm_tn, attn_o, dy_attn, N_CHIPS, tk=aw, tn=_tile(wg_attn_out.shape[2], 512),
                      name="mm_attn_out_dw")
    gw_conv_out = run(mm_tn, s_conv, dy_conv, N_CHIPS, tk=ch, tn=_tile(wg_conv_out.shape[2], 512),
                      name="mm_conv_out_dw")
    ac_names = ["w_attn_out", "w_conv_out"]
    px_ac = rs_pair_start("attn_conv_out", ac_names, [gw_attn_out, gw_conv_out])
    dattn_o = run(mm_nt, dy_attn, wg_attn_out, tko=_tile(aw, 1024), tn=_tile(wg_attn_out.shape[2], 512),
                  out_dtype=BF16, name="mm_attn_out_dx")
    ds_conv = run(mm_nt, dy_conv, wg_conv_out, tko=_tile(ch, 1024), tn=_tile(wg_conv_out.shape[2], 512),
                  out_dtype=BF16, name="mm_conv_out_dx")
    cx_ac = rs_chip_start("attn_conv_out", ac_names, px_ac)
    dca, dcb, dconv_w, dconv_vec = run(conv_bwd, ca, cb, co_conv, ds_conv, conv_w_pad, conv_ln_g, conv_ln_b)
    sh_ffn_in = rs_share_start("ffn_in", ["w_ffn_in"], cx_ffn_in)
    dqkv, dbp, dbc, dsinks, dqg, dkg = run(attn_bwd, p, bias_p, bias_c, sinks3, q_norm_g, k_norm_g, dattn_o,
                                           aw=aw, kvw=kvw)
    sh_mix = rs_share_start("mix_out", ["w_mix_out"], cx_mix)
    sh_ac = rs_share_start("attn_conv_out", ac_names, cx_ac)
    drel = run(bias_table_bwd, dbp, dbc, bucket_p, bucket_c).reshape(NUM_BUCKETS, nq)
    dp = jnp.concatenate([dqkv, dca, dcb, dga, dgc], axis=1)
    gw_in = run(mm_tn, h, dp, N_CHIPS, tk=d, tn=wg_in.shape[2], name="mm_in_dw")
    px_in = rs_pair_start("in", ["w_in"], [gw_in])
    dh = run(mm_nt, dp, wg_in, tko=_tile(d, 1024), tn=wg_in.shape[2], out_dtype=BF16, name="mm_in_dx")
    grad_x, acc_m = run(pre_mix_bwd, xe, dh, dx1, mod, norm_mix_g)

    dmod = jnp.concatenate([acc_m[0:1], acc_m[1:2], acc_f[3:4], acc_f[0:1], acc_f[1:2], acc_l[0:1]], axis=1)
    small_names = ["b_ada", "norm_mix_g", "q_norm_g", "k_norm_g", "attn_sinks", "rel_bias", "conv_b", "conv_ln_g",
                   "conv_ln_b", "norm_ffn_g"]
    small_w = [b_ada, norm_mix_g, q_norm_g, k_norm_g, attn_sinks, rel_bias, conv_b, conv_ln_g, conv_ln_b, norm_ffn_g]
    small_m = [m_b_ada, m_norm_mix_g, m_q_norm_g, m_k_norm_g, m_attn_sinks, m_rel_bias, m_conv_b, m_conv_ln_g,
               m_conv_ln_b, m_norm_ffn_g]
    small_v = [v_b_ada, v_norm_mix_g, v_q_norm_g, v_k_norm_g, v_attn_sinks, v_rel_bias, v_conv_b, v_conv_ln_g,
               v_conv_ln_b, v_norm_ffn_g]
    small_g = [dmod, acc_m[2:3], dqg, dkg, dsinks.reshape(1, nq), drel.reshape(1, NUM_BUCKETS * nq),
               dconv_vec[0:1], dconv_vec[1:2], dconv_vec[2:3], acc_f[2:3]]
    row3, offs3 = _row_pack(small_g + [dconv_w[:CONV_WIDTH].reshape(1, CONV_WIDTH * ch), loss11])
    got3 = run(allgather_small, row3, "allgather_small_grads")
    cx_in = rs_chip_start("in", ["w_in"], px_in)
    as_row = lambda a: a.reshape(1, -1)
    outs3 = run(small_sum_adamw, got3, offs3, [as_row(a) for a in small_w], [as_row(a) for a in small_m],
                [as_row(a) for a in small_v], [CONV_WIDTH * ch, 1])
    for i, (n, w) in enumerate(zip(small_names, small_w)):
        grads[n], deltas[n], new_m[n], new_v[n] = (o.reshape(w.shape) for o in outs3[4 * i:4 * i + 4])
    g_conv_w_all, loss_sum = outs3[-2].reshape(CONV_WIDTH, ch), outs3[-1]

    g_conv_w = lax.dynamic_slice_in_dim(g_conv_w_all, chip * ch_loc, ch_loc, axis=1)
    grads["conv_w"] = g_conv_w[None]
    dl, nm, nv = run(adamw, conv_w[0], g_conv_w, m_conv_w[0], v_conv_w[0], "adamw_conv_w")
    deltas["conv_w"], new_m["conv_w"], new_v["conv_w"] = dl[None], nm[None], nv[None]

    dmod_all = got3[:, 0, :N_MOD * d]
    dmod_cols = lax.dynamic_slice_in_dim(dmod_all, chip * nc_ada, nc_ada, axis=1)
    g_ada, dl, nm, nv = run(ada_grad_adamw, c_t, dmod_cols, w_ada[0], m_w_ada[0], v_w_ada[0])
    grads["w_ada"], deltas["w_ada"], new_m["w_ada"], new_v["w_ada"] = g_ada[None], dl[None], nm[None], nv[None]

    rs_finish("ffn_in", ["w_ffn_in"], sh_ffn_in)
    rs_finish("mix_out", ["w_mix_out"], sh_mix)
    rs_finish("attn_conv_out", ac_names, sh_ac)
    sh_in = rs_share_start("in", ["w_in"], cx_in)
    rs_finish("in", ["w_in"], sh_in)

    loss = loss_sum[0, 0]
    order = ["w_ada", "b_ada", "norm_mix_g", "w_in", "q_norm_g", "k_norm_g", "attn_sinks", "rel_bias", "w_attn_out",
             "conv_w", "conv_b", "conv_ln_g", "conv_ln_b", "w_conv_out", "w_mix_out", "norm_ffn_g", "w_ffn_in",
             "w_ffn_out"]
    return (loss, grad_x[None], *[grads[n] for n in order], *[deltas[n] for n in order],
            *[new_m[n] for n in order], *[new_v[n] for n in order])
```

```python
import functools
import math
from typing import Any, NamedTuple

import jax
import jax.numpy as jnp
import numpy as np
from jax import lax
from jax.experimental import pallas as pl
from jax.experimental.pallas import tpu as pltpu

F32 = jnp.float32
BF16 = jnp.bfloat16
MESH = pl.DeviceIdType.MESH

V7X_VMEM_BYTES = 64 * 1024 * 1024
VMEM_LIMIT = V7X_VMEM_BYTES - 8 * 1024 * 1024
LANES = 128
SUBLANES = 8
BF16_SUBLANES = 16

EPS = 1e-6
WINDOW = 128
BLOCK = 128
NUM_BUCKETS = 32
MAX_EXACT = NUM_BUCKETS // 2
MAX_DISTANCE = 128
CONV_WIDTH = 31
CONV_HALO = 32
ADAM_LR = 0.001
ADAM_B1 = 0.9
ADAM_B2 = 0.999
ADAM_EPS = 1e-08
ADAM_WD = 0.01
ADAM_STEP = 10
N_MOD = 6
SH_M, SC_M, GT_M, SH_F, SC_F, GT_F = range(6)

N_CHIPS = 4
N_DEV = 8

_ANY = pl.BlockSpec(memory_space=pl.ANY)
_VMEM = pl.BlockSpec(memory_space=pltpu.VMEM)
_SMEM = pl.BlockSpec(memory_space=pltpu.SMEM)
_HBM = pl.BlockSpec(memory_space=pltpu.HBM)
_SEM = pl.BlockSpec(memory_space=pltpu.SEMAPHORE)
_EFFECT = pltpu.SideEffectType.DATAFLOW_SIDE_EFFECTING


class InOrder:
    def __init__(self):
        self.token = None

    def __call__(self, fn, *args, **kw):
        return fn(*args, dep=self, **kw)


def _pallas(body, args, *, in_specs, out_specs, out_shape, name, dep=None, grid=(), n_prefetch=0, scratch=(),
            sem=None, **kw):
    n_lead = n_prefetch + len(in_specs)
    in_specs, args = list(in_specs), list(args)
    single = not isinstance(out_shape, (list, tuple))
    out_shapes = [out_shape] if single else list(out_shape)
    out_specs = [out_specs] if single else list(out_specs)
    if dep is not None:
        inner, n_out, takes = body, len(out_shapes), dep.token is not None

        def body(*refs):
            rest = refs[n_lead + (1 if takes else 0):]
            rest[n_out][...] = jnp.zeros((SUBLANES, LANES), F32)
            return inner(*refs[:n_lead], *rest[:n_out], *rest[n_out + 1:])

        if takes:
            in_specs.append(_ANY)
            args.append(dep.token)
        out_shapes.append(jax.ShapeDtypeStruct((SUBLANES, LANES), F32))
        out_specs.append(pl.BlockSpec((SUBLANES, LANES), lambda *_: (0, 0)))
    params = kw.pop("compiler_params", None)
    if params is None:
        params = pltpu.CompilerParams(dimension_semantics=sem, vmem_limit_bytes=VMEM_LIMIT)
    outs = pl.pallas_call(
        body,
        grid_spec=pltpu.PrefetchScalarGridSpec(num_scalar_prefetch=n_prefetch, grid=grid, in_specs=in_specs,
                                               out_specs=out_specs, scratch_shapes=list(scratch)),
        out_shape=out_shapes, compiler_params=params, name=name, **kw,
    )(*args)
    if dep is not None:
        dep.token = outs[-1]
        outs = outs[:-1]
    return outs[0] if single else list(outs)


def _tile(n, pref, unit=LANES):
    best = None
    for t in range(unit, min(n, pref) + 1, unit):
        if n % t == 0:
            best = t
    return best if best is not None else n


def _sigmoid(v):
    return 1.0 / (1.0 + jnp.exp(-v.astype(F32)))


ROW_CHUNK = 512


def _row_chunks(m, unit=SUBLANES):
    step = _tile(m, ROW_CHUNK, unit)
    return [(s, step) for s in range(0, m, step)]


def _ew_tiles(r, n, unit=SUBLANES, elems=512 * 1024):
    return _tile(r, max(unit, elems // n), unit), n


def _block_pos(j, perm):
    if perm is None:
        return j
    pos = 0
    for a, p in enumerate(perm):
        pos = pos + jnp.where(j == a, p, 0)
    return pos


def mm_nn(a, w, *, tn, tk, out_dtype, name, perm=None, dep=None):
    m, k = a.shape
    j, k2, nj = w.shape
    assert k == k2 and nj % tn == 0 and k % tk == 0
    npj, nk = nj // tn, k // tk

    def body(a_ref, w_ref, o_ref, *scratch):
        kk = pl.program_id(1)
        for s, sz in _row_chunks(m):
            rows = pl.ds(s, sz)
            p = jnp.dot(a_ref[rows, :], w_ref[...], preferred_element_type=F32)
            if nk == 1:
                o_ref[rows, :] = p.astype(out_dtype)
            else:
                acc = scratch[0]

                @pl.when(kk == 0)
                def _():
                    acc[rows, :] = p

                @pl.when(kk > 0)
                def _():
                    acc[rows, :] += p

                @pl.when(kk == nk - 1)
                def _():
                    o_ref[rows, :] = acc[rows, :].astype(out_dtype)

    return _pallas(
        body, [a, w], dep=dep, grid=(j * npj, nk),
        in_specs=[
            pl.BlockSpec((m, tk), lambda n, kk: (0, kk)),
            pl.BlockSpec((None, tk, tn), lambda n, kk: (n // npj, kk, n % npj)),
        ],
        out_specs=pl.BlockSpec((m, tn), lambda n, kk: (0, _block_pos(n // npj, perm) * npj + n % npj)),
        out_shape=jax.ShapeDtypeStruct((m, j * nj), out_dtype),
        scratch=[pltpu.VMEM((m, tn), F32)] if nk > 1 else [],
        sem=("parallel", "arbitrary"), name=name)


def mm_nt(g, w, *, tko, tn, name, out_dtype=F32, perm=None, dep=None):
    m, n = g.shape
    j, k, nj = w.shape
    assert n == j * nj and nj % tn == 0 and k % tko == 0
    npj, nr = nj // tn, n // tn
    in_place = out_dtype == F32

    def body(g_ref, w_ref, o_ref, *scratch):
        r = pl.program_id(1)
        acc = o_ref if in_place else (scratch[0] if nr > 1 else None)
        for s, sz in _row_chunks(m):
            rows = pl.ds(s, sz)
            p = lax.dot_general(g_ref[rows, :], w_ref[...], (((1,), (1,)), ((), ())), preferred_element_type=F32)
            if acc is None:
                o_ref[rows, :] = p.astype(out_dtype)
                continue

            @pl.when(r == 0)
            def _():
                acc[rows, :] = p

            @pl.when(r > 0)
            def _():
                acc[rows, :] += p

            if not in_place:
                @pl.when(r == nr - 1)
                def _():
                    o_ref[rows, :] = acc[rows, :].astype(out_dtype)

    return _pallas(
        body, [g, w], dep=dep, grid=(k // tko, nr),
        in_specs=[
            pl.BlockSpec((m, tn), lambda ko, r: (0, _block_pos(r // npj, perm) * npj + r % npj)),
            pl.BlockSpec((None, tko, tn), lambda ko, r: (r // npj, ko, r % npj)),
        ],
        out_specs=pl.BlockSpec((m, tko), lambda ko, r: (0, ko)),
        out_shape=jax.ShapeDtypeStruct((m, k), out_dtype),
        scratch=[pltpu.VMEM((m, tko), F32)] if (nr > 1 and not in_place) else [],
        sem=("parallel", "arbitrary"), name=name)


def mm_tn(a, g, n_blocks, *, tk, tn, name, perm=None, dep=None):
    m, k = a.shape
    m2, n = g.shape
    nj = n // n_blocks
    assert m == m2 and nj % tn == 0 and k % tk == 0
    npj = nj // tn

    def body(a_ref, g_ref, o_ref):
        for s, sz in _row_chunks(tk, LANES):
            p = lax.dot_general(a_ref[:, pl.ds(s, sz)], g_ref[...], (((0,), (0,)), ((), ())),
                                preferred_element_type=F32)
            o_ref[pl.ds(s, sz), :] = p.astype(BF16)

    return _pallas(
        body, [a, g], dep=dep, grid=(k // tk, n // tn),
        in_specs=[
            pl.BlockSpec((m, tk), lambda kk, nn: (0, kk)),
            pl.BlockSpec((m, tn), lambda kk, nn: (0, _block_pos(nn // npj, perm) * npj + nn % npj)),
        ],
        out_specs=pl.BlockSpec((None, tk, tn), lambda kk, nn: (nn // npj, kk, nn % npj)),
        out_shape=jax.ShapeDtypeStruct((n_blocks, k, nj), BF16),
        sem=("parallel", "parallel"), name=name)


ROW_TILE = 256


def _row_spec(tr, width):
    return pl.BlockSpec((tr, width), lambda i: (i, 0))


def _full_spec(shape):
    return pl.BlockSpec(shape, lambda *_: (0,) * len(shape))


def _rms(xv):
    return lax.rsqrt(jnp.mean(xv * xv, axis=-1, keepdims=True) + EPS)


def _mod_row(mod_ref, row):
    return mod_ref[pl.ds(row, 1), :]


def pre_mix_fwd(x, mod, gain, dep=None):
    t, d = x.shape
    tr = _tile(t, ROW_TILE, SUBLANES)

    def body(x_ref, mod_ref, g_ref, h_ref):
        xv = x_ref[...]
        y = xv * _rms(xv) * g_ref[...]
        h_ref[...] = (y * (1.0 + _mod_row(mod_ref, SC_M)) + _mod_row(mod_ref, SH_M)).astype(BF16)

    return _pallas(
        body, [x, mod, gain], dep=dep, grid=(t // tr,),
        in_specs=[_row_spec(tr, d), _full_spec(mod.shape), _full_spec(gain.shape)],
        out_specs=_row_spec(tr, d),
        out_shape=jax.ShapeDtypeStruct((t, d), BF16),
        sem=("parallel",), name="pre_mix_fwd")


def pre_ffn_fwd(x, o_m, mod, gain, dep=None):
    t, d = x.shape
    tr = _tile(t, ROW_TILE, SUBLANES)

    def body(x_ref, om_ref, mod_ref, g_ref, x1_ref, h_ref):
        x1 = x_ref[...] + _mod_row(mod_ref, GT_M) * om_ref[...]
        x1_ref[...] = x1
        y = x1 * _rms(x1) * g_ref[...]
        h_ref[...] = (y * (1.0 + _mod_row(mod_ref, SC_F)) + _mod_row(mod_ref, SH_F)).astype(BF16)

    return _pallas(
        body, [x, o_m, mod, gain], dep=dep, grid=(t // tr,),
        in_specs=[_row_spec(tr, d), _row_spec(tr, d), _full_spec(mod.shape), _full_spec(gain.shape)],
        out_specs=[_row_spec(tr, d), _row_spec(tr, d)],
        out_shape=[jax.ShapeDtypeStruct((t, d), F32), jax.ShapeDtypeStruct((t, d), BF16)],
        sem=("parallel",), name="pre_ffn_fwd")


def loss_head(x1, o_f, target, mod, dep=None):
    t, d = x1.shape
    tr = _tile(t, ROW_TILE, SUBLANES)

    def body(x1_ref, of_ref, tg_ref, mod_ref, loss_ref, dy_ref, dof_ref, acc_ref):
        i = pl.program_id(0)
        gt = _mod_row(mod_ref, GT_F)
        of = of_ref[...].astype(F32)
        err = x1_ref[...] + gt * of - tg_ref[...]
        dy = err * (1.0 / d)
        dy_ref[...] = dy.astype(BF16)
        dof_ref[...] = (dy * gt).astype(BF16)
        part = (0.5 / d) * jnp.sum(jnp.sum(err * err, axis=1, keepdims=True), axis=0, keepdims=True)
        dgt = jnp.sum(dy * of, axis=0, keepdims=True)

        @pl.when(i == 0)
        def _():
            loss_ref[...] = jnp.zeros_like(loss_ref)
            acc_ref[...] = jnp.zeros_like(acc_ref)

        loss_ref[...] += part
        acc_ref[pl.ds(0, 1), :] += dgt

    return _pallas(
        body, [x1, o_f, target, mod], dep=dep, grid=(t // tr,),
        in_specs=[_row_spec(tr, d), _row_spec(tr, d), _row_spec(tr, d), _full_spec(mod.shape)],
        out_specs=[_full_spec((1, 1)), _row_spec(tr, d), _row_spec(tr, d), _full_spec((SUBLANES, d))],
        out_shape=[jax.ShapeDtypeStruct((1, 1), F32), jax.ShapeDtypeStruct((t, d), BF16),
                   jax.ShapeDtypeStruct((t, d), BF16), jax.ShapeDtypeStruct((SUBLANES, d), F32)],
        sem=("arbitrary",), name="loss_head")


def _norm_bwd(xv, dh, sc, gain):
    rstd = _rms(xv)
    yn = xv * rstd
    dsh = jnp.sum(dh, axis=0, keepdims=True)
    dsc = jnp.sum(dh * (yn * gain), axis=0, keepdims=True)
    dgain = jnp.sum(dh * (1.0 + sc) * yn, axis=0, keepdims=True)
    dyn = dh * ((1.0 + sc) * gain)
    dx = rstd * (dyn - yn * jnp.mean(dyn * yn, axis=-1, keepdims=True))
    return dx, dsh, dsc, dgain


def pre_ffn_bwd(x1, dh2, dy, o_m, mod, gain, dep=None):
    t, d = x1.shape
    tr = _tile(t, ROW_TILE, SUBLANES)

    def body(x1_ref, dh_ref, dy_ref, om_ref, mod_ref, g_ref, dx1_ref, dom_ref, acc_ref):
        i = pl.program_id(0)
        dxn, dsh, dsc, dgain = _norm_bwd(x1_ref[...], dh_ref[...].astype(F32), _mod_row(mod_ref, SC_F), g_ref[...])
        dx1 = dy_ref[...] + dxn
        dx1_ref[...] = dx1
        dom_ref[...] = (dx1 * _mod_row(mod_ref, GT_M)).astype(BF16)
        dgt = jnp.sum(dx1 * om_ref[...], axis=0, keepdims=True)

        @pl.when(i == 0)
        def _():
            acc_ref[...] = jnp.zeros_like(acc_ref)

        acc_ref[pl.ds(0, 1), :] += dsh
        acc_ref[pl.ds(1, 1), :] += dsc
        acc_ref[pl.ds(2, 1), :] += dgain
        acc_ref[pl.ds(3, 1), :] += dgt

    return _pallas(
        body, [x1, dh2, dy, o_m, mod, gain], dep=dep, grid=(t // tr,),
        in_specs=[_row_spec(tr, d)] * 4 + [_full_spec(mod.shape), _full_spec(gain.shape)],
        out_specs=[_row_spec(tr, d), _row_spec(tr, d), _full_spec((SUBLANES, d))],
        out_shape=[jax.ShapeDtypeStruct((t, d), F32), jax.ShapeDtypeStruct((t, d), BF16),
                   jax.ShapeDtypeStruct((SUBLANES, d), F32)],
        sem=("arbitrary",), name="pre_ffn_bwd")


def pre_mix_bwd(x, dh, dx1, mod, gain, dep=None):
    t, d = x.shape
    tr = _tile(t, ROW_TILE, SUBLANES)

    def body(x_ref, dh_ref, dx1_ref, mod_ref, g_ref, gx_ref, acc_ref):
        i = pl.program_id(0)
        dxn, dsh, dsc, dgain = _norm_bwd(x_ref[...], dh_ref[...].astype(F32), _mod_row(mod_ref, SC_M), g_ref[...])
        gx_ref[...] = dx1_ref[...] + dxn

        @pl.when(i == 0)
        def _():
            acc_ref[...] = jnp.zeros_like(acc_ref)

        acc_ref[pl.ds(0, 1), :] += dsh
        acc_ref[pl.ds(1, 1), :] += dsc
        acc_ref[pl.ds(2, 1), :] += dgain

    return _pallas(
        body, [x, dh, dx1, mod, gain], dep=dep, grid=(t // tr,),
        in_specs=[_row_spec(tr, d)] * 3 + [_full_spec(mod.shape), _full_spec(gain.shape)],
        out_specs=[_row_spec(tr, d), _full_spec((SUBLANES, d))],
        out_shape=[jax.ShapeDtypeStruct((t, d), F32), jax.ShapeDtypeStruct((SUBLANES, d), F32)],
        sem=("arbitrary",), name="pre_mix_bwd")


def branch_out_merge(attn_o, s_conv, w_attn, w_conv, p, off_ga, off_gc, dep=None):
    t = attn_o.shape[0]
    j, ka, nj = w_attn.shape
    kc = w_conv.shape[1]
    assert off_ga % nj == 0 and off_gc % nj == 0

    def body(a_ref, s_ref, wa_ref, wc_ref, ga_ref, gc_ref, ya_ref, yc_ref, m_ref):
        for s, sz in _row_chunks(t):
            rows = pl.ds(s, sz)
            ya = jnp.dot(a_ref[rows, :], wa_ref[...], preferred_element_type=F32)
            yc = jnp.dot(s_ref[rows, :], wc_ref[...], preferred_element_type=F32)
            ya_ref[rows, :] = ya.astype(BF16)
            yc_ref[rows, :] = yc.astype(BF16)
            m_ref[rows, :] = (_sigmoid(ga_ref[rows, :]) * ya + _sigmoid(gc_ref[rows, :]) * yc).astype(BF16)

    col = pl.BlockSpec((t, nj), lambda b: (0, b))
    return _pallas(
        body, [attn_o, s_conv, w_attn, w_conv, p, p], dep=dep, grid=(j,),
        in_specs=[pl.BlockSpec((t, ka), lambda b: (0, 0)), pl.BlockSpec((t, kc), lambda b: (0, 0)),
                  pl.BlockSpec((None, ka, nj), lambda b: (b, 0, 0)), pl.BlockSpec((None, kc, nj), lambda b: (b, 0, 0)),
                  pl.BlockSpec((t, nj), lambda b: (0, off_ga // nj + b)),
                  pl.BlockSpec((t, nj), lambda b: (0, off_gc // nj + b))],
        out_specs=[col] * 3,
        out_shape=[jax.ShapeDtypeStruct((t, j * nj), BF16)] * 3,
        sem=("parallel",), name="branch_out_merge")


def mix_out_dx_merge_bwd(dom, w_mix, p, y_attn, y_conv, off_ga, off_gc, dep=None):
    t, d = y_attn.shape
    cw = math.gcd(math.gcd(off_ga, off_gc), math.gcd(d, 256))

    def body(dom_ref, w_ref, ga_ref, gc_ref, ya_ref, yc_ref, dya_ref, dyc_ref, dga_ref, dgc_ref):
        for s, sz in _row_chunks(t):
            rows = pl.ds(s, sz)
            dm = lax.dot_general(dom_ref[rows, :], w_ref[...], (((1,), (1,)), ((), ())), preferred_element_type=F32)
            sa = _sigmoid(ga_ref[rows, :])
            sc = _sigmoid(gc_ref[rows, :])
            dya_ref[rows, :] = (dm * sa).astype(BF16)
            dyc_ref[rows, :] = (dm * sc).astype(BF16)
            dga_ref[rows, :] = (dm * ya_ref[rows, :] * sa * (1.0 - sa)).astype(BF16)
            dgc_ref[rows, :] = (dm * yc_ref[rows, :] * sc * (1.0 - sc)).astype(BF16)

    col = pl.BlockSpec((t, cw), lambda j: (0, j))
    return _pallas(
        body, [dom, w_mix, p, p, y_attn, y_conv], dep=dep, grid=(d // cw,),
        in_specs=[pl.BlockSpec((t, d), lambda j: (0, 0)), pl.BlockSpec((cw, d), lambda j: (j, 0)),
                  pl.BlockSpec((t, cw), lambda j: (0, off_ga // cw + j)),
                  pl.BlockSpec((t, cw), lambda j: (0, off_gc // cw + j)), col, col],
        out_specs=[col] * 4,
        out_shape=[jax.ShapeDtypeStruct((t, d), BF16)] * 4,
        sem=("parallel",), name="mm_mix_out_dx_merge_bwd")


def ffn_perm(n_blocks):
    half = n_blocks // 2
    return tuple(2 * j if j < half else 2 * (j - half) + 1 for j in range(n_blocks))


def swiglu_fwd(f, nj, dep=None):
    t, two = f.shape
    tr = _tile(t, ROW_TILE, SUBLANES)
    npair = two // (2 * nj)

    def body(f_ref, o_ref):
        g = f_ref[:, :nj].astype(F32)
        u = f_ref[:, nj:].astype(F32)
        o_ref[...] = (g * _sigmoid(g) * u).astype(BF16)

    return _pallas(
        body, [f], dep=dep, grid=(t // tr, npair),
        in_specs=[pl.BlockSpec((tr, 2 * nj), lambda i, j: (i, j))],
        out_specs=pl.BlockSpec((tr, nj), lambda i, j: (i, j)),
        out_shape=jax.ShapeDtypeStruct((t, two // 2), BF16),
        sem=("parallel", "parallel"), name="swiglu_fwd")


def swiglu_bwd(f, dact, nj, dep=None):
    t, two = f.shape
    tr = _tile(t, ROW_TILE, SUBLANES)
    npair = two // (2 * nj)

    def body(f_ref, da_ref, o_ref):
        g = f_ref[:, :nj].astype(F32)
        u = f_ref[:, nj:].astype(F32)
        da = da_ref[...]
        s = _sigmoid(g)
        o_ref[:, :nj] = (da * u * (s * (1.0 + g * (1.0 - s)))).astype(BF16)
        o_ref[:, nj:] = (da * (g * s)).astype(BF16)

    return _pallas(
        body, [f, dact], dep=dep, grid=(t // tr, npair),
        in_specs=[pl.BlockSpec((tr, 2 * nj), lambda i, j: (i, j)), pl.BlockSpec((tr, nj), lambda i, j: (i, j))],
        out_specs=pl.BlockSpec((tr, 2 * nj), lambda i, j: (i, j)),
        out_shape=jax.ShapeDtypeStruct((t, two), BF16),
        sem=("parallel", "parallel"), name="swiglu_bwd")


def _t5_bucket_table():
    q_off = np.arange(BLOCK)
    k_off = np.arange(2 * BLOCK)
    dist = q_off[:, None] + BLOCK - k_off[None, :]
    n = np.maximum(dist, 0)
    nf = np.maximum(n, 1).astype(np.float32)
    large = MAX_EXACT + (np.log(nf / np.float32(MAX_EXACT)) / np.float32(math.log(MAX_DISTANCE / MAX_EXACT))
                         * np.float32(NUM_BUCKETS - MAX_EXACT)).astype(np.int32)
    large = np.minimum(large, NUM_BUCKETS - 1)
    bucket = np.where(n < MAX_EXACT, n, large).astype(np.int32)
    allowed = (dist >= 0) & (dist < WINDOW)
    return np.where(allowed, bucket, -1).astype(np.int32)


def bias_table(rel_bias, bucket_p, bucket_c, dep=None):
    nb, nq = rel_bias.shape

    def body(rb_ref, bkp_ref, bkc_ref, op_ref, oc_ref):
        for bk_ref, o_ref in ((bkp_ref, op_ref), (bkc_ref, oc_ref)):
            bk = bk_ref[...]
            for h in range(nq):
                acc = jnp.full(bk.shape, -jnp.inf, F32)
                for b in range(nb):
                    acc = jnp.where(bk == b, rb_ref[b, h], acc)
                o_ref[h] = acc

    return _pallas(
        body, [rel_bias, bucket_p, bucket_c], dep=dep,
        in_specs=[_SMEM, _VMEM, _VMEM], out_specs=[_VMEM, _VMEM],
        out_shape=[jax.ShapeDtypeStruct((nq,) + bucket_p.shape, F32)] * 2,
        name="bias_table")


def bias_table_bwd(dbp, dbc, bucket_p, bucket_c, dep=None):
    nq = dbp.shape[0]

    def body(dbp_ref, dbc_ref, bkp_ref, bkc_ref, o_ref):
        bkp, bkc = bkp_ref[...][None], bkc_ref[...][None]
        dp, dc = dbp_ref[...], dbc_ref[...]
        for b in range(NUM_BUCKETS):
            sel = jnp.where(bkp == b, dp, 0.0) + jnp.where(bkc == b, dc, 0.0)
            o_ref[b] = jnp.sum(jnp.sum(sel, axis=2, keepdims=True), axis=1, keepdims=True)

    return _pallas(
        body, [dbp, dbc, bucket_p, bucket_c], dep=dep,
        in_specs=[_VMEM] * 4, out_specs=_VMEM,
        out_shape=jax.ShapeDtypeStruct((NUM_BUCKETS, nq, 1, 1), F32),
        name="bias_table_bwd")


_BNT = (((2,), (2,)), ((0,), (0,)))
_BNN = (((2,), (1,)), ((0,), (0,)))
_BTN = (((1,), (1,)), ((0,), (0,)))


@jax.custom_vjp
def _bdot_nt(a, b):
    return lax.dot_general(a.astype(BF16), b.astype(BF16), _BNT, preferred_element_type=F32)


def _bdot_nt_fwd(a, b):
    return _bdot_nt(a, b), (a, b)


def _bdot_nt_bwd(res, g):
    a, b = res
    gb = g.astype(BF16)
    da = lax.dot_general(gb, b.astype(BF16), _BNN, preferred_element_type=F32)
    db = lax.dot_general(gb, a.astype(BF16), _BTN, preferred_element_type=F32)
    return da, db


_bdot_nt.defvjp(_bdot_nt_fwd, _bdot_nt_bwd)


@jax.custom_vjp
def _bdot_nn(a, b):
    return lax.dot_general(a.astype(BF16), b.astype(BF16), _BNN, preferred_element_type=F32)


def _bdot_nn_fwd(a, b):
    return _bdot_nn(a, b), (a, b)


def _bdot_nn_bwd(res, g):
    a, b = res
    gb = g.astype(BF16)
    da = lax.dot_general(gb, b.astype(BF16), _BNT, preferred_element_type=F32)
    db = lax.dot_general(a.astype(BF16), gb, _BTN, preferred_element_type=F32)
    return da, db


_bdot_nn.defvjp(_bdot_nn_fwd, _bdot_nn_bwd)


def _attn_math(q, kp, kc, vp, vc, bp, bc, sinks, qg, kg, *, prev_ok, scale):
    h, rows, _ = q.shape
    b = kp.shape[1]
    qn = q * _rms(q) * qg
    kpn = kp * _rms(kp) * kg
    kcn = kc * _rms(kc) * kg
    lp = _bdot_nt(qn, kpn) * scale + bp.reshape(h, rows, b)
    lc = _bdot_nt(qn, kcn) * scale + bc.reshape(h, rows, b)
    lp = jnp.where(prev_ok, lp, -jnp.inf)
    sink = jnp.broadcast_to(sinks, (sinks.shape[0], b, 1)).reshape(h, rows, 1)
    m = jnp.maximum(jnp.maximum(jnp.max(lp, axis=-1, keepdims=True), jnp.max(lc, axis=-1, keepdims=True)), sink)
    m = lax.stop_gradient(m)
    pp = jnp.exp(lp - m)
    pc = jnp.exp(lc - m)
    den = jnp.sum(pp, axis=-1, keepdims=True) + jnp.sum(pc, axis=-1, keepdims=True) + jnp.exp(sink - m)
    inv = 1.0 / den
    return _bdot_nn(pp * inv, vp) + _bdot_nn(pc * inv, vc)


def _attn_specs(p, aw, kvw, nq, hd, nblk, reverse):
    assert aw % (2 * kvw) == 0
    kv_col = aw // (2 * kvw)

    def blk(n):
        return nblk - 1 - n if reverse else n

    return [
        pl.BlockSpec((BLOCK, aw), lambda n: (blk(n), 0)),
        pl.BlockSpec((BLOCK, 2 * kvw), lambda n: (jnp.maximum(blk(n) - 1, 0), kv_col)),
        pl.BlockSpec((BLOCK, 2 * kvw), lambda n: (blk(n), kv_col)),
        _full_spec((nq, BLOCK, BLOCK)), _full_spec((nq, BLOCK, BLOCK)), _full_spec((nq, 1, 1)),
        _full_spec((1, hd)), _full_spec((1, hd)),
    ]


def _head_major(ref, n_heads, grp, hd, offset=0):
    return jnp.stack([
        jnp.concatenate([ref[:, pl.ds(offset + (grp * h + g) * hd, hd)].astype(F32) for g in range(grp)], axis=0)
        for h in range(n_heads)])


def _attn_inputs(nkv, grp, hd, kvw, q_ref, kvp_ref, kvc_ref):
    return (_head_major(q_ref, nkv, grp, hd), _head_major(kvp_ref, nkv, 1, hd), _head_major(kvc_ref, nkv, 1, hd),
            _head_major(kvp_ref, nkv, 1, hd, kvw), _head_major(kvc_ref, nkv, 1, hd, kvw))


def attn_fwd(p, bias_p, bias_c, sinks, qg, kg, *, aw, kvw, dep=None):
    t, hd = p.shape[0], qg.shape[-1]
    nq, nkv, nblk = aw // hd, kvw // hd, t // BLOCK
    grp = nq // nkv
    scale = hd ** -0.5

    def body(q_ref, kvp_ref, kvc_ref, bp_ref, bc_ref, s_ref, qg_ref, kg_ref, o_ref):
        prev_ok = pl.program_id(0) > 0
        out = _attn_math(*_attn_inputs(nkv, grp, hd, kvw, q_ref, kvp_ref, kvc_ref), bp_ref[...], bc_ref[...],
                         s_ref[...], qg_ref[...], kg_ref[...], prev_ok=prev_ok, scale=scale)
        for h in range(nkv):
            for g in range(grp):
                o_ref[:, pl.ds((grp * h + g) * hd, hd)] = out[h, g * BLOCK:(g + 1) * BLOCK].astype(BF16)

    return _pallas(
        body, [p, p, p, bias_p, bias_c, sinks, qg, kg], dep=dep, grid=(nblk,),
        in_specs=_attn_specs(p, aw, kvw, nq, hd, nblk, False),
        out_specs=pl.BlockSpec((BLOCK, aw), lambda n: (n, 0)),
        out_shape=jax.ShapeDtypeStruct((t, aw), BF16),
        sem=("parallel",), name="attn_fwd")


def attn_bwd(p, bias_p, bias_c, sinks, qg, kg, do, *, aw, kvw, dep=None):
    t, hd = p.shape[0], qg.shape[-1]
    nq, nkv, nblk = aw // hd, kvw // hd, t // BLOCK
    grp = nq // nkv
    scale = hd ** -0.5

    def body(q_ref, kvp_ref, kvc_ref, bp_ref, bc_ref, s_ref, qg_ref, kg_ref, do_ref,
             dqkv_ref, dbp_ref, dbc_ref, ds_ref, dqg_ref, dkg_ref, carry):
        i = pl.program_id(0)
        prev_ok = (nblk - 1 - i) > 0

        @pl.when(i == 0)
        def _():
            carry[...] = jnp.zeros_like(carry)
            dbp_ref[...] = jnp.zeros_like(dbp_ref)
            dbc_ref[...] = jnp.zeros_like(dbc_ref)
            ds_ref[...] = jnp.zeros_like(ds_ref)
            dqg_ref[...] = jnp.zeros_like(dqg_ref)
            dkg_ref[...] = jnp.zeros_like(dkg_ref)

        fn = functools.partial(_attn_math, prev_ok=prev_ok, scale=scale)
        _, vjp = jax.vjp(fn, *_attn_inputs(nkv, grp, hd, kvw, q_ref, kvp_ref, kvc_ref), bp_ref[...], bc_ref[...],
                         s_ref[...], qg_ref[...], kg_ref[...])
        dq, dkp, dkc, dvp, dvc, dbp, dbc, dsk, dqg, dkg = vjp(_head_major(do_ref, nkv, grp, hd))
        for h in range(nkv):
            for g in range(grp):
                dqkv_ref[:, pl.ds((grp * h + g) * hd, hd)] = dq[h, g * BLOCK:(g + 1) * BLOCK].astype(BF16)
            k_cols, v_cols = pl.ds(h * hd, hd), pl.ds(kvw + h * hd, hd)
            dqkv_ref[:, pl.ds(aw + h * hd, hd)] = (dkc[h] + carry[:, k_cols]).astype(BF16)
            dqkv_ref[:, pl.ds(aw + kvw + h * hd, hd)] = (dvc[h] + carry[:, v_cols]).astype(BF16)
            carry[:, k_cols] = dkp[h]
            carry[:, v_cols] = dvp[h]
        dbp_ref[...] += dbp
        dbc_ref[...] += dbc
        ds_ref[...] += dsk
        dqg_ref[...] += dqg
        dkg_ref[...] += dkg

    return _pallas(
        body, [p, p, p, bias_p, bias_c, sinks, qg, kg, do], dep=dep, grid=(nblk,),
        in_specs=_attn_specs(p, aw, kvw, nq, hd, nblk, True)
        + [pl.BlockSpec((BLOCK, aw), lambda n: (nblk - 1 - n, 0))],
        out_specs=[
            pl.BlockSpec((BLOCK, aw + 2 * kvw), lambda n: (nblk - 1 - n, 0)),
            _full_spec((nq, BLOCK, BLOCK)), _full_spec((nq, BLOCK, BLOCK)), _full_spec((nq, 1, 1)),
            _full_spec((1, hd)), _full_spec((1, hd)),
        ],
        out_shape=[
            jax.ShapeDtypeStruct((t, aw + 2 * kvw), BF16),
            jax.ShapeDtypeStruct((nq, BLOCK, BLOCK), F32),
            jax.ShapeDtypeStruct((nq, BLOCK, BLOCK), F32),
            jax.ShapeDtypeStruct((nq, 1, 1), F32),
            jax.ShapeDtypeStruct((1, hd), F32),
            jax.ShapeDtypeStruct((1, hd), F32),
        ],
        scratch=[pltpu.VMEM((BLOCK, 2 * kvw), F32)],
        sem=("arbitrary",), name="attn_bwd")


CONV_TILE = 256


def _conv_halo_specs(tb, ch, nblk):
    per = tb // CONV_HALO
    last = nblk * per - 1
    cur = pl.BlockSpec((tb, ch), lambda n: (n, 0))
    prev = pl.BlockSpec((CONV_HALO, ch), lambda n: (jnp.maximum(n * per - 1, 0), 0))
    nxt = pl.BlockSpec((CONV_HALO, ch), lambda n: (jnp.minimum((n + 1) * per, last), 0))
    return cur, prev, nxt


def _ln_silu(co, ln_g, ln_b):
    mu = jnp.mean(co, axis=-1, keepdims=True)
    cen = co - mu
    rstd = lax.rsqrt(jnp.mean(cen * cen, axis=-1, keepdims=True) + EPS)
    xhat = cen * rstd
    z = xhat * ln_g + ln_b
    return xhat, rstd, z


def _shifted_copies(src, shifted):
    rows = src.shape[0] - SUBLANES
    for r in range(1, SUBLANES):
        shifted[r, pl.ds(0, rows), :] = src[pl.ds(r, rows), :]


def _rows_from(src, shifted, start, n):
    r = start % SUBLANES
    if r == 0:
        return src[pl.ds(start, n), :]
    return shifted[r, pl.ds(start - r, n), :]


def conv_fwd(ca, cb, conv_w, conv_b, ln_g, ln_b, dep=None):
    t, ch = ca.shape
    tb = _tile(t, CONV_TILE, CONV_HALO)
    nblk = t // tb
    cur, prev, _ = _conv_halo_specs(tb, ch, nblk)
    lead = CONV_HALO - (CONV_WIDTH - 1)

    def body(ca_ref, cb_ref, cap_ref, cbp_ref, w_ref, b_ref, g_ref, bb_ref, s_ref, co_ref, ubuf, ushift):
        n = pl.program_id(0)
        halo = cap_ref[...] * _sigmoid(cbp_ref[...])
        ubuf[pl.ds(0, CONV_HALO), :] = jnp.where(n > 0, halo, 0.0)
        ubuf[pl.ds(CONV_HALO, tb), :] = ca_ref[...] * _sigmoid(cb_ref[...])
        _shifted_copies(ubuf, ushift)
        acc = jnp.broadcast_to(b_ref[...], (tb, ch))
        for k in range(CONV_WIDTH):
            acc = acc + w_ref[pl.ds(k, 1), :] * _rows_from(ubuf, ushift, lead + k, tb)
        co_ref[...] = acc
        _, _, z = _ln_silu(acc, g_ref[...], bb_ref[...])
        s_ref[...] = (z * _sigmoid(z)).astype(BF16)

    vec = _full_spec((1, ch))
    return _pallas(
        body, [ca, cb, ca, cb, conv_w, conv_b, ln_g, ln_b], dep=dep, grid=(nblk,),
        in_specs=[cur, cur, prev, prev, _full_spec(conv_w.shape), vec, vec, vec],
        out_specs=[cur, cur],
        out_shape=[jax.ShapeDtypeStruct((t, ch), BF16), jax.ShapeDtypeStruct((t, ch), F32)],
        scratch=[pltpu.VMEM((CONV_HALO + tb, ch), F32), pltpu.VMEM((SUBLANES, CONV_HALO + tb, ch), F32)],
        sem=("parallel",), name="conv_fwd")


def conv_bwd(ca, cb, co, ds, conv_w, ln_g, ln_b, dep=None):
    t, ch = ca.shape
    tb = _tile(t, CONV_TILE, CONV_HALO)
    nblk = t // tb
    cur, prev, nxt = _conv_halo_specs(tb, ch, nblk)
    lead = CONV_HALO - (CONV_WIDTH - 1)
    ext = tb + CONV_HALO

    def body(ca_ref, cb_ref, cap_ref, cbp_ref, co_ref, con_ref, ds_ref, dsn_ref, w_ref, g_ref, bb_ref,
             dca_ref, dcb_ref, dw_ref, dvec_ref, ubuf, dbuf, ushift, dshift):
        n = pl.program_id(0)
        is_last = n == nblk - 1
        sig_b = _sigmoid(cb_ref[...])
        cav = ca_ref[...].astype(F32)
        ubuf[pl.ds(0, CONV_HALO), :] = jnp.where(n > 0, cap_ref[...] * _sigmoid(cbp_ref[...]), 0.0)
        ubuf[pl.ds(CONV_HALO, tb), :] = cav * sig_b
        _shifted_copies(ubuf, ushift)
        co = jnp.concatenate([co_ref[...], con_ref[...]], axis=0)
        xhat, rstd, z = _ln_silu(co, g_ref[...], bb_ref[...])
        dsv = jnp.concatenate([ds_ref[...].astype(F32), jnp.where(is_last, 0.0, dsn_ref[...].astype(F32))], axis=0)
        sg = _sigmoid(z)
        dz = dsv * (sg * (1.0 + z * (1.0 - sg)))
        dxh = dz * g_ref[...]
        dco = rstd * (dxh - jnp.mean(dxh, axis=-1, keepdims=True)
                      - xhat * jnp.mean(dxh * xhat, axis=-1, keepdims=True))
        dbuf[...] = dco
        _shifted_copies(dbuf, dshift)

        @pl.when(n == 0)
        def _():
            dw_ref[...] = jnp.zeros_like(dw_ref)
            dvec_ref[...] = jnp.zeros_like(dvec_ref)

        dco_cur = dco[:tb]
        dvec_ref[pl.ds(0, 1), :] += jnp.sum(dco_cur, axis=0, keepdims=True)
        dvec_ref[pl.ds(1, 1), :] += jnp.sum(dz[:tb] * xhat[:tb], axis=0, keepdims=True)
        dvec_ref[pl.ds(2, 1), :] += jnp.sum(dz[:tb], axis=0, keepdims=True)
        du = jnp.zeros((tb, ch), F32)
        for k in range(CONV_WIDTH):
            du = du + w_ref[pl.ds(k, 1), :] * _rows_from(dbuf, dshift, CONV_WIDTH - 1 - k, tb)
            dw_ref[pl.ds(k, 1), :] += jnp.sum(dco_cur * _rows_from(ubuf, ushift, lead + k, tb), axis=0,
                                              keepdims=True)
        dca_ref[...] = (du * sig_b).astype(BF16)
        dcb_ref[...] = (du * cav * sig_b * (1.0 - sig_b)).astype(BF16)

    vec = _full_spec((1, ch))
    return _pallas(
        body, [ca, cb, ca, cb, co, co, ds, ds, conv_w, ln_g, ln_b], dep=dep, grid=(nblk,),
        in_specs=[cur, cur, prev, prev, cur, nxt, cur, nxt, _full_spec(conv_w.shape), vec, vec],
        out_specs=[cur, cur, _full_spec(conv_w.shape), _full_spec((SUBLANES, ch))],
        out_shape=[jax.ShapeDtypeStruct((t, ch), BF16), jax.ShapeDtypeStruct((t, ch), BF16),
                   jax.ShapeDtypeStruct(conv_w.shape, F32), jax.ShapeDtypeStruct((SUBLANES, ch), F32)],
        scratch=[pltpu.VMEM((CONV_HALO + tb, ch), F32), pltpu.VMEM((ext, ch), F32),
                 pltpu.VMEM((SUBLANES, CONV_HALO + tb, ch), F32), pltpu.VMEM((SUBLANES, ext, ch), F32)],
        sem=("arbitrary",), name="conv_bwd")


def ada_fwd(c_t, w_ada, dep=None):
    d, nc = w_ada.shape
    nex = c_t.shape[1]
    tn = _tile(nc, 512)

    def body(ct_ref, w_ref, o_ref):
        w = w_ref[...]
        ct = ct_ref[...]
        cact = ct * _sigmoid(ct)
        rows = [jnp.sum(w * cact[:, b:b + 1], axis=0, keepdims=True) for b in range(nex)]
        o_ref[...] = jnp.concatenate(rows, axis=0)

    return _pallas(
        body, [c_t, w_ada], dep=dep, grid=(nc // tn,),
        in_specs=[_full_spec(c_t.shape), pl.BlockSpec((d, tn), lambda j: (0, j))],
        out_specs=pl.BlockSpec((nex, tn), lambda j: (0, j)),
        out_shape=jax.ShapeDtypeStruct((nex, nc), F32),
        sem=("parallel",), name="ada_fwd")


def _adamw_math(w, g, m, v):
    m = ADAM_B1 * m + (1.0 - ADAM_B1) * g
    v = ADAM_B2 * v + (1.0 - ADAM_B2) * (g * g)
    m_hat = m / (1.0 - ADAM_B1 ** ADAM_STEP)
    v_hat = v / (1.0 - ADAM_B2 ** ADAM_STEP)
    delta = -ADAM_LR * (m_hat / (jnp.sqrt(v_hat) + ADAM_EPS) + ADAM_WD * w)
    return delta, m, v


def adamw(w, g, m, v, name, copy_grad=False, dep=None):
    r, n = w.shape
    tr, tn = _ew_tiles(r, n, elems=256 * 1024)
    n_out = 4 if copy_grad else 3

    def body(w_ref, g_ref, m_ref, v_ref, *outs):
        g = g_ref[...]
        if copy_grad:
            outs[0][...] = g
        outs[-3][...], outs[-2][...], outs[-1][...] = _adamw_math(w_ref[...], g, m_ref[...], v_ref[...])

    blk = pl.BlockSpec((tr, tn), lambda i, j: (i, j))
    return _pallas(
        body, [w, g, m, v], dep=dep, grid=(r // tr, n // tn),
        in_specs=[blk] * 4, out_specs=[blk] * n_out,
        out_shape=[jax.ShapeDtypeStruct((r, n), F32)] * n_out,
        sem=("parallel", "parallel"), name=name)


def ada_grad_adamw(c_t, dmod_cols, w, m, v, dep=None):
    d, nc = w.shape
    nex = c_t.shape[1]
    tr, tn = _ew_tiles(d, nc, elems=256 * 1024)

    def body(ct_ref, dm_ref, w_ref, m_ref, v_ref, g_ref, d_ref, nm_ref, nv_ref):
        ct = ct_ref[...]
        cact = ct * _sigmoid(ct)
        dm = dm_ref[...]
        g = cact[:, 0:1] * dm[0:1, :]
        for b in range(1, nex):
            g = g + cact[:, b:b + 1] * dm[b:b + 1, :]
        g_ref[...] = g
        d_ref[...], nm_ref[...], nv_ref[...] = _adamw_math(w_ref[...], g, m_ref[...], v_ref[...])

    blk = pl.BlockSpec((tr, tn), lambda i, j: (i, j))
    return _pallas(
        body, [c_t, dmod_cols, w, m, v], dep=dep, grid=(d // tr, nc // tn),
        in_specs=[pl.BlockSpec((tr, nex), lambda i, j: (i, 0)), pl.BlockSpec((nex, tn), lambda i, j: (0, j)),
                  blk, blk, blk],
        out_specs=[blk] * 4,
        out_shape=[jax.ShapeDtypeStruct((d, nc), F32)] * 4,
        sem=("parallel", "parallel"), name="ada_grad_adamw")


def _row_pack(parts):
    cols, offs, off = [], [], 0
    for p in parts:
        n = p.shape[1]
        width = -(-n // LANES) * LANES
        cols.append(jnp.pad(p, ((0, 0), (0, width - n))) if width != n else p)
        offs.append(off)
        off += width
    return jnp.concatenate(cols, axis=1), offs


def small_sum_adamw(gathered, offs, ws, ms, vs, extra_widths, dep=None):
    ndev = gathered.shape[0]
    npar = len(ws)

    def body(ga_ref, *refs):
        w_refs, m_refs, v_refs = refs[:npar], refs[npar:2 * npar], refs[2 * npar:3 * npar]
        outs = refs[3 * npar:]
        tot = ga_ref[0]
        for s in range(1, ndev):
            tot = tot + ga_ref[s]
        for i in range(npar):
            n = ws[i].shape[1]
            g = tot[:, offs[i]:offs[i] + n]
            outs[4 * i][...] = g
            outs[4 * i + 1][...], outs[4 * i + 2][...], outs[4 * i + 3][...] = _adamw_math(
                w_refs[i][...], g, m_refs[i][...], v_refs[i][...])
        for e, n in enumerate(extra_widths):
            off = offs[npar + e]
            outs[4 * npar + e][...] = tot[:, off:off + n]

    shapes = [jax.ShapeDtypeStruct(w.shape, F32) for w in ws for _ in range(4)]
    shapes += [jax.ShapeDtypeStruct((1, n), F32) for n in extra_widths]
    return _pallas(
        body, [gathered, *ws, *ms, *vs], dep=dep, in_specs=[_VMEM] * (1 + 3 * npar), out_specs=[_VMEM] * len(shapes),
        out_shape=shapes, name="small_sum_adamw")


def _position():
    return lax.axis_index("x"), lax.axis_index("y"), lax.axis_index("c")


def _other_chips(x, y):
    return [(1 - x, y), (x, 1 - y), (1 - x, 1 - y)]


def allgather_small(block, name, dep=None):
    def body(x_ref, out_ref, send_sems, recv_sems, local_sem):
        x, y, c = _position()
        me, sibling = (x, y, c), (x, y, 1 - c)
        chips = _other_chips(x, y)

        def slot(px, py, pc):
            return out_ref.at[4 * px + 2 * py + pc]

        def copy(k, block_of, to, src=None):
            return pltpu.make_async_remote_copy(
                src_ref=slot(*block_of) if src is None else src, dst_ref=slot(*block_of),
                send_sem=send_sems.at[k], recv_sem=recv_sems.at[k], device_id=to, device_id_type=MESH)

        mine = pltpu.make_async_copy(x_ref, slot(*me), local_sem)
        mine.start()
        first = [copy(0, me, sibling, src=x_ref)]
        first += [copy(1 + j, me, (*chip, c), src=x_ref) for j, chip in enumerate(chips)]
        for cp in first:
            cp.start()
        passed = [copy(4 + j, (*chip, c), sibling) for j, chip in enumerate(chips)]
        for j, chip in enumerate(chips):
            copy(1 + j, (*chip, c), me).wait_recv()
            passed[j].start()
        copy(0, sibling, me).wait_recv()
        for j, chip in enumerate(chips):
            copy(4 + j, (*chip, 1 - c), me).wait_recv()
        for cp in first + passed:
            cp.wait_send()
        mine.wait()

    return _pallas(
        body, [block], dep=dep,
        out_shape=jax.ShapeDtypeStruct((N_DEV, *block.shape), block.dtype),
        in_specs=[_VMEM], out_specs=_VMEM,
        scratch=[pltpu.SemaphoreType.DMA((7,)), pltpu.SemaphoreType.DMA((7,)), pltpu.SemaphoreType.DMA],
        name=name)


class Started(NamedTuple):
    send_sems: Any
    recv_sems: Any
    bufs: list


def exchange_start_many(name, buf_sets, plans, dep=None):
    sizes = [len(bufs) for bufs in buf_sets]
    first = [sum(sizes[:i]) for i in range(len(sizes))]
    flat = [b for bufs in buf_sets for b in bufs]
    nb, npl = len(flat), len(plans)

    def body(*refs):
        for i, (s, _, plan) in enumerate(plans):
            for cp in plan(refs[first[s]:first[s] + sizes[s]], refs[nb + 2 * i], refs[nb + 2 * i + 1]):
                cp.start()

    sems = [pltpu.SemaphoreType.DMA((n,)) for _, n, _ in plans for _ in range(2)]
    outs = _pallas(
        body, [pltpu.with_memory_space_constraint(b, pltpu.HBM) for b in flat], dep=dep, name=name,
        out_shape=(*sems, *[pltpu.HBM(b.shape, b.dtype) for b in flat]),
        in_specs=[_HBM] * nb,
        out_specs=(*[_SEM] * (2 * npl), *[_HBM] * nb),
        input_output_aliases={i: 2 * npl + i for i in range(nb)},
        compiler_params=pltpu.CompilerParams(has_side_effects=_EFFECT))
    new_bufs = outs[2 * npl:]
    return [Started(outs[2 * i], outs[2 * i + 1], list(new_bufs[first[s]:first[s] + sizes[s]]))
            for i, (s, _, _) in enumerate(plans)]


def exchange_start(name, bufs, n_copies, plan, dep=None):
    return exchange_start_many(name, [bufs], [(0, n_copies, plan)], dep=dep)[0]


def exchange_wait(name, started, plan, bufs=None, dep=None):
    if bufs is not None:
        started = started._replace(bufs=list(bufs))
    nb = len(started.bufs)

    def body(*refs):
        for cp in plan(refs[:nb], refs[nb], refs[nb + 1]):
            cp.wait_send()
            cp.wait_recv()

    outs = _pallas(
        body, [*started.bufs, started.send_sems, started.recv_sems], dep=dep, name=name,
        out_shape=tuple(pltpu.HBM(b.shape, b.dtype) for b in started.bufs),
        in_specs=[_HBM] * nb + [_SEM, _SEM],
        out_specs=tuple([_HBM] * nb),
        input_output_aliases={i: i for i in range(nb)},
        compiler_params=pltpu.CompilerParams(has_side_effects=_EFFECT))
    return list(outs)


def _remote(src, dst, send_sems, recv_sems, i, to):
    return pltpu.make_async_remote_copy(src_ref=src, dst_ref=dst, send_sem=send_sems.at[i], recv_sem=recv_sems.at[i],
                                        device_id=to, device_id_type=MESH)


def _half_rows(buf_rows, chip_idx, pc):
    half = buf_rows // (2 * N_CHIPS)
    return pl.ds((2 * chip_idx + pc) * half, half)


ALL_PEERS = (0, 1, 2)


def plan_gather_ici(refs, send_sems, recv_sems, peers=ALL_PEERS):
    x, y, c = _position()
    chips = _other_chips(x, y)
    copies = []
    for k, ref in enumerate(refs):
        rows = ref.at[_half_rows(ref.shape[0], 2 * x + y, c), :]
        for i, j in enumerate(peers):
            copies.append(_remote(rows, rows, send_sems, recv_sems, len(peers) * k + i, (*chips[j], c)))
    return copies


def plan_gather_relay(refs, send_sems, recv_sems):
    x, y, c = _position()
    copies = []
    for k, ref in enumerate(refs):
        quarter = ref.shape[0] // (4 * N_CHIPS)
        for i, (src_chip, to) in enumerate((((1 - x, y), (x, 1 - y, c)), ((x, 1 - y), (1 - x, y, c)))):
            start = (2 * (2 * src_chip[0] + src_chip[1]) + c) * 2 * quarter + i * quarter
            rows = ref.at[pl.ds(start, quarter), :]
            copies.append(_remote(rows, rows, send_sems, recv_sems, 2 * k + i, to))
    return copies


def plan_gather_d2d(refs, send_sems, recv_sems, peers=ALL_PEERS):
    x, y, c = _position()
    chips = _other_chips(x, y)
    copies = []
    for k, ref in enumerate(refs):
        for i, j in enumerate(peers):
            px, py = chips[j]
            rows = ref.at[_half_rows(ref.shape[0], 2 * px + py, c), :]
            copies.append(_remote(rows, rows, send_sems, recv_sems, len(peers) * k + i, (x, y, 1 - c)))
    return copies


def plan_pair_exchange(refs, send_sems, recv_sems):
    x, y, c = _position()
    nw = len(refs) // 2
    copies = []
    for k in range(nw):
        for chip in range(N_CHIPS):
            copies.append(_remote(refs[k].at[chip, 1 - c], refs[nw + k].at[chip], send_sems, recv_sems,
                                  N_CHIPS * k + chip, (x, y, 1 - c)))
    return copies


def plan_chip_exchange(refs, send_sems, recv_sems):
    x, y, c = _position()
    nw = len(refs) // 2
    copies = []
    for k in range(nw):
        for j, (px, py) in enumerate(_other_chips(x, y)):
            copies.append(_remote(refs[k].at[2 * px + py], refs[nw + k].at[2 * x + y], send_sems, recv_sems,
                                  3 * k + j, (px, py, c)))
    return copies


def plan_pair_share(refs, send_sems, recv_sems):
    x, y, c = _position()
    return [_remote(ref.at[c], ref.at[c], send_sems, recv_sems, k, (x, y, 1 - c)) for k, ref in enumerate(refs)]


def cast_into_slot(src, slot, n_slots, name, dep=None):
    r, n = src.shape
    tr, tn = _ew_tiles(r, n, BF16_SUBLANES)

    def body(slot_ref, s_ref, o_ref):
        o_ref[...] = s_ref[...].astype(BF16)

    return _pallas(
        body, [slot, src], dep=dep, n_prefetch=1, grid=(r // tr, n // tn),
        in_specs=[pl.BlockSpec((tr, tn), lambda i, j, sl: (i, j))],
        out_specs=pl.BlockSpec((None, tr, tn), lambda i, j, sl: (sl[0], i, j)),
        out_shape=jax.ShapeDtypeStruct((n_slots, r, n), BF16),
        sem=("parallel", "parallel"), name=name)


def pair_sum(g, r, core, name, dep=None):
    nchip, _, h, n = g.shape
    th, tn = _ew_tiles(h, n, BF16_SUBLANES)

    def body(core_ref, g_ref, r_ref, o_ref):
        o_ref[...] = (g_ref[...].astype(F32) + r_ref[...].astype(F32)).astype(BF16)

    return _pallas(
        body, [core, g, r], dep=dep, n_prefetch=1, grid=(nchip, h // th, n // tn),
        in_specs=[pl.BlockSpec((None, None, th, tn), lambda a, i, j, cr: (a, cr[0], i, j)),
                  pl.BlockSpec((None, th, tn), lambda a, i, j, cr: (a, i, j))],
        out_specs=pl.BlockSpec((None, th, tn), lambda a, i, j, cr: (a, i, j)),
        out_shape=jax.ShapeDtypeStruct((nchip, h, n), BF16),
        sem=("parallel", "parallel", "parallel"), name=name)


def chip_sum(own, got, where, name, dep=None):
    nchip, h, n = got.shape
    th, tn = _ew_tiles(h, n, BF16_SUBLANES, elems=256 * 1024)

    def body(where_ref, own_ref, *rest):
        got_refs, o_ref = rest[:nchip], rest[nchip]
        chip = where_ref[0]
        acc = None
        for s in range(nchip):
            term = jnp.where(chip == s, own_ref[...], got_refs[s][...]).astype(F32)
            acc = term if acc is None else acc + term
        o_ref[...] = acc

    def got_spec(s):
        return pl.BlockSpec((None, th, tn), lambda i, j, wr: (jnp.where(wr[0] == s, (s + 1) % nchip, s), i, j))

    return _pallas(
        body, [where, own, *[got] * nchip], dep=dep, n_prefetch=1, grid=(h // th, n // tn),
        in_specs=[pl.BlockSpec((None, th, tn), lambda i, j, wr: (wr[0], i, j))]
        + [got_spec(s) for s in range(nchip)],
        out_specs=pl.BlockSpec((None, th, tn), lambda i, j, wr: (wr[1], i, j)),
        out_shape=jax.ShapeDtypeStruct((2, h, n), F32),
        sem=("parallel", "parallel"), name=name)


def kernel(x, c, w_ada, b_ada, norm_mix_g, w_in, q_norm_g, k_norm_g, attn_sinks, rel_bias, w_attn_out, conv_w, conv_b, conv_ln_g, conv_ln_b, w_conv_out, w_mix_out, norm_ffn_g, w_ffn_in, w_ffn_out, loss_target, m_w_ada, m_b_ada, m_norm_mix_g, m_w_in, m_q_norm_g, m_k_norm_g, m_attn_sinks, m_rel_bias, m_w_attn_out, m_conv_w, m_conv_b, m_conv_ln_g, m_conv_ln_b, m_w_conv_out, m_w_mix_out, m_norm_ffn_g, m_w_ffn_in, m_w_ffn_out, v_w_ada, v_b_ada, v_norm_mix_g, v_w_in, v_q_norm_g, v_k_norm_g, v_attn_sinks, v_rel_bias, v_w_attn_out, v_conv_w, v_conv_b, v_conv_ln_g, v_conv_ln_b, v_w_conv_out, v_w_mix_out, v_norm_ffn_g, v_w_ffn_in, v_w_ffn_out):
    run = InOrder()
    xi, yi, ci = _position()
    chip = 2 * xi + yi
    me = 2 * chip + ci
    chip_arr = chip.astype(jnp.int32).reshape(1)
    core_arr = ci.astype(jnp.int32).reshape(1)
    where_arr = jnp.stack([chip, ci]).astype(jnp.int32)

    xe, tgt = x[0], loss_target[0]
    t, d = xe.shape
    hd = q_norm_g.shape[-1]
    nq = attn_sinks.shape[-1]
    aw = nq * hd
    ch = conv_b.shape[-1]
    in_width = N_CHIPS * w_in.shape[-1]
    kvw = (in_width - aw - 2 * ch - 2 * d) // 2
    nkv = kvw // hd
    dff = N_CHIPS * w_ffn_out.shape[1]
    off_k, off_v, off_ca = aw, aw + kvw, aw + 2 * kvw
    off_cb, off_ga, off_gc = off_ca + ch, off_ca + 2 * ch, off_ca + 2 * ch + d
    nc_ada = w_ada.shape[-1]
    ch_loc = conv_w.shape[-1]
    nj_ffn = w_ffn_in.shape[-1]
    perm_ffn = ffn_perm(N_CHIPS)

    big = {"w_in": w_in[0], "w_attn_out": w_attn_out[0], "w_conv_out": w_conv_out[0], "w_mix_out": w_mix_out[0],
           "w_ffn_in": w_ffn_in[0], "w_ffn_out": w_ffn_out[0]}
    moments = {"w_in": (m_w_in, v_w_in), "w_attn_out": (m_w_attn_out, v_w_attn_out),
               "w_conv_out": (m_w_conv_out, v_w_conv_out), "w_mix_out": (m_w_mix_out, v_w_mix_out),
               "w_ffn_in": (m_w_ffn_in, v_w_ffn_in), "w_ffn_out": (m_w_ffn_out, v_w_ffn_out)}
    gather_groups = {"in": ["w_in"], "branch_out": ["w_attn_out", "w_conv_out"], "mix_out": ["w_mix_out"],
                     "ffn_in": ["w_ffn_in"], "ffn_out": ["w_ffn_out"]}
    grads, deltas, new_m, new_v = {}, {}, {}, {}

    def gather_cast(gname):
        bufs = []
        for n in gather_groups[gname]:
            r, ncol = big[n].shape
            bufs.append(run(cast_into_slot, big[n], chip_arr, N_CHIPS, "cast_" + n).reshape(N_CHIPS * r, ncol))
        return bufs

    def gather_ici_start(gname, bufs):
        return run(exchange_start, "gather_ici_start_" + gname, bufs, 3 * len(bufs), plan_gather_ici)

    def gather_pass_on(gname, ici):
        landed = run(exchange_wait, "gather_ici_wait_" + gname, ici, plan_gather_ici)
        return run(exchange_start, "gather_d2d_start_" + gname, landed, 3 * len(landed), plan_gather_d2d)

    def gathered(gname, d2d):
        outs = run(exchange_wait, "gather_d2d_wait_" + gname, d2d, plan_gather_d2d)
        return [o.reshape(N_CHIPS, *big[n].shape) for o, n in zip(outs, gather_groups[gname])]

    def rs_pair_start(gname, names, partials):
        blocks = [g.reshape(N_CHIPS, 2, big[n].shape[0] // 2, big[n].shape[1]) for n, g in zip(names, partials)]
        land = [lax.empty((N_CHIPS,) + b.shape[2:], BF16) for b in blocks]
        return run(exchange_start, "pair_exchange_start_" + gname, blocks + land, N_CHIPS * len(blocks),
                   plan_pair_exchange)

    def rs_chip_start(gname, names, pair):
        nw = len(names)
        outs = run(exchange_wait, "pair_exchange_wait_" + gname, pair, plan_pair_exchange)
        sums = [run(pair_sum, g, r, core_arr, "pair_sum_" + n) for n, g, r in zip(names, outs[:nw], outs[nw:])]
        land = [lax.empty(s.shape, BF16) for s in sums]
        return run(exchange_start, "chip_exchange_start_" + gname, sums + land, 3 * nw, plan_chip_exchange)

    def rs_share_start(gname, names, chipx):
        nw = len(names)
        outs = run(exchange_wait, "chip_exchange_wait_" + gname, chipx, plan_chip_exchange)
        halves = [run(chip_sum, s, r, where_arr, "chip_sum_" + n) for n, s, r in zip(names, outs[:nw], outs[nw:])]
        return run(exchange_start, "pair_share_start_" + gname, halves, nw, plan_pair_share)

    def rs_finish(gname, names, share):
        fulls = run(exchange_wait, "pair_share_wait_" + gname, share, plan_pair_share)
        for n, g2 in zip(names, fulls):
            g, dl, nm, nv = run(adamw, big[n], g2.reshape(big[n].shape), moments[n][0][0], moments[n][1][0],
                                "adamw_" + n, copy_grad=True)
            grads[n], deltas[n], new_m[n], new_v[n] = g[None], dl[None], nm[None], nv[None]

    near, far = (0, 1), (2,)
    plan_ici_near = functools.partial(plan_gather_ici, peers=near)
    plan_ici_far, n_far = plan_gather_relay, 2
    plan_d2d_near = functools.partial(plan_gather_d2d, peers=near)
    plan_d2d_far = functools.partial(plan_gather_d2d, peers=far)
    bufs_in = gather_cast("in")
    row1, offs1 = _row_pack([c, conv_w[0].reshape(1, CONV_WIDTH * ch_loc)])
    got1 = run(allgather_small, row1, "allgather_cond")
    ici_near = run(exchange_start, "gather_ici_start_in_near", bufs_in, len(near), plan_ici_near)
    c_all = got1[:, 0, :d]
    conv_w_full = got1[0::2, 0, offs1[1]:offs1[1] + CONV_WIDTH * ch_loc].reshape(N_CHIPS, CONV_WIDTH, ch_loc)
    conv_w_full = jnp.transpose(conv_w_full, (1, 0, 2)).reshape(CONV_WIDTH, ch)
    conv_w_pad = jnp.pad(conv_w_full, ((0, 1), (0, 0)))
    c_t = jnp.transpose(c_all)
    mod_cols = run(ada_fwd, c_t, w_ada[0])
    rest_bufs = {gname: gather_cast(gname) for gname in gather_groups if gname != "in"}
    bucket = _t5_bucket_table()
    bucket_p, bucket_c = jnp.asarray(bucket[:, :BLOCK]), jnp.asarray(bucket[:, BLOCK:])
    bias_p, bias_c = run(bias_table, rel_bias, bucket_p, bucket_c)
    got2 = run(allgather_small, mod_cols, "allgather_mod")
    mod_all = got2.reshape(N_CHIPS, 2, N_DEV, nc_ada)[:, 0]
    mod = lax.dynamic_slice_in_dim(mod_all, me, 1, axis=1).reshape(1, N_CHIPS * nc_ada) + b_ada
    mod = jnp.pad(mod.reshape(N_MOD, d), ((0, SUBLANES - N_MOD), (0, 0)))

    landed = run(exchange_wait, "gather_ici_wait_in_near", ici_near, plan_ici_near)
    ici_far, d2d_near = run(exchange_start_many, "gather_start_in_far", [landed],
                            [(0, n_far, plan_ici_far), (0, len(near), plan_d2d_near)])
    h = run(pre_mix_fwd, xe, mod, norm_mix_g)
    ici = {}
    ici["branch_out"], ici["mix_out"], ici_near_ffn = run(
        exchange_start_many, "gather_ici_start_mid", [rest_bufs["branch_out"], rest_bufs["mix_out"], rest_bufs["ffn_in"]],
        [(0, 3 * len(rest_bufs["branch_out"]), plan_gather_ici), (1, 3, plan_gather_ici), (2, len(near), plan_ici_near)])
    landed = run(exchange_wait, "gather_d2d_wait_in_near", d2d_near, plan_d2d_near)
    landed = run(exchange_wait, "gather_ici_wait_in_far", ici_far, plan_ici_far, bufs=landed)
    d2d_far = run(exchange_start, "gather_d2d_start_in_far", landed, len(far), plan_d2d_far)
    landed = run(exchange_wait, "gather_d2d_wait_in_far", d2d_far, plan_d2d_far)
    wg_in = landed[0].reshape(N_CHIPS, *big["w_in"].shape)
    p = run(mm_nn, h, wg_in, tn=wg_in.shape[2], tk=d, out_dtype=BF16, name="mm_in")
    d2d_branch = gather_pass_on("branch_out", ici["branch_out"])

    sinks3 = attn_sinks.reshape(nq, 1, 1)
    attn_o = run(attn_fwd, p, bias_p, bias_c, sinks3, q_norm_g, k_norm_g, aw=aw, kvw=kvw)
    ca, cb = p[:, off_ca:off_cb], p[:, off_cb:off_ga]
    s_conv, co_conv = run(conv_fwd, ca, cb, conv_w_pad, conv_b, conv_ln_g, conv_ln_b)
    wg_attn_out, wg_conv_out = gathered("branch_out", d2d_branch)
    d2d_mix = gather_pass_on("mix_out", ici["mix_out"])
    landed = run(exchange_wait, "gather_ici_wait_ffn_in_near", ici_near_ffn, plan_ici_near)
    ici_far_ffn, d2d_near_ffn, ici["ffn_out"] = run(
        exchange_start_many, "gather_start_ffn_in_far", [landed, rest_bufs["ffn_out"]],
        [(0, n_far, plan_ici_far), (0, len(near), plan_d2d_near), (1, 3, plan_gather_ici)])
    y_attn, y_conv, merged = run(branch_out_merge, attn_o, s_conv, wg_attn_out, wg_conv_out, p, off_ga, off_gc)
    (wg_mix_out,) = gathered("mix_out", d2d_mix)
    wg_mix_out = wg_mix_out.reshape(1, d, d)
    o_m = run(mm_nn, merged, wg_mix_out, tn=_tile(d, 512), tk=d, out_dtype=BF16, name="mm_mix_out")
    landed = run(exchange_wait, "gather_d2d_wait_ffn_in_near", d2d_near_ffn, plan_d2d_near)
    landed = run(exchange_wait, "gather_ici_wait_ffn_in_far", ici_far_ffn, plan_ici_far, bufs=landed)
    d2d_far_ffn = run(exchange_start, "gather_d2d_start_ffn_in_far", landed, len(far), plan_d2d_far)
    x1, h2 = run(pre_ffn_fwd, xe, o_m, mod, norm_ffn_g)
    landed = run(exchange_wait, "gather_d2d_wait_ffn_in_far", d2d_far_ffn, plan_d2d_far)
    wg_ffn_in = landed[0].reshape(N_CHIPS, *big["w_ffn_in"].shape)
    f = run(mm_nn, h2, wg_ffn_in, tn=_tile(nj_ffn, 1408), tk=d, out_dtype=BF16, name="mm_ffn_in", perm=perm_ffn)
    d2d_ffn_out = gather_pass_on("ffn_out", ici["ffn_out"])
    act = run(swiglu_fwd, f, nj_ffn)
    (wg_ffn_out,) = gathered("ffn_out", d2d_ffn_out)
    wg_ffn_out = wg_ffn_out.reshape(1, dff, d)
    o_f = run(mm_nn, act, wg_ffn_out, tn=_tile(d, 512), tk=_tile(dff, 2816), out_dtype=BF16, name="mm_ffn_out")
    loss11, dy, dof, acc_l = run(loss_head, x1, o_f, tgt, mod)

    gw_ffn_out = run(mm_tn, act, dof, 1, tk=_tile(dff, 512), tn=d, name="mm_ffn_out_dw")
    px_ffn_out = rs_pair_start("ffn_out", ["w_ffn_out"], [gw_ffn_out])
    dact = run(mm_nt, dof, wg_ffn_out, tko=_tile(dff, 512), tn=d, out_dtype=BF16, name="mm_ffn_out_dx")
    cx_ffn_out = rs_chip_start("ffn_out", ["w_ffn_out"], px_ffn_out)
    df = run(swiglu_bwd, f, dact, nj_ffn)
    gw_ffn_in = run(mm_tn, h2, df, N_CHIPS, tk=d, tn=_tile(nj_ffn, 1408), name="mm_ffn_in_dw",
                    perm=perm_ffn)
    px_ffn_in = rs_pair_start("ffn_in", ["w_ffn_in"], [gw_ffn_in])
    dh2 = run(mm_nt, df, wg_ffn_in, tko=_tile(d, 512), tn=nj_ffn, out_dtype=BF16, name="mm_ffn_in_dx", perm=perm_ffn)
    sh_ffn_out = rs_share_start("ffn_out", ["w_ffn_out"], cx_ffn_out)
    cx_ffn_in = rs_chip_start("ffn_in", ["w_ffn_in"], px_ffn_in)
    dx1, dom, acc_f = run(pre_ffn_bwd, x1, dh2, dy, o_m, mod, norm_ffn_g)
    gw_mix_out = run(mm_tn, merged, dom, 1, tk=d, tn=_tile(d, 1024), name="mm_mix_out_dw")
    px_mix = rs_pair_start("mix_out", ["w_mix_out"], [gw_mix_out])
    dy_attn, dy_conv, dga, dgc = run(mix_out_dx_merge_bwd, dom, wg_mix_out.reshape(d, d), p, y_attn, y_conv,
                                     off_ga, off_gc)
    rs_finish("ffn_out", ["w_ffn_out"], sh_ffn_out)
    cx_mix = rs_chip_start("mix_out", ["w_mix_out"], px_mix)
    gw_attn_out = run(mm_tn, attn_o, dy_attn, N_CHIPS, tk=aw, tn=_tile(wg_attn_out.shape[2], 512),
                      name="mm_attn_out_dw")
    gw_conv_out = run(mm_tn, s_conv, dy_conv, N_CHIPS, tk=ch, tn=_tile(wg_conv_out.shape[2], 512),
                      name="mm_conv_out_dw")
    ac_names = ["w_attn_out", "w_conv_out"]
    px_ac = rs_pair_start("attn_conv_out", ac_names, [gw_attn_out, gw_conv_out])
    dattn_o = run(mm_nt, dy_attn, wg_attn_out, tko=_tile(aw, 1024), tn=_tile(wg_attn_out.shape[2], 512),
                  out_dtype=BF16, name="mm_attn_out_dx")
    ds_conv = run(mm_nt, dy_conv, wg_conv_out, tko=_tile(ch, 1024), tn=_tile(wg_conv_out.shape[2], 512),
                  out_dtype=BF16, name="mm_conv_out_dx")
    cx_ac = rs_chip_start("attn_conv_out", ac_names, px_ac)
    dca, dcb, dconv_w, dconv_vec = run(conv_bwd, ca, cb, co_conv, ds_conv, conv_w_pad, conv_ln_g, conv_ln_b)
    sh_ffn_in = rs_share_start("ffn_in", ["w_ffn_in"], cx_ffn_in)
    dqkv, dbp, dbc, dsinks, dqg, dkg = run(attn_bwd, p, bias_p, bias_c, sinks3, q_norm_g, k_norm_g, dattn_o,
                                           aw=aw, kvw=kvw)
    sh_mix = rs_share_start("mix_out", ["w_mix_out"], cx_mix)
    sh_ac = rs_share_start("attn_conv_out", ac_names, cx_ac)
    drel = run(bias_table_bwd, dbp, dbc, bucket_p, bucket_c).reshape(NUM_BUCKETS, nq)
    dp = jnp.concatenate([dqkv, dca, dcb, dga, dgc], axis=1)
    gw_in = run(mm_tn, h, dp, N_CHIPS, tk=d, tn=wg_in.shape[2], name="mm_in_dw")
    px_in = rs_pair_start("in", ["w_in"], [gw_in])
    dh = run(mm_nt, dp, wg_in, tko=_tile(d, 1024), tn=wg_in.shape[2], out_dtype=BF16, name="mm_in_dx")
    grad_x, acc_m = run(pre_mix_bwd, xe, dh, dx1, mod, norm_mix_g)

    dmod = jnp.concatenate([acc_m[0:1], acc_m[1:2], acc_f[3:4], acc_f[0:1], acc_f[1:2], acc_l[0:1]], axis=1)
    small_names = ["b_ada", "norm_mix_g", "q_norm_g", "k_norm_g", "attn_sinks", "rel_bias", "conv_b", "conv_ln_g",
                   "conv_ln_b", "norm_ffn_g"]
    small_w = [b_ada, norm_mix_g, q_norm_g, k_norm_g, attn_sinks, rel_bias, conv_b, conv_ln_g, conv_ln_b, norm_ffn_g]
    small_m = [m_b_ada, m_norm_mix_g, m_q_norm_g, m_k_norm_g, m_attn_sinks, m_rel_bias, m_conv_b, m_conv_ln_g,
               m_conv_ln_b, m_norm_ffn_g]
    small_v = [v_b_ada, v_norm_mix_g, v_q_norm_g, v_k_norm_g, v_attn_sinks, v_rel_bias, v_conv_b, v_conv_ln_g,
               v_conv_ln_b, v_norm_ffn_g]
    small_g = [dmod, acc_m[2:3], dqg, dkg, dsinks.reshape(1, nq), drel.reshape(1, NUM_BUCKETS * nq),
               dconv_vec[0:1], dconv_vec[1:2], dconv_vec[2:3], acc_f[2:3]]
    row3, offs3 = _row_pack(small_g + [dconv_w[:CONV_WIDTH].reshape(1, CONV_WIDTH * ch), loss11])
    got3 = run(allgather_small, row3, "allgather_small_grads")
    cx_in = rs_chip_start("in", ["w_in"], px_in)
    as_row = lambda a: a.reshape(1, -1)
    outs3 = run(small_sum_adamw, got3, offs3, [as_row(a) for a in small_w], [as_row(a) for a in small_m],
                [as_row(a) for a in small_v], [CONV_WIDTH * ch, 1])
    for i, (n, w) in enumerate(zip(small_names, small_w)):
        grads[n], deltas[n], new_m[n], new_v[n] = (o.reshape(w.shape) for o in outs3[4 * i:4 * i + 4])
    g_conv_w_all, loss_sum = outs3[-2].reshape(CONV_WIDTH, ch), outs3[-1]

    g_conv_w = lax.dynamic_slice_in_dim(g_conv_w_all, chip * ch_loc, ch_loc, axis=1)
    grads["conv_w"] = g_conv_w[None]
    dl, nm, nv = run(adamw, conv_w[0], g_conv_w, m_conv_w[0], v_conv_w[0], "adamw_conv_w")
    deltas["conv_w"], new_m["conv_w"], new_v["conv_w"] = dl[None], nm[None], nv[None]

    dmod_all = got3[:, 0, :N_MOD * d]
    dmod_cols = lax.dynamic_slice_in_dim(dmod_all, chip * nc_ada, nc_ada, axis=1)
    g_ada, dl, nm, nv = run(ada_grad_adamw, c_t, dmod_cols, w_ada[0], m_w_ada[0], v_w_ada[0])
    grads["w_ada"], deltas["w_ada"], new_m["w_ada"], new_v["w_ada"] = g_ada[None], dl[None], nm[None], nv[None]

    rs_finish("ffn_in", ["w_ffn_in"], sh_ffn_in)
    rs_finish("mix_out", ["w_mix_out"], sh_mix)
    rs_finish("attn_conv_out", ac_names, sh_ac)
    sh_in = rs_share_start("in", ["w_in"], cx_in)
    rs_finish("in", ["w_in"], sh_in)

    loss = loss_sum[0, 0]
    order = ["w_ada", "b_ada", "norm_mix_g", "w_in", "q_norm_g", "k_norm_g", "attn_sinks", "rel_bias", "w_attn_out",
             "conv_w", "conv_b", "conv_ln_g", "conv_ln_b", "w_conv_out", "w_mix_out", "norm_ffn_g", "w_ffn_in",
             "w_ffn_out"]
    return (loss, grad_x[None], *[grads[n] for n in order], *[deltas[n] for n in order],
            *[new_m[n] for n in order], *[new_v[n] for n in order])
```

```python
import functools
import math
from typing import Any, NamedTuple

import jax
import jax.numpy as jnp
import numpy as np
from jax import lax
from jax.experimental import pallas as pl
from jax.experimental.pallas import tpu as pltpu

F32 = jnp.float32
BF16 = jnp.bfloat16
MESH = pl.DeviceIdType.MESH

V7X_VMEM_BYTES = 64 * 1024 * 1024
VMEM_LIMIT = V7X_VMEM_BYTES - 8 * 1024 * 1024
LANES = 128
SUBLANES = 8
BF16_SUBLANES = 16

EPS = 1e-6
WINDOW = 128
BLOCK = 128
NUM_BUCKETS = 32
MAX_EXACT = NUM_BUCKETS // 2
MAX_DISTANCE = 128
CONV_WIDTH = 31
CONV_HALO = 32
ADAM_LR = 0.001
ADAM_B1 = 0.9
ADAM_B2 = 0.999
ADAM_EPS = 1e-08
ADAM_WD = 0.01
ADAM_STEP = 10
N_MOD = 6
SH_M, SC_M, GT_M, SH_F, SC_F, GT_F = range(6)

N_CHIPS = 4
N_DEV = 8

_ANY = pl.BlockSpec(memory_space=pl.ANY)
_VMEM = pl.BlockSpec(memory_space=pltpu.VMEM)
_SMEM = pl.BlockSpec(memory_space=pltpu.SMEM)
_HBM = pl.BlockSpec(memory_space=pltpu.HBM)
_SEM = pl.BlockSpec(memory_space=pltpu.SEMAPHORE)
_EFFECT = pltpu.SideEffectType.DATAFLOW_SIDE_EFFECTING


class InOrder:
    def __init__(self):
        self.token = None

    def __call__(self, fn, *args, **kw):
        return fn(*args, dep=self, **kw)


def _pallas(body, args, *, in_specs, out_specs, out_shape, name, dep=None, grid=(), n_prefetch=0, scratch=(),
            sem=None, **kw):
    n_lead = n_prefetch + len(in_specs)
    in_specs, args = list(in_specs), list(args)
    single = not isinstance(out_shape, (list, tuple))
    out_shapes = [out_shape] if single else list(out_shape)
    out_specs = [out_specs] if single else list(out_specs)
    if dep is not None:
        inner, n_out, takes = body, len(out_shapes), dep.token is not None

        def body(*refs):
            rest = refs[n_lead + (1 if takes else 0):]
            rest[n_out][...] = jnp.zeros((SUBLANES, LANES), F32)
            return inner(*refs[:n_lead], *rest[:n_out], *rest[n_out + 1:])

        if takes:
            in_specs.append(_ANY)
            args.append(dep.token)
        out_shapes.append(jax.ShapeDtypeStruct((SUBLANES, LANES), F32))
        out_specs.append(pl.BlockSpec((SUBLANES, LANES), lambda *_: (0, 0)))
    params = kw.pop("compiler_params", None)
    if params is None:
        params = pltpu.CompilerParams(dimension_semantics=sem, vmem_limit_bytes=VMEM_LIMIT)
    outs = pl.pallas_call(
        body,
        grid_spec=pltpu.PrefetchScalarGridSpec(num_scalar_prefetch=n_prefetch, grid=grid, in_specs=in_specs,
                                               out_specs=out_specs, scratch_shapes=list(scratch)),
        out_shape=out_shapes, compiler_params=params, name=name, **kw,
    )(*args)
    if dep is not None:
        dep.token = outs[-1]
        outs = outs[:-1]
    return outs[0] if single else list(outs)


def _tile(n, pref, unit=LANES):
    best = None
    for t in range(unit, min(n, pref) + 1, unit):
        if n % t == 0:
            best = t
    return best if best is not None else n


def _sigmoid(v):
    return 1.0 / (1.0 + jnp.exp(-v.astype(F32)))


ROW_CHUNK = 512
TILE_N = 512
TILE_WIDE = 1024
EW_ELEMS = 512 * 1024
EW_ELEMS_MANY = 256 * 1024


def _row_chunks(m, unit=SUBLANES):
    step = _tile(m, ROW_CHUNK, unit)
    return [(s, step) for s in range(0, m, step)]


def _ew_tiles(r, n, unit=SUBLANES, elems=EW_ELEMS):
    return _tile(r, max(unit, elems // n), unit), n


def _block_pos(j, perm):
    if perm is None:
        return j
    pos = 0
    for a, p in enumerate(perm):
        pos = pos + jnp.where(j == a, p, 0)
    return pos


def mm_nn(a, w, *, tn, tk, out_dtype, name, perm=None, dep=None):
    m, k = a.shape
    j, k2, nj = w.shape
    assert k == k2 and nj % tn == 0 and k % tk == 0
    npj, nk = nj // tn, k // tk

    def body(a_ref, w_ref, o_ref, *scratch):
        kk = pl.program_id(1)
        for s, sz in _row_chunks(m):
            rows = pl.ds(s, sz)
            p = jnp.dot(a_ref[rows, :], w_ref[...], preferred_element_type=F32)
            if nk == 1:
                o_ref[rows, :] = p.astype(out_dtype)
            else:
                acc = scratch[0]

                @pl.when(kk == 0)
                def _():
                    acc[rows, :] = p

                @pl.when(kk > 0)
                def _():
                    acc[rows, :] += p

                @pl.when(kk == nk - 1)
                def _():
                    o_ref[rows, :] = acc[rows, :].astype(out_dtype)

    return _pallas(
        body, [a, w], dep=dep, grid=(j * npj, nk),
        in_specs=[
            pl.BlockSpec((m, tk), lambda n, kk: (0, kk)),
            pl.BlockSpec((None, tk, tn), lambda n, kk: (n // npj, kk, n % npj)),
        ],
        out_specs=pl.BlockSpec((m, tn), lambda n, kk: (0, _block_pos(n // npj, perm) * npj + n % npj)),
        out_shape=jax.ShapeDtypeStruct((m, j * nj), out_dtype),
        scratch=[pltpu.VMEM((m, tn), F32)] if nk > 1 else [],
        sem=("parallel", "arbitrary"), name=name)


def mm_nt(g, w, *, tko, tn, name, out_dtype=F32, perm=None, dep=None):
    m, n = g.shape
    j, k, nj = w.shape
    assert n == j * nj and nj % tn == 0 and k % tko == 0
    npj, nr = nj // tn, n // tn
    in_place = out_dtype == F32

    def body(g_ref, w_ref, o_ref, *scratch):
        r = pl.program_id(1)
        acc = o_ref if in_place else (scratch[0] if nr > 1 else None)
        for s, sz in _row_chunks(m):
            rows = pl.ds(s, sz)
            p = lax.dot_general(g_ref[rows, :], w_ref[...], (((1,), (1,)), ((), ())), preferred_element_type=F32)
            if acc is None:
                o_ref[rows, :] = p.astype(out_dtype)
                continue

            @pl.when(r == 0)
            def _():
                acc[rows, :] = p

            @pl.when(r > 0)
            def _():
                acc[rows, :] += p

            if not in_place:
                @pl.when(r == nr - 1)
                def _():
                    o_ref[rows, :] = acc[rows, :].astype(out_dtype)

    return _pallas(
        body, [g, w], dep=dep, grid=(k // tko, nr),
        in_specs=[
            pl.BlockSpec((m, tn), lambda ko, r: (0, _block_pos(r // npj, perm) * npj + r % npj)),
            pl.BlockSpec((None, tko, tn), lambda ko, r: (r // npj, ko, r % npj)),
        ],
        out_specs=pl.BlockSpec((m, tko), lambda ko, r: (0, ko)),
        out_shape=jax.ShapeDtypeStruct((m, k), out_dtype),
        scratch=[pltpu.VMEM((m, tko), F32)] if (nr > 1 and not in_place) else [],
        sem=("parallel", "arbitrary"), name=name)


def mm_tn(a, g, n_blocks, *, tk, tn, name, perm=None, dep=None):
    m, k = a.shape
    m2, n = g.shape
    nj = n // n_blocks
    assert m == m2 and nj % tn == 0 and k % tk == 0
    npj = nj // tn

    def body(a_ref, g_ref, o_ref):
        for s, sz in _row_chunks(tk, LANES):
            p = lax.dot_general(a_ref[:, pl.ds(s, sz)], g_ref[...], (((0,), (0,)), ((), ())),
                                preferred_element_type=F32)
            o_ref[pl.ds(s, sz), :] = p.astype(BF16)

    return _pallas(
        body, [a, g], dep=dep, grid=(k // tk, n // tn),
        in_specs=[
            pl.BlockSpec((m, tk), lambda kk, nn: (0, kk)),
            pl.BlockSpec((m, tn), lambda kk, nn: (0, _block_pos(nn // npj, perm) * npj + nn % npj)),
        ],
        out_specs=pl.BlockSpec((None, tk, tn), lambda kk, nn: (nn // npj, kk, nn % npj)),
        out_shape=jax.ShapeDtypeStruct((n_blocks, k, nj), BF16),
        sem=("parallel", "parallel"), name=name)


ROW_TILE = 256


def _row_spec(tr, width):
    return pl.BlockSpec((tr, width), lambda i: (i, 0))


def _full_spec(shape):
    return pl.BlockSpec(shape, lambda *_: (0,) * len(shape))


def _rms(xv):
    return lax.rsqrt(jnp.mean(xv * xv, axis=-1, keepdims=True) + EPS)


def _mod_row(mod_ref, row):
    return mod_ref[pl.ds(row, 1), :]


def pre_mix_fwd(x, mod, gain, dep=None):
    t, d = x.shape
    tr = _tile(t, ROW_TILE, SUBLANES)

    def body(x_ref, mod_ref, g_ref, h_ref):
        xv = x_ref[...]
        y = xv * _rms(xv) * g_ref[...]
        h_ref[...] = (y * (1.0 + _mod_row(mod_ref, SC_M)) + _mod_row(mod_ref, SH_M)).astype(BF16)

    return _pallas(
        body, [x, mod, gain], dep=dep, grid=(t // tr,),
        in_specs=[_row_spec(tr, d), _full_spec(mod.shape), _full_spec(gain.shape)],
        out_specs=_row_spec(tr, d),
        out_shape=jax.ShapeDtypeStruct((t, d), BF16),
        sem=("parallel",), name="pre_mix_fwd")


def pre_ffn_fwd(x, o_m, mod, gain, dep=None):
    t, d = x.shape
    tr = _tile(t, ROW_TILE, SUBLANES)

    def body(x_ref, om_ref, mod_ref, g_ref, x1_ref, h_ref):
        x1 = x_ref[...] + _mod_row(mod_ref, GT_M) * om_ref[...]
        x1_ref[...] = x1
        y = x1 * _rms(x1) * g_ref[...]
        h_ref[...] = (y * (1.0 + _mod_row(mod_ref, SC_F)) + _mod_row(mod_ref, SH_F)).astype(BF16)

    return _pallas(
        body, [x, o_m, mod, gain], dep=dep, grid=(t // tr,),
        in_specs=[_row_spec(tr, d), _row_spec(tr, d), _full_spec(mod.shape), _full_spec(gain.shape)],
        out_specs=[_row_spec(tr, d), _row_spec(tr, d)],
        out_shape=[jax.ShapeDtypeStruct((t, d), F32), jax.ShapeDtypeStruct((t, d), BF16)],
        sem=("parallel",), name="pre_ffn_fwd")


def loss_head(x1, o_f, target, mod, dep=None):
    t, d = x1.shape
    tr = _tile(t, ROW_TILE, SUBLANES)

    def body(x1_ref, of_ref, tg_ref, mod_ref, loss_ref, dy_ref, dof_ref, acc_ref):
        i = pl.program_id(0)
        gt = _mod_row(mod_ref, GT_F)
        of = of_ref[...].astype(F32)
        err = x1_ref[...] + gt * of - tg_ref[...]
        dy = err * (1.0 / d)
        dy_ref[...] = dy.astype(BF16)
        dof_ref[...] = (dy * gt).astype(BF16)
        part = (0.5 / d) * jnp.sum(jnp.sum(err * err, axis=1, keepdims=True), axis=0, keepdims=True)
        dgt = jnp.sum(dy * of, axis=0, keepdims=True)

        @pl.when(i == 0)
        def _():
            loss_ref[...] = jnp.zeros_like(loss_ref)
            acc_ref[...] = jnp.zeros_like(acc_ref)

        loss_ref[...] += part
        acc_ref[pl.ds(0, 1), :] += dgt

    return _pallas(
        body, [x1, o_f, target, mod], dep=dep, grid=(t // tr,),
        in_specs=[_row_spec(tr, d), _row_spec(tr, d), _row_spec(tr, d), _full_spec(mod.shape)],
        out_specs=[_full_spec((1, 1)), _row_spec(tr, d), _row_spec(tr, d), _full_spec((SUBLANES, d))],
        out_shape=[jax.ShapeDtypeStruct((1, 1), F32), jax.ShapeDtypeStruct((t, d), BF16),
                   jax.ShapeDtypeStruct((t, d), BF16), jax.ShapeDtypeStruct((SUBLANES, d), F32)],
        sem=("arbitrary",), name="loss_head")


def _norm_bwd(xv, dh, sc, gain):
    rstd = _rms(xv)
    yn = xv * rstd
    dsh = jnp.sum(dh, axis=0, keepdims=True)
    dsc = jnp.sum(dh * (yn * gain), axis=0, keepdims=True)
    dgain = jnp.sum(dh * (1.0 + sc) * yn, axis=0, keepdims=True)
    dyn = dh * ((1.0 + sc) * gain)
    dx = rstd * (dyn - yn * jnp.mean(dyn * yn, axis=-1, keepdims=True))
    return dx, dsh, dsc, dgain


def pre_ffn_bwd(x1, dh2, dy, o_m, mod, gain, dep=None):
    t, d = x1.shape
    tr = _tile(t, ROW_TILE, SUBLANES)

    def body(x1_ref, dh_ref, dy_ref, om_ref, mod_ref, g_ref, dx1_ref, dom_ref, acc_ref):
        i = pl.program_id(0)
        dxn, dsh, dsc, dgain = _norm_bwd(x1_ref[...], dh_ref[...].astype(F32), _mod_row(mod_ref, SC_F), g_ref[...])
        dx1 = dy_ref[...] + dxn
        dx1_ref[...] = dx1
        dom_ref[...] = (dx1 * _mod_row(mod_ref, GT_M)).astype(BF16)
        dgt = jnp.sum(dx1 * om_ref[...], axis=0, keepdims=True)

        @pl.when(i == 0)
        def _():
            acc_ref[...] = jnp.zeros_like(acc_ref)

        acc_ref[pl.ds(0, 1), :] += dsh
        acc_ref[pl.ds(1, 1), :] += dsc
        acc_ref[pl.ds(2, 1), :] += dgain
        acc_ref[pl.ds(3, 1), :] += dgt

    return _pallas(
        body, [x1, dh2, dy, o_m, mod, gain], dep=dep, grid=(t // tr,),
        in_specs=[_row_spec(tr, d)] * 4 + [_full_spec(mod.shape), _full_spec(gain.shape)],
        out_specs=[_row_spec(tr, d), _row_spec(tr, d), _full_spec((SUBLANES, d))],
        out_shape=[jax.ShapeDtypeStruct((t, d), F32), jax.ShapeDtypeStruct((t, d), BF16),
                   jax.ShapeDtypeStruct((SUBLANES, d), F32)],
        sem=("arbitrary",), name="pre_ffn_bwd")


def pre_mix_bwd(x, dh, dx1, mod, gain, dep=None):
    t, d = x.shape
    tr = _tile(t, ROW_TILE, SUBLANES)

    def body(x_ref, dh_ref, dx1_ref, mod_ref, g_ref, gx_ref, acc_ref):
        i = pl.program_id(0)
        dxn, dsh, dsc, dgain = _norm_bwd(x_ref[...], dh_ref[...].astype(F32), _mod_row(mod_ref, SC_M), g_ref[...])
        gx_ref[...] = dx1_ref[...] + dxn

        @pl.when(i == 0)
        def _():
            acc_ref[...] = jnp.zeros_like(acc_ref)

        acc_ref[pl.ds(0, 1), :] += dsh
        acc_ref[pl.ds(1, 1), :] += dsc
        acc_ref[pl.ds(2, 1), :] += dgain

    return _pallas(
        body, [x, dh, dx1, mod, gain], dep=dep, grid=(t // tr,),
        in_specs=[_row_spec(tr, d)] * 3 + [_full_spec(mod.shape), _full_spec(gain.shape)],
        out_specs=[_row_spec(tr, d), _full_spec((SUBLANES, d))],
        out_shape=[jax.ShapeDtypeStruct((t, d), F32), jax.ShapeDtypeStruct((SUBLANES, d), F32)],
        sem=("arbitrary",), name="pre_mix_bwd")


def branch_out_merge(attn_o, s_conv, w_attn, w_conv, p, off_ga, off_gc, dep=None):
    t = attn_o.shape[0]
    j, ka, nj = w_attn.shape
    kc = w_conv.shape[1]
    assert off_ga % nj == 0 and off_gc % nj == 0

    def body(a_ref, s_ref, wa_ref, wc_ref, ga_ref, gc_ref, ya_ref, yc_ref, m_ref):
        for s, sz in _row_chunks(t):
            rows = pl.ds(s, sz)
            ya = jnp.dot(a_ref[rows, :], wa_ref[...], preferred_element_type=F32)
            yc = jnp.dot(s_ref[rows, :], wc_ref[...], preferred_element_type=F32)
            ya_ref[rows, :] = ya.astype(BF16)
            yc_ref[rows, :] = yc.astype(BF16)
            m_ref[rows, :] = (_sigmoid(ga_ref[rows, :]) * ya + _sigmoid(gc_ref[rows, :]) * yc).astype(BF16)

    col = pl.BlockSpec((t, nj), lambda b: (0, b))
    return _pallas(
        body, [attn_o, s_conv, w_attn, w_conv, p, p], dep=dep, grid=(j,),
        in_specs=[pl.BlockSpec((t, ka), lambda b: (0, 0)), pl.BlockSpec((t, kc), lambda b: (0, 0)),
                  pl.BlockSpec((None, ka, nj), lambda b: (b, 0, 0)), pl.BlockSpec((None, kc, nj), lambda b: (b, 0, 0)),
                  pl.BlockSpec((t, nj), lambda b: (0, off_ga // nj + b)),
                  pl.BlockSpec((t, nj), lambda b: (0, off_gc // nj + b))],
        out_specs=[col] * 3,
        out_shape=[jax.ShapeDtypeStruct((t, j * nj), BF16)] * 3,
        sem=("parallel",), name="branch_out_merge")


def mix_out_dx_merge_bwd(dom, w_mix, p, y_attn, y_conv, off_ga, off_gc, dep=None):
    t, d = y_attn.shape
    cw = math.gcd(math.gcd(off_ga, off_gc), math.gcd(d, TILE_N // 2))

    def body(dom_ref, w_ref, ga_ref, gc_ref, ya_ref, yc_ref, dya_ref, dyc_ref, dga_ref, dgc_ref):
        for s, sz in _row_chunks(t):
            rows = pl.ds(s, sz)
            dm = lax.dot_general(dom_ref[rows, :], w_ref[...], (((1,), (1,)), ((), ())), preferred_element_type=F32)
            sa = _sigmoid(ga_ref[rows, :])
            sc = _sigmoid(gc_ref[rows, :])
            dya_ref[rows, :] = (dm * sa).astype(BF16)
            dyc_ref[rows, :] = (dm * sc).astype(BF16)
            dga_ref[rows, :] = (dm * ya_ref[rows, :] * sa * (1.0 - sa)).astype(BF16)
            dgc_ref[rows, :] = (dm * yc_ref[rows, :] * sc * (1.0 - sc)).astype(BF16)

    col = pl.BlockSpec((t, cw), lambda j: (0, j))
    return _pallas(
        body, [dom, w_mix, p, p, y_attn, y_conv], dep=dep, grid=(d // cw,),
        in_specs=[pl.BlockSpec((t, d), lambda j: (0, 0)), pl.BlockSpec((cw, d), lambda j: (j, 0)),
                  pl.BlockSpec((t, cw), lambda j: (0, off_ga // cw + j)),
                  pl.BlockSpec((t, cw), lambda j: (0, off_gc // cw + j)), col, col],
        out_specs=[col] * 4,
        out_shape=[jax.ShapeDtypeStruct((t, d), BF16)] * 4,
        sem=("parallel",), name="mm_mix_out_dx_merge_bwd")


def ffn_perm(n_blocks):
    half = n_blocks // 2
    return tuple(2 * j if j < half else 2 * (j - half) + 1 for j in range(n_blocks))


def swiglu_fwd(f, nj, dep=None):
    t, two = f.shape
    tr = _tile(t, ROW_TILE, SUBLANES)
    npair = two // (2 * nj)

    def body(f_ref, o_ref):
        g = f_ref[:, :nj].astype(F32)
        u = f_ref[:, nj:].astype(F32)
        o_ref[...] = (g * _sigmoid(g) * u).astype(BF16)

    return _pallas(
        body, [f], dep=dep, grid=(t // tr, npair),
        in_specs=[pl.BlockSpec((tr, 2 * nj), lambda i, j: (i, j))],
        out_specs=pl.BlockSpec((tr, nj), lambda i, j: (i, j)),
        out_shape=jax.ShapeDtypeStruct((t, two // 2), BF16),
        sem=("parallel", "parallel"), name="swiglu_fwd")


def swiglu_bwd(f, dact, nj, dep=None):
    t, two = f.shape
    tr = _tile(t, ROW_TILE, SUBLANES)
    npair = two // (2 * nj)

    def body(f_ref, da_ref, o_ref):
        g = f_ref[:, :nj].astype(F32)
        u = f_ref[:, nj:].astype(F32)
        da = da_ref[...]
        s = _sigmoid(g)
        o_ref[:, :nj] = (da * u * (s * (1.0 + g * (1.0 - s)))).astype(BF16)
        o_ref[:, nj:] = (da * (g * s)).astype(BF16)

    return _pallas(
        body, [f, dact], dep=dep, grid=(t // tr, npair),
        in_specs=[pl.BlockSpec((tr, 2 * nj), lambda i, j: (i, j)), pl.BlockSpec((tr, nj), lambda i, j: (i, j))],
        out_specs=pl.BlockSpec((tr, 2 * nj), lambda i, j: (i, j)),
        out_shape=jax.ShapeDtypeStruct((t, two), BF16),
        sem=("parallel", "parallel"), name="swiglu_bwd")


def _t5_bucket_table():
    q_off = np.arange(BLOCK)
    k_off = np.arange(2 * BLOCK)
    dist = q_off[:, None] + BLOCK - k_off[None, :]
    n = np.maximum(dist, 0)
    nf = np.maximum(n, 1).astype(np.float32)
    large = MAX_EXACT + (np.log(nf / np.float32(MAX_EXACT)) / np.float32(math.log(MAX_DISTANCE / MAX_EXACT))
                         * np.float32(NUM_BUCKETS - MAX_EXACT)).astype(np.int32)
    large = np.minimum(large, NUM_BUCKETS - 1)
    bucket = np.where(n < MAX_EXACT, n, large).astype(np.int32)
    allowed = (dist >= 0) & (dist < WINDOW)
    return np.where(allowed, bucket, -1).astype(np.int32)


def bias_table(rel_bias, bucket_p, bucket_c, dep=None):
    nb, nq = rel_bias.shape

    def body(rb_ref, bkp_ref, bkc_ref, op_ref, oc_ref):
        for bk_ref, o_ref in ((bkp_ref, op_ref), (bkc_ref, oc_ref)):
            bk = bk_ref[...]
            for h in range(nq):
                acc = jnp.full(bk.shape, -jnp.inf, F32)
                for b in range(nb):
                    acc = jnp.where(bk == b, rb_ref[b, h], acc)
                o_ref[h] = acc

    return _pallas(
        body, [rel_bias, bucket_p, bucket_c], dep=dep,
        in_specs=[_SMEM, _VMEM, _VMEM], out_specs=[_VMEM, _VMEM],
        out_shape=[jax.ShapeDtypeStruct((nq,) + bucket_p.shape, F32)] * 2,
        name="bias_table")


def bias_table_bwd(dbp, dbc, bucket_p, bucket_c, dep=None):
    nq = dbp.shape[0]

    def body(dbp_ref, dbc_ref, bkp_ref, bkc_ref, o_ref):
        bkp, bkc = bkp_ref[...][None], bkc_ref[...][None]
        dp, dc = dbp_ref[...], dbc_ref[...]
        for b in range(NUM_BUCKETS):
            sel = jnp.where(bkp == b, dp, 0.0) + jnp.where(bkc == b, dc, 0.0)
            o_ref[b] = jnp.sum(jnp.sum(sel, axis=2, keepdims=True), axis=1, keepdims=True)

    return _pallas(
        body, [dbp, dbc, bucket_p, bucket_c], dep=dep,
        in_specs=[_VMEM] * 4, out_specs=_VMEM,
        out_shape=jax.ShapeDtypeStruct((NUM_BUCKETS, nq, 1, 1), F32),
        name="bias_table_bwd")


_BNT = (((2,), (2,)), ((0,), (0,)))
_BNN = (((2,), (1,)), ((0,), (0,)))
_BTN = (((1,), (1,)), ((0,), (0,)))


@jax.custom_vjp
def _bdot_nt(a, b):
    return lax.dot_general(a.astype(BF16), b.astype(BF16), _BNT, preferred_element_type=F32)


def _bdot_nt_fwd(a, b):
    return _bdot_nt(a, b), (a, b)


def _bdot_nt_bwd(res, g):
    a, b = res
    gb = g.astype(BF16)
    da = lax.dot_general(gb, b.astype(BF16), _BNN, preferred_element_type=F32)
    db = lax.dot_general(gb, a.astype(BF16), _BTN, preferred_element_type=F32)
    return da, db


_bdot_nt.defvjp(_bdot_nt_fwd, _bdot_nt_bwd)


@jax.custom_vjp
def _bdot_nn(a, b):
    return lax.dot_general(a.astype(BF16), b.astype(BF16), _BNN, preferred_element_type=F32)


def _bdot_nn_fwd(a, b):
    return _bdot_nn(a, b), (a, b)


def _bdot_nn_bwd(res, g):
    a, b = res
    gb = g.astype(BF16)
    da = lax.dot_general(gb, b.astype(BF16), _BNT, preferred_element_type=F32)
    db = lax.dot_general(a.astype(BF16), gb, _BTN, preferred_element_type=F32)
    return da, db


_bdot_nn.defvjp(_bdot_nn_fwd, _bdot_nn_bwd)


def _attn_math(q, kp, kc, vp, vc, bp, bc, sinks, qg, kg, *, prev_ok, scale):
    h, rows, _ = q.shape
    b = kp.shape[1]
    qn = q * _rms(q) * qg
    kpn = kp * _rms(kp) * kg
    kcn = kc * _rms(kc) * kg
    lp = _bdot_nt(qn, kpn) * scale + bp.reshape(h, rows, b)
    lc = _bdot_nt(qn, kcn) * scale + bc.reshape(h, rows, b)
    lp = jnp.where(prev_ok, lp, -jnp.inf)
    sink = jnp.broadcast_to(sinks, (sinks.shape[0], b, 1)).reshape(h, rows, 1)
    m = jnp.maximum(jnp.maximum(jnp.max(lp, axis=-1, keepdims=True), jnp.max(lc, axis=-1, keepdims=True)), sink)
    m = lax.stop_gradient(m)
    pp = jnp.exp(lp - m)
    pc = jnp.exp(lc - m)
    den = jnp.sum(pp, axis=-1, keepdims=True) + jnp.sum(pc, axis=-1, keepdims=True) + jnp.exp(sink - m)
    inv = 1.0 / den
    return _bdot_nn(pp * inv, vp) + _bdot_nn(pc * inv, vc)


def _attn_specs(p, aw, kvw, nq, hd, nblk, reverse):
    assert aw % (2 * kvw) == 0
    kv_col = aw // (2 * kvw)

    def blk(n):
        return nblk - 1 - n if reverse else n

    return [
        pl.BlockSpec((BLOCK, aw), lambda n: (blk(n), 0)),
        pl.BlockSpec((BLOCK, 2 * kvw), lambda n: (jnp.maximum(blk(n) - 1, 0), kv_col)),
        pl.BlockSpec((BLOCK, 2 * kvw), lambda n: (blk(n), kv_col)),
        _full_spec((nq, BLOCK, BLOCK)), _full_spec((nq, BLOCK, BLOCK)), _full_spec((nq, 1, 1)),
        _full_spec((1, hd)), _full_spec((1, hd)),
    ]


def _head_major(ref, n_heads, grp, hd, offset=0):
    return jnp.stack([
        jnp.concatenate([ref[:, pl.ds(offset + (grp * h + g) * hd, hd)].astype(F32) for g in range(grp)], axis=0)
        for h in range(n_heads)])


def _attn_inputs(nkv, grp, hd, kvw, q_ref, kvp_ref, kvc_ref):
    return (_head_major(q_ref, nkv, grp, hd), _head_major(kvp_ref, nkv, 1, hd), _head_major(kvc_ref, nkv, 1, hd),
            _head_major(kvp_ref, nkv, 1, hd, kvw), _head_major(kvc_ref, nkv, 1, hd, kvw))


def attn_fwd(p, bias_p, bias_c, sinks, qg, kg, *, aw, kvw, dep=None):
    t, hd = p.shape[0], qg.shape[-1]
    nq, nkv, nblk = aw // hd, kvw // hd, t // BLOCK
    grp = nq // nkv
    scale = hd ** -0.5

    def body(q_ref, kvp_ref, kvc_ref, bp_ref, bc_ref, s_ref, qg_ref, kg_ref, o_ref):
        prev_ok = pl.program_id(0) > 0
        out = _attn_math(*_attn_inputs(nkv, grp, hd, kvw, q_ref, kvp_ref, kvc_ref), bp_ref[...], bc_ref[...],
                         s_ref[...], qg_ref[...], kg_ref[...], prev_ok=prev_ok, scale=scale)
        for h in range(nkv):
            for g in range(grp):
                o_ref[:, pl.ds((grp * h + g) * hd, hd)] = out[h, g * BLOCK:(g + 1) * BLOCK].astype(BF16)

    return _pallas(
        body, [p, p, p, bias_p, bias_c, sinks, qg, kg], dep=dep, grid=(nblk,),
        in_specs=_attn_specs(p, aw, kvw, nq, hd, nblk, False),
        out_specs=pl.BlockSpec((BLOCK, aw), lambda n: (n, 0)),
        out_shape=jax.ShapeDtypeStruct((t, aw), BF16),
        sem=("parallel",), name="attn_fwd")


def attn_bwd(p, bias_p, bias_c, sinks, qg, kg, do, *, aw, kvw, dep=None):
    t, hd = p.shape[0], qg.shape[-1]
    nq, nkv, nblk = aw // hd, kvw // hd, t // BLOCK
    grp = nq // nkv
    scale = hd ** -0.5

    def body(q_ref, kvp_ref, kvc_ref, bp_ref, bc_ref, s_ref, qg_ref, kg_ref, do_ref,
             dqkv_ref, dbp_ref, dbc_ref, ds_ref, dqg_ref, dkg_ref, carry):
        i = pl.program_id(0)
        prev_ok = (nblk - 1 - i) > 0

        @pl.when(i == 0)
        def _():
            carry[...] = jnp.zeros_like(carry)
            dbp_ref[...] = jnp.zeros_like(dbp_ref)
            dbc_ref[...] = jnp.zeros_like(dbc_ref)
            ds_ref[...] = jnp.zeros_like(ds_ref)
            dqg_ref[...] = jnp.zeros_like(dqg_ref)
            dkg_ref[...] = jnp.zeros_like(dkg_ref)

        fn = functools.partial(_attn_math, prev_ok=prev_ok, scale=scale)
        _, vjp = jax.vjp(fn, *_attn_inputs(nkv, grp, hd, kvw, q_ref, kvp_ref, kvc_ref), bp_ref[...], bc_ref[...],
                         s_ref[...], qg_ref[...], kg_ref[...])
        dq, dkp, dkc, dvp, dvc, dbp, dbc, dsk, dqg, dkg = vjp(_head_major(do_ref, nkv, grp, hd))
        for h in range(nkv):
            for g in range(grp):
                dqkv_ref[:, pl.ds((grp * h + g) * hd, hd)] = dq[h, g * BLOCK:(g + 1) * BLOCK].astype(BF16)
            k_cols, v_cols = pl.ds(h * hd, hd), pl.ds(kvw + h * hd, hd)
            dqkv_ref[:, pl.ds(aw + h * hd, hd)] = (dkc[h] + carry[:, k_cols]).astype(BF16)
            dqkv_ref[:, pl.ds(aw + kvw + h * hd, hd)] = (dvc[h] + carry[:, v_cols]).astype(BF16)
            carry[:, k_cols] = dkp[h]
            carry[:, v_cols] = dvp[h]
        dbp_ref[...] += dbp
        dbc_ref[...] += dbc
        ds_ref[...] += dsk
        dqg_ref[...] += dqg
        dkg_ref[...] += dkg

    return _pallas(
        body, [p, p, p, bias_p, bias_c, sinks, qg, kg, do], dep=dep, grid=(nblk,),
        in_specs=_attn_specs(p, aw, kvw, nq, hd, nblk, True)
        + [pl.BlockSpec((BLOCK, aw), lambda n: (nblk - 1 - n, 0))],
        out_specs=[
            pl.BlockSpec((BLOCK, aw + 2 * kvw), lambda n: (nblk - 1 - n, 0)),
            _full_spec((nq, BLOCK, BLOCK)), _full_spec((nq, BLOCK, BLOCK)), _full_spec((nq, 1, 1)),
            _full_spec((1, hd)), _full_spec((1, hd)),
        ],
        out_shape=[
            jax.ShapeDtypeStruct((t, aw + 2 * kvw), BF16),
            jax.ShapeDtypeStruct((nq, BLOCK, BLOCK), F32),
            jax.ShapeDtypeStruct((nq, BLOCK, BLOCK), F32),
            jax.ShapeDtypeStruct((nq, 1, 1), F32),
            jax.ShapeDtypeStruct((1, hd), F32),
            jax.ShapeDtypeStruct((1, hd), F32),
        ],
        scratch=[pltpu.VMEM((BLOCK, 2 * kvw), F32)],
        sem=("arbitrary",), name="attn_bwd")


CONV_TILE = 256


def _conv_halo_specs(tb, ch, nblk):
    per = tb // CONV_HALO
    last = nblk * per - 1
    cur = pl.BlockSpec((tb, ch), lambda n: (n, 0))
    prev = pl.BlockSpec((CONV_HALO, ch), lambda n: (jnp.maximum(n * per - 1, 0), 0))
    nxt = pl.BlockSpec((CONV_HALO, ch), lambda n: (jnp.minimum((n + 1) * per, last), 0))
    return cur, prev, nxt


def _ln_silu(co, ln_g, ln_b):
    mu = jnp.mean(co, axis=-1, keepdims=True)
    cen = co - mu
    rstd = lax.rsqrt(jnp.mean(cen * cen, axis=-1, keepdims=True) + EPS)
    xhat = cen * rstd
    z = xhat * ln_g + ln_b
    return xhat, rstd, z


def _shifted_copies(src, shifted):
    rows = src.shape[0] - SUBLANES
    for r in range(1, SUBLANES):
        shifted[r, pl.ds(0, rows), :] = src[pl.ds(r, rows), :]


def _rows_from(src, shifted, start, n):
    r = start % SUBLANES
    if r == 0:
        return src[pl.ds(start, n), :]
    return shifted[r, pl.ds(start - r, n), :]


def conv_fwd(ca, cb, conv_w, conv_b, ln_g, ln_b, dep=None):
    t, ch = ca.shape
    tb = _tile(t, CONV_TILE, CONV_HALO)
    nblk = t // tb
    cur, prev, _ = _conv_halo_specs(tb, ch, nblk)
    lead = CONV_HALO - (CONV_WIDTH - 1)

    def body(ca_ref, cb_ref, cap_ref, cbp_ref, w_ref, b_ref, g_ref, bb_ref, s_ref, co_ref, ubuf, ushift):
        n = pl.program_id(0)
        halo = cap_ref[...] * _sigmoid(cbp_ref[...])
        ubuf[pl.ds(0, CONV_HALO), :] = jnp.where(n > 0, halo, 0.0)
        ubuf[pl.ds(CONV_HALO, tb), :] = ca_ref[...] * _sigmoid(cb_ref[...])
        _shifted_copies(ubuf, ushift)
        acc = jnp.broadcast_to(b_ref[...], (tb, ch))
        for k in range(CONV_WIDTH):
            acc = acc + w_ref[pl.ds(k, 1), :] * _rows_from(ubuf, ushift, lead + k, tb)
        co_ref[...] = acc
        _, _, z = _ln_silu(acc, g_ref[...], bb_ref[...])
        s_ref[...] = (z * _sigmoid(z)).astype(BF16)

    vec = _full_spec((1, ch))
    return _pallas(
        body, [ca, cb, ca, cb, conv_w, conv_b, ln_g, ln_b], dep=dep, grid=(nblk,),
        in_specs=[cur, cur, prev, prev, _full_spec(conv_w.shape), vec, vec, vec],
        out_specs=[cur, cur],
        out_shape=[jax.ShapeDtypeStruct((t, ch), BF16), jax.ShapeDtypeStruct((t, ch), F32)],
        scratch=[pltpu.VMEM((CONV_HALO + tb, ch), F32), pltpu.VMEM((SUBLANES, CONV_HALO + tb, ch), F32)],
        sem=("parallel",), name="conv_fwd")


def conv_bwd(ca, cb, co, ds, conv_w, ln_g, ln_b, dep=None):
    t, ch = ca.shape
    tb = _tile(t, CONV_TILE, CONV_HALO)
    nblk = t // tb
    cur, prev, nxt = _conv_halo_specs(tb, ch, nblk)
    lead = CONV_HALO - (CONV_WIDTH - 1)
    ext = tb + CONV_HALO

    def body(ca_ref, cb_ref, cap_ref, cbp_ref, co_ref, con_ref, ds_ref, dsn_ref, w_ref, g_ref, bb_ref,
             dca_ref, dcb_ref, dw_ref, dvec_ref, ubuf, dbuf, ushift, dshift):
        n = pl.program_id(0)
        is_last = n == nblk - 1
        sig_b = _sigmoid(cb_ref[...])
        cav = ca_ref[...].astype(F32)
        ubuf[pl.ds(0, CONV_HALO), :] = jnp.where(n > 0, cap_ref[...] * _sigmoid(cbp_ref[...]), 0.0)
        ubuf[pl.ds(CONV_HALO, tb), :] = cav * sig_b
        _shifted_copies(ubuf, ushift)
        co = jnp.concatenate([co_ref[...], con_ref[...]], axis=0)
        xhat, rstd, z = _ln_silu(co, g_ref[...], bb_ref[...])
        dsv = jnp.concatenate([ds_ref[...].astype(F32), jnp.where(is_last, 0.0, dsn_ref[...].astype(F32))], axis=0)
        sg = _sigmoid(z)
        dz = dsv * (sg * (1.0 + z * (1.0 - sg)))
        dxh = dz * g_ref[...]
        dco = rstd * (dxh - jnp.mean(dxh, axis=-1, keepdims=True)
                      - xhat * jnp.mean(dxh * xhat, axis=-1, keepdims=True))
        dbuf[...] = dco
        _shifted_copies(dbuf, dshift)

        @pl.when(n == 0)
        def _():
            dw_ref[...] = jnp.zeros_like(dw_ref)
            dvec_ref[...] = jnp.zeros_like(dvec_ref)

        dco_cur = dco[:tb]
        dvec_ref[pl.ds(0, 1), :] += jnp.sum(dco_cur, axis=0, keepdims=True)
        dvec_ref[pl.ds(1, 1), :] += jnp.sum(dz[:tb] * xhat[:tb], axis=0, keepdims=True)
        dvec_ref[pl.ds(2, 1), :] += jnp.sum(dz[:tb], axis=0, keepdims=True)
        du = jnp.zeros((tb, ch), F32)
        for k in range(CONV_WIDTH):
            du = du + w_ref[pl.ds(k, 1), :] * _rows_from(dbuf, dshift, CONV_WIDTH - 1 - k, tb)
            dw_ref[pl.ds(k, 1), :] += jnp.sum(dco_cur * _rows_from(ubuf, ushift, lead + k, tb), axis=0,
                                              keepdims=True)
        dca_ref[...] = (du * sig_b).astype(BF16)
        dcb_ref[...] = (du * cav * sig_b * (1.0 - sig_b)).astype(BF16)

    vec = _full_spec((1, ch))
    return _pallas(
        body, [ca, cb, ca, cb, co, co, ds, ds, conv_w, ln_g, ln_b], dep=dep, grid=(nblk,),
        in_specs=[cur, cur, prev, prev, cur, nxt, cur, nxt, _full_spec(conv_w.shape), vec, vec],
        out_specs=[cur, cur, _full_spec(conv_w.shape), _full_spec((SUBLANES, ch))],
        out_shape=[jax.ShapeDtypeStruct((t, ch), BF16), jax.ShapeDtypeStruct((t, ch), BF16),
                   jax.ShapeDtypeStruct(conv_w.shape, F32), jax.ShapeDtypeStruct((SUBLANES, ch), F32)],
        scratch=[pltpu.VMEM((CONV_HALO + tb, ch), F32), pltpu.VMEM((ext, ch), F32),
                 pltpu.VMEM((SUBLANES, CONV_HALO + tb, ch), F32), pltpu.VMEM((SUBLANES, ext, ch), F32)],
        sem=("arbitrary",), name="conv_bwd")


def ada_fwd(c_t, w_ada, dep=None):
    d, nc = w_ada.shape
    nex = c_t.shape[1]
    tn = _tile(nc, TILE_N)

    def body(ct_ref, w_ref, o_ref):
        w = w_ref[...]
        ct = ct_ref[...]
        cact = ct * _sigmoid(ct)
        rows = [jnp.sum(w * cact[:, b:b + 1], axis=0, keepdims=True) for b in range(nex)]
        o_ref[...] = jnp.concatenate(rows, axis=0)

    return _pallas(
        body, [c_t, w_ada], dep=dep, grid=(nc // tn,),
        in_specs=[_full_spec(c_t.shape), pl.BlockSpec((d, tn), lambda j: (0, j))],
        out_specs=pl.BlockSpec((nex, tn), lambda j: (0, j)),
        out_shape=jax.ShapeDtypeStruct((nex, nc), F32),
        sem=("parallel",), name="ada_fwd")


def _adamw_math(w, g, m, v):
    m = ADAM_B1 * m + (1.0 - ADAM_B1) * g
    v = ADAM_B2 * v + (1.0 - ADAM_B2) * (g * g)
    m_hat = m / (1.0 - ADAM_B1 ** ADAM_STEP)
    v_hat = v / (1.0 - ADAM_B2 ** ADAM_STEP)
    delta = -ADAM_LR * (m_hat / (jnp.sqrt(v_hat) + ADAM_EPS) + ADAM_WD * w)
    return delta, m, v


def adamw(w, g, m, v, name, copy_grad=False, dep=None):
    r, n = w.shape
    tr, tn = _ew_tiles(r, n, elems=EW_ELEMS_MANY)
    n_out = 4 if copy_grad else 3

    def body(w_ref, g_ref, m_ref, v_ref, *outs):
        g = g_ref[...]
        if copy_grad:
            outs[0][...] = g
        outs[-3][...], outs[-2][...], outs[-1][...] = _adamw_math(w_ref[...], g, m_ref[...], v_ref[...])

    blk = pl.BlockSpec((tr, tn), lambda i, j: (i, j))
    return _pallas(
        body, [w, g, m, v], dep=dep, grid=(r // tr, n // tn),
        in_specs=[blk] * 4, out_specs=[blk] * n_out,
        out_shape=[jax.ShapeDtypeStruct((r, n), F32)] * n_out,
        sem=("parallel", "parallel"), name=name)


def ada_grad_adamw(c_t, dmod_cols, w, m, v, dep=None):
    d, nc = w.shape
    nex = c_t.shape[1]
    tr, tn = _ew_tiles(d, nc, elems=EW_ELEMS_MANY)

    def body(ct_ref, dm_ref, w_ref, m_ref, v_ref, g_ref, d_ref, nm_ref, nv_ref):
        ct = ct_ref[...]
        cact = ct * _sigmoid(ct)
        dm = dm_ref[...]
        g = cact[:, 0:1] * dm[0:1, :]
        for b in range(1, nex):
            g = g + cact[:, b:b + 1] * dm[b:b + 1, :]
        g_ref[...] = g
        d_ref[...], nm_ref[...], nv_ref[...] = _adamw_math(w_ref[...], g, m_ref[...], v_ref[...])

    blk = pl.BlockSpec((tr, tn), lambda i, j: (i, j))
    return _pallas(
        body, [c_t, dmod_cols, w, m, v], dep=dep, grid=(d // tr, nc // tn),
        in_specs=[pl.BlockSpec((tr, nex), lambda i, j: (i, 0)), pl.BlockSpec((nex, tn), lambda i, j: (0, j)),
                  blk, blk, blk],
        out_specs=[blk] * 4,
        out_shape=[jax.ShapeDtypeStruct((d, nc), F32)] * 4,
        sem=("parallel", "parallel"), name="ada_grad_adamw")


def _row_pack(parts):
    cols, offs, off = [], [], 0
    for p in parts:
        n = p.shape[1]
        width = -(-n // LANES) * LANES
        cols.append(jnp.pad(p, ((0, 0), (0, width - n))) if width != n else p)
        offs.append(off)
        off += width
    return jnp.concatenate(cols, axis=1), offs


def small_sum_adamw(gathered, offs, ws, ms, vs, extra_widths, dep=None):
    ndev = gathered.shape[0]
    npar = len(ws)

    def body(ga_ref, *refs):
        w_refs, m_refs, v_refs = refs[:npar], refs[npar:2 * npar], refs[2 * npar:3 * npar]
        outs = refs[3 * npar:]
        tot = ga_ref[0]
        for s in range(1, ndev):
            tot = tot + ga_ref[s]
        for i in range(npar):
            n = ws[i].shape[1]
            g = tot[:, offs[i]:offs[i] + n]
            outs[4 * i][...] = g
            outs[4 * i + 1][...], outs[4 * i + 2][...], outs[4 * i + 3][...] = _adamw_math(
                w_refs[i][...], g, m_refs[i][...], v_refs[i][...])
        for e, n in enumerate(extra_widths):
            off = offs[npar + e]
            outs[4 * npar + e][...] = tot[:, off:off + n]

    shapes = [jax.ShapeDtypeStruct(w.shape, F32) for w in ws for _ in range(4)]
    shapes += [jax.ShapeDtypeStruct((1, n), F32) for n in extra_widths]
    return _pallas(
        body, [gathered, *ws, *ms, *vs], dep=dep, in_specs=[_VMEM] * (1 + 3 * npar), out_specs=[_VMEM] * len(shapes),
        out_shape=shapes, name="small_sum_adamw")


def _position():
    return lax.axis_index("x"), lax.axis_index("y"), lax.axis_index("c")


def _other_chips(x, y):
    return [(1 - x, y), (x, 1 - y), (1 - x, 1 - y)]


def allgather_small(block, name, dep=None):
    def body(x_ref, out_ref, send_sems, recv_sems, local_sem):
        x, y, c = _position()
        me, sibling = (x, y, c), (x, y, 1 - c)
        chips = _other_chips(x, y)

        def slot(px, py, pc):
            return out_ref.at[4 * px + 2 * py + pc]

        def copy(k, block_of, to, src=None):
            return pltpu.make_async_remote_copy(
                src_ref=slot(*block_of) if src is None else src, dst_ref=slot(*block_of),
                send_sem=send_sems.at[k], recv_sem=recv_sems.at[k], device_id=to, device_id_type=MESH)

        mine = pltpu.make_async_copy(x_ref, slot(*me), local_sem)
        mine.start()
        first = [copy(0, me, sibling, src=x_ref)]
        first += [copy(1 + j, me, (*chip, c), src=x_ref) for j, chip in enumerate(chips)]
        for cp in first:
            cp.start()
        passed = [copy(4 + j, (*chip, c), sibling) for j, chip in enumerate(chips)]
        for j, chip in enumerate(chips):
            copy(1 + j, (*chip, c), me).wait_recv()
            passed[j].start()
        copy(0, sibling, me).wait_recv()
        for j, chip in enumerate(chips):
            copy(4 + j, (*chip, 1 - c), me).wait_recv()
        for cp in first + passed:
            cp.wait_send()
        mine.wait()

    return _pallas(
        body, [block], dep=dep,
        out_shape=jax.ShapeDtypeStruct((N_DEV, *block.shape), block.dtype),
        in_specs=[_VMEM], out_specs=_VMEM,
        scratch=[pltpu.SemaphoreType.DMA((7,)), pltpu.SemaphoreType.DMA((7,)), pltpu.SemaphoreType.DMA],
        name=name)


class Started(NamedTuple):
    send_sems: Any
    recv_sems: Any
    bufs: list


def exchange_start_many(name, buf_sets, plans, dep=None):
    sizes = [len(bufs) for bufs in buf_sets]
    first = [sum(sizes[:i]) for i in range(len(sizes))]
    flat = [b for bufs in buf_sets for b in bufs]
    nb, npl = len(flat), len(plans)

    def body(*refs):
        for i, (s, _, plan) in enumerate(plans):
            for cp in plan(refs[first[s]:first[s] + sizes[s]], refs[nb + 2 * i], refs[nb + 2 * i + 1]):
                cp.start()

    sems = [pltpu.SemaphoreType.DMA((n,)) for _, n, _ in plans for _ in range(2)]
    outs = _pallas(
        body, [pltpu.with_memory_space_constraint(b, pltpu.HBM) for b in flat], dep=dep, name=name,
        out_shape=(*sems, *[pltpu.HBM(b.shape, b.dtype) for b in flat]),
        in_specs=[_HBM] * nb,
        out_specs=(*[_SEM] * (2 * npl), *[_HBM] * nb),
        input_output_aliases={i: 2 * npl + i for i in range(nb)},
        compiler_params=pltpu.CompilerParams(has_side_effects=_EFFECT))
    new_bufs = outs[2 * npl:]
    return [Started(outs[2 * i], outs[2 * i + 1], list(new_bufs[first[s]:first[s] + sizes[s]]))
            for i, (s, _, _) in enumerate(plans)]


def exchange_start(name, bufs, n_copies, plan, dep=None):
    return exchange_start_many(name, [bufs], [(0, n_copies, plan)], dep=dep)[0]


def exchange_wait(name, started, plan, bufs=None, dep=None):
    if bufs is not None:
        started = started._replace(bufs=list(bufs))
    nb = len(started.bufs)

    def body(*refs):
        for cp in plan(refs[:nb], refs[nb], refs[nb + 1]):
            cp.wait_send()
            cp.wait_recv()

    outs = _pallas(
        body, [*started.bufs, started.send_sems, started.recv_sems], dep=dep, name=name,
        out_shape=tuple(pltpu.HBM(b.shape, b.dtype) for b in started.bufs),
        in_specs=[_HBM] * nb + [_SEM, _SEM],
        out_specs=tuple([_HBM] * nb),
        input_output_aliases={i: i for i in range(nb)},
        compiler_params=pltpu.CompilerParams(has_side_effects=_EFFECT))
    return list(outs)


def _remote(src, dst, send_sems, recv_sems, i, to):
    return pltpu.make_async_remote_copy(src_ref=src, dst_ref=dst, send_sem=send_sems.at[i], recv_sem=recv_sems.at[i],
                                        device_id=to, device_id_type=MESH)


def _half_rows(buf_rows, chip_idx, pc):
    half = buf_rows // (2 * N_CHIPS)
    return pl.ds((2 * chip_idx + pc) * half, half)


ALL_PEERS = (0, 1, 2)


def plan_gather_ici(refs, send_sems, recv_sems, peers=ALL_PEERS):
    x, y, c = _position()
    chips = _other_chips(x, y)
    copies = []
    for k, ref in enumerate(refs):
        rows = ref.at[_half_rows(ref.shape[0], 2 * x + y, c), :]
        for i, j in enumerate(peers):
            copies.append(_remote(rows, rows, send_sems, recv_sems, len(peers) * k + i, (*chips[j], c)))
    return copies


def plan_gather_relay(refs, send_sems, recv_sems):
    x, y, c = _position()
    copies = []
    for k, ref in enumerate(refs):
        quarter = ref.shape[0] // (4 * N_CHIPS)
        for i, (src_chip, to) in enumerate((((1 - x, y), (x, 1 - y, c)), ((x, 1 - y), (1 - x, y, c)))):
            start = (2 * (2 * src_chip[0] + src_chip[1]) + c) * 2 * quarter + i * quarter
            rows = ref.at[pl.ds(start, quarter), :]
            copies.append(_remote(rows, rows, send_sems, recv_sems, 2 * k + i, to))
    return copies


def plan_gather_d2d(refs, send_sems, recv_sems, peers=ALL_PEERS):
    x, y, c = _position()
    chips = _other_chips(x, y)
    copies = []
    for k, ref in enumerate(refs):
        for i, j in enumerate(peers):
            px, py = chips[j]
            rows = ref.at[_half_rows(ref.shape[0], 2 * px + py, c), :]
            copies.append(_remote(rows, rows, send_sems, recv_sems, len(peers) * k + i, (x, y, 1 - c)))
    return copies


def plan_pair_exchange(refs, send_sems, recv_sems):
    x, y, c = _position()
    nw = len(refs) // 2
    copies = []
    for k in range(nw):
        for chip in range(N_CHIPS):
            copies.append(_remote(refs[k].at[chip, 1 - c], refs[nw + k].at[chip], send_sems, recv_sems,
                                  N_CHIPS * k + chip, (x, y, 1 - c)))
    return copies


def plan_chip_exchange(refs, send_sems, recv_sems):
    x, y, c = _position()
    nw = len(refs) // 2
    copies = []
    for k in range(nw):
        for j, (px, py) in enumerate(_other_chips(x, y)):
            copies.append(_remote(refs[k].at[2 * px + py], refs[nw + k].at[2 * x + y], send_sems, recv_sems,
                                  3 * k + j, (px, py, c)))
    return copies


def plan_pair_share(refs, send_sems, recv_sems):
    x, y, c = _position()
    return [_remote(ref.at[c], ref.at[c], send_sems, recv_sems, k, (x, y, 1 - c)) for k, ref in enumerate(refs)]


def cast_into_slot(src, slot, n_slots, name, dep=None):
    r, n = src.shape
    tr, tn = _ew_tiles(r, n, BF16_SUBLANES)

    def body(slot_ref, s_ref, o_ref):
        o_ref[...] = s_ref[...].astype(BF16)

    return _pallas(
        body, [slot, src], dep=dep, n_prefetch=1, grid=(r // tr, n // tn),
        in_specs=[pl.BlockSpec((tr, tn), lambda i, j, sl: (i, j))],
        out_specs=pl.BlockSpec((None, tr, tn), lambda i, j, sl: (sl[0], i, j)),
        out_shape=jax.ShapeDtypeStruct((n_slots, r, n), BF16),
        sem=("parallel", "parallel"), name=name)


def pair_sum(g, r, core, name, dep=None):
    nchip, _, h, n = g.shape
    th, tn = _ew_tiles(h, n, BF16_SUBLANES)

    def body(core_ref, g_ref, r_ref, o_ref):
        o_ref[...] = (g_ref[...].astype(F32) + r_ref[...].astype(F32)).astype(BF16)

    return _pallas(
        body, [core, g, r], dep=dep, n_prefetch=1, grid=(nchip, h // th, n // tn),
        in_specs=[pl.BlockSpec((None, None, th, tn), lambda a, i, j, cr: (a, cr[0], i, j)),
                  pl.BlockSpec((None, th, tn), lambda a, i, j, cr: (a, i, j))],
        out_specs=pl.BlockSpec((None, th, tn), lambda a, i, j, cr: (a, i, j)),
        out_shape=jax.ShapeDtypeStruct((nchip, h, n), BF16),
        sem=("parallel", "parallel", "parallel"), name=name)


def chip_sum(own, got, where, name, dep=None):
    nchip, h, n = got.shape
    th, tn = _ew_tiles(h, n, BF16_SUBLANES, elems=EW_ELEMS_MANY)

    def body(where_ref, own_ref, *rest):
        got_refs, o_ref = rest[:nchip], rest[nchip]
        chip = where_ref[0]
        acc = None
        for s in range(nchip):
            term = jnp.where(chip == s, own_ref[...], got_refs[s][...]).astype(F32)
            acc = term if acc is None else acc + term
        o_ref[...] = acc

    def got_spec(s):
        return pl.BlockSpec((None, th, tn), lambda i, j, wr: (jnp.where(wr[0] == s, (s + 1) % nchip, s), i, j))

    return _pallas(
        body, [where, own, *[got] * nchip], dep=dep, n_prefetch=1, grid=(h // th, n // tn),
        in_specs=[pl.BlockSpec((None, th, tn), lambda i, j, wr: (wr[0], i, j))]
        + [got_spec(s) for s in range(nchip)],
        out_specs=pl.BlockSpec((None, th, tn), lambda i, j, wr: (wr[1], i, j)),
        out_shape=jax.ShapeDtypeStruct((2, h, n), F32),
        sem=("parallel", "parallel"), name=name)


def kernel(x, c, w_ada, b_ada, norm_mix_g, w_in, q_norm_g, k_norm_g, attn_sinks, rel_bias, w_attn_out, conv_w, conv_b, conv_ln_g, conv_ln_b, w_conv_out, w_mix_out, norm_ffn_g, w_ffn_in, w_ffn_out, loss_target, m_w_ada, m_b_ada, m_norm_mix_g, m_w_in, m_q_norm_g, m_k_norm_g, m_attn_sinks, m_rel_bias, m_w_attn_out, m_conv_w, m_conv_b, m_conv_ln_g, m_conv_ln_b, m_w_conv_out, m_w_mix_out, m_norm_ffn_g, m_w_ffn_in, m_w_ffn_out, v_w_ada, v_b_ada, v_norm_mix_g, v_w_in, v_q_norm_g, v_k_norm_g, v_attn_sinks, v_rel_bias, v_w_attn_out, v_conv_w, v_conv_b, v_conv_ln_g, v_conv_ln_b, v_w_conv_out, v_w_mix_out, v_norm_ffn_g, v_w_ffn_in, v_w_ffn_out):
    run = InOrder()
    xi, yi, ci = _position()
    chip = 2 * xi + yi
    me = 2 * chip + ci
    chip_arr = chip.astype(jnp.int32).reshape(1)
    core_arr = ci.astype(jnp.int32).reshape(1)
    where_arr = jnp.stack([chip, ci]).astype(jnp.int32)

    xe, tgt = x[0], loss_target[0]
    t, d = xe.shape
    hd = q_norm_g.shape[-1]
    nq = attn_sinks.shape[-1]
    aw = nq * hd
    ch = conv_b.shape[-1]
    in_width = N_CHIPS * w_in.shape[-1]
    kvw = (in_width - aw - 2 * ch - 2 * d) // 2
    nkv = kvw // hd
    dff = N_CHIPS * w_ffn_out.shape[1]
    off_k, off_v, off_ca = aw, aw + kvw, aw + 2 * kvw
    off_cb, off_ga, off_gc = off_ca + ch, off_ca + 2 * ch, off_ca + 2 * ch + d
    nc_ada = w_ada.shape[-1]
    ch_loc = conv_w.shape[-1]
    nj_ffn = w_ffn_in.shape[-1]
    perm_ffn = ffn_perm(N_CHIPS)

    big = {"w_in": w_in[0], "w_attn_out": w_attn_out[0], "w_conv_out": w_conv_out[0], "w_mix_out": w_mix_out[0],
           "w_ffn_in": w_ffn_in[0], "w_ffn_out": w_ffn_out[0]}
    moments = {"w_in": (m_w_in, v_w_in), "w_attn_out": (m_w_attn_out, v_w_attn_out),
               "w_conv_out": (m_w_conv_out, v_w_conv_out), "w_mix_out": (m_w_mix_out, v_w_mix_out),
               "w_ffn_in": (m_w_ffn_in, v_w_ffn_in), "w_ffn_out": (m_w_ffn_out, v_w_ffn_out)}
    gather_groups = {"in": ["w_in"], "branch_out": ["w_attn_out", "w_conv_out"], "mix_out": ["w_mix_out"],
                     "ffn_in": ["w_ffn_in"], "ffn_out": ["w_ffn_out"]}
    grads, deltas, new_m, new_v = {}, {}, {}, {}

    def gather_cast(gname):
        bufs = []
        for n in gather_groups[gname]:
            r, ncol = big[n].shape
            bufs.append(run(cast_into_slot, big[n], chip_arr, N_CHIPS, "cast_" + n).reshape(N_CHIPS * r, ncol))
        return bufs

    def gather_pass_on(gname, ici):
        landed = run(exchange_wait, "gather_ici_wait_" + gname, ici, plan_gather_ici)
        return run(exchange_start, "gather_d2d_start_" + gname, landed, 3 * len(landed), plan_gather_d2d)

    def gathered(gname, d2d):
        outs = run(exchange_wait, "gather_d2d_wait_" + gname, d2d, plan_gather_d2d)
        return [o.reshape(N_CHIPS, *big[n].shape) for o, n in zip(outs, gather_groups[gname])]

    def rs_pair_start(gname, names, partials):
        blocks = [g.reshape(N_CHIPS, 2, big[n].shape[0] // 2, big[n].shape[1]) for n, g in zip(names, partials)]
        land = [lax.empty((N_CHIPS,) + b.shape[2:], BF16) for b in blocks]
        return run(exchange_start, "pair_exchange_start_" + gname, blocks + land, N_CHIPS * len(blocks),
                   plan_pair_exchange)

    def rs_chip_start(gname, names, pair):
        nw = len(names)
        outs = run(exchange_wait, "pair_exchange_wait_" + gname, pair, plan_pair_exchange)
        sums = [run(pair_sum, g, r, core_arr, "pair_sum_" + n) for n, g, r in zip(names, outs[:nw], outs[nw:])]
        land = [lax.empty(s.shape, BF16) for s in sums]
        return run(exchange_start, "chip_exchange_start_" + gname, sums + land, 3 * nw, plan_chip_exchange)

    def rs_share_start(gname, names, chipx):
        nw = len(names)
        outs = run(exchange_wait, "chip_exchange_wait_" + gname, chipx, plan_chip_exchange)
        halves = [run(chip_sum, s, r, where_arr, "chip_sum_" + n) for n, s, r in zip(names, outs[:nw], outs[nw:])]
        return run(exchange_start, "pair_share_start_" + gname, halves, nw, plan_pair_share)

    def rs_finish(gname, names, share):
        fulls = run(exchange_wait, "pair_share_wait_" + gname, share, plan_pair_share)
        for n, g2 in zip(names, fulls):
            g, dl, nm, nv = run(adamw, big[n], g2.reshape(big[n].shape), moments[n][0][0], moments[n][1][0],
                                "adamw_" + n, copy_grad=True)
            grads[n], deltas[n], new_m[n], new_v[n] = g[None], dl[None], nm[None], nv[None]

    near, far = (0, 1), (2,)
    plan_ici_near = functools.partial(plan_gather_ici, peers=near)
    plan_ici_far, n_far = plan_gather_relay, 2
    plan_d2d_near = functools.partial(plan_gather_d2d, peers=near)
    plan_d2d_far = functools.partial(plan_gather_d2d, peers=far)
    bufs_in = gather_cast("in")
    row1, offs1 = _row_pack([c, conv_w[0].reshape(1, CONV_WIDTH * ch_loc)])
    got1 = run(allgather_small, row1, "allgather_cond")
    ici_near = run(exchange_start, "gather_ici_start_in_near", bufs_in, len(near), plan_ici_near)
    c_all = got1[:, 0, :d]
    conv_w_full = got1[0::2, 0, offs1[1]:offs1[1] + CONV_WIDTH * ch_loc].reshape(N_CHIPS, CONV_WIDTH, ch_loc)
    conv_w_full = jnp.transpose(conv_w_full, (1, 0, 2)).reshape(CONV_WIDTH, ch)
    conv_w_pad = jnp.pad(conv_w_full, ((0, 1), (0, 0)))
    c_t = jnp.transpose(c_all)
    mod_cols = run(ada_fwd, c_t, w_ada[0])
    rest_bufs = {gname: gather_cast(gname) for gname in gather_groups if gname != "in"}
    bucket = _t5_bucket_table()
    bucket_p, bucket_c = jnp.asarray(bucket[:, :BLOCK]), jnp.asarray(bucket[:, BLOCK:])
    bias_p, bias_c = run(bias_table, rel_bias, bucket_p, bucket_c)
    got2 = run(allgather_small, mod_cols, "allgather_mod")
    mod_all = got2.reshape(N_CHIPS, 2, N_DEV, nc_ada)[:, 0]
    mod = lax.dynamic_slice_in_dim(mod_all, me, 1, axis=1).reshape(1, N_CHIPS * nc_ada) + b_ada
    mod = jnp.pad(mod.reshape(N_MOD, d), ((0, SUBLANES - N_MOD), (0, 0)))

    landed = run(exchange_wait, "gather_ici_wait_in_near", ici_near, plan_ici_near)
    ici_far, d2d_near = run(exchange_start_many, "gather_start_in_far", [landed],
                            [(0, n_far, plan_ici_far), (0, len(near), plan_d2d_near)])
    h = run(pre_mix_fwd, xe, mod, norm_mix_g)
    ici = {}
    ici["branch_out"], ici_near_ffn, ici["mix_out"] = run(
        exchange_start_many, "gather_ici_start_mid", [rest_bufs["branch_out"], rest_bufs["ffn_in"], rest_bufs["mix_out"]],
        [(0, 3 * len(rest_bufs["branch_out"]), plan_gather_ici), (1, len(near), plan_ici_near), (2, 3, plan_gather_ici)])
    landed = run(exchange_wait, "gather_d2d_wait_in_near", d2d_near, plan_d2d_near)
    landed = run(exchange_wait, "gather_ici_wait_in_far", ici_far, plan_ici_far, bufs=landed)
    d2d_far = run(exchange_start, "gather_d2d_start_in_far", landed, len(far), plan_d2d_far)
    landed = run(exchange_wait, "gather_d2d_wait_in_far", d2d_far, plan_d2d_far)
    wg_in = landed[0].reshape(N_CHIPS, *big["w_in"].shape)
    p = run(mm_nn, h, wg_in, tn=wg_in.shape[2], tk=d, out_dtype=BF16, name="mm_in")
    d2d_branch = gather_pass_on("branch_out", ici["branch_out"])

    sinks3 = attn_sinks.reshape(nq, 1, 1)
    attn_o = run(attn_fwd, p, bias_p, bias_c, sinks3, q_norm_g, k_norm_g, aw=aw, kvw=kvw)
    ca, cb = p[:, off_ca:off_cb], p[:, off_cb:off_ga]
    s_conv, co_conv = run(conv_fwd, ca, cb, conv_w_pad, conv_b, conv_ln_g, conv_ln_b)
    wg_attn_out, wg_conv_out = gathered("branch_out", d2d_branch)
    landed = run(exchange_wait, "gather_ici_wait_ffn_in_near", ici_near_ffn, plan_ici_near)
    ici_far_ffn, ici["ffn_out"] = run(
        exchange_start_many, "gather_start_ffn_in_far", [landed, rest_bufs["ffn_out"]],
        [(0, n_far, plan_ici_far), (1, 3, plan_gather_ici)])
    y_attn, y_conv, merged = run(branch_out_merge, attn_o, s_conv, wg_attn_out, wg_conv_out, p, off_ga, off_gc)
    landed_mix = run(exchange_wait, "gather_ici_wait_mix_out", ici["mix_out"], plan_gather_ici)
    d2d_mix, d2d_near_ffn = run(
        exchange_start_many, "gather_d2d_start_mix_ffn_in", [landed_mix, ici_far_ffn.bufs],
        [(0, 3 * len(landed_mix), plan_gather_d2d), (1, len(near), plan_d2d_near)])
    (wg_mix_out,) = gathered("mix_out", d2d_mix)
    wg_mix_out = wg_mix_out.reshape(1, d, d)
    o_m = run(mm_nn, merged, wg_mix_out, tn=_tile(d, TILE_N), tk=d, out_dtype=BF16, name="mm_mix_out")
    landed = run(exchange_wait, "gather_d2d_wait_ffn_in_near", d2d_near_ffn, plan_d2d_near)
    landed = run(exchange_wait, "gather_ici_wait_ffn_in_far", ici_far_ffn, plan_ici_far, bufs=landed)
    d2d_far_ffn = run(exchange_start, "gather_d2d_start_ffn_in_far", landed, len(far), plan_d2d_far)
    x1, h2 = run(pre_ffn_fwd, xe, o_m, mod, norm_ffn_g)
    landed = run(exchange_wait, "gather_d2d_wait_ffn_in_far", d2d_far_ffn, plan_d2d_far)
    wg_ffn_in = landed[0].reshape(N_CHIPS, *big["w_ffn_in"].shape)
    f = run(mm_nn, h2, wg_ffn_in, tn=_tile(nj_ffn, nj_ffn // 2), tk=d, out_dtype=BF16, name="mm_ffn_in", perm=perm_ffn)
    d2d_ffn_out = gather_pass_on("ffn_out", ici["ffn_out"])
    act = run(swiglu_fwd, f, nj_ffn)
    (wg_ffn_out,) = gathered("ffn_out", d2d_ffn_out)
    wg_ffn_out = wg_ffn_out.reshape(1, dff, d)
    o_f = run(mm_nn, act, wg_ffn_out, tn=_tile(d, TILE_N), tk=_tile(dff, dff // 2), out_dtype=BF16, name="mm_ffn_out")
    loss11, dy, dof, acc_l = run(loss_head, x1, o_f, tgt, mod)

    gw_ffn_out = run(mm_tn, act, dof, 1, tk=_tile(dff, TILE_N), tn=d, name="mm_ffn_out_dw")
    px_ffn_out = rs_pair_start("ffn_out", ["w_ffn_out"], [gw_ffn_out])
    dact = run(mm_nt, dof, wg_ffn_out, tko=_tile(dff, TILE_N), tn=d, out_dtype=BF16, name="mm_ffn_out_dx")
    cx_ffn_out = rs_chip_start("ffn_out", ["w_ffn_out"], px_ffn_out)
    df = run(swiglu_bwd, f, dact, nj_ffn)
    gw_ffn_in = run(mm_tn, h2, df, N_CHIPS, tk=d, tn=_tile(nj_ffn, nj_ffn // 2), name="mm_ffn_in_dw",
                    perm=perm_ffn)
    px_ffn_in = rs_pair_start("ffn_in", ["w_ffn_in"], [gw_ffn_in])
    dh2 = run(mm_nt, df, wg_ffn_in, tko=_tile(d, TILE_N), tn=nj_ffn, out_dtype=BF16, name="mm_ffn_in_dx", perm=perm_ffn)
    sh_ffn_out = rs_share_start("ffn_out", ["w_ffn_out"], cx_ffn_out)
    cx_ffn_in = rs_chip_start("ffn_in", ["w_ffn_in"], px_ffn_in)
    dx1, dom, acc_f = run(pre_ffn_bwd, x1, dh2, dy, o_m, mod, norm_ffn_g)
    gw_mix_out = run(mm_tn, merged, dom, 1, tk=d, tn=_tile(d, TILE_WIDE), name="mm_mix_out_dw")
    px_mix = rs_pair_start("mix_out", ["w_mix_out"], [gw_mix_out])
    dy_attn, dy_conv, dga, dgc = run(mix_out_dx_merge_bwd, dom, wg_mix_out.reshape(d, d), p, y_attn, y_conv,
                                     off_ga, off_gc)
    rs_finish("ffn_out", ["w_ffn_out"], sh_ffn_out)
    cx_mix = rs_chip_start("mix_out", ["w_mix_out"], px_mix)
    gw_attn_out = run(mm_tn, attn_o, dy_attn, N_CHIPS, tk=aw, tn=_tile(wg_attn_out.shape[2], TILE_N),
                      name="mm_attn_out_dw")
    gw_conv_out = run(mm_tn, s_conv, dy_conv, N_CHIPS, tk=ch, tn=_tile(wg_conv_out.shape[2], TILE_N),
                      name="mm_conv_out_dw")
    ac_names = ["w_attn_out", "w_conv_out"]
    px_ac = rs_pair_start("attn_conv_out", ac_names, [gw_attn_out, gw_conv_out])
    dattn_o = run(mm_nt, dy_attn, wg_attn_out, tko=_tile(aw, TILE_WIDE), tn=_tile(wg_attn_out.shape[2], TILE_N),
                  out_dtype=BF16, name="mm_attn_out_dx")
    ds_conv = run(mm_nt, dy_conv, wg_conv_out, tko=_tile(ch, TILE_WIDE), tn=_tile(wg_conv_out.shape[2], TILE_N),
                  out_dtype=BF16, name="mm_conv_out_dx")
    cx_ac = rs_chip_start("attn_conv_out", ac_names, px_ac)
    dca, dcb, dconv_w, dconv_vec = run(conv_bwd, ca, cb, co_conv, ds_conv, conv_w_pad, conv_ln_g, conv_ln_b)
    sh_ffn_in = rs_share_start("ffn_in", ["w_ffn_in"], cx_ffn_in)
    dqkv, dbp, dbc, dsinks, dqg, dkg = run(attn_bwd, p, bias_p, bias_c, sinks3, q_norm_g, k_norm_g, dattn_o,
                                           aw=aw, kvw=kvw)
    sh_mix = rs_share_start("mix_out", ["w_mix_out"], cx_mix)
    sh_ac = rs_share_start("attn_conv_out", ac_names, cx_ac)
    drel = run(bias_table_bwd, dbp, dbc, bucket_p, bucket_c).reshape(NUM_BUCKETS, nq)
    dp = jnp.concatenate([dqkv, dca, dcb, dga, dgc], axis=1)
    gw_in = run(mm_tn, h, dp, N_CHIPS, tk=d, tn=wg_in.shape[2], name="mm_in_dw")
    px_in = rs_pair_start("in", ["w_in"], [gw_in])
    dh = run(mm_nt, dp, wg_in, tko=_tile(d, TILE_WIDE), tn=wg_in.shape[2], out_dtype=BF16, name="mm_in_dx")
    grad_x, acc_m = run(pre_mix_bwd, xe, dh, dx1, mod, norm_mix_g)

    dmod = jnp.concatenate([acc_m[0:1], acc_m[1:2], acc_f[3:4], acc_f[0:1], acc_f[1:2], acc_l[0:1]], axis=1)
    small_names = ["b_ada", "norm_mix_g", "q_norm_g", "k_norm_g", "attn_sinks", "rel_bias", "conv_b", "conv_ln_g",
                   "conv_ln_b", "norm_ffn_g"]
    small_w = [b_ada, norm_mix_g, q_norm_g, k_norm_g, attn_sinks, rel_bias, conv_b, conv_ln_g, conv_ln_b, norm_ffn_g]
    small_m = [m_b_ada, m_norm_mix_g, m_q_norm_g, m_k_norm_g, m_attn_sinks, m_rel_bias, m_conv_b, m_conv_ln_g,
               m_conv_ln_b, m_norm_ffn_g]
    small_v = [v_b_ada, v_norm_mix_g, v_q_norm_g, v_k_norm_g, v_attn_sinks, v_rel_bias, v_conv_b, v_conv_ln_g,
               v_conv_ln_b, v_norm_ffn_g]
    small_g = [dmod, acc_m[2:3], dqg, dkg, dsinks.reshape(1, nq), drel.reshape(1, NUM_BUCKETS * nq),
               dconv_vec[0:1], dconv_vec[1:2], dconv_vec[2:3], acc_f[2:3]]
    row3, offs3 = _row_pack(small_g + [dconv_w[:CONV_WIDTH].reshape(1, CONV_WIDTH * ch), loss11])
    got3 = run(allgather_small, row3, "allgather_small_grads")
    cx_in = rs_chip_start("in", ["w_in"], px_in)
    as_row = lambda a: a.reshape(1, -1)
    outs3 = run(small_sum_adamw, got3, offs3, [as_row(a) for a in small_w], [as_row(a) for a in small_m],
                [as_row(a) for a in small_v], [CONV_WIDTH * ch, 1])
    for i, (n, w) in enumerate(zip(small_names, small_w)):
        grads[n], deltas[n], new_m[n], new_v[n] = (o.reshape(w.shape) for o in outs3[4 * i:4 * i + 4])
    g_conv_w_all, loss_sum = outs3[-2].reshape(CONV_WIDTH, ch), outs3[-1]

    g_conv_w = lax.dynamic_slice_in_dim(g_conv_w_all, chip * ch_loc, ch_loc, axis=1)
    grads["conv_w"] = g_conv_w[None]
    dl, nm, nv = run(adamw, conv_w[0], g_conv_w, m_conv_w[0], v_conv_w[0], "adamw_conv_w")
    deltas["conv_w"], new_m["conv_w"], new_v["conv_w"] = dl[None], nm[None], nv[None]

    dmod_all = got3[:, 0, :N_MOD * d]
    dmod_cols = lax.dynamic_slice_in_dim(dmod_all, chip * nc_ada, nc_ada, axis=1)
    g_ada, dl, nm, nv = run(ada_grad_adamw, c_t, dmod_cols, w_ada[0], m_w_ada[0], v_w_ada[0])
    grads["w_ada"], deltas["w_ada"], new_m["w_ada"], new_v["w_ada"] = g_ada[None], dl[None], nm[None], nv[None]

    rs_finish("ffn_in", ["w_ffn_in"], sh_ffn_in)
    rs_finish("mix_out", ["w_mix_out"], sh_mix)
    rs_finish("attn_conv_out", ac_names, sh_ac)
    sh_in = rs_share_start("in", ["w_in"], cx_in)
    rs_finish("in", ["w_in"], sh_in)

    loss = loss_sum[0, 0]
    order = ["w_ada", "b_ada", "norm_mix_g", "w_in", "q_norm_g", "k_norm_g", "attn_sinks", "rel_bias", "w_attn_out",
             "conv_w", "conv_b", "conv_ln_g", "conv_ln_b", "w_conv_out", "w_mix_out", "norm_ffn_g", "w_ffn_in",
             "w_ffn_out"]
    return (loss, grad_x[None], *[grads[n] for n in order], *[deltas[n] for n in order],
            *[new_m[n] for n in order], *[new_v[n] for n in order])
```

```python
import functools
import math
from typing import Any, NamedTuple

import jax
import jax.numpy as jnp
import numpy as np
from jax import lax
from jax.experimental import pallas as pl
from jax.experimental.pallas import tpu as pltpu

F32 = jnp.float32
BF16 = jnp.bfloat16
MESH = pl.DeviceIdType.MESH

V7X_VMEM_BYTES = 64 * 1024 * 1024
VMEM_LIMIT = V7X_VMEM_BYTES - 8 * 1024 * 1024
LANES = 128
SUBLANES = 8
BF16_SUBLANES = 16

EPS = 1e-6
WINDOW = 128
BLOCK = 128
NUM_BUCKETS = 32
MAX_EXACT = NUM_BUCKETS // 2
MAX_DISTANCE = 128
CONV_WIDTH = 31
CONV_HALO = 32
ADAM_LR = 0.001
ADAM_B1 = 0.9
ADAM_B2 = 0.999
ADAM_EPS = 1e-08
ADAM_WD = 0.01
ADAM_STEP = 10
N_MOD = 6
SH_M, SC_M, GT_M, SH_F, SC_F, GT_F = range(6)

N_CHIPS = 4
N_DEV = 8

_ANY = pl.BlockSpec(memory_space=pl.ANY)
_VMEM = pl.BlockSpec(memory_space=pltpu.VMEM)
_SMEM = pl.BlockSpec(memory_space=pltpu.SMEM)
_HBM = pl.BlockSpec(memory_space=pltpu.HBM)
_SEM = pl.BlockSpec(memory_space=pltpu.SEMAPHORE)
_EFFECT = pltpu.SideEffectType.DATAFLOW_SIDE_EFFECTING


class InOrder:
    def __init__(self):
        self.token = None

    def __call__(self, fn, *args, **kw):
        return fn(*args, dep=self, **kw)


def _pallas(body, args, *, in_specs, out_specs, out_shape, name, dep=None, grid=(), n_prefetch=0, scratch=(),
            sem=None, **kw):
    n_lead = n_prefetch + len(in_specs)
    in_specs, args = list(in_specs), list(args)
    single = not isinstance(out_shape, (list, tuple))
    out_shapes = [out_shape] if single else list(out_shape)
    out_specs = [out_specs] if single else list(out_specs)
    if dep is not None:
        inner, n_out, takes = body, len(out_shapes), dep.token is not None

        def body(*refs):
            rest = refs[n_lead + (1 if takes else 0):]
            rest[n_out][...] = jnp.zeros((SUBLANES, LANES), F32)
            return inner(*refs[:n_lead], *rest[:n_out], *rest[n_out + 1:])

        if takes:
            in_specs.append(_ANY)
            args.append(dep.token)
        out_shapes.append(jax.ShapeDtypeStruct((SUBLANES, LANES), F32))
        out_specs.append(pl.BlockSpec((SUBLANES, LANES), lambda *_: (0, 0)))
    params = kw.pop("compiler_params", None)
    if params is None:
        params = pltpu.CompilerParams(dimension_semantics=sem, vmem_limit_bytes=VMEM_LIMIT)
    outs = pl.pallas_call(
        body,
        grid_spec=pltpu.PrefetchScalarGridSpec(num_scalar_prefetch=n_prefetch, grid=grid, in_specs=in_specs,
                                               out_specs=out_specs, scratch_shapes=list(scratch)),
        out_shape=out_shapes, compiler_params=params, name=name, **kw,
    )(*args)
    if dep is not None:
        dep.token = outs[-1]
        outs = outs[:-1]
    return outs[0] if single else list(outs)


def _tile(n, pref, unit=LANES):
    best = None
    for t in range(unit, min(n, pref) + 1, unit):
        if n % t == 0:
            best = t
    return best if best is not None else n


def _sigmoid(v):
    return 1.0 / (1.0 + jnp.exp(-v.astype(F32)))


ROW_CHUNK = 1024
TILE_N = 512
TILE_WIDE = 1024
EW_ELEMS = 512 * 1024
EW_ELEMS_MANY = 256 * 1024


def _row_chunks(m, unit=SUBLANES):
    step = _tile(m, ROW_CHUNK, unit)
    return [(s, step) for s in range(0, m, step)]


def _ew_tiles(r, n, unit=SUBLANES, elems=EW_ELEMS):
    return _tile(r, max(unit, elems // n), unit), n


def _block_pos(j, perm):
    if perm is None:
        return j
    pos = 0
    for a, p in enumerate(perm):
        pos = pos + jnp.where(j == a, p, 0)
    return pos


def mm_nn(a, w, *, tn, tk, out_dtype, name, perm=None, dep=None):
    m, k = a.shape
    j, k2, nj = w.shape
    assert k == k2 and nj % tn == 0 and k % tk == 0
    npj, nk = nj // tn, k // tk

    def body(a_ref, w_ref, o_ref, *scratch):
        kk = pl.program_id(1)
        for s, sz in _row_chunks(m):
            rows = pl.ds(s, sz)
            p = jnp.dot(a_ref[rows, :], w_ref[...], preferred_element_type=F32)
            if nk == 1:
                o_ref[rows, :] = p.astype(out_dtype)
            else:
                acc = scratch[0]

                @pl.when(kk == 0)
                def _():
                    acc[rows, :] = p

                @pl.when(kk > 0)
                def _():
                    acc[rows, :] += p

                @pl.when(kk == nk - 1)
                def _():
                    o_ref[rows, :] = acc[rows, :].astype(out_dtype)

    return _pallas(
        body, [a, w], dep=dep, grid=(j * npj, nk),
        in_specs=[
            pl.BlockSpec((m, tk), lambda n, kk: (0, kk)),
            pl.BlockSpec((None, tk, tn), lambda n, kk: (n // npj, kk, n % npj)),
        ],
        out_specs=pl.BlockSpec((m, tn), lambda n, kk: (0, _block_pos(n // npj, perm) * npj + n % npj)),
        out_shape=jax.ShapeDtypeStruct((m, j * nj), out_dtype),
        scratch=[pltpu.VMEM((m, tn), F32)] if nk > 1 else [],
        sem=("parallel", "arbitrary"), name=name)


def mm_nt(g, w, *, tko, tn, name, out_dtype=F32, perm=None, dep=None):
    m, n = g.shape
    j, k, nj = w.shape
    assert n == j * nj and nj % tn == 0 and k % tko == 0
    npj, nr = nj // tn, n // tn
    in_place = out_dtype == F32

    def body(g_ref, w_ref, o_ref, *scratch):
        r = pl.program_id(1)
        acc = o_ref if in_place else (scratch[0] if nr > 1 else None)
        for s, sz in _row_chunks(m):
            rows = pl.ds(s, sz)
            p = lax.dot_general(g_ref[rows, :], w_ref[...], (((1,), (1,)), ((), ())), preferred_element_type=F32)
            if acc is None:
                o_ref[rows, :] = p.astype(out_dtype)
                continue

            @pl.when(r == 0)
            def _():
                acc[rows, :] = p

            @pl.when(r > 0)
            def _():
                acc[rows, :] += p

            if not in_place:
                @pl.when(r == nr - 1)
                def _():
                    o_ref[rows, :] = acc[rows, :].astype(out_dtype)

    return _pallas(
        body, [g, w], dep=dep, grid=(k // tko, nr),
        in_specs=[
            pl.BlockSpec((m, tn), lambda ko, r: (0, _block_pos(r // npj, perm) * npj + r % npj)),
            pl.BlockSpec((None, tko, tn), lambda ko, r: (r // npj, ko, r % npj)),
        ],
        out_specs=pl.BlockSpec((m, tko), lambda ko, r: (0, ko)),
        out_shape=jax.ShapeDtypeStruct((m, k), out_dtype),
        scratch=[pltpu.VMEM((m, tko), F32)] if (nr > 1 and not in_place) else [],
        sem=("parallel", "arbitrary"), name=name)


def mm_tn(a, g, n_blocks, *, tk, tn, name, perm=None, dep=None):
    m, k = a.shape
    m2, n = g.shape
    nj = n // n_blocks
    assert m == m2 and nj % tn == 0 and k % tk == 0
    npj = nj // tn

    def body(a_ref, g_ref, o_ref):
        for s, sz in _row_chunks(tk, LANES):
            p = lax.dot_general(a_ref[:, pl.ds(s, sz)], g_ref[...], (((0,), (0,)), ((), ())),
                                preferred_element_type=F32)
            o_ref[pl.ds(s, sz), :] = p.astype(BF16)

    return _pallas(
        body, [a, g], dep=dep, grid=(k // tk, n // tn),
        in_specs=[
            pl.BlockSpec((m, tk), lambda kk, nn: (0, kk)),
            pl.BlockSpec((m, tn), lambda kk, nn: (0, _block_pos(nn // npj, perm) * npj + nn % npj)),
        ],
        out_specs=pl.BlockSpec((None, tk, tn), lambda kk, nn: (nn // npj, kk, nn % npj)),
        out_shape=jax.ShapeDtypeStruct((n_blocks, k, nj), BF16),
        sem=("parallel", "parallel"), name=name)


ROW_TILE = 256


def _row_spec(tr, width):
    return pl.BlockSpec((tr, width), lambda i: (i, 0))


def _full_spec(shape):
    return pl.BlockSpec(shape, lambda *_: (0,) * len(shape))


def _rms(xv):
    return lax.rsqrt(jnp.mean(xv * xv, axis=-1, keepdims=True) + EPS)


def _mod_row(mod_ref, row):
    return mod_ref[pl.ds(row, 1), :]


def pre_mix_fwd(x, mod, gain, dep=None):
    t, d = x.shape
    tr = _tile(t, ROW_TILE, SUBLANES)

    def body(x_ref, mod_ref, g_ref, h_ref):
        xv = x_ref[...]
        y = xv * _rms(xv) * g_ref[...]
        h_ref[...] = (y * (1.0 + _mod_row(mod_ref, SC_M)) + _mod_row(mod_ref, SH_M)).astype(BF16)

    return _pallas(
        body, [x, mod, gain], dep=dep, grid=(t // tr,),
        in_specs=[_row_spec(tr, d), _full_spec(mod.shape), _full_spec(gain.shape)],
        out_specs=_row_spec(tr, d),
        out_shape=jax.ShapeDtypeStruct((t, d), BF16),
        sem=("parallel",), name="pre_mix_fwd")


def pre_ffn_fwd(x, o_m, mod, gain, dep=None):
    t, d = x.shape
    tr = _tile(t, ROW_TILE, SUBLANES)

    def body(x_ref, om_ref, mod_ref, g_ref, x1_ref, h_ref):
        x1 = x_ref[...] + _mod_row(mod_ref, GT_M) * om_ref[...]
        x1_ref[...] = x1
        y = x1 * _rms(x1) * g_ref[...]
        h_ref[...] = (y * (1.0 + _mod_row(mod_ref, SC_F)) + _mod_row(mod_ref, SH_F)).astype(BF16)

    return _pallas(
        body, [x, o_m, mod, gain], dep=dep, grid=(t // tr,),
        in_specs=[_row_spec(tr, d), _row_spec(tr, d), _full_spec(mod.shape), _full_spec(gain.shape)],
        out_specs=[_row_spec(tr, d), _row_spec(tr, d)],
        out_shape=[jax.ShapeDtypeStruct((t, d), F32), jax.ShapeDtypeStruct((t, d), BF16)],
        sem=("parallel",), name="pre_ffn_fwd")


def loss_head(x1, o_f, target, mod, dep=None):
    t, d = x1.shape
    tr = _tile(t, ROW_TILE, SUBLANES)

    def body(x1_ref, of_ref, tg_ref, mod_ref, loss_ref, dy_ref, dof_ref, acc_ref):
        i = pl.program_id(0)
        gt = _mod_row(mod_ref, GT_F)
        of = of_ref[...].astype(F32)
        err = x1_ref[...] + gt * of - tg_ref[...]
        dy = err * (1.0 / d)
        dy_ref[...] = dy.astype(BF16)
        dof_ref[...] = (dy * gt).astype(BF16)
        part = (0.5 / d) * jnp.sum(jnp.sum(err * err, axis=1, keepdims=True), axis=0, keepdims=True)
        dgt = jnp.sum(dy * of, axis=0, keepdims=True)

        @pl.when(i == 0)
        def _():
            loss_ref[...] = jnp.zeros_like(loss_ref)
            acc_ref[...] = jnp.zeros_like(acc_ref)

        loss_ref[...] += part
        acc_ref[pl.ds(0, 1), :] += dgt

    return _pallas(
        body, [x1, o_f, target, mod], dep=dep, grid=(t // tr,),
        in_specs=[_row_spec(tr, d), _row_spec(tr, d), _row_spec(tr, d), _full_spec(mod.shape)],
        out_specs=[_full_spec((1, 1)), _row_spec(tr, d), _row_spec(tr, d), _full_spec((SUBLANES, d))],
        out_shape=[jax.ShapeDtypeStruct((1, 1), F32), jax.ShapeDtypeStruct((t, d), BF16),
                   jax.ShapeDtypeStruct((t, d), BF16), jax.ShapeDtypeStruct((SUBLANES, d), F32)],
        sem=("arbitrary",), name="loss_head")


def _norm_bwd(xv, dh, sc, gain):
    rstd = _rms(xv)
    yn = xv * rstd
    dsh = jnp.sum(dh, axis=0, keepdims=True)
    dsc = jnp.sum(dh * (yn * gain), axis=0, keepdims=True)
    dgain = jnp.sum(dh * (1.0 + sc) * yn, axis=0, keepdims=True)
    dyn = dh * ((1.0 + sc) * gain)
    dx = rstd * (dyn - yn * jnp.mean(dyn * yn, axis=-1, keepdims=True))
    return dx, dsh, dsc, dgain


def pre_ffn_bwd(x1, dh2, dy, o_m, mod, gain, dep=None):
    t, d = x1.shape
    tr = _tile(t, ROW_TILE, SUBLANES)

    def body(x1_ref, dh_ref, dy_ref, om_ref, mod_ref, g_ref, dx1_ref, dom_ref, acc_ref):
        i = pl.program_id(0)
        dxn, dsh, dsc, dgain = _norm_bwd(x1_ref[...], dh_ref[...].astype(F32), _mod_row(mod_ref, SC_F), g_ref[...])
        dx1 = dy_ref[...] + dxn
        dx1_ref[...] = dx1
        dom_ref[...] = (dx1 * _mod_row(mod_ref, GT_M)).astype(BF16)
        dgt = jnp.sum(dx1 * om_ref[...], axis=0, keepdims=True)

        @pl.when(i == 0)
        def _():
            acc_ref[...] = jnp.zeros_like(acc_ref)

        acc_ref[pl.ds(0, 1), :] += dsh
        acc_ref[pl.ds(1, 1), :] += dsc
        acc_ref[pl.ds(2, 1), :] += dgain
        acc_ref[pl.ds(3, 1), :] += dgt

    return _pallas(
        body, [x1, dh2, dy, o_m, mod, gain], dep=dep, grid=(t // tr,),
        in_specs=[_row_spec(tr, d)] * 4 + [_full_spec(mod.shape), _full_spec(gain.shape)],
        out_specs=[_row_spec(tr, d), _row_spec(tr, d), _full_spec((SUBLANES, d))],
        out_shape=[jax.ShapeDtypeStruct((t, d), F32), jax.ShapeDtypeStruct((t, d), BF16),
                   jax.ShapeDtypeStruct((SUBLANES, d), F32)],
        sem=("arbitrary",), name="pre_ffn_bwd")


def pre_mix_bwd(x, dh, dx1, mod, gain, dep=None):
    t, d = x.shape
    tr = _tile(t, ROW_TILE, SUBLANES)

    def body(x_ref, dh_ref, dx1_ref, mod_ref, g_ref, gx_ref, acc_ref):
        i = pl.program_id(0)
        dxn, dsh, dsc, dgain = _norm_bwd(x_ref[...], dh_ref[...].astype(F32), _mod_row(mod_ref, SC_M), g_ref[...])
        gx_ref[...] = dx1_ref[...] + dxn

        @pl.when(i == 0)
        def _():
            acc_ref[...] = jnp.zeros_like(acc_ref)

        acc_ref[pl.ds(0, 1), :] += dsh
        acc_ref[pl.ds(1, 1), :] += dsc
        acc_ref[pl.ds(2, 1), :] += dgain

    return _pallas(
        body, [x, dh, dx1, mod, gain], dep=dep, grid=(t // tr,),
        in_specs=[_row_spec(tr, d)] * 3 + [_full_spec(mod.shape), _full_spec(gain.shape)],
        out_specs=[_row_spec(tr, d), _full_spec((SUBLANES, d))],
        out_shape=[jax.ShapeDtypeStruct((t, d), F32), jax.ShapeDtypeStruct((SUBLANES, d), F32)],
        sem=("arbitrary",), name="pre_mix_bwd")


def branch_out_merge(attn_o, s_conv, w_attn, w_conv, p, off_ga, off_gc, dep=None):
    t = attn_o.shape[0]
    j, ka, nj = w_attn.shape
    kc = w_conv.shape[1]
    assert off_ga % nj == 0 and off_gc % nj == 0

    def body(a_ref, s_ref, wa_ref, wc_ref, ga_ref, gc_ref, ya_ref, yc_ref, m_ref):
        for s, sz in _row_chunks(t):
            rows = pl.ds(s, sz)
            ya = jnp.dot(a_ref[rows, :], wa_ref[...], preferred_element_type=F32)
            yc = jnp.dot(s_ref[rows, :], wc_ref[...], preferred_element_type=F32)
            ya_ref[rows, :] = ya.astype(BF16)
            yc_ref[rows, :] = yc.astype(BF16)
            m_ref[rows, :] = (_sigmoid(ga_ref[rows, :]) * ya + _sigmoid(gc_ref[rows, :]) * yc).astype(BF16)

    col = pl.BlockSpec((t, nj), lambda b: (0, b))
    return _pallas(
        body, [attn_o, s_conv, w_attn, w_conv, p, p], dep=dep, grid=(j,),
        in_specs=[pl.BlockSpec((t, ka), lambda b: (0, 0)), pl.BlockSpec((t, kc), lambda b: (0, 0)),
                  pl.BlockSpec((None, ka, nj), lambda b: (b, 0, 0)), pl.BlockSpec((None, kc, nj), lambda b: (b, 0, 0)),
                  pl.BlockSpec((t, nj), lambda b: (0, off_ga // nj + b)),
                  pl.BlockSpec((t, nj), lambda b: (0, off_gc // nj + b))],
        out_specs=[col] * 3,
        out_shape=[jax.ShapeDtypeStruct((t, j * nj), BF16)] * 3,
        sem=("parallel",), name="branch_out_merge")


def mix_out_dx_merge_bwd(dom, w_mix, p, y_attn, y_conv, off_ga, off_gc, dep=None):
    t, d = y_attn.shape
    cw = math.gcd(math.gcd(off_ga, off_gc), math.gcd(d, TILE_N // 2))

    def body(dom_ref, w_ref, ga_ref, gc_ref, ya_ref, yc_ref, dya_ref, dyc_ref, dga_ref, dgc_ref):
        for s, sz in _row_chunks(t):
            rows = pl.ds(s, sz)
            dm = lax.dot_general(dom_ref[rows, :], w_ref[...], (((1,), (1,)), ((), ())), preferred_element_type=F32)
            sa = _sigmoid(ga_ref[rows, :])
            sc = _sigmoid(gc_ref[rows, :])
            dya_ref[rows, :] = (dm * sa).astype(BF16)
            dyc_ref[rows, :] = (dm * sc).astype(BF16)
            dga_ref[rows, :] = (dm * ya_ref[rows, :] * sa * (1.0 - sa)).astype(BF16)
            dgc_ref[rows, :] = (dm * yc_ref[rows, :] * sc * (1.0 - sc)).astype(BF16)

    col = pl.BlockSpec((t, cw), lambda j: (0, j))
    return _pallas(
        body, [dom, w_mix, p, p, y_attn, y_conv], dep=dep, grid=(d // cw,),
        in_specs=[pl.BlockSpec((t, d), lambda j: (0, 0)), pl.BlockSpec((cw, d), lambda j: (j, 0)),
                  pl.BlockSpec((t, cw), lambda j: (0, off_ga // cw + j)),
                  pl.BlockSpec((t, cw), lambda j: (0, off_gc // cw + j)), col, col],
        out_specs=[col] * 4,
        out_shape=[jax.ShapeDtypeStruct((t, d), BF16)] * 4,
        sem=("parallel",), name="mm_mix_out_dx_merge_bwd")


def ffn_perm(n_blocks):
    half = n_blocks // 2
    return tuple(2 * j if j < half else 2 * (j - half) + 1 for j in range(n_blocks))


def swiglu_fwd(f, nj, dep=None):
    t, two = f.shape
    tr = _tile(t, ROW_TILE, SUBLANES)
    npair = two // (2 * nj)

    def body(f_ref, o_ref):
        g = f_ref[:, :nj].astype(F32)
        u = f_ref[:, nj:].astype(F32)
        o_ref[...] = (g * _sigmoid(g) * u).astype(BF16)

    return _pallas(
        body, [f], dep=dep, grid=(t // tr, npair),
        in_specs=[pl.BlockSpec((tr, 2 * nj), lambda i, j: (i, j))],
        out_specs=pl.BlockSpec((tr, nj), lambda i, j: (i, j)),
        out_shape=jax.ShapeDtypeStruct((t, two // 2), BF16),
        sem=("parallel", "parallel"), name="swiglu_fwd")


def swiglu_bwd(f, dact, nj, dep=None):
    t, two = f.shape
    tr = _tile(t, ROW_TILE, SUBLANES)
    npair = two // (2 * nj)

    def body(f_ref, da_ref, o_ref):
        g = f_ref[:, :nj].astype(F32)
        u = f_ref[:, nj:].astype(F32)
        da = da_ref[...]
        s = _sigmoid(g)
        o_ref[:, :nj] = (da * u * (s * (1.0 + g * (1.0 - s)))).astype(BF16)
        o_ref[:, nj:] = (da * (g * s)).astype(BF16)

    return _pallas(
        body, [f, dact], dep=dep, grid=(t // tr, npair),
        in_specs=[pl.BlockSpec((tr, 2 * nj), lambda i, j: (i, j)), pl.BlockSpec((tr, nj), lambda i, j: (i, j))],
        out_specs=pl.BlockSpec((tr, 2 * nj), lambda i, j: (i, j)),
        out_shape=jax.ShapeDtypeStruct((t, two), BF16),
        sem=("parallel", "parallel"), name="swiglu_bwd")


def _t5_bucket_table():
    q_off = np.arange(BLOCK)
    k_off = np.arange(2 * BLOCK)
    dist = q_off[:, None] + BLOCK - k_off[None, :]
    n = np.maximum(dist, 0)
    nf = np.maximum(n, 1).astype(np.float32)
    large = MAX_EXACT + (np.log(nf / np.float32(MAX_EXACT)) / np.float32(math.log(MAX_DISTANCE / MAX_EXACT))
                         * np.float32(NUM_BUCKETS - MAX_EXACT)).astype(np.int32)
    large = np.minimum(large, NUM_BUCKETS - 1)
    bucket = np.where(n < MAX_EXACT, n, large).astype(np.int32)
    allowed = (dist >= 0) & (dist < WINDOW)
    return np.where(allowed, bucket, -1).astype(np.int32)


def bias_table(rel_bias, bucket_p, bucket_c, dep=None):
    nb, nq = rel_bias.shape

    def body(rb_ref, bkp_ref, bkc_ref, op_ref, oc_ref):
        for bk_ref, o_ref in ((bkp_ref, op_ref), (bkc_ref, oc_ref)):
            bk = bk_ref[...]
            for h in range(nq):
                acc = jnp.full(bk.shape, -jnp.inf, F32)
                for b in range(nb):
                    acc = jnp.where(bk == b, rb_ref[b, h], acc)
                o_ref[h] = acc

    return _pallas(
        body, [rel_bias, bucket_p, bucket_c], dep=dep,
        in_specs=[_SMEM, _VMEM, _VMEM], out_specs=[_VMEM, _VMEM],
        out_shape=[jax.ShapeDtypeStruct((nq,) + bucket_p.shape, F32)] * 2,
        name="bias_table")


def bias_table_bwd(dbp, dbc, bucket_p, bucket_c, dep=None):
    nq = dbp.shape[0]

    def body(dbp_ref, dbc_ref, bkp_ref, bkc_ref, o_ref):
        bkp, bkc = bkp_ref[...][None], bkc_ref[...][None]
        dp, dc = dbp_ref[...], dbc_ref[...]
        for b in range(NUM_BUCKETS):
            sel = jnp.where(bkp == b, dp, 0.0) + jnp.where(bkc == b, dc, 0.0)
            o_ref[b] = jnp.sum(jnp.sum(sel, axis=2, keepdims=True), axis=1, keepdims=True)

    return _pallas(
        body, [dbp, dbc, bucket_p, bucket_c], dep=dep,
        in_specs=[_VMEM] * 4, out_specs=_VMEM,
        out_shape=jax.ShapeDtypeStruct((NUM_BUCKETS, nq, 1, 1), F32),
        name="bias_table_bwd")


_BNT = (((2,), (2,)), ((0,), (0,)))
_BNN = (((2,), (1,)), ((0,), (0,)))
_BTN = (((1,), (1,)), ((0,), (0,)))


@jax.custom_vjp
def _bdot_nt(a, b):
    return lax.dot_general(a.astype(BF16), b.astype(BF16), _BNT, preferred_element_type=F32)


def _bdot_nt_fwd(a, b):
    return _bdot_nt(a, b), (a, b)


def _bdot_nt_bwd(res, g):
    a, b = res
    gb = g.astype(BF16)
    da = lax.dot_general(gb, b.astype(BF16), _BNN, preferred_element_type=F32)
    db = lax.dot_general(gb, a.astype(BF16), _BTN, preferred_element_type=F32)
    return da, db


_bdot_nt.defvjp(_bdot_nt_fwd, _bdot_nt_bwd)


@jax.custom_vjp
def _bdot_nn(a, b):
    return lax.dot_general(a.astype(BF16), b.astype(BF16), _BNN, preferred_element_type=F32)


def _bdot_nn_fwd(a, b):
    return _bdot_nn(a, b), (a, b)


def _bdot_nn_bwd(res, g):
    a, b = res
    gb = g.astype(BF16)
    da = lax.dot_general(gb, b.astype(BF16), _BNT, preferred_element_type=F32)
    db = lax.dot_general(a.astype(BF16), gb, _BTN, preferred_element_type=F32)
    return da, db


_bdot_nn.defvjp(_bdot_nn_fwd, _bdot_nn_bwd)


def _attn_math(q, kp, kc, vp, vc, bp, bc, sinks, qg, kg, *, prev_ok, scale):
    h, rows, _ = q.shape
    b = kp.shape[1]
    qn = q * _rms(q) * qg
    kpn = kp * _rms(kp) * kg
    kcn = kc * _rms(kc) * kg
    lp = _bdot_nt(qn, kpn) * scale + bp.reshape(h, rows, b)
    lc = _bdot_nt(qn, kcn) * scale + bc.reshape(h, rows, b)
    lp = jnp.where(prev_ok, lp, -jnp.inf)
    sink = jnp.broadcast_to(sinks, (sinks.shape[0], b, 1)).reshape(h, rows, 1)
    m = jnp.maximum(jnp.maximum(jnp.max(lp, axis=-1, keepdims=True), jnp.max(lc, axis=-1, keepdims=True)), sink)
    m = lax.stop_gradient(m)
    pp = jnp.exp(lp - m)
    pc = jnp.exp(lc - m)
    den = jnp.sum(pp, axis=-1, keepdims=True) + jnp.sum(pc, axis=-1, keepdims=True) + jnp.exp(sink - m)
    inv = 1.0 / den
    return _bdot_nn(pp * inv, vp) + _bdot_nn(pc * inv, vc)


def _attn_specs(p, aw, kvw, nq, hd, nblk, reverse):
    assert aw % (2 * kvw) == 0
    kv_col = aw // (2 * kvw)

    def blk(n):
        return nblk - 1 - n if reverse else n

    return [
        pl.BlockSpec((BLOCK, aw), lambda n: (blk(n), 0)),
        pl.BlockSpec((BLOCK, 2 * kvw), lambda n: (jnp.maximum(blk(n) - 1, 0), kv_col)),
        pl.BlockSpec((BLOCK, 2 * kvw), lambda n: (blk(n), kv_col)),
        _full_spec((nq, BLOCK, BLOCK)), _full_spec((nq, BLOCK, BLOCK)), _full_spec((nq, 1, 1)),
        _full_spec((1, hd)), _full_spec((1, hd)),
    ]


def _head_major(ref, n_heads, grp, hd, offset=0):
    return jnp.stack([
        jnp.concatenate([ref[:, pl.ds(offset + (grp * h + g) * hd, hd)].astype(F32) for g in range(grp)], axis=0)
        for h in range(n_heads)])


def _attn_inputs(nkv, grp, hd, kvw, q_ref, kvp_ref, kvc_ref):
    return (_head_major(q_ref, nkv, grp, hd), _head_major(kvp_ref, nkv, 1, hd), _head_major(kvc_ref, nkv, 1, hd),
            _head_major(kvp_ref, nkv, 1, hd, kvw), _head_major(kvc_ref, nkv, 1, hd, kvw))


def attn_fwd(p, bias_p, bias_c, sinks, qg, kg, *, aw, kvw, dep=None):
    t, hd = p.shape[0], qg.shape[-1]
    nq, nkv, nblk = aw // hd, kvw // hd, t // BLOCK
    grp = nq // nkv
    scale = hd ** -0.5

    def body(q_ref, kvp_ref, kvc_ref, bp_ref, bc_ref, s_ref, qg_ref, kg_ref, o_ref):
        prev_ok = pl.program_id(0) > 0
        out = _attn_math(*_attn_inputs(nkv, grp, hd, kvw, q_ref, kvp_ref, kvc_ref), bp_ref[...], bc_ref[...],
                         s_ref[...], qg_ref[...], kg_ref[...], prev_ok=prev_ok, scale=scale)
        for h in range(nkv):
            for g in range(grp):
                o_ref[:, pl.ds((grp * h + g) * hd, hd)] = out[h, g * BLOCK:(g + 1) * BLOCK].astype(BF16)

    return _pallas(
        body, [p, p, p, bias_p, bias_c, sinks, qg, kg], dep=dep, grid=(nblk,),
        in_specs=_attn_specs(p, aw, kvw, nq, hd, nblk, False),
        out_specs=pl.BlockSpec((BLOCK, aw), lambda n: (n, 0)),
        out_shape=jax.ShapeDtypeStruct((t, aw), BF16),
        sem=("parallel",), name="attn_fwd")


def attn_bwd(p, bias_p, bias_c, sinks, qg, kg, do, *, aw, kvw, dep=None):
    t, hd = p.shape[0], qg.shape[-1]
    nq, nkv, nblk = aw // hd, kvw // hd, t // BLOCK
    grp = nq // nkv
    scale = hd ** -0.5

    def body(q_ref, kvp_ref, kvc_ref, bp_ref, bc_ref, s_ref, qg_ref, kg_ref, do_ref,
             dqkv_ref, dbp_ref, dbc_ref, ds_ref, dqg_ref, dkg_ref, carry):
        i = pl.program_id(0)
        prev_ok = (nblk - 1 - i) > 0

        @pl.when(i == 0)
        def _():
            carry[...] = jnp.zeros_like(carry)
            dbp_ref[...] = jnp.zeros_like(dbp_ref)
            dbc_ref[...] = jnp.zeros_like(dbc_ref)
            ds_ref[...] = jnp.zeros_like(ds_ref)
            dqg_ref[...] = jnp.zeros_like(dqg_ref)
            dkg_ref[...] = jnp.zeros_like(dkg_ref)

        fn = functools.partial(_attn_math, prev_ok=prev_ok, scale=scale)
        _, vjp = jax.vjp(fn, *_attn_inputs(nkv, grp, hd, kvw, q_ref, kvp_ref, kvc_ref), bp_ref[...], bc_ref[...],
                         s_ref[...], qg_ref[...], kg_ref[...])
        dq, dkp, dkc, dvp, dvc, dbp, dbc, dsk, dqg, dkg = vjp(_head_major(do_ref, nkv, grp, hd))
        for h in range(nkv):
            for g in range(grp):
                dqkv_ref[:, pl.ds((grp * h + g) * hd, hd)] = dq[h, g * BLOCK:(g + 1) * BLOCK].astype(BF16)
            k_cols, v_cols = pl.ds(h * hd, hd), pl.ds(kvw + h * hd, hd)
            dqkv_ref[:, pl.ds(aw + h * hd, hd)] = (dkc[h] + carry[:, k_cols]).astype(BF16)
            dqkv_ref[:, pl.ds(aw + kvw + h * hd, hd)] = (dvc[h] + carry[:, v_cols]).astype(BF16)
            carry[:, k_cols] = dkp[h]
            carry[:, v_cols] = dvp[h]
        dbp_ref[...] += dbp
        dbc_ref[...] += dbc
        ds_ref[...] += dsk
        dqg_ref[...] += dqg
        dkg_ref[...] += dkg

    return _pallas(
        body, [p, p, p, bias_p, bias_c, sinks, qg, kg, do], dep=dep, grid=(nblk,),
        in_specs=_attn_specs(p, aw, kvw, nq, hd, nblk, True)
        + [pl.BlockSpec((BLOCK, aw), lambda n: (nblk - 1 - n, 0))],
        out_specs=[
            pl.BlockSpec((BLOCK, aw + 2 * kvw), lambda n: (nblk - 1 - n, 0)),
            _full_spec((nq, BLOCK, BLOCK)), _full_spec((nq, BLOCK, BLOCK)), _full_spec((nq, 1, 1)),
            _full_spec((1, hd)), _full_spec((1, hd)),
        ],
        out_shape=[
            jax.ShapeDtypeStruct((t, aw + 2 * kvw), BF16),
            jax.ShapeDtypeStruct((nq, BLOCK, BLOCK), F32),
            jax.ShapeDtypeStruct((nq, BLOCK, BLOCK), F32),
            jax.ShapeDtypeStruct((nq, 1, 1), F32),
            jax.ShapeDtypeStruct((1, hd), F32),
            jax.ShapeDtypeStruct((1, hd), F32),
        ],
        scratch=[pltpu.VMEM((BLOCK, 2 * kvw), F32)],
        sem=("arbitrary",), name="attn_bwd")


CONV_TILE = 256


def _conv_halo_specs(tb, ch, nblk):
    per = tb // CONV_HALO
    last = nblk * per - 1
    cur = pl.BlockSpec((tb, ch), lambda n: (n, 0))
    prev = pl.BlockSpec((CONV_HALO, ch), lambda n: (jnp.maximum(n * per - 1, 0), 0))
    nxt = pl.BlockSpec((CONV_HALO, ch), lambda n: (jnp.minimum((n + 1) * per, last), 0))
    return cur, prev, nxt


def _ln_silu(co, ln_g, ln_b):
    mu = jnp.mean(co, axis=-1, keepdims=True)
    cen = co - mu
    rstd = lax.rsqrt(jnp.mean(cen * cen, axis=-1, keepdims=True) + EPS)
    xhat = cen * rstd
    z = xhat * ln_g + ln_b
    return xhat, rstd, z


def _shifted_copies(src, shifted):
    rows = src.shape[0] - SUBLANES
    for r in range(1, SUBLANES):
        shifted[r, pl.ds(0, rows), :] = src[pl.ds(r, rows), :]


def _rows_from(src, shifted, start, n):
    r = start % SUBLANES
    if r == 0:
        return src[pl.ds(start, n), :]
    return shifted[r, pl.ds(start - r, n), :]


def conv_fwd(ca, cb, conv_w, conv_b, ln_g, ln_b, dep=None):
    t, ch = ca.shape
    tb = _tile(t, CONV_TILE, CONV_HALO)
    nblk = t // tb
    cur, prev, _ = _conv_halo_specs(tb, ch, nblk)
    lead = CONV_HALO - (CONV_WIDTH - 1)

    def body(ca_ref, cb_ref, cap_ref, cbp_ref, w_ref, b_ref, g_ref, bb_ref, s_ref, co_ref, ubuf, ushift):
        n = pl.program_id(0)
        halo = cap_ref[...] * _sigmoid(cbp_ref[...])
        ubuf[pl.ds(0, CONV_HALO), :] = jnp.where(n > 0, halo, 0.0)
        ubuf[pl.ds(CONV_HALO, tb), :] = ca_ref[...] * _sigmoid(cb_ref[...])
        _shifted_copies(ubuf, ushift)
        acc = jnp.broadcast_to(b_ref[...], (tb, ch))
        for k in range(CONV_WIDTH):
            acc = acc + w_ref[pl.ds(k, 1), :] * _rows_from(ubuf, ushift, lead + k, tb)
        co_ref[...] = acc
        _, _, z = _ln_silu(acc, g_ref[...], bb_ref[...])
        s_ref[...] = (z * _sigmoid(z)).astype(BF16)

    vec = _full_spec((1, ch))
    return _pallas(
        body, [ca, cb, ca, cb, conv_w, conv_b, ln_g, ln_b], dep=dep, grid=(nblk,),
        in_specs=[cur, cur, prev, prev, _full_spec(conv_w.shape), vec, vec, vec],
        out_specs=[cur, cur],
        out_shape=[jax.ShapeDtypeStruct((t, ch), BF16), jax.ShapeDtypeStruct((t, ch), F32)],
        scratch=[pltpu.VMEM((CONV_HALO + tb, ch), F32), pltpu.VMEM((SUBLANES, CONV_HALO + tb, ch), F32)],
        sem=("parallel",), name="conv_fwd")


def conv_bwd(ca, cb, co, ds, conv_w, ln_g, ln_b, dep=None):
    t, ch = ca.shape
    tb = _tile(t, CONV_TILE, CONV_HALO)
    nblk = t // tb
    cur, prev, nxt = _conv_halo_specs(tb, ch, nblk)
    lead = CONV_HALO - (CONV_WIDTH - 1)
    ext = tb + CONV_HALO

    def body(ca_ref, cb_ref, cap_ref, cbp_ref, co_ref, con_ref, ds_ref, dsn_ref, w_ref, g_ref, bb_ref,
             dca_ref, dcb_ref, dw_ref, dvec_ref, ubuf, dbuf, ushift, dshift):
        n = pl.program_id(0)
        is_last = n == nblk - 1
        sig_b = _sigmoid(cb_ref[...])
        cav = ca_ref[...].astype(F32)
        ubuf[pl.ds(0, CONV_HALO), :] = jnp.where(n > 0, cap_ref[...] * _sigmoid(cbp_ref[...]), 0.0)
        ubuf[pl.ds(CONV_HALO, tb), :] = cav * sig_b
        _shifted_copies(ubuf, ushift)
        co = jnp.concatenate([co_ref[...], con_ref[...]], axis=0)
        xhat, rstd, z = _ln_silu(co, g_ref[...], bb_ref[...])
        dsv = jnp.concatenate([ds_ref[...].astype(F32), jnp.where(is_last, 0.0, dsn_ref[...].astype(F32))], axis=0)
        sg = _sigmoid(z)
        dz = dsv * (sg * (1.0 + z * (1.0 - sg)))
        dxh = dz * g_ref[...]
        dco = rstd * (dxh - jnp.mean(dxh, axis=-1, keepdims=True)
                      - xhat * jnp.mean(dxh * xhat, axis=-1, keepdims=True))
        dbuf[...] = dco
        _shifted_copies(dbuf, dshift)

        @pl.when(n == 0)
        def _():
            dw_ref[...] = jnp.zeros_like(dw_ref)
            dvec_ref[...] = jnp.zeros_like(dvec_ref)

        dco_cur = dco[:tb]
        dvec_ref[pl.ds(0, 1), :] += jnp.sum(dco_cur, axis=0, keepdims=True)
        dvec_ref[pl.ds(1, 1), :] += jnp.sum(dz[:tb] * xhat[:tb], axis=0, keepdims=True)
        dvec_ref[pl.ds(2, 1), :] += jnp.sum(dz[:tb], axis=0, keepdims=True)
        du = jnp.zeros((tb, ch), F32)
        for k in range(CONV_WIDTH):
            du = du + w_ref[pl.ds(k, 1), :] * _rows_from(dbuf, dshift, CONV_WIDTH - 1 - k, tb)
            dw_ref[pl.ds(k, 1), :] += jnp.sum(dco_cur * _rows_from(ubuf, ushift, lead + k, tb), axis=0,
                                              keepdims=True)
        dca_ref[...] = (du * sig_b).astype(BF16)
        dcb_ref[...] = (du * cav * sig_b * (1.0 - sig_b)).astype(BF16)

    vec = _full_spec((1, ch))
    return _pallas(
        body, [ca, cb, ca, cb, co, co, ds, ds, conv_w, ln_g, ln_b], dep=dep, grid=(nblk,),
        in_specs=[cur, cur, prev, prev, cur, nxt, cur, nxt, _full_spec(conv_w.shape), vec, vec],
        out_specs=[cur, cur, _full_spec(conv_w.shape), _full_spec((SUBLANES, ch))],
        out_shape=[jax.ShapeDtypeStruct((t, ch), BF16), jax.ShapeDtypeStruct((t, ch), BF16),
                   jax.ShapeDtypeStruct(conv_w.shape, F32), jax.ShapeDtypeStruct((SUBLANES, ch), F32)],
        scratch=[pltpu.VMEM((CONV_HALO + tb, ch), F32), pltpu.VMEM((ext, ch), F32),
                 pltpu.VMEM((SUBLANES, CONV_HALO + tb, ch), F32), pltpu.VMEM((SUBLANES, ext, ch), F32)],
        sem=("arbitrary",), name="conv_bwd")


def ada_fwd(c_t, w_ada, dep=None):
    d, nc = w_ada.shape
    nex = c_t.shape[1]
    tn = _tile(nc, TILE_N)

    def body(ct_ref, w_ref, o_ref):
        w = w_ref[...]
        ct = ct_ref[...]
        cact = ct * _sigmoid(ct)
        rows = [jnp.sum(w * cact[:, b:b + 1], axis=0, keepdims=True) for b in range(nex)]
        o_ref[...] = jnp.concatenate(rows, axis=0)

    return _pallas(
        body, [c_t, w_ada], dep=dep, grid=(nc // tn,),
        in_specs=[_full_spec(c_t.shape), pl.BlockSpec((d, tn), lambda j: (0, j))],
        out_specs=pl.BlockSpec((nex, tn), lambda j: (0, j)),
        out_shape=jax.ShapeDtypeStruct((nex, nc), F32),
        sem=("parallel",), name="ada_fwd")


def _adamw_math(w, g, m, v):
    m = ADAM_B1 * m + (1.0 - ADAM_B1) * g
    v = ADAM_B2 * v + (1.0 - ADAM_B2) * (g * g)
    m_hat = m / (1.0 - ADAM_B1 ** ADAM_STEP)
    v_hat = v / (1.0 - ADAM_B2 ** ADAM_STEP)
    delta = -ADAM_LR * (m_hat / (jnp.sqrt(v_hat) + ADAM_EPS) + ADAM_WD * w)
    return delta, m, v


def adamw(w, g, m, v, name, copy_grad=False, dep=None):
    r, n = w.shape
    tr, tn = _ew_tiles(r, n, elems=EW_ELEMS_MANY)
    n_out = 4 if copy_grad else 3

    def body(w_ref, g_ref, m_ref, v_ref, *outs):
        g = g_ref[...]
        if copy_grad:
            outs[0][...] = g
        outs[-3][...], outs[-2][...], outs[-1][...] = _adamw_math(w_ref[...], g, m_ref[...], v_ref[...])

    blk = pl.BlockSpec((tr, tn), lambda i, j: (i, j))
    return _pallas(
        body, [w, g, m, v], dep=dep, grid=(r // tr, n // tn),
        in_specs=[blk] * 4, out_specs=[blk] * n_out,
        out_shape=[jax.ShapeDtypeStruct((r, n), F32)] * n_out,
        sem=("parallel", "parallel"), name=name)


def ada_grad_adamw(c_t, dmod_cols, w, m, v, dep=None):
    d, nc = w.shape
    nex = c_t.shape[1]
    tr, tn = _ew_tiles(d, nc, elems=EW_ELEMS_MANY)

    def body(ct_ref, dm_ref, w_ref, m_ref, v_ref, g_ref, d_ref, nm_ref, nv_ref):
        ct = ct_ref[...]
        cact = ct * _sigmoid(ct)
        dm = dm_ref[...]
        g = cact[:, 0:1] * dm[0:1, :]
        for b in range(1, nex):
            g = g + cact[:, b:b + 1] * dm[b:b + 1, :]
        g_ref[...] = g
        d_ref[...], nm_ref[...], nv_ref[...] = _adamw_math(w_ref[...], g, m_ref[...], v_ref[...])

    blk = pl.BlockSpec((tr, tn), lambda i, j: (i, j))
    return _pallas(
        body, [c_t, dmod_cols, w, m, v], dep=dep, grid=(d // tr, nc // tn),
        in_specs=[pl.BlockSpec((tr, nex), lambda i, j: (i, 0)), pl.BlockSpec((nex, tn), lambda i, j: (0, j)),
                  blk, blk, blk],
        out_specs=[blk] * 4,
        out_shape=[jax.ShapeDtypeStruct((d, nc), F32)] * 4,
        sem=("parallel", "parallel"), name="ada_grad_adamw")


def _row_pack(parts):
    cols, offs, off = [], [], 0
    for p in parts:
        n = p.shape[1]
        width = -(-n // LANES) * LANES
        cols.append(jnp.pad(p, ((0, 0), (0, width - n))) if width != n else p)
        offs.append(off)
        off += width
    return jnp.concatenate(cols, axis=1), offs


def small_sum_adamw(gathered, offs, ws, ms, vs, extra_widths, dep=None):
    ndev = gathered.shape[0]
    npar = len(ws)

    def body(ga_ref, *refs):
        w_refs, m_refs, v_refs = refs[:npar], refs[npar:2 * npar], refs[2 * npar:3 * npar]
        outs = refs[3 * npar:]
        tot = ga_ref[0]
        for s in range(1, ndev):
            tot = tot + ga_ref[s]
        for i in range(npar):
            n = ws[i].shape[1]
            g = tot[:, offs[i]:offs[i] + n]
            outs[4 * i][...] = g
            outs[4 * i + 1][...], outs[4 * i + 2][...], outs[4 * i + 3][...] = _adamw_math(
                w_refs[i][...], g, m_refs[i][...], v_refs[i][...])
        for e, n in enumerate(extra_widths):
            off = offs[npar + e]
            outs[4 * npar + e][...] = tot[:, off:off + n]

    shapes = [jax.ShapeDtypeStruct(w.shape, F32) for w in ws for _ in range(4)]
    shapes += [jax.ShapeDtypeStruct((1, n), F32) for n in extra_widths]
    return _pallas(
        body, [gathered, *ws, *ms, *vs], dep=dep, in_specs=[_VMEM] * (1 + 3 * npar), out_specs=[_VMEM] * len(shapes),
        out_shape=shapes, name="small_sum_adamw")


def _position():
    return lax.axis_index("x"), lax.axis_index("y"), lax.axis_index("c")


def _other_chips(x, y):
    return [(1 - x, y), (x, 1 - y), (1 - x, 1 - y)]


def allgather_small(block, name, dep=None):
    def body(x_ref, out_ref, send_sems, recv_sems, local_sem):
        x, y, c = _position()
        me, sibling = (x, y, c), (x, y, 1 - c)
        chips = _other_chips(x, y)

        def slot(px, py, pc):
            return out_ref.at[4 * px + 2 * py + pc]

        def copy(k, block_of, to, src=None):
            return pltpu.make_async_remote_copy(
                src_ref=slot(*block_of) if src is None else src, dst_ref=slot(*block_of),
                send_sem=send_sems.at[k], recv_sem=recv_sems.at[k], device_id=to, device_id_type=MESH)

        mine = pltpu.make_async_copy(x_ref, slot(*me), local_sem)
        mine.start()
        first = [copy(0, me, sibling, src=x_ref)]
        first += [copy(1 + j, me, (*chip, c), src=x_ref) for j, chip in enumerate(chips)]
        for cp in first:
            cp.start()
        passed = [copy(4 + j, (*chip, c), sibling) for j, chip in enumerate(chips)]
        for j, chip in enumerate(chips):
            copy(1 + j, (*chip, c), me).wait_recv()
            passed[j].start()
        copy(0, sibling, me).wait_recv()
        for j, chip in enumerate(chips):
            copy(4 + j, (*chip, 1 - c), me).wait_recv()
        for cp in first + passed:
            cp.wait_send()
        mine.wait()

    return _pallas(
        body, [block], dep=dep,
        out_shape=jax.ShapeDtypeStruct((N_DEV, *block.shape), block.dtype),
        in_specs=[_VMEM], out_specs=_VMEM,
        scratch=[pltpu.SemaphoreType.DMA((7,)), pltpu.SemaphoreType.DMA((7,)), pltpu.SemaphoreType.DMA],
        name=name)


class Started(NamedTuple):
    send_sems: Any
    recv_sems: Any
    bufs: list


def exchange_start_many(name, buf_sets, plans, dep=None):
    sizes = [len(bufs) for bufs in buf_sets]
    first = [sum(sizes[:i]) for i in range(len(sizes))]
    flat = [b for bufs in buf_sets for b in bufs]
    nb, npl = len(flat), len(plans)

    def body(*refs):
        for i, (s, _, plan) in enumerate(plans):
            for cp in plan(refs[first[s]:first[s] + sizes[s]], refs[nb + 2 * i], refs[nb + 2 * i + 1]):
                cp.start()

    sems = [pltpu.SemaphoreType.DMA((n,)) for _, n, _ in plans for _ in range(2)]
    outs = _pallas(
        body, [pltpu.with_memory_space_constraint(b, pltpu.HBM) for b in flat], dep=dep, name=name,
        out_shape=(*sems, *[pltpu.HBM(b.shape, b.dtype) for b in flat]),
        in_specs=[_HBM] * nb,
        out_specs=(*[_SEM] * (2 * npl), *[_HBM] * nb),
        input_output_aliases={i: 2 * npl + i for i in range(nb)},
        compiler_params=pltpu.CompilerParams(has_side_effects=_EFFECT))
    new_bufs = outs[2 * npl:]
    return [Started(outs[2 * i], outs[2 * i + 1], list(new_bufs[first[s]:first[s] + sizes[s]]))
            for i, (s, _, _) in enumerate(plans)]


def exchange_start(name, bufs, n_copies, plan, dep=None):
    return exchange_start_many(name, [bufs], [(0, n_copies, plan)], dep=dep)[0]


def exchange_wait(name, started, plan, bufs=None, dep=None):
    if bufs is not None:
        started = started._replace(bufs=list(bufs))
    nb = len(started.bufs)

    def body(*refs):
        for cp in plan(refs[:nb], refs[nb], refs[nb + 1]):
            cp.wait_send()
            cp.wait_recv()

    outs = _pallas(
        body, [*started.bufs, started.send_sems, started.recv_sems], dep=dep, name=name,
        out_shape=tuple(pltpu.HBM(b.shape, b.dtype) for b in started.bufs),
        in_specs=[_HBM] * nb + [_SEM, _SEM],
        out_specs=tuple([_HBM] * nb),
        input_output_aliases={i: i for i in range(nb)},
        compiler_params=pltpu.CompilerParams(has_side_effects=_EFFECT))
    return list(outs)


def _remote(src, dst, send_sems, recv_sems, i, to):
    return pltpu.make_async_remote_copy(src_ref=src, dst_ref=dst, send_sem=send_sems.at[i], recv_sem=recv_sems.at[i],
                                        device_id=to, device_id_type=MESH)


def _half_rows(buf_rows, chip_idx, pc):
    half = buf_rows // (2 * N_CHIPS)
    return pl.ds((2 * chip_idx + pc) * half, half)


ALL_PEERS = (0, 1, 2)


def plan_gather_ici(refs, send_sems, recv_sems, peers=ALL_PEERS):
    x, y, c = _position()
    chips = _other_chips(x, y)
    copies = []
    for k, ref in enumerate(refs):
        rows = ref.at[_half_rows(ref.shape[0], 2 * x + y, c), :]
        for i, j in enumerate(peers):
            copies.append(_remote(rows, rows, send_sems, recv_sems, len(peers) * k + i, (*chips[j], c)))
    return copies


def plan_gather_relay(refs, send_sems, recv_sems):
    x, y, c = _position()
    copies = []
    for k, ref in enumerate(refs):
        quarter = ref.shape[0] // (4 * N_CHIPS)
        for i, (src_chip, to) in enumerate((((1 - x, y), (x, 1 - y, c)), ((x, 1 - y), (1 - x, y, c)))):
            start = (2 * (2 * src_chip[0] + src_chip[1]) + c) * 2 * quarter + i * quarter
            rows = ref.at[pl.ds(start, quarter), :]
            copies.append(_remote(rows, rows, send_sems, recv_sems, 2 * k + i, to))
    return copies


def plan_gather_d2d(refs, send_sems, recv_sems, peers=ALL_PEERS):
    x, y, c = _position()
    chips = _other_chips(x, y)
    copies = []
    for k, ref in enumerate(refs):
        for i, j in enumerate(peers):
            px, py = chips[j]
            rows = ref.at[_half_rows(ref.shape[0], 2 * px + py, c), :]
            copies.append(_remote(rows, rows, send_sems, recv_sems, len(peers) * k + i, (x, y, 1 - c)))
    return copies


def plan_pair_exchange(refs, send_sems, recv_sems):
    x, y, c = _position()
    nw = len(refs) // 2
    copies = []
    for k in range(nw):
        for chip in range(N_CHIPS):
            copies.append(_remote(refs[k].at[chip, 1 - c], refs[nw + k].at[chip], send_sems, recv_sems,
                                  N_CHIPS * k + chip, (x, y, 1 - c)))
    return copies


def plan_chip_exchange(refs, send_sems, recv_sems):
    x, y, c = _position()
    nw = len(refs) // 2
    copies = []
    for k in range(nw):
        for j, (px, py) in enumerate(_other_chips(x, y)):
            copies.append(_remote(refs[k].at[2 * px + py], refs[nw + k].at[2 * x + y], send_sems, recv_sems,
                                  3 * k + j, (px, py, c)))
    return copies


def plan_pair_share(refs, send_sems, recv_sems):
    x, y, c = _position()
    return [_remote(ref.at[c], ref.at[c], send_sems, recv_sems, k, (x, y, 1 - c)) for k, ref in enumerate(refs)]


def cast_into_slot(src, slot, n_slots, name, dep=None):
    r, n = src.shape
    tr, tn = _ew_tiles(r, n, BF16_SUBLANES)

    def body(slot_ref, s_ref, o_ref):
        o_ref[...] = s_ref[...].astype(BF16)

    return _pallas(
        body, [slot, src], dep=dep, n_prefetch=1, grid=(r // tr, n // tn),
        in_specs=[pl.BlockSpec((tr, tn), lambda i, j, sl: (i, j))],
        out_specs=pl.BlockSpec((None, tr, tn), lambda i, j, sl: (sl[0], i, j)),
        out_shape=jax.ShapeDtypeStruct((n_slots, r, n), BF16),
        sem=("parallel", "parallel"), name=name)


def pair_sum(g, r, core, name, dep=None):
    nchip, _, h, n = g.shape
    th, tn = _ew_tiles(h, n, BF16_SUBLANES)

    def body(core_ref, g_ref, r_ref, o_ref):
        o_ref[...] = (g_ref[...].astype(F32) + r_ref[...].astype(F32)).astype(BF16)

    return _pallas(
        body, [core, g, r], dep=dep, n_prefetch=1, grid=(nchip, h // th, n // tn),
        in_specs=[pl.BlockSpec((None, None, th, tn), lambda a, i, j, cr: (a, cr[0], i, j)),
                  pl.BlockSpec((None, th, tn), lambda a, i, j, cr: (a, i, j))],
        out_specs=pl.BlockSpec((None, th, tn), lambda a, i, j, cr: (a, i, j)),
        out_shape=jax.ShapeDtypeStruct((nchip, h, n), BF16),
        sem=("parallel", "parallel", "parallel"), name=name)


def chip_sum(own, got, where, name, dep=None):
    nchip, h, n = got.shape
    th, tn = _ew_tiles(h, n, BF16_SUBLANES, elems=EW_ELEMS_MANY)

    def body(where_ref, own_ref, *rest):
        got_refs, o_ref = rest[:nchip], rest[nchip]
        chip = where_ref[0]
        acc = None
        for s in range(nchip):
            term = jnp.where(chip == s, own_ref[...], got_refs[s][...]).astype(F32)
            acc = term if acc is None else acc + term
        o_ref[...] = acc

    def got_spec(s):
        return pl.BlockSpec((None, th, tn), lambda i, j, wr: (jnp.where(wr[0] == s, (s + 1) % nchip, s), i, j))

    return _pallas(
        body, [where, own, *[got] * nchip], dep=dep, n_prefetch=1, grid=(h // th, n // tn),
        in_specs=[pl.BlockSpec((None, th, tn), lambda i, j, wr: (wr[0], i, j))]
        + [got_spec(s) for s in range(nchip)],
        out_specs=pl.BlockSpec((None, th, tn), lambda i, j, wr: (wr[1], i, j)),
        out_shape=jax.ShapeDtypeStruct((2, h, n), F32),
        sem=("parallel", "parallel"), name=name)


def kernel(x, c, w_ada, b_ada, norm_mix_g, w_in, q_norm_g, k_norm_g, attn_sinks, rel_bias, w_attn_out, conv_w, conv_b, conv_ln_g, conv_ln_b, w_conv_out, w_mix_out, norm_ffn_g, w_ffn_in, w_ffn_out, loss_target, m_w_ada, m_b_ada, m_norm_mix_g, m_w_in, m_q_norm_g, m_k_norm_g, m_attn_sinks, m_rel_bias, m_w_attn_out, m_conv_w, m_conv_b, m_conv_ln_g, m_conv_ln_b, m_w_conv_out, m_w_mix_out, m_norm_ffn_g, m_w_ffn_in, m_w_ffn_out, v_w_ada, v_b_ada, v_norm_mix_g, v_w_in, v_q_norm_g, v_k_norm_g, v_attn_sinks, v_rel_bias, v_w_attn_out, v_conv_w, v_conv_b, v_conv_ln_g, v_conv_ln_b, v_w_conv_out, v_w_mix_out, v_norm_ffn_g, v_w_ffn_in, v_w_ffn_out):
    run = InOrder()
    xi, yi, ci = _position()
    chip = 2 * xi + yi
    me = 2 * chip + ci
    chip_arr = chip.astype(jnp.int32).reshape(1)
    core_arr = ci.astype(jnp.int32).reshape(1)
    where_arr = jnp.stack([chip, ci]).astype(jnp.int32)

    xe, tgt = x[0], loss_target[0]
    t, d = xe.shape
    hd = q_norm_g.shape[-1]
    nq = attn_sinks.shape[-1]
    aw = nq * hd
    ch = conv_b.shape[-1]
    in_width = N_CHIPS * w_in.shape[-1]
    kvw = (in_width - aw - 2 * ch - 2 * d) // 2
    nkv = kvw // hd
    dff = N_CHIPS * w_ffn_out.shape[1]
    off_k, off_v, off_ca = aw, aw + kvw, aw + 2 * kvw
    off_cb, off_ga, off_gc = off_ca + ch, off_ca + 2 * ch, off_ca + 2 * ch + d
    nc_ada = w_ada.shape[-1]
    ch_loc = conv_w.shape[-1]
    nj_ffn = w_ffn_in.shape[-1]
    perm_ffn = ffn_perm(N_CHIPS)

    big = {"w_in": w_in[0], "w_attn_out": w_attn_out[0], "w_conv_out": w_conv_out[0], "w_mix_out": w_mix_out[0],
           "w_ffn_in": w_ffn_in[0], "w_ffn_out": w_ffn_out[0]}
    moments = {"w_in": (m_w_in, v_w_in), "w_attn_out": (m_w_attn_out, v_w_attn_out),
               "w_conv_out": (m_w_conv_out, v_w_conv_out), "w_mix_out": (m_w_mix_out, v_w_mix_out),
               "w_ffn_in": (m_w_ffn_in, v_w_ffn_in), "w_ffn_out": (m_w_ffn_out, v_w_ffn_out)}
    gather_groups = {"in": ["w_in"], "branch_out": ["w_attn_out", "w_conv_out"], "mix_out": ["w_mix_out"],
                     "ffn_in": ["w_ffn_in"], "ffn_out": ["w_ffn_out"]}
    grads, deltas, new_m, new_v = {}, {}, {}, {}

    def gather_cast(gname):
        bufs = []
        for n in gather_groups[gname]:
            r, ncol = big[n].shape
            bufs.append(run(cast_into_slot, big[n], chip_arr, N_CHIPS, "cast_" + n).reshape(N_CHIPS * r, ncol))
        return bufs

    def gather_pass_on(gname, ici):
        landed = run(exchange_wait, "gather_ici_wait_" + gname, ici, plan_gather_ici)
        return run(exchange_start, "gather_d2d_start_" + gname, landed, 3 * len(landed), plan_gather_d2d)

    def gathered(gname, d2d):
        outs = run(exchange_wait, "gather_d2d_wait_" + gname, d2d, plan_gather_d2d)
        return [o.reshape(N_CHIPS, *big[n].shape) for o, n in zip(outs, gather_groups[gname])]

    def rs_pair_start(gname, names, partials):
        blocks = [g.reshape(N_CHIPS, 2, big[n].shape[0] // 2, big[n].shape[1]) for n, g in zip(names, partials)]
        land = [lax.empty((N_CHIPS,) + b.shape[2:], BF16) for b in blocks]
        return run(exchange_start, "pair_exchange_start_" + gname, blocks + land, N_CHIPS * len(blocks),
                   plan_pair_exchange)

    def rs_chip_start(gname, names, pair):
        nw = len(names)
        outs = run(exchange_wait, "pair_exchange_wait_" + gname, pair, plan_pair_exchange)
        sums = [run(pair_sum, g, r, core_arr, "pair_sum_" + n) for n, g, r in zip(names, outs[:nw], outs[nw:])]
        land = [lax.empty(s.shape, BF16) for s in sums]
        return run(exchange_start, "chip_exchange_start_" + gname, sums + land, 3 * nw, plan_chip_exchange)

    def rs_share_start(gname, names, chipx):
        nw = len(names)
        outs = run(exchange_wait, "chip_exchange_wait_" + gname, chipx, plan_chip_exchange)
        halves = [run(chip_sum, s, r, where_arr, "chip_sum_" + n) for n, s, r in zip(names, outs[:nw], outs[nw:])]
        return run(exchange_start, "pair_share_start_" + gname, halves, nw, plan_pair_share)

    def rs_finish(gname, names, share):
        fulls = run(exchange_wait, "pair_share_wait_" + gname, share, plan_pair_share)
        for n, g2 in zip(names, fulls):
            g, dl, nm, nv = run(adamw, big[n], g2.reshape(big[n].shape), moments[n][0][0], moments[n][1][0],
                                "adamw_" + n, copy_grad=True)
            grads[n], deltas[n], new_m[n], new_v[n] = g[None], dl[None], nm[None], nv[None]

    near, far = (0, 1), (2,)
    plan_ici_near = functools.partial(plan_gather_ici, peers=near)
    plan_ici_far, n_far = plan_gather_relay, 2
    plan_d2d_near = functools.partial(plan_gather_d2d, peers=near)
    plan_d2d_far = functools.partial(plan_gather_d2d, peers=far)
    bufs_in = gather_cast("in")
    row1, offs1 = _row_pack([c, conv_w[0].reshape(1, CONV_WIDTH * ch_loc)])
    got1 = run(allgather_small, row1, "allgather_cond")
    ici_near = run(exchange_start, "gather_ici_start_in_near", bufs_in, len(near), plan_ici_near)
    c_all = got1[:, 0, :d]
    conv_w_full = got1[0::2, 0, offs1[1]:offs1[1] + CONV_WIDTH * ch_loc].reshape(N_CHIPS, CONV_WIDTH, ch_loc)
    conv_w_full = jnp.transpose(conv_w_full, (1, 0, 2)).reshape(CONV_WIDTH, ch)
    conv_w_pad = jnp.pad(conv_w_full, ((0, 1), (0, 0)))
    c_t = jnp.transpose(c_all)
    mod_cols = run(ada_fwd, c_t, w_ada[0])
    rest_bufs = {gname: gather_cast(gname) for gname in gather_groups if gname != "in"}
    bucket = _t5_bucket_table()
    bucket_p, bucket_c = jnp.asarray(bucket[:, :BLOCK]), jnp.asarray(bucket[:, BLOCK:])
    bias_p, bias_c = run(bias_table, rel_bias, bucket_p, bucket_c)
    got2 = run(allgather_small, mod_cols, "allgather_mod")
    mod_all = got2.reshape(N_CHIPS, 2, N_DEV, nc_ada)[:, 0]
    mod = lax.dynamic_slice_in_dim(mod_all, me, 1, axis=1).reshape(1, N_CHIPS * nc_ada) + b_ada
    mod = jnp.pad(mod.reshape(N_MOD, d), ((0, SUBLANES - N_MOD), (0, 0)))

    landed = run(exchange_wait, "gather_ici_wait_in_near", ici_near, plan_ici_near)
    ici_far, d2d_near = run(exchange_start_many, "gather_start_in_far", [landed],
                            [(0, n_far, plan_ici_far), (0, len(near), plan_d2d_near)])
    h = run(pre_mix_fwd, xe, mod, norm_mix_g)
    ici = {}
    ici["branch_out"], ici_near_ffn, ici["mix_out"] = run(
        exchange_start_many, "gather_ici_start_mid", [rest_bufs["branch_out"], rest_bufs["ffn_in"], rest_bufs["mix_out"]],
        [(0, 3 * len(rest_bufs["branch_out"]), plan_gather_ici), (1, len(near), plan_ici_near), (2, 3, plan_gather_ici)])
    landed = run(exchange_wait, "gather_d2d_wait_in_near", d2d_near, plan_d2d_near)
    landed = run(exchange_wait, "gather_ici_wait_in_far", ici_far, plan_ici_far, bufs=landed)
    d2d_far = run(exchange_start, "gather_d2d_start_in_far", landed, len(far), plan_d2d_far)
    landed = run(exchange_wait, "gather_d2d_wait_in_far", d2d_far, plan_d2d_far)
    wg_in = landed[0].reshape(N_CHIPS, *big["w_in"].shape)
    p = run(mm_nn, h, wg_in, tn=wg_in.shape[2], tk=d, out_dtype=BF16, name="mm_in")
    d2d_branch = gather_pass_on("branch_out", ici["branch_out"])

    sinks3 = attn_sinks.reshape(nq, 1, 1)
    attn_o = run(attn_fwd, p, bias_p, bias_c, sinks3, q_norm_g, k_norm_g, aw=aw, kvw=kvw)
    ca, cb = p[:, off_ca:off_cb], p[:, off_cb:off_ga]
    s_conv, co_conv = run(conv_fwd, ca, cb, conv_w_pad, conv_b, conv_ln_g, conv_ln_b)
    wg_attn_out, wg_conv_out = gathered("branch_out", d2d_branch)
    landed = run(exchange_wait, "gather_ici_wait_ffn_in_near", ici_near_ffn, plan_ici_near)
    ici_far_ffn, ici["ffn_out"] = run(
        exchange_start_many, "gather_start_ffn_in_far", [landed, rest_bufs["ffn_out"]],
        [(0, n_far, plan_ici_far), (1, 3, plan_gather_ici)])
    y_attn, y_conv, merged = run(branch_out_merge, attn_o, s_conv, wg_attn_out, wg_conv_out, p, off_ga, off_gc)
    landed_mix = run(exchange_wait, "gather_ici_wait_mix_out", ici["mix_out"], plan_gather_ici)
    d2d_mix, d2d_near_ffn = run(
        exchange_start_many, "gather_d2d_start_mix_ffn_in", [landed_mix, ici_far_ffn.bufs],
        [(0, 3 * len(landed_mix), plan_gather_d2d), (1, len(near), plan_d2d_near)])
    (wg_mix_out,) = gathered("mix_out", d2d_mix)
    wg_mix_out = wg_mix_out.reshape(1, d, d)
    o_m = run(mm_nn, merged, wg_mix_out, tn=_tile(d, TILE_N), tk=d, out_dtype=BF16, name="mm_mix_out")
    landed = run(exchange_wait, "gather_d2d_wait_ffn_in_near", d2d_near_ffn, plan_d2d_near)
    landed = run(exchange_wait, "gather_ici_wait_ffn_in_far", ici_far_ffn, plan_ici_far, bufs=landed)
    d2d_far_ffn = run(exchange_start, "gather_d2d_start_ffn_in_far", landed, len(far), plan_d2d_far)
    x1, h2 = run(pre_ffn_fwd, xe, o_m, mod, norm_ffn_g)
    landed = run(exchange_wait, "gather_d2d_wait_ffn_in_far", d2d_far_ffn, plan_d2d_far)
    wg_ffn_in = landed[0].reshape(N_CHIPS, *big["w_ffn_in"].shape)
    f = run(mm_nn, h2, wg_ffn_in, tn=_tile(nj_ffn, nj_ffn // 2), tk=d, out_dtype=BF16, name="mm_ffn_in", perm=perm_ffn)
    d2d_ffn_out = gather_pass_on("ffn_out", ici["ffn_out"])
    act = run(swiglu_fwd, f, nj_ffn)
    (wg_ffn_out,) = gathered("ffn_out", d2d_ffn_out)
    wg_ffn_out = wg_ffn_out.reshape(1, dff, d)
    o_f = run(mm_nn, act, wg_ffn_out, tn=_tile(d, TILE_N), tk=_tile(dff, dff // 2), out_dtype=BF16, name="mm_ffn_out")
    loss11, dy, dof, acc_l = run(loss_head, x1, o_f, tgt, mod)

    gw_ffn_out = run(mm_tn, act, dof, 1, tk=_tile(dff, TILE_N), tn=d, name="mm_ffn_out_dw")
    px_ffn_out = rs_pair_start("ffn_out", ["w_ffn_out"], [gw_ffn_out])
    dact = run(mm_nt, dof, wg_ffn_out, tko=_tile(dff, TILE_N), tn=d, out_dtype=BF16, name="mm_ffn_out_dx")
    cx_ffn_out = rs_chip_start("ffn_out", ["w_ffn_out"], px_ffn_out)
    df = run(swiglu_bwd, f, dact, nj_ffn)
    gw_ffn_in = run(mm_tn, h2, df, N_CHIPS, tk=d, tn=_tile(nj_ffn, nj_ffn // 2), name="mm_ffn_in_dw",
                    perm=perm_ffn)
    px_ffn_in = rs_pair_start("ffn_in", ["w_ffn_in"], [gw_ffn_in])
    dh2 = run(mm_nt, df, wg_ffn_in, tko=_tile(d, TILE_N), tn=nj_ffn, out_dtype=BF16, name="mm_ffn_in_dx", perm=perm_ffn)
    sh_ffn_out = rs_share_start("ffn_out", ["w_ffn_out"], cx_ffn_out)
    cx_ffn_in = rs_chip_start("ffn_in", ["w_ffn_in"], px_ffn_in)
    dx1, dom, acc_f = run(pre_ffn_bwd, x1, dh2, dy, o_m, mod, norm_ffn_g)
    gw_mix_out = run(mm_tn, merged, dom, 1, tk=d, tn=_tile(d, TILE_WIDE), name="mm_mix_out_dw")
    px_mix = rs_pair_start("mix_out", ["w_mix_out"], [gw_mix_out])
    dy_attn, dy_conv, dga, dgc = run(mix_out_dx_merge_bwd, dom, wg_mix_out.reshape(d, d), p, y_attn, y_conv,
                                     off_ga, off_gc)
    rs_finish("ffn_out", ["w_ffn_out"], sh_ffn_out)
    cx_mix = rs_chip_start("mix_out", ["w_mix_out"], px_mix)
    gw_attn_out = run(mm_tn, attn_o, dy_attn, N_CHIPS, tk=aw, tn=_tile(wg_attn_out.shape[2], TILE_N),
                      name="mm_attn_out_dw")
    gw_conv_out = run(mm_tn, s_conv, dy_conv, N_CHIPS, tk=ch, tn=_tile(wg_conv_out.shape[2], TILE_N),
                      name="mm_conv_out_dw")
    ac_names = ["w_attn_out", "w_conv_out"]
    px_ac = rs_pair_start("attn_conv_out", ac_names, [gw_attn_out, gw_conv_out])
    dattn_o = run(mm_nt, dy_attn, wg_attn_out, tko=_tile(aw, TILE_WIDE), tn=_tile(wg_attn_out.shape[2], TILE_N),
                  out_dtype=BF16, name="mm_attn_out_dx")
    ds_conv = run(mm_nt, dy_conv, wg_conv_out, tko=_tile(ch, TILE_WIDE), tn=_tile(wg_conv_out.shape[2], TILE_N),
                  out_dtype=BF16, name="mm_conv_out_dx")
    cx_ac = rs_chip_start("attn_conv_out", ac_names, px_ac)
    dca, dcb, dconv_w, dconv_vec = run(conv_bwd, ca, cb, co_conv, ds_conv, conv_w_pad, conv_ln_g, conv_ln_b)
    sh_ffn_in = rs_share_start("ffn_in", ["w_ffn_in"], cx_ffn_in)
    dqkv, dbp, dbc, dsinks, dqg, dkg = run(attn_bwd, p, bias_p, bias_c, sinks3, q_norm_g, k_norm_g, dattn_o,
                                           aw=aw, kvw=kvw)
    sh_mix = rs_share_start("mix_out", ["w_mix_out"], cx_mix)
    sh_ac = rs_share_start("attn_conv_out", ac_names, cx_ac)
    drel = run(bias_table_bwd, dbp, dbc, bucket_p, bucket_c).reshape(NUM_BUCKETS, nq)
    dp = jnp.concatenate([dqkv, dca, dcb, dga, dgc], axis=1)
    gw_in = run(mm_tn, h, dp, N_CHIPS, tk=d, tn=wg_in.shape[2], name="mm_in_dw")
    px_in = rs_pair_start("in", ["w_in"], [gw_in])
    dh = run(mm_nt, dp, wg_in, tko=_tile(d, TILE_WIDE), tn=wg_in.shape[2], out_dtype=BF16, name="mm_in_dx")
    grad_x, acc_m = run(pre_mix_bwd, xe, dh, dx1, mod, norm_mix_g)

    dmod = jnp.concatenate([acc_m[0:1], acc_m[1:2], acc_f[3:4], acc_f[0:1], acc_f[1:2], acc_l[0:1]], axis=1)
    small_names = ["b_ada", "norm_mix_g", "q_norm_g", "k_norm_g", "attn_sinks", "rel_bias", "conv_b", "conv_ln_g",
                   "conv_ln_b", "norm_ffn_g"]
    small_w = [b_ada, norm_mix_g, q_norm_g, k_norm_g, attn_sinks, rel_bias, conv_b, conv_ln_g, conv_ln_b, norm_ffn_g]
    small_m = [m_b_ada, m_norm_mix_g, m_q_norm_g, m_k_norm_g, m_attn_sinks, m_rel_bias, m_conv_b, m_conv_ln_g,
               m_conv_ln_b, m_norm_ffn_g]
    small_v = [v_b_ada, v_norm_mix_g, v_q_norm_g, v_k_norm_g, v_attn_sinks, v_rel_bias, v_conv_b, v_conv_ln_g,
               v_conv_ln_b, v_norm_ffn_g]
    small_g = [dmod, acc_m[2:3], dqg, dkg, dsinks.reshape(1, nq), drel.reshape(1, NUM_BUCKETS * nq),
               dconv_vec[0:1], dconv_vec[1:2], dconv_vec[2:3], acc_f[2:3]]
    row3, offs3 = _row_pack(small_g + [dconv_w[:CONV_WIDTH].reshape(1, CONV_WIDTH * ch), loss11])
    got3 = run(allgather_small, row3, "allgather_small_grads")
    cx_in = rs_chip_start("in", ["w_in"], px_in)
    as_row = lambda a: a.reshape(1, -1)
    outs3 = run(small_sum_adamw, got3, offs3, [as_row(a) for a in small_w], [as_row(a) for a in small_m],
                [as_row(a) for a in small_v], [CONV_WIDTH * ch, 1])
    for i, (n, w) in enumerate(zip(small_names, small_w)):
        grads[n], deltas[n], new_m[n], new_v[n] = (o.reshape(w.shape) for o in outs3[4 * i:4 * i + 4])
    g_conv_w_all, loss_sum = outs3[-2].reshape(CONV_WIDTH, ch), outs3[-1]

    g_conv_w = lax.dynamic_slice_in_dim(g_conv_w_all, chip * ch_loc, ch_loc, axis=1)
    grads["conv_w"] = g_conv_w[None]
    dl, nm, nv = run(adamw, conv_w[0], g_conv_w, m_conv_w[0], v_conv_w[0], "adamw_conv_w")
    deltas["conv_w"], new_m["conv_w"], new_v["conv_w"] = dl[None], nm[None], nv[None]

    dmod_all = got3[:, 0, :N_MOD * d]
    dmod_cols = lax.dynamic_slice_in_dim(dmod_all, chip * nc_ada, nc_ada, axis=1)
    g_ada, dl, nm, nv = run(ada_grad_adamw, c_t, dmod_cols, w_ada[0], m_w_ada[0], v_w_ada[0])
    grads["w_ada"], deltas["w_ada"], new_m["w_ada"], new_v["w_ada"] = g_ada[None], dl[None], nm[None], nv[None]

    rs_finish("ffn_in", ["w_ffn_in"], sh_ffn_in)
    rs_finish("mix_out", ["w_mix_out"], sh_mix)
    rs_finish("attn_conv_out", ac_names, sh_ac)
    sh_in = rs_share_start("in", ["w_in"], cx_in)
    rs_finish("in", ["w_in"], sh_in)

    loss = loss_sum[0, 0]
    order = ["w_ada", "b_ada", "norm_mix_g", "w_in", "q_norm_g", "k_norm_g", "attn_sinks", "rel_bias", "w_attn_out",
             "conv_w", "conv_b", "conv_ln_g", "conv_ln_b", "w_conv_out", "w_mix_out", "norm_ffn_g", "w_ffn_in",
             "w_ffn_out"]
    return (loss, grad_x[None], *[grads[n] for n in order], *[deltas[n] for n in order],
            *[new_m[n] for n in order], *[new_v[n] for n in order])
```

```python
import functools
import math
from typing import Any, NamedTuple

import jax
import jax.numpy as jnp
import numpy as np
from jax import lax
from jax.experimental import pallas as pl
from jax.experimental.pallas import tpu as pltpu

F32 = jnp.float32
BF16 = jnp.bfloat16
MESH = pl.DeviceIdType.MESH

V7X_VMEM_BYTES = 64 * 1024 * 1024
VMEM_LIMIT = V7X_VMEM_BYTES - 8 * 1024 * 1024
LANES = 128
SUBLANES = 8
BF16_SUBLANES = 16

EPS = 1e-6
WINDOW = 128
BLOCK = 128
NUM_BUCKETS = 32
MAX_EXACT = NUM_BUCKETS // 2
MAX_DISTANCE = 128
CONV_WIDTH = 31
CONV_HALO = 32
ADAM_LR = 0.001
ADAM_B1 = 0.9
ADAM_B2 = 0.999
ADAM_EPS = 1e-08
ADAM_WD = 0.01
ADAM_STEP = 10
N_MOD = 6
SH_M, SC_M, GT_M, SH_F, SC_F, GT_F = range(6)

N_CHIPS = 4
N_DEV = 8

_ANY = pl.BlockSpec(memory_space=pl.ANY)
_VMEM = pl.BlockSpec(memory_space=pltpu.VMEM)
_SMEM = pl.BlockSpec(memory_space=pltpu.SMEM)
_HBM = pl.BlockSpec(memory_space=pltpu.HBM)
_SEM = pl.BlockSpec(memory_space=pltpu.SEMAPHORE)
_EFFECT = pltpu.SideEffectType.DATAFLOW_SIDE_EFFECTING


class InOrder:
    def __init__(self):
        self.token = None

    def __call__(self, fn, *args, **kw):
        return fn(*args, dep=self, **kw)


def _pallas(body, args, *, in_specs, out_specs, out_shape, name, dep=None, grid=(), n_prefetch=0, scratch=(),
            sem=None, **kw):
    n_lead = n_prefetch + len(in_specs)
    in_specs, args = list(in_specs), list(args)
    single = not isinstance(out_shape, (list, tuple))
    out_shapes = [out_shape] if single else list(out_shape)
    out_specs = [out_specs] if single else list(out_specs)
    if dep is not None:
        inner, n_out, takes = body, len(out_shapes), dep.token is not None

        def body(*refs):
            rest = refs[n_lead + (1 if takes else 0):]
            rest[n_out][...] = jnp.zeros((SUBLANES, LANES), F32)
            return inner(*refs[:n_lead], *rest[:n_out], *rest[n_out + 1:])

        if takes:
            in_specs.append(_ANY)
            args.append(dep.token)
        out_shapes.append(jax.ShapeDtypeStruct((SUBLANES, LANES), F32))
        out_specs.append(pl.BlockSpec((SUBLANES, LANES), lambda *_: (0, 0)))
    params = kw.pop("compiler_params", None)
    if params is None:
        params = pltpu.CompilerParams(dimension_semantics=sem, vmem_limit_bytes=VMEM_LIMIT)
    outs = pl.pallas_call(
        body,
        grid_spec=pltpu.PrefetchScalarGridSpec(num_scalar_prefetch=n_prefetch, grid=grid, in_specs=in_specs,
                                               out_specs=out_specs, scratch_shapes=list(scratch)),
        out_shape=out_shapes, compiler_params=params, name=name, **kw,
    )(*args)
    if dep is not None:
        dep.token = outs[-1]
        outs = outs[:-1]
    return outs[0] if single else list(outs)


def _tile(n, pref, unit=LANES):
    best = None
    for t in range(unit, min(n, pref) + 1, unit):
        if n % t == 0:
            best = t
    return best if best is not None else n


def _sigmoid(v):
    return 1.0 / (1.0 + jnp.exp(-v.astype(F32)))


ROW_CHUNK = 1024
TILE_N = 512
TILE_WIDE = 1024
EW_ELEMS = 512 * 1024
EW_ELEMS_MANY = 384 * 1024


def _row_chunks(m, unit=SUBLANES):
    step = _tile(m, ROW_CHUNK, unit)
    return [(s, step) for s in range(0, m, step)]


def _ew_tiles(r, n, unit=SUBLANES, elems=EW_ELEMS):
    return _tile(r, max(unit, elems // n), unit), n


def _block_pos(j, perm):
    if perm is None:
        return j
    pos = 0
    for a, p in enumerate(perm):
        pos = pos + jnp.where(j == a, p, 0)
    return pos


def mm_nn(a, w, *, tn, tk, out_dtype, name, perm=None, dep=None):
    m, k = a.shape
    j, k2, nj = w.shape
    assert k == k2 and nj % tn == 0 and k % tk == 0
    npj, nk = nj // tn, k // tk

    def body(a_ref, w_ref, o_ref, *scratch):
        kk = pl.program_id(1)
        for s, sz in _row_chunks(m):
            rows = pl.ds(s, sz)
            p = jnp.dot(a_ref[rows, :], w_ref[...], preferred_element_type=F32)
            if nk == 1:
                o_ref[rows, :] = p.astype(out_dtype)
            else:
                acc = scratch[0]

                @pl.when(kk == 0)
                def _():
                    acc[rows, :] = p

                @pl.when(kk > 0)
                def _():
                    acc[rows, :] += p

                @pl.when(kk == nk - 1)
                def _():
                    o_ref[rows, :] = acc[rows, :].astype(out_dtype)

    return _pallas(
        body, [a, w], dep=dep, grid=(j * npj, nk),
        in_specs=[
            pl.BlockSpec((m, tk), lambda n, kk: (0, kk)),
            pl.BlockSpec((None, tk, tn), lambda n, kk: (n // npj, kk, n % npj)),
        ],
        out_specs=pl.BlockSpec((m, tn), lambda n, kk: (0, _block_pos(n // npj, perm) * npj + n % npj)),
        out_shape=jax.ShapeDtypeStruct((m, j * nj), out_dtype),
        scratch=[pltpu.VMEM((m, tn), F32)] if nk > 1 else [],
        sem=("parallel", "arbitrary"), name=name)


def mm_nt(g, w, *, tko, tn, name, out_dtype=F32, perm=None, dep=None):
    m, n = g.shape
    j, k, nj = w.shape
    assert n == j * nj and nj % tn == 0 and k % tko == 0
    npj, nr = nj // tn, n // tn
    in_place = out_dtype == F32

    def body(g_ref, w_ref, o_ref, *scratch):
        r = pl.program_id(1)
        acc = o_ref if in_place else (scratch[0] if nr > 1 else None)
        for s, sz in _row_chunks(m):
            rows = pl.ds(s, sz)
            p = lax.dot_general(g_ref[rows, :], w_ref[...], (((1,), (1,)), ((), ())), preferred_element_type=F32)
            if acc is None:
                o_ref[rows, :] = p.astype(out_dtype)
                continue

            @pl.when(r == 0)
            def _():
                acc[rows, :] = p

            @pl.when(r > 0)
            def _():
                acc[rows, :] += p

            if not in_place:
                @pl.when(r == nr - 1)
                def _():
                    o_ref[rows, :] = acc[rows, :].astype(out_dtype)

    return _pallas(
        body, [g, w], dep=dep, grid=(k // tko, nr),
        in_specs=[
            pl.BlockSpec((m, tn), lambda ko, r: (0, _block_pos(r // npj, perm) * npj + r % npj)),
            pl.BlockSpec((None, tko, tn), lambda ko, r: (r // npj, ko, r % npj)),
        ],
        out_specs=pl.BlockSpec((m, tko), lambda ko, r: (0, ko)),
        out_shape=jax.ShapeDtypeStruct((m, k), out_dtype),
        scratch=[pltpu.VMEM((m, tko), F32)] if (nr > 1 and not in_place) else [],
        sem=("parallel", "arbitrary"), name=name)


def mm_tn(a, g, n_blocks, *, tk, tn, name, perm=None, dep=None):
    m, k = a.shape
    m2, n = g.shape
    nj = n // n_blocks
    assert m == m2 and nj % tn == 0 and k % tk == 0
    npj = nj // tn

    def body(a_ref, g_ref, o_ref):
        for s, sz in _row_chunks(tk, LANES):
            p = lax.dot_general(a_ref[:, pl.ds(s, sz)], g_ref[...], (((0,), (0,)), ((), ())),
                                preferred_element_type=F32)
            o_ref[pl.ds(s, sz), :] = p.astype(BF16)

    return _pallas(
        body, [a, g], dep=dep, grid=(k // tk, n // tn),
        in_specs=[
            pl.BlockSpec((m, tk), lambda kk, nn: (0, kk)),
            pl.BlockSpec((m, tn), lambda kk, nn: (0, _block_pos(nn // npj, perm) * npj + nn % npj)),
        ],
        out_specs=pl.BlockSpec((None, tk, tn), lambda kk, nn: (nn // npj, kk, nn % npj)),
        out_shape=jax.ShapeDtypeStruct((n_blocks, k, nj), BF16),
        sem=("parallel", "parallel"), name=name)


ROW_TILE = 256


def _row_spec(tr, width):
    return pl.BlockSpec((tr, width), lambda i: (i, 0))


def _full_spec(shape):
    return pl.BlockSpec(shape, lambda *_: (0,) * len(shape))


def _rms(xv):
    return lax.rsqrt(jnp.mean(xv * xv, axis=-1, keepdims=True) + EPS)


def _mod_row(mod_ref, row):
    return mod_ref[pl.ds(row, 1), :]


def pre_mix_fwd(x, mod, gain, dep=None):
    t, d = x.shape
    tr = _tile(t, ROW_TILE, SUBLANES)

    def body(x_ref, mod_ref, g_ref, h_ref):
        xv = x_ref[...]
        y = xv * _rms(xv) * g_ref[...]
        h_ref[...] = (y * (1.0 + _mod_row(mod_ref, SC_M)) + _mod_row(mod_ref, SH_M)).astype(BF16)

    return _pallas(
        body, [x, mod, gain], dep=dep, grid=(t // tr,),
        in_specs=[_row_spec(tr, d), _full_spec(mod.shape), _full_spec(gain.shape)],
        out_specs=_row_spec(tr, d),
        out_shape=jax.ShapeDtypeStruct((t, d), BF16),
        sem=("parallel",), name="pre_mix_fwd")


def pre_ffn_fwd(x, o_m, mod, gain, dep=None):
    t, d = x.shape
    tr = _tile(t, ROW_TILE, SUBLANES)

    def body(x_ref, om_ref, mod_ref, g_ref, x1_ref, h_ref):
        x1 = x_ref[...] + _mod_row(mod_ref, GT_M) * om_ref[...]
        x1_ref[...] = x1
        y = x1 * _rms(x1) * g_ref[...]
        h_ref[...] = (y * (1.0 + _mod_row(mod_ref, SC_F)) + _mod_row(mod_ref, SH_F)).astype(BF16)

    return _pallas(
        body, [x, o_m, mod, gain], dep=dep, grid=(t // tr,),
        in_specs=[_row_spec(tr, d), _row_spec(tr, d), _full_spec(mod.shape), _full_spec(gain.shape)],
        out_specs=[_row_spec(tr, d), _row_spec(tr, d)],
        out_shape=[jax.ShapeDtypeStruct((t, d), F32), jax.ShapeDtypeStruct((t, d), BF16)],
        sem=("parallel",), name="pre_ffn_fwd")


def loss_head(x1, o_f, target, mod, dep=None):
    t, d = x1.shape
    tr = _tile(t, ROW_TILE, SUBLANES)

    def body(x1_ref, of_ref, tg_ref, mod_ref, loss_ref, dy_ref, dof_ref, acc_ref):
        i = pl.program_id(0)
        gt = _mod_row(mod_ref, GT_F)
        of = of_ref[...].astype(F32)
        err = x1_ref[...] + gt * of - tg_ref[...]
        dy = err * (1.0 / d)
        dy_ref[...] = dy.astype(BF16)
        dof_ref[...] = (dy * gt).astype(BF16)
        part = (0.5 / d) * jnp.sum(jnp.sum(err * err, axis=1, keepdims=True), axis=0, keepdims=True)
        dgt = jnp.sum(dy * of, axis=0, keepdims=True)

        @pl.when(i == 0)
        def _():
            loss_ref[...] = jnp.zeros_like(loss_ref)
            acc_ref[...] = jnp.zeros_like(acc_ref)

        loss_ref[...] += part
        acc_ref[pl.ds(0, 1), :] += dgt

    return _pallas(
        body, [x1, o_f, target, mod], dep=dep, grid=(t // tr,),
        in_specs=[_row_spec(tr, d), _row_spec(tr, d), _row_spec(tr, d), _full_spec(mod.shape)],
        out_specs=[_full_spec((1, 1)), _row_spec(tr, d), _row_spec(tr, d), _full_spec((SUBLANES, d))],
        out_shape=[jax.ShapeDtypeStruct((1, 1), F32), jax.ShapeDtypeStruct((t, d), BF16),
                   jax.ShapeDtypeStruct((t, d), BF16), jax.ShapeDtypeStruct((SUBLANES, d), F32)],
        sem=("arbitrary",), name="loss_head")


def _norm_bwd(xv, dh, sc, gain):
    rstd = _rms(xv)
    yn = xv * rstd
    dsh = jnp.sum(dh, axis=0, keepdims=True)
    dsc = jnp.sum(dh * (yn * gain), axis=0, keepdims=True)
    dgain = jnp.sum(dh * (1.0 + sc) * yn, axis=0, keepdims=True)
    dyn = dh * ((1.0 + sc) * gain)
    dx = rstd * (dyn - yn * jnp.mean(dyn * yn, axis=-1, keepdims=True))
    return dx, dsh, dsc, dgain


def pre_ffn_bwd(x1, dh2, dy, o_m, mod, gain, dep=None):
    t, d = x1.shape
    tr = _tile(t, ROW_TILE, SUBLANES)

    def body(x1_ref, dh_ref, dy_ref, om_ref, mod_ref, g_ref, dx1_ref, dom_ref, acc_ref):
        i = pl.program_id(0)
        dxn, dsh, dsc, dgain = _norm_bwd(x1_ref[...], dh_ref[...].astype(F32), _mod_row(mod_ref, SC_F), g_ref[...])
        dx1 = dy_ref[...] + dxn
        dx1_ref[...] = dx1
        dom_ref[...] = (dx1 * _mod_row(mod_ref, GT_M)).astype(BF16)
        dgt = jnp.sum(dx1 * om_ref[...], axis=0, keepdims=True)

        @pl.when(i == 0)
        def _():
            acc_ref[...] = jnp.zeros_like(acc_ref)

        acc_ref[pl.ds(0, 1), :] += dsh
        acc_ref[pl.ds(1, 1), :] += dsc
        acc_ref[pl.ds(2, 1), :] += dgain
        acc_ref[pl.ds(3, 1), :] += dgt

    return _pallas(
        body, [x1, dh2, dy, o_m, mod, gain], dep=dep, grid=(t // tr,),
        in_specs=[_row_spec(tr, d)] * 4 + [_full_spec(mod.shape), _full_spec(gain.shape)],
        out_specs=[_row_spec(tr, d), _row_spec(tr, d), _full_spec((SUBLANES, d))],
        out_shape=[jax.ShapeDtypeStruct((t, d), F32), jax.ShapeDtypeStruct((t, d), BF16),
                   jax.ShapeDtypeStruct((SUBLANES, d), F32)],
        sem=("arbitrary",), name="pre_ffn_bwd")


def pre_mix_bwd(x, dh, dx1, mod, gain, dep=None):
    t, d = x.shape
    tr = _tile(t, ROW_TILE, SUBLANES)

    def body(x_ref, dh_ref, dx1_ref, mod_ref, g_ref, gx_ref, acc_ref):
        i = pl.program_id(0)
        dxn, dsh, dsc, dgain = _norm_bwd(x_ref[...], dh_ref[...].astype(F32), _mod_row(mod_ref, SC_M), g_ref[...])
        gx_ref[...] = dx1_ref[...] + dxn

        @pl.when(i == 0)
        def _():
            acc_ref[...] = jnp.zeros_like(acc_ref)

        acc_ref[pl.ds(0, 1), :] += dsh
        acc_ref[pl.ds(1, 1), :] += dsc
        acc_ref[pl.ds(2, 1), :] += dgain

    return _pallas(
        body, [x, dh, dx1, mod, gain], dep=dep, grid=(t // tr,),
        in_specs=[_row_spec(tr, d)] * 3 + [_full_spec(mod.shape), _full_spec(gain.shape)],
        out_specs=[_row_spec(tr, d), _full_spec((SUBLANES, d))],
        out_shape=[jax.ShapeDtypeStruct((t, d), F32), jax.ShapeDtypeStruct((SUBLANES, d), F32)],
        sem=("arbitrary",), name="pre_mix_bwd")


def branch_out_merge(attn_o, s_conv, w_attn, w_conv, p, off_ga, off_gc, dep=None):
    t = attn_o.shape[0]
    j, ka, nj = w_attn.shape
    kc = w_conv.shape[1]
    assert off_ga % nj == 0 and off_gc % nj == 0

    def body(a_ref, s_ref, wa_ref, wc_ref, ga_ref, gc_ref, ya_ref, yc_ref, m_ref):
        for s, sz in _row_chunks(t):
            rows = pl.ds(s, sz)
            ya = jnp.dot(a_ref[rows, :], wa_ref[...], preferred_element_type=F32)
            yc = jnp.dot(s_ref[rows, :], wc_ref[...], preferred_element_type=F32)
            ya_ref[rows, :] = ya.astype(BF16)
            yc_ref[rows, :] = yc.astype(BF16)
            m_ref[rows, :] = (_sigmoid(ga_ref[rows, :]) * ya + _sigmoid(gc_ref[rows, :]) * yc).astype(BF16)

    col = pl.BlockSpec((t, nj), lambda b: (0, b))
    return _pallas(
        body, [attn_o, s_conv, w_attn, w_conv, p, p], dep=dep, grid=(j,),
        in_specs=[pl.BlockSpec((t, ka), lambda b: (0, 0)), pl.BlockSpec((t, kc), lambda b: (0, 0)),
                  pl.BlockSpec((None, ka, nj), lambda b: (b, 0, 0)), pl.BlockSpec((None, kc, nj), lambda b: (b, 0, 0)),
                  pl.BlockSpec((t, nj), lambda b: (0, off_ga // nj + b)),
                  pl.BlockSpec((t, nj), lambda b: (0, off_gc // nj + b))],
        out_specs=[col] * 3,
        out_shape=[jax.ShapeDtypeStruct((t, j * nj), BF16)] * 3,
        sem=("parallel",), name="branch_out_merge")


def mix_out_dx_merge_bwd(dom, w_mix, p, y_attn, y_conv, off_ga, off_gc, dep=None):
    t, d = y_attn.shape
    cw = math.gcd(math.gcd(off_ga, off_gc), math.gcd(d, TILE_N // 2))

    def body(dom_ref, w_ref, ga_ref, gc_ref, ya_ref, yc_ref, dya_ref, dyc_ref, dga_ref, dgc_ref):
        for s, sz in _row_chunks(t):
            rows = pl.ds(s, sz)
            dm = lax.dot_general(dom_ref[rows, :], w_ref[...], (((1,), (1,)), ((), ())), preferred_element_type=F32)
            sa = _sigmoid(ga_ref[rows, :])
            sc = _sigmoid(gc_ref[rows, :])
            dya_ref[rows, :] = (dm * sa).astype(BF16)
            dyc_ref[rows, :] = (dm * sc).astype(BF16)
            dga_ref[rows, :] = (dm * ya_ref[rows, :] * sa * (1.0 - sa)).astype(BF16)
            dgc_ref[rows, :] = (dm * yc_ref[rows, :] * sc * (1.0 - sc)).astype(BF16)

    col = pl.BlockSpec((t, cw), lambda j: (0, j))
    return _pallas(
        body, [dom, w_mix, p, p, y_attn, y_conv], dep=dep, grid=(d // cw,),
        in_specs=[pl.BlockSpec((t, d), lambda j: (0, 0)), pl.BlockSpec((cw, d), lambda j: (j, 0)),
                  pl.BlockSpec((t, cw), lambda j: (0, off_ga // cw + j)),
                  pl.BlockSpec((t, cw), lambda j: (0, off_gc // cw + j)), col, col],
        out_specs=[col] * 4,
        out_shape=[jax.ShapeDtypeStruct((t, d), BF16)] * 4,
        sem=("parallel",), name="mm_mix_out_dx_merge_bwd")


def ffn_perm(n_blocks):
    half = n_blocks // 2
    return tuple(2 * j if j < half else 2 * (j - half) + 1 for j in range(n_blocks))


def swiglu_fwd(f, nj, dep=None):
    t, two = f.shape
    tr = _tile(t, ROW_TILE, SUBLANES)
    npair = two // (2 * nj)

    def body(f_ref, o_ref):
        g = f_ref[:, :nj].astype(F32)
        u = f_ref[:, nj:].astype(F32)
        o_ref[...] = (g * _sigmoid(g) * u).astype(BF16)

    return _pallas(
        body, [f], dep=dep, grid=(t // tr, npair),
        in_specs=[pl.BlockSpec((tr, 2 * nj), lambda i, j: (i, j))],
        out_specs=pl.BlockSpec((tr, nj), lambda i, j: (i, j)),
        out_shape=jax.ShapeDtypeStruct((t, two // 2), BF16),
        sem=("parallel", "parallel"), name="swiglu_fwd")


def swiglu_bwd(f, dact, nj, dep=None):
    t, two = f.shape
    tr = _tile(t, ROW_TILE, SUBLANES)
    npair = two // (2 * nj)

    def body(f_ref, da_ref, o_ref):
        g = f_ref[:, :nj].astype(F32)
        u = f_ref[:, nj:].astype(F32)
        da = da_ref[...]
        s = _sigmoid(g)
        o_ref[:, :nj] = (da * u * (s * (1.0 + g * (1.0 - s)))).astype(BF16)
        o_ref[:, nj:] = (da * (g * s)).astype(BF16)

    return _pallas(
        body, [f, dact], dep=dep, grid=(t // tr, npair),
        in_specs=[pl.BlockSpec((tr, 2 * nj), lambda i, j: (i, j)), pl.BlockSpec((tr, nj), lambda i, j: (i, j))],
        out_specs=pl.BlockSpec((tr, 2 * nj), lambda i, j: (i, j)),
        out_shape=jax.ShapeDtypeStruct((t, two), BF16),
        sem=("parallel", "parallel"), name="swiglu_bwd")


def _t5_bucket_table():
    q_off = np.arange(BLOCK)
    k_off = np.arange(2 * BLOCK)
    dist = q_off[:, None] + BLOCK - k_off[None, :]
    n = np.maximum(dist, 0)
    nf = np.maximum(n, 1).astype(np.float32)
    large = MAX_EXACT + (np.log(nf / np.float32(MAX_EXACT)) / np.float32(math.log(MAX_DISTANCE / MAX_EXACT))
                         * np.float32(NUM_BUCKETS - MAX_EXACT)).astype(np.int32)
    large = np.minimum(large, NUM_BUCKETS - 1)
    bucket = np.where(n < MAX_EXACT, n, large).astype(np.int32)
    allowed = (dist >= 0) & (dist < WINDOW)
    return np.where(allowed, bucket, -1).astype(np.int32)


def bias_table(rel_bias, bucket_p, bucket_c, dep=None):
    nb, nq = rel_bias.shape

    def body(rb_ref, bkp_ref, bkc_ref, op_ref, oc_ref):
        for bk_ref, o_ref in ((bkp_ref, op_ref), (bkc_ref, oc_ref)):
            bk = bk_ref[...]
            for h in range(nq):
                acc = jnp.full(bk.shape, -jnp.inf, F32)
                for b in range(nb):
                    acc = jnp.where(bk == b, rb_ref[b, h], acc)
                o_ref[h] = acc

    return _pallas(
        body, [rel_bias, bucket_p, bucket_c], dep=dep,
        in_specs=[_SMEM, _VMEM, _VMEM], out_specs=[_VMEM, _VMEM],
        out_shape=[jax.ShapeDtypeStruct((nq,) + bucket_p.shape, F32)] * 2,
        name="bias_table")


def bias_table_bwd(dbp, dbc, bucket_p, bucket_c, dep=None):
    nq = dbp.shape[0]

    def body(dbp_ref, dbc_ref, bkp_ref, bkc_ref, o_ref):
        bkp, bkc = bkp_ref[...][None], bkc_ref[...][None]
        dp, dc = dbp_ref[...], dbc_ref[...]
        for b in range(NUM_BUCKETS):
            sel = jnp.where(bkp == b, dp, 0.0) + jnp.where(bkc == b, dc, 0.0)
            o_ref[b] = jnp.sum(jnp.sum(sel, axis=2, keepdims=True), axis=1, keepdims=True)

    return _pallas(
        body, [dbp, dbc, bucket_p, bucket_c], dep=dep,
        in_specs=[_VMEM] * 4, out_specs=_VMEM,
        out_shape=jax.ShapeDtypeStruct((NUM_BUCKETS, nq, 1, 1), F32),
        name="bias_table_bwd")


_BNT = (((2,), (2,)), ((0,), (0,)))
_BNN = (((2,), (1,)), ((0,), (0,)))
_BTN = (((1,), (1,)), ((0,), (0,)))


@jax.custom_vjp
def _bdot_nt(a, b):
    return lax.dot_general(a.astype(BF16), b.astype(BF16), _BNT, preferred_element_type=F32)


def _bdot_nt_fwd(a, b):
    return _bdot_nt(a, b), (a, b)


def _bdot_nt_bwd(res, g):
    a, b = res
    gb = g.astype(BF16)
    da = lax.dot_general(gb, b.astype(BF16), _BNN, preferred_element_type=F32)
    db = lax.dot_general(gb, a.astype(BF16), _BTN, preferred_element_type=F32)
    return da, db


_bdot_nt.defvjp(_bdot_nt_fwd, _bdot_nt_bwd)


@jax.custom_vjp
def _bdot_nn(a, b):
    return lax.dot_general(a.astype(BF16), b.astype(BF16), _BNN, preferred_element_type=F32)


def _bdot_nn_fwd(a, b):
    return _bdot_nn(a, b), (a, b)


def _bdot_nn_bwd(res, g):
    a, b = res
    gb = g.astype(BF16)
    da = lax.dot_general(gb, b.astype(BF16), _BNT, preferred_element_type=F32)
    db = lax.dot_general(a.astype(BF16), gb, _BTN, preferred_element_type=F32)
    return da, db


_bdot_nn.defvjp(_bdot_nn_fwd, _bdot_nn_bwd)


def _attn_math(q, kp, kc, vp, vc, bp, bc, sinks, qg, kg, *, prev_ok, scale):
    h, rows, _ = q.shape
    b = kp.shape[1]
    qn = q * _rms(q) * qg
    kpn = kp * _rms(kp) * kg
    kcn = kc * _rms(kc) * kg
    lp = _bdot_nt(qn, kpn) * scale + bp.reshape(h, rows, b)
    lc = _bdot_nt(qn, kcn) * scale + bc.reshape(h, rows, b)
    lp = jnp.where(prev_ok, lp, -jnp.inf)
    sink = jnp.broadcast_to(sinks, (sinks.shape[0], b, 1)).reshape(h, rows, 1)
    m = jnp.maximum(jnp.maximum(jnp.max(lp, axis=-1, keepdims=True), jnp.max(lc, axis=-1, keepdims=True)), sink)
    m = lax.stop_gradient(m)
    pp = jnp.exp(lp - m)
    pc = jnp.exp(lc - m)
    den = jnp.sum(pp, axis=-1, keepdims=True) + jnp.sum(pc, axis=-1, keepdims=True) + jnp.exp(sink - m)
    inv = 1.0 / den
    return _bdot_nn(pp * inv, vp) + _bdot_nn(pc * inv, vc)


def _attn_specs(p, aw, kvw, nq, hd, nblk, reverse):
    assert aw % (2 * kvw) == 0
    kv_col = aw // (2 * kvw)

    def blk(n):
        return nblk - 1 - n if reverse else n

    return [
        pl.BlockSpec((BLOCK, aw), lambda n: (blk(n), 0)),
        pl.BlockSpec((BLOCK, 2 * kvw), lambda n: (jnp.maximum(blk(n) - 1, 0), kv_col)),
        pl.BlockSpec((BLOCK, 2 * kvw), lambda n: (blk(n), kv_col)),
        _full_spec((nq, BLOCK, BLOCK)), _full_spec((nq, BLOCK, BLOCK)), _full_spec((nq, 1, 1)),
        _full_spec((1, hd)), _full_spec((1, hd)),
    ]


def _head_major(ref, n_heads, grp, hd, offset=0):
    return jnp.stack([
        jnp.concatenate([ref[:, pl.ds(offset + (grp * h + g) * hd, hd)].astype(F32) for g in range(grp)], axis=0)
        for h in range(n_heads)])


def _attn_inputs(nkv, grp, hd, kvw, q_ref, kvp_ref, kvc_ref):
    return (_head_major(q_ref, nkv, grp, hd), _head_major(kvp_ref, nkv, 1, hd), _head_major(kvc_ref, nkv, 1, hd),
            _head_major(kvp_ref, nkv, 1, hd, kvw), _head_major(kvc_ref, nkv, 1, hd, kvw))


def attn_fwd(p, bias_p, bias_c, sinks, qg, kg, *, aw, kvw, dep=None):
    t, hd = p.shape[0], qg.shape[-1]
    nq, nkv, nblk = aw // hd, kvw // hd, t // BLOCK
    grp = nq // nkv
    scale = hd ** -0.5

    def body(q_ref, kvp_ref, kvc_ref, bp_ref, bc_ref, s_ref, qg_ref, kg_ref, o_ref):
        prev_ok = pl.program_id(0) > 0
        out = _attn_math(*_attn_inputs(nkv, grp, hd, kvw, q_ref, kvp_ref, kvc_ref), bp_ref[...], bc_ref[...],
                         s_ref[...], qg_ref[...], kg_ref[...], prev_ok=prev_ok, scale=scale)
        for h in range(nkv):
            for g in range(grp):
                o_ref[:, pl.ds((grp * h + g) * hd, hd)] = out[h, g * BLOCK:(g + 1) * BLOCK].astype(BF16)

    return _pallas(
        body, [p, p, p, bias_p, bias_c, sinks, qg, kg], dep=dep, grid=(nblk,),
        in_specs=_attn_specs(p, aw, kvw, nq, hd, nblk, False),
        out_specs=pl.BlockSpec((BLOCK, aw), lambda n: (n, 0)),
        out_shape=jax.ShapeDtypeStruct((t, aw), BF16),
        sem=("parallel",), name="attn_fwd")


def attn_bwd(p, bias_p, bias_c, sinks, qg, kg, do, *, aw, kvw, dep=None):
    t, hd = p.shape[0], qg.shape[-1]
    nq, nkv, nblk = aw // hd, kvw // hd, t // BLOCK
    grp = nq // nkv
    scale = hd ** -0.5

    def body(q_ref, kvp_ref, kvc_ref, bp_ref, bc_ref, s_ref, qg_ref, kg_ref, do_ref,
             dqkv_ref, dbp_ref, dbc_ref, ds_ref, dqg_ref, dkg_ref, carry):
        i = pl.program_id(0)
        prev_ok = (nblk - 1 - i) > 0

        @pl.when(i == 0)
        def _():
            carry[...] = jnp.zeros_like(carry)
            dbp_ref[...] = jnp.zeros_like(dbp_ref)
            dbc_ref[...] = jnp.zeros_like(dbc_ref)
            ds_ref[...] = jnp.zeros_like(ds_ref)
            dqg_ref[...] = jnp.zeros_like(dqg_ref)
            dkg_ref[...] = jnp.zeros_like(dkg_ref)

        fn = functools.partial(_attn_math, prev_ok=prev_ok, scale=scale)
        _, vjp = jax.vjp(fn, *_attn_inputs(nkv, grp, hd, kvw, q_ref, kvp_ref, kvc_ref), bp_ref[...], bc_ref[...],
                         s_ref[...], qg_ref[...], kg_ref[...])
        dq, dkp, dkc, dvp, dvc, dbp, dbc, dsk, dqg, dkg = vjp(_head_major(do_ref, nkv, grp, hd))
        for h in range(nkv):
            for g in range(grp):
                dqkv_ref[:, pl.ds((grp * h + g) * hd, hd)] = dq[h, g * BLOCK:(g + 1) * BLOCK].astype(BF16)
            k_cols, v_cols = pl.ds(h * hd, hd), pl.ds(kvw + h * hd, hd)
            dqkv_ref[:, pl.ds(aw + h * hd, hd)] = (dkc[h] + carry[:, k_cols]).astype(BF16)
            dqkv_ref[:, pl.ds(aw + kvw + h * hd, hd)] = (dvc[h] + carry[:, v_cols]).astype(BF16)
            carry[:, k_cols] = dkp[h]
            carry[:, v_cols] = dvp[h]
        dbp_ref[...] += dbp
        dbc_ref[...] += dbc
        ds_ref[...] += dsk
        dqg_ref[...] += dqg
        dkg_ref[...] += dkg

    return _pallas(
        body, [p, p, p, bias_p, bias_c, sinks, qg, kg, do], dep=dep, grid=(nblk,),
        in_specs=_attn_specs(p, aw, kvw, nq, hd, nblk, True)
        + [pl.BlockSpec((BLOCK, aw), lambda n: (nblk - 1 - n, 0))],
        out_specs=[
            pl.BlockSpec((BLOCK, aw + 2 * kvw), lambda n: (nblk - 1 - n, 0)),
            _full_spec((nq, BLOCK, BLOCK)), _full_spec((nq, BLOCK, BLOCK)), _full_spec((nq, 1, 1)),
            _full_spec((1, hd)), _full_spec((1, hd)),
        ],
        out_shape=[
            jax.ShapeDtypeStruct((t, aw + 2 * kvw), BF16),
            jax.ShapeDtypeStruct((nq, BLOCK, BLOCK), F32),
            jax.ShapeDtypeStruct((nq, BLOCK, BLOCK), F32),
            jax.ShapeDtypeStruct((nq, 1, 1), F32),
            jax.ShapeDtypeStruct((1, hd), F32),
            jax.ShapeDtypeStruct((1, hd), F32),
        ],
        scratch=[pltpu.VMEM((BLOCK, 2 * kvw), F32)],
        sem=("arbitrary",), name="attn_bwd")


CONV_TILE = 256


def _conv_halo_specs(tb, ch, nblk):
    per = tb // CONV_HALO
    last = nblk * per - 1
    cur = pl.BlockSpec((tb, ch), lambda n: (n, 0))
    prev = pl.BlockSpec((CONV_HALO, ch), lambda n: (jnp.maximum(n * per - 1, 0), 0))
    nxt = pl.BlockSpec((CONV_HALO, ch), lambda n: (jnp.minimum((n + 1) * per, last), 0))
    return cur, prev, nxt


def _ln_silu(co, ln_g, ln_b):
    mu = jnp.mean(co, axis=-1, keepdims=True)
    cen = co - mu
    rstd = lax.rsqrt(jnp.mean(cen * cen, axis=-1, keepdims=True) + EPS)
    xhat = cen * rstd
    z = xhat * ln_g + ln_b
    return xhat, rstd, z


def _shifted_copies(src, shifted):
    rows = src.shape[0] - SUBLANES
    for r in range(1, SUBLANES):
        shifted[r, pl.ds(0, rows), :] = src[pl.ds(r, rows), :]


def _rows_from(src, shifted, start, n):
    r = start % SUBLANES
    if r == 0:
        return src[pl.ds(start, n), :]
    return shifted[r, pl.ds(start - r, n), :]


def conv_fwd(ca, cb, conv_w, conv_b, ln_g, ln_b, dep=None):
    t, ch = ca.shape
    tb = _tile(t, CONV_TILE, CONV_HALO)
    nblk = t // tb
    cur, prev, _ = _conv_halo_specs(tb, ch, nblk)
    lead = CONV_HALO - (CONV_WIDTH - 1)

    def body(ca_ref, cb_ref, cap_ref, cbp_ref, w_ref, b_ref, g_ref, bb_ref, s_ref, co_ref, ubuf, ushift):
        n = pl.program_id(0)
        halo = cap_ref[...] * _sigmoid(cbp_ref[...])
        ubuf[pl.ds(0, CONV_HALO), :] = jnp.where(n > 0, halo, 0.0)
        ubuf[pl.ds(CONV_HALO, tb), :] = ca_ref[...] * _sigmoid(cb_ref[...])
        _shifted_copies(ubuf, ushift)
        acc = jnp.broadcast_to(b_ref[...], (tb, ch))
        for k in range(CONV_WIDTH):
            acc = acc + w_ref[pl.ds(k, 1), :] * _rows_from(ubuf, ushift, lead + k, tb)
        co_ref[...] = acc
        _, _, z = _ln_silu(acc, g_ref[...], bb_ref[...])
        s_ref[...] = (z * _sigmoid(z)).astype(BF16)

    vec = _full_spec((1, ch))
    return _pallas(
        body, [ca, cb, ca, cb, conv_w, conv_b, ln_g, ln_b], dep=dep, grid=(nblk,),
        in_specs=[cur, cur, prev, prev, _full_spec(conv_w.shape), vec, vec, vec],
        out_specs=[cur, cur],
        out_shape=[jax.ShapeDtypeStruct((t, ch), BF16), jax.ShapeDtypeStruct((t, ch), F32)],
        scratch=[pltpu.VMEM((CONV_HALO + tb, ch), F32), pltpu.VMEM((SUBLANES, CONV_HALO + tb, ch), F32)],
        sem=("parallel",), name="conv_fwd")


def conv_bwd(ca, cb, co, ds, conv_w, ln_g, ln_b, dep=None):
    t, ch = ca.shape
    tb = _tile(t, CONV_TILE, CONV_HALO)
    nblk = t // tb
    cur, prev, nxt = _conv_halo_specs(tb, ch, nblk)
    lead = CONV_HALO - (CONV_WIDTH - 1)
    ext = tb + CONV_HALO

    def body(ca_ref, cb_ref, cap_ref, cbp_ref, co_ref, con_ref, ds_ref, dsn_ref, w_ref, g_ref, bb_ref,
             dca_ref, dcb_ref, dw_ref, dvec_ref, ubuf, dbuf, ushift, dshift):
        n = pl.program_id(0)
        is_last = n == nblk - 1
        sig_b = _sigmoid(cb_ref[...])
        cav = ca_ref[...].astype(F32)
        ubuf[pl.ds(0, CONV_HALO), :] = jnp.where(n > 0, cap_ref[...] * _sigmoid(cbp_ref[...]), 0.0)
        ubuf[pl.ds(CONV_HALO, tb), :] = cav * sig_b
        _shifted_copies(ubuf, ushift)
        co = jnp.concatenate([co_ref[...], con_ref[...]], axis=0)
        xhat, rstd, z = _ln_silu(co, g_ref[...], bb_ref[...])
        dsv = jnp.concatenate([ds_ref[...].astype(F32), jnp.where(is_last, 0.0, dsn_ref[...].astype(F32))], axis=0)
        sg = _sigmoid(z)
        dz = dsv * (sg * (1.0 + z * (1.0 - sg)))
        dxh = dz * g_ref[...]
        dco = rstd * (dxh - jnp.mean(dxh, axis=-1, keepdims=True)
                      - xhat * jnp.mean(dxh * xhat, axis=-1, keepdims=True))
        dbuf[...] = dco
        _shifted_copies(dbuf, dshift)

        @pl.when(n == 0)
        def _():
            dw_ref[...] = jnp.zeros_like(dw_ref)
            dvec_ref[...] = jnp.zeros_like(dvec_ref)

        dco_cur = dco[:tb]
        dvec_ref[pl.ds(0, 1), :] += jnp.sum(dco_cur, axis=0, keepdims=True)
        dvec_ref[pl.ds(1, 1), :] += jnp.sum(dz[:tb] * xhat[:tb], axis=0, keepdims=True)
        dvec_ref[pl.ds(2, 1), :] += jnp.sum(dz[:tb], axis=0, keepdims=True)
        du = jnp.zeros((tb, ch), F32)
        for k in range(CONV_WIDTH):
            du = du + w_ref[pl.ds(k, 1), :] * _rows_from(dbuf, dshift, CONV_WIDTH - 1 - k, tb)
            dw_ref[pl.ds(k, 1), :] += jnp.sum(dco_cur * _rows_from(ubuf, ushift, lead + k, tb), axis=0,
                                              keepdims=True)
        dca_ref[...] = (du * sig_b).astype(BF16)
        dcb_ref[...] = (du * cav * sig_b * (1.0 - sig_b)).astype(BF16)

    vec = _full_spec((1, ch))
    return _pallas(
        body, [ca, cb, ca, cb, co, co, ds, ds, conv_w, ln_g, ln_b], dep=dep, grid=(nblk,),
        in_specs=[cur, cur, prev, prev, cur, nxt, cur, nxt, _full_spec(conv_w.shape), vec, vec],
        out_specs=[cur, cur, _full_spec(conv_w.shape), _full_spec((SUBLANES, ch))],
        out_shape=[jax.ShapeDtypeStruct((t, ch), BF16), jax.ShapeDtypeStruct((t, ch), BF16),
                   jax.ShapeDtypeStruct(conv_w.shape, F32), jax.ShapeDtypeStruct((SUBLANES, ch), F32)],
        scratch=[pltpu.VMEM((CONV_HALO + tb, ch), F32), pltpu.VMEM((ext, ch), F32),
                 pltpu.VMEM((SUBLANES, CONV_HALO + tb, ch), F32), pltpu.VMEM((SUBLANES, ext, ch), F32)],
        sem=("arbitrary",), name="conv_bwd")


def ada_fwd(c_t, w_ada, dep=None):
    d, nc = w_ada.shape
    nex = c_t.shape[1]
    tn = _tile(nc, TILE_N)

    def body(ct_ref, w_ref, o_ref):
        w = w_ref[...]
        ct = ct_ref[...]
        cact = ct * _sigmoid(ct)
        rows = [jnp.sum(w * cact[:, b:b + 1], axis=0, keepdims=True) for b in range(nex)]
        o_ref[...] = jnp.concatenate(rows, axis=0)

    return _pallas(
        body, [c_t, w_ada], dep=dep, grid=(nc // tn,),
        in_specs=[_full_spec(c_t.shape), pl.BlockSpec((d, tn), lambda j: (0, j))],
        out_specs=pl.BlockSpec((nex, tn), lambda j: (0, j)),
        out_shape=jax.ShapeDtypeStruct((nex, nc), F32),
        sem=("parallel",), name="ada_fwd")


def _adamw_math(w, g, m, v):
    m = ADAM_B1 * m + (1.0 - ADAM_B1) * g
    v = ADAM_B2 * v + (1.0 - ADAM_B2) * (g * g)
    m_hat = m / (1.0 - ADAM_B1 ** ADAM_STEP)
    v_hat = v / (1.0 - ADAM_B2 ** ADAM_STEP)
    delta = -ADAM_LR * (m_hat / (jnp.sqrt(v_hat) + ADAM_EPS) + ADAM_WD * w)
    return delta, m, v


def adamw(w, g, m, v, name, copy_grad=False, dep=None):
    r, n = w.shape
    tr, tn = _ew_tiles(r, n, elems=EW_ELEMS_MANY)
    n_out = 4 if copy_grad else 3

    def body(w_ref, g_ref, m_ref, v_ref, *outs):
        g = g_ref[...]
        if copy_grad:
            outs[0][...] = g
        outs[-3][...], outs[-2][...], outs[-1][...] = _adamw_math(w_ref[...], g, m_ref[...], v_ref[...])

    blk = pl.BlockSpec((tr, tn), lambda i, j: (i, j))
    return _pallas(
        body, [w, g, m, v], dep=dep, grid=(r // tr, n // tn),
        in_specs=[blk] * 4, out_specs=[blk] * n_out,
        out_shape=[jax.ShapeDtypeStruct((r, n), F32)] * n_out,
        sem=("parallel", "parallel"), name=name)


def ada_grad_adamw(c_t, dmod_cols, w, m, v, dep=None):
    d, nc = w.shape
    nex = c_t.shape[1]
    tr, tn = _ew_tiles(d, nc, elems=EW_ELEMS_MANY)

    def body(ct_ref, dm_ref, w_ref, m_ref, v_ref, g_ref, d_ref, nm_ref, nv_ref):
        ct = ct_ref[...]
        cact = ct * _sigmoid(ct)
        dm = dm_ref[...]
        g = cact[:, 0:1] * dm[0:1, :]
        for b in range(1, nex):
            g = g + cact[:, b:b + 1] * dm[b:b + 1, :]
        g_ref[...] = g
        d_ref[...], nm_ref[...], nv_ref[...] = _adamw_math(w_ref[...], g, m_ref[...], v_ref[...])

    blk = pl.BlockSpec((tr, tn), lambda i, j: (i, j))
    return _pallas(
        body, [c_t, dmod_cols, w, m, v], dep=dep, grid=(d // tr, nc // tn),
        in_specs=[pl.BlockSpec((tr, nex), lambda i, j: (i, 0)), pl.BlockSpec((nex, tn), lambda i, j: (0, j)),
                  blk, blk, blk],
        out_specs=[blk] * 4,
        out_shape=[jax.ShapeDtypeStruct((d, nc), F32)] * 4,
        sem=("parallel", "parallel"), name="ada_grad_adamw")


def _row_pack(parts):
    cols, offs, off = [], [], 0
    for p in parts:
        n = p.shape[1]
        width = -(-n // LANES) * LANES
        cols.append(jnp.pad(p, ((0, 0), (0, width - n))) if width != n else p)
        offs.append(off)
        off += width
    return jnp.concatenate(cols, axis=1), offs


def small_sum_adamw(gathered, offs, ws, ms, vs, extra_widths, dep=None):
    ndev = gathered.shape[0]
    npar = len(ws)

    def body(ga_ref, *refs):
        w_refs, m_refs, v_refs = refs[:npar], refs[npar:2 * npar], refs[2 * npar:3 * npar]
        outs = refs[3 * npar:]
        tot = ga_ref[0]
        for s in range(1, ndev):
            tot = tot + ga_ref[s]
        for i in range(npar):
            n = ws[i].shape[1]
            g = tot[:, offs[i]:offs[i] + n]
            outs[4 * i][...] = g
            outs[4 * i + 1][...], outs[4 * i + 2][...], outs[4 * i + 3][...] = _adamw_math(
                w_refs[i][...], g, m_refs[i][...], v_refs[i][...])
        for e, n in enumerate(extra_widths):
            off = offs[npar + e]
            outs[4 * npar + e][...] = tot[:, off:off + n]

    shapes = [jax.ShapeDtypeStruct(w.shape, F32) for w in ws for _ in range(4)]
    shapes += [jax.ShapeDtypeStruct((1, n), F32) for n in extra_widths]
    return _pallas(
        body, [gathered, *ws, *ms, *vs], dep=dep, in_specs=[_VMEM] * (1 + 3 * npar), out_specs=[_VMEM] * len(shapes),
        out_shape=shapes, name="small_sum_adamw")


def _position():
    return lax.axis_index("x"), lax.axis_index("y"), lax.axis_index("c")


def _other_chips(x, y):
    return [(1 - x, y), (x, 1 - y), (1 - x, 1 - y)]


def allgather_small(block, name, dep=None):
    def body(x_ref, out_ref, send_sems, recv_sems, local_sem):
        x, y, c = _position()
        me, sibling = (x, y, c), (x, y, 1 - c)
        chips = _other_chips(x, y)

        def slot(px, py, pc):
            return out_ref.at[4 * px + 2 * py + pc]

        def copy(k, block_of, to, src=None):
            return pltpu.make_async_remote_copy(
                src_ref=slot(*block_of) if src is None else src, dst_ref=slot(*block_of),
                send_sem=send_sems.at[k], recv_sem=recv_sems.at[k], device_id=to, device_id_type=MESH)

        mine = pltpu.make_async_copy(x_ref, slot(*me), local_sem)
        mine.start()
        first = [copy(0, me, sibling, src=x_ref)]
        first += [copy(1 + j, me, (*chip, c), src=x_ref) for j, chip in enumerate(chips)]
        for cp in first:
            cp.start()
        passed = [copy(4 + j, (*chip, c), sibling) for j, chip in enumerate(chips)]
        for j, chip in enumerate(chips):
            copy(1 + j, (*chip, c), me).wait_recv()
            passed[j].start()
        copy(0, sibling, me).wait_recv()
        for j, chip in enumerate(chips):
            copy(4 + j, (*chip, 1 - c), me).wait_recv()
        for cp in first + passed:
            cp.wait_send()
        mine.wait()

    return _pallas(
        body, [block], dep=dep,
        out_shape=jax.ShapeDtypeStruct((N_DEV, *block.shape), block.dtype),
        in_specs=[_VMEM], out_specs=_VMEM,
        scratch=[pltpu.SemaphoreType.DMA((7,)), pltpu.SemaphoreType.DMA((7,)), pltpu.SemaphoreType.DMA],
        name=name)


class Started(NamedTuple):
    send_sems: Any
    recv_sems: Any
    bufs: list


def exchange_start_many(name, buf_sets, plans, dep=None):
    sizes = [len(bufs) for bufs in buf_sets]
    first = [sum(sizes[:i]) for i in range(len(sizes))]
    flat = [b for bufs in buf_sets for b in bufs]
    nb, npl = len(flat), len(plans)

    def body(*refs):
        for i, (s, _, plan) in enumerate(plans):
            for cp in plan(refs[first[s]:first[s] + sizes[s]], refs[nb + 2 * i], refs[nb + 2 * i + 1]):
                cp.start()

    sems = [pltpu.SemaphoreType.DMA((n,)) for _, n, _ in plans for _ in range(2)]
    outs = _pallas(
        body, [pltpu.with_memory_space_constraint(b, pltpu.HBM) for b in flat], dep=dep, name=name,
        out_shape=(*sems, *[pltpu.HBM(b.shape, b.dtype) for b in flat]),
        in_specs=[_HBM] * nb,
        out_specs=(*[_SEM] * (2 * npl), *[_HBM] * nb),
        input_output_aliases={i: 2 * npl + i for i in range(nb)},
        compiler_params=pltpu.CompilerParams(has_side_effects=_EFFECT))
    new_bufs = outs[2 * npl:]
    return [Started(outs[2 * i], outs[2 * i + 1], list(new_bufs[first[s]:first[s] + sizes[s]]))
            for i, (s, _, _) in enumerate(plans)]


def exchange_start(name, bufs, n_copies, plan, dep=None):
    return exchange_start_many(name, [bufs], [(0, n_copies, plan)], dep=dep)[0]


def exchange_wait(name, started, plan, bufs=None, dep=None):
    if bufs is not None:
        started = started._replace(bufs=list(bufs))
    nb = len(started.bufs)

    def body(*refs):
        for cp in plan(refs[:nb], refs[nb], refs[nb + 1]):
            cp.wait_send()
            cp.wait_recv()

    outs = _pallas(
        body, [*started.bufs, started.send_sems, started.recv_sems], dep=dep, name=name,
        out_shape=tuple(pltpu.HBM(b.shape, b.dtype) for b in started.bufs),
        in_specs=[_HBM] * nb + [_SEM, _SEM],
        out_specs=tuple([_HBM] * nb),
        input_output_aliases={i: i for i in range(nb)},
        compiler_params=pltpu.CompilerParams(has_side_effects=_EFFECT))
    return list(outs)


def _remote(src, dst, send_sems, recv_sems, i, to):
    return pltpu.make_async_remote_copy(src_ref=src, dst_ref=dst, send_sem=send_sems.at[i], recv_sem=recv_sems.at[i],
                                        device_id=to, device_id_type=MESH)


def _half_rows(buf_rows, chip_idx, pc):
    half = buf_rows // (2 * N_CHIPS)
    return pl.ds((2 * chip_idx + pc) * half, half)


ALL_PEERS = (0, 1, 2)


def plan_gather_ici(refs, send_sems, recv_sems, peers=ALL_PEERS):
    x, y, c = _position()
    chips = _other_chips(x, y)
    copies = []
    for k, ref in enumerate(refs):
        rows = ref.at[_half_rows(ref.shape[0], 2 * x + y, c), :]
        for i, j in enumerate(peers):
            copies.append(_remote(rows, rows, send_sems, recv_sems, len(peers) * k + i, (*chips[j], c)))
    return copies


def plan_gather_relay(refs, send_sems, recv_sems):
    x, y, c = _position()
    copies = []
    for k, ref in enumerate(refs):
        quarter = ref.shape[0] // (4 * N_CHIPS)
        for i, (src_chip, to) in enumerate((((1 - x, y), (x, 1 - y, c)), ((x, 1 - y), (1 - x, y, c)))):
            start = (2 * (2 * src_chip[0] + src_chip[1]) + c) * 2 * quarter + i * quarter
            rows = ref.at[pl.ds(start, quarter), :]
            copies.append(_remote(rows, rows, send_sems, recv_sems, 2 * k + i, to))
    return copies


def plan_gather_d2d(refs, send_sems, recv_sems, peers=ALL_PEERS):
    x, y, c = _position()
    chips = _other_chips(x, y)
    copies = []
    for k, ref in enumerate(refs):
        for i, j in enumerate(peers):
            px, py = chips[j]
            rows = ref.at[_half_rows(ref.shape[0], 2 * px + py, c), :]
            copies.append(_remote(rows, rows, send_sems, recv_sems, len(peers) * k + i, (x, y, 1 - c)))
    return copies


def plan_pair_exchange(refs, send_sems, recv_sems):
    x, y, c = _position()
    nw = len(refs) // 2
    copies = []
    for k in range(nw):
        for chip in range(N_CHIPS):
            copies.append(_remote(refs[k].at[chip, 1 - c], refs[nw + k].at[chip], send_sems, recv_sems,
                                  N_CHIPS * k + chip, (x, y, 1 - c)))
    return copies


def plan_chip_exchange(refs, send_sems, recv_sems):
    x, y, c = _position()
    nw = len(refs) // 2
    copies = []
    for k in range(nw):
        for j, (px, py) in enumerate(_other_chips(x, y)):
            copies.append(_remote(refs[k].at[2 * px + py], refs[nw + k].at[2 * x + y], send_sems, recv_sems,
                                  3 * k + j, (px, py, c)))
    return copies


def plan_pair_share(refs, send_sems, recv_sems):
    x, y, c = _position()
    return [_remote(ref.at[c], ref.at[c], send_sems, recv_sems, k, (x, y, 1 - c)) for k, ref in enumerate(refs)]


def cast_into_slot(src, slot, n_slots, name, dep=None):
    r, n = src.shape
    tr, tn = _ew_tiles(r, n, BF16_SUBLANES)

    def body(slot_ref, s_ref, o_ref):
        o_ref[...] = s_ref[...].astype(BF16)

    return _pallas(
        body, [slot, src], dep=dep, n_prefetch=1, grid=(r // tr, n // tn),
        in_specs=[pl.BlockSpec((tr, tn), lambda i, j, sl: (i, j))],
        out_specs=pl.BlockSpec((None, tr, tn), lambda i, j, sl: (sl[0], i, j)),
        out_shape=jax.ShapeDtypeStruct((n_slots, r, n), BF16),
        sem=("parallel", "parallel"), name=name)


def pair_sum(g, r, core, name, dep=None):
    nchip, _, h, n = g.shape
    th, tn = _ew_tiles(h, n, BF16_SUBLANES)

    def body(core_ref, g_ref, r_ref, o_ref):
        o_ref[...] = (g_ref[...].astype(F32) + r_ref[...].astype(F32)).astype(BF16)

    return _pallas(
        body, [core, g, r], dep=dep, n_prefetch=1, grid=(nchip, h // th, n // tn),
        in_specs=[pl.BlockSpec((None, None, th, tn), lambda a, i, j, cr: (a, cr[0], i, j)),
                  pl.BlockSpec((None, th, tn), lambda a, i, j, cr: (a, i, j))],
        out_specs=pl.BlockSpec((None, th, tn), lambda a, i, j, cr: (a, i, j)),
        out_shape=jax.ShapeDtypeStruct((nchip, h, n), BF16),
        sem=("parallel", "parallel", "parallel"), name=name)


def chip_sum(own, got, where, name, dep=None):
    nchip, h, n = got.shape
    th, tn = _ew_tiles(h, n, BF16_SUBLANES, elems=EW_ELEMS_MANY)

    def body(where_ref, own_ref, *rest):
        got_refs, o_ref = rest[:nchip], rest[nchip]
        chip = where_ref[0]
        acc = None
        for s in range(nchip):
            term = jnp.where(chip == s, own_ref[...], got_refs[s][...]).astype(F32)
            acc = term if acc is None else acc + term
        o_ref[...] = acc

    def got_spec(s):
        return pl.BlockSpec((None, th, tn), lambda i, j, wr: (jnp.where(wr[0] == s, (s + 1) % nchip, s), i, j))

    return _pallas(
        body, [where, own, *[got] * nchip], dep=dep, n_prefetch=1, grid=(h // th, n // tn),
        in_specs=[pl.BlockSpec((None, th, tn), lambda i, j, wr: (wr[0], i, j))]
        + [got_spec(s) for s in range(nchip)],
        out_specs=pl.BlockSpec((None, th, tn), lambda i, j, wr: (wr[1], i, j)),
        out_shape=jax.ShapeDtypeStruct((2, h, n), F32),
        sem=("parallel", "parallel"), name=name)


def kernel(x, c, w_ada, b_ada, norm_mix_g, w_in, q_norm_g, k_norm_g, attn_sinks, rel_bias, w_attn_out, conv_w, conv_b, conv_ln_g, conv_ln_b, w_conv_out, w_mix_out, norm_ffn_g, w_ffn_in, w_ffn_out, loss_target, m_w_ada, m_b_ada, m_norm_mix_g, m_w_in, m_q_norm_g, m_k_norm_g, m_attn_sinks, m_rel_bias, m_w_attn_out, m_conv_w, m_conv_b, m_conv_ln_g, m_conv_ln_b, m_w_conv_out, m_w_mix_out, m_norm_ffn_g, m_w_ffn_in, m_w_ffn_out, v_w_ada, v_b_ada, v_norm_mix_g, v_w_in, v_q_norm_g, v_k_norm_g, v_attn_sinks, v_rel_bias, v_w_attn_out, v_conv_w, v_conv_b, v_conv_ln_g, v_conv_ln_b, v_w_conv_out, v_w_mix_out, v_norm_ffn_g, v_w_ffn_in, v_w_ffn_out):
    run = InOrder()
    xi, yi, ci = _position()
    chip = 2 * xi + yi
    me = 2 * chip + ci
    chip_arr = chip.astype(jnp.int32).reshape(1)
    core_arr = ci.astype(jnp.int32).reshape(1)
    where_arr = jnp.stack([chip, ci]).astype(jnp.int32)

    xe, tgt = x[0], loss_target[0]
    t, d = xe.shape
    hd = q_norm_g.shape[-1]
    nq = attn_sinks.shape[-1]
    aw = nq * hd
    ch = conv_b.shape[-1]
    in_width = N_CHIPS * w_in.shape[-1]
    kvw = (in_width - aw - 2 * ch - 2 * d) // 2
    nkv = kvw // hd
    dff = N_CHIPS * w_ffn_out.shape[1]
    off_k, off_v, off_ca = aw, aw + kvw, aw + 2 * kvw
    off_cb, off_ga, off_gc = off_ca + ch, off_ca + 2 * ch, off_ca + 2 * ch + d
    nc_ada = w_ada.shape[-1]
    ch_loc = conv_w.shape[-1]
    nj_ffn = w_ffn_in.shape[-1]
    perm_ffn = ffn_perm(N_CHIPS)

    big = {"w_in": w_in[0], "w_attn_out": w_attn_out[0], "w_conv_out": w_conv_out[0], "w_mix_out": w_mix_out[0],
           "w_ffn_in": w_ffn_in[0], "w_ffn_out": w_ffn_out[0]}
    moments = {"w_in": (m_w_in, v_w_in), "w_attn_out": (m_w_attn_out, v_w_attn_out),
               "w_conv_out": (m_w_conv_out, v_w_conv_out), "w_mix_out": (m_w_mix_out, v_w_mix_out),
               "w_ffn_in": (m_w_ffn_in, v_w_ffn_in), "w_ffn_out": (m_w_ffn_out, v_w_ffn_out)}
    gather_groups = {"in": ["w_in"], "branch_out": ["w_attn_out", "w_conv_out"], "mix_out": ["w_mix_out"],
                     "ffn_in": ["w_ffn_in"], "ffn_out": ["w_ffn_out"]}
    grads, deltas, new_m, new_v = {}, {}, {}, {}

    def gather_cast(gname):
        bufs = []
        for n in gather_groups[gname]:
            r, ncol = big[n].shape
            bufs.append(run(cast_into_slot, big[n], chip_arr, N_CHIPS, "cast_" + n).reshape(N_CHIPS * r, ncol))
        return bufs

    def gather_pass_on(gname, ici):
        landed = run(exchange_wait, "gather_ici_wait_" + gname, ici, plan_gather_ici)
        return run(exchange_start, "gather_d2d_start_" + gname, landed, 3 * len(landed), plan_gather_d2d)

    def gathered(gname, d2d):
        outs = run(exchange_wait, "gather_d2d_wait_" + gname, d2d, plan_gather_d2d)
        return [o.reshape(N_CHIPS, *big[n].shape) for o, n in zip(outs, gather_groups[gname])]

    def rs_pair_start(gname, names, partials):
        blocks = [g.reshape(N_CHIPS, 2, big[n].shape[0] // 2, big[n].shape[1]) for n, g in zip(names, partials)]
        land = [lax.empty((N_CHIPS,) + b.shape[2:], BF16) for b in blocks]
        return run(exchange_start, "pair_exchange_start_" + gname, blocks + land, N_CHIPS * len(blocks),
                   plan_pair_exchange)

    def rs_chip_start(gname, names, pair):
        nw = len(names)
        outs = run(exchange_wait, "pair_exchange_wait_" + gname, pair, plan_pair_exchange)
        sums = [run(pair_sum, g, r, core_arr, "pair_sum_" + n) for n, g, r in zip(names, outs[:nw], outs[nw:])]
        land = [lax.empty(s.shape, BF16) for s in sums]
        return run(exchange_start, "chip_exchange_start_" + gname, sums + land, 3 * nw, plan_chip_exchange)

    def rs_share_start(gname, names, chipx):
        nw = len(names)
        outs = run(exchange_wait, "chip_exchange_wait_" + gname, chipx, plan_chip_exchange)
        halves = [run(chip_sum, s, r, where_arr, "chip_sum_" + n) for n, s, r in zip(names, outs[:nw], outs[nw:])]
        return run(exchange_start, "pair_share_start_" + gname, halves, nw, plan_pair_share)

    def rs_finish(gname, names, share):
        fulls = run(exchange_wait, "pair_share_wait_" + gname, share, plan_pair_share)
        for n, g2 in zip(names, fulls):
            g, dl, nm, nv = run(adamw, big[n], g2.reshape(big[n].shape), moments[n][0][0], moments[n][1][0],
                                "adamw_" + n, copy_grad=True)
            grads[n], deltas[n], new_m[n], new_v[n] = g[None], dl[None], nm[None], nv[None]

    near, far = (0, 1), (2,)
    plan_ici_near = functools.partial(plan_gather_ici, peers=near)
    plan_ici_far, n_far = plan_gather_relay, 2
    plan_d2d_near = functools.partial(plan_gather_d2d, peers=near)
    plan_d2d_far = functools.partial(plan_gather_d2d, peers=far)
    bufs_in = gather_cast("in")
    row1, offs1 = _row_pack([c, conv_w[0].reshape(1, CONV_WIDTH * ch_loc)])
    got1 = run(allgather_small, row1, "allgather_cond")
    ici_near = run(exchange_start, "gather_ici_start_in_near", bufs_in, len(near), plan_ici_near)
    c_all = got1[:, 0, :d]
    conv_w_full = got1[0::2, 0, offs1[1]:offs1[1] + CONV_WIDTH * ch_loc].reshape(N_CHIPS, CONV_WIDTH, ch_loc)
    conv_w_full = jnp.transpose(conv_w_full, (1, 0, 2)).reshape(CONV_WIDTH, ch)
    conv_w_pad = jnp.pad(conv_w_full, ((0, 1), (0, 0)))
    c_t = jnp.transpose(c_all)
    mod_cols = run(ada_fwd, c_t, w_ada[0])
    rest_bufs = {gname: gather_cast(gname) for gname in gather_groups if gname != "in"}
    bucket = _t5_bucket_table()
    bucket_p, bucket_c = jnp.asarray(bucket[:, :BLOCK]), jnp.asarray(bucket[:, BLOCK:])
    bias_p, bias_c = run(bias_table, rel_bias, bucket_p, bucket_c)
    got2 = run(allgather_small, mod_cols, "allgather_mod")
    mod_all = got2.reshape(N_CHIPS, 2, N_DEV, nc_ada)[:, 0]
    mod = lax.dynamic_slice_in_dim(mod_all, me, 1, axis=1).reshape(1, N_CHIPS * nc_ada) + b_ada
    mod = jnp.pad(mod.reshape(N_MOD, d), ((0, SUBLANES - N_MOD), (0, 0)))

    landed = run(exchange_wait, "gather_ici_wait_in_near", ici_near, plan_ici_near)
    ici_far, d2d_near = run(exchange_start_many, "gather_start_in_far", [landed],
                            [(0, n_far, plan_ici_far), (0, len(near), plan_d2d_near)])
    h = run(pre_mix_fwd, xe, mod, norm_mix_g)
    ici = {}
    ici["branch_out"], ici_near_ffn, ici["mix_out"] = run(
        exchange_start_many, "gather_ici_start_mid", [rest_bufs["branch_out"], rest_bufs["ffn_in"], rest_bufs["mix_out"]],
        [(0, 3 * len(rest_bufs["branch_out"]), plan_gather_ici), (1, len(near), plan_ici_near), (2, 3, plan_gather_ici)])
    landed = run(exchange_wait, "gather_d2d_wait_in_near", d2d_near, plan_d2d_near)
    landed = run(exchange_wait, "gather_ici_wait_in_far", ici_far, plan_ici_far, bufs=landed)
    d2d_far = run(exchange_start, "gather_d2d_start_in_far", landed, len(far), plan_d2d_far)
    landed = run(exchange_wait, "gather_d2d_wait_in_far", d2d_far, plan_d2d_far)
    wg_in = landed[0].reshape(N_CHIPS, *big["w_in"].shape)
    p = run(mm_nn, h, wg_in, tn=wg_in.shape[2], tk=d, out_dtype=BF16, name="mm_in")
    d2d_branch = gather_pass_on("branch_out", ici["branch_out"])

    sinks3 = attn_sinks.reshape(nq, 1, 1)
    attn_o = run(attn_fwd, p, bias_p, bias_c, sinks3, q_norm_g, k_norm_g, aw=aw, kvw=kvw)
    ca, cb = p[:, off_ca:off_cb], p[:, off_cb:off_ga]
    s_conv, co_conv = run(conv_fwd, ca, cb, conv_w_pad, conv_b, conv_ln_g, conv_ln_b)
    wg_attn_out, wg_conv_out = gathered("branch_out", d2d_branch)
    landed = run(exchange_wait, "gather_ici_wait_ffn_in_near", ici_near_ffn, plan_ici_near)
    ici_far_ffn, ici["ffn_out"] = run(
        exchange_start_many, "gather_start_ffn_in_far", [landed, rest_bufs["ffn_out"]],
        [(0, n_far, plan_ici_far), (1, 3, plan_gather_ici)])
    y_attn, y_conv, merged = run(branch_out_merge, attn_o, s_conv, wg_attn_out, wg_conv_out, p, off_ga, off_gc)
    landed_mix = run(exchange_wait, "gather_ici_wait_mix_out", ici["mix_out"], plan_gather_ici)
    d2d_mix, d2d_near_ffn = run(
        exchange_start_many, "gather_d2d_start_mix_ffn_in", [landed_mix, ici_far_ffn.bufs],
        [(0, 3 * len(landed_mix), plan_gather_d2d), (1, len(near), plan_d2d_near)])
    (wg_mix_out,) = gathered("mix_out", d2d_mix)
    wg_mix_out = wg_mix_out.reshape(1, d, d)
    o_m = run(mm_nn, merged, wg_mix_out, tn=_tile(d, TILE_N), tk=d, out_dtype=BF16, name="mm_mix_out")
    landed = run(exchange_wait, "gather_d2d_wait_ffn_in_near", d2d_near_ffn, plan_d2d_near)
    landed = run(exchange_wait, "gather_ici_wait_ffn_in_far", ici_far_ffn, plan_ici_far, bufs=landed)
    d2d_far_ffn = run(exchange_start, "gather_d2d_start_ffn_in_far", landed, len(far), plan_d2d_far)
    x1, h2 = run(pre_ffn_fwd, xe, o_m, mod, norm_ffn_g)
    landed = run(exchange_wait, "gather_d2d_wait_ffn_in_far", d2d_far_ffn, plan_d2d_far)
    wg_ffn_in = landed[0].reshape(N_CHIPS, *big["w_ffn_in"].shape)
    f = run(mm_nn, h2, wg_ffn_in, tn=_tile(nj_ffn, nj_ffn // 2), tk=d, out_dtype=BF16, name="mm_ffn_in", perm=perm_ffn)
    d2d_ffn_out = gather_pass_on("ffn_out", ici["ffn_out"])
    act = run(swiglu_fwd, f, nj_ffn)
    (wg_ffn_out,) = gathered("ffn_out", d2d_ffn_out)
    wg_ffn_out = wg_ffn_out.reshape(1, dff, d)
    o_f = run(mm_nn, act, wg_ffn_out, tn=_tile(d, TILE_N), tk=_tile(dff, dff // 2), out_dtype=BF16, name="mm_ffn_out")
    loss11, dy, dof, acc_l = run(loss_head, x1, o_f, tgt, mod)

    gw_ffn_out = run(mm_tn, act, dof, 1, tk=_tile(dff, TILE_N), tn=d, name="mm_ffn_out_dw")
    px_ffn_out = rs_pair_start("ffn_out", ["w_ffn_out"], [gw_ffn_out])
    dact = run(mm_nt, dof, wg_ffn_out, tko=_tile(dff, TILE_N), tn=d, out_dtype=BF16, name="mm_ffn_out_dx")
    cx_ffn_out = rs_chip_start("ffn_out", ["w_ffn_out"], px_ffn_out)
    df = run(swiglu_bwd, f, dact, nj_ffn)
    gw_ffn_in = run(mm_tn, h2, df, N_CHIPS, tk=d, tn=_tile(nj_ffn, nj_ffn // 2), name="mm_ffn_in_dw",
                    perm=perm_ffn)
    px_ffn_in = rs_pair_start("ffn_in", ["w_ffn_in"], [gw_ffn_in])
    dh2 = run(mm_nt, df, wg_ffn_in, tko=_tile(d, TILE_N), tn=nj_ffn, out_dtype=BF16, name="mm_ffn_in_dx", perm=perm_ffn)
    sh_ffn_out = rs_share_start("ffn_out", ["w_ffn_out"], cx_ffn_out)
    cx_ffn_in = rs_chip_start("ffn_in", ["w_ffn_in"], px_ffn_in)
    dx1, dom, acc_f = run(pre_ffn_bwd, x1, dh2, dy, o_m, mod, norm_ffn_g)
    gw_mix_out = run(mm_tn, merged, dom, 1, tk=d, tn=_tile(d, TILE_WIDE), name="mm_mix_out_dw")
    px_mix = rs_pair_start("mix_out", ["w_mix_out"], [gw_mix_out])
    dy_attn, dy_conv, dga, dgc = run(mix_out_dx_merge_bwd, dom, wg_mix_out.reshape(d, d), p, y_attn, y_conv,
                                     off_ga, off_gc)
    rs_finish("ffn_out", ["w_ffn_out"], sh_ffn_out)
    cx_mix = rs_chip_start("mix_out", ["w_mix_out"], px_mix)
    gw_attn_out = run(mm_tn, attn_o, dy_attn, N_CHIPS, tk=aw, tn=_tile(wg_attn_out.shape[2], TILE_N),
                      name="mm_attn_out_dw")
    gw_conv_out = run(mm_tn, s_conv, dy_conv, N_CHIPS, tk=ch, tn=_tile(wg_conv_out.shape[2], TILE_N),
                      name="mm_conv_out_dw")
    ac_names = ["w_attn_out", "w_conv_out"]
    px_ac = rs_pair_start("attn_conv_out", ac_names, [gw_attn_out, gw_conv_out])
    dattn_o = run(mm_nt, dy_attn, wg_attn_out, tko=_tile(aw, TILE_WIDE), tn=_tile(wg_attn_out.shape[2], TILE_N),
                  out_dtype=BF16, name="mm_attn_out_dx")
    ds_conv = run(mm_nt, dy_conv, wg_conv_out, tko=_tile(ch, TILE_WIDE), tn=_tile(wg_conv_out.shape[2], TILE_N),
                  out_dtype=BF16, name="mm_conv_out_dx")
    cx_ac = rs_chip_start("attn_conv_out", ac_names, px_ac)
    dca, dcb, dconv_w, dconv_vec = run(conv_bwd, ca, cb, co_conv, ds_conv, conv_w_pad, conv_ln_g, conv_ln_b)
    sh_ffn_in = rs_share_start("ffn_in", ["w_ffn_in"], cx_ffn_in)
    dqkv, dbp, dbc, dsinks, dqg, dkg = run(attn_bwd, p, bias_p, bias_c, sinks3, q_norm_g, k_norm_g, dattn_o,
                                           aw=aw, kvw=kvw)
    sh_mix = rs_share_start("mix_out", ["w_mix_out"], cx_mix)
    sh_ac = rs_share_start("attn_conv_out", ac_names, cx_ac)
    drel = run(bias_table_bwd, dbp, dbc, bucket_p, bucket_c).reshape(NUM_BUCKETS, nq)
    dp = jnp.concatenate([dqkv, dca, dcb, dga, dgc], axis=1)
    gw_in = run(mm_tn, h, dp, N_CHIPS, tk=d, tn=wg_in.shape[2], name="mm_in_dw")
    px_in = rs_pair_start("in", ["w_in"], [gw_in])
    dh = run(mm_nt, dp, wg_in, tko=_tile(d, TILE_WIDE), tn=wg_in.shape[2], out_dtype=BF16, name="mm_in_dx")
    grad_x, acc_m = run(pre_mix_bwd, xe, dh, dx1, mod, norm_mix_g)

    dmod = jnp.concatenate([acc_m[0:1], acc_m[1:2], acc_f[3:4], acc_f[0:1], acc_f[1:2], acc_l[0:1]], axis=1)
    small_names = ["b_ada", "norm_mix_g", "q_norm_g", "k_norm_g", "attn_sinks", "rel_bias", "conv_b", "conv_ln_g",
                   "conv_ln_b", "norm_ffn_g"]
    small_w = [b_ada, norm_mix_g, q_norm_g, k_norm_g, attn_sinks, rel_bias, conv_b, conv_ln_g, conv_ln_b, norm_ffn_g]
    small_m = [m_b_ada, m_norm_mix_g, m_q_norm_g, m_k_norm_g, m_attn_sinks, m_rel_bias, m_conv_b, m_conv_ln_g,
               m_conv_ln_b, m_norm_ffn_g]
    small_v = [v_b_ada, v_norm_mix_g, v_q_norm_g, v_k_norm_g, v_attn_sinks, v_rel_bias, v_conv_b, v_conv_ln_g,
               v_conv_ln_b, v_norm_ffn_g]
    small_g = [dmod, acc_m[2:3], dqg, dkg, dsinks.reshape(1, nq), drel.reshape(1, NUM_BUCKETS * nq),
               dconv_vec[0:1], dconv_vec[1:2], dconv_vec[2:3], acc_f[2:3]]
    row3, offs3 = _row_pack(small_g + [dconv_w[:CONV_WIDTH].reshape(1, CONV_WIDTH * ch), loss11])
    got3 = run(allgather_small, row3, "allgather_small_grads")
    cx_in = rs_chip_start("in", ["w_in"], px_in)
    as_row = lambda a: a.reshape(1, -1)
    outs3 = run(small_sum_adamw, got3, offs3, [as_row(a) for a in small_w], [as_row(a) for a in small_m],
                [as_row(a) for a in small_v], [CONV_WIDTH * ch, 1])
    for i, (n, w) in enumerate(zip(small_names, small_w)):
        grads[n], deltas[n], new_m[n], new_v[n] = (o.reshape(w.shape) for o in outs3[4 * i:4 * i + 4])
    g_conv_w_all, loss_sum = outs3[-2].reshape(CONV_WIDTH, ch), outs3[-1]

    g_conv_w = lax.dynamic_slice_in_dim(g_conv_w_all, chip * ch_loc, ch_loc, axis=1)
    grads["conv_w"] = g_conv_w[None]
    dl, nm, nv = run(adamw, conv_w[0], g_conv_w, m_conv_w[0], v_conv_w[0], "adamw_conv_w")
    deltas["conv_w"], new_m["conv_w"], new_v["conv_w"] = dl[None], nm[None], nv[None]

    dmod_all = got3[:, 0, :N_MOD * d]
    dmod_cols = lax.dynamic_slice_in_dim(dmod_all, chip * nc_ada, nc_ada, axis=1)
    g_ada, dl, nm, nv = run(ada_grad_adamw, c_t, dmod_cols, w_ada[0], m_w_ada[0], v_w_ada[0])
    grads["w_ada"], deltas["w_ada"], new_m["w_ada"], new_v["w_ada"] = g_ada[None], dl[None], nm[None], nv[None]

    rs_finish("ffn_in", ["w_ffn_in"], sh_ffn_in)
    rs_finish("mix_out", ["w_mix_out"], sh_mix)
    rs_finish("attn_conv_out", ac_names, sh_ac)
    sh_in = rs_share_start("in", ["w_in"], cx_in)
    rs_finish("in", ["w_in"], sh_in)

    loss = loss_sum[0, 0]
    order = ["w_ada", "b_ada", "norm_mix_g", "w_in", "q_norm_g", "k_norm_g", "attn_sinks", "rel_bias", "w_attn_out",
             "conv_w", "conv_b", "conv_ln_g", "conv_ln_b", "w_conv_out", "w_mix_out", "norm_ffn_g", "w_ffn_in",
             "w_ffn_out"]
    return (loss, grad_x[None], *[grads[n] for n in order], *[deltas[n] for n in order],
            *[new_m[n] for n in order], *[new_v[n] for n in order])
```

```python
import functools
import math
from typing import Any, NamedTuple

import jax
import jax.numpy as jnp
import numpy as np
from jax import lax
from jax.experimental import pallas as pl
from jax.experimental.pallas import tpu as pltpu

F32 = jnp.float32
BF16 = jnp.bfloat16
MESH = pl.DeviceIdType.MESH

V7X_VMEM_BYTES = 64 * 1024 * 1024
VMEM_LIMIT = V7X_VMEM_BYTES - 8 * 1024 * 1024
LANES = 128
SUBLANES = 8
BF16_SUBLANES = 16

EPS = 1e-6
WINDOW = 128
BLOCK = 128
NUM_BUCKETS = 32
MAX_EXACT = NUM_BUCKETS // 2
MAX_DISTANCE = 128
CONV_WIDTH = 31
CONV_HALO = 32
ADAM_LR = 0.001
ADAM_B1 = 0.9
ADAM_B2 = 0.999
ADAM_EPS = 1e-08
ADAM_WD = 0.01
ADAM_STEP = 10
N_MOD = 6
SH_M, SC_M, GT_M, SH_F, SC_F, GT_F = range(6)

N_CHIPS = 4
N_DEV = 8

_ANY = pl.BlockSpec(memory_space=pl.ANY)
_VMEM = pl.BlockSpec(memory_space=pltpu.VMEM)
_SMEM = pl.BlockSpec(memory_space=pltpu.SMEM)
_HBM = pl.BlockSpec(memory_space=pltpu.HBM)
_SEM = pl.BlockSpec(memory_space=pltpu.SEMAPHORE)
_EFFECT = pltpu.SideEffectType.DATAFLOW_SIDE_EFFECTING


class InOrder:
    def __init__(self):
        self.token = None

    def __call__(self, fn, *args, **kw):
        return fn(*args, dep=self, **kw)


def _pallas(body, args, *, in_specs, out_specs, out_shape, name, dep=None, grid=(), n_prefetch=0, scratch=(),
            sem=None, **kw):
    n_lead = n_prefetch + len(in_specs)
    in_specs, args = list(in_specs), list(args)
    single = not isinstance(out_shape, (list, tuple))
    out_shapes = [out_shape] if single else list(out_shape)
    out_specs = [out_specs] if single else list(out_specs)
    if dep is not None:
        inner, n_out, takes = body, len(out_shapes), dep.token is not None

        def body(*refs):
            rest = refs[n_lead + (1 if takes else 0):]
            rest[n_out][...] = jnp.zeros((SUBLANES, LANES), F32)
            return inner(*refs[:n_lead], *rest[:n_out], *rest[n_out + 1:])

        if takes:
            in_specs.append(_ANY)
            args.append(dep.token)
        out_shapes.append(jax.ShapeDtypeStruct((SUBLANES, LANES), F32))
        out_specs.append(pl.BlockSpec((SUBLANES, LANES), lambda *_: (0, 0)))
    params = kw.pop("compiler_params", None)
    if params is None:
        params = pltpu.CompilerParams(dimension_semantics=sem, vmem_limit_bytes=VMEM_LIMIT)
    outs = pl.pallas_call(
        body,
        grid_spec=pltpu.PrefetchScalarGridSpec(num_scalar_prefetch=n_prefetch, grid=grid, in_specs=in_specs,
                                               out_specs=out_specs, scratch_shapes=list(scratch)),
        out_shape=out_shapes, compiler_params=params, name=name, **kw,
    )(*args)
    if dep is not None:
        dep.token = outs[-1]
        outs = outs[:-1]
    return outs[0] if single else list(outs)


def _tile(n, pref, unit=LANES):
    best = None
    for t in range(unit, min(n, pref) + 1, unit):
        if n % t == 0:
            best = t
    return best if best is not None else n


def _sigmoid(v):
    return 1.0 / (1.0 + jnp.exp(-v.astype(F32)))


ROW_CHUNK = 1024
TILE_N = 512
TILE_WIDE = 1024
EW_ELEMS = 512 * 1024
EW_ELEMS_MANY = 704 * 1024
EW_MIN_STEPS_MANY = 4


def _row_chunks(m, unit=SUBLANES):
    step = _tile(m, ROW_CHUNK, unit)
    return [(s, step) for s in range(0, m, step)]


def _ew_tiles(r, n, unit=SUBLANES, elems=EW_ELEMS, min_steps=1):
    return _tile(r, max(unit, min(elems // n, r // min_steps)), unit), n


def _block_pos(j, perm):
    if perm is None:
        return j
    pos = 0
    for a, p in enumerate(perm):
        pos = pos + jnp.where(j == a, p, 0)
    return pos


def mm_nn(a, w, *, tn, tk, out_dtype, name, perm=None, dep=None):
    m, k = a.shape
    j, k2, nj = w.shape
    assert k == k2 and nj % tn == 0 and k % tk == 0
    npj, nk = nj // tn, k // tk

    def body(a_ref, w_ref, o_ref, *scratch):
        kk = pl.program_id(1)
        for s, sz in _row_chunks(m):
            rows = pl.ds(s, sz)
            p = jnp.dot(a_ref[rows, :], w_ref[...], preferred_element_type=F32)
            if nk == 1:
                o_ref[rows, :] = p.astype(out_dtype)
            else:
                acc = scratch[0]

                @pl.when(kk == 0)
                def _():
                    acc[rows, :] = p

                @pl.when(kk > 0)
                def _():
                    acc[rows, :] += p

                @pl.when(kk == nk - 1)
                def _():
                    o_ref[rows, :] = acc[rows, :].astype(out_dtype)

    return _pallas(
        body, [a, w], dep=dep, grid=(j * npj, nk),
        in_specs=[
            pl.BlockSpec((m, tk), lambda n, kk: (0, kk)),
            pl.BlockSpec((None, tk, tn), lambda n, kk: (n // npj, kk, n % npj)),
        ],
        out_specs=pl.BlockSpec((m, tn), lambda n, kk: (0, _block_pos(n // npj, perm) * npj + n % npj)),
        out_shape=jax.ShapeDtypeStruct((m, j * nj), out_dtype),
        scratch=[pltpu.VMEM((m, tn), F32)] if nk > 1 else [],
        sem=("parallel", "arbitrary"), name=name)


def mm_nt(g, w, *, tko, tn, name, out_dtype=F32, perm=None, dep=None):
    m, n = g.shape
    j, k, nj = w.shape
    assert n == j * nj and nj % tn == 0 and k % tko == 0
    npj, nr = nj // tn, n // tn
    in_place = out_dtype == F32

    def body(g_ref, w_ref, o_ref, *scratch):
        r = pl.program_id(1)
        acc = o_ref if in_place else (scratch[0] if nr > 1 else None)
        for s, sz in _row_chunks(m):
            rows = pl.ds(s, sz)
            p = lax.dot_general(g_ref[rows, :], w_ref[...], (((1,), (1,)), ((), ())), preferred_element_type=F32)
            if acc is None:
                o_ref[rows, :] = p.astype(out_dtype)
                continue

            @pl.when(r == 0)
            def _():
                acc[rows, :] = p

            @pl.when(r > 0)
            def _():
                acc[rows, :] += p

            if not in_place:
                @pl.when(r == nr - 1)
                def _():
                    o_ref[rows, :] = acc[rows, :].astype(out_dtype)

    return _pallas(
        body, [g, w], dep=dep, grid=(k // tko, nr),
        in_specs=[
            pl.BlockSpec((m, tn), lambda ko, r: (0, _block_pos(r // npj, perm) * npj + r % npj)),
            pl.BlockSpec((None, tko, tn), lambda ko, r: (r // npj, ko, r % npj)),
        ],
        out_specs=pl.BlockSpec((m, tko), lambda ko, r: (0, ko)),
        out_shape=jax.ShapeDtypeStruct((m, k), out_dtype),
        scratch=[pltpu.VMEM((m, tko), F32)] if (nr > 1 and not in_place) else [],
        sem=("parallel", "arbitrary"), name=name)


def mm_tn(a, g, n_blocks, *, tk, tn, name, perm=None, dep=None):
    m, k = a.shape
    m2, n = g.shape
    nj = n // n_blocks
    assert m == m2 and nj % tn == 0 and k % tk == 0
    npj = nj // tn

    def body(a_ref, g_ref, o_ref):
        for s, sz in _row_chunks(tk, LANES):
            p = lax.dot_general(a_ref[:, pl.ds(s, sz)], g_ref[...], (((0,), (0,)), ((), ())),
                                preferred_element_type=F32)
            o_ref[pl.ds(s, sz), :] = p.astype(BF16)

    return _pallas(
        body, [a, g], dep=dep, grid=(k // tk, n // tn),
        in_specs=[
            pl.BlockSpec((m, tk), lambda kk, nn: (0, kk)),
            pl.BlockSpec((m, tn), lambda kk, nn: (0, _block_pos(nn // npj, perm) * npj + nn % npj)),
        ],
        out_specs=pl.BlockSpec((None, tk, tn), lambda kk, nn: (nn // npj, kk, nn % npj)),
        out_shape=jax.ShapeDtypeStruct((n_blocks, k, nj), BF16),
        sem=("parallel", "parallel"), name=name)


ROW_TILE = 256


def _row_spec(tr, width):
    return pl.BlockSpec((tr, width), lambda i: (i, 0))


def _full_spec(shape):
    return pl.BlockSpec(shape, lambda *_: (0,) * len(shape))


def _rms(xv):
    return lax.rsqrt(jnp.mean(xv * xv, axis=-1, keepdims=True) + EPS)


def _mod_row(mod_ref, row):
    return mod_ref[pl.ds(row, 1), :]


def pre_mix_fwd(x, mod, gain, dep=None):
    t, d = x.shape
    tr = _tile(t, ROW_TILE, SUBLANES)

    def body(x_ref, mod_ref, g_ref, h_ref):
        xv = x_ref[...]
        y = xv * _rms(xv) * g_ref[...]
        h_ref[...] = (y * (1.0 + _mod_row(mod_ref, SC_M)) + _mod_row(mod_ref, SH_M)).astype(BF16)

    return _pallas(
        body, [x, mod, gain], dep=dep, grid=(t // tr,),
        in_specs=[_row_spec(tr, d), _full_spec(mod.shape), _full_spec(gain.shape)],
        out_specs=_row_spec(tr, d),
        out_shape=jax.ShapeDtypeStruct((t, d), BF16),
        sem=("parallel",), name="pre_mix_fwd")


def pre_ffn_fwd(x, o_m, mod, gain, dep=None):
    t, d = x.shape
    tr = _tile(t, ROW_TILE, SUBLANES)

    def body(x_ref, om_ref, mod_ref, g_ref, x1_ref, h_ref):
        x1 = x_ref[...] + _mod_row(mod_ref, GT_M) * om_ref[...]
        x1_ref[...] = x1
        y = x1 * _rms(x1) * g_ref[...]
        h_ref[...] = (y * (1.0 + _mod_row(mod_ref, SC_F)) + _mod_row(mod_ref, SH_F)).astype(BF16)

    return _pallas(
        body, [x, o_m, mod, gain], dep=dep, grid=(t // tr,),
        in_specs=[_row_spec(tr, d), _row_spec(tr, d), _full_spec(mod.shape), _full_spec(gain.shape)],
        out_specs=[_row_spec(tr, d), _row_spec(tr, d)],
        out_shape=[jax.ShapeDtypeStruct((t, d), F32), jax.ShapeDtypeStruct((t, d), BF16)],
        sem=("parallel",), name="pre_ffn_fwd")


def loss_head(x1, o_f, target, mod, dep=None):
    t, d = x1.shape
    tr = _tile(t, ROW_TILE, SUBLANES)

    def body(x1_ref, of_ref, tg_ref, mod_ref, loss_ref, dy_ref, dof_ref, acc_ref):
        i = pl.program_id(0)
        gt = _mod_row(mod_ref, GT_F)
        of = of_ref[...].astype(F32)
        err = x1_ref[...] + gt * of - tg_ref[...]
        dy = err * (1.0 / d)
        dy_ref[...] = dy.astype(BF16)
        dof_ref[...] = (dy * gt).astype(BF16)
        part = (0.5 / d) * jnp.sum(jnp.sum(err * err, axis=1, keepdims=True), axis=0, keepdims=True)
        dgt = jnp.sum(dy * of, axis=0, keepdims=True)

        @pl.when(i == 0)
        def _():
            loss_ref[...] = jnp.zeros_like(loss_ref)
            acc_ref[...] = jnp.zeros_like(acc_ref)

        loss_ref[...] += part
        acc_ref[pl.ds(0, 1), :] += dgt

    return _pallas(
        body, [x1, o_f, target, mod], dep=dep, grid=(t // tr,),
        in_specs=[_row_spec(tr, d), _row_spec(tr, d), _row_spec(tr, d), _full_spec(mod.shape)],
        out_specs=[_full_spec((1, 1)), _row_spec(tr, d), _row_spec(tr, d), _full_spec((SUBLANES, d))],
        out_shape=[jax.ShapeDtypeStruct((1, 1), F32), jax.ShapeDtypeStruct((t, d), BF16),
                   jax.ShapeDtypeStruct((t, d), BF16), jax.ShapeDtypeStruct((SUBLANES, d), F32)],
        sem=("arbitrary",), name="loss_head")


def _norm_bwd(xv, dh, sc, gain):
    rstd = _rms(xv)
    yn = xv * rstd
    dsh = jnp.sum(dh, axis=0, keepdims=True)
    dsc = jnp.sum(dh * (yn * gain), axis=0, keepdims=True)
    dgain = jnp.sum(dh * (1.0 + sc) * yn, axis=0, keepdims=True)
    dyn = dh * ((1.0 + sc) * gain)
    dx = rstd * (dyn - yn * jnp.mean(dyn * yn, axis=-1, keepdims=True))
    return dx, dsh, dsc, dgain


def pre_ffn_bwd(x1, dh2, dy, o_m, mod, gain, dep=None):
    t, d = x1.shape
    tr = _tile(t, ROW_TILE, SUBLANES)

    def body(x1_ref, dh_ref, dy_ref, om_ref, mod_ref, g_ref, dx1_ref, dom_ref, acc_ref):
        i = pl.program_id(0)
        dxn, dsh, dsc, dgain = _norm_bwd(x1_ref[...], dh_ref[...].astype(F32), _mod_row(mod_ref, SC_F), g_ref[...])
        dx1 = dy_ref[...] + dxn
        dx1_ref[...] = dx1
        dom_ref[...] = (dx1 * _mod_row(mod_ref, GT_M)).astype(BF16)
        dgt = jnp.sum(dx1 * om_ref[...], axis=0, keepdims=True)

        @pl.when(i == 0)
        def _():
            acc_ref[...] = jnp.zeros_like(acc_ref)

        acc_ref[pl.ds(0, 1), :] += dsh
        acc_ref[pl.ds(1, 1), :] += dsc
        acc_ref[pl.ds(2, 1), :] += dgain
        acc_ref[pl.ds(3, 1), :] += dgt

    return _pallas(
        body, [x1, dh2, dy, o_m, mod, gain], dep=dep, grid=(t // tr,),
        in_specs=[_row_spec(tr, d)] * 4 + [_full_spec(mod.shape), _full_spec(gain.shape)],
        out_specs=[_row_spec(tr, d), _row_spec(tr, d), _full_spec((SUBLANES, d))],
        out_shape=[jax.ShapeDtypeStruct((t, d), F32), jax.ShapeDtypeStruct((t, d), BF16),
                   jax.ShapeDtypeStruct((SUBLANES, d), F32)],
        sem=("arbitrary",), name="pre_ffn_bwd")


def pre_mix_bwd(x, dh, dx1, mod, gain, dep=None):
    t, d = x.shape
    tr = _tile(t, ROW_TILE, SUBLANES)

    def body(x_ref, dh_ref, dx1_ref, mod_ref, g_ref, gx_ref, acc_ref):
        i = pl.program_id(0)
        dxn, dsh, dsc, dgain = _norm_bwd(x_ref[...], dh_ref[...].astype(F32), _mod_row(mod_ref, SC_M), g_ref[...])
        gx_ref[...] = dx1_ref[...] + dxn

        @pl.when(i == 0)
        def _():
            acc_ref[...] = jnp.zeros_like(acc_ref)

        acc_ref[pl.ds(0, 1), :] += dsh
        acc_ref[pl.ds(1, 1), :] += dsc
        acc_ref[pl.ds(2, 1), :] += dgain

    return _pallas(
        body, [x, dh, dx1, mod, gain], dep=dep, grid=(t // tr,),
        in_specs=[_row_spec(tr, d)] * 3 + [_full_spec(mod.shape), _full_spec(gain.shape)],
        out_specs=[_row_spec(tr, d), _full_spec((SUBLANES, d))],
        out_shape=[jax.ShapeDtypeStruct((t, d), F32), jax.ShapeDtypeStruct((SUBLANES, d), F32)],
        sem=("arbitrary",), name="pre_mix_bwd")


def branch_out_merge(attn_o, s_conv, w_attn, w_conv, p, off_ga, off_gc, dep=None):
    t = attn_o.shape[0]
    j, ka, nj = w_attn.shape
    kc = w_conv.shape[1]
    assert off_ga % nj == 0 and off_gc % nj == 0

    def body(a_ref, s_ref, wa_ref, wc_ref, ga_ref, gc_ref, ya_ref, yc_ref, m_ref):
        for s, sz in _row_chunks(t):
            rows = pl.ds(s, sz)
            ya = jnp.dot(a_ref[rows, :], wa_ref[...], preferred_element_type=F32)
            yc = jnp.dot(s_ref[rows, :], wc_ref[...], preferred_element_type=F32)
            ya_ref[rows, :] = ya.astype(BF16)
            yc_ref[rows, :] = yc.astype(BF16)
            m_ref[rows, :] = (_sigmoid(ga_ref[rows, :]) * ya + _sigmoid(gc_ref[rows, :]) * yc).astype(BF16)

    col = pl.BlockSpec((t, nj), lambda b: (0, b))
    return _pallas(
        body, [attn_o, s_conv, w_attn, w_conv, p, p], dep=dep, grid=(j,),
        in_specs=[pl.BlockSpec((t, ka), lambda b: (0, 0)), pl.BlockSpec((t, kc), lambda b: (0, 0)),
                  pl.BlockSpec((None, ka, nj), lambda b: (b, 0, 0)), pl.BlockSpec((None, kc, nj), lambda b: (b, 0, 0)),
                  pl.BlockSpec((t, nj), lambda b: (0, off_ga // nj + b)),
                  pl.BlockSpec((t, nj), lambda b: (0, off_gc // nj + b))],
        out_specs=[col] * 3,
        out_shape=[jax.ShapeDtypeStruct((t, j * nj), BF16)] * 3,
        sem=("parallel",), name="branch_out_merge")


def mix_out_dx_merge_bwd(dom, w_mix, p, y_attn, y_conv, off_ga, off_gc, dep=None):
    t, d = y_attn.shape
    cw = math.gcd(math.gcd(off_ga, off_gc), math.gcd(d, TILE_N // 2))

    def body(dom_ref, w_ref, ga_ref, gc_ref, ya_ref, yc_ref, dya_ref, dyc_ref, dga_ref, dgc_ref):
        for s, sz in _row_chunks(t):
            rows = pl.ds(s, sz)
            dm = lax.dot_general(dom_ref[rows, :], w_ref[...], (((1,), (1,)), ((), ())), preferred_element_type=F32)
            sa = _sigmoid(ga_ref[rows, :])
            sc = _sigmoid(gc_ref[rows, :])
            dya_ref[rows, :] = (dm * sa).astype(BF16)
            dyc_ref[rows, :] = (dm * sc).astype(BF16)
            dga_ref[rows, :] = (dm * ya_ref[rows, :] * sa * (1.0 - sa)).astype(BF16)
            dgc_ref[rows, :] = (dm * yc_ref[rows, :] * sc * (1.0 - sc)).astype(BF16)

    col = pl.BlockSpec((t, cw), lambda j: (0, j))
    return _pallas(
        body, [dom, w_mix, p, p, y_attn, y_conv], dep=dep, grid=(d // cw,),
        in_specs=[pl.BlockSpec((t, d), lambda j: (0, 0)), pl.BlockSpec((cw, d), lambda j: (j, 0)),
                  pl.BlockSpec((t, cw), lambda j: (0, off_ga // cw + j)),
                  pl.BlockSpec((t, cw), lambda j: (0, off_gc // cw + j)), col, col],
        out_specs=[col] * 4,
        out_shape=[jax.ShapeDtypeStruct((t, d), BF16)] * 4,
        sem=("parallel",), name="mm_mix_out_dx_merge_bwd")


def ffn_perm(n_blocks):
    half = n_blocks // 2
    return tuple(2 * j if j < half else 2 * (j - half) + 1 for j in range(n_blocks))


def swiglu_fwd(f, nj, dep=None):
    t, two = f.shape
    tr = _tile(t, ROW_TILE, SUBLANES)
    npair = two // (2 * nj)

    def body(f_ref, o_ref):
        g = f_ref[:, :nj].astype(F32)
        u = f_ref[:, nj:].astype(F32)
        o_ref[...] = (g * _sigmoid(g) * u).astype(BF16)

    return _pallas(
        body, [f], dep=dep, grid=(t // tr, npair),
        in_specs=[pl.BlockSpec((tr, 2 * nj), lambda i, j: (i, j))],
        out_specs=pl.BlockSpec((tr, nj), lambda i, j: (i, j)),
        out_shape=jax.ShapeDtypeStruct((t, two // 2), BF16),
        sem=("parallel", "parallel"), name="swiglu_fwd")


def swiglu_bwd(f, dact, nj, dep=None):
    t, two = f.shape
    tr = _tile(t, ROW_TILE, SUBLANES)
    npair = two // (2 * nj)

    def body(f_ref, da_ref, o_ref):
        g = f_ref[:, :nj].astype(F32)
        u = f_ref[:, nj:].astype(F32)
        da = da_ref[...]
        s = _sigmoid(g)
        o_ref[:, :nj] = (da * u * (s * (1.0 + g * (1.0 - s)))).astype(BF16)
        o_ref[:, nj:] = (da * (g * s)).astype(BF16)

    return _pallas(
        body, [f, dact], dep=dep, grid=(t // tr, npair),
        in_specs=[pl.BlockSpec((tr, 2 * nj), lambda i, j: (i, j)), pl.BlockSpec((tr, nj), lambda i, j: (i, j))],
        out_specs=pl.BlockSpec((tr, 2 * nj), lambda i, j: (i, j)),
        out_shape=jax.ShapeDtypeStruct((t, two), BF16),
        sem=("parallel", "parallel"), name="swiglu_bwd")


def _t5_bucket_table():
    q_off = np.arange(BLOCK)
    k_off = np.arange(2 * BLOCK)
    dist = q_off[:, None] + BLOCK - k_off[None, :]
    n = np.maximum(dist, 0)
    nf = np.maximum(n, 1).astype(np.float32)
    large = MAX_EXACT + (np.log(nf / np.float32(MAX_EXACT)) / np.float32(math.log(MAX_DISTANCE / MAX_EXACT))
                         * np.float32(NUM_BUCKETS - MAX_EXACT)).astype(np.int32)
    large = np.minimum(large, NUM_BUCKETS - 1)
    bucket = np.where(n < MAX_EXACT, n, large).astype(np.int32)
    allowed = (dist >= 0) & (dist < WINDOW)
    return np.where(allowed, bucket, -1).astype(np.int32)


def bias_table(rel_bias, bucket_p, bucket_c, dep=None):
    nb, nq = rel_bias.shape

    def body(rb_ref, bkp_ref, bkc_ref, op_ref, oc_ref):
        for bk_ref, o_ref in ((bkp_ref, op_ref), (bkc_ref, oc_ref)):
            bk = bk_ref[...]
            for h in range(nq):
                acc = jnp.full(bk.shape, -jnp.inf, F32)
                for b in range(nb):
                    acc = jnp.where(bk == b, rb_ref[b, h], acc)
                o_ref[h] = acc

    return _pallas(
        body, [rel_bias, bucket_p, bucket_c], dep=dep,
        in_specs=[_SMEM, _VMEM, _VMEM], out_specs=[_VMEM, _VMEM],
        out_shape=[jax.ShapeDtypeStruct((nq,) + bucket_p.shape, F32)] * 2,
        name="bias_table")


def bias_table_bwd(dbp, dbc, bucket_p, bucket_c, dep=None):
    nq = dbp.shape[0]

    def body(dbp_ref, dbc_ref, bkp_ref, bkc_ref, o_ref):
        bkp, bkc = bkp_ref[...][None], bkc_ref[...][None]
        dp, dc = dbp_ref[...], dbc_ref[...]
        for b in range(NUM_BUCKETS):
            sel = jnp.where(bkp == b, dp, 0.0) + jnp.where(bkc == b, dc, 0.0)
            o_ref[b] = jnp.sum(jnp.sum(sel, axis=2, keepdims=True), axis=1, keepdims=True)

    return _pallas(
        body, [dbp, dbc, bucket_p, bucket_c], dep=dep,
        in_specs=[_VMEM] * 4, out_specs=_VMEM,
        out_shape=jax.ShapeDtypeStruct((NUM_BUCKETS, nq, 1, 1), F32),
        name="bias_table_bwd")


_BNT = (((2,), (2,)), ((0,), (0,)))
_BNN = (((2,), (1,)), ((0,), (0,)))
_BTN = (((1,), (1,)), ((0,), (0,)))


@jax.custom_vjp
def _bdot_nt(a, b):
    return lax.dot_general(a.astype(BF16), b.astype(BF16), _BNT, preferred_element_type=F32)


def _bdot_nt_fwd(a, b):
    return _bdot_nt(a, b), (a, b)


def _bdot_nt_bwd(res, g):
    a, b = res
    gb = g.astype(BF16)
    da = lax.dot_general(gb, b.astype(BF16), _BNN, preferred_element_type=F32)
    db = lax.dot_general(gb, a.astype(BF16), _BTN, preferred_element_type=F32)
    return da, db


_bdot_nt.defvjp(_bdot_nt_fwd, _bdot_nt_bwd)


@jax.custom_vjp
def _bdot_nn(a, b):
    return lax.dot_general(a.astype(BF16), b.astype(BF16), _BNN, preferred_element_type=F32)


def _bdot_nn_fwd(a, b):
    return _bdot_nn(a, b), (a, b)


def _bdot_nn_bwd(res, g):
    a, b = res
    gb = g.astype(BF16)
    da = lax.dot_general(gb, b.astype(BF16), _BNT, preferred_element_type=F32)
    db = lax.dot_general(a.astype(BF16), gb, _BTN, preferred_element_type=F32)
    return da, db


_bdot_nn.defvjp(_bdot_nn_fwd, _bdot_nn_bwd)


def _attn_math(q, kp, kc, vp, vc, bp, bc, sinks, qg, kg, *, prev_ok, scale):
    h, rows, _ = q.shape
    b = kp.shape[1]
    qn = q * _rms(q) * qg
    kpn = kp * _rms(kp) * kg
    kcn = kc * _rms(kc) * kg
    lp = _bdot_nt(qn, kpn) * scale + bp.reshape(h, rows, b)
    lc = _bdot_nt(qn, kcn) * scale + bc.reshape(h, rows, b)
    lp = jnp.where(prev_ok, lp, -jnp.inf)
    sink = jnp.broadcast_to(sinks, (sinks.shape[0], b, 1)).reshape(h, rows, 1)
    m = jnp.maximum(jnp.maximum(jnp.max(lp, axis=-1, keepdims=True), jnp.max(lc, axis=-1, keepdims=True)), sink)
    m = lax.stop_gradient(m)
    pp = jnp.exp(lp - m)
    pc = jnp.exp(lc - m)
    den = jnp.sum(pp, axis=-1, keepdims=True) + jnp.sum(pc, axis=-1, keepdims=True) + jnp.exp(sink - m)
    inv = 1.0 / den
    return _bdot_nn(pp * inv, vp) + _bdot_nn(pc * inv, vc)


def _attn_specs(p, aw, kvw, nq, hd, nblk, reverse):
    assert aw % (2 * kvw) == 0
    kv_col = aw // (2 * kvw)

    def blk(n):
        return nblk - 1 - n if reverse else n

    return [
        pl.BlockSpec((BLOCK, aw), lambda n: (blk(n), 0)),
        pl.BlockSpec((BLOCK, 2 * kvw), lambda n: (jnp.maximum(blk(n) - 1, 0), kv_col)),
        pl.BlockSpec((BLOCK, 2 * kvw), lambda n: (blk(n), kv_col)),
        _full_spec((nq, BLOCK, BLOCK)), _full_spec((nq, BLOCK, BLOCK)), _full_spec((nq, 1, 1)),
        _full_spec((1, hd)), _full_spec((1, hd)),
    ]


def _head_major(ref, n_heads, grp, hd, offset=0):
    return jnp.stack([
        jnp.concatenate([ref[:, pl.ds(offset + (grp * h + g) * hd, hd)].astype(F32) for g in range(grp)], axis=0)
        for h in range(n_heads)])


def _attn_inputs(nkv, grp, hd, kvw, q_ref, kvp_ref, kvc_ref):
    return (_head_major(q_ref, nkv, grp, hd), _head_major(kvp_ref, nkv, 1, hd), _head_major(kvc_ref, nkv, 1, hd),
            _head_major(kvp_ref, nkv, 1, hd, kvw), _head_major(kvc_ref, nkv, 1, hd, kvw))


def attn_fwd(p, bias_p, bias_c, sinks, qg, kg, *, aw, kvw, dep=None):
    t, hd = p.shape[0], qg.shape[-1]
    nq, nkv, nblk = aw // hd, kvw // hd, t // BLOCK
    grp = nq // nkv
    scale = hd ** -0.5

    def body(q_ref, kvp_ref, kvc_ref, bp_ref, bc_ref, s_ref, qg_ref, kg_ref, o_ref):
        prev_ok = pl.program_id(0) > 0
        out = _attn_math(*_attn_inputs(nkv, grp, hd, kvw, q_ref, kvp_ref, kvc_ref), bp_ref[...], bc_ref[...],
                         s_ref[...], qg_ref[...], kg_ref[...], prev_ok=prev_ok, scale=scale)
        for h in range(nkv):
            for g in range(grp):
                o_ref[:, pl.ds((grp * h + g) * hd, hd)] = out[h, g * BLOCK:(g + 1) * BLOCK].astype(BF16)

    return _pallas(
        body, [p, p, p, bias_p, bias_c, sinks, qg, kg], dep=dep, grid=(nblk,),
        in_specs=_attn_specs(p, aw, kvw, nq, hd, nblk, False),
        out_specs=pl.BlockSpec((BLOCK, aw), lambda n: (n, 0)),
        out_shape=jax.ShapeDtypeStruct((t, aw), BF16),
        sem=("parallel",), name="attn_fwd")


def attn_bwd(p, bias_p, bias_c, sinks, qg, kg, do, *, aw, kvw, dep=None):
    t, hd = p.shape[0], qg.shape[-1]
    nq, nkv, nblk = aw // hd, kvw // hd, t // BLOCK
    grp = nq // nkv
    scale = hd ** -0.5

    def body(q_ref, kvp_ref, kvc_ref, bp_ref, bc_ref, s_ref, qg_ref, kg_ref, do_ref,
             dqkv_ref, dbp_ref, dbc_ref, ds_ref, dqg_ref, dkg_ref, carry):
        i = pl.program_id(0)
        prev_ok = (nblk - 1 - i) > 0

        @pl.when(i == 0)
        def _():
            carry[...] = jnp.zeros_like(carry)
            dbp_ref[...] = jnp.zeros_like(dbp_ref)
            dbc_ref[...] = jnp.zeros_like(dbc_ref)
            ds_ref[...] = jnp.zeros_like(ds_ref)
            dqg_ref[...] = jnp.zeros_like(dqg_ref)
            dkg_ref[...] = jnp.zeros_like(dkg_ref)

        fn = functools.partial(_attn_math, prev_ok=prev_ok, scale=scale)
        _, vjp = jax.vjp(fn, *_attn_inputs(nkv, grp, hd, kvw, q_ref, kvp_ref, kvc_ref), bp_ref[...], bc_ref[...],
                         s_ref[...], qg_ref[...], kg_ref[...])
        dq, dkp, dkc, dvp, dvc, dbp, dbc, dsk, dqg, dkg = vjp(_head_major(do_ref, nkv, grp, hd))
        for h in range(nkv):
            for g in range(grp):
                dqkv_ref[:, pl.ds((grp * h + g) * hd, hd)] = dq[h, g * BLOCK:(g + 1) * BLOCK].astype(BF16)
            k_cols, v_cols = pl.ds(h * hd, hd), pl.ds(kvw + h * hd, hd)
            dqkv_ref[:, pl.ds(aw + h * hd, hd)] = (dkc[h] + carry[:, k_cols]).astype(BF16)
            dqkv_ref[:, pl.ds(aw + kvw + h * hd, hd)] = (dvc[h] + carry[:, v_cols]).astype(BF16)
            carry[:, k_cols] = dkp[h]
            carry[:, v_cols] = dvp[h]
        dbp_ref[...] += dbp
        dbc_ref[...] += dbc
        ds_ref[...] += dsk
        dqg_ref[...] += dqg
        dkg_ref[...] += dkg

    return _pallas(
        body, [p, p, p, bias_p, bias_c, sinks, qg, kg, do], dep=dep, grid=(nblk,),
        in_specs=_attn_specs(p, aw, kvw, nq, hd, nblk, True)
        + [pl.BlockSpec((BLOCK, aw), lambda n: (nblk - 1 - n, 0))],
        out_specs=[
            pl.BlockSpec((BLOCK, aw + 2 * kvw), lambda n: (nblk - 1 - n, 0)),
            _full_spec((nq, BLOCK, BLOCK)), _full_spec((nq, BLOCK, BLOCK)), _full_spec((nq, 1, 1)),
            _full_spec((1, hd)), _full_spec((1, hd)),
        ],
        out_shape=[
            jax.ShapeDtypeStruct((t, aw + 2 * kvw), BF16),
            jax.ShapeDtypeStruct((nq, BLOCK, BLOCK), F32),
            jax.ShapeDtypeStruct((nq, BLOCK, BLOCK), F32),
            jax.ShapeDtypeStruct((nq, 1, 1), F32),
            jax.ShapeDtypeStruct((1, hd), F32),
            jax.ShapeDtypeStruct((1, hd), F32),
        ],
        scratch=[pltpu.VMEM((BLOCK, 2 * kvw), F32)],
        sem=("arbitrary",), name="attn_bwd")


CONV_TILE = 256


def _conv_halo_specs(tb, ch, nblk):
    per = tb // CONV_HALO
    last = nblk * per - 1
    cur = pl.BlockSpec((tb, ch), lambda n: (n, 0))
    prev = pl.BlockSpec((CONV_HALO, ch), lambda n: (jnp.maximum(n * per - 1, 0), 0))
    nxt = pl.BlockSpec((CONV_HALO, ch), lambda n: (jnp.minimum((n + 1) * per, last), 0))
    return cur, prev, nxt


def _ln_silu(co, ln_g, ln_b):
    mu = jnp.mean(co, axis=-1, keepdims=True)
    cen = co - mu
    rstd = lax.rsqrt(jnp.mean(cen * cen, axis=-1, keepdims=True) + EPS)
    xhat = cen * rstd
    z = xhat * ln_g + ln_b
    return xhat, rstd, z


def _shifted_copies(src, shifted):
    rows = src.shape[0] - SUBLANES
    for r in range(1, SUBLANES):
        shifted[r, pl.ds(0, rows), :] = src[pl.ds(r, rows), :]


def _rows_from(src, shifted, start, n):
    r = start % SUBLANES
    if r == 0:
        return src[pl.ds(start, n), :]
    return shifted[r, pl.ds(start - r, n), :]


def conv_fwd(ca, cb, conv_w, conv_b, ln_g, ln_b, dep=None):
    t, ch = ca.shape
    tb = _tile(t, CONV_TILE, CONV_HALO)
    nblk = t // tb
    cur, prev, _ = _conv_halo_specs(tb, ch, nblk)
    lead = CONV_HALO - (CONV_WIDTH - 1)

    def body(ca_ref, cb_ref, cap_ref, cbp_ref, w_ref, b_ref, g_ref, bb_ref, s_ref, co_ref, ubuf, ushift):
        n = pl.program_id(0)
        halo = cap_ref[...] * _sigmoid(cbp_ref[...])
        ubuf[pl.ds(0, CONV_HALO), :] = jnp.where(n > 0, halo, 0.0)
        ubuf[pl.ds(CONV_HALO, tb), :] = ca_ref[...] * _sigmoid(cb_ref[...])
        _shifted_copies(ubuf, ushift)
        acc = jnp.broadcast_to(b_ref[...], (tb, ch))
        for k in range(CONV_WIDTH):
            acc = acc + w_ref[pl.ds(k, 1), :] * _rows_from(ubuf, ushift, lead + k, tb)
        co_ref[...] = acc
        _, _, z = _ln_silu(acc, g_ref[...], bb_ref[...])
        s_ref[...] = (z * _sigmoid(z)).astype(BF16)

    vec = _full_spec((1, ch))
    return _pallas(
        body, [ca, cb, ca, cb, conv_w, conv_b, ln_g, ln_b], dep=dep, grid=(nblk,),
        in_specs=[cur, cur, prev, prev, _full_spec(conv_w.shape), vec, vec, vec],
        out_specs=[cur, cur],
        out_shape=[jax.ShapeDtypeStruct((t, ch), BF16), jax.ShapeDtypeStruct((t, ch), F32)],
        scratch=[pltpu.VMEM((CONV_HALO + tb, ch), F32), pltpu.VMEM((SUBLANES, CONV_HALO + tb, ch), F32)],
        sem=("parallel",), name="conv_fwd")


def conv_bwd(ca, cb, co, ds, conv_w, ln_g, ln_b, dep=None):
    t, ch = ca.shape
    tb = _tile(t, CONV_TILE, CONV_HALO)
    nblk = t // tb
    cur, prev, nxt = _conv_halo_specs(tb, ch, nblk)
    lead = CONV_HALO - (CONV_WIDTH - 1)
    ext = tb + CONV_HALO

    def body(ca_ref, cb_ref, cap_ref, cbp_ref, co_ref, con_ref, ds_ref, dsn_ref, w_ref, g_ref, bb_ref,
             dca_ref, dcb_ref, dw_ref, dvec_ref, ubuf, dbuf, ushift, dshift):
        n = pl.program_id(0)
        is_last = n == nblk - 1
        sig_b = _sigmoid(cb_ref[...])
        cav = ca_ref[...].astype(F32)
        ubuf[pl.ds(0, CONV_HALO), :] = jnp.where(n > 0, cap_ref[...] * _sigmoid(cbp_ref[...]), 0.0)
        ubuf[pl.ds(CONV_HALO, tb), :] = cav * sig_b
        _shifted_copies(ubuf, ushift)
        co = jnp.concatenate([co_ref[...], con_ref[...]], axis=0)
        xhat, rstd, z = _ln_silu(co, g_ref[...], bb_ref[...])
        dsv = jnp.concatenate([ds_ref[...].astype(F32), jnp.where(is_last, 0.0, dsn_ref[...].astype(F32))], axis=0)
        sg = _sigmoid(z)
        dz = dsv * (sg * (1.0 + z * (1.0 - sg)))
        dxh = dz * g_ref[...]
        dco = rstd * (dxh - jnp.mean(dxh, axis=-1, keepdims=True)
                      - xhat * jnp.mean(dxh * xhat, axis=-1, keepdims=True))
        dbuf[...] = dco
        _shifted_copies(dbuf, dshift)

        @pl.when(n == 0)
        def _():
            dw_ref[...] = jnp.zeros_like(dw_ref)
            dvec_ref[...] = jnp.zeros_like(dvec_ref)

        dco_cur = dco[:tb]
        dvec_ref[pl.ds(0, 1), :] += jnp.sum(dco_cur, axis=0, keepdims=True)
        dvec_ref[pl.ds(1, 1), :] += jnp.sum(dz[:tb] * xhat[:tb], axis=0, keepdims=True)
        dvec_ref[pl.ds(2, 1), :] += jnp.sum(dz[:tb], axis=0, keepdims=True)
        du = jnp.zeros((tb, ch), F32)
        for k in range(CONV_WIDTH):
            du = du + w_ref[pl.ds(k, 1), :] * _rows_from(dbuf, dshift, CONV_WIDTH - 1 - k, tb)
            dw_ref[pl.ds(k, 1), :] += jnp.sum(dco_cur * _rows_from(ubuf, ushift, lead + k, tb), axis=0,
                                              keepdims=True)
        dca_ref[...] = (du * sig_b).astype(BF16)
        dcb_ref[...] = (du * cav * sig_b * (1.0 - sig_b)).astype(BF16)

    vec = _full_spec((1, ch))
    return _pallas(
        body, [ca, cb, ca, cb, co, co, ds, ds, conv_w, ln_g, ln_b], dep=dep, grid=(nblk,),
        in_specs=[cur, cur, prev, prev, cur, nxt, cur, nxt, _full_spec(conv_w.shape), vec, vec],
        out_specs=[cur, cur, _full_spec(conv_w.shape), _full_spec((SUBLANES, ch))],
        out_shape=[jax.ShapeDtypeStruct((t, ch), BF16), jax.ShapeDtypeStruct((t, ch), BF16),
                   jax.ShapeDtypeStruct(conv_w.shape, F32), jax.ShapeDtypeStruct((SUBLANES, ch), F32)],
        scratch=[pltpu.VMEM((CONV_HALO + tb, ch), F32), pltpu.VMEM((ext, ch), F32),
                 pltpu.VMEM((SUBLANES, CONV_HALO + tb, ch), F32), pltpu.VMEM((SUBLANES, ext, ch), F32)],
        sem=("arbitrary",), name="conv_bwd")


def ada_fwd(c_t, w_ada, dep=None):
    d, nc = w_ada.shape
    nex = c_t.shape[1]
    tn = _tile(nc, TILE_N)

    def body(ct_ref, w_ref, o_ref):
        w = w_ref[...]
        ct = ct_ref[...]
        cact = ct * _sigmoid(ct)
        rows = [jnp.sum(w * cact[:, b:b + 1], axis=0, keepdims=True) for b in range(nex)]
        o_ref[...] = jnp.concatenate(rows, axis=0)

    return _pallas(
        body, [c_t, w_ada], dep=dep, grid=(nc // tn,),
        in_specs=[_full_spec(c_t.shape), pl.BlockSpec((d, tn), lambda j: (0, j))],
        out_specs=pl.BlockSpec((nex, tn), lambda j: (0, j)),
        out_shape=jax.ShapeDtypeStruct((nex, nc), F32),
        sem=("parallel",), name="ada_fwd")


def _adamw_math(w, g, m, v):
    m = ADAM_B1 * m + (1.0 - ADAM_B1) * g
    v = ADAM_B2 * v + (1.0 - ADAM_B2) * (g * g)
    m_hat = m / (1.0 - ADAM_B1 ** ADAM_STEP)
    v_hat = v / (1.0 - ADAM_B2 ** ADAM_STEP)
    delta = -ADAM_LR * (m_hat / (jnp.sqrt(v_hat) + ADAM_EPS) + ADAM_WD * w)
    return delta, m, v


def adamw(w, g, m, v, name, copy_grad=False, dep=None):
    r, n = w.shape
    tr, tn = _ew_tiles(r, n, elems=EW_ELEMS_MANY, min_steps=EW_MIN_STEPS_MANY)
    n_out = 4 if copy_grad else 3

    def body(w_ref, g_ref, m_ref, v_ref, *outs):
        g = g_ref[...]
        if copy_grad:
            outs[0][...] = g
        outs[-3][...], outs[-2][...], outs[-1][...] = _adamw_math(w_ref[...], g, m_ref[...], v_ref[...])

    blk = pl.BlockSpec((tr, tn), lambda i, j: (i, j))
    return _pallas(
        body, [w, g, m, v], dep=dep, grid=(r // tr, n // tn),
        in_specs=[blk] * 4, out_specs=[blk] * n_out,
        out_shape=[jax.ShapeDtypeStruct((r, n), F32)] * n_out,
        sem=("parallel", "parallel"), name=name)


def ada_grad_adamw(c_t, dmod_cols, w, m, v, dep=None):
    d, nc = w.shape
    nex = c_t.shape[1]
    tr, tn = _ew_tiles(d, nc, elems=EW_ELEMS_MANY, min_steps=EW_MIN_STEPS_MANY)

    def body(ct_ref, dm_ref, w_ref, m_ref, v_ref, g_ref, d_ref, nm_ref, nv_ref):
        ct = ct_ref[...]
        cact = ct * _sigmoid(ct)
        dm = dm_ref[...]
        g = cact[:, 0:1] * dm[0:1, :]
        for b in range(1, nex):
            g = g + cact[:, b:b + 1] * dm[b:b + 1, :]
        g_ref[...] = g
        d_ref[...], nm_ref[...], nv_ref[...] = _adamw_math(w_ref[...], g, m_ref[...], v_ref[...])

    blk = pl.BlockSpec((tr, tn), lambda i, j: (i, j))
    return _pallas(
        body, [c_t, dmod_cols, w, m, v], dep=dep, grid=(d // tr, nc // tn),
        in_specs=[pl.BlockSpec((tr, nex), lambda i, j: (i, 0)), pl.BlockSpec((nex, tn), lambda i, j: (0, j)),
                  blk, blk, blk],
        out_specs=[blk] * 4,
        out_shape=[jax.ShapeDtypeStruct((d, nc), F32)] * 4,
        sem=("parallel", "parallel"), name="ada_grad_adamw")


def _row_pack(parts):
    cols, offs, off = [], [], 0
    for p in parts:
        n = p.shape[1]
        width = -(-n // LANES) * LANES
        cols.append(jnp.pad(p, ((0, 0), (0, width - n))) if width != n else p)
        offs.append(off)
        off += width
    return jnp.concatenate(cols, axis=1), offs


def small_sum_adamw(gathered, offs, ws, ms, vs, extra_widths, dep=None):
    ndev = gathered.shape[0]
    npar = len(ws)

    def body(ga_ref, *refs):
        w_refs, m_refs, v_refs = refs[:npar], refs[npar:2 * npar], refs[2 * npar:3 * npar]
        outs = refs[3 * npar:]
        tot = ga_ref[0]
        for s in range(1, ndev):
            tot = tot + ga_ref[s]
        for i in range(npar):
            n = ws[i].shape[1]
            g = tot[:, offs[i]:offs[i] + n]
            outs[4 * i][...] = g
            outs[4 * i + 1][...], outs[4 * i + 2][...], outs[4 * i + 3][...] = _adamw_math(
                w_refs[i][...], g, m_refs[i][...], v_refs[i][...])
        for e, n in enumerate(extra_widths):
            off = offs[npar + e]
            outs[4 * npar + e][...] = tot[:, off:off + n]

    shapes = [jax.ShapeDtypeStruct(w.shape, F32) for w in ws for _ in range(4)]
    shapes += [jax.ShapeDtypeStruct((1, n), F32) for n in extra_widths]
    return _pallas(
        body, [gathered, *ws, *ms, *vs], dep=dep, in_specs=[_VMEM] * (1 + 3 * npar), out_specs=[_VMEM] * len(shapes),
        out_shape=shapes, name="small_sum_adamw")


def _position():
    return lax.axis_index("x"), lax.axis_index("y"), lax.axis_index("c")


def _other_chips(x, y):
    return [(1 - x, y), (x, 1 - y), (1 - x, 1 - y)]


def allgather_small(block, name, dep=None):
    def body(x_ref, out_ref, send_sems, recv_sems, local_sem):
        x, y, c = _position()
        me, sibling = (x, y, c), (x, y, 1 - c)
        chips = _other_chips(x, y)

        def slot(px, py, pc):
            return out_ref.at[4 * px + 2 * py + pc]

        def copy(k, block_of, to, src=None):
            return pltpu.make_async_remote_copy(
                src_ref=slot(*block_of) if src is None else src, dst_ref=slot(*block_of),
                send_sem=send_sems.at[k], recv_sem=recv_sems.at[k], device_id=to, device_id_type=MESH)

        mine = pltpu.make_async_copy(x_ref, slot(*me), local_sem)
        mine.start()
        first = [copy(0, me, sibling, src=x_ref)]
        first += [copy(1 + j, me, (*chip, c), src=x_ref) for j, chip in enumerate(chips)]
        for cp in first:
            cp.start()
        passed = [copy(4 + j, (*chip, c), sibling) for j, chip in enumerate(chips)]
        for j, chip in enumerate(chips):
            copy(1 + j, (*chip, c), me).wait_recv()
            passed[j].start()
        copy(0, sibling, me).wait_recv()
        for j, chip in enumerate(chips):
            copy(4 + j, (*chip, 1 - c), me).wait_recv()
        for cp in first + passed:
            cp.wait_send()
        mine.wait()

    return _pallas(
        body, [block], dep=dep,
        out_shape=jax.ShapeDtypeStruct((N_DEV, *block.shape), block.dtype),
        in_specs=[_VMEM], out_specs=_VMEM,
        scratch=[pltpu.SemaphoreType.DMA((7,)), pltpu.SemaphoreType.DMA((7,)), pltpu.SemaphoreType.DMA],
        name=name)


class Started(NamedTuple):
    send_sems: Any
    recv_sems: Any
    bufs: list


def exchange_start_many(name, buf_sets, plans, dep=None):
    sizes = [len(bufs) for bufs in buf_sets]
    first = [sum(sizes[:i]) for i in range(len(sizes))]
    flat = [b for bufs in buf_sets for b in bufs]
    nb, npl = len(flat), len(plans)

    def body(*refs):
        for i, (s, _, plan) in enumerate(plans):
            for cp in plan(refs[first[s]:first[s] + sizes[s]], refs[nb + 2 * i], refs[nb + 2 * i + 1]):
                cp.start()

    sems = [pltpu.SemaphoreType.DMA((n,)) for _, n, _ in plans for _ in range(2)]
    outs = _pallas(
        body, [pltpu.with_memory_space_constraint(b, pltpu.HBM) for b in flat], dep=dep, name=name,
        out_shape=(*sems, *[pltpu.HBM(b.shape, b.dtype) for b in flat]),
        in_specs=[_HBM] * nb,
        out_specs=(*[_SEM] * (2 * npl), *[_HBM] * nb),
        input_output_aliases={i: 2 * npl + i for i in range(nb)},
        compiler_params=pltpu.CompilerParams(has_side_effects=_EFFECT))
    new_bufs = outs[2 * npl:]
    return [Started(outs[2 * i], outs[2 * i + 1], list(new_bufs[first[s]:first[s] + sizes[s]]))
            for i, (s, _, _) in enumerate(plans)]


def exchange_start(name, bufs, n_copies, plan, dep=None):
    return exchange_start_many(name, [bufs], [(0, n_copies, plan)], dep=dep)[0]


def exchange_wait(name, started, plan, bufs=None, dep=None):
    if bufs is not None:
        started = started._replace(bufs=list(bufs))
    nb = len(started.bufs)

    def body(*refs):
        for cp in plan(refs[:nb], refs[nb], refs[nb + 1]):
            cp.wait_send()
            cp.wait_recv()

    outs = _pallas(
        body, [*started.bufs, started.send_sems, started.recv_sems], dep=dep, name=name,
        out_shape=tuple(pltpu.HBM(b.shape, b.dtype) for b in started.bufs),
        in_specs=[_HBM] * nb + [_SEM, _SEM],
        out_specs=tuple([_HBM] * nb),
        input_output_aliases={i: i for i in range(nb)},
        compiler_params=pltpu.CompilerParams(has_side_effects=_EFFECT))
    return list(outs)


def _remote(src, dst, send_sems, recv_sems, i, to):
    return pltpu.make_async_remote_copy(src_ref=src, dst_ref=dst, send_sem=send_sems.at[i], recv_sem=recv_sems.at[i],
                                        device_id=to, device_id_type=MESH)


def _half_rows(buf_rows, chip_idx, pc):
    half = buf_rows // (2 * N_CHIPS)
    return pl.ds((2 * chip_idx + pc) * half, half)


ALL_PEERS = (0, 1, 2)


def plan_gather_ici(refs, send_sems, recv_sems, peers=ALL_PEERS):
    x, y, c = _position()
    chips = _other_chips(x, y)
    copies = []
    for k, ref in enumerate(refs):
        rows = ref.at[_half_rows(ref.shape[0], 2 * x + y, c), :]
        for i, j in enumerate(peers):
            copies.append(_remote(rows, rows, send_sems, recv_sems, len(peers) * k + i, (*chips[j], c)))
    return copies


def plan_gather_relay(refs, send_sems, recv_sems):
    x, y, c = _position()
    copies = []
    for k, ref in enumerate(refs):
        quarter = ref.shape[0] // (4 * N_CHIPS)
        for i, (src_chip, to) in enumerate((((1 - x, y), (x, 1 - y, c)), ((x, 1 - y), (1 - x, y, c)))):
            start = (2 * (2 * src_chip[0] + src_chip[1]) + c) * 2 * quarter + i * quarter
            rows = ref.at[pl.ds(start, quarter), :]
            copies.append(_remote(rows, rows, send_sems, recv_sems, 2 * k + i, to))
    return copies


def plan_gather_d2d(refs, send_sems, recv_sems, peers=ALL_PEERS):
    x, y, c = _position()
    chips = _other_chips(x, y)
    copies = []
    for k, ref in enumerate(refs):
        for i, j in enumerate(peers):
            px, py = chips[j]
            rows = ref.at[_half_rows(ref.shape[0], 2 * px + py, c), :]
            copies.append(_remote(rows, rows, send_sems, recv_sems, len(peers) * k + i, (x, y, 1 - c)))
    return copies


def plan_pair_exchange(refs, send_sems, recv_sems):
    x, y, c = _position()
    nw = len(refs) // 2
    copies = []
    for k in range(nw):
        for chip in range(N_CHIPS):
            copies.append(_remote(refs[k].at[chip, 1 - c], refs[nw + k].at[chip], send_sems, recv_sems,
                                  N_CHIPS * k + chip, (x, y, 1 - c)))
    return copies


def plan_chip_exchange(refs, send_sems, recv_sems):
    x, y, c = _position()
    nw = len(refs) // 2
    copies = []
    for k in range(nw):
        for j, (px, py) in enumerate(_other_chips(x, y)):
            copies.append(_remote(refs[k].at[2 * px + py], refs[nw + k].at[2 * x + y], send_sems, recv_sems,
                                  3 * k + j, (px, py, c)))
    return copies


def plan_pair_share(refs, send_sems, recv_sems):
    x, y, c = _position()
    return [_remote(ref.at[c], ref.at[c], send_sems, recv_sems, k, (x, y, 1 - c)) for k, ref in enumerate(refs)]


def cast_into_slot(src, slot, n_slots, name, dep=None):
    r, n = src.shape
    tr, tn = _ew_tiles(r, n, BF16_SUBLANES)

    def body(slot_ref, s_ref, o_ref):
        o_ref[...] = s_ref[...].astype(BF16)

    return _pallas(
        body, [slot, src], dep=dep, n_prefetch=1, grid=(r // tr, n // tn),
        in_specs=[pl.BlockSpec((tr, tn), lambda i, j, sl: (i, j))],
        out_specs=pl.BlockSpec((None, tr, tn), lambda i, j, sl: (sl[0], i, j)),
        out_shape=jax.ShapeDtypeStruct((n_slots, r, n), BF16),
        sem=("parallel", "parallel"), name=name)


def pair_sum(g, r, core, name, dep=None):
    nchip, _, h, n = g.shape
    th, tn = _ew_tiles(h, n, BF16_SUBLANES)

    def body(core_ref, g_ref, r_ref, o_ref):
        o_ref[...] = (g_ref[...].astype(F32) + r_ref[...].astype(F32)).astype(BF16)

    return _pallas(
        body, [core, g, r], dep=dep, n_prefetch=1, grid=(nchip, h // th, n // tn),
        in_specs=[pl.BlockSpec((None, None, th, tn), lambda a, i, j, cr: (a, cr[0], i, j)),
                  pl.BlockSpec((None, th, tn), lambda a, i, j, cr: (a, i, j))],
        out_specs=pl.BlockSpec((None, th, tn), lambda a, i, j, cr: (a, i, j)),
        out_shape=jax.ShapeDtypeStruct((nchip, h, n), BF16),
        sem=("parallel", "parallel", "parallel"), name=name)


def chip_sum(own, got, where, name, dep=None):
    nchip, h, n = got.shape
    th, tn = _ew_tiles(h, n, BF16_SUBLANES, elems=EW_ELEMS_MANY, min_steps=EW_MIN_STEPS_MANY)

    def body(where_ref, own_ref, *rest):
        got_refs, o_ref = rest[:nchip], rest[nchip]
        chip = where_ref[0]
        acc = None
        for s in range(nchip):
            term = jnp.where(chip == s, own_ref[...], got_refs[s][...]).astype(F32)
            acc = term if acc is None else acc + term
        o_ref[...] = acc

    def got_spec(s):
        return pl.BlockSpec((None, th, tn), lambda i, j, wr: (jnp.where(wr[0] == s, (s + 1) % nchip, s), i, j))

    return _pallas(
        body, [where, own, *[got] * nchip], dep=dep, n_prefetch=1, grid=(h // th, n // tn),
        in_specs=[pl.BlockSpec((None, th, tn), lambda i, j, wr: (wr[0], i, j))]
        + [got_spec(s) for s in range(nchip)],
        out_specs=pl.BlockSpec((None, th, tn), lambda i, j, wr: (wr[1], i, j)),
        out_shape=jax.ShapeDtypeStruct((2, h, n), F32),
        sem=("parallel", "parallel"), name=name)


def kernel(x, c, w_ada, b_ada, norm_mix_g, w_in, q_norm_g, k_norm_g, attn_sinks, rel_bias, w_attn_out, conv_w, conv_b, conv_ln_g, conv_ln_b, w_conv_out, w_mix_out, norm_ffn_g, w_ffn_in, w_ffn_out, loss_target, m_w_ada, m_b_ada, m_norm_mix_g, m_w_in, m_q_norm_g, m_k_norm_g, m_attn_sinks, m_rel_bias, m_w_attn_out, m_conv_w, m_conv_b, m_conv_ln_g, m_conv_ln_b, m_w_conv_out, m_w_mix_out, m_norm_ffn_g, m_w_ffn_in, m_w_ffn_out, v_w_ada, v_b_ada, v_norm_mix_g, v_w_in, v_q_norm_g, v_k_norm_g, v_attn_sinks, v_rel_bias, v_w_attn_out, v_conv_w, v_conv_b, v_conv_ln_g, v_conv_ln_b, v_w_conv_out, v_w_mix_out, v_norm_ffn_g, v_w_ffn_in, v_w_ffn_out):
    run = InOrder()
    xi, yi, ci = _position()
    chip = 2 * xi + yi
    me = 2 * chip + ci
    chip_arr = chip.astype(jnp.int32).reshape(1)
    core_arr = ci.astype(jnp.int32).reshape(1)
    where_arr = jnp.stack([chip, ci]).astype(jnp.int32)

    xe, tgt = x[0], loss_target[0]
    t, d = xe.shape
    hd = q_norm_g.shape[-1]
    nq = attn_sinks.shape[-1]
    aw = nq * hd
    ch = conv_b.shape[-1]
    in_width = N_CHIPS * w_in.shape[-1]
    kvw = (in_width - aw - 2 * ch - 2 * d) // 2
    nkv = kvw // hd
    dff = N_CHIPS * w_ffn_out.shape[1]
    off_k, off_v, off_ca = aw, aw + kvw, aw + 2 * kvw
    off_cb, off_ga, off_gc = off_ca + ch, off_ca + 2 * ch, off_ca + 2 * ch + d
    nc_ada = w_ada.shape[-1]
    ch_loc = conv_w.shape[-1]
    nj_ffn = w_ffn_in.shape[-1]
    perm_ffn = ffn_perm(N_CHIPS)

    big = {"w_in": w_in[0], "w_attn_out": w_attn_out[0], "w_conv_out": w_conv_out[0], "w_mix_out": w_mix_out[0],
           "w_ffn_in": w_ffn_in[0], "w_ffn_out": w_ffn_out[0]}
    moments = {"w_in": (m_w_in, v_w_in), "w_attn_out": (m_w_attn_out, v_w_attn_out),
               "w_conv_out": (m_w_conv_out, v_w_conv_out), "w_mix_out": (m_w_mix_out, v_w_mix_out),
               "w_ffn_in": (m_w_ffn_in, v_w_ffn_in), "w_ffn_out": (m_w_ffn_out, v_w_ffn_out)}
    gather_groups = {"in": ["w_in"], "branch_out": ["w_attn_out", "w_conv_out"], "mix_out": ["w_mix_out"],
                     "ffn_in": ["w_ffn_in"], "ffn_out": ["w_ffn_out"]}
    grads, deltas, new_m, new_v = {}, {}, {}, {}

    def gather_cast(gname):
        bufs = []
        for n in gather_groups[gname]:
            r, ncol = big[n].shape
            bufs.append(run(cast_into_slot, big[n], chip_arr, N_CHIPS, "cast_" + n).reshape(N_CHIPS * r, ncol))
        return bufs

    def gather_pass_on(gname, ici):
        landed = run(exchange_wait, "gather_ici_wait_" + gname, ici, plan_gather_ici)
        return run(exchange_start, "gather_d2d_start_" + gname, landed, 3 * len(landed), plan_gather_d2d)

    def gathered(gname, d2d):
        outs = run(exchange_wait, "gather_d2d_wait_" + gname, d2d, plan_gather_d2d)
        return [o.reshape(N_CHIPS, *big[n].shape) for o, n in zip(outs, gather_groups[gname])]

    def rs_pair_start(gname, names, partials):
        blocks = [g.reshape(N_CHIPS, 2, big[n].shape[0] // 2, big[n].shape[1]) for n, g in zip(names, partials)]
        land = [lax.empty((N_CHIPS,) + b.shape[2:], BF16) for b in blocks]
        return run(exchange_start, "pair_exchange_start_" + gname, blocks + land, N_CHIPS * len(blocks),
                   plan_pair_exchange)

    def rs_chip_start(gname, names, pair):
        nw = len(names)
        outs = run(exchange_wait, "pair_exchange_wait_" + gname, pair, plan_pair_exchange)
        sums = [run(pair_sum, g, r, core_arr, "pair_sum_" + n) for n, g, r in zip(names, outs[:nw], outs[nw:])]
        land = [lax.empty(s.shape, BF16) for s in sums]
        return run(exchange_start, "chip_exchange_start_" + gname, sums + land, 3 * nw, plan_chip_exchange)

    def rs_share_start(gname, names, chipx):
        nw = len(names)
        outs = run(exchange_wait, "chip_exchange_wait_" + gname, chipx, plan_chip_exchange)
        halves = [run(chip_sum, s, r, where_arr, "chip_sum_" + n) for n, s, r in zip(names, outs[:nw], outs[nw:])]
        return run(exchange_start, "pair_share_start_" + gname, halves, nw, plan_pair_share)

    def rs_finish(gname, names, share):
        fulls = run(exchange_wait, "pair_share_wait_" + gname, share, plan_pair_share)
        for n, g2 in zip(names, fulls):
            g, dl, nm, nv = run(adamw, big[n], g2.reshape(big[n].shape), moments[n][0][0], moments[n][1][0],
                                "adamw_" + n, copy_grad=True)
            grads[n], deltas[n], new_m[n], new_v[n] = g[None], dl[None], nm[None], nv[None]

    near, far = (0, 1), (2,)
    plan_ici_near = functools.partial(plan_gather_ici, peers=near)
    plan_ici_far, n_far = plan_gather_relay, 2
    plan_d2d_near = functools.partial(plan_gather_d2d, peers=near)
    plan_d2d_far = functools.partial(plan_gather_d2d, peers=far)
    bufs_in = gather_cast("in")
    row1, offs1 = _row_pack([c, conv_w[0].reshape(1, CONV_WIDTH * ch_loc)])
    got1 = run(allgather_small, row1, "allgather_cond")
    ici_near = run(exchange_start, "gather_ici_start_in_near", bufs_in, len(near), plan_ici_near)
    c_all = got1[:, 0, :d]
    conv_w_full = got1[0::2, 0, offs1[1]:offs1[1] + CONV_WIDTH * ch_loc].reshape(N_CHIPS, CONV_WIDTH, ch_loc)
    conv_w_full = jnp.transpose(conv_w_full, (1, 0, 2)).reshape(CONV_WIDTH, ch)
    conv_w_pad = jnp.pad(conv_w_full, ((0, 1), (0, 0)))
    c_t = jnp.transpose(c_all)
    mod_cols = run(ada_fwd, c_t, w_ada[0])
    rest_bufs = {gname: gather_cast(gname) for gname in gather_groups if gname != "in"}
    bucket = _t5_bucket_table()
    bucket_p, bucket_c = jnp.asarray(bucket[:, :BLOCK]), jnp.asarray(bucket[:, BLOCK:])
    bias_p, bias_c = run(bias_table, rel_bias, bucket_p, bucket_c)
    got2 = run(allgather_small, mod_cols, "allgather_mod")
    mod_all = got2.reshape(N_CHIPS, 2, N_DEV, nc_ada)[:, 0]
    mod = lax.dynamic_slice_in_dim(mod_all, me, 1, axis=1).reshape(1, N_CHIPS * nc_ada) + b_ada
    mod = jnp.pad(mod.reshape(N_MOD, d), ((0, SUBLANES - N_MOD), (0, 0)))

    landed = run(exchange_wait, "gather_ici_wait_in_near", ici_near, plan_ici_near)
    ici_far, d2d_near = run(exchange_start_many, "gather_start_in_far", [landed],
                            [(0, n_far, plan_ici_far), (0, len(near), plan_d2d_near)])
    h = run(pre_mix_fwd, xe, mod, norm_mix_g)
    ici = {}
    ici["branch_out"], ici_near_ffn, ici["mix_out"] = run(
        exchange_start_many, "gather_ici_start_mid", [rest_bufs["branch_out"], rest_bufs["ffn_in"], rest_bufs["mix_out"]],
        [(0, 3 * len(rest_bufs["branch_out"]), plan_gather_ici), (1, len(near), plan_ici_near), (2, 3, plan_gather_ici)])
    landed = run(exchange_wait, "gather_d2d_wait_in_near", d2d_near, plan_d2d_near)
    landed = run(exchange_wait, "gather_ici_wait_in_far", ici_far, plan_ici_far, bufs=landed)
    d2d_far = run(exchange_start, "gather_d2d_start_in_far", landed, len(far), plan_d2d_far)
    landed = run(exchange_wait, "gather_d2d_wait_in_far", d2d_far, plan_d2d_far)
    wg_in = landed[0].reshape(N_CHIPS, *big["w_in"].shape)
    p = run(mm_nn, h, wg_in, tn=wg_in.shape[2], tk=d, out_dtype=BF16, name="mm_in")
    d2d_branch = gather_pass_on("branch_out", ici["branch_out"])

    sinks3 = attn_sinks.reshape(nq, 1, 1)
    attn_o = run(attn_fwd, p, bias_p, bias_c, sinks3, q_norm_g, k_norm_g, aw=aw, kvw=kvw)
    ca, cb = p[:, off_ca:off_cb], p[:, off_cb:off_ga]
    s_conv, co_conv = run(conv_fwd, ca, cb, conv_w_pad, conv_b, conv_ln_g, conv_ln_b)
    wg_attn_out, wg_conv_out = gathered("branch_out", d2d_branch)
    landed = run(exchange_wait, "gather_ici_wait_ffn_in_near", ici_near_ffn, plan_ici_near)
    ici_far_ffn, ici["ffn_out"] = run(
        exchange_start_many, "gather_start_ffn_in_far", [landed, rest_bufs["ffn_out"]],
        [(0, n_far, plan_ici_far), (1, 3, plan_gather_ici)])
    y_attn, y_conv, merged = run(branch_out_merge, attn_o, s_conv, wg_attn_out, wg_conv_out, p, off_ga, off_gc)
    landed_mix = run(exchange_wait, "gather_ici_wait_mix_out", ici["mix_out"], plan_gather_ici)
    d2d_mix, d2d_near_ffn = run(
        exchange_start_many, "gather_d2d_start_mix_ffn_in", [landed_mix, ici_far_ffn.bufs],
        [(0, 3 * len(landed_mix), plan_gather_d2d), (1, len(near), plan_d2d_near)])
    (wg_mix_out,) = gathered("mix_out", d2d_mix)
    wg_mix_out = wg_mix_out.reshape(1, d, d)
    o_m = run(mm_nn, merged, wg_mix_out, tn=_tile(d, TILE_N), tk=d, out_dtype=BF16, name="mm_mix_out")
    landed = run(exchange_wait, "gather_d2d_wait_ffn_in_near", d2d_near_ffn, plan_d2d_near)
    landed = run(exchange_wait, "gather_ici_wait_ffn_in_far", ici_far_ffn, plan_ici_far, bufs=landed)
    d2d_far_ffn = run(exchange_start, "gather_d2d_start_ffn_in_far", landed, len(far), plan_d2d_far)
    x1, h2 = run(pre_ffn_fwd, xe, o_m, mod, norm_ffn_g)
    landed = run(exchange_wait, "gather_d2d_wait_ffn_in_far", d2d_far_ffn, plan_d2d_far)
    wg_ffn_in = landed[0].reshape(N_CHIPS, *big["w_ffn_in"].shape)
    f = run(mm_nn, h2, wg_ffn_in, tn=_tile(nj_ffn, nj_ffn // 2), tk=d, out_dtype=BF16, name="mm_ffn_in", perm=perm_ffn)
    d2d_ffn_out = gather_pass_on("ffn_out", ici["ffn_out"])
    act = run(swiglu_fwd, f, nj_ffn)
    (wg_ffn_out,) = gathered("ffn_out", d2d_ffn_out)
    wg_ffn_out = wg_ffn_out.reshape(1, dff, d)
    o_f = run(mm_nn, act, wg_ffn_out, tn=_tile(d, TILE_N), tk=_tile(dff, dff // 2), out_dtype=BF16, name="mm_ffn_out")
    loss11, dy, dof, acc_l = run(loss_head, x1, o_f, tgt, mod)

    gw_ffn_out = run(mm_tn, act, dof, 1, tk=_tile(dff, TILE_N), tn=d, name="mm_ffn_out_dw")
    px_ffn_out = rs_pair_start("ffn_out", ["w_ffn_out"], [gw_ffn_out])
    dact = run(mm_nt, dof, wg_ffn_out, tko=_tile(dff, TILE_N), tn=d, out_dtype=BF16, name="mm_ffn_out_dx")
    cx_ffn_out = rs_chip_start("ffn_out", ["w_ffn_out"], px_ffn_out)
    df = run(swiglu_bwd, f, dact, nj_ffn)
    gw_ffn_in = run(mm_tn, h2, df, N_CHIPS, tk=d, tn=_tile(nj_ffn, nj_ffn // 2), name="mm_ffn_in_dw",
                    perm=perm_ffn)
    px_ffn_in = rs_pair_start("ffn_in", ["w_ffn_in"], [gw_ffn_in])
    dh2 = run(mm_nt, df, wg_ffn_in, tko=_tile(d, TILE_N), tn=nj_ffn, out_dtype=BF16, name="mm_ffn_in_dx", perm=perm_ffn)
    sh_ffn_out = rs_share_start("ffn_out", ["w_ffn_out"], cx_ffn_out)
    cx_ffn_in = rs_chip_start("ffn_in", ["w_ffn_in"], px_ffn_in)
    dx1, dom, acc_f = run(pre_ffn_bwd, x1, dh2, dy, o_m, mod, norm_ffn_g)
    gw_mix_out = run(mm_tn, merged, dom, 1, tk=d, tn=_tile(d, TILE_WIDE), name="mm_mix_out_dw")
    px_mix = rs_pair_start("mix_out", ["w_mix_out"], [gw_mix_out])
    dy_attn, dy_conv, dga, dgc = run(mix_out_dx_merge_bwd, dom, wg_mix_out.reshape(d, d), p, y_attn, y_conv,
                                     off_ga, off_gc)
    rs_finish("ffn_out", ["w_ffn_out"], sh_ffn_out)
    cx_mix = rs_chip_start("mix_out", ["w_mix_out"], px_mix)
    gw_attn_out = run(mm_tn, attn_o, dy_attn, N_CHIPS, tk=aw, tn=_tile(wg_attn_out.shape[2], TILE_N),
                      name="mm_attn_out_dw")
    gw_conv_out = run(mm_tn, s_conv, dy_conv, N_CHIPS, tk=ch, tn=_tile(wg_conv_out.shape[2], TILE_N),
                      name="mm_conv_out_dw")
    ac_names = ["w_attn_out", "w_conv_out"]
    px_ac = rs_pair_start("attn_conv_out", ac_names, [gw_attn_out, gw_conv_out])
    dattn_o = run(mm_nt, dy_attn, wg_attn_out, tko=_tile(aw, TILE_WIDE), tn=_tile(wg_attn_out.shape[2], TILE_N),
                  out_dtype=BF16, name="mm_attn_out_dx")
    ds_conv = run(mm_nt, dy_conv, wg_conv_out, tko=_tile(ch, TILE_WIDE), tn=_tile(wg_conv_out.shape[2], TILE_N),
                  out_dtype=BF16, name="mm_conv_out_dx")
    cx_ac = rs_chip_start("attn_conv_out", ac_names, px_ac)
    dca, dcb, dconv_w, dconv_vec = run(conv_bwd, ca, cb, co_conv, ds_conv, conv_w_pad, conv_ln_g, conv_ln_b)
    sh_ffn_in = rs_share_start("ffn_in", ["w_ffn_in"], cx_ffn_in)
    dqkv, dbp, dbc, dsinks, dqg, dkg = run(attn_bwd, p, bias_p, bias_c, sinks3, q_norm_g, k_norm_g, dattn_o,
                                           aw=aw, kvw=kvw)
    sh_mix = rs_share_start("mix_out", ["w_mix_out"], cx_mix)
    sh_ac = rs_share_start("attn_conv_out", ac_names, cx_ac)
    drel = run(bias_table_bwd, dbp, dbc, bucket_p, bucket_c).reshape(NUM_BUCKETS, nq)
    dp = jnp.concatenate([dqkv, dca, dcb, dga, dgc], axis=1)
    gw_in = run(mm_tn, h, dp, N_CHIPS, tk=d, tn=wg_in.shape[2], name="mm_in_dw")
    px_in = rs_pair_start("in", ["w_in"], [gw_in])
    dh = run(mm_nt, dp, wg_in, tko=_tile(d, TILE_WIDE), tn=wg_in.shape[2], out_dtype=BF16, name="mm_in_dx")
    grad_x, acc_m = run(pre_mix_bwd, xe, dh, dx1, mod, norm_mix_g)

    dmod = jnp.concatenate([acc_m[0:1], acc_m[1:2], acc_f[3:4], acc_f[0:1], acc_f[1:2], acc_l[0:1]], axis=1)
    small_names = ["b_ada", "norm_mix_g", "q_norm_g", "k_norm_g", "attn_sinks", "rel_bias", "conv_b", "conv_ln_g",
                   "conv_ln_b", "norm_ffn_g"]
    small_w = [b_ada, norm_mix_g, q_norm_g, k_norm_g, attn_sinks, rel_bias, conv_b, conv_ln_g, conv_ln_b, norm_ffn_g]
    small_m = [m_b_ada, m_norm_mix_g, m_q_norm_g, m_k_norm_g, m_attn_sinks, m_rel_bias, m_conv_b, m_conv_ln_g,
               m_conv_ln_b, m_norm_ffn_g]
    small_v = [v_b_ada, v_norm_mix_g, v_q_norm_g, v_k_norm_g, v_attn_sinks, v_rel_bias, v_conv_b, v_conv_ln_g,
               v_conv_ln_b, v_norm_ffn_g]
    small_g = [dmod, acc_m[2:3], dqg, dkg, dsinks.reshape(1, nq), drel.reshape(1, NUM_BUCKETS * nq),
               dconv_vec[0:1], dconv_vec[1:2], dconv_vec[2:3], acc_f[2:3]]
    row3, offs3 = _row_pack(small_g + [dconv_w[:CONV_WIDTH].reshape(1, CONV_WIDTH * ch), loss11])
    got3 = run(allgather_small, row3, "allgather_small_grads")
    cx_in = rs_chip_start("in", ["w_in"], px_in)
    as_row = lambda a: a.reshape(1, -1)
    outs3 = run(small_sum_adamw, got3, offs3, [as_row(a) for a in small_w], [as_row(a) for a in small_m],
                [as_row(a) for a in small_v], [CONV_WIDTH * ch, 1])
    for i, (n, w) in enumerate(zip(small_names, small_w)):
        grads[n], deltas[n], new_m[n], new_v[n] = (o.reshape(w.shape) for o in outs3[4 * i:4 * i + 4])
    g_conv_w_all, loss_sum = outs3[-2].reshape(CONV_WIDTH, ch), outs3[-1]

    g_conv_w = lax.dynamic_slice_in_dim(g_conv_w_all, chip * ch_loc, ch_loc, axis=1)
    grads["conv_w"] = g_conv_w[None]
    dl, nm, nv = run(adamw, conv_w[0], g_conv_w, m_conv_w[0], v_conv_w[0], "adamw_conv_w")
    deltas["conv_w"], new_m["conv_w"], new_v["conv_w"] = dl[None], nm[None], nv[None]

    dmod_all = got3[:, 0, :N_MOD * d]
    dmod_cols = lax.dynamic_slice_in_dim(dmod_all, chip * nc_ada, nc_ada, axis=1)
    g_ada, dl, nm, nv = run(ada_grad_adamw, c_t, dmod_cols, w_ada[0], m_w_ada[0], v_w_ada[0])
    grads["w_ada"], deltas["w_ada"], new_m["w_ada"], new_v["w_ada"] = g_ada[None], dl[None], nm[None], nv[None]

    rs_finish("ffn_in", ["w_ffn_in"], sh_ffn_in)
    rs_finish("mix_out", ["w_mix_out"], sh_mix)
    rs_finish("attn_conv_out", ac_names, sh_ac)
    sh_in = rs_share_start("in", ["w_in"], cx_in)
    rs_finish("in", ["w_in"], sh_in)

    loss = loss_sum[0, 0]
    order = ["w_ada", "b_ada", "norm_mix_g", "w_in", "q_norm_g", "k_norm_g", "attn_sinks", "rel_bias", "w_attn_out",
             "conv_w", "conv_b", "conv_ln_g", "conv_ln_b", "w_conv_out", "w_mix_out", "norm_ffn_g", "w_ffn_in",
             "w_ffn_out"]
    return (loss, grad_x[None], *[grads[n] for n in order], *[deltas[n] for n in order],
            *[new_m[n] for n in order], *[new_v[n] for n in order])
```

```python
import functools
import math
from typing import Any, NamedTuple

import jax
import jax.numpy as jnp
import numpy as np
from jax import lax
from jax.experimental import pallas as pl
from jax.experimental.pallas import tpu as pltpu

F32 = jnp.float32
BF16 = jnp.bfloat16
MESH = pl.DeviceIdType.MESH

V7X_VMEM_BYTES = 64 * 1024 * 1024
VMEM_LIMIT = V7X_VMEM_BYTES - 8 * 1024 * 1024
LANES = 128
SUBLANES = 8
BF16_SUBLANES = 16

EPS = 1e-6
WINDOW = 128
BLOCK = 128
NUM_BUCKETS = 32
MAX_EXACT = NUM_BUCKETS // 2
MAX_DISTANCE = 128
CONV_WIDTH = 31
CONV_HALO = 32
ADAM_LR = 0.001
ADAM_B1 = 0.9
ADAM_B2 = 0.999
ADAM_EPS = 1e-08
ADAM_WD = 0.01
ADAM_STEP = 10
N_MOD = 6
SH_M, SC_M, GT_M, SH_F, SC_F, GT_F = range(6)

N_CHIPS = 4
N_DEV = 8

_ANY = pl.BlockSpec(memory_space=pl.ANY)
_VMEM = pl.BlockSpec(memory_space=pltpu.VMEM)
_SMEM = pl.BlockSpec(memory_space=pltpu.SMEM)
_HBM = pl.BlockSpec(memory_space=pltpu.HBM)
_SEM = pl.BlockSpec(memory_space=pltpu.SEMAPHORE)
_EFFECT = pltpu.SideEffectType.DATAFLOW_SIDE_EFFECTING


class InOrder:
    def __init__(self):
        self.token = None

    def __call__(self, fn, *args, **kw):
        return fn(*args, dep=self, **kw)


def _pallas(body, args, *, in_specs, out_specs, out_shape, name, dep=None, grid=(), n_prefetch=0, scratch=(),
            sem=None, **kw):
    n_lead = n_prefetch + len(in_specs)
    in_specs, args = list(in_specs), list(args)
    single = not isinstance(out_shape, (list, tuple))
    out_shapes = [out_shape] if single else list(out_shape)
    out_specs = [out_specs] if single else list(out_specs)
    if dep is not None:
        inner, n_out, takes = body, len(out_shapes), dep.token is not None

        def body(*refs):
            rest = refs[n_lead + (1 if takes else 0):]
            rest[n_out][...] = jnp.zeros((SUBLANES, LANES), F32)
            return inner(*refs[:n_lead], *rest[:n_out], *rest[n_out + 1:])

        if takes:
            in_specs.append(_ANY)
            args.append(dep.token)
        out_shapes.append(jax.ShapeDtypeStruct((SUBLANES, LANES), F32))
        out_specs.append(pl.BlockSpec((SUBLANES, LANES), lambda *_: (0, 0)))
    params = kw.pop("compiler_params", None)
    if params is None:
        params = pltpu.CompilerParams(dimension_semantics=sem, vmem_limit_bytes=VMEM_LIMIT)
    outs = pl.pallas_call(
        body,
        grid_spec=pltpu.PrefetchScalarGridSpec(num_scalar_prefetch=n_prefetch, grid=grid, in_specs=in_specs,
                                               out_specs=out_specs, scratch_shapes=list(scratch)),
        out_shape=out_shapes, compiler_params=params, name=name, **kw,
    )(*args)
    if dep is not None:
        dep.token = outs[-1]
        outs = outs[:-1]
    return outs[0] if single else list(outs)


def _tile(n, pref, unit=LANES):
    best = None
    for t in range(unit, min(n, pref) + 1, unit):
        if n % t == 0:
            best = t
    return best if best is not None else n


def _sigmoid(v):
    return 1.0 / (1.0 + jnp.exp(-v.astype(F32)))


ROW_CHUNK = 1024
TILE_N = 512
TILE_WIDE = 1024
EW_ELEMS = 1024 * 1024
EW_ELEMS_MANY = 384 * 1024


def _row_chunks(m, unit=SUBLANES):
    step = _tile(m, ROW_CHUNK, unit)
    return [(s, step) for s in range(0, m, step)]


def _ew_tiles(r, n, unit=SUBLANES, elems=EW_ELEMS):
    return _tile(r, max(unit, elems // n), unit), n


def _block_pos(j, perm):
    if perm is None:
        return j
    pos = 0
    for a, p in enumerate(perm):
        pos = pos + jnp.where(j == a, p, 0)
    return pos


def mm_nn(a, w, *, tn, tk, out_dtype, name, perm=None, dep=None):
    m, k = a.shape
    j, k2, nj = w.shape
    assert k == k2 and nj % tn == 0 and k % tk == 0
    npj, nk = nj // tn, k // tk

    def body(a_ref, w_ref, o_ref, *scratch):
        kk = pl.program_id(1)
        for s, sz in _row_chunks(m):
            rows = pl.ds(s, sz)
            p = jnp.dot(a_ref[rows, :], w_ref[...], preferred_element_type=F32)
            if nk == 1:
                o_ref[rows, :] = p.astype(out_dtype)
            else:
                acc = scratch[0]

                @pl.when(kk == 0)
                def _():
                    acc[rows, :] = p

                @pl.when(kk > 0)
                def _():
                    acc[rows, :] += p

                @pl.when(kk == nk - 1)
                def _():
                    o_ref[rows, :] = acc[rows, :].astype(out_dtype)

    return _pallas(
        body, [a, w], dep=dep, grid=(j * npj, nk),
        in_specs=[
            pl.BlockSpec((m, tk), lambda n, kk: (0, kk)),
            pl.BlockSpec((None, tk, tn), lambda n, kk: (n // npj, kk, n % npj)),
        ],
        out_specs=pl.BlockSpec((m, tn), lambda n, kk: (0, _block_pos(n // npj, perm) * npj + n % npj)),
        out_shape=jax.ShapeDtypeStruct((m, j * nj), out_dtype),
        scratch=[pltpu.VMEM((m, tn), F32)] if nk > 1 else [],
        sem=("parallel", "arbitrary"), name=name)


def mm_nt(g, w, *, tko, tn, name, out_dtype=F32, perm=None, dep=None):
    m, n = g.shape
    j, k, nj = w.shape
    assert n == j * nj and nj % tn == 0 and k % tko == 0
    npj, nr = nj // tn, n // tn
    in_place = out_dtype == F32

    def body(g_ref, w_ref, o_ref, *scratch):
        r = pl.program_id(1)
        acc = o_ref if in_place else (scratch[0] if nr > 1 else None)
        for s, sz in _row_chunks(m):
            rows = pl.ds(s, sz)
            p = lax.dot_general(g_ref[rows, :], w_ref[...], (((1,), (1,)), ((), ())), preferred_element_type=F32)
            if acc is None:
                o_ref[rows, :] = p.astype(out_dtype)
                continue

            @pl.when(r == 0)
            def _():
                acc[rows, :] = p

            @pl.when(r > 0)
            def _():
                acc[rows, :] += p

            if not in_place:
                @pl.when(r == nr - 1)
                def _():
                    o_ref[rows, :] = acc[rows, :].astype(out_dtype)

    return _pallas(
        body, [g, w], dep=dep, grid=(k // tko, nr),
        in_specs=[
            pl.BlockSpec((m, tn), lambda ko, r: (0, _block_pos(r // npj, perm) * npj + r % npj)),
            pl.BlockSpec((None, tko, tn), lambda ko, r: (r // npj, ko, r % npj)),
        ],
        out_specs=pl.BlockSpec((m, tko), lambda ko, r: (0, ko)),
        out_shape=jax.ShapeDtypeStruct((m, k), out_dtype),
        scratch=[pltpu.VMEM((m, tko), F32)] if (nr > 1 and not in_place) else [],
        sem=("parallel", "arbitrary"), name=name)


def mm_tn(a, g, n_blocks, *, tk, tn, name, perm=None, dep=None):
    m, k = a.shape
    m2, n = g.shape
    nj = n // n_blocks
    assert m == m2 and nj % tn == 0 and k % tk == 0
    npj = nj // tn

    def body(a_ref, g_ref, o_ref):
        for s, sz in _row_chunks(tk, LANES):
            p = lax.dot_general(a_ref[:, pl.ds(s, sz)], g_ref[...], (((0,), (0,)), ((), ())),
                                preferred_element_type=F32)
            o_ref[pl.ds(s, sz), :] = p.astype(BF16)

    return _pallas(
        body, [a, g], dep=dep, grid=(k // tk, n // tn),
        in_specs=[
            pl.BlockSpec((m, tk), lambda kk, nn: (0, kk)),
            pl.BlockSpec((m, tn), lambda kk, nn: (0, _block_pos(nn // npj, perm) * npj + nn % npj)),
        ],
        out_specs=pl.BlockSpec((None, tk, tn), lambda kk, nn: (nn // npj, kk, nn % npj)),
        out_shape=jax.ShapeDtypeStruct((n_blocks, k, nj), BF16),
        sem=("parallel", "parallel"), name=name)


ROW_TILE = 256


def _row_spec(tr, width):
    return pl.BlockSpec((tr, width), lambda i: (i, 0))


def _full_spec(shape):
    return pl.BlockSpec(shape, lambda *_: (0,) * len(shape))


def _rms(xv):
    return lax.rsqrt(jnp.mean(xv * xv, axis=-1, keepdims=True) + EPS)


def _mod_row(mod_ref, row):
    return mod_ref[pl.ds(row, 1), :]


def pre_mix_fwd(x, mod, gain, dep=None):
    t, d = x.shape
    tr = _tile(t, ROW_TILE, SUBLANES)

    def body(x_ref, mod_ref, g_ref, h_ref):
        xv = x_ref[...]
        y = xv * _rms(xv) * g_ref[...]
        h_ref[...] = (y * (1.0 + _mod_row(mod_ref, SC_M)) + _mod_row(mod_ref, SH_M)).astype(BF16)

    return _pallas(
        body, [x, mod, gain], dep=dep, grid=(t // tr,),
        in_specs=[_row_spec(tr, d), _full_spec(mod.shape), _full_spec(gain.shape)],
        out_specs=_row_spec(tr, d),
        out_shape=jax.ShapeDtypeStruct((t, d), BF16),
        sem=("parallel",), name="pre_mix_fwd")


def pre_ffn_fwd(x, o_m, mod, gain, dep=None):
    t, d = x.shape
    tr = _tile(t, ROW_TILE, SUBLANES)

    def body(x_ref, om_ref, mod_ref, g_ref, x1_ref, h_ref):
        x1 = x_ref[...] + _mod_row(mod_ref, GT_M) * om_ref[...]
        x1_ref[...] = x1
        y = x1 * _rms(x1) * g_ref[...]
        h_ref[...] = (y * (1.0 + _mod_row(mod_ref, SC_F)) + _mod_row(mod_ref, SH_F)).astype(BF16)

    return _pallas(
        body, [x, o_m, mod, gain], dep=dep, grid=(t // tr,),
        in_specs=[_row_spec(tr, d), _row_spec(tr, d), _full_spec(mod.shape), _full_spec(gain.shape)],
        out_specs=[_row_spec(tr, d), _row_spec(tr, d)],
        out_shape=[jax.ShapeDtypeStruct((t, d), F32), jax.ShapeDtypeStruct((t, d), BF16)],
        sem=("parallel",), name="pre_ffn_fwd")


def loss_head(x1, o_f, target, mod, dep=None):
    t, d = x1.shape
    tr = _tile(t, ROW_TILE, SUBLANES)

    def body(x1_ref, of_ref, tg_ref, mod_ref, loss_ref, dy_ref, dof_ref, acc_ref):
        i = pl.program_id(0)
        gt = _mod_row(mod_ref, GT_F)
        of = of_ref[...].astype(F32)
        err = x1_ref[...] + gt * of - tg_ref[...]
        dy = err * (1.0 / d)
        dy_ref[...] = dy.astype(BF16)
        dof_ref[...] = (dy * gt).astype(BF16)
        part = (0.5 / d) * jnp.sum(jnp.sum(err * err, axis=1, keepdims=True), axis=0, keepdims=True)
        dgt = jnp.sum(dy * of, axis=0, keepdims=True)

        @pl.when(i == 0)
        def _():
            loss_ref[...] = jnp.zeros_like(loss_ref)
            acc_ref[...] = jnp.zeros_like(acc_ref)

        loss_ref[...] += part
        acc_ref[pl.ds(0, 1), :] += dgt

    return _pallas(
        body, [x1, o_f, target, mod], dep=dep, grid=(t // tr,),
        in_specs=[_row_spec(tr, d), _row_spec(tr, d), _row_spec(tr, d), _full_spec(mod.shape)],
        out_specs=[_full_spec((1, 1)), _row_spec(tr, d), _row_spec(tr, d), _full_spec((SUBLANES, d))],
        out_shape=[jax.ShapeDtypeStruct((1, 1), F32), jax.ShapeDtypeStruct((t, d), BF16),
                   jax.ShapeDtypeStruct((t, d), BF16), jax.ShapeDtypeStruct((SUBLANES, d), F32)],
        sem=("arbitrary",), name="loss_head")


def _norm_bwd(xv, dh, sc, gain):
    rstd = _rms(xv)
    yn = xv * rstd
    dsh = jnp.sum(dh, axis=0, keepdims=True)
    dsc = jnp.sum(dh * (yn * gain), axis=0, keepdims=True)
    dgain = jnp.sum(dh * (1.0 + sc) * yn, axis=0, keepdims=True)
    dyn = dh * ((1.0 + sc) * gain)
    dx = rstd * (dyn - yn * jnp.mean(dyn * yn, axis=-1, keepdims=True))
    return dx, dsh, dsc, dgain


def pre_ffn_bwd(x1, dh2, dy, o_m, mod, gain, dep=None):
    t, d = x1.shape
    tr = _tile(t, ROW_TILE, SUBLANES)

    def body(x1_ref, dh_ref, dy_ref, om_ref, mod_ref, g_ref, dx1_ref, dom_ref, acc_ref):
        i = pl.program_id(0)
        dxn, dsh, dsc, dgain = _norm_bwd(x1_ref[...], dh_ref[...].astype(F32), _mod_row(mod_ref, SC_F), g_ref[...])
        dx1 = dy_ref[...] + dxn
        dx1_ref[...] = dx1
        dom_ref[...] = (dx1 * _mod_row(mod_ref, GT_M)).astype(BF16)
        dgt = jnp.sum(dx1 * om_ref[...], axis=0, keepdims=True)

        @pl.when(i == 0)
        def _():
            acc_ref[...] = jnp.zeros_like(acc_ref)

        acc_ref[pl.ds(0, 1), :] += dsh
        acc_ref[pl.ds(1, 1), :] += dsc
        acc_ref[pl.ds(2, 1), :] += dgain
        acc_ref[pl.ds(3, 1), :] += dgt

    return _pallas(
        body, [x1, dh2, dy, o_m, mod, gain], dep=dep, grid=(t // tr,),
        in_specs=[_row_spec(tr, d)] * 4 + [_full_spec(mod.shape), _full_spec(gain.shape)],
        out_specs=[_row_spec(tr, d), _row_spec(tr, d), _full_spec((SUBLANES, d))],
        out_shape=[jax.ShapeDtypeStruct((t, d), F32), jax.ShapeDtypeStruct((t, d), BF16),
                   jax.ShapeDtypeStruct((SUBLANES, d), F32)],
        sem=("arbitrary",), name="pre_ffn_bwd")


def pre_mix_bwd(x, dh, dx1, mod, gain, dep=None):
    t, d = x.shape
    tr = _tile(t, ROW_TILE, SUBLANES)

    def body(x_ref, dh_ref, dx1_ref, mod_ref, g_ref, gx_ref, acc_ref):
        i = pl.program_id(0)
        dxn, dsh, dsc, dgain = _norm_bwd(x_ref[...], dh_ref[...].astype(F32), _mod_row(mod_ref, SC_M), g_ref[...])
        gx_ref[...] = dx1_ref[...] + dxn

        @pl.when(i == 0)
        def _():
            acc_ref[...] = jnp.zeros_like(acc_ref)

        acc_ref[pl.ds(0, 1), :] += dsh
        acc_ref[pl.ds(1, 1), :] += dsc
        acc_ref[pl.ds(2, 1), :] += dgain

    return _pallas(
        body, [x, dh, dx1, mod, gain], dep=dep, grid=(t // tr,),
        in_specs=[_row_spec(tr, d)] * 3 + [_full_spec(mod.shape), _full_spec(gain.shape)],
        out_specs=[_row_spec(tr, d), _full_spec((SUBLANES, d))],
        out_shape=[jax.ShapeDtypeStruct((t, d), F32), jax.ShapeDtypeStruct((SUBLANES, d), F32)],
        sem=("arbitrary",), name="pre_mix_bwd")


def branch_out_merge(attn_o, s_conv, w_attn, w_conv, p, off_ga, off_gc, dep=None):
    t = attn_o.shape[0]
    j, ka, nj = w_attn.shape
    kc = w_conv.shape[1]
    assert off_ga % nj == 0 and off_gc % nj == 0

    def body(a_ref, s_ref, wa_ref, wc_ref, ga_ref, gc_ref, ya_ref, yc_ref, m_ref):
        for s, sz in _row_chunks(t):
            rows = pl.ds(s, sz)
            ya = jnp.dot(a_ref[rows, :], wa_ref[...], preferred_element_type=F32)
            yc = jnp.dot(s_ref[rows, :], wc_ref[...], preferred_element_type=F32)
            ya_ref[rows, :] = ya.astype(BF16)
            yc_ref[rows, :] = yc.astype(BF16)
            m_ref[rows, :] = (_sigmoid(ga_ref[rows, :]) * ya + _sigmoid(gc_ref[rows, :]) * yc).astype(BF16)

    col = pl.BlockSpec((t, nj), lambda b: (0, b))
    return _pallas(
        body, [attn_o, s_conv, w_attn, w_conv, p, p], dep=dep, grid=(j,),
        in_specs=[pl.BlockSpec((t, ka), lambda b: (0, 0)), pl.BlockSpec((t, kc), lambda b: (0, 0)),
                  pl.BlockSpec((None, ka, nj), lambda b: (b, 0, 0)), pl.BlockSpec((None, kc, nj), lambda b: (b, 0, 0)),
                  pl.BlockSpec((t, nj), lambda b: (0, off_ga // nj + b)),
                  pl.BlockSpec((t, nj), lambda b: (0, off_gc // nj + b))],
        out_specs=[col] * 3,
        out_shape=[jax.ShapeDtypeStruct((t, j * nj), BF16)] * 3,
        sem=("parallel",), name="branch_out_merge")


def mix_out_dx_merge_bwd(dom, w_mix, p, y_attn, y_conv, off_ga, off_gc, dep=None):
    t, d = y_attn.shape
    cw = math.gcd(math.gcd(off_ga, off_gc), math.gcd(d, TILE_N // 2))

    def body(dom_ref, w_ref, ga_ref, gc_ref, ya_ref, yc_ref, dya_ref, dyc_ref, dga_ref, dgc_ref):
        for s, sz in _row_chunks(t):
            rows = pl.ds(s, sz)
            dm = lax.dot_general(dom_ref[rows, :], w_ref[...], (((1,), (1,)), ((), ())), preferred_element_type=F32)
            sa = _sigmoid(ga_ref[rows, :])
            sc = _sigmoid(gc_ref[rows, :])
            dya_ref[rows, :] = (dm * sa).astype(BF16)
            dyc_ref[rows, :] = (dm * sc).astype(BF16)
            dga_ref[rows, :] = (dm * ya_ref[rows, :] * sa * (1.0 - sa)).astype(BF16)
            dgc_ref[rows, :] = (dm * yc_ref[rows, :] * sc * (1.0 - sc)).astype(BF16)

    col = pl.BlockSpec((t, cw), lambda j: (0, j))
    return _pallas(
        body, [dom, w_mix, p, p, y_attn, y_conv], dep=dep, grid=(d // cw,),
        in_specs=[pl.BlockSpec((t, d), lambda j: (0, 0)), pl.BlockSpec((cw, d), lambda j: (j, 0)),
                  pl.BlockSpec((t, cw), lambda j: (0, off_ga // cw + j)),
                  pl.BlockSpec((t, cw), lambda j: (0, off_gc // cw + j)), col, col],
        out_specs=[col] * 4,
        out_shape=[jax.ShapeDtypeStruct((t, d), BF16)] * 4,
        sem=("parallel",), name="mm_mix_out_dx_merge_bwd")


def ffn_perm(n_blocks):
    half = n_blocks // 2
    return tuple(2 * j if j < half else 2 * (j - half) + 1 for j in range(n_blocks))


def swiglu_fwd(f, nj, dep=None):
    t, two = f.shape
    tr = _tile(t, ROW_TILE, SUBLANES)
    npair = two // (2 * nj)

    def body(f_ref, o_ref):
        g = f_ref[:, :nj].astype(F32)
        u = f_ref[:, nj:].astype(F32)
        o_ref[...] = (g * _sigmoid(g) * u).astype(BF16)

    return _pallas(
        body, [f], dep=dep, grid=(t // tr, npair),
        in_specs=[pl.BlockSpec((tr, 2 * nj), lambda i, j: (i, j))],
        out_specs=pl.BlockSpec((tr, nj), lambda i, j: (i, j)),
        out_shape=jax.ShapeDtypeStruct((t, two // 2), BF16),
        sem=("parallel", "parallel"), name="swiglu_fwd")


def swiglu_bwd(f, dact, nj, dep=None):
    t, two = f.shape
    tr = _tile(t, ROW_TILE, SUBLANES)
    npair = two // (2 * nj)

    def body(f_ref, da_ref, o_ref):
        g = f_ref[:, :nj].astype(F32)
        u = f_ref[:, nj:].astype(F32)
        da = da_ref[...]
        s = _sigmoid(g)
        o_ref[:, :nj] = (da * u * (s * (1.0 + g * (1.0 - s)))).astype(BF16)
        o_ref[:, nj:] = (da * (g * s)).astype(BF16)

    return _pallas(
        body, [f, dact], dep=dep, grid=(t // tr, npair),
        in_specs=[pl.BlockSpec((tr, 2 * nj), lambda i, j: (i, j)), pl.BlockSpec((tr, nj), lambda i, j: (i, j))],
        out_specs=pl.BlockSpec((tr, 2 * nj), lambda i, j: (i, j)),
        out_shape=jax.ShapeDtypeStruct((t, two), BF16),
        sem=("parallel", "parallel"), name="swiglu_bwd")


def _t5_bucket_table():
    q_off = np.arange(BLOCK)
    k_off = np.arange(2 * BLOCK)
    dist = q_off[:, None] + BLOCK - k_off[None, :]
    n = np.maximum(dist, 0)
    nf = np.maximum(n, 1).astype(np.float32)
    large = MAX_EXACT + (np.log(nf / np.float32(MAX_EXACT)) / np.float32(math.log(MAX_DISTANCE / MAX_EXACT))
                         * np.float32(NUM_BUCKETS - MAX_EXACT)).astype(np.int32)
    large = np.minimum(large, NUM_BUCKETS - 1)
    bucket = np.where(n < MAX_EXACT, n, large).astype(np.int32)
    allowed = (dist >= 0) & (dist < WINDOW)
    return np.where(allowed, bucket, -1).astype(np.int32)


def bias_table(rel_bias, bucket_p, bucket_c, dep=None):
    nb, nq = rel_bias.shape

    def body(rb_ref, bkp_ref, bkc_ref, op_ref, oc_ref):
        for bk_ref, o_ref in ((bkp_ref, op_ref), (bkc_ref, oc_ref)):
            bk = bk_ref[...]
            for h in range(nq):
                acc = jnp.full(bk.shape, -jnp.inf, F32)
                for b in range(nb):
                    acc = jnp.where(bk == b, rb_ref[b, h], acc)
                o_ref[h] = acc

    return _pallas(
        body, [rel_bias, bucket_p, bucket_c], dep=dep,
        in_specs=[_SMEM, _VMEM, _VMEM], out_specs=[_VMEM, _VMEM],
        out_shape=[jax.ShapeDtypeStruct((nq,) + bucket_p.shape, F32)] * 2,
        name="bias_table")


def bias_table_bwd(dbp, dbc, bucket_p, bucket_c, dep=None):
    nq = dbp.shape[0]

    def body(dbp_ref, dbc_ref, bkp_ref, bkc_ref, o_ref):
        bkp, bkc = bkp_ref[...][None], bkc_ref[...][None]
        dp, dc = dbp_ref[...], dbc_ref[...]
        for b in range(NUM_BUCKETS):
            sel = jnp.where(bkp == b, dp, 0.0) + jnp.where(bkc == b, dc, 0.0)
            o_ref[b] = jnp.sum(jnp.sum(sel, axis=2, keepdims=True), axis=1, keepdims=True)

    return _pallas(
        body, [dbp, dbc, bucket_p, bucket_c], dep=dep,
        in_specs=[_VMEM] * 4, out_specs=_VMEM,
        out_shape=jax.ShapeDtypeStruct((NUM_BUCKETS, nq, 1, 1), F32),
        name="bias_table_bwd")


_BNT = (((2,), (2,)), ((0,), (0,)))
_BNN = (((2,), (1,)), ((0,), (0,)))
_BTN = (((1,), (1,)), ((0,), (0,)))


@jax.custom_vjp
def _bdot_nt(a, b):
    return lax.dot_general(a.astype(BF16), b.astype(BF16), _BNT, preferred_element_type=F32)


def _bdot_nt_fwd(a, b):
    return _bdot_nt(a, b), (a, b)


def _bdot_nt_bwd(res, g):
    a, b = res
    gb = g.astype(BF16)
    da = lax.dot_general(gb, b.astype(BF16), _BNN, preferred_element_type=F32)
    db = lax.dot_general(gb, a.astype(BF16), _BTN, preferred_element_type=F32)
    return da, db


_bdot_nt.defvjp(_bdot_nt_fwd, _bdot_nt_bwd)


@jax.custom_vjp
def _bdot_nn(a, b):
    return lax.dot_general(a.astype(BF16), b.astype(BF16), _BNN, preferred_element_type=F32)


def _bdot_nn_fwd(a, b):
    return _bdot_nn(a, b), (a, b)


def _bdot_nn_bwd(res, g):
    a, b = res
    gb = g.astype(BF16)
    da = lax.dot_general(gb, b.astype(BF16), _BNT, preferred_element_type=F32)
    db = lax.dot_general(a.astype(BF16), gb, _BTN, preferred_element_type=F32)
    return da, db


_bdot_nn.defvjp(_bdot_nn_fwd, _bdot_nn_bwd)


def _attn_math(q, kp, kc, vp, vc, bp, bc, sinks, qg, kg, *, prev_ok, scale):
    h, rows, _ = q.shape
    b = kp.shape[1]
    qn = q * _rms(q) * qg
    kpn = kp * _rms(kp) * kg
    kcn = kc * _rms(kc) * kg
    lp = _bdot_nt(qn, kpn) * scale + bp.reshape(h, rows, b)
    lc = _bdot_nt(qn, kcn) * scale + bc.reshape(h, rows, b)
    lp = jnp.where(prev_ok, lp, -jnp.inf)
    sink = jnp.broadcast_to(sinks, (sinks.shape[0], b, 1)).reshape(h, rows, 1)
    m = jnp.maximum(jnp.maximum(jnp.max(lp, axis=-1, keepdims=True), jnp.max(lc, axis=-1, keepdims=True)), sink)
    m = lax.stop_gradient(m)
    pp = jnp.exp(lp - m)
    pc = jnp.exp(lc - m)
    den = jnp.sum(pp, axis=-1, keepdims=True) + jnp.sum(pc, axis=-1, keepdims=True) + jnp.exp(sink - m)
    inv = 1.0 / den
    return _bdot_nn(pp * inv, vp) + _bdot_nn(pc * inv, vc)


def _attn_specs(p, aw, kvw, nq, hd, nblk, reverse):
    assert aw % (2 * kvw) == 0
    kv_col = aw // (2 * kvw)

    def blk(n):
        return nblk - 1 - n if reverse else n

    return [
        pl.BlockSpec((BLOCK, aw), lambda n: (blk(n), 0)),
        pl.BlockSpec((BLOCK, 2 * kvw), lambda n: (jnp.maximum(blk(n) - 1, 0), kv_col)),
        pl.BlockSpec((BLOCK, 2 * kvw), lambda n: (blk(n), kv_col)),
        _full_spec((nq, BLOCK, BLOCK)), _full_spec((nq, BLOCK, BLOCK)), _full_spec((nq, 1, 1)),
        _full_spec((1, hd)), _full_spec((1, hd)),
    ]


def _head_major(ref, n_heads, grp, hd, offset=0):
    return jnp.stack([
        jnp.concatenate([ref[:, pl.ds(offset + (grp * h + g) * hd, hd)].astype(F32) for g in range(grp)], axis=0)
        for h in range(n_heads)])


def _attn_inputs(nkv, grp, hd, kvw, q_ref, kvp_ref, kvc_ref):
    return (_head_major(q_ref, nkv, grp, hd), _head_major(kvp_ref, nkv, 1, hd), _head_major(kvc_ref, nkv, 1, hd),
            _head_major(kvp_ref, nkv, 1, hd, kvw), _head_major(kvc_ref, nkv, 1, hd, kvw))


def attn_fwd(p, bias_p, bias_c, sinks, qg, kg, *, aw, kvw, dep=None):
    t, hd = p.shape[0], qg.shape[-1]
    nq, nkv, nblk = aw // hd, kvw // hd, t // BLOCK
    grp = nq // nkv
    scale = hd ** -0.5

    def body(q_ref, kvp_ref, kvc_ref, bp_ref, bc_ref, s_ref, qg_ref, kg_ref, o_ref):
        prev_ok = pl.program_id(0) > 0
        out = _attn_math(*_attn_inputs(nkv, grp, hd, kvw, q_ref, kvp_ref, kvc_ref), bp_ref[...], bc_ref[...],
                         s_ref[...], qg_ref[...], kg_ref[...], prev_ok=prev_ok, scale=scale)
        for h in range(nkv):
            for g in range(grp):
                o_ref[:, pl.ds((grp * h + g) * hd, hd)] = out[h, g * BLOCK:(g + 1) * BLOCK].astype(BF16)

    return _pallas(
        body, [p, p, p, bias_p, bias_c, sinks, qg, kg], dep=dep, grid=(nblk,),
        in_specs=_attn_specs(p, aw, kvw, nq, hd, nblk, False),
        out_specs=pl.BlockSpec((BLOCK, aw), lambda n: (n, 0)),
        out_shape=jax.ShapeDtypeStruct((t, aw), BF16),
        sem=("parallel",), name="attn_fwd")


def attn_bwd(p, bias_p, bias_c, sinks, qg, kg, do, *, aw, kvw, dep=None):
    t, hd = p.shape[0], qg.shape[-1]
    nq, nkv, nblk = aw // hd, kvw // hd, t // BLOCK
    grp = nq // nkv
    scale = hd ** -0.5

    def body(q_ref, kvp_ref, kvc_ref, bp_ref, bc_ref, s_ref, qg_ref, kg_ref, do_ref,
             dqkv_ref, dbp_ref, dbc_ref, ds_ref, dqg_ref, dkg_ref, carry):
        i = pl.program_id(0)
        prev_ok = (nblk - 1 - i) > 0

        @pl.when(i == 0)
        def _():
            carry[...] = jnp.zeros_like(carry)
            dbp_ref[...] = jnp.zeros_like(dbp_ref)
            dbc_ref[...] = jnp.zeros_like(dbc_ref)
            ds_ref[...] = jnp.zeros_like(ds_ref)
            dqg_ref[...] = jnp.zeros_like(dqg_ref)
            dkg_ref[...] = jnp.zeros_like(dkg_ref)

        fn = functools.partial(_attn_math, prev_ok=prev_ok, scale=scale)
        _, vjp = jax.vjp(fn, *_attn_inputs(nkv, grp, hd, kvw, q_ref, kvp_ref, kvc_ref), bp_ref[...], bc_ref[...],
                         s_ref[...], qg_ref[...], kg_ref[...])
        dq, dkp, dkc, dvp, dvc, dbp, dbc, dsk, dqg, dkg = vjp(_head_major(do_ref, nkv, grp, hd))
        for h in range(nkv):
            for g in range(grp):
                dqkv_ref[:, pl.ds((grp * h + g) * hd, hd)] = dq[h, g * BLOCK:(g + 1) * BLOCK].astype(BF16)
            k_cols, v_cols = pl.ds(h * hd, hd), pl.ds(kvw + h * hd, hd)
            dqkv_ref[:, pl.ds(aw + h * hd, hd)] = (dkc[h] + carry[:, k_cols]).astype(BF16)
            dqkv_ref[:, pl.ds(aw + kvw + h * hd, hd)] = (dvc[h] + carry[:, v_cols]).astype(BF16)
            carry[:, k_cols] = dkp[h]
            carry[:, v_cols] = dvp[h]
        dbp_ref[...] += dbp
        dbc_ref[...] += dbc
        ds_ref[...] += dsk
        dqg_ref[...] += dqg
        dkg_ref[...] += dkg

    return _pallas(
        body, [p, p, p, bias_p, bias_c, sinks, qg, kg, do], dep=dep, grid=(nblk,),
        in_specs=_attn_specs(p, aw, kvw, nq, hd, nblk, True)
        + [pl.BlockSpec((BLOCK, aw), lambda n: (nblk - 1 - n, 0))],
        out_specs=[
            pl.BlockSpec((BLOCK, aw + 2 * kvw), lambda n: (nblk - 1 - n, 0)),
            _full_spec((nq, BLOCK, BLOCK)), _full_spec((nq, BLOCK, BLOCK)), _full_spec((nq, 1, 1)),
            _full_spec((1, hd)), _full_spec((1, hd)),
        ],
        out_shape=[
            jax.ShapeDtypeStruct((t, aw + 2 * kvw), BF16),
            jax.ShapeDtypeStruct((nq, BLOCK, BLOCK), F32),
            jax.ShapeDtypeStruct((nq, BLOCK, BLOCK), F32),
            jax.ShapeDtypeStruct((nq, 1, 1), F32),
            jax.ShapeDtypeStruct((1, hd), F32),
            jax.ShapeDtypeStruct((1, hd), F32),
        ],
        scratch=[pltpu.VMEM((BLOCK, 2 * kvw), F32)],
        sem=("arbitrary",), name="attn_bwd")


CONV_TILE = 256


def _conv_halo_specs(tb, ch, nblk):
    per = tb // CONV_HALO
    last = nblk * per - 1
    cur = pl.BlockSpec((tb, ch), lambda n: (n, 0))
    prev = pl.BlockSpec((CONV_HALO, ch), lambda n: (jnp.maximum(n * per - 1, 0), 0))
    nxt = pl.BlockSpec((CONV_HALO, ch), lambda n: (jnp.minimum((n + 1) * per, last), 0))
    return cur, prev, nxt


def _ln_silu(co, ln_g, ln_b):
    mu = jnp.mean(co, axis=-1, keepdims=True)
    cen = co - mu
    rstd = lax.rsqrt(jnp.mean(cen * cen, axis=-1, keepdims=True) + EPS)
    xhat = cen * rstd
    z = xhat * ln_g + ln_b
    return xhat, rstd, z


def _shifted_copies(src, shifted):
    rows = src.shape[0] - SUBLANES
    for r in range(1, SUBLANES):
        shifted[r, pl.ds(0, rows), :] = src[pl.ds(r, rows), :]


def _rows_from(src, shifted, start, n):
    r = start % SUBLANES
    if r == 0:
        return src[pl.ds(start, n), :]
    return shifted[r, pl.ds(start - r, n), :]


def conv_fwd(ca, cb, conv_w, conv_b, ln_g, ln_b, dep=None):
    t, ch = ca.shape
    tb = _tile(t, CONV_TILE, CONV_HALO)
    nblk = t // tb
    cur, prev, _ = _conv_halo_specs(tb, ch, nblk)
    lead = CONV_HALO - (CONV_WIDTH - 1)

    def body(ca_ref, cb_ref, cap_ref, cbp_ref, w_ref, b_ref, g_ref, bb_ref, s_ref, co_ref, ubuf, ushift):
        n = pl.program_id(0)
        halo = cap_ref[...] * _sigmoid(cbp_ref[...])
        ubuf[pl.ds(0, CONV_HALO), :] = jnp.where(n > 0, halo, 0.0)
        ubuf[pl.ds(CONV_HALO, tb), :] = ca_ref[...] * _sigmoid(cb_ref[...])
        _shifted_copies(ubuf, ushift)
        acc = jnp.broadcast_to(b_ref[...], (tb, ch))
        for k in range(CONV_WIDTH):
            acc = acc + w_ref[pl.ds(k, 1), :] * _rows_from(ubuf, ushift, lead + k, tb)
        co_ref[...] = acc
        _, _, z = _ln_silu(acc, g_ref[...], bb_ref[...])
        s_ref[...] = (z * _sigmoid(z)).astype(BF16)

    vec = _full_spec((1, ch))
    return _pallas(
        body, [ca, cb, ca, cb, conv_w, conv_b, ln_g, ln_b], dep=dep, grid=(nblk,),
        in_specs=[cur, cur, prev, prev, _full_spec(conv_w.shape), vec, vec, vec],
        out_specs=[cur, cur],
        out_shape=[jax.ShapeDtypeStruct((t, ch), BF16), jax.ShapeDtypeStruct((t, ch), F32)],
        scratch=[pltpu.VMEM((CONV_HALO + tb, ch), F32), pltpu.VMEM((SUBLANES, CONV_HALO + tb, ch), F32)],
        sem=("parallel",), name="conv_fwd")


def conv_bwd(ca, cb, co, ds, conv_w, ln_g, ln_b, dep=None):
    t, ch = ca.shape
    tb = _tile(t, CONV_TILE, CONV_HALO)
    nblk = t // tb
    cur, prev, nxt = _conv_halo_specs(tb, ch, nblk)
    lead = CONV_HALO - (CONV_WIDTH - 1)
    ext = tb + CONV_HALO

    def body(ca_ref, cb_ref, cap_ref, cbp_ref, co_ref, con_ref, ds_ref, dsn_ref, w_ref, g_ref, bb_ref,
             dca_ref, dcb_ref, dw_ref, dvec_ref, ubuf, dbuf, ushift, dshift):
        n = pl.program_id(0)
        is_last = n == nblk - 1
        sig_b = _sigmoid(cb_ref[...])
        cav = ca_ref[...].astype(F32)
        ubuf[pl.ds(0, CONV_HALO), :] = jnp.where(n > 0, cap_ref[...] * _sigmoid(cbp_ref[...]), 0.0)
        ubuf[pl.ds(CONV_HALO, tb), :] = cav * sig_b
        _shifted_copies(ubuf, ushift)
        co = jnp.concatenate([co_ref[...], con_ref[...]], axis=0)
        xhat, rstd, z = _ln_silu(co, g_ref[...], bb_ref[...])
        dsv = jnp.concatenate([ds_ref[...].astype(F32), jnp.where(is_last, 0.0, dsn_ref[...].astype(F32))], axis=0)
        sg = _sigmoid(z)
        dz = dsv * (sg * (1.0 + z * (1.0 - sg)))
        dxh = dz * g_ref[...]
        dco = rstd * (dxh - jnp.mean(dxh, axis=-1, keepdims=True)
                      - xhat * jnp.mean(dxh * xhat, axis=-1, keepdims=True))
        dbuf[...] = dco
        _shifted_copies(dbuf, dshift)

        @pl.when(n == 0)
        def _():
            dw_ref[...] = jnp.zeros_like(dw_ref)
            dvec_ref[...] = jnp.zeros_like(dvec_ref)

        dco_cur = dco[:tb]
        dvec_ref[pl.ds(0, 1), :] += jnp.sum(dco_cur, axis=0, keepdims=True)
        dvec_ref[pl.ds(1, 1), :] += jnp.sum(dz[:tb] * xhat[:tb], axis=0, keepdims=True)
        dvec_ref[pl.ds(2, 1), :] += jnp.sum(dz[:tb], axis=0, keepdims=True)
        du = jnp.zeros((tb, ch), F32)
        for k in range(CONV_WIDTH):
            du = du + w_ref[pl.ds(k, 1), :] * _rows_from(dbuf, dshift, CONV_WIDTH - 1 - k, tb)
            dw_ref[pl.ds(k, 1), :] += jnp.sum(dco_cur * _rows_from(ubuf, ushift, lead + k, tb), axis=0,
                                              keepdims=True)
        dca_ref[...] = (du * sig_b).astype(BF16)
        dcb_ref[...] = (du * cav * sig_b * (1.0 - sig_b)).astype(BF16)

    vec = _full_spec((1, ch))
    return _pallas(
        body, [ca, cb, ca, cb, co, co, ds, ds, conv_w, ln_g, ln_b], dep=dep, grid=(nblk,),
        in_specs=[cur, cur, prev, prev, cur, nxt, cur, nxt, _full_spec(conv_w.shape), vec, vec],
        out_specs=[cur, cur, _full_spec(conv_w.shape), _full_spec((SUBLANES, ch))],
        out_shape=[jax.ShapeDtypeStruct((t, ch), BF16), jax.ShapeDtypeStruct((t, ch), BF16),
                   jax.ShapeDtypeStruct(conv_w.shape, F32), jax.ShapeDtypeStruct((SUBLANES, ch), F32)],
        scratch=[pltpu.VMEM((CONV_HALO + tb, ch), F32), pltpu.VMEM((ext, ch), F32),
                 pltpu.VMEM((SUBLANES, CONV_HALO + tb, ch), F32), pltpu.VMEM((SUBLANES, ext, ch), F32)],
        sem=("arbitrary",), name="conv_bwd")


def ada_fwd(c_t, w_ada, dep=None):
    d, nc = w_ada.shape
    nex = c_t.shape[1]
    tn = _tile(nc, TILE_N)

    def body(ct_ref, w_ref, o_ref):
        w = w_ref[...]
        ct = ct_ref[...]
        cact = ct * _sigmoid(ct)
        rows = [jnp.sum(w * cact[:, b:b + 1], axis=0, keepdims=True) for b in range(nex)]
        o_ref[...] = jnp.concatenate(rows, axis=0)

    return _pallas(
        body, [c_t, w_ada], dep=dep, grid=(nc // tn,),
        in_specs=[_full_spec(c_t.shape), pl.BlockSpec((d, tn), lambda j: (0, j))],
        out_specs=pl.BlockSpec((nex, tn), lambda j: (0, j)),
        out_shape=jax.ShapeDtypeStruct((nex, nc), F32),
        sem=("parallel",), name="ada_fwd")


def _adamw_math(w, g, m, v):
    m = ADAM_B1 * m + (1.0 - ADAM_B1) * g
    v = ADAM_B2 * v + (1.0 - ADAM_B2) * (g * g)
    m_hat = m / (1.0 - ADAM_B1 ** ADAM_STEP)
    v_hat = v / (1.0 - ADAM_B2 ** ADAM_STEP)
    delta = -ADAM_LR * (m_hat / (jnp.sqrt(v_hat) + ADAM_EPS) + ADAM_WD * w)
    return delta, m, v


def adamw(w, g, m, v, name, copy_grad=False, dep=None):
    r, n = w.shape
    tr, tn = _ew_tiles(r, n, elems=EW_ELEMS_MANY)
    n_out = 4 if copy_grad else 3

    def body(w_ref, g_ref, m_ref, v_ref, *outs):
        g = g_ref[...]
        if copy_grad:
            outs[0][...] = g
        outs[-3][...], outs[-2][...], outs[-1][...] = _adamw_math(w_ref[...], g, m_ref[...], v_ref[...])

    blk = pl.BlockSpec((tr, tn), lambda i, j: (i, j))
    return _pallas(
        body, [w, g, m, v], dep=dep, grid=(r // tr, n // tn),
        in_specs=[blk] * 4, out_specs=[blk] * n_out,
        out_shape=[jax.ShapeDtypeStruct((r, n), F32)] * n_out,
        sem=("parallel", "parallel"), name=name)


def ada_grad_adamw(c_t, dmod_cols, w, m, v, dep=None):
    d, nc = w.shape
    nex = c_t.shape[1]
    tr, tn = _ew_tiles(d, nc, elems=EW_ELEMS_MANY)

    def body(ct_ref, dm_ref, w_ref, m_ref, v_ref, g_ref, d_ref, nm_ref, nv_ref):
        ct = ct_ref[...]
        cact = ct * _sigmoid(ct)
        dm = dm_ref[...]
        g = cact[:, 0:1] * dm[0:1, :]
        for b in range(1, nex):
            g = g + cact[:, b:b + 1] * dm[b:b + 1, :]
        g_ref[...] = g
        d_ref[...], nm_ref[...], nv_ref[...] = _adamw_math(w_ref[...], g, m_ref[...], v_ref[...])

    blk = pl.BlockSpec((tr, tn), lambda i, j: (i, j))
    return _pallas(
        body, [c_t, dmod_cols, w, m, v], dep=dep, grid=(d // tr, nc // tn),
        in_specs=[pl.BlockSpec((tr, nex), lambda i, j: (i, 0)), pl.BlockSpec((nex, tn), lambda i, j: (0, j)),
                  blk, blk, blk],
        out_specs=[blk] * 4,
        out_shape=[jax.ShapeDtypeStruct((d, nc), F32)] * 4,
        sem=("parallel", "parallel"), name="ada_grad_adamw")


def _row_pack(parts):
    cols, offs, off = [], [], 0
    for p in parts:
        n = p.shape[1]
        width = -(-n // LANES) * LANES
        cols.append(jnp.pad(p, ((0, 0), (0, width - n))) if width != n else p)
        offs.append(off)
        off += width
    return jnp.concatenate(cols, axis=1), offs


def small_sum_adamw(gathered, offs, ws, ms, vs, extra_widths, dep=None):
    ndev = gathered.shape[0]
    npar = len(ws)

    def body(ga_ref, *refs):
        w_refs, m_refs, v_refs = refs[:npar], refs[npar:2 * npar], refs[2 * npar:3 * npar]
        outs = refs[3 * npar:]
        tot = ga_ref[0]
        for s in range(1, ndev):
            tot = tot + ga_ref[s]
        for i in range(npar):
            n = ws[i].shape[1]
            g = tot[:, offs[i]:offs[i] + n]
            outs[4 * i][...] = g
            outs[4 * i + 1][...], outs[4 * i + 2][...], outs[4 * i + 3][...] = _adamw_math(
                w_refs[i][...], g, m_refs[i][...], v_refs[i][...])
        for e, n in enumerate(extra_widths):
            off = offs[npar + e]
            outs[4 * npar + e][...] = tot[:, off:off + n]

    shapes = [jax.ShapeDtypeStruct(w.shape, F32) for w in ws for _ in range(4)]
    shapes += [jax.ShapeDtypeStruct((1, n), F32) for n in extra_widths]
    return _pallas(
        body, [gathered, *ws, *ms, *vs], dep=dep, in_specs=[_VMEM] * (1 + 3 * npar), out_specs=[_VMEM] * len(shapes),
        out_shape=shapes, name="small_sum_adamw")


def _position():
    return lax.axis_index("x"), lax.axis_index("y"), lax.axis_index("c")


def _other_chips(x, y):
    return [(1 - x, y), (x, 1 - y), (1 - x, 1 - y)]


def allgather_small(block, name, dep=None):
    def body(x_ref, out_ref, send_sems, recv_sems, local_sem):
        x, y, c = _position()
        me, sibling = (x, y, c), (x, y, 1 - c)
        chips = _other_chips(x, y)

        def slot(px, py, pc):
            return out_ref.at[4 * px + 2 * py + pc]

        def copy(k, block_of, to, src=None):
            return pltpu.make_async_remote_copy(
                src_ref=slot(*block_of) if src is None else src, dst_ref=slot(*block_of),
                send_sem=send_sems.at[k], recv_sem=recv_sems.at[k], device_id=to, device_id_type=MESH)

        mine = pltpu.make_async_copy(x_ref, slot(*me), local_sem)
        mine.start()
        first = [copy(0, me, sibling, src=x_ref)]
        first += [copy(1 + j, me, (*chip, c), src=x_ref) for j, chip in enumerate(chips)]
        for cp in first:
            cp.start()
        passed = [copy(4 + j, (*chip, c), sibling) for j, chip in enumerate(chips)]
        for j, chip in enumerate(chips):
            copy(1 + j, (*chip, c), me).wait_recv()
            passed[j].start()
        copy(0, sibling, me).wait_recv()
        for j, chip in enumerate(chips):
            copy(4 + j, (*chip, 1 - c), me).wait_recv()
        for cp in first + passed:
            cp.wait_send()
        mine.wait()

    return _pallas(
        body, [block], dep=dep,
        out_shape=jax.ShapeDtypeStruct((N_DEV, *block.shape), block.dtype),
        in_specs=[_VMEM], out_specs=_VMEM,
        scratch=[pltpu.SemaphoreType.DMA((7,)), pltpu.SemaphoreType.DMA((7,)), pltpu.SemaphoreType.DMA],
        name=name)


class Started(NamedTuple):
    send_sems: Any
    recv_sems: Any
    bufs: list


def exchange_start_many(name, buf_sets, plans, dep=None):
    sizes = [len(bufs) for bufs in buf_sets]
    first = [sum(sizes[:i]) for i in range(len(sizes))]
    flat = [b for bufs in buf_sets for b in bufs]
    nb, npl = len(flat), len(plans)

    def body(*refs):
        for i, (s, _, plan) in enumerate(plans):
            for cp in plan(refs[first[s]:first[s] + sizes[s]], refs[nb + 2 * i], refs[nb + 2 * i + 1]):
                cp.start()

    sems = [pltpu.SemaphoreType.DMA((n,)) for _, n, _ in plans for _ in range(2)]
    outs = _pallas(
        body, [pltpu.with_memory_space_constraint(b, pltpu.HBM) for b in flat], dep=dep, name=name,
        out_shape=(*sems, *[pltpu.HBM(b.shape, b.dtype) for b in flat]),
        in_specs=[_HBM] * nb,
        out_specs=(*[_SEM] * (2 * npl), *[_HBM] * nb),
        input_output_aliases={i: 2 * npl + i for i in range(nb)},
        compiler_params=pltpu.CompilerParams(has_side_effects=_EFFECT))
    new_bufs = outs[2 * npl:]
    return [Started(outs[2 * i], outs[2 * i + 1], list(new_bufs[first[s]:first[s] + sizes[s]]))
            for i, (s, _, _) in enumerate(plans)]


def exchange_start(name, bufs, n_copies, plan, dep=None):
    return exchange_start_many(name, [bufs], [(0, n_copies, plan)], dep=dep)[0]


def exchange_wait(name, started, plan, bufs=None, dep=None):
    if bufs is not None:
        started = started._replace(bufs=list(bufs))
    nb = len(started.bufs)

    def body(*refs):
        for cp in plan(refs[:nb], refs[nb], refs[nb + 1]):
            cp.wait_send()
            cp.wait_recv()

    outs = _pallas(
        body, [*started.bufs, started.send_sems, started.recv_sems], dep=dep, name=name,
        out_shape=tuple(pltpu.HBM(b.shape, b.dtype) for b in started.bufs),
        in_specs=[_HBM] * nb + [_SEM, _SEM],
        out_specs=tuple([_HBM] * nb),
        input_output_aliases={i: i for i in range(nb)},
        compiler_params=pltpu.CompilerParams(has_side_effects=_EFFECT))
    return list(outs)


def _remote(src, dst, send_sems, recv_sems, i, to):
    return pltpu.make_async_remote_copy(src_ref=src, dst_ref=dst, send_sem=send_sems.at[i], recv_sem=recv_sems.at[i],
                                        device_id=to, device_id_type=MESH)


def _half_rows(buf_rows, chip_idx, pc):
    half = buf_rows // (2 * N_CHIPS)
    return pl.ds((2 * chip_idx + pc) * half, half)


ALL_PEERS = (0, 1, 2)


def plan_gather_ici(refs, send_sems, recv_sems, peers=ALL_PEERS):
    x, y, c = _position()
    chips = _other_chips(x, y)
    copies = []
    for k, ref in enumerate(refs):
        rows = ref.at[_half_rows(ref.shape[0], 2 * x + y, c), :]
        for i, j in enumerate(peers):
            copies.append(_remote(rows, rows, send_sems, recv_sems, len(peers) * k + i, (*chips[j], c)))
    return copies


def plan_gather_relay(refs, send_sems, recv_sems):
    x, y, c = _position()
    copies = []
    for k, ref in enumerate(refs):
        quarter = ref.shape[0] // (4 * N_CHIPS)
        for i, (src_chip, to) in enumerate((((1 - x, y), (x, 1 - y, c)), ((x, 1 - y), (1 - x, y, c)))):
            start = (2 * (2 * src_chip[0] + src_chip[1]) + c) * 2 * quarter + i * quarter
            rows = ref.at[pl.ds(start, quarter), :]
            copies.append(_remote(rows, rows, send_sems, recv_sems, 2 * k + i, to))
    return copies


def plan_gather_d2d(refs, send_sems, recv_sems, peers=ALL_PEERS):
    x, y, c = _position()
    chips = _other_chips(x, y)
    copies = []
    for k, ref in enumerate(refs):
        for i, j in enumerate(peers):
            px, py = chips[j]
            rows = ref.at[_half_rows(ref.shape[0], 2 * px + py, c), :]
            copies.append(_remote(rows, rows, send_sems, recv_sems, len(peers) * k + i, (x, y, 1 - c)))
    return copies


def plan_pair_exchange(refs, send_sems, recv_sems):
    x, y, c = _position()
    nw = len(refs) // 2
    copies = []
    for k in range(nw):
        for chip in range(N_CHIPS):
            copies.append(_remote(refs[k].at[chip, 1 - c], refs[nw + k].at[chip], send_sems, recv_sems,
                                  N_CHIPS * k + chip, (x, y, 1 - c)))
    return copies


def plan_chip_exchange(refs, send_sems, recv_sems):
    x, y, c = _position()
    nw = len(refs) // 2
    copies = []
    for k in range(nw):
        for j, (px, py) in enumerate(_other_chips(x, y)):
            copies.append(_remote(refs[k].at[2 * px + py], refs[nw + k].at[2 * x + y], send_sems, recv_sems,
                                  3 * k + j, (px, py, c)))
    return copies


def plan_pair_share(refs, send_sems, recv_sems):
    x, y, c = _position()
    return [_remote(ref.at[c], ref.at[c], send_sems, recv_sems, k, (x, y, 1 - c)) for k, ref in enumerate(refs)]


def cast_into_slot(src, slot, n_slots, name, dep=None):
    r, n = src.shape
    tr, tn = _ew_tiles(r, n, BF16_SUBLANES)

    def body(slot_ref, s_ref, o_ref):
        o_ref[...] = s_ref[...].astype(BF16)

    return _pallas(
        body, [slot, src], dep=dep, n_prefetch=1, grid=(r // tr, n // tn),
        in_specs=[pl.BlockSpec((tr, tn), lambda i, j, sl: (i, j))],
        out_specs=pl.BlockSpec((None, tr, tn), lambda i, j, sl: (sl[0], i, j)),
        out_shape=jax.ShapeDtypeStruct((n_slots, r, n), BF16),
        sem=("parallel", "parallel"), name=name)


def pair_sum(g, r, core, name, dep=None):
    nchip, _, h, n = g.shape
    th, tn = _ew_tiles(h, n, BF16_SUBLANES)

    def body(core_ref, g_ref, r_ref, o_ref):
        o_ref[...] = (g_ref[...].astype(F32) + r_ref[...].astype(F32)).astype(BF16)

    return _pallas(
        body, [core, g, r], dep=dep, n_prefetch=1, grid=(nchip, h // th, n // tn),
        in_specs=[pl.BlockSpec((None, None, th, tn), lambda a, i, j, cr: (a, cr[0], i, j)),
                  pl.BlockSpec((None, th, tn), lambda a, i, j, cr: (a, i, j))],
        out_specs=pl.BlockSpec((None, th, tn), lambda a, i, j, cr: (a, i, j)),
        out_shape=jax.ShapeDtypeStruct((nchip, h, n), BF16),
        sem=("parallel", "parallel", "parallel"), name=name)


def chip_sum(own, got, where, name, dep=None):
    nchip, h, n = got.shape
    th, tn = _ew_tiles(h, n, BF16_SUBLANES, elems=EW_ELEMS_MANY)

    def body(where_ref, own_ref, *rest):
        got_refs, o_ref = rest[:nchip], rest[nchip]
        chip = where_ref[0]
        acc = None
        for s in range(nchip):
            term = jnp.where(chip == s, own_ref[...], got_refs[s][...]).astype(F32)
            acc = term if acc is None else acc + term
        o_ref[...] = acc

    def got_spec(s):
        return pl.BlockSpec((None, th, tn), lambda i, j, wr: (jnp.where(wr[0] == s, (s + 1) % nchip, s), i, j))

    return _pallas(
        body, [where, own, *[got] * nchip], dep=dep, n_prefetch=1, grid=(h // th, n // tn),
        in_specs=[pl.BlockSpec((None, th, tn), lambda i, j, wr: (wr[0], i, j))]
        + [got_spec(s) for s in range(nchip)],
        out_specs=pl.BlockSpec((None, th, tn), lambda i, j, wr: (wr[1], i, j)),
        out_shape=jax.ShapeDtypeStruct((2, h, n), F32),
        sem=("parallel", "parallel"), name=name)


def kernel(x, c, w_ada, b_ada, norm_mix_g, w_in, q_norm_g, k_norm_g, attn_sinks, rel_bias, w_attn_out, conv_w, conv_b, conv_ln_g, conv_ln_b, w_conv_out, w_mix_out, norm_ffn_g, w_ffn_in, w_ffn_out, loss_target, m_w_ada, m_b_ada, m_norm_mix_g, m_w_in, m_q_norm_g, m_k_norm_g, m_attn_sinks, m_rel_bias, m_w_attn_out, m_conv_w, m_conv_b, m_conv_ln_g, m_conv_ln_b, m_w_conv_out, m_w_mix_out, m_norm_ffn_g, m_w_ffn_in, m_w_ffn_out, v_w_ada, v_b_ada, v_norm_mix_g, v_w_in, v_q_norm_g, v_k_norm_g, v_attn_sinks, v_rel_bias, v_w_attn_out, v_conv_w, v_conv_b, v_conv_ln_g, v_conv_ln_b, v_w_conv_out, v_w_mix_out, v_norm_ffn_g, v_w_ffn_in, v_w_ffn_out):
    run = InOrder()
    xi, yi, ci = _position()
    chip = 2 * xi + yi
    me = 2 * chip + ci
    chip_arr = chip.astype(jnp.int32).reshape(1)
    core_arr = ci.astype(jnp.int32).reshape(1)
    where_arr = jnp.stack([chip, ci]).astype(jnp.int32)

    xe, tgt = x[0], loss_target[0]
    t, d = xe.shape
    hd = q_norm_g.shape[-1]
    nq = attn_sinks.shape[-1]
    aw = nq * hd
    ch = conv_b.shape[-1]
    in_width = N_CHIPS * w_in.shape[-1]
    kvw = (in_width - aw - 2 * ch - 2 * d) // 2
    nkv = kvw // hd
    dff = N_CHIPS * w_ffn_out.shape[1]
    off_k, off_v, off_ca = aw, aw + kvw, aw + 2 * kvw
    off_cb, off_ga, off_gc = off_ca + ch, off_ca + 2 * ch, off_ca + 2 * ch + d
    nc_ada = w_ada.shape[-1]
    ch_loc = conv_w.shape[-1]
    nj_ffn = w_ffn_in.shape[-1]
    perm_ffn = ffn_perm(N_CHIPS)

    big = {"w_in": w_in[0], "w_attn_out": w_attn_out[0], "w_conv_out": w_conv_out[0], "w_mix_out": w_mix_out[0],
           "w_ffn_in": w_ffn_in[0], "w_ffn_out": w_ffn_out[0]}
    moments = {"w_in": (m_w_in, v_w_in), "w_attn_out": (m_w_attn_out, v_w_attn_out),
               "w_conv_out": (m_w_conv_out, v_w_conv_out), "w_mix_out": (m_w_mix_out, v_w_mix_out),
               "w_ffn_in": (m_w_ffn_in, v_w_ffn_in), "w_ffn_out": (m_w_ffn_out, v_w_ffn_out)}
    gather_groups = {"in": ["w_in"], "branch_out": ["w_attn_out", "w_conv_out"], "mix_out": ["w_mix_out"],
                     "ffn_in": ["w_ffn_in"], "ffn_out": ["w_ffn_out"]}
    grads, deltas, new_m, new_v = {}, {}, {}, {}

    def gather_cast(gname):
        bufs = []
        for n in gather_groups[gname]:
            r, ncol = big[n].shape
            bufs.append(run(cast_into_slot, big[n], chip_arr, N_CHIPS, "cast_" + n).reshape(N_CHIPS * r, ncol))
        return bufs

    def gather_pass_on(gname, ici):
        landed = run(exchange_wait, "gather_ici_wait_" + gname, ici, plan_gather_ici)
        return run(exchange_start, "gather_d2d_start_" + gname, landed, 3 * len(landed), plan_gather_d2d)

    def gathered(gname, d2d):
        outs = run(exchange_wait, "gather_d2d_wait_" + gname, d2d, plan_gather_d2d)
        return [o.reshape(N_CHIPS, *big[n].shape) for o, n in zip(outs, gather_groups[gname])]

    def rs_pair_start(gname, names, partials):
        blocks = [g.reshape(N_CHIPS, 2, big[n].shape[0] // 2, big[n].shape[1]) for n, g in zip(names, partials)]
        land = [lax.empty((N_CHIPS,) + b.shape[2:], BF16) for b in blocks]
        return run(exchange_start, "pair_exchange_start_" + gname, blocks + land, N_CHIPS * len(blocks),
                   plan_pair_exchange)

    def rs_chip_start(gname, names, pair):
        nw = len(names)
        outs = run(exchange_wait, "pair_exchange_wait_" + gname, pair, plan_pair_exchange)
        sums = [run(pair_sum, g, r, core_arr, "pair_sum_" + n) for n, g, r in zip(names, outs[:nw], outs[nw:])]
        land = [lax.empty(s.shape, BF16) for s in sums]
        return run(exchange_start, "chip_exchange_start_" + gname, sums + land, 3 * nw, plan_chip_exchange)

    def rs_share_start(gname, names, chipx):
        nw = len(names)
        outs = run(exchange_wait, "chip_exchange_wait_" + gname, chipx, plan_chip_exchange)
        halves = [run(chip_sum, s, r, where_arr, "chip_sum_" + n) for n, s, r in zip(names, outs[:nw], outs[nw:])]
        return run(exchange_start, "pair_share_start_" + gname, halves, nw, plan_pair_share)

    def rs_finish(gname, names, share):
        fulls = run(exchange_wait, "pair_share_wait_" + gname, share, plan_pair_share)
        for n, g2 in zip(names, fulls):
            g, dl, nm, nv = run(adamw, big[n], g2.reshape(big[n].shape), moments[n][0][0], moments[n][1][0],
                                "adamw_" + n, copy_grad=True)
            grads[n], deltas[n], new_m[n], new_v[n] = g[None], dl[None], nm[None], nv[None]

    near, far = (0, 1), (2,)
    plan_ici_near = functools.partial(plan_gather_ici, peers=near)
    plan_ici_far, n_far = plan_gather_relay, 2
    plan_d2d_near = functools.partial(plan_gather_d2d, peers=near)
    plan_d2d_far = functools.partial(plan_gather_d2d, peers=far)
    bufs_in = gather_cast("in")
    row1, offs1 = _row_pack([c, conv_w[0].reshape(1, CONV_WIDTH * ch_loc)])
    got1 = run(allgather_small, row1, "allgather_cond")
    ici_near = run(exchange_start, "gather_ici_start_in_near", bufs_in, len(near), plan_ici_near)
    c_all = got1[:, 0, :d]
    conv_w_full = got1[0::2, 0, offs1[1]:offs1[1] + CONV_WIDTH * ch_loc].reshape(N_CHIPS, CONV_WIDTH, ch_loc)
    conv_w_full = jnp.transpose(conv_w_full, (1, 0, 2)).reshape(CONV_WIDTH, ch)
    conv_w_pad = jnp.pad(conv_w_full, ((0, 1), (0, 0)))
    c_t = jnp.transpose(c_all)
    mod_cols = run(ada_fwd, c_t, w_ada[0])
    rest_bufs = {gname: gather_cast(gname) for gname in gather_groups if gname != "in"}
    bucket = _t5_bucket_table()
    bucket_p, bucket_c = jnp.asarray(bucket[:, :BLOCK]), jnp.asarray(bucket[:, BLOCK:])
    bias_p, bias_c = run(bias_table, rel_bias, bucket_p, bucket_c)
    got2 = run(allgather_small, mod_cols, "allgather_mod")
    mod_all = got2.reshape(N_CHIPS, 2, N_DEV, nc_ada)[:, 0]
    mod = lax.dynamic_slice_in_dim(mod_all, me, 1, axis=1).reshape(1, N_CHIPS * nc_ada) + b_ada
    mod = jnp.pad(mod.reshape(N_MOD, d), ((0, SUBLANES - N_MOD), (0, 0)))

    landed = run(exchange_wait, "gather_ici_wait_in_near", ici_near, plan_ici_near)
    ici_far, d2d_near = run(exchange_start_many, "gather_start_in_far", [landed],
                            [(0, n_far, plan_ici_far), (0, len(near), plan_d2d_near)])
    h = run(pre_mix_fwd, xe, mod, norm_mix_g)
    ici = {}
    ici["branch_out"], ici_near_ffn, ici["mix_out"] = run(
        exchange_start_many, "gather_ici_start_mid", [rest_bufs["branch_out"], rest_bufs["ffn_in"], rest_bufs["mix_out"]],
        [(0, 3 * len(rest_bufs["branch_out"]), plan_gather_ici), (1, len(near), plan_ici_near), (2, 3, plan_gather_ici)])
    landed = run(exchange_wait, "gather_d2d_wait_in_near", d2d_near, plan_d2d_near)
    landed = run(exchange_wait, "gather_ici_wait_in_far", ici_far, plan_ici_far, bufs=landed)
    d2d_far = run(exchange_start, "gather_d2d_start_in_far", landed, len(far), plan_d2d_far)
    landed = run(exchange_wait, "gather_d2d_wait_in_far", d2d_far, plan_d2d_far)
    wg_in = landed[0].reshape(N_CHIPS, *big["w_in"].shape)
    p = run(mm_nn, h, wg_in, tn=wg_in.shape[2], tk=d, out_dtype=BF16, name="mm_in")
    d2d_branch = gather_pass_on("branch_out", ici["branch_out"])

    sinks3 = attn_sinks.reshape(nq, 1, 1)
    attn_o = run(attn_fwd, p, bias_p, bias_c, sinks3, q_norm_g, k_norm_g, aw=aw, kvw=kvw)
    ca, cb = p[:, off_ca:off_cb], p[:, off_cb:off_ga]
    s_conv, co_conv = run(conv_fwd, ca, cb, conv_w_pad, conv_b, conv_ln_g, conv_ln_b)
    wg_attn_out, wg_conv_out = gathered("branch_out", d2d_branch)
    landed = run(exchange_wait, "gather_ici_wait_ffn_in_near", ici_near_ffn, plan_ici_near)
    ici_far_ffn, ici["ffn_out"] = run(
        exchange_start_many, "gather_start_ffn_in_far", [landed, rest_bufs["ffn_out"]],
        [(0, n_far, plan_ici_far), (1, 3, plan_gather_ici)])
    y_attn, y_conv, merged = run(branch_out_merge, attn_o, s_conv, wg_attn_out, wg_conv_out, p, off_ga, off_gc)
    landed_mix = run(exchange_wait, "gather_ici_wait_mix_out", ici["mix_out"], plan_gather_ici)
    d2d_mix, d2d_near_ffn = run(
        exchange_start_many, "gather_d2d_start_mix_ffn_in", [landed_mix, ici_far_ffn.bufs],
        [(0, 3 * len(landed_mix), plan_gather_d2d), (1, len(near), plan_d2d_near)])
    (wg_mix_out,) = gathered("mix_out", d2d_mix)
    wg_mix_out = wg_mix_out.reshape(1, d, d)
    o_m = run(mm_nn, merged, wg_mix_out, tn=_tile(d, TILE_N), tk=d, out_dtype=BF16, name="mm_mix_out")
    landed = run(exchange_wait, "gather_d2d_wait_ffn_in_near", d2d_near_ffn, plan_d2d_near)
    landed = run(exchange_wait, "gather_ici_wait_ffn_in_far", ici_far_ffn, plan_ici_far, bufs=landed)
    d2d_far_ffn = run(exchange_start, "gather_d2d_start_ffn_in_far", landed, len(far), plan_d2d_far)
    x1, h2 = run(pre_ffn_fwd, xe, o_m, mod, norm_ffn_g)
    landed = run(exchange_wait, "gather_d2d_wait_ffn_in_far", d2d_far_ffn, plan_d2d_far)
    wg_ffn_in = landed[0].reshape(N_CHIPS, *big["w_ffn_in"].shape)
    f = run(mm_nn, h2, wg_ffn_in, tn=_tile(nj_ffn, nj_ffn // 2), tk=d, out_dtype=BF16, name="mm_ffn_in", perm=perm_ffn)
    d2d_ffn_out = gather_pass_on("ffn_out", ici["ffn_out"])
    act = run(swiglu_fwd, f, nj_ffn)
    (wg_ffn_out,) = gathered("ffn_out", d2d_ffn_out)
    wg_ffn_out = wg_ffn_out.reshape(1, dff, d)
    o_f = run(mm_nn, act, wg_ffn_out, tn=_tile(d, TILE_N), tk=_tile(dff, dff // 2), out_dtype=BF16, name="mm_ffn_out")
    loss11, dy, dof, acc_l = run(loss_head, x1, o_f, tgt, mod)

    gw_ffn_out = run(mm_tn, act, dof, 1, tk=_tile(dff, TILE_N), tn=d, name="mm_ffn_out_dw")
    px_ffn_out = rs_pair_start("ffn_out", ["w_ffn_out"], [gw_ffn_out])
    dact = run(mm_nt, dof, wg_ffn_out, tko=_tile(dff, TILE_N), tn=d, out_dtype=BF16, name="mm_ffn_out_dx")
    cx_ffn_out = rs_chip_start("ffn_out", ["w_ffn_out"], px_ffn_out)
    df = run(swiglu_bwd, f, dact, nj_ffn)
    gw_ffn_in = run(mm_tn, h2, df, N_CHIPS, tk=d, tn=_tile(nj_ffn, nj_ffn // 2), name="mm_ffn_in_dw",
                    perm=perm_ffn)
    px_ffn_in = rs_pair_start("ffn_in", ["w_ffn_in"], [gw_ffn_in])
    dh2 = run(mm_nt, df, wg_ffn_in, tko=_tile(d, TILE_N), tn=nj_ffn, out_dtype=BF16, name="mm_ffn_in_dx", perm=perm_ffn)
    sh_ffn_out = rs_share_start("ffn_out", ["w_ffn_out"], cx_ffn_out)
    cx_ffn_in = rs_chip_start("ffn_in", ["w_ffn_in"], px_ffn_in)
    dx1, dom, acc_f = run(pre_ffn_bwd, x1, dh2, dy, o_m, mod, norm_ffn_g)
    gw_mix_out = run(mm_tn, merged, dom, 1, tk=d, tn=_tile(d, TILE_WIDE), name="mm_mix_out_dw")
    px_mix = rs_pair_start("mix_out", ["w_mix_out"], [gw_mix_out])
    dy_attn, dy_conv, dga, dgc = run(mix_out_dx_merge_bwd, dom, wg_mix_out.reshape(d, d), p, y_attn, y_conv,
                                     off_ga, off_gc)
    rs_finish("ffn_out", ["w_ffn_out"], sh_ffn_out)
    cx_mix = rs_chip_start("mix_out", ["w_mix_out"], px_mix)
    gw_attn_out = run(mm_tn, attn_o, dy_attn, N_CHIPS, tk=aw, tn=_tile(wg_attn_out.shape[2], TILE_N),
                      name="mm_attn_out_dw")
    gw_conv_out = run(mm_tn, s_conv, dy_conv, N_CHIPS, tk=ch, tn=_tile(wg_conv_out.shape[2], TILE_N),
                      name="mm_conv_out_dw")
    ac_names = ["w_attn_out", "w_conv_out"]
    px_ac = rs_pair_start("attn_conv_out", ac_names, [gw_attn_out, gw_conv_out])
    dattn_o = run(mm_nt, dy_attn, wg_attn_out, tko=_tile(aw, TILE_WIDE), tn=_tile(wg_attn_out.shape[2], TILE_N),
                  out_dtype=BF16, name="mm_attn_out_dx")
    ds_conv = run(mm_nt, dy_conv, wg_conv_out, tko=_tile(ch, TILE_WIDE), tn=_tile(wg_conv_out.shape[2], TILE_N),
                  out_dtype=BF16, name="mm_conv_out_dx")
    cx_ac = rs_chip_start("attn_conv_out", ac_names, px_ac)
    dca, dcb, dconv_w, dconv_vec = run(conv_bwd, ca, cb, co_conv, ds_conv, conv_w_pad, conv_ln_g, conv_ln_b)
    sh_ffn_in = rs_share_start("ffn_in", ["w_ffn_in"], cx_ffn_in)
    dqkv, dbp, dbc, dsinks, dqg, dkg = run(attn_bwd, p, bias_p, bias_c, sinks3, q_norm_g, k_norm_g, dattn_o,
                                           aw=aw, kvw=kvw)
    sh_mix = rs_share_start("mix_out", ["w_mix_out"], cx_mix)
    sh_ac = rs_share_start("attn_conv_out", ac_names, cx_ac)
    drel = run(bias_table_bwd, dbp, dbc, bucket_p, bucket_c).reshape(NUM_BUCKETS, nq)
    dp = jnp.concatenate([dqkv, dca, dcb, dga, dgc], axis=1)
    gw_in = run(mm_tn, h, dp, N_CHIPS, tk=d, tn=wg_in.shape[2], name="mm_in_dw")
    px_in = rs_pair_start("in", ["w_in"], [gw_in])
    dh = run(mm_nt, dp, wg_in, tko=_tile(d, TILE_WIDE), tn=wg_in.shape[2], out_dtype=BF16, name="mm_in_dx")
    grad_x, acc_m = run(pre_mix_bwd, xe, dh, dx1, mod, norm_mix_g)

    dmod = jnp.concatenate([acc_m[0:1], acc_m[1:2], acc_f[3:4], acc_f[0:1], acc_f[1:2], acc_l[0:1]], axis=1)
    small_names = ["b_ada", "norm_mix_g", "q_norm_g", "k_norm_g", "attn_sinks", "rel_bias", "conv_b", "conv_ln_g",
                   "conv_ln_b", "norm_ffn_g"]
    small_w = [b_ada, norm_mix_g, q_norm_g, k_norm_g, attn_sinks, rel_bias, conv_b, conv_ln_g, conv_ln_b, norm_ffn_g]
    small_m = [m_b_ada, m_norm_mix_g, m_q_norm_g, m_k_norm_g, m_attn_sinks, m_rel_bias, m_conv_b, m_conv_ln_g,
               m_conv_ln_b, m_norm_ffn_g]
    small_v = [v_b_ada, v_norm_mix_g, v_q_norm_g, v_k_norm_g, v_attn_sinks, v_rel_bias, v_conv_b, v_conv_ln_g,
               v_conv_ln_b, v_norm_ffn_g]
    small_g = [dmod, acc_m[2:3], dqg, dkg, dsinks.reshape(1, nq), drel.reshape(1, NUM_BUCKETS * nq),
               dconv_vec[0:1], dconv_vec[1:2], dconv_vec[2:3], acc_f[2:3]]
    row3, offs3 = _row_pack(small_g + [dconv_w[:CONV_WIDTH].reshape(1, CONV_WIDTH * ch), loss11])
    got3 = run(allgather_small, row3, "allgather_small_grads")
    cx_in = rs_chip_start("in", ["w_in"], px_in)
    as_row = lambda a: a.reshape(1, -1)
    outs3 = run(small_sum_adamw, got3, offs3, [as_row(a) for a in small_w], [as_row(a) for a in small_m],
                [as_row(a) for a in small_v], [CONV_WIDTH * ch, 1])
    for i, (n, w) in enumerate(zip(small_names, small_w)):
        grads[n], deltas[n], new_m[n], new_v[n] = (o.reshape(w.shape) for o in outs3[4 * i:4 * i + 4])
    g_conv_w_all, loss_sum = outs3[-2].reshape(CONV_WIDTH, ch), outs3[-1]

    g_conv_w = lax.dynamic_slice_in_dim(g_conv_w_all, chip * ch_loc, ch_loc, axis=1)
    grads["conv_w"] = g_conv_w[None]
    dl, nm, nv = run(adamw, conv_w[0], g_conv_w, m_conv_w[0], v_conv_w[0], "adamw_conv_w")
    deltas["conv_w"], new_m["conv_w"], new_v["conv_w"] = dl[None], nm[None], nv[None]

    dmod_all = got3[:, 0, :N_MOD * d]
    dmod_cols = lax.dynamic_slice_in_dim(dmod_all, chip * nc_ada, nc_ada, axis=1)
    g_ada, dl, nm, nv = run(ada_grad_adamw, c_t, dmod_cols, w_ada[0], m_w_ada[0], v_w_ada[0])
    grads["w_ada"], deltas["w_ada"], new_m["w_ada"], new_v["w_ada"] = g_ada[None], dl[None], nm[None], nv[None]

    rs_finish("ffn_in", ["w_ffn_in"], sh_ffn_in)
    rs_finish("mix_out", ["w_mix_out"], sh_mix)
    rs_finish("attn_conv_out", ac_names, sh_ac)
    sh_in = rs_share_start("in", ["w_in"], cx_in)
    rs_finish("in", ["w_in"], sh_in)

    loss = loss_sum[0, 0]
    order = ["w_ada", "b_ada", "norm_mix_g", "w_in", "q_norm_g", "k_norm_g", "attn_sinks", "rel_bias", "w_attn_out",
             "conv_w", "conv_b", "conv_ln_g", "conv_ln_b", "w_conv_out", "w_mix_out", "norm_ffn_g", "w_ffn_in",
             "w_ffn_out"]
    return (loss, grad_x[None], *[grads[n] for n in order], *[deltas[n] for n in order],
            *[new_m[n] for n in order], *[new_v[n] for n in order])
```
